```python
import jax
import jax.numpy as jnp
from jax import lax
import numpy as np


D_MODEL = 1024
BATCH = 8
SEQ = 4096
DEPTH = 2

N_A_LAYERS = DEPTH // 2
N_B_LAYERS = DEPTH - N_A_LAYERS
HEAD_DIM = 64
MIX_WIDTH = D_MODEL
MEM_HEADS = 4
MEM_WIDTH = MEM_HEADS * HEAD_DIM
MAIN_WIDTH = MIX_WIDTH - MEM_WIDTH
CHUNK = 128
A_GROUPS = 6
A_GROUP_DIM = MAIN_WIDTH // A_GROUPS
FOX_HEADS = MAIN_WIDTH // HEAD_DIM
Q_BLOCK = 128
N_MEM = 256
D_FF = -(-8 * D_MODEL // (3 * 256)) * 256
RMS_EPS = 1e-6
LN_EPS = 1e-5
FORGET_BIAS = 4.0

kernel_name = "yoco_gmlp_fox_memory_hybrid"


def rmsnorm(x, g):
    xf = x.astype(jnp.float32)
    y = xf * lax.rsqrt(jnp.mean(xf * xf, axis=-1, keepdims=True) + RMS_EPS)
    return (y * g.astype(jnp.float32)).astype(x.dtype)


def layernorm(x, g, b):
    xf = x.astype(jnp.float32)
    mu = jnp.mean(xf, axis=-1, keepdims=True)
    xc = xf - mu
    y = xc * lax.rsqrt(jnp.mean(xc * xc, axis=-1, keepdims=True) + LN_EPS)
    return (y * g.astype(jnp.float32) + b.astype(jnp.float32)).astype(x.dtype)


def memory_attention(q_mem, mem_n, w_mem_kv):
    B, S, _ = q_mem.shape
    M = mem_n.shape[1]
    k, v = jnp.split(mem_n @ w_mem_kv, 2, axis=-1)
    q = q_mem.reshape(B, S, MEM_HEADS, HEAD_DIM)
    k = k.reshape(B, M, MEM_HEADS, HEAD_DIM)
    v = v.reshape(B, M, MEM_HEADS, HEAD_DIM)
    logits = jnp.einsum('bshd,bmhd->bhsm', q, k).astype(jnp.float32) * (HEAD_DIM ** -0.5)
    p = jax.nn.softmax(logits, axis=-1).astype(v.dtype)
    o = jnp.einsum('bhsm,bmhd->bshd', p, v)
    return o.reshape(B, S, MEM_WIDTH)


def chunked_spatial_gating(u, v, w_s, b_s, ln_g, ln_b):
    B, S, _ = u.shape
    u = jax.nn.gelu(u)
    v = layernorm(jax.nn.gelu(v), ln_g, ln_b)
    nc = S // CHUNK
    vc = v.reshape(B, nc, CHUNK, A_GROUPS, A_GROUP_DIM)
    mask = jnp.tril(jnp.ones((CHUNK, CHUNK), dtype=bool))
    w = jnp.where(mask[None], w_s, jnp.zeros((), w_s.dtype))
    s = jnp.einsum('gts,bcsgd->bctgd', w, vc) + b_s.T[None, None, :, :, None]
    return u * s.reshape(B, S, MAIN_WIDTH)


def forgetting_attention(q, k, v, log_f):
    B, S, H, Dh = q.shape
    c = jnp.cumsum(log_f, axis=1)
    cT = c.transpose(0, 2, 1)
    nb = S // Q_BLOCK
    qb = q.reshape(B, nb, Q_BLOCK, H, Dh).transpose(1, 0, 2, 3, 4)
    cb = cT.reshape(B, H, nb, Q_BLOCK).transpose(2, 0, 1, 3)
    kpos = jnp.arange(S)
    scale = Dh ** -0.5

    def block(args):
        i, qi, ci = args
        qpos = i * Q_BLOCK + jnp.arange(Q_BLOCK)
        logits = jnp.einsum('bthd,bshd->bhts', qi, k).astype(jnp.float32) * scale
        logits = logits + ci[..., :, None] - cT[..., None, :]
        mask = kpos[None, :] <= qpos[:, None]
        logits = jnp.where(mask, logits, -jnp.inf)
        p = jax.nn.softmax(logits, axis=-1).astype(v.dtype)
        return jnp.einsum('bhts,bshd->bthd', p, v)

    o = lax.map(block, (jnp.arange(nb), qb, cb))
    return o.transpose(1, 0, 2, 3, 4).reshape(B, S, H * Dh)


def _fwd_setup_inputs(seed: int = 0) -> dict:
    key = jax.random.key(seed)
    ks = jax.random.split(key, 24)
    f32 = jnp.float32

    def dense(k, shape, fan_in):
        return jax.random.normal(k, shape, f32) * (fan_in ** -0.5)

    def gain(k, shape):
        return 1.0 + 0.05 * jax.random.normal(k, shape, f32)

    x = jax.random.normal(ks[0], (BATCH, SEQ, D_MODEL), f32)
    mem = jax.random.normal(ks[1], (BATCH, N_MEM, D_MODEL), f32)
    w_shared_kv = jnp.concatenate([
        dense(ks[19], (D_MODEL, 2 * MAIN_WIDTH), D_MODEL),
        0.1 * dense(ks[20], (D_MODEL, FOX_HEADS), D_MODEL)], axis=-1)
    return {
        'x': x,
        'mem': mem,
        'ln_mix_pre': gain(ks[2], (DEPTH, D_MODEL)),
        'ln_mix_post': gain(ks[3], (DEPTH, D_MODEL)),
        'ln_ffn_pre': gain(ks[4], (DEPTH, D_MODEL)),
        'ln_ffn_post': gain(ks[5], (DEPTH, D_MODEL)),
        'ln_mem': gain(ks[6], (DEPTH, D_MODEL)),
        'w_mem_kv': dense(ks[7], (DEPTH, D_MODEL, 2 * MEM_WIDTH), D_MODEL),
        'w_out': dense(ks[8], (DEPTH, MIX_WIDTH, D_MODEL), MIX_WIDTH),
        'w_ffn_gate': dense(ks[9], (DEPTH, D_MODEL, D_FF), D_MODEL),
        'w_ffn_up': dense(ks[10], (DEPTH, D_MODEL, D_FF), D_MODEL),
        'w_ffn_down': dense(ks[11], (DEPTH, D_FF, D_MODEL), D_FF),
        'w_in_a': dense(ks[12], (N_A_LAYERS, D_MODEL, 2 * MAIN_WIDTH + MEM_WIDTH), D_MODEL),
        'w_spatial': dense(ks[13], (N_A_LAYERS, A_GROUPS, CHUNK, CHUNK), CHUNK),
        'b_spatial': 1.0 + 0.05 * jax.random.normal(ks[14], (N_A_LAYERS, A_GROUPS, CHUNK), f32),
        'ln_v_g': gain(ks[15], (N_A_LAYERS, MAIN_WIDTH)),
        'ln_v_b': 0.02 * jax.random.normal(ks[16], (N_A_LAYERS, MAIN_WIDTH), f32),
        'ln_shared': gain(ks[17], (D_MODEL,)),
        'w_shared_kv': w_shared_kv,
        'b_forget': FORGET_BIAS + 0.1 * jax.random.normal(ks[18], (FOX_HEADS,), f32),
        'w_in_b': dense(ks[21], (N_B_LAYERS, D_MODEL, MAIN_WIDTH + MEM_WIDTH), D_MODEL),
    }


def _fwd_reference(x, mem, ln_mix_pre, ln_mix_post, ln_ffn_pre, ln_ffn_post, ln_mem,
              w_mem_kv, w_out, w_ffn_gate, w_ffn_up, w_ffn_down, w_in_a,
              w_spatial, b_spatial, ln_v_g, ln_v_b, ln_shared, w_shared_kv,
              b_forget, w_in_b):
    B, S, _ = x.shape
    h = x
    k_s = v_s = log_f_s = None
    for layer in range(DEPTH):
        a = rmsnorm(h, ln_mix_pre[layer])
        mem_n = rmsnorm(mem, ln_mem[layer])
        if layer < N_A_LAYERS:
            proj = a @ w_in_a[layer]
            u = proj[..., :MAIN_WIDTH]
            v = proj[..., MAIN_WIDTH:2 * MAIN_WIDTH]
            q_mem = proj[..., 2 * MAIN_WIDTH:]
            main = chunked_spatial_gating(u, v, w_spatial[layer], b_spatial[layer],
                                          ln_v_g[layer], ln_v_b[layer])
        else:
            if layer == N_A_LAYERS:
                s_in = rmsnorm(h, ln_shared)
                kvf = s_in @ w_shared_kv
                k_s = kvf[..., :MAIN_WIDTH].reshape(B, S, FOX_HEADS, HEAD_DIM)
                v_s = kvf[..., MAIN_WIDTH:2 * MAIN_WIDTH].reshape(B, S, FOX_HEADS, HEAD_DIM)
                log_f_s = jax.nn.log_sigmoid(kvf[..., 2 * MAIN_WIDTH:].astype(jnp.float32)
                                             + b_forget.astype(jnp.float32))
            proj = a @ w_in_b[layer - N_A_LAYERS]
            q = proj[..., :MAIN_WIDTH].reshape(B, S, FOX_HEADS, HEAD_DIM)
            q_mem = proj[..., MAIN_WIDTH:]
            main = forgetting_attention(q, k_s, v_s, log_f_s)
        mixed = jnp.concatenate([main, memory_attention(q_mem, mem_n, w_mem_kv[layer])], axis=-1)
        h = h + rmsnorm(mixed @ w_out[layer], ln_mix_post[layer])
        f = rmsnorm(h, ln_ffn_pre[layer])
        f = (jax.nn.silu(f @ w_ffn_gate[layer]) * (f @ w_ffn_up[layer])) @ w_ffn_down[layer]
        h = h + rmsnorm(f, ln_ffn_post[layer])
    return h


import jax as _jax
import jax.numpy as _jnp

TWIN_FORMAT = 'train_step'
FWD_PARAMS = ['x', 'mem', 'ln_mix_pre', 'ln_mix_post', 'ln_ffn_pre', 'ln_ffn_post', 'ln_mem', 'w_mem_kv', 'w_out', 'w_ffn_gate', 'w_ffn_up', 'w_ffn_down', 'w_in_a', 'w_spatial', 'b_spatial', 'ln_v_g', 'ln_v_b', 'ln_shared', 'w_shared_kv', 'b_forget', 'w_in_b']
TWIN_WEIGHTS = ['ln_mix_pre', 'ln_mix_post', 'ln_ffn_pre', 'ln_ffn_post', 'ln_mem', 'w_mem_kv', 'w_out', 'w_ffn_gate', 'w_ffn_up', 'w_ffn_down', 'w_in_a', 'w_spatial', 'b_spatial', 'ln_v_g', 'ln_v_b', 'ln_shared', 'w_shared_kv', 'b_forget', 'w_in_b']
TWIN_DIFF_INPUT = 'x'
TWIN_INPUTS = ['x', 'mem', 'ln_mix_pre', 'ln_mix_post', 'ln_ffn_pre', 'ln_ffn_post', 'ln_mem', 'w_mem_kv', 'w_out', 'w_ffn_gate', 'w_ffn_up', 'w_ffn_down', 'w_in_a', 'w_spatial', 'b_spatial', 'ln_v_g', 'ln_v_b', 'ln_shared', 'w_shared_kv', 'b_forget', 'w_in_b', 'loss_target', 'm_ln_mix_pre', 'm_ln_mix_post', 'm_ln_ffn_pre', 'm_ln_ffn_post', 'm_ln_mem', 'm_w_mem_kv', 'm_w_out', 'm_w_ffn_gate', 'm_w_ffn_up', 'm_w_ffn_down', 'm_w_in_a', 'm_w_spatial', 'm_b_spatial', 'm_ln_v_g', 'm_ln_v_b', 'm_ln_shared', 'm_w_shared_kv', 'm_b_forget', 'm_w_in_b', 'v_ln_mix_pre', 'v_ln_mix_post', 'v_ln_ffn_pre', 'v_ln_ffn_post', 'v_ln_mem', 'v_w_mem_kv', 'v_w_out', 'v_w_ffn_gate', 'v_w_ffn_up', 'v_w_ffn_down', 'v_w_in_a', 'v_w_spatial', 'v_b_spatial', 'v_ln_v_g', 'v_ln_v_b', 'v_ln_shared', 'v_w_shared_kv', 'v_b_forget', 'v_w_in_b']
TWIN_OUTPUTS = ['loss', 'grad_x', 'grad_ln_mix_pre', 'grad_ln_mix_post', 'grad_ln_ffn_pre', 'grad_ln_ffn_post', 'grad_ln_mem', 'grad_w_mem_kv', 'grad_w_out', 'grad_w_ffn_gate', 'grad_w_ffn_up', 'grad_w_ffn_down', 'grad_w_in_a', 'grad_w_spatial', 'grad_b_spatial', 'grad_ln_v_g', 'grad_ln_v_b', 'grad_ln_shared', 'grad_w_shared_kv', 'grad_b_forget', 'grad_w_in_b', 'delta_ln_mix_pre', 'delta_ln_mix_post', 'delta_ln_ffn_pre', 'delta_ln_ffn_post', 'delta_ln_mem', 'delta_w_mem_kv', 'delta_w_out', 'delta_w_ffn_gate', 'delta_w_ffn_up', 'delta_w_ffn_down', 'delta_w_in_a', 'delta_w_spatial', 'delta_b_spatial', 'delta_ln_v_g', 'delta_ln_v_b', 'delta_ln_shared', 'delta_w_shared_kv', 'delta_b_forget', 'delta_w_in_b', 'new_m_ln_mix_pre', 'new_m_ln_mix_post', 'new_m_ln_ffn_pre', 'new_m_ln_ffn_post', 'new_m_ln_mem', 'new_m_w_mem_kv', 'new_m_w_out', 'new_m_w_ffn_gate', 'new_m_w_ffn_up', 'new_m_w_ffn_down', 'new_m_w_in_a', 'new_m_w_spatial', 'new_m_b_spatial', 'new_m_ln_v_g', 'new_m_ln_v_b', 'new_m_ln_shared', 'new_m_w_shared_kv', 'new_m_b_forget', 'new_m_w_in_b', 'new_v_ln_mix_pre', 'new_v_ln_mix_post', 'new_v_ln_ffn_pre', 'new_v_ln_ffn_post', 'new_v_ln_mem', 'new_v_w_mem_kv', 'new_v_w_out', 'new_v_w_ffn_gate', 'new_v_w_ffn_up', 'new_v_w_ffn_down', 'new_v_w_in_a', 'new_v_w_spatial', 'new_v_b_spatial', 'new_v_ln_v_g', 'new_v_ln_v_b', 'new_v_ln_shared', 'new_v_w_shared_kv', 'new_v_b_forget', 'new_v_w_in_b']
TWIN_LEAF_KINDS = {'loss': 'loss', 'grad_x': 'grad_x', 'grad_ln_mix_pre': 'grad_w', 'grad_ln_mix_post': 'grad_w', 'grad_ln_ffn_pre': 'grad_w', 'grad_ln_ffn_post': 'grad_w', 'grad_ln_mem': 'grad_w', 'grad_w_mem_kv': 'grad_w', 'grad_w_out': 'grad_w', 'grad_w_ffn_gate': 'grad_w', 'grad_w_ffn_up': 'grad_w', 'grad_w_ffn_down': 'grad_w', 'grad_w_in_a': 'grad_w', 'grad_w_spatial': 'grad_w', 'grad_b_spatial': 'grad_w', 'grad_ln_v_g': 'grad_w', 'grad_ln_v_b': 'grad_w', 'grad_ln_shared': 'grad_w', 'grad_w_shared_kv': 'grad_w', 'grad_b_forget': 'grad_w', 'grad_w_in_b': 'grad_w', 'delta_ln_mix_pre': 'delta_w', 'delta_ln_mix_post': 'delta_w', 'delta_ln_ffn_pre': 'delta_w', 'delta_ln_ffn_post': 'delta_w', 'delta_ln_mem': 'delta_w', 'delta_w_mem_kv': 'delta_w', 'delta_w_out': 'delta_w', 'delta_w_ffn_gate': 'delta_w', 'delta_w_ffn_up': 'delta_w', 'delta_w_ffn_down': 'delta_w', 'delta_w_in_a': 'delta_w', 'delta_w_spatial': 'delta_w', 'delta_b_spatial': 'delta_w', 'delta_ln_v_g': 'delta_w', 'delta_ln_v_b': 'delta_w', 'delta_ln_shared': 'delta_w', 'delta_w_shared_kv': 'delta_w', 'delta_b_forget': 'delta_w', 'delta_w_in_b': 'delta_w', 'new_m_ln_mix_pre': 'new_m', 'new_m_ln_mix_post': 'new_m', 'new_m_ln_ffn_pre': 'new_m', 'new_m_ln_ffn_post': 'new_m', 'new_m_ln_mem': 'new_m', 'new_m_w_mem_kv': 'new_m', 'new_m_w_out': 'new_m', 'new_m_w_ffn_gate': 'new_m', 'new_m_w_ffn_up': 'new_m', 'new_m_w_ffn_down': 'new_m', 'new_m_w_in_a': 'new_m', 'new_m_w_spatial': 'new_m', 'new_m_b_spatial': 'new_m', 'new_m_ln_v_g': 'new_m', 'new_m_ln_v_b': 'new_m', 'new_m_ln_shared': 'new_m', 'new_m_w_shared_kv': 'new_m', 'new_m_b_forget': 'new_m', 'new_m_w_in_b': 'new_m', 'new_v_ln_mix_pre': 'new_v', 'new_v_ln_mix_post': 'new_v', 'new_v_ln_ffn_pre': 'new_v', 'new_v_ln_ffn_post': 'new_v', 'new_v_ln_mem': 'new_v', 'new_v_w_mem_kv': 'new_v', 'new_v_w_out': 'new_v', 'new_v_w_ffn_gate': 'new_v', 'new_v_w_ffn_up': 'new_v', 'new_v_w_ffn_down': 'new_v', 'new_v_w_in_a': 'new_v', 'new_v_w_spatial': 'new_v', 'new_v_b_spatial': 'new_v', 'new_v_ln_v_g': 'new_v', 'new_v_ln_v_b': 'new_v', 'new_v_ln_shared': 'new_v', 'new_v_w_shared_kv': 'new_v', 'new_v_b_forget': 'new_v', 'new_v_w_in_b': 'new_v'}


def _forward(args):
    return _fwd_reference(*[args[k] for k in FWD_PARAMS])


def _output_shape():
    def fwd():
        inp = _fwd_setup_inputs(0)
        return _fwd_reference(*[inp[k] for k in FWD_PARAMS])
    out = _jax.eval_shape(fwd)
    return out.shape, out.dtype

N_MICROBATCH = 1
ADAM_LR = 0.001
ADAM_B1 = 0.9
ADAM_B2 = 0.999
ADAM_EPS = 1e-08
ADAM_WD = 0.01
ADAM_STEP = 10
PER_EXAMPLE_BATCH_AXIS = {'x': 0, 'mem': 0, 'loss_target': 0}
SHARED_INPUTS = []
_WEIGHT_DTYPES = {'ln_mix_pre': _jnp.float32, 'ln_mix_post': _jnp.float32, 'ln_ffn_pre': _jnp.float32, 'ln_ffn_post': _jnp.float32, 'ln_mem': _jnp.float32, 'w_mem_kv': _jnp.float32, 'w_out': _jnp.float32, 'w_ffn_gate': _jnp.float32, 'w_ffn_up': _jnp.float32, 'w_ffn_down': _jnp.float32, 'w_in_a': _jnp.float32, 'w_spatial': _jnp.float32, 'b_spatial': _jnp.float32, 'ln_v_g': _jnp.float32, 'ln_v_b': _jnp.float32, 'ln_shared': _jnp.float32, 'w_shared_kv': _jnp.float32, 'b_forget': _jnp.float32, 'w_in_b': _jnp.float32}
MOMENT_SCALE = {'ln_mix_pre': 1.133575e+00, 'ln_mix_post': 3.833461e+01, 'ln_ffn_pre': 6.740505e+00, 'ln_ffn_post': 3.250647e+01, 'ln_mem': 9.096784e-01, 'w_mem_kv': 1.247844e+00, 'w_out': 1.384475e+01, 'w_ffn_gate': 2.252433e+00, 'w_ffn_up': 3.183902e+00, 'w_ffn_down': 5.339157e+00, 'w_in_a': 1.069223e+00, 'w_spatial': 5.563308e-01, 'b_spatial': 1.356924e+00, 'ln_v_g': 7.875381e-01, 'ln_v_b': 1.251796e+00, 'ln_shared': 1.164216e+01, 'w_shared_kv': 9.707411e+00, 'b_forget': 5.190347e+00, 'w_in_b': 5.785899e-01}


def _to_microbatches(a, axis):
    t = _jnp.moveaxis(a, axis, 0)
    t = t.reshape((N_MICROBATCH, t.shape[0] // N_MICROBATCH) + t.shape[1:])
    return _jnp.moveaxis(t, 1, axis + 1)


def setup_inputs(seed: int = 0) -> dict:
    inp = _fwd_setup_inputs(seed)
    key = _jax.random.fold_in(_jax.random.key(seed), 7919)
    shape, _ = _output_shape()
    out = dict(inp)
    out["loss_target"] = _jax.random.normal(_jax.random.fold_in(key, 0), shape, _jnp.float32)
    for i, name in enumerate(TWIN_WEIGHTS):
        w = inp[name].astype(_jnp.float32)
        if MOMENT_SCALE is None:
            s = _jnp.sqrt(_jnp.mean(_jnp.square(w)) + 1e-30)
        else:
            s = MOMENT_SCALE[name]
        km, kv = _jax.random.split(_jax.random.fold_in(key, i + 1))
        out[name] = w
        out["m_" + name] = s * _jax.random.normal(km, w.shape, _jnp.float32)
        out["v_" + name] = (s * s) * _jax.random.uniform(kv, w.shape, _jnp.float32, 0.5, 1.5)
    if N_MICROBATCH > 1:
        for name, axis in PER_EXAMPLE_BATCH_AXIS.items():
            out[name] = _to_microbatches(out[name], axis)
    return {'x': out['x'], 'mem': out['mem'], 'ln_mix_pre': out['ln_mix_pre'], 'ln_mix_post': out['ln_mix_post'], 'ln_ffn_pre': out['ln_ffn_pre'], 'ln_ffn_post': out['ln_ffn_post'], 'ln_mem': out['ln_mem'], 'w_mem_kv': out['w_mem_kv'], 'w_out': out['w_out'], 'w_ffn_gate': out['w_ffn_gate'], 'w_ffn_up': out['w_ffn_up'], 'w_ffn_down': out['w_ffn_down'], 'w_in_a': out['w_in_a'], 'w_spatial': out['w_spatial'], 'b_spatial': out['b_spatial'], 'ln_v_g': out['ln_v_g'], 'ln_v_b': out['ln_v_b'], 'ln_shared': out['ln_shared'], 'w_shared_kv': out['w_shared_kv'], 'b_forget': out['b_forget'], 'w_in_b': out['w_in_b'], 'loss_target': out['loss_target'], 'm_ln_mix_pre': out['m_ln_mix_pre'], 'm_ln_mix_post': out['m_ln_mix_post'], 'm_ln_ffn_pre': out['m_ln_ffn_pre'], 'm_ln_ffn_post': out['m_ln_ffn_post'], 'm_ln_mem': out['m_ln_mem'], 'm_w_mem_kv': out['m_w_mem_kv'], 'm_w_out': out['m_w_out'], 'm_w_ffn_gate': out['m_w_ffn_gate'], 'm_w_ffn_up': out['m_w_ffn_up'], 'm_w_ffn_down': out['m_w_ffn_down'], 'm_w_in_a': out['m_w_in_a'], 'm_w_spatial': out['m_w_spatial'], 'm_b_spatial': out['m_b_spatial'], 'm_ln_v_g': out['m_ln_v_g'], 'm_ln_v_b': out['m_ln_v_b'], 'm_ln_shared': out['m_ln_shared'], 'm_w_shared_kv': out['m_w_shared_kv'], 'm_b_forget': out['m_b_forget'], 'm_w_in_b': out['m_w_in_b'], 'v_ln_mix_pre': out['v_ln_mix_pre'], 'v_ln_mix_post': out['v_ln_mix_post'], 'v_ln_ffn_pre': out['v_ln_ffn_pre'], 'v_ln_ffn_post': out['v_ln_ffn_post'], 'v_ln_mem': out['v_ln_mem'], 'v_w_mem_kv': out['v_w_mem_kv'], 'v_w_out': out['v_w_out'], 'v_w_ffn_gate': out['v_w_ffn_gate'], 'v_w_ffn_up': out['v_w_ffn_up'], 'v_w_ffn_down': out['v_w_ffn_down'], 'v_w_in_a': out['v_w_in_a'], 'v_w_spatial': out['v_w_spatial'], 'v_b_spatial': out['v_b_spatial'], 'v_ln_v_g': out['v_ln_v_g'], 'v_ln_v_b': out['v_ln_v_b'], 'v_ln_shared': out['v_ln_shared'], 'v_w_shared_kv': out['v_w_shared_kv'], 'v_b_forget': out['v_b_forget'], 'v_w_in_b': out['v_w_in_b']}


def _loss(weights, diff, rest, loss_target):
    with _jax.named_scope("forward"):
        args = {**rest, TWIN_DIFF_INPUT: diff, **{k: w.astype(_WEIGHT_DTYPES[k]) for k, w in weights.items()}}
        y = _forward(args)
    with _jax.named_scope("loss_head"):
        err = _jnp.square(y.astype(_jnp.float32) - loss_target)
        return 0.5 * _jnp.sum(_jnp.mean(err, axis=-1)) if err.ndim else 0.5 * err


def _adamw(w, g, m, v):
    m = ADAM_B1 * m + (1.0 - ADAM_B1) * g
    v = ADAM_B2 * v + (1.0 - ADAM_B2) * _jnp.square(g)
    m_hat = m / (1.0 - ADAM_B1 ** ADAM_STEP)
    v_hat = v / (1.0 - ADAM_B2 ** ADAM_STEP)
    delta = -ADAM_LR * (m_hat / (_jnp.sqrt(v_hat) + ADAM_EPS) + ADAM_WD * w)
    return delta, m, v


def reference(x, mem, ln_mix_pre, ln_mix_post, ln_ffn_pre, ln_ffn_post, ln_mem, w_mem_kv, w_out, w_ffn_gate, w_ffn_up, w_ffn_down, w_in_a, w_spatial, b_spatial, ln_v_g, ln_v_b, ln_shared, w_shared_kv, b_forget, w_in_b, loss_target, m_ln_mix_pre, m_ln_mix_post, m_ln_ffn_pre, m_ln_ffn_post, m_ln_mem, m_w_mem_kv, m_w_out, m_w_ffn_gate, m_w_ffn_up, m_w_ffn_down, m_w_in_a, m_w_spatial, m_b_spatial, m_ln_v_g, m_ln_v_b, m_ln_shared, m_w_shared_kv, m_b_forget, m_w_in_b, v_ln_mix_pre, v_ln_mix_post, v_ln_ffn_pre, v_ln_ffn_post, v_ln_mem, v_w_mem_kv, v_w_out, v_w_ffn_gate, v_w_ffn_up, v_w_ffn_down, v_w_in_a, v_w_spatial, v_b_spatial, v_ln_v_g, v_ln_v_b, v_ln_shared, v_w_shared_kv, v_b_forget, v_w_in_b):
    given = dict(x=x, mem=mem, ln_mix_pre=ln_mix_pre, ln_mix_post=ln_mix_post, ln_ffn_pre=ln_ffn_pre, ln_ffn_post=ln_ffn_post, ln_mem=ln_mem, w_mem_kv=w_mem_kv, w_out=w_out, w_ffn_gate=w_ffn_gate, w_ffn_up=w_ffn_up, w_ffn_down=w_ffn_down, w_in_a=w_in_a, w_spatial=w_spatial, b_spatial=b_spatial, ln_v_g=ln_v_g, ln_v_b=ln_v_b, ln_shared=ln_shared, w_shared_kv=w_shared_kv, b_forget=b_forget, w_in_b=w_in_b, loss_target=loss_target, m_ln_mix_pre=m_ln_mix_pre, m_ln_mix_post=m_ln_mix_post, m_ln_ffn_pre=m_ln_ffn_pre, m_ln_ffn_post=m_ln_ffn_post, m_ln_mem=m_ln_mem, m_w_mem_kv=m_w_mem_kv, m_w_out=m_w_out, m_w_ffn_gate=m_w_ffn_gate, m_w_ffn_up=m_w_ffn_up, m_w_ffn_down=m_w_ffn_down, m_w_in_a=m_w_in_a, m_w_spatial=m_w_spatial, m_b_spatial=m_b_spatial, m_ln_v_g=m_ln_v_g, m_ln_v_b=m_ln_v_b, m_ln_shared=m_ln_shared, m_w_shared_kv=m_w_shared_kv, m_b_forget=m_b_forget, m_w_in_b=m_w_in_b, v_ln_mix_pre=v_ln_mix_pre, v_ln_mix_post=v_ln_mix_post, v_ln_ffn_pre=v_ln_ffn_pre, v_ln_ffn_post=v_ln_ffn_post, v_ln_mem=v_ln_mem, v_w_mem_kv=v_w_mem_kv, v_w_out=v_w_out, v_w_ffn_gate=v_w_ffn_gate, v_w_ffn_up=v_w_ffn_up, v_w_ffn_down=v_w_ffn_down, v_w_in_a=v_w_in_a, v_w_spatial=v_w_spatial, v_b_spatial=v_b_spatial, v_ln_v_g=v_ln_v_g, v_ln_v_b=v_ln_v_b, v_ln_shared=v_ln_shared, v_w_shared_kv=v_w_shared_kv, v_b_forget=v_b_forget, v_w_in_b=v_w_in_b)
    weights = {n: given[n] for n in TWIN_WEIGHTS}
    shared = {n: given[n] for n in SHARED_INPUTS}
    per_example = {n: given[n] for n in ['x', 'mem']}
    grad_fn = _jax.value_and_grad(_loss, argnums=(0, 1))

    def one_microbatch(ex, loss_target):
        ex = dict(ex)
        diff = ex.pop(TWIN_DIFF_INPUT)
        return grad_fn(weights, diff, {**shared, **ex}, loss_target)

    if N_MICROBATCH == 1:
        loss, (grad_w, grad_x) = one_microbatch(per_example, given["loss_target"])
    else:
        def body(carry, xs):
            loss_sum, grad_sum = carry
            l_k, (gw_k, gx_k) = one_microbatch(xs[0], xs[1])
            with _jax.named_scope("update"):
                return (loss_sum + l_k, _jax.tree.map(_jnp.add, grad_sum, gw_k)), gx_k

        init = (_jnp.zeros((), _jnp.float32), _jax.tree.map(_jnp.zeros_like, weights))
        (loss, grad_w), grad_x = _jax.lax.scan(body, init, (per_example, given["loss_target"]))
    with _jax.named_scope("update"):
        delta_w, new_m, new_v = {}, {}, {}
        for n in TWIN_WEIGHTS:
            delta_w[n], new_m[n], new_v[n] = _adamw(weights[n], grad_w[n], given["m_" + n], given["v_" + n])
    return (loss, grad_x, *[grad_w[n] for n in TWIN_WEIGHTS], *[delta_w[n] for n in TWIN_WEIGHTS],
            *[new_m[n] for n in TWIN_WEIGHTS], *[new_v[n] for n in TWIN_WEIGHTS])
```

```python
import functools
import math

import jax
import jax.numpy as jnp
from jax import lax
from jax.experimental import pallas as pl
from jax.experimental.pallas import tpu as pltpu

f32 = jnp.float32
bf16 = jnp.bfloat16
SDS = jax.ShapeDtypeStruct

D_MODEL = 1024
MAIN_WIDTH = 768
MEM_WIDTH = 256
HEAD_DIM = 64
MEM_HEADS = 4
FOX_HEADS = 12
FOX_PAIRS = FOX_HEADS // 2
CHUNK = 128
A_GROUPS = 6
D_FF = 2816
KV_WIDTH = 2 * MAIN_WIDTH + FOX_HEADS
KV_PAD = 1792
RMS_EPS = 1e-6
LN_EPS = 1e-5
ATT_SCALE = HEAD_DIM ** -0.5
ADAM_LR, ADAM_B1, ADAM_B2, ADAM_EPS, ADAM_WD, ADAM_STEP = 0.001, 0.9, 0.999, 1e-08, 0.01, 10
N_DEV = 8
AXES = ("x", "y", "c")
MESH = pl.DeviceIdType.MESH
V7X_VMEM_LIMIT = 56 * 1024 * 1024
LANES = 128
FLAT_W = 512
ROW_PAD = 16


def _cparams(*sem):
    return pltpu.CompilerParams(dimension_semantics=sem or None, vmem_limit_bytes=V7X_VMEM_LIMIT)


def _dot(a, b):
    return jnp.dot(a, b, preferred_element_type=f32)


def _dot_nt(a, b):
    return lax.dot_general(a, b, (((1,), (1,)), ((), ())), preferred_element_type=f32)


def _dot_tn(a, b):
    return lax.dot_general(a, b, (((0,), (0,)), ((), ())), preferred_element_type=f32)


def _gelu(x):
    k = math.sqrt(2.0 / math.pi)
    t = jnp.tanh(k * (x + 0.044715 * x * x * x))
    return 0.5 * x * (1.0 + t), t


def _gelu_grad(x, t):
    k = math.sqrt(2.0 / math.pi)
    return 0.5 * (1.0 + t) + 0.5 * x * (1.0 - t * t) * k * (1.0 + 3.0 * 0.044715 * x * x)


def _sigmoid(x):
    return 1.0 / (1.0 + jnp.exp(-x))


def rms_fwd(x, gains, name, tm=512):
    m, d = x.shape
    tm = min(tm, m)
    n = len(gains)

    def body(x_ref, *refs):
        xv = x_ref[...]
        y = xv * lax.rsqrt(jnp.sum(xv * xv, axis=-1, keepdims=True) * (1.0 / d) + RMS_EPS)
        for g_ref, o_ref in zip(refs[:n], refs[n:]):
            o_ref[...] = (y * g_ref[...]).astype(bf16)

    row = pl.BlockSpec((tm, d), lambda i: (i, 0))
    vec = pl.BlockSpec((1, d), lambda i: (0, 0))
    return pl.pallas_call(body, grid=(m // tm,), in_specs=[row] + [vec] * n, out_specs=[row] * n,
                          out_shape=[SDS((m, d), bf16)] * n, name=name, compiler_params=_cparams("parallel"))(x, *gains)


def resnorm(h, y, g_post, gains, name, tm=512):
    m, d = h.shape
    n = len(gains)

    def body(h_ref, y_ref, gp_ref, *refs):
        yv = y_ref[...]
        yn = yv * lax.rsqrt(jnp.sum(yv * yv, axis=-1, keepdims=True) * (1.0 / d) + RMS_EPS)
        hn = h_ref[...] + yn * gp_ref[...]
        refs[n][...] = hn
        if n:
            z = hn * lax.rsqrt(jnp.sum(hn * hn, axis=-1, keepdims=True) * (1.0 / d) + RMS_EPS)
            for g_ref, o_ref in zip(refs[:n], refs[n + 1:]):
                o_ref[...] = (z * g_ref[...]).astype(bf16)

    row = pl.BlockSpec((tm, d), lambda i: (i, 0))
    vec = pl.BlockSpec((1, d), lambda i: (0, 0))
    return pl.pallas_call(body, grid=(m // tm,), in_specs=[row, row, vec] + [vec] * n, out_specs=[row] * (n + 1),
                          out_shape=[SDS((m, d), f32)] + [SDS((m, d), bf16)] * n, name=name,
                          compiler_params=_cparams("parallel"))(h, y, g_post, *gains)


def rms_bwd(x, g, dy, add, out_dtype, name, tm=512):
    m, d = x.shape
    tm = min(tm, m)
    has_add = add is not None

    def body(x_ref, g_ref, dy_ref, *refs):
        dx_ref, dg_ref = refs[-2], refs[-1]
        xv = x_ref[...]
        dyv = dy_ref[...].astype(f32)
        r = lax.rsqrt(jnp.sum(xv * xv, axis=-1, keepdims=True) * (1.0 / d) + RMS_EPS)
        xn = xv * r
        dyg = dyv * g_ref[...]
        dx = r * (dyg - xn * (jnp.sum(dyg * xn, axis=-1, keepdims=True) * (1.0 / d)))
        if has_add:
            dx = dx + refs[0][...]
        dx_ref[...] = dx.astype(out_dtype)

        @pl.when(pl.program_id(0) == 0)
        def _():
            dg_ref[...] = jnp.zeros_like(dg_ref)

        dg_ref[...] += jnp.sum(dyv * xn, axis=0, keepdims=True)

    row = pl.BlockSpec((tm, d), lambda i: (i, 0))
    vec = pl.BlockSpec((1, d), lambda i: (0, 0))
    ins = [x, g, dy] + ([add] if has_add else [])
    return pl.pallas_call(body, grid=(m // tm,), in_specs=[row, vec, row] + ([row] if has_add else []),
                          out_specs=[row, vec], out_shape=[SDS((m, d), out_dtype), SDS((1, d), f32)], name=name,
                          compiler_params=_cparams("arbitrary"))(*ins)


def loss_grad(h, tgt, name, tm=512):
    m, d = h.shape

    def body(h_ref, t_ref, dy_ref, l_ref):
        e = h_ref[...] - t_ref[...]
        dy_ref[...] = e * (1.0 / d)

        @pl.when(pl.program_id(0) == 0)
        def _():
            l_ref[...] = jnp.zeros_like(l_ref)

        part = jnp.sum(jnp.sum(e * e, axis=-1, keepdims=True), axis=0, keepdims=True) * (0.5 / d)
        l_ref[...] += jnp.broadcast_to(part, l_ref.shape)

    row = pl.BlockSpec((tm, d), lambda i: (i, 0))
    return pl.pallas_call(body, grid=(m // tm,), in_specs=[row, row],
                          out_specs=[row, pl.BlockSpec((8, LANES), lambda i: (0, 0))],
                          out_shape=[SDS((m, d), f32), SDS((8, LANES), f32)], name=name,
                          compiler_params=_cparams("arbitrary"))(h, tgt)


def mm(a, b, name, trans_b=False, out_dtype=f32, tm=512, tn=512):
    m, k = a.shape
    n = b.shape[0] if trans_b else b.shape[1]
    tm, tn = min(tm, m), min(tn, n)
    assert m % tm == 0 and n % tn == 0, (name, m, n, tm, tn)

    def body(a_ref, b_ref, o_ref):
        r = _dot_nt(a_ref[...], b_ref[...]) if trans_b else _dot(a_ref[...], b_ref[...])
        o_ref[...] = r.astype(out_dtype)

    b_spec = pl.BlockSpec((tn, k), lambda j, i: (j, 0)) if trans_b else pl.BlockSpec((k, tn), lambda j, i: (0, j))
    return pl.pallas_call(body, grid=(n // tn, m // tm), in_specs=[pl.BlockSpec((tm, k), lambda j, i: (i, 0)), b_spec],
                          out_specs=pl.BlockSpec((tm, tn), lambda j, i: (i, j)), out_shape=SDS((m, n), out_dtype),
                          name=name, compiler_params=_cparams("parallel", "parallel"))(a, b)


def mm_tn(a, g, name, tk=512, tn=512):
    s, k = a.shape
    n = g.shape[1]
    tk, tn = min(tk, k), min(tn, n)
    assert k % tk == 0 and n % tn == 0, (name, k, n, tk, tn)

    def body(a_ref, g_ref, o_ref):
        o_ref[...] = _dot_tn(a_ref[...], g_ref[...])

    return pl.pallas_call(body, grid=(k // tk, n // tn),
                          in_specs=[pl.BlockSpec((s, tk), lambda i, j: (0, i)), pl.BlockSpec((s, tn), lambda i, j: (0, j))],
                          out_specs=pl.BlockSpec((tk, tn), lambda i, j: (i, j)), out_shape=SDS((k, n), f32), name=name,
                          compiler_params=_cparams("parallel", "parallel"))(a, g)


def ffn_up(f, wgu, name, tm=256, tc=256):
    s, d = f.shape
    ff = wgu.shape[1] // 2

    def body(f_ref, w_ref, gu_ref, act_ref):
        fv = f_ref[...]
        for j in range(ff // tc):
            lo = j * tc
            gg = _dot(fv, w_ref[:, lo:lo + tc])
            uu = _dot(fv, w_ref[:, ff + lo:ff + lo + tc])
            gu_ref[:, lo:lo + tc] = gg.astype(bf16)
            gu_ref[:, ff + lo:ff + lo + tc] = uu.astype(bf16)
            act_ref[:, lo:lo + tc] = (gg * _sigmoid(gg) * uu).astype(bf16)

    return pl.pallas_call(body, grid=(s // tm,),
                          in_specs=[pl.BlockSpec((tm, d), lambda i: (i, 0)), pl.BlockSpec((d, 2 * ff), lambda i: (0, 0))],
                          out_specs=[pl.BlockSpec((tm, 2 * ff), lambda i: (i, 0)), pl.BlockSpec((tm, ff), lambda i: (i, 0))],
                          out_shape=[SDS((s, 2 * ff), bf16), SDS((s, ff), bf16)], name=name,
                          compiler_params=_cparams("parallel"))(f, wgu)


def ffn_act_bwd(gu, d_act, name, tm=256):
    s, ff2 = gu.shape
    ff = ff2 // 2

    def body(gu_ref, da_ref, o_ref):
        gg = gu_ref[:, :ff].astype(f32)
        uu = gu_ref[:, ff:].astype(f32)
        da = da_ref[...]
        sg = _sigmoid(gg)
        o_ref[:, :ff] = (da * uu * (sg * (1.0 + gg * (1.0 - sg)))).astype(bf16)
        o_ref[:, ff:] = (da * gg * sg).astype(bf16)

    return pl.pallas_call(body, grid=(s // tm,),
                          in_specs=[pl.BlockSpec((tm, ff2), lambda i: (i, 0)), pl.BlockSpec((tm, ff), lambda i: (i, 0))],
                          out_specs=pl.BlockSpec((tm, ff2), lambda i: (i, 0)), out_shape=SDS((s, ff2), bf16), name=name,
                          compiler_params=_cparams("parallel"))(gu, d_act)


def _gmlp_forward_chunk(u, v, w_refs, bias, ln_g, ln_b):
    gu, tu = _gelu(u)
    gv, tv = _gelu(v)
    mu = jnp.sum(gv, axis=-1, keepdims=True) * (1.0 / MAIN_WIDTH)
    xc = gv - mu
    rstd = lax.rsqrt(jnp.sum(xc * xc, axis=-1, keepdims=True) * (1.0 / MAIN_WIDTH) + LN_EPS)
    xhat = xc * rstd
    vln = xhat * ln_g + ln_b
    row = lax.broadcasted_iota(jnp.int32, (CHUNK, CHUNK), 0)
    col = lax.broadcasted_iota(jnp.int32, (CHUNK, CHUNK), 1)
    s_parts = []
    for g in range(A_GROUPS):
        w = jnp.where(col <= row, w_refs[g], jnp.zeros((), bf16))
        s_parts.append(_dot(w, vln[:, g * CHUNK:(g + 1) * CHUNK].astype(bf16)) + bias[:, g:g + 1])
    return gu, tu, tv, rstd, xhat, vln, s_parts


def gmlp_fwd(proj, ws, bs_t, ln_g, ln_b, name, tm=512):
    s = proj.shape[0]

    def body(u_ref, v_ref, w_ref, b_ref, g_ref, bb_ref, o_ref):
        bias = b_ref[...]
        for c in range(tm // CHUNK):
            rows = slice(c * CHUNK, (c + 1) * CHUNK)
            gu, _, _, _, _, _, s_parts = _gmlp_forward_chunk(u_ref[rows, :], v_ref[rows, :], w_ref, bias, g_ref[...], bb_ref[...])
            for g in range(A_GROUPS):
                cols = slice(g * CHUNK, (g + 1) * CHUNK)
                o_ref[rows, cols] = (gu[:, cols] * s_parts[g]).astype(bf16)

    vec = pl.BlockSpec((1, MAIN_WIDTH), lambda i: (0, 0))
    return pl.pallas_call(
        body, grid=(s // tm,),
        in_specs=[pl.BlockSpec((tm, MAIN_WIDTH), lambda i: (i, 0)), pl.BlockSpec((tm, MAIN_WIDTH), lambda i: (i, 1)),
                  pl.BlockSpec((A_GROUPS, CHUNK, CHUNK), lambda i: (0, 0, 0)), pl.BlockSpec((CHUNK, A_GROUPS), lambda i: (0, 0)), vec, vec],
        out_specs=pl.BlockSpec((tm, MAIN_WIDTH), lambda i: (i, 0)), out_shape=SDS((s, MAIN_WIDTH), bf16), name=name,
        compiler_params=_cparams("parallel"))(proj, proj, ws, bs_t, ln_g, ln_b)


def gmlp_bwd(proj, d_mixed, ws, ws_t, bs_t, ln_g, ln_b, name, tm=512):
    s = proj.shape[0]

    def body(u_ref, v_ref, dm_ref, w_ref, wt_ref, b_ref, g_ref, bb_ref, duv_ref, dw_ref, db_ref, dg_ref, dbb_ref):
        @pl.when(pl.program_id(0) == 0)
        def _():
            dw_ref[...] = jnp.zeros_like(dw_ref)
            db_ref[...] = jnp.zeros_like(db_ref)
            dg_ref[...] = jnp.zeros_like(dg_ref)
            dbb_ref[...] = jnp.zeros_like(dbb_ref)

        bias = b_ref[...]
        ln_gv = g_ref[...]
        row = lax.broadcasted_iota(jnp.int32, (CHUNK, CHUNK), 0)
        col = lax.broadcasted_iota(jnp.int32, (CHUNK, CHUNK), 1)
        lane = lax.broadcasted_iota(jnp.int32, (CHUNK, LANES), 1)
        for c in range(tm // CHUNK):
            rows = slice(c * CHUNK, (c + 1) * CHUNK)
            u = u_ref[rows, :]
            v = v_ref[rows, :]
            gu, tu, tv, rstd, xhat, vln, s_parts = _gmlp_forward_chunk(u, v, w_ref, bias, ln_gv, bb_ref[...])
            dm = dm_ref[rows, :]
            d_vln_parts = []
            d_gu_parts = []
            db_acc = jnp.zeros((CHUNK, LANES), f32)
            for g in range(A_GROUPS):
                cols = slice(g * CHUNK, (g + 1) * CHUNK)
                dmg = dm[:, cols]
                d_gu_parts.append(dmg * s_parts[g])
                d_s = dmg * gu[:, cols]
                db_acc = db_acc + jnp.where(lane == g, jnp.sum(d_s, axis=-1, keepdims=True), 0.0)
                d_sb = d_s.astype(bf16)
                dw_ref[g] += jnp.where(col <= row, _dot_nt(d_sb, vln[:, cols].astype(bf16)), 0.0)
                wt = jnp.where(row <= col, wt_ref[g], jnp.zeros((), bf16))
                d_vln_parts.append(_dot(wt, d_sb))
            db_ref[...] += db_acc
            d_vln = jnp.concatenate(d_vln_parts, axis=-1)
            d_gu = jnp.concatenate(d_gu_parts, axis=-1)
            dg_ref[...] += jnp.sum(d_vln * xhat, axis=0, keepdims=True)
            dbb_ref[...] += jnp.sum(d_vln, axis=0, keepdims=True)
            dxh = d_vln * ln_gv
            m1 = jnp.sum(dxh, axis=-1, keepdims=True) * (1.0 / MAIN_WIDTH)
            m2 = jnp.sum(dxh * xhat, axis=-1, keepdims=True) * (1.0 / MAIN_WIDTH)
            d_gv = rstd * (dxh - m1 - xhat * m2)
            duv_ref[rows, :MAIN_WIDTH] = (d_gu * _gelu_grad(u, tu)).astype(bf16)
            duv_ref[rows, MAIN_WIDTH:] = (d_gv * _gelu_grad(v, tv)).astype(bf16)

    vec = pl.BlockSpec((1, MAIN_WIDTH), lambda i: (0, 0))
    wspec = pl.BlockSpec((A_GROUPS, CHUNK, CHUNK), lambda i: (0, 0, 0))
    return pl.pallas_call(
        body, grid=(s // tm,),
        in_specs=[pl.BlockSpec((tm, MAIN_WIDTH), lambda i: (i, 0)), pl.BlockSpec((tm, MAIN_WIDTH), lambda i: (i, 1)),
                  pl.BlockSpec((tm, MAIN_WIDTH), lambda i: (i, 0)), wspec, wspec, pl.BlockSpec((CHUNK, A_GROUPS), lambda i: (0, 0)), vec, vec],
        out_specs=[pl.BlockSpec((tm, 2 * MAIN_WIDTH), lambda i: (i, 0)), wspec, pl.BlockSpec((CHUNK, LANES), lambda i: (0, 0)), vec, vec],
        out_shape=[SDS((s, 2 * MAIN_WIDTH), bf16), SDS((A_GROUPS, CHUNK, CHUNK), f32), SDS((CHUNK, LANES), f32),
                   SDS((1, MAIN_WIDTH), f32), SDS((1, MAIN_WIDTH), f32)],
        name=name, compiler_params=_cparams("arbitrary"))(proj, proj, d_mixed, ws, ws_t, bs_t, ln_g, ln_b)


def _head_mask(width, h):
    lane = lax.broadcasted_iota(jnp.int32, (1, width), 1)
    return (lane >= h * HEAD_DIM) & (lane < (h + 1) * HEAD_DIM)


def mem_attn_fwd(proj, q_block, kv, name, tm=512):
    s = proj.shape[0]
    n_mem = kv.shape[0]

    def body(q_ref, kv_ref, o_ref):
        q = q_ref[...].astype(f32)
        k = kv_ref[:, :MEM_WIDTH].astype(bf16)
        v = kv_ref[:, MEM_WIDTH:].astype(bf16)
        out = jnp.zeros((tm, MEM_WIDTH), f32)
        for h in range(MEM_HEADS):
            msk = _head_mask(MEM_WIDTH, h)
            qh = jnp.where(msk, q, 0.0).astype(bf16)
            sc = _dot_nt(qh, k) * ATT_SCALE
            e = jnp.exp(sc - jnp.max(sc, axis=-1, keepdims=True))
            p = e / jnp.sum(e, axis=-1, keepdims=True)
            out = jnp.where(msk, _dot(p.astype(bf16), v), out)
        o_ref[...] = out.astype(bf16)

    return pl.pallas_call(body, grid=(s // tm,),
                          in_specs=[pl.BlockSpec((tm, MEM_WIDTH), lambda i: (i, q_block)), pl.BlockSpec((n_mem, 2 * MEM_WIDTH), lambda i: (0, 0))],
                          out_specs=pl.BlockSpec((tm, MEM_WIDTH), lambda i: (i, 0)), out_shape=SDS((s, MEM_WIDTH), bf16), name=name,
                          compiler_params=_cparams("parallel"))(proj, kv)


def mem_attn_bwd(proj, q_block, kv, d_mixed, name, tm=512):
    s = proj.shape[0]
    n_mem = kv.shape[0]

    def body(q_ref, kv_ref, do_ref, dq_ref, dkv_ref):
        @pl.when(pl.program_id(0) == 0)
        def _():
            dkv_ref[...] = jnp.zeros_like(dkv_ref)

        q = q_ref[...].astype(f32)
        do = do_ref[...]
        k = kv_ref[:, :MEM_WIDTH].astype(bf16)
        v = kv_ref[:, MEM_WIDTH:].astype(bf16)
        dq = jnp.zeros((tm, MEM_WIDTH), f32)
        dk = jnp.zeros((n_mem, MEM_WIDTH), f32)
        dv = jnp.zeros((n_mem, MEM_WIDTH), f32)
        for h in range(MEM_HEADS):
            msk = _head_mask(MEM_WIDTH, h)
            qh = jnp.where(msk, q, 0.0).astype(bf16)
            doh = jnp.where(msk, do, 0.0).astype(bf16)
            sc = _dot_nt(qh, k) * ATT_SCALE
            e = jnp.exp(sc - jnp.max(sc, axis=-1, keepdims=True))
            p = e / jnp.sum(e, axis=-1, keepdims=True)
            dp = _dot_nt(doh, v)
            ds = p * (dp - jnp.sum(dp * p, axis=-1, keepdims=True))
            dsb = (ds * ATT_SCALE).astype(bf16)
            dq = jnp.where(msk, _dot(dsb, k), dq)
            dk = dk + _dot_tn(dsb, qh)
            dv = dv + _dot_tn(p.astype(bf16), doh)
        dq_ref[...] = dq.astype(bf16)
        dkv_ref[:, :MEM_WIDTH] += dk
        dkv_ref[:, MEM_WIDTH:] += dv

    return pl.pallas_call(
        body, grid=(s // tm,),
        in_specs=[pl.BlockSpec((tm, MEM_WIDTH), lambda i: (i, q_block)), pl.BlockSpec((n_mem, 2 * MEM_WIDTH), lambda i: (0, 0)),
                  pl.BlockSpec((tm, MEM_WIDTH), lambda i: (i, MAIN_WIDTH // MEM_WIDTH))],
        out_specs=[pl.BlockSpec((tm, MEM_WIDTH), lambda i: (i, 0)), pl.BlockSpec((n_mem, 2 * MEM_WIDTH), lambda i: (0, 0))],
        out_shape=[SDS((s, MEM_WIDTH), bf16), SDS((n_mem, 2 * MEM_WIDTH), f32)], name=name,
        compiler_params=_cparams("arbitrary"))(proj, kv, d_mixed)


def _tri(t, upper):
    r = lax.broadcasted_iota(jnp.int32, (t, t), 0)
    c = lax.broadcasted_iota(jnp.int32, (t, t), 1)
    return ((r <= c) if upper else (r >= c)).astype(f32)


def fgate_fwd(z_t, b, name, t=512):
    hh, s = z_t.shape

    def body(z_ref, b_ref, c_ref):
        u = _tri(t, True)
        carry = jnp.zeros((hh, 1), f32)
        for blk in range(s // t):
            x = z_ref[:, blk * t:(blk + 1) * t] + b_ref[...]
            logf = jnp.minimum(x, 0.0) - jnp.log(1.0 + jnp.exp(-jnp.abs(x)))
            y = jnp.dot(logf, u, precision=lax.Precision.HIGHEST, preferred_element_type=f32) + carry
            c_ref[:, blk * t:(blk + 1) * t] = y
            carry = y[:, t - 1:t]

    return pl.pallas_call(body, out_shape=SDS((hh, s), f32), name=name, compiler_params=_cparams())(z_t, b)


def fgate_bwd(dc_t, z_t, b, name, t=512):
    hh, s = z_t.shape

    def body(dc_ref, z_ref, b_ref, dz_ref, db_ref):
        low = _tri(t, False)
        carry = jnp.zeros((hh, 1), f32)
        total = jnp.zeros((hh, 1), f32)
        for blk in reversed(range(s // t)):
            cols = slice(blk * t, (blk + 1) * t)
            y = jnp.dot(dc_ref[:, cols], low, precision=lax.Precision.HIGHEST, preferred_element_type=f32) + carry
            carry = y[:, 0:1]
            dz = y * _sigmoid(-(z_ref[:, cols] + b_ref[...]))
            dz_ref[:, cols] = dz
            total = total + jnp.sum(dz, axis=-1, keepdims=True)
        db_ref[...] = jnp.broadcast_to(total, db_ref.shape)

    return pl.pallas_call(body, out_shape=[SDS((hh, s), f32), SDS((hh, LANES), f32)], name=name,
                          compiler_params=_cparams())(dc_t, z_t, b)


def _pair_masks():
    lane = lax.broadcasted_iota(jnp.int32, (1, LANES), 1)
    return [lane < HEAD_DIM, lane >= HEAD_DIM]


def fox_fwd(q, k, v, c_col, c_row, name, tq=512):
    s = k.shape[0]
    nq = s // tq

    def body(q_ref, k_ref, v_ref, cc_ref, cr_ref, o_ref, lse_ref):
        i = pl.program_id(1)
        qv = q_ref[...]
        masks = _pair_masks()
        row = lax.broadcasted_iota(jnp.int32, (tq, tq), 0)
        col = lax.broadcasted_iota(jnp.int32, (tq, tq), 1)
        res = []
        for hh in range(2):
            qh = jnp.where(masks[hh], qv, jnp.zeros((), bf16))
            ct = cc_ref[:, hh:hh + 1]

            def block(j, carry, diag, qh=qh, ct=ct, hh=hh):
                m, l, acc = carry
                lo = pl.multiple_of(j * tq, tq)
                sc = _dot_nt(qh, k_ref[pl.ds(lo, tq), :]) * ATT_SCALE + (ct - cr_ref[hh:hh + 1, pl.ds(lo, tq)])
                if diag:
                    sc = jnp.where(col <= row, sc, -jnp.inf)
                m_new = jnp.maximum(m, jnp.max(sc, axis=-1, keepdims=True))
                alpha = jnp.exp(m - m_new)
                p = jnp.exp(sc - m_new)
                l = alpha * l + jnp.sum(p, axis=-1, keepdims=True)
                p_hi = p.astype(bf16)
                p_lo = (p - p_hi.astype(f32)).astype(bf16)
                vs = v_ref[pl.ds(lo, tq), :]
                acc = alpha * acc + (_dot(p_hi, vs) + _dot(p_lo, vs))
                return m_new, l, acc

            carry = (jnp.full((tq, 1), -jnp.inf, f32), jnp.zeros((tq, 1), f32), jnp.zeros((tq, LANES), f32))
            carry = lax.fori_loop(0, i, functools.partial(block, diag=False), carry)
            m, l, acc = block(i, carry, True)
            res.append((acc / l, m + jnp.log(l)))
        o_ref[...] = jnp.where(masks[0], res[0][0], res[1][0])
        lse_ref[...] = jnp.where(masks[0], res[0][1], res[1][1])

    return pl.pallas_call(
        body, grid=(FOX_PAIRS, nq),
        in_specs=[pl.BlockSpec((tq, LANES), lambda p, i: (i, p)), pl.BlockSpec((s, LANES), lambda p, i: (0, p)),
                  pl.BlockSpec((s, LANES), lambda p, i: (0, p)), pl.BlockSpec((None, tq, 2), lambda p, i: (p, i, 0)),
                  pl.BlockSpec((None, 2, s), lambda p, i: (p, 0, 0))],
        out_specs=[pl.BlockSpec((tq, LANES), lambda p, i: (i, p)), pl.BlockSpec((None, tq, LANES), lambda p, i: (p, i, 0))],
        out_shape=[SDS((s, MAIN_WIDTH), f32), SDS((FOX_PAIRS, s, LANES), f32)], name=name,
        compiler_params=_cparams("parallel", "parallel"))(q, k, v, c_col, c_row)


def fox_bwd(q, k, v, d_mixed, o, lse, c_col, c_row, name, tq=512):
    s = k.shape[0]
    nq = s // tq

    def body(q_ref, k_ref, v_ref, do_ref, o_ref, lse_ref, cc_ref, cr_ref, dq_ref, dk_ref, dv_ref, dc_ref):
        j = pl.program_id(1)

        @pl.when(j == 0)
        def _():
            dq_ref[...] = jnp.zeros_like(dq_ref)

        masks = _pair_masks()
        row = lax.broadcasted_iota(jnp.int32, (tq, tq), 0)
        col = lax.broadcasted_iota(jnp.int32, (tq, tq), 1)
        kj = k_ref[...]
        vj = v_ref[...]
        lo_j = pl.multiple_of(j * tq, tq)

        def block(i, carry, diag):
            dk, dv, dc0, dc1 = carry
            dcs = [dc0, dc1]
            lo = pl.multiple_of(i * tq, tq)
            qi = q_ref[pl.ds(lo, tq), :]
            doi = do_ref[pl.ds(lo, tq), :]
            prod = doi.astype(bf16).astype(f32) * o_ref[pl.ds(lo, tq), :]
            lse_i = lse_ref[pl.ds(lo, tq), :]
            cc_i = cc_ref[pl.ds(lo, tq), :]
            dq_i = jnp.zeros((tq, LANES), f32)
            for hh in range(2):
                qh = jnp.where(masks[hh], qi, jnp.zeros((), bf16))
                doh = jnp.where(masks[hh], doi, 0.0).astype(bf16)
                delta = jnp.sum(jnp.where(masks[hh], prod, 0.0), axis=-1, keepdims=True)
                sc = _dot_nt(qh, kj) * ATT_SCALE + (cc_i[:, hh:hh + 1] - cr_ref[hh:hh + 1, pl.ds(lo_j, tq)])
                p = jnp.exp(sc - lse_i[:, hh * HEAD_DIM:hh * HEAD_DIM + 1])
                if diag:
                    p = jnp.where(col <= row, p, 0.0)
                dv = dv + _dot_tn(p.astype(bf16), doh)
                ds = p * (_dot_nt(doh, vj) - delta)
                dcs[hh] = dcs[hh] + jnp.sum(ds, axis=0, keepdims=True)
                dsb = (ds * ATT_SCALE).astype(bf16)
                dq_i = jnp.where(masks[hh], _dot(dsb, kj), dq_i)
                dk = dk + _dot_tn(dsb, qh)
            dq_ref[pl.ds(lo, tq), :] += dq_i
            return dk, dv, dcs[0], dcs[1]

        zero = jnp.zeros((tq, LANES), f32)
        zrow = jnp.zeros((1, tq), f32)
        carry = block(j, (zero, zero, zrow, zrow), True)
        dk, dv, dc0, dc1 = lax.fori_loop(j + 1, nq, functools.partial(block, diag=False), carry)
        dk_ref[...] = dk.astype(bf16)
        dv_ref[...] = dv.astype(bf16)
        dc_ref[0:1, :] = -dc0
        dc_ref[1:2, :] = -dc1

    full = lambda p, j: (0, p)
    tile = lambda p, j: (j, p)
    return pl.pallas_call(
        body, grid=(FOX_PAIRS, nq),
        in_specs=[pl.BlockSpec((s, LANES), full), pl.BlockSpec((tq, LANES), tile), pl.BlockSpec((tq, LANES), tile),
                  pl.BlockSpec((s, LANES), full), pl.BlockSpec((s, LANES), full), pl.BlockSpec((None, s, LANES), lambda p, j: (p, 0, 0)),
                  pl.BlockSpec((None, s, 2), lambda p, j: (p, 0, 0)), pl.BlockSpec((None, 2, s), lambda p, j: (p, 0, 0))],
        out_specs=[pl.BlockSpec((s, LANES), full), pl.BlockSpec((tq, LANES), tile), pl.BlockSpec((tq, LANES), tile),
                   pl.BlockSpec((None, 2, tq), lambda p, j: (p, 0, j))],
        out_shape=[SDS((s, MAIN_WIDTH), f32), SDS((s, MAIN_WIDTH), bf16), SDS((s, MAIN_WIDTH), bf16), SDS((FOX_PAIRS, 2, s), f32)],
        name=name, compiler_params=_cparams("parallel", "arbitrary"))(q, k, v, d_mixed, o, lse, c_col, c_row)


def adamw(w, g, m, v, name, tr=256):
    r, c = w.shape
    tr = min(tr, r)
    assert r % tr == 0, (name, r, tr)
    c1 = 1.0 / (1.0 - ADAM_B1 ** ADAM_STEP)
    c2 = 1.0 / (1.0 - ADAM_B2 ** ADAM_STEP)

    def body(w_ref, g_ref, m_ref, v_ref, d_ref, mo_ref, vo_ref):
        gv = g_ref[...]
        mn = ADAM_B1 * m_ref[...] + (1.0 - ADAM_B1) * gv
        vn = ADAM_B2 * v_ref[...] + (1.0 - ADAM_B2) * gv * gv
        mo_ref[...] = mn
        vo_ref[...] = vn
        d_ref[...] = -ADAM_LR * ((mn * c1) / (jnp.sqrt(vn * c2) + ADAM_EPS) + ADAM_WD * w_ref[...])

    spec = pl.BlockSpec((tr, c), lambda i: (i, 0))
    return pl.pallas_call(body, grid=(r // tr,), in_specs=[spec] * 4, out_specs=[spec] * 3, out_shape=[SDS((r, c), f32)] * 3,
                          name=name, compiler_params=_cparams("parallel"))(w, g, m, v)


def sum_leading(x, name, out_dtype=f32, tr=None):
    n, r, c = x.shape
    tr = tr or r
    assert r % tr == 0

    def body(x_ref, o_ref):
        acc = x_ref[0].astype(f32)
        for k in range(1, n):
            acc = acc + x_ref[k].astype(f32)
        o_ref[...] = acc.astype(out_dtype)

    return pl.pallas_call(body, grid=(r // tr,), in_specs=[pl.BlockSpec((n, tr, c), lambda i: (0, i, 0))],
                          out_specs=pl.BlockSpec((tr, c), lambda i: (i, 0)), out_shape=SDS((r, c), out_dtype), name=name,
                          compiler_params=_cparams("parallel"))(x)


def add_pairs(a, b, name, tr):
    n, r, c = a.shape
    assert r % tr == 0

    def body(a_ref, b_ref, o_ref):
        o_ref[...] = (a_ref[...].astype(f32) + b_ref[...].astype(f32)).astype(bf16)

    spec = pl.BlockSpec((None, tr, c), lambda k, i: (k, i, 0))
    return pl.pallas_call(body, grid=(n, r // tr), in_specs=[spec, spec], out_specs=spec, out_shape=SDS((n, r, c), bf16), name=name,
                          compiler_params=_cparams("parallel", "parallel"))(a, b)


def sum_own_and_received(own, recv, name, tr):
    r, c = own.shape
    assert r % tr == 0

    def body(a_ref, b_ref, o_ref):
        acc = a_ref[...].astype(f32)
        for k in range(3):
            acc = acc + b_ref[k].astype(f32)
        o_ref[...] = acc

    return pl.pallas_call(body, grid=(r // tr,),
                          in_specs=[pl.BlockSpec((tr, c), lambda i: (i, 0)), pl.BlockSpec((3, tr, c), lambda i: (0, i, 0))],
                          out_specs=pl.BlockSpec((tr, c), lambda i: (i, 0)), out_shape=SDS((r, c), f32), name=name,
                          compiler_params=_cparams("parallel"))(own, recv)


_ANY = pl.BlockSpec(memory_space=pl.ANY)
_DMA = pltpu.SemaphoreType.DMA


def all_gather(x, name):
    r, c = x.shape

    def body(x_ref, out_ref, send_sems, recv_sems, local_sem):
        mx, my, mc = lax.axis_index("x"), lax.axis_index("y"), lax.axis_index("c")
        me, sibling = (mx, my, mc), (mx, my, 1 - mc)
        chips = [(1 - mx, my), (mx, 1 - my), (1 - mx, 1 - my)]

        def rows(px, py, pc):
            return out_ref.at[4 * px + 2 * py + pc]

        def copy(k, block, to, src=None):
            return pltpu.make_async_remote_copy(src_ref=rows(*block) if src is None else src, dst_ref=rows(*block),
                                                send_sem=send_sems.at[k], recv_sem=recv_sems.at[k], device_id=to, device_id_type=MESH)

        mine = pltpu.make_async_copy(x_ref, rows(*me), local_sem)
        mine.start()
        first = [copy(0, me, sibling, src=x_ref)] + [copy(1 + j, me, (*chip, mc), src=x_ref) for j, chip in enumerate(chips)]
        for cp in first:
            cp.start()
        passed = [copy(4 + j, (*chip, mc), sibling) for j, chip in enumerate(chips)]
        for j, chip in enumerate(chips):
            copy(1 + j, (*chip, mc), me).wait_recv()
            passed[j].start()
        copy(0, sibling, me).wait_recv()
        for j, chip in enumerate(chips):
            copy(4 + j, (*chip, 1 - mc), me).wait_recv()
        for cp in first + passed:
            cp.wait_send()
        mine.wait()

    return pl.pallas_call(body, out_shape=SDS((N_DEV, r, c), x.dtype), in_specs=[_ANY], out_specs=_ANY,
                          scratch_shapes=[_DMA((7,)), _DMA((7,)), _DMA(())], name=name)(x)


def exchange_sibling(x, name):
    def body(x_ref, out_ref, send_sem, recv_sem):
        sibling = (lax.axis_index("x"), lax.axis_index("y"), 1 - lax.axis_index("c"))
        cp = pltpu.make_async_remote_copy(src_ref=x_ref, dst_ref=out_ref, send_sem=send_sem, recv_sem=recv_sem,
                                          device_id=sibling, device_id_type=MESH)
        cp.start()
        cp.wait()

    return pl.pallas_call(body, out_shape=SDS(x.shape, x.dtype), in_specs=[_ANY], out_specs=_ANY,
                          scratch_shapes=[_DMA(()), _DMA(())], name=name)(x)


def exchange_chips(x, name):
    _, r, c = x.shape

    def body(x_ref, out_ref, send_sems, recv_sems):
        mx, my, mc = lax.axis_index("x"), lax.axis_index("y"), lax.axis_index("c")
        chips = [(1 - mx, my), (mx, 1 - my), (1 - mx, 1 - my)]
        copies = [pltpu.make_async_remote_copy(src_ref=x_ref.at[2 * px + py], dst_ref=out_ref.at[j], send_sem=send_sems.at[j],
                                               recv_sem=recv_sems.at[j], device_id=(px, py, mc), device_id_type=MESH)
                  for j, (px, py) in enumerate(chips)]
        for cp in copies:
            cp.start()
        for cp in copies:
            cp.wait()

    return pl.pallas_call(body, out_shape=SDS((3, r, c), x.dtype), in_specs=[_ANY], out_specs=_ANY,
                          scratch_shapes=[_DMA((3,)), _DMA((3,))], name=name)(x)


_SHARDED = [
    ("w_mem_kv", "row", (2, 128, 512)),
    ("w_out", "row", (2, 128, 1024)),
    ("w_ffn_gate", "col", (2, 1024, 352)),
    ("w_ffn_up", "col", (2, 1024, 352)),
    ("w_ffn_down", "row", (2, 352, 1024)),
    ("w_in_a", "col", (1, 1024, 224)),
    ("w_shared_kv", "row", (1, 128, KV_WIDTH)),
    ("w_in_b", "row", (1, 128, 1024)),
]


def _seg_rows(shape):
    n = math.prod(shape)
    rows = -(-n // FLAT_W)
    return -(-rows // ROW_PAD) * ROW_PAD


FLAT_TILE = 512
_WEIGHT_ROWS = sum(_seg_rows(shape) for _, _, shape in _SHARDED)
_TAIL_ROWS = FLAT_TILE - _WEIGHT_ROWS % FLAT_TILE
assert _TAIL_ROWS >= ROW_PAD
FLAT_ROWS = _WEIGHT_ROWS + _TAIL_ROWS


def _row_tile(rows, cap=512):
    return max(t for t in range(8, min(rows, cap) + 1, 8) if rows % t == 0)


def _to_rows(flat, rows):
    pad = rows * FLAT_W - flat.shape[-1]
    if pad:
        flat = jnp.pad(flat, [(0, 0)] * (flat.ndim - 1) + [(0, pad)])
    return flat.reshape(flat.shape[:-1] + (rows, FLAT_W))


def _pack_local(local, ln_v):
    parts = [_to_rows(local[name].reshape(-1).astype(bf16), _seg_rows(shape)) for name, _, shape in _SHARDED]
    bits = lax.bitcast_convert_type(jnp.pad(ln_v, ((0, 0), (0, LANES - ln_v.shape[1]))), bf16)
    parts.append(_to_rows(bits.reshape(-1), _TAIL_ROWS))
    return jnp.concatenate(parts, axis=0)


def _unpack_full(gathered):
    full = {}
    lo = 0
    for name, kind, shape in _SHARDED:
        rows = _seg_rows(shape)
        seg = gathered[:, lo:lo + rows].reshape(N_DEV, -1)[:, :math.prod(shape)].reshape((N_DEV,) + shape)
        lo += rows
        if kind == "row":
            full[name] = seg.transpose(1, 0, 2, 3).reshape(shape[0], N_DEV * shape[1], shape[2])
        else:
            full[name] = seg.transpose(1, 2, 0, 3).reshape(shape[0], shape[1], N_DEV * shape[2])
    per_word = 4 // jnp.dtype(bf16).itemsize
    bits = gathered[:, lo, :2 * LANES * per_word].reshape((N_DEV, 2, LANES) + ((per_word,) if per_word > 1 else ()))
    ln_v = lax.bitcast_convert_type(bits, f32)[:, :, :MAIN_WIDTH // N_DEV]
    return full, ln_v.transpose(1, 0, 2).reshape(2, MAIN_WIDTH)


def _pack_grads(grads):
    parts = []
    for name, kind, shape in _SHARDED:
        g = grads[name]
        if kind == "row":
            seg = g.reshape(shape[0], N_DEV, shape[1], shape[2]).transpose(1, 0, 2, 3)
        else:
            seg = g.reshape(shape[0], shape[1], N_DEV, shape[2]).transpose(2, 0, 1, 3)
        parts.append(_to_rows(seg.reshape(N_DEV, -1).astype(bf16), _seg_rows(shape)))
    parts.append(jnp.zeros((N_DEV, _TAIL_ROWS, FLAT_W), bf16))
    return jnp.concatenate(parts, axis=1)


def _unpack_local(flat):
    out = {}
    lo = 0
    for name, _, shape in _SHARDED:
        rows = _seg_rows(shape)
        out[name] = flat[lo:lo + rows].reshape(-1)[:math.prod(shape)].reshape(shape)
        lo += rows
    return out


_SMALL = [
    ("ln_mix_pre", (2, 1024)), ("ln_mix_post", (2, 1024)), ("ln_ffn_pre", (2, 1024)), ("ln_ffn_post", (2, 1024)),
    ("ln_mem", (2, 1024)), ("w_spatial", (1, 6, 128, 128)), ("b_spatial", (1, 6, 128)), ("ln_shared", (1024,)),
    ("b_forget", (12,)), ("ln_v_g", (1, 768)), ("ln_v_b", (1, 768)),
]
_SMALL_TILE = 8 * LANES


def _small_rows(shape):
    return -(-math.prod(shape) // _SMALL_TILE) * 8


def _pack_small(vals, shapes):
    parts = []
    for name, shape in shapes:
        flat = vals[name].reshape(-1).astype(f32)
        rows = _small_rows(shape)
        parts.append(jnp.pad(flat, (0, rows * LANES - flat.shape[0])).reshape(rows, LANES))
    return jnp.concatenate(parts, axis=0)


def _unpack_small(buf, shapes):
    out = {}
    lo = 0
    for name, shape in shapes:
        rows = _small_rows(shape)
        out[name] = buf[lo:lo + rows].reshape(-1)[:math.prod(shape)].reshape(shape)
        lo += rows
    return out


def kernel(x, mem, ln_mix_pre, ln_mix_post, ln_ffn_pre, ln_ffn_post, ln_mem, w_mem_kv, w_out, w_ffn_gate, w_ffn_up, w_ffn_down, w_in_a, w_spatial, b_spatial, ln_v_g, ln_v_b, ln_shared, w_shared_kv, b_forget, w_in_b, loss_target, m_ln_mix_pre, m_ln_mix_post, m_ln_ffn_pre, m_ln_ffn_post, m_ln_mem, m_w_mem_kv, m_w_out, m_w_ffn_gate, m_w_ffn_up, m_w_ffn_down, m_w_in_a, m_w_spatial, m_b_spatial, m_ln_v_g, m_ln_v_b, m_ln_shared, m_w_shared_kv, m_b_forget, m_w_in_b, v_ln_mix_pre, v_ln_mix_post, v_ln_ffn_pre, v_ln_ffn_post, v_ln_mem, v_w_mem_kv, v_w_out, v_w_ffn_gate, v_w_ffn_up, v_w_ffn_down, v_w_in_a, v_w_spatial, v_b_spatial, v_ln_v_g, v_ln_v_b, v_ln_shared, v_w_shared_kv, v_b_forget, v_w_in_b):
    weights = dict(ln_mix_pre=ln_mix_pre, ln_mix_post=ln_mix_post, ln_ffn_pre=ln_ffn_pre, ln_ffn_post=ln_ffn_post, ln_mem=ln_mem,
                   w_mem_kv=w_mem_kv, w_out=w_out, w_ffn_gate=w_ffn_gate, w_ffn_up=w_ffn_up, w_ffn_down=w_ffn_down, w_in_a=w_in_a,
                   w_spatial=w_spatial, b_spatial=b_spatial, ln_v_g=ln_v_g, ln_v_b=ln_v_b, ln_shared=ln_shared,
                   w_shared_kv=w_shared_kv, b_forget=b_forget, w_in_b=w_in_b)
    mom_m = dict(ln_mix_pre=m_ln_mix_pre, ln_mix_post=m_ln_mix_post, ln_ffn_pre=m_ln_ffn_pre, ln_ffn_post=m_ln_ffn_post, ln_mem=m_ln_mem,
                 w_mem_kv=m_w_mem_kv, w_out=m_w_out, w_ffn_gate=m_w_ffn_gate, w_ffn_up=m_w_ffn_up, w_ffn_down=m_w_ffn_down, w_in_a=m_w_in_a,
                 w_spatial=m_w_spatial, b_spatial=m_b_spatial, ln_v_g=m_ln_v_g, ln_v_b=m_ln_v_b, ln_shared=m_ln_shared,
                 w_shared_kv=m_w_shared_kv, b_forget=m_b_forget, w_in_b=m_w_in_b)
    mom_v = dict(ln_mix_pre=v_ln_mix_pre, ln_mix_post=v_ln_mix_post, ln_ffn_pre=v_ln_ffn_pre, ln_ffn_post=v_ln_ffn_post, ln_mem=v_ln_mem,
                 w_mem_kv=v_w_mem_kv, w_out=v_w_out, w_ffn_gate=v_w_ffn_gate, w_ffn_up=v_w_ffn_up, w_ffn_down=v_w_ffn_down, w_in_a=v_w_in_a,
                 w_spatial=v_w_spatial, b_spatial=v_b_spatial, ln_v_g=v_ln_v_g, ln_v_b=v_ln_v_b, ln_shared=v_ln_shared,
                 w_shared_kv=v_w_shared_kv, b_forget=v_b_forget, w_in_b=v_w_in_b)
    names = list(weights)
    mx, my, mc = lax.axis_index("x"), lax.axis_index("y"), lax.axis_index("c")
    me = 4 * mx + 2 * my + mc

    h0 = x[0]
    mem0 = mem[0]
    tgt = loss_target[0]
    seq = h0.shape[0]

    local = {n: weights[n] for n, _, _ in _SHARDED}
    local["w_shared_kv"] = w_shared_kv[None]
    gathered = all_gather(_pack_local(local, jnp.concatenate([ln_v_g, ln_v_b], axis=0)), "gather_weights")
    full, ln_v = _unpack_full(gathered)
    lnv_g, lnv_b = ln_v[0:1], ln_v[1:2]
    w_gu = jnp.concatenate([full["w_ffn_gate"], full["w_ffn_up"]], axis=-1)
    w_kv = jnp.pad(full["w_shared_kv"][0], ((0, 0), (0, KV_PAD - KV_WIDTH)))
    ws = w_spatial[0].astype(bf16)
    ws_t = ws.transpose(0, 2, 1)
    bs_t = b_spatial[0].T
    vec = lambda a: a.reshape(1, -1)

    (a0,) = rms_fwd(h0, [vec(ln_mix_pre[0])], "a0_norm")
    proj0 = mm(a0, full["w_in_a"][0], "proj0", tn=256)
    (memn0,) = rms_fwd(mem0, [vec(ln_mem[0])], "mem0_norm")
    kvm0 = mm(memn0, full["w_mem_kv"][0], "kvm0")
    main0 = gmlp_fwd(proj0, ws, bs_t, lnv_g, lnv_b, "gmlp_fwd")
    om0 = mem_attn_fwd(proj0, 2 * MAIN_WIDTH // MEM_WIDTH, kvm0, "mem_attn0")
    mixed0 = jnp.concatenate([main0, om0], axis=-1)
    y1_0 = mm(mixed0, full["w_out"][0], "mix_out0")
    hmid0, f0 = resnorm(h0, y1_0, vec(ln_mix_post[0]), [vec(ln_ffn_pre[0])], "resnorm_mix0")
    gu0, act0 = ffn_up(f0, w_gu[0], "ffn_up0")
    y2_0 = mm(act0, full["w_ffn_down"][0], "ffn_down0")
    h1, a1, sin1 = resnorm(hmid0, y2_0, vec(ln_ffn_post[0]), [vec(ln_mix_pre[1]), vec(ln_shared)], "resnorm_ffn0")

    kvf = mm(sin1, w_kv, "kvf", tn=256)
    proj1 = mm(a1, full["w_in_b"][0], "proj1")
    z_t = jnp.pad(kvf[:, 2 * MAIN_WIDTH:KV_WIDTH].T, ((0, 16 - FOX_HEADS), (0, 0)))
    bf_col = jnp.pad(b_forget, (0, 16 - FOX_HEADS)).reshape(16, 1)
    c_t = fgate_fwd(z_t, bf_col, "fgate_fwd")
    c_row = c_t[:FOX_HEADS].reshape(FOX_PAIRS, 2, seq)
    c_col = c_row.transpose(0, 2, 1)
    qb = proj1.astype(bf16)
    kb = kvf[:, :MAIN_WIDTH].astype(bf16)
    vb = kvf[:, MAIN_WIDTH:2 * MAIN_WIDTH].astype(bf16)
    main1, lse = fox_fwd(qb, kb, vb, c_col, c_row, "fox_fwd")
    (memn1,) = rms_fwd(mem0, [vec(ln_mem[1])], "mem1_norm")
    kvm1 = mm(memn1, full["w_mem_kv"][1], "kvm1")
    om1 = mem_attn_fwd(qb, MAIN_WIDTH // MEM_WIDTH, kvm1, "mem_attn1")
    mixed1 = jnp.concatenate([main1.astype(bf16), om1], axis=-1)
    y1_1 = mm(mixed1, full["w_out"][1], "mix_out1")
    hmid1, f1 = resnorm(h1, y1_1, vec(ln_mix_post[1]), [vec(ln_ffn_pre[1])], "resnorm_mix1")
    gu1, act1 = ffn_up(f1, w_gu[1], "ffn_up1")
    y2_1 = mm(act1, full["w_ffn_down"][1], "ffn_down1")
    (h2,) = resnorm(hmid1, y2_1, vec(ln_ffn_post[1]), [], "resnorm_ffn1")
    dh, loss_tile = loss_grad(h2, tgt, "loss")
    loss = lax.psum(loss_tile[0, 0], AXES)

    grads = {}
    small = {}

    def ffn_backward(layer, dh_out, y2, hmid, f, gu, act):
        d_y2, dg_post = rms_bwd(y2, vec(ln_ffn_post[layer]), dh_out, None, bf16, f"ffn_post_bwd{layer}")
        d_act = mm(d_y2, full["w_ffn_down"][layer], f"d_act{layer}", trans_b=True, tn=1408)
        dw_down = mm_tn(act, d_y2, f"dw_down{layer}", tk=256)
        d_gu = ffn_act_bwd(gu, d_act, f"ffn_act_bwd{layer}")
        dw_gu = mm_tn(f, d_gu, f"dw_gu{layer}")
        d_f = mm(d_gu, w_gu[layer], f"d_f{layer}", trans_b=True)
        dh_mid, dg_pre = rms_bwd(hmid, vec(ln_ffn_pre[layer]), d_f, dh_out, f32, f"ffn_pre_bwd{layer}")
        return dh_mid, dg_post, dg_pre, dw_down, dw_gu

    def mix_out_backward(layer, dh_mid, y1, mixed):
        d_y1, dg_post = rms_bwd(y1, vec(ln_mix_post[layer]), dh_mid, None, bf16, f"mix_post_bwd{layer}")
        dw_out = mm_tn(mixed, d_y1, f"dw_out{layer}")
        d_mixed = mm(d_y1, full["w_out"][layer], f"d_mixed{layer}", trans_b=True)
        return d_mixed, dg_post, dw_out

    def mem_backward(layer, q_src, q_block, kvm, memn, d_mixed):
        d_qm, d_kvm = mem_attn_bwd(q_src, q_block, kvm, d_mixed, f"mem_attn_bwd{layer}")
        d_kvm_b = d_kvm.astype(bf16)
        dw_mkv = mm_tn(memn, d_kvm_b, f"dw_mem_kv{layer}")
        d_memn = mm(d_kvm_b, full["w_mem_kv"][layer], f"d_memn{layer}", trans_b=True)
        _, dg_mem = rms_bwd(mem0, vec(ln_mem[layer]), d_memn, None, bf16, f"mem_norm_bwd{layer}")
        return d_qm, dw_mkv, dg_mem

    dh_mid1, dg_fpost1, dg_fpre1, dw_down1, dw_gu1 = ffn_backward(1, dh, y2_1, hmid1, f1, gu1, act1)
    d_mixed1, dg_mpost1, dw_out1 = mix_out_backward(1, dh_mid1, y1_1, mixed1)
    d_qm1, dw_mkv1, dg_mem1 = mem_backward(1, qb, MAIN_WIDTH // MEM_WIDTH, kvm1, memn1, d_mixed1)
    dq, dk, dv, dc = fox_bwd(qb, kb, vb, d_mixed1, main1, lse, c_col, c_row, "fox_bwd")
    dc_t = jnp.pad(dc.reshape(FOX_HEADS, seq), ((0, 16 - FOX_HEADS), (0, 0)))
    dz_t, db_f = fgate_bwd(dc_t, z_t, bf_col, "fgate_bwd")
    d_kvf = jnp.concatenate([dk, dv, jnp.pad(dz_t[:FOX_HEADS].T.astype(bf16), ((0, 0), (0, KV_PAD - KV_WIDTH)))], axis=-1)
    d_proj1 = jnp.concatenate([dq.astype(bf16), d_qm1], axis=-1)
    dw_in_b = mm_tn(a1, d_proj1, "dw_in_b")
    d_a1 = mm(d_proj1, full["w_in_b"][0], "d_a1", trans_b=True)
    dw_kv = mm_tn(sin1, d_kvf, "dw_kv", tn=256)
    d_sin = mm(d_kvf, w_kv, "d_sin", trans_b=True)
    dh1_a, dg_pre1 = rms_bwd(h1, vec(ln_mix_pre[1]), d_a1, dh_mid1, f32, "mix_pre_bwd1")
    dh1, dg_shared = rms_bwd(h1, vec(ln_shared), d_sin, dh1_a, f32, "shared_norm_bwd")

    dh_mid0, dg_fpost0, dg_fpre0, dw_down0, dw_gu0 = ffn_backward(0, dh1, y2_0, hmid0, f0, gu0, act0)
    d_mixed0, dg_mpost0, dw_out0 = mix_out_backward(0, dh_mid0, y1_0, mixed0)
    d_qm0, dw_mkv0, dg_mem0 = mem_backward(0, proj0, 2 * MAIN_WIDTH // MEM_WIDTH, kvm0, memn0, d_mixed0)
    d_uv, dw_s, db_s, dg_lnv, db_lnv = gmlp_bwd(proj0, d_mixed0, ws, ws_t, bs_t, lnv_g, lnv_b, "gmlp_bwd")
    d_proj0 = jnp.concatenate([d_uv, d_qm0], axis=-1)
    dw_in_a = mm_tn(a0, d_proj0, "dw_in_a", tn=256)
    d_a0 = mm(d_proj0, full["w_in_a"][0], "d_a0", trans_b=True)
    grad_x, dg_pre0 = rms_bwd(h0, vec(ln_mix_pre[0]), d_a0, dh_mid0, f32, "mix_pre_bwd0")

    grads["w_mem_kv"] = jnp.stack([dw_mkv0, dw_mkv1])
    grads["w_out"] = jnp.stack([dw_out0, dw_out1])
    grads["w_ffn_gate"] = jnp.stack([dw_gu0[:, :D_FF], dw_gu1[:, :D_FF]])
    grads["w_ffn_up"] = jnp.stack([dw_gu0[:, D_FF:], dw_gu1[:, D_FF:]])
    grads["w_ffn_down"] = jnp.stack([dw_down0, dw_down1])
    grads["w_in_a"] = dw_in_a[None]
    grads["w_shared_kv"] = dw_kv[None, :, :KV_WIDTH]
    grads["w_in_b"] = dw_in_b[None]
    gflat = _pack_grads(grads)
    by_core = gflat.reshape(4, 2, FLAT_ROWS, FLAT_W)
    keep = lax.dynamic_index_in_dim(by_core, mc, axis=1, keepdims=False)
    send = lax.dynamic_index_in_dim(by_core, 1 - mc, axis=1, keepdims=False)
    from_sibling = exchange_sibling(send, "rs_sibling")
    chip_sum = add_pairs(keep, from_sibling, "rs_add_sibling", tr=FLAT_TILE)
    from_chips = exchange_chips(chip_sum, "rs_chips")
    own = lax.dynamic_index_in_dim(chip_sum, 2 * mx + my, axis=0, keepdims=False)
    g_local = _unpack_local(sum_own_and_received(own, from_chips, "rs_sum", tr=FLAT_TILE))
    g_local["w_shared_kv"] = g_local["w_shared_kv"][0]

    small["ln_mix_pre"] = jnp.concatenate([dg_pre0, dg_pre1], axis=0)
    small["ln_mix_post"] = jnp.concatenate([dg_mpost0, dg_mpost1], axis=0)
    small["ln_ffn_pre"] = jnp.concatenate([dg_fpre0, dg_fpre1], axis=0)
    small["ln_ffn_post"] = jnp.concatenate([dg_fpost0, dg_fpost1], axis=0)
    small["ln_mem"] = jnp.concatenate([dg_mem0, dg_mem1], axis=0)
    small["w_spatial"] = dw_s[None]
    small["b_spatial"] = db_s[:, :A_GROUPS].T[None]
    small["ln_shared"] = dg_shared[0]
    small["b_forget"] = db_f[:FOX_HEADS, 0]
    small["ln_v_g"] = dg_lnv
    small["ln_v_b"] = db_lnv
    small_sum = sum_leading(all_gather(_pack_small(small, _SMALL), "gather_small_grads"), "sum_small_grads")
    g_small = _unpack_small(small_sum, _SMALL)
    shard = MAIN_WIDTH // N_DEV
    for n in ("ln_v_g", "ln_v_b"):
        g_small[n] = lax.dynamic_slice_in_dim(g_small[n], me * shard, shard, axis=1)
    grad_w = {**g_small, **g_local}

    delta, new_m, new_v = {}, {}, {}
    for n, _, shape in _SHARDED:
        two_d = (-1, weights[n].shape[-1])
        d_, m_, v_ = adamw(weights[n].reshape(two_d), grad_w[n].reshape(two_d), mom_m[n].reshape(two_d), mom_v[n].reshape(two_d),
                           f"adamw_{n}", tr=_row_tile(math.prod(weights[n].shape[:-1])))
        delta[n], new_m[n], new_v[n] = (t.reshape(weights[n].shape) for t in (d_, m_, v_))
    small_local_shapes = [(n, tuple(weights[n].shape)) for n, _ in _SMALL]
    packed = [_pack_small(src, small_local_shapes) for src in (weights, grad_w, mom_m, mom_v)]
    outs = adamw(*packed, "adamw_small", tr=packed[0].shape[0])
    for dst, buf in zip((delta, new_m, new_v), outs):
        dst.update(_unpack_small(buf, small_local_shapes))

    return (loss, grad_x[None], *[grad_w[n] for n in names], *[delta[n] for n in names],
            *[new_m[n] for n in names], *[new_v[n] for n in names])
```

```python
import functools
import math

import jax
import jax.numpy as jnp
from jax import lax
from jax.experimental import pallas as pl
from jax.experimental.pallas import tpu as pltpu

f32 = jnp.float32
bf16 = jnp.bfloat16
SDS = jax.ShapeDtypeStruct

D_MODEL = 1024
MAIN_WIDTH = 768
MEM_WIDTH = 256
HEAD_DIM = 64
MEM_HEADS = 4
FOX_HEADS = 12
FOX_PAIRS = FOX_HEADS // 2
CHUNK = 128
A_GROUPS = 6
FF_SHARD = 352
FF_SHARD_PAD = 384
FF_PAD = 8 * FF_SHARD_PAD
KV_WIDTH = 2 * MAIN_WIDTH + FOX_HEADS
KV_PAD = 1792
RMS_EPS = 1e-6
LN_EPS = 1e-5
ATT_SCALE = HEAD_DIM ** -0.5
ADAM_LR, ADAM_B1, ADAM_B2, ADAM_EPS, ADAM_WD, ADAM_STEP = 0.001, 0.9, 0.999, 1e-08, 0.01, 10
N_DEV = 8
AXES = ("x", "y", "c")
MESH = pl.DeviceIdType.MESH
V7X_VMEM_LIMIT = 56 * 1024 * 1024
LANES = 128
FLAT_W = 512
ROW_PAD = 16


def _cparams(*sem):
    return pltpu.CompilerParams(dimension_semantics=sem or None, vmem_limit_bytes=V7X_VMEM_LIMIT)


def _dot(a, b):
    return jnp.dot(a, b, preferred_element_type=f32)


def _dot_nt(a, b):
    return lax.dot_general(a, b, (((1,), (1,)), ((), ())), preferred_element_type=f32)


def _dot_tn(a, b):
    return lax.dot_general(a, b, (((0,), (0,)), ((), ())), preferred_element_type=f32)


def _gelu(x):
    k = math.sqrt(2.0 / math.pi)
    t = jnp.tanh(k * (x + 0.044715 * x * x * x))
    return 0.5 * x * (1.0 + t), t


def _gelu_grad(x, t):
    k = math.sqrt(2.0 / math.pi)
    return 0.5 * (1.0 + t) + 0.5 * x * (1.0 - t * t) * k * (1.0 + 3.0 * 0.044715 * x * x)


def _sigmoid(x):
    return 1.0 / (1.0 + jnp.exp(-x))


def rms_fwd(x, gains, name, tm=512):
    m, d = x.shape
    tm = min(tm, m)
    n = len(gains)

    def body(x_ref, *refs):
        xv = x_ref[...]
        y = xv * lax.rsqrt(jnp.sum(xv * xv, axis=-1, keepdims=True) * (1.0 / d) + RMS_EPS)
        for g_ref, o_ref in zip(refs[:n], refs[n:]):
            o_ref[...] = (y * g_ref[...]).astype(bf16)

    row = pl.BlockSpec((tm, d), lambda i: (i, 0))
    vec = pl.BlockSpec((1, d), lambda i: (0, 0))
    return pl.pallas_call(body, grid=(m // tm,), in_specs=[row] + [vec] * n, out_specs=[row] * n,
                          out_shape=[SDS((m, d), bf16)] * n, name=name, compiler_params=_cparams("parallel"))(x, *gains)


def resnorm(h, y, g_post, gains, name, tm=512):
    m, d = h.shape
    n = len(gains)

    def body(h_ref, y_ref, gp_ref, *refs):
        yv = y_ref[...]
        yn = yv * lax.rsqrt(jnp.sum(yv * yv, axis=-1, keepdims=True) * (1.0 / d) + RMS_EPS)
        hn = h_ref[...] + yn * gp_ref[...]
        refs[n][...] = hn
        if n:
            z = hn * lax.rsqrt(jnp.sum(hn * hn, axis=-1, keepdims=True) * (1.0 / d) + RMS_EPS)
            for g_ref, o_ref in zip(refs[:n], refs[n + 1:]):
                o_ref[...] = (z * g_ref[...]).astype(bf16)

    row = pl.BlockSpec((tm, d), lambda i: (i, 0))
    vec = pl.BlockSpec((1, d), lambda i: (0, 0))
    return pl.pallas_call(body, grid=(m // tm,), in_specs=[row, row, vec] + [vec] * n, out_specs=[row] * (n + 1),
                          out_shape=[SDS((m, d), f32)] + [SDS((m, d), bf16)] * n, name=name,
                          compiler_params=_cparams("parallel"))(h, y, g_post, *gains)


def rms_bwd(x, g, dy, add, out_dtype, name, tm=512):
    m, d = x.shape
    tm = min(tm, m)
    has_add = add is not None

    def body(x_ref, g_ref, dy_ref, *refs):
        dx_ref, dg_ref = refs[-2], refs[-1]
        xv = x_ref[...]
        dyv = dy_ref[...].astype(f32)
        r = lax.rsqrt(jnp.sum(xv * xv, axis=-1, keepdims=True) * (1.0 / d) + RMS_EPS)
        xn = xv * r
        dyg = dyv * g_ref[...]
        dx = r * (dyg - xn * (jnp.sum(dyg * xn, axis=-1, keepdims=True) * (1.0 / d)))
        if has_add:
            dx = dx + refs[0][...]
        dx_ref[...] = dx.astype(out_dtype)

        @pl.when(pl.program_id(0) == 0)
        def _():
            dg_ref[...] = jnp.zeros_like(dg_ref)

        dg_ref[...] += jnp.sum(dyv * xn, axis=0, keepdims=True)

    row = pl.BlockSpec((tm, d), lambda i: (i, 0))
    vec = pl.BlockSpec((1, d), lambda i: (0, 0))
    ins = [x, g, dy] + ([add] if has_add else [])
    return pl.pallas_call(body, grid=(m // tm,), in_specs=[row, vec, row] + ([row] if has_add else []),
                          out_specs=[row, vec], out_shape=[SDS((m, d), out_dtype), SDS((1, d), f32)], name=name,
                          compiler_params=_cparams("arbitrary"))(*ins)


def loss_grad(h, tgt, name, tm=512):
    m, d = h.shape

    def body(h_ref, t_ref, dy_ref, l_ref):
        e = h_ref[...] - t_ref[...]
        dy_ref[...] = e * (1.0 / d)

        @pl.when(pl.program_id(0) == 0)
        def _():
            l_ref[...] = jnp.zeros_like(l_ref)

        part = jnp.sum(jnp.sum(e * e, axis=-1, keepdims=True), axis=0, keepdims=True) * (0.5 / d)
        l_ref[...] += jnp.broadcast_to(part, l_ref.shape)

    row = pl.BlockSpec((tm, d), lambda i: (i, 0))
    return pl.pallas_call(body, grid=(m // tm,), in_specs=[row, row],
                          out_specs=[row, pl.BlockSpec((8, LANES), lambda i: (0, 0))],
                          out_shape=[SDS((m, d), f32), SDS((8, LANES), f32)], name=name,
                          compiler_params=_cparams("arbitrary"))(h, tgt)


def mm(a, b, name, trans_b=False, out_dtype=f32, tm=512, tn=512, layer=None, col0=0, ncols=None):
    m, k = a.shape
    n_all = b.shape[-2] if trans_b else b.shape[-1]
    n = n_all if ncols is None else ncols
    tm, tn = min(tm, m), min(tn, n)
    assert m % tm == 0 and n % tn == 0 and col0 % tn == 0 and not (trans_b and col0), (name, m, n, tm, tn)
    jb = col0 // tn
    lead = () if layer is None else (None,)
    sel = () if layer is None else (layer,)

    def body(a_ref, b_ref, o_ref):
        r = _dot_nt(a_ref[...], b_ref[...]) if trans_b else _dot(a_ref[...], b_ref[...])
        o_ref[...] = r.astype(out_dtype)

    if trans_b:
        b_spec = pl.BlockSpec(lead + (tn, k), lambda j, i: sel + (j, 0))
    else:
        b_spec = pl.BlockSpec(lead + (k, tn), lambda j, i: sel + (0, jb + j))
    return pl.pallas_call(body, grid=(n // tn, m // tm), in_specs=[pl.BlockSpec((tm, k), lambda j, i: (i, 0)), b_spec],
                          out_specs=pl.BlockSpec((tm, tn), lambda j, i: (i, j)), out_shape=SDS((m, n), out_dtype),
                          name=name, compiler_params=_cparams("parallel", "parallel"))(a, b)


def mm_tn(a, g, name, tk=512, tn=512, out_dtype=bf16, col0=0, ncols=None):
    s, k = a.shape
    n = g.shape[1] if ncols is None else ncols
    tk, tn = min(tk, k), min(tn, n)
    assert k % tk == 0 and n % tn == 0 and col0 % tn == 0, (name, k, n, tk, tn)
    jb = col0 // tn

    def body(a_ref, g_ref, o_ref):
        o_ref[...] = _dot_tn(a_ref[...], g_ref[...]).astype(out_dtype)

    return pl.pallas_call(body, grid=(k // tk, n // tn),
                          in_specs=[pl.BlockSpec((s, tk), lambda i, j: (0, i)), pl.BlockSpec((s, tn), lambda i, j: (0, jb + j))],
                          out_specs=pl.BlockSpec((tk, tn), lambda i, j: (i, j)), out_shape=SDS((k, n), out_dtype), name=name,
                          compiler_params=_cparams("parallel", "parallel"))(a, g)


def mm2_nt(a, b1, b2, name, tm=512, tn=512):
    m = a.shape[0]
    n, ff = b1.shape
    assert a.shape[1] == 2 * ff and m % tm == 0 and n % tn == 0

    def body(a1_ref, a2_ref, b1_ref, b2_ref, o_ref):
        o_ref[...] = _dot_nt(a1_ref[...], b1_ref[...]) + _dot_nt(a2_ref[...], b2_ref[...])

    b_spec = pl.BlockSpec((tn, ff), lambda j, i: (j, 0))
    return pl.pallas_call(body, grid=(n // tn, m // tm),
                          in_specs=[pl.BlockSpec((tm, ff), lambda j, i: (i, 0)), pl.BlockSpec((tm, ff), lambda j, i: (i, 1)), b_spec, b_spec],
                          out_specs=pl.BlockSpec((tm, tn), lambda j, i: (i, j)), out_shape=SDS((m, n), f32), name=name,
                          compiler_params=_cparams("parallel", "parallel"))(a, a, b1, b2)


def ffn_up(f, wg, wu, name, tm=256, tc=256):
    s, d = f.shape
    ff = wg.shape[-1]

    def body(f_ref, wg_ref, wu_ref, gu_ref, act_ref):
        fv = f_ref[...]
        for j in range(ff // tc):
            lo = j * tc
            gg = _dot(fv, wg_ref[:, lo:lo + tc])
            uu = _dot(fv, wu_ref[:, lo:lo + tc])
            gu_ref[:, lo:lo + tc] = gg.astype(bf16)
            gu_ref[:, ff + lo:ff + lo + tc] = uu.astype(bf16)
            act_ref[:, lo:lo + tc] = (gg * _sigmoid(gg) * uu).astype(bf16)

    w_spec = pl.BlockSpec((d, ff), lambda i: (0, 0))
    return pl.pallas_call(body, grid=(s // tm,), in_specs=[pl.BlockSpec((tm, d), lambda i: (i, 0)), w_spec, w_spec],
                          out_specs=[pl.BlockSpec((tm, 2 * ff), lambda i: (i, 0)), pl.BlockSpec((tm, ff), lambda i: (i, 0))],
                          out_shape=[SDS((s, 2 * ff), bf16), SDS((s, ff), bf16)], name=name,
                          compiler_params=_cparams("parallel"))(f, wg, wu)


def ffn_act_bwd(gu, d_act, name, tm=256):
    s, ff2 = gu.shape
    ff = ff2 // 2

    def body(gu_ref, da_ref, o_ref):
        gg = gu_ref[:, :ff].astype(f32)
        uu = gu_ref[:, ff:].astype(f32)
        da = da_ref[...]
        sg = _sigmoid(gg)
        o_ref[:, :ff] = (da * uu * (sg * (1.0 + gg * (1.0 - sg)))).astype(bf16)
        o_ref[:, ff:] = (da * gg * sg).astype(bf16)

    return pl.pallas_call(body, grid=(s // tm,),
                          in_specs=[pl.BlockSpec((tm, ff2), lambda i: (i, 0)), pl.BlockSpec((tm, ff), lambda i: (i, 0))],
                          out_specs=pl.BlockSpec((tm, ff2), lambda i: (i, 0)), out_shape=SDS((s, ff2), bf16), name=name,
                          compiler_params=_cparams("parallel"))(gu, d_act)


def _gmlp_forward_chunk(u, v, w_refs, bias, ln_g, ln_b):
    gu, tu = _gelu(u)
    gv, tv = _gelu(v)
    mu = jnp.sum(gv, axis=-1, keepdims=True) * (1.0 / MAIN_WIDTH)
    xc = gv - mu
    rstd = lax.rsqrt(jnp.sum(xc * xc, axis=-1, keepdims=True) * (1.0 / MAIN_WIDTH) + LN_EPS)
    xhat = xc * rstd
    vln = xhat * ln_g + ln_b
    row = lax.broadcasted_iota(jnp.int32, (CHUNK, CHUNK), 0)
    col = lax.broadcasted_iota(jnp.int32, (CHUNK, CHUNK), 1)
    s_parts = []
    for g in range(A_GROUPS):
        w = jnp.where(col <= row, w_refs[g], jnp.zeros((), bf16))
        s_parts.append(_dot(w, vln[:, g * CHUNK:(g + 1) * CHUNK].astype(bf16)) + bias[:, g:g + 1])
    return gu, tu, tv, rstd, xhat, vln, s_parts


def gmlp_fwd(proj, ws, bs_t, ln_g, ln_b, name, tm=512):
    s = proj.shape[0]

    def body(u_ref, v_ref, w_ref, b_ref, g_ref, bb_ref, o_ref):
        bias = b_ref[...]
        for c in range(tm // CHUNK):
            rows = slice(c * CHUNK, (c + 1) * CHUNK)
            gu, _, _, _, _, _, s_parts = _gmlp_forward_chunk(u_ref[rows, :], v_ref[rows, :], w_ref, bias, g_ref[...], bb_ref[...])
            for g in range(A_GROUPS):
                cols = slice(g * CHUNK, (g + 1) * CHUNK)
                o_ref[rows, cols] = (gu[:, cols] * s_parts[g]).astype(bf16)

    vec = pl.BlockSpec((1, MAIN_WIDTH), lambda i: (0, 0))
    return pl.pallas_call(
        body, grid=(s // tm,),
        in_specs=[pl.BlockSpec((tm, MAIN_WIDTH), lambda i: (i, 0)), pl.BlockSpec((tm, MAIN_WIDTH), lambda i: (i, 1)),
                  pl.BlockSpec((A_GROUPS, CHUNK, CHUNK), lambda i: (0, 0, 0)), pl.BlockSpec((CHUNK, A_GROUPS), lambda i: (0, 0)), vec, vec],
        out_specs=pl.BlockSpec((tm, MAIN_WIDTH), lambda i: (i, 0)), out_shape=SDS((s, MAIN_WIDTH), bf16), name=name,
        compiler_params=_cparams("parallel"))(proj, proj, ws, bs_t, ln_g, ln_b)


def gmlp_bwd(proj, d_mixed, ws, ws_t, bs_t, ln_g, ln_b, name, tm=512):
    s = proj.shape[0]

    def body(u_ref, v_ref, dm_ref, w_ref, wt_ref, b_ref, g_ref, bb_ref, duv_ref, dw_ref, db_ref, dg_ref, dbb_ref):
        @pl.when(pl.program_id(0) == 0)
        def _():
            dw_ref[...] = jnp.zeros_like(dw_ref)
            db_ref[...] = jnp.zeros_like(db_ref)
            dg_ref[...] = jnp.zeros_like(dg_ref)
            dbb_ref[...] = jnp.zeros_like(dbb_ref)

        bias = b_ref[...]
        ln_gv = g_ref[...]
        row = lax.broadcasted_iota(jnp.int32, (CHUNK, CHUNK), 0)
        col = lax.broadcasted_iota(jnp.int32, (CHUNK, CHUNK), 1)
        lane = lax.broadcasted_iota(jnp.int32, (CHUNK, LANES), 1)
        for c in range(tm // CHUNK):
            rows = slice(c * CHUNK, (c + 1) * CHUNK)
            u = u_ref[rows, :]
            v = v_ref[rows, :]
            gu, tu, tv, rstd, xhat, vln, s_parts = _gmlp_forward_chunk(u, v, w_ref, bias, ln_gv, bb_ref[...])
            dm = dm_ref[rows, :]
            d_vln_parts = []
            d_gu_parts = []
            db_acc = jnp.zeros((CHUNK, LANES), f32)
            for g in range(A_GROUPS):
                cols = slice(g * CHUNK, (g + 1) * CHUNK)
                dmg = dm[:, cols]
                d_gu_parts.append(dmg * s_parts[g])
                d_s = dmg * gu[:, cols]
                db_acc = db_acc + jnp.where(lane == g, jnp.sum(d_s, axis=-1, keepdims=True), 0.0)
                d_sb = d_s.astype(bf16)
                dw_ref[g] += jnp.where(col <= row, _dot_nt(d_sb, vln[:, cols].astype(bf16)), 0.0)
                wt = jnp.where(row <= col, wt_ref[g], jnp.zeros((), bf16))
                d_vln_parts.append(_dot(wt, d_sb))
            db_ref[...] += db_acc
            d_vln = jnp.concatenate(d_vln_parts, axis=-1)
            d_gu = jnp.concatenate(d_gu_parts, axis=-1)
            dg_ref[...] += jnp.sum(d_vln * xhat, axis=0, keepdims=True)
            dbb_ref[...] += jnp.sum(d_vln, axis=0, keepdims=True)
            dxh = d_vln * ln_gv
            m1 = jnp.sum(dxh, axis=-1, keepdims=True) * (1.0 / MAIN_WIDTH)
            m2 = jnp.sum(dxh * xhat, axis=-1, keepdims=True) * (1.0 / MAIN_WIDTH)
            d_gv = rstd * (dxh - m1 - xhat * m2)
            duv_ref[rows, :MAIN_WIDTH] = (d_gu * _gelu_grad(u, tu)).astype(bf16)
            duv_ref[rows, MAIN_WIDTH:] = (d_gv * _gelu_grad(v, tv)).astype(bf16)

    vec = pl.BlockSpec((1, MAIN_WIDTH), lambda i: (0, 0))
    wspec = pl.BlockSpec((A_GROUPS, CHUNK, CHUNK), lambda i: (0, 0, 0))
    return pl.pallas_call(
        body, grid=(s // tm,),
        in_specs=[pl.BlockSpec((tm, MAIN_WIDTH), lambda i: (i, 0)), pl.BlockSpec((tm, MAIN_WIDTH), lambda i: (i, 1)),
                  pl.BlockSpec((tm, MAIN_WIDTH), lambda i: (i, 0)), wspec, wspec, pl.BlockSpec((CHUNK, A_GROUPS), lambda i: (0, 0)), vec, vec],
        out_specs=[pl.BlockSpec((tm, 2 * MAIN_WIDTH), lambda i: (i, 0)), wspec, pl.BlockSpec((CHUNK, LANES), lambda i: (0, 0)), vec, vec],
        out_shape=[SDS((s, 2 * MAIN_WIDTH), bf16), SDS((A_GROUPS, CHUNK, CHUNK), f32), SDS((CHUNK, LANES), f32),
                   SDS((1, MAIN_WIDTH), f32), SDS((1, MAIN_WIDTH), f32)],
        name=name, compiler_params=_cparams("arbitrary"))(proj, proj, d_mixed, ws, ws_t, bs_t, ln_g, ln_b)


def _head_mask(width, h):
    lane = lax.broadcasted_iota(jnp.int32, (1, width), 1)
    return (lane >= h * HEAD_DIM) & (lane < (h + 1) * HEAD_DIM)


def mem_attn_fwd(proj, q_block, kv, name, tm=512):
    s = proj.shape[0]
    n_mem = kv.shape[0]

    def body(q_ref, kv_ref, o_ref):
        q = q_ref[...].astype(f32)
        k = kv_ref[:, :MEM_WIDTH].astype(bf16)
        v = kv_ref[:, MEM_WIDTH:].astype(bf16)
        out = jnp.zeros((tm, MEM_WIDTH), f32)
        for h in range(MEM_HEADS):
            msk = _head_mask(MEM_WIDTH, h)
            qh = jnp.where(msk, q, 0.0).astype(bf16)
            sc = _dot_nt(qh, k) * ATT_SCALE
            e = jnp.exp(sc - jnp.max(sc, axis=-1, keepdims=True))
            p = e / jnp.sum(e, axis=-1, keepdims=True)
            out = jnp.where(msk, _dot(p.astype(bf16), v), out)
        o_ref[...] = out.astype(bf16)

    return pl.pallas_call(body, grid=(s // tm,),
                          in_specs=[pl.BlockSpec((tm, MEM_WIDTH), lambda i: (i, q_block)), pl.BlockSpec((n_mem, 2 * MEM_WIDTH), lambda i: (0, 0))],
                          out_specs=pl.BlockSpec((tm, MEM_WIDTH), lambda i: (i, 0)), out_shape=SDS((s, MEM_WIDTH), bf16), name=name,
                          compiler_params=_cparams("parallel"))(proj, kv)


def mem_attn_bwd(proj, q_block, kv, d_mixed, name, tm=512):
    s = proj.shape[0]
    n_mem = kv.shape[0]

    def body(q_ref, kv_ref, do_ref, dq_ref, dkv_ref):
        @pl.when(pl.program_id(0) == 0)
        def _():
            dkv_ref[...] = jnp.zeros_like(dkv_ref)

        q = q_ref[...].astype(f32)
        do = do_ref[...]
        k = kv_ref[:, :MEM_WIDTH].astype(bf16)
        v = kv_ref[:, MEM_WIDTH:].astype(bf16)
        dq = jnp.zeros((tm, MEM_WIDTH), f32)
        dk = jnp.zeros((n_mem, MEM_WIDTH), f32)
        dv = jnp.zeros((n_mem, MEM_WIDTH), f32)
        for h in range(MEM_HEADS):
            msk = _head_mask(MEM_WIDTH, h)
            qh = jnp.where(msk, q, 0.0).astype(bf16)
            doh = jnp.where(msk, do, 0.0).astype(bf16)
            sc = _dot_nt(qh, k) * ATT_SCALE
            e = jnp.exp(sc - jnp.max(sc, axis=-1, keepdims=True))
            p = e / jnp.sum(e, axis=-1, keepdims=True)
            dp = _dot_nt(doh, v)
            ds = p * (dp - jnp.sum(dp * p, axis=-1, keepdims=True))
            dsb = (ds * ATT_SCALE).astype(bf16)
            dq = jnp.where(msk, _dot(dsb, k), dq)
            dk = dk + _dot_tn(dsb, qh)
            dv = dv + _dot_tn(p.astype(bf16), doh)
        dq_ref[...] = dq.astype(bf16)
        dkv_ref[:, :MEM_WIDTH] += dk
        dkv_ref[:, MEM_WIDTH:] += dv

    return pl.pallas_call(
        body, grid=(s // tm,),
        in_specs=[pl.BlockSpec((tm, MEM_WIDTH), lambda i: (i, q_block)), pl.BlockSpec((n_mem, 2 * MEM_WIDTH), lambda i: (0, 0)),
                  pl.BlockSpec((tm, MEM_WIDTH), lambda i: (i, MAIN_WIDTH // MEM_WIDTH))],
        out_specs=[pl.BlockSpec((tm, MEM_WIDTH), lambda i: (i, 0)), pl.BlockSpec((n_mem, 2 * MEM_WIDTH), lambda i: (0, 0))],
        out_shape=[SDS((s, MEM_WIDTH), bf16), SDS((n_mem, 2 * MEM_WIDTH), f32)], name=name,
        compiler_params=_cparams("arbitrary"))(proj, kv, d_mixed)


def _tri(t, upper):
    r = lax.broadcasted_iota(jnp.int32, (t, t), 0)
    c = lax.broadcasted_iota(jnp.int32, (t, t), 1)
    return ((r <= c) if upper else (r >= c)).astype(f32)


def fgate_fwd(z_t, b, name, t=512):
    hh, s = z_t.shape

    def body(z_ref, b_ref, c_ref):
        u = _tri(t, True)
        carry = jnp.zeros((hh, 1), f32)
        for blk in range(s // t):
            x = z_ref[:, blk * t:(blk + 1) * t] + b_ref[...]
            logf = jnp.minimum(x, 0.0) - jnp.log(1.0 + jnp.exp(-jnp.abs(x)))
            y = jnp.dot(logf, u, precision=lax.Precision.HIGHEST, preferred_element_type=f32) + carry
            c_ref[:, blk * t:(blk + 1) * t] = y
            carry = y[:, t - 1:t]

    return pl.pallas_call(body, out_shape=SDS((hh, s), f32), name=name, compiler_params=_cparams())(z_t, b)


def fgate_bwd(dc_t, z_t, b, name, t=512):
    hh, s = z_t.shape

    def body(dc_ref, z_ref, b_ref, dz_ref, db_ref):
        low = _tri(t, False)
        carry = jnp.zeros((hh, 1), f32)
        total = jnp.zeros((hh, 1), f32)
        for blk in reversed(range(s // t)):
            cols = slice(blk * t, (blk + 1) * t)
            y = jnp.dot(dc_ref[:, cols], low, precision=lax.Precision.HIGHEST, preferred_element_type=f32) + carry
            carry = y[:, 0:1]
            dz = y * _sigmoid(-(z_ref[:, cols] + b_ref[...]))
            dz_ref[:, cols] = dz
            total = total + jnp.sum(dz, axis=-1, keepdims=True)
        db_ref[...] = jnp.broadcast_to(total, db_ref.shape)

    return pl.pallas_call(body, out_shape=[SDS((hh, s), f32), SDS((hh, LANES), f32)], name=name,
                          compiler_params=_cparams())(dc_t, z_t, b)


def _pair_masks():
    lane = lax.broadcasted_iota(jnp.int32, (1, LANES), 1)
    return [lane < HEAD_DIM, lane >= HEAD_DIM]


def fox_fwd(q, kv, c_col, c_row, name, tq=512):
    s = kv.shape[0]
    nq = s // tq

    def body(q_ref, k_ref, v_ref, cc_ref, cr_ref, o_ref, lse_ref):
        i = pl.program_id(1)
        qv = q_ref[...]
        masks = _pair_masks()
        row = lax.broadcasted_iota(jnp.int32, (tq, tq), 0)
        col = lax.broadcasted_iota(jnp.int32, (tq, tq), 1)
        res = []
        for hh in range(2):
            qh = jnp.where(masks[hh], qv, jnp.zeros((), bf16))
            ct = cc_ref[:, hh:hh + 1]

            def block(j, carry, diag, qh=qh, ct=ct, hh=hh):
                m, l, acc = carry
                lo = pl.multiple_of(j * tq, tq)
                sc = _dot_nt(qh, k_ref[pl.ds(lo, tq), :]) * ATT_SCALE + (ct - cr_ref[hh:hh + 1, pl.ds(lo, tq)])
                if diag:
                    sc = jnp.where(col <= row, sc, -jnp.inf)
                m_new = jnp.maximum(m, jnp.max(sc, axis=-1, keepdims=True))
                alpha = jnp.exp(m - m_new)
                p = jnp.exp(sc - m_new)
                l = alpha * l + jnp.sum(p, axis=-1, keepdims=True)
                p_hi = p.astype(bf16)
                p_lo = (p - p_hi.astype(f32)).astype(bf16)
                vs = v_ref[pl.ds(lo, tq), :]
                acc = alpha * acc + (_dot(p_hi, vs) + _dot(p_lo, vs))
                return m_new, l, acc

            carry = (jnp.full((tq, 1), -jnp.inf, f32), jnp.zeros((tq, 1), f32), jnp.zeros((tq, LANES), f32))
            carry = lax.fori_loop(0, i, functools.partial(block, diag=False), carry)
            m, l, acc = block(i, carry, True)
            res.append((acc / l, m + jnp.log(l)))
        o_ref[...] = jnp.where(masks[0], res[0][0], res[1][0])
        lse_ref[...] = jnp.where(masks[0], res[0][1], res[1][1])

    return pl.pallas_call(
        body, grid=(FOX_PAIRS, nq),
        in_specs=[pl.BlockSpec((tq, LANES), lambda p, i: (i, p)), pl.BlockSpec((s, LANES), lambda p, i: (0, p)),
                  pl.BlockSpec((s, LANES), lambda p, i: (0, FOX_PAIRS + p)), pl.BlockSpec((None, tq, 2), lambda p, i: (p, i, 0)),
                  pl.BlockSpec((None, 2, s), lambda p, i: (p, 0, 0))],
        out_specs=[pl.BlockSpec((tq, LANES), lambda p, i: (i, p)), pl.BlockSpec((None, tq, LANES), lambda p, i: (p, i, 0))],
        out_shape=[SDS((s, MAIN_WIDTH), f32), SDS((FOX_PAIRS, s, LANES), f32)], name=name,
        compiler_params=_cparams("parallel", "parallel"))(q, kv, kv, c_col, c_row)


def fox_bwd(q, kv, d_mixed, o, lse, c_col, c_row, name, tq=512):
    s = kv.shape[0]
    nq = s // tq

    def body(q_ref, k_ref, v_ref, do_ref, o_ref, lse_ref, cc_ref, cr_ref, dq_ref, dk_ref, dv_ref, dc_ref):
        j = pl.program_id(1)

        @pl.when(j == 0)
        def _():
            dq_ref[...] = jnp.zeros_like(dq_ref)

        masks = _pair_masks()
        row = lax.broadcasted_iota(jnp.int32, (tq, tq), 0)
        col = lax.broadcasted_iota(jnp.int32, (tq, tq), 1)
        kj = k_ref[...]
        vj = v_ref[...]
        lo_j = pl.multiple_of(j * tq, tq)

        def block(i, carry, diag):
            dk, dv, dc0, dc1 = carry
            dcs = [dc0, dc1]
            lo = pl.multiple_of(i * tq, tq)
            qi = q_ref[pl.ds(lo, tq), :]
            doi = do_ref[pl.ds(lo, tq), :]
            prod = doi.astype(bf16).astype(f32) * o_ref[pl.ds(lo, tq), :]
            lse_i = lse_ref[pl.ds(lo, tq), :]
            cc_i = cc_ref[pl.ds(lo, tq), :]
            dq_i = jnp.zeros((tq, LANES), f32)
            for hh in range(2):
                qh = jnp.where(masks[hh], qi, jnp.zeros((), bf16))
                doh = jnp.where(masks[hh], doi, 0.0).astype(bf16)
                delta = jnp.sum(jnp.where(masks[hh], prod, 0.0), axis=-1, keepdims=True)
                sc = _dot_nt(qh, kj) * ATT_SCALE + (cc_i[:, hh:hh + 1] - cr_ref[hh:hh + 1, pl.ds(lo_j, tq)])
                p = jnp.exp(sc - lse_i[:, hh * HEAD_DIM:hh * HEAD_DIM + 1])
                if diag:
                    p = jnp.where(col <= row, p, 0.0)
                dv = dv + _dot_tn(p.astype(bf16), doh)
                ds = p * (_dot_nt(doh, vj) - delta)
                dcs[hh] = dcs[hh] + jnp.sum(ds, axis=0, keepdims=True)
                dsb = (ds * ATT_SCALE).astype(bf16)
                dq_i = jnp.where(masks[hh], _dot(dsb, kj), dq_i)
                dk = dk + _dot_tn(dsb, qh)
            dq_ref[pl.ds(lo, tq), :] += dq_i
            return dk, dv, dcs[0], dcs[1]

        zero = jnp.zeros((tq, LANES), f32)
        zrow = jnp.zeros((1, tq), f32)
        carry = block(j, (zero, zero, zrow, zrow), True)
        dk, dv, dc0, dc1 = lax.fori_loop(j + 1, nq, functools.partial(block, diag=False), carry)
        dk_ref[...] = dk.astype(bf16)
        dv_ref[...] = dv.astype(bf16)
        dc_ref[0:1, :] = -dc0
        dc_ref[1:2, :] = -dc1

    full = lambda p, j: (0, p)
    tile = lambda p, j: (j, p)
    return pl.pallas_call(
        body, grid=(FOX_PAIRS, nq),
        in_specs=[pl.BlockSpec((s, LANES), full), pl.BlockSpec((tq, LANES), tile), pl.BlockSpec((tq, LANES), lambda p, j: (j, FOX_PAIRS + p)),
                  pl.BlockSpec((s, LANES), full), pl.BlockSpec((s, LANES), full), pl.BlockSpec((None, s, LANES), lambda p, j: (p, 0, 0)),
                  pl.BlockSpec((None, s, 2), lambda p, j: (p, 0, 0)), pl.BlockSpec((None, 2, s), lambda p, j: (p, 0, 0))],
        out_specs=[pl.BlockSpec((s, LANES), full), pl.BlockSpec((tq, LANES), tile), pl.BlockSpec((tq, LANES), tile),
                   pl.BlockSpec((None, 2, tq), lambda p, j: (p, 0, j))],
        out_shape=[SDS((s, MAIN_WIDTH), f32), SDS((s, MAIN_WIDTH), bf16), SDS((s, MAIN_WIDTH), bf16), SDS((FOX_PAIRS, 2, s), f32)],
        name=name, compiler_params=_cparams("parallel", "arbitrary"))(q, kv, kv, d_mixed, o, lse, c_col, c_row)


def adamw(w, g, m, v, name, tr=256):
    r, c = w.shape
    tr = min(tr, r)
    assert r % tr == 0, (name, r, tr)
    c1 = 1.0 / (1.0 - ADAM_B1 ** ADAM_STEP)
    c2 = 1.0 / (1.0 - ADAM_B2 ** ADAM_STEP)

    def body(w_ref, g_ref, m_ref, v_ref, d_ref, mo_ref, vo_ref):
        gv = g_ref[...]
        mn = ADAM_B1 * m_ref[...] + (1.0 - ADAM_B1) * gv
        vn = ADAM_B2 * v_ref[...] + (1.0 - ADAM_B2) * gv * gv
        mo_ref[...] = mn
        vo_ref[...] = vn
        d_ref[...] = -ADAM_LR * ((mn * c1) / (jnp.sqrt(vn * c2) + ADAM_EPS) + ADAM_WD * w_ref[...])

    spec = pl.BlockSpec((tr, c), lambda i: (i, 0))
    return pl.pallas_call(body, grid=(r // tr,), in_specs=[spec] * 4, out_specs=[spec] * 3, out_shape=[SDS((r, c), f32)] * 3,
                          name=name, compiler_params=_cparams("parallel"))(w, g, m, v)


def sum_leading(x, name, out_dtype=f32, tr=None):
    n, r, c = x.shape
    tr = tr or r
    assert r % tr == 0

    def body(x_ref, o_ref):
        acc = x_ref[0].astype(f32)
        for k in range(1, n):
            acc = acc + x_ref[k].astype(f32)
        o_ref[...] = acc.astype(out_dtype)

    return pl.pallas_call(body, grid=(r // tr,), in_specs=[pl.BlockSpec((n, tr, c), lambda i: (0, i, 0))],
                          out_specs=pl.BlockSpec((tr, c), lambda i: (i, 0)), out_shape=SDS((r, c), out_dtype), name=name,
                          compiler_params=_cparams("parallel"))(x)


_ANY = pl.BlockSpec(memory_space=pl.ANY)
_DMA = pltpu.SemaphoreType.DMA


_HBM = pl.BlockSpec(memory_space=pltpu.HBM)
_SEM = pl.BlockSpec(memory_space=pltpu.SEMAPHORE)
_EFFECT = pltpu.SideEffectType.DATAFLOW_SIDE_EFFECTING
_FLIPS = [(0, 0, 1), (1, 0, 0), (0, 1, 0), (1, 1, 0), (1, 0, 1), (0, 1, 1), (1, 1, 1)]


def _me():
    return lax.axis_index("x"), lax.axis_index("y"), lax.axis_index("c")


def _peers():
    mx, my, mc = _me()
    return [(jnp.bitwise_xor(mx, fx), jnp.bitwise_xor(my, fy), jnp.bitwise_xor(mc, fc)) for fx, fy, fc in _FLIPS]


def _index(dev):
    return 4 * dev[0] + 2 * dev[1] + dev[2]


def _win(ref, axis, k, size, count=1):
    idx = [slice(None)] * len(ref.shape)
    idx[axis] = pl.ds(k * size, count * size)
    return ref.at[tuple(idx)]


def _hbm(a):
    return pltpu.with_memory_space_constraint(a, pltpu.HBM)


def _exchange_start(srcs, lands, copies_of, name):
    n = len(srcs)

    def body(*refs):
        src = refs[:n]
        send_sems, recv_sems, self_sems = refs[2 * n:2 * n + 3]
        land = refs[3 * n + 3:4 * n + 3]
        token = refs[4 * n + 3]
        me = _index(_me())
        for a in range(n):
            for s_ref, d_ref, peer in copies_of(a, src[a], land[a], me):
                if peer is None:
                    pltpu.make_async_copy(s_ref, d_ref, self_sems.at[a]).start()
                else:
                    pltpu.make_async_remote_copy(src_ref=s_ref, dst_ref=d_ref, send_sem=send_sems.at[a], recv_sem=recv_sems.at[a],
                                                 device_id=peer, device_id_type=MESH).start()
        token[...] = jnp.zeros_like(token)

    outs = pl.pallas_call(
        body, name=name,
        out_shape=(_DMA((n,)), _DMA((n,)), _DMA((n,)), *[pltpu.HBM(s.shape, s.dtype) for s in srcs],
                   *[pltpu.HBM(l.shape, l.dtype) for l in lands], SDS((8, LANES), f32)),
        in_specs=[_HBM] * (2 * n), out_specs=(_SEM, _SEM, _SEM, *[_HBM] * (2 * n), pl.BlockSpec(memory_space=pltpu.VMEM)),
        input_output_aliases={i: 3 + i for i in range(2 * n)},
        compiler_params=pltpu.CompilerParams(has_side_effects=_EFFECT),
    )(*[_hbm(s) for s in srcs], *[_hbm(lax.empty(l.shape, l.dtype)) for l in lands])
    return dict(sems=outs[:3], srcs=list(outs[3:3 + n]), lands=list(outs[3 + n:3 + 2 * n]), token=outs[3 + 2 * n])


def _exchange_wait(started, waits_of, after, name):
    srcs, lands = started["srcs"], started["lands"]
    n = len(srcs)

    def body(*refs):
        src = refs[:n]
        land = refs[n:2 * n]
        send_sems, recv_sems, self_sems = refs[2 * n:2 * n + 3]
        me = _index(_me())
        for a in range(n):
            seven, (s_ref, d_ref) = waits_of(a, src[a], land[a], me)
            both = pltpu.make_async_remote_copy(src_ref=seven, dst_ref=seven, send_sem=send_sems.at[a], recv_sem=recv_sems.at[a],
                                                device_id=_me(), device_id_type=MESH)
            both.wait_send()
            both.wait_recv()
            pltpu.make_async_copy(s_ref, d_ref, self_sems.at[a]).wait()

    outs = pl.pallas_call(
        body, name=name, out_shape=tuple(pltpu.HBM(t.shape, t.dtype) for t in srcs + lands),
        in_specs=[_HBM] * (2 * n) + [_SEM] * 3 + [_ANY], out_specs=tuple([_HBM] * (2 * n)),
        input_output_aliases={i: i for i in range(2 * n)},
        compiler_params=pltpu.CompilerParams(has_side_effects=_EFFECT),
    )(*srcs, *lands, *started["sems"], after)
    return list(outs[n:])


def gather_start(locs, axes, name):
    lands = [SDS(tuple(N_DEV * d if i == ax else d for i, d in enumerate(l.shape)), l.dtype) for l, ax in zip(locs, axes)]

    def copies_of(a, src, land, me):
        mine = _win(land, axes[a], me, src.shape[axes[a]])
        return [(src, mine, peer) for peer in _peers()] + [(src, mine, None)]

    return _exchange_start(locs, lands, copies_of, name)


def gather_wait(started, axes, after, name):
    def waits_of(a, src, land, me):
        size = src.shape[axes[a]]
        return _win(land, axes[a], 0, size, N_DEV - 1), (src, _win(land, axes[a], me, size))

    return _exchange_wait(started, waits_of, after, name)


def scatter_start(grads, axes, name):
    lands = [SDS((N_DEV,) + tuple(d // N_DEV if i == ax else d for i, d in enumerate(g.shape)), g.dtype) for g, ax in zip(grads, axes)]

    def copies_of(a, src, land, me):
        size = src.shape[axes[a]] // N_DEV
        out = [(_win(src, axes[a], _index(peer), size), land.at[me], peer) for peer in _peers()]
        return out + [(_win(src, axes[a], me, size), land.at[me], None)]

    return _exchange_start(grads, lands, copies_of, name)


def scatter_wait(started, axes, after, name):
    def waits_of(a, src, land, me):
        size = src.shape[axes[a]] // N_DEV
        return land.at[pl.ds(0, N_DEV - 1)], (_win(src, axes[a], me, size), land.at[me])

    return _exchange_wait(started, waits_of, after, name)


def _row_tile(rows, cap=512):
    return max(t for t in range(8, min(rows, cap) + 1, 8) if rows % t == 0)


_SMALL = [
    ("ln_mix_pre", (2, 1024)), ("ln_mix_post", (2, 1024)), ("ln_ffn_pre", (2, 1024)), ("ln_ffn_post", (2, 1024)),
    ("ln_mem", (2, 1024)), ("w_spatial", (1, 6, 128, 128)), ("b_spatial", (1, 6, 128)), ("ln_shared", (1024,)),
    ("b_forget", (12,)), ("ln_v_g", (1, 768)), ("ln_v_b", (1, 768)),
]
_SMALL_TILE = 8 * LANES


def _small_rows(shape):
    return -(-math.prod(shape) // _SMALL_TILE) * 8


def _pack_small(vals, shapes):
    parts = []
    for name, shape in shapes:
        flat = vals[name].reshape(-1).astype(f32)
        rows = _small_rows(shape)
        parts.append(jnp.pad(flat, (0, rows * LANES - flat.shape[0])).reshape(rows, LANES))
    return jnp.concatenate(parts, axis=0)


def _unpack_small(buf, shapes):
    out = {}
    lo = 0
    for name, shape in shapes:
        rows = _small_rows(shape)
        out[name] = buf[lo:lo + rows].reshape(-1)[:math.prod(shape)].reshape(shape)
        lo += rows
    return out


def kernel(x, mem, ln_mix_pre, ln_mix_post, ln_ffn_pre, ln_ffn_post, ln_mem, w_mem_kv, w_out, w_ffn_gate, w_ffn_up, w_ffn_down, w_in_a, w_spatial, b_spatial, ln_v_g, ln_v_b, ln_shared, w_shared_kv, b_forget, w_in_b, loss_target, m_ln_mix_pre, m_ln_mix_post, m_ln_ffn_pre, m_ln_ffn_post, m_ln_mem, m_w_mem_kv, m_w_out, m_w_ffn_gate, m_w_ffn_up, m_w_ffn_down, m_w_in_a, m_w_spatial, m_b_spatial, m_ln_v_g, m_ln_v_b, m_ln_shared, m_w_shared_kv, m_b_forget, m_w_in_b, v_ln_mix_pre, v_ln_mix_post, v_ln_ffn_pre, v_ln_ffn_post, v_ln_mem, v_w_mem_kv, v_w_out, v_w_ffn_gate, v_w_ffn_up, v_w_ffn_down, v_w_in_a, v_w_spatial, v_b_spatial, v_ln_v_g, v_ln_v_b, v_ln_shared, v_w_shared_kv, v_b_forget, v_w_in_b):
    weights = dict(ln_mix_pre=ln_mix_pre, ln_mix_post=ln_mix_post, ln_ffn_pre=ln_ffn_pre, ln_ffn_post=ln_ffn_post, ln_mem=ln_mem,
                   w_mem_kv=w_mem_kv, w_out=w_out, w_ffn_gate=w_ffn_gate, w_ffn_up=w_ffn_up, w_ffn_down=w_ffn_down, w_in_a=w_in_a,
                   w_spatial=w_spatial, b_spatial=b_spatial, ln_v_g=ln_v_g, ln_v_b=ln_v_b, ln_shared=ln_shared,
                   w_shared_kv=w_shared_kv, b_forget=b_forget, w_in_b=w_in_b)
    mom_m = dict(ln_mix_pre=m_ln_mix_pre, ln_mix_post=m_ln_mix_post, ln_ffn_pre=m_ln_ffn_pre, ln_ffn_post=m_ln_ffn_post, ln_mem=m_ln_mem,
                 w_mem_kv=m_w_mem_kv, w_out=m_w_out, w_ffn_gate=m_w_ffn_gate, w_ffn_up=m_w_ffn_up, w_ffn_down=m_w_ffn_down, w_in_a=m_w_in_a,
                 w_spatial=m_w_spatial, b_spatial=m_b_spatial, ln_v_g=m_ln_v_g, ln_v_b=m_ln_v_b, ln_shared=m_ln_shared,
                 w_shared_kv=m_w_shared_kv, b_forget=m_b_forget, w_in_b=m_w_in_b)
    mom_v = dict(ln_mix_pre=v_ln_mix_pre, ln_mix_post=v_ln_mix_post, ln_ffn_pre=v_ln_ffn_pre, ln_ffn_post=v_ln_ffn_post, ln_mem=v_ln_mem,
                 w_mem_kv=v_w_mem_kv, w_out=v_w_out, w_ffn_gate=v_w_ffn_gate, w_ffn_up=v_w_ffn_up, w_ffn_down=v_w_ffn_down, w_in_a=v_w_in_a,
                 w_spatial=v_w_spatial, b_spatial=v_b_spatial, ln_v_g=v_ln_v_g, ln_v_b=v_ln_v_b, ln_shared=v_ln_shared,
                 w_shared_kv=v_w_shared_kv, b_forget=v_b_forget, w_in_b=v_w_in_b)
    names = list(weights)
    mx, my, mc = lax.axis_index("x"), lax.axis_index("y"), lax.axis_index("c")
    me = 4 * mx + 2 * my + mc

    h0 = x[0]
    mem0 = mem[0]
    tgt = loss_target[0]
    seq = h0.shape[0]

    vec = lambda a: a.reshape(1, -1)
    pad_to = lambda a, axis, size: jnp.pad(a, [(0, size - a.shape[i] if i == axis else 0) for i in range(a.ndim)])

    def after(tok, a):
        return a + tok[0, 0]

    lnv_loc = pad_to(jnp.concatenate([ln_v_g, ln_v_b], axis=0), 0, 8)
    st_a = gather_start([w_in_a.astype(bf16), pad_to(lnv_loc, 1, LANES)[None]], [0, 0], "gather_a_start")
    st_b = gather_start([after(st_a["token"], w_mem_kv).astype(bf16), w_out.astype(bf16)], [1, 1], "gather_b_start")
    ffn_locs = lambda l, tok: [pad_to(after(tok, w_ffn_gate[l]).astype(bf16), 1, FF_SHARD_PAD),
                               pad_to(w_ffn_up[l].astype(bf16), 1, FF_SHARD_PAD), pad_to(w_ffn_down[l].astype(bf16), 0, FF_SHARD_PAD)]
    st_c = gather_start(ffn_locs(0, st_b["token"]), [1, 1, 0], "gather_c_start")
    st_d = gather_start([after(st_c["token"], w_in_b[0]).astype(bf16), pad_to(w_shared_kv.astype(bf16), 1, KV_PAD)], [0, 0],
                        "gather_d_start")
    st_e = gather_start(ffn_locs(1, st_d["token"]), [1, 1, 0], "gather_e_start")
    ws = w_spatial[0].astype(bf16)
    ws_t = ws.transpose(0, 2, 1)
    bs_t = b_spatial[0].T

    (a0,) = rms_fwd(h0, [after(st_e["token"], vec(ln_mix_pre[0]))], "a0_norm")
    w_in_a8, lnv8 = gather_wait(st_a, [0, 0], a0, "gather_a_wait")
    w_in_a_full = w_in_a8.transpose(1, 0, 2).reshape(D_MODEL, -1)
    lnv_g = lnv8[:, 0, :MAIN_WIDTH // N_DEV].reshape(1, MAIN_WIDTH)
    lnv_b = lnv8[:, 1, :MAIN_WIDTH // N_DEV].reshape(1, MAIN_WIDTH)
    proj0 = mm(a0, w_in_a_full, "proj0", tn=256)
    main0 = gmlp_fwd(proj0, ws, bs_t, lnv_g, lnv_b, "gmlp_fwd")
    w_mkv, w_o = gather_wait(st_b, [1, 1], main0, "gather_b_wait")
    (memn0,) = rms_fwd(mem0, [vec(ln_mem[0])], "mem0_norm")
    kvm0 = mm(memn0, w_mkv, "kvm0", layer=0)
    om0 = mem_attn_fwd(proj0, 2 * MAIN_WIDTH // MEM_WIDTH, kvm0, "mem_attn0")
    mixed0 = jnp.concatenate([main0, om0], axis=-1)
    y1_0 = mm(mixed0, w_o, "mix_out0", layer=0)
    hmid0, f0 = resnorm(h0, y1_0, vec(ln_mix_post[0]), [vec(ln_ffn_pre[0])], "resnorm_mix0")
    w_g0, w_u0, w_d0 = gather_wait(st_c, [1, 1, 0], f0, "gather_c_wait")
    gu0, act0 = ffn_up(f0, w_g0, w_u0, "ffn_up0")
    y2_0 = mm(act0, w_d0, "ffn_down0")
    h1, a1, sin1 = resnorm(hmid0, y2_0, vec(ln_ffn_post[0]), [vec(ln_mix_pre[1]), vec(ln_shared)], "resnorm_ffn0")

    w_inb, w_kv = gather_wait(st_d, [0, 0], sin1, "gather_d_wait")
    kvb = mm(sin1, w_kv, "kv_shared", out_dtype=bf16, ncols=2 * MAIN_WIDTH)
    zf = mm(sin1, w_kv, "forget_logits", tn=256, col0=2 * MAIN_WIDTH, ncols=256)
    qb = mm(a1, w_inb, "proj1", out_dtype=bf16)
    z_t = jnp.pad(zf[:, :FOX_HEADS].T, ((0, 16 - FOX_HEADS), (0, 0)))
    bf_col = jnp.pad(b_forget, (0, 16 - FOX_HEADS)).reshape(16, 1)
    c_t = fgate_fwd(z_t, bf_col, "fgate_fwd")
    c_row = c_t[:FOX_HEADS].reshape(FOX_PAIRS, 2, seq)
    c_col = c_row.transpose(0, 2, 1)
    main1, lse = fox_fwd(qb, kvb, c_col, c_row, "fox_fwd")
    (memn1,) = rms_fwd(mem0, [vec(ln_mem[1])], "mem1_norm")
    kvm1 = mm(memn1, w_mkv, "kvm1", layer=1)
    om1 = mem_attn_fwd(qb, MAIN_WIDTH // MEM_WIDTH, kvm1, "mem_attn1")
    mixed1 = jnp.concatenate([main1.astype(bf16), om1], axis=-1)
    y1_1 = mm(mixed1, w_o, "mix_out1", layer=1)
    hmid1, f1 = resnorm(h1, y1_1, vec(ln_mix_post[1]), [vec(ln_ffn_pre[1])], "resnorm_mix1")
    w_g1, w_u1, w_d1 = gather_wait(st_e, [1, 1, 0], f1, "gather_e_wait")
    gu1, act1 = ffn_up(f1, w_g1, w_u1, "ffn_up1")
    y2_1 = mm(act1, w_d1, "ffn_down1")
    (h2,) = resnorm(hmid1, y2_1, vec(ln_ffn_post[1]), [], "resnorm_ffn1")
    dh, loss_tile = loss_grad(h2, tgt, "loss")
    loss = lax.psum(loss_tile[0, 0], AXES)
    ffn_w = [(w_g0, w_u0, w_d0), (w_g1, w_u1, w_d1)]

    small = {}

    def ffn_backward(layer, dh_out, y2, hmid, f, gu, act):
        w_g, w_u, w_d = ffn_w[layer]
        d_y2, dg_post = rms_bwd(y2, vec(ln_ffn_post[layer]), dh_out, None, bf16, f"ffn_post_bwd{layer}")
        d_act = mm(d_y2, w_d, f"d_act{layer}", trans_b=True, tn=FF_PAD // 2)
        dw_down = mm_tn(act, d_y2, f"dw_down{layer}", tk=256)
        d_gu = ffn_act_bwd(gu, d_act, f"ffn_act_bwd{layer}")
        dw_g = mm_tn(f, d_gu, f"dw_gate{layer}", ncols=FF_PAD)
        dw_u = mm_tn(f, d_gu, f"dw_up{layer}", col0=FF_PAD, ncols=FF_PAD)
        d_f = mm2_nt(d_gu, w_g, w_u, f"d_f{layer}")
        dh_mid, dg_pre = rms_bwd(hmid, vec(ln_ffn_pre[layer]), d_f, dh_out, f32, f"ffn_pre_bwd{layer}")
        return dh_mid, dg_post, dg_pre, [dw_g, dw_u, dw_down]

    def mix_out_backward(layer, dh_mid, y1, mixed):
        d_y1, dg_post = rms_bwd(y1, vec(ln_mix_post[layer]), dh_mid, None, bf16, f"mix_post_bwd{layer}")
        dw_out = mm_tn(mixed, d_y1, f"dw_out{layer}")
        d_mixed = mm(d_y1, w_o, f"d_mixed{layer}", trans_b=True, layer=layer)
        return d_mixed, dg_post, dw_out

    def mem_backward(layer, q_src, q_block, kvm, memn, d_mixed):
        d_qm, d_kvm = mem_attn_bwd(q_src, q_block, kvm, d_mixed, f"mem_attn_bwd{layer}")
        d_kvm_b = d_kvm.astype(bf16)
        dw_mkv = mm_tn(memn, d_kvm_b, f"dw_mem_kv{layer}")
        d_memn = mm(d_kvm_b, w_mkv, f"d_memn{layer}", trans_b=True, layer=layer)
        _, dg_mem = rms_bwd(mem0, vec(ln_mem[layer]), d_memn, None, bf16, f"mem_norm_bwd{layer}")
        return d_qm, dw_mkv, dg_mem

    layer_axes = [1, 1, 0, 0, 0]

    dh_mid1, dg_fpost1, dg_fpre1, dw_ffn1 = ffn_backward(1, dh, y2_1, hmid1, f1, gu1, act1)
    d_mixed1, dg_mpost1, dw_out1 = mix_out_backward(1, dh_mid1, y1_1, mixed1)
    d_qm1, dw_mkv1, dg_mem1 = mem_backward(1, qb, MAIN_WIDTH // MEM_WIDTH, kvm1, memn1, d_mixed1)
    rs_1 = scatter_start(dw_ffn1 + [dw_out1, dw_mkv1], layer_axes, "scatter_layer1_start")
    dq, dk, dv, dc = fox_bwd(qb, kvb, d_mixed1, main1, lse, after(rs_1["token"], c_col), c_row, "fox_bwd")
    dc_t = jnp.pad(dc.reshape(FOX_HEADS, seq), ((0, 16 - FOX_HEADS), (0, 0)))
    dz_t, db_f = fgate_bwd(dc_t, z_t, bf_col, "fgate_bwd")
    d_kvf = jnp.concatenate([dk, dv, jnp.pad(dz_t[:FOX_HEADS].T.astype(bf16), ((0, 0), (0, KV_PAD - KV_WIDTH)))], axis=-1)
    d_proj1 = jnp.concatenate([dq.astype(bf16), d_qm1], axis=-1)
    dw_in_b = mm_tn(a1, d_proj1, "dw_in_b")
    d_a1 = mm(d_proj1, w_inb, "d_a1", trans_b=True)
    dw_kv = mm_tn(sin1, d_kvf, "dw_kv", tn=256)
    d_sin = mm(d_kvf, w_kv, "d_sin", trans_b=True)
    rs_2 = scatter_start([dw_in_b, dw_kv], [0, 0], "scatter_shared_start")
    dh1_a, dg_pre1 = rms_bwd(h1, after(rs_2["token"], vec(ln_mix_pre[1])), d_a1, dh_mid1, f32, "mix_pre_bwd1")
    dh1, dg_shared = rms_bwd(h1, vec(ln_shared), d_sin, dh1_a, f32, "shared_norm_bwd")

    dh_mid0, dg_fpost0, dg_fpre0, dw_ffn0 = ffn_backward(0, dh1, y2_0, hmid0, f0, gu0, act0)
    d_mixed0, dg_mpost0, dw_out0 = mix_out_backward(0, dh_mid0, y1_0, mixed0)
    d_qm0, dw_mkv0, dg_mem0 = mem_backward(0, proj0, 2 * MAIN_WIDTH // MEM_WIDTH, kvm0, memn0, d_mixed0)
    rs_3 = scatter_start(dw_ffn0 + [dw_out0, dw_mkv0], layer_axes, "scatter_layer0_start")
    d_uv, dw_s, db_s, dg_lnv, db_lnv = gmlp_bwd(proj0, d_mixed0, ws, ws_t, bs_t, after(rs_3["token"], lnv_g), lnv_b, "gmlp_bwd")
    d_proj0 = jnp.concatenate([d_uv, d_qm0], axis=-1)
    dw_in_a = mm_tn(a0, d_proj0, "dw_in_a", tn=256)
    d_a0 = mm(d_proj0, w_in_a_full, "d_a0", trans_b=True)
    rs_4 = scatter_start([dw_in_a.reshape(D_MODEL, N_DEV, -1).transpose(1, 0, 2)], [0], "scatter_in_a_start")
    grad_x, dg_pre0 = rms_bwd(h0, after(rs_4["token"], vec(ln_mix_pre[0])), d_a0, dh_mid0, f32, "mix_pre_bwd0")

    small["ln_mix_pre"] = jnp.concatenate([dg_pre0, dg_pre1], axis=0)
    small["ln_mix_post"] = jnp.concatenate([dg_mpost0, dg_mpost1], axis=0)
    small["ln_ffn_pre"] = jnp.concatenate([dg_fpre0, dg_fpre1], axis=0)
    small["ln_ffn_post"] = jnp.concatenate([dg_fpost0, dg_fpost1], axis=0)
    small["ln_mem"] = jnp.concatenate([dg_mem0, dg_mem1], axis=0)
    small["w_spatial"] = dw_s[None]
    small["b_spatial"] = db_s[:, :A_GROUPS].T[None]
    small["ln_shared"] = dg_shared[0]
    small["b_forget"] = db_f[:FOX_HEADS, 0]
    small["ln_v_g"] = dg_lnv
    small["ln_v_b"] = db_lnv
    st_small = gather_start([_pack_small(small, _SMALL)[None]], [0], "gather_small_grads_start")

    def owned(started, axes, wait_after, name):
        recv = scatter_wait(started, axes, wait_after, name)
        return [sum_leading(r.reshape((N_DEV, -1, r.shape[-1])), f"{name}_sum{i}", tr=_row_tile(math.prod(r.shape[1:-1])))
                for i, r in enumerate(recv)]

    g1 = owned(rs_1, layer_axes, after(st_small["token"], grad_x[:8, :LANES]), "scatter_layer1_wait")
    g2 = owned(rs_2, [0, 0], g1[0], "scatter_shared_wait")
    g0 = owned(rs_3, layer_axes, g2[0], "scatter_layer0_wait")
    (g_in_a,) = owned(rs_4, [0], g0[0], "scatter_in_a_wait")
    g_local = dict(
        w_ffn_gate=jnp.stack([g0[0], g1[0]])[:, :, :FF_SHARD], w_ffn_up=jnp.stack([g0[1], g1[1]])[:, :, :FF_SHARD],
        w_ffn_down=jnp.stack([g0[2], g1[2]])[:, :FF_SHARD], w_out=jnp.stack([g0[3], g1[3]]), w_mem_kv=jnp.stack([g0[4], g1[4]]),
        w_in_b=g2[0][None], w_shared_kv=g2[1][:, :KV_WIDTH], w_in_a=g_in_a[None])
    (small_all,) = gather_wait(st_small, [0], g_in_a, "gather_small_grads_wait")
    g_small = _unpack_small(sum_leading(small_all, "sum_small_grads"), _SMALL)
    shard = MAIN_WIDTH // N_DEV
    for n in ("ln_v_g", "ln_v_b"):
        g_small[n] = lax.dynamic_slice_in_dim(g_small[n], me * shard, shard, axis=1)
    grad_w = {**g_small, **g_local}

    delta, new_m, new_v = {}, {}, {}
    for n in g_local:
        two_d = (-1, weights[n].shape[-1])
        d_, m_, v_ = adamw(weights[n].reshape(two_d), grad_w[n].reshape(two_d), mom_m[n].reshape(two_d), mom_v[n].reshape(two_d),
                           f"adamw_{n}", tr=_row_tile(math.prod(weights[n].shape[:-1])))
        delta[n], new_m[n], new_v[n] = (t.reshape(weights[n].shape) for t in (d_, m_, v_))
    small_local_shapes = [(n, tuple(weights[n].shape)) for n, _ in _SMALL]
    packed = [_pack_small(src, small_local_shapes) for src in (weights, grad_w, mom_m, mom_v)]
    outs = adamw(*packed, "adamw_small", tr=packed[0].shape[0])
    for dst, buf in zip((delta, new_m, new_v), outs):
        dst.update(_unpack_small(buf, small_local_shapes))

    return (loss, grad_x[None], *[grad_w[n] for n in names], *[delta[n] for n in names],
            *[new_m[n] for n in names], *[new_v[n] for n in names])
```

```python
import functools
import math

import jax
import jax.numpy as jnp
from jax import lax
from jax.experimental import pallas as pl
from jax.experimental.pallas import tpu as pltpu

f32 = jnp.float32
bf16 = jnp.bfloat16
SDS = jax.ShapeDtypeStruct

D_MODEL = 1024
MAIN_WIDTH = 768
MEM_WIDTH = 256
HEAD_DIM = 64
MEM_HEADS = 4
FOX_HEADS = 12
FOX_PAIRS = FOX_HEADS // 2
CHUNK = 128
A_GROUPS = 6
FF_SHARD = 352
FF_SHARD_PAD = 384
FF_PAD = 8 * FF_SHARD_PAD
KV_WIDTH = 2 * MAIN_WIDTH + FOX_HEADS
KV_PAD = 1792
RMS_EPS = 1e-6
LN_EPS = 1e-5
ATT_SCALE = HEAD_DIM ** -0.5
ADAM_LR, ADAM_B1, ADAM_B2, ADAM_EPS, ADAM_WD, ADAM_STEP = 0.001, 0.9, 0.999, 1e-08, 0.01, 10
N_DEV = 8
AXES = ("x", "y", "c")
MESH = pl.DeviceIdType.MESH
V7X_VMEM_LIMIT = 56 * 1024 * 1024
LANES = 128
FLAT_W = 512
ROW_PAD = 16


def _cparams(*sem):
    return pltpu.CompilerParams(dimension_semantics=sem or None, vmem_limit_bytes=V7X_VMEM_LIMIT)


def _dot(a, b):
    return jnp.dot(a, b, preferred_element_type=f32)


def _dot_nt(a, b):
    return lax.dot_general(a, b, (((1,), (1,)), ((), ())), preferred_element_type=f32)


def _dot_tn(a, b):
    return lax.dot_general(a, b, (((0,), (0,)), ((), ())), preferred_element_type=f32)


def _gelu(x):
    k = math.sqrt(2.0 / math.pi)
    t = jnp.tanh(k * (x + 0.044715 * x * x * x))
    return 0.5 * x * (1.0 + t), t


def _gelu_grad(x, t):
    k = math.sqrt(2.0 / math.pi)
    return 0.5 * (1.0 + t) + 0.5 * x * (1.0 - t * t) * k * (1.0 + 3.0 * 0.044715 * x * x)


def _sigmoid(x):
    return 1.0 / (1.0 + jnp.exp(-x))


def rms_fwd(x, gains, name, tm=512):
    m, d = x.shape
    tm = min(tm, m)
    n = len(gains)

    def body(x_ref, *refs):
        xv = x_ref[...]
        y = xv * lax.rsqrt(jnp.sum(xv * xv, axis=-1, keepdims=True) * (1.0 / d) + RMS_EPS)
        for g_ref, o_ref in zip(refs[:n], refs[n:]):
            o_ref[...] = (y * g_ref[...]).astype(bf16)

    row = pl.BlockSpec((tm, d), lambda i: (i, 0))
    vec = pl.BlockSpec((1, d), lambda i: (0, 0))
    return pl.pallas_call(body, grid=(m // tm,), in_specs=[row] + [vec] * n, out_specs=[row] * n,
                          out_shape=[SDS((m, d), bf16)] * n, name=name, compiler_params=_cparams("parallel"))(x, *gains)


def resnorm(h, y, g_post, gains, name, tm=512):
    m, d = h.shape
    n = len(gains)

    def body(h_ref, y_ref, gp_ref, *refs):
        yv = y_ref[...]
        yn = yv * lax.rsqrt(jnp.sum(yv * yv, axis=-1, keepdims=True) * (1.0 / d) + RMS_EPS)
        hn = h_ref[...] + yn * gp_ref[...]
        refs[n][...] = hn
        if n:
            z = hn * lax.rsqrt(jnp.sum(hn * hn, axis=-1, keepdims=True) * (1.0 / d) + RMS_EPS)
            for g_ref, o_ref in zip(refs[:n], refs[n + 1:]):
                o_ref[...] = (z * g_ref[...]).astype(bf16)

    row = pl.BlockSpec((tm, d), lambda i: (i, 0))
    vec = pl.BlockSpec((1, d), lambda i: (0, 0))
    return pl.pallas_call(body, grid=(m // tm,), in_specs=[row, row, vec] + [vec] * n, out_specs=[row] * (n + 1),
                          out_shape=[SDS((m, d), f32)] + [SDS((m, d), bf16)] * n, name=name,
                          compiler_params=_cparams("parallel"))(h, y, g_post, *gains)


def rms_bwd(x, g, dy, add, out_dtype, name, tm=512):
    m, d = x.shape
    tm = min(tm, m)
    has_add = add is not None

    def body(x_ref, g_ref, dy_ref, *refs):
        dx_ref, dg_ref = refs[-2], refs[-1]
        xv = x_ref[...]
        dyv = dy_ref[...].astype(f32)
        r = lax.rsqrt(jnp.sum(xv * xv, axis=-1, keepdims=True) * (1.0 / d) + RMS_EPS)
        xn = xv * r
        dyg = dyv * g_ref[...]
        dx = r * (dyg - xn * (jnp.sum(dyg * xn, axis=-1, keepdims=True) * (1.0 / d)))
        if has_add:
            dx = dx + refs[0][...]
        dx_ref[...] = dx.astype(out_dtype)

        @pl.when(pl.program_id(0) == 0)
        def _():
            dg_ref[...] = jnp.zeros_like(dg_ref)

        dg_ref[...] += jnp.sum(dyv * xn, axis=0, keepdims=True)

    row = pl.BlockSpec((tm, d), lambda i: (i, 0))
    vec = pl.BlockSpec((1, d), lambda i: (0, 0))
    ins = [x, g, dy] + ([add] if has_add else [])
    return pl.pallas_call(body, grid=(m // tm,), in_specs=[row, vec, row] + ([row] if has_add else []),
                          out_specs=[row, vec], out_shape=[SDS((m, d), out_dtype), SDS((1, d), f32)], name=name,
                          compiler_params=_cparams("arbitrary"))(*ins)


def loss_grad(h, tgt, name, tm=512):
    m, d = h.shape

    def body(h_ref, t_ref, dy_ref, l_ref):
        e = h_ref[...] - t_ref[...]
        dy_ref[...] = e * (1.0 / d)

        @pl.when(pl.program_id(0) == 0)
        def _():
            l_ref[...] = jnp.zeros_like(l_ref)

        part = jnp.sum(jnp.sum(e * e, axis=-1, keepdims=True), axis=0, keepdims=True) * (0.5 / d)
        l_ref[...] += jnp.broadcast_to(part, l_ref.shape)

    row = pl.BlockSpec((tm, d), lambda i: (i, 0))
    return pl.pallas_call(body, grid=(m // tm,), in_specs=[row, row],
                          out_specs=[row, pl.BlockSpec((8, LANES), lambda i: (0, 0))],
                          out_shape=[SDS((m, d), f32), SDS((8, LANES), f32)], name=name,
                          compiler_params=_cparams("arbitrary"))(h, tgt)


def mm(a, b, name, trans_b=False, out_dtype=f32, tm=512, tn=1024, layer=None, col0=0, ncols=None, dep=None):
    m, k = a.shape
    n_all = b.shape[-2] if trans_b else b.shape[-1]
    n = n_all if ncols is None else ncols
    tm, tn = min(tm, m), min(tn, n)
    assert m % tm == 0 and n % tn == 0 and col0 % tn == 0 and not (trans_b and col0), (name, m, n, tm, tn)
    jb = col0 // tn
    lead = () if layer is None else (None,)
    sel = () if layer is None else (layer,)

    def body(a_ref, b_ref, *rest):
        r = _dot_nt(a_ref[...], b_ref[...]) if trans_b else _dot(a_ref[...], b_ref[...])
        rest[-1][...] = r.astype(out_dtype)

    if trans_b:
        b_spec = pl.BlockSpec(lead + (tn, k), lambda j, i: sel + (j, 0))
    else:
        b_spec = pl.BlockSpec(lead + (k, tn), lambda j, i: sel + (0, jb + j))
    deps = [] if dep is None else [dep]
    dep_specs = [pl.BlockSpec((8, LANES), lambda j, i: (0, 0))] * len(deps)
    return pl.pallas_call(body, grid=(n // tn, m // tm), in_specs=[pl.BlockSpec((tm, k), lambda j, i: (i, 0)), b_spec] + dep_specs,
                          out_specs=pl.BlockSpec((tm, tn), lambda j, i: (i, j)), out_shape=SDS((m, n), out_dtype),
                          name=name, compiler_params=_cparams("parallel", "parallel"))(a, b, *deps)


def mm_tn(a, g, name, tk=512, tn=1024, out_dtype=bf16, col0=0, ncols=None):
    s, k = a.shape
    n = g.shape[1] if ncols is None else ncols
    tk, tn = min(tk, k), min(tn, n)
    assert k % tk == 0 and n % tn == 0 and col0 % tn == 0, (name, k, n, tk, tn)
    jb = col0 // tn

    def body(a_ref, g_ref, o_ref):
        o_ref[...] = _dot_tn(a_ref[...], g_ref[...]).astype(out_dtype)

    return pl.pallas_call(body, grid=(k // tk, n // tn),
                          in_specs=[pl.BlockSpec((s, tk), lambda i, j: (0, i)), pl.BlockSpec((s, tn), lambda i, j: (0, jb + j))],
                          out_specs=pl.BlockSpec((tk, tn), lambda i, j: (i, j)), out_shape=SDS((k, n), out_dtype), name=name,
                          compiler_params=_cparams("parallel", "parallel"))(a, g)


def mm2_nt(a, b1, b2, name, tm=512, tn=1024):
    m = a.shape[0]
    n, ff = b1.shape
    assert a.shape[1] == 2 * ff and m % tm == 0 and n % tn == 0

    def body(a1_ref, a2_ref, b1_ref, b2_ref, o_ref):
        o_ref[...] = _dot_nt(a1_ref[...], b1_ref[...]) + _dot_nt(a2_ref[...], b2_ref[...])

    b_spec = pl.BlockSpec((tn, ff), lambda j, i: (j, 0))
    return pl.pallas_call(body, grid=(n // tn, m // tm),
                          in_specs=[pl.BlockSpec((tm, ff), lambda j, i: (i, 0)), pl.BlockSpec((tm, ff), lambda j, i: (i, 1)), b_spec, b_spec],
                          out_specs=pl.BlockSpec((tm, tn), lambda j, i: (i, j)), out_shape=SDS((m, n), f32), name=name,
                          compiler_params=_cparams("parallel", "parallel"))(a, a, b1, b2)


def ffn_up(f, wg, wu, name, tm=256, tc=256):
    s, d = f.shape
    ff = wg.shape[-1]

    def body(f_ref, wg_ref, wu_ref, gu_ref, act_ref):
        fv = f_ref[...]
        for j in range(ff // tc):
            lo = j * tc
            gg = _dot(fv, wg_ref[:, lo:lo + tc])
            uu = _dot(fv, wu_ref[:, lo:lo + tc])
            gu_ref[:, lo:lo + tc] = gg.astype(bf16)
            gu_ref[:, ff + lo:ff + lo + tc] = uu.astype(bf16)
            act_ref[:, lo:lo + tc] = (gg * _sigmoid(gg) * uu).astype(bf16)

    w_spec = pl.BlockSpec((d, ff), lambda i: (0, 0))
    return pl.pallas_call(body, grid=(s // tm,), in_specs=[pl.BlockSpec((tm, d), lambda i: (i, 0)), w_spec, w_spec],
                          out_specs=[pl.BlockSpec((tm, 2 * ff), lambda i: (i, 0)), pl.BlockSpec((tm, ff), lambda i: (i, 0))],
                          out_shape=[SDS((s, 2 * ff), bf16), SDS((s, ff), bf16)], name=name,
                          compiler_params=_cparams("parallel"))(f, wg, wu)


def ffn_act_bwd(gu, d_act, name, tm=256):
    s, ff2 = gu.shape
    ff = ff2 // 2

    def body(gu_ref, da_ref, o_ref):
        gg = gu_ref[:, :ff].astype(f32)
        uu = gu_ref[:, ff:].astype(f32)
        da = da_ref[...]
        sg = _sigmoid(gg)
        o_ref[:, :ff] = (da * uu * (sg * (1.0 + gg * (1.0 - sg)))).astype(bf16)
        o_ref[:, ff:] = (da * gg * sg).astype(bf16)

    return pl.pallas_call(body, grid=(s // tm,),
                          in_specs=[pl.BlockSpec((tm, ff2), lambda i: (i, 0)), pl.BlockSpec((tm, ff), lambda i: (i, 0))],
                          out_specs=pl.BlockSpec((tm, ff2), lambda i: (i, 0)), out_shape=SDS((s, ff2), bf16), name=name,
                          compiler_params=_cparams("parallel"))(gu, d_act)


def _gmlp_forward_chunk(u, v, w_refs, bias, ln_g, ln_b):
    gu, tu = _gelu(u)
    gv, tv = _gelu(v)
    mu = jnp.sum(gv, axis=-1, keepdims=True) * (1.0 / MAIN_WIDTH)
    xc = gv - mu
    rstd = lax.rsqrt(jnp.sum(xc * xc, axis=-1, keepdims=True) * (1.0 / MAIN_WIDTH) + LN_EPS)
    xhat = xc * rstd
    vln = xhat * ln_g + ln_b
    row = lax.broadcasted_iota(jnp.int32, (CHUNK, CHUNK), 0)
    col = lax.broadcasted_iota(jnp.int32, (CHUNK, CHUNK), 1)
    s_parts = []
    for g in range(A_GROUPS):
        w = jnp.where(col <= row, w_refs[g], jnp.zeros((), bf16))
        s_parts.append(_dot(w, vln[:, g * CHUNK:(g + 1) * CHUNK].astype(bf16)) + bias[:, g:g + 1])
    return gu, tu, tv, rstd, xhat, vln, s_parts


def gmlp_fwd(proj, ws, bs_t, ln_g, ln_b, name, tm=512):
    s = proj.shape[0]

    def body(u_ref, v_ref, w_ref, b_ref, g_ref, bb_ref, o_ref):
        bias = b_ref[...]
        for c in range(tm // CHUNK):
            rows = slice(c * CHUNK, (c + 1) * CHUNK)
            gu, _, _, _, _, _, s_parts = _gmlp_forward_chunk(u_ref[rows, :], v_ref[rows, :], w_ref, bias, g_ref[...], bb_ref[...])
            for g in range(A_GROUPS):
                cols = slice(g * CHUNK, (g + 1) * CHUNK)
                o_ref[rows, cols] = (gu[:, cols] * s_parts[g]).astype(bf16)

    vec = pl.BlockSpec((1, MAIN_WIDTH), lambda i: (0, 0))
    return pl.pallas_call(
        body, grid=(s // tm,),
        in_specs=[pl.BlockSpec((tm, MAIN_WIDTH), lambda i: (i, 0)), pl.BlockSpec((tm, MAIN_WIDTH), lambda i: (i, 1)),
                  pl.BlockSpec((A_GROUPS, CHUNK, CHUNK), lambda i: (0, 0, 0)), pl.BlockSpec((CHUNK, A_GROUPS), lambda i: (0, 0)), vec, vec],
        out_specs=pl.BlockSpec((tm, MAIN_WIDTH), lambda i: (i, 0)), out_shape=SDS((s, MAIN_WIDTH), bf16), name=name,
        compiler_params=_cparams("parallel"))(proj, proj, ws, bs_t, ln_g, ln_b)


def gmlp_bwd(proj, d_mixed, ws, ws_t, bs_t, ln_g, ln_b, name, tm=512):
    s = proj.shape[0]

    def body(u_ref, v_ref, dm_ref, w_ref, wt_ref, b_ref, g_ref, bb_ref, duv_ref, dw_ref, db_ref, dg_ref, dbb_ref):
        @pl.when(pl.program_id(0) == 0)
        def _():
            dw_ref[...] = jnp.zeros_like(dw_ref)
            db_ref[...] = jnp.zeros_like(db_ref)
            dg_ref[...] = jnp.zeros_like(dg_ref)
            dbb_ref[...] = jnp.zeros_like(dbb_ref)

        bias = b_ref[...]
        ln_gv = g_ref[...]
        row = lax.broadcasted_iota(jnp.int32, (CHUNK, CHUNK), 0)
        col = lax.broadcasted_iota(jnp.int32, (CHUNK, CHUNK), 1)
        lane = lax.broadcasted_iota(jnp.int32, (CHUNK, LANES), 1)
        for c in range(tm // CHUNK):
            rows = slice(c * CHUNK, (c + 1) * CHUNK)
            u = u_ref[rows, :]
            v = v_ref[rows, :]
            gu, tu, tv, rstd, xhat, vln, s_parts = _gmlp_forward_chunk(u, v, w_ref, bias, ln_gv, bb_ref[...])
            dm = dm_ref[rows, :]
            d_vln_parts = []
            d_gu_parts = []
            db_acc = jnp.zeros((CHUNK, LANES), f32)
            for g in range(A_GROUPS):
                cols = slice(g * CHUNK, (g + 1) * CHUNK)
                dmg = dm[:, cols]
                d_gu_parts.append(dmg * s_parts[g])
                d_s = dmg * gu[:, cols]
                db_acc = db_acc + jnp.where(lane == g, jnp.sum(d_s, axis=-1, keepdims=True), 0.0)
                d_sb = d_s.astype(bf16)
                dw_ref[g] += jnp.where(col <= row, _dot_nt(d_sb, vln[:, cols].astype(bf16)), 0.0)
                wt = jnp.where(row <= col, wt_ref[g], jnp.zeros((), bf16))
                d_vln_parts.append(_dot(wt, d_sb))
            db_ref[...] += db_acc
            d_vln = jnp.concatenate(d_vln_parts, axis=-1)
            d_gu = jnp.concatenate(d_gu_parts, axis=-1)
            dg_ref[...] += jnp.sum(d_vln * xhat, axis=0, keepdims=True)
            dbb_ref[...] += jnp.sum(d_vln, axis=0, keepdims=True)
            dxh = d_vln * ln_gv
            m1 = jnp.sum(dxh, axis=-1, keepdims=True) * (1.0 / MAIN_WIDTH)
            m2 = jnp.sum(dxh * xhat, axis=-1, keepdims=True) * (1.0 / MAIN_WIDTH)
            d_gv = rstd * (dxh - m1 - xhat * m2)
            duv_ref[rows, :MAIN_WIDTH] = (d_gu * _gelu_grad(u, tu)).astype(bf16)
            duv_ref[rows, MAIN_WIDTH:] = (d_gv * _gelu_grad(v, tv)).astype(bf16)

    vec = pl.BlockSpec((1, MAIN_WIDTH), lambda i: (0, 0))
    wspec = pl.BlockSpec((A_GROUPS, CHUNK, CHUNK), lambda i: (0, 0, 0))
    return pl.pallas_call(
        body, grid=(s // tm,),
        in_specs=[pl.BlockSpec((tm, MAIN_WIDTH), lambda i: (i, 0)), pl.BlockSpec((tm, MAIN_WIDTH), lambda i: (i, 1)),
                  pl.BlockSpec((tm, MAIN_WIDTH), lambda i: (i, 0)), wspec, wspec, pl.BlockSpec((CHUNK, A_GROUPS), lambda i: (0, 0)), vec, vec],
        out_specs=[pl.BlockSpec((tm, 2 * MAIN_WIDTH), lambda i: (i, 0)), wspec, pl.BlockSpec((CHUNK, LANES), lambda i: (0, 0)), vec, vec],
        out_shape=[SDS((s, 2 * MAIN_WIDTH), bf16), SDS((A_GROUPS, CHUNK, CHUNK), f32), SDS((CHUNK, LANES), f32),
                   SDS((1, MAIN_WIDTH), f32), SDS((1, MAIN_WIDTH), f32)],
        name=name, compiler_params=_cparams("arbitrary"))(proj, proj, d_mixed, ws, ws_t, bs_t, ln_g, ln_b)


def _head_mask(width, h):
    lane = lax.broadcasted_iota(jnp.int32, (1, width), 1)
    return (lane >= h * HEAD_DIM) & (lane < (h + 1) * HEAD_DIM)


def mem_attn_fwd(proj, q_block, kv, name, tm=512):
    s = proj.shape[0]
    n_mem = kv.shape[0]

    def body(q_ref, kv_ref, o_ref):
        q = q_ref[...].astype(f32)
        k = kv_ref[:, :MEM_WIDTH].astype(bf16)
        v = kv_ref[:, MEM_WIDTH:].astype(bf16)
        out = jnp.zeros((tm, MEM_WIDTH), f32)
        for h in range(MEM_HEADS):
            msk = _head_mask(MEM_WIDTH, h)
            qh = jnp.where(msk, q, 0.0).astype(bf16)
            sc = _dot_nt(qh, k) * ATT_SCALE
            e = jnp.exp(sc - jnp.max(sc, axis=-1, keepdims=True))
            p = e / jnp.sum(e, axis=-1, keepdims=True)
            out = jnp.where(msk, _dot(p.astype(bf16), v), out)
        o_ref[...] = out.astype(bf16)

    return pl.pallas_call(body, grid=(s // tm,),
                          in_specs=[pl.BlockSpec((tm, MEM_WIDTH), lambda i: (i, q_block)), pl.BlockSpec((n_mem, 2 * MEM_WIDTH), lambda i: (0, 0))],
                          out_specs=pl.BlockSpec((tm, MEM_WIDTH), lambda i: (i, 0)), out_shape=SDS((s, MEM_WIDTH), bf16), name=name,
                          compiler_params=_cparams("parallel"))(proj, kv)


def mem_attn_bwd(proj, q_block, kv, d_mixed, name, tm=512):
    s = proj.shape[0]
    n_mem = kv.shape[0]

    def body(q_ref, kv_ref, do_ref, dq_ref, dkv_ref):
        @pl.when(pl.program_id(0) == 0)
        def _():
            dkv_ref[...] = jnp.zeros_like(dkv_ref)

        q = q_ref[...].astype(f32)
        do = do_ref[...]
        k = kv_ref[:, :MEM_WIDTH].astype(bf16)
        v = kv_ref[:, MEM_WIDTH:].astype(bf16)
        dq = jnp.zeros((tm, MEM_WIDTH), f32)
        dk = jnp.zeros((n_mem, MEM_WIDTH), f32)
        dv = jnp.zeros((n_mem, MEM_WIDTH), f32)
        for h in range(MEM_HEADS):
            msk = _head_mask(MEM_WIDTH, h)
            qh = jnp.where(msk, q, 0.0).astype(bf16)
            doh = jnp.where(msk, do, 0.0).astype(bf16)
            sc = _dot_nt(qh, k) * ATT_SCALE
            e = jnp.exp(sc - jnp.max(sc, axis=-1, keepdims=True))
            p = e / jnp.sum(e, axis=-1, keepdims=True)
            dp = _dot_nt(doh, v)
            ds = p * (dp - jnp.sum(dp * p, axis=-1, keepdims=True))
            dsb = (ds * ATT_SCALE).astype(bf16)
            dq = jnp.where(msk, _dot(dsb, k), dq)
            dk = dk + _dot_tn(dsb, qh)
            dv = dv + _dot_tn(p.astype(bf16), doh)
        dq_ref[...] = dq.astype(bf16)
        dkv_ref[:, :MEM_WIDTH] += dk
        dkv_ref[:, MEM_WIDTH:] += dv

    return pl.pallas_call(
        body, grid=(s // tm,),
        in_specs=[pl.BlockSpec((tm, MEM_WIDTH), lambda i: (i, q_block)), pl.BlockSpec((n_mem, 2 * MEM_WIDTH), lambda i: (0, 0)),
                  pl.BlockSpec((tm, MEM_WIDTH), lambda i: (i, MAIN_WIDTH // MEM_WIDTH))],
        out_specs=[pl.BlockSpec((tm, MEM_WIDTH), lambda i: (i, 0)), pl.BlockSpec((n_mem, 2 * MEM_WIDTH), lambda i: (0, 0))],
        out_shape=[SDS((s, MEM_WIDTH), bf16), SDS((n_mem, 2 * MEM_WIDTH), f32)], name=name,
        compiler_params=_cparams("arbitrary"))(proj, kv, d_mixed)


def _tri(t, upper):
    r = lax.broadcasted_iota(jnp.int32, (t, t), 0)
    c = lax.broadcasted_iota(jnp.int32, (t, t), 1)
    return ((r <= c) if upper else (r >= c)).astype(f32)


def fgate_fwd(z_t, b, name, t=512):
    hh, s = z_t.shape

    def body(z_ref, b_ref, c_ref):
        u = _tri(t, True)
        carry = jnp.zeros((hh, 1), f32)
        for blk in range(s // t):
            x = z_ref[:, blk * t:(blk + 1) * t] + b_ref[...]
            logf = jnp.minimum(x, 0.0) - jnp.log(1.0 + jnp.exp(-jnp.abs(x)))
            y = jnp.dot(logf, u, precision=lax.Precision.HIGHEST, preferred_element_type=f32) + carry
            c_ref[:, blk * t:(blk + 1) * t] = y
            carry = y[:, t - 1:t]

    return pl.pallas_call(body, out_shape=SDS((hh, s), f32), name=name, compiler_params=_cparams())(z_t, b)


def fgate_bwd(dc_t, z_t, b, name, t=512):
    hh, s = z_t.shape

    def body(dc_ref, z_ref, b_ref, dz_ref, db_ref):
        low = _tri(t, False)
        carry = jnp.zeros((hh, 1), f32)
        total = jnp.zeros((hh, 1), f32)
        for blk in reversed(range(s // t)):
            cols = slice(blk * t, (blk + 1) * t)
            y = jnp.dot(dc_ref[:, cols], low, precision=lax.Precision.HIGHEST, preferred_element_type=f32) + carry
            carry = y[:, 0:1]
            dz = y * _sigmoid(-(z_ref[:, cols] + b_ref[...]))
            dz_ref[:, cols] = dz
            total = total + jnp.sum(dz, axis=-1, keepdims=True)
        db_ref[...] = jnp.broadcast_to(total, db_ref.shape)

    return pl.pallas_call(body, out_shape=[SDS((hh, s), f32), SDS((hh, LANES), f32)], name=name,
                          compiler_params=_cparams())(dc_t, z_t, b)


def _pair_masks():
    lane = lax.broadcasted_iota(jnp.int32, (1, LANES), 1)
    return [lane < HEAD_DIM, lane >= HEAD_DIM]


def fox_fwd(q, kv, c_col, c_row, name, tq=512):
    s = kv.shape[0]
    nq = s // tq

    def body(q_ref, k_ref, v_ref, cc_ref, cr_ref, o_ref, lse_ref):
        i = pl.program_id(1)
        qv = q_ref[...]
        masks = _pair_masks()
        row = lax.broadcasted_iota(jnp.int32, (tq, tq), 0)
        col = lax.broadcasted_iota(jnp.int32, (tq, tq), 1)
        qh = [jnp.where(masks[hh], qv, jnp.zeros((), bf16)) * ATT_SCALE for hh in range(2)]
        ct = [cc_ref[:, hh:hh + 1] for hh in range(2)]

        def block(j, carry, diag):
            lo = pl.multiple_of(j * tq, tq)
            ks = k_ref[pl.ds(lo, tq), :]
            vs = v_ref[pl.ds(lo, tq), :]
            out = []
            for hh in range(2):
                m, l, acc = carry[hh]
                sc = _dot_nt(qh[hh], ks) + (ct[hh] - cr_ref[hh:hh + 1, pl.ds(lo, tq)])
                if diag:
                    sc = jnp.where(col <= row, sc, -jnp.inf)
                m_new = jnp.maximum(m, jnp.max(sc, axis=-1, keepdims=True))
                alpha = jnp.exp(m - m_new)
                p = jnp.exp(sc - m_new)
                l = alpha * l + jnp.sum(p, axis=-1, keepdims=True)
                p_hi = p.astype(bf16)
                p_lo = (p - p_hi.astype(f32)).astype(bf16)
                acc = alpha * acc + (_dot(p_hi, vs) + _dot(p_lo, vs))
                out.append((m_new, l, acc))
            return tuple(out)

        init = (jnp.full((tq, 1), -jnp.inf, f32), jnp.zeros((tq, 1), f32), jnp.zeros((tq, LANES), f32))
        carry = lax.fori_loop(0, i, functools.partial(block, diag=False), (init, init))
        res = [(acc / l, m + jnp.log(l)) for m, l, acc in block(i, carry, True)]
        o_ref[...] = jnp.where(masks[0], res[0][0], res[1][0])
        lse_ref[...] = jnp.where(masks[0], res[0][1], res[1][1])

    return pl.pallas_call(
        body, grid=(FOX_PAIRS, nq),
        in_specs=[pl.BlockSpec((tq, LANES), lambda p, i: (i, p)), pl.BlockSpec((s, LANES), lambda p, i: (0, p)),
                  pl.BlockSpec((s, LANES), lambda p, i: (0, FOX_PAIRS + p)), pl.BlockSpec((None, tq, 2), lambda p, i: (p, i, 0)),
                  pl.BlockSpec((None, 2, s), lambda p, i: (p, 0, 0))],
        out_specs=[pl.BlockSpec((tq, LANES), lambda p, i: (i, p)), pl.BlockSpec((None, tq, LANES), lambda p, i: (p, i, 0))],
        out_shape=[SDS((s, MAIN_WIDTH), f32), SDS((FOX_PAIRS, s, LANES), f32)], name=name,
        compiler_params=_cparams("parallel", "parallel"))(q, kv, kv, c_col, c_row)


def fox_bwd(q, kv, d_mixed, o, lse, c_col, c_row, name, tq=512):
    s = kv.shape[0]
    nq = s // tq

    def body(q_ref, k_ref, v_ref, do_ref, o_ref, lse_ref, cc_ref, cr_ref, dq_ref, dk_ref, dv_ref, dc_ref):
        j = pl.program_id(1)

        @pl.when(j == 0)
        def _():
            dq_ref[...] = jnp.zeros_like(dq_ref)

        masks = _pair_masks()
        row = lax.broadcasted_iota(jnp.int32, (tq, tq), 0)
        col = lax.broadcasted_iota(jnp.int32, (tq, tq), 1)
        kj = k_ref[...]
        vj = v_ref[...]
        lo_j = pl.multiple_of(j * tq, tq)

        def block(i, carry, diag):
            dk, dv, dc0, dc1 = carry
            dcs = [dc0, dc1]
            lo = pl.multiple_of(i * tq, tq)
            qi = q_ref[pl.ds(lo, tq), :]
            doi = do_ref[pl.ds(lo, tq), :]
            prod = doi.astype(bf16).astype(f32) * o_ref[pl.ds(lo, tq), :]
            lse_i = lse_ref[pl.ds(lo, tq), :]
            cc_i = cc_ref[pl.ds(lo, tq), :]
            dq_i = jnp.zeros((tq, LANES), f32)
            for hh in range(2):
                qh = jnp.where(masks[hh], qi, jnp.zeros((), bf16))
                doh = jnp.where(masks[hh], doi, 0.0).astype(bf16)
                delta = jnp.sum(jnp.where(masks[hh], prod, 0.0), axis=-1, keepdims=True)
                sc = _dot_nt(qh, kj) * ATT_SCALE + (cc_i[:, hh:hh + 1] - cr_ref[hh:hh + 1, pl.ds(lo_j, tq)])
                p = jnp.exp(sc - lse_i[:, hh * HEAD_DIM:hh * HEAD_DIM + 1])
                if diag:
                    p = jnp.where(col <= row, p, 0.0)
                dv = dv + _dot_tn(p.astype(bf16), doh)
                ds = p * (_dot_nt(doh, vj) - delta)
                dcs[hh] = dcs[hh] + jnp.sum(ds, axis=0, keepdims=True)
                dsb = (ds * ATT_SCALE).astype(bf16)
                dq_i = jnp.where(masks[hh], _dot(dsb, kj), dq_i)
                dk = dk + _dot_tn(dsb, qh)
            dq_ref[pl.ds(lo, tq), :] += dq_i
            return dk, dv, dcs[0], dcs[1]

        zero = jnp.zeros((tq, LANES), f32)
        zrow = jnp.zeros((1, tq), f32)
        carry = block(j, (zero, zero, zrow, zrow), True)
        dk, dv, dc0, dc1 = lax.fori_loop(j + 1, nq, functools.partial(block, diag=False), carry)
        dk_ref[...] = dk.astype(bf16)
        dv_ref[...] = dv.astype(bf16)
        dc_ref[0:1, :] = -dc0
        dc_ref[1:2, :] = -dc1

    full = lambda p, j: (0, p)
    tile = lambda p, j: (j, p)
    return pl.pallas_call(
        body, grid=(FOX_PAIRS, nq),
        in_specs=[pl.BlockSpec((s, LANES), full), pl.BlockSpec((tq, LANES), tile), pl.BlockSpec((tq, LANES), lambda p, j: (j, FOX_PAIRS + p)),
                  pl.BlockSpec((s, LANES), full), pl.BlockSpec((s, LANES), full), pl.BlockSpec((None, s, LANES), lambda p, j: (p, 0, 0)),
                  pl.BlockSpec((None, s, 2), lambda p, j: (p, 0, 0)), pl.BlockSpec((None, 2, s), lambda p, j: (p, 0, 0))],
        out_specs=[pl.BlockSpec((s, LANES), full), pl.BlockSpec((tq, LANES), tile), pl.BlockSpec((tq, LANES), tile),
                   pl.BlockSpec((None, 2, tq), lambda p, j: (p, 0, j))],
        out_shape=[SDS((s, MAIN_WIDTH), f32), SDS((s, MAIN_WIDTH), bf16), SDS((s, MAIN_WIDTH), bf16), SDS((FOX_PAIRS, 2, s), f32)],
        name=name, compiler_params=_cparams("parallel", "arbitrary"))(q, kv, kv, d_mixed, o, lse, c_col, c_row)


def adamw(w, g, m, v, name, tr=256):
    r, c = w.shape
    tr = min(tr, r)
    assert r % tr == 0, (name, r, tr)
    c1 = 1.0 / (1.0 - ADAM_B1 ** ADAM_STEP)
    c2 = 1.0 / (1.0 - ADAM_B2 ** ADAM_STEP)

    def body(w_ref, g_ref, m_ref, v_ref, d_ref, mo_ref, vo_ref):
        gv = g_ref[...]
        mn = ADAM_B1 * m_ref[...] + (1.0 - ADAM_B1) * gv
        vn = ADAM_B2 * v_ref[...] + (1.0 - ADAM_B2) * gv * gv
        mo_ref[...] = mn
        vo_ref[...] = vn
        d_ref[...] = -ADAM_LR * ((mn * c1) / (jnp.sqrt(vn * c2) + ADAM_EPS) + ADAM_WD * w_ref[...])

    spec = pl.BlockSpec((tr, c), lambda i: (i, 0))
    return pl.pallas_call(body, grid=(r // tr,), in_specs=[spec] * 4, out_specs=[spec] * 3, out_shape=[SDS((r, c), f32)] * 3,
                          name=name, compiler_params=_cparams("parallel"))(w, g, m, v)


def sum_leading(x, name, out_dtype=f32, tr=None):
    n, r, c = x.shape
    tr = tr or r
    assert r % tr == 0

    def body(x_ref, o_ref):
        acc = x_ref[0].astype(f32)
        for k in range(1, n):
            acc = acc + x_ref[k].astype(f32)
        o_ref[...] = acc.astype(out_dtype)

    return pl.pallas_call(body, grid=(r // tr,), in_specs=[pl.BlockSpec((n, tr, c), lambda i: (0, i, 0))],
                          out_specs=pl.BlockSpec((tr, c), lambda i: (i, 0)), out_shape=SDS((r, c), out_dtype), name=name,
                          compiler_params=_cparams("parallel"))(x)


_ANY = pl.BlockSpec(memory_space=pl.ANY)
_DMA = pltpu.SemaphoreType.DMA


_HBM = pl.BlockSpec(memory_space=pltpu.HBM)
_SEM = pl.BlockSpec(memory_space=pltpu.SEMAPHORE)
_EFFECT = pltpu.SideEffectType.DATAFLOW_SIDE_EFFECTING
_FLIPS = [(0, 0, 1), (1, 0, 0), (0, 1, 0), (1, 1, 0), (1, 0, 1), (0, 1, 1), (1, 1, 1)]


def _me():
    return lax.axis_index("x"), lax.axis_index("y"), lax.axis_index("c")


def _peers():
    mx, my, mc = _me()
    return [(jnp.bitwise_xor(mx, fx), jnp.bitwise_xor(my, fy), jnp.bitwise_xor(mc, fc)) for fx, fy, fc in _FLIPS]


def _index(dev):
    return 4 * dev[0] + 2 * dev[1] + dev[2]


def _win(ref, axis, k, size, count=1):
    idx = [slice(None)] * len(ref.shape)
    idx[axis] = pl.ds(k * size, count * size)
    return ref.at[tuple(idx)]


def _hbm(a):
    return pltpu.with_memory_space_constraint(a, pltpu.HBM)


def _exchange_start(srcs, lands, copies_of, name):
    n = len(srcs)

    def body(*refs):
        src = refs[:n]
        send_sems, recv_sems, self_sems = refs[2 * n:2 * n + 3]
        land = refs[3 * n + 3:4 * n + 3]
        token = refs[4 * n + 3]
        me = _index(_me())
        for a in range(n):
            for s_ref, d_ref, peer in copies_of(a, src[a], land[a], me):
                if peer is None:
                    pltpu.make_async_copy(s_ref, d_ref, self_sems.at[a]).start()
                else:
                    pltpu.make_async_remote_copy(src_ref=s_ref, dst_ref=d_ref, send_sem=send_sems.at[a], recv_sem=recv_sems.at[a],
                                                 device_id=peer, device_id_type=MESH).start()
        token[...] = jnp.zeros_like(token)

    outs = pl.pallas_call(
        body, name=name,
        out_shape=(_DMA((n,)), _DMA((n,)), _DMA((n,)), *[pltpu.HBM(s.shape, s.dtype) for s in srcs],
                   *[pltpu.HBM(l.shape, l.dtype) for l in lands], SDS((8, LANES), f32)),
        in_specs=[_HBM] * (2 * n), out_specs=(_SEM, _SEM, _SEM, *[_HBM] * (2 * n), pl.BlockSpec(memory_space=pltpu.VMEM)),
        input_output_aliases={i: 3 + i for i in range(2 * n)},
        compiler_params=pltpu.CompilerParams(has_side_effects=_EFFECT),
    )(*[_hbm(s) for s in srcs], *[_hbm(lax.empty(l.shape, l.dtype)) for l in lands])
    return dict(sems=outs[:3], srcs=list(outs[3:3 + n]), lands=list(outs[3 + n:3 + 2 * n]), token=outs[3 + 2 * n])


def _exchange_wait(started, waits_of, after, name):
    srcs, lands = started["srcs"], started["lands"]
    n = len(srcs)

    def body(*refs):
        src = refs[:n]
        land = refs[n:2 * n]
        send_sems, recv_sems, self_sems = refs[2 * n:2 * n + 3]
        me = _index(_me())
        for a in range(n):
            seven, (s_ref, d_ref) = waits_of(a, src[a], land[a], me)
            both = pltpu.make_async_remote_copy(src_ref=seven, dst_ref=seven, send_sem=send_sems.at[a], recv_sem=recv_sems.at[a],
                                                device_id=_me(), device_id_type=MESH)
            both.wait_send()
            both.wait_recv()
            pltpu.make_async_copy(s_ref, d_ref, self_sems.at[a]).wait()

    outs = pl.pallas_call(
        body, name=name, out_shape=tuple(pltpu.HBM(t.shape, t.dtype) for t in srcs + lands),
        in_specs=[_HBM] * (2 * n) + [_SEM] * 3 + [_ANY], out_specs=tuple([_HBM] * (2 * n)),
        input_output_aliases={i: i for i in range(2 * n)},
        compiler_params=pltpu.CompilerParams(has_side_effects=_EFFECT),
    )(*srcs, *lands, *started["sems"], after)
    return list(outs[n:])


def gather_start(locs, axes, name):
    lands = [SDS(tuple(N_DEV * d if i == ax else d for i, d in enumerate(l.shape)), l.dtype) for l, ax in zip(locs, axes)]

    def copies_of(a, src, land, me):
        mine = _win(land, axes[a], me, src.shape[axes[a]])
        return [(src, mine, peer) for peer in _peers()] + [(src, mine, None)]

    return _exchange_start(locs, lands, copies_of, name)


def gather_wait(started, axes, after, name):
    def waits_of(a, src, land, me):
        size = src.shape[axes[a]]
        return _win(land, axes[a], 0, size, N_DEV - 1), (src, _win(land, axes[a], me, size))

    return _exchange_wait(started, waits_of, after, name)


def scatter_start(grads, axes, name):
    lands = [SDS((N_DEV,) + tuple(d // N_DEV if i == ax else d for i, d in enumerate(g.shape)), g.dtype) for g, ax in zip(grads, axes)]

    def copies_of(a, src, land, me):
        size = src.shape[axes[a]] // N_DEV
        out = [(_win(src, axes[a], _index(peer), size), land.at[me], peer) for peer in _peers()]
        return out + [(_win(src, axes[a], me, size), land.at[me], None)]

    return _exchange_start(grads, lands, copies_of, name)


def scatter_wait(started, axes, after, name):
    def waits_of(a, src, land, me):
        size = src.shape[axes[a]] // N_DEV
        return land.at[pl.ds(0, N_DEV - 1)], (_win(src, axes[a], me, size), land.at[me])

    return _exchange_wait(started, waits_of, after, name)


def _row_tile(rows, cap=512):
    return max(t for t in range(8, min(rows, cap) + 1, 8) if rows % t == 0)


_SMALL = [
    ("ln_mix_pre", (2, 1024)), ("ln_mix_post", (2, 1024)), ("ln_ffn_pre", (2, 1024)), ("ln_ffn_post", (2, 1024)),
    ("ln_mem", (2, 1024)), ("w_spatial", (1, 6, 128, 128)), ("b_spatial", (1, 6, 128)), ("ln_shared", (1024,)),
    ("b_forget", (12,)), ("ln_v_g", (1, 768)), ("ln_v_b", (1, 768)),
]
_SMALL_TILE = 8 * LANES


def _small_rows(shape):
    return -(-math.prod(shape) // _SMALL_TILE) * 8


def _pack_small(vals, shapes):
    parts = []
    for name, shape in shapes:
        flat = vals[name].reshape(-1).astype(f32)
        rows = _small_rows(shape)
        parts.append(jnp.pad(flat, (0, rows * LANES - flat.shape[0])).reshape(rows, LANES))
    return jnp.concatenate(parts, axis=0)


def _unpack_small(buf, shapes):
    out = {}
    lo = 0
    for name, shape in shapes:
        rows = _small_rows(shape)
        out[name] = buf[lo:lo + rows].reshape(-1)[:math.prod(shape)].reshape(shape)
        lo += rows
    return out


def kernel(x, mem, ln_mix_pre, ln_mix_post, ln_ffn_pre, ln_ffn_post, ln_mem, w_mem_kv, w_out, w_ffn_gate, w_ffn_up, w_ffn_down, w_in_a, w_spatial, b_spatial, ln_v_g, ln_v_b, ln_shared, w_shared_kv, b_forget, w_in_b, loss_target, m_ln_mix_pre, m_ln_mix_post, m_ln_ffn_pre, m_ln_ffn_post, m_ln_mem, m_w_mem_kv, m_w_out, m_w_ffn_gate, m_w_ffn_up, m_w_ffn_down, m_w_in_a, m_w_spatial, m_b_spatial, m_ln_v_g, m_ln_v_b, m_ln_shared, m_w_shared_kv, m_b_forget, m_w_in_b, v_ln_mix_pre, v_ln_mix_post, v_ln_ffn_pre, v_ln_ffn_post, v_ln_mem, v_w_mem_kv, v_w_out, v_w_ffn_gate, v_w_ffn_up, v_w_ffn_down, v_w_in_a, v_w_spatial, v_b_spatial, v_ln_v_g, v_ln_v_b, v_ln_shared, v_w_shared_kv, v_b_forget, v_w_in_b):
    weights = dict(ln_mix_pre=ln_mix_pre, ln_mix_post=ln_mix_post, ln_ffn_pre=ln_ffn_pre, ln_ffn_post=ln_ffn_post, ln_mem=ln_mem,
                   w_mem_kv=w_mem_kv, w_out=w_out, w_ffn_gate=w_ffn_gate, w_ffn_up=w_ffn_up, w_ffn_down=w_ffn_down, w_in_a=w_in_a,
                   w_spatial=w_spatial, b_spatial=b_spatial, ln_v_g=ln_v_g, ln_v_b=ln_v_b, ln_shared=ln_shared,
                   w_shared_kv=w_shared_kv, b_forget=b_forget, w_in_b=w_in_b)
    mom_m = dict(ln_mix_pre=m_ln_mix_pre, ln_mix_post=m_ln_mix_post, ln_ffn_pre=m_ln_ffn_pre, ln_ffn_post=m_ln_ffn_post, ln_mem=m_ln_mem,
                 w_mem_kv=m_w_mem_kv, w_out=m_w_out, w_ffn_gate=m_w_ffn_gate, w_ffn_up=m_w_ffn_up, w_ffn_down=m_w_ffn_down, w_in_a=m_w_in_a,
                 w_spatial=m_w_spatial, b_spatial=m_b_spatial, ln_v_g=m_ln_v_g, ln_v_b=m_ln_v_b, ln_shared=m_ln_shared,
                 w_shared_kv=m_w_shared_kv, b_forget=m_b_forget, w_in_b=m_w_in_b)
    mom_v = dict(ln_mix_pre=v_ln_mix_pre, ln_mix_post=v_ln_mix_post, ln_ffn_pre=v_ln_ffn_pre, ln_ffn_post=v_ln_ffn_post, ln_mem=v_ln_mem,
                 w_mem_kv=v_w_mem_kv, w_out=v_w_out, w_ffn_gate=v_w_ffn_gate, w_ffn_up=v_w_ffn_up, w_ffn_down=v_w_ffn_down, w_in_a=v_w_in_a,
                 w_spatial=v_w_spatial, b_spatial=v_b_spatial, ln_v_g=v_ln_v_g, ln_v_b=v_ln_v_b, ln_shared=v_ln_shared,
                 w_shared_kv=v_w_shared_kv, b_forget=v_b_forget, w_in_b=v_w_in_b)
    names = list(weights)
    mx, my, mc = lax.axis_index("x"), lax.axis_index("y"), lax.axis_index("c")
    me = 4 * mx + 2 * my + mc

    h0 = x[0]
    mem0 = mem[0]
    tgt = loss_target[0]
    seq = h0.shape[0]

    vec = lambda a: a.reshape(1, -1)
    pad_to = lambda a, axis, size: jnp.pad(a, [(0, size - a.shape[i] if i == axis else 0) for i in range(a.ndim)])

    def after(tok, a):
        return a + tok[0, 0].astype(a.dtype)

    lnv_loc = pad_to(jnp.concatenate([ln_v_g, ln_v_b], axis=0), 0, 8)
    st_a = gather_start([w_in_a.astype(bf16), pad_to(lnv_loc, 1, LANES)[None]], [0, 0], "gather_a_start")
    mix_locs = lambda l, tok: [after(tok, w_mem_kv[l]).astype(bf16), w_out[l].astype(bf16)]
    ffn_locs = lambda l, tok: [pad_to(after(tok, w_ffn_gate[l]).astype(bf16), 1, FF_SHARD_PAD),
                               pad_to(w_ffn_up[l].astype(bf16), 1, FF_SHARD_PAD), pad_to(w_ffn_down[l].astype(bf16), 0, FF_SHARD_PAD)]
    st_b = [gather_start(mix_locs(0, st_a["token"]), [0, 0], "gather_b0_start"), None]
    st_c = gather_start(ffn_locs(0, st_b[0]["token"]), [1, 1, 0], "gather_c_start")
    st_d = gather_start([after(st_c["token"], w_in_b[0]).astype(bf16), pad_to(w_shared_kv.astype(bf16), 1, KV_PAD)], [0, 0],
                        "gather_d_start")
    st_b[1] = gather_start(mix_locs(1, st_d["token"]), [0, 0], "gather_b1_start")
    st_e = gather_start(ffn_locs(1, st_b[1]["token"]), [1, 1, 0], "gather_e_start")
    ws = w_spatial[0].astype(bf16)
    ws_t = ws.transpose(0, 2, 1)
    bs_t = b_spatial[0].T

    (a0,) = rms_fwd(h0, [after(st_e["token"], vec(ln_mix_pre[0]))], "a0_norm")
    w_in_a8, lnv8 = gather_wait(st_a, [0, 0], a0, "gather_a_wait")
    w_in_a_full = w_in_a8.transpose(1, 0, 2).reshape(D_MODEL, -1)
    lnv_g = lnv8[:, 0, :MAIN_WIDTH // N_DEV].reshape(1, MAIN_WIDTH)
    lnv_b = lnv8[:, 1, :MAIN_WIDTH // N_DEV].reshape(1, MAIN_WIDTH)
    proj0 = mm(a0, w_in_a_full, "proj0", tn=896)
    main0 = gmlp_fwd(proj0, ws, bs_t, lnv_g, lnv_b, "gmlp_fwd")
    w_mkv, w_o = [None, None], [None, None]
    w_mkv[0], w_o[0] = gather_wait(st_b[0], [0, 0], main0, "gather_b0_wait")
    (memn0,) = rms_fwd(mem0, [vec(ln_mem[0])], "mem0_norm")
    kvm0 = mm(memn0, w_mkv[0], "kvm0")
    om0 = mem_attn_fwd(proj0, 2 * MAIN_WIDTH // MEM_WIDTH, kvm0, "mem_attn0")
    mixed0 = jnp.concatenate([main0, om0], axis=-1)
    y1_0 = mm(mixed0, w_o[0], "mix_out0")
    hmid0, f0 = resnorm(h0, y1_0, vec(ln_mix_post[0]), [vec(ln_ffn_pre[0])], "resnorm_mix0")
    w_g0, w_u0, w_d0 = gather_wait(st_c, [1, 1, 0], f0, "gather_c_wait")
    gu0, act0 = ffn_up(f0, w_g0, w_u0, "ffn_up0")
    y2_0 = mm(act0, w_d0, "ffn_down0")
    h1, a1, sin1 = resnorm(hmid0, y2_0, vec(ln_ffn_post[0]), [vec(ln_mix_pre[1]), vec(ln_shared)], "resnorm_ffn0")

    w_inb, w_kv = gather_wait(st_d, [0, 0], sin1, "gather_d_wait")
    kvb = mm(sin1, w_kv, "kv_shared", out_dtype=bf16, tn=MAIN_WIDTH, ncols=2 * MAIN_WIDTH)
    zf = mm(sin1, w_kv, "forget_logits", tn=256, col0=2 * MAIN_WIDTH, ncols=256)
    qb = mm(a1, w_inb, "proj1", out_dtype=bf16)
    z_t = jnp.pad(zf[:, :FOX_HEADS].T, ((0, 16 - FOX_HEADS), (0, 0)))
    bf_col = jnp.pad(b_forget, (0, 16 - FOX_HEADS)).reshape(16, 1)
    c_t = fgate_fwd(z_t, bf_col, "fgate_fwd")
    c_row = c_t[:FOX_HEADS].reshape(FOX_PAIRS, 2, seq)
    c_col = c_row.transpose(0, 2, 1)
    main1, lse = fox_fwd(qb, kvb, c_col, c_row, "fox_fwd")
    w_mkv[1], w_o[1] = gather_wait(st_b[1], [0, 0], main1, "gather_b1_wait")
    (memn1,) = rms_fwd(mem0, [vec(ln_mem[1])], "mem1_norm")
    kvm1 = mm(memn1, w_mkv[1], "kvm1")
    om1 = mem_attn_fwd(qb, MAIN_WIDTH // MEM_WIDTH, kvm1, "mem_attn1")
    mixed1 = jnp.concatenate([main1.astype(bf16), om1], axis=-1)
    y1_1 = mm(mixed1, w_o[1], "mix_out1")
    hmid1, f1 = resnorm(h1, y1_1, vec(ln_mix_post[1]), [vec(ln_ffn_pre[1])], "resnorm_mix1")
    w_g1, w_u1, w_d1 = gather_wait(st_e, [1, 1, 0], f1, "gather_e_wait")
    gu1, act1 = ffn_up(f1, w_g1, w_u1, "ffn_up1")
    y2_1 = mm(act1, w_d1, "ffn_down1")
    (h2,) = resnorm(hmid1, y2_1, vec(ln_ffn_post[1]), [], "resnorm_ffn1")
    dh, loss_tile = loss_grad(h2, tgt, "loss")
    loss = lax.psum(loss_tile[0, 0], AXES)
    ffn_w = [(w_g0, w_u0, w_d0), (w_g1, w_u1, w_d1)]

    small = {}

    def ffn_backward(layer, dh_out, y2, hmid, f, gu, act):
        w_g, w_u, w_d = ffn_w[layer]
        d_y2, dg_post = rms_bwd(y2, vec(ln_ffn_post[layer]), dh_out, None, bf16, f"ffn_post_bwd{layer}")
        d_act = mm(d_y2, w_d, f"d_act{layer}", trans_b=True, tn=FF_PAD // 2)
        dw_down = mm_tn(act, d_y2, f"dw_down{layer}", tk=256)
        d_gu = ffn_act_bwd(gu, d_act, f"ffn_act_bwd{layer}")
        dw_g = mm_tn(f, d_gu, f"dw_gate{layer}", ncols=FF_PAD)
        dw_u = mm_tn(f, d_gu, f"dw_up{layer}", col0=FF_PAD, ncols=FF_PAD)
        d_f = mm2_nt(d_gu, w_g, w_u, f"d_f{layer}")
        dh_mid, dg_pre = rms_bwd(hmid, vec(ln_ffn_pre[layer]), d_f, dh_out, f32, f"ffn_pre_bwd{layer}")
        return dh_mid, dg_post, dg_pre, [dw_g, dw_u, dw_down]

    def mix_out_backward(layer, dh_mid, y1, mixed, tok):
        d_y1, dg_post = rms_bwd(y1, after(tok, vec(ln_mix_post[layer])), dh_mid, None, bf16, f"mix_post_bwd{layer}")
        dw_out = mm_tn(mixed, d_y1, f"dw_out{layer}")
        d_mixed = mm(d_y1, w_o[layer], f"d_mixed{layer}", trans_b=True)
        return d_mixed, dg_post, dw_out

    def mem_backward(layer, q_src, q_block, kvm, memn, d_mixed):
        d_qm, d_kvm = mem_attn_bwd(q_src, q_block, kvm, d_mixed, f"mem_attn_bwd{layer}")
        d_kvm_b = d_kvm.astype(bf16)
        dw_mkv = mm_tn(memn, d_kvm_b, f"dw_mem_kv{layer}")
        d_memn = mm(d_kvm_b, w_mkv[layer], f"d_memn{layer}", trans_b=True)
        _, dg_mem = rms_bwd(mem0, vec(ln_mem[layer]), d_memn, None, bf16, f"mem_norm_bwd{layer}")
        return d_qm, dw_mkv, dg_mem

    ffn_axes = [1, 1, 0]

    dh_mid1, dg_fpost1, dg_fpre1, dw_ffn1 = ffn_backward(1, dh, y2_1, hmid1, f1, gu1, act1)
    rs_ffn1 = scatter_start(dw_ffn1, ffn_axes, "scatter_ffn1_start")
    d_mixed1, dg_mpost1, dw_out1 = mix_out_backward(1, dh_mid1, y1_1, mixed1, rs_ffn1["token"])
    d_qm1, dw_mkv1, dg_mem1 = mem_backward(1, qb, MAIN_WIDTH // MEM_WIDTH, kvm1, memn1, d_mixed1)
    rs_mix1 = scatter_start([dw_out1, dw_mkv1], [0, 0], "scatter_mix1_start")
    dq, dk, dv, dc = fox_bwd(qb, kvb, d_mixed1, main1, lse, after(rs_mix1["token"], c_col), c_row, "fox_bwd")
    dc_t = jnp.pad(dc.reshape(FOX_HEADS, seq), ((0, 16 - FOX_HEADS), (0, 0)))
    dz_t, db_f = fgate_bwd(dc_t, z_t, bf_col, "fgate_bwd")
    d_kvf = jnp.concatenate([dk, dv, jnp.pad(dz_t[:FOX_HEADS].T.astype(bf16), ((0, 0), (0, KV_PAD - KV_WIDTH)))], axis=-1)
    d_proj1 = jnp.concatenate([dq.astype(bf16), d_qm1], axis=-1)
    dw_in_b = mm_tn(a1, d_proj1, "dw_in_b")
    d_a1 = mm(d_proj1, w_inb, "d_a1", trans_b=True)
    dw_kv = mm_tn(sin1, d_kvf, "dw_kv", tn=896)
    d_sin = mm(d_kvf, w_kv, "d_sin", trans_b=True)
    rs_2 = scatter_start([dw_in_b, dw_kv], [0, 0], "scatter_shared_start")
    dh1_a, dg_pre1 = rms_bwd(h1, after(rs_2["token"], vec(ln_mix_pre[1])), d_a1, dh_mid1, f32, "mix_pre_bwd1")
    dh1, dg_shared = rms_bwd(h1, vec(ln_shared), d_sin, dh1_a, f32, "shared_norm_bwd")

    dh_mid0, dg_fpost0, dg_fpre0, dw_ffn0 = ffn_backward(0, dh1, y2_0, hmid0, f0, gu0, act0)
    rs_ffn0 = scatter_start(dw_ffn0, ffn_axes, "scatter_ffn0_start")
    d_mixed0, dg_mpost0, dw_out0 = mix_out_backward(0, dh_mid0, y1_0, mixed0, rs_ffn0["token"])
    d_qm0, dw_mkv0, dg_mem0 = mem_backward(0, proj0, 2 * MAIN_WIDTH // MEM_WIDTH, kvm0, memn0, d_mixed0)
    rs_mix0 = scatter_start([dw_out0, dw_mkv0], [0, 0], "scatter_mix0_start")
    d_uv, dw_s, db_s, dg_lnv, db_lnv = gmlp_bwd(proj0, d_mixed0, ws, ws_t, bs_t, after(rs_mix0["token"], lnv_g), lnv_b, "gmlp_bwd")

    small["ln_mix_pre"] = jnp.concatenate([jnp.zeros_like(dg_pre1), dg_pre1], axis=0)
    small["ln_mix_post"] = jnp.concatenate([dg_mpost0, dg_mpost1], axis=0)
    small["ln_ffn_pre"] = jnp.concatenate([dg_fpre0, dg_fpre1], axis=0)
    small["ln_ffn_post"] = jnp.concatenate([dg_fpost0, dg_fpost1], axis=0)
    small["ln_mem"] = jnp.concatenate([dg_mem0, dg_mem1], axis=0)
    small["w_spatial"] = dw_s[None]
    small["b_spatial"] = db_s[:, :A_GROUPS].T[None]
    small["ln_shared"] = dg_shared[0]
    small["b_forget"] = db_f[:FOX_HEADS, 0]
    small["ln_v_g"] = dg_lnv
    small["ln_v_b"] = db_lnv
    st_small = gather_start([_pack_small(small, _SMALL)[None]], [0], "gather_small_grads_start")
    d_proj0 = jnp.concatenate([d_uv, after(st_small["token"], d_qm0)], axis=-1)
    dw_in_a = mm_tn(a0, d_proj0, "dw_in_a", tn=896)
    rs_in_a = scatter_start([dw_in_a.reshape(D_MODEL, N_DEV, -1).transpose(1, 0, 2)], [0], "scatter_in_a_start")
    d_a0 = mm(d_proj0, w_in_a_full, "d_a0", trans_b=True, dep=rs_in_a["token"])
    grad_x, dg_pre0 = rms_bwd(h0, vec(ln_mix_pre[0]), d_a0, dh_mid0, f32, "mix_pre_bwd0")
    st_last = gather_start([dg_pre0.reshape(1, 8, LANES)], [0], "gather_last_grad_start")

    def owned(started, axes, wait_after, name):
        recv = scatter_wait(started, axes, wait_after, name)
        return [sum_leading(r.reshape((N_DEV, -1, r.shape[-1])), f"{name}_sum{i}", tr=_row_tile(math.prod(r.shape[1:-1])))
                for i, r in enumerate(recv)]

    g_ffn1 = owned(rs_ffn1, ffn_axes, after(st_last["token"], grad_x[:8, :LANES]), "scatter_ffn1_wait")
    g_mix1 = owned(rs_mix1, [0, 0], g_ffn1[0], "scatter_mix1_wait")
    g2 = owned(rs_2, [0, 0], g_mix1[0], "scatter_shared_wait")
    g_ffn0 = owned(rs_ffn0, ffn_axes, g2[0], "scatter_ffn0_wait")
    g_mix0 = owned(rs_mix0, [0, 0], g_ffn0[0], "scatter_mix0_wait")
    (g_in_a,) = owned(rs_in_a, [0], g_mix0[0], "scatter_in_a_wait")
    g_local = dict(
        w_ffn_gate=jnp.stack([g_ffn0[0], g_ffn1[0]])[:, :, :FF_SHARD], w_ffn_up=jnp.stack([g_ffn0[1], g_ffn1[1]])[:, :, :FF_SHARD],
        w_ffn_down=jnp.stack([g_ffn0[2], g_ffn1[2]])[:, :FF_SHARD], w_out=jnp.stack([g_mix0[0], g_mix1[0]]),
        w_mem_kv=jnp.stack([g_mix0[1], g_mix1[1]]), w_in_b=g2[0][None], w_shared_kv=g2[1][:, :KV_WIDTH], w_in_a=g_in_a[None])
    (small_all,) = gather_wait(st_small, [0], g_in_a, "gather_small_grads_wait")
    (last_all,) = gather_wait(st_last, [0], small_all, "gather_last_grad_wait")
    g_small = _unpack_small(sum_leading(small_all, "sum_small_grads"), _SMALL)
    g_small["ln_mix_pre"] = jnp.concatenate([sum_leading(last_all, "sum_last_grad").reshape(1, D_MODEL), g_small["ln_mix_pre"][1:]], axis=0)
    shard = MAIN_WIDTH // N_DEV
    for n in ("ln_v_g", "ln_v_b"):
        g_small[n] = lax.dynamic_slice_in_dim(g_small[n], me * shard, shard, axis=1)
    grad_w = {**g_small, **g_local}

    delta, new_m, new_v = {}, {}, {}
    for n in g_local:
        two_d = (-1, weights[n].shape[-1])
        d_, m_, v_ = adamw(weights[n].reshape(two_d), grad_w[n].reshape(two_d), mom_m[n].reshape(two_d), mom_v[n].reshape(two_d),
                           f"adamw_{n}", tr=_row_tile(math.prod(weights[n].shape[:-1])))
        delta[n], new_m[n], new_v[n] = (t.reshape(weights[n].shape) for t in (d_, m_, v_))
    small_local_shapes = [(n, tuple(weights[n].shape)) for n, _ in _SMALL]
    packed = [_pack_small(src, small_local_shapes) for src in (weights, grad_w, mom_m, mom_v)]
    outs = adamw(*packed, "adamw_small", tr=packed[0].shape[0])
    for dst, buf in zip((delta, new_m, new_v), outs):
        dst.update(_unpack_small(buf, small_local_shapes))

    return (loss, grad_x[None], *[grad_w[n] for n in names], *[delta[n] for n in names],
            *[new_m[n] for n in names], *[new_v[n] for n in names])
```

```python
import functools
import math

import jax
import jax.numpy as jnp
from jax import lax
from jax.experimental import pallas as pl
from jax.experimental.pallas import tpu as pltpu

f32 = jnp.float32
bf16 = jnp.bfloat16
SDS = jax.ShapeDtypeStruct

D_MODEL = 1024
MAIN_WIDTH = 768
MEM_WIDTH = 256
HEAD_DIM = 64
MEM_HEADS = 4
FOX_HEADS = 12
FOX_PAIRS = FOX_HEADS // 2
CHUNK = 128
A_GROUPS = 6
FF_SHARD = 352
FF_SHARD_PAD = 384
FF_PAD = 8 * FF_SHARD_PAD
KV_WIDTH = 2 * MAIN_WIDTH + FOX_HEADS
KV_PAD = 1792
RMS_EPS = 1e-6
LN_EPS = 1e-5
ATT_SCALE = HEAD_DIM ** -0.5
ADAM_LR, ADAM_B1, ADAM_B2, ADAM_EPS, ADAM_WD, ADAM_STEP = 0.001, 0.9, 0.999, 1e-08, 0.01, 10
N_DEV = 8
AXES = ("x", "y", "c")
MESH = pl.DeviceIdType.MESH
V7X_VMEM_LIMIT = 56 * 1024 * 1024
LANES = 128
FLAT_W = 512
ROW_PAD = 16


def _cparams(*sem):
    return pltpu.CompilerParams(dimension_semantics=sem or None, vmem_limit_bytes=V7X_VMEM_LIMIT)


def _dot(a, b):
    return jnp.dot(a, b, preferred_element_type=f32)


def _dot_nt(a, b):
    return lax.dot_general(a, b, (((1,), (1,)), ((), ())), preferred_element_type=f32)


def _dot_tn(a, b):
    return lax.dot_general(a, b, (((0,), (0,)), ((), ())), preferred_element_type=f32)


def _gelu(x):
    k = math.sqrt(2.0 / math.pi)
    t = jnp.tanh(k * (x + 0.044715 * x * x * x))
    return 0.5 * x * (1.0 + t), t


def _gelu_grad(x, t):
    k = math.sqrt(2.0 / math.pi)
    return 0.5 * (1.0 + t) + 0.5 * x * (1.0 - t * t) * k * (1.0 + 3.0 * 0.044715 * x * x)


def _sigmoid(x):
    return 1.0 / (1.0 + jnp.exp(-x))


def rms_fwd(x, gains, name, tm=512):
    m, d = x.shape
    tm = min(tm, m)
    n = len(gains)

    def body(x_ref, *refs):
        xv = x_ref[...]
        y = xv * lax.rsqrt(jnp.sum(xv * xv, axis=-1, keepdims=True) * (1.0 / d) + RMS_EPS)
        for g_ref, o_ref in zip(refs[:n], refs[n:]):
            o_ref[...] = (y * g_ref[...]).astype(bf16)

    row = pl.BlockSpec((tm, d), lambda i: (i, 0))
    vec = pl.BlockSpec((1, d), lambda i: (0, 0))
    return pl.pallas_call(body, grid=(m // tm,), in_specs=[row] + [vec] * n, out_specs=[row] * n,
                          out_shape=[SDS((m, d), bf16)] * n, name=name, compiler_params=_cparams("parallel"))(x, *gains)


def rms_bwd(x, g, dy, add, out_dtype, name, tm=512):
    m, d = x.shape
    tm = min(tm, m)
    has_add = add is not None

    def body(x_ref, g_ref, dy_ref, *refs):
        dx_ref, dg_ref = refs[-2], refs[-1]
        xv = x_ref[...]
        dyv = dy_ref[...].astype(f32)
        r = lax.rsqrt(jnp.sum(xv * xv, axis=-1, keepdims=True) * (1.0 / d) + RMS_EPS)
        xn = xv * r
        dyg = dyv * g_ref[...]
        dx = r * (dyg - xn * (jnp.sum(dyg * xn, axis=-1, keepdims=True) * (1.0 / d)))
        if has_add:
            dx = dx + refs[0][...]
        dx_ref[...] = dx.astype(out_dtype)

        @pl.when(pl.program_id(0) == 0)
        def _():
            dg_ref[...] = jnp.zeros_like(dg_ref)

        dg_ref[...] += jnp.sum(dyv * xn, axis=0, keepdims=True)

    row = pl.BlockSpec((tm, d), lambda i: (i, 0))
    vec = pl.BlockSpec((1, d), lambda i: (0, 0))
    ins = [x, g, dy] + ([add] if has_add else [])
    return pl.pallas_call(body, grid=(m // tm,), in_specs=[row, vec, row] + ([row] if has_add else []),
                          out_specs=[row, vec], out_shape=[SDS((m, d), out_dtype), SDS((1, d), f32)], name=name,
                          compiler_params=_cparams("arbitrary"))(*ins)


def loss_grad(h, tgt, name, tm=512):
    m, d = h.shape

    def body(h_ref, t_ref, dy_ref, l_ref):
        e = h_ref[...] - t_ref[...]
        dy_ref[...] = e * (1.0 / d)

        @pl.when(pl.program_id(0) == 0)
        def _():
            l_ref[...] = jnp.zeros_like(l_ref)

        part = jnp.sum(jnp.sum(e * e, axis=-1, keepdims=True), axis=0, keepdims=True) * (0.5 / d)
        l_ref[...] += jnp.broadcast_to(part, l_ref.shape)

    row = pl.BlockSpec((tm, d), lambda i: (i, 0))
    return pl.pallas_call(body, grid=(m // tm,), in_specs=[row, row],
                          out_specs=[row, pl.BlockSpec((8, LANES), lambda i: (0, 0))],
                          out_shape=[SDS((m, d), f32), SDS((8, LANES), f32)], name=name,
                          compiler_params=_cparams("arbitrary"))(h, tgt)


def mm(a, b, name, trans_b=False, out_dtype=f32, tm=512, tn=1024, layer=None, col0=0, ncols=None, dep=None):
    m, k = a.shape
    n_all = b.shape[-2] if trans_b else b.shape[-1]
    n = n_all if ncols is None else ncols
    tm, tn = min(tm, m), min(tn, n)
    assert m % tm == 0 and n % tn == 0 and col0 % tn == 0 and not (trans_b and col0), (name, m, n, tm, tn)
    jb = col0 // tn
    lead = () if layer is None else (None,)
    sel = () if layer is None else (layer,)

    def body(a_ref, b_ref, *rest):
        r = _dot_nt(a_ref[...], b_ref[...]) if trans_b else _dot(a_ref[...], b_ref[...])
        rest[-1][...] = r.astype(out_dtype)

    if trans_b:
        b_spec = pl.BlockSpec(lead + (tn, k), lambda j, i: sel + (j, 0))
    else:
        b_spec = pl.BlockSpec(lead + (k, tn), lambda j, i: sel + (0, jb + j))
    deps = [] if dep is None else [dep]
    dep_specs = [pl.BlockSpec((8, LANES), lambda j, i: (0, 0))] * len(deps)
    return pl.pallas_call(body, grid=(n // tn, m // tm), in_specs=[pl.BlockSpec((tm, k), lambda j, i: (i, 0)), b_spec] + dep_specs,
                          out_specs=pl.BlockSpec((tm, tn), lambda j, i: (i, j)), out_shape=SDS((m, n), out_dtype),
                          name=name, compiler_params=_cparams("parallel", "parallel"))(a, b, *deps)


def mm_tn(a, g, name, tk=512, tn=1024, out_dtype=bf16, col0=0, ncols=None):
    s, k = a.shape
    n = g.shape[1] if ncols is None else ncols
    tk, tn = min(tk, k), min(tn, n)
    assert k % tk == 0 and n % tn == 0 and col0 % tn == 0, (name, k, n, tk, tn)
    jb = col0 // tn

    def body(a_ref, g_ref, o_ref):
        o_ref[...] = _dot_tn(a_ref[...], g_ref[...]).astype(out_dtype)

    return pl.pallas_call(body, grid=(k // tk, n // tn),
                          in_specs=[pl.BlockSpec((s, tk), lambda i, j: (0, i)), pl.BlockSpec((s, tn), lambda i, j: (0, jb + j))],
                          out_specs=pl.BlockSpec((tk, tn), lambda i, j: (i, j)), out_shape=SDS((k, n), out_dtype), name=name,
                          compiler_params=_cparams("parallel", "parallel"))(a, g)


def _rms(xv):
    return xv * lax.rsqrt(jnp.sum(xv * xv, axis=-1, keepdims=True) * (1.0 / xv.shape[-1]) + RMS_EPS)


def _rms_bwd_math(xv, g, dy):
    d = xv.shape[-1]
    r = lax.rsqrt(jnp.sum(xv * xv, axis=-1, keepdims=True) * (1.0 / d) + RMS_EPS)
    xn = xv * r
    dyg = dy * g
    dx = r * (dyg - xn * (jnp.sum(dyg * xn, axis=-1, keepdims=True) * (1.0 / d)))
    return dx, jnp.sum(dy * xn, axis=0, keepdims=True)


def mm_resnorm(a, b, h, g_post, gains, name, tm=256):
    m, k = a.shape
    d = b.shape[1]
    n = len(gains)

    def body(a_ref, b_ref, h_ref, gp_ref, *refs):
        y = _dot(a_ref[...], b_ref[...])
        refs[n][...] = y
        hn = h_ref[...] + _rms(y) * gp_ref[...]
        refs[n + 1][...] = hn
        if n:
            z = _rms(hn)
            for g_ref, o_ref in zip(refs[:n], refs[n + 2:]):
                o_ref[...] = (z * g_ref[...]).astype(bf16)

    row = pl.BlockSpec((tm, d), lambda i: (i, 0))
    vec = pl.BlockSpec((1, d), lambda i: (0, 0))
    return pl.pallas_call(body, grid=(m // tm,),
                          in_specs=[pl.BlockSpec((tm, k), lambda i: (i, 0)), pl.BlockSpec((k, d), lambda i: (0, 0)), row, vec] + [vec] * n,
                          out_specs=[row] * (n + 2), out_shape=[SDS((m, d), f32)] * 2 + [SDS((m, d), bf16)] * n, name=name,
                          compiler_params=_cparams("parallel"))(a, b, h, g_post, *gains)


def ffn_act_grad(d_y2, w_d, gu, name, tm=512, tn=1536):
    s, d = d_y2.shape
    ff = w_d.shape[0]
    nb = ff // tn

    def body(a_ref, b_ref, g_ref, u_ref, dg_ref, du_ref):
        da = _dot_nt(a_ref[...], b_ref[...])
        gg = g_ref[...].astype(f32)
        sg = _sigmoid(gg)
        dg_ref[...] = (da * u_ref[...].astype(f32) * (sg * (1.0 + gg * (1.0 - sg)))).astype(bf16)
        du_ref[...] = (da * gg * sg).astype(bf16)

    tile = pl.BlockSpec((tm, tn), lambda j, i: (i, j))
    return pl.pallas_call(body, grid=(nb, s // tm),
                          in_specs=[pl.BlockSpec((tm, d), lambda j, i: (i, 0)), pl.BlockSpec((tn, d), lambda j, i: (j, 0)), tile,
                                    pl.BlockSpec((tm, tn), lambda j, i: (i, nb + j))],
                          out_specs=[tile, tile], out_shape=[SDS((s, ff), bf16)] * 2, name=name,
                          compiler_params=_cparams("parallel", "parallel"))(d_y2, w_d, gu, gu)


def ffn_in_grad(d_g, d_u, w_g, w_u, hmid, dh_out, g_pre, y1, g_post, name, tm=256):
    s, ff = d_g.shape
    d = w_g.shape[0]

    def body(dg_ref, du_ref, wg_ref, wu_ref, hm_ref, dho_ref, gpre_ref, y1_ref, gpost_ref, dhm_ref, dy1_ref, dgpre_ref, dgpost_ref):
        @pl.when(pl.program_id(0) == 0)
        def _():
            dgpre_ref[...] = jnp.zeros_like(dgpre_ref)
            dgpost_ref[...] = jnp.zeros_like(dgpost_ref)

        d_f = _dot_nt(dg_ref[...], wg_ref[...]) + _dot_nt(du_ref[...], wu_ref[...])
        dx, dg1 = _rms_bwd_math(hm_ref[...], gpre_ref[...], d_f)
        dh_mid = dho_ref[...] + dx
        dhm_ref[...] = dh_mid
        dgpre_ref[...] += dg1
        dy1, dg2 = _rms_bwd_math(y1_ref[...], gpost_ref[...], dh_mid)
        dy1_ref[...] = dy1.astype(bf16)
        dgpost_ref[...] += dg2

    row = pl.BlockSpec((tm, d), lambda i: (i, 0))
    vec = pl.BlockSpec((1, d), lambda i: (0, 0))
    wide = pl.BlockSpec((tm, ff), lambda i: (i, 0))
    w_spec = pl.BlockSpec((d, ff), lambda i: (0, 0))
    return pl.pallas_call(body, grid=(s // tm,), in_specs=[wide, wide, w_spec, w_spec, row, row, vec, row, vec],
                          out_specs=[row, row, vec, vec], out_shape=[SDS((s, d), f32), SDS((s, d), bf16), SDS((1, d), f32), SDS((1, d), f32)],
                          name=name, compiler_params=_cparams("arbitrary"))(d_g, d_u, w_g, w_u, hmid, dh_out, g_pre, y1, g_post)


def proj_in_grad(pairs, x, add, name, tm=256, dep=None):
    s, d = x.shape
    n = len(pairs)
    deps = [] if dep is None else [dep]

    def body(*refs):
        x_ref, add_ref = refs[3 * n], refs[3 * n + 1]
        outs = refs[3 * n + 2 + len(deps):]

        @pl.when(pl.program_id(0) == 0)
        def _():
            for o in outs[1:]:
                o[...] = jnp.zeros_like(o)

        xv = x_ref[...]
        dx = add_ref[...]
        for i in range(n):
            a_ref, b_ref, g_ref = refs[3 * i:3 * i + 3]
            dxi, dgi = _rms_bwd_math(xv, g_ref[...], _dot_nt(a_ref[...], b_ref[...]))
            dx = dx + dxi
            outs[1 + i][...] += dgi
        outs[0][...] = dx

    row = pl.BlockSpec((tm, d), lambda i: (i, 0))
    vec = pl.BlockSpec((1, d), lambda i: (0, 0))
    in_specs, args = [], []
    for a, b, g in pairs:
        k = a.shape[1]
        in_specs += [pl.BlockSpec((tm, k), lambda i: (i, 0)), pl.BlockSpec((d, k), lambda i: (0, 0)), vec]
        args += [a, b, g]
    in_specs += [row, row] + [pl.BlockSpec((8, LANES), lambda i: (0, 0))] * len(deps)
    out = pl.pallas_call(body, grid=(s // tm,), in_specs=in_specs, out_specs=[row] + [vec] * n,
                         out_shape=[SDS((s, d), f32)] + [SDS((1, d), f32)] * n, name=name,
                         compiler_params=_cparams("arbitrary"))(*args, x, add, *deps)
    return out[0], out[1:]


def ffn_up(f, wg, wu, name, tm=256, tc=256):
    s, d = f.shape
    ff = wg.shape[-1]

    def body(f_ref, wg_ref, wu_ref, gu_ref, act_ref):
        fv = f_ref[...]
        for j in range(ff // tc):
            lo = j * tc
            gg = _dot(fv, wg_ref[:, lo:lo + tc])
            uu = _dot(fv, wu_ref[:, lo:lo + tc])
            gu_ref[:, lo:lo + tc] = gg.astype(bf16)
            gu_ref[:, ff + lo:ff + lo + tc] = uu.astype(bf16)
            act_ref[:, lo:lo + tc] = (gg * _sigmoid(gg) * uu).astype(bf16)

    w_spec = pl.BlockSpec((d, ff), lambda i: (0, 0))
    return pl.pallas_call(body, grid=(s // tm,), in_specs=[pl.BlockSpec((tm, d), lambda i: (i, 0)), w_spec, w_spec],
                          out_specs=[pl.BlockSpec((tm, 2 * ff), lambda i: (i, 0)), pl.BlockSpec((tm, ff), lambda i: (i, 0))],
                          out_shape=[SDS((s, 2 * ff), bf16), SDS((s, ff), bf16)], name=name,
                          compiler_params=_cparams("parallel"))(f, wg, wu)


def _gmlp_forward_chunk(u, v, w_refs, bias, ln_g, ln_b):
    gu, tu = _gelu(u)
    gv, tv = _gelu(v)
    mu = jnp.sum(gv, axis=-1, keepdims=True) * (1.0 / MAIN_WIDTH)
    xc = gv - mu
    rstd = lax.rsqrt(jnp.sum(xc * xc, axis=-1, keepdims=True) * (1.0 / MAIN_WIDTH) + LN_EPS)
    xhat = xc * rstd
    vln = xhat * ln_g + ln_b
    row = lax.broadcasted_iota(jnp.int32, (CHUNK, CHUNK), 0)
    col = lax.broadcasted_iota(jnp.int32, (CHUNK, CHUNK), 1)
    s_parts = []
    for g in range(A_GROUPS):
        w = jnp.where(col <= row, w_refs[g], jnp.zeros((), bf16))
        s_parts.append(_dot(w, vln[:, g * CHUNK:(g + 1) * CHUNK].astype(bf16)) + bias[:, g:g + 1])
    return gu, tu, tv, rstd, xhat, vln, s_parts


def gmlp_fwd(proj, ws, bs_t, ln_g, ln_b, name, tm=512):
    s = proj.shape[0]

    def body(u_ref, v_ref, w_ref, b_ref, g_ref, bb_ref, o_ref):
        bias = b_ref[...]
        for c in range(tm // CHUNK):
            rows = slice(c * CHUNK, (c + 1) * CHUNK)
            gu, _, _, _, _, _, s_parts = _gmlp_forward_chunk(u_ref[rows, :], v_ref[rows, :], w_ref, bias, g_ref[...], bb_ref[...])
            for g in range(A_GROUPS):
                cols = slice(g * CHUNK, (g + 1) * CHUNK)
                o_ref[rows, cols] = (gu[:, cols] * s_parts[g]).astype(bf16)

    vec = pl.BlockSpec((1, MAIN_WIDTH), lambda i: (0, 0))
    return pl.pallas_call(
        body, grid=(s // tm,),
        in_specs=[pl.BlockSpec((tm, MAIN_WIDTH), lambda i: (i, 0)), pl.BlockSpec((tm, MAIN_WIDTH), lambda i: (i, 1)),
                  pl.BlockSpec((A_GROUPS, CHUNK, CHUNK), lambda i: (0, 0, 0)), pl.BlockSpec((CHUNK, A_GROUPS), lambda i: (0, 0)), vec, vec],
        out_specs=pl.BlockSpec((tm, MAIN_WIDTH), lambda i: (i, 0)), out_shape=SDS((s, MAIN_WIDTH), bf16), name=name,
        compiler_params=_cparams("parallel"))(proj, proj, ws, bs_t, ln_g, ln_b)


def gmlp_bwd(proj, d_mixed, ws, ws_t, bs_t, ln_g, ln_b, name, tm=512):
    s = proj.shape[0]

    def body(u_ref, v_ref, dm_ref, w_ref, wt_ref, b_ref, g_ref, bb_ref, duv_ref, dw_ref, db_ref, dg_ref, dbb_ref):
        @pl.when(pl.program_id(0) == 0)
        def _():
            dw_ref[...] = jnp.zeros_like(dw_ref)
            db_ref[...] = jnp.zeros_like(db_ref)
            dg_ref[...] = jnp.zeros_like(dg_ref)
            dbb_ref[...] = jnp.zeros_like(dbb_ref)

        bias = b_ref[...]
        ln_gv = g_ref[...]
        row = lax.broadcasted_iota(jnp.int32, (CHUNK, CHUNK), 0)
        col = lax.broadcasted_iota(jnp.int32, (CHUNK, CHUNK), 1)
        lane = lax.broadcasted_iota(jnp.int32, (CHUNK, LANES), 1)
        for c in range(tm // CHUNK):
            rows = slice(c * CHUNK, (c + 1) * CHUNK)
            u = u_ref[rows, :]
            v = v_ref[rows, :]
            gu, tu, tv, rstd, xhat, vln, s_parts = _gmlp_forward_chunk(u, v, w_ref, bias, ln_gv, bb_ref[...])
            dm = dm_ref[rows, :]
            d_vln_parts = []
            d_gu_parts = []
            db_acc = jnp.zeros((CHUNK, LANES), f32)
            for g in range(A_GROUPS):
                cols = slice(g * CHUNK, (g + 1) * CHUNK)
                dmg = dm[:, cols]
                d_gu_parts.append(dmg * s_parts[g])
                d_s = dmg * gu[:, cols]
                db_acc = db_acc + jnp.where(lane == g, jnp.sum(d_s, axis=-1, keepdims=True), 0.0)
                d_sb = d_s.astype(bf16)
                dw_ref[g] += jnp.where(col <= row, _dot_nt(d_sb, vln[:, cols].astype(bf16)), 0.0)
                wt = jnp.where(row <= col, wt_ref[g], jnp.zeros((), bf16))
                d_vln_parts.append(_dot(wt, d_sb))
            db_ref[...] += db_acc
            d_vln = jnp.concatenate(d_vln_parts, axis=-1)
            d_gu = jnp.concatenate(d_gu_parts, axis=-1)
            dg_ref[...] += jnp.sum(d_vln * xhat, axis=0, keepdims=True)
            dbb_ref[...] += jnp.sum(d_vln, axis=0, keepdims=True)
            dxh = d_vln * ln_gv
            m1 = jnp.sum(dxh, axis=-1, keepdims=True) * (1.0 / MAIN_WIDTH)
            m2 = jnp.sum(dxh * xhat, axis=-1, keepdims=True) * (1.0 / MAIN_WIDTH)
            d_gv = rstd * (dxh - m1 - xhat * m2)
            duv_ref[rows, :MAIN_WIDTH] = (d_gu * _gelu_grad(u, tu)).astype(bf16)
            duv_ref[rows, MAIN_WIDTH:] = (d_gv * _gelu_grad(v, tv)).astype(bf16)

    vec = pl.BlockSpec((1, MAIN_WIDTH), lambda i: (0, 0))
    wspec = pl.BlockSpec((A_GROUPS, CHUNK, CHUNK), lambda i: (0, 0, 0))
    return pl.pallas_call(
        body, grid=(s // tm,),
        in_specs=[pl.BlockSpec((tm, MAIN_WIDTH), lambda i: (i, 0)), pl.BlockSpec((tm, MAIN_WIDTH), lambda i: (i, 1)),
                  pl.BlockSpec((tm, MAIN_WIDTH), lambda i: (i, 0)), wspec, wspec, pl.BlockSpec((CHUNK, A_GROUPS), lambda i: (0, 0)), vec, vec],
        out_specs=[pl.BlockSpec((tm, 2 * MAIN_WIDTH), lambda i: (i, 0)), wspec, pl.BlockSpec((CHUNK, LANES), lambda i: (0, 0)), vec, vec],
        out_shape=[SDS((s, 2 * MAIN_WIDTH), bf16), SDS((A_GROUPS, CHUNK, CHUNK), f32), SDS((CHUNK, LANES), f32),
                   SDS((1, MAIN_WIDTH), f32), SDS((1, MAIN_WIDTH), f32)],
        name=name, compiler_params=_cparams("arbitrary"))(proj, proj, d_mixed, ws, ws_t, bs_t, ln_g, ln_b)


def _head_mask(width, h):
    lane = lax.broadcasted_iota(jnp.int32, (1, width), 1)
    return (lane >= h * HEAD_DIM) & (lane < (h + 1) * HEAD_DIM)


def mem_attn_fwd(proj, q_block, kv, name, tm=512):
    s = proj.shape[0]
    n_mem = kv.shape[0]

    def body(q_ref, kv_ref, o_ref):
        q = q_ref[...].astype(f32)
        k = kv_ref[:, :MEM_WIDTH].astype(bf16)
        v = kv_ref[:, MEM_WIDTH:].astype(bf16)
        out = jnp.zeros((tm, MEM_WIDTH), f32)
        for h in range(MEM_HEADS):
            msk = _head_mask(MEM_WIDTH, h)
            qh = jnp.where(msk, q, 0.0).astype(bf16)
            sc = _dot_nt(qh, k) * ATT_SCALE
            e = jnp.exp(sc - jnp.max(sc, axis=-1, keepdims=True))
            p = e / jnp.sum(e, axis=-1, keepdims=True)
            out = jnp.where(msk, _dot(p.astype(bf16), v), out)
        o_ref[...] = out.astype(bf16)

    return pl.pallas_call(body, grid=(s // tm,),
                          in_specs=[pl.BlockSpec((tm, MEM_WIDTH), lambda i: (i, q_block)), pl.BlockSpec((n_mem, 2 * MEM_WIDTH), lambda i: (0, 0))],
                          out_specs=pl.BlockSpec((tm, MEM_WIDTH), lambda i: (i, 0)), out_shape=SDS((s, MEM_WIDTH), bf16), name=name,
                          compiler_params=_cparams("parallel"))(proj, kv)


def mem_attn_bwd(proj, q_block, kv, d_mixed, name, tm=512):
    s = proj.shape[0]
    n_mem = kv.shape[0]

    def body(q_ref, kv_ref, do_ref, dq_ref, dkv_ref):
        @pl.when(pl.program_id(0) == 0)
        def _():
            dkv_ref[...] = jnp.zeros_like(dkv_ref)

        q = q_ref[...].astype(f32)
        do = do_ref[...]
        k = kv_ref[:, :MEM_WIDTH].astype(bf16)
        v = kv_ref[:, MEM_WIDTH:].astype(bf16)
        dq = jnp.zeros((tm, MEM_WIDTH), f32)
        dk = jnp.zeros((n_mem, MEM_WIDTH), f32)
        dv = jnp.zeros((n_mem, MEM_WIDTH), f32)
        for h in range(MEM_HEADS):
            msk = _head_mask(MEM_WIDTH, h)
            qh = jnp.where(msk, q, 0.0).astype(bf16)
            doh = jnp.where(msk, do, 0.0).astype(bf16)
            sc = _dot_nt(qh, k) * ATT_SCALE
            e = jnp.exp(sc - jnp.max(sc, axis=-1, keepdims=True))
            p = e / jnp.sum(e, axis=-1, keepdims=True)
            dp = _dot_nt(doh, v)
            ds = p * (dp - jnp.sum(dp * p, axis=-1, keepdims=True))
            dsb = (ds * ATT_SCALE).astype(bf16)
            dq = jnp.where(msk, _dot(dsb, k), dq)
            dk = dk + _dot_tn(dsb, qh)
            dv = dv + _dot_tn(p.astype(bf16), doh)
        dq_ref[...] = dq.astype(bf16)
        dkv_ref[:, :MEM_WIDTH] += dk
        dkv_ref[:, MEM_WIDTH:] += dv

    return pl.pallas_call(
        body, grid=(s // tm,),
        in_specs=[pl.BlockSpec((tm, MEM_WIDTH), lambda i: (i, q_block)), pl.BlockSpec((n_mem, 2 * MEM_WIDTH), lambda i: (0, 0)),
                  pl.BlockSpec((tm, MEM_WIDTH), lambda i: (i, MAIN_WIDTH // MEM_WIDTH))],
        out_specs=[pl.BlockSpec((tm, MEM_WIDTH), lambda i: (i, 0)), pl.BlockSpec((n_mem, 2 * MEM_WIDTH), lambda i: (0, 0))],
        out_shape=[SDS((s, MEM_WIDTH), bf16), SDS((n_mem, 2 * MEM_WIDTH), f32)], name=name,
        compiler_params=_cparams("arbitrary"))(proj, kv, d_mixed)


def _tri(t, upper):
    r = lax.broadcasted_iota(jnp.int32, (t, t), 0)
    c = lax.broadcasted_iota(jnp.int32, (t, t), 1)
    return ((r <= c) if upper else (r >= c)).astype(f32)


def fgate_fwd(z_t, b, name, t=512):
    hh, s = z_t.shape

    def body(z_ref, b_ref, c_ref):
        u = _tri(t, True)
        carry = jnp.zeros((hh, 1), f32)
        for blk in range(s // t):
            x = z_ref[:, blk * t:(blk + 1) * t] + b_ref[...]
            logf = jnp.minimum(x, 0.0) - jnp.log(1.0 + jnp.exp(-jnp.abs(x)))
            y = jnp.dot(logf, u, precision=lax.Precision.HIGHEST, preferred_element_type=f32) + carry
            c_ref[:, blk * t:(blk + 1) * t] = y
            carry = y[:, t - 1:t]

    return pl.pallas_call(body, out_shape=SDS((hh, s), f32), name=name, compiler_params=_cparams())(z_t, b)


def fgate_bwd(dc_t, z_t, b, name, t=512):
    hh, s = z_t.shape

    def body(dc_ref, z_ref, b_ref, dz_ref, db_ref):
        low = _tri(t, False)
        carry = jnp.zeros((hh, 1), f32)
        total = jnp.zeros((hh, 1), f32)
        for blk in reversed(range(s // t)):
            cols = slice(blk * t, (blk + 1) * t)
            y = jnp.dot(dc_ref[:, cols], low, precision=lax.Precision.HIGHEST, preferred_element_type=f32) + carry
            carry = y[:, 0:1]
            dz = y * _sigmoid(-(z_ref[:, cols] + b_ref[...]))
            dz_ref[:, cols] = dz
            total = total + jnp.sum(dz, axis=-1, keepdims=True)
        db_ref[...] = jnp.broadcast_to(total, db_ref.shape)

    return pl.pallas_call(body, out_shape=[SDS((hh, s), f32), SDS((hh, LANES), f32)], name=name,
                          compiler_params=_cparams())(dc_t, z_t, b)


def _pair_masks():
    lane = lax.broadcasted_iota(jnp.int32, (1, LANES), 1)
    return [lane < HEAD_DIM, lane >= HEAD_DIM]


def fox_fwd(q, kv, c_col, c_row, name, tq=512):
    s = kv.shape[0]
    nq = s // tq

    def body(q_ref, k_ref, v_ref, cc_ref, cr_ref, o_ref, lse_ref):
        i = pl.program_id(1)
        qv = q_ref[...]
        masks = _pair_masks()
        row = lax.broadcasted_iota(jnp.int32, (tq, tq), 0)
        col = lax.broadcasted_iota(jnp.int32, (tq, tq), 1)
        qh = [jnp.where(masks[hh], qv, jnp.zeros((), bf16)) * ATT_SCALE for hh in range(2)]
        ct = [cc_ref[:, hh:hh + 1] for hh in range(2)]

        def block(j, carry, diag):
            lo = pl.multiple_of(j * tq, tq)
            ks = k_ref[pl.ds(lo, tq), :]
            vs = v_ref[pl.ds(lo, tq), :]
            out = []
            for hh in range(2):
                m, l, acc = carry[hh]
                sc = _dot_nt(qh[hh], ks) + (ct[hh] - cr_ref[hh:hh + 1, pl.ds(lo, tq)])
                if diag:
                    sc = jnp.where(col <= row, sc, -jnp.inf)
                m_new = jnp.maximum(m, jnp.max(sc, axis=-1, keepdims=True))
                alpha = jnp.exp(m - m_new)
                p = jnp.exp(sc - m_new)
                l = alpha * l + jnp.sum(p, axis=-1, keepdims=True)
                p_hi = p.astype(bf16)
                p_lo = (p - p_hi.astype(f32)).astype(bf16)
                acc = alpha * acc + (_dot(p_hi, vs) + _dot(p_lo, vs))
                out.append((m_new, l, acc))
            return tuple(out)

        init = (jnp.full((tq, 1), -jnp.inf, f32), jnp.zeros((tq, 1), f32), jnp.zeros((tq, LANES), f32))
        carry = lax.fori_loop(0, i, functools.partial(block, diag=False), (init, init))
        res = [(acc / l, m + jnp.log(l)) for m, l, acc in block(i, carry, True)]
        o_ref[...] = jnp.where(masks[0], res[0][0], res[1][0])
        lse_ref[...] = jnp.where(masks[0], res[0][1], res[1][1])

    return pl.pallas_call(
        body, grid=(FOX_PAIRS, nq),
        in_specs=[pl.BlockSpec((tq, LANES), lambda p, i: (i, p)), pl.BlockSpec((s, LANES), lambda p, i: (0, p)),
                  pl.BlockSpec((s, LANES), lambda p, i: (0, FOX_PAIRS + p)), pl.BlockSpec((None, tq, 2), lambda p, i: (p, i, 0)),
                  pl.BlockSpec((None, 2, s), lambda p, i: (p, 0, 0))],
        out_specs=[pl.BlockSpec((tq, LANES), lambda p, i: (i, p)), pl.BlockSpec((None, tq, LANES), lambda p, i: (p, i, 0))],
        out_shape=[SDS((s, MAIN_WIDTH), f32), SDS((FOX_PAIRS, s, LANES), f32)], name=name,
        compiler_params=_cparams("parallel", "parallel"))(q, kv, kv, c_col, c_row)


def fox_bwd(q, kv, d_mixed, o, lse, c_col, c_row, name, tq=512):
    s = kv.shape[0]
    nq = s // tq

    def body(q_ref, k_ref, v_ref, do_ref, o_ref, lse_ref, cc_ref, cr_ref, dq_ref, dk_ref, dv_ref, dc_ref):
        j = pl.program_id(1)

        @pl.when(j == 0)
        def _():
            dq_ref[...] = jnp.zeros_like(dq_ref)

        masks = _pair_masks()
        row = lax.broadcasted_iota(jnp.int32, (tq, tq), 0)
        col = lax.broadcasted_iota(jnp.int32, (tq, tq), 1)
        kj = k_ref[...]
        vj = v_ref[...]
        lo_j = pl.multiple_of(j * tq, tq)

        def block(i, carry, diag):
            dk, dv, dc0, dc1 = carry
            dcs = [dc0, dc1]
            lo = pl.multiple_of(i * tq, tq)
            qi = q_ref[pl.ds(lo, tq), :]
            doi = do_ref[pl.ds(lo, tq), :]
            prod = doi.astype(bf16).astype(f32) * o_ref[pl.ds(lo, tq), :]
            lse_i = lse_ref[pl.ds(lo, tq), :]
            cc_i = cc_ref[pl.ds(lo, tq), :]
            dq_i = jnp.zeros((tq, LANES), f32)
            for hh in range(2):
                qh = jnp.where(masks[hh], qi, jnp.zeros((), bf16))
                doh = jnp.where(masks[hh], doi, 0.0).astype(bf16)
                delta = jnp.sum(jnp.where(masks[hh], prod, 0.0), axis=-1, keepdims=True)
                sc = _dot_nt(qh, kj) * ATT_SCALE + (cc_i[:, hh:hh + 1] - cr_ref[hh:hh + 1, pl.ds(lo_j, tq)])
                p = jnp.exp(sc - lse_i[:, hh * HEAD_DIM:hh * HEAD_DIM + 1])
                if diag:
                    p = jnp.where(col <= row, p, 0.0)
                dv = dv + _dot_tn(p.astype(bf16), doh)
                ds = p * (_dot_nt(doh, vj) - delta)
                dcs[hh] = dcs[hh] + jnp.sum(ds, axis=0, keepdims=True)
                dsb = (ds * ATT_SCALE).astype(bf16)
                dq_i = jnp.where(masks[hh], _dot(dsb, kj), dq_i)
                dk = dk + _dot_tn(dsb, qh)
            dq_ref[pl.ds(lo, tq), :] += dq_i
            return dk, dv, dcs[0], dcs[1]

        zero = jnp.zeros((tq, LANES), f32)
        zrow = jnp.zeros((1, tq), f32)
        carry = block(j, (zero, zero, zrow, zrow), True)
        dk, dv, dc0, dc1 = lax.fori_loop(j + 1, nq, functools.partial(block, diag=False), carry)
        dk_ref[...] = dk.astype(bf16)
        dv_ref[...] = dv.astype(bf16)
        dc_ref[0:1, :] = -dc0
        dc_ref[1:2, :] = -dc1

    full = lambda p, j: (0, p)
    tile = lambda p, j: (j, p)
    return pl.pallas_call(
        body, grid=(FOX_PAIRS, nq),
        in_specs=[pl.BlockSpec((s, LANES), full), pl.BlockSpec((tq, LANES), tile), pl.BlockSpec((tq, LANES), lambda p, j: (j, FOX_PAIRS + p)),
                  pl.BlockSpec((s, LANES), full), pl.BlockSpec((s, LANES), full), pl.BlockSpec((None, s, LANES), lambda p, j: (p, 0, 0)),
                  pl.BlockSpec((None, s, 2), lambda p, j: (p, 0, 0)), pl.BlockSpec((None, 2, s), lambda p, j: (p, 0, 0))],
        out_specs=[pl.BlockSpec((s, LANES), full), pl.BlockSpec((tq, LANES), tile), pl.BlockSpec((tq, LANES), tile),
                   pl.BlockSpec((None, 2, tq), lambda p, j: (p, 0, j))],
        out_shape=[SDS((s, MAIN_WIDTH), f32), SDS((s, MAIN_WIDTH), bf16), SDS((s, MAIN_WIDTH), bf16), SDS((FOX_PAIRS, 2, s), f32)],
        name=name, compiler_params=_cparams("parallel", "arbitrary"))(q, kv, kv, d_mixed, o, lse, c_col, c_row)


def adamw(w, g, m, v, name, tr=256):
    r, c = w.shape
    tr = min(tr, r)
    assert r % tr == 0, (name, r, tr)
    c1 = 1.0 / (1.0 - ADAM_B1 ** ADAM_STEP)
    c2 = 1.0 / (1.0 - ADAM_B2 ** ADAM_STEP)

    def body(w_ref, g_ref, m_ref, v_ref, d_ref, mo_ref, vo_ref):
        gv = g_ref[...]
        mn = ADAM_B1 * m_ref[...] + (1.0 - ADAM_B1) * gv
        vn = ADAM_B2 * v_ref[...] + (1.0 - ADAM_B2) * gv * gv
        mo_ref[...] = mn
        vo_ref[...] = vn
        d_ref[...] = -ADAM_LR * ((mn * c1) / (jnp.sqrt(vn * c2) + ADAM_EPS) + ADAM_WD * w_ref[...])

    spec = pl.BlockSpec((tr, c), lambda i: (i, 0))
    return pl.pallas_call(body, grid=(r // tr,), in_specs=[spec] * 4, out_specs=[spec] * 3, out_shape=[SDS((r, c), f32)] * 3,
                          name=name, compiler_params=_cparams("parallel"))(w, g, m, v)


def sum_leading(x, name, out_dtype=f32, tr=None):
    n, r, c = x.shape
    tr = tr or r
    assert r % tr == 0

    def body(x_ref, o_ref):
        acc = x_ref[0].astype(f32)
        for k in range(1, n):
            acc = acc + x_ref[k].astype(f32)
        o_ref[...] = acc.astype(out_dtype)

    return pl.pallas_call(body, grid=(r // tr,), in_specs=[pl.BlockSpec((n, tr, c), lambda i: (0, i, 0))],
                          out_specs=pl.BlockSpec((tr, c), lambda i: (i, 0)), out_shape=SDS((r, c), out_dtype), name=name,
                          compiler_params=_cparams("parallel"))(x)


_ANY = pl.BlockSpec(memory_space=pl.ANY)
_DMA = pltpu.SemaphoreType.DMA


_HBM = pl.BlockSpec(memory_space=pltpu.HBM)
_SEM = pl.BlockSpec(memory_space=pltpu.SEMAPHORE)
_EFFECT = pltpu.SideEffectType.DATAFLOW_SIDE_EFFECTING
_FLIPS = [(0, 0, 1), (1, 0, 0), (0, 1, 0), (1, 1, 0), (1, 0, 1), (0, 1, 1), (1, 1, 1)]


def _me():
    return lax.axis_index("x"), lax.axis_index("y"), lax.axis_index("c")


def _peers():
    mx, my, mc = _me()
    return [(jnp.bitwise_xor(mx, fx), jnp.bitwise_xor(my, fy), jnp.bitwise_xor(mc, fc)) for fx, fy, fc in _FLIPS]


def _index(dev):
    return 4 * dev[0] + 2 * dev[1] + dev[2]


def _win(ref, axis, k, size, count=1):
    idx = [slice(None)] * len(ref.shape)
    idx[axis] = pl.ds(k * size, count * size)
    return ref.at[tuple(idx)]


def _hbm(a):
    return pltpu.with_memory_space_constraint(a, pltpu.HBM)


def _exchange_start(srcs, lands, copies_of, name):
    n = len(srcs)

    def body(*refs):
        src = refs[:n]
        send_sems, recv_sems, self_sems = refs[2 * n:2 * n + 3]
        land = refs[3 * n + 3:4 * n + 3]
        token = refs[4 * n + 3]
        me = _index(_me())
        for a in range(n):
            for s_ref, d_ref, peer in copies_of(a, src[a], land[a], me):
                if peer is None:
                    pltpu.make_async_copy(s_ref, d_ref, self_sems.at[a]).start()
                else:
                    pltpu.make_async_remote_copy(src_ref=s_ref, dst_ref=d_ref, send_sem=send_sems.at[a], recv_sem=recv_sems.at[a],
                                                 device_id=peer, device_id_type=MESH).start()
        token[...] = jnp.zeros_like(token)

    outs = pl.pallas_call(
        body, name=name,
        out_shape=(_DMA((n,)), _DMA((n,)), _DMA((n,)), *[pltpu.HBM(s.shape, s.dtype) for s in srcs],
                   *[pltpu.HBM(l.shape, l.dtype) for l in lands], SDS((8, LANES), f32)),
        in_specs=[_HBM] * (2 * n), out_specs=(_SEM, _SEM, _SEM, *[_HBM] * (2 * n), pl.BlockSpec(memory_space=pltpu.VMEM)),
        input_output_aliases={i: 3 + i for i in range(2 * n)},
        compiler_params=pltpu.CompilerParams(has_side_effects=_EFFECT),
    )(*[_hbm(s) for s in srcs], *[_hbm(lax.empty(l.shape, l.dtype)) for l in lands])
    return dict(sems=outs[:3], srcs=list(outs[3:3 + n]), lands=list(outs[3 + n:3 + 2 * n]), token=outs[3 + 2 * n])


def _exchange_wait(started, waits_of, after, name):
    srcs, lands = started["srcs"], started["lands"]
    n = len(srcs)

    def body(*refs):
        src = refs[:n]
        land = refs[n:2 * n]
        send_sems, recv_sems, self_sems = refs[2 * n:2 * n + 3]
        me = _index(_me())
        for a in range(n):
            seven, (s_ref, d_ref) = waits_of(a, src[a], land[a], me)
            both = pltpu.make_async_remote_copy(src_ref=seven, dst_ref=seven, send_sem=send_sems.at[a], recv_sem=recv_sems.at[a],
                                                device_id=_me(), device_id_type=MESH)
            both.wait_send()
            both.wait_recv()
            pltpu.make_async_copy(s_ref, d_ref, self_sems.at[a]).wait()

    outs = pl.pallas_call(
        body, name=name, out_shape=tuple(pltpu.HBM(t.shape, t.dtype) for t in srcs + lands),
        in_specs=[_HBM] * (2 * n) + [_SEM] * 3 + [_ANY], out_specs=tuple([_HBM] * (2 * n)),
        input_output_aliases={i: i for i in range(2 * n)},
        compiler_params=pltpu.CompilerParams(has_side_effects=_EFFECT),
    )(*srcs, *lands, *started["sems"], after)
    return list(outs[n:])


def gather_start(locs, axes, name):
    lands = [SDS(tuple(N_DEV * d if i == ax else d for i, d in enumerate(l.shape)), l.dtype) for l, ax in zip(locs, axes)]

    def copies_of(a, src, land, me):
        mine = _win(land, axes[a], me, src.shape[axes[a]])
        return [(src, mine, peer) for peer in _peers()] + [(src, mine, None)]

    return _exchange_start(locs, lands, copies_of, name)


def gather_wait(started, axes, after, name):
    def waits_of(a, src, land, me):
        size = src.shape[axes[a]]
        return _win(land, axes[a], 0, size, N_DEV - 1), (src, _win(land, axes[a], me, size))

    return _exchange_wait(started, waits_of, after, name)


def scatter_start(grads, axes, name):
    lands = [SDS((N_DEV,) + tuple(d // N_DEV if i == ax else d for i, d in enumerate(g.shape)), g.dtype) for g, ax in zip(grads, axes)]

    def copies_of(a, src, land, me):
        size = src.shape[axes[a]] // N_DEV
        out = [(_win(src, axes[a], _index(peer), size), land.at[me], peer) for peer in _peers()]
        return out + [(_win(src, axes[a], me, size), land.at[me], None)]

    return _exchange_start(grads, lands, copies_of, name)


def scatter_wait(started, axes, after, name):
    def waits_of(a, src, land, me):
        size = src.shape[axes[a]] // N_DEV
        return land.at[pl.ds(0, N_DEV - 1)], (_win(src, axes[a], me, size), land.at[me])

    return _exchange_wait(started, waits_of, after, name)


def _row_tile(rows, cap=512):
    return max(t for t in range(8, min(rows, cap) + 1, 8) if rows % t == 0)


_SMALL = [
    ("ln_mix_pre", (2, 1024)), ("ln_mix_post", (2, 1024)), ("ln_ffn_pre", (2, 1024)), ("ln_ffn_post", (2, 1024)),
    ("ln_mem", (2, 1024)), ("w_spatial", (1, 6, 128, 128)), ("b_spatial", (1, 6, 128)), ("ln_shared", (1024,)),
    ("b_forget", (12,)), ("ln_v_g", (1, 768)), ("ln_v_b", (1, 768)),
]
_SMALL_TILE = 8 * LANES


def _small_rows(shape):
    return -(-math.prod(shape) // _SMALL_TILE) * 8


def _pack_small(vals, shapes):
    parts = []
    for name, shape in shapes:
        flat = vals[name].reshape(-1).astype(f32)
        rows = _small_rows(shape)
        parts.append(jnp.pad(flat, (0, rows * LANES - flat.shape[0])).reshape(rows, LANES))
    return jnp.concatenate(parts, axis=0)


def _unpack_small(buf, shapes):
    out = {}
    lo = 0
    for name, shape in shapes:
        rows = _small_rows(shape)
        out[name] = buf[lo:lo + rows].reshape(-1)[:math.prod(shape)].reshape(shape)
        lo += rows
    return out


def kernel(x, mem, ln_mix_pre, ln_mix_post, ln_ffn_pre, ln_ffn_post, ln_mem, w_mem_kv, w_out, w_ffn_gate, w_ffn_up, w_ffn_down, w_in_a, w_spatial, b_spatial, ln_v_g, ln_v_b, ln_shared, w_shared_kv, b_forget, w_in_b, loss_target, m_ln_mix_pre, m_ln_mix_post, m_ln_ffn_pre, m_ln_ffn_post, m_ln_mem, m_w_mem_kv, m_w_out, m_w_ffn_gate, m_w_ffn_up, m_w_ffn_down, m_w_in_a, m_w_spatial, m_b_spatial, m_ln_v_g, m_ln_v_b, m_ln_shared, m_w_shared_kv, m_b_forget, m_w_in_b, v_ln_mix_pre, v_ln_mix_post, v_ln_ffn_pre, v_ln_ffn_post, v_ln_mem, v_w_mem_kv, v_w_out, v_w_ffn_gate, v_w_ffn_up, v_w_ffn_down, v_w_in_a, v_w_spatial, v_b_spatial, v_ln_v_g, v_ln_v_b, v_ln_shared, v_w_shared_kv, v_b_forget, v_w_in_b):
    weights = dict(ln_mix_pre=ln_mix_pre, ln_mix_post=ln_mix_post, ln_ffn_pre=ln_ffn_pre, ln_ffn_post=ln_ffn_post, ln_mem=ln_mem,
                   w_mem_kv=w_mem_kv, w_out=w_out, w_ffn_gate=w_ffn_gate, w_ffn_up=w_ffn_up, w_ffn_down=w_ffn_down, w_in_a=w_in_a,
                   w_spatial=w_spatial, b_spatial=b_spatial, ln_v_g=ln_v_g, ln_v_b=ln_v_b, ln_shared=ln_shared,
                   w_shared_kv=w_shared_kv, b_forget=b_forget, w_in_b=w_in_b)
    mom_m = dict(ln_mix_pre=m_ln_mix_pre, ln_mix_post=m_ln_mix_post, ln_ffn_pre=m_ln_ffn_pre, ln_ffn_post=m_ln_ffn_post, ln_mem=m_ln_mem,
                 w_mem_kv=m_w_mem_kv, w_out=m_w_out, w_ffn_gate=m_w_ffn_gate, w_ffn_up=m_w_ffn_up, w_ffn_down=m_w_ffn_down, w_in_a=m_w_in_a,
                 w_spatial=m_w_spatial, b_spatial=m_b_spatial, ln_v_g=m_ln_v_g, ln_v_b=m_ln_v_b, ln_shared=m_ln_shared,
                 w_shared_kv=m_w_shared_kv, b_forget=m_b_forget, w_in_b=m_w_in_b)
    mom_v = dict(ln_mix_pre=v_ln_mix_pre, ln_mix_post=v_ln_mix_post, ln_ffn_pre=v_ln_ffn_pre, ln_ffn_post=v_ln_ffn_post, ln_mem=v_ln_mem,
                 w_mem_kv=v_w_mem_kv, w_out=v_w_out, w_ffn_gate=v_w_ffn_gate, w_ffn_up=v_w_ffn_up, w_ffn_down=v_w_ffn_down, w_in_a=v_w_in_a,
                 w_spatial=v_w_spatial, b_spatial=v_b_spatial, ln_v_g=v_ln_v_g, ln_v_b=v_ln_v_b, ln_shared=v_ln_shared,
                 w_shared_kv=v_w_shared_kv, b_forget=v_b_forget, w_in_b=v_w_in_b)
    names = list(weights)
    mx, my, mc = lax.axis_index("x"), lax.axis_index("y"), lax.axis_index("c")
    me = 4 * mx + 2 * my + mc

    h0 = x[0]
    mem0 = mem[0]
    tgt = loss_target[0]
    seq = h0.shape[0]

    vec = lambda a: a.reshape(1, -1)
    pad_to = lambda a, axis, size: jnp.pad(a, [(0, size - a.shape[i] if i == axis else 0) for i in range(a.ndim)])

    def after(tok, a):
        return a + tok[0, 0].astype(a.dtype)

    lnv_loc = pad_to(jnp.concatenate([ln_v_g, ln_v_b], axis=0), 0, 8)
    st_a = gather_start([w_in_a.astype(bf16), pad_to(lnv_loc, 1, LANES)[None]], [0, 0], "gather_a_start")
    mix_locs = lambda l, tok: [after(tok, w_mem_kv[l]).astype(bf16), w_out[l].astype(bf16)]
    ffn_locs = lambda l, tok: [pad_to(after(tok, w_ffn_gate[l]).astype(bf16), 1, FF_SHARD_PAD),
                               pad_to(w_ffn_up[l].astype(bf16), 1, FF_SHARD_PAD), pad_to(w_ffn_down[l].astype(bf16), 0, FF_SHARD_PAD)]
    st_b = [gather_start(mix_locs(0, st_a["token"]), [0, 0], "gather_b0_start"), None]
    st_c = gather_start(ffn_locs(0, st_b[0]["token"]), [1, 1, 0], "gather_c_start")
    st_d = gather_start([after(st_c["token"], w_in_b[0]).astype(bf16), pad_to(w_shared_kv.astype(bf16), 1, KV_PAD)], [0, 0],
                        "gather_d_start")
    st_b[1] = gather_start(mix_locs(1, st_d["token"]), [0, 0], "gather_b1_start")
    st_e = gather_start(ffn_locs(1, st_b[1]["token"]), [1, 1, 0], "gather_e_start")
    ws = w_spatial[0].astype(bf16)
    ws_t = ws.transpose(0, 2, 1)
    bs_t = b_spatial[0].T

    (a0,) = rms_fwd(h0, [after(st_e["token"], vec(ln_mix_pre[0]))], "a0_norm")
    w_in_a8, lnv8 = gather_wait(st_a, [0, 0], a0, "gather_a_wait")
    w_in_a_full = w_in_a8.transpose(1, 0, 2).reshape(D_MODEL, -1)
    lnv_g = lnv8[:, 0, :MAIN_WIDTH // N_DEV].reshape(1, MAIN_WIDTH)
    lnv_b = lnv8[:, 1, :MAIN_WIDTH // N_DEV].reshape(1, MAIN_WIDTH)
    proj0 = mm(a0, w_in_a_full, "proj0", tn=896)
    main0 = gmlp_fwd(proj0, ws, bs_t, lnv_g, lnv_b, "gmlp_fwd")
    w_mkv, w_o = [None, None], [None, None]
    w_mkv[0], w_o[0] = gather_wait(st_b[0], [0, 0], main0, "gather_b0_wait")
    (memn0,) = rms_fwd(mem0, [vec(ln_mem[0])], "mem0_norm")
    kvm0 = mm(memn0, w_mkv[0], "kvm0")
    om0 = mem_attn_fwd(proj0, 2 * MAIN_WIDTH // MEM_WIDTH, kvm0, "mem_attn0")
    mixed0 = jnp.concatenate([main0, om0], axis=-1)
    y1_0, hmid0, f0 = mm_resnorm(mixed0, w_o[0], h0, vec(ln_mix_post[0]), [vec(ln_ffn_pre[0])], "mix_out0")
    w_g0, w_u0, w_d0 = gather_wait(st_c, [1, 1, 0], f0, "gather_c_wait")
    gu0, act0 = ffn_up(f0, w_g0, w_u0, "ffn_up0")
    y2_0, h1, a1, sin1 = mm_resnorm(act0, w_d0, hmid0, vec(ln_ffn_post[0]), [vec(ln_mix_pre[1]), vec(ln_shared)], "ffn_down0")

    w_inb, w_kv = gather_wait(st_d, [0, 0], sin1, "gather_d_wait")
    kvb = mm(sin1, w_kv, "kv_shared", out_dtype=bf16, tn=MAIN_WIDTH, ncols=2 * MAIN_WIDTH)
    zf = mm(sin1, w_kv, "forget_logits", tn=256, col0=2 * MAIN_WIDTH, ncols=256)
    qb = mm(a1, w_inb, "proj1", out_dtype=bf16)
    z_t = jnp.pad(zf[:, :FOX_HEADS].T, ((0, 16 - FOX_HEADS), (0, 0)))
    bf_col = jnp.pad(b_forget, (0, 16 - FOX_HEADS)).reshape(16, 1)
    c_t = fgate_fwd(z_t, bf_col, "fgate_fwd")
    c_row = c_t[:FOX_HEADS].reshape(FOX_PAIRS, 2, seq)
    c_col = c_row.transpose(0, 2, 1)
    main1, lse = fox_fwd(qb, kvb, c_col, c_row, "fox_fwd")
    w_mkv[1], w_o[1] = gather_wait(st_b[1], [0, 0], main1, "gather_b1_wait")
    (memn1,) = rms_fwd(mem0, [vec(ln_mem[1])], "mem1_norm")
    kvm1 = mm(memn1, w_mkv[1], "kvm1")
    om1 = mem_attn_fwd(qb, MAIN_WIDTH // MEM_WIDTH, kvm1, "mem_attn1")
    mixed1 = jnp.concatenate([main1.astype(bf16), om1], axis=-1)
    y1_1, hmid1, f1 = mm_resnorm(mixed1, w_o[1], h1, vec(ln_mix_post[1]), [vec(ln_ffn_pre[1])], "mix_out1")
    w_g1, w_u1, w_d1 = gather_wait(st_e, [1, 1, 0], f1, "gather_e_wait")
    gu1, act1 = ffn_up(f1, w_g1, w_u1, "ffn_up1")
    y2_1, h2 = mm_resnorm(act1, w_d1, hmid1, vec(ln_ffn_post[1]), [], "ffn_down1")
    dh, loss_tile = loss_grad(h2, tgt, "loss")
    loss = lax.psum(loss_tile[0, 0], AXES)
    ffn_w = [(w_g0, w_u0, w_d0), (w_g1, w_u1, w_d1)]

    small = {}

    def ffn_backward(layer, dh_out, y2, hmid, f, gu, act, y1):
        w_g, w_u, w_d = ffn_w[layer]
        d_y2, dg_fpost = rms_bwd(y2, vec(ln_ffn_post[layer]), dh_out, None, bf16, f"ffn_post_bwd{layer}")
        d_g, d_u = ffn_act_grad(d_y2, w_d, gu, f"ffn_act_grad{layer}")
        dw_down = mm_tn(act, d_y2, f"dw_down{layer}", tk=256)
        dw_g = mm_tn(f, d_g, f"dw_gate{layer}")
        dw_u = mm_tn(f, d_u, f"dw_up{layer}")
        dh_mid, d_y1, dg_fpre, dg_mpost = ffn_in_grad(d_g, d_u, w_g, w_u, hmid, dh_out, vec(ln_ffn_pre[layer]), y1,
                                                      vec(ln_mix_post[layer]), f"ffn_in_grad{layer}")
        return dh_mid, d_y1, dg_fpost, dg_fpre, dg_mpost, [dw_g, dw_u, dw_down]

    def mix_out_backward(layer, d_y1, mixed, tok):
        dw_out = mm_tn(mixed, d_y1, f"dw_out{layer}")
        d_mixed = mm(d_y1, w_o[layer], f"d_mixed{layer}", trans_b=True, dep=tok)
        return d_mixed, dw_out

    def mem_backward(layer, q_src, q_block, kvm, memn, d_mixed):
        d_qm, d_kvm = mem_attn_bwd(q_src, q_block, kvm, d_mixed, f"mem_attn_bwd{layer}")
        d_kvm_b = d_kvm.astype(bf16)
        dw_mkv = mm_tn(memn, d_kvm_b, f"dw_mem_kv{layer}")
        d_memn = mm(d_kvm_b, w_mkv[layer], f"d_memn{layer}", trans_b=True)
        _, dg_mem = rms_bwd(mem0, vec(ln_mem[layer]), d_memn, None, bf16, f"mem_norm_bwd{layer}")
        return d_qm, dw_mkv, dg_mem

    ffn_axes = [1, 1, 0]

    dh_mid1, d_y1_1, dg_fpost1, dg_fpre1, dg_mpost1, dw_ffn1 = ffn_backward(1, dh, y2_1, hmid1, f1, gu1, act1, y1_1)
    rs_ffn1 = scatter_start(dw_ffn1, ffn_axes, "scatter_ffn1_start")
    d_mixed1, dw_out1 = mix_out_backward(1, d_y1_1, mixed1, rs_ffn1["token"])
    d_qm1, dw_mkv1, dg_mem1 = mem_backward(1, qb, MAIN_WIDTH // MEM_WIDTH, kvm1, memn1, d_mixed1)
    rs_mix1 = scatter_start([dw_out1, dw_mkv1], [0, 0], "scatter_mix1_start")
    dq, dk, dv, dc = fox_bwd(qb, kvb, d_mixed1, main1, lse, after(rs_mix1["token"], c_col), c_row, "fox_bwd")
    dc_t = jnp.pad(dc.reshape(FOX_HEADS, seq), ((0, 16 - FOX_HEADS), (0, 0)))
    dz_t, db_f = fgate_bwd(dc_t, z_t, bf_col, "fgate_bwd")
    d_kvf = jnp.concatenate([dk, dv, jnp.pad(dz_t[:FOX_HEADS].T.astype(bf16), ((0, 0), (0, KV_PAD - KV_WIDTH)))], axis=-1)
    d_proj1 = jnp.concatenate([dq.astype(bf16), d_qm1], axis=-1)
    dw_in_b = mm_tn(a1, d_proj1, "dw_in_b")
    dw_kv = mm_tn(sin1, d_kvf, "dw_kv", tn=896)
    rs_2 = scatter_start([dw_in_b, dw_kv], [0, 0], "scatter_shared_start")
    dh1, (dg_pre1, dg_shared) = proj_in_grad([(d_proj1, w_inb, vec(ln_mix_pre[1])), (d_kvf, w_kv, vec(ln_shared))], h1, dh_mid1,
                                             "in_grad1", dep=rs_2["token"])

    dh_mid0, d_y1_0, dg_fpost0, dg_fpre0, dg_mpost0, dw_ffn0 = ffn_backward(0, dh1, y2_0, hmid0, f0, gu0, act0, y1_0)
    rs_ffn0 = scatter_start(dw_ffn0, ffn_axes, "scatter_ffn0_start")
    d_mixed0, dw_out0 = mix_out_backward(0, d_y1_0, mixed0, rs_ffn0["token"])
    d_qm0, dw_mkv0, dg_mem0 = mem_backward(0, proj0, 2 * MAIN_WIDTH // MEM_WIDTH, kvm0, memn0, d_mixed0)
    rs_mix0 = scatter_start([dw_out0, dw_mkv0], [0, 0], "scatter_mix0_start")
    d_uv, dw_s, db_s, dg_lnv, db_lnv = gmlp_bwd(proj0, d_mixed0, ws, ws_t, bs_t, after(rs_mix0["token"], lnv_g), lnv_b, "gmlp_bwd")

    small["ln_mix_pre"] = jnp.concatenate([jnp.zeros_like(dg_pre1), dg_pre1], axis=0)
    small["ln_mix_post"] = jnp.concatenate([dg_mpost0, dg_mpost1], axis=0)
    small["ln_ffn_pre"] = jnp.concatenate([dg_fpre0, dg_fpre1], axis=0)
    small["ln_ffn_post"] = jnp.concatenate([dg_fpost0, dg_fpost1], axis=0)
    small["ln_mem"] = jnp.concatenate([dg_mem0, dg_mem1], axis=0)
    small["w_spatial"] = dw_s[None]
    small["b_spatial"] = db_s[:, :A_GROUPS].T[None]
    small["ln_shared"] = dg_shared[0]
    small["b_forget"] = db_f[:FOX_HEADS, 0]
    small["ln_v_g"] = dg_lnv
    small["ln_v_b"] = db_lnv
    st_small = gather_start([_pack_small(small, _SMALL)[None]], [0], "gather_small_grads_start")
    d_proj0 = jnp.concatenate([d_uv, after(st_small["token"], d_qm0)], axis=-1)
    dw_in_a = mm_tn(a0, d_proj0, "dw_in_a", tn=896)
    rs_in_a = scatter_start([dw_in_a.reshape(D_MODEL, N_DEV, -1).transpose(1, 0, 2)], [0], "scatter_in_a_start")
    grad_x, (dg_pre0,) = proj_in_grad([(d_proj0, w_in_a_full, vec(ln_mix_pre[0]))], h0, dh_mid0, "in_grad0", dep=rs_in_a["token"])
    st_last = gather_start([dg_pre0.reshape(1, 8, LANES)], [0], "gather_last_grad_start")

    def owned(started, axes, wait_after, name):
        recv = scatter_wait(started, axes, wait_after, name)
        return [sum_leading(r.reshape((N_DEV, -1, r.shape[-1])), f"{name}_sum{i}", tr=_row_tile(math.prod(r.shape[1:-1])))
                for i, r in enumerate(recv)]

    g_ffn1 = owned(rs_ffn1, ffn_axes, after(st_last["token"], grad_x[:8, :LANES]), "scatter_ffn1_wait")
    g_mix1 = owned(rs_mix1, [0, 0], g_ffn1[0], "scatter_mix1_wait")
    g2 = owned(rs_2, [0, 0], g_mix1[0], "scatter_shared_wait")
    g_ffn0 = owned(rs_ffn0, ffn_axes, g2[0], "scatter_ffn0_wait")
    g_mix0 = owned(rs_mix0, [0, 0], g_ffn0[0], "scatter_mix0_wait")
    (g_in_a,) = owned(rs_in_a, [0], g_mix0[0], "scatter_in_a_wait")
    g_local = dict(
        w_ffn_gate=jnp.stack([g_ffn0[0], g_ffn1[0]])[:, :, :FF_SHARD], w_ffn_up=jnp.stack([g_ffn0[1], g_ffn1[1]])[:, :, :FF_SHARD],
        w_ffn_down=jnp.stack([g_ffn0[2], g_ffn1[2]])[:, :FF_SHARD], w_out=jnp.stack([g_mix0[0], g_mix1[0]]),
        w_mem_kv=jnp.stack([g_mix0[1], g_mix1[1]]), w_in_b=g2[0][None], w_shared_kv=g2[1][:, :KV_WIDTH], w_in_a=g_in_a[None])
    (small_all,) = gather_wait(st_small, [0], g_in_a, "gather_small_grads_wait")
    (last_all,) = gather_wait(st_last, [0], small_all, "gather_last_grad_wait")
    g_small = _unpack_small(sum_leading(small_all, "sum_small_grads"), _SMALL)
    g_small["ln_mix_pre"] = jnp.concatenate([sum_leading(last_all, "sum_last_grad").reshape(1, D_MODEL), g_small["ln_mix_pre"][1:]], axis=0)
    shard = MAIN_WIDTH // N_DEV
    for n in ("ln_v_g", "ln_v_b"):
        g_small[n] = lax.dynamic_slice_in_dim(g_small[n], me * shard, shard, axis=1)
    grad_w = {**g_small, **g_local}

    delta, new_m, new_v = {}, {}, {}
    for n in g_local:
        two_d = (-1, weights[n].shape[-1])
        d_, m_, v_ = adamw(weights[n].reshape(two_d), grad_w[n].reshape(two_d), mom_m[n].reshape(two_d), mom_v[n].reshape(two_d),
                           f"adamw_{n}", tr=_row_tile(math.prod(weights[n].shape[:-1])))
        delta[n], new_m[n], new_v[n] = (t.reshape(weights[n].shape) for t in (d_, m_, v_))
    small_local_shapes = [(n, tuple(weights[n].shape)) for n, _ in _SMALL]
    packed = [_pack_small(src, small_local_shapes) for src in (weights, grad_w, mom_m, mom_v)]
    outs = adamw(*packed, "adamw_small", tr=packed[0].shape[0])
    for dst, buf in zip((delta, new_m, new_v), outs):
        dst.update(_unpack_small(buf, small_local_shapes))

    return (loss, grad_x[None], *[grad_w[n] for n in names], *[delta[n] for n in names],
            *[new_m[n] for n in names], *[new_v[n] for n in names])
```

```python
import functools
import math

import jax
import jax.numpy as jnp
from jax import lax
from jax.experimental import pallas as pl
from jax.experimental.pallas import tpu as pltpu

f32 = jnp.float32
bf16 = jnp.bfloat16
SDS = jax.ShapeDtypeStruct

D_MODEL = 1024
MAIN_WIDTH = 768
MEM_WIDTH = 256
HEAD_DIM = 64
MEM_HEADS = 4
FOX_HEADS = 12
FOX_PAIRS = FOX_HEADS // 2
CHUNK = 128
A_GROUPS = 6
FF_SHARD = 352
FF_SHARD_PAD = 384
FF_PAD = 8 * FF_SHARD_PAD
KV_WIDTH = 2 * MAIN_WIDTH + FOX_HEADS
KV_PAD = 1792
RMS_EPS = 1e-6
LN_EPS = 1e-5
ATT_SCALE = HEAD_DIM ** -0.5
ADAM_LR, ADAM_B1, ADAM_B2, ADAM_EPS, ADAM_WD, ADAM_STEP = 0.001, 0.9, 0.999, 1e-08, 0.01, 10
N_DEV = 8
AXES = ("x", "y", "c")
MESH = pl.DeviceIdType.MESH
V7X_VMEM_LIMIT = 56 * 1024 * 1024
LANES = 128
FLAT_W = 512
ROW_PAD = 16


def _cparams(*sem):
    return pltpu.CompilerParams(dimension_semantics=sem or None, vmem_limit_bytes=V7X_VMEM_LIMIT)


def _dot(a, b):
    return jnp.dot(a, b, preferred_element_type=f32)


def _dot_nt(a, b):
    return lax.dot_general(a, b, (((1,), (1,)), ((), ())), preferred_element_type=f32)


def _dot_tn(a, b):
    return lax.dot_general(a, b, (((0,), (0,)), ((), ())), preferred_element_type=f32)


def _gelu(x):
    k = math.sqrt(2.0 / math.pi)
    t = jnp.tanh(k * (x + 0.044715 * x * x * x))
    return 0.5 * x * (1.0 + t), t


def _gelu_grad(x, t):
    k = math.sqrt(2.0 / math.pi)
    return 0.5 * (1.0 + t) + 0.5 * x * (1.0 - t * t) * k * (1.0 + 3.0 * 0.044715 * x * x)


def _sigmoid(x):
    return 1.0 / (1.0 + jnp.exp(-x))


def rms_fwd(x, gains, name, tm=512):
    m, d = x.shape
    tm = min(tm, m)
    n = len(gains)

    def body(x_ref, *refs):
        xv = x_ref[...]
        y = xv * lax.rsqrt(jnp.sum(xv * xv, axis=-1, keepdims=True) * (1.0 / d) + RMS_EPS)
        for g_ref, o_ref in zip(refs[:n], refs[n:]):
            o_ref[...] = (y * g_ref[...]).astype(bf16)

    row = pl.BlockSpec((tm, d), lambda i: (i, 0))
    vec = pl.BlockSpec((1, d), lambda i: (0, 0))
    return pl.pallas_call(body, grid=(m // tm,), in_specs=[row] + [vec] * n, out_specs=[row] * n,
                          out_shape=[SDS((m, d), bf16)] * n, name=name, compiler_params=_cparams("parallel"))(x, *gains)


def rms_bwd(x, g, dy, add, out_dtype, name, tm=512):
    m, d = x.shape
    tm = min(tm, m)
    has_add = add is not None

    def body(x_ref, g_ref, dy_ref, *refs):
        dx_ref, dg_ref = refs[-2], refs[-1]
        xv = x_ref[...]
        dyv = dy_ref[...].astype(f32)
        r = lax.rsqrt(jnp.sum(xv * xv, axis=-1, keepdims=True) * (1.0 / d) + RMS_EPS)
        xn = xv * r
        dyg = dyv * g_ref[...]
        dx = r * (dyg - xn * (jnp.sum(dyg * xn, axis=-1, keepdims=True) * (1.0 / d)))
        if has_add:
            dx = dx + refs[0][...]
        dx_ref[...] = dx.astype(out_dtype)

        @pl.when(pl.program_id(0) == 0)
        def _():
            dg_ref[...] = jnp.zeros_like(dg_ref)

        dg_ref[...] += jnp.sum(dyv * xn, axis=0, keepdims=True)

    row = pl.BlockSpec((tm, d), lambda i: (i, 0))
    vec = pl.BlockSpec((1, d), lambda i: (0, 0))
    ins = [x, g, dy] + ([add] if has_add else [])
    return pl.pallas_call(body, grid=(m // tm,), in_specs=[row, vec, row] + ([row] if has_add else []),
                          out_specs=[row, vec], out_shape=[SDS((m, d), out_dtype), SDS((1, d), f32)], name=name,
                          compiler_params=_cparams("arbitrary"))(*ins)


def loss_grad(h, tgt, name, tm=512):
    m, d = h.shape

    def body(h_ref, t_ref, dy_ref, l_ref):
        e = h_ref[...] - t_ref[...]
        dy_ref[...] = e * (1.0 / d)

        @pl.when(pl.program_id(0) == 0)
        def _():
            l_ref[...] = jnp.zeros_like(l_ref)

        part = jnp.sum(jnp.sum(e * e, axis=-1, keepdims=True), axis=0, keepdims=True) * (0.5 / d)
        l_ref[...] += jnp.broadcast_to(part, l_ref.shape)

    row = pl.BlockSpec((tm, d), lambda i: (i, 0))
    return pl.pallas_call(body, grid=(m // tm,), in_specs=[row, row],
                          out_specs=[row, pl.BlockSpec((8, LANES), lambda i: (0, 0))],
                          out_shape=[SDS((m, d), f32), SDS((8, LANES), f32)], name=name,
                          compiler_params=_cparams("arbitrary"))(h, tgt)


def mm(a, b, name, trans_b=False, out_dtype=f32, tm=512, tn=1024, layer=None, col0=0, ncols=None, dep=None):
    m, k = a.shape
    n_all = b.shape[-2] if trans_b else b.shape[-1]
    n = n_all if ncols is None else ncols
    tm, tn = min(tm, m), min(tn, n)
    assert m % tm == 0 and n % tn == 0 and col0 % tn == 0 and not (trans_b and col0), (name, m, n, tm, tn)
    jb = col0 // tn
    lead = () if layer is None else (None,)
    sel = () if layer is None else (layer,)

    def body(a_ref, b_ref, *rest):
        r = _dot_nt(a_ref[...], b_ref[...]) if trans_b else _dot(a_ref[...], b_ref[...])
        rest[-1][...] = r.astype(out_dtype)

    if trans_b:
        b_spec = pl.BlockSpec(lead + (tn, k), lambda j, i: sel + (j, 0))
    else:
        b_spec = pl.BlockSpec(lead + (k, tn), lambda j, i: sel + (0, jb + j))
    deps = [] if dep is None else [dep]
    dep_specs = [pl.BlockSpec((8, LANES), lambda j, i: (0, 0))] * len(deps)
    return pl.pallas_call(body, grid=(n // tn, m // tm), in_specs=[pl.BlockSpec((tm, k), lambda j, i: (i, 0)), b_spec] + dep_specs,
                          out_specs=pl.BlockSpec((tm, tn), lambda j, i: (i, j)), out_shape=SDS((m, n), out_dtype),
                          name=name, compiler_params=_cparams("parallel", "parallel"))(a, b, *deps)


def mm_tn(a, g, name, tk=512, tn=1024, out_dtype=bf16, dep=None):
    s, k = a.shape
    n = g.shape[1]
    tk, tn = min(tk, k), min(tn, n)
    assert k % tk == 0 and n % tn == 0, (name, k, n, tk, tn)

    def body(a_ref, g_ref, *rest):
        rest[-1][...] = _dot_tn(a_ref[...], g_ref[...]).astype(out_dtype)

    deps = [] if dep is None else [dep]
    dep_specs = [pl.BlockSpec((8, LANES), lambda i, j: (0, 0))] * len(deps)
    return pl.pallas_call(body, grid=(k // tk, n // tn),
                          in_specs=[pl.BlockSpec((s, tk), lambda i, j: (0, i)), pl.BlockSpec((s, tn), lambda i, j: (0, j))] + dep_specs,
                          out_specs=pl.BlockSpec((tk, tn), lambda i, j: (i, j)), out_shape=SDS((k, n), out_dtype), name=name,
                          compiler_params=_cparams("parallel", "parallel"))(a, g, *deps)


def _rms(xv):
    return xv * lax.rsqrt(jnp.sum(xv * xv, axis=-1, keepdims=True) * (1.0 / xv.shape[-1]) + RMS_EPS)


def _rms_bwd_math(xv, g, dy):
    d = xv.shape[-1]
    r = lax.rsqrt(jnp.sum(xv * xv, axis=-1, keepdims=True) * (1.0 / d) + RMS_EPS)
    xn = xv * r
    dyg = dy * g
    dx = r * (dyg - xn * (jnp.sum(dyg * xn, axis=-1, keepdims=True) * (1.0 / d)))
    return dx, jnp.sum(dy * xn, axis=0, keepdims=True)


def mm_resnorm(a, b, h, g_post, gains, name, tm=256):
    m, k = a.shape
    d = b.shape[1]
    n = len(gains)

    def body(a_ref, b_ref, h_ref, gp_ref, *refs):
        y = _dot(a_ref[...], b_ref[...])
        refs[n][...] = y
        hn = h_ref[...] + _rms(y) * gp_ref[...]
        refs[n + 1][...] = hn
        if n:
            z = _rms(hn)
            for g_ref, o_ref in zip(refs[:n], refs[n + 2:]):
                o_ref[...] = (z * g_ref[...]).astype(bf16)

    row = pl.BlockSpec((tm, d), lambda i: (i, 0))
    vec = pl.BlockSpec((1, d), lambda i: (0, 0))
    return pl.pallas_call(body, grid=(m // tm,),
                          in_specs=[pl.BlockSpec((tm, k), lambda i: (i, 0)), pl.BlockSpec((k, d), lambda i: (0, 0)), row, vec] + [vec] * n,
                          out_specs=[row] * (n + 2), out_shape=[SDS((m, d), f32)] * 2 + [SDS((m, d), bf16)] * n, name=name,
                          compiler_params=_cparams("parallel"))(a, b, h, g_post, *gains)


def ffn_act_grad(d_y2, w_d, gu, name, tm=512, tn=1536):
    s, d = d_y2.shape
    ff = w_d.shape[0]
    nb = ff // tn

    def body(a_ref, b_ref, g_ref, u_ref, dg_ref, du_ref):
        da = _dot_nt(a_ref[...], b_ref[...])
        gg = g_ref[...].astype(f32)
        sg = _sigmoid(gg)
        dg_ref[...] = (da * u_ref[...].astype(f32) * (sg * (1.0 + gg * (1.0 - sg)))).astype(bf16)
        du_ref[...] = (da * gg * sg).astype(bf16)

    tile = pl.BlockSpec((tm, tn), lambda j, i: (i, j))
    return pl.pallas_call(body, grid=(nb, s // tm),
                          in_specs=[pl.BlockSpec((tm, d), lambda j, i: (i, 0)), pl.BlockSpec((tn, d), lambda j, i: (j, 0)), tile,
                                    pl.BlockSpec((tm, tn), lambda j, i: (i, nb + j))],
                          out_specs=[tile, tile], out_shape=[SDS((s, ff), bf16)] * 2, name=name,
                          compiler_params=_cparams("parallel", "parallel"))(d_y2, w_d, gu, gu)


def ffn_in_grad(d_g, d_u, w_g, w_u, hmid, dh_out, g_pre, y1, g_post, name, tm=256):
    s, ff = d_g.shape
    d = w_g.shape[0]

    def body(dg_ref, du_ref, wg_ref, wu_ref, hm_ref, dho_ref, gpre_ref, y1_ref, gpost_ref, dhm_ref, dy1_ref, dgpre_ref, dgpost_ref):
        @pl.when(pl.program_id(0) == 0)
        def _():
            dgpre_ref[...] = jnp.zeros_like(dgpre_ref)
            dgpost_ref[...] = jnp.zeros_like(dgpost_ref)

        d_f = _dot_nt(dg_ref[...], wg_ref[...]) + _dot_nt(du_ref[...], wu_ref[...])
        dx, dg1 = _rms_bwd_math(hm_ref[...], gpre_ref[...], d_f)
        dh_mid = dho_ref[...] + dx
        dhm_ref[...] = dh_mid
        dgpre_ref[...] += dg1
        dy1, dg2 = _rms_bwd_math(y1_ref[...], gpost_ref[...], dh_mid)
        dy1_ref[...] = dy1.astype(bf16)
        dgpost_ref[...] += dg2

    row = pl.BlockSpec((tm, d), lambda i: (i, 0))
    vec = pl.BlockSpec((1, d), lambda i: (0, 0))
    wide = pl.BlockSpec((tm, ff), lambda i: (i, 0))
    w_spec = pl.BlockSpec((d, ff), lambda i: (0, 0))
    return pl.pallas_call(body, grid=(s // tm,), in_specs=[wide, wide, w_spec, w_spec, row, row, vec, row, vec],
                          out_specs=[row, row, vec, vec], out_shape=[SDS((s, d), f32), SDS((s, d), bf16), SDS((1, d), f32), SDS((1, d), f32)],
                          name=name, compiler_params=_cparams("arbitrary"))(d_g, d_u, w_g, w_u, hmid, dh_out, g_pre, y1, g_post)


def proj_in_grad(pairs, x, add, name, tm=256, dep=None):
    s, d = x.shape
    n = len(pairs)
    deps = [] if dep is None else [dep]

    def body(*refs):
        x_ref, add_ref = refs[3 * n], refs[3 * n + 1]
        outs = refs[3 * n + 2 + len(deps):]

        @pl.when(pl.program_id(0) == 0)
        def _():
            for o in outs[1:]:
                o[...] = jnp.zeros_like(o)

        xv = x_ref[...]
        dx = add_ref[...]
        for i in range(n):
            a_ref, b_ref, g_ref = refs[3 * i:3 * i + 3]
            dxi, dgi = _rms_bwd_math(xv, g_ref[...], _dot_nt(a_ref[...], b_ref[...]))
            dx = dx + dxi
            outs[1 + i][...] += dgi
        outs[0][...] = dx

    row = pl.BlockSpec((tm, d), lambda i: (i, 0))
    vec = pl.BlockSpec((1, d), lambda i: (0, 0))
    in_specs, args = [], []
    for a, b, g in pairs:
        k = a.shape[1]
        in_specs += [pl.BlockSpec((tm, k), lambda i: (i, 0)), pl.BlockSpec((d, k), lambda i: (0, 0)), vec]
        args += [a, b, g]
    in_specs += [row, row] + [pl.BlockSpec((8, LANES), lambda i: (0, 0))] * len(deps)
    out = pl.pallas_call(body, grid=(s // tm,), in_specs=in_specs, out_specs=[row] + [vec] * n,
                         out_shape=[SDS((s, d), f32)] + [SDS((1, d), f32)] * n, name=name,
                         compiler_params=_cparams("arbitrary"))(*args, x, add, *deps)
    return out[0], out[1:]


def ffn_up(f, wg, wu, name, tm=256, tc=256):
    s, d = f.shape
    ff = wg.shape[-1]

    def body(f_ref, wg_ref, wu_ref, gu_ref, act_ref):
        fv = f_ref[...]
        for j in range(ff // tc):
            lo = j * tc
            gg = _dot(fv, wg_ref[:, lo:lo + tc])
            uu = _dot(fv, wu_ref[:, lo:lo + tc])
            gu_ref[:, lo:lo + tc] = gg.astype(bf16)
            gu_ref[:, ff + lo:ff + lo + tc] = uu.astype(bf16)
            act_ref[:, lo:lo + tc] = (gg * _sigmoid(gg) * uu).astype(bf16)

    w_spec = pl.BlockSpec((d, ff), lambda i: (0, 0))
    return pl.pallas_call(body, grid=(s // tm,), in_specs=[pl.BlockSpec((tm, d), lambda i: (i, 0)), w_spec, w_spec],
                          out_specs=[pl.BlockSpec((tm, 2 * ff), lambda i: (i, 0)), pl.BlockSpec((tm, ff), lambda i: (i, 0))],
                          out_shape=[SDS((s, 2 * ff), bf16), SDS((s, ff), bf16)], name=name,
                          compiler_params=_cparams("parallel"))(f, wg, wu)


def _gmlp_forward_chunk(u, v, w_refs, bias, ln_g, ln_b):
    gu, tu = _gelu(u)
    gv, tv = _gelu(v)
    mu = jnp.sum(gv, axis=-1, keepdims=True) * (1.0 / MAIN_WIDTH)
    xc = gv - mu
    rstd = lax.rsqrt(jnp.sum(xc * xc, axis=-1, keepdims=True) * (1.0 / MAIN_WIDTH) + LN_EPS)
    xhat = xc * rstd
    vln = xhat * ln_g + ln_b
    row = lax.broadcasted_iota(jnp.int32, (CHUNK, CHUNK), 0)
    col = lax.broadcasted_iota(jnp.int32, (CHUNK, CHUNK), 1)
    s_parts = []
    for g in range(A_GROUPS):
        w = jnp.where(col <= row, w_refs[g], jnp.zeros((), bf16))
        s_parts.append(_dot(w, vln[:, g * CHUNK:(g + 1) * CHUNK].astype(bf16)) + bias[:, g:g + 1])
    return gu, tu, tv, rstd, xhat, vln, s_parts


def gmlp_fwd(proj, ws, bs_t, ln_g, ln_b, name, tm=512):
    s = proj.shape[0]

    def body(u_ref, v_ref, w_ref, b_ref, g_ref, bb_ref, o_ref):
        bias = b_ref[...]
        for c in range(tm // CHUNK):
            rows = slice(c * CHUNK, (c + 1) * CHUNK)
            gu, _, _, _, _, _, s_parts = _gmlp_forward_chunk(u_ref[rows, :], v_ref[rows, :], w_ref, bias, g_ref[...], bb_ref[...])
            for g in range(A_GROUPS):
                cols = slice(g * CHUNK, (g + 1) * CHUNK)
                o_ref[rows, cols] = (gu[:, cols] * s_parts[g]).astype(bf16)

    vec = pl.BlockSpec((1, MAIN_WIDTH), lambda i: (0, 0))
    return pl.pallas_call(
        body, grid=(s // tm,),
        in_specs=[pl.BlockSpec((tm, MAIN_WIDTH), lambda i: (i, 0)), pl.BlockSpec((tm, MAIN_WIDTH), lambda i: (i, 1)),
                  pl.BlockSpec((A_GROUPS, CHUNK, CHUNK), lambda i: (0, 0, 0)), pl.BlockSpec((CHUNK, A_GROUPS), lambda i: (0, 0)), vec, vec],
        out_specs=pl.BlockSpec((tm, MAIN_WIDTH), lambda i: (i, 0)), out_shape=SDS((s, MAIN_WIDTH), bf16), name=name,
        compiler_params=_cparams("parallel"))(proj, proj, ws, bs_t, ln_g, ln_b)


def gmlp_bwd(proj, d_mixed, ws, ws_t, bs_t, ln_g, ln_b, name, tm=512):
    s = proj.shape[0]

    def body(u_ref, v_ref, dm_ref, w_ref, wt_ref, b_ref, g_ref, bb_ref, duv_ref, dw_ref, db_ref, dg_ref, dbb_ref):
        @pl.when(pl.program_id(0) == 0)
        def _():
            dw_ref[...] = jnp.zeros_like(dw_ref)
            db_ref[...] = jnp.zeros_like(db_ref)
            dg_ref[...] = jnp.zeros_like(dg_ref)
            dbb_ref[...] = jnp.zeros_like(dbb_ref)

        bias = b_ref[...]
        ln_gv = g_ref[...]
        row = lax.broadcasted_iota(jnp.int32, (CHUNK, CHUNK), 0)
        col = lax.broadcasted_iota(jnp.int32, (CHUNK, CHUNK), 1)
        lane = lax.broadcasted_iota(jnp.int32, (CHUNK, LANES), 1)
        for c in range(tm // CHUNK):
            rows = slice(c * CHUNK, (c + 1) * CHUNK)
            u = u_ref[rows, :]
            v = v_ref[rows, :]
            gu, tu, tv, rstd, xhat, vln, s_parts = _gmlp_forward_chunk(u, v, w_ref, bias, ln_gv, bb_ref[...])
            dm = dm_ref[rows, :]
            d_vln_parts = []
            d_gu_parts = []
            db_acc = jnp.zeros((CHUNK, LANES), f32)
            for g in range(A_GROUPS):
                cols = slice(g * CHUNK, (g + 1) * CHUNK)
                dmg = dm[:, cols]
                d_gu_parts.append(dmg * s_parts[g])
                d_s = dmg * gu[:, cols]
                db_acc = db_acc + jnp.where(lane == g, jnp.sum(d_s, axis=-1, keepdims=True), 0.0)
                d_sb = d_s.astype(bf16)
                dw_ref[g] += jnp.where(col <= row, _dot_nt(d_sb, vln[:, cols].astype(bf16)), 0.0)
                wt = jnp.where(row <= col, wt_ref[g], jnp.zeros((), bf16))
                d_vln_parts.append(_dot(wt, d_sb))
            db_ref[...] += db_acc
            d_vln = jnp.concatenate(d_vln_parts, axis=-1)
            d_gu = jnp.concatenate(d_gu_parts, axis=-1)
            dg_ref[...] += jnp.sum(d_vln * xhat, axis=0, keepdims=True)
            dbb_ref[...] += jnp.sum(d_vln, axis=0, keepdims=True)
            dxh = d_vln * ln_gv
            m1 = jnp.sum(dxh, axis=-1, keepdims=True) * (1.0 / MAIN_WIDTH)
            m2 = jnp.sum(dxh * xhat, axis=-1, keepdims=True) * (1.0 / MAIN_WIDTH)
            d_gv = rstd * (dxh - m1 - xhat * m2)
            duv_ref[rows, :MAIN_WIDTH] = (d_gu * _gelu_grad(u, tu)).astype(bf16)
            duv_ref[rows, MAIN_WIDTH:] = (d_gv * _gelu_grad(v, tv)).astype(bf16)

    vec = pl.BlockSpec((1, MAIN_WIDTH), lambda i: (0, 0))
    wspec = pl.BlockSpec((A_GROUPS, CHUNK, CHUNK), lambda i: (0, 0, 0))
    return pl.pallas_call(
        body, grid=(s // tm,),
        in_specs=[pl.BlockSpec((tm, MAIN_WIDTH), lambda i: (i, 0)), pl.BlockSpec((tm, MAIN_WIDTH), lambda i: (i, 1)),
                  pl.BlockSpec((tm, MAIN_WIDTH), lambda i: (i, 0)), wspec, wspec, pl.BlockSpec((CHUNK, A_GROUPS), lambda i: (0, 0)), vec, vec],
        out_specs=[pl.BlockSpec((tm, 2 * MAIN_WIDTH), lambda i: (i, 0)), wspec, pl.BlockSpec((CHUNK, LANES), lambda i: (0, 0)), vec, vec],
        out_shape=[SDS((s, 2 * MAIN_WIDTH), bf16), SDS((A_GROUPS, CHUNK, CHUNK), f32), SDS((CHUNK, LANES), f32),
                   SDS((1, MAIN_WIDTH), f32), SDS((1, MAIN_WIDTH), f32)],
        name=name, compiler_params=_cparams("arbitrary"))(proj, proj, d_mixed, ws, ws_t, bs_t, ln_g, ln_b)


def _head_mask(width, h):
    lane = lax.broadcasted_iota(jnp.int32, (1, width), 1)
    return (lane >= h * HEAD_DIM) & (lane < (h + 1) * HEAD_DIM)


def mem_attn_fwd(proj, q_block, kv, name, tm=512):
    s = proj.shape[0]
    n_mem = kv.shape[0]

    def body(q_ref, kv_ref, o_ref):
        q = q_ref[...].astype(f32)
        k = kv_ref[:, :MEM_WIDTH].astype(bf16)
        v = kv_ref[:, MEM_WIDTH:].astype(bf16)
        out = jnp.zeros((tm, MEM_WIDTH), f32)
        for h in range(MEM_HEADS):
            msk = _head_mask(MEM_WIDTH, h)
            qh = jnp.where(msk, q, 0.0).astype(bf16)
            sc = _dot_nt(qh, k) * ATT_SCALE
            e = jnp.exp(sc - jnp.max(sc, axis=-1, keepdims=True))
            p = e / jnp.sum(e, axis=-1, keepdims=True)
            out = jnp.where(msk, _dot(p.astype(bf16), v), out)
        o_ref[...] = out.astype(bf16)

    return pl.pallas_call(body, grid=(s // tm,),
                          in_specs=[pl.BlockSpec((tm, MEM_WIDTH), lambda i: (i, q_block)), pl.BlockSpec((n_mem, 2 * MEM_WIDTH), lambda i: (0, 0))],
                          out_specs=pl.BlockSpec((tm, MEM_WIDTH), lambda i: (i, 0)), out_shape=SDS((s, MEM_WIDTH), bf16), name=name,
                          compiler_params=_cparams("parallel"))(proj, kv)


def mem_attn_bwd(proj, q_block, kv, d_mixed, name, tm=512):
    s = proj.shape[0]
    n_mem = kv.shape[0]

    def body(q_ref, kv_ref, do_ref, dq_ref, dkv_ref):
        @pl.when(pl.program_id(0) == 0)
        def _():
            dkv_ref[...] = jnp.zeros_like(dkv_ref)

        q = q_ref[...].astype(f32)
        do = do_ref[...]
        k = kv_ref[:, :MEM_WIDTH].astype(bf16)
        v = kv_ref[:, MEM_WIDTH:].astype(bf16)
        dq = jnp.zeros((tm, MEM_WIDTH), f32)
        dk = jnp.zeros((n_mem, MEM_WIDTH), f32)
        dv = jnp.zeros((n_mem, MEM_WIDTH), f32)
        for h in range(MEM_HEADS):
            msk = _head_mask(MEM_WIDTH, h)
            qh = jnp.where(msk, q, 0.0).astype(bf16)
            doh = jnp.where(msk, do, 0.0).astype(bf16)
            sc = _dot_nt(qh, k) * ATT_SCALE
            e = jnp.exp(sc - jnp.max(sc, axis=-1, keepdims=True))
            p = e / jnp.sum(e, axis=-1, keepdims=True)
            dp = _dot_nt(doh, v)
            ds = p * (dp - jnp.sum(dp * p, axis=-1, keepdims=True))
            dsb = (ds * ATT_SCALE).astype(bf16)
            dq = jnp.where(msk, _dot(dsb, k), dq)
            dk = dk + _dot_tn(dsb, qh)
            dv = dv + _dot_tn(p.astype(bf16), doh)
        dq_ref[...] = dq.astype(bf16)
        dkv_ref[:, :MEM_WIDTH] += dk
        dkv_ref[:, MEM_WIDTH:] += dv

    return pl.pallas_call(
        body, grid=(s // tm,),
        in_specs=[pl.BlockSpec((tm, MEM_WIDTH), lambda i: (i, q_block)), pl.BlockSpec((n_mem, 2 * MEM_WIDTH), lambda i: (0, 0)),
                  pl.BlockSpec((tm, MEM_WIDTH), lambda i: (i, MAIN_WIDTH // MEM_WIDTH))],
        out_specs=[pl.BlockSpec((tm, MEM_WIDTH), lambda i: (i, 0)), pl.BlockSpec((n_mem, 2 * MEM_WIDTH), lambda i: (0, 0))],
        out_shape=[SDS((s, MEM_WIDTH), bf16), SDS((n_mem, 2 * MEM_WIDTH), f32)], name=name,
        compiler_params=_cparams("arbitrary"))(proj, kv, d_mixed)


def _tri(t, upper):
    r = lax.broadcasted_iota(jnp.int32, (t, t), 0)
    c = lax.broadcasted_iota(jnp.int32, (t, t), 1)
    return ((r <= c) if upper else (r >= c)).astype(f32)


def fgate_fwd(z_t, b, name, t=512):
    hh, s = z_t.shape

    def body(z_ref, b_ref, c_ref):
        u = _tri(t, True)
        carry = jnp.zeros((hh, 1), f32)
        for blk in range(s // t):
            x = z_ref[:, blk * t:(blk + 1) * t] + b_ref[...]
            logf = jnp.minimum(x, 0.0) - jnp.log(1.0 + jnp.exp(-jnp.abs(x)))
            y = jnp.dot(logf, u, precision=lax.Precision.HIGHEST, preferred_element_type=f32) + carry
            c_ref[:, blk * t:(blk + 1) * t] = y
            carry = y[:, t - 1:t]

    return pl.pallas_call(body, out_shape=SDS((hh, s), f32), name=name, compiler_params=_cparams())(z_t, b)


def fgate_bwd(dc_t, z_t, b, name, t=512):
    hh, s = z_t.shape

    def body(dc_ref, z_ref, b_ref, dz_ref, db_ref):
        low = _tri(t, False)
        carry = jnp.zeros((hh, 1), f32)
        total = jnp.zeros((hh, 1), f32)
        for blk in reversed(range(s // t)):
            cols = slice(blk * t, (blk + 1) * t)
            y = jnp.dot(dc_ref[:, cols], low, precision=lax.Precision.HIGHEST, preferred_element_type=f32) + carry
            carry = y[:, 0:1]
            dz = y * _sigmoid(-(z_ref[:, cols] + b_ref[...]))
            dz_ref[:, cols] = dz
            total = total + jnp.sum(dz, axis=-1, keepdims=True)
        db_ref[...] = jnp.broadcast_to(total, db_ref.shape)

    return pl.pallas_call(body, out_shape=[SDS((hh, s), f32), SDS((hh, LANES), f32)], name=name,
                          compiler_params=_cparams())(dc_t, z_t, b)


def _pair_masks():
    lane = lax.broadcasted_iota(jnp.int32, (1, LANES), 1)
    return [lane < HEAD_DIM, lane >= HEAD_DIM]


def fox_fwd(q, kv, c_col, c_row, name, tq=512):
    s = kv.shape[0]
    nq = s // tq

    def body(q_ref, k_ref, v_ref, cc_ref, cr_ref, o_ref, lse_ref):
        i = pl.program_id(1)
        qv = q_ref[...]
        masks = _pair_masks()
        row = lax.broadcasted_iota(jnp.int32, (tq, tq), 0)
        col = lax.broadcasted_iota(jnp.int32, (tq, tq), 1)
        qh = [jnp.where(masks[hh], qv, jnp.zeros((), bf16)) * ATT_SCALE for hh in range(2)]
        ct = [cc_ref[:, hh:hh + 1] for hh in range(2)]

        def block(j, carry, diag):
            lo = pl.multiple_of(j * tq, tq)
            ks = k_ref[pl.ds(lo, tq), :]
            vs = v_ref[pl.ds(lo, tq), :]
            out = []
            for hh in range(2):
                m, l, acc = carry[hh]
                sc = _dot_nt(qh[hh], ks) + (ct[hh] - cr_ref[hh:hh + 1, pl.ds(lo, tq)])
                if diag:
                    sc = jnp.where(col <= row, sc, -jnp.inf)
                m_new = jnp.maximum(m, jnp.max(sc, axis=-1, keepdims=True))
                alpha = jnp.exp(m - m_new)
                p = jnp.exp(sc - m_new)
                l = alpha * l + jnp.sum(p, axis=-1, keepdims=True)
                p_hi = p.astype(bf16)
                p_lo = (p - p_hi.astype(f32)).astype(bf16)
                acc = alpha * acc + (_dot(p_hi, vs) + _dot(p_lo, vs))
                out.append((m_new, l, acc))
            return tuple(out)

        init = (jnp.full((tq, 1), -jnp.inf, f32), jnp.zeros((tq, 1), f32), jnp.zeros((tq, LANES), f32))
        carry = lax.fori_loop(0, i, functools.partial(block, diag=False), (init, init))
        res = [(acc / l, m + jnp.log(l)) for m, l, acc in block(i, carry, True)]
        o_ref[...] = jnp.where(masks[0], res[0][0], res[1][0])
        lse_ref[...] = jnp.where(masks[0], res[0][1], res[1][1])

    return pl.pallas_call(
        body, grid=(FOX_PAIRS, nq),
        in_specs=[pl.BlockSpec((tq, LANES), lambda p, i: (i, p)), pl.BlockSpec((s, LANES), lambda p, i: (0, p)),
                  pl.BlockSpec((s, LANES), lambda p, i: (0, FOX_PAIRS + p)), pl.BlockSpec((None, tq, 2), lambda p, i: (p, i, 0)),
                  pl.BlockSpec((None, 2, s), lambda p, i: (p, 0, 0))],
        out_specs=[pl.BlockSpec((tq, LANES), lambda p, i: (i, p)), pl.BlockSpec((None, tq, LANES), lambda p, i: (p, i, 0))],
        out_shape=[SDS((s, MAIN_WIDTH), f32), SDS((FOX_PAIRS, s, LANES), f32)], name=name,
        compiler_params=_cparams("parallel", "parallel"))(q, kv, kv, c_col, c_row)


def fox_bwd(q, kv, d_mixed, o, lse, c_col, c_row, name, tq=512):
    s = kv.shape[0]
    nq = s // tq

    def body(q_ref, k_ref, v_ref, do_ref, o_ref, lse_ref, cc_ref, cr_ref, dq_ref, dk_ref, dv_ref, dc_ref):
        j = pl.program_id(1)

        @pl.when(j == 0)
        def _():
            dq_ref[...] = jnp.zeros_like(dq_ref)

        masks = _pair_masks()
        row = lax.broadcasted_iota(jnp.int32, (tq, tq), 0)
        col = lax.broadcasted_iota(jnp.int32, (tq, tq), 1)
        kj = k_ref[...]
        vj = v_ref[...]
        lo_j = pl.multiple_of(j * tq, tq)

        def block(i, carry, diag):
            dk, dv, dc0, dc1 = carry
            dcs = [dc0, dc1]
            lo = pl.multiple_of(i * tq, tq)
            qi = q_ref[pl.ds(lo, tq), :]
            doi = do_ref[pl.ds(lo, tq), :]
            prod = doi.astype(bf16).astype(f32) * o_ref[pl.ds(lo, tq), :]
            lse_i = lse_ref[pl.ds(lo, tq), :]
            cc_i = cc_ref[pl.ds(lo, tq), :]
            dq_i = jnp.zeros((tq, LANES), f32)
            for hh in range(2):
                qh = jnp.where(masks[hh], qi, jnp.zeros((), bf16))
                doh = jnp.where(masks[hh], doi, 0.0).astype(bf16)
                delta = jnp.sum(jnp.where(masks[hh], prod, 0.0), axis=-1, keepdims=True)
                sc = _dot_nt(qh, kj) * ATT_SCALE + (cc_i[:, hh:hh + 1] - cr_ref[hh:hh + 1, pl.ds(lo_j, tq)])
                p = jnp.exp(sc - lse_i[:, hh * HEAD_DIM:hh * HEAD_DIM + 1])
                if diag:
                    p = jnp.where(col <= row, p, 0.0)
                dv = dv + _dot_tn(p.astype(bf16), doh)
                ds = p * (_dot_nt(doh, vj) - delta)
                dcs[hh] = dcs[hh] + jnp.sum(ds, axis=0, keepdims=True)
                dsb = (ds * ATT_SCALE).astype(bf16)
                dq_i = jnp.where(masks[hh], _dot(dsb, kj), dq_i)
                dk = dk + _dot_tn(dsb, qh)
            dq_ref[pl.ds(lo, tq), :] += dq_i
            return dk, dv, dcs[0], dcs[1]

        zero = jnp.zeros((tq, LANES), f32)
        zrow = jnp.zeros((1, tq), f32)
        carry = block(j, (zero, zero, zrow, zrow), True)
        dk, dv, dc0, dc1 = lax.fori_loop(j + 1, nq, functools.partial(block, diag=False), carry)
        dk_ref[...] = dk.astype(bf16)
        dv_ref[...] = dv.astype(bf16)
        dc_ref[0:1, :] = -dc0
        dc_ref[1:2, :] = -dc1

    full = lambda p, j: (0, p)
    tile = lambda p, j: (j, p)
    return pl.pallas_call(
        body, grid=(FOX_PAIRS, nq),
        in_specs=[pl.BlockSpec((s, LANES), full), pl.BlockSpec((tq, LANES), tile), pl.BlockSpec((tq, LANES), lambda p, j: (j, FOX_PAIRS + p)),
                  pl.BlockSpec((s, LANES), full), pl.BlockSpec((s, LANES), full), pl.BlockSpec((None, s, LANES), lambda p, j: (p, 0, 0)),
                  pl.BlockSpec((None, s, 2), lambda p, j: (p, 0, 0)), pl.BlockSpec((None, 2, s), lambda p, j: (p, 0, 0))],
        out_specs=[pl.BlockSpec((s, LANES), full), pl.BlockSpec((tq, LANES), tile), pl.BlockSpec((tq, LANES), tile),
                   pl.BlockSpec((None, 2, tq), lambda p, j: (p, 0, j))],
        out_shape=[SDS((s, MAIN_WIDTH), f32), SDS((s, MAIN_WIDTH), bf16), SDS((s, MAIN_WIDTH), bf16), SDS((FOX_PAIRS, 2, s), f32)],
        name=name, compiler_params=_cparams("parallel", "arbitrary"))(q, kv, kv, d_mixed, o, lse, c_col, c_row)


def adamw(w, g, m, v, name, tr=256):
    r, c = w.shape
    tr = min(tr, r)
    assert r % tr == 0, (name, r, tr)
    c1 = 1.0 / (1.0 - ADAM_B1 ** ADAM_STEP)
    c2 = 1.0 / (1.0 - ADAM_B2 ** ADAM_STEP)

    def body(w_ref, g_ref, m_ref, v_ref, d_ref, mo_ref, vo_ref):
        gv = g_ref[...]
        mn = ADAM_B1 * m_ref[...] + (1.0 - ADAM_B1) * gv
        vn = ADAM_B2 * v_ref[...] + (1.0 - ADAM_B2) * gv * gv
        mo_ref[...] = mn
        vo_ref[...] = vn
        d_ref[...] = -ADAM_LR * ((mn * c1) / (jnp.sqrt(vn * c2) + ADAM_EPS) + ADAM_WD * w_ref[...])

    spec = pl.BlockSpec((tr, c), lambda i: (i, 0))
    return pl.pallas_call(body, grid=(r // tr,), in_specs=[spec] * 4, out_specs=[spec] * 3, out_shape=[SDS((r, c), f32)] * 3,
                          name=name, compiler_params=_cparams("parallel"))(w, g, m, v)


def sum_leading(x, name, out_dtype=f32, tr=None):
    n, r, c = x.shape
    tr = tr or r
    assert r % tr == 0

    def body(x_ref, o_ref):
        acc = x_ref[0].astype(f32)
        for k in range(1, n):
            acc = acc + x_ref[k].astype(f32)
        o_ref[...] = acc.astype(out_dtype)

    return pl.pallas_call(body, grid=(r // tr,), in_specs=[pl.BlockSpec((n, tr, c), lambda i: (0, i, 0))],
                          out_specs=pl.BlockSpec((tr, c), lambda i: (i, 0)), out_shape=SDS((r, c), out_dtype), name=name,
                          compiler_params=_cparams("parallel"))(x)


_ANY = pl.BlockSpec(memory_space=pl.ANY)
_DMA = pltpu.SemaphoreType.DMA


_HBM = pl.BlockSpec(memory_space=pltpu.HBM)
_SEM = pl.BlockSpec(memory_space=pltpu.SEMAPHORE)
_EFFECT = pltpu.SideEffectType.DATAFLOW_SIDE_EFFECTING
_FLIPS = [(0, 0, 1), (1, 0, 0), (0, 1, 0), (1, 1, 0), (1, 0, 1), (0, 1, 1), (1, 1, 1)]


def _me():
    return lax.axis_index("x"), lax.axis_index("y"), lax.axis_index("c")


def _peers():
    mx, my, mc = _me()
    return [(jnp.bitwise_xor(mx, fx), jnp.bitwise_xor(my, fy), jnp.bitwise_xor(mc, fc)) for fx, fy, fc in _FLIPS]


def _index(dev):
    return 4 * dev[0] + 2 * dev[1] + dev[2]


def _win(ref, axis, k, size, count=1):
    idx = [slice(None)] * len(ref.shape)
    idx[axis] = pl.ds(k * size, count * size)
    return ref.at[tuple(idx)]


def _hbm(a):
    return pltpu.with_memory_space_constraint(a, pltpu.HBM)


def _exchange_start(srcs, lands, copies_of, name):
    n = len(srcs)

    def body(*refs):
        src = refs[:n]
        send_sems, recv_sems, self_sems = refs[2 * n:2 * n + 3]
        land = refs[3 * n + 3:4 * n + 3]
        token = refs[4 * n + 3]
        me = _index(_me())
        for a in range(n):
            for s_ref, d_ref, peer in copies_of(a, src[a], land[a], me):
                if peer is None:
                    pltpu.make_async_copy(s_ref, d_ref, self_sems.at[a]).start()
                else:
                    pltpu.make_async_remote_copy(src_ref=s_ref, dst_ref=d_ref, send_sem=send_sems.at[a], recv_sem=recv_sems.at[a],
                                                 device_id=peer, device_id_type=MESH).start()
        token[...] = jnp.zeros_like(token)

    outs = pl.pallas_call(
        body, name=name,
        out_shape=(_DMA((n,)), _DMA((n,)), _DMA((n,)), *[pltpu.HBM(s.shape, s.dtype) for s in srcs],
                   *[pltpu.HBM(l.shape, l.dtype) for l in lands], SDS((8, LANES), f32)),
        in_specs=[_HBM] * (2 * n), out_specs=(_SEM, _SEM, _SEM, *[_HBM] * (2 * n), pl.BlockSpec(memory_space=pltpu.VMEM)),
        input_output_aliases={i: 3 + i for i in range(2 * n)},
        compiler_params=pltpu.CompilerParams(has_side_effects=_EFFECT),
    )(*[_hbm(s) for s in srcs], *[_hbm(lax.empty(l.shape, l.dtype)) for l in lands])
    return dict(sems=outs[:3], srcs=list(outs[3:3 + n]), lands=list(outs[3 + n:3 + 2 * n]), token=outs[3 + 2 * n])


def _exchange_wait(started, waits_of, after, name):
    srcs, lands = started["srcs"], started["lands"]
    n = len(srcs)

    def body(*refs):
        src = refs[:n]
        land = refs[n:2 * n]
        send_sems, recv_sems, self_sems = refs[2 * n:2 * n + 3]
        me = _index(_me())
        for a in range(n):
            seven, (s_ref, d_ref) = waits_of(a, src[a], land[a], me)
            both = pltpu.make_async_remote_copy(src_ref=seven, dst_ref=seven, send_sem=send_sems.at[a], recv_sem=recv_sems.at[a],
                                                device_id=_me(), device_id_type=MESH)
            both.wait_send()
            both.wait_recv()
            pltpu.make_async_copy(s_ref, d_ref, self_sems.at[a]).wait()

    outs = pl.pallas_call(
        body, name=name, out_shape=tuple(pltpu.HBM(t.shape, t.dtype) for t in srcs + lands),
        in_specs=[_HBM] * (2 * n) + [_SEM] * 3 + [_ANY], out_specs=tuple([_HBM] * (2 * n)),
        input_output_aliases={i: i for i in range(2 * n)},
        compiler_params=pltpu.CompilerParams(has_side_effects=_EFFECT),
    )(*srcs, *lands, *started["sems"], after)
    return list(outs[n:])


def gather_start(locs, axes, name):
    lands = [SDS(tuple(N_DEV * d if i == ax else d for i, d in enumerate(l.shape)), l.dtype) for l, ax in zip(locs, axes)]

    def copies_of(a, src, land, me):
        mine = _win(land, axes[a], me, src.shape[axes[a]])
        return [(src, mine, peer) for peer in _peers()] + [(src, mine, None)]

    return _exchange_start(locs, lands, copies_of, name)


def gather_wait(started, axes, after, name):
    def waits_of(a, src, land, me):
        size = src.shape[axes[a]]
        return _win(land, axes[a], 0, size, N_DEV - 1), (src, _win(land, axes[a], me, size))

    return _exchange_wait(started, waits_of, after, name)


def scatter_start(grads, axes, name):
    lands = [SDS((N_DEV,) + tuple(d // N_DEV if i == ax else d for i, d in enumerate(g.shape)), g.dtype) for g, ax in zip(grads, axes)]

    def copies_of(a, src, land, me):
        size = src.shape[axes[a]] // N_DEV
        out = [(_win(src, axes[a], _index(peer), size), land.at[me], peer) for peer in _peers()]
        return out + [(_win(src, axes[a], me, size), land.at[me], None)]

    return _exchange_start(grads, lands, copies_of, name)


def scatter_wait(started, axes, after, name):
    def waits_of(a, src, land, me):
        size = src.shape[axes[a]] // N_DEV
        return land.at[pl.ds(0, N_DEV - 1)], (_win(src, axes[a], me, size), land.at[me])

    return _exchange_wait(started, waits_of, after, name)


def _row_tile(rows, cap=512):
    return max(t for t in range(8, min(rows, cap) + 1, 8) if rows % t == 0)


_SMALL = [
    ("ln_mix_pre", (2, 1024)), ("ln_mix_post", (2, 1024)), ("ln_ffn_pre", (2, 1024)), ("ln_ffn_post", (2, 1024)),
    ("ln_mem", (2, 1024)), ("w_spatial", (1, 6, 128, 128)), ("b_spatial", (1, 6, 128)), ("ln_shared", (1024,)),
    ("b_forget", (12,)), ("ln_v_g", (1, 768)), ("ln_v_b", (1, 768)),
]
_SMALL_TILE = 8 * LANES


def _small_rows(shape):
    return -(-math.prod(shape) // _SMALL_TILE) * 8


def _pack_small(vals, shapes):
    parts = []
    for name, shape in shapes:
        flat = vals[name].reshape(-1).astype(f32)
        rows = _small_rows(shape)
        parts.append(jnp.pad(flat, (0, rows * LANES - flat.shape[0])).reshape(rows, LANES))
    return jnp.concatenate(parts, axis=0)


def _unpack_small(buf, shapes):
    out = {}
    lo = 0
    for name, shape in shapes:
        rows = _small_rows(shape)
        out[name] = buf[lo:lo + rows].reshape(-1)[:math.prod(shape)].reshape(shape)
        lo += rows
    return out


def kernel(x, mem, ln_mix_pre, ln_mix_post, ln_ffn_pre, ln_ffn_post, ln_mem, w_mem_kv, w_out, w_ffn_gate, w_ffn_up, w_ffn_down, w_in_a, w_spatial, b_spatial, ln_v_g, ln_v_b, ln_shared, w_shared_kv, b_forget, w_in_b, loss_target, m_ln_mix_pre, m_ln_mix_post, m_ln_ffn_pre, m_ln_ffn_post, m_ln_mem, m_w_mem_kv, m_w_out, m_w_ffn_gate, m_w_ffn_up, m_w_ffn_down, m_w_in_a, m_w_spatial, m_b_spatial, m_ln_v_g, m_ln_v_b, m_ln_shared, m_w_shared_kv, m_b_forget, m_w_in_b, v_ln_mix_pre, v_ln_mix_post, v_ln_ffn_pre, v_ln_ffn_post, v_ln_mem, v_w_mem_kv, v_w_out, v_w_ffn_gate, v_w_ffn_up, v_w_ffn_down, v_w_in_a, v_w_spatial, v_b_spatial, v_ln_v_g, v_ln_v_b, v_ln_shared, v_w_shared_kv, v_b_forget, v_w_in_b):
    weights = dict(ln_mix_pre=ln_mix_pre, ln_mix_post=ln_mix_post, ln_ffn_pre=ln_ffn_pre, ln_ffn_post=ln_ffn_post, ln_mem=ln_mem,
                   w_mem_kv=w_mem_kv, w_out=w_out, w_ffn_gate=w_ffn_gate, w_ffn_up=w_ffn_up, w_ffn_down=w_ffn_down, w_in_a=w_in_a,
                   w_spatial=w_spatial, b_spatial=b_spatial, ln_v_g=ln_v_g, ln_v_b=ln_v_b, ln_shared=ln_shared,
                   w_shared_kv=w_shared_kv, b_forget=b_forget, w_in_b=w_in_b)
    mom_m = dict(ln_mix_pre=m_ln_mix_pre, ln_mix_post=m_ln_mix_post, ln_ffn_pre=m_ln_ffn_pre, ln_ffn_post=m_ln_ffn_post, ln_mem=m_ln_mem,
                 w_mem_kv=m_w_mem_kv, w_out=m_w_out, w_ffn_gate=m_w_ffn_gate, w_ffn_up=m_w_ffn_up, w_ffn_down=m_w_ffn_down, w_in_a=m_w_in_a,
                 w_spatial=m_w_spatial, b_spatial=m_b_spatial, ln_v_g=m_ln_v_g, ln_v_b=m_ln_v_b, ln_shared=m_ln_shared,
                 w_shared_kv=m_w_shared_kv, b_forget=m_b_forget, w_in_b=m_w_in_b)
    mom_v = dict(ln_mix_pre=v_ln_mix_pre, ln_mix_post=v_ln_mix_post, ln_ffn_pre=v_ln_ffn_pre, ln_ffn_post=v_ln_ffn_post, ln_mem=v_ln_mem,
                 w_mem_kv=v_w_mem_kv, w_out=v_w_out, w_ffn_gate=v_w_ffn_gate, w_ffn_up=v_w_ffn_up, w_ffn_down=v_w_ffn_down, w_in_a=v_w_in_a,
                 w_spatial=v_w_spatial, b_spatial=v_b_spatial, ln_v_g=v_ln_v_g, ln_v_b=v_ln_v_b, ln_shared=v_ln_shared,
                 w_shared_kv=v_w_shared_kv, b_forget=v_b_forget, w_in_b=v_w_in_b)
    names = list(weights)
    mx, my, mc = lax.axis_index("x"), lax.axis_index("y"), lax.axis_index("c")
    me = 4 * mx + 2 * my + mc

    h0 = x[0]
    mem0 = mem[0]
    tgt = loss_target[0]
    seq = h0.shape[0]

    vec = lambda a: a.reshape(1, -1)
    pad_to = lambda a, axis, size: jnp.pad(a, [(0, size - a.shape[i] if i == axis else 0) for i in range(a.ndim)])

    def after(tok, a):
        return a + tok[0, 0].astype(a.dtype)

    lnv_loc = pad_to(jnp.concatenate([ln_v_g, ln_v_b], axis=0), 0, 8)
    st_a = gather_start([w_in_a.astype(bf16), pad_to(lnv_loc, 1, LANES)[None]], [0, 0], "gather_a_start")
    mix_locs = lambda l, tok: [after(tok, w_mem_kv[l]).astype(bf16), w_out[l].astype(bf16)]
    def ffn_gather_start(l, tok):
        gate_up = gather_start([pad_to(after(tok, w_ffn_gate[l]).astype(bf16), 1, FF_SHARD_PAD),
                                pad_to(w_ffn_up[l].astype(bf16), 1, FF_SHARD_PAD)], [1, 1], f"gather_gate_up{l}_start")
        down = gather_start([pad_to(after(gate_up["token"], w_ffn_down[l]).astype(bf16), 0, FF_SHARD_PAD)], [0], f"gather_down{l}_start")
        return gate_up, down

    st_b = [gather_start(mix_locs(0, st_a["token"]), [0, 0], "gather_b0_start"), None]
    st_c = ffn_gather_start(0, st_b[0]["token"])
    st_d = gather_start([after(st_c[1]["token"], w_in_b[0]).astype(bf16), pad_to(w_shared_kv.astype(bf16), 1, KV_PAD)], [0, 0],
                        "gather_d_start")
    st_b[1] = gather_start(mix_locs(1, st_d["token"]), [0, 0], "gather_b1_start")
    st_e = ffn_gather_start(1, st_b[1]["token"])
    ws = w_spatial[0].astype(bf16)
    ws_t = ws.transpose(0, 2, 1)
    bs_t = b_spatial[0].T

    (a0,) = rms_fwd(h0, [after(st_e[1]["token"], vec(ln_mix_pre[0]))], "a0_norm")
    w_in_a8, lnv8 = gather_wait(st_a, [0, 0], a0, "gather_a_wait")
    w_in_a_full = w_in_a8.transpose(1, 0, 2).reshape(D_MODEL, -1)
    lnv_g = lnv8[:, 0, :MAIN_WIDTH // N_DEV].reshape(1, MAIN_WIDTH)
    lnv_b = lnv8[:, 1, :MAIN_WIDTH // N_DEV].reshape(1, MAIN_WIDTH)
    proj0 = mm(a0, w_in_a_full, "proj0", tn=896)
    main0 = gmlp_fwd(proj0, ws, bs_t, lnv_g, lnv_b, "gmlp_fwd")
    w_mkv, w_o = [None, None], [None, None]
    w_mkv[0], w_o[0] = gather_wait(st_b[0], [0, 0], main0, "gather_b0_wait")
    (memn0,) = rms_fwd(mem0, [vec(ln_mem[0])], "mem0_norm")
    kvm0 = mm(memn0, w_mkv[0], "kvm0")
    om0 = mem_attn_fwd(proj0, 2 * MAIN_WIDTH // MEM_WIDTH, kvm0, "mem_attn0")
    mixed0 = jnp.concatenate([main0, om0], axis=-1)
    y1_0, hmid0, f0 = mm_resnorm(mixed0, w_o[0], h0, vec(ln_mix_post[0]), [vec(ln_ffn_pre[0])], "mix_out0")
    w_g0, w_u0 = gather_wait(st_c[0], [1, 1], f0, "gather_gate_up0_wait")
    gu0, act0 = ffn_up(f0, w_g0, w_u0, "ffn_up0")
    (w_d0,) = gather_wait(st_c[1], [0], act0, "gather_down0_wait")
    y2_0, h1, a1, sin1 = mm_resnorm(act0, w_d0, hmid0, vec(ln_ffn_post[0]), [vec(ln_mix_pre[1]), vec(ln_shared)], "ffn_down0")

    w_inb, w_kv = gather_wait(st_d, [0, 0], sin1, "gather_d_wait")
    kvb = mm(sin1, w_kv, "kv_shared", out_dtype=bf16, tn=MAIN_WIDTH, ncols=2 * MAIN_WIDTH)
    zf = mm(sin1, w_kv, "forget_logits", tn=256, col0=2 * MAIN_WIDTH, ncols=256)
    qb = mm(a1, w_inb, "proj1", out_dtype=bf16)
    z_t = jnp.pad(zf[:, :FOX_HEADS].T, ((0, 16 - FOX_HEADS), (0, 0)))
    bf_col = jnp.pad(b_forget, (0, 16 - FOX_HEADS)).reshape(16, 1)
    c_t = fgate_fwd(z_t, bf_col, "fgate_fwd")
    c_row = c_t[:FOX_HEADS].reshape(FOX_PAIRS, 2, seq)
    c_col = c_row.transpose(0, 2, 1)
    main1, lse = fox_fwd(qb, kvb, c_col, c_row, "fox_fwd")
    w_mkv[1], w_o[1] = gather_wait(st_b[1], [0, 0], main1, "gather_b1_wait")
    (memn1,) = rms_fwd(mem0, [vec(ln_mem[1])], "mem1_norm")
    kvm1 = mm(memn1, w_mkv[1], "kvm1")
    om1 = mem_attn_fwd(qb, MAIN_WIDTH // MEM_WIDTH, kvm1, "mem_attn1")
    mixed1 = jnp.concatenate([main1.astype(bf16), om1], axis=-1)
    y1_1, hmid1, f1 = mm_resnorm(mixed1, w_o[1], h1, vec(ln_mix_post[1]), [vec(ln_ffn_pre[1])], "mix_out1")
    w_g1, w_u1 = gather_wait(st_e[0], [1, 1], f1, "gather_gate_up1_wait")
    gu1, act1 = ffn_up(f1, w_g1, w_u1, "ffn_up1")
    (w_d1,) = gather_wait(st_e[1], [0], act1, "gather_down1_wait")
    y2_1, h2 = mm_resnorm(act1, w_d1, hmid1, vec(ln_ffn_post[1]), [], "ffn_down1")
    dh, loss_tile = loss_grad(h2, tgt, "loss")
    loss = lax.psum(loss_tile[0, 0], AXES)
    ffn_w = [(w_g0, w_u0, w_d0), (w_g1, w_u1, w_d1)]

    small = {}

    def ffn_backward(layer, dh_out, y2, hmid, f, gu, act, y1):
        w_g, w_u, w_d = ffn_w[layer]
        d_y2, dg_fpost = rms_bwd(y2, vec(ln_ffn_post[layer]), dh_out, None, bf16, f"ffn_post_bwd{layer}")
        dw_down = mm_tn(act, d_y2, f"dw_down{layer}", tk=256)
        rs_down = scatter_start([dw_down], [0], f"scatter_down{layer}_start")
        d_g, d_u = ffn_act_grad(d_y2, w_d, gu, f"ffn_act_grad{layer}")
        dw_g = mm_tn(f, d_g, f"dw_gate{layer}", dep=rs_down["token"])
        dw_u = mm_tn(f, d_u, f"dw_up{layer}")
        rs_gate_up = scatter_start([dw_g, dw_u], [1, 1], f"scatter_gate_up{layer}_start")
        dh_mid, d_y1, dg_fpre, dg_mpost = ffn_in_grad(d_g, d_u, w_g, w_u, hmid, dh_out, after(rs_gate_up["token"], vec(ln_ffn_pre[layer])),
                                                      y1, vec(ln_mix_post[layer]), f"ffn_in_grad{layer}")
        return dh_mid, d_y1, dg_fpost, dg_fpre, dg_mpost, (rs_down, rs_gate_up)

    def mix_out_backward(layer, d_y1, mixed):
        dw_out = mm_tn(mixed, d_y1, f"dw_out{layer}")
        d_mixed = mm(d_y1, w_o[layer], f"d_mixed{layer}", trans_b=True)
        return d_mixed, dw_out

    def mem_backward(layer, q_src, q_block, kvm, memn, d_mixed):
        d_qm, d_kvm = mem_attn_bwd(q_src, q_block, kvm, d_mixed, f"mem_attn_bwd{layer}")
        d_kvm_b = d_kvm.astype(bf16)
        dw_mkv = mm_tn(memn, d_kvm_b, f"dw_mem_kv{layer}")
        d_memn = mm(d_kvm_b, w_mkv[layer], f"d_memn{layer}", trans_b=True)
        _, dg_mem = rms_bwd(mem0, vec(ln_mem[layer]), d_memn, None, bf16, f"mem_norm_bwd{layer}")
        return d_qm, dw_mkv, dg_mem


    dh_mid1, d_y1_1, dg_fpost1, dg_fpre1, dg_mpost1, rs_ffn1 = ffn_backward(1, dh, y2_1, hmid1, f1, gu1, act1, y1_1)
    d_mixed1, dw_out1 = mix_out_backward(1, d_y1_1, mixed1)
    d_qm1, dw_mkv1, dg_mem1 = mem_backward(1, qb, MAIN_WIDTH // MEM_WIDTH, kvm1, memn1, d_mixed1)
    rs_mix1 = scatter_start([dw_out1, dw_mkv1], [0, 0], "scatter_mix1_start")
    dq, dk, dv, dc = fox_bwd(qb, kvb, d_mixed1, main1, lse, after(rs_mix1["token"], c_col), c_row, "fox_bwd")
    dc_t = jnp.pad(dc.reshape(FOX_HEADS, seq), ((0, 16 - FOX_HEADS), (0, 0)))
    dz_t, db_f = fgate_bwd(dc_t, z_t, bf_col, "fgate_bwd")
    d_kvf = jnp.concatenate([dk, dv, jnp.pad(dz_t[:FOX_HEADS].T.astype(bf16), ((0, 0), (0, KV_PAD - KV_WIDTH)))], axis=-1)
    d_proj1 = jnp.concatenate([dq.astype(bf16), d_qm1], axis=-1)
    dw_in_b = mm_tn(a1, d_proj1, "dw_in_b")
    dw_kv = mm_tn(sin1, d_kvf, "dw_kv", tn=896)
    rs_2 = scatter_start([dw_in_b, dw_kv], [0, 0], "scatter_shared_start")
    dh1, (dg_pre1, dg_shared) = proj_in_grad([(d_proj1, w_inb, vec(ln_mix_pre[1])), (d_kvf, w_kv, vec(ln_shared))], h1, dh_mid1,
                                             "in_grad1", dep=rs_2["token"])

    dh_mid0, d_y1_0, dg_fpost0, dg_fpre0, dg_mpost0, rs_ffn0 = ffn_backward(0, dh1, y2_0, hmid0, f0, gu0, act0, y1_0)
    d_mixed0, dw_out0 = mix_out_backward(0, d_y1_0, mixed0)
    d_qm0, dw_mkv0, dg_mem0 = mem_backward(0, proj0, 2 * MAIN_WIDTH // MEM_WIDTH, kvm0, memn0, d_mixed0)
    rs_mix0 = scatter_start([dw_out0, dw_mkv0], [0, 0], "scatter_mix0_start")
    d_uv, dw_s, db_s, dg_lnv, db_lnv = gmlp_bwd(proj0, d_mixed0, ws, ws_t, bs_t, after(rs_mix0["token"], lnv_g), lnv_b, "gmlp_bwd")

    small["ln_mix_pre"] = jnp.concatenate([jnp.zeros_like(dg_pre1), dg_pre1], axis=0)
    small["ln_mix_post"] = jnp.concatenate([dg_mpost0, dg_mpost1], axis=0)
    small["ln_ffn_pre"] = jnp.concatenate([dg_fpre0, dg_fpre1], axis=0)
    small["ln_ffn_post"] = jnp.concatenate([dg_fpost0, dg_fpost1], axis=0)
    small["ln_mem"] = jnp.concatenate([dg_mem0, dg_mem1], axis=0)
    small["w_spatial"] = dw_s[None]
    small["b_spatial"] = db_s[:, :A_GROUPS].T[None]
    small["ln_shared"] = dg_shared[0]
    small["b_forget"] = db_f[:FOX_HEADS, 0]
    small["ln_v_g"] = dg_lnv
    small["ln_v_b"] = db_lnv
    st_small = gather_start([_pack_small(small, _SMALL)[None]], [0], "gather_small_grads_start")
    d_proj0 = jnp.concatenate([d_uv, after(st_small["token"], d_qm0)], axis=-1)
    dw_in_a = mm_tn(a0, d_proj0, "dw_in_a", tn=896)
    rs_in_a = scatter_start([dw_in_a.reshape(D_MODEL, N_DEV, -1).transpose(1, 0, 2)], [0], "scatter_in_a_start")
    grad_x, (dg_pre0,) = proj_in_grad([(d_proj0, w_in_a_full, vec(ln_mix_pre[0]))], h0, dh_mid0, "in_grad0", dep=rs_in_a["token"])
    st_last = gather_start([dg_pre0.reshape(1, 8, LANES)], [0], "gather_last_grad_start")

    def owned(started, axes, wait_after, name):
        recv = scatter_wait(started, axes, wait_after, name)
        return [sum_leading(r.reshape((N_DEV, -1, r.shape[-1])), f"{name}_sum{i}", tr=_row_tile(math.prod(r.shape[1:-1])))
                for i, r in enumerate(recv)]

    (g_down1,) = owned(rs_ffn1[0], [0], after(st_last["token"], grad_x[:8, :LANES]), "scatter_down1_wait")
    g_gu1 = owned(rs_ffn1[1], [1, 1], g_down1, "scatter_gate_up1_wait")
    g_mix1 = owned(rs_mix1, [0, 0], g_gu1[0], "scatter_mix1_wait")
    g2 = owned(rs_2, [0, 0], g_mix1[0], "scatter_shared_wait")
    (g_down0,) = owned(rs_ffn0[0], [0], g2[0], "scatter_down0_wait")
    g_gu0 = owned(rs_ffn0[1], [1, 1], g_down0, "scatter_gate_up0_wait")
    g_mix0 = owned(rs_mix0, [0, 0], g_gu0[0], "scatter_mix0_wait")
    (g_in_a,) = owned(rs_in_a, [0], g_mix0[0], "scatter_in_a_wait")
    g_local = dict(
        w_ffn_gate=jnp.stack([g_gu0[0], g_gu1[0]])[:, :, :FF_SHARD], w_ffn_up=jnp.stack([g_gu0[1], g_gu1[1]])[:, :, :FF_SHARD],
        w_ffn_down=jnp.stack([g_down0, g_down1])[:, :FF_SHARD], w_out=jnp.stack([g_mix0[0], g_mix1[0]]),
        w_mem_kv=jnp.stack([g_mix0[1], g_mix1[1]]), w_in_b=g2[0][None], w_shared_kv=g2[1][:, :KV_WIDTH], w_in_a=g_in_a[None])
    (small_all,) = gather_wait(st_small, [0], g_in_a, "gather_small_grads_wait")
    (last_all,) = gather_wait(st_last, [0], small_all, "gather_last_grad_wait")
    g_small = _unpack_small(sum_leading(small_all, "sum_small_grads"), _SMALL)
    g_small["ln_mix_pre"] = jnp.concatenate([sum_leading(last_all, "sum_last_grad").reshape(1, D_MODEL), g_small["ln_mix_pre"][1:]], axis=0)
    shard = MAIN_WIDTH // N_DEV
    for n in ("ln_v_g", "ln_v_b"):
        g_small[n] = lax.dynamic_slice_in_dim(g_small[n], me * shard, shard, axis=1)
    grad_w = {**g_small, **g_local}

    delta, new_m, new_v = {}, {}, {}
    for n in g_local:
        two_d = (-1, weights[n].shape[-1])
        d_, m_, v_ = adamw(weights[n].reshape(two_d), grad_w[n].reshape(two_d), mom_m[n].reshape(two_d), mom_v[n].reshape(two_d),
                           f"adamw_{n}", tr=_row_tile(math.prod(weights[n].shape[:-1])))
        delta[n], new_m[n], new_v[n] = (t.reshape(weights[n].shape) for t in (d_, m_, v_))
    small_local_shapes = [(n, tuple(weights[n].shape)) for n, _ in _SMALL]
    packed = [_pack_small(src, small_local_shapes) for src in (weights, grad_w, mom_m, mom_v)]
    outs = adamw(*packed, "adamw_small", tr=packed[0].shape[0])
    for dst, buf in zip((delta, new_m, new_v), outs):
        dst.update(_unpack_small(buf, small_local_shapes))

    return (loss, grad_x[None], *[grad_w[n] for n in names], *[delta[n] for n in names],
            *[new_m[n] for n in names], *[new_v[n] for n in names])
```

```python
import functools
import math

import jax
import jax.numpy as jnp
from jax import lax
from jax.experimental import pallas as pl
from jax.experimental.pallas import tpu as pltpu

f32 = jnp.float32
bf16 = jnp.bfloat16
SDS = jax.ShapeDtypeStruct

D_MODEL = 1024
MAIN_WIDTH = 768
MEM_WIDTH = 256
HEAD_DIM = 64
MEM_HEADS = 4
FOX_HEADS = 12
FOX_PAIRS = FOX_HEADS // 2
CHUNK = 128
A_GROUPS = 6
FF_SHARD = 352
FF_SHARD_PAD = 384
FF_PAD = 8 * FF_SHARD_PAD
KV_WIDTH = 2 * MAIN_WIDTH + FOX_HEADS
KV_PAD = 1792
RMS_EPS = 1e-6
LN_EPS = 1e-5
ATT_SCALE = HEAD_DIM ** -0.5
ADAM_LR, ADAM_B1, ADAM_B2, ADAM_EPS, ADAM_WD, ADAM_STEP = 0.001, 0.9, 0.999, 1e-08, 0.01, 10
N_DEV = 8
AXES = ("x", "y", "c")
MESH = pl.DeviceIdType.MESH
V7X_VMEM_LIMIT = 56 * 1024 * 1024
LANES = 128
FLAT_W = 512
ROW_PAD = 16


def _cparams(*sem):
    return pltpu.CompilerParams(dimension_semantics=sem or None, vmem_limit_bytes=V7X_VMEM_LIMIT)


def _dot(a, b):
    return jnp.dot(a, b, preferred_element_type=f32)


def _dot_nt(a, b):
    return lax.dot_general(a, b, (((1,), (1,)), ((), ())), preferred_element_type=f32)


def _dot_tn(a, b):
    return lax.dot_general(a, b, (((0,), (0,)), ((), ())), preferred_element_type=f32)


def _gelu(x):
    k = math.sqrt(2.0 / math.pi)
    t = jnp.tanh(k * (x + 0.044715 * x * x * x))
    return 0.5 * x * (1.0 + t), t


def _gelu_grad(x, t):
    k = math.sqrt(2.0 / math.pi)
    return 0.5 * (1.0 + t) + 0.5 * x * (1.0 - t * t) * k * (1.0 + 3.0 * 0.044715 * x * x)


def _sigmoid(x):
    return 1.0 / (1.0 + jnp.exp(-x))


def rms_fwd(x, gains, name, tm=512):
    m, d = x.shape
    tm = min(tm, m)
    n = len(gains)

    def body(x_ref, *refs):
        xv = x_ref[...]
        y = xv * lax.rsqrt(jnp.sum(xv * xv, axis=-1, keepdims=True) * (1.0 / d) + RMS_EPS)
        for g_ref, o_ref in zip(refs[:n], refs[n:]):
            o_ref[...] = (y * g_ref[...]).astype(bf16)

    row = pl.BlockSpec((tm, d), lambda i: (i, 0))
    vec = pl.BlockSpec((1, d), lambda i: (0, 0))
    return pl.pallas_call(body, grid=(m // tm,), in_specs=[row] + [vec] * n, out_specs=[row] * n,
                          out_shape=[SDS((m, d), bf16)] * n, name=name, compiler_params=_cparams("parallel"))(x, *gains)


def rms_bwd(x, g, dy, add, out_dtype, name, tm=512):
    m, d = x.shape
    tm = min(tm, m)
    has_add = add is not None

    def body(x_ref, g_ref, dy_ref, *refs):
        dx_ref, dg_ref = refs[-2], refs[-1]
        xv = x_ref[...]
        dyv = dy_ref[...].astype(f32)
        r = lax.rsqrt(jnp.sum(xv * xv, axis=-1, keepdims=True) * (1.0 / d) + RMS_EPS)
        xn = xv * r
        dyg = dyv * g_ref[...]
        dx = r * (dyg - xn * (jnp.sum(dyg * xn, axis=-1, keepdims=True) * (1.0 / d)))
        if has_add:
            dx = dx + refs[0][...]
        dx_ref[...] = dx.astype(out_dtype)

        @pl.when(pl.program_id(0) == 0)
        def _():
            dg_ref[...] = jnp.zeros_like(dg_ref)

        dg_ref[...] += jnp.sum(dyv * xn, axis=0, keepdims=True)

    row = pl.BlockSpec((tm, d), lambda i: (i, 0))
    vec = pl.BlockSpec((1, d), lambda i: (0, 0))
    ins = [x, g, dy] + ([add] if has_add else [])
    return pl.pallas_call(body, grid=(m // tm,), in_specs=[row, vec, row] + ([row] if has_add else []),
                          out_specs=[row, vec], out_shape=[SDS((m, d), out_dtype), SDS((1, d), f32)], name=name,
                          compiler_params=_cparams("arbitrary"))(*ins)


def loss_grad(h, tgt, name, tm=512):
    m, d = h.shape

    def body(h_ref, t_ref, dy_ref, l_ref):
        e = h_ref[...] - t_ref[...]
        dy_ref[...] = e * (1.0 / d)

        @pl.when(pl.program_id(0) == 0)
        def _():
            l_ref[...] = jnp.zeros_like(l_ref)

        part = jnp.sum(jnp.sum(e * e, axis=-1, keepdims=True), axis=0, keepdims=True) * (0.5 / d)
        l_ref[...] += jnp.broadcast_to(part, l_ref.shape)

    row = pl.BlockSpec((tm, d), lambda i: (i, 0))
    return pl.pallas_call(body, grid=(m // tm,), in_specs=[row, row],
                          out_specs=[row, pl.BlockSpec((8, LANES), lambda i: (0, 0))],
                          out_shape=[SDS((m, d), f32), SDS((8, LANES), f32)], name=name,
                          compiler_params=_cparams("arbitrary"))(h, tgt)


def mm(a, b, name, trans_b=False, out_dtype=f32, tm=512, tn=1024, layer=None, col0=0, ncols=None, dep=None):
    m, k = a.shape
    n_all = b.shape[-2] if trans_b else b.shape[-1]
    n = n_all if ncols is None else ncols
    tm, tn = min(tm, m), min(tn, n)
    assert m % tm == 0 and n % tn == 0 and col0 % tn == 0 and not (trans_b and col0), (name, m, n, tm, tn)
    jb = col0 // tn
    lead = () if layer is None else (None,)
    sel = () if layer is None else (layer,)

    def body(a_ref, b_ref, *rest):
        r = _dot_nt(a_ref[...], b_ref[...]) if trans_b else _dot(a_ref[...], b_ref[...])
        rest[-1][...] = r.astype(out_dtype)

    if trans_b:
        b_spec = pl.BlockSpec(lead + (tn, k), lambda j, i: sel + (j, 0))
    else:
        b_spec = pl.BlockSpec(lead + (k, tn), lambda j, i: sel + (0, jb + j))
    deps = [] if dep is None else [dep]
    dep_specs = [pl.BlockSpec((8, LANES), lambda j, i: (0, 0))] * len(deps)
    return pl.pallas_call(body, grid=(n // tn, m // tm), in_specs=[pl.BlockSpec((tm, k), lambda j, i: (i, 0)), b_spec] + dep_specs,
                          out_specs=pl.BlockSpec((tm, tn), lambda j, i: (i, j)), out_shape=SDS((m, n), out_dtype),
                          name=name, compiler_params=_cparams("parallel", "parallel"))(a, b, *deps)


def mm_tn(a, g, name, tk=512, tn=1024, out_dtype=bf16, dep=None):
    s, k = a.shape
    n = g.shape[1]
    tk, tn = min(tk, k), min(tn, n)
    assert k % tk == 0 and n % tn == 0, (name, k, n, tk, tn)

    def body(a_ref, g_ref, *rest):
        rest[-1][...] = _dot_tn(a_ref[...], g_ref[...]).astype(out_dtype)

    deps = [] if dep is None else [dep]
    dep_specs = [pl.BlockSpec((8, LANES), lambda i, j: (0, 0))] * len(deps)
    return pl.pallas_call(body, grid=(k // tk, n // tn),
                          in_specs=[pl.BlockSpec((s, tk), lambda i, j: (0, i)), pl.BlockSpec((s, tn), lambda i, j: (0, j))] + dep_specs,
                          out_specs=pl.BlockSpec((tk, tn), lambda i, j: (i, j)), out_shape=SDS((k, n), out_dtype), name=name,
                          compiler_params=_cparams("parallel", "parallel"))(a, g, *deps)


def _rms(xv):
    return xv * lax.rsqrt(jnp.sum(xv * xv, axis=-1, keepdims=True) * (1.0 / xv.shape[-1]) + RMS_EPS)


def _rms_bwd_math(xv, g, dy):
    d = xv.shape[-1]
    r = lax.rsqrt(jnp.sum(xv * xv, axis=-1, keepdims=True) * (1.0 / d) + RMS_EPS)
    xn = xv * r
    dyg = dy * g
    dx = r * (dyg - xn * (jnp.sum(dyg * xn, axis=-1, keepdims=True) * (1.0 / d)))
    return dx, jnp.sum(dy * xn, axis=0, keepdims=True)


def mm_resnorm(a, b, h, g_post, gains, name, tm=256):
    m, k = a.shape
    d = b.shape[1]
    n = len(gains)

    def body(a_ref, b_ref, h_ref, gp_ref, *refs):
        y = _dot(a_ref[...], b_ref[...])
        refs[n][...] = y
        hn = h_ref[...] + _rms(y) * gp_ref[...]
        refs[n + 1][...] = hn
        if n:
            z = _rms(hn)
            for g_ref, o_ref in zip(refs[:n], refs[n + 2:]):
                o_ref[...] = (z * g_ref[...]).astype(bf16)

    row = pl.BlockSpec((tm, d), lambda i: (i, 0))
    vec = pl.BlockSpec((1, d), lambda i: (0, 0))
    return pl.pallas_call(body, grid=(m // tm,),
                          in_specs=[pl.BlockSpec((tm, k), lambda i: (i, 0)), pl.BlockSpec((k, d), lambda i: (0, 0)), row, vec] + [vec] * n,
                          out_specs=[row] * (n + 2), out_shape=[SDS((m, d), f32)] * 2 + [SDS((m, d), bf16)] * n, name=name,
                          compiler_params=_cparams("parallel"))(a, b, h, g_post, *gains)


def ffn_act_grad(d_y2, w_d, gu, name, tm=512, tn=1536):
    s, d = d_y2.shape
    ff = w_d.shape[0]
    nb = ff // tn

    def body(a_ref, b_ref, g_ref, u_ref, dg_ref, du_ref):
        da = _dot_nt(a_ref[...], b_ref[...])
        gg = g_ref[...].astype(f32)
        sg = _sigmoid(gg)
        dg_ref[...] = (da * u_ref[...].astype(f32) * (sg * (1.0 + gg * (1.0 - sg)))).astype(bf16)
        du_ref[...] = (da * gg * sg).astype(bf16)

    tile = pl.BlockSpec((tm, tn), lambda j, i: (i, j))
    return pl.pallas_call(body, grid=(nb, s // tm),
                          in_specs=[pl.BlockSpec((tm, d), lambda j, i: (i, 0)), pl.BlockSpec((tn, d), lambda j, i: (j, 0)), tile,
                                    pl.BlockSpec((tm, tn), lambda j, i: (i, nb + j))],
                          out_specs=[tile, tile], out_shape=[SDS((s, ff), bf16)] * 2, name=name,
                          compiler_params=_cparams("parallel", "parallel"))(d_y2, w_d, gu, gu)


def ffn_in_grad(d_g, d_u, w_g, w_u, hmid, dh_out, g_pre, y1, g_post, name, tm=256):
    s, ff = d_g.shape
    d = w_g.shape[0]

    def body(dg_ref, du_ref, wg_ref, wu_ref, hm_ref, dho_ref, gpre_ref, y1_ref, gpost_ref, dhm_ref, dy1_ref, dgpre_ref, dgpost_ref):
        @pl.when(pl.program_id(0) == 0)
        def _():
            dgpre_ref[...] = jnp.zeros_like(dgpre_ref)
            dgpost_ref[...] = jnp.zeros_like(dgpost_ref)

        d_f = _dot_nt(dg_ref[...], wg_ref[...]) + _dot_nt(du_ref[...], wu_ref[...])
        dx, dg1 = _rms_bwd_math(hm_ref[...], gpre_ref[...], d_f)
        dh_mid = dho_ref[...] + dx
        dhm_ref[...] = dh_mid
        dgpre_ref[...] += dg1
        dy1, dg2 = _rms_bwd_math(y1_ref[...], gpost_ref[...], dh_mid)
        dy1_ref[...] = dy1.astype(bf16)
        dgpost_ref[...] += dg2

    row = pl.BlockSpec((tm, d), lambda i: (i, 0))
    vec = pl.BlockSpec((1, d), lambda i: (0, 0))
    wide = pl.BlockSpec((tm, ff), lambda i: (i, 0))
    w_spec = pl.BlockSpec((d, ff), lambda i: (0, 0))
    return pl.pallas_call(body, grid=(s // tm,), in_specs=[wide, wide, w_spec, w_spec, row, row, vec, row, vec],
                          out_specs=[row, row, vec, vec], out_shape=[SDS((s, d), f32), SDS((s, d), bf16), SDS((1, d), f32), SDS((1, d), f32)],
                          name=name, compiler_params=_cparams("arbitrary"))(d_g, d_u, w_g, w_u, hmid, dh_out, g_pre, y1, g_post)


def proj_in_grad(pairs, x, add, name, tm=256, dep=None):
    s, d = x.shape
    n = len(pairs)
    deps = [] if dep is None else [dep]

    def body(*refs):
        x_ref, add_ref = refs[3 * n], refs[3 * n + 1]
        outs = refs[3 * n + 2 + len(deps):]

        @pl.when(pl.program_id(0) == 0)
        def _():
            for o in outs[1:]:
                o[...] = jnp.zeros_like(o)

        xv = x_ref[...]
        dx = add_ref[...]
        for i in range(n):
            a_ref, b_ref, g_ref = refs[3 * i:3 * i + 3]
            dxi, dgi = _rms_bwd_math(xv, g_ref[...], _dot_nt(a_ref[...], b_ref[...]))
            dx = dx + dxi
            outs[1 + i][...] += dgi
        outs[0][...] = dx

    row = pl.BlockSpec((tm, d), lambda i: (i, 0))
    vec = pl.BlockSpec((1, d), lambda i: (0, 0))
    in_specs, args = [], []
    for a, b, g in pairs:
        k = a.shape[1]
        in_specs += [pl.BlockSpec((tm, k), lambda i: (i, 0)), pl.BlockSpec((d, k), lambda i: (0, 0)), vec]
        args += [a, b, g]
    in_specs += [row, row] + [pl.BlockSpec((8, LANES), lambda i: (0, 0))] * len(deps)
    out = pl.pallas_call(body, grid=(s // tm,), in_specs=in_specs, out_specs=[row] + [vec] * n,
                         out_shape=[SDS((s, d), f32)] + [SDS((1, d), f32)] * n, name=name,
                         compiler_params=_cparams("arbitrary"))(*args, x, add, *deps)
    return out[0], out[1:]


def ffn_up(f, wg, wu, name, tm=256, tc=256):
    s, d = f.shape
    ff = wg.shape[-1]

    def body(f_ref, wg_ref, wu_ref, gu_ref, act_ref):
        fv = f_ref[...]
        for j in range(ff // tc):
            lo = j * tc
            gg = _dot(fv, wg_ref[:, lo:lo + tc])
            uu = _dot(fv, wu_ref[:, lo:lo + tc])
            gu_ref[:, lo:lo + tc] = gg.astype(bf16)
            gu_ref[:, ff + lo:ff + lo + tc] = uu.astype(bf16)
            act_ref[:, lo:lo + tc] = (gg * _sigmoid(gg) * uu).astype(bf16)

    w_spec = pl.BlockSpec((d, ff), lambda i: (0, 0))
    return pl.pallas_call(body, grid=(s // tm,), in_specs=[pl.BlockSpec((tm, d), lambda i: (i, 0)), w_spec, w_spec],
                          out_specs=[pl.BlockSpec((tm, 2 * ff), lambda i: (i, 0)), pl.BlockSpec((tm, ff), lambda i: (i, 0))],
                          out_shape=[SDS((s, 2 * ff), bf16), SDS((s, ff), bf16)], name=name,
                          compiler_params=_cparams("parallel"))(f, wg, wu)


def _gmlp_forward_chunk(u, v, w_refs, bias, ln_g, ln_b):
    gu, tu = _gelu(u)
    gv, tv = _gelu(v)
    mu = jnp.sum(gv, axis=-1, keepdims=True) * (1.0 / MAIN_WIDTH)
    xc = gv - mu
    rstd = lax.rsqrt(jnp.sum(xc * xc, axis=-1, keepdims=True) * (1.0 / MAIN_WIDTH) + LN_EPS)
    xhat = xc * rstd
    vln = xhat * ln_g + ln_b
    row = lax.broadcasted_iota(jnp.int32, (CHUNK, CHUNK), 0)
    col = lax.broadcasted_iota(jnp.int32, (CHUNK, CHUNK), 1)
    s_parts = []
    for g in range(A_GROUPS):
        w = jnp.where(col <= row, w_refs[g], jnp.zeros((), bf16))
        s_parts.append(_dot(w, vln[:, g * CHUNK:(g + 1) * CHUNK].astype(bf16)) + bias[:, g:g + 1])
    return gu, tu, tv, rstd, xhat, vln, s_parts


def gmlp_fwd(proj, ws, bs_t, ln_g, ln_b, name, tm=512):
    s = proj.shape[0]

    def body(u_ref, v_ref, w_ref, b_ref, g_ref, bb_ref, o_ref):
        bias = b_ref[...]
        for c in range(tm // CHUNK):
            rows = slice(c * CHUNK, (c + 1) * CHUNK)
            gu, _, _, _, _, _, s_parts = _gmlp_forward_chunk(u_ref[rows, :], v_ref[rows, :], w_ref, bias, g_ref[...], bb_ref[...])
            for g in range(A_GROUPS):
                cols = slice(g * CHUNK, (g + 1) * CHUNK)
                o_ref[rows, cols] = (gu[:, cols] * s_parts[g]).astype(bf16)

    vec = pl.BlockSpec((1, MAIN_WIDTH), lambda i: (0, 0))
    return pl.pallas_call(
        body, grid=(s // tm,),
        in_specs=[pl.BlockSpec((tm, MAIN_WIDTH), lambda i: (i, 0)), pl.BlockSpec((tm, MAIN_WIDTH), lambda i: (i, 1)),
                  pl.BlockSpec((A_GROUPS, CHUNK, CHUNK), lambda i: (0, 0, 0)), pl.BlockSpec((CHUNK, A_GROUPS), lambda i: (0, 0)), vec, vec],
        out_specs=pl.BlockSpec((tm, MAIN_WIDTH), lambda i: (i, 0)), out_shape=SDS((s, MAIN_WIDTH), bf16), name=name,
        compiler_params=_cparams("parallel"))(proj, proj, ws, bs_t, ln_g, ln_b)


def gmlp_bwd(proj, d_mixed, ws, ws_t, bs_t, ln_g, ln_b, name, tm=512):
    s = proj.shape[0]

    def body(u_ref, v_ref, dm_ref, w_ref, wt_ref, b_ref, g_ref, bb_ref, duv_ref, dw_ref, db_ref, dg_ref, dbb_ref):
        @pl.when(pl.program_id(0) == 0)
        def _():
            dw_ref[...] = jnp.zeros_like(dw_ref)
            db_ref[...] = jnp.zeros_like(db_ref)
            dg_ref[...] = jnp.zeros_like(dg_ref)
            dbb_ref[...] = jnp.zeros_like(dbb_ref)

        bias = b_ref[...]
        ln_gv = g_ref[...]
        row = lax.broadcasted_iota(jnp.int32, (CHUNK, CHUNK), 0)
        col = lax.broadcasted_iota(jnp.int32, (CHUNK, CHUNK), 1)
        lane = lax.broadcasted_iota(jnp.int32, (CHUNK, LANES), 1)
        for c in range(tm // CHUNK):
            rows = slice(c * CHUNK, (c + 1) * CHUNK)
            u = u_ref[rows, :]
            v = v_ref[rows, :]
            gu, tu, tv, rstd, xhat, vln, s_parts = _gmlp_forward_chunk(u, v, w_ref, bias, ln_gv, bb_ref[...])
            dm = dm_ref[rows, :]
            d_vln_parts = []
            d_gu_parts = []
            db_acc = jnp.zeros((CHUNK, LANES), f32)
            for g in range(A_GROUPS):
                cols = slice(g * CHUNK, (g + 1) * CHUNK)
                dmg = dm[:, cols]
                d_gu_parts.append(dmg * s_parts[g])
                d_s = dmg * gu[:, cols]
                db_acc = db_acc + jnp.where(lane == g, jnp.sum(d_s, axis=-1, keepdims=True), 0.0)
                d_sb = d_s.astype(bf16)
                dw_ref[g] += jnp.where(col <= row, _dot_nt(d_sb, vln[:, cols].astype(bf16)), 0.0)
                wt = jnp.where(row <= col, wt_ref[g], jnp.zeros((), bf16))
                d_vln_parts.append(_dot(wt, d_sb))
            db_ref[...] += db_acc
            d_vln = jnp.concatenate(d_vln_parts, axis=-1)
            d_gu = jnp.concatenate(d_gu_parts, axis=-1)
            dg_ref[...] += jnp.sum(d_vln * xhat, axis=0, keepdims=True)
            dbb_ref[...] += jnp.sum(d_vln, axis=0, keepdims=True)
            dxh = d_vln * ln_gv
            m1 = jnp.sum(dxh, axis=-1, keepdims=True) * (1.0 / MAIN_WIDTH)
            m2 = jnp.sum(dxh * xhat, axis=-1, keepdims=True) * (1.0 / MAIN_WIDTH)
            d_gv = rstd * (dxh - m1 - xhat * m2)
            duv_ref[rows, :MAIN_WIDTH] = (d_gu * _gelu_grad(u, tu)).astype(bf16)
            duv_ref[rows, MAIN_WIDTH:] = (d_gv * _gelu_grad(v, tv)).astype(bf16)

    vec = pl.BlockSpec((1, MAIN_WIDTH), lambda i: (0, 0))
    wspec = pl.BlockSpec((A_GROUPS, CHUNK, CHUNK), lambda i: (0, 0, 0))
    return pl.pallas_call(
        body, grid=(s // tm,),
        in_specs=[pl.BlockSpec((tm, MAIN_WIDTH), lambda i: (i, 0)), pl.BlockSpec((tm, MAIN_WIDTH), lambda i: (i, 1)),
                  pl.BlockSpec((tm, MAIN_WIDTH), lambda i: (i, 0)), wspec, wspec, pl.BlockSpec((CHUNK, A_GROUPS), lambda i: (0, 0)), vec, vec],
        out_specs=[pl.BlockSpec((tm, 2 * MAIN_WIDTH), lambda i: (i, 0)), wspec, pl.BlockSpec((CHUNK, LANES), lambda i: (0, 0)), vec, vec],
        out_shape=[SDS((s, 2 * MAIN_WIDTH), bf16), SDS((A_GROUPS, CHUNK, CHUNK), f32), SDS((CHUNK, LANES), f32),
                   SDS((1, MAIN_WIDTH), f32), SDS((1, MAIN_WIDTH), f32)],
        name=name, compiler_params=_cparams("arbitrary"))(proj, proj, d_mixed, ws, ws_t, bs_t, ln_g, ln_b)


def _head_mask(width, h):
    lane = lax.broadcasted_iota(jnp.int32, (1, width), 1)
    return (lane >= h * HEAD_DIM) & (lane < (h + 1) * HEAD_DIM)


def mem_attn_fwd(proj, q_block, kv, name, tm=512):
    s = proj.shape[0]
    n_mem = kv.shape[0]

    def body(q_ref, kv_ref, o_ref):
        q = q_ref[...].astype(f32)
        k = kv_ref[:, :MEM_WIDTH].astype(bf16)
        v = kv_ref[:, MEM_WIDTH:].astype(bf16)
        out = jnp.zeros((tm, MEM_WIDTH), f32)
        for h in range(MEM_HEADS):
            msk = _head_mask(MEM_WIDTH, h)
            qh = jnp.where(msk, q, 0.0).astype(bf16)
            sc = _dot_nt(qh, k) * ATT_SCALE
            e = jnp.exp(sc - jnp.max(sc, axis=-1, keepdims=True))
            p = e / jnp.sum(e, axis=-1, keepdims=True)
            out = jnp.where(msk, _dot(p.astype(bf16), v), out)
        o_ref[...] = out.astype(bf16)

    return pl.pallas_call(body, grid=(s // tm,),
                          in_specs=[pl.BlockSpec((tm, MEM_WIDTH), lambda i: (i, q_block)), pl.BlockSpec((n_mem, 2 * MEM_WIDTH), lambda i: (0, 0))],
                          out_specs=pl.BlockSpec((tm, MEM_WIDTH), lambda i: (i, 0)), out_shape=SDS((s, MEM_WIDTH), bf16), name=name,
                          compiler_params=_cparams("parallel"))(proj, kv)


def mem_attn_bwd(proj, q_block, kv, d_mixed, name, tm=512):
    s = proj.shape[0]
    n_mem = kv.shape[0]

    def body(q_ref, kv_ref, do_ref, dq_ref, dkv_ref):
        @pl.when(pl.program_id(0) == 0)
        def _():
            dkv_ref[...] = jnp.zeros_like(dkv_ref)

        q = q_ref[...].astype(f32)
        do = do_ref[...]
        k = kv_ref[:, :MEM_WIDTH].astype(bf16)
        v = kv_ref[:, MEM_WIDTH:].astype(bf16)
        dq = jnp.zeros((tm, MEM_WIDTH), f32)
        dk = jnp.zeros((n_mem, MEM_WIDTH), f32)
        dv = jnp.zeros((n_mem, MEM_WIDTH), f32)
        for h in range(MEM_HEADS):
            msk = _head_mask(MEM_WIDTH, h)
            qh = jnp.where(msk, q, 0.0).astype(bf16)
            doh = jnp.where(msk, do, 0.0).astype(bf16)
            sc = _dot_nt(qh, k) * ATT_SCALE
            e = jnp.exp(sc - jnp.max(sc, axis=-1, keepdims=True))
            p = e / jnp.sum(e, axis=-1, keepdims=True)
            dp = _dot_nt(doh, v)
            ds = p * (dp - jnp.sum(dp * p, axis=-1, keepdims=True))
            dsb = (ds * ATT_SCALE).astype(bf16)
            dq = jnp.where(msk, _dot(dsb, k), dq)
            dk = dk + _dot_tn(dsb, qh)
            dv = dv + _dot_tn(p.astype(bf16), doh)
        dq_ref[...] = dq.astype(bf16)
        dkv_ref[:, :MEM_WIDTH] += dk
        dkv_ref[:, MEM_WIDTH:] += dv

    return pl.pallas_call(
        body, grid=(s // tm,),
        in_specs=[pl.BlockSpec((tm, MEM_WIDTH), lambda i: (i, q_block)), pl.BlockSpec((n_mem, 2 * MEM_WIDTH), lambda i: (0, 0)),
                  pl.BlockSpec((tm, MEM_WIDTH), lambda i: (i, MAIN_WIDTH // MEM_WIDTH))],
        out_specs=[pl.BlockSpec((tm, MEM_WIDTH), lambda i: (i, 0)), pl.BlockSpec((n_mem, 2 * MEM_WIDTH), lambda i: (0, 0))],
        out_shape=[SDS((s, MEM_WIDTH), bf16), SDS((n_mem, 2 * MEM_WIDTH), f32)], name=name,
        compiler_params=_cparams("arbitrary"))(proj, kv, d_mixed)


def _tri(t, upper):
    r = lax.broadcasted_iota(jnp.int32, (t, t), 0)
    c = lax.broadcasted_iota(jnp.int32, (t, t), 1)
    return ((r <= c) if upper else (r >= c)).astype(f32)


def fgate_fwd(z_t, b, name, t=512):
    hh, s = z_t.shape

    def body(z_ref, b_ref, c_ref):
        u = _tri(t, True)
        carry = jnp.zeros((hh, 1), f32)
        for blk in range(s // t):
            x = z_ref[:, blk * t:(blk + 1) * t] + b_ref[...]
            logf = jnp.minimum(x, 0.0) - jnp.log(1.0 + jnp.exp(-jnp.abs(x)))
            y = jnp.dot(logf, u, precision=lax.Precision.HIGHEST, preferred_element_type=f32) + carry
            c_ref[:, blk * t:(blk + 1) * t] = y
            carry = y[:, t - 1:t]

    return pl.pallas_call(body, out_shape=SDS((hh, s), f32), name=name, compiler_params=_cparams())(z_t, b)


def fgate_bwd(dc_t, z_t, b, name, t=512):
    hh, s = z_t.shape

    def body(dc_ref, z_ref, b_ref, dz_ref, db_ref):
        low = _tri(t, False)
        carry = jnp.zeros((hh, 1), f32)
        total = jnp.zeros((hh, 1), f32)
        for blk in reversed(range(s // t)):
            cols = slice(blk * t, (blk + 1) * t)
            y = jnp.dot(dc_ref[:, cols], low, precision=lax.Precision.HIGHEST, preferred_element_type=f32) + carry
            carry = y[:, 0:1]
            dz = y * _sigmoid(-(z_ref[:, cols] + b_ref[...]))
            dz_ref[:, cols] = dz
            total = total + jnp.sum(dz, axis=-1, keepdims=True)
        db_ref[...] = jnp.broadcast_to(total, db_ref.shape)

    return pl.pallas_call(body, out_shape=[SDS((hh, s), f32), SDS((hh, LANES), f32)], name=name,
                          compiler_params=_cparams())(dc_t, z_t, b)


def _pair_masks():
    lane = lax.broadcasted_iota(jnp.int32, (1, LANES), 1)
    return [lane < HEAD_DIM, lane >= HEAD_DIM]


def _tile_base(cr_ref, hh, lo):
    return cr_ref[hh:hh + 1, pl.ds(lo, LANES)][:, 0:1]


def fox_fwd(q, kv, c_row, name, tq=512):
    s = kv.shape[0]
    nq = s // tq

    def body(q_ref, k_ref, v_ref, cr_ref, o_ref, lse_ref):
        i = pl.program_id(1)
        qv = q_ref[...]
        masks = _pair_masks()
        row = lax.broadcasted_iota(jnp.int32, (tq, tq), 0)
        col = lax.broadcasted_iota(jnp.int32, (tq, tq), 1)
        qh = [jnp.where(masks[hh], qv, jnp.zeros((), bf16)) * ATT_SCALE for hh in range(2)]
        ct = [_tile_base(cr_ref, hh, pl.multiple_of(i * tq, tq)) for hh in range(2)]

        def block(j, carry, diag):
            lo = pl.multiple_of(j * tq, tq)
            ks = k_ref[pl.ds(lo, tq), :]
            vs = v_ref[pl.ds(lo, tq), :]
            out = []
            for hh in range(2):
                m, l, acc = carry[hh]
                sc = _dot_nt(qh[hh], ks) + (ct[hh] - cr_ref[hh:hh + 1, pl.ds(lo, tq)])
                if diag:
                    sc = jnp.where(col <= row, sc, -jnp.inf)
                m_new = jnp.maximum(m, jnp.max(sc, axis=-1, keepdims=True))
                alpha = jnp.exp(m - m_new)
                p = jnp.exp(sc - m_new)
                l = alpha * l + jnp.sum(p, axis=-1, keepdims=True)
                p_hi = p.astype(bf16)
                p_lo = (p - p_hi.astype(f32)).astype(bf16)
                acc = alpha * acc + (_dot(p_hi, vs) + _dot(p_lo, vs))
                out.append((m_new, l, acc))
            return tuple(out)

        init = (jnp.full((tq, 1), -jnp.inf, f32), jnp.zeros((tq, 1), f32), jnp.zeros((tq, LANES), f32))
        carry = lax.fori_loop(0, i, functools.partial(block, diag=False), (init, init))
        res = [(acc / l, m + jnp.log(l)) for m, l, acc in block(i, carry, True)]
        o_ref[...] = jnp.where(masks[0], res[0][0], res[1][0])
        lse_ref[...] = jnp.where(masks[0], res[0][1], res[1][1])

    return pl.pallas_call(
        body, grid=(FOX_PAIRS, nq),
        in_specs=[pl.BlockSpec((tq, LANES), lambda p, i: (i, p)), pl.BlockSpec((s, LANES), lambda p, i: (0, p)),
                  pl.BlockSpec((s, LANES), lambda p, i: (0, FOX_PAIRS + p)), pl.BlockSpec((None, 2, s), lambda p, i: (p, 0, 0))],
        out_specs=[pl.BlockSpec((tq, LANES), lambda p, i: (i, p)), pl.BlockSpec((None, tq, LANES), lambda p, i: (p, i, 0))],
        out_shape=[SDS((s, MAIN_WIDTH), f32), SDS((FOX_PAIRS, s, LANES), f32)], name=name,
        compiler_params=_cparams("parallel", "parallel"))(q, kv, kv, c_row)


def fox_bwd(q, q_t, kv, d_mixed, do_t, o, lse, c_row, name, tq=512):
    s = kv.shape[0]
    nq = s // tq

    def body(q_ref, qt_ref, k_ref, v_ref, do_ref, dot_ref, o_ref, lse_ref, cr_ref, dq_ref, dk_ref, dv_ref, dc_ref):
        j = pl.program_id(1)

        @pl.when(j == 0)
        def _():
            dq_ref[...] = jnp.zeros_like(dq_ref)

        masks = _pair_masks()
        sub = lax.broadcasted_iota(jnp.int32, (LANES, 1), 0)
        sub_masks = [sub < HEAD_DIM, sub >= HEAD_DIM]
        row = lax.broadcasted_iota(jnp.int32, (tq, tq), 0)
        col = lax.broadcasted_iota(jnp.int32, (tq, tq), 1)
        kj = k_ref[...]
        vj = v_ref[...]
        lo_j = pl.multiple_of(j * tq, tq)

        def block(i, carry, diag):
            dk_t, dv_t, dc0, dc1 = carry
            dcs = [dc0, dc1]
            lo = pl.multiple_of(i * tq, tq)
            qi = q_ref[pl.ds(lo, tq), :]
            qt_i = qt_ref[:, pl.ds(lo, tq)]
            doi = do_ref[pl.ds(lo, tq), :]
            dot_i = dot_ref[:, pl.ds(lo, tq)]
            prod = doi.astype(bf16).astype(f32) * o_ref[pl.ds(lo, tq), :]
            lse_i = lse_ref[pl.ds(lo, tq), :]
            dq_i = jnp.zeros((tq, LANES), f32)
            for hh in range(2):
                qh = jnp.where(masks[hh], qi, jnp.zeros((), bf16))
                doh = jnp.where(masks[hh], doi, 0.0).astype(bf16)
                delta = jnp.sum(jnp.where(masks[hh], prod, 0.0), axis=-1, keepdims=True)
                sc = _dot_nt(qh, kj) * ATT_SCALE + (_tile_base(cr_ref, hh, lo) - cr_ref[hh:hh + 1, pl.ds(lo_j, tq)])
                p = jnp.exp(sc - lse_i[:, hh * HEAD_DIM:hh * HEAD_DIM + 1])
                if diag:
                    p = jnp.where(col <= row, p, 0.0)
                dv_t = dv_t + _dot(jnp.where(sub_masks[hh], dot_i, jnp.zeros((), bf16)), p.astype(bf16))
                ds = p * (_dot_nt(doh, vj) - delta)
                dcs[hh] = dcs[hh] + jnp.sum(ds, axis=0, keepdims=True)
                dsb = (ds * ATT_SCALE).astype(bf16)
                dq_i = jnp.where(masks[hh], _dot(dsb, kj), dq_i)
                dk_t = dk_t + _dot(jnp.where(sub_masks[hh], qt_i, jnp.zeros((), bf16)), dsb)
            dq_ref[pl.ds(lo, tq), :] += dq_i
            return dk_t, dv_t, dcs[0], dcs[1]

        zero = jnp.zeros((LANES, tq), f32)
        zrow = jnp.zeros((1, tq), f32)
        carry = block(j, (zero, zero, zrow, zrow), True)
        dk_t, dv_t, dc0, dc1 = lax.fori_loop(j + 1, nq, functools.partial(block, diag=False), carry)
        dk_ref[...] = dk_t.astype(bf16)
        dv_ref[...] = dv_t.astype(bf16)
        dc_ref[0:1, :] = -dc0
        dc_ref[1:2, :] = -dc1

    full = lambda p, j: (0, p)
    full_t = lambda p, j: (p, 0)
    tile = lambda p, j: (j, p)
    tile_t = lambda p, j: (p, j)
    return pl.pallas_call(
        body, grid=(FOX_PAIRS, nq),
        in_specs=[pl.BlockSpec((s, LANES), full), pl.BlockSpec((LANES, s), full_t), pl.BlockSpec((tq, LANES), tile),
                  pl.BlockSpec((tq, LANES), lambda p, j: (j, FOX_PAIRS + p)), pl.BlockSpec((s, LANES), full), pl.BlockSpec((LANES, s), full_t),
                  pl.BlockSpec((s, LANES), full), pl.BlockSpec((None, s, LANES), lambda p, j: (p, 0, 0)),
                  pl.BlockSpec((None, 2, s), lambda p, j: (p, 0, 0))],
        out_specs=[pl.BlockSpec((s, LANES), full), pl.BlockSpec((LANES, tq), tile_t), pl.BlockSpec((LANES, tq), tile_t),
                   pl.BlockSpec((None, 2, tq), lambda p, j: (p, 0, j))],
        out_shape=[SDS((s, MAIN_WIDTH), f32), SDS((MAIN_WIDTH, s), bf16), SDS((MAIN_WIDTH, s), bf16), SDS((FOX_PAIRS, 2, s), f32)],
        name=name, compiler_params=_cparams("parallel", "arbitrary"))(q, q_t, kv, kv, d_mixed, do_t, o, lse, c_row)


def adamw(w, g, m, v, name, tr=256):
    r, c = w.shape
    tr = min(tr, r)
    assert r % tr == 0, (name, r, tr)
    c1 = 1.0 / (1.0 - ADAM_B1 ** ADAM_STEP)
    c2 = 1.0 / (1.0 - ADAM_B2 ** ADAM_STEP)

    def body(w_ref, g_ref, m_ref, v_ref, d_ref, mo_ref, vo_ref):
        gv = g_ref[...]
        mn = ADAM_B1 * m_ref[...] + (1.0 - ADAM_B1) * gv
        vn = ADAM_B2 * v_ref[...] + (1.0 - ADAM_B2) * gv * gv
        mo_ref[...] = mn
        vo_ref[...] = vn
        d_ref[...] = -ADAM_LR * ((mn * c1) / (jnp.sqrt(vn * c2) + ADAM_EPS) + ADAM_WD * w_ref[...])

    spec = pl.BlockSpec((tr, c), lambda i: (i, 0))
    return pl.pallas_call(body, grid=(r // tr,), in_specs=[spec] * 4, out_specs=[spec] * 3, out_shape=[SDS((r, c), f32)] * 3,
                          name=name, compiler_params=_cparams("parallel"))(w, g, m, v)


def sum_leading(x, name, out_dtype=f32, tr=None):
    n, r, c = x.shape
    tr = tr or r
    assert r % tr == 0

    def body(x_ref, o_ref):
        acc = x_ref[0].astype(f32)
        for k in range(1, n):
            acc = acc + x_ref[k].astype(f32)
        o_ref[...] = acc.astype(out_dtype)

    return pl.pallas_call(body, grid=(r // tr,), in_specs=[pl.BlockSpec((n, tr, c), lambda i: (0, i, 0))],
                          out_specs=pl.BlockSpec((tr, c), lambda i: (i, 0)), out_shape=SDS((r, c), out_dtype), name=name,
                          compiler_params=_cparams("parallel"))(x)


_ANY = pl.BlockSpec(memory_space=pl.ANY)
_DMA = pltpu.SemaphoreType.DMA


_HBM = pl.BlockSpec(memory_space=pltpu.HBM)
_SEM = pl.BlockSpec(memory_space=pltpu.SEMAPHORE)
_EFFECT = pltpu.SideEffectType.DATAFLOW_SIDE_EFFECTING
_FLIPS = [(0, 0, 1), (1, 0, 0), (0, 1, 0), (1, 1, 0), (1, 0, 1), (0, 1, 1), (1, 1, 1)]


def _me():
    return lax.axis_index("x"), lax.axis_index("y"), lax.axis_index("c")


def _peers():
    mx, my, mc = _me()
    return [(jnp.bitwise_xor(mx, fx), jnp.bitwise_xor(my, fy), jnp.bitwise_xor(mc, fc)) for fx, fy, fc in _FLIPS]


def _index(dev):
    return 4 * dev[0] + 2 * dev[1] + dev[2]


def _win(ref, axis, k, size, count=1):
    idx = [slice(None)] * len(ref.shape)
    idx[axis] = pl.ds(k * size, count * size)
    return ref.at[tuple(idx)]


def _hbm(a):
    return pltpu.with_memory_space_constraint(a, pltpu.HBM)


def _exchange_start(srcs, lands, copies_of, name):
    n = len(srcs)

    def body(*refs):
        src = refs[:n]
        send_sems, recv_sems, self_sems = refs[2 * n:2 * n + 3]
        land = refs[3 * n + 3:4 * n + 3]
        token = refs[4 * n + 3]
        me = _index(_me())
        for a in range(n):
            for s_ref, d_ref, peer in copies_of(a, src[a], land[a], me):
                if peer is None:
                    pltpu.make_async_copy(s_ref, d_ref, self_sems.at[a]).start()
                else:
                    pltpu.make_async_remote_copy(src_ref=s_ref, dst_ref=d_ref, send_sem=send_sems.at[a], recv_sem=recv_sems.at[a],
                                                 device_id=peer, device_id_type=MESH).start()
        token[...] = jnp.zeros_like(token)

    outs = pl.pallas_call(
        body, name=name,
        out_shape=(_DMA((n,)), _DMA((n,)), _DMA((n,)), *[pltpu.HBM(s.shape, s.dtype) for s in srcs],
                   *[pltpu.HBM(l.shape, l.dtype) for l in lands], SDS((8, LANES), f32)),
        in_specs=[_HBM] * (2 * n), out_specs=(_SEM, _SEM, _SEM, *[_HBM] * (2 * n), pl.BlockSpec(memory_space=pltpu.VMEM)),
        input_output_aliases={i: 3 + i for i in range(2 * n)},
        compiler_params=pltpu.CompilerParams(has_side_effects=_EFFECT),
    )(*[_hbm(s) for s in srcs], *[_hbm(lax.empty(l.shape, l.dtype)) for l in lands])
    return dict(sems=outs[:3], srcs=list(outs[3:3 + n]), lands=list(outs[3 + n:3 + 2 * n]), token=outs[3 + 2 * n])


def _exchange_wait(started, waits_of, after, name):
    srcs, lands = started["srcs"], started["lands"]
    n = len(srcs)

    def body(*refs):
        src = refs[:n]
        land = refs[n:2 * n]
        send_sems, recv_sems, self_sems = refs[2 * n:2 * n + 3]
        me = _index(_me())
        for a in range(n):
            seven, (s_ref, d_ref) = waits_of(a, src[a], land[a], me)
            both = pltpu.make_async_remote_copy(src_ref=seven, dst_ref=seven, send_sem=send_sems.at[a], recv_sem=recv_sems.at[a],
                                                device_id=_me(), device_id_type=MESH)
            both.wait_send()
            both.wait_recv()
            pltpu.make_async_copy(s_ref, d_ref, self_sems.at[a]).wait()

    outs = pl.pallas_call(
        body, name=name, out_shape=tuple(pltpu.HBM(t.shape, t.dtype) for t in srcs + lands),
        in_specs=[_HBM] * (2 * n) + [_SEM] * 3 + [_ANY], out_specs=tuple([_HBM] * (2 * n)),
        input_output_aliases={i: i for i in range(2 * n)},
        compiler_params=pltpu.CompilerParams(has_side_effects=_EFFECT),
    )(*srcs, *lands, *started["sems"], after)
    return list(outs[n:])


def gather_start(locs, axes, name):
    lands = [SDS(tuple(N_DEV * d if i == ax else d for i, d in enumerate(l.shape)), l.dtype) for l, ax in zip(locs, axes)]

    def copies_of(a, src, land, me):
        mine = _win(land, axes[a], me, src.shape[axes[a]])
        return [(src, mine, peer) for peer in _peers()] + [(src, mine, None)]

    return _exchange_start(locs, lands, copies_of, name)


def gather_wait(started, axes, after, name):
    def waits_of(a, src, land, me):
        size = src.shape[axes[a]]
        return _win(land, axes[a], 0, size, N_DEV - 1), (src, _win(land, axes[a], me, size))

    return _exchange_wait(started, waits_of, after, name)


def scatter_start(grads, axes, name):
    lands = [SDS((N_DEV,) + tuple(d // N_DEV if i == ax else d for i, d in enumerate(g.shape)), g.dtype) for g, ax in zip(grads, axes)]

    def copies_of(a, src, land, me):
        size = src.shape[axes[a]] // N_DEV
        out = [(_win(src, axes[a], _index(peer), size), land.at[me], peer) for peer in _peers()]
        return out + [(_win(src, axes[a], me, size), land.at[me], None)]

    return _exchange_start(grads, lands, copies_of, name)


def scatter_wait(started, axes, after, name):
    def waits_of(a, src, land, me):
        size = src.shape[axes[a]] // N_DEV
        return land.at[pl.ds(0, N_DEV - 1)], (_win(src, axes[a], me, size), land.at[me])

    return _exchange_wait(started, waits_of, after, name)


def _row_tile(rows, cap=512):
    return max(t for t in range(8, min(rows, cap) + 1, 8) if rows % t == 0)


_SMALL = [
    ("ln_mix_pre", (2, 1024)), ("ln_mix_post", (2, 1024)), ("ln_ffn_pre", (2, 1024)), ("ln_ffn_post", (2, 1024)),
    ("ln_mem", (2, 1024)), ("w_spatial", (1, 6, 128, 128)), ("b_spatial", (1, 6, 128)), ("ln_shared", (1024,)),
    ("b_forget", (12,)), ("ln_v_g", (1, 768)), ("ln_v_b", (1, 768)),
]
_SMALL_TILE = 8 * LANES


def _small_rows(shape):
    return -(-math.prod(shape) // _SMALL_TILE) * 8


def _pack_small(vals, shapes):
    parts = []
    for name, shape in shapes:
        flat = vals[name].reshape(-1).astype(f32)
        rows = _small_rows(shape)
        parts.append(jnp.pad(flat, (0, rows * LANES - flat.shape[0])).reshape(rows, LANES))
    return jnp.concatenate(parts, axis=0)


def _unpack_small(buf, shapes):
    out = {}
    lo = 0
    for name, shape in shapes:
        rows = _small_rows(shape)
        out[name] = buf[lo:lo + rows].reshape(-1)[:math.prod(shape)].reshape(shape)
        lo += rows
    return out


def kernel(x, mem, ln_mix_pre, ln_mix_post, ln_ffn_pre, ln_ffn_post, ln_mem, w_mem_kv, w_out, w_ffn_gate, w_ffn_up, w_ffn_down, w_in_a, w_spatial, b_spatial, ln_v_g, ln_v_b, ln_shared, w_shared_kv, b_forget, w_in_b, loss_target, m_ln_mix_pre, m_ln_mix_post, m_ln_ffn_pre, m_ln_ffn_post, m_ln_mem, m_w_mem_kv, m_w_out, m_w_ffn_gate, m_w_ffn_up, m_w_ffn_down, m_w_in_a, m_w_spatial, m_b_spatial, m_ln_v_g, m_ln_v_b, m_ln_shared, m_w_shared_kv, m_b_forget, m_w_in_b, v_ln_mix_pre, v_ln_mix_post, v_ln_ffn_pre, v_ln_ffn_post, v_ln_mem, v_w_mem_kv, v_w_out, v_w_ffn_gate, v_w_ffn_up, v_w_ffn_down, v_w_in_a, v_w_spatial, v_b_spatial, v_ln_v_g, v_ln_v_b, v_ln_shared, v_w_shared_kv, v_b_forget, v_w_in_b):
    weights = dict(ln_mix_pre=ln_mix_pre, ln_mix_post=ln_mix_post, ln_ffn_pre=ln_ffn_pre, ln_ffn_post=ln_ffn_post, ln_mem=ln_mem,
                   w_mem_kv=w_mem_kv, w_out=w_out, w_ffn_gate=w_ffn_gate, w_ffn_up=w_ffn_up, w_ffn_down=w_ffn_down, w_in_a=w_in_a,
                   w_spatial=w_spatial, b_spatial=b_spatial, ln_v_g=ln_v_g, ln_v_b=ln_v_b, ln_shared=ln_shared,
                   w_shared_kv=w_shared_kv, b_forget=b_forget, w_in_b=w_in_b)
    mom_m = dict(ln_mix_pre=m_ln_mix_pre, ln_mix_post=m_ln_mix_post, ln_ffn_pre=m_ln_ffn_pre, ln_ffn_post=m_ln_ffn_post, ln_mem=m_ln_mem,
                 w_mem_kv=m_w_mem_kv, w_out=m_w_out, w_ffn_gate=m_w_ffn_gate, w_ffn_up=m_w_ffn_up, w_ffn_down=m_w_ffn_down, w_in_a=m_w_in_a,
                 w_spatial=m_w_spatial, b_spatial=m_b_spatial, ln_v_g=m_ln_v_g, ln_v_b=m_ln_v_b, ln_shared=m_ln_shared,
                 w_shared_kv=m_w_shared_kv, b_forget=m_b_forget, w_in_b=m_w_in_b)
    mom_v = dict(ln_mix_pre=v_ln_mix_pre, ln_mix_post=v_ln_mix_post, ln_ffn_pre=v_ln_ffn_pre, ln_ffn_post=v_ln_ffn_post, ln_mem=v_ln_mem,
                 w_mem_kv=v_w_mem_kv, w_out=v_w_out, w_ffn_gate=v_w_ffn_gate, w_ffn_up=v_w_ffn_up, w_ffn_down=v_w_ffn_down, w_in_a=v_w_in_a,
                 w_spatial=v_w_spatial, b_spatial=v_b_spatial, ln_v_g=v_ln_v_g, ln_v_b=v_ln_v_b, ln_shared=v_ln_shared,
                 w_shared_kv=v_w_shared_kv, b_forget=v_b_forget, w_in_b=v_w_in_b)
    names = list(weights)
    mx, my, mc = lax.axis_index("x"), lax.axis_index("y"), lax.axis_index("c")
    me = 4 * mx + 2 * my + mc

    h0 = x[0]
    mem0 = mem[0]
    tgt = loss_target[0]
    seq = h0.shape[0]

    vec = lambda a: a.reshape(1, -1)
    pad_to = lambda a, axis, size: jnp.pad(a, [(0, size - a.shape[i] if i == axis else 0) for i in range(a.ndim)])

    def after(tok, a):
        return a + tok[0, 0].astype(a.dtype)

    lnv_loc = pad_to(jnp.concatenate([ln_v_g, ln_v_b], axis=0), 0, 8)
    st_a = gather_start([w_in_a.astype(bf16), pad_to(lnv_loc, 1, LANES)[None]], [0, 0], "gather_a_start")
    mix_locs = lambda l, tok: [after(tok, w_mem_kv[l]).astype(bf16), w_out[l].astype(bf16)]
    def ffn_gather_start(l, tok):
        gate_up = gather_start([pad_to(after(tok, w_ffn_gate[l]).astype(bf16), 1, FF_SHARD_PAD),
                                pad_to(w_ffn_up[l].astype(bf16), 1, FF_SHARD_PAD)], [1, 1], f"gather_gate_up{l}_start")
        down = gather_start([pad_to(after(gate_up["token"], w_ffn_down[l]).astype(bf16), 0, FF_SHARD_PAD)], [0], f"gather_down{l}_start")
        return gate_up, down

    st_b = [gather_start(mix_locs(0, st_a["token"]), [0, 0], "gather_b0_start"), None]
    st_c = ffn_gather_start(0, st_b[0]["token"])
    st_d = gather_start([after(st_c[1]["token"], w_in_b[0]).astype(bf16), pad_to(w_shared_kv.astype(bf16), 1, KV_PAD)], [0, 0],
                        "gather_d_start")
    st_b[1] = gather_start(mix_locs(1, st_d["token"]), [0, 0], "gather_b1_start")
    st_e = ffn_gather_start(1, st_b[1]["token"])
    ws = w_spatial[0].astype(bf16)
    ws_t = ws.transpose(0, 2, 1)
    bs_t = b_spatial[0].T

    (a0,) = rms_fwd(h0, [after(st_e[1]["token"], vec(ln_mix_pre[0]))], "a0_norm")
    w_in_a8, lnv8 = gather_wait(st_a, [0, 0], a0, "gather_a_wait")
    w_in_a_full = w_in_a8.transpose(1, 0, 2).reshape(D_MODEL, -1)
    lnv_g = lnv8[:, 0, :MAIN_WIDTH // N_DEV].reshape(1, MAIN_WIDTH)
    lnv_b = lnv8[:, 1, :MAIN_WIDTH // N_DEV].reshape(1, MAIN_WIDTH)
    proj0 = mm(a0, w_in_a_full, "proj0", tn=896)
    main0 = gmlp_fwd(proj0, ws, bs_t, lnv_g, lnv_b, "gmlp_fwd")
    w_mkv, w_o = [None, None], [None, None]
    w_mkv[0], w_o[0] = gather_wait(st_b[0], [0, 0], main0, "gather_b0_wait")
    (memn0,) = rms_fwd(mem0, [vec(ln_mem[0])], "mem0_norm")
    kvm0 = mm(memn0, w_mkv[0], "kvm0")
    om0 = mem_attn_fwd(proj0, 2 * MAIN_WIDTH // MEM_WIDTH, kvm0, "mem_attn0")
    mixed0 = jnp.concatenate([main0, om0], axis=-1)
    y1_0, hmid0, f0 = mm_resnorm(mixed0, w_o[0], h0, vec(ln_mix_post[0]), [vec(ln_ffn_pre[0])], "mix_out0")
    w_g0, w_u0 = gather_wait(st_c[0], [1, 1], f0, "gather_gate_up0_wait")
    gu0, act0 = ffn_up(f0, w_g0, w_u0, "ffn_up0")
    (w_d0,) = gather_wait(st_c[1], [0], act0, "gather_down0_wait")
    y2_0, h1, a1, sin1 = mm_resnorm(act0, w_d0, hmid0, vec(ln_ffn_post[0]), [vec(ln_mix_pre[1]), vec(ln_shared)], "ffn_down0")

    w_inb, w_kv = gather_wait(st_d, [0, 0], sin1, "gather_d_wait")
    kvb = mm(sin1, w_kv, "kv_shared", out_dtype=bf16, tn=MAIN_WIDTH, ncols=2 * MAIN_WIDTH)
    zf = mm(sin1, w_kv, "forget_logits", tn=256, col0=2 * MAIN_WIDTH, ncols=256)
    qb = mm(a1, w_inb, "proj1", out_dtype=bf16)
    z_t = jnp.pad(zf[:, :FOX_HEADS].T, ((0, 16 - FOX_HEADS), (0, 0)))
    bf_col = jnp.pad(b_forget, (0, 16 - FOX_HEADS)).reshape(16, 1)
    c_t = fgate_fwd(z_t, bf_col, "fgate_fwd")
    c_row = c_t[:FOX_HEADS].reshape(FOX_PAIRS, 2, seq)
    main1, lse = fox_fwd(qb, kvb, c_row, "fox_fwd")
    w_mkv[1], w_o[1] = gather_wait(st_b[1], [0, 0], main1, "gather_b1_wait")
    (memn1,) = rms_fwd(mem0, [vec(ln_mem[1])], "mem1_norm")
    kvm1 = mm(memn1, w_mkv[1], "kvm1")
    om1 = mem_attn_fwd(qb, MAIN_WIDTH // MEM_WIDTH, kvm1, "mem_attn1")
    mixed1 = jnp.concatenate([main1.astype(bf16), om1], axis=-1)
    y1_1, hmid1, f1 = mm_resnorm(mixed1, w_o[1], h1, vec(ln_mix_post[1]), [vec(ln_ffn_pre[1])], "mix_out1")
    w_g1, w_u1 = gather_wait(st_e[0], [1, 1], f1, "gather_gate_up1_wait")
    gu1, act1 = ffn_up(f1, w_g1, w_u1, "ffn_up1")
    (w_d1,) = gather_wait(st_e[1], [0], act1, "gather_down1_wait")
    y2_1, h2 = mm_resnorm(act1, w_d1, hmid1, vec(ln_ffn_post[1]), [], "ffn_down1")
    dh, loss_tile = loss_grad(h2, tgt, "loss")
    ffn_w = [(w_g0, w_u0, w_d0), (w_g1, w_u1, w_d1)]

    small = {}

    def ffn_backward(layer, dh_out, y2, hmid, f, gu, act, y1):
        w_g, w_u, w_d = ffn_w[layer]
        d_y2, dg_fpost = rms_bwd(y2, vec(ln_ffn_post[layer]), dh_out, None, bf16, f"ffn_post_bwd{layer}")
        dw_down = mm_tn(act, d_y2, f"dw_down{layer}", tk=256)
        rs_down = scatter_start([dw_down], [0], f"scatter_down{layer}_start")
        d_g, d_u = ffn_act_grad(d_y2, w_d, gu, f"ffn_act_grad{layer}")
        dw_g = mm_tn(f, d_g, f"dw_gate{layer}", dep=rs_down["token"])
        dw_u = mm_tn(f, d_u, f"dw_up{layer}")
        rs_gate_up = scatter_start([dw_g, dw_u], [1, 1], f"scatter_gate_up{layer}_start")
        dh_mid, d_y1, dg_fpre, dg_mpost = ffn_in_grad(d_g, d_u, w_g, w_u, hmid, dh_out, after(rs_gate_up["token"], vec(ln_ffn_pre[layer])),
                                                      y1, vec(ln_mix_post[layer]), f"ffn_in_grad{layer}")
        return dh_mid, d_y1, dg_fpost, dg_fpre, dg_mpost, (rs_down, rs_gate_up)

    def mix_out_backward(layer, d_y1, mixed):
        dw_out = mm_tn(mixed, d_y1, f"dw_out{layer}")
        d_mixed = mm(d_y1, w_o[layer], f"d_mixed{layer}", trans_b=True)
        return d_mixed, dw_out

    def mem_backward(layer, q_src, q_block, kvm, memn, d_mixed):
        d_qm, d_kvm = mem_attn_bwd(q_src, q_block, kvm, d_mixed, f"mem_attn_bwd{layer}")
        d_kvm_b = d_kvm.astype(bf16)
        dw_mkv = mm_tn(memn, d_kvm_b, f"dw_mem_kv{layer}")
        d_memn = mm(d_kvm_b, w_mkv[layer], f"d_memn{layer}", trans_b=True)
        _, dg_mem = rms_bwd(mem0, vec(ln_mem[layer]), d_memn, None, bf16, f"mem_norm_bwd{layer}")
        return d_qm, dw_mkv, dg_mem


    dh_mid1, d_y1_1, dg_fpost1, dg_fpre1, dg_mpost1, rs_ffn1 = ffn_backward(1, dh, y2_1, hmid1, f1, gu1, act1, y1_1)
    d_mixed1, dw_out1 = mix_out_backward(1, d_y1_1, mixed1)
    d_qm1, dw_mkv1, dg_mem1 = mem_backward(1, qb, MAIN_WIDTH // MEM_WIDTH, kvm1, memn1, d_mixed1)
    rs_mix1 = scatter_start([dw_out1, dw_mkv1], [0, 0], "scatter_mix1_start")
    q_t = qb[:, :MAIN_WIDTH].T
    do_t = d_mixed1[:, :MAIN_WIDTH].astype(bf16).T
    dq, dk_t, dv_t, dc = fox_bwd(qb, q_t, kvb, d_mixed1, do_t, main1, lse, after(rs_mix1["token"], c_row), "fox_bwd")
    dc_t = jnp.pad(dc.reshape(FOX_HEADS, seq), ((0, 16 - FOX_HEADS), (0, 0)))
    dz_t, db_f = fgate_bwd(dc_t, z_t, bf_col, "fgate_bwd")
    d_kvf = jnp.concatenate([dk_t.T, dv_t.T, jnp.pad(dz_t[:FOX_HEADS].T.astype(bf16), ((0, 0), (0, KV_PAD - KV_WIDTH)))], axis=-1)
    d_proj1 = jnp.concatenate([dq.astype(bf16), d_qm1], axis=-1)
    dw_in_b = mm_tn(a1, d_proj1, "dw_in_b")
    dw_kv = mm_tn(sin1, d_kvf, "dw_kv", tn=896)
    rs_2 = scatter_start([dw_in_b, dw_kv], [0, 0], "scatter_shared_start")
    dh1, (dg_pre1, dg_shared) = proj_in_grad([(d_proj1, w_inb, vec(ln_mix_pre[1])), (d_kvf, w_kv, vec(ln_shared))], h1, dh_mid1,
                                             "in_grad1", dep=rs_2["token"])

    dh_mid0, d_y1_0, dg_fpost0, dg_fpre0, dg_mpost0, rs_ffn0 = ffn_backward(0, dh1, y2_0, hmid0, f0, gu0, act0, y1_0)
    d_mixed0, dw_out0 = mix_out_backward(0, d_y1_0, mixed0)
    d_qm0, dw_mkv0, dg_mem0 = mem_backward(0, proj0, 2 * MAIN_WIDTH // MEM_WIDTH, kvm0, memn0, d_mixed0)
    rs_mix0 = scatter_start([dw_out0, dw_mkv0], [0, 0], "scatter_mix0_start")
    d_uv, dw_s, db_s, dg_lnv, db_lnv = gmlp_bwd(proj0, d_mixed0, ws, ws_t, bs_t, after(rs_mix0["token"], lnv_g), lnv_b, "gmlp_bwd")

    small["ln_mix_pre"] = jnp.concatenate([jnp.zeros_like(dg_pre1), dg_pre1], axis=0)
    small["ln_mix_post"] = jnp.concatenate([dg_mpost0, dg_mpost1], axis=0)
    small["ln_ffn_pre"] = jnp.concatenate([dg_fpre0, dg_fpre1], axis=0)
    small["ln_ffn_post"] = jnp.concatenate([dg_fpost0, dg_fpost1], axis=0)
    small["ln_mem"] = jnp.concatenate([dg_mem0, dg_mem1], axis=0)
    small["w_spatial"] = dw_s[None]
    small["b_spatial"] = db_s[:, :A_GROUPS].T[None]
    small["ln_shared"] = dg_shared[0]
    small["b_forget"] = db_f[:FOX_HEADS, 0]
    small["ln_v_g"] = dg_lnv
    small["ln_v_b"] = db_lnv
    small_rows = jnp.concatenate([_pack_small(small, _SMALL), loss_tile], axis=0)
    st_small = gather_start([small_rows[None]], [0], "gather_small_grads_start")
    d_proj0 = jnp.concatenate([d_uv, after(st_small["token"], d_qm0)], axis=-1)
    dw_in_a = mm_tn(a0, d_proj0, "dw_in_a", tn=896)
    rs_in_a = scatter_start([dw_in_a.reshape(D_MODEL, N_DEV, -1).transpose(1, 0, 2)], [0], "scatter_in_a_start")
    grad_x, (dg_pre0,) = proj_in_grad([(d_proj0, w_in_a_full, vec(ln_mix_pre[0]))], h0, dh_mid0, "in_grad0", dep=rs_in_a["token"])
    st_last = gather_start([dg_pre0.reshape(1, 8, LANES)], [0], "gather_last_grad_start")

    def owned(started, axes, wait_after, name):
        recv = scatter_wait(started, axes, wait_after, name)
        return [sum_leading(r.reshape((N_DEV, -1, r.shape[-1])), f"{name}_sum{i}", tr=_row_tile(math.prod(r.shape[1:-1])))
                for i, r in enumerate(recv)]

    (g_down1,) = owned(rs_ffn1[0], [0], after(st_last["token"], grad_x[:8, :LANES]), "scatter_down1_wait")
    g_gu1 = owned(rs_ffn1[1], [1, 1], g_down1, "scatter_gate_up1_wait")
    g_mix1 = owned(rs_mix1, [0, 0], g_gu1[0], "scatter_mix1_wait")
    g2 = owned(rs_2, [0, 0], g_mix1[0], "scatter_shared_wait")
    (g_down0,) = owned(rs_ffn0[0], [0], g2[0], "scatter_down0_wait")
    g_gu0 = owned(rs_ffn0[1], [1, 1], g_down0, "scatter_gate_up0_wait")
    g_mix0 = owned(rs_mix0, [0, 0], g_gu0[0], "scatter_mix0_wait")
    (g_in_a,) = owned(rs_in_a, [0], g_mix0[0], "scatter_in_a_wait")
    g_local = dict(
        w_ffn_gate=jnp.stack([g_gu0[0], g_gu1[0]])[:, :, :FF_SHARD], w_ffn_up=jnp.stack([g_gu0[1], g_gu1[1]])[:, :, :FF_SHARD],
        w_ffn_down=jnp.stack([g_down0, g_down1])[:, :FF_SHARD], w_out=jnp.stack([g_mix0[0], g_mix1[0]]),
        w_mem_kv=jnp.stack([g_mix0[1], g_mix1[1]]), w_in_b=g2[0][None], w_shared_kv=g2[1][:, :KV_WIDTH], w_in_a=g_in_a[None])
    (small_all,) = gather_wait(st_small, [0], g_in_a, "gather_small_grads_wait")
    (last_all,) = gather_wait(st_last, [0], small_all, "gather_last_grad_wait")
    small_sum = sum_leading(small_all, "sum_small_grads")
    loss = small_sum[small_rows.shape[0] - 1, 0]
    g_small = _unpack_small(small_sum, _SMALL)
    g_small["ln_mix_pre"] = jnp.concatenate([sum_leading(last_all, "sum_last_grad").reshape(1, D_MODEL), g_small["ln_mix_pre"][1:]], axis=0)
    shard = MAIN_WIDTH // N_DEV
    for n in ("ln_v_g", "ln_v_b"):
        g_small[n] = lax.dynamic_slice_in_dim(g_small[n], me * shard, shard, axis=1)
    grad_w = {**g_small, **g_local}

    delta, new_m, new_v = {}, {}, {}
    for n in g_local:
        two_d = (-1, weights[n].shape[-1])
        d_, m_, v_ = adamw(weights[n].reshape(two_d), grad_w[n].reshape(two_d), mom_m[n].reshape(two_d), mom_v[n].reshape(two_d),
                           f"adamw_{n}", tr=_row_tile(math.prod(weights[n].shape[:-1])))
        delta[n], new_m[n], new_v[n] = (t.reshape(weights[n].shape) for t in (d_, m_, v_))
    small_local_shapes = [(n, tuple(weights[n].shape)) for n, _ in _SMALL]
    packed = [_pack_small(src, small_local_shapes) for src in (weights, grad_w, mom_m, mom_v)]
    outs = adamw(*packed, "adamw_small", tr=packed[0].shape[0])
    for dst, buf in zip((delta, new_m, new_v), outs):
        dst.update(_unpack_small(buf, small_local_shapes))

    return (loss, grad_x[None], *[grad_w[n] for n in names], *[delta[n] for n in names],
            *[new_m[n] for n in names], *[new_v[n] for n in names])
```

```python
import functools
import math

import jax
import jax.numpy as jnp
from jax import lax
from jax.experimental import pallas as pl
from jax.experimental.pallas import tpu as pltpu

f32 = jnp.float32
bf16 = jnp.bfloat16
SDS = jax.ShapeDtypeStruct

D_MODEL = 1024
MAIN_WIDTH = 768
MEM_WIDTH = 256
HEAD_DIM = 64
MEM_HEADS = 4
FOX_HEADS = 12
FOX_PAIRS = FOX_HEADS // 2
CHUNK = 128
A_GROUPS = 6
FF_SHARD = 352
FF_SHARD_PAD = 384
FF_PAD = 8 * FF_SHARD_PAD
KV_WIDTH = 2 * MAIN_WIDTH + FOX_HEADS
KV_PAD = 1792
RMS_EPS = 1e-6
LN_EPS = 1e-5
ATT_SCALE = HEAD_DIM ** -0.5
ADAM_LR, ADAM_B1, ADAM_B2, ADAM_EPS, ADAM_WD, ADAM_STEP = 0.001, 0.9, 0.999, 1e-08, 0.01, 10
N_DEV = 8
AXES = ("x", "y", "c")
MESH = pl.DeviceIdType.MESH
V7X_VMEM_LIMIT = 56 * 1024 * 1024
LANES = 128
FLAT_W = 512
ROW_PAD = 16


def _cparams(*sem):
    return pltpu.CompilerParams(dimension_semantics=sem or None, vmem_limit_bytes=V7X_VMEM_LIMIT)


def _dot(a, b):
    return jnp.dot(a, b, preferred_element_type=f32)


def _dot_nt(a, b):
    return lax.dot_general(a, b, (((1,), (1,)), ((), ())), preferred_element_type=f32)


def _dot_tn(a, b):
    return lax.dot_general(a, b, (((0,), (0,)), ((), ())), preferred_element_type=f32)


def _gelu(x):
    k = math.sqrt(2.0 / math.pi)
    t = jnp.tanh(k * (x + 0.044715 * x * x * x))
    return 0.5 * x * (1.0 + t), t


def _gelu_grad(x, t):
    k = math.sqrt(2.0 / math.pi)
    return 0.5 * (1.0 + t) + 0.5 * x * (1.0 - t * t) * k * (1.0 + 3.0 * 0.044715 * x * x)


def _sigmoid(x):
    return 1.0 / (1.0 + jnp.exp(-x))


def rms_fwd(x, gains, name, tm=512):
    m, d = x.shape
    tm = min(tm, m)
    n = len(gains)

    def body(x_ref, *refs):
        xv = x_ref[...]
        y = xv * lax.rsqrt(jnp.sum(xv * xv, axis=-1, keepdims=True) * (1.0 / d) + RMS_EPS)
        for g_ref, o_ref in zip(refs[:n], refs[n:]):
            o_ref[...] = (y * g_ref[...]).astype(bf16)

    row = pl.BlockSpec((tm, d), lambda i: (i, 0))
    vec = pl.BlockSpec((1, d), lambda i: (0, 0))
    return pl.pallas_call(body, grid=(m // tm,), in_specs=[row] + [vec] * n, out_specs=[row] * n,
                          out_shape=[SDS((m, d), bf16)] * n, name=name, compiler_params=_cparams("parallel"))(x, *gains)


def rms_bwd(x, g, dy, add, out_dtype, name, tm=512):
    m, d = x.shape
    tm = min(tm, m)
    has_add = add is not None

    def body(x_ref, g_ref, dy_ref, *refs):
        dx_ref, dg_ref = refs[-2], refs[-1]
        xv = x_ref[...]
        dyv = dy_ref[...].astype(f32)
        r = lax.rsqrt(jnp.sum(xv * xv, axis=-1, keepdims=True) * (1.0 / d) + RMS_EPS)
        xn = xv * r
        dyg = dyv * g_ref[...]
        dx = r * (dyg - xn * (jnp.sum(dyg * xn, axis=-1, keepdims=True) * (1.0 / d)))
        if has_add:
            dx = dx + refs[0][...]
        dx_ref[...] = dx.astype(out_dtype)

        @pl.when(pl.program_id(0) == 0)
        def _():
            dg_ref[...] = jnp.zeros_like(dg_ref)

        dg_ref[...] += jnp.sum(dyv * xn, axis=0, keepdims=True)

    row = pl.BlockSpec((tm, d), lambda i: (i, 0))
    vec = pl.BlockSpec((1, d), lambda i: (0, 0))
    ins = [x, g, dy] + ([add] if has_add else [])
    return pl.pallas_call(body, grid=(m // tm,), in_specs=[row, vec, row] + ([row] if has_add else []),
                          out_specs=[row, vec], out_shape=[SDS((m, d), out_dtype), SDS((1, d), f32)], name=name,
                          compiler_params=_cparams("arbitrary"))(*ins)


def loss_grad(h, tgt, name, tm=512):
    m, d = h.shape

    def body(h_ref, t_ref, dy_ref, l_ref):
        e = h_ref[...] - t_ref[...]
        dy_ref[...] = e * (1.0 / d)

        @pl.when(pl.program_id(0) == 0)
        def _():
            l_ref[...] = jnp.zeros_like(l_ref)

        part = jnp.sum(jnp.sum(e * e, axis=-1, keepdims=True), axis=0, keepdims=True) * (0.5 / d)
        l_ref[...] += jnp.broadcast_to(part, l_ref.shape)

    row = pl.BlockSpec((tm, d), lambda i: (i, 0))
    return pl.pallas_call(body, grid=(m // tm,), in_specs=[row, row],
                          out_specs=[row, pl.BlockSpec((8, LANES), lambda i: (0, 0))],
                          out_shape=[SDS((m, d), f32), SDS((8, LANES), f32)], name=name,
                          compiler_params=_cparams("arbitrary"))(h, tgt)


def mm(a, b, name, trans_b=False, out_dtype=f32, tm=512, tn=1024, layer=None, col0=0, ncols=None, dep=None):
    m, k = a.shape
    n_all = b.shape[-2] if trans_b else b.shape[-1]
    n = n_all if ncols is None else ncols
    tm, tn = min(tm, m), min(tn, n)
    assert m % tm == 0 and n % tn == 0 and col0 % tn == 0 and not (trans_b and col0), (name, m, n, tm, tn)
    jb = col0 // tn
    lead = () if layer is None else (None,)
    sel = () if layer is None else (layer,)

    def body(a_ref, b_ref, *rest):
        r = _dot_nt(a_ref[...], b_ref[...]) if trans_b else _dot(a_ref[...], b_ref[...])
        rest[-1][...] = r.astype(out_dtype)

    if trans_b:
        b_spec = pl.BlockSpec(lead + (tn, k), lambda j, i: sel + (j, 0))
    else:
        b_spec = pl.BlockSpec(lead + (k, tn), lambda j, i: sel + (0, jb + j))
    deps = [] if dep is None else [dep]
    dep_specs = [pl.BlockSpec((8, LANES), lambda j, i: (0, 0))] * len(deps)
    return pl.pallas_call(body, grid=(n // tn, m // tm), in_specs=[pl.BlockSpec((tm, k), lambda j, i: (i, 0)), b_spec] + dep_specs,
                          out_specs=pl.BlockSpec((tm, tn), lambda j, i: (i, j)), out_shape=SDS((m, n), out_dtype),
                          name=name, compiler_params=_cparams("parallel", "parallel"))(a, b, *deps)


def mm_tn(a, g, name, tk=512, tn=1024, out_dtype=bf16, dep=None):
    s, k = a.shape
    n = g.shape[1]
    tk, tn = min(tk, k), min(tn, n)
    assert k % tk == 0 and n % tn == 0, (name, k, n, tk, tn)

    def body(a_ref, g_ref, *rest):
        rest[-1][...] = _dot_tn(a_ref[...], g_ref[...]).astype(out_dtype)

    deps = [] if dep is None else [dep]
    dep_specs = [pl.BlockSpec((8, LANES), lambda i, j: (0, 0))] * len(deps)
    return pl.pallas_call(body, grid=(k // tk, n // tn),
                          in_specs=[pl.BlockSpec((s, tk), lambda i, j: (0, i)), pl.BlockSpec((s, tn), lambda i, j: (0, j))] + dep_specs,
                          out_specs=pl.BlockSpec((tk, tn), lambda i, j: (i, j)), out_shape=SDS((k, n), out_dtype), name=name,
                          compiler_params=_cparams("parallel", "parallel"))(a, g, *deps)


def _resident(shape, index_map):
    return pl.BlockSpec(shape, index_map, pipeline_mode=pl.Buffered(1))


def _rms(xv):
    return xv * lax.rsqrt(jnp.sum(xv * xv, axis=-1, keepdims=True) * (1.0 / xv.shape[-1]) + RMS_EPS)


def _rms_bwd_math(xv, g, dy):
    d = xv.shape[-1]
    r = lax.rsqrt(jnp.sum(xv * xv, axis=-1, keepdims=True) * (1.0 / d) + RMS_EPS)
    xn = xv * r
    dyg = dy * g
    dx = r * (dyg - xn * (jnp.sum(dyg * xn, axis=-1, keepdims=True) * (1.0 / d)))
    return dx, jnp.sum(dy * xn, axis=0, keepdims=True)


SUB_ROWS = 256


def mm_resnorm(a, b, h, g_post, gains, name, tm=512):
    m, k = a.shape
    d = b.shape[1]
    n = len(gains)

    def body(a_ref, b_ref, h_ref, gp_ref, *refs):
        for r in range(tm // SUB_ROWS):
            rows = slice(r * SUB_ROWS, (r + 1) * SUB_ROWS)
            y = _dot(a_ref[rows, :], b_ref[...])
            refs[n][rows, :] = y
            hn = h_ref[rows, :] + _rms(y) * gp_ref[...]
            refs[n + 1][rows, :] = hn
            if n:
                z = _rms(hn)
                for g_ref, o_ref in zip(refs[:n], refs[n + 2:]):
                    o_ref[rows, :] = (z * g_ref[...]).astype(bf16)

    row = pl.BlockSpec((tm, d), lambda i: (i, 0))
    vec = pl.BlockSpec((1, d), lambda i: (0, 0))
    return pl.pallas_call(body, grid=(m // tm,),
                          in_specs=[pl.BlockSpec((tm, k), lambda i: (i, 0)), _resident((k, d), lambda i: (0, 0)), row, vec] + [vec] * n,
                          out_specs=[row] * (n + 2), out_shape=[SDS((m, d), f32)] * 2 + [SDS((m, d), bf16)] * n, name=name,
                          compiler_params=_cparams("parallel"))(a, b, h, g_post, *gains)


def ffn_act_grad(d_y2, w_d, gu, name, tm=512, tn=1536):
    s, d = d_y2.shape
    ff = w_d.shape[0]
    nb = ff // tn

    def body(a_ref, b_ref, g_ref, u_ref, dg_ref, du_ref):
        av = a_ref[...]
        tc = 256
        for c in range(tn // tc):
            cols = slice(c * tc, (c + 1) * tc)
            da = _dot_nt(av, b_ref[cols, :])
            gg = g_ref[:, cols].astype(f32)
            sg = _sigmoid(gg)
            dg_ref[:, cols] = (da * u_ref[:, cols].astype(f32) * (sg * (1.0 + gg * (1.0 - sg)))).astype(bf16)
            du_ref[:, cols] = (da * gg * sg).astype(bf16)

    tile = pl.BlockSpec((tm, tn), lambda j, i: (i, j))
    return pl.pallas_call(body, grid=(nb, s // tm),
                          in_specs=[pl.BlockSpec((tm, d), lambda j, i: (i, 0)), pl.BlockSpec((tn, d), lambda j, i: (j, 0)), tile,
                                    pl.BlockSpec((tm, tn), lambda j, i: (i, nb + j))],
                          out_specs=[tile, tile], out_shape=[SDS((s, ff), bf16)] * 2, name=name,
                          compiler_params=_cparams("parallel", "parallel"))(d_y2, w_d, gu, gu)


def ffn_in_grad(d_g, d_u, w_g, w_u, hmid, dh_out, g_pre, y1, g_post, name, tm=512):
    s, ff = d_g.shape
    d = w_g.shape[0]

    def body(dg_ref, du_ref, wg_ref, wu_ref, hm_ref, dho_ref, gpre_ref, y1_ref, gpost_ref, dhm_ref, dy1_ref, dgpre_ref, dgpost_ref):
        @pl.when(pl.program_id(0) == 0)
        def _():
            dgpre_ref[...] = jnp.zeros_like(dgpre_ref)
            dgpost_ref[...] = jnp.zeros_like(dgpost_ref)

        for r in range(tm // SUB_ROWS):
            rows = slice(r * SUB_ROWS, (r + 1) * SUB_ROWS)
            d_f = _dot_nt(dg_ref[rows, :], wg_ref[...]) + _dot_nt(du_ref[rows, :], wu_ref[...])
            dx, dg1 = _rms_bwd_math(hm_ref[rows, :], gpre_ref[...], d_f)
            dh_mid = dho_ref[rows, :] + dx
            dhm_ref[rows, :] = dh_mid
            dgpre_ref[...] += dg1
            dy1, dg2 = _rms_bwd_math(y1_ref[rows, :], gpost_ref[...], dh_mid)
            dy1_ref[rows, :] = dy1.astype(bf16)
            dgpost_ref[...] += dg2

    row = pl.BlockSpec((tm, d), lambda i: (i, 0))
    vec = pl.BlockSpec((1, d), lambda i: (0, 0))
    wide = pl.BlockSpec((tm, ff), lambda i: (i, 0))
    w_spec = _resident((d, ff), lambda i: (0, 0))
    return pl.pallas_call(body, grid=(s // tm,), in_specs=[wide, wide, w_spec, w_spec, row, row, vec, row, vec],
                          out_specs=[row, row, vec, vec], out_shape=[SDS((s, d), f32), SDS((s, d), bf16), SDS((1, d), f32), SDS((1, d), f32)],
                          name=name, compiler_params=_cparams("arbitrary"))(d_g, d_u, w_g, w_u, hmid, dh_out, g_pre, y1, g_post)


def proj_in_grad(pairs, x, add, name, tm=512, dep=None):
    s, d = x.shape
    n = len(pairs)
    deps = [] if dep is None else [dep]

    def body(*refs):
        x_ref, add_ref = refs[3 * n], refs[3 * n + 1]
        outs = refs[3 * n + 2 + len(deps):]

        @pl.when(pl.program_id(0) == 0)
        def _():
            for o in outs[1:]:
                o[...] = jnp.zeros_like(o)

        for r in range(tm // SUB_ROWS):
            rows = slice(r * SUB_ROWS, (r + 1) * SUB_ROWS)
            xv = x_ref[rows, :]
            dx = add_ref[rows, :]
            for i in range(n):
                a_ref, b_ref, g_ref = refs[3 * i:3 * i + 3]
                dxi, dgi = _rms_bwd_math(xv, g_ref[...], _dot_nt(a_ref[rows, :], b_ref[...]))
                dx = dx + dxi
                outs[1 + i][...] += dgi
            outs[0][rows, :] = dx

    row = pl.BlockSpec((tm, d), lambda i: (i, 0))
    vec = pl.BlockSpec((1, d), lambda i: (0, 0))
    in_specs, args = [], []
    for a, b, g in pairs:
        k = a.shape[1]
        in_specs += [pl.BlockSpec((tm, k), lambda i: (i, 0)), _resident((d, k), lambda i: (0, 0)), vec]
        args += [a, b, g]
    in_specs += [row, row] + [pl.BlockSpec((8, LANES), lambda i: (0, 0))] * len(deps)
    out = pl.pallas_call(body, grid=(s // tm,), in_specs=in_specs, out_specs=[row] + [vec] * n,
                         out_shape=[SDS((s, d), f32)] + [SDS((1, d), f32)] * n, name=name,
                         compiler_params=_cparams("arbitrary"))(*args, x, add, *deps)
    return out[0], out[1:]


def ffn_up(f, wg, wu, name, tm=512, tc=256):
    s, d = f.shape
    ff = wg.shape[-1]

    def body(f_ref, wg_ref, wu_ref, gu_ref, act_ref):
        fv = f_ref[...]
        for j in range(ff // tc):
            lo = j * tc
            gg = _dot(fv, wg_ref[:, lo:lo + tc])
            uu = _dot(fv, wu_ref[:, lo:lo + tc])
            gu_ref[:, lo:lo + tc] = gg.astype(bf16)
            gu_ref[:, ff + lo:ff + lo + tc] = uu.astype(bf16)
            act_ref[:, lo:lo + tc] = (gg * _sigmoid(gg) * uu).astype(bf16)

    w_spec = _resident((d, ff), lambda i: (0, 0))
    return pl.pallas_call(body, grid=(s // tm,), in_specs=[pl.BlockSpec((tm, d), lambda i: (i, 0)), w_spec, w_spec],
                          out_specs=[pl.BlockSpec((tm, 2 * ff), lambda i: (i, 0)), pl.BlockSpec((tm, ff), lambda i: (i, 0))],
                          out_shape=[SDS((s, 2 * ff), bf16), SDS((s, ff), bf16)], name=name,
                          compiler_params=_cparams("parallel"))(f, wg, wu)


def _gmlp_forward_chunk(u, v, w_refs, bias, ln_g, ln_b):
    gu, tu = _gelu(u)
    gv, tv = _gelu(v)
    mu = jnp.sum(gv, axis=-1, keepdims=True) * (1.0 / MAIN_WIDTH)
    xc = gv - mu
    rstd = lax.rsqrt(jnp.sum(xc * xc, axis=-1, keepdims=True) * (1.0 / MAIN_WIDTH) + LN_EPS)
    xhat = xc * rstd
    vln = xhat * ln_g + ln_b
    row = lax.broadcasted_iota(jnp.int32, (CHUNK, CHUNK), 0)
    col = lax.broadcasted_iota(jnp.int32, (CHUNK, CHUNK), 1)
    s_parts = []
    for g in range(A_GROUPS):
        w = jnp.where(col <= row, w_refs[g], jnp.zeros((), bf16))
        s_parts.append(_dot(w, vln[:, g * CHUNK:(g + 1) * CHUNK].astype(bf16)) + bias[:, g:g + 1])
    return gu, tu, tv, rstd, xhat, vln, s_parts


def gmlp_fwd(proj, ws, bs_t, ln_g, ln_b, name, tm=512):
    s = proj.shape[0]

    def body(u_ref, v_ref, w_ref, b_ref, g_ref, bb_ref, o_ref):
        bias = b_ref[...]
        for c in range(tm // CHUNK):
            rows = slice(c * CHUNK, (c + 1) * CHUNK)
            gu, _, _, _, _, _, s_parts = _gmlp_forward_chunk(u_ref[rows, :], v_ref[rows, :], w_ref, bias, g_ref[...], bb_ref[...])
            for g in range(A_GROUPS):
                cols = slice(g * CHUNK, (g + 1) * CHUNK)
                o_ref[rows, cols] = (gu[:, cols] * s_parts[g]).astype(bf16)

    vec = pl.BlockSpec((1, MAIN_WIDTH), lambda i: (0, 0))
    return pl.pallas_call(
        body, grid=(s // tm,),
        in_specs=[pl.BlockSpec((tm, MAIN_WIDTH), lambda i: (i, 0)), pl.BlockSpec((tm, MAIN_WIDTH), lambda i: (i, 1)),
                  pl.BlockSpec((A_GROUPS, CHUNK, CHUNK), lambda i: (0, 0, 0)), pl.BlockSpec((CHUNK, A_GROUPS), lambda i: (0, 0)), vec, vec],
        out_specs=pl.BlockSpec((tm, MAIN_WIDTH), lambda i: (i, 0)), out_shape=SDS((s, MAIN_WIDTH), bf16), name=name,
        compiler_params=_cparams("parallel"))(proj, proj, ws, bs_t, ln_g, ln_b)


def gmlp_bwd(proj, d_mixed, ws, ws_t, bs_t, ln_g, ln_b, name, tm=512):
    s = proj.shape[0]

    def body(u_ref, v_ref, dm_ref, w_ref, wt_ref, b_ref, g_ref, bb_ref, duv_ref, dw_ref, db_ref, dg_ref, dbb_ref):
        @pl.when(pl.program_id(0) == 0)
        def _():
            dw_ref[...] = jnp.zeros_like(dw_ref)
            db_ref[...] = jnp.zeros_like(db_ref)
            dg_ref[...] = jnp.zeros_like(dg_ref)
            dbb_ref[...] = jnp.zeros_like(dbb_ref)

        bias = b_ref[...]
        ln_gv = g_ref[...]
        row = lax.broadcasted_iota(jnp.int32, (CHUNK, CHUNK), 0)
        col = lax.broadcasted_iota(jnp.int32, (CHUNK, CHUNK), 1)
        lane = lax.broadcasted_iota(jnp.int32, (CHUNK, LANES), 1)
        for c in range(tm // CHUNK):
            rows = slice(c * CHUNK, (c + 1) * CHUNK)
            u = u_ref[rows, :]
            v = v_ref[rows, :]
            gu, tu, tv, rstd, xhat, vln, s_parts = _gmlp_forward_chunk(u, v, w_ref, bias, ln_gv, bb_ref[...])
            dm = dm_ref[rows, :]
            d_vln_parts = []
            d_gu_parts = []
            db_acc = jnp.zeros((CHUNK, LANES), f32)
            for g in range(A_GROUPS):
                cols = slice(g * CHUNK, (g + 1) * CHUNK)
                dmg = dm[:, cols]
                d_gu_parts.append(dmg * s_parts[g])
                d_s = dmg * gu[:, cols]
                db_acc = db_acc + jnp.where(lane == g, jnp.sum(d_s, axis=-1, keepdims=True), 0.0)
                d_sb = d_s.astype(bf16)
                dw_ref[g] += jnp.where(col <= row, _dot_nt(d_sb, vln[:, cols].astype(bf16)), 0.0)
                wt = jnp.where(row <= col, wt_ref[g], jnp.zeros((), bf16))
                d_vln_parts.append(_dot(wt, d_sb))
            db_ref[...] += db_acc
            d_vln = jnp.concatenate(d_vln_parts, axis=-1)
            d_gu = jnp.concatenate(d_gu_parts, axis=-1)
            dg_ref[...] += jnp.sum(d_vln * xhat, axis=0, keepdims=True)
            dbb_ref[...] += jnp.sum(d_vln, axis=0, keepdims=True)
            dxh = d_vln * ln_gv
            m1 = jnp.sum(dxh, axis=-1, keepdims=True) * (1.0 / MAIN_WIDTH)
            m2 = jnp.sum(dxh * xhat, axis=-1, keepdims=True) * (1.0 / MAIN_WIDTH)
            d_gv = rstd * (dxh - m1 - xhat * m2)
            duv_ref[rows, :MAIN_WIDTH] = (d_gu * _gelu_grad(u, tu)).astype(bf16)
            duv_ref[rows, MAIN_WIDTH:] = (d_gv * _gelu_grad(v, tv)).astype(bf16)

    vec = pl.BlockSpec((1, MAIN_WIDTH), lambda i: (0, 0))
    wspec = pl.BlockSpec((A_GROUPS, CHUNK, CHUNK), lambda i: (0, 0, 0))
    return pl.pallas_call(
        body, grid=(s // tm,),
        in_specs=[pl.BlockSpec((tm, MAIN_WIDTH), lambda i: (i, 0)), pl.BlockSpec((tm, MAIN_WIDTH), lambda i: (i, 1)),
                  pl.BlockSpec((tm, MAIN_WIDTH), lambda i: (i, 0)), wspec, wspec, pl.BlockSpec((CHUNK, A_GROUPS), lambda i: (0, 0)), vec, vec],
        out_specs=[pl.BlockSpec((tm, 2 * MAIN_WIDTH), lambda i: (i, 0)), wspec, pl.BlockSpec((CHUNK, LANES), lambda i: (0, 0)), vec, vec],
        out_shape=[SDS((s, 2 * MAIN_WIDTH), bf16), SDS((A_GROUPS, CHUNK, CHUNK), f32), SDS((CHUNK, LANES), f32),
                   SDS((1, MAIN_WIDTH), f32), SDS((1, MAIN_WIDTH), f32)],
        name=name, compiler_params=_cparams("arbitrary"))(proj, proj, d_mixed, ws, ws_t, bs_t, ln_g, ln_b)


def _head_mask(width, h):
    lane = lax.broadcasted_iota(jnp.int32, (1, width), 1)
    return (lane >= h * HEAD_DIM) & (lane < (h + 1) * HEAD_DIM)


def mem_attn_fwd(proj, q_block, kv, name, tm=512):
    s = proj.shape[0]
    n_mem = kv.shape[0]

    def body(q_ref, kv_ref, o_ref):
        q = q_ref[...].astype(f32)
        k = kv_ref[:, :MEM_WIDTH].astype(bf16)
        v = kv_ref[:, MEM_WIDTH:].astype(bf16)
        out = jnp.zeros((tm, MEM_WIDTH), f32)
        for h in range(MEM_HEADS):
            msk = _head_mask(MEM_WIDTH, h)
            qh = jnp.where(msk, q, 0.0).astype(bf16)
            sc = _dot_nt(qh, k) * ATT_SCALE
            e = jnp.exp(sc - jnp.max(sc, axis=-1, keepdims=True))
            p = e / jnp.sum(e, axis=-1, keepdims=True)
            out = jnp.where(msk, _dot(p.astype(bf16), v), out)
        o_ref[...] = out.astype(bf16)

    return pl.pallas_call(body, grid=(s // tm,),
                          in_specs=[pl.BlockSpec((tm, MEM_WIDTH), lambda i: (i, q_block)), pl.BlockSpec((n_mem, 2 * MEM_WIDTH), lambda i: (0, 0))],
                          out_specs=pl.BlockSpec((tm, MEM_WIDTH), lambda i: (i, 0)), out_shape=SDS((s, MEM_WIDTH), bf16), name=name,
                          compiler_params=_cparams("parallel"))(proj, kv)


def mem_attn_bwd(proj, q_block, kv, d_mixed, name, tm=512):
    s = proj.shape[0]
    n_mem = kv.shape[0]

    def body(q_ref, kv_ref, do_ref, dq_ref, dkv_ref):
        @pl.when(pl.program_id(0) == 0)
        def _():
            dkv_ref[...] = jnp.zeros_like(dkv_ref)

        q = q_ref[...].astype(f32)
        do = do_ref[...]
        k = kv_ref[:, :MEM_WIDTH].astype(bf16)
        v = kv_ref[:, MEM_WIDTH:].astype(bf16)
        dq = jnp.zeros((tm, MEM_WIDTH), f32)
        dk = jnp.zeros((n_mem, MEM_WIDTH), f32)
        dv = jnp.zeros((n_mem, MEM_WIDTH), f32)
        for h in range(MEM_HEADS):
            msk = _head_mask(MEM_WIDTH, h)
            qh = jnp.where(msk, q, 0.0).astype(bf16)
            doh = jnp.where(msk, do, 0.0).astype(bf16)
            sc = _dot_nt(qh, k) * ATT_SCALE
            e = jnp.exp(sc - jnp.max(sc, axis=-1, keepdims=True))
            p = e / jnp.sum(e, axis=-1, keepdims=True)
            dp = _dot_nt(doh, v)
            ds = p * (dp - jnp.sum(dp * p, axis=-1, keepdims=True))
            dsb = (ds * ATT_SCALE).astype(bf16)
            dq = jnp.where(msk, _dot(dsb, k), dq)
            dk = dk + _dot_tn(dsb, qh)
            dv = dv + _dot_tn(p.astype(bf16), doh)
        dq_ref[...] = dq.astype(bf16)
        dkv_ref[:, :MEM_WIDTH] += dk
        dkv_ref[:, MEM_WIDTH:] += dv

    return pl.pallas_call(
        body, grid=(s // tm,),
        in_specs=[pl.BlockSpec((tm, MEM_WIDTH), lambda i: (i, q_block)), pl.BlockSpec((n_mem, 2 * MEM_WIDTH), lambda i: (0, 0)),
                  pl.BlockSpec((tm, MEM_WIDTH), lambda i: (i, MAIN_WIDTH // MEM_WIDTH))],
        out_specs=[pl.BlockSpec((tm, MEM_WIDTH), lambda i: (i, 0)), pl.BlockSpec((n_mem, 2 * MEM_WIDTH), lambda i: (0, 0))],
        out_shape=[SDS((s, MEM_WIDTH), bf16), SDS((n_mem, 2 * MEM_WIDTH), f32)], name=name,
        compiler_params=_cparams("arbitrary"))(proj, kv, d_mixed)


def _tri(t, upper):
    r = lax.broadcasted_iota(jnp.int32, (t, t), 0)
    c = lax.broadcasted_iota(jnp.int32, (t, t), 1)
    return ((r <= c) if upper else (r >= c)).astype(f32)


def fgate_fwd(z_t, b, name, t=512):
    hh, s = z_t.shape

    def body(z_ref, b_ref, c_ref):
        u = _tri(t, True)
        carry = jnp.zeros((hh, 1), f32)
        for blk in range(s // t):
            x = z_ref[:, blk * t:(blk + 1) * t] + b_ref[...]
            logf = jnp.minimum(x, 0.0) - jnp.log(1.0 + jnp.exp(-jnp.abs(x)))
            y = jnp.dot(logf, u, precision=lax.Precision.HIGHEST, preferred_element_type=f32) + carry
            c_ref[:, blk * t:(blk + 1) * t] = y
            carry = y[:, t - 1:t]

    return pl.pallas_call(body, out_shape=SDS((hh, s), f32), name=name, compiler_params=_cparams())(z_t, b)


def fgate_bwd(dc_t, z_t, b, name, t=512):
    hh, s = z_t.shape

    def body(dc_ref, z_ref, b_ref, dz_ref, db_ref):
        low = _tri(t, False)
        carry = jnp.zeros((hh, 1), f32)
        total = jnp.zeros((hh, 1), f32)
        for blk in reversed(range(s // t)):
            cols = slice(blk * t, (blk + 1) * t)
            y = jnp.dot(dc_ref[:, cols], low, precision=lax.Precision.HIGHEST, preferred_element_type=f32) + carry
            carry = y[:, 0:1]
            dz = y * _sigmoid(-(z_ref[:, cols] + b_ref[...]))
            dz_ref[:, cols] = dz
            total = total + jnp.sum(dz, axis=-1, keepdims=True)
        db_ref[...] = jnp.broadcast_to(total, db_ref.shape)

    return pl.pallas_call(body, out_shape=[SDS((hh, s), f32), SDS((hh, LANES), f32)], name=name,
                          compiler_params=_cparams())(dc_t, z_t, b)


def _pair_masks():
    lane = lax.broadcasted_iota(jnp.int32, (1, LANES), 1)
    return [lane < HEAD_DIM, lane >= HEAD_DIM]


def _tile_base(cr_ref, hh, lo):
    return cr_ref[hh:hh + 1, pl.ds(lo, LANES)][:, 0:1]


def fox_fwd(q, kv, c_row, name, tq=512):
    s = kv.shape[0]
    nq = s // tq

    def body(q_ref, k_ref, v_ref, cr_ref, o_ref, lse_ref):
        i = pl.program_id(1)
        qv = q_ref[...]
        masks = _pair_masks()
        row = lax.broadcasted_iota(jnp.int32, (tq, tq), 0)
        col = lax.broadcasted_iota(jnp.int32, (tq, tq), 1)
        qh = [jnp.where(masks[hh], qv, jnp.zeros((), bf16)) * ATT_SCALE for hh in range(2)]
        ct = [_tile_base(cr_ref, hh, pl.multiple_of(i * tq, tq)) for hh in range(2)]

        def block(j, carry, diag):
            lo = pl.multiple_of(j * tq, tq)
            ks = k_ref[pl.ds(lo, tq), :]
            vs = v_ref[pl.ds(lo, tq), :]
            out = []
            for hh in range(2):
                m, l, acc = carry[hh]
                sc = _dot_nt(qh[hh], ks) + (ct[hh] - cr_ref[hh:hh + 1, pl.ds(lo, tq)])
                if diag:
                    sc = jnp.where(col <= row, sc, -jnp.inf)
                m_new = jnp.maximum(m, jnp.max(sc, axis=-1, keepdims=True))
                alpha = jnp.exp(m - m_new)
                p = jnp.exp(sc - m_new)
                l = alpha * l + jnp.sum(p, axis=-1, keepdims=True)
                p_hi = p.astype(bf16)
                p_lo = (p - p_hi.astype(f32)).astype(bf16)
                acc = alpha * acc + (_dot(p_hi, vs) + _dot(p_lo, vs))
                out.append((m_new, l, acc))
            return tuple(out)

        init = (jnp.full((tq, 1), -jnp.inf, f32), jnp.zeros((tq, 1), f32), jnp.zeros((tq, LANES), f32))
        carry = lax.fori_loop(0, i, functools.partial(block, diag=False), (init, init))
        res = [(acc / l, m + jnp.log(l)) for m, l, acc in block(i, carry, True)]
        o_ref[...] = jnp.where(masks[0], res[0][0], res[1][0])
        lse_ref[...] = jnp.where(masks[0], res[0][1], res[1][1])

    return pl.pallas_call(
        body, grid=(FOX_PAIRS, nq),
        in_specs=[pl.BlockSpec((tq, LANES), lambda p, i: (i, p)), pl.BlockSpec((s, LANES), lambda p, i: (0, p)),
                  pl.BlockSpec((s, LANES), lambda p, i: (0, FOX_PAIRS + p)), pl.BlockSpec((None, 2, s), lambda p, i: (p, 0, 0))],
        out_specs=[pl.BlockSpec((tq, LANES), lambda p, i: (i, p)), pl.BlockSpec((None, tq, LANES), lambda p, i: (p, i, 0))],
        out_shape=[SDS((s, MAIN_WIDTH), f32), SDS((FOX_PAIRS, s, LANES), f32)], name=name,
        compiler_params=_cparams("parallel", "parallel"))(q, kv, kv, c_row)


def fox_bwd(q, q_t, kv, d_mixed, do_t, o, lse, c_row, name, tq=512):
    s = kv.shape[0]
    nq = s // tq

    def body(q_ref, qt_ref, k_ref, v_ref, do_ref, dot_ref, o_ref, lse_ref, cr_ref, dq_ref, dk_ref, dv_ref, dc_ref):
        j = pl.program_id(1)

        @pl.when(j == 0)
        def _():
            dq_ref[...] = jnp.zeros_like(dq_ref)

        masks = _pair_masks()
        sub = lax.broadcasted_iota(jnp.int32, (LANES, 1), 0)
        sub_masks = [sub < HEAD_DIM, sub >= HEAD_DIM]
        row = lax.broadcasted_iota(jnp.int32, (tq, tq), 0)
        col = lax.broadcasted_iota(jnp.int32, (tq, tq), 1)
        kj = k_ref[...]
        vj = v_ref[...]
        lo_j = pl.multiple_of(j * tq, tq)

        def block(i, carry, diag):
            dk_t, dv_t, dc0, dc1 = carry
            dcs = [dc0, dc1]
            lo = pl.multiple_of(i * tq, tq)
            qi = q_ref[pl.ds(lo, tq), :]
            qt_i = qt_ref[:, pl.ds(lo, tq)]
            doi = do_ref[pl.ds(lo, tq), :]
            dot_i = dot_ref[:, pl.ds(lo, tq)]
            prod = doi.astype(bf16).astype(f32) * o_ref[pl.ds(lo, tq), :]
            lse_i = lse_ref[pl.ds(lo, tq), :]
            dq_i = jnp.zeros((tq, LANES), f32)
            for hh in range(2):
                qh = jnp.where(masks[hh], qi, jnp.zeros((), bf16))
                doh = jnp.where(masks[hh], doi, 0.0).astype(bf16)
                delta = jnp.sum(jnp.where(masks[hh], prod, 0.0), axis=-1, keepdims=True)
                sc = _dot_nt(qh, kj) * ATT_SCALE + (_tile_base(cr_ref, hh, lo) - cr_ref[hh:hh + 1, pl.ds(lo_j, tq)])
                p = jnp.exp(sc - lse_i[:, hh * HEAD_DIM:hh * HEAD_DIM + 1])
                if diag:
                    p = jnp.where(col <= row, p, 0.0)
                dv_t = dv_t + _dot(jnp.where(sub_masks[hh], dot_i, jnp.zeros((), bf16)), p.astype(bf16))
                ds = p * (_dot_nt(doh, vj) - delta)
                dcs[hh] = dcs[hh] + jnp.sum(ds, axis=0, keepdims=True)
                dsb = (ds * ATT_SCALE).astype(bf16)
                dq_i = jnp.where(masks[hh], _dot(dsb, kj), dq_i)
                dk_t = dk_t + _dot(jnp.where(sub_masks[hh], qt_i, jnp.zeros((), bf16)), dsb)
            dq_ref[pl.ds(lo, tq), :] += dq_i
            return dk_t, dv_t, dcs[0], dcs[1]

        zero = jnp.zeros((LANES, tq), f32)
        zrow = jnp.zeros((1, tq), f32)
        carry = block(j, (zero, zero, zrow, zrow), True)
        dk_t, dv_t, dc0, dc1 = lax.fori_loop(j + 1, nq, functools.partial(block, diag=False), carry)
        dk_ref[...] = dk_t.astype(bf16)
        dv_ref[...] = dv_t.astype(bf16)
        dc_ref[0:1, :] = -dc0
        dc_ref[1:2, :] = -dc1

    full = lambda p, j: (0, p)
    full_t = lambda p, j: (p, 0)
    tile = lambda p, j: (j, p)
    tile_t = lambda p, j: (p, j)
    return pl.pallas_call(
        body, grid=(FOX_PAIRS, nq),
        in_specs=[pl.BlockSpec((s, LANES), full), pl.BlockSpec((LANES, s), full_t), pl.BlockSpec((tq, LANES), tile),
                  pl.BlockSpec((tq, LANES), lambda p, j: (j, FOX_PAIRS + p)), pl.BlockSpec((s, LANES), full), pl.BlockSpec((LANES, s), full_t),
                  pl.BlockSpec((s, LANES), full), pl.BlockSpec((None, s, LANES), lambda p, j: (p, 0, 0)),
                  pl.BlockSpec((None, 2, s), lambda p, j: (p, 0, 0))],
        out_specs=[pl.BlockSpec((s, LANES), full), pl.BlockSpec((LANES, tq), tile_t), pl.BlockSpec((LANES, tq), tile_t),
                   pl.BlockSpec((None, 2, tq), lambda p, j: (p, 0, j))],
        out_shape=[SDS((s, MAIN_WIDTH), f32), SDS((MAIN_WIDTH, s), bf16), SDS((MAIN_WIDTH, s), bf16), SDS((FOX_PAIRS, 2, s), f32)],
        name=name, compiler_params=_cparams("parallel", "arbitrary"))(q, q_t, kv, kv, d_mixed, do_t, o, lse, c_row)


def adamw(w, g, m, v, name, tr=256):
    r, c = w.shape
    tr = min(tr, r)
    assert r % tr == 0, (name, r, tr)
    c1 = 1.0 / (1.0 - ADAM_B1 ** ADAM_STEP)
    c2 = 1.0 / (1.0 - ADAM_B2 ** ADAM_STEP)

    def body(w_ref, g_ref, m_ref, v_ref, d_ref, mo_ref, vo_ref):
        gv = g_ref[...]
        mn = ADAM_B1 * m_ref[...] + (1.0 - ADAM_B1) * gv
        vn = ADAM_B2 * v_ref[...] + (1.0 - ADAM_B2) * gv * gv
        mo_ref[...] = mn
        vo_ref[...] = vn
        d_ref[...] = -ADAM_LR * ((mn * c1) / (jnp.sqrt(vn * c2) + ADAM_EPS) + ADAM_WD * w_ref[...])

    spec = pl.BlockSpec((tr, c), lambda i: (i, 0))
    return pl.pallas_call(body, grid=(r // tr,), in_specs=[spec] * 4, out_specs=[spec] * 3, out_shape=[SDS((r, c), f32)] * 3,
                          name=name, compiler_params=_cparams("parallel"))(w, g, m, v)


def sum_leading(x, name, out_dtype=f32, tr=None):
    n, r, c = x.shape
    tr = tr or r
    assert r % tr == 0

    def body(x_ref, o_ref):
        acc = x_ref[0].astype(f32)
        for k in range(1, n):
            acc = acc + x_ref[k].astype(f32)
        o_ref[...] = acc.astype(out_dtype)

    return pl.pallas_call(body, grid=(r // tr,), in_specs=[pl.BlockSpec((n, tr, c), lambda i: (0, i, 0))],
                          out_specs=pl.BlockSpec((tr, c), lambda i: (i, 0)), out_shape=SDS((r, c), out_dtype), name=name,
                          compiler_params=_cparams("parallel"))(x)


_ANY = pl.BlockSpec(memory_space=pl.ANY)
_DMA = pltpu.SemaphoreType.DMA


_HBM = pl.BlockSpec(memory_space=pltpu.HBM)
_SEM = pl.BlockSpec(memory_space=pltpu.SEMAPHORE)
_EFFECT = pltpu.SideEffectType.DATAFLOW_SIDE_EFFECTING
_FLIPS = [(0, 0, 1), (1, 0, 0), (0, 1, 0), (1, 1, 0), (1, 0, 1), (0, 1, 1), (1, 1, 1)]


def _me():
    return lax.axis_index("x"), lax.axis_index("y"), lax.axis_index("c")


def _peers():
    mx, my, mc = _me()
    return [(jnp.bitwise_xor(mx, fx), jnp.bitwise_xor(my, fy), jnp.bitwise_xor(mc, fc)) for fx, fy, fc in _FLIPS]


def _index(dev):
    return 4 * dev[0] + 2 * dev[1] + dev[2]


def _win(ref, axis, k, size, count=1):
    idx = [slice(None)] * len(ref.shape)
    idx[axis] = pl.ds(k * size, count * size)
    return ref.at[tuple(idx)]


def _hbm(a):
    return pltpu.with_memory_space_constraint(a, pltpu.HBM)


def _exchange_start(srcs, lands, copies_of, name):
    n = len(srcs)

    def body(*refs):
        src = refs[:n]
        send_sems, recv_sems, self_sems = refs[2 * n:2 * n + 3]
        land = refs[3 * n + 3:4 * n + 3]
        token = refs[4 * n + 3]
        me = _index(_me())
        for a in range(n):
            for s_ref, d_ref, peer in copies_of(a, src[a], land[a], me):
                if peer is None:
                    pltpu.make_async_copy(s_ref, d_ref, self_sems.at[a]).start()
                else:
                    pltpu.make_async_remote_copy(src_ref=s_ref, dst_ref=d_ref, send_sem=send_sems.at[a], recv_sem=recv_sems.at[a],
                                                 device_id=peer, device_id_type=MESH).start()
        token[...] = jnp.zeros_like(token)

    outs = pl.pallas_call(
        body, name=name,
        out_shape=(_DMA((n,)), _DMA((n,)), _DMA((n,)), *[pltpu.HBM(s.shape, s.dtype) for s in srcs],
                   *[pltpu.HBM(l.shape, l.dtype) for l in lands], SDS((8, LANES), f32)),
        in_specs=[_HBM] * (2 * n), out_specs=(_SEM, _SEM, _SEM, *[_HBM] * (2 * n), pl.BlockSpec(memory_space=pltpu.VMEM)),
        input_output_aliases={i: 3 + i for i in range(2 * n)},
        compiler_params=pltpu.CompilerParams(has_side_effects=_EFFECT),
    )(*[_hbm(s) for s in srcs], *[_hbm(lax.empty(l.shape, l.dtype)) for l in lands])
    return dict(sems=outs[:3], srcs=list(outs[3:3 + n]), lands=list(outs[3 + n:3 + 2 * n]), token=outs[3 + 2 * n])


def _exchange_wait(started, waits_of, after, name):
    srcs, lands = started["srcs"], started["lands"]
    n = len(srcs)

    def body(*refs):
        src = refs[:n]
        land = refs[n:2 * n]
        send_sems, recv_sems, self_sems = refs[2 * n:2 * n + 3]
        me = _index(_me())
        for a in range(n):
            seven, (s_ref, d_ref) = waits_of(a, src[a], land[a], me)
            both = pltpu.make_async_remote_copy(src_ref=seven, dst_ref=seven, send_sem=send_sems.at[a], recv_sem=recv_sems.at[a],
                                                device_id=_me(), device_id_type=MESH)
            both.wait_send()
            both.wait_recv()
            pltpu.make_async_copy(s_ref, d_ref, self_sems.at[a]).wait()

    outs = pl.pallas_call(
        body, name=name, out_shape=tuple(pltpu.HBM(t.shape, t.dtype) for t in srcs + lands),
        in_specs=[_HBM] * (2 * n) + [_SEM] * 3 + [_ANY], out_specs=tuple([_HBM] * (2 * n)),
        input_output_aliases={i: i for i in range(2 * n)},
        compiler_params=pltpu.CompilerParams(has_side_effects=_EFFECT),
    )(*srcs, *lands, *started["sems"], after)
    return list(outs[n:])


def gather_start(locs, axes, name):
    lands = [SDS(tuple(N_DEV * d if i == ax else d for i, d in enumerate(l.shape)), l.dtype) for l, ax in zip(locs, axes)]

    def copies_of(a, src, land, me):
        mine = _win(land, axes[a], me, src.shape[axes[a]])
        return [(src, mine, peer) for peer in _peers()] + [(src, mine, None)]

    return _exchange_start(locs, lands, copies_of, name)


def gather_wait(started, axes, after, name):
    def waits_of(a, src, land, me):
        size = src.shape[axes[a]]
        return _win(land, axes[a], 0, size, N_DEV - 1), (src, _win(land, axes[a], me, size))

    return _exchange_wait(started, waits_of, after, name)


def scatter_start(grads, axes, name):
    lands = [SDS((N_DEV,) + tuple(d // N_DEV if i == ax else d for i, d in enumerate(g.shape)), g.dtype) for g, ax in zip(grads, axes)]

    def copies_of(a, src, land, me):
        size = src.shape[axes[a]] // N_DEV
        out = [(_win(src, axes[a], _index(peer), size), land.at[me], peer) for peer in _peers()]
        return out + [(_win(src, axes[a], me, size), land.at[me], None)]

    return _exchange_start(grads, lands, copies_of, name)


def scatter_wait(started, axes, after, name):
    def waits_of(a, src, land, me):
        size = src.shape[axes[a]] // N_DEV
        return land.at[pl.ds(0, N_DEV - 1)], (_win(src, axes[a], me, size), land.at[me])

    return _exchange_wait(started, waits_of, after, name)


def _row_tile(rows, cap=512):
    return max(t for t in range(8, min(rows, cap) + 1, 8) if rows % t == 0)


_SMALL = [
    ("ln_mix_pre", (2, 1024)), ("ln_mix_post", (2, 1024)), ("ln_ffn_pre", (2, 1024)), ("ln_ffn_post", (2, 1024)),
    ("ln_mem", (2, 1024)), ("w_spatial", (1, 6, 128, 128)), ("b_spatial", (1, 6, 128)), ("ln_shared", (1024,)),
    ("b_forget", (12,)), ("ln_v_g", (1, 768)), ("ln_v_b", (1, 768)),
]
_SMALL_TILE = 8 * LANES


def _small_rows(shape):
    return -(-math.prod(shape) // _SMALL_TILE) * 8


def _pack_small(vals, shapes):
    parts = []
    for name, shape in shapes:
        flat = vals[name].reshape(-1).astype(f32)
        rows = _small_rows(shape)
        parts.append(jnp.pad(flat, (0, rows * LANES - flat.shape[0])).reshape(rows, LANES))
    return jnp.concatenate(parts, axis=0)


def _unpack_small(buf, shapes):
    out = {}
    lo = 0
    for name, shape in shapes:
        rows = _small_rows(shape)
        out[name] = buf[lo:lo + rows].reshape(-1)[:math.prod(shape)].reshape(shape)
        lo += rows
    return out


def kernel(x, mem, ln_mix_pre, ln_mix_post, ln_ffn_pre, ln_ffn_post, ln_mem, w_mem_kv, w_out, w_ffn_gate, w_ffn_up, w_ffn_down, w_in_a, w_spatial, b_spatial, ln_v_g, ln_v_b, ln_shared, w_shared_kv, b_forget, w_in_b, loss_target, m_ln_mix_pre, m_ln_mix_post, m_ln_ffn_pre, m_ln_ffn_post, m_ln_mem, m_w_mem_kv, m_w_out, m_w_ffn_gate, m_w_ffn_up, m_w_ffn_down, m_w_in_a, m_w_spatial, m_b_spatial, m_ln_v_g, m_ln_v_b, m_ln_shared, m_w_shared_kv, m_b_forget, m_w_in_b, v_ln_mix_pre, v_ln_mix_post, v_ln_ffn_pre, v_ln_ffn_post, v_ln_mem, v_w_mem_kv, v_w_out, v_w_ffn_gate, v_w_ffn_up, v_w_ffn_down, v_w_in_a, v_w_spatial, v_b_spatial, v_ln_v_g, v_ln_v_b, v_ln_shared, v_w_shared_kv, v_b_forget, v_w_in_b):
    weights = dict(ln_mix_pre=ln_mix_pre, ln_mix_post=ln_mix_post, ln_ffn_pre=ln_ffn_pre, ln_ffn_post=ln_ffn_post, ln_mem=ln_mem,
                   w_mem_kv=w_mem_kv, w_out=w_out, w_ffn_gate=w_ffn_gate, w_ffn_up=w_ffn_up, w_ffn_down=w_ffn_down, w_in_a=w_in_a,
                   w_spatial=w_spatial, b_spatial=b_spatial, ln_v_g=ln_v_g, ln_v_b=ln_v_b, ln_shared=ln_shared,
                   w_shared_kv=w_shared_kv, b_forget=b_forget, w_in_b=w_in_b)
    mom_m = dict(ln_mix_pre=m_ln_mix_pre, ln_mix_post=m_ln_mix_post, ln_ffn_pre=m_ln_ffn_pre, ln_ffn_post=m_ln_ffn_post, ln_mem=m_ln_mem,
                 w_mem_kv=m_w_mem_kv, w_out=m_w_out, w_ffn_gate=m_w_ffn_gate, w_ffn_up=m_w_ffn_up, w_ffn_down=m_w_ffn_down, w_in_a=m_w_in_a,
                 w_spatial=m_w_spatial, b_spatial=m_b_spatial, ln_v_g=m_ln_v_g, ln_v_b=m_ln_v_b, ln_shared=m_ln_shared,
                 w_shared_kv=m_w_shared_kv, b_forget=m_b_forget, w_in_b=m_w_in_b)
    mom_v = dict(ln_mix_pre=v_ln_mix_pre, ln_mix_post=v_ln_mix_post, ln_ffn_pre=v_ln_ffn_pre, ln_ffn_post=v_ln_ffn_post, ln_mem=v_ln_mem,
                 w_mem_kv=v_w_mem_kv, w_out=v_w_out, w_ffn_gate=v_w_ffn_gate, w_ffn_up=v_w_ffn_up, w_ffn_down=v_w_ffn_down, w_in_a=v_w_in_a,
                 w_spatial=v_w_spatial, b_spatial=v_b_spatial, ln_v_g=v_ln_v_g, ln_v_b=v_ln_v_b, ln_shared=v_ln_shared,
                 w_shared_kv=v_w_shared_kv, b_forget=v_b_forget, w_in_b=v_w_in_b)
    names = list(weights)
    mx, my, mc = lax.axis_index("x"), lax.axis_index("y"), lax.axis_index("c")
    me = 4 * mx + 2 * my + mc

    h0 = x[0]
    mem0 = mem[0]
    tgt = loss_target[0]
    seq = h0.shape[0]

    vec = lambda a: a.reshape(1, -1)
    pad_to = lambda a, axis, size: jnp.pad(a, [(0, size - a.shape[i] if i == axis else 0) for i in range(a.ndim)])

    def after(tok, a):
        return a + tok[0, 0].astype(a.dtype)

    lnv_loc = pad_to(jnp.concatenate([ln_v_g, ln_v_b], axis=0), 0, 8)
    st_a = gather_start([w_in_a.astype(bf16), pad_to(lnv_loc, 1, LANES)[None]], [0, 0], "gather_a_start")
    mix_locs = lambda l, tok: [after(tok, w_mem_kv[l]).astype(bf16), w_out[l].astype(bf16)]
    def ffn_gather_start(l, tok):
        gate_up = gather_start([pad_to(after(tok, w_ffn_gate[l]).astype(bf16), 1, FF_SHARD_PAD),
                                pad_to(w_ffn_up[l].astype(bf16), 1, FF_SHARD_PAD)], [1, 1], f"gather_gate_up{l}_start")
        down = gather_start([pad_to(after(gate_up["token"], w_ffn_down[l]).astype(bf16), 0, FF_SHARD_PAD)], [0], f"gather_down{l}_start")
        return gate_up, down

    st_b = [gather_start(mix_locs(0, st_a["token"]), [0, 0], "gather_b0_start"), None]
    st_c = ffn_gather_start(0, st_b[0]["token"])
    st_d = gather_start([after(st_c[1]["token"], w_in_b[0]).astype(bf16), pad_to(w_shared_kv.astype(bf16), 1, KV_PAD)], [0, 0],
                        "gather_d_start")
    st_b[1] = gather_start(mix_locs(1, st_d["token"]), [0, 0], "gather_b1_start")
    st_e = ffn_gather_start(1, st_b[1]["token"])
    ws = w_spatial[0].astype(bf16)
    ws_t = ws.transpose(0, 2, 1)
    bs_t = b_spatial[0].T

    (a0,) = rms_fwd(h0, [after(st_e[1]["token"], vec(ln_mix_pre[0]))], "a0_norm")
    w_in_a8, lnv8 = gather_wait(st_a, [0, 0], a0, "gather_a_wait")
    w_in_a_full = w_in_a8.transpose(1, 0, 2).reshape(D_MODEL, -1)
    lnv_g = lnv8[:, 0, :MAIN_WIDTH // N_DEV].reshape(1, MAIN_WIDTH)
    lnv_b = lnv8[:, 1, :MAIN_WIDTH // N_DEV].reshape(1, MAIN_WIDTH)
    proj0 = mm(a0, w_in_a_full, "proj0", tn=896)
    main0 = gmlp_fwd(proj0, ws, bs_t, lnv_g, lnv_b, "gmlp_fwd")
    w_mkv, w_o = [None, None], [None, None]
    w_mkv[0], w_o[0] = gather_wait(st_b[0], [0, 0], main0, "gather_b0_wait")
    (memn0,) = rms_fwd(mem0, [vec(ln_mem[0])], "mem0_norm")
    kvm0 = mm(memn0, w_mkv[0], "kvm0")
    om0 = mem_attn_fwd(proj0, 2 * MAIN_WIDTH // MEM_WIDTH, kvm0, "mem_attn0")
    mixed0 = jnp.concatenate([main0, om0], axis=-1)
    y1_0, hmid0, f0 = mm_resnorm(mixed0, w_o[0], h0, vec(ln_mix_post[0]), [vec(ln_ffn_pre[0])], "mix_out0")
    w_g0, w_u0 = gather_wait(st_c[0], [1, 1], f0, "gather_gate_up0_wait")
    gu0, act0 = ffn_up(f0, w_g0, w_u0, "ffn_up0")
    (w_d0,) = gather_wait(st_c[1], [0], act0, "gather_down0_wait")
    y2_0, h1, a1, sin1 = mm_resnorm(act0, w_d0, hmid0, vec(ln_ffn_post[0]), [vec(ln_mix_pre[1]), vec(ln_shared)], "ffn_down0")

    w_inb, w_kv = gather_wait(st_d, [0, 0], sin1, "gather_d_wait")
    kvb = mm(sin1, w_kv, "kv_shared", out_dtype=bf16, tn=MAIN_WIDTH, ncols=2 * MAIN_WIDTH)
    zf = mm(sin1, w_kv, "forget_logits", tn=256, col0=2 * MAIN_WIDTH, ncols=256)
    qb = mm(a1, w_inb, "proj1", out_dtype=bf16)
    z_t = jnp.pad(zf[:, :FOX_HEADS].T, ((0, 16 - FOX_HEADS), (0, 0)))
    bf_col = jnp.pad(b_forget, (0, 16 - FOX_HEADS)).reshape(16, 1)
    c_t = fgate_fwd(z_t, bf_col, "fgate_fwd")
    c_row = c_t[:FOX_HEADS].reshape(FOX_PAIRS, 2, seq)
    main1, lse = fox_fwd(qb, kvb, c_row, "fox_fwd")
    w_mkv[1], w_o[1] = gather_wait(st_b[1], [0, 0], main1, "gather_b1_wait")
    (memn1,) = rms_fwd(mem0, [vec(ln_mem[1])], "mem1_norm")
    kvm1 = mm(memn1, w_mkv[1], "kvm1")
    om1 = mem_attn_fwd(qb, MAIN_WIDTH // MEM_WIDTH, kvm1, "mem_attn1")
    mixed1 = jnp.concatenate([main1.astype(bf16), om1], axis=-1)
    y1_1, hmid1, f1 = mm_resnorm(mixed1, w_o[1], h1, vec(ln_mix_post[1]), [vec(ln_ffn_pre[1])], "mix_out1")
    w_g1, w_u1 = gather_wait(st_e[0], [1, 1], f1, "gather_gate_up1_wait")
    gu1, act1 = ffn_up(f1, w_g1, w_u1, "ffn_up1")
    (w_d1,) = gather_wait(st_e[1], [0], act1, "gather_down1_wait")
    y2_1, h2 = mm_resnorm(act1, w_d1, hmid1, vec(ln_ffn_post[1]), [], "ffn_down1")
    dh, loss_tile = loss_grad(h2, tgt, "loss")
    ffn_w = [(w_g0, w_u0, w_d0), (w_g1, w_u1, w_d1)]

    small = {}

    def ffn_backward(layer, dh_out, y2, hmid, f, gu, act, y1):
        w_g, w_u, w_d = ffn_w[layer]
        d_y2, dg_fpost = rms_bwd(y2, vec(ln_ffn_post[layer]), dh_out, None, bf16, f"ffn_post_bwd{layer}")
        dw_down = mm_tn(act, d_y2, f"dw_down{layer}", tk=256)
        rs_down = scatter_start([dw_down], [0], f"scatter_down{layer}_start")
        d_g, d_u = ffn_act_grad(d_y2, w_d, gu, f"ffn_act_grad{layer}")
        dw_g = mm_tn(f, d_g, f"dw_gate{layer}", dep=rs_down["token"])
        dw_u = mm_tn(f, d_u, f"dw_up{layer}")
        rs_gate_up = scatter_start([dw_g, dw_u], [1, 1], f"scatter_gate_up{layer}_start")
        dh_mid, d_y1, dg_fpre, dg_mpost = ffn_in_grad(d_g, d_u, w_g, w_u, hmid, dh_out, after(rs_gate_up["token"], vec(ln_ffn_pre[layer])),
                                                      y1, vec(ln_mix_post[layer]), f"ffn_in_grad{layer}")
        return dh_mid, d_y1, dg_fpost, dg_fpre, dg_mpost, (rs_down, rs_gate_up)

    def mix_out_backward(layer, d_y1, mixed):
        dw_out = mm_tn(mixed, d_y1, f"dw_out{layer}")
        d_mixed = mm(d_y1, w_o[layer], f"d_mixed{layer}", trans_b=True)
        return d_mixed, dw_out

    def mem_backward(layer, q_src, q_block, kvm, memn, d_mixed):
        d_qm, d_kvm = mem_attn_bwd(q_src, q_block, kvm, d_mixed, f"mem_attn_bwd{layer}")
        d_kvm_b = d_kvm.astype(bf16)
        dw_mkv = mm_tn(memn, d_kvm_b, f"dw_mem_kv{layer}")
        d_memn = mm(d_kvm_b, w_mkv[layer], f"d_memn{layer}", trans_b=True)
        _, dg_mem = rms_bwd(mem0, vec(ln_mem[layer]), d_memn, None, bf16, f"mem_norm_bwd{layer}")
        return d_qm, dw_mkv, dg_mem


    dh_mid1, d_y1_1, dg_fpost1, dg_fpre1, dg_mpost1, rs_ffn1 = ffn_backward(1, dh, y2_1, hmid1, f1, gu1, act1, y1_1)
    d_mixed1, dw_out1 = mix_out_backward(1, d_y1_1, mixed1)
    d_qm1, dw_mkv1, dg_mem1 = mem_backward(1, qb, MAIN_WIDTH // MEM_WIDTH, kvm1, memn1, d_mixed1)
    rs_mix1 = scatter_start([dw_out1, dw_mkv1], [0, 0], "scatter_mix1_start")
    q_t = qb[:, :MAIN_WIDTH].T
    do_t = d_mixed1[:, :MAIN_WIDTH].astype(bf16).T
    dq, dk_t, dv_t, dc = fox_bwd(qb, q_t, kvb, d_mixed1, do_t, main1, lse, after(rs_mix1["token"], c_row), "fox_bwd")
    dc_t = jnp.pad(dc.reshape(FOX_HEADS, seq), ((0, 16 - FOX_HEADS), (0, 0)))
    dz_t, db_f = fgate_bwd(dc_t, z_t, bf_col, "fgate_bwd")
    d_kvf = jnp.concatenate([dk_t.T, dv_t.T, jnp.pad(dz_t[:FOX_HEADS].T.astype(bf16), ((0, 0), (0, KV_PAD - KV_WIDTH)))], axis=-1)
    d_proj1 = jnp.concatenate([dq.astype(bf16), d_qm1], axis=-1)
    dw_in_b = mm_tn(a1, d_proj1, "dw_in_b")
    dw_kv = mm_tn(sin1, d_kvf, "dw_kv", tn=896)
    rs_2 = scatter_start([dw_in_b, dw_kv], [0, 0], "scatter_shared_start")
    dh1, (dg_pre1, dg_shared) = proj_in_grad([(d_proj1, w_inb, vec(ln_mix_pre[1])), (d_kvf, w_kv, vec(ln_shared))], h1, dh_mid1,
                                             "in_grad1", dep=rs_2["token"])

    dh_mid0, d_y1_0, dg_fpost0, dg_fpre0, dg_mpost0, rs_ffn0 = ffn_backward(0, dh1, y2_0, hmid0, f0, gu0, act0, y1_0)
    d_mixed0, dw_out0 = mix_out_backward(0, d_y1_0, mixed0)
    d_qm0, dw_mkv0, dg_mem0 = mem_backward(0, proj0, 2 * MAIN_WIDTH // MEM_WIDTH, kvm0, memn0, d_mixed0)
    rs_mix0 = scatter_start([dw_out0, dw_mkv0], [0, 0], "scatter_mix0_start")
    d_uv, dw_s, db_s, dg_lnv, db_lnv = gmlp_bwd(proj0, d_mixed0, ws, ws_t, bs_t, after(rs_mix0["token"], lnv_g), lnv_b, "gmlp_bwd")

    small["ln_mix_pre"] = jnp.concatenate([jnp.zeros_like(dg_pre1), dg_pre1], axis=0)
    small["ln_mix_post"] = jnp.concatenate([dg_mpost0, dg_mpost1], axis=0)
    small["ln_ffn_pre"] = jnp.concatenate([dg_fpre0, dg_fpre1], axis=0)
    small["ln_ffn_post"] = jnp.concatenate([dg_fpost0, dg_fpost1], axis=0)
    small["ln_mem"] = jnp.concatenate([dg_mem0, dg_mem1], axis=0)
    small["w_spatial"] = dw_s[None]
    small["b_spatial"] = db_s[:, :A_GROUPS].T[None]
    small["ln_shared"] = dg_shared[0]
    small["b_forget"] = db_f[:FOX_HEADS, 0]
    small["ln_v_g"] = dg_lnv
    small["ln_v_b"] = db_lnv
    small_rows = jnp.concatenate([_pack_small(small, _SMALL), loss_tile], axis=0)
    st_small = gather_start([small_rows[None]], [0], "gather_small_grads_start")
    d_proj0 = jnp.concatenate([d_uv, after(st_small["token"], d_qm0)], axis=-1)
    dw_in_a = mm_tn(a0, d_proj0, "dw_in_a", tn=896)
    rs_in_a = scatter_start([dw_in_a.reshape(D_MODEL, N_DEV, -1).transpose(1, 0, 2)], [0], "scatter_in_a_start")
    grad_x, (dg_pre0,) = proj_in_grad([(d_proj0, w_in_a_full, vec(ln_mix_pre[0]))], h0, dh_mid0, "in_grad0", dep=rs_in_a["token"])
    st_last = gather_start([dg_pre0.reshape(1, 8, LANES)], [0], "gather_last_grad_start")

    def owned(started, axes, wait_after, name):
        recv = scatter_wait(started, axes, wait_after, name)
        return [sum_leading(r.reshape((N_DEV, -1, r.shape[-1])), f"{name}_sum{i}", tr=_row_tile(math.prod(r.shape[1:-1])))
                for i, r in enumerate(recv)]

    (g_down1,) = owned(rs_ffn1[0], [0], after(st_last["token"], grad_x[:8, :LANES]), "scatter_down1_wait")
    g_gu1 = owned(rs_ffn1[1], [1, 1], g_down1, "scatter_gate_up1_wait")
    g_mix1 = owned(rs_mix1, [0, 0], g_gu1[0], "scatter_mix1_wait")
    g2 = owned(rs_2, [0, 0], g_mix1[0], "scatter_shared_wait")
    (g_down0,) = owned(rs_ffn0[0], [0], g2[0], "scatter_down0_wait")
    g_gu0 = owned(rs_ffn0[1], [1, 1], g_down0, "scatter_gate_up0_wait")
    g_mix0 = owned(rs_mix0, [0, 0], g_gu0[0], "scatter_mix0_wait")
    (g_in_a,) = owned(rs_in_a, [0], g_mix0[0], "scatter_in_a_wait")
    g_local = dict(
        w_ffn_gate=jnp.stack([g_gu0[0], g_gu1[0]])[:, :, :FF_SHARD], w_ffn_up=jnp.stack([g_gu0[1], g_gu1[1]])[:, :, :FF_SHARD],
        w_ffn_down=jnp.stack([g_down0, g_down1])[:, :FF_SHARD], w_out=jnp.stack([g_mix0[0], g_mix1[0]]),
        w_mem_kv=jnp.stack([g_mix0[1], g_mix1[1]]), w_in_b=g2[0][None], w_shared_kv=g2[1][:, :KV_WIDTH], w_in_a=g_in_a[None])
    (small_all,) = gather_wait(st_small, [0], g_in_a, "gather_small_grads_wait")
    (last_all,) = gather_wait(st_last, [0], small_all, "gather_last_grad_wait")
    small_sum = sum_leading(small_all, "sum_small_grads")
    loss = small_sum[small_rows.shape[0] - 1, 0]
    g_small = _unpack_small(small_sum, _SMALL)
    g_small["ln_mix_pre"] = jnp.concatenate([sum_leading(last_all, "sum_last_grad").reshape(1, D_MODEL), g_small["ln_mix_pre"][1:]], axis=0)
    shard = MAIN_WIDTH // N_DEV
    for n in ("ln_v_g", "ln_v_b"):
        g_small[n] = lax.dynamic_slice_in_dim(g_small[n], me * shard, shard, axis=1)
    grad_w = {**g_small, **g_local}

    delta, new_m, new_v = {}, {}, {}
    for n in g_local:
        two_d = (-1, weights[n].shape[-1])
        d_, m_, v_ = adamw(weights[n].reshape(two_d), grad_w[n].reshape(two_d), mom_m[n].reshape(two_d), mom_v[n].reshape(two_d),
                           f"adamw_{n}", tr=_row_tile(math.prod(weights[n].shape[:-1])))
        delta[n], new_m[n], new_v[n] = (t.reshape(weights[n].shape) for t in (d_, m_, v_))
    small_local_shapes = [(n, tuple(weights[n].shape)) for n, _ in _SMALL]
    packed = [_pack_small(src, small_local_shapes) for src in (weights, grad_w, mom_m, mom_v)]
    outs = adamw(*packed, "adamw_small", tr=packed[0].shape[0])
    for dst, buf in zip((delta, new_m, new_v), outs):
        dst.update(_unpack_small(buf, small_local_shapes))

    return (loss, grad_x[None], *[grad_w[n] for n in names], *[delta[n] for n in names],
            *[new_m[n] for n in names], *[new_v[n] for n in names])
```

```python
import functools
import math

import jax
import jax.numpy as jnp
from jax import lax
from jax.experimental import pallas as pl
from jax.experimental.pallas import tpu as pltpu

f32 = jnp.float32
bf16 = jnp.bfloat16
SDS = jax.ShapeDtypeStruct

D_MODEL = 1024
MAIN_WIDTH = 768
MEM_WIDTH = 256
HEAD_DIM = 64
MEM_HEADS = 4
FOX_HEADS = 12
FOX_PAIRS = FOX_HEADS // 2
CHUNK = 128
A_GROUPS = 6
FF_SHARD = 352
FF_SHARD_PAD = 384
FF_PAD = 8 * FF_SHARD_PAD
KV_WIDTH = 2 * MAIN_WIDTH + FOX_HEADS
KV_PAD = 1792
RMS_EPS = 1e-6
LN_EPS = 1e-5
ATT_SCALE = HEAD_DIM ** -0.5
ADAM_LR, ADAM_B1, ADAM_B2, ADAM_EPS, ADAM_WD, ADAM_STEP = 0.001, 0.9, 0.999, 1e-08, 0.01, 10
N_DEV = 8
AXES = ("x", "y", "c")
MESH = pl.DeviceIdType.MESH
V7X_VMEM_LIMIT = 56 * 1024 * 1024
LANES = 128
FLAT_W = 512
ROW_PAD = 16


def _cparams(*sem):
    return pltpu.CompilerParams(dimension_semantics=sem or None, vmem_limit_bytes=V7X_VMEM_LIMIT)


def _dot(a, b):
    return jnp.dot(a, b, preferred_element_type=f32)


def _dot_nt(a, b):
    return lax.dot_general(a, b, (((1,), (1,)), ((), ())), preferred_element_type=f32)


def _dot_tn(a, b):
    return lax.dot_general(a, b, (((0,), (0,)), ((), ())), preferred_element_type=f32)


def _gelu(x):
    k = math.sqrt(2.0 / math.pi)
    t = jnp.tanh(k * (x + 0.044715 * x * x * x))
    return 0.5 * x * (1.0 + t), t


def _gelu_grad(x, t):
    k = math.sqrt(2.0 / math.pi)
    return 0.5 * (1.0 + t) + 0.5 * x * (1.0 - t * t) * k * (1.0 + 3.0 * 0.044715 * x * x)


def _sigmoid(x):
    return 1.0 / (1.0 + jnp.exp(-x))


def rms_fwd(x, gains, name, tm=512):
    m, d = x.shape
    tm = min(tm, m)
    n = len(gains)

    def body(x_ref, *refs):
        xv = x_ref[...]
        y = xv * lax.rsqrt(jnp.sum(xv * xv, axis=-1, keepdims=True) * (1.0 / d) + RMS_EPS)
        for g_ref, o_ref in zip(refs[:n], refs[n:]):
            o_ref[...] = (y * g_ref[...]).astype(bf16)

    row = pl.BlockSpec((tm, d), lambda i: (i, 0))
    vec = pl.BlockSpec((1, d), lambda i: (0, 0))
    return pl.pallas_call(body, grid=(m // tm,), in_specs=[row] + [vec] * n, out_specs=[row] * n,
                          out_shape=[SDS((m, d), bf16)] * n, name=name, compiler_params=_cparams("parallel"))(x, *gains)


def rms_bwd(x, g, dy, add, out_dtype, name, tm=512):
    m, d = x.shape
    tm = min(tm, m)
    has_add = add is not None

    def body(x_ref, g_ref, dy_ref, *refs):
        dx_ref, dg_ref = refs[-2], refs[-1]
        xv = x_ref[...]
        dyv = dy_ref[...].astype(f32)
        r = lax.rsqrt(jnp.sum(xv * xv, axis=-1, keepdims=True) * (1.0 / d) + RMS_EPS)
        xn = xv * r
        dyg = dyv * g_ref[...]
        dx = r * (dyg - xn * (jnp.sum(dyg * xn, axis=-1, keepdims=True) * (1.0 / d)))
        if has_add:
            dx = dx + refs[0][...]
        dx_ref[...] = dx.astype(out_dtype)

        @pl.when(pl.program_id(0) == 0)
        def _():
            dg_ref[...] = jnp.zeros_like(dg_ref)

        dg_ref[...] += jnp.sum(dyv * xn, axis=0, keepdims=True)

    row = pl.BlockSpec((tm, d), lambda i: (i, 0))
    vec = pl.BlockSpec((1, d), lambda i: (0, 0))
    ins = [x, g, dy] + ([add] if has_add else [])
    return pl.pallas_call(body, grid=(m // tm,), in_specs=[row, vec, row] + ([row] if has_add else []),
                          out_specs=[row, vec], out_shape=[SDS((m, d), out_dtype), SDS((1, d), f32)], name=name,
                          compiler_params=_cparams("arbitrary"))(*ins)


def loss_grad(h, tgt, name, tm=512):
    m, d = h.shape

    def body(h_ref, t_ref, dy_ref, l_ref):
        e = h_ref[...] - t_ref[...]
        dy_ref[...] = e * (1.0 / d)

        @pl.when(pl.program_id(0) == 0)
        def _():
            l_ref[...] = jnp.zeros_like(l_ref)

        part = jnp.sum(jnp.sum(e * e, axis=-1, keepdims=True), axis=0, keepdims=True) * (0.5 / d)
        l_ref[...] += jnp.broadcast_to(part, l_ref.shape)

    row = pl.BlockSpec((tm, d), lambda i: (i, 0))
    return pl.pallas_call(body, grid=(m // tm,), in_specs=[row, row],
                          out_specs=[row, pl.BlockSpec((8, LANES), lambda i: (0, 0))],
                          out_shape=[SDS((m, d), f32), SDS((8, LANES), f32)], name=name,
                          compiler_params=_cparams("arbitrary"))(h, tgt)


def mm(a, b, name, trans_b=False, out_dtype=f32, tm=512, tn=1024, layer=None, col0=0, ncols=None, dep=None):
    m, k = a.shape
    n_all = b.shape[-2] if trans_b else b.shape[-1]
    n = n_all if ncols is None else ncols
    tm, tn = min(tm, m), min(tn, n)
    assert m % tm == 0 and n % tn == 0 and col0 % tn == 0 and not (trans_b and col0), (name, m, n, tm, tn)
    jb = col0 // tn
    lead = () if layer is None else (None,)
    sel = () if layer is None else (layer,)

    def body(a_ref, b_ref, *rest):
        r = _dot_nt(a_ref[...], b_ref[...]) if trans_b else _dot(a_ref[...], b_ref[...])
        rest[-1][...] = r.astype(out_dtype)

    if trans_b:
        b_spec = pl.BlockSpec(lead + (tn, k), lambda j, i: sel + (j, 0))
    else:
        b_spec = pl.BlockSpec(lead + (k, tn), lambda j, i: sel + (0, jb + j))
    deps = [] if dep is None else [dep]
    dep_specs = [pl.BlockSpec((8, LANES), lambda j, i: (0, 0))] * len(deps)
    return pl.pallas_call(body, grid=(n // tn, m // tm), in_specs=[pl.BlockSpec((tm, k), lambda j, i: (i, 0)), b_spec] + dep_specs,
                          out_specs=pl.BlockSpec((tm, tn), lambda j, i: (i, j)), out_shape=SDS((m, n), out_dtype),
                          name=name, compiler_params=_cparams("parallel", "parallel"))(a, b, *deps)


def mm_tn(a, g, name, tk=512, tn=1024, out_dtype=bf16, dep=None):
    s, k = a.shape
    n = g.shape[1]
    tk, tn = min(tk, k), min(tn, n)
    assert k % tk == 0 and n % tn == 0, (name, k, n, tk, tn)

    def body(a_ref, g_ref, *rest):
        rest[-1][...] = _dot_tn(a_ref[...], g_ref[...]).astype(out_dtype)

    deps = [] if dep is None else [dep]
    dep_specs = [pl.BlockSpec((8, LANES), lambda i, j: (0, 0))] * len(deps)
    return pl.pallas_call(body, grid=(k // tk, n // tn),
                          in_specs=[pl.BlockSpec((s, tk), lambda i, j: (0, i)), pl.BlockSpec((s, tn), lambda i, j: (0, j))] + dep_specs,
                          out_specs=pl.BlockSpec((tk, tn), lambda i, j: (i, j)), out_shape=SDS((k, n), out_dtype), name=name,
                          compiler_params=_cparams("parallel", "parallel"))(a, g, *deps)


def _resident(shape, index_map):
    return pl.BlockSpec(shape, index_map, pipeline_mode=pl.Buffered(1))


def _rms(xv):
    return xv * lax.rsqrt(jnp.sum(xv * xv, axis=-1, keepdims=True) * (1.0 / xv.shape[-1]) + RMS_EPS)


def _rms_bwd_math(xv, g, dy):
    d = xv.shape[-1]
    r = lax.rsqrt(jnp.sum(xv * xv, axis=-1, keepdims=True) * (1.0 / d) + RMS_EPS)
    xn = xv * r
    dyg = dy * g
    dx = r * (dyg - xn * (jnp.sum(dyg * xn, axis=-1, keepdims=True) * (1.0 / d)))
    return dx, jnp.sum(dy * xn, axis=0, keepdims=True)


SUB_ROWS = 256


def mm_resnorm(a, b, h, g_post, gains, name, tm=512):
    m, k = a.shape
    d = b.shape[1]
    n = len(gains)

    def body(a_ref, b_ref, h_ref, gp_ref, *refs):
        for r in range(tm // SUB_ROWS):
            rows = slice(r * SUB_ROWS, (r + 1) * SUB_ROWS)
            y = _dot(a_ref[rows, :], b_ref[...])
            refs[n][rows, :] = y
            hn = h_ref[rows, :] + _rms(y) * gp_ref[...]
            refs[n + 1][rows, :] = hn
            if n:
                z = _rms(hn)
                for g_ref, o_ref in zip(refs[:n], refs[n + 2:]):
                    o_ref[rows, :] = (z * g_ref[...]).astype(bf16)

    row = pl.BlockSpec((tm, d), lambda i: (i, 0))
    vec = pl.BlockSpec((1, d), lambda i: (0, 0))
    return pl.pallas_call(body, grid=(m // tm,),
                          in_specs=[pl.BlockSpec((tm, k), lambda i: (i, 0)), _resident((k, d), lambda i: (0, 0)), row, vec] + [vec] * n,
                          out_specs=[row] * (n + 2), out_shape=[SDS((m, d), f32)] * 2 + [SDS((m, d), bf16)] * n, name=name,
                          compiler_params=_cparams("parallel"))(a, b, h, g_post, *gains)


def ffn_act_grad(d_y2, w_d, gu, name, tm=512, tn=1536):
    s, d = d_y2.shape
    ff = w_d.shape[0]
    nb = ff // tn

    def body(a_ref, b_ref, g_ref, u_ref, dg_ref, du_ref):
        av = a_ref[...]
        tc = 256
        for c in range(tn // tc):
            cols = slice(c * tc, (c + 1) * tc)
            da = _dot_nt(av, b_ref[cols, :])
            gg = g_ref[:, cols].astype(f32)
            sg = _sigmoid(gg)
            dg_ref[:, cols] = (da * u_ref[:, cols].astype(f32) * (sg * (1.0 + gg * (1.0 - sg)))).astype(bf16)
            du_ref[:, cols] = (da * gg * sg).astype(bf16)

    tile = pl.BlockSpec((tm, tn), lambda j, i: (i, j))
    return pl.pallas_call(body, grid=(nb, s // tm),
                          in_specs=[pl.BlockSpec((tm, d), lambda j, i: (i, 0)), pl.BlockSpec((tn, d), lambda j, i: (j, 0)), tile,
                                    pl.BlockSpec((tm, tn), lambda j, i: (i, nb + j))],
                          out_specs=[tile, tile], out_shape=[SDS((s, ff), bf16)] * 2, name=name,
                          compiler_params=_cparams("parallel", "parallel"))(d_y2, w_d, gu, gu)


def ffn_in_grad(d_g, d_u, w_g, w_u, hmid, dh_out, g_pre, y1, g_post, name, tm=512):
    s, ff = d_g.shape
    d = w_g.shape[0]

    def body(dg_ref, du_ref, wg_ref, wu_ref, hm_ref, dho_ref, gpre_ref, y1_ref, gpost_ref, dhm_ref, dy1_ref, dgpre_ref, dgpost_ref):
        @pl.when(pl.program_id(0) == 0)
        def _():
            dgpre_ref[...] = jnp.zeros_like(dgpre_ref)
            dgpost_ref[...] = jnp.zeros_like(dgpost_ref)

        for r in range(tm // SUB_ROWS):
            rows = slice(r * SUB_ROWS, (r + 1) * SUB_ROWS)
            d_f = _dot_nt(dg_ref[rows, :], wg_ref[...]) + _dot_nt(du_ref[rows, :], wu_ref[...])
            dx, dg1 = _rms_bwd_math(hm_ref[rows, :], gpre_ref[...], d_f)
            dh_mid = dho_ref[rows, :] + dx
            dhm_ref[rows, :] = dh_mid
            dgpre_ref[...] += dg1
            dy1, dg2 = _rms_bwd_math(y1_ref[rows, :], gpost_ref[...], dh_mid)
            dy1_ref[rows, :] = dy1.astype(bf16)
            dgpost_ref[...] += dg2

    row = pl.BlockSpec((tm, d), lambda i: (i, 0))
    vec = pl.BlockSpec((1, d), lambda i: (0, 0))
    wide = pl.BlockSpec((tm, ff), lambda i: (i, 0))
    w_spec = _resident((d, ff), lambda i: (0, 0))
    return pl.pallas_call(body, grid=(s // tm,), in_specs=[wide, wide, w_spec, w_spec, row, row, vec, row, vec],
                          out_specs=[row, row, vec, vec], out_shape=[SDS((s, d), f32), SDS((s, d), bf16), SDS((1, d), f32), SDS((1, d), f32)],
                          name=name, compiler_params=_cparams("arbitrary"))(d_g, d_u, w_g, w_u, hmid, dh_out, g_pre, y1, g_post)


def proj_in_grad(pairs, x, add, name, tm=512, dep=None):
    s, d = x.shape
    n = len(pairs)
    deps = [] if dep is None else [dep]

    def body(*refs):
        x_ref, add_ref = refs[3 * n], refs[3 * n + 1]
        outs = refs[3 * n + 2 + len(deps):]

        @pl.when(pl.program_id(0) == 0)
        def _():
            for o in outs[1:]:
                o[...] = jnp.zeros_like(o)

        for r in range(tm // SUB_ROWS):
            rows = slice(r * SUB_ROWS, (r + 1) * SUB_ROWS)
            xv = x_ref[rows, :]
            dx = add_ref[rows, :]
            for i in range(n):
                a_ref, b_ref, g_ref = refs[3 * i:3 * i + 3]
                dxi, dgi = _rms_bwd_math(xv, g_ref[...], _dot_nt(a_ref[rows, :], b_ref[...]))
                dx = dx + dxi
                outs[1 + i][...] += dgi
            outs[0][rows, :] = dx

    row = pl.BlockSpec((tm, d), lambda i: (i, 0))
    vec = pl.BlockSpec((1, d), lambda i: (0, 0))
    in_specs, args = [], []
    for a, b, g in pairs:
        k = a.shape[1]
        in_specs += [pl.BlockSpec((tm, k), lambda i: (i, 0)), _resident((d, k), lambda i: (0, 0)), vec]
        args += [a, b, g]
    in_specs += [row, row] + [pl.BlockSpec((8, LANES), lambda i: (0, 0))] * len(deps)
    out = pl.pallas_call(body, grid=(s // tm,), in_specs=in_specs, out_specs=[row] + [vec] * n,
                         out_shape=[SDS((s, d), f32)] + [SDS((1, d), f32)] * n, name=name,
                         compiler_params=_cparams("arbitrary"))(*args, x, add, *deps)
    return out[0], out[1:]


def ffn_up(f, wg, wu, name, tm=512, tc=256):
    s, d = f.shape
    ff = wg.shape[-1]

    def body(f_ref, wg_ref, wu_ref, gu_ref, act_ref):
        fv = f_ref[...]
        for j in range(ff // tc):
            lo = j * tc
            gg = _dot(fv, wg_ref[:, lo:lo + tc])
            uu = _dot(fv, wu_ref[:, lo:lo + tc])
            gu_ref[:, lo:lo + tc] = gg.astype(bf16)
            gu_ref[:, ff + lo:ff + lo + tc] = uu.astype(bf16)
            act_ref[:, lo:lo + tc] = (gg * _sigmoid(gg) * uu).astype(bf16)

    w_spec = _resident((d, ff), lambda i: (0, 0))
    return pl.pallas_call(body, grid=(s // tm,), in_specs=[pl.BlockSpec((tm, d), lambda i: (i, 0)), w_spec, w_spec],
                          out_specs=[pl.BlockSpec((tm, 2 * ff), lambda i: (i, 0)), pl.BlockSpec((tm, ff), lambda i: (i, 0))],
                          out_shape=[SDS((s, 2 * ff), bf16), SDS((s, ff), bf16)], name=name,
                          compiler_params=_cparams("parallel"))(f, wg, wu)


def _gmlp_forward_chunk(u, v, w_refs, bias, ln_g, ln_b):
    gu, tu = _gelu(u)
    gv, tv = _gelu(v)
    mu = jnp.sum(gv, axis=-1, keepdims=True) * (1.0 / MAIN_WIDTH)
    xc = gv - mu
    rstd = lax.rsqrt(jnp.sum(xc * xc, axis=-1, keepdims=True) * (1.0 / MAIN_WIDTH) + LN_EPS)
    xhat = xc * rstd
    vln = xhat * ln_g + ln_b
    row = lax.broadcasted_iota(jnp.int32, (CHUNK, CHUNK), 0)
    col = lax.broadcasted_iota(jnp.int32, (CHUNK, CHUNK), 1)
    s_parts = []
    for g in range(A_GROUPS):
        w = jnp.where(col <= row, w_refs[g], jnp.zeros((), bf16))
        s_parts.append(_dot(w, vln[:, g * CHUNK:(g + 1) * CHUNK].astype(bf16)) + bias[:, g:g + 1])
    return gu, tu, tv, rstd, xhat, vln, s_parts


def gmlp_fwd(proj, ws, bs_t, ln_g, ln_b, name, tm=512):
    s = proj.shape[0]

    def body(u_ref, v_ref, w_ref, b_ref, g_ref, bb_ref, o_ref):
        bias = b_ref[...]
        for c in range(tm // CHUNK):
            rows = slice(c * CHUNK, (c + 1) * CHUNK)
            gu, _, _, _, _, _, s_parts = _gmlp_forward_chunk(u_ref[rows, :], v_ref[rows, :], w_ref, bias, g_ref[...], bb_ref[...])
            for g in range(A_GROUPS):
                cols = slice(g * CHUNK, (g + 1) * CHUNK)
                o_ref[rows, cols] = (gu[:, cols] * s_parts[g]).astype(bf16)

    vec = pl.BlockSpec((1, MAIN_WIDTH), lambda i: (0, 0))
    return pl.pallas_call(
        body, grid=(s // tm,),
        in_specs=[pl.BlockSpec((tm, MAIN_WIDTH), lambda i: (i, 0)), pl.BlockSpec((tm, MAIN_WIDTH), lambda i: (i, 1)),
                  pl.BlockSpec((A_GROUPS, CHUNK, CHUNK), lambda i: (0, 0, 0)), pl.BlockSpec((CHUNK, A_GROUPS), lambda i: (0, 0)), vec, vec],
        out_specs=pl.BlockSpec((tm, MAIN_WIDTH), lambda i: (i, 0)), out_shape=SDS((s, MAIN_WIDTH), bf16), name=name,
        compiler_params=_cparams("parallel"))(proj, proj, ws, bs_t, ln_g, ln_b)


def gmlp_bwd(proj, d_mixed, ws, ws_t, bs_t, ln_g, ln_b, name, tm=512):
    s = proj.shape[0]

    def body(u_ref, v_ref, dm_ref, w_ref, wt_ref, b_ref, g_ref, bb_ref, duv_ref, dw_ref, db_ref, dg_ref, dbb_ref):
        @pl.when(pl.program_id(0) == 0)
        def _():
            dw_ref[...] = jnp.zeros_like(dw_ref)
            db_ref[...] = jnp.zeros_like(db_ref)
            dg_ref[...] = jnp.zeros_like(dg_ref)
            dbb_ref[...] = jnp.zeros_like(dbb_ref)

        bias = b_ref[...]
        ln_gv = g_ref[...]
        row = lax.broadcasted_iota(jnp.int32, (CHUNK, CHUNK), 0)
        col = lax.broadcasted_iota(jnp.int32, (CHUNK, CHUNK), 1)
        lane = lax.broadcasted_iota(jnp.int32, (CHUNK, LANES), 1)
        for c in range(tm // CHUNK):
            rows = slice(c * CHUNK, (c + 1) * CHUNK)
            u = u_ref[rows, :]
            v = v_ref[rows, :]
            gu, tu, tv, rstd, xhat, vln, s_parts = _gmlp_forward_chunk(u, v, w_ref, bias, ln_gv, bb_ref[...])
            dm = dm_ref[rows, :]
            d_vln_parts = []
            d_gu_parts = []
            db_acc = jnp.zeros((CHUNK, LANES), f32)
            for g in range(A_GROUPS):
                cols = slice(g * CHUNK, (g + 1) * CHUNK)
                dmg = dm[:, cols]
                d_gu_parts.append(dmg * s_parts[g])
                d_s = dmg * gu[:, cols]
                db_acc = db_acc + jnp.where(lane == g, jnp.sum(d_s, axis=-1, keepdims=True), 0.0)
                d_sb = d_s.astype(bf16)
                dw_ref[g] += jnp.where(col <= row, _dot_nt(d_sb, vln[:, cols].astype(bf16)), 0.0)
                wt = jnp.where(row <= col, wt_ref[g], jnp.zeros((), bf16))
                d_vln_parts.append(_dot(wt, d_sb))
            db_ref[...] += db_acc
            d_vln = jnp.concatenate(d_vln_parts, axis=-1)
            d_gu = jnp.concatenate(d_gu_parts, axis=-1)
            dg_ref[...] += jnp.sum(d_vln * xhat, axis=0, keepdims=True)
            dbb_ref[...] += jnp.sum(d_vln, axis=0, keepdims=True)
            dxh = d_vln * ln_gv
            m1 = jnp.sum(dxh, axis=-1, keepdims=True) * (1.0 / MAIN_WIDTH)
            m2 = jnp.sum(dxh * xhat, axis=-1, keepdims=True) * (1.0 / MAIN_WIDTH)
            d_gv = rstd * (dxh - m1 - xhat * m2)
            duv_ref[rows, :MAIN_WIDTH] = (d_gu * _gelu_grad(u, tu)).astype(bf16)
            duv_ref[rows, MAIN_WIDTH:] = (d_gv * _gelu_grad(v, tv)).astype(bf16)

    vec = pl.BlockSpec((1, MAIN_WIDTH), lambda i: (0, 0))
    wspec = pl.BlockSpec((A_GROUPS, CHUNK, CHUNK), lambda i: (0, 0, 0))
    return pl.pallas_call(
        body, grid=(s // tm,),
        in_specs=[pl.BlockSpec((tm, MAIN_WIDTH), lambda i: (i, 0)), pl.BlockSpec((tm, MAIN_WIDTH), lambda i: (i, 1)),
                  pl.BlockSpec((tm, MAIN_WIDTH), lambda i: (i, 0)), wspec, wspec, pl.BlockSpec((CHUNK, A_GROUPS), lambda i: (0, 0)), vec, vec],
        out_specs=[pl.BlockSpec((tm, 2 * MAIN_WIDTH), lambda i: (i, 0)), wspec, pl.BlockSpec((CHUNK, LANES), lambda i: (0, 0)), vec, vec],
        out_shape=[SDS((s, 2 * MAIN_WIDTH), bf16), SDS((A_GROUPS, CHUNK, CHUNK), f32), SDS((CHUNK, LANES), f32),
                   SDS((1, MAIN_WIDTH), f32), SDS((1, MAIN_WIDTH), f32)],
        name=name, compiler_params=_cparams("arbitrary"))(proj, proj, d_mixed, ws, ws_t, bs_t, ln_g, ln_b)


def _head_mask(width, h):
    lane = lax.broadcasted_iota(jnp.int32, (1, width), 1)
    return (lane >= h * HEAD_DIM) & (lane < (h + 1) * HEAD_DIM)


def mem_attn_fwd(proj, q_block, kv, name, tm=512):
    s = proj.shape[0]
    n_mem = kv.shape[0]

    def body(q_ref, kv_ref, o_ref):
        q = q_ref[...].astype(f32)
        k = kv_ref[:, :MEM_WIDTH].astype(bf16)
        v = kv_ref[:, MEM_WIDTH:].astype(bf16)
        out = jnp.zeros((tm, MEM_WIDTH), f32)
        for h in range(MEM_HEADS):
            msk = _head_mask(MEM_WIDTH, h)
            qh = jnp.where(msk, q, 0.0).astype(bf16)
            sc = _dot_nt(qh, k) * ATT_SCALE
            e = jnp.exp(sc - jnp.max(sc, axis=-1, keepdims=True))
            p = e / jnp.sum(e, axis=-1, keepdims=True)
            out = jnp.where(msk, _dot(p.astype(bf16), v), out)
        o_ref[...] = out.astype(bf16)

    return pl.pallas_call(body, grid=(s // tm,),
                          in_specs=[pl.BlockSpec((tm, MEM_WIDTH), lambda i: (i, q_block)), pl.BlockSpec((n_mem, 2 * MEM_WIDTH), lambda i: (0, 0))],
                          out_specs=pl.BlockSpec((tm, MEM_WIDTH), lambda i: (i, 0)), out_shape=SDS((s, MEM_WIDTH), bf16), name=name,
                          compiler_params=_cparams("parallel"))(proj, kv)


def mem_attn_bwd(proj, q_block, kv, d_mixed, name, tm=512):
    s = proj.shape[0]
    n_mem = kv.shape[0]

    def body(q_ref, kv_ref, do_ref, dq_ref, dkv_ref):
        @pl.when(pl.program_id(0) == 0)
        def _():
            dkv_ref[...] = jnp.zeros_like(dkv_ref)

        q = q_ref[...].astype(f32)
        do = do_ref[...]
        k = kv_ref[:, :MEM_WIDTH].astype(bf16)
        v = kv_ref[:, MEM_WIDTH:].astype(bf16)
        dq = jnp.zeros((tm, MEM_WIDTH), f32)
        dk = jnp.zeros((n_mem, MEM_WIDTH), f32)
        dv = jnp.zeros((n_mem, MEM_WIDTH), f32)
        for h in range(MEM_HEADS):
            msk = _head_mask(MEM_WIDTH, h)
            qh = jnp.where(msk, q, 0.0).astype(bf16)
            doh = jnp.where(msk, do, 0.0).astype(bf16)
            sc = _dot_nt(qh, k) * ATT_SCALE
            e = jnp.exp(sc - jnp.max(sc, axis=-1, keepdims=True))
            p = e / jnp.sum(e, axis=-1, keepdims=True)
            dp = _dot_nt(doh, v)
            ds = p * (dp - jnp.sum(dp * p, axis=-1, keepdims=True))
            dsb = (ds * ATT_SCALE).astype(bf16)
            dq = jnp.where(msk, _dot(dsb, k), dq)
            dk = dk + _dot_tn(dsb, qh)
            dv = dv + _dot_tn(p.astype(bf16), doh)
        dq_ref[...] = dq.astype(bf16)
        dkv_ref[:, :MEM_WIDTH] += dk
        dkv_ref[:, MEM_WIDTH:] += dv

    return pl.pallas_call(
        body, grid=(s // tm,),
        in_specs=[pl.BlockSpec((tm, MEM_WIDTH), lambda i: (i, q_block)), pl.BlockSpec((n_mem, 2 * MEM_WIDTH), lambda i: (0, 0)),
                  pl.BlockSpec((tm, MEM_WIDTH), lambda i: (i, MAIN_WIDTH // MEM_WIDTH))],
        out_specs=[pl.BlockSpec((tm, MEM_WIDTH), lambda i: (i, 0)), pl.BlockSpec((n_mem, 2 * MEM_WIDTH), lambda i: (0, 0))],
        out_shape=[SDS((s, MEM_WIDTH), bf16), SDS((n_mem, 2 * MEM_WIDTH), f32)], name=name,
        compiler_params=_cparams("arbitrary"))(proj, kv, d_mixed)


def _tri(t, upper):
    r = lax.broadcasted_iota(jnp.int32, (t, t), 0)
    c = lax.broadcasted_iota(jnp.int32, (t, t), 1)
    return ((r <= c) if upper else (r >= c)).astype(f32)


def fgate_fwd(z_t, b, name, t=512):
    hh, s = z_t.shape

    def body(z_ref, b_ref, c_ref):
        u = _tri(t, True)
        carry = jnp.zeros((hh, 1), f32)
        for blk in range(s // t):
            x = z_ref[:, blk * t:(blk + 1) * t] + b_ref[...]
            logf = jnp.minimum(x, 0.0) - jnp.log(1.0 + jnp.exp(-jnp.abs(x)))
            y = jnp.dot(logf, u, precision=lax.Precision.HIGHEST, preferred_element_type=f32) + carry
            c_ref[:, blk * t:(blk + 1) * t] = y
            carry = y[:, t - 1:t]

    return pl.pallas_call(body, out_shape=SDS((hh, s), f32), name=name, compiler_params=_cparams())(z_t, b)


def fgate_bwd(dc_t, z_t, b, name, t=512):
    hh, s = z_t.shape

    def body(dc_ref, z_ref, b_ref, dz_ref, db_ref):
        low = _tri(t, False)
        carry = jnp.zeros((hh, 1), f32)
        total = jnp.zeros((hh, 1), f32)
        for blk in reversed(range(s // t)):
            cols = slice(blk * t, (blk + 1) * t)
            y = jnp.dot(dc_ref[:, cols], low, precision=lax.Precision.HIGHEST, preferred_element_type=f32) + carry
            carry = y[:, 0:1]
            dz = y * _sigmoid(-(z_ref[:, cols] + b_ref[...]))
            dz_ref[:, cols] = dz
            total = total + jnp.sum(dz, axis=-1, keepdims=True)
        db_ref[...] = jnp.broadcast_to(total, db_ref.shape)

    return pl.pallas_call(body, out_shape=[SDS((hh, s), f32), SDS((hh, LANES), f32)], name=name,
                          compiler_params=_cparams())(dc_t, z_t, b)


def _pair_masks():
    lane = lax.broadcasted_iota(jnp.int32, (1, LANES), 1)
    return [lane < HEAD_DIM, lane >= HEAD_DIM]


def _tile_base(cr_ref, hh, lo):
    return cr_ref[hh:hh + 1, pl.ds(lo, LANES)][:, 0:1]


def fox_fwd(q, kv, c_row, name, tq=512):
    s = kv.shape[0]
    nq = s // tq

    def body(q_ref, k_ref, v_ref, cr_ref, o_ref, lse_ref):
        i = pl.program_id(1)
        qv = q_ref[...]
        masks = _pair_masks()
        row = lax.broadcasted_iota(jnp.int32, (tq, tq), 0)
        col = lax.broadcasted_iota(jnp.int32, (tq, tq), 1)
        qh = [jnp.where(masks[hh], qv, jnp.zeros((), bf16)) * ATT_SCALE for hh in range(2)]
        ct = [_tile_base(cr_ref, hh, pl.multiple_of(i * tq, tq)) for hh in range(2)]

        def block(j, carry, diag):
            lo = pl.multiple_of(j * tq, tq)
            ks = k_ref[pl.ds(lo, tq), :]
            vs = v_ref[pl.ds(lo, tq), :]
            out = []
            for hh in range(2):
                m, l, acc = carry[hh]
                sc = _dot_nt(qh[hh], ks) + (ct[hh] - cr_ref[hh:hh + 1, pl.ds(lo, tq)])
                if diag:
                    sc = jnp.where(col <= row, sc, -jnp.inf)
                m_new = jnp.maximum(m, jnp.max(sc, axis=-1, keepdims=True))
                alpha = jnp.exp(m - m_new)
                p = jnp.exp(sc - m_new)
                l = alpha * l + jnp.sum(p, axis=-1, keepdims=True)
                p_hi = p.astype(bf16)
                p_lo = (p - p_hi.astype(f32)).astype(bf16)
                acc = alpha * acc + (_dot(p_hi, vs) + _dot(p_lo, vs))
                out.append((m_new, l, acc))
            return tuple(out)

        init = (jnp.full((tq, 1), -jnp.inf, f32), jnp.zeros((tq, 1), f32), jnp.zeros((tq, LANES), f32))
        carry = lax.fori_loop(0, i, functools.partial(block, diag=False), (init, init))
        res = [(acc / l, m + jnp.log(l)) for m, l, acc in block(i, carry, True)]
        o_ref[...] = jnp.where(masks[0], res[0][0], res[1][0])
        lse_ref[...] = jnp.where(masks[0], res[0][1], res[1][1])

    return pl.pallas_call(
        body, grid=(FOX_PAIRS, nq),
        in_specs=[pl.BlockSpec((tq, LANES), lambda p, i: (i, p)), pl.BlockSpec((s, LANES), lambda p, i: (0, p)),
                  pl.BlockSpec((s, LANES), lambda p, i: (0, FOX_PAIRS + p)), pl.BlockSpec((None, 2, s), lambda p, i: (p, 0, 0))],
        out_specs=[pl.BlockSpec((tq, LANES), lambda p, i: (i, p)), pl.BlockSpec((None, tq, LANES), lambda p, i: (p, i, 0))],
        out_shape=[SDS((s, MAIN_WIDTH), f32), SDS((FOX_PAIRS, s, LANES), f32)], name=name,
        compiler_params=_cparams("parallel", "parallel"))(q, kv, kv, c_row)


def fox_bwd(q, kv, d_mixed, o, lse, c_row, name, tq=512):
    s = kv.shape[0]
    nq = s // tq

    def body(q_ref, k_ref, v_ref, do_ref, o_ref, lse_ref, cr_ref, dq_ref, dk_ref, dv_ref, dc_ref):
        j = pl.program_id(1)

        @pl.when(j == 0)
        def _():
            dq_ref[...] = jnp.zeros_like(dq_ref)

        masks = _pair_masks()
        sub = lax.broadcasted_iota(jnp.int32, (LANES, 1), 0)
        sub_masks = [sub < HEAD_DIM, sub >= HEAD_DIM]
        row = lax.broadcasted_iota(jnp.int32, (tq, tq), 0)
        col = lax.broadcasted_iota(jnp.int32, (tq, tq), 1)
        kj = k_ref[...]
        vj = v_ref[...]
        lo_j = pl.multiple_of(j * tq, tq)

        def block(i, carry, diag):
            dk_t, dv_t, dc0, dc1 = carry
            dcs = [dc0, dc1]
            lo = pl.multiple_of(i * tq, tq)
            qi = q_ref[pl.ds(lo, tq), :]
            qt_i = qi.T
            doi = do_ref[pl.ds(lo, tq), :]
            dot_i = doi.astype(bf16).T
            prod = doi.astype(bf16).astype(f32) * o_ref[pl.ds(lo, tq), :]
            lse_i = lse_ref[pl.ds(lo, tq), :]
            dq_i = jnp.zeros((tq, LANES), f32)
            for hh in range(2):
                qh = jnp.where(masks[hh], qi, jnp.zeros((), bf16))
                doh = jnp.where(masks[hh], doi, 0.0).astype(bf16)
                delta = jnp.sum(jnp.where(masks[hh], prod, 0.0), axis=-1, keepdims=True)
                sc = _dot_nt(qh, kj) * ATT_SCALE + (_tile_base(cr_ref, hh, lo) - cr_ref[hh:hh + 1, pl.ds(lo_j, tq)])
                p = jnp.exp(sc - lse_i[:, hh * HEAD_DIM:hh * HEAD_DIM + 1])
                if diag:
                    p = jnp.where(col <= row, p, 0.0)
                dv_t = dv_t + _dot(jnp.where(sub_masks[hh], dot_i, jnp.zeros((), bf16)), p.astype(bf16))
                ds = p * (_dot_nt(doh, vj) - delta)
                dcs[hh] = dcs[hh] + jnp.sum(ds, axis=0, keepdims=True)
                dsb = (ds * ATT_SCALE).astype(bf16)
                dq_i = jnp.where(masks[hh], _dot(dsb, kj), dq_i)
                dk_t = dk_t + _dot(jnp.where(sub_masks[hh], qt_i, jnp.zeros((), bf16)), dsb)
            dq_ref[pl.ds(lo, tq), :] += dq_i
            return dk_t, dv_t, dcs[0], dcs[1]

        zero = jnp.zeros((LANES, tq), f32)
        zrow = jnp.zeros((1, tq), f32)
        carry = block(j, (zero, zero, zrow, zrow), True)
        dk_t, dv_t, dc0, dc1 = lax.fori_loop(j + 1, nq, functools.partial(block, diag=False), carry)
        dk_ref[...] = dk_t.T.astype(bf16)
        dv_ref[...] = dv_t.T.astype(bf16)
        dc_ref[0:1, :] = -dc0
        dc_ref[1:2, :] = -dc1

    full = lambda p, j: (0, p)
    tile = lambda p, j: (j, p)
    return pl.pallas_call(
        body, grid=(FOX_PAIRS, nq),
        in_specs=[pl.BlockSpec((s, LANES), full), pl.BlockSpec((tq, LANES), tile), pl.BlockSpec((tq, LANES), lambda p, j: (j, FOX_PAIRS + p)),
                  pl.BlockSpec((s, LANES), full), pl.BlockSpec((s, LANES), full), pl.BlockSpec((None, s, LANES), lambda p, j: (p, 0, 0)),
                  pl.BlockSpec((None, 2, s), lambda p, j: (p, 0, 0))],
        out_specs=[pl.BlockSpec((s, LANES), full), pl.BlockSpec((tq, LANES), tile), pl.BlockSpec((tq, LANES), tile),
                   pl.BlockSpec((None, 2, tq), lambda p, j: (p, 0, j))],
        out_shape=[SDS((s, MAIN_WIDTH), f32), SDS((s, MAIN_WIDTH), bf16), SDS((s, MAIN_WIDTH), bf16), SDS((FOX_PAIRS, 2, s), f32)],
        name=name, compiler_params=_cparams("parallel", "arbitrary"))(q, kv, kv, d_mixed, o, lse, c_row)


def adamw(w, g, m, v, name, tr=256):
    r, c = w.shape
    tr = min(tr, r)
    assert r % tr == 0, (name, r, tr)
    c1 = 1.0 / (1.0 - ADAM_B1 ** ADAM_STEP)
    c2 = 1.0 / (1.0 - ADAM_B2 ** ADAM_STEP)

    def body(w_ref, g_ref, m_ref, v_ref, d_ref, mo_ref, vo_ref):
        gv = g_ref[...]
        mn = ADAM_B1 * m_ref[...] + (1.0 - ADAM_B1) * gv
        vn = ADAM_B2 * v_ref[...] + (1.0 - ADAM_B2) * gv * gv
        mo_ref[...] = mn
        vo_ref[...] = vn
        d_ref[...] = -ADAM_LR * ((mn * c1) / (jnp.sqrt(vn * c2) + ADAM_EPS) + ADAM_WD * w_ref[...])

    spec = pl.BlockSpec((tr, c), lambda i: (i, 0))
    return pl.pallas_call(body, grid=(r // tr,), in_specs=[spec] * 4, out_specs=[spec] * 3, out_shape=[SDS((r, c), f32)] * 3,
                          name=name, compiler_params=_cparams("parallel"))(w, g, m, v)


def sum_leading(x, name, out_dtype=f32, tr=None):
    n, r, c = x.shape
    tr = tr or r
    assert r % tr == 0

    def body(x_ref, o_ref):
        acc = x_ref[0].astype(f32)
        for k in range(1, n):
            acc = acc + x_ref[k].astype(f32)
        o_ref[...] = acc.astype(out_dtype)

    return pl.pallas_call(body, grid=(r // tr,), in_specs=[pl.BlockSpec((n, tr, c), lambda i: (0, i, 0))],
                          out_specs=pl.BlockSpec((tr, c), lambda i: (i, 0)), out_shape=SDS((r, c), out_dtype), name=name,
                          compiler_params=_cparams("parallel"))(x)


_ANY = pl.BlockSpec(memory_space=pl.ANY)
_DMA = pltpu.SemaphoreType.DMA


_HBM = pl.BlockSpec(memory_space=pltpu.HBM)
_SEM = pl.BlockSpec(memory_space=pltpu.SEMAPHORE)
_EFFECT = pltpu.SideEffectType.DATAFLOW_SIDE_EFFECTING
_FLIPS = [(0, 0, 1), (1, 0, 0), (0, 1, 0), (1, 1, 0), (1, 0, 1), (0, 1, 1), (1, 1, 1)]


def _me():
    return lax.axis_index("x"), lax.axis_index("y"), lax.axis_index("c")


def _peers():
    mx, my, mc = _me()
    return [(jnp.bitwise_xor(mx, fx), jnp.bitwise_xor(my, fy), jnp.bitwise_xor(mc, fc)) for fx, fy, fc in _FLIPS]


def _index(dev):
    return 4 * dev[0] + 2 * dev[1] + dev[2]


def _win(ref, axis, k, size, count=1):
    idx = [slice(None)] * len(ref.shape)
    idx[axis] = pl.ds(k * size, count * size)
    return ref.at[tuple(idx)]


def _hbm(a):
    return pltpu.with_memory_space_constraint(a, pltpu.HBM)


def _exchange_start(srcs, lands, copies_of, name):
    n = len(srcs)

    def body(*refs):
        src = refs[:n]
        send_sems, recv_sems, self_sems = refs[2 * n:2 * n + 3]
        land = refs[3 * n + 3:4 * n + 3]
        token = refs[4 * n + 3]
        me = _index(_me())
        for a in range(n):
            for s_ref, d_ref, peer in copies_of(a, src[a], land[a], me):
                if peer is None:
                    pltpu.make_async_copy(s_ref, d_ref, self_sems.at[a]).start()
                else:
                    pltpu.make_async_remote_copy(src_ref=s_ref, dst_ref=d_ref, send_sem=send_sems.at[a], recv_sem=recv_sems.at[a],
                                                 device_id=peer, device_id_type=MESH).start()
        token[...] = jnp.zeros_like(token)

    outs = pl.pallas_call(
        body, name=name,
        out_shape=(_DMA((n,)), _DMA((n,)), _DMA((n,)), *[pltpu.HBM(s.shape, s.dtype) for s in srcs],
                   *[pltpu.HBM(l.shape, l.dtype) for l in lands], SDS((8, LANES), f32)),
        in_specs=[_HBM] * (2 * n), out_specs=(_SEM, _SEM, _SEM, *[_HBM] * (2 * n), pl.BlockSpec(memory_space=pltpu.VMEM)),
        input_output_aliases={i: 3 + i for i in range(2 * n)},
        compiler_params=pltpu.CompilerParams(has_side_effects=_EFFECT),
    )(*[_hbm(s) for s in srcs], *[_hbm(lax.empty(l.shape, l.dtype)) for l in lands])
    return dict(sems=outs[:3], srcs=list(outs[3:3 + n]), lands=list(outs[3 + n:3 + 2 * n]), token=outs[3 + 2 * n])


def _exchange_wait(started, waits_of, after, name):
    srcs, lands = started["srcs"], started["lands"]
    n = len(srcs)

    def body(*refs):
        src = refs[:n]
        land = refs[n:2 * n]
        send_sems, recv_sems, self_sems = refs[2 * n:2 * n + 3]
        me = _index(_me())
        for a in range(n):
            seven, (s_ref, d_ref) = waits_of(a, src[a], land[a], me)
            both = pltpu.make_async_remote_copy(src_ref=seven, dst_ref=seven, send_sem=send_sems.at[a], recv_sem=recv_sems.at[a],
                                                device_id=_me(), device_id_type=MESH)
            both.wait_send()
            both.wait_recv()
            pltpu.make_async_copy(s_ref, d_ref, self_sems.at[a]).wait()

    outs = pl.pallas_call(
        body, name=name, out_shape=tuple(pltpu.HBM(t.shape, t.dtype) for t in srcs + lands),
        in_specs=[_HBM] * (2 * n) + [_SEM] * 3 + [_ANY], out_specs=tuple([_HBM] * (2 * n)),
        input_output_aliases={i: i for i in range(2 * n)},
        compiler_params=pltpu.CompilerParams(has_side_effects=_EFFECT),
    )(*srcs, *lands, *started["sems"], after)
    return list(outs[n:])


def gather_start(locs, axes, name):
    lands = [SDS(tuple(N_DEV * d if i == ax else d for i, d in enumerate(l.shape)), l.dtype) for l, ax in zip(locs, axes)]

    def copies_of(a, src, land, me):
        mine = _win(land, axes[a], me, src.shape[axes[a]])
        return [(src, mine, peer) for peer in _peers()] + [(src, mine, None)]

    return _exchange_start(locs, lands, copies_of, name)


def gather_wait(started, axes, after, name):
    def waits_of(a, src, land, me):
        size = src.shape[axes[a]]
        return _win(land, axes[a], 0, size, N_DEV - 1), (src, _win(land, axes[a], me, size))

    return _exchange_wait(started, waits_of, after, name)


def scatter_start(grads, axes, name):
    lands = [SDS((N_DEV,) + tuple(d // N_DEV if i == ax else d for i, d in enumerate(g.shape)), g.dtype) for g, ax in zip(grads, axes)]

    def copies_of(a, src, land, me):
        size = src.shape[axes[a]] // N_DEV
        out = [(_win(src, axes[a], _index(peer), size), land.at[me], peer) for peer in _peers()]
        return out + [(_win(src, axes[a], me, size), land.at[me], None)]

    return _exchange_start(grads, lands, copies_of, name)


def scatter_wait(started, axes, after, name):
    def waits_of(a, src, land, me):
        size = src.shape[axes[a]] // N_DEV
        return land.at[pl.ds(0, N_DEV - 1)], (_win(src, axes[a], me, size), land.at[me])

    return _exchange_wait(started, waits_of, after, name)


def _row_tile(rows, cap=512):
    return max(t for t in range(8, min(rows, cap) + 1, 8) if rows % t == 0)


_SMALL = [
    ("ln_mix_pre", (2, 1024)), ("ln_mix_post", (2, 1024)), ("ln_ffn_pre", (2, 1024)), ("ln_ffn_post", (2, 1024)),
    ("ln_mem", (2, 1024)), ("w_spatial", (1, 6, 128, 128)), ("b_spatial", (1, 6, 128)), ("ln_shared", (1024,)),
    ("b_forget", (12,)), ("ln_v_g", (1, 768)), ("ln_v_b", (1, 768)),
]
_SMALL_TILE = 8 * LANES


def _small_rows(shape):
    return -(-math.prod(shape) // _SMALL_TILE) * 8


def _pack_small(vals, shapes):
    parts = []
    for name, shape in shapes:
        flat = vals[name].reshape(-1).astype(f32)
        rows = _small_rows(shape)
        parts.append(jnp.pad(flat, (0, rows * LANES - flat.shape[0])).reshape(rows, LANES))
    return jnp.concatenate(parts, axis=0)


def _unpack_small(buf, shapes):
    out = {}
    lo = 0
    for name, shape in shapes:
        rows = _small_rows(shape)
        out[name] = buf[lo:lo + rows].reshape(-1)[:math.prod(shape)].reshape(shape)
        lo += rows
    return out


def kernel(x, mem, ln_mix_pre, ln_mix_post, ln_ffn_pre, ln_ffn_post, ln_mem, w_mem_kv, w_out, w_ffn_gate, w_ffn_up, w_ffn_down, w_in_a, w_spatial, b_spatial, ln_v_g, ln_v_b, ln_shared, w_shared_kv, b_forget, w_in_b, loss_target, m_ln_mix_pre, m_ln_mix_post, m_ln_ffn_pre, m_ln_ffn_post, m_ln_mem, m_w_mem_kv, m_w_out, m_w_ffn_gate, m_w_ffn_up, m_w_ffn_down, m_w_in_a, m_w_spatial, m_b_spatial, m_ln_v_g, m_ln_v_b, m_ln_shared, m_w_shared_kv, m_b_forget, m_w_in_b, v_ln_mix_pre, v_ln_mix_post, v_ln_ffn_pre, v_ln_ffn_post, v_ln_mem, v_w_mem_kv, v_w_out, v_w_ffn_gate, v_w_ffn_up, v_w_ffn_down, v_w_in_a, v_w_spatial, v_b_spatial, v_ln_v_g, v_ln_v_b, v_ln_shared, v_w_shared_kv, v_b_forget, v_w_in_b):
    weights = dict(ln_mix_pre=ln_mix_pre, ln_mix_post=ln_mix_post, ln_ffn_pre=ln_ffn_pre, ln_ffn_post=ln_ffn_post, ln_mem=ln_mem,
                   w_mem_kv=w_mem_kv, w_out=w_out, w_ffn_gate=w_ffn_gate, w_ffn_up=w_ffn_up, w_ffn_down=w_ffn_down, w_in_a=w_in_a,
                   w_spatial=w_spatial, b_spatial=b_spatial, ln_v_g=ln_v_g, ln_v_b=ln_v_b, ln_shared=ln_shared,
                   w_shared_kv=w_shared_kv, b_forget=b_forget, w_in_b=w_in_b)
    mom_m = dict(ln_mix_pre=m_ln_mix_pre, ln_mix_post=m_ln_mix_post, ln_ffn_pre=m_ln_ffn_pre, ln_ffn_post=m_ln_ffn_post, ln_mem=m_ln_mem,
                 w_mem_kv=m_w_mem_kv, w_out=m_w_out, w_ffn_gate=m_w_ffn_gate, w_ffn_up=m_w_ffn_up, w_ffn_down=m_w_ffn_down, w_in_a=m_w_in_a,
                 w_spatial=m_w_spatial, b_spatial=m_b_spatial, ln_v_g=m_ln_v_g, ln_v_b=m_ln_v_b, ln_shared=m_ln_shared,
                 w_shared_kv=m_w_shared_kv, b_forget=m_b_forget, w_in_b=m_w_in_b)
    mom_v = dict(ln_mix_pre=v_ln_mix_pre, ln_mix_post=v_ln_mix_post, ln_ffn_pre=v_ln_ffn_pre, ln_ffn_post=v_ln_ffn_post, ln_mem=v_ln_mem,
                 w_mem_kv=v_w_mem_kv, w_out=v_w_out, w_ffn_gate=v_w_ffn_gate, w_ffn_up=v_w_ffn_up, w_ffn_down=v_w_ffn_down, w_in_a=v_w_in_a,
                 w_spatial=v_w_spatial, b_spatial=v_b_spatial, ln_v_g=v_ln_v_g, ln_v_b=v_ln_v_b, ln_shared=v_ln_shared,
                 w_shared_kv=v_w_shared_kv, b_forget=v_b_forget, w_in_b=v_w_in_b)
    names = list(weights)
    mx, my, mc = lax.axis_index("x"), lax.axis_index("y"), lax.axis_index("c")
    me = 4 * mx + 2 * my + mc

    h0 = x[0]
    mem0 = mem[0]
    tgt = loss_target[0]
    seq = h0.shape[0]

    vec = lambda a: a.reshape(1, -1)
    pad_to = lambda a, axis, size: jnp.pad(a, [(0, size - a.shape[i] if i == axis else 0) for i in range(a.ndim)])

    def after(tok, a):
        return a + tok[0, 0].astype(a.dtype)

    lnv_loc = pad_to(jnp.concatenate([ln_v_g, ln_v_b], axis=0), 0, 8)
    st_a = gather_start([w_in_a.astype(bf16), pad_to(lnv_loc, 1, LANES)[None]], [0, 0], "gather_a_start")
    mix_locs = lambda l, tok: [after(tok, w_mem_kv[l]).astype(bf16), w_out[l].astype(bf16)]
    def ffn_gather_start(l, tok):
        gate_up = gather_start([pad_to(after(tok, w_ffn_gate[l]).astype(bf16), 1, FF_SHARD_PAD),
                                pad_to(w_ffn_up[l].astype(bf16), 1, FF_SHARD_PAD)], [1, 1], f"gather_gate_up{l}_start")
        down = gather_start([pad_to(after(gate_up["token"], w_ffn_down[l]).astype(bf16), 0, FF_SHARD_PAD)], [0], f"gather_down{l}_start")
        return gate_up, down

    st_b = [gather_start(mix_locs(0, st_a["token"]), [0, 0], "gather_b0_start"), None]
    st_c = ffn_gather_start(0, st_b[0]["token"])
    st_d = gather_start([after(st_c[1]["token"], w_in_b[0]).astype(bf16), pad_to(w_shared_kv.astype(bf16), 1, KV_PAD)], [0, 0],
                        "gather_d_start")
    st_b[1] = gather_start(mix_locs(1, st_d["token"]), [0, 0], "gather_b1_start")
    st_e = ffn_gather_start(1, st_b[1]["token"])
    ws = w_spatial[0].astype(bf16)
    ws_t = ws.transpose(0, 2, 1)
    bs_t = b_spatial[0].T

    (a0,) = rms_fwd(h0, [after(st_e[1]["token"], vec(ln_mix_pre[0]))], "a0_norm")
    w_in_a8, lnv8 = gather_wait(st_a, [0, 0], a0, "gather_a_wait")
    w_in_a_full = w_in_a8.transpose(1, 0, 2).reshape(D_MODEL, -1)
    lnv_g = lnv8[:, 0, :MAIN_WIDTH // N_DEV].reshape(1, MAIN_WIDTH)
    lnv_b = lnv8[:, 1, :MAIN_WIDTH // N_DEV].reshape(1, MAIN_WIDTH)
    proj0 = mm(a0, w_in_a_full, "proj0", tn=896)
    main0 = gmlp_fwd(proj0, ws, bs_t, lnv_g, lnv_b, "gmlp_fwd")
    w_mkv, w_o = [None, None], [None, None]
    w_mkv[0], w_o[0] = gather_wait(st_b[0], [0, 0], main0, "gather_b0_wait")
    (memn0,) = rms_fwd(mem0, [vec(ln_mem[0])], "mem0_norm")
    kvm0 = mm(memn0, w_mkv[0], "kvm0")
    om0 = mem_attn_fwd(proj0, 2 * MAIN_WIDTH // MEM_WIDTH, kvm0, "mem_attn0")
    mixed0 = jnp.concatenate([main0, om0], axis=-1)
    y1_0, hmid0, f0 = mm_resnorm(mixed0, w_o[0], h0, vec(ln_mix_post[0]), [vec(ln_ffn_pre[0])], "mix_out0")
    w_g0, w_u0 = gather_wait(st_c[0], [1, 1], f0, "gather_gate_up0_wait")
    gu0, act0 = ffn_up(f0, w_g0, w_u0, "ffn_up0")
    (w_d0,) = gather_wait(st_c[1], [0], act0, "gather_down0_wait")
    y2_0, h1, a1, sin1 = mm_resnorm(act0, w_d0, hmid0, vec(ln_ffn_post[0]), [vec(ln_mix_pre[1]), vec(ln_shared)], "ffn_down0")

    w_inb, w_kv = gather_wait(st_d, [0, 0], sin1, "gather_d_wait")
    kvb = mm(sin1, w_kv, "kv_shared", out_dtype=bf16, tn=MAIN_WIDTH, ncols=2 * MAIN_WIDTH)
    zf = mm(sin1, w_kv, "forget_logits", tn=256, col0=2 * MAIN_WIDTH, ncols=256)
    qb = mm(a1, w_inb, "proj1", out_dtype=bf16)
    z_t = jnp.pad(zf[:, :FOX_HEADS].T, ((0, 16 - FOX_HEADS), (0, 0)))
    bf_col = jnp.pad(b_forget, (0, 16 - FOX_HEADS)).reshape(16, 1)
    c_t = fgate_fwd(z_t, bf_col, "fgate_fwd")
    c_row = c_t[:FOX_HEADS].reshape(FOX_PAIRS, 2, seq)
    main1, lse = fox_fwd(qb, kvb, c_row, "fox_fwd")
    w_mkv[1], w_o[1] = gather_wait(st_b[1], [0, 0], main1, "gather_b1_wait")
    (memn1,) = rms_fwd(mem0, [vec(ln_mem[1])], "mem1_norm")
    kvm1 = mm(memn1, w_mkv[1], "kvm1")
    om1 = mem_attn_fwd(qb, MAIN_WIDTH // MEM_WIDTH, kvm1, "mem_attn1")
    mixed1 = jnp.concatenate([main1.astype(bf16), om1], axis=-1)
    y1_1, hmid1, f1 = mm_resnorm(mixed1, w_o[1], h1, vec(ln_mix_post[1]), [vec(ln_ffn_pre[1])], "mix_out1")
    w_g1, w_u1 = gather_wait(st_e[0], [1, 1], f1, "gather_gate_up1_wait")
    gu1, act1 = ffn_up(f1, w_g1, w_u1, "ffn_up1")
    (w_d1,) = gather_wait(st_e[1], [0], act1, "gather_down1_wait")
    y2_1, h2 = mm_resnorm(act1, w_d1, hmid1, vec(ln_ffn_post[1]), [], "ffn_down1")
    dh, loss_tile = loss_grad(h2, tgt, "loss")
    ffn_w = [(w_g0, w_u0, w_d0), (w_g1, w_u1, w_d1)]

    small = {}

    def ffn_backward(layer, dh_out, y2, hmid, f, gu, act, y1):
        w_g, w_u, w_d = ffn_w[layer]
        d_y2, dg_fpost = rms_bwd(y2, vec(ln_ffn_post[layer]), dh_out, None, bf16, f"ffn_post_bwd{layer}")
        dw_down = mm_tn(act, d_y2, f"dw_down{layer}", tk=256)
        rs_down = scatter_start([dw_down], [0], f"scatter_down{layer}_start")
        d_g, d_u = ffn_act_grad(d_y2, w_d, gu, f"ffn_act_grad{layer}")
        dw_g = mm_tn(f, d_g, f"dw_gate{layer}", dep=rs_down["token"])
        dw_u = mm_tn(f, d_u, f"dw_up{layer}")
        rs_gate_up = scatter_start([dw_g, dw_u], [1, 1], f"scatter_gate_up{layer}_start")
        dh_mid, d_y1, dg_fpre, dg_mpost = ffn_in_grad(d_g, d_u, w_g, w_u, hmid, dh_out, after(rs_gate_up["token"], vec(ln_ffn_pre[layer])),
                                                      y1, vec(ln_mix_post[layer]), f"ffn_in_grad{layer}")
        return dh_mid, d_y1, dg_fpost, dg_fpre, dg_mpost, (rs_down, rs_gate_up)

    def mix_out_backward(layer, d_y1, mixed):
        dw_out = mm_tn(mixed, d_y1, f"dw_out{layer}")
        d_mixed = mm(d_y1, w_o[layer], f"d_mixed{layer}", trans_b=True)
        return d_mixed, dw_out

    def mem_backward(layer, q_src, q_block, kvm, memn, d_mixed):
        d_qm, d_kvm = mem_attn_bwd(q_src, q_block, kvm, d_mixed, f"mem_attn_bwd{layer}")
        d_kvm_b = d_kvm.astype(bf16)
        dw_mkv = mm_tn(memn, d_kvm_b, f"dw_mem_kv{layer}")
        d_memn = mm(d_kvm_b, w_mkv[layer], f"d_memn{layer}", trans_b=True)
        _, dg_mem = rms_bwd(mem0, vec(ln_mem[layer]), d_memn, None, bf16, f"mem_norm_bwd{layer}")
        return d_qm, dw_mkv, dg_mem


    dh_mid1, d_y1_1, dg_fpost1, dg_fpre1, dg_mpost1, rs_ffn1 = ffn_backward(1, dh, y2_1, hmid1, f1, gu1, act1, y1_1)
    d_mixed1, dw_out1 = mix_out_backward(1, d_y1_1, mixed1)
    d_qm1, dw_mkv1, dg_mem1 = mem_backward(1, qb, MAIN_WIDTH // MEM_WIDTH, kvm1, memn1, d_mixed1)
    rs_mix1 = scatter_start([dw_out1, dw_mkv1], [0, 0], "scatter_mix1_start")
    dq, dk, dv, dc = fox_bwd(qb, kvb, d_mixed1, main1, lse, after(rs_mix1["token"], c_row), "fox_bwd")
    dc_t = jnp.pad(dc.reshape(FOX_HEADS, seq), ((0, 16 - FOX_HEADS), (0, 0)))
    dz_t, db_f = fgate_bwd(dc_t, z_t, bf_col, "fgate_bwd")
    d_kvf = jnp.concatenate([dk, dv, jnp.pad(dz_t[:FOX_HEADS].T.astype(bf16), ((0, 0), (0, KV_PAD - KV_WIDTH)))], axis=-1)
    d_proj1 = jnp.concatenate([dq.astype(bf16), d_qm1], axis=-1)
    dw_in_b = mm_tn(a1, d_proj1, "dw_in_b")
    dw_kv = mm_tn(sin1, d_kvf, "dw_kv", tn=896)
    rs_2 = scatter_start([dw_in_b, dw_kv], [0, 0], "scatter_shared_start")
    dh1, (dg_pre1, dg_shared) = proj_in_grad([(d_proj1, w_inb, vec(ln_mix_pre[1])), (d_kvf, w_kv, vec(ln_shared))], h1, dh_mid1,
                                             "in_grad1", dep=rs_2["token"])

    dh_mid0, d_y1_0, dg_fpost0, dg_fpre0, dg_mpost0, rs_ffn0 = ffn_backward(0, dh1, y2_0, hmid0, f0, gu0, act0, y1_0)
    d_mixed0, dw_out0 = mix_out_backward(0, d_y1_0, mixed0)
    d_qm0, dw_mkv0, dg_mem0 = mem_backward(0, proj0, 2 * MAIN_WIDTH // MEM_WIDTH, kvm0, memn0, d_mixed0)
    rs_mix0 = scatter_start([dw_out0, dw_mkv0], [0, 0], "scatter_mix0_start")
    d_uv, dw_s, db_s, dg_lnv, db_lnv = gmlp_bwd(proj0, d_mixed0, ws, ws_t, bs_t, after(rs_mix0["token"], lnv_g), lnv_b, "gmlp_bwd")

    small["ln_mix_pre"] = jnp.concatenate([jnp.zeros_like(dg_pre1), dg_pre1], axis=0)
    small["ln_mix_post"] = jnp.concatenate([dg_mpost0, dg_mpost1], axis=0)
    small["ln_ffn_pre"] = jnp.concatenate([dg_fpre0, dg_fpre1], axis=0)
    small["ln_ffn_post"] = jnp.concatenate([dg_fpost0, dg_fpost1], axis=0)
    small["ln_mem"] = jnp.concatenate([dg_mem0, dg_mem1], axis=0)
    small["w_spatial"] = dw_s[None]
    small["b_spatial"] = db_s[:, :A_GROUPS].T[None]
    small["ln_shared"] = dg_shared[0]
    small["b_forget"] = db_f[:FOX_HEADS, 0]
    small["ln_v_g"] = dg_lnv
    small["ln_v_b"] = db_lnv
    small_rows = jnp.concatenate([_pack_small(small, _SMALL), loss_tile], axis=0)
    st_small = gather_start([small_rows[None]], [0], "gather_small_grads_start")
    d_proj0 = jnp.concatenate([d_uv, after(st_small["token"], d_qm0)], axis=-1)
    dw_in_a = mm_tn(a0, d_proj0, "dw_in_a", tn=896)
    rs_in_a = scatter_start([dw_in_a.reshape(D_MODEL, N_DEV, -1).transpose(1, 0, 2)], [0], "scatter_in_a_start")
    grad_x, (dg_pre0,) = proj_in_grad([(d_proj0, w_in_a_full, vec(ln_mix_pre[0]))], h0, dh_mid0, "in_grad0", dep=rs_in_a["token"])
    st_last = gather_start([dg_pre0.reshape(1, 8, LANES)], [0], "gather_last_grad_start")

    def owned(started, axes, wait_after, name):
        recv = scatter_wait(started, axes, wait_after, name)
        return [sum_leading(r.reshape((N_DEV, -1, r.shape[-1])), f"{name}_sum{i}", tr=_row_tile(math.prod(r.shape[1:-1])))
                for i, r in enumerate(recv)]

    (g_down1,) = owned(rs_ffn1[0], [0], after(st_last["token"], grad_x[:8, :LANES]), "scatter_down1_wait")
    g_gu1 = owned(rs_ffn1[1], [1, 1], g_down1, "scatter_gate_up1_wait")
    g_mix1 = owned(rs_mix1, [0, 0], g_gu1[0], "scatter_mix1_wait")
    g2 = owned(rs_2, [0, 0], g_mix1[0], "scatter_shared_wait")
    (g_down0,) = owned(rs_ffn0[0], [0], g2[0], "scatter_down0_wait")
    g_gu0 = owned(rs_ffn0[1], [1, 1], g_down0, "scatter_gate_up0_wait")
    g_mix0 = owned(rs_mix0, [0, 0], g_gu0[0], "scatter_mix0_wait")
    (g_in_a,) = owned(rs_in_a, [0], g_mix0[0], "scatter_in_a_wait")
    g_local = dict(
        w_ffn_gate=jnp.stack([g_gu0[0], g_gu1[0]])[:, :, :FF_SHARD], w_ffn_up=jnp.stack([g_gu0[1], g_gu1[1]])[:, :, :FF_SHARD],
        w_ffn_down=jnp.stack([g_down0, g_down1])[:, :FF_SHARD], w_out=jnp.stack([g_mix0[0], g_mix1[0]]),
        w_mem_kv=jnp.stack([g_mix0[1], g_mix1[1]]), w_in_b=g2[0][None], w_shared_kv=g2[1][:, :KV_WIDTH], w_in_a=g_in_a[None])
    (small_all,) = gather_wait(st_small, [0], g_in_a, "gather_small_grads_wait")
    (last_all,) = gather_wait(st_last, [0], small_all, "gather_last_grad_wait")
    small_sum = sum_leading(small_all, "sum_small_grads")
    loss = small_sum[small_rows.shape[0] - 1, 0]
    g_small = _unpack_small(small_sum, _SMALL)
    g_small["ln_mix_pre"] = jnp.concatenate([sum_leading(last_all, "sum_last_grad").reshape(1, D_MODEL), g_small["ln_mix_pre"][1:]], axis=0)
    shard = MAIN_WIDTH // N_DEV
    for n in ("ln_v_g", "ln_v_b"):
        g_small[n] = lax.dynamic_slice_in_dim(g_small[n], me * shard, shard, axis=1)
    grad_w = {**g_small, **g_local}

    delta, new_m, new_v = {}, {}, {}
    for n in g_local:
        two_d = (-1, weights[n].shape[-1])
        d_, m_, v_ = adamw(weights[n].reshape(two_d), grad_w[n].reshape(two_d), mom_m[n].reshape(two_d), mom_v[n].reshape(two_d),
                           f"adamw_{n}", tr=_row_tile(math.prod(weights[n].shape[:-1])))
        delta[n], new_m[n], new_v[n] = (t.reshape(weights[n].shape) for t in (d_, m_, v_))
    small_local_shapes = [(n, tuple(weights[n].shape)) for n, _ in _SMALL]
    packed = [_pack_small(src, small_local_shapes) for src in (weights, grad_w, mom_m, mom_v)]
    outs = adamw(*packed, "adamw_small", tr=packed[0].shape[0])
    for dst, buf in zip((delta, new_m, new_v), outs):
        dst.update(_unpack_small(buf, small_local_shapes))

    return (loss, grad_x[None], *[grad_w[n] for n in names], *[delta[n] for n in names],
            *[new_m[n] for n in names], *[new_v[n] for n in names])
```

```python
import functools
import math

import jax
import jax.numpy as jnp
from jax import lax
from jax.experimental import pallas as pl
from jax.experimental.pallas import tpu as pltpu

f32 = jnp.float32
bf16 = jnp.bfloat16
SDS = jax.ShapeDtypeStruct

D_MODEL = 1024
MAIN_WIDTH = 768
MEM_WIDTH = 256
HEAD_DIM = 64
MEM_HEADS = 4
FOX_HEADS = 12
FOX_PAIRS = FOX_HEADS // 2
CHUNK = 128
A_GROUPS = 6
FF_SHARD = 352
FF_SHARD_PAD = 384
FF_PAD = 8 * FF_SHARD_PAD
KV_WIDTH = 2 * MAIN_WIDTH + FOX_HEADS
KV_PAD = 1792
RMS_EPS = 1e-6
LN_EPS = 1e-5
ATT_SCALE = HEAD_DIM ** -0.5
ADAM_LR, ADAM_B1, ADAM_B2, ADAM_EPS, ADAM_WD, ADAM_STEP = 0.001, 0.9, 0.999, 1e-08, 0.01, 10
N_DEV = 8
AXES = ("x", "y", "c")
MESH = pl.DeviceIdType.MESH
V7X_VMEM_LIMIT = 56 * 1024 * 1024
LANES = 128
FLAT_W = 512
ROW_PAD = 16


def _cparams(*sem):
    return pltpu.CompilerParams(dimension_semantics=sem or None, vmem_limit_bytes=V7X_VMEM_LIMIT)


def _dot(a, b):
    return jnp.dot(a, b, preferred_element_type=f32)


def _dot_nt(a, b):
    return lax.dot_general(a, b, (((1,), (1,)), ((), ())), preferred_element_type=f32)


def _dot_tn(a, b):
    return lax.dot_general(a, b, (((0,), (0,)), ((), ())), preferred_element_type=f32)


def _gelu(x):
    k = math.sqrt(2.0 / math.pi)
    t = jnp.tanh(k * (x + 0.044715 * x * x * x))
    return 0.5 * x * (1.0 + t), t


def _gelu_grad(x, t):
    k = math.sqrt(2.0 / math.pi)
    return 0.5 * (1.0 + t) + 0.5 * x * (1.0 - t * t) * k * (1.0 + 3.0 * 0.044715 * x * x)


def _sigmoid(x):
    return 1.0 / (1.0 + jnp.exp(-x))


def rms_fwd(x, gains, name, tm=512):
    m, d = x.shape
    tm = min(tm, m)
    n = len(gains)

    def body(x_ref, *refs):
        xv = x_ref[...]
        y = xv * lax.rsqrt(jnp.sum(xv * xv, axis=-1, keepdims=True) * (1.0 / d) + RMS_EPS)
        for g_ref, o_ref in zip(refs[:n], refs[n:]):
            o_ref[...] = (y * g_ref[...]).astype(bf16)

    row = pl.BlockSpec((tm, d), lambda i: (i, 0))
    vec = pl.BlockSpec((1, d), lambda i: (0, 0))
    return pl.pallas_call(body, grid=(m // tm,), in_specs=[row] + [vec] * n, out_specs=[row] * n,
                          out_shape=[SDS((m, d), bf16)] * n, name=name, compiler_params=_cparams("parallel"))(x, *gains)


def rms_bwd(x, g, dy, add, out_dtype, name, tm=512):
    m, d = x.shape
    tm = min(tm, m)
    has_add = add is not None

    def body(x_ref, g_ref, dy_ref, *refs):
        dx_ref, dg_ref = refs[-2], refs[-1]
        xv = x_ref[...]
        dyv = dy_ref[...].astype(f32)
        r = lax.rsqrt(jnp.sum(xv * xv, axis=-1, keepdims=True) * (1.0 / d) + RMS_EPS)
        xn = xv * r
        dyg = dyv * g_ref[...]
        dx = r * (dyg - xn * (jnp.sum(dyg * xn, axis=-1, keepdims=True) * (1.0 / d)))
        if has_add:
            dx = dx + refs[0][...]
        dx_ref[...] = dx.astype(out_dtype)

        @pl.when(pl.program_id(0) == 0)
        def _():
            dg_ref[...] = jnp.zeros_like(dg_ref)

        dg_ref[...] += jnp.sum(dyv * xn, axis=0, keepdims=True)

    row = pl.BlockSpec((tm, d), lambda i: (i, 0))
    vec = pl.BlockSpec((1, d), lambda i: (0, 0))
    ins = [x, g, dy] + ([add] if has_add else [])
    return pl.pallas_call(body, grid=(m // tm,), in_specs=[row, vec, row] + ([row] if has_add else []),
                          out_specs=[row, vec], out_shape=[SDS((m, d), out_dtype), SDS((1, d), f32)], name=name,
                          compiler_params=_cparams("arbitrary"))(*ins)


def loss_grad(h, tgt, name, tm=512):
    m, d = h.shape

    def body(h_ref, t_ref, dy_ref, l_ref):
        e = h_ref[...] - t_ref[...]
        dy_ref[...] = e * (1.0 / d)

        @pl.when(pl.program_id(0) == 0)
        def _():
            l_ref[...] = jnp.zeros_like(l_ref)

        part = jnp.sum(jnp.sum(e * e, axis=-1, keepdims=True), axis=0, keepdims=True) * (0.5 / d)
        l_ref[...] += jnp.broadcast_to(part, l_ref.shape)

    row = pl.BlockSpec((tm, d), lambda i: (i, 0))
    return pl.pallas_call(body, grid=(m // tm,), in_specs=[row, row],
                          out_specs=[row, pl.BlockSpec((8, LANES), lambda i: (0, 0))],
                          out_shape=[SDS((m, d), f32), SDS((8, LANES), f32)], name=name,
                          compiler_params=_cparams("arbitrary"))(h, tgt)


def mm(a, b, name, trans_b=False, out_dtype=f32, tm=512, tn=1024, layer=None, col0=0, ncols=None, dep=None):
    m, k = a.shape
    n_all = b.shape[-2] if trans_b else b.shape[-1]
    n = n_all if ncols is None else ncols
    tm, tn = min(tm, m), min(tn, n)
    assert m % tm == 0 and n % tn == 0 and col0 % tn == 0 and not (trans_b and col0), (name, m, n, tm, tn)
    jb = col0 // tn
    lead = () if layer is None else (None,)
    sel = () if layer is None else (layer,)

    def body(a_ref, b_ref, *rest):
        r = _dot_nt(a_ref[...], b_ref[...]) if trans_b else _dot(a_ref[...], b_ref[...])
        rest[-1][...] = r.astype(out_dtype)

    if trans_b:
        b_spec = pl.BlockSpec(lead + (tn, k), lambda j, i: sel + (j, 0))
    else:
        b_spec = pl.BlockSpec(lead + (k, tn), lambda j, i: sel + (0, jb + j))
    deps = [] if dep is None else [dep]
    dep_specs = [pl.BlockSpec((8, LANES), lambda j, i: (0, 0))] * len(deps)
    return pl.pallas_call(body, grid=(n // tn, m // tm), in_specs=[pl.BlockSpec((tm, k), lambda j, i: (i, 0)), b_spec] + dep_specs,
                          out_specs=pl.BlockSpec((tm, tn), lambda j, i: (i, j)), out_shape=SDS((m, n), out_dtype),
                          name=name, compiler_params=_cparams("parallel", "parallel"))(a, b, *deps)


def mm_tn(a, g, name, tk=512, tn=1024, out_dtype=bf16, dep=None):
    s, k = a.shape
    n = g.shape[1]
    tk, tn = min(tk, k), min(tn, n)
    assert k % tk == 0 and n % tn == 0, (name, k, n, tk, tn)

    def body(a_ref, g_ref, *rest):
        rest[-1][...] = _dot_tn(a_ref[...], g_ref[...]).astype(out_dtype)

    deps = [] if dep is None else [dep]
    dep_specs = [pl.BlockSpec((8, LANES), lambda i, j: (0, 0))] * len(deps)
    return pl.pallas_call(body, grid=(k // tk, n // tn),
                          in_specs=[pl.BlockSpec((s, tk), lambda i, j: (0, i)), pl.BlockSpec((s, tn), lambda i, j: (0, j))] + dep_specs,
                          out_specs=pl.BlockSpec((tk, tn), lambda i, j: (i, j)), out_shape=SDS((k, n), out_dtype), name=name,
                          compiler_params=_cparams("parallel", "parallel"))(a, g, *deps)


def _resident(shape, index_map):
    return pl.BlockSpec(shape, index_map, pipeline_mode=pl.Buffered(1))


def _rms(xv):
    return xv * lax.rsqrt(jnp.sum(xv * xv, axis=-1, keepdims=True) * (1.0 / xv.shape[-1]) + RMS_EPS)


def _rms_bwd_math(xv, g, dy):
    d = xv.shape[-1]
    r = lax.rsqrt(jnp.sum(xv * xv, axis=-1, keepdims=True) * (1.0 / d) + RMS_EPS)
    xn = xv * r
    dyg = dy * g
    dx = r * (dyg - xn * (jnp.sum(dyg * xn, axis=-1, keepdims=True) * (1.0 / d)))
    return dx, jnp.sum(dy * xn, axis=0, keepdims=True)


SUB_ROWS = 256


def mm_resnorm(a, b, h, g_post, gains, name, tm=512):
    m, k = a.shape
    d = b.shape[1]
    n = len(gains)

    def body(a_ref, b_ref, h_ref, gp_ref, *refs):
        for r in range(tm // SUB_ROWS):
            rows = slice(r * SUB_ROWS, (r + 1) * SUB_ROWS)
            y = _dot(a_ref[rows, :], b_ref[...])
            refs[n][rows, :] = y
            hn = h_ref[rows, :] + _rms(y) * gp_ref[...]
            refs[n + 1][rows, :] = hn
            if n:
                z = _rms(hn)
                for g_ref, o_ref in zip(refs[:n], refs[n + 2:]):
                    o_ref[rows, :] = (z * g_ref[...]).astype(bf16)

    row = pl.BlockSpec((tm, d), lambda i: (i, 0))
    vec = pl.BlockSpec((1, d), lambda i: (0, 0))
    return pl.pallas_call(body, grid=(m // tm,),
                          in_specs=[pl.BlockSpec((tm, k), lambda i: (i, 0)), _resident((k, d), lambda i: (0, 0)), row, vec] + [vec] * n,
                          out_specs=[row] * (n + 2), out_shape=[SDS((m, d), f32)] * 2 + [SDS((m, d), bf16)] * n, name=name,
                          compiler_params=_cparams("parallel"))(a, b, h, g_post, *gains)


def ffn_act_grad(d_y2, w_d, gu, name, tm=512, tn=1536):
    s, d = d_y2.shape
    ff = w_d.shape[0]
    nb = ff // tn

    def body(a_ref, b_ref, g_ref, u_ref, dg_ref, du_ref):
        av = a_ref[...]
        tc = 256
        for c in range(tn // tc):
            cols = slice(c * tc, (c + 1) * tc)
            da = _dot_nt(av, b_ref[cols, :])
            gg = g_ref[:, cols].astype(f32)
            sg = _sigmoid(gg)
            dg_ref[:, cols] = (da * u_ref[:, cols].astype(f32) * (sg * (1.0 + gg * (1.0 - sg)))).astype(bf16)
            du_ref[:, cols] = (da * gg * sg).astype(bf16)

    tile = pl.BlockSpec((tm, tn), lambda j, i: (i, j))
    return pl.pallas_call(body, grid=(nb, s // tm),
                          in_specs=[pl.BlockSpec((tm, d), lambda j, i: (i, 0)), pl.BlockSpec((tn, d), lambda j, i: (j, 0)), tile,
                                    pl.BlockSpec((tm, tn), lambda j, i: (i, nb + j))],
                          out_specs=[tile, tile], out_shape=[SDS((s, ff), bf16)] * 2, name=name,
                          compiler_params=_cparams("parallel", "parallel"))(d_y2, w_d, gu, gu)


def ffn_in_grad(d_g, d_u, w_g, w_u, hmid, dh_out, g_pre, y1, g_post, name, tm=512):
    s, ff = d_g.shape
    d = w_g.shape[0]

    def body(dg_ref, du_ref, wg_ref, wu_ref, hm_ref, dho_ref, gpre_ref, y1_ref, gpost_ref, dhm_ref, dy1_ref, dgpre_ref, dgpost_ref):
        @pl.when(pl.program_id(0) == 0)
        def _():
            dgpre_ref[...] = jnp.zeros_like(dgpre_ref)
            dgpost_ref[...] = jnp.zeros_like(dgpost_ref)

        for r in range(tm // SUB_ROWS):
            rows = slice(r * SUB_ROWS, (r + 1) * SUB_ROWS)
            d_f = _dot_nt(dg_ref[rows, :], wg_ref[...]) + _dot_nt(du_ref[rows, :], wu_ref[...])
            dx, dg1 = _rms_bwd_math(hm_ref[rows, :], gpre_ref[...], d_f)
            dh_mid = dho_ref[rows, :] + dx
            dhm_ref[rows, :] = dh_mid
            dgpre_ref[...] += dg1
            dy1, dg2 = _rms_bwd_math(y1_ref[rows, :], gpost_ref[...], dh_mid)
            dy1_ref[rows, :] = dy1.astype(bf16)
            dgpost_ref[...] += dg2

    row = pl.BlockSpec((tm, d), lambda i: (i, 0))
    vec = pl.BlockSpec((1, d), lambda i: (0, 0))
    wide = pl.BlockSpec((tm, ff), lambda i: (i, 0))
    w_spec = _resident((d, ff), lambda i: (0, 0))
    return pl.pallas_call(body, grid=(s // tm,), in_specs=[wide, wide, w_spec, w_spec, row, row, vec, row, vec],
                          out_specs=[row, row, vec, vec], out_shape=[SDS((s, d), f32), SDS((s, d), bf16), SDS((1, d), f32), SDS((1, d), f32)],
                          name=name, compiler_params=_cparams("arbitrary"))(d_g, d_u, w_g, w_u, hmid, dh_out, g_pre, y1, g_post)


def proj_in_grad(pairs, x, add, name, tm=512, dep=None):
    s, d = x.shape
    n = len(pairs)
    deps = [] if dep is None else [dep]

    def body(*refs):
        x_ref, add_ref = refs[3 * n], refs[3 * n + 1]
        outs = refs[3 * n + 2 + len(deps):]

        @pl.when(pl.program_id(0) == 0)
        def _():
            for o in outs[1:]:
                o[...] = jnp.zeros_like(o)

        for r in range(tm // SUB_ROWS):
            rows = slice(r * SUB_ROWS, (r + 1) * SUB_ROWS)
            xv = x_ref[rows, :]
            dx = add_ref[rows, :]
            for i in range(n):
                a_ref, b_ref, g_ref = refs[3 * i:3 * i + 3]
                dxi, dgi = _rms_bwd_math(xv, g_ref[...], _dot_nt(a_ref[rows, :], b_ref[...]))
                dx = dx + dxi
                outs[1 + i][...] += dgi
            outs[0][rows, :] = dx

    row = pl.BlockSpec((tm, d), lambda i: (i, 0))
    vec = pl.BlockSpec((1, d), lambda i: (0, 0))
    in_specs, args = [], []
    for a, b, g in pairs:
        k = a.shape[1]
        in_specs += [pl.BlockSpec((tm, k), lambda i: (i, 0)), _resident((d, k), lambda i: (0, 0)), vec]
        args += [a, b, g]
    in_specs += [row, row] + [pl.BlockSpec((8, LANES), lambda i: (0, 0))] * len(deps)
    out = pl.pallas_call(body, grid=(s // tm,), in_specs=in_specs, out_specs=[row] + [vec] * n,
                         out_shape=[SDS((s, d), f32)] + [SDS((1, d), f32)] * n, name=name,
                         compiler_params=_cparams("arbitrary"))(*args, x, add, *deps)
    return out[0], out[1:]


def ffn_up(f, wg, wu, name, tm=512, tc=256):
    s, d = f.shape
    ff = wg.shape[-1]

    def body(f_ref, wg_ref, wu_ref, gu_ref, act_ref):
        fv = f_ref[...]
        for j in range(ff // tc):
            lo = j * tc
            gg = _dot(fv, wg_ref[:, lo:lo + tc])
            uu = _dot(fv, wu_ref[:, lo:lo + tc])
            gu_ref[:, lo:lo + tc] = gg.astype(bf16)
            gu_ref[:, ff + lo:ff + lo + tc] = uu.astype(bf16)
            act_ref[:, lo:lo + tc] = (gg * _sigmoid(gg) * uu).astype(bf16)

    w_spec = _resident((d, ff), lambda i: (0, 0))
    return pl.pallas_call(body, grid=(s // tm,), in_specs=[pl.BlockSpec((tm, d), lambda i: (i, 0)), w_spec, w_spec],
                          out_specs=[pl.BlockSpec((tm, 2 * ff), lambda i: (i, 0)), pl.BlockSpec((tm, ff), lambda i: (i, 0))],
                          out_shape=[SDS((s, 2 * ff), bf16), SDS((s, ff), bf16)], name=name,
                          compiler_params=_cparams("parallel"))(f, wg, wu)


def _gmlp_forward_chunk(u, v, w_refs, bias, ln_g, ln_b):
    gu, tu = _gelu(u)
    gv, tv = _gelu(v)
    mu = jnp.sum(gv, axis=-1, keepdims=True) * (1.0 / MAIN_WIDTH)
    xc = gv - mu
    rstd = lax.rsqrt(jnp.sum(xc * xc, axis=-1, keepdims=True) * (1.0 / MAIN_WIDTH) + LN_EPS)
    xhat = xc * rstd
    vln = xhat * ln_g + ln_b
    row = lax.broadcasted_iota(jnp.int32, (CHUNK, CHUNK), 0)
    col = lax.broadcasted_iota(jnp.int32, (CHUNK, CHUNK), 1)
    s_parts = []
    for g in range(A_GROUPS):
        w = jnp.where(col <= row, w_refs[g], jnp.zeros((), bf16))
        s_parts.append(_dot(w, vln[:, g * CHUNK:(g + 1) * CHUNK].astype(bf16)) + bias[:, g:g + 1])
    return gu, tu, tv, rstd, xhat, vln, s_parts


def gmlp_fwd(proj, ws, bs_t, ln_g, ln_b, name, tm=512):
    s = proj.shape[0]

    def body(u_ref, v_ref, w_ref, b_ref, g_ref, bb_ref, o_ref):
        bias = b_ref[...]
        for c in range(tm // CHUNK):
            rows = slice(c * CHUNK, (c + 1) * CHUNK)
            gu, _, _, _, _, _, s_parts = _gmlp_forward_chunk(u_ref[rows, :], v_ref[rows, :], w_ref, bias, g_ref[...], bb_ref[...])
            for g in range(A_GROUPS):
                cols = slice(g * CHUNK, (g + 1) * CHUNK)
                o_ref[rows, cols] = (gu[:, cols] * s_parts[g]).astype(bf16)

    vec = pl.BlockSpec((1, MAIN_WIDTH), lambda i: (0, 0))
    return pl.pallas_call(
        body, grid=(s // tm,),
        in_specs=[pl.BlockSpec((tm, MAIN_WIDTH), lambda i: (i, 0)), pl.BlockSpec((tm, MAIN_WIDTH), lambda i: (i, 1)),
                  pl.BlockSpec((A_GROUPS, CHUNK, CHUNK), lambda i: (0, 0, 0)), pl.BlockSpec((CHUNK, A_GROUPS), lambda i: (0, 0)), vec, vec],
        out_specs=pl.BlockSpec((tm, MAIN_WIDTH), lambda i: (i, 0)), out_shape=SDS((s, MAIN_WIDTH), bf16), name=name,
        compiler_params=_cparams("parallel"))(proj, proj, ws, bs_t, ln_g, ln_b)


def gmlp_bwd(proj, d_mixed, ws, ws_t, bs_t, ln_g, ln_b, name, tm=512):
    s = proj.shape[0]

    def body(u_ref, v_ref, dm_ref, w_ref, wt_ref, b_ref, g_ref, bb_ref, duv_ref, dw_ref, db_ref, dg_ref, dbb_ref):
        @pl.when(pl.program_id(0) == 0)
        def _():
            dw_ref[...] = jnp.zeros_like(dw_ref)
            db_ref[...] = jnp.zeros_like(db_ref)
            dg_ref[...] = jnp.zeros_like(dg_ref)
            dbb_ref[...] = jnp.zeros_like(dbb_ref)

        bias = b_ref[...]
        ln_gv = g_ref[...]
        row = lax.broadcasted_iota(jnp.int32, (CHUNK, CHUNK), 0)
        col = lax.broadcasted_iota(jnp.int32, (CHUNK, CHUNK), 1)
        lane = lax.broadcasted_iota(jnp.int32, (CHUNK, LANES), 1)
        for c in range(tm // CHUNK):
            rows = slice(c * CHUNK, (c + 1) * CHUNK)
            u = u_ref[rows, :]
            v = v_ref[rows, :]
            gu, tu, tv, rstd, xhat, vln, s_parts = _gmlp_forward_chunk(u, v, w_ref, bias, ln_gv, bb_ref[...])
            dm = dm_ref[rows, :]
            d_vln_parts = []
            d_gu_parts = []
            db_acc = jnp.zeros((CHUNK, LANES), f32)
            for g in range(A_GROUPS):
                cols = slice(g * CHUNK, (g + 1) * CHUNK)
                dmg = dm[:, cols]
                d_gu_parts.append(dmg * s_parts[g])
                d_s = dmg * gu[:, cols]
                db_acc = db_acc + jnp.where(lane == g, jnp.sum(d_s, axis=-1, keepdims=True), 0.0)
                d_sb = d_s.astype(bf16)
                dw_ref[g] += jnp.where(col <= row, _dot_nt(d_sb, vln[:, cols].astype(bf16)), 0.0)
                wt = jnp.where(row <= col, wt_ref[g], jnp.zeros((), bf16))
                d_vln_parts.append(_dot(wt, d_sb))
            db_ref[...] += db_acc
            d_vln = jnp.concatenate(d_vln_parts, axis=-1)
            d_gu = jnp.concatenate(d_gu_parts, axis=-1)
            dg_ref[...] += jnp.sum(d_vln * xhat, axis=0, keepdims=True)
            dbb_ref[...] += jnp.sum(d_vln, axis=0, keepdims=True)
            dxh = d_vln * ln_gv
            m1 = jnp.sum(dxh, axis=-1, keepdims=True) * (1.0 / MAIN_WIDTH)
            m2 = jnp.sum(dxh * xhat, axis=-1, keepdims=True) * (1.0 / MAIN_WIDTH)
            d_gv = rstd * (dxh - m1 - xhat * m2)
            duv_ref[rows, :MAIN_WIDTH] = (d_gu * _gelu_grad(u, tu)).astype(bf16)
            duv_ref[rows, MAIN_WIDTH:] = (d_gv * _gelu_grad(v, tv)).astype(bf16)

    vec = pl.BlockSpec((1, MAIN_WIDTH), lambda i: (0, 0))
    wspec = pl.BlockSpec((A_GROUPS, CHUNK, CHUNK), lambda i: (0, 0, 0))
    return pl.pallas_call(
        body, grid=(s // tm,),
        in_specs=[pl.BlockSpec((tm, MAIN_WIDTH), lambda i: (i, 0)), pl.BlockSpec((tm, MAIN_WIDTH), lambda i: (i, 1)),
                  pl.BlockSpec((tm, MAIN_WIDTH), lambda i: (i, 0)), wspec, wspec, pl.BlockSpec((CHUNK, A_GROUPS), lambda i: (0, 0)), vec, vec],
        out_specs=[pl.BlockSpec((tm, 2 * MAIN_WIDTH), lambda i: (i, 0)), wspec, pl.BlockSpec((CHUNK, LANES), lambda i: (0, 0)), vec, vec],
        out_shape=[SDS((s, 2 * MAIN_WIDTH), bf16), SDS((A_GROUPS, CHUNK, CHUNK), f32), SDS((CHUNK, LANES), f32),
                   SDS((1, MAIN_WIDTH), f32), SDS((1, MAIN_WIDTH), f32)],
        name=name, compiler_params=_cparams("arbitrary"))(proj, proj, d_mixed, ws, ws_t, bs_t, ln_g, ln_b)


def _head_mask(width, h):
    lane = lax.broadcasted_iota(jnp.int32, (1, width), 1)
    return (lane >= h * HEAD_DIM) & (lane < (h + 1) * HEAD_DIM)


def mem_attn_fwd(proj, q_block, kv, name, tm=512):
    s = proj.shape[0]
    n_mem = kv.shape[0]

    def body(q_ref, kv_ref, o_ref):
        q = q_ref[...].astype(f32)
        k = kv_ref[:, :MEM_WIDTH].astype(bf16)
        v = kv_ref[:, MEM_WIDTH:].astype(bf16)
        out = jnp.zeros((tm, MEM_WIDTH), f32)
        for h in range(MEM_HEADS):
            msk = _head_mask(MEM_WIDTH, h)
            qh = jnp.where(msk, q, 0.0).astype(bf16)
            sc = _dot_nt(qh, k) * ATT_SCALE
            e = jnp.exp(sc - jnp.max(sc, axis=-1, keepdims=True))
            p = e / jnp.sum(e, axis=-1, keepdims=True)
            out = jnp.where(msk, _dot(p.astype(bf16), v), out)
        o_ref[...] = out.astype(bf16)

    return pl.pallas_call(body, grid=(s // tm,),
                          in_specs=[pl.BlockSpec((tm, MEM_WIDTH), lambda i: (i, q_block)), pl.BlockSpec((n_mem, 2 * MEM_WIDTH), lambda i: (0, 0))],
                          out_specs=pl.BlockSpec((tm, MEM_WIDTH), lambda i: (i, 0)), out_shape=SDS((s, MEM_WIDTH), bf16), name=name,
                          compiler_params=_cparams("parallel"))(proj, kv)


def mem_attn_bwd(proj, q_block, kv, d_mixed, name, tm=512):
    s = proj.shape[0]
    n_mem = kv.shape[0]

    def body(q_ref, kv_ref, do_ref, dq_ref, dkv_ref):
        @pl.when(pl.program_id(0) == 0)
        def _():
            dkv_ref[...] = jnp.zeros_like(dkv_ref)

        q = q_ref[...].astype(f32)
        do = do_ref[...]
        k = kv_ref[:, :MEM_WIDTH].astype(bf16)
        v = kv_ref[:, MEM_WIDTH:].astype(bf16)
        dq = jnp.zeros((tm, MEM_WIDTH), f32)
        dk = jnp.zeros((n_mem, MEM_WIDTH), f32)
        dv = jnp.zeros((n_mem, MEM_WIDTH), f32)
        for h in range(MEM_HEADS):
            msk = _head_mask(MEM_WIDTH, h)
            qh = jnp.where(msk, q, 0.0).astype(bf16)
            doh = jnp.where(msk, do, 0.0).astype(bf16)
            sc = _dot_nt(qh, k) * ATT_SCALE
            e = jnp.exp(sc - jnp.max(sc, axis=-1, keepdims=True))
            p = e / jnp.sum(e, axis=-1, keepdims=True)
            dp = _dot_nt(doh, v)
            ds = p * (dp - jnp.sum(dp * p, axis=-1, keepdims=True))
            dsb = (ds * ATT_SCALE).astype(bf16)
            dq = jnp.where(msk, _dot(dsb, k), dq)
            dk = dk + _dot_tn(dsb, qh)
            dv = dv + _dot_tn(p.astype(bf16), doh)
        dq_ref[...] = dq.astype(bf16)
        dkv_ref[:, :MEM_WIDTH] += dk
        dkv_ref[:, MEM_WIDTH:] += dv

    return pl.pallas_call(
        body, grid=(s // tm,),
        in_specs=[pl.BlockSpec((tm, MEM_WIDTH), lambda i: (i, q_block)), pl.BlockSpec((n_mem, 2 * MEM_WIDTH), lambda i: (0, 0)),
                  pl.BlockSpec((tm, MEM_WIDTH), lambda i: (i, MAIN_WIDTH // MEM_WIDTH))],
        out_specs=[pl.BlockSpec((tm, MEM_WIDTH), lambda i: (i, 0)), pl.BlockSpec((n_mem, 2 * MEM_WIDTH), lambda i: (0, 0))],
        out_shape=[SDS((s, MEM_WIDTH), bf16), SDS((n_mem, 2 * MEM_WIDTH), f32)], name=name,
        compiler_params=_cparams("arbitrary"))(proj, kv, d_mixed)


def _tri(t, upper):
    r = lax.broadcasted_iota(jnp.int32, (t, t), 0)
    c = lax.broadcasted_iota(jnp.int32, (t, t), 1)
    return ((r <= c) if upper else (r >= c)).astype(f32)


def fgate_fwd(z_t, b, name, t=512):
    hh, s = z_t.shape

    def body(z_ref, b_ref, c_ref):
        u = _tri(t, True)
        carry = jnp.zeros((hh, 1), f32)
        for blk in range(s // t):
            x = z_ref[:, blk * t:(blk + 1) * t] + b_ref[...]
            logf = jnp.minimum(x, 0.0) - jnp.log(1.0 + jnp.exp(-jnp.abs(x)))
            y = jnp.dot(logf, u, precision=lax.Precision.HIGHEST, preferred_element_type=f32) + carry
            c_ref[:, blk * t:(blk + 1) * t] = y
            carry = y[:, t - 1:t]

    return pl.pallas_call(body, out_shape=SDS((hh, s), f32), name=name, compiler_params=_cparams())(z_t, b)


def fgate_bwd(dc_t, z_t, b, name, t=512):
    hh, s = z_t.shape

    def body(dc_ref, z_ref, b_ref, dz_ref, db_ref):
        low = _tri(t, False)
        carry = jnp.zeros((hh, 1), f32)
        total = jnp.zeros((hh, 1), f32)
        for blk in reversed(range(s // t)):
            cols = slice(blk * t, (blk + 1) * t)
            y = jnp.dot(dc_ref[:, cols], low, precision=lax.Precision.HIGHEST, preferred_element_type=f32) + carry
            carry = y[:, 0:1]
            dz = y * _sigmoid(-(z_ref[:, cols] + b_ref[...]))
            dz_ref[:, cols] = dz
            total = total + jnp.sum(dz, axis=-1, keepdims=True)
        db_ref[...] = jnp.broadcast_to(total, db_ref.shape)

    return pl.pallas_call(body, out_shape=[SDS((hh, s), f32), SDS((hh, LANES), f32)], name=name,
                          compiler_params=_cparams())(dc_t, z_t, b)


def _pair_masks():
    lane = lax.broadcasted_iota(jnp.int32, (1, LANES), 1)
    return [lane < HEAD_DIM, lane >= HEAD_DIM]


def _tile_base(cr_ref, hh, lo):
    return cr_ref[hh:hh + 1, pl.ds(lo, LANES)][:, 0:1]


def fox_fwd(q, kv, c_row, name, tq=512):
    s = kv.shape[0]
    nq = s // tq

    def body(q_ref, k_ref, v_ref, cr_ref, o_ref, lse_ref):
        i = pl.program_id(1)
        qv = q_ref[...]
        masks = _pair_masks()
        row = lax.broadcasted_iota(jnp.int32, (tq, tq), 0)
        col = lax.broadcasted_iota(jnp.int32, (tq, tq), 1)
        qh = [jnp.where(masks[hh], qv, jnp.zeros((), bf16)) * ATT_SCALE for hh in range(2)]
        ct = [_tile_base(cr_ref, hh, pl.multiple_of(i * tq, tq)) for hh in range(2)]

        def block(j, carry, diag):
            lo = pl.multiple_of(j * tq, tq)
            ks = k_ref[pl.ds(lo, tq), :]
            vs = v_ref[pl.ds(lo, tq), :]
            out = []
            for hh in range(2):
                m, l, acc = carry[hh]
                sc = _dot_nt(qh[hh], ks) + (ct[hh] - cr_ref[hh:hh + 1, pl.ds(lo, tq)])
                if diag:
                    sc = jnp.where(col <= row, sc, -jnp.inf)
                m_new = jnp.maximum(m, jnp.max(sc, axis=-1, keepdims=True))
                alpha = jnp.exp(m - m_new)
                p = jnp.exp(sc - m_new)
                l = alpha * l + jnp.sum(p, axis=-1, keepdims=True)
                p_hi = p.astype(bf16)
                p_lo = (p - p_hi.astype(f32)).astype(bf16)
                acc = alpha * acc + (_dot(p_hi, vs) + _dot(p_lo, vs))
                out.append((m_new, l, acc))
            return tuple(out)

        init = (jnp.full((tq, 1), -jnp.inf, f32), jnp.zeros((tq, 1), f32), jnp.zeros((tq, LANES), f32))
        carry = lax.fori_loop(0, i, functools.partial(block, diag=False), (init, init))
        res = [(acc / l, m + jnp.log(l)) for m, l, acc in block(i, carry, True)]
        o_ref[...] = jnp.where(masks[0], res[0][0], res[1][0])
        lse_ref[...] = jnp.where(masks[0], res[0][1], res[1][1])

    return pl.pallas_call(
        body, grid=(FOX_PAIRS, nq),
        in_specs=[pl.BlockSpec((tq, LANES), lambda p, i: (i, p)), pl.BlockSpec((s, LANES), lambda p, i: (0, p)),
                  pl.BlockSpec((s, LANES), lambda p, i: (0, FOX_PAIRS + p)), pl.BlockSpec((None, 2, s), lambda p, i: (p, 0, 0))],
        out_specs=[pl.BlockSpec((tq, LANES), lambda p, i: (i, p)), pl.BlockSpec((None, tq, LANES), lambda p, i: (p, i, 0))],
        out_shape=[SDS((s, MAIN_WIDTH), f32), SDS((FOX_PAIRS, s, LANES), f32)], name=name,
        compiler_params=_cparams("parallel", "parallel"))(q, kv, kv, c_row)


def fox_bwd(q, kv, d_mixed, o, lse, c_row, name, tq=512):
    s = kv.shape[0]
    nq = s // tq

    def body(q_ref, k_ref, v_ref, do_ref, o_ref, lse_ref, cr_ref, dq_ref, dk_ref, dv_ref, dc_ref):
        j = pl.program_id(1)

        @pl.when(j == 0)
        def _():
            dq_ref[...] = jnp.zeros_like(dq_ref)

        masks = _pair_masks()
        sub = lax.broadcasted_iota(jnp.int32, (LANES, 1), 0)
        sub_masks = [sub < HEAD_DIM, sub >= HEAD_DIM]
        row = lax.broadcasted_iota(jnp.int32, (tq, tq), 0)
        col = lax.broadcasted_iota(jnp.int32, (tq, tq), 1)
        kj = k_ref[...]
        vj = v_ref[...]
        lo_j = pl.multiple_of(j * tq, tq)

        def block(i, carry, diag):
            dk_t, dv_t, dc0, dc1 = carry
            dcs = [dc0, dc1]
            lo = pl.multiple_of(i * tq, tq)
            qi = q_ref[pl.ds(lo, tq), :]
            qi = qi * ATT_SCALE
            qt_i = qi.T
            doi = do_ref[pl.ds(lo, tq), :]
            dot_i = doi.astype(bf16).T
            prod = doi.astype(bf16).astype(f32) * o_ref[pl.ds(lo, tq), :]
            lse_i = lse_ref[pl.ds(lo, tq), :]
            dq_i = jnp.zeros((tq, LANES), f32)
            for hh in range(2):
                qh = jnp.where(masks[hh], qi, jnp.zeros((), bf16))
                doh = jnp.where(masks[hh], doi, 0.0).astype(bf16)
                delta = jnp.sum(jnp.where(masks[hh], prod, 0.0), axis=-1, keepdims=True)
                sc = _dot_nt(qh, kj) + (_tile_base(cr_ref, hh, lo) - cr_ref[hh:hh + 1, pl.ds(lo_j, tq)])
                p = jnp.exp(sc - lse_i[:, hh * HEAD_DIM:hh * HEAD_DIM + 1])
                if diag:
                    p = jnp.where(col <= row, p, 0.0)
                dv_t = dv_t + _dot(jnp.where(sub_masks[hh], dot_i, jnp.zeros((), bf16)), p.astype(bf16))
                ds = p * (_dot_nt(doh, vj) - delta)
                dcs[hh] = dcs[hh] + jnp.sum(ds, axis=0, keepdims=True)
                dsb = ds.astype(bf16)
                dq_i = jnp.where(masks[hh], _dot(dsb, kj), dq_i)
                dk_t = dk_t + _dot(jnp.where(sub_masks[hh], qt_i, jnp.zeros((), bf16)), dsb)
            dq_ref[pl.ds(lo, tq), :] += dq_i * ATT_SCALE
            return dk_t, dv_t, dcs[0], dcs[1]

        zero = jnp.zeros((LANES, tq), f32)
        zrow = jnp.zeros((1, tq), f32)
        carry = block(j, (zero, zero, zrow, zrow), True)
        dk_t, dv_t, dc0, dc1 = lax.fori_loop(j + 1, nq, functools.partial(block, diag=False), carry)
        dk_ref[...] = dk_t.T.astype(bf16)
        dv_ref[...] = dv_t.T.astype(bf16)
        dc_ref[0:1, :] = -dc0
        dc_ref[1:2, :] = -dc1

    full = lambda p, j: (0, p)
    tile = lambda p, j: (j, p)
    return pl.pallas_call(
        body, grid=(FOX_PAIRS, nq),
        in_specs=[pl.BlockSpec((s, LANES), full), pl.BlockSpec((tq, LANES), tile), pl.BlockSpec((tq, LANES), lambda p, j: (j, FOX_PAIRS + p)),
                  pl.BlockSpec((s, LANES), full), pl.BlockSpec((s, LANES), full), pl.BlockSpec((None, s, LANES), lambda p, j: (p, 0, 0)),
                  pl.BlockSpec((None, 2, s), lambda p, j: (p, 0, 0))],
        out_specs=[pl.BlockSpec((s, LANES), full), pl.BlockSpec((tq, LANES), tile), pl.BlockSpec((tq, LANES), tile),
                   pl.BlockSpec((None, 2, tq), lambda p, j: (p, 0, j))],
        out_shape=[SDS((s, MAIN_WIDTH), f32), SDS((s, MAIN_WIDTH), bf16), SDS((s, MAIN_WIDTH), bf16), SDS((FOX_PAIRS, 2, s), f32)],
        name=name, compiler_params=_cparams("parallel", "arbitrary"))(q, kv, kv, d_mixed, o, lse, c_row)


def adamw(w, g, m, v, name, tr=256):
    r, c = w.shape
    tr = min(tr, r)
    assert r % tr == 0, (name, r, tr)
    c1 = 1.0 / (1.0 - ADAM_B1 ** ADAM_STEP)
    c2 = 1.0 / (1.0 - ADAM_B2 ** ADAM_STEP)

    def body(w_ref, g_ref, m_ref, v_ref, d_ref, mo_ref, vo_ref):
        gv = g_ref[...]
        mn = ADAM_B1 * m_ref[...] + (1.0 - ADAM_B1) * gv
        vn = ADAM_B2 * v_ref[...] + (1.0 - ADAM_B2) * gv * gv
        mo_ref[...] = mn
        vo_ref[...] = vn
        d_ref[...] = -ADAM_LR * ((mn * c1) / (jnp.sqrt(vn * c2) + ADAM_EPS) + ADAM_WD * w_ref[...])

    spec = pl.BlockSpec((tr, c), lambda i: (i, 0))
    return pl.pallas_call(body, grid=(r // tr,), in_specs=[spec] * 4, out_specs=[spec] * 3, out_shape=[SDS((r, c), f32)] * 3,
                          name=name, compiler_params=_cparams("parallel"))(w, g, m, v)


def sum_leading(x, name, out_dtype=f32, tr=None):
    n, r, c = x.shape
    tr = tr or r
    assert r % tr == 0

    def body(x_ref, o_ref):
        acc = x_ref[0].astype(f32)
        for k in range(1, n):
            acc = acc + x_ref[k].astype(f32)
        o_ref[...] = acc.astype(out_dtype)

    return pl.pallas_call(body, grid=(r // tr,), in_specs=[pl.BlockSpec((n, tr, c), lambda i: (0, i, 0))],
                          out_specs=pl.BlockSpec((tr, c), lambda i: (i, 0)), out_shape=SDS((r, c), out_dtype), name=name,
                          compiler_params=_cparams("parallel"))(x)


_ANY = pl.BlockSpec(memory_space=pl.ANY)
_DMA = pltpu.SemaphoreType.DMA


_HBM = pl.BlockSpec(memory_space=pltpu.HBM)
_SEM = pl.BlockSpec(memory_space=pltpu.SEMAPHORE)
_EFFECT = pltpu.SideEffectType.DATAFLOW_SIDE_EFFECTING
_FLIPS = [(0, 0, 1), (1, 0, 0), (0, 1, 0), (1, 1, 0), (1, 0, 1), (0, 1, 1), (1, 1, 1)]


def _me():
    return lax.axis_index("x"), lax.axis_index("y"), lax.axis_index("c")


def _peers():
    mx, my, mc = _me()
    return [(jnp.bitwise_xor(mx, fx), jnp.bitwise_xor(my, fy), jnp.bitwise_xor(mc, fc)) for fx, fy, fc in _FLIPS]


def _index(dev):
    return 4 * dev[0] + 2 * dev[1] + dev[2]


def _win(ref, axis, k, size, count=1):
    idx = [slice(None)] * len(ref.shape)
    idx[axis] = pl.ds(k * size, count * size)
    return ref.at[tuple(idx)]


def _hbm(a):
    return pltpu.with_memory_space_constraint(a, pltpu.HBM)


def _exchange_start(srcs, lands, copies_of, name):
    n = len(srcs)

    def body(*refs):
        src = refs[:n]
        send_sems, recv_sems, self_sems = refs[2 * n:2 * n + 3]
        land = refs[3 * n + 3:4 * n + 3]
        token = refs[4 * n + 3]
        me = _index(_me())
        for a in range(n):
            for s_ref, d_ref, peer in copies_of(a, src[a], land[a], me):
                if peer is None:
                    pltpu.make_async_copy(s_ref, d_ref, self_sems.at[a]).start()
                else:
                    pltpu.make_async_remote_copy(src_ref=s_ref, dst_ref=d_ref, send_sem=send_sems.at[a], recv_sem=recv_sems.at[a],
                                                 device_id=peer, device_id_type=MESH).start()
        token[...] = jnp.zeros_like(token)

    outs = pl.pallas_call(
        body, name=name,
        out_shape=(_DMA((n,)), _DMA((n,)), _DMA((n,)), *[pltpu.HBM(s.shape, s.dtype) for s in srcs],
                   *[pltpu.HBM(l.shape, l.dtype) for l in lands], SDS((8, LANES), f32)),
        in_specs=[_HBM] * (2 * n), out_specs=(_SEM, _SEM, _SEM, *[_HBM] * (2 * n), pl.BlockSpec(memory_space=pltpu.VMEM)),
        input_output_aliases={i: 3 + i for i in range(2 * n)},
        compiler_params=pltpu.CompilerParams(has_side_effects=_EFFECT),
    )(*[_hbm(s) for s in srcs], *[_hbm(lax.empty(l.shape, l.dtype)) for l in lands])
    return dict(sems=outs[:3], srcs=list(outs[3:3 + n]), lands=list(outs[3 + n:3 + 2 * n]), token=outs[3 + 2 * n])


def _exchange_wait(started, waits_of, after, name):
    srcs, lands = started["srcs"], started["lands"]
    n = len(srcs)

    def body(*refs):
        src = refs[:n]
        land = refs[n:2 * n]
        send_sems, recv_sems, self_sems = refs[2 * n:2 * n + 3]
        me = _index(_me())
        for a in range(n):
            seven, (s_ref, d_ref) = waits_of(a, src[a], land[a], me)
            both = pltpu.make_async_remote_copy(src_ref=seven, dst_ref=seven, send_sem=send_sems.at[a], recv_sem=recv_sems.at[a],
                                                device_id=_me(), device_id_type=MESH)
            both.wait_send()
            both.wait_recv()
            pltpu.make_async_copy(s_ref, d_ref, self_sems.at[a]).wait()

    outs = pl.pallas_call(
        body, name=name, out_shape=tuple(pltpu.HBM(t.shape, t.dtype) for t in srcs + lands),
        in_specs=[_HBM] * (2 * n) + [_SEM] * 3 + [_ANY], out_specs=tuple([_HBM] * (2 * n)),
        input_output_aliases={i: i for i in range(2 * n)},
        compiler_params=pltpu.CompilerParams(has_side_effects=_EFFECT),
    )(*srcs, *lands, *started["sems"], after)
    return list(outs[n:])


def gather_start(locs, axes, name):
    lands = [SDS(tuple(N_DEV * d if i == ax else d for i, d in enumerate(l.shape)), l.dtype) for l, ax in zip(locs, axes)]

    def copies_of(a, src, land, me):
        mine = _win(land, axes[a], me, src.shape[axes[a]])
        return [(src, mine, peer) for peer in _peers()] + [(src, mine, None)]

    return _exchange_start(locs, lands, copies_of, name)


def gather_wait(started, axes, after, name):
    def waits_of(a, src, land, me):
        size = src.shape[axes[a]]
        return _win(land, axes[a], 0, size, N_DEV - 1), (src, _win(land, axes[a], me, size))

    return _exchange_wait(started, waits_of, after, name)


def scatter_start(grads, axes, name):
    lands = [SDS((N_DEV,) + tuple(d // N_DEV if i == ax else d for i, d in enumerate(g.shape)), g.dtype) for g, ax in zip(grads, axes)]

    def copies_of(a, src, land, me):
        size = src.shape[axes[a]] // N_DEV
        out = [(_win(src, axes[a], _index(peer), size), land.at[me], peer) for peer in _peers()]
        return out + [(_win(src, axes[a], me, size), land.at[me], None)]

    return _exchange_start(grads, lands, copies_of, name)


def scatter_wait(started, axes, after, name):
    def waits_of(a, src, land, me):
        size = src.shape[axes[a]] // N_DEV
        return land.at[pl.ds(0, N_DEV - 1)], (_win(src, axes[a], me, size), land.at[me])

    return _exchange_wait(started, waits_of, after, name)


def _row_tile(rows, cap=512):
    return max(t for t in range(8, min(rows, cap) + 1, 8) if rows % t == 0)


_SMALL = [
    ("ln_mix_pre", (2, 1024)), ("ln_mix_post", (2, 1024)), ("ln_ffn_pre", (2, 1024)), ("ln_ffn_post", (2, 1024)),
    ("ln_mem", (2, 1024)), ("w_spatial", (1, 6, 128, 128)), ("b_spatial", (1, 6, 128)), ("ln_shared", (1024,)),
    ("b_forget", (12,)), ("ln_v_g", (1, 768)), ("ln_v_b", (1, 768)),
]
_SMALL_TILE = 8 * LANES


def _small_rows(shape):
    return -(-math.prod(shape) // _SMALL_TILE) * 8


def _pack_small(vals, shapes):
    parts = []
    for name, shape in shapes:
        flat = vals[name].reshape(-1).astype(f32)
        rows = _small_rows(shape)
        parts.append(jnp.pad(flat, (0, rows * LANES - flat.shape[0])).reshape(rows, LANES))
    return jnp.concatenate(parts, axis=0)


def _unpack_small(buf, shapes):
    out = {}
    lo = 0
    for name, shape in shapes:
        rows = _small_rows(shape)
        out[name] = buf[lo:lo + rows].reshape(-1)[:math.prod(shape)].reshape(shape)
        lo += rows
    return out


def kernel(x, mem, ln_mix_pre, ln_mix_post, ln_ffn_pre, ln_ffn_post, ln_mem, w_mem_kv, w_out, w_ffn_gate, w_ffn_up, w_ffn_down, w_in_a, w_spatial, b_spatial, ln_v_g, ln_v_b, ln_shared, w_shared_kv, b_forget, w_in_b, loss_target, m_ln_mix_pre, m_ln_mix_post, m_ln_ffn_pre, m_ln_ffn_post, m_ln_mem, m_w_mem_kv, m_w_out, m_w_ffn_gate, m_w_ffn_up, m_w_ffn_down, m_w_in_a, m_w_spatial, m_b_spatial, m_ln_v_g, m_ln_v_b, m_ln_shared, m_w_shared_kv, m_b_forget, m_w_in_b, v_ln_mix_pre, v_ln_mix_post, v_ln_ffn_pre, v_ln_ffn_post, v_ln_mem, v_w_mem_kv, v_w_out, v_w_ffn_gate, v_w_ffn_up, v_w_ffn_down, v_w_in_a, v_w_spatial, v_b_spatial, v_ln_v_g, v_ln_v_b, v_ln_shared, v_w_shared_kv, v_b_forget, v_w_in_b):
    weights = dict(ln_mix_pre=ln_mix_pre, ln_mix_post=ln_mix_post, ln_ffn_pre=ln_ffn_pre, ln_ffn_post=ln_ffn_post, ln_mem=ln_mem,
                   w_mem_kv=w_mem_kv, w_out=w_out, w_ffn_gate=w_ffn_gate, w_ffn_up=w_ffn_up, w_ffn_down=w_ffn_down, w_in_a=w_in_a,
                   w_spatial=w_spatial, b_spatial=b_spatial, ln_v_g=ln_v_g, ln_v_b=ln_v_b, ln_shared=ln_shared,
                   w_shared_kv=w_shared_kv, b_forget=b_forget, w_in_b=w_in_b)
    mom_m = dict(ln_mix_pre=m_ln_mix_pre, ln_mix_post=m_ln_mix_post, ln_ffn_pre=m_ln_ffn_pre, ln_ffn_post=m_ln_ffn_post, ln_mem=m_ln_mem,
                 w_mem_kv=m_w_mem_kv, w_out=m_w_out, w_ffn_gate=m_w_ffn_gate, w_ffn_up=m_w_ffn_up, w_ffn_down=m_w_ffn_down, w_in_a=m_w_in_a,
                 w_spatial=m_w_spatial, b_spatial=m_b_spatial, ln_v_g=m_ln_v_g, ln_v_b=m_ln_v_b, ln_shared=m_ln_shared,
                 w_shared_kv=m_w_shared_kv, b_forget=m_b_forget, w_in_b=m_w_in_b)
    mom_v = dict(ln_mix_pre=v_ln_mix_pre, ln_mix_post=v_ln_mix_post, ln_ffn_pre=v_ln_ffn_pre, ln_ffn_post=v_ln_ffn_post, ln_mem=v_ln_mem,
                 w_mem_kv=v_w_mem_kv, w_out=v_w_out, w_ffn_gate=v_w_ffn_gate, w_ffn_up=v_w_ffn_up, w_ffn_down=v_w_ffn_down, w_in_a=v_w_in_a,
                 w_spatial=v_w_spatial, b_spatial=v_b_spatial, ln_v_g=v_ln_v_g, ln_v_b=v_ln_v_b, ln_shared=v_ln_shared,
                 w_shared_kv=v_w_shared_kv, b_forget=v_b_forget, w_in_b=v_w_in_b)
    names = list(weights)
    mx, my, mc = lax.axis_index("x"), lax.axis_index("y"), lax.axis_index("c")
    me = 4 * mx + 2 * my + mc

    h0 = x[0]
    mem0 = mem[0]
    tgt = loss_target[0]
    seq = h0.shape[0]

    vec = lambda a: a.reshape(1, -1)
    pad_to = lambda a, axis, size: jnp.pad(a, [(0, size - a.shape[i] if i == axis else 0) for i in range(a.ndim)])

    def after(tok, a):
        return a + tok[0, 0].astype(a.dtype)

    lnv_loc = pad_to(jnp.concatenate([ln_v_g, ln_v_b], axis=0), 0, 8)
    st_a = gather_start([w_in_a.astype(bf16), pad_to(lnv_loc, 1, LANES)[None]], [0, 0], "gather_a_start")
    mix_locs = lambda l, tok: [after(tok, w_mem_kv[l]).astype(bf16), w_out[l].astype(bf16)]
    def ffn_gather_start(l, tok):
        gate_up = gather_start([pad_to(after(tok, w_ffn_gate[l]).astype(bf16), 1, FF_SHARD_PAD),
                                pad_to(w_ffn_up[l].astype(bf16), 1, FF_SHARD_PAD)], [1, 1], f"gather_gate_up{l}_start")
        down = gather_start([pad_to(after(gate_up["token"], w_ffn_down[l]).astype(bf16), 0, FF_SHARD_PAD)], [0], f"gather_down{l}_start")
        return gate_up, down

    st_b = [gather_start(mix_locs(0, st_a["token"]), [0, 0], "gather_b0_start"), None]
    st_c = ffn_gather_start(0, st_b[0]["token"])
    st_d = gather_start([after(st_c[1]["token"], w_in_b[0]).astype(bf16), pad_to(w_shared_kv.astype(bf16), 1, KV_PAD)], [0, 0],
                        "gather_d_start")
    st_b[1] = gather_start(mix_locs(1, st_d["token"]), [0, 0], "gather_b1_start")
    st_e = ffn_gather_start(1, st_b[1]["token"])
    ws = w_spatial[0].astype(bf16)
    ws_t = ws.transpose(0, 2, 1)
    bs_t = b_spatial[0].T

    (a0,) = rms_fwd(h0, [after(st_e[1]["token"], vec(ln_mix_pre[0]))], "a0_norm")
    w_in_a8, lnv8 = gather_wait(st_a, [0, 0], a0, "gather_a_wait")
    w_in_a_full = w_in_a8.transpose(1, 0, 2).reshape(D_MODEL, -1)
    lnv_g = lnv8[:, 0, :MAIN_WIDTH // N_DEV].reshape(1, MAIN_WIDTH)
    lnv_b = lnv8[:, 1, :MAIN_WIDTH // N_DEV].reshape(1, MAIN_WIDTH)
    proj0 = mm(a0, w_in_a_full, "proj0", tn=896)
    main0 = gmlp_fwd(proj0, ws, bs_t, lnv_g, lnv_b, "gmlp_fwd")
    w_mkv, w_o = [None, None], [None, None]
    w_mkv[0], w_o[0] = gather_wait(st_b[0], [0, 0], main0, "gather_b0_wait")
    (memn0,) = rms_fwd(mem0, [vec(ln_mem[0])], "mem0_norm")
    kvm0 = mm(memn0, w_mkv[0], "kvm0")
    om0 = mem_attn_fwd(proj0, 2 * MAIN_WIDTH // MEM_WIDTH, kvm0, "mem_attn0")
    mixed0 = jnp.concatenate([main0, om0], axis=-1)
    y1_0, hmid0, f0 = mm_resnorm(mixed0, w_o[0], h0, vec(ln_mix_post[0]), [vec(ln_ffn_pre[0])], "mix_out0")
    w_g0, w_u0 = gather_wait(st_c[0], [1, 1], f0, "gather_gate_up0_wait")
    gu0, act0 = ffn_up(f0, w_g0, w_u0, "ffn_up0")
    (w_d0,) = gather_wait(st_c[1], [0], act0, "gather_down0_wait")
    y2_0, h1, a1, sin1 = mm_resnorm(act0, w_d0, hmid0, vec(ln_ffn_post[0]), [vec(ln_mix_pre[1]), vec(ln_shared)], "ffn_down0")

    w_inb, w_kv = gather_wait(st_d, [0, 0], sin1, "gather_d_wait")
    kvb = mm(sin1, w_kv, "kv_shared", out_dtype=bf16, tn=MAIN_WIDTH, ncols=2 * MAIN_WIDTH)
    zf = mm(sin1, w_kv, "forget_logits", tn=256, col0=2 * MAIN_WIDTH, ncols=256)
    qb = mm(a1, w_inb, "proj1", out_dtype=bf16)
    z_t = jnp.pad(zf[:, :FOX_HEADS].T, ((0, 16 - FOX_HEADS), (0, 0)))
    bf_col = jnp.pad(b_forget, (0, 16 - FOX_HEADS)).reshape(16, 1)
    c_t = fgate_fwd(z_t, bf_col, "fgate_fwd")
    c_row = c_t[:FOX_HEADS].reshape(FOX_PAIRS, 2, seq)
    main1, lse = fox_fwd(qb, kvb, c_row, "fox_fwd")
    w_mkv[1], w_o[1] = gather_wait(st_b[1], [0, 0], main1, "gather_b1_wait")
    (memn1,) = rms_fwd(mem0, [vec(ln_mem[1])], "mem1_norm")
    kvm1 = mm(memn1, w_mkv[1], "kvm1")
    om1 = mem_attn_fwd(qb, MAIN_WIDTH // MEM_WIDTH, kvm1, "mem_attn1")
    mixed1 = jnp.concatenate([main1.astype(bf16), om1], axis=-1)
    y1_1, hmid1, f1 = mm_resnorm(mixed1, w_o[1], h1, vec(ln_mix_post[1]), [vec(ln_ffn_pre[1])], "mix_out1")
    w_g1, w_u1 = gather_wait(st_e[0], [1, 1], f1, "gather_gate_up1_wait")
    gu1, act1 = ffn_up(f1, w_g1, w_u1, "ffn_up1")
    (w_d1,) = gather_wait(st_e[1], [0], act1, "gather_down1_wait")
    y2_1, h2 = mm_resnorm(act1, w_d1, hmid1, vec(ln_ffn_post[1]), [], "ffn_down1")
    dh, loss_tile = loss_grad(h2, tgt, "loss")
    ffn_w = [(w_g0, w_u0, w_d0), (w_g1, w_u1, w_d1)]

    small = {}

    def ffn_backward(layer, dh_out, y2, hmid, f, gu, act, y1):
        w_g, w_u, w_d = ffn_w[layer]
        d_y2, dg_fpost = rms_bwd(y2, vec(ln_ffn_post[layer]), dh_out, None, bf16, f"ffn_post_bwd{layer}")
        dw_down = mm_tn(act, d_y2, f"dw_down{layer}", tk=256)
        rs_down = scatter_start([dw_down], [0], f"scatter_down{layer}_start")
        d_g, d_u = ffn_act_grad(d_y2, w_d, gu, f"ffn_act_grad{layer}")
        dw_g = mm_tn(f, d_g, f"dw_gate{layer}", dep=rs_down["token"])
        dw_u = mm_tn(f, d_u, f"dw_up{layer}")
        rs_gate_up = scatter_start([dw_g, dw_u], [1, 1], f"scatter_gate_up{layer}_start")
        dh_mid, d_y1, dg_fpre, dg_mpost = ffn_in_grad(d_g, d_u, w_g, w_u, hmid, dh_out, after(rs_gate_up["token"], vec(ln_ffn_pre[layer])),
                                                      y1, vec(ln_mix_post[layer]), f"ffn_in_grad{layer}")
        return dh_mid, d_y1, dg_fpost, dg_fpre, dg_mpost, (rs_down, rs_gate_up)

    def mix_out_backward(layer, d_y1, mixed):
        dw_out = mm_tn(mixed, d_y1, f"dw_out{layer}")
        d_mixed = mm(d_y1, w_o[layer], f"d_mixed{layer}", trans_b=True)
        return d_mixed, dw_out

    def mem_backward(layer, q_src, q_block, kvm, memn, d_mixed):
        d_qm, d_kvm = mem_attn_bwd(q_src, q_block, kvm, d_mixed, f"mem_attn_bwd{layer}")
        d_kvm_b = d_kvm.astype(bf16)
        dw_mkv = mm_tn(memn, d_kvm_b, f"dw_mem_kv{layer}")
        d_memn = mm(d_kvm_b, w_mkv[layer], f"d_memn{layer}", trans_b=True)
        _, dg_mem = rms_bwd(mem0, vec(ln_mem[layer]), d_memn, None, bf16, f"mem_norm_bwd{layer}")
        return d_qm, dw_mkv, dg_mem


    dh_mid1, d_y1_1, dg_fpost1, dg_fpre1, dg_mpost1, rs_ffn1 = ffn_backward(1, dh, y2_1, hmid1, f1, gu1, act1, y1_1)
    d_mixed1, dw_out1 = mix_out_backward(1, d_y1_1, mixed1)
    d_qm1, dw_mkv1, dg_mem1 = mem_backward(1, qb, MAIN_WIDTH // MEM_WIDTH, kvm1, memn1, d_mixed1)
    rs_mix1 = scatter_start([dw_out1, dw_mkv1], [0, 0], "scatter_mix1_start")
    dq, dk, dv, dc = fox_bwd(qb, kvb, d_mixed1, main1, lse, after(rs_mix1["token"], c_row), "fox_bwd")
    dc_t = jnp.pad(dc.reshape(FOX_HEADS, seq), ((0, 16 - FOX_HEADS), (0, 0)))
    dz_t, db_f = fgate_bwd(dc_t, z_t, bf_col, "fgate_bwd")
    d_kvf = jnp.concatenate([dk, dv, jnp.pad(dz_t[:FOX_HEADS].T.astype(bf16), ((0, 0), (0, KV_PAD - KV_WIDTH)))], axis=-1)
    d_proj1 = jnp.concatenate([dq.astype(bf16), d_qm1], axis=-1)
    dw_in_b = mm_tn(a1, d_proj1, "dw_in_b")
    dw_kv = mm_tn(sin1, d_kvf, "dw_kv", tn=896)
    rs_2 = scatter_start([dw_in_b, dw_kv], [0, 0], "scatter_shared_start")
    dh1, (dg_pre1, dg_shared) = proj_in_grad([(d_proj1, w_inb, vec(ln_mix_pre[1])), (d_kvf, w_kv, vec(ln_shared))], h1, dh_mid1,
                                             "in_grad1", dep=rs_2["token"])

    dh_mid0, d_y1_0, dg_fpost0, dg_fpre0, dg_mpost0, rs_ffn0 = ffn_backward(0, dh1, y2_0, hmid0, f0, gu0, act0, y1_0)
    d_mixed0, dw_out0 = mix_out_backward(0, d_y1_0, mixed0)
    d_qm0, dw_mkv0, dg_mem0 = mem_backward(0, proj0, 2 * MAIN_WIDTH // MEM_WIDTH, kvm0, memn0, d_mixed0)
    rs_mix0 = scatter_start([dw_out0, dw_mkv0], [0, 0], "scatter_mix0_start")
    d_uv, dw_s, db_s, dg_lnv, db_lnv = gmlp_bwd(proj0, d_mixed0, ws, ws_t, bs_t, after(rs_mix0["token"], lnv_g), lnv_b, "gmlp_bwd")

    small["ln_mix_pre"] = jnp.concatenate([jnp.zeros_like(dg_pre1), dg_pre1], axis=0)
    small["ln_mix_post"] = jnp.concatenate([dg_mpost0, dg_mpost1], axis=0)
    small["ln_ffn_pre"] = jnp.concatenate([dg_fpre0, dg_fpre1], axis=0)
    small["ln_ffn_post"] = jnp.concatenate([dg_fpost0, dg_fpost1], axis=0)
    small["ln_mem"] = jnp.concatenate([dg_mem0, dg_mem1], axis=0)
    small["w_spatial"] = dw_s[None]
    small["b_spatial"] = db_s[:, :A_GROUPS].T[None]
    small["ln_shared"] = dg_shared[0]
    small["b_forget"] = db_f[:FOX_HEADS, 0]
    small["ln_v_g"] = dg_lnv
    small["ln_v_b"] = db_lnv
    small_rows = jnp.concatenate([_pack_small(small, _SMALL), loss_tile], axis=0)
    st_small = gather_start([small_rows[None]], [0], "gather_small_grads_start")
    d_proj0 = jnp.concatenate([d_uv, after(st_small["token"], d_qm0)], axis=-1)
    dw_in_a = mm_tn(a0, d_proj0, "dw_in_a", tn=896)
    rs_in_a = scatter_start([dw_in_a.reshape(D_MODEL, N_DEV, -1).transpose(1, 0, 2)], [0], "scatter_in_a_start")
    grad_x, (dg_pre0,) = proj_in_grad([(d_proj0, w_in_a_full, vec(ln_mix_pre[0]))], h0, dh_mid0, "in_grad0", dep=rs_in_a["token"])
    st_last = gather_start([dg_pre0.reshape(1, 8, LANES)], [0], "gather_last_grad_start")

    def owned(started, axes, wait_after, name):
        recv = scatter_wait(started, axes, wait_after, name)
        return [sum_leading(r.reshape((N_DEV, -1, r.shape[-1])), f"{name}_sum{i}", tr=_row_tile(math.prod(r.shape[1:-1])))
                for i, r in enumerate(recv)]

    (g_down1,) = owned(rs_ffn1[0], [0], after(st_last["token"], grad_x[:8, :LANES]), "scatter_down1_wait")
    g_gu1 = owned(rs_ffn1[1], [1, 1], g_down1, "scatter_gate_up1_wait")
    g_mix1 = owned(rs_mix1, [0, 0], g_gu1[0], "scatter_mix1_wait")
    g2 = owned(rs_2, [0, 0], g_mix1[0], "scatter_shared_wait")
    (g_down0,) = owned(rs_ffn0[0], [0], g2[0], "scatter_down0_wait")
    g_gu0 = owned(rs_ffn0[1], [1, 1], g_down0, "scatter_gate_up0_wait")
    g_mix0 = owned(rs_mix0, [0, 0], g_gu0[0], "scatter_mix0_wait")
    (g_in_a,) = owned(rs_in_a, [0], g_mix0[0], "scatter_in_a_wait")
    g_local = dict(
        w_ffn_gate=jnp.stack([g_gu0[0], g_gu1[0]])[:, :, :FF_SHARD], w_ffn_up=jnp.stack([g_gu0[1], g_gu1[1]])[:, :, :FF_SHARD],
        w_ffn_down=jnp.stack([g_down0, g_down1])[:, :FF_SHARD], w_out=jnp.stack([g_mix0[0], g_mix1[0]]),
        w_mem_kv=jnp.stack([g_mix0[1], g_mix1[1]]), w_in_b=g2[0][None], w_shared_kv=g2[1][:, :KV_WIDTH], w_in_a=g_in_a[None])
    (small_all,) = gather_wait(st_small, [0], g_in_a, "gather_small_grads_wait")
    (last_all,) = gather_wait(st_last, [0], small_all, "gather_last_grad_wait")
    small_sum = sum_leading(small_all, "sum_small_grads")
    loss = small_sum[small_rows.shape[0] - 1, 0]
    g_small = _unpack_small(small_sum, _SMALL)
    g_small["ln_mix_pre"] = jnp.concatenate([sum_leading(last_all, "sum_last_grad").reshape(1, D_MODEL), g_small["ln_mix_pre"][1:]], axis=0)
    shard = MAIN_WIDTH // N_DEV
    for n in ("ln_v_g", "ln_v_b"):
        g_small[n] = lax.dynamic_slice_in_dim(g_small[n], me * shard, shard, axis=1)
    grad_w = {**g_small, **g_local}

    delta, new_m, new_v = {}, {}, {}
    for n in g_local:
        two_d = (-1, weights[n].shape[-1])
        d_, m_, v_ = adamw(weights[n].reshape(two_d), grad_w[n].reshape(two_d), mom_m[n].reshape(two_d), mom_v[n].reshape(two_d),
                           f"adamw_{n}", tr=_row_tile(math.prod(weights[n].shape[:-1])))
        delta[n], new_m[n], new_v[n] = (t.reshape(weights[n].shape) for t in (d_, m_, v_))
    small_local_shapes = [(n, tuple(weights[n].shape)) for n, _ in _SMALL]
    packed = [_pack_small(src, small_local_shapes) for src in (weights, grad_w, mom_m, mom_v)]
    outs = adamw(*packed, "adamw_small", tr=packed[0].shape[0])
    for dst, buf in zip((delta, new_m, new_v), outs):
        dst.update(_unpack_small(buf, small_local_shapes))

    return (loss, grad_x[None], *[grad_w[n] for n in names], *[delta[n] for n in names],
            *[new_m[n] for n in names], *[new_v[n] for n in names])
```

```python
import functools
import math

import jax
import jax.numpy as jnp
from jax import lax
from jax.experimental import pallas as pl
from jax.experimental.pallas import tpu as pltpu

f32 = jnp.float32
bf16 = jnp.bfloat16
SDS = jax.ShapeDtypeStruct

D_MODEL = 1024
MAIN_WIDTH = 768
MEM_WIDTH = 256
HEAD_DIM = 64
MEM_HEADS = 4
FOX_HEADS = 12
FOX_PAIRS = FOX_HEADS // 2
CHUNK = 128
A_GROUPS = 6
FF_SHARD = 352
FF_SHARD_PAD = 384
FF_PAD = 8 * FF_SHARD_PAD
KV_WIDTH = 2 * MAIN_WIDTH + FOX_HEADS
KV_PAD = 1792
RMS_EPS = 1e-6
LN_EPS = 1e-5
ATT_SCALE = HEAD_DIM ** -0.5
ADAM_LR, ADAM_B1, ADAM_B2, ADAM_EPS, ADAM_WD, ADAM_STEP = 0.001, 0.9, 0.999, 1e-08, 0.01, 10
N_DEV = 8
AXES = ("x", "y", "c")
MESH = pl.DeviceIdType.MESH
V7X_VMEM_LIMIT = 56 * 1024 * 1024
LANES = 128
FLAT_W = 512
ROW_PAD = 16


def _cparams(*sem):
    return pltpu.CompilerParams(dimension_semantics=sem or None, vmem_limit_bytes=V7X_VMEM_LIMIT)


def _dot(a, b):
    return jnp.dot(a, b, preferred_element_type=f32)


def _dot_nt(a, b):
    return lax.dot_general(a, b, (((1,), (1,)), ((), ())), preferred_element_type=f32)


def _dot_tn(a, b):
    return lax.dot_general(a, b, (((0,), (0,)), ((), ())), preferred_element_type=f32)


def _gelu(x):
    k = math.sqrt(2.0 / math.pi)
    t = jnp.tanh(k * (x + 0.044715 * x * x * x))
    return 0.5 * x * (1.0 + t), t


def _gelu_grad(x, t):
    k = math.sqrt(2.0 / math.pi)
    return 0.5 * (1.0 + t) + 0.5 * x * (1.0 - t * t) * k * (1.0 + 3.0 * 0.044715 * x * x)


def _sigmoid(x):
    return 1.0 / (1.0 + jnp.exp(-x))


def rms_fwd(x, gains, name, tm=512):
    m, d = x.shape
    tm = min(tm, m)
    n = len(gains)

    def body(x_ref, *refs):
        xv = x_ref[...]
        y = xv * lax.rsqrt(jnp.sum(xv * xv, axis=-1, keepdims=True) * (1.0 / d) + RMS_EPS)
        for g_ref, o_ref in zip(refs[:n], refs[n:]):
            o_ref[...] = (y * g_ref[...]).astype(bf16)

    row = pl.BlockSpec((tm, d), lambda i: (i, 0))
    vec = pl.BlockSpec((1, d), lambda i: (0, 0))
    return pl.pallas_call(body, grid=(m // tm,), in_specs=[row] + [vec] * n, out_specs=[row] * n,
                          out_shape=[SDS((m, d), bf16)] * n, name=name, compiler_params=_cparams("parallel"))(x, *gains)


def rms_bwd(x, g, dy, add, out_dtype, name, tm=512):
    m, d = x.shape
    tm = min(tm, m)
    has_add = add is not None

    def body(x_ref, g_ref, dy_ref, *refs):
        dx_ref, dg_ref = refs[-2], refs[-1]
        xv = x_ref[...]
        dyv = dy_ref[...].astype(f32)
        r = lax.rsqrt(jnp.sum(xv * xv, axis=-1, keepdims=True) * (1.0 / d) + RMS_EPS)
        xn = xv * r
        dyg = dyv * g_ref[...]
        dx = r * (dyg - xn * (jnp.sum(dyg * xn, axis=-1, keepdims=True) * (1.0 / d)))
        if has_add:
            dx = dx + refs[0][...]
        dx_ref[...] = dx.astype(out_dtype)

        @pl.when(pl.program_id(0) == 0)
        def _():
            dg_ref[...] = jnp.zeros_like(dg_ref)

        dg_ref[...] += jnp.sum(dyv * xn, axis=0, keepdims=True)

    row = pl.BlockSpec((tm, d), lambda i: (i, 0))
    vec = pl.BlockSpec((1, d), lambda i: (0, 0))
    ins = [x, g, dy] + ([add] if has_add else [])
    return pl.pallas_call(body, grid=(m // tm,), in_specs=[row, vec, row] + ([row] if has_add else []),
                          out_specs=[row, vec], out_shape=[SDS((m, d), out_dtype), SDS((1, d), f32)], name=name,
                          compiler_params=_cparams("arbitrary"))(*ins)


def loss_grad(h, tgt, name, tm=512):
    m, d = h.shape

    def body(h_ref, t_ref, dy_ref, l_ref):
        e = h_ref[...] - t_ref[...]
        dy_ref[...] = e * (1.0 / d)

        @pl.when(pl.program_id(0) == 0)
        def _():
            l_ref[...] = jnp.zeros_like(l_ref)

        part = jnp.sum(jnp.sum(e * e, axis=-1, keepdims=True), axis=0, keepdims=True) * (0.5 / d)
        l_ref[...] += jnp.broadcast_to(part, l_ref.shape)

    row = pl.BlockSpec((tm, d), lambda i: (i, 0))
    return pl.pallas_call(body, grid=(m // tm,), in_specs=[row, row],
                          out_specs=[row, pl.BlockSpec((8, LANES), lambda i: (0, 0))],
                          out_shape=[SDS((m, d), f32), SDS((8, LANES), f32)], name=name,
                          compiler_params=_cparams("arbitrary"))(h, tgt)


def mm(a, b, name, trans_b=False, out_dtype=f32, tm=512, tn=1024, layer=None, col0=0, ncols=None, dep=None):
    m, k = a.shape
    n_all = b.shape[-2] if trans_b else b.shape[-1]
    n = n_all if ncols is None else ncols
    tm, tn = min(tm, m), min(tn, n)
    assert m % tm == 0 and n % tn == 0 and col0 % tn == 0 and not (trans_b and col0), (name, m, n, tm, tn)
    jb = col0 // tn
    lead = () if layer is None else (None,)
    sel = () if layer is None else (layer,)

    def body(a_ref, b_ref, *rest):
        r = _dot_nt(a_ref[...], b_ref[...]) if trans_b else _dot(a_ref[...], b_ref[...])
        rest[-1][...] = r.astype(out_dtype)

    if trans_b:
        b_spec = pl.BlockSpec(lead + (tn, k), lambda j, i: sel + (j, 0))
    else:
        b_spec = pl.BlockSpec(lead + (k, tn), lambda j, i: sel + (0, jb + j))
    deps = [] if dep is None else [dep]
    dep_specs = [pl.BlockSpec((8, LANES), lambda j, i: (0, 0))] * len(deps)
    return pl.pallas_call(body, grid=(n // tn, m // tm), in_specs=[pl.BlockSpec((tm, k), lambda j, i: (i, 0)), b_spec] + dep_specs,
                          out_specs=pl.BlockSpec((tm, tn), lambda j, i: (i, j)), out_shape=SDS((m, n), out_dtype),
                          name=name, compiler_params=_cparams("parallel", "parallel"))(a, b, *deps)


def mm_tn(a, g, name, tk=512, tn=1024, out_dtype=bf16, dep=None):
    s, k = a.shape
    n = g.shape[1]
    tk, tn = min(tk, k), min(tn, n)
    assert k % tk == 0 and n % tn == 0, (name, k, n, tk, tn)

    def body(a_ref, g_ref, *rest):
        rest[-1][...] = _dot_tn(a_ref[...], g_ref[...]).astype(out_dtype)

    deps = [] if dep is None else [dep]
    dep_specs = [pl.BlockSpec((8, LANES), lambda i, j: (0, 0))] * len(deps)
    return pl.pallas_call(body, grid=(k // tk, n // tn),
                          in_specs=[pl.BlockSpec((s, tk), lambda i, j: (0, i)), pl.BlockSpec((s, tn), lambda i, j: (0, j))] + dep_specs,
                          out_specs=pl.BlockSpec((tk, tn), lambda i, j: (i, j)), out_shape=SDS((k, n), out_dtype), name=name,
                          compiler_params=_cparams("parallel", "parallel"))(a, g, *deps)


def _resident(shape, index_map):
    return pl.BlockSpec(shape, index_map, pipeline_mode=pl.Buffered(1))


def _rms(xv):
    return xv * lax.rsqrt(jnp.sum(xv * xv, axis=-1, keepdims=True) * (1.0 / xv.shape[-1]) + RMS_EPS)


def _rms_bwd_math(xv, g, dy):
    d = xv.shape[-1]
    r = lax.rsqrt(jnp.sum(xv * xv, axis=-1, keepdims=True) * (1.0 / d) + RMS_EPS)
    xn = xv * r
    dyg = dy * g
    dx = r * (dyg - xn * (jnp.sum(dyg * xn, axis=-1, keepdims=True) * (1.0 / d)))
    return dx, jnp.sum(dy * xn, axis=0, keepdims=True)


SUB_ROWS = 256


def mm_resnorm(a, b, h, g_post, gains, name, tm=512):
    m, k = a.shape
    d = b.shape[1]
    n = len(gains)

    def body(a_ref, b_ref, h_ref, gp_ref, *refs):
        for r in range(tm // SUB_ROWS):
            rows = slice(r * SUB_ROWS, (r + 1) * SUB_ROWS)
            y = _dot(a_ref[rows, :], b_ref[...])
            refs[n][rows, :] = y
            hn = h_ref[rows, :] + _rms(y) * gp_ref[...]
            refs[n + 1][rows, :] = hn
            if n:
                z = _rms(hn)
                for g_ref, o_ref in zip(refs[:n], refs[n + 2:]):
                    o_ref[rows, :] = (z * g_ref[...]).astype(bf16)

    row = pl.BlockSpec((tm, d), lambda i: (i, 0))
    vec = pl.BlockSpec((1, d), lambda i: (0, 0))
    return pl.pallas_call(body, grid=(m // tm,),
                          in_specs=[pl.BlockSpec((tm, k), lambda i: (i, 0)), _resident((k, d), lambda i: (0, 0)), row, vec] + [vec] * n,
                          out_specs=[row] * (n + 2), out_shape=[SDS((m, d), f32)] * 2 + [SDS((m, d), bf16)] * n, name=name,
                          compiler_params=_cparams("parallel"))(a, b, h, g_post, *gains)


def ffn_act_grad(d_y2, w_d, factors, name, tm=512, tn=1536):
    s, d = d_y2.shape
    ff = w_d.shape[0]
    nb = ff // tn

    def body(a_ref, b_ref, g_ref, u_ref, dg_ref, du_ref):
        av = a_ref[...]
        tc = 256
        for c in range(tn // tc):
            cols = slice(c * tc, (c + 1) * tc)
            da = _dot_nt(av, b_ref[cols, :])
            dg_ref[:, cols] = (da * g_ref[:, cols].astype(f32)).astype(bf16)
            du_ref[:, cols] = (da * u_ref[:, cols].astype(f32)).astype(bf16)

    tile = pl.BlockSpec((tm, tn), lambda j, i: (i, j))
    return pl.pallas_call(body, grid=(nb, s // tm),
                          in_specs=[pl.BlockSpec((tm, d), lambda j, i: (i, 0)), pl.BlockSpec((tn, d), lambda j, i: (j, 0)), tile,
                                    pl.BlockSpec((tm, tn), lambda j, i: (i, nb + j))],
                          out_specs=[tile, tile], out_shape=[SDS((s, ff), bf16)] * 2, name=name,
                          compiler_params=_cparams("parallel", "parallel"))(d_y2, w_d, factors, factors)


def ffn_in_grad(d_g, d_u, w_g, w_u, hmid, dh_out, g_pre, y1, g_post, name, tm=512):
    s, ff = d_g.shape
    d = w_g.shape[0]

    def body(dg_ref, du_ref, wg_ref, wu_ref, hm_ref, dho_ref, gpre_ref, y1_ref, gpost_ref, dhm_ref, dy1_ref, dgpre_ref, dgpost_ref):
        @pl.when(pl.program_id(0) == 0)
        def _():
            dgpre_ref[...] = jnp.zeros_like(dgpre_ref)
            dgpost_ref[...] = jnp.zeros_like(dgpost_ref)

        for r in range(tm // SUB_ROWS):
            rows = slice(r * SUB_ROWS, (r + 1) * SUB_ROWS)
            d_f = _dot_nt(dg_ref[rows, :], wg_ref[...]) + _dot_nt(du_ref[rows, :], wu_ref[...])
            dx, dg1 = _rms_bwd_math(hm_ref[rows, :], gpre_ref[...], d_f)
            dh_mid = dho_ref[rows, :] + dx
            dhm_ref[rows, :] = dh_mid
            dgpre_ref[...] += dg1
            dy1, dg2 = _rms_bwd_math(y1_ref[rows, :], gpost_ref[...], dh_mid)
            dy1_ref[rows, :] = dy1.astype(bf16)
            dgpost_ref[...] += dg2

    row = pl.BlockSpec((tm, d), lambda i: (i, 0))
    vec = pl.BlockSpec((1, d), lambda i: (0, 0))
    wide = pl.BlockSpec((tm, ff), lambda i: (i, 0))
    w_spec = _resident((d, ff), lambda i: (0, 0))
    return pl.pallas_call(body, grid=(s // tm,), in_specs=[wide, wide, w_spec, w_spec, row, row, vec, row, vec],
                          out_specs=[row, row, vec, vec], out_shape=[SDS((s, d), f32), SDS((s, d), bf16), SDS((1, d), f32), SDS((1, d), f32)],
                          name=name, compiler_params=_cparams("arbitrary"))(d_g, d_u, w_g, w_u, hmid, dh_out, g_pre, y1, g_post)


def proj_in_grad(pairs, x, add, name, tm=512, dep=None):
    s, d = x.shape
    n = len(pairs)
    deps = [] if dep is None else [dep]

    def body(*refs):
        x_ref, add_ref = refs[3 * n], refs[3 * n + 1]
        outs = refs[3 * n + 2 + len(deps):]

        @pl.when(pl.program_id(0) == 0)
        def _():
            for o in outs[1:]:
                o[...] = jnp.zeros_like(o)

        for r in range(tm // SUB_ROWS):
            rows = slice(r * SUB_ROWS, (r + 1) * SUB_ROWS)
            xv = x_ref[rows, :]
            dx = add_ref[rows, :]
            for i in range(n):
                a_ref, b_ref, g_ref = refs[3 * i:3 * i + 3]
                dxi, dgi = _rms_bwd_math(xv, g_ref[...], _dot_nt(a_ref[rows, :], b_ref[...]))
                dx = dx + dxi
                outs[1 + i][...] += dgi
            outs[0][rows, :] = dx

    row = pl.BlockSpec((tm, d), lambda i: (i, 0))
    vec = pl.BlockSpec((1, d), lambda i: (0, 0))
    in_specs, args = [], []
    for a, b, g in pairs:
        k = a.shape[1]
        in_specs += [pl.BlockSpec((tm, k), lambda i: (i, 0)), _resident((d, k), lambda i: (0, 0)), vec]
        args += [a, b, g]
    in_specs += [row, row] + [pl.BlockSpec((8, LANES), lambda i: (0, 0))] * len(deps)
    out = pl.pallas_call(body, grid=(s // tm,), in_specs=in_specs, out_specs=[row] + [vec] * n,
                         out_shape=[SDS((s, d), f32)] + [SDS((1, d), f32)] * n, name=name,
                         compiler_params=_cparams("arbitrary"))(*args, x, add, *deps)
    return out[0], out[1:]


def ffn_up(f, wg, wu, name, tm=512, tc=256):
    s, d = f.shape
    ff = wg.shape[-1]

    def body(f_ref, wg_ref, wu_ref, fac_ref, act_ref):
        fv = f_ref[...]
        for j in range(ff // tc):
            lo = j * tc
            gg = _dot(fv, wg_ref[:, lo:lo + tc])
            uu = _dot(fv, wu_ref[:, lo:lo + tc])
            sg = _sigmoid(gg)
            silu = gg * sg
            fac_ref[:, lo:lo + tc] = (uu * (sg + silu * (1.0 - sg))).astype(bf16)
            fac_ref[:, ff + lo:ff + lo + tc] = silu.astype(bf16)
            act_ref[:, lo:lo + tc] = (silu * uu).astype(bf16)

    w_spec = _resident((d, ff), lambda i: (0, 0))
    return pl.pallas_call(body, grid=(s // tm,), in_specs=[pl.BlockSpec((tm, d), lambda i: (i, 0)), w_spec, w_spec],
                          out_specs=[pl.BlockSpec((tm, 2 * ff), lambda i: (i, 0)), pl.BlockSpec((tm, ff), lambda i: (i, 0))],
                          out_shape=[SDS((s, 2 * ff), bf16), SDS((s, ff), bf16)], name=name,
                          compiler_params=_cparams("parallel"))(f, wg, wu)


def _gmlp_forward_chunk(u, v, w_refs, bias, ln_g, ln_b):
    gu, tu = _gelu(u)
    gv, tv = _gelu(v)
    mu = jnp.sum(gv, axis=-1, keepdims=True) * (1.0 / MAIN_WIDTH)
    xc = gv - mu
    rstd = lax.rsqrt(jnp.sum(xc * xc, axis=-1, keepdims=True) * (1.0 / MAIN_WIDTH) + LN_EPS)
    xhat = xc * rstd
    vln = xhat * ln_g + ln_b
    row = lax.broadcasted_iota(jnp.int32, (CHUNK, CHUNK), 0)
    col = lax.broadcasted_iota(jnp.int32, (CHUNK, CHUNK), 1)
    s_parts = []
    for g in range(A_GROUPS):
        w = jnp.where(col <= row, w_refs[g], jnp.zeros((), bf16))
        s_parts.append(_dot(w, vln[:, g * CHUNK:(g + 1) * CHUNK].astype(bf16)) + bias[:, g:g + 1])
    return gu, tu, tv, rstd, xhat, vln, s_parts


def gmlp_fwd(proj, ws, bs_t, ln_g, ln_b, name, tm=512):
    s = proj.shape[0]

    def body(u_ref, v_ref, w_ref, b_ref, g_ref, bb_ref, o_ref):
        bias = b_ref[...]
        for c in range(tm // CHUNK):
            rows = slice(c * CHUNK, (c + 1) * CHUNK)
            gu, _, _, _, _, _, s_parts = _gmlp_forward_chunk(u_ref[rows, :], v_ref[rows, :], w_ref, bias, g_ref[...], bb_ref[...])
            for g in range(A_GROUPS):
                cols = slice(g * CHUNK, (g + 1) * CHUNK)
                o_ref[rows, cols] = (gu[:, cols] * s_parts[g]).astype(bf16)

    vec = pl.BlockSpec((1, MAIN_WIDTH), lambda i: (0, 0))
    return pl.pallas_call(
        body, grid=(s // tm,),
        in_specs=[pl.BlockSpec((tm, MAIN_WIDTH), lambda i: (i, 0)), pl.BlockSpec((tm, MAIN_WIDTH), lambda i: (i, 1)),
                  pl.BlockSpec((A_GROUPS, CHUNK, CHUNK), lambda i: (0, 0, 0)), pl.BlockSpec((CHUNK, A_GROUPS), lambda i: (0, 0)), vec, vec],
        out_specs=pl.BlockSpec((tm, MAIN_WIDTH), lambda i: (i, 0)), out_shape=SDS((s, MAIN_WIDTH), bf16), name=name,
        compiler_params=_cparams("parallel"))(proj, proj, ws, bs_t, ln_g, ln_b)


def gmlp_bwd(proj, d_mixed, ws, ws_t, bs_t, ln_g, ln_b, name, tm=512):
    s = proj.shape[0]

    def body(u_ref, v_ref, dm_ref, w_ref, wt_ref, b_ref, g_ref, bb_ref, duv_ref, dw_ref, db_ref, dg_ref, dbb_ref):
        @pl.when(pl.program_id(0) == 0)
        def _():
            dw_ref[...] = jnp.zeros_like(dw_ref)
            db_ref[...] = jnp.zeros_like(db_ref)
            dg_ref[...] = jnp.zeros_like(dg_ref)
            dbb_ref[...] = jnp.zeros_like(dbb_ref)

        bias = b_ref[...]
        ln_gv = g_ref[...]
        row = lax.broadcasted_iota(jnp.int32, (CHUNK, CHUNK), 0)
        col = lax.broadcasted_iota(jnp.int32, (CHUNK, CHUNK), 1)
        lane = lax.broadcasted_iota(jnp.int32, (CHUNK, LANES), 1)
        for c in range(tm // CHUNK):
            rows = slice(c * CHUNK, (c + 1) * CHUNK)
            u = u_ref[rows, :]
            v = v_ref[rows, :]
            gu, tu, tv, rstd, xhat, vln, s_parts = _gmlp_forward_chunk(u, v, w_ref, bias, ln_gv, bb_ref[...])
            dm = dm_ref[rows, :]
            d_vln_parts = []
            d_gu_parts = []
            db_acc = jnp.zeros((CHUNK, LANES), f32)
            for g in range(A_GROUPS):
                cols = slice(g * CHUNK, (g + 1) * CHUNK)
                dmg = dm[:, cols]
                d_gu_parts.append(dmg * s_parts[g])
                d_s = dmg * gu[:, cols]
                db_acc = db_acc + jnp.where(lane == g, jnp.sum(d_s, axis=-1, keepdims=True), 0.0)
                d_sb = d_s.astype(bf16)
                dw_ref[g] += jnp.where(col <= row, _dot_nt(d_sb, vln[:, cols].astype(bf16)), 0.0)
                wt = jnp.where(row <= col, wt_ref[g], jnp.zeros((), bf16))
                d_vln_parts.append(_dot(wt, d_sb))
            db_ref[...] += db_acc
            d_vln = jnp.concatenate(d_vln_parts, axis=-1)
            d_gu = jnp.concatenate(d_gu_parts, axis=-1)
            dg_ref[...] += jnp.sum(d_vln * xhat, axis=0, keepdims=True)
            dbb_ref[...] += jnp.sum(d_vln, axis=0, keepdims=True)
            dxh = d_vln * ln_gv
            m1 = jnp.sum(dxh, axis=-1, keepdims=True) * (1.0 / MAIN_WIDTH)
            m2 = jnp.sum(dxh * xhat, axis=-1, keepdims=True) * (1.0 / MAIN_WIDTH)
            d_gv = rstd * (dxh - m1 - xhat * m2)
            duv_ref[rows, :MAIN_WIDTH] = (d_gu * _gelu_grad(u, tu)).astype(bf16)
            duv_ref[rows, MAIN_WIDTH:] = (d_gv * _gelu_grad(v, tv)).astype(bf16)

    vec = pl.BlockSpec((1, MAIN_WIDTH), lambda i: (0, 0))
    wspec = pl.BlockSpec((A_GROUPS, CHUNK, CHUNK), lambda i: (0, 0, 0))
    return pl.pallas_call(
        body, grid=(s // tm,),
        in_specs=[pl.BlockSpec((tm, MAIN_WIDTH), lambda i: (i, 0)), pl.BlockSpec((tm, MAIN_WIDTH), lambda i: (i, 1)),
                  pl.BlockSpec((tm, MAIN_WIDTH), lambda i: (i, 0)), wspec, wspec, pl.BlockSpec((CHUNK, A_GROUPS), lambda i: (0, 0)), vec, vec],
        out_specs=[pl.BlockSpec((tm, 2 * MAIN_WIDTH), lambda i: (i, 0)), wspec, pl.BlockSpec((CHUNK, LANES), lambda i: (0, 0)), vec, vec],
        out_shape=[SDS((s, 2 * MAIN_WIDTH), bf16), SDS((A_GROUPS, CHUNK, CHUNK), f32), SDS((CHUNK, LANES), f32),
                   SDS((1, MAIN_WIDTH), f32), SDS((1, MAIN_WIDTH), f32)],
        name=name, compiler_params=_cparams("arbitrary"))(proj, proj, d_mixed, ws, ws_t, bs_t, ln_g, ln_b)


def _head_mask(width, h):
    lane = lax.broadcasted_iota(jnp.int32, (1, width), 1)
    return (lane >= h * HEAD_DIM) & (lane < (h + 1) * HEAD_DIM)


def mem_attn_fwd(proj, q_block, kv, name, tm=512):
    s = proj.shape[0]
    n_mem = kv.shape[0]

    def body(q_ref, kv_ref, o_ref):
        q = q_ref[...].astype(f32)
        k = kv_ref[:, :MEM_WIDTH].astype(bf16)
        v = kv_ref[:, MEM_WIDTH:].astype(bf16)
        out = jnp.zeros((tm, MEM_WIDTH), f32)
        for h in range(MEM_HEADS):
            msk = _head_mask(MEM_WIDTH, h)
            qh = jnp.where(msk, q, 0.0).astype(bf16)
            sc = _dot_nt(qh, k) * ATT_SCALE
            e = jnp.exp(sc - jnp.max(sc, axis=-1, keepdims=True))
            p = e / jnp.sum(e, axis=-1, keepdims=True)
            out = jnp.where(msk, _dot(p.astype(bf16), v), out)
        o_ref[...] = out.astype(bf16)

    return pl.pallas_call(body, grid=(s // tm,),
                          in_specs=[pl.BlockSpec((tm, MEM_WIDTH), lambda i: (i, q_block)), pl.BlockSpec((n_mem, 2 * MEM_WIDTH), lambda i: (0, 0))],
                          out_specs=pl.BlockSpec((tm, MEM_WIDTH), lambda i: (i, 0)), out_shape=SDS((s, MEM_WIDTH), bf16), name=name,
                          compiler_params=_cparams("parallel"))(proj, kv)


def mem_attn_bwd(proj, q_block, kv, d_mixed, name, tm=512):
    s = proj.shape[0]
    n_mem = kv.shape[0]

    def body(q_ref, kv_ref, do_ref, dq_ref, dkv_ref):
        @pl.when(pl.program_id(0) == 0)
        def _():
            dkv_ref[...] = jnp.zeros_like(dkv_ref)

        q = q_ref[...].astype(f32)
        do = do_ref[...]
        k = kv_ref[:, :MEM_WIDTH].astype(bf16)
        v = kv_ref[:, MEM_WIDTH:].astype(bf16)
        dq = jnp.zeros((tm, MEM_WIDTH), f32)
        dk = jnp.zeros((n_mem, MEM_WIDTH), f32)
        dv = jnp.zeros((n_mem, MEM_WIDTH), f32)
        for h in range(MEM_HEADS):
            msk = _head_mask(MEM_WIDTH, h)
            qh = jnp.where(msk, q, 0.0).astype(bf16)
            doh = jnp.where(msk, do, 0.0).astype(bf16)
            sc = _dot_nt(qh, k) * ATT_SCALE
            e = jnp.exp(sc - jnp.max(sc, axis=-1, keepdims=True))
            p = e / jnp.sum(e, axis=-1, keepdims=True)
            dp = _dot_nt(doh, v)
            ds = p * (dp - jnp.sum(dp * p, axis=-1, keepdims=True))
            dsb = (ds * ATT_SCALE).astype(bf16)
            dq = jnp.where(msk, _dot(dsb, k), dq)
            dk = dk + _dot_tn(dsb, qh)
            dv = dv + _dot_tn(p.astype(bf16), doh)
        dq_ref[...] = dq.astype(bf16)
        dkv_ref[:, :MEM_WIDTH] += dk
        dkv_ref[:, MEM_WIDTH:] += dv

    return pl.pallas_call(
        body, grid=(s // tm,),
        in_specs=[pl.BlockSpec((tm, MEM_WIDTH), lambda i: (i, q_block)), pl.BlockSpec((n_mem, 2 * MEM_WIDTH), lambda i: (0, 0)),
                  pl.BlockSpec((tm, MEM_WIDTH), lambda i: (i, MAIN_WIDTH // MEM_WIDTH))],
        out_specs=[pl.BlockSpec((tm, MEM_WIDTH), lambda i: (i, 0)), pl.BlockSpec((n_mem, 2 * MEM_WIDTH), lambda i: (0, 0))],
        out_shape=[SDS((s, MEM_WIDTH), bf16), SDS((n_mem, 2 * MEM_WIDTH), f32)], name=name,
        compiler_params=_cparams("arbitrary"))(proj, kv, d_mixed)


def _tri(t, upper):
    r = lax.broadcasted_iota(jnp.int32, (t, t), 0)
    c = lax.broadcasted_iota(jnp.int32, (t, t), 1)
    return ((r <= c) if upper else (r >= c)).astype(f32)


def fgate_fwd(z_t, b, name, t=512):
    hh, s = z_t.shape

    def body(z_ref, b_ref, c_ref):
        u = _tri(t, True)
        carry = jnp.zeros((hh, 1), f32)
        for blk in range(s // t):
            x = z_ref[:, blk * t:(blk + 1) * t] + b_ref[...]
            logf = jnp.minimum(x, 0.0) - jnp.log(1.0 + jnp.exp(-jnp.abs(x)))
            y = jnp.dot(logf, u, precision=lax.Precision.HIGHEST, preferred_element_type=f32) + carry
            c_ref[:, blk * t:(blk + 1) * t] = y
            carry = y[:, t - 1:t]

    return pl.pallas_call(body, out_shape=SDS((hh, s), f32), name=name, compiler_params=_cparams())(z_t, b)


def fgate_bwd(dc_t, z_t, b, name, t=512):
    hh, s = z_t.shape

    def body(dc_ref, z_ref, b_ref, dz_ref, db_ref):
        low = _tri(t, False)
        carry = jnp.zeros((hh, 1), f32)
        total = jnp.zeros((hh, 1), f32)
        for blk in reversed(range(s // t)):
            cols = slice(blk * t, (blk + 1) * t)
            y = jnp.dot(dc_ref[:, cols], low, precision=lax.Precision.HIGHEST, preferred_element_type=f32) + carry
            carry = y[:, 0:1]
            dz = y * _sigmoid(-(z_ref[:, cols] + b_ref[...]))
            dz_ref[:, cols] = dz
            total = total + jnp.sum(dz, axis=-1, keepdims=True)
        db_ref[...] = jnp.broadcast_to(total, db_ref.shape)

    return pl.pallas_call(body, out_shape=[SDS((hh, s), f32), SDS((hh, LANES), f32)], name=name,
                          compiler_params=_cparams())(dc_t, z_t, b)


def _pair_masks():
    lane = lax.broadcasted_iota(jnp.int32, (1, LANES), 1)
    return [lane < HEAD_DIM, lane >= HEAD_DIM]


def _tile_base(cr_ref, hh, lo):
    return cr_ref[hh:hh + 1, pl.ds(lo, LANES)][:, 0:1]


def fox_fwd(q, kv, c_row, name, tq=512):
    s = kv.shape[0]
    nq = s // tq

    def body(q_ref, k_ref, v_ref, cr_ref, o_ref, lse_ref):
        i = pl.program_id(1)
        qv = q_ref[...]
        masks = _pair_masks()
        row = lax.broadcasted_iota(jnp.int32, (tq, tq), 0)
        col = lax.broadcasted_iota(jnp.int32, (tq, tq), 1)
        qh = [jnp.where(masks[hh], qv, jnp.zeros((), bf16)) * ATT_SCALE for hh in range(2)]
        ct = [_tile_base(cr_ref, hh, pl.multiple_of(i * tq, tq)) for hh in range(2)]

        def block(j, carry, diag):
            lo = pl.multiple_of(j * tq, tq)
            ks = k_ref[pl.ds(lo, tq), :]
            vs = v_ref[pl.ds(lo, tq), :]
            out = []
            for hh in range(2):
                m, l, acc = carry[hh]
                sc = _dot_nt(qh[hh], ks) + (ct[hh] - cr_ref[hh:hh + 1, pl.ds(lo, tq)])
                if diag:
                    sc = jnp.where(col <= row, sc, -jnp.inf)
                m_new = jnp.maximum(m, jnp.max(sc, axis=-1, keepdims=True))
                alpha = jnp.exp(m - m_new)
                p = jnp.exp(sc - m_new)
                l = alpha * l + jnp.sum(p, axis=-1, keepdims=True)
                p_hi = p.astype(bf16)
                p_lo = (p - p_hi.astype(f32)).astype(bf16)
                acc = alpha * acc + (_dot(p_hi, vs) + _dot(p_lo, vs))
                out.append((m_new, l, acc))
            return tuple(out)

        init = (jnp.full((tq, 1), -jnp.inf, f32), jnp.zeros((tq, 1), f32), jnp.zeros((tq, LANES), f32))
        carry = lax.fori_loop(0, i, functools.partial(block, diag=False), (init, init))
        res = [(acc / l, m + jnp.log(l)) for m, l, acc in block(i, carry, True)]
        o_ref[...] = jnp.where(masks[0], res[0][0], res[1][0])
        lse_ref[...] = jnp.where(masks[0], res[0][1], res[1][1])

    return pl.pallas_call(
        body, grid=(FOX_PAIRS, nq),
        in_specs=[pl.BlockSpec((tq, LANES), lambda p, i: (i, p)), pl.BlockSpec((s, LANES), lambda p, i: (0, p)),
                  pl.BlockSpec((s, LANES), lambda p, i: (0, FOX_PAIRS + p)), pl.BlockSpec((None, 2, s), lambda p, i: (p, 0, 0))],
        out_specs=[pl.BlockSpec((tq, LANES), lambda p, i: (i, p)), pl.BlockSpec((None, tq, LANES), lambda p, i: (p, i, 0))],
        out_shape=[SDS((s, MAIN_WIDTH), f32), SDS((FOX_PAIRS, s, LANES), f32)], name=name,
        compiler_params=_cparams("parallel", "parallel"))(q, kv, kv, c_row)


def fox_bwd(q, kv, d_mixed, o, lse, c_row, name, tq=512):
    s = kv.shape[0]
    nq = s // tq

    def body(q_ref, k_ref, v_ref, do_ref, o_ref, lse_ref, cr_ref, dq_ref, dk_ref, dv_ref, dc_ref):
        j = pl.program_id(1)

        @pl.when(j == 0)
        def _():
            dq_ref[...] = jnp.zeros_like(dq_ref)

        masks = _pair_masks()
        sub = lax.broadcasted_iota(jnp.int32, (LANES, 1), 0)
        sub_masks = [sub < HEAD_DIM, sub >= HEAD_DIM]
        row = lax.broadcasted_iota(jnp.int32, (tq, tq), 0)
        col = lax.broadcasted_iota(jnp.int32, (tq, tq), 1)
        kj = k_ref[...]
        vj = v_ref[...]
        lo_j = pl.multiple_of(j * tq, tq)

        def block(i, carry, diag):
            dk_t, dv_t, dc0, dc1 = carry
            dcs = [dc0, dc1]
            lo = pl.multiple_of(i * tq, tq)
            qi = q_ref[pl.ds(lo, tq), :]
            qi = qi * ATT_SCALE
            qt_i = qi.T
            doi = do_ref[pl.ds(lo, tq), :]
            dot_i = doi.astype(bf16).T
            prod = doi.astype(bf16).astype(f32) * o_ref[pl.ds(lo, tq), :]
            lse_i = lse_ref[pl.ds(lo, tq), :]
            dq_i = jnp.zeros((tq, LANES), f32)
            for hh in range(2):
                qh = jnp.where(masks[hh], qi, jnp.zeros((), bf16))
                doh = jnp.where(masks[hh], doi, 0.0).astype(bf16)
                delta = jnp.sum(jnp.where(masks[hh], prod, 0.0), axis=-1, keepdims=True)
                sc = _dot_nt(qh, kj) + (_tile_base(cr_ref, hh, lo) - cr_ref[hh:hh + 1, pl.ds(lo_j, tq)])
                p = jnp.exp(sc - lse_i[:, hh * HEAD_DIM:hh * HEAD_DIM + 1])
                if diag:
                    p = jnp.where(col <= row, p, 0.0)
                dv_t = dv_t + _dot(jnp.where(sub_masks[hh], dot_i, jnp.zeros((), bf16)), p.astype(bf16))
                ds = p * (_dot_nt(doh, vj) - delta)
                dcs[hh] = dcs[hh] + jnp.sum(ds, axis=0, keepdims=True)
                dsb = ds.astype(bf16)
                dq_i = jnp.where(masks[hh], _dot(dsb, kj), dq_i)
                dk_t = dk_t + _dot(jnp.where(sub_masks[hh], qt_i, jnp.zeros((), bf16)), dsb)
            dq_ref[pl.ds(lo, tq), :] += dq_i * ATT_SCALE
            return dk_t, dv_t, dcs[0], dcs[1]

        zero = jnp.zeros((LANES, tq), f32)
        zrow = jnp.zeros((1, tq), f32)
        carry = block(j, (zero, zero, zrow, zrow), True)
        dk_t, dv_t, dc0, dc1 = lax.fori_loop(j + 1, nq, functools.partial(block, diag=False), carry)
        dk_ref[...] = dk_t.T.astype(bf16)
        dv_ref[...] = dv_t.T.astype(bf16)
        dc_ref[0:1, :] = -dc0
        dc_ref[1:2, :] = -dc1

    full = lambda p, j: (0, p)
    tile = lambda p, j: (j, p)
    return pl.pallas_call(
        body, grid=(FOX_PAIRS, nq),
        in_specs=[pl.BlockSpec((s, LANES), full), pl.BlockSpec((tq, LANES), tile), pl.BlockSpec((tq, LANES), lambda p, j: (j, FOX_PAIRS + p)),
                  pl.BlockSpec((s, LANES), full), pl.BlockSpec((s, LANES), full), pl.BlockSpec((None, s, LANES), lambda p, j: (p, 0, 0)),
                  pl.BlockSpec((None, 2, s), lambda p, j: (p, 0, 0))],
        out_specs=[pl.BlockSpec((s, LANES), full), pl.BlockSpec((tq, LANES), tile), pl.BlockSpec((tq, LANES), tile),
                   pl.BlockSpec((None, 2, tq), lambda p, j: (p, 0, j))],
        out_shape=[SDS((s, MAIN_WIDTH), f32), SDS((s, MAIN_WIDTH), bf16), SDS((s, MAIN_WIDTH), bf16), SDS((FOX_PAIRS, 2, s), f32)],
        name=name, compiler_params=_cparams("parallel", "arbitrary"))(q, kv, kv, d_mixed, o, lse, c_row)


def adamw(w, g, m, v, name, tr=256):
    r, c = w.shape
    tr = min(tr, r)
    assert r % tr == 0, (name, r, tr)
    c1 = 1.0 / (1.0 - ADAM_B1 ** ADAM_STEP)
    c2 = 1.0 / (1.0 - ADAM_B2 ** ADAM_STEP)

    def body(w_ref, g_ref, m_ref, v_ref, d_ref, mo_ref, vo_ref):
        gv = g_ref[...]
        mn = ADAM_B1 * m_ref[...] + (1.0 - ADAM_B1) * gv
        vn = ADAM_B2 * v_ref[...] + (1.0 - ADAM_B2) * gv * gv
        mo_ref[...] = mn
        vo_ref[...] = vn
        d_ref[...] = -ADAM_LR * ((mn * c1) / (jnp.sqrt(vn * c2) + ADAM_EPS) + ADAM_WD * w_ref[...])

    spec = pl.BlockSpec((tr, c), lambda i: (i, 0))
    return pl.pallas_call(body, grid=(r // tr,), in_specs=[spec] * 4, out_specs=[spec] * 3, out_shape=[SDS((r, c), f32)] * 3,
                          name=name, compiler_params=_cparams("parallel"))(w, g, m, v)


def sum_leading(x, name, out_dtype=f32, tr=None):
    n, r, c = x.shape
    tr = tr or r
    assert r % tr == 0

    def body(x_ref, o_ref):
        acc = x_ref[0].astype(f32)
        for k in range(1, n):
            acc = acc + x_ref[k].astype(f32)
        o_ref[...] = acc.astype(out_dtype)

    return pl.pallas_call(body, grid=(r // tr,), in_specs=[pl.BlockSpec((n, tr, c), lambda i: (0, i, 0))],
                          out_specs=pl.BlockSpec((tr, c), lambda i: (i, 0)), out_shape=SDS((r, c), out_dtype), name=name,
                          compiler_params=_cparams("parallel"))(x)


_ANY = pl.BlockSpec(memory_space=pl.ANY)
_DMA = pltpu.SemaphoreType.DMA


_HBM = pl.BlockSpec(memory_space=pltpu.HBM)
_SEM = pl.BlockSpec(memory_space=pltpu.SEMAPHORE)
_EFFECT = pltpu.SideEffectType.DATAFLOW_SIDE_EFFECTING
_FLIPS = [(0, 0, 1), (1, 0, 0), (0, 1, 0), (1, 1, 0), (1, 0, 1), (0, 1, 1), (1, 1, 1)]


def _me():
    return lax.axis_index("x"), lax.axis_index("y"), lax.axis_index("c")


def _peers():
    mx, my, mc = _me()
    return [(jnp.bitwise_xor(mx, fx), jnp.bitwise_xor(my, fy), jnp.bitwise_xor(mc, fc)) for fx, fy, fc in _FLIPS]


def _index(dev):
    return 4 * dev[0] + 2 * dev[1] + dev[2]


def _win(ref, axis, k, size, count=1):
    idx = [slice(None)] * len(ref.shape)
    idx[axis] = pl.ds(k * size, count * size)
    return ref.at[tuple(idx)]


def _hbm(a):
    return pltpu.with_memory_space_constraint(a, pltpu.HBM)


def _exchange_start(srcs, lands, copies_of, name):
    n = len(srcs)

    def body(*refs):
        src = refs[:n]
        send_sems, recv_sems, self_sems = refs[2 * n:2 * n + 3]
        land = refs[3 * n + 3:4 * n + 3]
        token = refs[4 * n + 3]
        me = _index(_me())
        for a in range(n):
            for s_ref, d_ref, peer in copies_of(a, src[a], land[a], me):
                if peer is None:
                    pltpu.make_async_copy(s_ref, d_ref, self_sems.at[a]).start()
                else:
                    pltpu.make_async_remote_copy(src_ref=s_ref, dst_ref=d_ref, send_sem=send_sems.at[a], recv_sem=recv_sems.at[a],
                                                 device_id=peer, device_id_type=MESH).start()
        token[...] = jnp.zeros_like(token)

    outs = pl.pallas_call(
        body, name=name,
        out_shape=(_DMA((n,)), _DMA((n,)), _DMA((n,)), *[pltpu.HBM(s.shape, s.dtype) for s in srcs],
                   *[pltpu.HBM(l.shape, l.dtype) for l in lands], SDS((8, LANES), f32)),
        in_specs=[_HBM] * (2 * n), out_specs=(_SEM, _SEM, _SEM, *[_HBM] * (2 * n), pl.BlockSpec(memory_space=pltpu.VMEM)),
        input_output_aliases={i: 3 + i for i in range(2 * n)},
        compiler_params=pltpu.CompilerParams(has_side_effects=_EFFECT),
    )(*[_hbm(s) for s in srcs], *[_hbm(lax.empty(l.shape, l.dtype)) for l in lands])
    return dict(sems=outs[:3], srcs=list(outs[3:3 + n]), lands=list(outs[3 + n:3 + 2 * n]), token=outs[3 + 2 * n])


def _exchange_wait(started, waits_of, after, name):
    srcs, lands = started["srcs"], started["lands"]
    n = len(srcs)

    def body(*refs):
        src = refs[:n]
        land = refs[n:2 * n]
        send_sems, recv_sems, self_sems = refs[2 * n:2 * n + 3]
        me = _index(_me())
        for a in range(n):
            seven, (s_ref, d_ref) = waits_of(a, src[a], land[a], me)
            both = pltpu.make_async_remote_copy(src_ref=seven, dst_ref=seven, send_sem=send_sems.at[a], recv_sem=recv_sems.at[a],
                                                device_id=_me(), device_id_type=MESH)
            both.wait_send()
            both.wait_recv()
            pltpu.make_async_copy(s_ref, d_ref, self_sems.at[a]).wait()

    outs = pl.pallas_call(
        body, name=name, out_shape=tuple(pltpu.HBM(t.shape, t.dtype) for t in srcs + lands),
        in_specs=[_HBM] * (2 * n) + [_SEM] * 3 + [_ANY], out_specs=tuple([_HBM] * (2 * n)),
        input_output_aliases={i: i for i in range(2 * n)},
        compiler_params=pltpu.CompilerParams(has_side_effects=_EFFECT),
    )(*srcs, *lands, *started["sems"], after)
    return list(outs[n:])


def gather_start(locs, axes, name):
    lands = [SDS(tuple(N_DEV * d if i == ax else d for i, d in enumerate(l.shape)), l.dtype) for l, ax in zip(locs, axes)]

    def copies_of(a, src, land, me):
        mine = _win(land, axes[a], me, src.shape[axes[a]])
        return [(src, mine, peer) for peer in _peers()] + [(src, mine, None)]

    return _exchange_start(locs, lands, copies_of, name)


def gather_wait(started, axes, after, name):
    def waits_of(a, src, land, me):
        size = src.shape[axes[a]]
        return _win(land, axes[a], 0, size, N_DEV - 1), (src, _win(land, axes[a], me, size))

    return _exchange_wait(started, waits_of, after, name)


def scatter_start(grads, axes, name):
    lands = [SDS((N_DEV,) + tuple(d // N_DEV if i == ax else d for i, d in enumerate(g.shape)), g.dtype) for g, ax in zip(grads, axes)]

    def copies_of(a, src, land, me):
        size = src.shape[axes[a]] // N_DEV
        out = [(_win(src, axes[a], _index(peer), size), land.at[me], peer) for peer in _peers()]
        return out + [(_win(src, axes[a], me, size), land.at[me], None)]

    return _exchange_start(grads, lands, copies_of, name)


def scatter_wait(started, axes, after, name):
    def waits_of(a, src, land, me):
        size = src.shape[axes[a]] // N_DEV
        return land.at[pl.ds(0, N_DEV - 1)], (_win(src, axes[a], me, size), land.at[me])

    return _exchange_wait(started, waits_of, after, name)


def _row_tile(rows, cap=512):
    return max(t for t in range(8, min(rows, cap) + 1, 8) if rows % t == 0)


_SMALL = [
    ("ln_mix_pre", (2, 1024)), ("ln_mix_post", (2, 1024)), ("ln_ffn_pre", (2, 1024)), ("ln_ffn_post", (2, 1024)),
    ("ln_mem", (2, 1024)), ("w_spatial", (1, 6, 128, 128)), ("b_spatial", (1, 6, 128)), ("ln_shared", (1024,)),
    ("b_forget", (12,)), ("ln_v_g", (1, 768)), ("ln_v_b", (1, 768)),
]
_SMALL_TILE = 8 * LANES


def _small_rows(shape):
    return -(-math.prod(shape) // _SMALL_TILE) * 8


def _pack_small(vals, shapes):
    parts = []
    for name, shape in shapes:
        flat = vals[name].reshape(-1).astype(f32)
        rows = _small_rows(shape)
        parts.append(jnp.pad(flat, (0, rows * LANES - flat.shape[0])).reshape(rows, LANES))
    return jnp.concatenate(parts, axis=0)


def _unpack_small(buf, shapes):
    out = {}
    lo = 0
    for name, shape in shapes:
        rows = _small_rows(shape)
        out[name] = buf[lo:lo + rows].reshape(-1)[:math.prod(shape)].reshape(shape)
        lo += rows
    return out


def kernel(x, mem, ln_mix_pre, ln_mix_post, ln_ffn_pre, ln_ffn_post, ln_mem, w_mem_kv, w_out, w_ffn_gate, w_ffn_up, w_ffn_down, w_in_a, w_spatial, b_spatial, ln_v_g, ln_v_b, ln_shared, w_shared_kv, b_forget, w_in_b, loss_target, m_ln_mix_pre, m_ln_mix_post, m_ln_ffn_pre, m_ln_ffn_post, m_ln_mem, m_w_mem_kv, m_w_out, m_w_ffn_gate, m_w_ffn_up, m_w_ffn_down, m_w_in_a, m_w_spatial, m_b_spatial, m_ln_v_g, m_ln_v_b, m_ln_shared, m_w_shared_kv, m_b_forget, m_w_in_b, v_ln_mix_pre, v_ln_mix_post, v_ln_ffn_pre, v_ln_ffn_post, v_ln_mem, v_w_mem_kv, v_w_out, v_w_ffn_gate, v_w_ffn_up, v_w_ffn_down, v_w_in_a, v_w_spatial, v_b_spatial, v_ln_v_g, v_ln_v_b, v_ln_shared, v_w_shared_kv, v_b_forget, v_w_in_b):
    weights = dict(ln_mix_pre=ln_mix_pre, ln_mix_post=ln_mix_post, ln_ffn_pre=ln_ffn_pre, ln_ffn_post=ln_ffn_post, ln_mem=ln_mem,
                   w_mem_kv=w_mem_kv, w_out=w_out, w_ffn_gate=w_ffn_gate, w_ffn_up=w_ffn_up, w_ffn_down=w_ffn_down, w_in_a=w_in_a,
                   w_spatial=w_spatial, b_spatial=b_spatial, ln_v_g=ln_v_g, ln_v_b=ln_v_b, ln_shared=ln_shared,
                   w_shared_kv=w_shared_kv, b_forget=b_forget, w_in_b=w_in_b)
    mom_m = dict(ln_mix_pre=m_ln_mix_pre, ln_mix_post=m_ln_mix_post, ln_ffn_pre=m_ln_ffn_pre, ln_ffn_post=m_ln_ffn_post, ln_mem=m_ln_mem,
                 w_mem_kv=m_w_mem_kv, w_out=m_w_out, w_ffn_gate=m_w_ffn_gate, w_ffn_up=m_w_ffn_up, w_ffn_down=m_w_ffn_down, w_in_a=m_w_in_a,
                 w_spatial=m_w_spatial, b_spatial=m_b_spatial, ln_v_g=m_ln_v_g, ln_v_b=m_ln_v_b, ln_shared=m_ln_shared,
                 w_shared_kv=m_w_shared_kv, b_forget=m_b_forget, w_in_b=m_w_in_b)
    mom_v = dict(ln_mix_pre=v_ln_mix_pre, ln_mix_post=v_ln_mix_post, ln_ffn_pre=v_ln_ffn_pre, ln_ffn_post=v_ln_ffn_post, ln_mem=v_ln_mem,
                 w_mem_kv=v_w_mem_kv, w_out=v_w_out, w_ffn_gate=v_w_ffn_gate, w_ffn_up=v_w_ffn_up, w_ffn_down=v_w_ffn_down, w_in_a=v_w_in_a,
                 w_spatial=v_w_spatial, b_spatial=v_b_spatial, ln_v_g=v_ln_v_g, ln_v_b=v_ln_v_b, ln_shared=v_ln_shared,
                 w_shared_kv=v_w_shared_kv, b_forget=v_b_forget, w_in_b=v_w_in_b)
    names = list(weights)
    mx, my, mc = lax.axis_index("x"), lax.axis_index("y"), lax.axis_index("c")
    me = 4 * mx + 2 * my + mc

    h0 = x[0]
    mem0 = mem[0]
    tgt = loss_target[0]
    seq = h0.shape[0]

    vec = lambda a: a.reshape(1, -1)
    pad_to = lambda a, axis, size: jnp.pad(a, [(0, size - a.shape[i] if i == axis else 0) for i in range(a.ndim)])

    def after(tok, a):
        return a + tok[0, 0].astype(a.dtype)

    lnv_loc = pad_to(jnp.concatenate([ln_v_g, ln_v_b], axis=0), 0, 8)
    st_a = gather_start([w_in_a.astype(bf16), pad_to(lnv_loc, 1, LANES)[None]], [0, 0], "gather_a_start")
    mix_locs = lambda l, tok: [after(tok, w_mem_kv[l]).astype(bf16), w_out[l].astype(bf16)]
    def ffn_gather_start(l, tok):
        gate_up = gather_start([pad_to(after(tok, w_ffn_gate[l]).astype(bf16), 1, FF_SHARD_PAD),
                                pad_to(w_ffn_up[l].astype(bf16), 1, FF_SHARD_PAD)], [1, 1], f"gather_gate_up{l}_start")
        down = gather_start([pad_to(after(gate_up["token"], w_ffn_down[l]).astype(bf16), 0, FF_SHARD_PAD)], [0], f"gather_down{l}_start")
        return gate_up, down

    st_b = [gather_start(mix_locs(0, st_a["token"]), [0, 0], "gather_b0_start"), None]
    st_c = ffn_gather_start(0, st_b[0]["token"])
    st_d = gather_start([after(st_c[1]["token"], w_in_b[0]).astype(bf16), pad_to(w_shared_kv.astype(bf16), 1, KV_PAD)], [0, 0],
                        "gather_d_start")
    st_b[1] = gather_start(mix_locs(1, st_d["token"]), [0, 0], "gather_b1_start")
    st_e = ffn_gather_start(1, st_b[1]["token"])
    ws = w_spatial[0].astype(bf16)
    ws_t = ws.transpose(0, 2, 1)
    bs_t = b_spatial[0].T

    (a0,) = rms_fwd(h0, [after(st_e[1]["token"], vec(ln_mix_pre[0]))], "a0_norm")
    w_in_a8, lnv8 = gather_wait(st_a, [0, 0], a0, "gather_a_wait")
    w_in_a_full = w_in_a8.transpose(1, 0, 2).reshape(D_MODEL, -1)
    lnv_g = lnv8[:, 0, :MAIN_WIDTH // N_DEV].reshape(1, MAIN_WIDTH)
    lnv_b = lnv8[:, 1, :MAIN_WIDTH // N_DEV].reshape(1, MAIN_WIDTH)
    proj0 = mm(a0, w_in_a_full, "proj0", tn=896)
    main0 = gmlp_fwd(proj0, ws, bs_t, lnv_g, lnv_b, "gmlp_fwd")
    w_mkv, w_o = [None, None], [None, None]
    w_mkv[0], w_o[0] = gather_wait(st_b[0], [0, 0], main0, "gather_b0_wait")
    (memn0,) = rms_fwd(mem0, [vec(ln_mem[0])], "mem0_norm")
    kvm0 = mm(memn0, w_mkv[0], "kvm0")
    om0 = mem_attn_fwd(proj0, 2 * MAIN_WIDTH // MEM_WIDTH, kvm0, "mem_attn0")
    mixed0 = jnp.concatenate([main0, om0], axis=-1)
    y1_0, hmid0, f0 = mm_resnorm(mixed0, w_o[0], h0, vec(ln_mix_post[0]), [vec(ln_ffn_pre[0])], "mix_out0")
    w_g0, w_u0 = gather_wait(st_c[0], [1, 1], f0, "gather_gate_up0_wait")
    gu0, act0 = ffn_up(f0, w_g0, w_u0, "ffn_up0")
    (w_d0,) = gather_wait(st_c[1], [0], act0, "gather_down0_wait")
    y2_0, h1, a1, sin1 = mm_resnorm(act0, w_d0, hmid0, vec(ln_ffn_post[0]), [vec(ln_mix_pre[1]), vec(ln_shared)], "ffn_down0")

    w_inb, w_kv = gather_wait(st_d, [0, 0], sin1, "gather_d_wait")
    kvb = mm(sin1, w_kv, "kv_shared", out_dtype=bf16, tn=MAIN_WIDTH, ncols=2 * MAIN_WIDTH)
    zf = mm(sin1, w_kv, "forget_logits", tn=256, col0=2 * MAIN_WIDTH, ncols=256)
    qb = mm(a1, w_inb, "proj1", out_dtype=bf16)
    z_t = jnp.pad(zf[:, :FOX_HEADS].T, ((0, 16 - FOX_HEADS), (0, 0)))
    bf_col = jnp.pad(b_forget, (0, 16 - FOX_HEADS)).reshape(16, 1)
    c_t = fgate_fwd(z_t, bf_col, "fgate_fwd")
    c_row = c_t[:FOX_HEADS].reshape(FOX_PAIRS, 2, seq)
    main1, lse = fox_fwd(qb, kvb, c_row, "fox_fwd")
    w_mkv[1], w_o[1] = gather_wait(st_b[1], [0, 0], main1, "gather_b1_wait")
    (memn1,) = rms_fwd(mem0, [vec(ln_mem[1])], "mem1_norm")
    kvm1 = mm(memn1, w_mkv[1], "kvm1")
    om1 = mem_attn_fwd(qb, MAIN_WIDTH // MEM_WIDTH, kvm1, "mem_attn1")
    mixed1 = jnp.concatenate([main1.astype(bf16), om1], axis=-1)
    y1_1, hmid1, f1 = mm_resnorm(mixed1, w_o[1], h1, vec(ln_mix_post[1]), [vec(ln_ffn_pre[1])], "mix_out1")
    w_g1, w_u1 = gather_wait(st_e[0], [1, 1], f1, "gather_gate_up1_wait")
    gu1, act1 = ffn_up(f1, w_g1, w_u1, "ffn_up1")
    (w_d1,) = gather_wait(st_e[1], [0], act1, "gather_down1_wait")
    y2_1, h2 = mm_resnorm(act1, w_d1, hmid1, vec(ln_ffn_post[1]), [], "ffn_down1")
    dh, loss_tile = loss_grad(h2, tgt, "loss")
    ffn_w = [(w_g0, w_u0, w_d0), (w_g1, w_u1, w_d1)]

    small = {}

    def ffn_backward(layer, dh_out, y2, hmid, f, gu, act, y1):
        w_g, w_u, w_d = ffn_w[layer]
        d_y2, dg_fpost = rms_bwd(y2, vec(ln_ffn_post[layer]), dh_out, None, bf16, f"ffn_post_bwd{layer}")
        dw_down = mm_tn(act, d_y2, f"dw_down{layer}", tk=256)
        rs_down = scatter_start([dw_down], [0], f"scatter_down{layer}_start")
        d_g, d_u = ffn_act_grad(d_y2, w_d, gu, f"ffn_act_grad{layer}")
        dw_g = mm_tn(f, d_g, f"dw_gate{layer}", dep=rs_down["token"])
        dw_u = mm_tn(f, d_u, f"dw_up{layer}")
        rs_gate_up = scatter_start([dw_g, dw_u], [1, 1], f"scatter_gate_up{layer}_start")
        dh_mid, d_y1, dg_fpre, dg_mpost = ffn_in_grad(d_g, d_u, w_g, w_u, hmid, dh_out, after(rs_gate_up["token"], vec(ln_ffn_pre[layer])),
                                                      y1, vec(ln_mix_post[layer]), f"ffn_in_grad{layer}")
        return dh_mid, d_y1, dg_fpost, dg_fpre, dg_mpost, (rs_down, rs_gate_up)

    def mix_out_backward(layer, d_y1, mixed):
        dw_out = mm_tn(mixed, d_y1, f"dw_out{layer}")
        d_mixed = mm(d_y1, w_o[layer], f"d_mixed{layer}", trans_b=True)
        return d_mixed, dw_out

    def mem_backward(layer, q_src, q_block, kvm, memn, d_mixed):
        d_qm, d_kvm = mem_attn_bwd(q_src, q_block, kvm, d_mixed, f"mem_attn_bwd{layer}")
        d_kvm_b = d_kvm.astype(bf16)
        dw_mkv = mm_tn(memn, d_kvm_b, f"dw_mem_kv{layer}")
        d_memn = mm(d_kvm_b, w_mkv[layer], f"d_memn{layer}", trans_b=True)
        _, dg_mem = rms_bwd(mem0, vec(ln_mem[layer]), d_memn, None, bf16, f"mem_norm_bwd{layer}")
        return d_qm, dw_mkv, dg_mem


    dh_mid1, d_y1_1, dg_fpost1, dg_fpre1, dg_mpost1, rs_ffn1 = ffn_backward(1, dh, y2_1, hmid1, f1, gu1, act1, y1_1)
    d_mixed1, dw_out1 = mix_out_backward(1, d_y1_1, mixed1)
    d_qm1, dw_mkv1, dg_mem1 = mem_backward(1, qb, MAIN_WIDTH // MEM_WIDTH, kvm1, memn1, d_mixed1)
    rs_mix1 = scatter_start([dw_out1, dw_mkv1], [0, 0], "scatter_mix1_start")
    dq, dk, dv, dc = fox_bwd(qb, kvb, d_mixed1, main1, lse, after(rs_mix1["token"], c_row), "fox_bwd")
    dc_t = jnp.pad(dc.reshape(FOX_HEADS, seq), ((0, 16 - FOX_HEADS), (0, 0)))
    dz_t, db_f = fgate_bwd(dc_t, z_t, bf_col, "fgate_bwd")
    d_kvf = jnp.concatenate([dk, dv, jnp.pad(dz_t[:FOX_HEADS].T.astype(bf16), ((0, 0), (0, KV_PAD - KV_WIDTH)))], axis=-1)
    d_proj1 = jnp.concatenate([dq.astype(bf16), d_qm1], axis=-1)
    dw_in_b = mm_tn(a1, d_proj1, "dw_in_b")
    dw_kv = mm_tn(sin1, d_kvf, "dw_kv", tn=896)
    rs_2 = scatter_start([dw_in_b, dw_kv], [0, 0], "scatter_shared_start")
    dh1, (dg_pre1, dg_shared) = proj_in_grad([(d_proj1, w_inb, vec(ln_mix_pre[1])), (d_kvf, w_kv, vec(ln_shared))], h1, dh_mid1,
                                             "in_grad1", dep=rs_2["token"])

    dh_mid0, d_y1_0, dg_fpost0, dg_fpre0, dg_mpost0, rs_ffn0 = ffn_backward(0, dh1, y2_0, hmid0, f0, gu0, act0, y1_0)
    d_mixed0, dw_out0 = mix_out_backward(0, d_y1_0, mixed0)
    d_qm0, dw_mkv0, dg_mem0 = mem_backward(0, proj0, 2 * MAIN_WIDTH // MEM_WIDTH, kvm0, memn0, d_mixed0)
    rs_mix0 = scatter_start([dw_out0, dw_mkv0], [0, 0], "scatter_mix0_start")
    d_uv, dw_s, db_s, dg_lnv, db_lnv = gmlp_bwd(proj0, d_mixed0, ws, ws_t, bs_t, after(rs_mix0["token"], lnv_g), lnv_b, "gmlp_bwd")

    small["ln_mix_pre"] = jnp.concatenate([jnp.zeros_like(dg_pre1), dg_pre1], axis=0)
    small["ln_mix_post"] = jnp.concatenate([dg_mpost0, dg_mpost1], axis=0)
    small["ln_ffn_pre"] = jnp.concatenate([dg_fpre0, dg_fpre1], axis=0)
    small["ln_ffn_post"] = jnp.concatenate([dg_fpost0, dg_fpost1], axis=0)
    small["ln_mem"] = jnp.concatenate([dg_mem0, dg_mem1], axis=0)
    small["w_spatial"] = dw_s[None]
    small["b_spatial"] = db_s[:, :A_GROUPS].T[None]
    small["ln_shared"] = dg_shared[0]
    small["b_forget"] = db_f[:FOX_HEADS, 0]
    small["ln_v_g"] = dg_lnv
    small["ln_v_b"] = db_lnv
    small_rows = jnp.concatenate([_pack_small(small, _SMALL), loss_tile], axis=0)
    st_small = gather_start([small_rows[None]], [0], "gather_small_grads_start")
    d_proj0 = jnp.concatenate([d_uv, after(st_small["token"], d_qm0)], axis=-1)
    dw_in_a = mm_tn(a0, d_proj0, "dw_in_a", tn=896)
    rs_in_a = scatter_start([dw_in_a.reshape(D_MODEL, N_DEV, -1).transpose(1, 0, 2)], [0], "scatter_in_a_start")
    grad_x, (dg_pre0,) = proj_in_grad([(d_proj0, w_in_a_full, vec(ln_mix_pre[0]))], h0, dh_mid0, "in_grad0", dep=rs_in_a["token"])
    st_last = gather_start([dg_pre0.reshape(1, 8, LANES)], [0], "gather_last_grad_start")

    def owned(started, axes, wait_after, name):
        recv = scatter_wait(started, axes, wait_after, name)
        return [sum_leading(r.reshape((N_DEV, -1, r.shape[-1])), f"{name}_sum{i}", tr=_row_tile(math.prod(r.shape[1:-1])))
                for i, r in enumerate(recv)]

    (g_down1,) = owned(rs_ffn1[0], [0], after(st_last["token"], grad_x[:8, :LANES]), "scatter_down1_wait")
    g_gu1 = owned(rs_ffn1[1], [1, 1], g_down1, "scatter_gate_up1_wait")
    g_mix1 = owned(rs_mix1, [0, 0], g_gu1[0], "scatter_mix1_wait")
    g2 = owned(rs_2, [0, 0], g_mix1[0], "scatter_shared_wait")
    (g_down0,) = owned(rs_ffn0[0], [0], g2[0], "scatter_down0_wait")
    g_gu0 = owned(rs_ffn0[1], [1, 1], g_down0, "scatter_gate_up0_wait")
    g_mix0 = owned(rs_mix0, [0, 0], g_gu0[0], "scatter_mix0_wait")
    (g_in_a,) = owned(rs_in_a, [0], g_mix0[0], "scatter_in_a_wait")
    g_local = dict(
        w_ffn_gate=jnp.stack([g_gu0[0], g_gu1[0]])[:, :, :FF_SHARD], w_ffn_up=jnp.stack([g_gu0[1], g_gu1[1]])[:, :, :FF_SHARD],
        w_ffn_down=jnp.stack([g_down0, g_down1])[:, :FF_SHARD], w_out=jnp.stack([g_mix0[0], g_mix1[0]]),
        w_mem_kv=jnp.stack([g_mix0[1], g_mix1[1]]), w_in_b=g2[0][None], w_shared_kv=g2[1][:, :KV_WIDTH], w_in_a=g_in_a[None])
    (small_all,) = gather_wait(st_small, [0], g_in_a, "gather_small_grads_wait")
    (last_all,) = gather_wait(st_last, [0], small_all, "gather_last_grad_wait")
    small_sum = sum_leading(small_all, "sum_small_grads")
    loss = small_sum[small_rows.shape[0] - 1, 0]
    g_small = _unpack_small(small_sum, _SMALL)
    g_small["ln_mix_pre"] = jnp.concatenate([sum_leading(last_all, "sum_last_grad").reshape(1, D_MODEL), g_small["ln_mix_pre"][1:]], axis=0)
    shard = MAIN_WIDTH // N_DEV
    for n in ("ln_v_g", "ln_v_b"):
        g_small[n] = lax.dynamic_slice_in_dim(g_small[n], me * shard, shard, axis=1)
    grad_w = {**g_small, **g_local}

    delta, new_m, new_v = {}, {}, {}
    for n in g_local:
        two_d = (-1, weights[n].shape[-1])
        d_, m_, v_ = adamw(weights[n].reshape(two_d), grad_w[n].reshape(two_d), mom_m[n].reshape(two_d), mom_v[n].reshape(two_d),
                           f"adamw_{n}", tr=_row_tile(math.prod(weights[n].shape[:-1])))
        delta[n], new_m[n], new_v[n] = (t.reshape(weights[n].shape) for t in (d_, m_, v_))
    small_local_shapes = [(n, tuple(weights[n].shape)) for n, _ in _SMALL]
    packed = [_pack_small(src, small_local_shapes) for src in (weights, grad_w, mom_m, mom_v)]
    outs = adamw(*packed, "adamw_small", tr=packed[0].shape[0])
    for dst, buf in zip((delta, new_m, new_v), outs):
        dst.update(_unpack_small(buf, small_local_shapes))

    return (loss, grad_x[None], *[grad_w[n] for n in names], *[delta[n] for n in names],
            *[new_m[n] for n in names], *[new_v[n] for n in names])
```

```python
import functools
import math

import jax
import jax.numpy as jnp
from jax import lax
from jax.experimental import pallas as pl
from jax.experimental.pallas import tpu as pltpu

f32 = jnp.float32
bf16 = jnp.bfloat16
SDS = jax.ShapeDtypeStruct

D_MODEL = 1024
MAIN_WIDTH = 768
MEM_WIDTH = 256
HEAD_DIM = 64
MEM_HEADS = 4
FOX_HEADS = 12
FOX_PAIRS = FOX_HEADS // 2
CHUNK = 128
A_GROUPS = 6
FF_SHARD = 352
FF_SHARD_PAD = 384
FF_PAD = 8 * FF_SHARD_PAD
KV_WIDTH = 2 * MAIN_WIDTH + FOX_HEADS
KV_PAD = 1792
RMS_EPS = 1e-6
LN_EPS = 1e-5
ATT_SCALE = HEAD_DIM ** -0.5
ADAM_LR, ADAM_B1, ADAM_B2, ADAM_EPS, ADAM_WD, ADAM_STEP = 0.001, 0.9, 0.999, 1e-08, 0.01, 10
N_DEV = 8
AXES = ("x", "y", "c")
MESH = pl.DeviceIdType.MESH
V7X_VMEM_LIMIT = 56 * 1024 * 1024
LANES = 128
FLAT_W = 512
ROW_PAD = 16


def _cparams(*sem):
    return pltpu.CompilerParams(dimension_semantics=sem or None, vmem_limit_bytes=V7X_VMEM_LIMIT)


def _dot(a, b):
    return jnp.dot(a, b, preferred_element_type=f32)


def _dot_nt(a, b):
    return lax.dot_general(a, b, (((1,), (1,)), ((), ())), preferred_element_type=f32)


def _dot_tn(a, b):
    return lax.dot_general(a, b, (((0,), (0,)), ((), ())), preferred_element_type=f32)


def _gelu(x):
    k = math.sqrt(2.0 / math.pi)
    t = jnp.tanh(k * (x + 0.044715 * x * x * x))
    return 0.5 * x * (1.0 + t), t


def _gelu_grad(x, t):
    k = math.sqrt(2.0 / math.pi)
    return 0.5 * (1.0 + t) + 0.5 * x * (1.0 - t * t) * k * (1.0 + 3.0 * 0.044715 * x * x)


def _sigmoid(x):
    return 1.0 / (1.0 + jnp.exp(-x))


def rms_fwd(x, gains, name, tm=512):
    m, d = x.shape
    tm = min(tm, m)
    n = len(gains)

    def body(x_ref, *refs):
        xv = x_ref[...]
        y = xv * lax.rsqrt(jnp.sum(xv * xv, axis=-1, keepdims=True) * (1.0 / d) + RMS_EPS)
        for g_ref, o_ref in zip(refs[:n], refs[n:]):
            o_ref[...] = (y * g_ref[...]).astype(bf16)

    row = pl.BlockSpec((tm, d), lambda i: (i, 0))
    vec = pl.BlockSpec((1, d), lambda i: (0, 0))
    return pl.pallas_call(body, grid=(m // tm,), in_specs=[row] + [vec] * n, out_specs=[row] * n,
                          out_shape=[SDS((m, d), bf16)] * n, name=name, compiler_params=_cparams("parallel"))(x, *gains)


def rms_bwd(x, g, dy, add, out_dtype, name, tm=512):
    m, d = x.shape
    tm = min(tm, m)
    has_add = add is not None

    def body(x_ref, g_ref, dy_ref, *refs):
        dx_ref, dg_ref = refs[-2], refs[-1]
        xv = x_ref[...]
        dyv = dy_ref[...].astype(f32)
        r = lax.rsqrt(jnp.sum(xv * xv, axis=-1, keepdims=True) * (1.0 / d) + RMS_EPS)
        xn = xv * r
        dyg = dyv * g_ref[...]
        dx = r * (dyg - xn * (jnp.sum(dyg * xn, axis=-1, keepdims=True) * (1.0 / d)))
        if has_add:
            dx = dx + refs[0][...]
        dx_ref[...] = dx.astype(out_dtype)

        @pl.when(pl.program_id(0) == 0)
        def _():
            dg_ref[...] = jnp.zeros_like(dg_ref)

        dg_ref[...] += jnp.sum(dyv * xn, axis=0, keepdims=True)

    row = pl.BlockSpec((tm, d), lambda i: (i, 0))
    vec = pl.BlockSpec((1, d), lambda i: (0, 0))
    ins = [x, g, dy] + ([add] if has_add else [])
    return pl.pallas_call(body, grid=(m // tm,), in_specs=[row, vec, row] + ([row] if has_add else []),
                          out_specs=[row, vec], out_shape=[SDS((m, d), out_dtype), SDS((1, d), f32)], name=name,
                          compiler_params=_cparams("arbitrary"))(*ins)


def loss_grad(h, tgt, name, tm=512):
    m, d = h.shape

    def body(h_ref, t_ref, dy_ref, l_ref):
        e = h_ref[...] - t_ref[...]
        dy_ref[...] = e * (1.0 / d)

        @pl.when(pl.program_id(0) == 0)
        def _():
            l_ref[...] = jnp.zeros_like(l_ref)

        part = jnp.sum(jnp.sum(e * e, axis=-1, keepdims=True), axis=0, keepdims=True) * (0.5 / d)
        l_ref[...] += jnp.broadcast_to(part, l_ref.shape)

    row = pl.BlockSpec((tm, d), lambda i: (i, 0))
    return pl.pallas_call(body, grid=(m // tm,), in_specs=[row, row],
                          out_specs=[row, pl.BlockSpec((8, LANES), lambda i: (0, 0))],
                          out_shape=[SDS((m, d), f32), SDS((8, LANES), f32)], name=name,
                          compiler_params=_cparams("arbitrary"))(h, tgt)


def mm(a, b, name, trans_b=False, out_dtype=f32, tm=1024, tn=1024, layer=None, col0=0, ncols=None, dep=None):
    m, k = a.shape
    n_all = b.shape[-2] if trans_b else b.shape[-1]
    n = n_all if ncols is None else ncols
    tm, tn = min(tm, m), min(tn, n)
    assert m % tm == 0 and n % tn == 0 and col0 % tn == 0 and not (trans_b and col0), (name, m, n, tm, tn)
    jb = col0 // tn
    lead = () if layer is None else (None,)
    sel = () if layer is None else (layer,)

    def body(a_ref, b_ref, *rest):
        r = _dot_nt(a_ref[...], b_ref[...]) if trans_b else _dot(a_ref[...], b_ref[...])
        rest[-1][...] = r.astype(out_dtype)

    if trans_b:
        b_spec = pl.BlockSpec(lead + (tn, k), lambda j, i: sel + (j, 0))
    else:
        b_spec = pl.BlockSpec(lead + (k, tn), lambda j, i: sel + (0, jb + j))
    deps = [] if dep is None else [dep]
    dep_specs = [pl.BlockSpec((8, LANES), lambda j, i: (0, 0))] * len(deps)
    return pl.pallas_call(body, grid=(n // tn, m // tm), in_specs=[pl.BlockSpec((tm, k), lambda j, i: (i, 0)), b_spec] + dep_specs,
                          out_specs=pl.BlockSpec((tm, tn), lambda j, i: (i, j)), out_shape=SDS((m, n), out_dtype),
                          name=name, compiler_params=_cparams("parallel", "parallel"))(a, b, *deps)


def mm_tn(a, g, name, tk=1024, tn=1024, out_dtype=bf16, dep=None):
    s, k = a.shape
    n = g.shape[1]
    tk, tn = min(tk, k), min(tn, n)
    assert k % tk == 0 and n % tn == 0, (name, k, n, tk, tn)

    def body(a_ref, g_ref, *rest):
        rest[-1][...] = _dot_tn(a_ref[...], g_ref[...]).astype(out_dtype)

    deps = [] if dep is None else [dep]
    dep_specs = [pl.BlockSpec((8, LANES), lambda i, j: (0, 0))] * len(deps)
    return pl.pallas_call(body, grid=(k // tk, n // tn),
                          in_specs=[pl.BlockSpec((s, tk), lambda i, j: (0, i)), pl.BlockSpec((s, tn), lambda i, j: (0, j))] + dep_specs,
                          out_specs=pl.BlockSpec((tk, tn), lambda i, j: (i, j)), out_shape=SDS((k, n), out_dtype), name=name,
                          compiler_params=_cparams("parallel", "parallel"))(a, g, *deps)


def _resident(shape, index_map):
    return pl.BlockSpec(shape, index_map, pipeline_mode=pl.Buffered(1))


def _rms(xv):
    return xv * lax.rsqrt(jnp.sum(xv * xv, axis=-1, keepdims=True) * (1.0 / xv.shape[-1]) + RMS_EPS)


def _rms_bwd_math(xv, g, dy):
    d = xv.shape[-1]
    r = lax.rsqrt(jnp.sum(xv * xv, axis=-1, keepdims=True) * (1.0 / d) + RMS_EPS)
    xn = xv * r
    dyg = dy * g
    dx = r * (dyg - xn * (jnp.sum(dyg * xn, axis=-1, keepdims=True) * (1.0 / d)))
    return dx, jnp.sum(dy * xn, axis=0, keepdims=True)


SUB_ROWS = 512


def mm_resnorm(a, b, h, g_post, gains, name, tm=512):
    m, k = a.shape
    d = b.shape[1]
    n = len(gains)

    def body(a_ref, b_ref, h_ref, gp_ref, *refs):
        for r in range(tm // SUB_ROWS):
            rows = slice(r * SUB_ROWS, (r + 1) * SUB_ROWS)
            y = _dot(a_ref[rows, :], b_ref[...])
            refs[n][rows, :] = y
            hn = h_ref[rows, :] + _rms(y) * gp_ref[...]
            refs[n + 1][rows, :] = hn
            if n:
                z = _rms(hn)
                for g_ref, o_ref in zip(refs[:n], refs[n + 2:]):
                    o_ref[rows, :] = (z * g_ref[...]).astype(bf16)

    row = pl.BlockSpec((tm, d), lambda i: (i, 0))
    vec = pl.BlockSpec((1, d), lambda i: (0, 0))
    return pl.pallas_call(body, grid=(m // tm,),
                          in_specs=[pl.BlockSpec((tm, k), lambda i: (i, 0)), _resident((k, d), lambda i: (0, 0)), row, vec] + [vec] * n,
                          out_specs=[row] * (n + 2), out_shape=[SDS((m, d), f32)] * 2 + [SDS((m, d), bf16)] * n, name=name,
                          compiler_params=_cparams("parallel"))(a, b, h, g_post, *gains)


def ffn_act_grad(d_y2, w_d, factors, name, tm=1024, tn=1536):
    s, d = d_y2.shape
    ff = w_d.shape[0]
    nb = ff // tn

    def body(a_ref, b_ref, g_ref, u_ref, dg_ref, du_ref):
        av = a_ref[...]
        tc = 256
        for c in range(tn // tc):
            cols = slice(c * tc, (c + 1) * tc)
            da = _dot_nt(av, b_ref[cols, :])
            dg_ref[:, cols] = (da * g_ref[:, cols].astype(f32)).astype(bf16)
            du_ref[:, cols] = (da * u_ref[:, cols].astype(f32)).astype(bf16)

    tile = pl.BlockSpec((tm, tn), lambda j, i: (i, j))
    return pl.pallas_call(body, grid=(nb, s // tm),
                          in_specs=[pl.BlockSpec((tm, d), lambda j, i: (i, 0)), pl.BlockSpec((tn, d), lambda j, i: (j, 0)), tile,
                                    pl.BlockSpec((tm, tn), lambda j, i: (i, nb + j))],
                          out_specs=[tile, tile], out_shape=[SDS((s, ff), bf16)] * 2, name=name,
                          compiler_params=_cparams("parallel", "parallel"))(d_y2, w_d, factors, factors)


def ffn_in_grad(d_g, d_u, w_g, w_u, hmid, dh_out, g_pre, y1, g_post, name, tm=512):
    s, ff = d_g.shape
    d = w_g.shape[0]

    def body(dg_ref, du_ref, wg_ref, wu_ref, hm_ref, dho_ref, gpre_ref, y1_ref, gpost_ref, dhm_ref, dy1_ref, dgpre_ref, dgpost_ref):
        @pl.when(pl.program_id(0) == 0)
        def _():
            dgpre_ref[...] = jnp.zeros_like(dgpre_ref)
            dgpost_ref[...] = jnp.zeros_like(dgpost_ref)

        for r in range(tm // SUB_ROWS):
            rows = slice(r * SUB_ROWS, (r + 1) * SUB_ROWS)
            d_f = _dot_nt(dg_ref[rows, :], wg_ref[...]) + _dot_nt(du_ref[rows, :], wu_ref[...])
            dx, dg1 = _rms_bwd_math(hm_ref[rows, :], gpre_ref[...], d_f)
            dh_mid = dho_ref[rows, :] + dx
            dhm_ref[rows, :] = dh_mid
            dgpre_ref[...] += dg1
            dy1, dg2 = _rms_bwd_math(y1_ref[rows, :], gpost_ref[...], dh_mid)
            dy1_ref[rows, :] = dy1.astype(bf16)
            dgpost_ref[...] += dg2

    row = pl.BlockSpec((tm, d), lambda i: (i, 0))
    vec = pl.BlockSpec((1, d), lambda i: (0, 0))
    wide = pl.BlockSpec((tm, ff), lambda i: (i, 0))
    w_spec = _resident((d, ff), lambda i: (0, 0))
    return pl.pallas_call(body, grid=(s // tm,), in_specs=[wide, wide, w_spec, w_spec, row, row, vec, row, vec],
                          out_specs=[row, row, vec, vec], out_shape=[SDS((s, d), f32), SDS((s, d), bf16), SDS((1, d), f32), SDS((1, d), f32)],
                          name=name, compiler_params=_cparams("arbitrary"))(d_g, d_u, w_g, w_u, hmid, dh_out, g_pre, y1, g_post)


def proj_in_grad(pairs, x, add, name, tm=512, dep=None):
    s, d = x.shape
    n = len(pairs)
    deps = [] if dep is None else [dep]

    def body(*refs):
        x_ref, add_ref = refs[3 * n], refs[3 * n + 1]
        outs = refs[3 * n + 2 + len(deps):]

        @pl.when(pl.program_id(0) == 0)
        def _():
            for o in outs[1:]:
                o[...] = jnp.zeros_like(o)

        for r in range(tm // SUB_ROWS):
            rows = slice(r * SUB_ROWS, (r + 1) * SUB_ROWS)
            xv = x_ref[rows, :]
            dx = add_ref[rows, :]
            for i in range(n):
                a_ref, b_ref, g_ref = refs[3 * i:3 * i + 3]
                dxi, dgi = _rms_bwd_math(xv, g_ref[...], _dot_nt(a_ref[rows, :], b_ref[...]))
                dx = dx + dxi
                outs[1 + i][...] += dgi
            outs[0][rows, :] = dx

    row = pl.BlockSpec((tm, d), lambda i: (i, 0))
    vec = pl.BlockSpec((1, d), lambda i: (0, 0))
    in_specs, args = [], []
    for a, b, g in pairs:
        k = a.shape[1]
        in_specs += [pl.BlockSpec((tm, k), lambda i: (i, 0)), _resident((d, k), lambda i: (0, 0)), vec]
        args += [a, b, g]
    in_specs += [row, row] + [pl.BlockSpec((8, LANES), lambda i: (0, 0))] * len(deps)
    out = pl.pallas_call(body, grid=(s // tm,), in_specs=in_specs, out_specs=[row] + [vec] * n,
                         out_shape=[SDS((s, d), f32)] + [SDS((1, d), f32)] * n, name=name,
                         compiler_params=_cparams("arbitrary"))(*args, x, add, *deps)
    return out[0], out[1:]


def ffn_up(f, wg, wu, name, tm=512, tc=256):
    s, d = f.shape
    ff = wg.shape[-1]

    def body(f_ref, wg_ref, wu_ref, fac_ref, act_ref):
        fv = f_ref[...]
        for j in range(ff // tc):
            lo = j * tc
            gg = _dot(fv, wg_ref[:, lo:lo + tc])
            uu = _dot(fv, wu_ref[:, lo:lo + tc])
            sg = _sigmoid(gg)
            silu = gg * sg
            fac_ref[:, lo:lo + tc] = (uu * (sg + silu * (1.0 - sg))).astype(bf16)
            fac_ref[:, ff + lo:ff + lo + tc] = silu.astype(bf16)
            act_ref[:, lo:lo + tc] = (silu * uu).astype(bf16)

    w_spec = _resident((d, ff), lambda i: (0, 0))
    return pl.pallas_call(body, grid=(s // tm,), in_specs=[pl.BlockSpec((tm, d), lambda i: (i, 0)), w_spec, w_spec],
                          out_specs=[pl.BlockSpec((tm, 2 * ff), lambda i: (i, 0)), pl.BlockSpec((tm, ff), lambda i: (i, 0))],
                          out_shape=[SDS((s, 2 * ff), bf16), SDS((s, ff), bf16)], name=name,
                          compiler_params=_cparams("parallel"))(f, wg, wu)


def _gmlp_forward_chunk(u, v, w_refs, bias, ln_g, ln_b):
    gu, tu = _gelu(u)
    gv, tv = _gelu(v)
    mu = jnp.sum(gv, axis=-1, keepdims=True) * (1.0 / MAIN_WIDTH)
    xc = gv - mu
    rstd = lax.rsqrt(jnp.sum(xc * xc, axis=-1, keepdims=True) * (1.0 / MAIN_WIDTH) + LN_EPS)
    xhat = xc * rstd
    vln = xhat * ln_g + ln_b
    row = lax.broadcasted_iota(jnp.int32, (CHUNK, CHUNK), 0)
    col = lax.broadcasted_iota(jnp.int32, (CHUNK, CHUNK), 1)
    s_parts = []
    for g in range(A_GROUPS):
        w = jnp.where(col <= row, w_refs[g], jnp.zeros((), bf16))
        s_parts.append(_dot(w, vln[:, g * CHUNK:(g + 1) * CHUNK].astype(bf16)) + bias[:, g:g + 1])
    return gu, tu, tv, rstd, xhat, vln, s_parts


def gmlp_fwd(proj, ws, bs_t, ln_g, ln_b, name, tm=512):
    s = proj.shape[0]

    def body(u_ref, v_ref, w_ref, b_ref, g_ref, bb_ref, o_ref):
        bias = b_ref[...]
        for c in range(tm // CHUNK):
            rows = slice(c * CHUNK, (c + 1) * CHUNK)
            gu, _, _, _, _, _, s_parts = _gmlp_forward_chunk(u_ref[rows, :], v_ref[rows, :], w_ref, bias, g_ref[...], bb_ref[...])
            for g in range(A_GROUPS):
                cols = slice(g * CHUNK, (g + 1) * CHUNK)
                o_ref[rows, cols] = (gu[:, cols] * s_parts[g]).astype(bf16)

    vec = pl.BlockSpec((1, MAIN_WIDTH), lambda i: (0, 0))
    return pl.pallas_call(
        body, grid=(s // tm,),
        in_specs=[pl.BlockSpec((tm, MAIN_WIDTH), lambda i: (i, 0)), pl.BlockSpec((tm, MAIN_WIDTH), lambda i: (i, 1)),
                  pl.BlockSpec((A_GROUPS, CHUNK, CHUNK), lambda i: (0, 0, 0)), pl.BlockSpec((CHUNK, A_GROUPS), lambda i: (0, 0)), vec, vec],
        out_specs=pl.BlockSpec((tm, MAIN_WIDTH), lambda i: (i, 0)), out_shape=SDS((s, MAIN_WIDTH), bf16), name=name,
        compiler_params=_cparams("parallel"))(proj, proj, ws, bs_t, ln_g, ln_b)


def gmlp_bwd(proj, d_mixed, ws, ws_t, bs_t, ln_g, ln_b, name, tm=512):
    s = proj.shape[0]

    def body(u_ref, v_ref, dm_ref, w_ref, wt_ref, b_ref, g_ref, bb_ref, duv_ref, dw_ref, db_ref, dg_ref, dbb_ref):
        @pl.when(pl.program_id(0) == 0)
        def _():
            dw_ref[...] = jnp.zeros_like(dw_ref)
            db_ref[...] = jnp.zeros_like(db_ref)
            dg_ref[...] = jnp.zeros_like(dg_ref)
            dbb_ref[...] = jnp.zeros_like(dbb_ref)

        bias = b_ref[...]
        ln_gv = g_ref[...]
        row = lax.broadcasted_iota(jnp.int32, (CHUNK, CHUNK), 0)
        col = lax.broadcasted_iota(jnp.int32, (CHUNK, CHUNK), 1)
        lane = lax.broadcasted_iota(jnp.int32, (CHUNK, LANES), 1)
        for c in range(tm // CHUNK):
            rows = slice(c * CHUNK, (c + 1) * CHUNK)
            u = u_ref[rows, :]
            v = v_ref[rows, :]
            gu, tu, tv, rstd, xhat, vln, s_parts = _gmlp_forward_chunk(u, v, w_ref, bias, ln_gv, bb_ref[...])
            dm = dm_ref[rows, :]
            d_vln_parts = []
            d_gu_parts = []
            db_acc = jnp.zeros((CHUNK, LANES), f32)
            for g in range(A_GROUPS):
                cols = slice(g * CHUNK, (g + 1) * CHUNK)
                dmg = dm[:, cols]
                d_gu_parts.append(dmg * s_parts[g])
                d_s = dmg * gu[:, cols]
                db_acc = db_acc + jnp.where(lane == g, jnp.sum(d_s, axis=-1, keepdims=True), 0.0)
                d_sb = d_s.astype(bf16)
                dw_ref[g] += jnp.where(col <= row, _dot_nt(d_sb, vln[:, cols].astype(bf16)), 0.0)
                wt = jnp.where(row <= col, wt_ref[g], jnp.zeros((), bf16))
                d_vln_parts.append(_dot(wt, d_sb))
            db_ref[...] += db_acc
            d_vln = jnp.concatenate(d_vln_parts, axis=-1)
            d_gu = jnp.concatenate(d_gu_parts, axis=-1)
            dg_ref[...] += jnp.sum(d_vln * xhat, axis=0, keepdims=True)
            dbb_ref[...] += jnp.sum(d_vln, axis=0, keepdims=True)
            dxh = d_vln * ln_gv
            m1 = jnp.sum(dxh, axis=-1, keepdims=True) * (1.0 / MAIN_WIDTH)
            m2 = jnp.sum(dxh * xhat, axis=-1, keepdims=True) * (1.0 / MAIN_WIDTH)
            d_gv = rstd * (dxh - m1 - xhat * m2)
            duv_ref[rows, :MAIN_WIDTH] = (d_gu * _gelu_grad(u, tu)).astype(bf16)
            duv_ref[rows, MAIN_WIDTH:] = (d_gv * _gelu_grad(v, tv)).astype(bf16)

    vec = pl.BlockSpec((1, MAIN_WIDTH), lambda i: (0, 0))
    wspec = pl.BlockSpec((A_GROUPS, CHUNK, CHUNK), lambda i: (0, 0, 0))
    return pl.pallas_call(
        body, grid=(s // tm,),
        in_specs=[pl.BlockSpec((tm, MAIN_WIDTH), lambda i: (i, 0)), pl.BlockSpec((tm, MAIN_WIDTH), lambda i: (i, 1)),
                  pl.BlockSpec((tm, MAIN_WIDTH), lambda i: (i, 0)), wspec, wspec, pl.BlockSpec((CHUNK, A_GROUPS), lambda i: (0, 0)), vec, vec],
        out_specs=[pl.BlockSpec((tm, 2 * MAIN_WIDTH), lambda i: (i, 0)), wspec, pl.BlockSpec((CHUNK, LANES), lambda i: (0, 0)), vec, vec],
        out_shape=[SDS((s, 2 * MAIN_WIDTH), bf16), SDS((A_GROUPS, CHUNK, CHUNK), f32), SDS((CHUNK, LANES), f32),
                   SDS((1, MAIN_WIDTH), f32), SDS((1, MAIN_WIDTH), f32)],
        name=name, compiler_params=_cparams("arbitrary"))(proj, proj, d_mixed, ws, ws_t, bs_t, ln_g, ln_b)


def _head_mask(width, h):
    lane = lax.broadcasted_iota(jnp.int32, (1, width), 1)
    return (lane >= h * HEAD_DIM) & (lane < (h + 1) * HEAD_DIM)


def mem_attn_fwd(proj, q_block, kv, name, tm=512):
    s = proj.shape[0]
    n_mem = kv.shape[0]

    def body(q_ref, kv_ref, o_ref):
        q = q_ref[...].astype(f32)
        k = kv_ref[:, :MEM_WIDTH].astype(bf16)
        v = kv_ref[:, MEM_WIDTH:].astype(bf16)
        out = jnp.zeros((tm, MEM_WIDTH), f32)
        for h in range(MEM_HEADS):
            msk = _head_mask(MEM_WIDTH, h)
            qh = jnp.where(msk, q, 0.0).astype(bf16)
            sc = _dot_nt(qh, k) * ATT_SCALE
            e = jnp.exp(sc - jnp.max(sc, axis=-1, keepdims=True))
            p = e / jnp.sum(e, axis=-1, keepdims=True)
            out = jnp.where(msk, _dot(p.astype(bf16), v), out)
        o_ref[...] = out.astype(bf16)

    return pl.pallas_call(body, grid=(s // tm,),
                          in_specs=[pl.BlockSpec((tm, MEM_WIDTH), lambda i: (i, q_block)), pl.BlockSpec((n_mem, 2 * MEM_WIDTH), lambda i: (0, 0))],
                          out_specs=pl.BlockSpec((tm, MEM_WIDTH), lambda i: (i, 0)), out_shape=SDS((s, MEM_WIDTH), bf16), name=name,
                          compiler_params=_cparams("parallel"))(proj, kv)


def mem_attn_bwd(proj, q_block, kv, d_mixed, name, tm=512):
    s = proj.shape[0]
    n_mem = kv.shape[0]

    def body(q_ref, kv_ref, do_ref, dq_ref, dkv_ref):
        @pl.when(pl.program_id(0) == 0)
        def _():
            dkv_ref[...] = jnp.zeros_like(dkv_ref)

        q = q_ref[...].astype(f32)
        do = do_ref[...]
        k = kv_ref[:, :MEM_WIDTH].astype(bf16)
        v = kv_ref[:, MEM_WIDTH:].astype(bf16)
        dq = jnp.zeros((tm, MEM_WIDTH), f32)
        dk = jnp.zeros((n_mem, MEM_WIDTH), f32)
        dv = jnp.zeros((n_mem, MEM_WIDTH), f32)
        for h in range(MEM_HEADS):
            msk = _head_mask(MEM_WIDTH, h)
            qh = jnp.where(msk, q, 0.0).astype(bf16)
            doh = jnp.where(msk, do, 0.0).astype(bf16)
            sc = _dot_nt(qh, k) * ATT_SCALE
            e = jnp.exp(sc - jnp.max(sc, axis=-1, keepdims=True))
            p = e / jnp.sum(e, axis=-1, keepdims=True)
            dp = _dot_nt(doh, v)
            ds = p * (dp - jnp.sum(dp * p, axis=-1, keepdims=True))
            dsb = (ds * ATT_SCALE).astype(bf16)
            dq = jnp.where(msk, _dot(dsb, k), dq)
            dk = dk + _dot_tn(dsb, qh)
            dv = dv + _dot_tn(p.astype(bf16), doh)
        dq_ref[...] = dq.astype(bf16)
        dkv_ref[:, :MEM_WIDTH] += dk
        dkv_ref[:, MEM_WIDTH:] += dv

    return pl.pallas_call(
        body, grid=(s // tm,),
        in_specs=[pl.BlockSpec((tm, MEM_WIDTH), lambda i: (i, q_block)), pl.BlockSpec((n_mem, 2 * MEM_WIDTH), lambda i: (0, 0)),
                  pl.BlockSpec((tm, MEM_WIDTH), lambda i: (i, MAIN_WIDTH // MEM_WIDTH))],
        out_specs=[pl.BlockSpec((tm, MEM_WIDTH), lambda i: (i, 0)), pl.BlockSpec((n_mem, 2 * MEM_WIDTH), lambda i: (0, 0))],
        out_shape=[SDS((s, MEM_WIDTH), bf16), SDS((n_mem, 2 * MEM_WIDTH), f32)], name=name,
        compiler_params=_cparams("arbitrary"))(proj, kv, d_mixed)


def _tri(t, upper):
    r = lax.broadcasted_iota(jnp.int32, (t, t), 0)
    c = lax.broadcasted_iota(jnp.int32, (t, t), 1)
    return ((r <= c) if upper else (r >= c)).astype(f32)


def fgate_fwd(z_t, b, name, t=512):
    hh, s = z_t.shape

    def body(z_ref, b_ref, c_ref):
        u = _tri(t, True)
        carry = jnp.zeros((hh, 1), f32)
        for blk in range(s // t):
            x = z_ref[:, blk * t:(blk + 1) * t] + b_ref[...]
            logf = jnp.minimum(x, 0.0) - jnp.log(1.0 + jnp.exp(-jnp.abs(x)))
            y = jnp.dot(logf, u, precision=lax.Precision.HIGHEST, preferred_element_type=f32) + carry
            c_ref[:, blk * t:(blk + 1) * t] = y
            carry = y[:, t - 1:t]

    return pl.pallas_call(body, out_shape=SDS((hh, s), f32), name=name, compiler_params=_cparams())(z_t, b)


def fgate_bwd(dc_t, z_t, b, name, t=512):
    hh, s = z_t.shape

    def body(dc_ref, z_ref, b_ref, dz_ref, db_ref):
        low = _tri(t, False)
        carry = jnp.zeros((hh, 1), f32)
        total = jnp.zeros((hh, 1), f32)
        for blk in reversed(range(s // t)):
            cols = slice(blk * t, (blk + 1) * t)
            y = jnp.dot(dc_ref[:, cols], low, precision=lax.Precision.HIGHEST, preferred_element_type=f32) + carry
            carry = y[:, 0:1]
            dz = y * _sigmoid(-(z_ref[:, cols] + b_ref[...]))
            dz_ref[:, cols] = dz
            total = total + jnp.sum(dz, axis=-1, keepdims=True)
        db_ref[...] = jnp.broadcast_to(total, db_ref.shape)

    return pl.pallas_call(body, out_shape=[SDS((hh, s), f32), SDS((hh, LANES), f32)], name=name,
                          compiler_params=_cparams())(dc_t, z_t, b)


def _pair_masks():
    lane = lax.broadcasted_iota(jnp.int32, (1, LANES), 1)
    return [lane < HEAD_DIM, lane >= HEAD_DIM]


def _tile_base(cr_ref, hh, lo):
    return cr_ref[hh:hh + 1, pl.ds(lo, LANES)][:, 0:1]


def fox_fwd(q, kv, c_row, name, tq=512):
    s = kv.shape[0]
    nq = s // tq

    def body(q_ref, k_ref, v_ref, cr_ref, o_ref, lse_ref):
        i = pl.program_id(1)
        qv = q_ref[...]
        masks = _pair_masks()
        row = lax.broadcasted_iota(jnp.int32, (tq, tq), 0)
        col = lax.broadcasted_iota(jnp.int32, (tq, tq), 1)
        qh = [jnp.where(masks[hh], qv, jnp.zeros((), bf16)) * ATT_SCALE for hh in range(2)]
        ct = [_tile_base(cr_ref, hh, pl.multiple_of(i * tq, tq)) for hh in range(2)]

        def block(j, carry, diag):
            lo = pl.multiple_of(j * tq, tq)
            ks = k_ref[pl.ds(lo, tq), :]
            vs = v_ref[pl.ds(lo, tq), :]
            out = []
            for hh in range(2):
                m, l, acc = carry[hh]
                sc = _dot_nt(qh[hh], ks) + (ct[hh] - cr_ref[hh:hh + 1, pl.ds(lo, tq)])
                if diag:
                    sc = jnp.where(col <= row, sc, -jnp.inf)
                m_new = jnp.maximum(m, jnp.max(sc, axis=-1, keepdims=True))
                alpha = jnp.exp(m - m_new)
                p = jnp.exp(sc - m_new)
                l = alpha * l + jnp.sum(p, axis=-1, keepdims=True)
                p_hi = p.astype(bf16)
                p_lo = (p - p_hi.astype(f32)).astype(bf16)
                acc = alpha * acc + (_dot(p_hi, vs) + _dot(p_lo, vs))
                out.append((m_new, l, acc))
            return tuple(out)

        init = (jnp.full((tq, 1), -jnp.inf, f32), jnp.zeros((tq, 1), f32), jnp.zeros((tq, LANES), f32))
        carry = lax.fori_loop(0, i, functools.partial(block, diag=False), (init, init))
        res = [(acc / l, m + jnp.log(l)) for m, l, acc in block(i, carry, True)]
        o_ref[...] = jnp.where(masks[0], res[0][0], res[1][0])
        lse_ref[...] = jnp.where(masks[0], res[0][1], res[1][1])

    return pl.pallas_call(
        body, grid=(FOX_PAIRS, nq),
        in_specs=[pl.BlockSpec((tq, LANES), lambda p, i: (i, p)), pl.BlockSpec((s, LANES), lambda p, i: (0, p)),
                  pl.BlockSpec((s, LANES), lambda p, i: (0, FOX_PAIRS + p)), pl.BlockSpec((None, 2, s), lambda p, i: (p, 0, 0))],
        out_specs=[pl.BlockSpec((tq, LANES), lambda p, i: (i, p)), pl.BlockSpec((None, tq, LANES), lambda p, i: (p, i, 0))],
        out_shape=[SDS((s, MAIN_WIDTH), f32), SDS((FOX_PAIRS, s, LANES), f32)], name=name,
        compiler_params=_cparams("parallel", "parallel"))(q, kv, kv, c_row)


def fox_bwd(q, kv, d_mixed, o, lse, c_row, name, tq=512):
    s = kv.shape[0]
    nq = s // tq

    def body(q_ref, k_ref, v_ref, do_ref, o_ref, lse_ref, cr_ref, dq_ref, dk_ref, dv_ref, dc_ref):
        j = pl.program_id(1)

        @pl.when(j == 0)
        def _():
            dq_ref[...] = jnp.zeros_like(dq_ref)

        masks = _pair_masks()
        sub = lax.broadcasted_iota(jnp.int32, (LANES, 1), 0)
        sub_masks = [sub < HEAD_DIM, sub >= HEAD_DIM]
        row = lax.broadcasted_iota(jnp.int32, (tq, tq), 0)
        col = lax.broadcasted_iota(jnp.int32, (tq, tq), 1)
        kj = k_ref[...]
        vj = v_ref[...]
        lo_j = pl.multiple_of(j * tq, tq)

        def block(i, carry, diag):
            dk_t, dv_t, dc0, dc1 = carry
            dcs = [dc0, dc1]
            lo = pl.multiple_of(i * tq, tq)
            qi = q_ref[pl.ds(lo, tq), :]
            qi = qi * ATT_SCALE
            qt_i = qi.T
            doi = do_ref[pl.ds(lo, tq), :]
            dot_i = doi.astype(bf16).T
            prod = doi.astype(bf16).astype(f32) * o_ref[pl.ds(lo, tq), :]
            lse_i = lse_ref[pl.ds(lo, tq), :]
            dq_i = jnp.zeros((tq, LANES), f32)
            for hh in range(2):
                qh = jnp.where(masks[hh], qi, jnp.zeros((), bf16))
                doh = jnp.where(masks[hh], doi, 0.0).astype(bf16)
                delta = jnp.sum(jnp.where(masks[hh], prod, 0.0), axis=-1, keepdims=True)
                sc = _dot_nt(qh, kj) + (_tile_base(cr_ref, hh, lo) - cr_ref[hh:hh + 1, pl.ds(lo_j, tq)])
                p = jnp.exp(sc - lse_i[:, hh * HEAD_DIM:hh * HEAD_DIM + 1])
                if diag:
                    p = jnp.where(col <= row, p, 0.0)
                dv_t = dv_t + _dot(jnp.where(sub_masks[hh], dot_i, jnp.zeros((), bf16)), p.astype(bf16))
                ds = p * (_dot_nt(doh, vj) - delta)
                dcs[hh] = dcs[hh] + jnp.sum(ds, axis=0, keepdims=True)
                dsb = ds.astype(bf16)
                dq_i = jnp.where(masks[hh], _dot(dsb, kj), dq_i)
                dk_t = dk_t + _dot(jnp.where(sub_masks[hh], qt_i, jnp.zeros((), bf16)), dsb)
            dq_ref[pl.ds(lo, tq), :] += dq_i * ATT_SCALE
            return dk_t, dv_t, dcs[0], dcs[1]

        zero = jnp.zeros((LANES, tq), f32)
        zrow = jnp.zeros((1, tq), f32)
        carry = block(j, (zero, zero, zrow, zrow), True)
        dk_t, dv_t, dc0, dc1 = lax.fori_loop(j + 1, nq, functools.partial(block, diag=False), carry)
        dk_ref[...] = dk_t.T.astype(bf16)
        dv_ref[...] = dv_t.T.astype(bf16)
        dc_ref[0:1, :] = -dc0
        dc_ref[1:2, :] = -dc1

    full = lambda p, j: (0, p)
    tile = lambda p, j: (j, p)
    return pl.pallas_call(
        body, grid=(FOX_PAIRS, nq),
        in_specs=[pl.BlockSpec((s, LANES), full), pl.BlockSpec((tq, LANES), tile), pl.BlockSpec((tq, LANES), lambda p, j: (j, FOX_PAIRS + p)),
                  pl.BlockSpec((s, LANES), full), pl.BlockSpec((s, LANES), full), pl.BlockSpec((None, s, LANES), lambda p, j: (p, 0, 0)),
                  pl.BlockSpec((None, 2, s), lambda p, j: (p, 0, 0))],
        out_specs=[pl.BlockSpec((s, LANES), full), pl.BlockSpec((tq, LANES), tile), pl.BlockSpec((tq, LANES), tile),
                   pl.BlockSpec((None, 2, tq), lambda p, j: (p, 0, j))],
        out_shape=[SDS((s, MAIN_WIDTH), f32), SDS((s, MAIN_WIDTH), bf16), SDS((s, MAIN_WIDTH), bf16), SDS((FOX_PAIRS, 2, s), f32)],
        name=name, compiler_params=_cparams("parallel", "arbitrary"))(q, kv, kv, d_mixed, o, lse, c_row)


def adamw(w, g, m, v, name, tr=256):
    r, c = w.shape
    tr = min(tr, r)
    assert r % tr == 0, (name, r, tr)
    c1 = 1.0 / (1.0 - ADAM_B1 ** ADAM_STEP)
    c2 = 1.0 / (1.0 - ADAM_B2 ** ADAM_STEP)

    def body(w_ref, g_ref, m_ref, v_ref, d_ref, mo_ref, vo_ref):
        gv = g_ref[...]
        mn = ADAM_B1 * m_ref[...] + (1.0 - ADAM_B1) * gv
        vn = ADAM_B2 * v_ref[...] + (1.0 - ADAM_B2) * gv * gv
        mo_ref[...] = mn
        vo_ref[...] = vn
        d_ref[...] = -ADAM_LR * ((mn * c1) / (jnp.sqrt(vn * c2) + ADAM_EPS) + ADAM_WD * w_ref[...])

    spec = pl.BlockSpec((tr, c), lambda i: (i, 0))
    return pl.pallas_call(body, grid=(r // tr,), in_specs=[spec] * 4, out_specs=[spec] * 3, out_shape=[SDS((r, c), f32)] * 3,
                          name=name, compiler_params=_cparams("parallel"))(w, g, m, v)


def sum_leading(x, name, out_dtype=f32, tr=None):
    n, r, c = x.shape
    tr = tr or r
    assert r % tr == 0

    def body(x_ref, o_ref):
        acc = x_ref[0].astype(f32)
        for k in range(1, n):
            acc = acc + x_ref[k].astype(f32)
        o_ref[...] = acc.astype(out_dtype)

    return pl.pallas_call(body, grid=(r // tr,), in_specs=[pl.BlockSpec((n, tr, c), lambda i: (0, i, 0))],
                          out_specs=pl.BlockSpec((tr, c), lambda i: (i, 0)), out_shape=SDS((r, c), out_dtype), name=name,
                          compiler_params=_cparams("parallel"))(x)


_ANY = pl.BlockSpec(memory_space=pl.ANY)
_DMA = pltpu.SemaphoreType.DMA


_HBM = pl.BlockSpec(memory_space=pltpu.HBM)
_SEM = pl.BlockSpec(memory_space=pltpu.SEMAPHORE)
_EFFECT = pltpu.SideEffectType.DATAFLOW_SIDE_EFFECTING
_FLIPS = [(0, 0, 1), (1, 0, 0), (0, 1, 0), (1, 1, 0), (1, 0, 1), (0, 1, 1), (1, 1, 1)]


def _me():
    return lax.axis_index("x"), lax.axis_index("y"), lax.axis_index("c")


def _peers():
    mx, my, mc = _me()
    return [(jnp.bitwise_xor(mx, fx), jnp.bitwise_xor(my, fy), jnp.bitwise_xor(mc, fc)) for fx, fy, fc in _FLIPS]


def _index(dev):
    return 4 * dev[0] + 2 * dev[1] + dev[2]


def _win(ref, axis, k, size, count=1):
    idx = [slice(None)] * len(ref.shape)
    idx[axis] = pl.ds(k * size, count * size)
    return ref.at[tuple(idx)]


def _hbm(a):
    return pltpu.with_memory_space_constraint(a, pltpu.HBM)


def _exchange_start(srcs, lands, copies_of, name):
    n = len(srcs)

    def body(*refs):
        src = refs[:n]
        send_sems, recv_sems, self_sems = refs[2 * n:2 * n + 3]
        land = refs[3 * n + 3:4 * n + 3]
        token = refs[4 * n + 3]
        me = _index(_me())
        for a in range(n):
            for s_ref, d_ref, peer in copies_of(a, src[a], land[a], me):
                if peer is None:
                    pltpu.make_async_copy(s_ref, d_ref, self_sems.at[a]).start()
                else:
                    pltpu.make_async_remote_copy(src_ref=s_ref, dst_ref=d_ref, send_sem=send_sems.at[a], recv_sem=recv_sems.at[a],
                                                 device_id=peer, device_id_type=MESH).start()
        token[...] = jnp.zeros_like(token)

    outs = pl.pallas_call(
        body, name=name,
        out_shape=(_DMA((n,)), _DMA((n,)), _DMA((n,)), *[pltpu.HBM(s.shape, s.dtype) for s in srcs],
                   *[pltpu.HBM(l.shape, l.dtype) for l in lands], SDS((8, LANES), f32)),
        in_specs=[_HBM] * (2 * n), out_specs=(_SEM, _SEM, _SEM, *[_HBM] * (2 * n), pl.BlockSpec(memory_space=pltpu.VMEM)),
        input_output_aliases={i: 3 + i for i in range(2 * n)},
        compiler_params=pltpu.CompilerParams(has_side_effects=_EFFECT),
    )(*[_hbm(s) for s in srcs], *[_hbm(lax.empty(l.shape, l.dtype)) for l in lands])
    return dict(sems=outs[:3], srcs=list(outs[3:3 + n]), lands=list(outs[3 + n:3 + 2 * n]), token=outs[3 + 2 * n])


def _exchange_wait(started, waits_of, after, name):
    srcs, lands = started["srcs"], started["lands"]
    n = len(srcs)

    def body(*refs):
        src = refs[:n]
        land = refs[n:2 * n]
        send_sems, recv_sems, self_sems = refs[2 * n:2 * n + 3]
        me = _index(_me())
        for a in range(n):
            seven, (s_ref, d_ref) = waits_of(a, src[a], land[a], me)
            both = pltpu.make_async_remote_copy(src_ref=seven, dst_ref=seven, send_sem=send_sems.at[a], recv_sem=recv_sems.at[a],
                                                device_id=_me(), device_id_type=MESH)
            both.wait_send()
            both.wait_recv()
            pltpu.make_async_copy(s_ref, d_ref, self_sems.at[a]).wait()

    outs = pl.pallas_call(
        body, name=name, out_shape=tuple(pltpu.HBM(t.shape, t.dtype) for t in srcs + lands),
        in_specs=[_HBM] * (2 * n) + [_SEM] * 3 + [_ANY], out_specs=tuple([_HBM] * (2 * n)),
        input_output_aliases={i: i for i in range(2 * n)},
        compiler_params=pltpu.CompilerParams(has_side_effects=_EFFECT),
    )(*srcs, *lands, *started["sems"], after)
    return list(outs[n:])


def gather_start(locs, axes, name):
    lands = [SDS(tuple(N_DEV * d if i == ax else d for i, d in enumerate(l.shape)), l.dtype) for l, ax in zip(locs, axes)]

    def copies_of(a, src, land, me):
        mine = _win(land, axes[a], me, src.shape[axes[a]])
        return [(src, mine, peer) for peer in _peers()] + [(src, mine, None)]

    return _exchange_start(locs, lands, copies_of, name)


def gather_wait(started, axes, after, name):
    def waits_of(a, src, land, me):
        size = src.shape[axes[a]]
        return _win(land, axes[a], 0, size, N_DEV - 1), (src, _win(land, axes[a], me, size))

    return _exchange_wait(started, waits_of, after, name)


def scatter_start(grads, axes, name):
    lands = [SDS((N_DEV,) + tuple(d // N_DEV if i == ax else d for i, d in enumerate(g.shape)), g.dtype) for g, ax in zip(grads, axes)]

    def copies_of(a, src, land, me):
        size = src.shape[axes[a]] // N_DEV
        out = [(_win(src, axes[a], _index(peer), size), land.at[me], peer) for peer in _peers()]
        return out + [(_win(src, axes[a], me, size), land.at[me], None)]

    return _exchange_start(grads, lands, copies_of, name)


def scatter_wait(started, axes, after, name):
    def waits_of(a, src, land, me):
        size = src.shape[axes[a]] // N_DEV
        return land.at[pl.ds(0, N_DEV - 1)], (_win(src, axes[a], me, size), land.at[me])

    return _exchange_wait(started, waits_of, after, name)


def _row_tile(rows, cap=512):
    return max(t for t in range(8, min(rows, cap) + 1, 8) if rows % t == 0)


_SMALL = [
    ("ln_mix_pre", (2, 1024)), ("ln_mix_post", (2, 1024)), ("ln_ffn_pre", (2, 1024)), ("ln_ffn_post", (2, 1024)),
    ("ln_mem", (2, 1024)), ("w_spatial", (1, 6, 128, 128)), ("b_spatial", (1, 6, 128)), ("ln_shared", (1024,)),
    ("b_forget", (12,)), ("ln_v_g", (1, 768)), ("ln_v_b", (1, 768)),
]
_SMALL_TILE = 8 * LANES


def _small_rows(shape):
    return -(-math.prod(shape) // _SMALL_TILE) * 8


def _pack_small(vals, shapes):
    parts = []
    for name, shape in shapes:
        flat = vals[name].reshape(-1).astype(f32)
        rows = _small_rows(shape)
        parts.append(jnp.pad(flat, (0, rows * LANES - flat.shape[0])).reshape(rows, LANES))
    return jnp.concatenate(parts, axis=0)


def _unpack_small(buf, shapes):
    out = {}
    lo = 0
    for name, shape in shapes:
        rows = _small_rows(shape)
        out[name] = buf[lo:lo + rows].reshape(-1)[:math.prod(shape)].reshape(shape)
        lo += rows
    return out


def kernel(x, mem, ln_mix_pre, ln_mix_post, ln_ffn_pre, ln_ffn_post, ln_mem, w_mem_kv, w_out, w_ffn_gate, w_ffn_up, w_ffn_down, w_in_a, w_spatial, b_spatial, ln_v_g, ln_v_b, ln_shared, w_shared_kv, b_forget, w_in_b, loss_target, m_ln_mix_pre, m_ln_mix_post, m_ln_ffn_pre, m_ln_ffn_post, m_ln_mem, m_w_mem_kv, m_w_out, m_w_ffn_gate, m_w_ffn_up, m_w_ffn_down, m_w_in_a, m_w_spatial, m_b_spatial, m_ln_v_g, m_ln_v_b, m_ln_shared, m_w_shared_kv, m_b_forget, m_w_in_b, v_ln_mix_pre, v_ln_mix_post, v_ln_ffn_pre, v_ln_ffn_post, v_ln_mem, v_w_mem_kv, v_w_out, v_w_ffn_gate, v_w_ffn_up, v_w_ffn_down, v_w_in_a, v_w_spatial, v_b_spatial, v_ln_v_g, v_ln_v_b, v_ln_shared, v_w_shared_kv, v_b_forget, v_w_in_b):
    weights = dict(ln_mix_pre=ln_mix_pre, ln_mix_post=ln_mix_post, ln_ffn_pre=ln_ffn_pre, ln_ffn_post=ln_ffn_post, ln_mem=ln_mem,
                   w_mem_kv=w_mem_kv, w_out=w_out, w_ffn_gate=w_ffn_gate, w_ffn_up=w_ffn_up, w_ffn_down=w_ffn_down, w_in_a=w_in_a,
                   w_spatial=w_spatial, b_spatial=b_spatial, ln_v_g=ln_v_g, ln_v_b=ln_v_b, ln_shared=ln_shared,
                   w_shared_kv=w_shared_kv, b_forget=b_forget, w_in_b=w_in_b)
    mom_m = dict(ln_mix_pre=m_ln_mix_pre, ln_mix_post=m_ln_mix_post, ln_ffn_pre=m_ln_ffn_pre, ln_ffn_post=m_ln_ffn_post, ln_mem=m_ln_mem,
                 w_mem_kv=m_w_mem_kv, w_out=m_w_out, w_ffn_gate=m_w_ffn_gate, w_ffn_up=m_w_ffn_up, w_ffn_down=m_w_ffn_down, w_in_a=m_w_in_a,
                 w_spatial=m_w_spatial, b_spatial=m_b_spatial, ln_v_g=m_ln_v_g, ln_v_b=m_ln_v_b, ln_shared=m_ln_shared,
                 w_shared_kv=m_w_shared_kv, b_forget=m_b_forget, w_in_b=m_w_in_b)
    mom_v = dict(ln_mix_pre=v_ln_mix_pre, ln_mix_post=v_ln_mix_post, ln_ffn_pre=v_ln_ffn_pre, ln_ffn_post=v_ln_ffn_post, ln_mem=v_ln_mem,
                 w_mem_kv=v_w_mem_kv, w_out=v_w_out, w_ffn_gate=v_w_ffn_gate, w_ffn_up=v_w_ffn_up, w_ffn_down=v_w_ffn_down, w_in_a=v_w_in_a,
                 w_spatial=v_w_spatial, b_spatial=v_b_spatial, ln_v_g=v_ln_v_g, ln_v_b=v_ln_v_b, ln_shared=v_ln_shared,
                 w_shared_kv=v_w_shared_kv, b_forget=v_b_forget, w_in_b=v_w_in_b)
    names = list(weights)
    mx, my, mc = lax.axis_index("x"), lax.axis_index("y"), lax.axis_index("c")
    me = 4 * mx + 2 * my + mc

    h0 = x[0]
    mem0 = mem[0]
    tgt = loss_target[0]
    seq = h0.shape[0]

    vec = lambda a: a.reshape(1, -1)
    pad_to = lambda a, axis, size: jnp.pad(a, [(0, size - a.shape[i] if i == axis else 0) for i in range(a.ndim)])

    def after(tok, a):
        return a + tok[0, 0].astype(a.dtype)

    lnv_loc = pad_to(jnp.concatenate([ln_v_g, ln_v_b], axis=0), 0, 8)
    st_a = gather_start([w_in_a.astype(bf16), pad_to(lnv_loc, 1, LANES)[None]], [0, 0], "gather_a_start")
    mix_locs = lambda l, tok: [after(tok, w_mem_kv[l]).astype(bf16), w_out[l].astype(bf16)]
    def ffn_gather_start(l, tok):
        gate_up = gather_start([pad_to(after(tok, w_ffn_gate[l]).astype(bf16), 1, FF_SHARD_PAD),
                                pad_to(w_ffn_up[l].astype(bf16), 1, FF_SHARD_PAD)], [1, 1], f"gather_gate_up{l}_start")
        down = gather_start([pad_to(after(gate_up["token"], w_ffn_down[l]).astype(bf16), 0, FF_SHARD_PAD)], [0], f"gather_down{l}_start")
        return gate_up, down

    st_b = [gather_start(mix_locs(0, st_a["token"]), [0, 0], "gather_b0_start"), None]
    st_c = ffn_gather_start(0, st_b[0]["token"])
    st_d = gather_start([after(st_c[1]["token"], w_in_b[0]).astype(bf16), pad_to(w_shared_kv.astype(bf16), 1, KV_PAD)], [0, 0],
                        "gather_d_start")
    st_b[1] = gather_start(mix_locs(1, st_d["token"]), [0, 0], "gather_b1_start")
    st_e = ffn_gather_start(1, st_b[1]["token"])
    ws = w_spatial[0].astype(bf16)
    ws_t = ws.transpose(0, 2, 1)
    bs_t = b_spatial[0].T

    (a0,) = rms_fwd(h0, [after(st_e[1]["token"], vec(ln_mix_pre[0]))], "a0_norm")
    w_in_a8, lnv8 = gather_wait(st_a, [0, 0], a0, "gather_a_wait")
    w_in_a_full = w_in_a8.transpose(1, 0, 2).reshape(D_MODEL, -1)
    lnv_g = lnv8[:, 0, :MAIN_WIDTH // N_DEV].reshape(1, MAIN_WIDTH)
    lnv_b = lnv8[:, 1, :MAIN_WIDTH // N_DEV].reshape(1, MAIN_WIDTH)
    proj0 = mm(a0, w_in_a_full, "proj0", tn=896)
    main0 = gmlp_fwd(proj0, ws, bs_t, lnv_g, lnv_b, "gmlp_fwd")
    w_mkv, w_o = [None, None], [None, None]
    w_mkv[0], w_o[0] = gather_wait(st_b[0], [0, 0], main0, "gather_b0_wait")
    (memn0,) = rms_fwd(mem0, [vec(ln_mem[0])], "mem0_norm")
    kvm0 = mm(memn0, w_mkv[0], "kvm0")
    om0 = mem_attn_fwd(proj0, 2 * MAIN_WIDTH // MEM_WIDTH, kvm0, "mem_attn0")
    mixed0 = jnp.concatenate([main0, om0], axis=-1)
    y1_0, hmid0, f0 = mm_resnorm(mixed0, w_o[0], h0, vec(ln_mix_post[0]), [vec(ln_ffn_pre[0])], "mix_out0")
    w_g0, w_u0 = gather_wait(st_c[0], [1, 1], f0, "gather_gate_up0_wait")
    gu0, act0 = ffn_up(f0, w_g0, w_u0, "ffn_up0")
    (w_d0,) = gather_wait(st_c[1], [0], act0, "gather_down0_wait")
    y2_0, h1, a1, sin1 = mm_resnorm(act0, w_d0, hmid0, vec(ln_ffn_post[0]), [vec(ln_mix_pre[1]), vec(ln_shared)], "ffn_down0")

    w_inb, w_kv = gather_wait(st_d, [0, 0], sin1, "gather_d_wait")
    kvb = mm(sin1, w_kv, "kv_shared", out_dtype=bf16, tn=MAIN_WIDTH, ncols=2 * MAIN_WIDTH)
    zf = mm(sin1, w_kv, "forget_logits", tn=256, col0=2 * MAIN_WIDTH, ncols=256)
    qb = mm(a1, w_inb, "proj1", out_dtype=bf16)
    z_t = jnp.pad(zf[:, :FOX_HEADS].T, ((0, 16 - FOX_HEADS), (0, 0)))
    bf_col = jnp.pad(b_forget, (0, 16 - FOX_HEADS)).reshape(16, 1)
    c_t = fgate_fwd(z_t, bf_col, "fgate_fwd")
    c_row = c_t[:FOX_HEADS].reshape(FOX_PAIRS, 2, seq)
    main1, lse = fox_fwd(qb, kvb, c_row, "fox_fwd")
    w_mkv[1], w_o[1] = gather_wait(st_b[1], [0, 0], main1, "gather_b1_wait")
    (memn1,) = rms_fwd(mem0, [vec(ln_mem[1])], "mem1_norm")
    kvm1 = mm(memn1, w_mkv[1], "kvm1")
    om1 = mem_attn_fwd(qb, MAIN_WIDTH // MEM_WIDTH, kvm1, "mem_attn1")
    mixed1 = jnp.concatenate([main1.astype(bf16), om1], axis=-1)
    y1_1, hmid1, f1 = mm_resnorm(mixed1, w_o[1], h1, vec(ln_mix_post[1]), [vec(ln_ffn_pre[1])], "mix_out1")
    w_g1, w_u1 = gather_wait(st_e[0], [1, 1], f1, "gather_gate_up1_wait")
    gu1, act1 = ffn_up(f1, w_g1, w_u1, "ffn_up1")
    (w_d1,) = gather_wait(st_e[1], [0], act1, "gather_down1_wait")
    y2_1, h2 = mm_resnorm(act1, w_d1, hmid1, vec(ln_ffn_post[1]), [], "ffn_down1")
    dh, loss_tile = loss_grad(h2, tgt, "loss")
    ffn_w = [(w_g0, w_u0, w_d0), (w_g1, w_u1, w_d1)]

    small = {}

    def ffn_backward(layer, dh_out, y2, hmid, f, gu, act, y1):
        w_g, w_u, w_d = ffn_w[layer]
        d_y2, dg_fpost = rms_bwd(y2, vec(ln_ffn_post[layer]), dh_out, None, bf16, f"ffn_post_bwd{layer}")
        dw_down = mm_tn(act, d_y2, f"dw_down{layer}")
        rs_down = scatter_start([dw_down], [0], f"scatter_down{layer}_start")
        d_g, d_u = ffn_act_grad(d_y2, w_d, gu, f"ffn_act_grad{layer}")
        dw_g = mm_tn(f, d_g, f"dw_gate{layer}", dep=rs_down["token"])
        dw_u = mm_tn(f, d_u, f"dw_up{layer}")
        rs_gate_up = scatter_start([dw_g, dw_u], [1, 1], f"scatter_gate_up{layer}_start")
        dh_mid, d_y1, dg_fpre, dg_mpost = ffn_in_grad(d_g, d_u, w_g, w_u, hmid, dh_out, after(rs_gate_up["token"], vec(ln_ffn_pre[layer])),
                                                      y1, vec(ln_mix_post[layer]), f"ffn_in_grad{layer}")
        return dh_mid, d_y1, dg_fpost, dg_fpre, dg_mpost, (rs_down, rs_gate_up)

    def mix_out_backward(layer, d_y1, mixed):
        dw_out = mm_tn(mixed, d_y1, f"dw_out{layer}")
        d_mixed = mm(d_y1, w_o[layer], f"d_mixed{layer}", trans_b=True)
        return d_mixed, dw_out

    def mem_backward(layer, q_src, q_block, kvm, memn, d_mixed):
        d_qm, d_kvm = mem_attn_bwd(q_src, q_block, kvm, d_mixed, f"mem_attn_bwd{layer}")
        d_kvm_b = d_kvm.astype(bf16)
        dw_mkv = mm_tn(memn, d_kvm_b, f"dw_mem_kv{layer}")
        d_memn = mm(d_kvm_b, w_mkv[layer], f"d_memn{layer}", trans_b=True)
        _, dg_mem = rms_bwd(mem0, vec(ln_mem[layer]), d_memn, None, bf16, f"mem_norm_bwd{layer}")
        return d_qm, dw_mkv, dg_mem


    dh_mid1, d_y1_1, dg_fpost1, dg_fpre1, dg_mpost1, rs_ffn1 = ffn_backward(1, dh, y2_1, hmid1, f1, gu1, act1, y1_1)
    d_mixed1, dw_out1 = mix_out_backward(1, d_y1_1, mixed1)
    d_qm1, dw_mkv1, dg_mem1 = mem_backward(1, qb, MAIN_WIDTH // MEM_WIDTH, kvm1, memn1, d_mixed1)
    rs_mix1 = scatter_start([dw_out1, dw_mkv1], [0, 0], "scatter_mix1_start")
    dq, dk, dv, dc = fox_bwd(qb, kvb, d_mixed1, main1, lse, after(rs_mix1["token"], c_row), "fox_bwd")
    dc_t = jnp.pad(dc.reshape(FOX_HEADS, seq), ((0, 16 - FOX_HEADS), (0, 0)))
    dz_t, db_f = fgate_bwd(dc_t, z_t, bf_col, "fgate_bwd")
    d_kvf = jnp.concatenate([dk, dv, jnp.pad(dz_t[:FOX_HEADS].T.astype(bf16), ((0, 0), (0, KV_PAD - KV_WIDTH)))], axis=-1)
    d_proj1 = jnp.concatenate([dq.astype(bf16), d_qm1], axis=-1)
    dw_in_b = mm_tn(a1, d_proj1, "dw_in_b")
    dw_kv = mm_tn(sin1, d_kvf, "dw_kv", tn=896)
    rs_2 = scatter_start([dw_in_b, dw_kv], [0, 0], "scatter_shared_start")
    dh1, (dg_pre1, dg_shared) = proj_in_grad([(d_proj1, w_inb, vec(ln_mix_pre[1])), (d_kvf, w_kv, vec(ln_shared))], h1, dh_mid1,
                                             "in_grad1", dep=rs_2["token"])

    dh_mid0, d_y1_0, dg_fpost0, dg_fpre0, dg_mpost0, rs_ffn0 = ffn_backward(0, dh1, y2_0, hmid0, f0, gu0, act0, y1_0)
    d_mixed0, dw_out0 = mix_out_backward(0, d_y1_0, mixed0)
    d_qm0, dw_mkv0, dg_mem0 = mem_backward(0, proj0, 2 * MAIN_WIDTH // MEM_WIDTH, kvm0, memn0, d_mixed0)
    rs_mix0 = scatter_start([dw_out0, dw_mkv0], [0, 0], "scatter_mix0_start")
    d_uv, dw_s, db_s, dg_lnv, db_lnv = gmlp_bwd(proj0, d_mixed0, ws, ws_t, bs_t, after(rs_mix0["token"], lnv_g), lnv_b, "gmlp_bwd")

    small["ln_mix_pre"] = jnp.concatenate([jnp.zeros_like(dg_pre1), dg_pre1], axis=0)
    small["ln_mix_post"] = jnp.concatenate([dg_mpost0, dg_mpost1], axis=0)
    small["ln_ffn_pre"] = jnp.concatenate([dg_fpre0, dg_fpre1], axis=0)
    small["ln_ffn_post"] = jnp.concatenate([dg_fpost0, dg_fpost1], axis=0)
    small["ln_mem"] = jnp.concatenate([dg_mem0, dg_mem1], axis=0)
    small["w_spatial"] = dw_s[None]
    small["b_spatial"] = db_s[:, :A_GROUPS].T[None]
    small["ln_shared"] = dg_shared[0]
    small["b_forget"] = db_f[:FOX_HEADS, 0]
    small["ln_v_g"] = dg_lnv
    small["ln_v_b"] = db_lnv
    small_rows = jnp.concatenate([_pack_small(small, _SMALL), loss_tile], axis=0)
    st_small = gather_start([small_rows[None]], [0], "gather_small_grads_start")
    d_proj0 = jnp.concatenate([d_uv, after(st_small["token"], d_qm0)], axis=-1)
    dw_in_a = mm_tn(a0, d_proj0, "dw_in_a", tn=896)
    rs_in_a = scatter_start([dw_in_a.reshape(D_MODEL, N_DEV, -1).transpose(1, 0, 2)], [0], "scatter_in_a_start")
    grad_x, (dg_pre0,) = proj_in_grad([(d_proj0, w_in_a_full, vec(ln_mix_pre[0]))], h0, dh_mid0, "in_grad0", dep=rs_in_a["token"])
    st_last = gather_start([dg_pre0.reshape(1, 8, LANES)], [0], "gather_last_grad_start")

    def owned(started, axes, wait_after, name):
        recv = scatter_wait(started, axes, wait_after, name)
        return [sum_leading(r.reshape((N_DEV, -1, r.shape[-1])), f"{name}_sum{i}", tr=_row_tile(math.prod(r.shape[1:-1])))
                for i, r in enumerate(recv)]

    (g_down1,) = owned(rs_ffn1[0], [0], after(st_last["token"], grad_x[:8, :LANES]), "scatter_down1_wait")
    g_gu1 = owned(rs_ffn1[1], [1, 1], g_down1, "scatter_gate_up1_wait")
    g_mix1 = owned(rs_mix1, [0, 0], g_gu1[0], "scatter_mix1_wait")
    g2 = owned(rs_2, [0, 0], g_mix1[0], "scatter_shared_wait")
    (g_down0,) = owned(rs_ffn0[0], [0], g2[0], "scatter_down0_wait")
    g_gu0 = owned(rs_ffn0[1], [1, 1], g_down0, "scatter_gate_up0_wait")
    g_mix0 = owned(rs_mix0, [0, 0], g_gu0[0], "scatter_mix0_wait")
    (g_in_a,) = owned(rs_in_a, [0], g_mix0[0], "scatter_in_a_wait")
    g_local = dict(
        w_ffn_gate=jnp.stack([g_gu0[0], g_gu1[0]])[:, :, :FF_SHARD], w_ffn_up=jnp.stack([g_gu0[1], g_gu1[1]])[:, :, :FF_SHARD],
        w_ffn_down=jnp.stack([g_down0, g_down1])[:, :FF_SHARD], w_out=jnp.stack([g_mix0[0], g_mix1[0]]),
        w_mem_kv=jnp.stack([g_mix0[1], g_mix1[1]]), w_in_b=g2[0][None], w_shared_kv=g2[1][:, :KV_WIDTH], w_in_a=g_in_a[None])
    (small_all,) = gather_wait(st_small, [0], g_in_a, "gather_small_grads_wait")
    (last_all,) = gather_wait(st_last, [0], small_all, "gather_last_grad_wait")
    small_sum = sum_leading(small_all, "sum_small_grads")
    loss = small_sum[small_rows.shape[0] - 1, 0]
    g_small = _unpack_small(small_sum, _SMALL)
    g_small["ln_mix_pre"] = jnp.concatenate([sum_leading(last_all, "sum_last_grad").reshape(1, D_MODEL), g_small["ln_mix_pre"][1:]], axis=0)
    shard = MAIN_WIDTH // N_DEV
    for n in ("ln_v_g", "ln_v_b"):
        g_small[n] = lax.dynamic_slice_in_dim(g_small[n], me * shard, shard, axis=1)
    grad_w = {**g_small, **g_local}

    delta, new_m, new_v = {}, {}, {}
    for n in g_local:
        two_d = (-1, weights[n].shape[-1])
        d_, m_, v_ = adamw(weights[n].reshape(two_d), grad_w[n].reshape(two_d), mom_m[n].reshape(two_d), mom_v[n].reshape(two_d),
                           f"adamw_{n}", tr=_row_tile(math.prod(weights[n].shape[:-1])))
        delta[n], new_m[n], new_v[n] = (t.reshape(weights[n].shape) for t in (d_, m_, v_))
    small_local_shapes = [(n, tuple(weights[n].shape)) for n, _ in _SMALL]
    packed = [_pack_small(src, small_local_shapes) for src in (weights, grad_w, mom_m, mom_v)]
    outs = adamw(*packed, "adamw_small", tr=packed[0].shape[0])
    for dst, buf in zip((delta, new_m, new_v), outs):
        dst.update(_unpack_small(buf, small_local_shapes))

    return (loss, grad_x[None], *[grad_w[n] for n in names], *[delta[n] for n in names],
            *[new_m[n] for n in names], *[new_v[n] for n in names])
```

```python
import functools
import math

import jax
import jax.numpy as jnp
from jax import lax
from jax.experimental import pallas as pl
from jax.experimental.pallas import tpu as pltpu

f32 = jnp.float32
bf16 = jnp.bfloat16
SDS = jax.ShapeDtypeStruct

D_MODEL = 1024
MAIN_WIDTH = 768
MEM_WIDTH = 256
HEAD_DIM = 64
MEM_HEADS = 4
FOX_HEADS = 12
FOX_PAIRS = FOX_HEADS // 2
CHUNK = 128
A_GROUPS = 6
FF_SHARD = 352
FF_SHARD_PAD = 384
FF_PAD = 8 * FF_SHARD_PAD
KV_WIDTH = 2 * MAIN_WIDTH + FOX_HEADS
KV_PAD = 1792
RMS_EPS = 1e-6
LN_EPS = 1e-5
ATT_SCALE = HEAD_DIM ** -0.5
ADAM_LR, ADAM_B1, ADAM_B2, ADAM_EPS, ADAM_WD, ADAM_STEP = 0.001, 0.9, 0.999, 1e-08, 0.01, 10
N_DEV = 8
AXES = ("x", "y", "c")
MESH = pl.DeviceIdType.MESH
V7X_VMEM_LIMIT = 56 * 1024 * 1024
LANES = 128
FLAT_W = 512
ROW_PAD = 16


def _cparams(*sem):
    return pltpu.CompilerParams(dimension_semantics=sem or None, vmem_limit_bytes=V7X_VMEM_LIMIT)


def _dot(a, b):
    return jnp.dot(a, b, preferred_element_type=f32)


def _dot_nt(a, b):
    return lax.dot_general(a, b, (((1,), (1,)), ((), ())), preferred_element_type=f32)


def _dot_tn(a, b):
    return lax.dot_general(a, b, (((0,), (0,)), ((), ())), preferred_element_type=f32)


def _gelu(x):
    k = math.sqrt(2.0 / math.pi)
    t = jnp.tanh(k * (x + 0.044715 * x * x * x))
    return 0.5 * x * (1.0 + t), t


def _gelu_grad(x, t):
    k = math.sqrt(2.0 / math.pi)
    return 0.5 * (1.0 + t) + 0.5 * x * (1.0 - t * t) * k * (1.0 + 3.0 * 0.044715 * x * x)


def _sigmoid(x):
    return 1.0 / (1.0 + jnp.exp(-x))


def rms_fwd(x, gains, name, tm=512):
    m, d = x.shape
    tm = min(tm, m)
    n = len(gains)

    def body(x_ref, *refs):
        xv = x_ref[...]
        y = xv * lax.rsqrt(jnp.sum(xv * xv, axis=-1, keepdims=True) * (1.0 / d) + RMS_EPS)
        for g_ref, o_ref in zip(refs[:n], refs[n:]):
            o_ref[...] = (y * g_ref[...]).astype(bf16)

    row = pl.BlockSpec((tm, d), lambda i: (i, 0))
    vec = pl.BlockSpec((1, d), lambda i: (0, 0))
    return pl.pallas_call(body, grid=(m // tm,), in_specs=[row] + [vec] * n, out_specs=[row] * n,
                          out_shape=[SDS((m, d), bf16)] * n, name=name, compiler_params=_cparams("parallel"))(x, *gains)


def rms_bwd(x, g, dy, add, out_dtype, name, tm=512):
    m, d = x.shape
    tm = min(tm, m)
    has_add = add is not None

    def body(x_ref, g_ref, dy_ref, *refs):
        dx_ref, dg_ref = refs[-2], refs[-1]
        xv = x_ref[...]
        dyv = dy_ref[...].astype(f32)
        r = lax.rsqrt(jnp.sum(xv * xv, axis=-1, keepdims=True) * (1.0 / d) + RMS_EPS)
        xn = xv * r
        dyg = dyv * g_ref[...]
        dx = r * (dyg - xn * (jnp.sum(dyg * xn, axis=-1, keepdims=True) * (1.0 / d)))
        if has_add:
            dx = dx + refs[0][...]
        dx_ref[...] = dx.astype(out_dtype)

        @pl.when(pl.program_id(0) == 0)
        def _():
            dg_ref[...] = jnp.zeros_like(dg_ref)

        dg_ref[...] += jnp.sum(dyv * xn, axis=0, keepdims=True)

    row = pl.BlockSpec((tm, d), lambda i: (i, 0))
    vec = pl.BlockSpec((1, d), lambda i: (0, 0))
    ins = [x, g, dy] + ([add] if has_add else [])
    return pl.pallas_call(body, grid=(m // tm,), in_specs=[row, vec, row] + ([row] if has_add else []),
                          out_specs=[row, vec], out_shape=[SDS((m, d), out_dtype), SDS((1, d), f32)], name=name,
                          compiler_params=_cparams("arbitrary"))(*ins)


def loss_grad(h, tgt, name, tm=512):
    m, d = h.shape

    def body(h_ref, t_ref, dy_ref, l_ref):
        e = h_ref[...] - t_ref[...]
        dy_ref[...] = e * (1.0 / d)

        @pl.when(pl.program_id(0) == 0)
        def _():
            l_ref[...] = jnp.zeros_like(l_ref)

        part = jnp.sum(jnp.sum(e * e, axis=-1, keepdims=True), axis=0, keepdims=True) * (0.5 / d)
        l_ref[...] += jnp.broadcast_to(part, l_ref.shape)

    row = pl.BlockSpec((tm, d), lambda i: (i, 0))
    return pl.pallas_call(body, grid=(m // tm,), in_specs=[row, row],
                          out_specs=[row, pl.BlockSpec((8, LANES), lambda i: (0, 0))],
                          out_shape=[SDS((m, d), f32), SDS((8, LANES), f32)], name=name,
                          compiler_params=_cparams("arbitrary"))(h, tgt)


def mm(a, b, name, trans_b=False, out_dtype=f32, tm=1024, tn=1024, layer=None, col0=0, ncols=None, dep=None):
    m, k = a.shape
    n_all = b.shape[-2] if trans_b else b.shape[-1]
    n = n_all if ncols is None else ncols
    tm, tn = min(tm, m), min(tn, n)
    assert m % tm == 0 and n % tn == 0 and col0 % tn == 0 and not (trans_b and col0), (name, m, n, tm, tn)
    jb = col0 // tn
    lead = () if layer is None else (None,)
    sel = () if layer is None else (layer,)

    def body(a_ref, b_ref, *rest):
        r = _dot_nt(a_ref[...], b_ref[...]) if trans_b else _dot(a_ref[...], b_ref[...])
        rest[-1][...] = r.astype(out_dtype)

    if trans_b:
        b_spec = pl.BlockSpec(lead + (tn, k), lambda j, i: sel + (j, 0))
    else:
        b_spec = pl.BlockSpec(lead + (k, tn), lambda j, i: sel + (0, jb + j))
    deps = [] if dep is None else [dep]
    dep_specs = [pl.BlockSpec((8, LANES), lambda j, i: (0, 0))] * len(deps)
    return pl.pallas_call(body, grid=(n // tn, m // tm), in_specs=[pl.BlockSpec((tm, k), lambda j, i: (i, 0)), b_spec] + dep_specs,
                          out_specs=pl.BlockSpec((tm, tn), lambda j, i: (i, j)), out_shape=SDS((m, n), out_dtype),
                          name=name, compiler_params=_cparams("parallel", "parallel"))(a, b, *deps)


def mm_tn(a, g, name, tk=1024, tn=1024, out_dtype=bf16, dep=None):
    s, k = a.shape
    n = g.shape[1]
    tk, tn = min(tk, k), min(tn, n)
    assert k % tk == 0 and n % tn == 0, (name, k, n, tk, tn)

    def body(a_ref, g_ref, *rest):
        rest[-1][...] = _dot_tn(a_ref[...], g_ref[...]).astype(out_dtype)

    deps = [] if dep is None else [dep]
    dep_specs = [pl.BlockSpec((8, LANES), lambda i, j: (0, 0))] * len(deps)
    return pl.pallas_call(body, grid=(k // tk, n // tn),
                          in_specs=[pl.BlockSpec((s, tk), lambda i, j: (0, i)), pl.BlockSpec((s, tn), lambda i, j: (0, j))] + dep_specs,
                          out_specs=pl.BlockSpec((tk, tn), lambda i, j: (i, j)), out_shape=SDS((k, n), out_dtype), name=name,
                          compiler_params=_cparams("parallel", "parallel"))(a, g, *deps)


def _resident(shape, index_map):
    return pl.BlockSpec(shape, index_map, pipeline_mode=pl.Buffered(1))


def _rms(xv):
    return xv * lax.rsqrt(jnp.sum(xv * xv, axis=-1, keepdims=True) * (1.0 / xv.shape[-1]) + RMS_EPS)


def _rms_bwd_math(xv, g, dy):
    d = xv.shape[-1]
    r = lax.rsqrt(jnp.sum(xv * xv, axis=-1, keepdims=True) * (1.0 / d) + RMS_EPS)
    xn = xv * r
    dyg = dy * g
    dx = r * (dyg - xn * (jnp.sum(dyg * xn, axis=-1, keepdims=True) * (1.0 / d)))
    return dx, jnp.sum(dy * xn, axis=0, keepdims=True)


SUB_ROWS = 512


def mm_resnorm(a, b, h, g_post, gains, name, tm=512):
    m, k = a.shape
    d = b.shape[1]
    n = len(gains)

    def body(a_ref, b_ref, h_ref, gp_ref, *refs):
        for r in range(tm // SUB_ROWS):
            rows = slice(r * SUB_ROWS, (r + 1) * SUB_ROWS)
            y = _dot(a_ref[rows, :], b_ref[...])
            refs[n][rows, :] = y
            hn = h_ref[rows, :] + _rms(y) * gp_ref[...]
            refs[n + 1][rows, :] = hn
            if n:
                z = _rms(hn)
                for g_ref, o_ref in zip(refs[:n], refs[n + 2:]):
                    o_ref[rows, :] = (z * g_ref[...]).astype(bf16)

    row = pl.BlockSpec((tm, d), lambda i: (i, 0))
    vec = pl.BlockSpec((1, d), lambda i: (0, 0))
    return pl.pallas_call(body, grid=(m // tm,),
                          in_specs=[pl.BlockSpec((tm, k), lambda i: (i, 0)), _resident((k, d), lambda i: (0, 0)), row, vec] + [vec] * n,
                          out_specs=[row] * (n + 2), out_shape=[SDS((m, d), f32)] * 2 + [SDS((m, d), bf16)] * n, name=name,
                          compiler_params=_cparams("parallel"))(a, b, h, g_post, *gains)


def mm_resnorm_loss(a, b, h, g_post, tgt, name, tm=512):
    m, k = a.shape
    d = b.shape[1]

    def body(a_ref, b_ref, h_ref, gp_ref, t_ref, dh_ref, dy_ref, dg_ref, l_ref):
        @pl.when(pl.program_id(0) == 0)
        def _():
            dg_ref[...] = jnp.zeros_like(dg_ref)
            l_ref[...] = jnp.zeros_like(l_ref)

        y = _dot(a_ref[...], b_ref[...])
        e = h_ref[...] + _rms(y) * gp_ref[...] - t_ref[...]
        dh = e * (1.0 / d)
        dh_ref[...] = dh
        part = jnp.sum(jnp.sum(e * e, axis=-1, keepdims=True), axis=0, keepdims=True) * (0.5 / d)
        l_ref[...] += jnp.broadcast_to(part, l_ref.shape)
        dy, dg = _rms_bwd_math(y, gp_ref[...], dh)
        dy_ref[...] = dy.astype(bf16)
        dg_ref[...] += dg

    row = pl.BlockSpec((tm, d), lambda i: (i, 0))
    vec = pl.BlockSpec((1, d), lambda i: (0, 0))
    return pl.pallas_call(body, grid=(m // tm,),
                          in_specs=[pl.BlockSpec((tm, k), lambda i: (i, 0)), _resident((k, d), lambda i: (0, 0)), row, vec, row],
                          out_specs=[row, row, vec, pl.BlockSpec((8, LANES), lambda i: (0, 0))],
                          out_shape=[SDS((m, d), f32), SDS((m, d), bf16), SDS((1, d), f32), SDS((8, LANES), f32)], name=name,
                          compiler_params=_cparams("arbitrary"))(a, b, h, g_post, tgt)


def ffn_act_grad(d_y2, w_d, factors, name, tm=1024, tn=1536):
    s, d = d_y2.shape
    ff = w_d.shape[0]
    nb = ff // tn

    def body(a_ref, b_ref, g_ref, u_ref, dg_ref, du_ref):
        av = a_ref[...]
        tc = 256
        for c in range(tn // tc):
            cols = slice(c * tc, (c + 1) * tc)
            da = _dot_nt(av, b_ref[cols, :])
            dg_ref[:, cols] = (da * g_ref[:, cols].astype(f32)).astype(bf16)
            du_ref[:, cols] = (da * u_ref[:, cols].astype(f32)).astype(bf16)

    tile = pl.BlockSpec((tm, tn), lambda j, i: (i, j))
    return pl.pallas_call(body, grid=(nb, s // tm),
                          in_specs=[pl.BlockSpec((tm, d), lambda j, i: (i, 0)), pl.BlockSpec((tn, d), lambda j, i: (j, 0)), tile,
                                    pl.BlockSpec((tm, tn), lambda j, i: (i, nb + j))],
                          out_specs=[tile, tile], out_shape=[SDS((s, ff), bf16)] * 2, name=name,
                          compiler_params=_cparams("parallel", "parallel"))(d_y2, w_d, factors, factors)


def ffn_in_grad(d_g, d_u, w_g, w_u, hmid, dh_out, g_pre, y1, g_post, name, tm=512):
    s, ff = d_g.shape
    d = w_g.shape[0]

    def body(dg_ref, du_ref, wg_ref, wu_ref, hm_ref, dho_ref, gpre_ref, y1_ref, gpost_ref, dhm_ref, dy1_ref, dgpre_ref, dgpost_ref):
        @pl.when(pl.program_id(0) == 0)
        def _():
            dgpre_ref[...] = jnp.zeros_like(dgpre_ref)
            dgpost_ref[...] = jnp.zeros_like(dgpost_ref)

        for r in range(tm // SUB_ROWS):
            rows = slice(r * SUB_ROWS, (r + 1) * SUB_ROWS)
            d_f = _dot_nt(dg_ref[rows, :], wg_ref[...]) + _dot_nt(du_ref[rows, :], wu_ref[...])
            dx, dg1 = _rms_bwd_math(hm_ref[rows, :], gpre_ref[...], d_f)
            dh_mid = dho_ref[rows, :] + dx
            dhm_ref[rows, :] = dh_mid
            dgpre_ref[...] += dg1
            dy1, dg2 = _rms_bwd_math(y1_ref[rows, :], gpost_ref[...], dh_mid)
            dy1_ref[rows, :] = dy1.astype(bf16)
            dgpost_ref[...] += dg2

    row = pl.BlockSpec((tm, d), lambda i: (i, 0))
    vec = pl.BlockSpec((1, d), lambda i: (0, 0))
    wide = pl.BlockSpec((tm, ff), lambda i: (i, 0))
    w_spec = _resident((d, ff), lambda i: (0, 0))
    return pl.pallas_call(body, grid=(s // tm,), in_specs=[wide, wide, w_spec, w_spec, row, row, vec, row, vec],
                          out_specs=[row, row, vec, vec], out_shape=[SDS((s, d), f32), SDS((s, d), bf16), SDS((1, d), f32), SDS((1, d), f32)],
                          name=name, compiler_params=_cparams("arbitrary"))(d_g, d_u, w_g, w_u, hmid, dh_out, g_pre, y1, g_post)


def proj_in_grad(pairs, x, add, name, tm=512, dep=None):
    s, d = x.shape
    n = len(pairs)
    deps = [] if dep is None else [dep]

    def body(*refs):
        x_ref, add_ref = refs[3 * n], refs[3 * n + 1]
        outs = refs[3 * n + 2 + len(deps):]

        @pl.when(pl.program_id(0) == 0)
        def _():
            for o in outs[1:]:
                o[...] = jnp.zeros_like(o)

        for r in range(tm // SUB_ROWS):
            rows = slice(r * SUB_ROWS, (r + 1) * SUB_ROWS)
            xv = x_ref[rows, :]
            dx = add_ref[rows, :]
            for i in range(n):
                a_ref, b_ref, g_ref = refs[3 * i:3 * i + 3]
                dxi, dgi = _rms_bwd_math(xv, g_ref[...], _dot_nt(a_ref[rows, :], b_ref[...]))
                dx = dx + dxi
                outs[1 + i][...] += dgi
            outs[0][rows, :] = dx

    row = pl.BlockSpec((tm, d), lambda i: (i, 0))
    vec = pl.BlockSpec((1, d), lambda i: (0, 0))
    in_specs, args = [], []
    for a, b, g in pairs:
        k = a.shape[1]
        in_specs += [pl.BlockSpec((tm, k), lambda i: (i, 0)), _resident((d, k), lambda i: (0, 0)), vec]
        args += [a, b, g]
    in_specs += [row, row] + [pl.BlockSpec((8, LANES), lambda i: (0, 0))] * len(deps)
    out = pl.pallas_call(body, grid=(s // tm,), in_specs=in_specs, out_specs=[row] + [vec] * n,
                         out_shape=[SDS((s, d), f32)] + [SDS((1, d), f32)] * n, name=name,
                         compiler_params=_cparams("arbitrary"))(*args, x, add, *deps)
    return out[0], out[1:]


def ffn_up(f, wg, wu, name, tm=512, tc=256):
    s, d = f.shape
    ff = wg.shape[-1]

    def body(f_ref, wg_ref, wu_ref, fac_ref, act_ref):
        fv = f_ref[...]
        for j in range(ff // tc):
            lo = j * tc
            gg = _dot(fv, wg_ref[:, lo:lo + tc])
            uu = _dot(fv, wu_ref[:, lo:lo + tc])
            sg = _sigmoid(gg)
            silu = gg * sg
            fac_ref[:, lo:lo + tc] = (uu * (sg + silu * (1.0 - sg))).astype(bf16)
            fac_ref[:, ff + lo:ff + lo + tc] = silu.astype(bf16)
            act_ref[:, lo:lo + tc] = (silu * uu).astype(bf16)

    w_spec = _resident((d, ff), lambda i: (0, 0))
    return pl.pallas_call(body, grid=(s // tm,), in_specs=[pl.BlockSpec((tm, d), lambda i: (i, 0)), w_spec, w_spec],
                          out_specs=[pl.BlockSpec((tm, 2 * ff), lambda i: (i, 0)), pl.BlockSpec((tm, ff), lambda i: (i, 0))],
                          out_shape=[SDS((s, 2 * ff), bf16), SDS((s, ff), bf16)], name=name,
                          compiler_params=_cparams("parallel"))(f, wg, wu)


def _gmlp_forward_chunk(u, v, w_refs, bias, ln_g, ln_b):
    gu, tu = _gelu(u)
    gv, tv = _gelu(v)
    mu = jnp.sum(gv, axis=-1, keepdims=True) * (1.0 / MAIN_WIDTH)
    xc = gv - mu
    rstd = lax.rsqrt(jnp.sum(xc * xc, axis=-1, keepdims=True) * (1.0 / MAIN_WIDTH) + LN_EPS)
    xhat = xc * rstd
    vln = xhat * ln_g + ln_b
    row = lax.broadcasted_iota(jnp.int32, (CHUNK, CHUNK), 0)
    col = lax.broadcasted_iota(jnp.int32, (CHUNK, CHUNK), 1)
    s_parts = []
    for g in range(A_GROUPS):
        w = jnp.where(col <= row, w_refs[g], jnp.zeros((), bf16))
        s_parts.append(_dot(w, vln[:, g * CHUNK:(g + 1) * CHUNK].astype(bf16)) + bias[:, g:g + 1])
    return gu, tu, tv, rstd, xhat, vln, s_parts


def gmlp_fwd(proj, ws, bs_t, ln_g, ln_b, name, tm=512):
    s = proj.shape[0]

    def body(u_ref, v_ref, w_ref, b_ref, g_ref, bb_ref, o_ref):
        bias = b_ref[...]
        for c in range(tm // CHUNK):
            rows = slice(c * CHUNK, (c + 1) * CHUNK)
            gu, _, _, _, _, _, s_parts = _gmlp_forward_chunk(u_ref[rows, :], v_ref[rows, :], w_ref, bias, g_ref[...], bb_ref[...])
            for g in range(A_GROUPS):
                cols = slice(g * CHUNK, (g + 1) * CHUNK)
                o_ref[rows, cols] = (gu[:, cols] * s_parts[g]).astype(bf16)

    vec = pl.BlockSpec((1, MAIN_WIDTH), lambda i: (0, 0))
    return pl.pallas_call(
        body, grid=(s // tm,),
        in_specs=[pl.BlockSpec((tm, MAIN_WIDTH), lambda i: (i, 0)), pl.BlockSpec((tm, MAIN_WIDTH), lambda i: (i, 1)),
                  pl.BlockSpec((A_GROUPS, CHUNK, CHUNK), lambda i: (0, 0, 0)), pl.BlockSpec((CHUNK, A_GROUPS), lambda i: (0, 0)), vec, vec],
        out_specs=pl.BlockSpec((tm, MAIN_WIDTH), lambda i: (i, 0)), out_shape=SDS((s, MAIN_WIDTH), bf16), name=name,
        compiler_params=_cparams("parallel"))(proj, proj, ws, bs_t, ln_g, ln_b)


def gmlp_bwd(proj, d_mixed, ws, ws_t, bs_t, ln_g, ln_b, name, tm=512):
    s = proj.shape[0]

    def body(u_ref, v_ref, dm_ref, w_ref, wt_ref, b_ref, g_ref, bb_ref, duv_ref, dw_ref, db_ref, dg_ref, dbb_ref):
        @pl.when(pl.program_id(0) == 0)
        def _():
            dw_ref[...] = jnp.zeros_like(dw_ref)
            db_ref[...] = jnp.zeros_like(db_ref)
            dg_ref[...] = jnp.zeros_like(dg_ref)
            dbb_ref[...] = jnp.zeros_like(dbb_ref)

        bias = b_ref[...]
        ln_gv = g_ref[...]
        row = lax.broadcasted_iota(jnp.int32, (CHUNK, CHUNK), 0)
        col = lax.broadcasted_iota(jnp.int32, (CHUNK, CHUNK), 1)
        lane = lax.broadcasted_iota(jnp.int32, (CHUNK, LANES), 1)
        for c in range(tm // CHUNK):
            rows = slice(c * CHUNK, (c + 1) * CHUNK)
            u = u_ref[rows, :]
            v = v_ref[rows, :]
            gu, tu, tv, rstd, xhat, vln, s_parts = _gmlp_forward_chunk(u, v, w_ref, bias, ln_gv, bb_ref[...])
            dm = dm_ref[rows, :]
            d_vln_parts = []
            d_gu_parts = []
            db_acc = jnp.zeros((CHUNK, LANES), f32)
            for g in range(A_GROUPS):
                cols = slice(g * CHUNK, (g + 1) * CHUNK)
                dmg = dm[:, cols]
                d_gu_parts.append(dmg * s_parts[g])
                d_s = dmg * gu[:, cols]
                db_acc = db_acc + jnp.where(lane == g, jnp.sum(d_s, axis=-1, keepdims=True), 0.0)
                d_sb = d_s.astype(bf16)
                dw_ref[g] += jnp.where(col <= row, _dot_nt(d_sb, vln[:, cols].astype(bf16)), 0.0)
                wt = jnp.where(row <= col, wt_ref[g], jnp.zeros((), bf16))
                d_vln_parts.append(_dot(wt, d_sb))
            db_ref[...] += db_acc
            d_vln = jnp.concatenate(d_vln_parts, axis=-1)
            d_gu = jnp.concatenate(d_gu_parts, axis=-1)
            dg_ref[...] += jnp.sum(d_vln * xhat, axis=0, keepdims=True)
            dbb_ref[...] += jnp.sum(d_vln, axis=0, keepdims=True)
            dxh = d_vln * ln_gv
            m1 = jnp.sum(dxh, axis=-1, keepdims=True) * (1.0 / MAIN_WIDTH)
            m2 = jnp.sum(dxh * xhat, axis=-1, keepdims=True) * (1.0 / MAIN_WIDTH)
            d_gv = rstd * (dxh - m1 - xhat * m2)
            duv_ref[rows, :MAIN_WIDTH] = (d_gu * _gelu_grad(u, tu)).astype(bf16)
            duv_ref[rows, MAIN_WIDTH:] = (d_gv * _gelu_grad(v, tv)).astype(bf16)

    vec = pl.BlockSpec((1, MAIN_WIDTH), lambda i: (0, 0))
    wspec = pl.BlockSpec((A_GROUPS, CHUNK, CHUNK), lambda i: (0, 0, 0))
    return pl.pallas_call(
        body, grid=(s // tm,),
        in_specs=[pl.BlockSpec((tm, MAIN_WIDTH), lambda i: (i, 0)), pl.BlockSpec((tm, MAIN_WIDTH), lambda i: (i, 1)),
                  pl.BlockSpec((tm, MAIN_WIDTH), lambda i: (i, 0)), wspec, wspec, pl.BlockSpec((CHUNK, A_GROUPS), lambda i: (0, 0)), vec, vec],
        out_specs=[pl.BlockSpec((tm, 2 * MAIN_WIDTH), lambda i: (i, 0)), wspec, pl.BlockSpec((CHUNK, LANES), lambda i: (0, 0)), vec, vec],
        out_shape=[SDS((s, 2 * MAIN_WIDTH), bf16), SDS((A_GROUPS, CHUNK, CHUNK), f32), SDS((CHUNK, LANES), f32),
                   SDS((1, MAIN_WIDTH), f32), SDS((1, MAIN_WIDTH), f32)],
        name=name, compiler_params=_cparams("arbitrary"))(proj, proj, d_mixed, ws, ws_t, bs_t, ln_g, ln_b)


def _head_mask(width, h):
    lane = lax.broadcasted_iota(jnp.int32, (1, width), 1)
    return (lane >= h * HEAD_DIM) & (lane < (h + 1) * HEAD_DIM)


def mem_attn_fwd(proj, q_block, kv, name, tm=512):
    s = proj.shape[0]
    n_mem = kv.shape[0]

    def body(q_ref, kv_ref, o_ref):
        q = q_ref[...].astype(f32)
        k = kv_ref[:, :MEM_WIDTH].astype(bf16)
        v = kv_ref[:, MEM_WIDTH:].astype(bf16)
        out = jnp.zeros((tm, MEM_WIDTH), f32)
        for h in range(MEM_HEADS):
            msk = _head_mask(MEM_WIDTH, h)
            qh = jnp.where(msk, q, 0.0).astype(bf16)
            sc = _dot_nt(qh, k) * ATT_SCALE
            e = jnp.exp(sc - jnp.max(sc, axis=-1, keepdims=True))
            p = e / jnp.sum(e, axis=-1, keepdims=True)
            out = jnp.where(msk, _dot(p.astype(bf16), v), out)
        o_ref[...] = out.astype(bf16)

    return pl.pallas_call(body, grid=(s // tm,),
                          in_specs=[pl.BlockSpec((tm, MEM_WIDTH), lambda i: (i, q_block)), pl.BlockSpec((n_mem, 2 * MEM_WIDTH), lambda i: (0, 0))],
                          out_specs=pl.BlockSpec((tm, MEM_WIDTH), lambda i: (i, 0)), out_shape=SDS((s, MEM_WIDTH), bf16), name=name,
                          compiler_params=_cparams("parallel"))(proj, kv)


def mem_attn_bwd(proj, q_block, kv, d_mixed, name, tm=512):
    s = proj.shape[0]
    n_mem = kv.shape[0]

    def body(q_ref, kv_ref, do_ref, dq_ref, dkv_ref):
        @pl.when(pl.program_id(0) == 0)
        def _():
            dkv_ref[...] = jnp.zeros_like(dkv_ref)

        q = q_ref[...].astype(f32)
        do = do_ref[...]
        k = kv_ref[:, :MEM_WIDTH].astype(bf16)
        v = kv_ref[:, MEM_WIDTH:].astype(bf16)
        dq = jnp.zeros((tm, MEM_WIDTH), f32)
        dk = jnp.zeros((n_mem, MEM_WIDTH), f32)
        dv = jnp.zeros((n_mem, MEM_WIDTH), f32)
        for h in range(MEM_HEADS):
            msk = _head_mask(MEM_WIDTH, h)
            qh = jnp.where(msk, q, 0.0).astype(bf16)
            doh = jnp.where(msk, do, 0.0).astype(bf16)
            sc = _dot_nt(qh, k) * ATT_SCALE
            e = jnp.exp(sc - jnp.max(sc, axis=-1, keepdims=True))
            p = e / jnp.sum(e, axis=-1, keepdims=True)
            dp = _dot_nt(doh, v)
            ds = p * (dp - jnp.sum(dp * p, axis=-1, keepdims=True))
            dsb = (ds * ATT_SCALE).astype(bf16)
            dq = jnp.where(msk, _dot(dsb, k), dq)
            dk = dk + _dot_tn(dsb, qh)
            dv = dv + _dot_tn(p.astype(bf16), doh)
        dq_ref[...] = dq.astype(bf16)
        dkv_ref[:, :MEM_WIDTH] += dk
        dkv_ref[:, MEM_WIDTH:] += dv

    return pl.pallas_call(
        body, grid=(s // tm,),
        in_specs=[pl.BlockSpec((tm, MEM_WIDTH), lambda i: (i, q_block)), pl.BlockSpec((n_mem, 2 * MEM_WIDTH), lambda i: (0, 0)),
                  pl.BlockSpec((tm, MEM_WIDTH), lambda i: (i, MAIN_WIDTH // MEM_WIDTH))],
        out_specs=[pl.BlockSpec((tm, MEM_WIDTH), lambda i: (i, 0)), pl.BlockSpec((n_mem, 2 * MEM_WIDTH), lambda i: (0, 0))],
        out_shape=[SDS((s, MEM_WIDTH), bf16), SDS((n_mem, 2 * MEM_WIDTH), f32)], name=name,
        compiler_params=_cparams("arbitrary"))(proj, kv, d_mixed)


def _tri(t, upper):
    r = lax.broadcasted_iota(jnp.int32, (t, t), 0)
    c = lax.broadcasted_iota(jnp.int32, (t, t), 1)
    return ((r <= c) if upper else (r >= c)).astype(f32)


def fgate_fwd(z_t, b, name, t=512):
    hh, s = z_t.shape

    def body(z_ref, b_ref, c_ref):
        u = _tri(t, True)
        carry = jnp.zeros((hh, 1), f32)
        for blk in range(s // t):
            x = z_ref[:, blk * t:(blk + 1) * t] + b_ref[...]
            logf = jnp.minimum(x, 0.0) - jnp.log(1.0 + jnp.exp(-jnp.abs(x)))
            y = jnp.dot(logf, u, precision=lax.Precision.HIGHEST, preferred_element_type=f32) + carry
            c_ref[:, blk * t:(blk + 1) * t] = y
            carry = y[:, t - 1:t]

    return pl.pallas_call(body, out_shape=SDS((hh, s), f32), name=name, compiler_params=_cparams())(z_t, b)


def fgate_bwd(dc_t, z_t, b, name, t=512):
    hh, s = z_t.shape

    def body(dc_ref, z_ref, b_ref, dz_ref, db_ref):
        low = _tri(t, False)
        carry = jnp.zeros((hh, 1), f32)
        total = jnp.zeros((hh, 1), f32)
        for blk in reversed(range(s // t)):
            cols = slice(blk * t, (blk + 1) * t)
            y = jnp.dot(dc_ref[:, cols], low, precision=lax.Precision.HIGHEST, preferred_element_type=f32) + carry
            carry = y[:, 0:1]
            dz = y * _sigmoid(-(z_ref[:, cols] + b_ref[...]))
            dz_ref[:, cols] = dz
            total = total + jnp.sum(dz, axis=-1, keepdims=True)
        db_ref[...] = jnp.broadcast_to(total, db_ref.shape)

    return pl.pallas_call(body, out_shape=[SDS((hh, s), f32), SDS((hh, LANES), f32)], name=name,
                          compiler_params=_cparams())(dc_t, z_t, b)


def _pair_masks():
    lane = lax.broadcasted_iota(jnp.int32, (1, LANES), 1)
    return [lane < HEAD_DIM, lane >= HEAD_DIM]


def _tile_base(cr_ref, hh, lo):
    return cr_ref[hh:hh + 1, pl.ds(lo, LANES)][:, 0:1]


def fox_fwd(q, kv, c_row, name, tq=512):
    s = kv.shape[0]
    nq = s // tq

    def body(q_ref, k_ref, v_ref, cr_ref, o_ref, lse_ref):
        i = pl.program_id(1)
        qv = q_ref[...]
        masks = _pair_masks()
        row = lax.broadcasted_iota(jnp.int32, (tq, tq), 0)
        col = lax.broadcasted_iota(jnp.int32, (tq, tq), 1)
        qh = [jnp.where(masks[hh], qv, jnp.zeros((), bf16)) * ATT_SCALE for hh in range(2)]
        ct = [_tile_base(cr_ref, hh, pl.multiple_of(i * tq, tq)) for hh in range(2)]

        def block(j, carry, diag):
            lo = pl.multiple_of(j * tq, tq)
            ks = k_ref[pl.ds(lo, tq), :]
            vs = v_ref[pl.ds(lo, tq), :]
            out = []
            for hh in range(2):
                m, l, acc = carry[hh]
                sc = _dot_nt(qh[hh], ks) + (ct[hh] - cr_ref[hh:hh + 1, pl.ds(lo, tq)])
                if diag:
                    sc = jnp.where(col <= row, sc, -jnp.inf)
                m_new = jnp.maximum(m, jnp.max(sc, axis=-1, keepdims=True))
                alpha = jnp.exp(m - m_new)
                p = jnp.exp(sc - m_new)
                l = alpha * l + jnp.sum(p, axis=-1, keepdims=True)
                p_hi = p.astype(bf16)
                p_lo = (p - p_hi.astype(f32)).astype(bf16)
                acc = alpha * acc + (_dot(p_hi, vs) + _dot(p_lo, vs))
                out.append((m_new, l, acc))
            return tuple(out)

        init = (jnp.full((tq, 1), -jnp.inf, f32), jnp.zeros((tq, 1), f32), jnp.zeros((tq, LANES), f32))
        carry = lax.fori_loop(0, i, functools.partial(block, diag=False), (init, init))
        res = [(acc / l, m + jnp.log(l)) for m, l, acc in block(i, carry, True)]
        o_ref[...] = jnp.where(masks[0], res[0][0], res[1][0])
        lse_ref[...] = jnp.where(masks[0], res[0][1], res[1][1])

    return pl.pallas_call(
        body, grid=(FOX_PAIRS, nq),
        in_specs=[pl.BlockSpec((tq, LANES), lambda p, i: (i, p)), pl.BlockSpec((s, LANES), lambda p, i: (0, p)),
                  pl.BlockSpec((s, LANES), lambda p, i: (0, FOX_PAIRS + p)), pl.BlockSpec((None, 2, s), lambda p, i: (p, 0, 0))],
        out_specs=[pl.BlockSpec((tq, LANES), lambda p, i: (i, p)), pl.BlockSpec((None, tq, LANES), lambda p, i: (p, i, 0))],
        out_shape=[SDS((s, MAIN_WIDTH), f32), SDS((FOX_PAIRS, s, LANES), f32)], name=name,
        compiler_params=_cparams("parallel", "parallel"))(q, kv, kv, c_row)


def fox_bwd(q, kv, d_mixed, o, lse, c_row, name, tq=512):
    s = kv.shape[0]
    nq = s // tq

    def body(q_ref, k_ref, v_ref, do_ref, o_ref, lse_ref, cr_ref, dq_ref, dk_ref, dv_ref, dc_ref):
        j = pl.program_id(1)

        @pl.when(j == 0)
        def _():
            dq_ref[...] = jnp.zeros_like(dq_ref)

        masks = _pair_masks()
        sub = lax.broadcasted_iota(jnp.int32, (LANES, 1), 0)
        sub_masks = [sub < HEAD_DIM, sub >= HEAD_DIM]
        row = lax.broadcasted_iota(jnp.int32, (tq, tq), 0)
        col = lax.broadcasted_iota(jnp.int32, (tq, tq), 1)
        kj = k_ref[...]
        vj = v_ref[...]
        lo_j = pl.multiple_of(j * tq, tq)

        def block(i, carry, diag):
            dk_t, dv_t, dc0, dc1 = carry
            dcs = [dc0, dc1]
            lo = pl.multiple_of(i * tq, tq)
            qi = q_ref[pl.ds(lo, tq), :]
            qi = qi * ATT_SCALE
            qt_i = qi.T
            doi = do_ref[pl.ds(lo, tq), :]
            dot_i = doi.astype(bf16).T
            prod = doi.astype(bf16).astype(f32) * o_ref[pl.ds(lo, tq), :]
            lse_i = lse_ref[pl.ds(lo, tq), :]
            dq_i = jnp.zeros((tq, LANES), f32)
            for hh in range(2):
                qh = jnp.where(masks[hh], qi, jnp.zeros((), bf16))
                doh = jnp.where(masks[hh], doi, 0.0).astype(bf16)
                delta = jnp.sum(jnp.where(masks[hh], prod, 0.0), axis=-1, keepdims=True)
                sc = _dot_nt(qh, kj) + (_tile_base(cr_ref, hh, lo) - cr_ref[hh:hh + 1, pl.ds(lo_j, tq)])
                p = jnp.exp(sc - lse_i[:, hh * HEAD_DIM:hh * HEAD_DIM + 1])
                if diag:
                    p = jnp.where(col <= row, p, 0.0)
                dv_t = dv_t + _dot(jnp.where(sub_masks[hh], dot_i, jnp.zeros((), bf16)), p.astype(bf16))
                ds = p * (_dot_nt(doh, vj) - delta)
                dcs[hh] = dcs[hh] + jnp.sum(ds, axis=0, keepdims=True)
                dsb = ds.astype(bf16)
                dq_i = jnp.where(masks[hh], _dot(dsb, kj), dq_i)
                dk_t = dk_t + _dot(jnp.where(sub_masks[hh], qt_i, jnp.zeros((), bf16)), dsb)
            dq_ref[pl.ds(lo, tq), :] += dq_i * ATT_SCALE
            return dk_t, dv_t, dcs[0], dcs[1]

        zero = jnp.zeros((LANES, tq), f32)
        zrow = jnp.zeros((1, tq), f32)
        carry = block(j, (zero, zero, zrow, zrow), True)
        dk_t, dv_t, dc0, dc1 = lax.fori_loop(j + 1, nq, functools.partial(block, diag=False), carry)
        dk_ref[...] = dk_t.T.astype(bf16)
        dv_ref[...] = dv_t.T.astype(bf16)
        dc_ref[0:1, :] = -dc0
        dc_ref[1:2, :] = -dc1

    full = lambda p, j: (0, p)
    tile = lambda p, j: (j, p)
    return pl.pallas_call(
        body, grid=(FOX_PAIRS, nq),
        in_specs=[pl.BlockSpec((s, LANES), full), pl.BlockSpec((tq, LANES), tile), pl.BlockSpec((tq, LANES), lambda p, j: (j, FOX_PAIRS + p)),
                  pl.BlockSpec((s, LANES), full), pl.BlockSpec((s, LANES), full), pl.BlockSpec((None, s, LANES), lambda p, j: (p, 0, 0)),
                  pl.BlockSpec((None, 2, s), lambda p, j: (p, 0, 0))],
        out_specs=[pl.BlockSpec((s, LANES), full), pl.BlockSpec((tq, LANES), tile), pl.BlockSpec((tq, LANES), tile),
                   pl.BlockSpec((None, 2, tq), lambda p, j: (p, 0, j))],
        out_shape=[SDS((s, MAIN_WIDTH), f32), SDS((s, MAIN_WIDTH), bf16), SDS((s, MAIN_WIDTH), bf16), SDS((FOX_PAIRS, 2, s), f32)],
        name=name, compiler_params=_cparams("parallel", "arbitrary"))(q, kv, kv, d_mixed, o, lse, c_row)


def adamw(w, g, m, v, name, tr=256):
    r, c = w.shape
    tr = min(tr, r)
    assert r % tr == 0, (name, r, tr)
    c1 = 1.0 / (1.0 - ADAM_B1 ** ADAM_STEP)
    c2 = 1.0 / (1.0 - ADAM_B2 ** ADAM_STEP)

    def body(w_ref, g_ref, m_ref, v_ref, d_ref, mo_ref, vo_ref):
        gv = g_ref[...]
        mn = ADAM_B1 * m_ref[...] + (1.0 - ADAM_B1) * gv
        vn = ADAM_B2 * v_ref[...] + (1.0 - ADAM_B2) * gv * gv
        mo_ref[...] = mn
        vo_ref[...] = vn
        d_ref[...] = -ADAM_LR * ((mn * c1) / (jnp.sqrt(vn * c2) + ADAM_EPS) + ADAM_WD * w_ref[...])

    spec = pl.BlockSpec((tr, c), lambda i: (i, 0))
    return pl.pallas_call(body, grid=(r // tr,), in_specs=[spec] * 4, out_specs=[spec] * 3, out_shape=[SDS((r, c), f32)] * 3,
                          name=name, compiler_params=_cparams("parallel"))(w, g, m, v)


def sum_leading(x, name, out_dtype=f32, tr=None):
    n, r, c = x.shape
    tr = tr or r
    assert r % tr == 0

    def body(x_ref, o_ref):
        acc = x_ref[0].astype(f32)
        for k in range(1, n):
            acc = acc + x_ref[k].astype(f32)
        o_ref[...] = acc.astype(out_dtype)

    return pl.pallas_call(body, grid=(r // tr,), in_specs=[pl.BlockSpec((n, tr, c), lambda i: (0, i, 0))],
                          out_specs=pl.BlockSpec((tr, c), lambda i: (i, 0)), out_shape=SDS((r, c), out_dtype), name=name,
                          compiler_params=_cparams("parallel"))(x)


_ANY = pl.BlockSpec(memory_space=pl.ANY)
_DMA = pltpu.SemaphoreType.DMA


_HBM = pl.BlockSpec(memory_space=pltpu.HBM)
_SEM = pl.BlockSpec(memory_space=pltpu.SEMAPHORE)
_EFFECT = pltpu.SideEffectType.DATAFLOW_SIDE_EFFECTING
_FLIPS = [(0, 0, 1), (1, 0, 0), (0, 1, 0), (1, 1, 0), (1, 0, 1), (0, 1, 1), (1, 1, 1)]


def _me():
    return lax.axis_index("x"), lax.axis_index("y"), lax.axis_index("c")


def _peers():
    mx, my, mc = _me()
    return [(jnp.bitwise_xor(mx, fx), jnp.bitwise_xor(my, fy), jnp.bitwise_xor(mc, fc)) for fx, fy, fc in _FLIPS]


def _index(dev):
    return 4 * dev[0] + 2 * dev[1] + dev[2]


def _win(ref, axis, k, size, count=1):
    idx = [slice(None)] * len(ref.shape)
    idx[axis] = pl.ds(k * size, count * size)
    return ref.at[tuple(idx)]


def _hbm(a):
    return pltpu.with_memory_space_constraint(a, pltpu.HBM)


def _exchange_start(srcs, lands, copies_of, name):
    n = len(srcs)

    def body(*refs):
        src = refs[:n]
        send_sems, recv_sems, self_sems = refs[2 * n:2 * n + 3]
        land = refs[3 * n + 3:4 * n + 3]
        token = refs[4 * n + 3]
        me = _index(_me())
        for a in range(n):
            for s_ref, d_ref, peer in copies_of(a, src[a], land[a], me):
                if peer is None:
                    pltpu.make_async_copy(s_ref, d_ref, self_sems.at[a]).start()
                else:
                    pltpu.make_async_remote_copy(src_ref=s_ref, dst_ref=d_ref, send_sem=send_sems.at[a], recv_sem=recv_sems.at[a],
                                                 device_id=peer, device_id_type=MESH).start()
        token[...] = jnp.zeros_like(token)

    outs = pl.pallas_call(
        body, name=name,
        out_shape=(_DMA((n,)), _DMA((n,)), _DMA((n,)), *[pltpu.HBM(s.shape, s.dtype) for s in srcs],
                   *[pltpu.HBM(l.shape, l.dtype) for l in lands], SDS((8, LANES), f32)),
        in_specs=[_HBM] * (2 * n), out_specs=(_SEM, _SEM, _SEM, *[_HBM] * (2 * n), pl.BlockSpec(memory_space=pltpu.VMEM)),
        input_output_aliases={i: 3 + i for i in range(2 * n)},
        compiler_params=pltpu.CompilerParams(has_side_effects=_EFFECT),
    )(*[_hbm(s) for s in srcs], *[_hbm(lax.empty(l.shape, l.dtype)) for l in lands])
    return dict(sems=outs[:3], srcs=list(outs[3:3 + n]), lands=list(outs[3 + n:3 + 2 * n]), token=outs[3 + 2 * n])


def _exchange_wait(started, waits_of, after, name):
    srcs, lands = started["srcs"], started["lands"]
    n = len(srcs)

    def body(*refs):
        src = refs[:n]
        land = refs[n:2 * n]
        send_sems, recv_sems, self_sems = refs[2 * n:2 * n + 3]
        me = _index(_me())
        for a in range(n):
            seven, (s_ref, d_ref) = waits_of(a, src[a], land[a], me)
            both = pltpu.make_async_remote_copy(src_ref=seven, dst_ref=seven, send_sem=send_sems.at[a], recv_sem=recv_sems.at[a],
                                                device_id=_me(), device_id_type=MESH)
            both.wait_send()
            both.wait_recv()
            pltpu.make_async_copy(s_ref, d_ref, self_sems.at[a]).wait()

    outs = pl.pallas_call(
        body, name=name, out_shape=tuple(pltpu.HBM(t.shape, t.dtype) for t in srcs + lands),
        in_specs=[_HBM] * (2 * n) + [_SEM] * 3 + [_ANY], out_specs=tuple([_HBM] * (2 * n)),
        input_output_aliases={i: i for i in range(2 * n)},
        compiler_params=pltpu.CompilerParams(has_side_effects=_EFFECT),
    )(*srcs, *lands, *started["sems"], after)
    return list(outs[n:])


def gather_start(locs, axes, name):
    lands = [SDS(tuple(N_DEV * d if i == ax else d for i, d in enumerate(l.shape)), l.dtype) for l, ax in zip(locs, axes)]

    def copies_of(a, src, land, me):
        mine = _win(land, axes[a], me, src.shape[axes[a]])
        return [(src, mine, peer) for peer in _peers()] + [(src, mine, None)]

    return _exchange_start(locs, lands, copies_of, name)


def gather_wait(started, axes, after, name):
    def waits_of(a, src, land, me):
        size = src.shape[axes[a]]
        return _win(land, axes[a], 0, size, N_DEV - 1), (src, _win(land, axes[a], me, size))

    return _exchange_wait(started, waits_of, after, name)


def scatter_start(grads, axes, name):
    lands = [SDS((N_DEV,) + tuple(d // N_DEV if i == ax else d for i, d in enumerate(g.shape)), g.dtype) for g, ax in zip(grads, axes)]

    def copies_of(a, src, land, me):
        size = src.shape[axes[a]] // N_DEV
        out = [(_win(src, axes[a], _index(peer), size), land.at[me], peer) for peer in _peers()]
        return out + [(_win(src, axes[a], me, size), land.at[me], None)]

    return _exchange_start(grads, lands, copies_of, name)


def scatter_wait(started, axes, after, name):
    def waits_of(a, src, land, me):
        size = src.shape[axes[a]] // N_DEV
        return land.at[pl.ds(0, N_DEV - 1)], (_win(src, axes[a], me, size), land.at[me])

    return _exchange_wait(started, waits_of, after, name)


def _row_tile(rows, cap=512):
    return max(t for t in range(8, min(rows, cap) + 1, 8) if rows % t == 0)


_SMALL = [
    ("ln_mix_pre", (2, 1024)), ("ln_mix_post", (2, 1024)), ("ln_ffn_pre", (2, 1024)), ("ln_ffn_post", (2, 1024)),
    ("ln_mem", (2, 1024)), ("w_spatial", (1, 6, 128, 128)), ("b_spatial", (1, 6, 128)), ("ln_shared", (1024,)),
    ("b_forget", (12,)), ("ln_v_g", (1, 768)), ("ln_v_b", (1, 768)),
]
_SMALL_TILE = 8 * LANES


def _small_rows(shape):
    return -(-math.prod(shape) // _SMALL_TILE) * 8


def _pack_small(vals, shapes):
    parts = []
    for name, shape in shapes:
        flat = vals[name].reshape(-1).astype(f32)
        rows = _small_rows(shape)
        parts.append(jnp.pad(flat, (0, rows * LANES - flat.shape[0])).reshape(rows, LANES))
    return jnp.concatenate(parts, axis=0)


def _unpack_small(buf, shapes):
    out = {}
    lo = 0
    for name, shape in shapes:
        rows = _small_rows(shape)
        out[name] = buf[lo:lo + rows].reshape(-1)[:math.prod(shape)].reshape(shape)
        lo += rows
    return out


def kernel(x, mem, ln_mix_pre, ln_mix_post, ln_ffn_pre, ln_ffn_post, ln_mem, w_mem_kv, w_out, w_ffn_gate, w_ffn_up, w_ffn_down, w_in_a, w_spatial, b_spatial, ln_v_g, ln_v_b, ln_shared, w_shared_kv, b_forget, w_in_b, loss_target, m_ln_mix_pre, m_ln_mix_post, m_ln_ffn_pre, m_ln_ffn_post, m_ln_mem, m_w_mem_kv, m_w_out, m_w_ffn_gate, m_w_ffn_up, m_w_ffn_down, m_w_in_a, m_w_spatial, m_b_spatial, m_ln_v_g, m_ln_v_b, m_ln_shared, m_w_shared_kv, m_b_forget, m_w_in_b, v_ln_mix_pre, v_ln_mix_post, v_ln_ffn_pre, v_ln_ffn_post, v_ln_mem, v_w_mem_kv, v_w_out, v_w_ffn_gate, v_w_ffn_up, v_w_ffn_down, v_w_in_a, v_w_spatial, v_b_spatial, v_ln_v_g, v_ln_v_b, v_ln_shared, v_w_shared_kv, v_b_forget, v_w_in_b):
    weights = dict(ln_mix_pre=ln_mix_pre, ln_mix_post=ln_mix_post, ln_ffn_pre=ln_ffn_pre, ln_ffn_post=ln_ffn_post, ln_mem=ln_mem,
                   w_mem_kv=w_mem_kv, w_out=w_out, w_ffn_gate=w_ffn_gate, w_ffn_up=w_ffn_up, w_ffn_down=w_ffn_down, w_in_a=w_in_a,
                   w_spatial=w_spatial, b_spatial=b_spatial, ln_v_g=ln_v_g, ln_v_b=ln_v_b, ln_shared=ln_shared,
                   w_shared_kv=w_shared_kv, b_forget=b_forget, w_in_b=w_in_b)
    mom_m = dict(ln_mix_pre=m_ln_mix_pre, ln_mix_post=m_ln_mix_post, ln_ffn_pre=m_ln_ffn_pre, ln_ffn_post=m_ln_ffn_post, ln_mem=m_ln_mem,
                 w_mem_kv=m_w_mem_kv, w_out=m_w_out, w_ffn_gate=m_w_ffn_gate, w_ffn_up=m_w_ffn_up, w_ffn_down=m_w_ffn_down, w_in_a=m_w_in_a,
                 w_spatial=m_w_spatial, b_spatial=m_b_spatial, ln_v_g=m_ln_v_g, ln_v_b=m_ln_v_b, ln_shared=m_ln_shared,
                 w_shared_kv=m_w_shared_kv, b_forget=m_b_forget, w_in_b=m_w_in_b)
    mom_v = dict(ln_mix_pre=v_ln_mix_pre, ln_mix_post=v_ln_mix_post, ln_ffn_pre=v_ln_ffn_pre, ln_ffn_post=v_ln_ffn_post, ln_mem=v_ln_mem,
                 w_mem_kv=v_w_mem_kv, w_out=v_w_out, w_ffn_gate=v_w_ffn_gate, w_ffn_up=v_w_ffn_up, w_ffn_down=v_w_ffn_down, w_in_a=v_w_in_a,
                 w_spatial=v_w_spatial, b_spatial=v_b_spatial, ln_v_g=v_ln_v_g, ln_v_b=v_ln_v_b, ln_shared=v_ln_shared,
                 w_shared_kv=v_w_shared_kv, b_forget=v_b_forget, w_in_b=v_w_in_b)
    names = list(weights)
    mx, my, mc = lax.axis_index("x"), lax.axis_index("y"), lax.axis_index("c")
    me = 4 * mx + 2 * my + mc

    h0 = x[0]
    mem0 = mem[0]
    tgt = loss_target[0]
    seq = h0.shape[0]

    vec = lambda a: a.reshape(1, -1)
    pad_to = lambda a, axis, size: jnp.pad(a, [(0, size - a.shape[i] if i == axis else 0) for i in range(a.ndim)])

    def after(tok, a):
        return a + tok[0, 0].astype(a.dtype)

    lnv_loc = pad_to(jnp.concatenate([ln_v_g, ln_v_b], axis=0), 0, 8)
    st_a = gather_start([w_in_a.astype(bf16), pad_to(lnv_loc, 1, LANES)[None]], [0, 0], "gather_a_start")
    mix_locs = lambda l, tok: [after(tok, w_mem_kv[l]).astype(bf16), w_out[l].astype(bf16)]
    def ffn_gather_start(l, tok):
        gate_up = gather_start([pad_to(after(tok, w_ffn_gate[l]).astype(bf16), 1, FF_SHARD_PAD),
                                pad_to(w_ffn_up[l].astype(bf16), 1, FF_SHARD_PAD)], [1, 1], f"gather_gate_up{l}_start")
        down = gather_start([pad_to(after(gate_up["token"], w_ffn_down[l]).astype(bf16), 0, FF_SHARD_PAD)], [0], f"gather_down{l}_start")
        return gate_up, down

    st_b = [gather_start(mix_locs(0, st_a["token"]), [0, 0], "gather_b0_start"), None]
    st_c = ffn_gather_start(0, st_b[0]["token"])
    st_d = gather_start([after(st_c[1]["token"], w_in_b[0]).astype(bf16), pad_to(w_shared_kv.astype(bf16), 1, KV_PAD)], [0, 0],
                        "gather_d_start")
    st_b[1] = gather_start(mix_locs(1, st_d["token"]), [0, 0], "gather_b1_start")
    st_e = ffn_gather_start(1, st_b[1]["token"])
    ws = w_spatial[0].astype(bf16)
    ws_t = ws.transpose(0, 2, 1)
    bs_t = b_spatial[0].T

    (a0,) = rms_fwd(h0, [after(st_e[1]["token"], vec(ln_mix_pre[0]))], "a0_norm")
    w_in_a8, lnv8 = gather_wait(st_a, [0, 0], a0, "gather_a_wait")
    w_in_a_full = w_in_a8.transpose(1, 0, 2).reshape(D_MODEL, -1)
    lnv_g = lnv8[:, 0, :MAIN_WIDTH // N_DEV].reshape(1, MAIN_WIDTH)
    lnv_b = lnv8[:, 1, :MAIN_WIDTH // N_DEV].reshape(1, MAIN_WIDTH)
    proj0 = mm(a0, w_in_a_full, "proj0", tn=896)
    main0 = gmlp_fwd(proj0, ws, bs_t, lnv_g, lnv_b, "gmlp_fwd")
    w_mkv, w_o = [None, None], [None, None]
    w_mkv[0], w_o[0] = gather_wait(st_b[0], [0, 0], main0, "gather_b0_wait")
    (memn0,) = rms_fwd(mem0, [vec(ln_mem[0])], "mem0_norm")
    kvm0 = mm(memn0, w_mkv[0], "kvm0")
    om0 = mem_attn_fwd(proj0, 2 * MAIN_WIDTH // MEM_WIDTH, kvm0, "mem_attn0")
    mixed0 = jnp.concatenate([main0, om0], axis=-1)
    y1_0, hmid0, f0 = mm_resnorm(mixed0, w_o[0], h0, vec(ln_mix_post[0]), [vec(ln_ffn_pre[0])], "mix_out0")
    w_g0, w_u0 = gather_wait(st_c[0], [1, 1], f0, "gather_gate_up0_wait")
    gu0, act0 = ffn_up(f0, w_g0, w_u0, "ffn_up0")
    (w_d0,) = gather_wait(st_c[1], [0], act0, "gather_down0_wait")
    y2_0, h1, a1, sin1 = mm_resnorm(act0, w_d0, hmid0, vec(ln_ffn_post[0]), [vec(ln_mix_pre[1]), vec(ln_shared)], "ffn_down0")

    w_inb, w_kv = gather_wait(st_d, [0, 0], sin1, "gather_d_wait")
    kvb = mm(sin1, w_kv, "kv_shared", out_dtype=bf16, tn=MAIN_WIDTH, ncols=2 * MAIN_WIDTH)
    zf = mm(sin1, w_kv, "forget_logits", tn=256, col0=2 * MAIN_WIDTH, ncols=256)
    qb = mm(a1, w_inb, "proj1", out_dtype=bf16)
    z_t = jnp.pad(zf[:, :FOX_HEADS].T, ((0, 16 - FOX_HEADS), (0, 0)))
    bf_col = jnp.pad(b_forget, (0, 16 - FOX_HEADS)).reshape(16, 1)
    c_t = fgate_fwd(z_t, bf_col, "fgate_fwd")
    c_row = c_t[:FOX_HEADS].reshape(FOX_PAIRS, 2, seq)
    main1, lse = fox_fwd(qb, kvb, c_row, "fox_fwd")
    w_mkv[1], w_o[1] = gather_wait(st_b[1], [0, 0], main1, "gather_b1_wait")
    (memn1,) = rms_fwd(mem0, [vec(ln_mem[1])], "mem1_norm")
    kvm1 = mm(memn1, w_mkv[1], "kvm1")
    om1 = mem_attn_fwd(qb, MAIN_WIDTH // MEM_WIDTH, kvm1, "mem_attn1")
    mixed1 = jnp.concatenate([main1.astype(bf16), om1], axis=-1)
    y1_1, hmid1, f1 = mm_resnorm(mixed1, w_o[1], h1, vec(ln_mix_post[1]), [vec(ln_ffn_pre[1])], "mix_out1")
    w_g1, w_u1 = gather_wait(st_e[0], [1, 1], f1, "gather_gate_up1_wait")
    gu1, act1 = ffn_up(f1, w_g1, w_u1, "ffn_up1")
    (w_d1,) = gather_wait(st_e[1], [0], act1, "gather_down1_wait")
    dh, d_y2_1, dg_fpost1, loss_tile = mm_resnorm_loss(act1, w_d1, hmid1, vec(ln_ffn_post[1]), tgt, "ffn_down1_loss")
    ffn_w = [(w_g0, w_u0, w_d0), (w_g1, w_u1, w_d1)]

    small = {}

    def ffn_backward(layer, dh_out, d_y2, hmid, f, gu, act, y1):
        w_g, w_u, w_d = ffn_w[layer]
        dw_down = mm_tn(act, d_y2, f"dw_down{layer}")
        rs_down = scatter_start([dw_down], [0], f"scatter_down{layer}_start")
        d_g, d_u = ffn_act_grad(d_y2, w_d, gu, f"ffn_act_grad{layer}")
        dw_g = mm_tn(f, d_g, f"dw_gate{layer}", dep=rs_down["token"])
        dw_u = mm_tn(f, d_u, f"dw_up{layer}")
        rs_gate_up = scatter_start([dw_g, dw_u], [1, 1], f"scatter_gate_up{layer}_start")
        dh_mid, d_y1, dg_fpre, dg_mpost = ffn_in_grad(d_g, d_u, w_g, w_u, hmid, dh_out, after(rs_gate_up["token"], vec(ln_ffn_pre[layer])),
                                                      y1, vec(ln_mix_post[layer]), f"ffn_in_grad{layer}")
        return dh_mid, d_y1, dg_fpre, dg_mpost, (rs_down, rs_gate_up)

    def mix_out_backward(layer, d_y1, mixed):
        dw_out = mm_tn(mixed, d_y1, f"dw_out{layer}")
        d_mixed = mm(d_y1, w_o[layer], f"d_mixed{layer}", trans_b=True)
        return d_mixed, dw_out

    def mem_backward(layer, q_src, q_block, kvm, memn, d_mixed):
        d_qm, d_kvm = mem_attn_bwd(q_src, q_block, kvm, d_mixed, f"mem_attn_bwd{layer}")
        d_kvm_b = d_kvm.astype(bf16)
        dw_mkv = mm_tn(memn, d_kvm_b, f"dw_mem_kv{layer}")
        d_memn = mm(d_kvm_b, w_mkv[layer], f"d_memn{layer}", trans_b=True)
        _, dg_mem = rms_bwd(mem0, vec(ln_mem[layer]), d_memn, None, bf16, f"mem_norm_bwd{layer}")
        return d_qm, dw_mkv, dg_mem


    dh_mid1, d_y1_1, dg_fpre1, dg_mpost1, rs_ffn1 = ffn_backward(1, dh, d_y2_1, hmid1, f1, gu1, act1, y1_1)
    d_mixed1, dw_out1 = mix_out_backward(1, d_y1_1, mixed1)
    d_qm1, dw_mkv1, dg_mem1 = mem_backward(1, qb, MAIN_WIDTH // MEM_WIDTH, kvm1, memn1, d_mixed1)
    rs_mix1 = scatter_start([dw_out1, dw_mkv1], [0, 0], "scatter_mix1_start")
    dq, dk, dv, dc = fox_bwd(qb, kvb, d_mixed1, main1, lse, after(rs_mix1["token"], c_row), "fox_bwd")
    dc_t = jnp.pad(dc.reshape(FOX_HEADS, seq), ((0, 16 - FOX_HEADS), (0, 0)))
    dz_t, db_f = fgate_bwd(dc_t, z_t, bf_col, "fgate_bwd")
    d_kvf = jnp.concatenate([dk, dv, jnp.pad(dz_t[:FOX_HEADS].T.astype(bf16), ((0, 0), (0, KV_PAD - KV_WIDTH)))], axis=-1)
    d_proj1 = jnp.concatenate([dq.astype(bf16), d_qm1], axis=-1)
    dw_in_b = mm_tn(a1, d_proj1, "dw_in_b")
    dw_kv = mm_tn(sin1, d_kvf, "dw_kv", tn=896)
    rs_2 = scatter_start([dw_in_b, dw_kv], [0, 0], "scatter_shared_start")
    dh1, (dg_pre1, dg_shared) = proj_in_grad([(d_proj1, w_inb, vec(ln_mix_pre[1])), (d_kvf, w_kv, vec(ln_shared))], h1, dh_mid1,
                                             "in_grad1", dep=rs_2["token"])

    d_y2_0, dg_fpost0 = rms_bwd(y2_0, vec(ln_ffn_post[0]), dh1, None, bf16, "ffn_post_bwd0")
    dh_mid0, d_y1_0, dg_fpre0, dg_mpost0, rs_ffn0 = ffn_backward(0, dh1, d_y2_0, hmid0, f0, gu0, act0, y1_0)
    d_mixed0, dw_out0 = mix_out_backward(0, d_y1_0, mixed0)
    d_qm0, dw_mkv0, dg_mem0 = mem_backward(0, proj0, 2 * MAIN_WIDTH // MEM_WIDTH, kvm0, memn0, d_mixed0)
    rs_mix0 = scatter_start([dw_out0, dw_mkv0], [0, 0], "scatter_mix0_start")
    d_uv, dw_s, db_s, dg_lnv, db_lnv = gmlp_bwd(proj0, d_mixed0, ws, ws_t, bs_t, after(rs_mix0["token"], lnv_g), lnv_b, "gmlp_bwd")

    small["ln_mix_pre"] = jnp.concatenate([jnp.zeros_like(dg_pre1), dg_pre1], axis=0)
    small["ln_mix_post"] = jnp.concatenate([dg_mpost0, dg_mpost1], axis=0)
    small["ln_ffn_pre"] = jnp.concatenate([dg_fpre0, dg_fpre1], axis=0)
    small["ln_ffn_post"] = jnp.concatenate([dg_fpost0, dg_fpost1], axis=0)
    small["ln_mem"] = jnp.concatenate([dg_mem0, dg_mem1], axis=0)
    small["w_spatial"] = dw_s[None]
    small["b_spatial"] = db_s[:, :A_GROUPS].T[None]
    small["ln_shared"] = dg_shared[0]
    small["b_forget"] = db_f[:FOX_HEADS, 0]
    small["ln_v_g"] = dg_lnv
    small["ln_v_b"] = db_lnv
    small_rows = jnp.concatenate([_pack_small(small, _SMALL), loss_tile], axis=0)
    st_small = gather_start([small_rows[None]], [0], "gather_small_grads_start")
    d_proj0 = jnp.concatenate([d_uv, after(st_small["token"], d_qm0)], axis=-1)
    dw_in_a = mm_tn(a0, d_proj0, "dw_in_a", tn=896)
    rs_in_a = scatter_start([dw_in_a.reshape(D_MODEL, N_DEV, -1).transpose(1, 0, 2)], [0], "scatter_in_a_start")
    grad_x, (dg_pre0,) = proj_in_grad([(d_proj0, w_in_a_full, vec(ln_mix_pre[0]))], h0, dh_mid0, "in_grad0", dep=rs_in_a["token"])
    st_last = gather_start([dg_pre0.reshape(1, 8, LANES)], [0], "gather_last_grad_start")

    def owned(started, axes, wait_after, name):
        recv = scatter_wait(started, axes, wait_after, name)
        return [sum_leading(r.reshape((N_DEV, -1, r.shape[-1])), f"{name}_sum{i}", tr=_row_tile(math.prod(r.shape[1:-1])))
                for i, r in enumerate(recv)]

    (g_down1,) = owned(rs_ffn1[0], [0], after(st_last["token"], grad_x[:8, :LANES]), "scatter_down1_wait")
    g_gu1 = owned(rs_ffn1[1], [1, 1], g_down1, "scatter_gate_up1_wait")
    g_mix1 = owned(rs_mix1, [0, 0], g_gu1[0], "scatter_mix1_wait")
    g2 = owned(rs_2, [0, 0], g_mix1[0], "scatter_shared_wait")
    (g_down0,) = owned(rs_ffn0[0], [0], g2[0], "scatter_down0_wait")
    g_gu0 = owned(rs_ffn0[1], [1, 1], g_down0, "scatter_gate_up0_wait")
    g_mix0 = owned(rs_mix0, [0, 0], g_gu0[0], "scatter_mix0_wait")
    (g_in_a,) = owned(rs_in_a, [0], g_mix0[0], "scatter_in_a_wait")
    g_local = dict(
        w_ffn_gate=jnp.stack([g_gu0[0], g_gu1[0]])[:, :, :FF_SHARD], w_ffn_up=jnp.stack([g_gu0[1], g_gu1[1]])[:, :, :FF_SHARD],
        w_ffn_down=jnp.stack([g_down0, g_down1])[:, :FF_SHARD], w_out=jnp.stack([g_mix0[0], g_mix1[0]]),
        w_mem_kv=jnp.stack([g_mix0[1], g_mix1[1]]), w_in_b=g2[0][None], w_shared_kv=g2[1][:, :KV_WIDTH], w_in_a=g_in_a[None])
    (small_all,) = gather_wait(st_small, [0], g_in_a, "gather_small_grads_wait")
    (last_all,) = gather_wait(st_last, [0], small_all, "gather_last_grad_wait")
    small_sum = sum_leading(small_all, "sum_small_grads")
    loss = small_sum[small_rows.shape[0] - 1, 0]
    g_small = _unpack_small(small_sum, _SMALL)
    g_small["ln_mix_pre"] = jnp.concatenate([sum_leading(last_all, "sum_last_grad").reshape(1, D_MODEL), g_small["ln_mix_pre"][1:]], axis=0)
    shard = MAIN_WIDTH // N_DEV
    for n in ("ln_v_g", "ln_v_b"):
        g_small[n] = lax.dynamic_slice_in_dim(g_small[n], me * shard, shard, axis=1)
    grad_w = {**g_small, **g_local}

    delta, new_m, new_v = {}, {}, {}
    for n in g_local:
        two_d = (-1, weights[n].shape[-1])
        d_, m_, v_ = adamw(weights[n].reshape(two_d), grad_w[n].reshape(two_d), mom_m[n].reshape(two_d), mom_v[n].reshape(two_d),
                           f"adamw_{n}", tr=_row_tile(math.prod(weights[n].shape[:-1])))
        delta[n], new_m[n], new_v[n] = (t.reshape(weights[n].shape) for t in (d_, m_, v_))
    small_local_shapes = [(n, tuple(weights[n].shape)) for n, _ in _SMALL]
    packed = [_pack_small(src, small_local_shapes) for src in (weights, grad_w, mom_m, mom_v)]
    outs = adamw(*packed, "adamw_small", tr=packed[0].shape[0])
    for dst, buf in zip((delta, new_m, new_v), outs):
        dst.update(_unpack_small(buf, small_local_shapes))

    return (loss, grad_x[None], *[grad_w[n] for n in names], *[delta[n] for n in names],
            *[new_m[n] for n in names], *[new_v[n] for n in names])
```

```python
import functools
import math

import jax
import jax.numpy as jnp
from jax import lax
from jax.experimental import pallas as pl
from jax.experimental.pallas import tpu as pltpu

f32 = jnp.float32
bf16 = jnp.bfloat16
SDS = jax.ShapeDtypeStruct

D_MODEL = 1024
MAIN_WIDTH = 768
MEM_WIDTH = 256
HEAD_DIM = 64
MEM_HEADS = 4
FOX_HEADS = 12
FOX_PAIRS = FOX_HEADS // 2
CHUNK = 128
A_GROUPS = 6
FF_SHARD = 352
FF_SHARD_PAD = 384
FF_PAD = 8 * FF_SHARD_PAD
KV_WIDTH = 2 * MAIN_WIDTH + FOX_HEADS
KV_PAD = 1792
RMS_EPS = 1e-6
LN_EPS = 1e-5
ATT_SCALE = HEAD_DIM ** -0.5
ADAM_LR, ADAM_B1, ADAM_B2, ADAM_EPS, ADAM_WD, ADAM_STEP = 0.001, 0.9, 0.999, 1e-08, 0.01, 10
N_DEV = 8
AXES = ("x", "y", "c")
MESH = pl.DeviceIdType.MESH
V7X_VMEM_LIMIT = 56 * 1024 * 1024
LANES = 128
FLAT_W = 512
ROW_PAD = 16


def _cparams(*sem):
    return pltpu.CompilerParams(dimension_semantics=sem or None, vmem_limit_bytes=V7X_VMEM_LIMIT)


def _dot(a, b):
    return jnp.dot(a, b, preferred_element_type=f32)


def _dot_nt(a, b):
    return lax.dot_general(a, b, (((1,), (1,)), ((), ())), preferred_element_type=f32)


def _dot_tn(a, b):
    return lax.dot_general(a, b, (((0,), (0,)), ((), ())), preferred_element_type=f32)


def _gelu(x):
    k = math.sqrt(2.0 / math.pi)
    t = jnp.tanh(k * (x + 0.044715 * x * x * x))
    return 0.5 * x * (1.0 + t), t


def _gelu_grad(x, t):
    k = math.sqrt(2.0 / math.pi)
    return 0.5 * (1.0 + t) + 0.5 * x * (1.0 - t * t) * k * (1.0 + 3.0 * 0.044715 * x * x)


def _sigmoid(x):
    return 1.0 / (1.0 + jnp.exp(-x))


def rms_fwd(x, gains, name, tm=512):
    m, d = x.shape
    tm = min(tm, m)
    n = len(gains)

    def body(x_ref, *refs):
        xv = x_ref[...]
        y = xv * lax.rsqrt(jnp.sum(xv * xv, axis=-1, keepdims=True) * (1.0 / d) + RMS_EPS)
        for g_ref, o_ref in zip(refs[:n], refs[n:]):
            o_ref[...] = (y * g_ref[...]).astype(bf16)

    row = pl.BlockSpec((tm, d), lambda i: (i, 0))
    vec = pl.BlockSpec((1, d), lambda i: (0, 0))
    return pl.pallas_call(body, grid=(m // tm,), in_specs=[row] + [vec] * n, out_specs=[row] * n,
                          out_shape=[SDS((m, d), bf16)] * n, name=name, compiler_params=_cparams("parallel"))(x, *gains)


def rms_bwd(x, g, dy, add, out_dtype, name, tm=512):
    m, d = x.shape
    tm = min(tm, m)
    has_add = add is not None

    def body(x_ref, g_ref, dy_ref, *refs):
        dx_ref, dg_ref = refs[-2], refs[-1]
        xv = x_ref[...]
        dyv = dy_ref[...].astype(f32)
        r = lax.rsqrt(jnp.sum(xv * xv, axis=-1, keepdims=True) * (1.0 / d) + RMS_EPS)
        xn = xv * r
        dyg = dyv * g_ref[...]
        dx = r * (dyg - xn * (jnp.sum(dyg * xn, axis=-1, keepdims=True) * (1.0 / d)))
        if has_add:
            dx = dx + refs[0][...]
        dx_ref[...] = dx.astype(out_dtype)

        @pl.when(pl.program_id(0) == 0)
        def _():
            dg_ref[...] = jnp.zeros_like(dg_ref)

        dg_ref[...] += jnp.sum(dyv * xn, axis=0, keepdims=True)

    row = pl.BlockSpec((tm, d), lambda i: (i, 0))
    vec = pl.BlockSpec((1, d), lambda i: (0, 0))
    ins = [x, g, dy] + ([add] if has_add else [])
    return pl.pallas_call(body, grid=(m // tm,), in_specs=[row, vec, row] + ([row] if has_add else []),
                          out_specs=[row, vec], out_shape=[SDS((m, d), out_dtype), SDS((1, d), f32)], name=name,
                          compiler_params=_cparams("arbitrary"))(*ins)


def mm(a, b, name, trans_b=False, out_dtype=f32, tm=1024, tn=1024, layer=None, col0=0, ncols=None, dep=None):
    m, k = a.shape
    n_all = b.shape[-2] if trans_b else b.shape[-1]
    n = n_all if ncols is None else ncols
    tm, tn = min(tm, m), min(tn, n)
    assert m % tm == 0 and n % tn == 0 and col0 % tn == 0 and not (trans_b and col0), (name, m, n, tm, tn)
    jb = col0 // tn
    lead = () if layer is None else (None,)
    sel = () if layer is None else (layer,)

    def body(a_ref, b_ref, *rest):
        r = _dot_nt(a_ref[...], b_ref[...]) if trans_b else _dot(a_ref[...], b_ref[...])
        rest[-1][...] = r.astype(out_dtype)

    if trans_b:
        b_spec = pl.BlockSpec(lead + (tn, k), lambda j, i: sel + (j, 0))
    else:
        b_spec = pl.BlockSpec(lead + (k, tn), lambda j, i: sel + (0, jb + j))
    deps = [] if dep is None else [dep]
    dep_specs = [pl.BlockSpec((8, LANES), lambda j, i: (0, 0))] * len(deps)
    return pl.pallas_call(body, grid=(n // tn, m // tm), in_specs=[pl.BlockSpec((tm, k), lambda j, i: (i, 0)), b_spec] + dep_specs,
                          out_specs=pl.BlockSpec((tm, tn), lambda j, i: (i, j)), out_shape=SDS((m, n), out_dtype),
                          name=name, compiler_params=_cparams("parallel", "parallel"))(a, b, *deps)


def mm_tn(a, g, name, tk=1024, tn=1024, out_dtype=bf16, dep=None):
    s, k = a.shape
    n = g.shape[1]
    tk, tn = min(tk, k), min(tn, n)
    assert k % tk == 0 and n % tn == 0, (name, k, n, tk, tn)

    def body(a_ref, g_ref, *rest):
        rest[-1][...] = _dot_tn(a_ref[...], g_ref[...]).astype(out_dtype)

    deps = [] if dep is None else [dep]
    dep_specs = [pl.BlockSpec((8, LANES), lambda i, j: (0, 0))] * len(deps)
    return pl.pallas_call(body, grid=(k // tk, n // tn),
                          in_specs=[pl.BlockSpec((s, tk), lambda i, j: (0, i)), pl.BlockSpec((s, tn), lambda i, j: (0, j))] + dep_specs,
                          out_specs=pl.BlockSpec((tk, tn), lambda i, j: (i, j)), out_shape=SDS((k, n), out_dtype), name=name,
                          compiler_params=_cparams("parallel", "parallel"))(a, g, *deps)


def _resident(shape, index_map):
    return pl.BlockSpec(shape, index_map, pipeline_mode=pl.Buffered(1))


def _rms(xv):
    return xv * lax.rsqrt(jnp.sum(xv * xv, axis=-1, keepdims=True) * (1.0 / xv.shape[-1]) + RMS_EPS)


def _rms_bwd_math(xv, g, dy):
    d = xv.shape[-1]
    r = lax.rsqrt(jnp.sum(xv * xv, axis=-1, keepdims=True) * (1.0 / d) + RMS_EPS)
    xn = xv * r
    dyg = dy * g
    dx = r * (dyg - xn * (jnp.sum(dyg * xn, axis=-1, keepdims=True) * (1.0 / d)))
    return dx, jnp.sum(dy * xn, axis=0, keepdims=True)


SUB_ROWS = 512


def mm_resnorm(a, b, h, g_post, gains, name, tm=512):
    m, k = a.shape
    d = b.shape[1]
    n = len(gains)

    def body(a_ref, b_ref, h_ref, gp_ref, *refs):
        for r in range(tm // SUB_ROWS):
            rows = slice(r * SUB_ROWS, (r + 1) * SUB_ROWS)
            y = _dot(a_ref[rows, :], b_ref[...])
            refs[n][rows, :] = y
            hn = h_ref[rows, :] + _rms(y) * gp_ref[...]
            refs[n + 1][rows, :] = hn
            if n:
                z = _rms(hn)
                for g_ref, o_ref in zip(refs[:n], refs[n + 2:]):
                    o_ref[rows, :] = (z * g_ref[...]).astype(bf16)

    row = pl.BlockSpec((tm, d), lambda i: (i, 0))
    vec = pl.BlockSpec((1, d), lambda i: (0, 0))
    return pl.pallas_call(body, grid=(m // tm,),
                          in_specs=[pl.BlockSpec((tm, k), lambda i: (i, 0)), _resident((k, d), lambda i: (0, 0)), row, vec] + [vec] * n,
                          out_specs=[row] * (n + 2), out_shape=[SDS((m, d), f32)] * 2 + [SDS((m, d), bf16)] * n, name=name,
                          compiler_params=_cparams("parallel"))(a, b, h, g_post, *gains)


def mm_resnorm_loss(a, b, h, g_post, tgt, name, tm=512):
    m, k = a.shape
    d = b.shape[1]

    def body(a_ref, b_ref, h_ref, gp_ref, t_ref, dh_ref, dy_ref, dg_ref, l_ref):
        @pl.when(pl.program_id(0) == 0)
        def _():
            dg_ref[...] = jnp.zeros_like(dg_ref)
            l_ref[...] = jnp.zeros_like(l_ref)

        y = _dot(a_ref[...], b_ref[...])
        e = h_ref[...] + _rms(y) * gp_ref[...] - t_ref[...]
        dh = e * (1.0 / d)
        dh_ref[...] = dh
        part = jnp.sum(jnp.sum(e * e, axis=-1, keepdims=True), axis=0, keepdims=True) * (0.5 / d)
        l_ref[...] += jnp.broadcast_to(part, l_ref.shape)
        dy, dg = _rms_bwd_math(y, gp_ref[...], dh)
        dy_ref[...] = dy.astype(bf16)
        dg_ref[...] += dg

    row = pl.BlockSpec((tm, d), lambda i: (i, 0))
    vec = pl.BlockSpec((1, d), lambda i: (0, 0))
    return pl.pallas_call(body, grid=(m // tm,),
                          in_specs=[pl.BlockSpec((tm, k), lambda i: (i, 0)), _resident((k, d), lambda i: (0, 0)), row, vec, row],
                          out_specs=[row, row, vec, pl.BlockSpec((8, LANES), lambda i: (0, 0))],
                          out_shape=[SDS((m, d), f32), SDS((m, d), bf16), SDS((1, d), f32), SDS((8, LANES), f32)], name=name,
                          compiler_params=_cparams("arbitrary"))(a, b, h, g_post, tgt)


def ffn_act_grad(d_y2, w_d, factors, name, tm=1024, tn=1536):
    s, d = d_y2.shape
    ff = w_d.shape[0]
    nb = ff // tn

    def body(a_ref, b_ref, g_ref, u_ref, dg_ref, du_ref):
        av = a_ref[...]
        tc = 256
        for c in range(tn // tc):
            cols = slice(c * tc, (c + 1) * tc)
            da = _dot_nt(av, b_ref[cols, :])
            dg_ref[:, cols] = (da * g_ref[:, cols].astype(f32)).astype(bf16)
            du_ref[:, cols] = (da * u_ref[:, cols].astype(f32)).astype(bf16)

    tile = pl.BlockSpec((tm, tn), lambda j, i: (i, j))
    return pl.pallas_call(body, grid=(nb, s // tm),
                          in_specs=[pl.BlockSpec((tm, d), lambda j, i: (i, 0)), pl.BlockSpec((tn, d), lambda j, i: (j, 0)), tile,
                                    pl.BlockSpec((tm, tn), lambda j, i: (i, nb + j))],
                          out_specs=[tile, tile], out_shape=[SDS((s, ff), bf16)] * 2, name=name,
                          compiler_params=_cparams("parallel", "parallel"))(d_y2, w_d, factors, factors)


def ffn_in_grad(d_g, d_u, w_g, w_u, hmid, dh_out, g_pre, y1, g_post, name, tm=512):
    s, ff = d_g.shape
    d = w_g.shape[0]

    def body(dg_ref, du_ref, wg_ref, wu_ref, hm_ref, dho_ref, gpre_ref, y1_ref, gpost_ref, dhm_ref, dy1_ref, dgpre_ref, dgpost_ref):
        @pl.when(pl.program_id(0) == 0)
        def _():
            dgpre_ref[...] = jnp.zeros_like(dgpre_ref)
            dgpost_ref[...] = jnp.zeros_like(dgpost_ref)

        for r in range(tm // SUB_ROWS):
            rows = slice(r * SUB_ROWS, (r + 1) * SUB_ROWS)
            d_f = _dot_nt(dg_ref[rows, :], wg_ref[...]) + _dot_nt(du_ref[rows, :], wu_ref[...])
            dx, dg1 = _rms_bwd_math(hm_ref[rows, :], gpre_ref[...], d_f)
            dh_mid = dho_ref[rows, :] + dx
            dhm_ref[rows, :] = dh_mid
            dgpre_ref[...] += dg1
            dy1, dg2 = _rms_bwd_math(y1_ref[rows, :], gpost_ref[...], dh_mid)
            dy1_ref[rows, :] = dy1.astype(bf16)
            dgpost_ref[...] += dg2

    row = pl.BlockSpec((tm, d), lambda i: (i, 0))
    vec = pl.BlockSpec((1, d), lambda i: (0, 0))
    wide = pl.BlockSpec((tm, ff), lambda i: (i, 0))
    w_spec = _resident((d, ff), lambda i: (0, 0))
    return pl.pallas_call(body, grid=(s // tm,), in_specs=[wide, wide, w_spec, w_spec, row, row, vec, row, vec],
                          out_specs=[row, row, vec, vec], out_shape=[SDS((s, d), f32), SDS((s, d), bf16), SDS((1, d), f32), SDS((1, d), f32)],
                          name=name, compiler_params=_cparams("arbitrary"))(d_g, d_u, w_g, w_u, hmid, dh_out, g_pre, y1, g_post)


def proj_in_grad(pairs, x, add, name, tm=512, dep=None, below=None):
    s, d = x.shape
    n = len(pairs)
    extra = [] if dep is None else [dep]
    n_below = 0 if below is None else 2

    def body(*refs):
        x_ref, add_ref = refs[3 * n], refs[3 * n + 1]
        below_refs = refs[3 * n + 2:3 * n + 2 + n_below]
        outs = refs[3 * n + 2 + n_below + len(extra):]

        @pl.when(pl.program_id(0) == 0)
        def _():
            for o in outs[1:1 + n] + outs[2 + n:]:
                o[...] = jnp.zeros_like(o)

        xv = x_ref[...]
        dx = add_ref[...]
        for i in range(n):
            a_ref, b_ref, g_ref = refs[3 * i:3 * i + 3]
            dxi, dgi = _rms_bwd_math(xv, g_ref[...], _dot_nt(a_ref[...], b_ref[...]))
            dx = dx + dxi
            outs[1 + i][...] += dgi
        outs[0][...] = dx
        if below is not None:
            dy, dg = _rms_bwd_math(below_refs[0][...], below_refs[1][...], dx)
            outs[1 + n][...] = dy.astype(bf16)
            outs[2 + n][...] += dg

    row = pl.BlockSpec((tm, d), lambda i: (i, 0))
    vec = pl.BlockSpec((1, d), lambda i: (0, 0))
    in_specs, args = [], []
    for a, b, g in pairs:
        k = a.shape[1]
        in_specs += [pl.BlockSpec((tm, k), lambda i: (i, 0)), _resident((d, k), lambda i: (0, 0)), vec]
        args += [a, b, g]
    in_specs += [row, row] + [row, vec][:n_below] + [pl.BlockSpec((8, LANES), lambda i: (0, 0))] * len(extra)
    out_specs = [row] + [vec] * n + [row, vec][:n_below]
    out_shape = [SDS((s, d), f32)] + [SDS((1, d), f32)] * n + [SDS((s, d), bf16), SDS((1, d), f32)][:n_below]
    out = pl.pallas_call(body, grid=(s // tm,), in_specs=in_specs, out_specs=out_specs, out_shape=out_shape, name=name,
                         compiler_params=_cparams("arbitrary"))(*args, x, add, *(below or ()), *extra)
    return (out[0], out[1:1 + n]) + tuple(out[1 + n:])


def ffn_up(f, wg, wu, name, tm=512, tc=256):
    s, d = f.shape
    ff = wg.shape[-1]

    def body(f_ref, wg_ref, wu_ref, fac_ref, act_ref):
        fv = f_ref[...]
        for j in range(ff // tc):
            lo = j * tc
            gg = _dot(fv, wg_ref[:, lo:lo + tc])
            uu = _dot(fv, wu_ref[:, lo:lo + tc])
            sg = _sigmoid(gg)
            silu = gg * sg
            fac_ref[:, lo:lo + tc] = (uu * (sg + silu * (1.0 - sg))).astype(bf16)
            fac_ref[:, ff + lo:ff + lo + tc] = silu.astype(bf16)
            act_ref[:, lo:lo + tc] = (silu * uu).astype(bf16)

    w_spec = _resident((d, ff), lambda i: (0, 0))
    return pl.pallas_call(body, grid=(s // tm,), in_specs=[pl.BlockSpec((tm, d), lambda i: (i, 0)), w_spec, w_spec],
                          out_specs=[pl.BlockSpec((tm, 2 * ff), lambda i: (i, 0)), pl.BlockSpec((tm, ff), lambda i: (i, 0))],
                          out_shape=[SDS((s, 2 * ff), bf16), SDS((s, ff), bf16)], name=name,
                          compiler_params=_cparams("parallel"))(f, wg, wu)


def _gmlp_forward_chunk(u, v, w_refs, bias, ln_g, ln_b):
    gu, tu = _gelu(u)
    gv, tv = _gelu(v)
    mu = jnp.sum(gv, axis=-1, keepdims=True) * (1.0 / MAIN_WIDTH)
    xc = gv - mu
    rstd = lax.rsqrt(jnp.sum(xc * xc, axis=-1, keepdims=True) * (1.0 / MAIN_WIDTH) + LN_EPS)
    xhat = xc * rstd
    vln = xhat * ln_g + ln_b
    row = lax.broadcasted_iota(jnp.int32, (CHUNK, CHUNK), 0)
    col = lax.broadcasted_iota(jnp.int32, (CHUNK, CHUNK), 1)
    s_parts = []
    for g in range(A_GROUPS):
        w = jnp.where(col <= row, w_refs[g], jnp.zeros((), bf16))
        s_parts.append(_dot(w, vln[:, g * CHUNK:(g + 1) * CHUNK].astype(bf16)) + bias[:, g:g + 1])
    return gu, tu, tv, rstd, xhat, vln, s_parts


def gmlp_fwd(proj, ws, bs_t, ln_g, ln_b, name, tm=512):
    s = proj.shape[0]

    def body(u_ref, v_ref, w_ref, b_ref, g_ref, bb_ref, o_ref):
        bias = b_ref[...]
        for c in range(tm // CHUNK):
            rows = slice(c * CHUNK, (c + 1) * CHUNK)
            gu, _, _, _, _, _, s_parts = _gmlp_forward_chunk(u_ref[rows, :], v_ref[rows, :], w_ref, bias, g_ref[...], bb_ref[...])
            for g in range(A_GROUPS):
                cols = slice(g * CHUNK, (g + 1) * CHUNK)
                o_ref[rows, cols] = (gu[:, cols] * s_parts[g]).astype(bf16)

    vec = pl.BlockSpec((1, MAIN_WIDTH), lambda i: (0, 0))
    return pl.pallas_call(
        body, grid=(s // tm,),
        in_specs=[pl.BlockSpec((tm, MAIN_WIDTH), lambda i: (i, 0)), pl.BlockSpec((tm, MAIN_WIDTH), lambda i: (i, 1)),
                  pl.BlockSpec((A_GROUPS, CHUNK, CHUNK), lambda i: (0, 0, 0)), pl.BlockSpec((CHUNK, A_GROUPS), lambda i: (0, 0)), vec, vec],
        out_specs=pl.BlockSpec((tm, MAIN_WIDTH), lambda i: (i, 0)), out_shape=SDS((s, MAIN_WIDTH), bf16), name=name,
        compiler_params=_cparams("parallel"))(proj, proj, ws, bs_t, ln_g, ln_b)


def gmlp_bwd(proj, d_mixed, ws, ws_t, bs_t, ln_g, ln_b, name, tm=512):
    s = proj.shape[0]

    def body(u_ref, v_ref, dm_ref, w_ref, wt_ref, b_ref, g_ref, bb_ref, duv_ref, dw_ref, db_ref, dg_ref, dbb_ref):
        @pl.when(pl.program_id(0) == 0)
        def _():
            dw_ref[...] = jnp.zeros_like(dw_ref)
            db_ref[...] = jnp.zeros_like(db_ref)
            dg_ref[...] = jnp.zeros_like(dg_ref)
            dbb_ref[...] = jnp.zeros_like(dbb_ref)

        bias = b_ref[...]
        ln_gv = g_ref[...]
        row = lax.broadcasted_iota(jnp.int32, (CHUNK, CHUNK), 0)
        col = lax.broadcasted_iota(jnp.int32, (CHUNK, CHUNK), 1)
        lane = lax.broadcasted_iota(jnp.int32, (CHUNK, LANES), 1)
        for c in range(tm // CHUNK):
            rows = slice(c * CHUNK, (c + 1) * CHUNK)
            u = u_ref[rows, :]
            v = v_ref[rows, :]
            gu, tu, tv, rstd, xhat, vln, s_parts = _gmlp_forward_chunk(u, v, w_ref, bias, ln_gv, bb_ref[...])
            dm = dm_ref[rows, :]
            d_vln_parts = []
            d_gu_parts = []
            db_acc = jnp.zeros((CHUNK, LANES), f32)
            for g in range(A_GROUPS):
                cols = slice(g * CHUNK, (g + 1) * CHUNK)
                dmg = dm[:, cols]
                d_gu_parts.append(dmg * s_parts[g])
                d_s = dmg * gu[:, cols]
                db_acc = db_acc + jnp.where(lane == g, jnp.sum(d_s, axis=-1, keepdims=True), 0.0)
                d_sb = d_s.astype(bf16)
                dw_ref[g] += jnp.where(col <= row, _dot_nt(d_sb, vln[:, cols].astype(bf16)), 0.0)
                wt = jnp.where(row <= col, wt_ref[g], jnp.zeros((), bf16))
                d_vln_parts.append(_dot(wt, d_sb))
            db_ref[...] += db_acc
            d_vln = jnp.concatenate(d_vln_parts, axis=-1)
            d_gu = jnp.concatenate(d_gu_parts, axis=-1)
            dg_ref[...] += jnp.sum(d_vln * xhat, axis=0, keepdims=True)
            dbb_ref[...] += jnp.sum(d_vln, axis=0, keepdims=True)
            dxh = d_vln * ln_gv
            m1 = jnp.sum(dxh, axis=-1, keepdims=True) * (1.0 / MAIN_WIDTH)
            m2 = jnp.sum(dxh * xhat, axis=-1, keepdims=True) * (1.0 / MAIN_WIDTH)
            d_gv = rstd * (dxh - m1 - xhat * m2)
            duv_ref[rows, :MAIN_WIDTH] = (d_gu * _gelu_grad(u, tu)).astype(bf16)
            duv_ref[rows, MAIN_WIDTH:] = (d_gv * _gelu_grad(v, tv)).astype(bf16)

    vec = pl.BlockSpec((1, MAIN_WIDTH), lambda i: (0, 0))
    wspec = pl.BlockSpec((A_GROUPS, CHUNK, CHUNK), lambda i: (0, 0, 0))
    return pl.pallas_call(
        body, grid=(s // tm,),
        in_specs=[pl.BlockSpec((tm, MAIN_WIDTH), lambda i: (i, 0)), pl.BlockSpec((tm, MAIN_WIDTH), lambda i: (i, 1)),
                  pl.BlockSpec((tm, MAIN_WIDTH), lambda i: (i, 0)), wspec, wspec, pl.BlockSpec((CHUNK, A_GROUPS), lambda i: (0, 0)), vec, vec],
        out_specs=[pl.BlockSpec((tm, 2 * MAIN_WIDTH), lambda i: (i, 0)), wspec, pl.BlockSpec((CHUNK, LANES), lambda i: (0, 0)), vec, vec],
        out_shape=[SDS((s, 2 * MAIN_WIDTH), bf16), SDS((A_GROUPS, CHUNK, CHUNK), f32), SDS((CHUNK, LANES), f32),
                   SDS((1, MAIN_WIDTH), f32), SDS((1, MAIN_WIDTH), f32)],
        name=name, compiler_params=_cparams("arbitrary"))(proj, proj, d_mixed, ws, ws_t, bs_t, ln_g, ln_b)


def _head_mask(width, h):
    lane = lax.broadcasted_iota(jnp.int32, (1, width), 1)
    return (lane >= h * HEAD_DIM) & (lane < (h + 1) * HEAD_DIM)


def mem_attn_fwd(proj, q_block, kv, name, tm=512):
    s = proj.shape[0]
    n_mem = kv.shape[0]

    def body(q_ref, kv_ref, o_ref):
        q = q_ref[...].astype(f32)
        k = kv_ref[:, :MEM_WIDTH].astype(bf16)
        v = kv_ref[:, MEM_WIDTH:].astype(bf16)
        out = jnp.zeros((tm, MEM_WIDTH), f32)
        for h in range(MEM_HEADS):
            msk = _head_mask(MEM_WIDTH, h)
            qh = jnp.where(msk, q, 0.0).astype(bf16)
            sc = _dot_nt(qh, k) * ATT_SCALE
            e = jnp.exp(sc - jnp.max(sc, axis=-1, keepdims=True))
            p = e / jnp.sum(e, axis=-1, keepdims=True)
            out = jnp.where(msk, _dot(p.astype(bf16), v), out)
        o_ref[...] = out.astype(bf16)

    return pl.pallas_call(body, grid=(s // tm,),
                          in_specs=[pl.BlockSpec((tm, MEM_WIDTH), lambda i: (i, q_block)), pl.BlockSpec((n_mem, 2 * MEM_WIDTH), lambda i: (0, 0))],
                          out_specs=pl.BlockSpec((tm, MEM_WIDTH), lambda i: (i, 0)), out_shape=SDS((s, MEM_WIDTH), bf16), name=name,
                          compiler_params=_cparams("parallel"))(proj, kv)


def mem_attn_bwd(proj, q_block, kv, d_mixed, name, tm=512):
    s = proj.shape[0]
    n_mem = kv.shape[0]

    def body(q_ref, kv_ref, do_ref, dq_ref, dkv_ref):
        @pl.when(pl.program_id(0) == 0)
        def _():
            dkv_ref[...] = jnp.zeros_like(dkv_ref)

        q = q_ref[...].astype(f32)
        do = do_ref[...]
        k = kv_ref[:, :MEM_WIDTH].astype(bf16)
        v = kv_ref[:, MEM_WIDTH:].astype(bf16)
        dq = jnp.zeros((tm, MEM_WIDTH), f32)
        dk = jnp.zeros((n_mem, MEM_WIDTH), f32)
        dv = jnp.zeros((n_mem, MEM_WIDTH), f32)
        for h in range(MEM_HEADS):
            msk = _head_mask(MEM_WIDTH, h)
            qh = jnp.where(msk, q, 0.0).astype(bf16)
            doh = jnp.where(msk, do, 0.0).astype(bf16)
            sc = _dot_nt(qh, k) * ATT_SCALE
            e = jnp.exp(sc - jnp.max(sc, axis=-1, keepdims=True))
            p = e / jnp.sum(e, axis=-1, keepdims=True)
            dp = _dot_nt(doh, v)
            ds = p * (dp - jnp.sum(dp * p, axis=-1, keepdims=True))
            dsb = (ds * ATT_SCALE).astype(bf16)
            dq = jnp.where(msk, _dot(dsb, k), dq)
            dk = dk + _dot_tn(dsb, qh)
            dv = dv + _dot_tn(p.astype(bf16), doh)
        dq_ref[...] = dq.astype(bf16)
        dkv_ref[:, :MEM_WIDTH] += dk
        dkv_ref[:, MEM_WIDTH:] += dv

    return pl.pallas_call(
        body, grid=(s // tm,),
        in_specs=[pl.BlockSpec((tm, MEM_WIDTH), lambda i: (i, q_block)), pl.BlockSpec((n_mem, 2 * MEM_WIDTH), lambda i: (0, 0)),
                  pl.BlockSpec((tm, MEM_WIDTH), lambda i: (i, MAIN_WIDTH // MEM_WIDTH))],
        out_specs=[pl.BlockSpec((tm, MEM_WIDTH), lambda i: (i, 0)), pl.BlockSpec((n_mem, 2 * MEM_WIDTH), lambda i: (0, 0))],
        out_shape=[SDS((s, MEM_WIDTH), bf16), SDS((n_mem, 2 * MEM_WIDTH), f32)], name=name,
        compiler_params=_cparams("arbitrary"))(proj, kv, d_mixed)


def _tri(t, upper):
    r = lax.broadcasted_iota(jnp.int32, (t, t), 0)
    c = lax.broadcasted_iota(jnp.int32, (t, t), 1)
    return ((r <= c) if upper else (r >= c)).astype(f32)


def fgate_fwd(z_t, b, name, t=512):
    hh, s = z_t.shape

    def body(z_ref, b_ref, c_ref):
        u = _tri(t, True)
        carry = jnp.zeros((hh, 1), f32)
        for blk in range(s // t):
            x = z_ref[:, blk * t:(blk + 1) * t] + b_ref[...]
            logf = jnp.minimum(x, 0.0) - jnp.log(1.0 + jnp.exp(-jnp.abs(x)))
            y = jnp.dot(logf, u, precision=lax.Precision.HIGHEST, preferred_element_type=f32) + carry
            c_ref[:, blk * t:(blk + 1) * t] = y
            carry = y[:, t - 1:t]

    return pl.pallas_call(body, out_shape=SDS((hh, s), f32), name=name, compiler_params=_cparams())(z_t, b)


def fgate_bwd(dc_t, z_t, b, name, t=512):
    hh, s = z_t.shape

    def body(dc_ref, z_ref, b_ref, dz_ref, db_ref):
        low = _tri(t, False)
        carry = jnp.zeros((hh, 1), f32)
        total = jnp.zeros((hh, 1), f32)
        for blk in reversed(range(s // t)):
            cols = slice(blk * t, (blk + 1) * t)
            y = jnp.dot(dc_ref[:, cols], low, precision=lax.Precision.HIGHEST, preferred_element_type=f32) + carry
            carry = y[:, 0:1]
            dz = y * _sigmoid(-(z_ref[:, cols] + b_ref[...]))
            dz_ref[:, cols] = dz
            total = total + jnp.sum(dz, axis=-1, keepdims=True)
        db_ref[...] = jnp.broadcast_to(total, db_ref.shape)

    return pl.pallas_call(body, out_shape=[SDS((hh, s), f32), SDS((hh, LANES), f32)], name=name,
                          compiler_params=_cparams())(dc_t, z_t, b)


def _pair_masks():
    lane = lax.broadcasted_iota(jnp.int32, (1, LANES), 1)
    return [lane < HEAD_DIM, lane >= HEAD_DIM]


def _tile_base(cr_ref, hh, lo):
    return cr_ref[hh:hh + 1, pl.ds(lo, LANES)][:, 0:1]


def fox_fwd(q, kv, c_row, name, tq=512):
    s = kv.shape[0]
    nq = s // tq

    def body(q_ref, k_ref, v_ref, cr_ref, o_ref, lse_ref, ob_ref):
        i = pl.program_id(1)
        qv = q_ref[...]
        masks = _pair_masks()
        row = lax.broadcasted_iota(jnp.int32, (tq, tq), 0)
        col = lax.broadcasted_iota(jnp.int32, (tq, tq), 1)
        qh = [jnp.where(masks[hh], qv, jnp.zeros((), bf16)) * ATT_SCALE for hh in range(2)]
        ct = [_tile_base(cr_ref, hh, pl.multiple_of(i * tq, tq)) for hh in range(2)]

        def block(j, carry, diag):
            lo = pl.multiple_of(j * tq, tq)
            ks = k_ref[pl.ds(lo, tq), :]
            vs = v_ref[pl.ds(lo, tq), :]
            out = []
            for hh in range(2):
                m, l, acc = carry[hh]
                sc = _dot_nt(qh[hh], ks) + (ct[hh] - cr_ref[hh:hh + 1, pl.ds(lo, tq)])
                if diag:
                    sc = jnp.where(col <= row, sc, -jnp.inf)
                m_new = jnp.maximum(m, jnp.max(sc, axis=-1, keepdims=True))
                alpha = jnp.exp(m - m_new)
                p = jnp.exp(sc - m_new)
                l = alpha * l + jnp.sum(p, axis=-1, keepdims=True)
                p_hi = p.astype(bf16)
                p_lo = (p - p_hi.astype(f32)).astype(bf16)
                acc = alpha * acc + (_dot(p_hi, vs) + _dot(p_lo, vs))
                out.append((m_new, l, acc))
            return tuple(out)

        init = (jnp.full((tq, 1), -jnp.inf, f32), jnp.zeros((tq, 1), f32), jnp.zeros((tq, LANES), f32))
        carry = lax.fori_loop(0, i, functools.partial(block, diag=False), (init, init))
        res = [(acc / l, m + jnp.log(l)) for m, l, acc in block(i, carry, True)]
        out = jnp.where(masks[0], res[0][0], res[1][0])
        o_ref[...] = out
        ob_ref[...] = out.astype(bf16)
        lse_ref[...] = jnp.where(masks[0], res[0][1], res[1][1])

    return pl.pallas_call(
        body, grid=(FOX_PAIRS, nq),
        in_specs=[pl.BlockSpec((tq, LANES), lambda p, i: (i, p)), pl.BlockSpec((s, LANES), lambda p, i: (0, p)),
                  pl.BlockSpec((s, LANES), lambda p, i: (0, FOX_PAIRS + p)), pl.BlockSpec((None, 2, s), lambda p, i: (p, 0, 0))],
        out_specs=[pl.BlockSpec((tq, LANES), lambda p, i: (i, p)), pl.BlockSpec((None, tq, LANES), lambda p, i: (p, i, 0)),
                   pl.BlockSpec((tq, LANES), lambda p, i: (i, p))],
        out_shape=[SDS((s, MAIN_WIDTH), f32), SDS((FOX_PAIRS, s, LANES), f32), SDS((s, MAIN_WIDTH), bf16)], name=name,
        compiler_params=_cparams("parallel", "parallel"))(q, kv, kv, c_row)


def fox_bwd(q, kv, d_mixed, o, lse, c_row, name, tq=512):
    s = kv.shape[0]
    nq = s // tq

    def body(q_ref, k_ref, v_ref, do_ref, o_ref, lse_ref, cr_ref, dq_ref, dk_ref, dv_ref, dc_ref):
        j = pl.program_id(1)

        @pl.when(j == 0)
        def _():
            dq_ref[...] = jnp.zeros_like(dq_ref)

        masks = _pair_masks()
        sub = lax.broadcasted_iota(jnp.int32, (LANES, 1), 0)
        sub_masks = [sub < HEAD_DIM, sub >= HEAD_DIM]
        row = lax.broadcasted_iota(jnp.int32, (tq, tq), 0)
        col = lax.broadcasted_iota(jnp.int32, (tq, tq), 1)
        kj = k_ref[...]
        vj = v_ref[...]
        lo_j = pl.multiple_of(j * tq, tq)

        def block(i, carry, diag):
            dk_t, dv_t, dc0, dc1 = carry
            dcs = [dc0, dc1]
            lo = pl.multiple_of(i * tq, tq)
            qi = q_ref[pl.ds(lo, tq), :]
            qi = qi * ATT_SCALE
            qt_i = qi.T
            doi = do_ref[pl.ds(lo, tq), :]
            dot_i = doi.astype(bf16).T
            prod = doi.astype(bf16).astype(f32) * o_ref[pl.ds(lo, tq), :]
            lse_i = lse_ref[pl.ds(lo, tq), :]
            dq_i = jnp.zeros((tq, LANES), f32)
            for hh in range(2):
                qh = jnp.where(masks[hh], qi, jnp.zeros((), bf16))
                doh = jnp.where(masks[hh], doi, 0.0).astype(bf16)
                delta = jnp.sum(jnp.where(masks[hh], prod, 0.0), axis=-1, keepdims=True)
                sc = _dot_nt(qh, kj) + (_tile_base(cr_ref, hh, lo) - cr_ref[hh:hh + 1, pl.ds(lo_j, tq)])
                p = jnp.exp(sc - lse_i[:, hh * HEAD_DIM:hh * HEAD_DIM + 1])
                if diag:
                    p = jnp.where(col <= row, p, 0.0)
                dv_t = dv_t + _dot(jnp.where(sub_masks[hh], dot_i, jnp.zeros((), bf16)), p.astype(bf16))
                ds = p * (_dot_nt(doh, vj) - delta)
                dcs[hh] = dcs[hh] + jnp.sum(ds, axis=0, keepdims=True)
                dsb = ds.astype(bf16)
                dq_i = jnp.where(masks[hh], _dot(dsb, kj), dq_i)
                dk_t = dk_t + _dot(jnp.where(sub_masks[hh], qt_i, jnp.zeros((), bf16)), dsb)
            dq_ref[pl.ds(lo, tq), :] += dq_i * ATT_SCALE
            return dk_t, dv_t, dcs[0], dcs[1]

        zero = jnp.zeros((LANES, tq), f32)
        zrow = jnp.zeros((1, tq), f32)
        carry = block(j, (zero, zero, zrow, zrow), True)
        dk_t, dv_t, dc0, dc1 = lax.fori_loop(j + 1, nq, functools.partial(block, diag=False), carry)
        dk_ref[...] = dk_t.T.astype(bf16)
        dv_ref[...] = dv_t.T.astype(bf16)
        dc_ref[0:1, :] = -dc0
        dc_ref[1:2, :] = -dc1

    full = lambda p, j: (0, p)
    tile = lambda p, j: (j, p)
    return pl.pallas_call(
        body, grid=(FOX_PAIRS, nq),
        in_specs=[pl.BlockSpec((s, LANES), full), pl.BlockSpec((tq, LANES), tile), pl.BlockSpec((tq, LANES), lambda p, j: (j, FOX_PAIRS + p)),
                  pl.BlockSpec((s, LANES), full), pl.BlockSpec((s, LANES), full), pl.BlockSpec((None, s, LANES), lambda p, j: (p, 0, 0)),
                  pl.BlockSpec((None, 2, s), lambda p, j: (p, 0, 0))],
        out_specs=[pl.BlockSpec((s, LANES), full), pl.BlockSpec((tq, LANES), tile), pl.BlockSpec((tq, LANES), tile),
                   pl.BlockSpec((None, 2, tq), lambda p, j: (p, 0, j))],
        out_shape=[SDS((s, MAIN_WIDTH), f32), SDS((s, MAIN_WIDTH), bf16), SDS((s, MAIN_WIDTH), bf16), SDS((FOX_PAIRS, 2, s), f32)],
        name=name, compiler_params=_cparams("parallel", "arbitrary"))(q, kv, kv, d_mixed, o, lse, c_row)


def adamw(w, g, m, v, name, tr=256):
    r, c = w.shape
    tr = min(tr, r)
    assert r % tr == 0, (name, r, tr)
    c1 = 1.0 / (1.0 - ADAM_B1 ** ADAM_STEP)
    c2 = 1.0 / (1.0 - ADAM_B2 ** ADAM_STEP)

    def body(w_ref, g_ref, m_ref, v_ref, d_ref, mo_ref, vo_ref):
        gv = g_ref[...]
        mn = ADAM_B1 * m_ref[...] + (1.0 - ADAM_B1) * gv
        vn = ADAM_B2 * v_ref[...] + (1.0 - ADAM_B2) * gv * gv
        mo_ref[...] = mn
        vo_ref[...] = vn
        d_ref[...] = -ADAM_LR * ((mn * c1) / (jnp.sqrt(vn * c2) + ADAM_EPS) + ADAM_WD * w_ref[...])

    spec = pl.BlockSpec((tr, c), lambda i: (i, 0))
    return pl.pallas_call(body, grid=(r // tr,), in_specs=[spec] * 4, out_specs=[spec] * 3, out_shape=[SDS((r, c), f32)] * 3,
                          name=name, compiler_params=_cparams("parallel"))(w, g, m, v)


def sum_leading(x, name, out_dtype=f32, tr=None):
    n, r, c = x.shape
    tr = tr or r
    assert r % tr == 0

    def body(x_ref, o_ref):
        acc = x_ref[0].astype(f32)
        for k in range(1, n):
            acc = acc + x_ref[k].astype(f32)
        o_ref[...] = acc.astype(out_dtype)

    return pl.pallas_call(body, grid=(r // tr,), in_specs=[pl.BlockSpec((n, tr, c), lambda i: (0, i, 0))],
                          out_specs=pl.BlockSpec((tr, c), lambda i: (i, 0)), out_shape=SDS((r, c), out_dtype), name=name,
                          compiler_params=_cparams("parallel"))(x)


_ANY = pl.BlockSpec(memory_space=pl.ANY)
_DMA = pltpu.SemaphoreType.DMA


_HBM = pl.BlockSpec(memory_space=pltpu.HBM)
_SEM = pl.BlockSpec(memory_space=pltpu.SEMAPHORE)
_EFFECT = pltpu.SideEffectType.DATAFLOW_SIDE_EFFECTING
_FLIPS = [(0, 0, 1), (1, 0, 0), (0, 1, 0), (1, 1, 0), (1, 0, 1), (0, 1, 1), (1, 1, 1)]


def _me():
    return lax.axis_index("x"), lax.axis_index("y"), lax.axis_index("c")


def _peers():
    mx, my, mc = _me()
    return [(jnp.bitwise_xor(mx, fx), jnp.bitwise_xor(my, fy), jnp.bitwise_xor(mc, fc)) for fx, fy, fc in _FLIPS]


def _index(dev):
    return 4 * dev[0] + 2 * dev[1] + dev[2]


def _win(ref, axis, k, size, count=1):
    idx = [slice(None)] * len(ref.shape)
    idx[axis] = pl.ds(k * size, count * size)
    return ref.at[tuple(idx)]


def _hbm(a):
    return pltpu.with_memory_space_constraint(a, pltpu.HBM)


def _exchange_start(srcs, lands, copies_of, name):
    n = len(srcs)

    def body(*refs):
        src = refs[:n]
        send_sems, recv_sems, self_sems = refs[2 * n:2 * n + 3]
        land = refs[3 * n + 3:4 * n + 3]
        token = refs[4 * n + 3]
        me = _index(_me())
        for a in range(n):
            for s_ref, d_ref, peer in copies_of(a, src[a], land[a], me):
                if peer is None:
                    pltpu.make_async_copy(s_ref, d_ref, self_sems.at[a]).start()
                else:
                    pltpu.make_async_remote_copy(src_ref=s_ref, dst_ref=d_ref, send_sem=send_sems.at[a], recv_sem=recv_sems.at[a],
                                                 device_id=peer, device_id_type=MESH).start()
        token[...] = jnp.zeros_like(token)

    outs = pl.pallas_call(
        body, name=name,
        out_shape=(_DMA((n,)), _DMA((n,)), _DMA((n,)), *[pltpu.HBM(s.shape, s.dtype) for s in srcs],
                   *[pltpu.HBM(l.shape, l.dtype) for l in lands], SDS((8, LANES), f32)),
        in_specs=[_HBM] * (2 * n), out_specs=(_SEM, _SEM, _SEM, *[_HBM] * (2 * n), pl.BlockSpec(memory_space=pltpu.VMEM)),
        input_output_aliases={i: 3 + i for i in range(2 * n)},
        compiler_params=pltpu.CompilerParams(has_side_effects=_EFFECT),
    )(*[_hbm(s) for s in srcs], *[_hbm(lax.empty(l.shape, l.dtype)) for l in lands])
    return dict(sems=outs[:3], srcs=list(outs[3:3 + n]), lands=list(outs[3 + n:3 + 2 * n]), token=outs[3 + 2 * n])


def _exchange_wait(started, waits_of, after, name, which=None):
    which = list(range(len(started["srcs"]))) if which is None else which
    srcs, lands = [started["srcs"][a] for a in which], [started["lands"][a] for a in which]
    n = len(which)

    def body(*refs):
        src = refs[:n]
        land = refs[n:2 * n]
        send_sems, recv_sems, self_sems = refs[2 * n:2 * n + 3]
        me = _index(_me())
        for pos, a in enumerate(which):
            seven, (s_ref, d_ref) = waits_of(a, src[pos], land[pos], me)
            both = pltpu.make_async_remote_copy(src_ref=seven, dst_ref=seven, send_sem=send_sems.at[a], recv_sem=recv_sems.at[a],
                                                device_id=_me(), device_id_type=MESH)
            both.wait_send()
            both.wait_recv()
            pltpu.make_async_copy(s_ref, d_ref, self_sems.at[a]).wait()

    outs = pl.pallas_call(
        body, name=name, out_shape=tuple(pltpu.HBM(t.shape, t.dtype) for t in srcs + lands),
        in_specs=[_HBM] * (2 * n) + [_SEM] * 3 + [_ANY], out_specs=tuple([_HBM] * (2 * n)),
        input_output_aliases={i: i for i in range(2 * n)},
        compiler_params=pltpu.CompilerParams(has_side_effects=_EFFECT),
    )(*srcs, *lands, *started["sems"], after)
    return list(outs[n:])


def gather_start(locs, axes, name):
    lands = [SDS(tuple(N_DEV * d if i == ax else d for i, d in enumerate(l.shape)), l.dtype) for l, ax in zip(locs, axes)]

    def copies_of(a, src, land, me):
        mine = _win(land, axes[a], me, src.shape[axes[a]])
        return [(src, mine, peer) for peer in _peers()] + [(src, mine, None)]

    return _exchange_start(locs, lands, copies_of, name)


def gather_wait(started, axes, after, name, which=None):
    def waits_of(a, src, land, me):
        size = src.shape[axes[a]]
        return _win(land, axes[a], 0, size, N_DEV - 1), (src, _win(land, axes[a], me, size))

    return _exchange_wait(started, waits_of, after, name, which)


def scatter_start(grads, axes, name):
    lands = [SDS((N_DEV,) + tuple(d // N_DEV if i == ax else d for i, d in enumerate(g.shape)), g.dtype) for g, ax in zip(grads, axes)]

    def copies_of(a, src, land, me):
        size = src.shape[axes[a]] // N_DEV
        out = [(_win(src, axes[a], _index(peer), size), land.at[me], peer) for peer in _peers()]
        return out + [(_win(src, axes[a], me, size), land.at[me], None)]

    return _exchange_start(grads, lands, copies_of, name)


def scatter_wait(started, axes, after, name):
    def waits_of(a, src, land, me):
        size = src.shape[axes[a]] // N_DEV
        return land.at[pl.ds(0, N_DEV - 1)], (_win(src, axes[a], me, size), land.at[me])

    return _exchange_wait(started, waits_of, after, name)


def _row_tile(rows, cap=512):
    return max(t for t in range(8, min(rows, cap) + 1, 8) if rows % t == 0)


_SMALL = [
    ("ln_mix_pre", (2, 1024)), ("ln_mix_post", (2, 1024)), ("ln_ffn_pre", (2, 1024)), ("ln_ffn_post", (2, 1024)),
    ("ln_mem", (2, 1024)), ("w_spatial", (1, 6, 128, 128)), ("b_spatial", (1, 6, 128)), ("ln_shared", (1024,)),
    ("b_forget", (12,)), ("ln_v_g", (1, 768)), ("ln_v_b", (1, 768)),
]
_SMALL_TILE = 8 * LANES


def _small_rows(shape):
    return -(-math.prod(shape) // _SMALL_TILE) * 8


def _pack_small(vals, shapes):
    parts = []
    for name, shape in shapes:
        flat = vals[name].reshape(-1).astype(f32)
        rows = _small_rows(shape)
        parts.append(jnp.pad(flat, (0, rows * LANES - flat.shape[0])).reshape(rows, LANES))
    return jnp.concatenate(parts, axis=0)


def _unpack_small(buf, shapes):
    out = {}
    lo = 0
    for name, shape in shapes:
        rows = _small_rows(shape)
        out[name] = buf[lo:lo + rows].reshape(-1)[:math.prod(shape)].reshape(shape)
        lo += rows
    return out


def kernel(x, mem, ln_mix_pre, ln_mix_post, ln_ffn_pre, ln_ffn_post, ln_mem, w_mem_kv, w_out, w_ffn_gate, w_ffn_up, w_ffn_down, w_in_a, w_spatial, b_spatial, ln_v_g, ln_v_b, ln_shared, w_shared_kv, b_forget, w_in_b, loss_target, m_ln_mix_pre, m_ln_mix_post, m_ln_ffn_pre, m_ln_ffn_post, m_ln_mem, m_w_mem_kv, m_w_out, m_w_ffn_gate, m_w_ffn_up, m_w_ffn_down, m_w_in_a, m_w_spatial, m_b_spatial, m_ln_v_g, m_ln_v_b, m_ln_shared, m_w_shared_kv, m_b_forget, m_w_in_b, v_ln_mix_pre, v_ln_mix_post, v_ln_ffn_pre, v_ln_ffn_post, v_ln_mem, v_w_mem_kv, v_w_out, v_w_ffn_gate, v_w_ffn_up, v_w_ffn_down, v_w_in_a, v_w_spatial, v_b_spatial, v_ln_v_g, v_ln_v_b, v_ln_shared, v_w_shared_kv, v_b_forget, v_w_in_b):
    weights = dict(ln_mix_pre=ln_mix_pre, ln_mix_post=ln_mix_post, ln_ffn_pre=ln_ffn_pre, ln_ffn_post=ln_ffn_post, ln_mem=ln_mem,
                   w_mem_kv=w_mem_kv, w_out=w_out, w_ffn_gate=w_ffn_gate, w_ffn_up=w_ffn_up, w_ffn_down=w_ffn_down, w_in_a=w_in_a,
                   w_spatial=w_spatial, b_spatial=b_spatial, ln_v_g=ln_v_g, ln_v_b=ln_v_b, ln_shared=ln_shared,
                   w_shared_kv=w_shared_kv, b_forget=b_forget, w_in_b=w_in_b)
    mom_m = dict(ln_mix_pre=m_ln_mix_pre, ln_mix_post=m_ln_mix_post, ln_ffn_pre=m_ln_ffn_pre, ln_ffn_post=m_ln_ffn_post, ln_mem=m_ln_mem,
                 w_mem_kv=m_w_mem_kv, w_out=m_w_out, w_ffn_gate=m_w_ffn_gate, w_ffn_up=m_w_ffn_up, w_ffn_down=m_w_ffn_down, w_in_a=m_w_in_a,
                 w_spatial=m_w_spatial, b_spatial=m_b_spatial, ln_v_g=m_ln_v_g, ln_v_b=m_ln_v_b, ln_shared=m_ln_shared,
                 w_shared_kv=m_w_shared_kv, b_forget=m_b_forget, w_in_b=m_w_in_b)
    mom_v = dict(ln_mix_pre=v_ln_mix_pre, ln_mix_post=v_ln_mix_post, ln_ffn_pre=v_ln_ffn_pre, ln_ffn_post=v_ln_ffn_post, ln_mem=v_ln_mem,
                 w_mem_kv=v_w_mem_kv, w_out=v_w_out, w_ffn_gate=v_w_ffn_gate, w_ffn_up=v_w_ffn_up, w_ffn_down=v_w_ffn_down, w_in_a=v_w_in_a,
                 w_spatial=v_w_spatial, b_spatial=v_b_spatial, ln_v_g=v_ln_v_g, ln_v_b=v_ln_v_b, ln_shared=v_ln_shared,
                 w_shared_kv=v_w_shared_kv, b_forget=v_b_forget, w_in_b=v_w_in_b)
    names = list(weights)
    mx, my, mc = lax.axis_index("x"), lax.axis_index("y"), lax.axis_index("c")
    me = 4 * mx + 2 * my + mc

    h0 = x[0]
    mem0 = mem[0]
    tgt = loss_target[0]
    seq = h0.shape[0]

    vec = lambda a: a.reshape(1, -1)
    pad_to = lambda a, axis, size: jnp.pad(a, [(0, size - a.shape[i] if i == axis else 0) for i in range(a.ndim)])

    def after(tok, a):
        return a + tok[0, 0].astype(a.dtype)

    lnv_loc = pad_to(jnp.concatenate([ln_v_g, ln_v_b], axis=0), 0, 8)
    layer_locs = lambda l: [w_mem_kv[l].astype(bf16), w_out[l].astype(bf16), pad_to(w_ffn_gate[l].astype(bf16), 1, FF_SHARD_PAD),
                            pad_to(w_ffn_up[l].astype(bf16), 1, FF_SHARD_PAD), pad_to(w_ffn_down[l].astype(bf16), 0, FF_SHARD_PAD)]
    w_locs = ([w_in_a.astype(bf16), pad_to(lnv_loc, 1, LANES)[None]] + layer_locs(0)
              + [w_in_b[0].astype(bf16), pad_to(w_shared_kv.astype(bf16), 1, KV_PAD)] + layer_locs(1))
    w_axes = [0, 0] + [0, 0, 1, 1, 0] + [0, 0] + [0, 0, 1, 1, 0]
    st_w = gather_start(w_locs, w_axes, "gather_weights_start")
    weights_wait = lambda which, wait_after, name: gather_wait(st_w, w_axes, wait_after, name, which)
    ws = w_spatial[0].astype(bf16)
    ws_t = ws.transpose(0, 2, 1)
    bs_t = b_spatial[0].T

    (a0,) = rms_fwd(h0, [after(st_w["token"], vec(ln_mix_pre[0]))], "a0_norm")
    w_in_a8, lnv8 = weights_wait([0, 1], a0, "gather_a_wait")
    w_in_a_full = w_in_a8.transpose(1, 0, 2).reshape(D_MODEL, -1)
    lnv_g = lnv8[:, 0, :MAIN_WIDTH // N_DEV].reshape(1, MAIN_WIDTH)
    lnv_b = lnv8[:, 1, :MAIN_WIDTH // N_DEV].reshape(1, MAIN_WIDTH)
    proj0 = mm(a0, w_in_a_full, "proj0", tn=896)
    main0 = gmlp_fwd(proj0, ws, bs_t, lnv_g, lnv_b, "gmlp_fwd")
    w_mkv, w_o = [None, None], [None, None]
    w_mkv[0], w_o[0] = weights_wait([2, 3], main0, "gather_b0_wait")
    (memn0,) = rms_fwd(mem0, [vec(ln_mem[0])], "mem0_norm")
    kvm0 = mm(memn0, w_mkv[0], "kvm0")
    om0 = mem_attn_fwd(proj0, 2 * MAIN_WIDTH // MEM_WIDTH, kvm0, "mem_attn0")
    mixed0 = jnp.concatenate([main0, om0], axis=-1)
    y1_0, hmid0, f0 = mm_resnorm(mixed0, w_o[0], h0, vec(ln_mix_post[0]), [vec(ln_ffn_pre[0])], "mix_out0")
    w_g0, w_u0 = weights_wait([4, 5], f0, "gather_gate_up0_wait")
    gu0, act0 = ffn_up(f0, w_g0, w_u0, "ffn_up0")
    (w_d0,) = weights_wait([6], act0, "gather_down0_wait")
    y2_0, h1, a1, sin1 = mm_resnorm(act0, w_d0, hmid0, vec(ln_ffn_post[0]), [vec(ln_mix_pre[1]), vec(ln_shared)], "ffn_down0")

    w_inb, w_kv = weights_wait([7, 8], sin1, "gather_d_wait")
    kvb = mm(sin1, w_kv, "kv_shared", out_dtype=bf16, tn=MAIN_WIDTH, ncols=2 * MAIN_WIDTH)
    zf = mm(sin1, w_kv, "forget_logits", tn=256, col0=2 * MAIN_WIDTH, ncols=256)
    qb = mm(a1, w_inb, "proj1", out_dtype=bf16)
    z_t = jnp.pad(zf[:, :FOX_HEADS].T, ((0, 16 - FOX_HEADS), (0, 0)))
    bf_col = jnp.pad(b_forget, (0, 16 - FOX_HEADS)).reshape(16, 1)
    c_t = fgate_fwd(z_t, bf_col, "fgate_fwd")
    c_row = c_t[:FOX_HEADS].reshape(FOX_PAIRS, 2, seq)
    main1, lse, main1_b = fox_fwd(qb, kvb, c_row, "fox_fwd")
    w_mkv[1], w_o[1] = weights_wait([9, 10], main1, "gather_b1_wait")
    (memn1,) = rms_fwd(mem0, [vec(ln_mem[1])], "mem1_norm")
    kvm1 = mm(memn1, w_mkv[1], "kvm1")
    om1 = mem_attn_fwd(qb, MAIN_WIDTH // MEM_WIDTH, kvm1, "mem_attn1")
    mixed1 = jnp.concatenate([main1_b, om1], axis=-1)
    y1_1, hmid1, f1 = mm_resnorm(mixed1, w_o[1], h1, vec(ln_mix_post[1]), [vec(ln_ffn_pre[1])], "mix_out1")
    w_g1, w_u1 = weights_wait([11, 12], f1, "gather_gate_up1_wait")
    gu1, act1 = ffn_up(f1, w_g1, w_u1, "ffn_up1")
    (w_d1,) = weights_wait([13], act1, "gather_down1_wait")
    dh, d_y2_1, dg_fpost1, loss_tile = mm_resnorm_loss(act1, w_d1, hmid1, vec(ln_ffn_post[1]), tgt, "ffn_down1_loss")
    ffn_w = [(w_g0, w_u0, w_d0), (w_g1, w_u1, w_d1)]

    small = {}

    def ffn_backward(layer, dh_out, d_y2, hmid, f, gu, act, y1):
        w_g, w_u, w_d = ffn_w[layer]
        dw_down = mm_tn(act, d_y2, f"dw_down{layer}")
        rs_down = scatter_start([dw_down], [0], f"scatter_down{layer}_start")
        d_g, d_u = ffn_act_grad(d_y2, w_d, gu, f"ffn_act_grad{layer}")
        dw_g = mm_tn(f, d_g, f"dw_gate{layer}", dep=rs_down["token"])
        dw_u = mm_tn(f, d_u, f"dw_up{layer}")
        rs_gate_up = scatter_start([dw_g, dw_u], [1, 1], f"scatter_gate_up{layer}_start")
        dh_mid, d_y1, dg_fpre, dg_mpost = ffn_in_grad(d_g, d_u, w_g, w_u, hmid, dh_out, after(rs_gate_up["token"], vec(ln_ffn_pre[layer])),
                                                      y1, vec(ln_mix_post[layer]), f"ffn_in_grad{layer}")
        return dh_mid, d_y1, dg_fpre, dg_mpost, (rs_down, rs_gate_up)

    def mix_out_backward(layer, d_y1, mixed):
        dw_out = mm_tn(mixed, d_y1, f"dw_out{layer}")
        d_mixed = mm(d_y1, w_o[layer], f"d_mixed{layer}", trans_b=True)
        return d_mixed, dw_out

    def mem_backward(layer, q_src, q_block, kvm, memn, d_mixed):
        d_qm, d_kvm = mem_attn_bwd(q_src, q_block, kvm, d_mixed, f"mem_attn_bwd{layer}")
        d_kvm_b = d_kvm.astype(bf16)
        dw_mkv = mm_tn(memn, d_kvm_b, f"dw_mem_kv{layer}")
        d_memn = mm(d_kvm_b, w_mkv[layer], f"d_memn{layer}", trans_b=True)
        _, dg_mem = rms_bwd(mem0, vec(ln_mem[layer]), d_memn, None, bf16, f"mem_norm_bwd{layer}")
        return d_qm, dw_mkv, dg_mem


    dh_mid1, d_y1_1, dg_fpre1, dg_mpost1, rs_ffn1 = ffn_backward(1, dh, d_y2_1, hmid1, f1, gu1, act1, y1_1)
    d_mixed1, dw_out1 = mix_out_backward(1, d_y1_1, mixed1)
    d_qm1, dw_mkv1, dg_mem1 = mem_backward(1, qb, MAIN_WIDTH // MEM_WIDTH, kvm1, memn1, d_mixed1)
    rs_mix1 = scatter_start([dw_out1, dw_mkv1], [0, 0], "scatter_mix1_start")
    dq, dk, dv, dc = fox_bwd(qb, kvb, d_mixed1, main1, lse, after(rs_mix1["token"], c_row), "fox_bwd")
    dc_t = jnp.pad(dc.reshape(FOX_HEADS, seq), ((0, 16 - FOX_HEADS), (0, 0)))
    dz_t, db_f = fgate_bwd(dc_t, z_t, bf_col, "fgate_bwd")
    d_kvf = jnp.concatenate([dk, dv, jnp.pad(dz_t[:FOX_HEADS].T.astype(bf16), ((0, 0), (0, KV_PAD - KV_WIDTH)))], axis=-1)
    d_proj1 = jnp.concatenate([dq.astype(bf16), d_qm1], axis=-1)
    dw_in_b = mm_tn(a1, d_proj1, "dw_in_b")
    dw_kv = mm_tn(sin1, d_kvf, "dw_kv", tn=896)
    rs_2 = scatter_start([dw_in_b, dw_kv], [0, 0], "scatter_shared_start")
    dh1, (dg_pre1, dg_shared), d_y2_0, dg_fpost0 = proj_in_grad(
        [(d_proj1, w_inb, vec(ln_mix_pre[1])), (d_kvf, w_kv, vec(ln_shared))], h1, dh_mid1, "in_grad1", dep=rs_2["token"],
        below=(y2_0, vec(ln_ffn_post[0])))

    dh_mid0, d_y1_0, dg_fpre0, dg_mpost0, rs_ffn0 = ffn_backward(0, dh1, d_y2_0, hmid0, f0, gu0, act0, y1_0)
    d_mixed0, dw_out0 = mix_out_backward(0, d_y1_0, mixed0)
    d_qm0, dw_mkv0, dg_mem0 = mem_backward(0, proj0, 2 * MAIN_WIDTH // MEM_WIDTH, kvm0, memn0, d_mixed0)
    rs_mix0 = scatter_start([dw_out0, dw_mkv0], [0, 0], "scatter_mix0_start")
    d_uv, dw_s, db_s, dg_lnv, db_lnv = gmlp_bwd(proj0, d_mixed0, ws, ws_t, bs_t, after(rs_mix0["token"], lnv_g), lnv_b, "gmlp_bwd")

    small["ln_mix_pre"] = jnp.concatenate([jnp.zeros_like(dg_pre1), dg_pre1], axis=0)
    small["ln_mix_post"] = jnp.concatenate([dg_mpost0, dg_mpost1], axis=0)
    small["ln_ffn_pre"] = jnp.concatenate([dg_fpre0, dg_fpre1], axis=0)
    small["ln_ffn_post"] = jnp.concatenate([dg_fpost0, dg_fpost1], axis=0)
    small["ln_mem"] = jnp.concatenate([dg_mem0, dg_mem1], axis=0)
    small["w_spatial"] = dw_s[None]
    small["b_spatial"] = db_s[:, :A_GROUPS].T[None]
    small["ln_shared"] = dg_shared[0]
    small["b_forget"] = db_f[:FOX_HEADS, 0]
    small["ln_v_g"] = dg_lnv
    small["ln_v_b"] = db_lnv
    small_rows = jnp.concatenate([_pack_small(small, _SMALL), loss_tile], axis=0)
    st_small = gather_start([small_rows[None]], [0], "gather_small_grads_start")
    d_proj0 = jnp.concatenate([d_uv, after(st_small["token"], d_qm0)], axis=-1)
    dw_in_a = mm_tn(a0, d_proj0, "dw_in_a", tn=896)
    rs_in_a = scatter_start([dw_in_a.reshape(D_MODEL, N_DEV, -1).transpose(1, 0, 2)], [0], "scatter_in_a_start")
    grad_x, (dg_pre0,) = proj_in_grad([(d_proj0, w_in_a_full, vec(ln_mix_pre[0]))], h0, dh_mid0, "in_grad0", dep=rs_in_a["token"])
    st_last = gather_start([dg_pre0.reshape(1, 8, LANES)], [0], "gather_last_grad_start")

    def owned(started, axes, wait_after, name):
        recv = scatter_wait(started, axes, wait_after, name)
        return [sum_leading(r.reshape((N_DEV, -1, r.shape[-1])), f"{name}_sum{i}", tr=_row_tile(math.prod(r.shape[1:-1])))
                for i, r in enumerate(recv)]

    (g_down1,) = owned(rs_ffn1[0], [0], after(st_last["token"], grad_x[:8, :LANES]), "scatter_down1_wait")
    g_gu1 = owned(rs_ffn1[1], [1, 1], g_down1, "scatter_gate_up1_wait")
    g_mix1 = owned(rs_mix1, [0, 0], g_gu1[0], "scatter_mix1_wait")
    g2 = owned(rs_2, [0, 0], g_mix1[0], "scatter_shared_wait")
    (g_down0,) = owned(rs_ffn0[0], [0], g2[0], "scatter_down0_wait")
    g_gu0 = owned(rs_ffn0[1], [1, 1], g_down0, "scatter_gate_up0_wait")
    g_mix0 = owned(rs_mix0, [0, 0], g_gu0[0], "scatter_mix0_wait")
    (g_in_a,) = owned(rs_in_a, [0], g_mix0[0], "scatter_in_a_wait")
    g_local = dict(
        w_ffn_gate=jnp.stack([g_gu0[0], g_gu1[0]])[:, :, :FF_SHARD], w_ffn_up=jnp.stack([g_gu0[1], g_gu1[1]])[:, :, :FF_SHARD],
        w_ffn_down=jnp.stack([g_down0, g_down1])[:, :FF_SHARD], w_out=jnp.stack([g_mix0[0], g_mix1[0]]),
        w_mem_kv=jnp.stack([g_mix0[1], g_mix1[1]]), w_in_b=g2[0][None], w_shared_kv=g2[1][:, :KV_WIDTH], w_in_a=g_in_a[None])
    (small_all,) = gather_wait(st_small, [0], g_in_a, "gather_small_grads_wait")
    (last_all,) = gather_wait(st_last, [0], small_all, "gather_last_grad_wait")
    small_sum = sum_leading(small_all, "sum_small_grads")
    loss = small_sum[small_rows.shape[0] - 1, 0]
    g_small = _unpack_small(small_sum, _SMALL)
    g_small["ln_mix_pre"] = jnp.concatenate([sum_leading(last_all, "sum_last_grad").reshape(1, D_MODEL), g_small["ln_mix_pre"][1:]], axis=0)
    shard = MAIN_WIDTH // N_DEV
    for n in ("ln_v_g", "ln_v_b"):
        g_small[n] = lax.dynamic_slice_in_dim(g_small[n], me * shard, shard, axis=1)
    grad_w = {**g_small, **g_local}

    delta, new_m, new_v = {}, {}, {}
    for n in g_local:
        two_d = (-1, weights[n].shape[-1])
        d_, m_, v_ = adamw(weights[n].reshape(two_d), grad_w[n].reshape(two_d), mom_m[n].reshape(two_d), mom_v[n].reshape(two_d),
                           f"adamw_{n}", tr=_row_tile(math.prod(weights[n].shape[:-1])))
        delta[n], new_m[n], new_v[n] = (t.reshape(weights[n].shape) for t in (d_, m_, v_))
    small_local_shapes = [(n, tuple(weights[n].shape)) for n, _ in _SMALL]
    packed = [_pack_small(src, small_local_shapes) for src in (weights, grad_w, mom_m, mom_v)]
    outs = adamw(*packed, "adamw_small", tr=packed[0].shape[0])
    for dst, buf in zip((delta, new_m, new_v), outs):
        dst.update(_unpack_small(buf, small_local_shapes))

    return (loss, grad_x[None], *[grad_w[n] for n in names], *[delta[n] for n in names],
            *[new_m[n] for n in names], *[new_v[n] for n in names])
```

```python
import functools
import math

import jax
import jax.numpy as jnp
from jax import lax
from jax.experimental import pallas as pl
from jax.experimental.pallas import tpu as pltpu

f32 = jnp.float32
bf16 = jnp.bfloat16
SDS = jax.ShapeDtypeStruct

D_MODEL = 1024
MAIN_WIDTH = 768
MEM_WIDTH = 256
HEAD_DIM = 64
MEM_HEADS = 4
FOX_HEADS = 12
FOX_PAIRS = FOX_HEADS // 2
CHUNK = 128
A_GROUPS = 6
FF_SHARD = 352
FF_SHARD_PAD = 384
FF_PAD = 8 * FF_SHARD_PAD
KV_WIDTH = 2 * MAIN_WIDTH + FOX_HEADS
KV_PAD = 1792
RMS_EPS = 1e-6
LN_EPS = 1e-5
ATT_SCALE = HEAD_DIM ** -0.5
ADAM_LR, ADAM_B1, ADAM_B2, ADAM_EPS, ADAM_WD, ADAM_STEP = 0.001, 0.9, 0.999, 1e-08, 0.01, 10
N_DEV = 8
AXES = ("x", "y", "c")
MESH = pl.DeviceIdType.MESH
V7X_VMEM_LIMIT = 56 * 1024 * 1024
LANES = 128
FLAT_W = 512
ROW_PAD = 16


def _cparams(*sem):
    return pltpu.CompilerParams(dimension_semantics=sem or None, vmem_limit_bytes=V7X_VMEM_LIMIT)


def _dot(a, b):
    return jnp.dot(a, b, preferred_element_type=f32)


def _dot_nt(a, b):
    return lax.dot_general(a, b, (((1,), (1,)), ((), ())), preferred_element_type=f32)


def _dot_tn(a, b):
    return lax.dot_general(a, b, (((0,), (0,)), ((), ())), preferred_element_type=f32)


def _gelu(x):
    k = math.sqrt(2.0 / math.pi)
    t = jnp.tanh(k * (x + 0.044715 * x * x * x))
    return 0.5 * x * (1.0 + t), t


def _gelu_grad(x, t):
    k = math.sqrt(2.0 / math.pi)
    return 0.5 * (1.0 + t) + 0.5 * x * (1.0 - t * t) * k * (1.0 + 3.0 * 0.044715 * x * x)


def _sigmoid(x):
    return 1.0 / (1.0 + jnp.exp(-x))


def rms_fwd(x, gains, name, tm=512):
    m, d = x.shape
    tm = min(tm, m)
    n = len(gains)

    def body(x_ref, *refs):
        xv = x_ref[...]
        y = xv * lax.rsqrt(jnp.sum(xv * xv, axis=-1, keepdims=True) * (1.0 / d) + RMS_EPS)
        for g_ref, o_ref in zip(refs[:n], refs[n:]):
            o_ref[...] = (y * g_ref[...]).astype(bf16)

    row = pl.BlockSpec((tm, d), lambda i: (i, 0))
    vec = pl.BlockSpec((1, d), lambda i: (0, 0))
    return pl.pallas_call(body, grid=(m // tm,), in_specs=[row] + [vec] * n, out_specs=[row] * n,
                          out_shape=[SDS((m, d), bf16)] * n, name=name, compiler_params=_cparams("parallel"))(x, *gains)


def rms_bwd(x, g, dy, add, out_dtype, name, tm=512):
    m, d = x.shape
    tm = min(tm, m)
    has_add = add is not None

    def body(x_ref, g_ref, dy_ref, *refs):
        dx_ref, dg_ref = refs[-2], refs[-1]
        xv = x_ref[...]
        dyv = dy_ref[...].astype(f32)
        r = lax.rsqrt(jnp.sum(xv * xv, axis=-1, keepdims=True) * (1.0 / d) + RMS_EPS)
        xn = xv * r
        dyg = dyv * g_ref[...]
        dx = r * (dyg - xn * (jnp.sum(dyg * xn, axis=-1, keepdims=True) * (1.0 / d)))
        if has_add:
            dx = dx + refs[0][...]
        dx_ref[...] = dx.astype(out_dtype)

        @pl.when(pl.program_id(0) == 0)
        def _():
            dg_ref[...] = jnp.zeros_like(dg_ref)

        dg_ref[...] += jnp.sum(dyv * xn, axis=0, keepdims=True)

    row = pl.BlockSpec((tm, d), lambda i: (i, 0))
    vec = pl.BlockSpec((1, d), lambda i: (0, 0))
    ins = [x, g, dy] + ([add] if has_add else [])
    return pl.pallas_call(body, grid=(m // tm,), in_specs=[row, vec, row] + ([row] if has_add else []),
                          out_specs=[row, vec], out_shape=[SDS((m, d), out_dtype), SDS((1, d), f32)], name=name,
                          compiler_params=_cparams("arbitrary"))(*ins)


def mm(a, b, name, trans_b=False, out_dtype=f32, tm=1024, tn=1024, layer=None, col0=0, ncols=None, dep=None):
    m, k = a.shape
    n_all = b.shape[-2] if trans_b else b.shape[-1]
    n = n_all if ncols is None else ncols
    tm, tn = min(tm, m), min(tn, n)
    assert m % tm == 0 and n % tn == 0 and col0 % tn == 0 and not (trans_b and col0), (name, m, n, tm, tn)
    jb = col0 // tn
    lead = () if layer is None else (None,)
    sel = () if layer is None else (layer,)

    def body(a_ref, b_ref, *rest):
        r = _dot_nt(a_ref[...], b_ref[...]) if trans_b else _dot(a_ref[...], b_ref[...])
        rest[-1][...] = r.astype(out_dtype)

    if trans_b:
        b_spec = pl.BlockSpec(lead + (tn, k), lambda j, i: sel + (j, 0))
    else:
        b_spec = pl.BlockSpec(lead + (k, tn), lambda j, i: sel + (0, jb + j))
    deps = [] if dep is None else [dep]
    dep_specs = [pl.BlockSpec((8, LANES), lambda j, i: (0, 0))] * len(deps)
    return pl.pallas_call(body, grid=(n // tn, m // tm), in_specs=[pl.BlockSpec((tm, k), lambda j, i: (i, 0)), b_spec] + dep_specs,
                          out_specs=pl.BlockSpec((tm, tn), lambda j, i: (i, j)), out_shape=SDS((m, n), out_dtype),
                          name=name, compiler_params=_cparams("parallel", "parallel"))(a, b, *deps)


def mm_tn(a, g, name, tk=1024, tn=1024, out_dtype=bf16, dep=None):
    s, k = a.shape
    n = g.shape[1]
    tk, tn = min(tk, k), min(tn, n)
    assert k % tk == 0 and n % tn == 0, (name, k, n, tk, tn)

    def body(a_ref, g_ref, *rest):
        rest[-1][...] = _dot_tn(a_ref[...], g_ref[...]).astype(out_dtype)

    deps = [] if dep is None else [dep]
    dep_specs = [pl.BlockSpec((8, LANES), lambda i, j: (0, 0))] * len(deps)
    return pl.pallas_call(body, grid=(k // tk, n // tn),
                          in_specs=[pl.BlockSpec((s, tk), lambda i, j: (0, i)), pl.BlockSpec((s, tn), lambda i, j: (0, j))] + dep_specs,
                          out_specs=pl.BlockSpec((tk, tn), lambda i, j: (i, j)), out_shape=SDS((k, n), out_dtype), name=name,
                          compiler_params=_cparams("parallel", "parallel"))(a, g, *deps)


def _resident(shape, index_map):
    return pl.BlockSpec(shape, index_map, pipeline_mode=pl.Buffered(1))


def _rms(xv):
    return xv * lax.rsqrt(jnp.sum(xv * xv, axis=-1, keepdims=True) * (1.0 / xv.shape[-1]) + RMS_EPS)


def _rms_bwd_math(xv, g, dy):
    d = xv.shape[-1]
    r = lax.rsqrt(jnp.sum(xv * xv, axis=-1, keepdims=True) * (1.0 / d) + RMS_EPS)
    xn = xv * r
    dyg = dy * g
    dx = r * (dyg - xn * (jnp.sum(dyg * xn, axis=-1, keepdims=True) * (1.0 / d)))
    return dx, jnp.sum(dy * xn, axis=0, keepdims=True)


SUB_ROWS = 512


def mm_resnorm(a, b, h, g_post, gains, name, tm=512):
    m, k = a.shape
    d = b.shape[1]
    n = len(gains)

    def body(a_ref, b_ref, h_ref, gp_ref, *refs):
        for r in range(tm // SUB_ROWS):
            rows = slice(r * SUB_ROWS, (r + 1) * SUB_ROWS)
            y = _dot(a_ref[rows, :], b_ref[...])
            refs[n][rows, :] = y
            hn = h_ref[rows, :] + _rms(y) * gp_ref[...]
            refs[n + 1][rows, :] = hn
            if n:
                z = _rms(hn)
                for g_ref, o_ref in zip(refs[:n], refs[n + 2:]):
                    o_ref[rows, :] = (z * g_ref[...]).astype(bf16)

    row = pl.BlockSpec((tm, d), lambda i: (i, 0))
    vec = pl.BlockSpec((1, d), lambda i: (0, 0))
    return pl.pallas_call(body, grid=(m // tm,),
                          in_specs=[pl.BlockSpec((tm, k), lambda i: (i, 0)), _resident((k, d), lambda i: (0, 0)), row, vec] + [vec] * n,
                          out_specs=[row] * (n + 2), out_shape=[SDS((m, d), f32)] * 2 + [SDS((m, d), bf16)] * n, name=name,
                          compiler_params=_cparams("parallel"))(a, b, h, g_post, *gains)


def mm_resnorm_loss(a, b, h, g_post, tgt, name, tm=512):
    m, k = a.shape
    d = b.shape[1]

    def body(a_ref, b_ref, h_ref, gp_ref, t_ref, dh_ref, dy_ref, dg_ref, l_ref):
        @pl.when(pl.program_id(0) == 0)
        def _():
            dg_ref[...] = jnp.zeros_like(dg_ref)
            l_ref[...] = jnp.zeros_like(l_ref)

        y = _dot(a_ref[...], b_ref[...])
        e = h_ref[...] + _rms(y) * gp_ref[...] - t_ref[...]
        dh = e * (1.0 / d)
        dh_ref[...] = dh
        part = jnp.sum(jnp.sum(e * e, axis=-1, keepdims=True), axis=0, keepdims=True) * (0.5 / d)
        l_ref[...] += jnp.broadcast_to(part, l_ref.shape)
        dy, dg = _rms_bwd_math(y, gp_ref[...], dh)
        dy_ref[...] = dy.astype(bf16)
        dg_ref[...] += dg

    row = pl.BlockSpec((tm, d), lambda i: (i, 0))
    vec = pl.BlockSpec((1, d), lambda i: (0, 0))
    return pl.pallas_call(body, grid=(m // tm,),
                          in_specs=[pl.BlockSpec((tm, k), lambda i: (i, 0)), _resident((k, d), lambda i: (0, 0)), row, vec, row],
                          out_specs=[row, row, vec, pl.BlockSpec((8, LANES), lambda i: (0, 0))],
                          out_shape=[SDS((m, d), f32), SDS((m, d), bf16), SDS((1, d), f32), SDS((8, LANES), f32)], name=name,
                          compiler_params=_cparams("arbitrary"))(a, b, h, g_post, tgt)


def ffn_act_grad(d_y2, w_d, factors, name, tm=1024, tn=1536):
    s, d = d_y2.shape
    ff = w_d.shape[0]
    nb = ff // tn

    def body(a_ref, b_ref, g_ref, u_ref, dg_ref, du_ref):
        av = a_ref[...]
        tc = 256
        for c in range(tn // tc):
            cols = slice(c * tc, (c + 1) * tc)
            da = _dot_nt(av, b_ref[cols, :])
            dg_ref[:, cols] = (da * g_ref[:, cols].astype(f32)).astype(bf16)
            du_ref[:, cols] = (da * u_ref[:, cols].astype(f32)).astype(bf16)

    tile = pl.BlockSpec((tm, tn), lambda j, i: (i, j))
    return pl.pallas_call(body, grid=(nb, s // tm),
                          in_specs=[pl.BlockSpec((tm, d), lambda j, i: (i, 0)), pl.BlockSpec((tn, d), lambda j, i: (j, 0)), tile,
                                    pl.BlockSpec((tm, tn), lambda j, i: (i, nb + j))],
                          out_specs=[tile, tile], out_shape=[SDS((s, ff), bf16)] * 2, name=name,
                          compiler_params=_cparams("parallel", "parallel"))(d_y2, w_d, factors, factors)


def ffn_in_grad(d_g, d_u, w_g, w_u, hmid, dh_out, g_pre, y1, g_post, name, tm=512):
    s, ff = d_g.shape
    d = w_g.shape[0]

    def body(dg_ref, du_ref, wg_ref, wu_ref, hm_ref, dho_ref, gpre_ref, y1_ref, gpost_ref, dhm_ref, dy1_ref, dgpre_ref, dgpost_ref):
        @pl.when(pl.program_id(0) == 0)
        def _():
            dgpre_ref[...] = jnp.zeros_like(dgpre_ref)
            dgpost_ref[...] = jnp.zeros_like(dgpost_ref)

        for r in range(tm // SUB_ROWS):
            rows = slice(r * SUB_ROWS, (r + 1) * SUB_ROWS)
            d_f = _dot_nt(dg_ref[rows, :], wg_ref[...]) + _dot_nt(du_ref[rows, :], wu_ref[...])
            dx, dg1 = _rms_bwd_math(hm_ref[rows, :], gpre_ref[...], d_f)
            dh_mid = dho_ref[rows, :] + dx
            dhm_ref[rows, :] = dh_mid
            dgpre_ref[...] += dg1
            dy1, dg2 = _rms_bwd_math(y1_ref[rows, :], gpost_ref[...], dh_mid)
            dy1_ref[rows, :] = dy1.astype(bf16)
            dgpost_ref[...] += dg2

    row = pl.BlockSpec((tm, d), lambda i: (i, 0))
    vec = pl.BlockSpec((1, d), lambda i: (0, 0))
    wide = pl.BlockSpec((tm, ff), lambda i: (i, 0))
    w_spec = _resident((d, ff), lambda i: (0, 0))
    return pl.pallas_call(body, grid=(s // tm,), in_specs=[wide, wide, w_spec, w_spec, row, row, vec, row, vec],
                          out_specs=[row, row, vec, vec], out_shape=[SDS((s, d), f32), SDS((s, d), bf16), SDS((1, d), f32), SDS((1, d), f32)],
                          name=name, compiler_params=_cparams("arbitrary"))(d_g, d_u, w_g, w_u, hmid, dh_out, g_pre, y1, g_post)


def proj_in_grad(pairs, x, add, name, tm=512, dep=None, below=None):
    s, d = x.shape
    n = len(pairs)
    extra = [] if dep is None else [dep]
    n_below = 0 if below is None else 2

    def body(*refs):
        x_ref, add_ref = refs[3 * n], refs[3 * n + 1]
        below_refs = refs[3 * n + 2:3 * n + 2 + n_below]
        outs = refs[3 * n + 2 + n_below + len(extra):]

        @pl.when(pl.program_id(0) == 0)
        def _():
            for o in outs[1:1 + n] + outs[2 + n:]:
                o[...] = jnp.zeros_like(o)

        xv = x_ref[...]
        dx = add_ref[...]
        for i in range(n):
            a_ref, b_ref, g_ref = refs[3 * i:3 * i + 3]
            dxi, dgi = _rms_bwd_math(xv, g_ref[...], _dot_nt(a_ref[...], b_ref[...]))
            dx = dx + dxi
            outs[1 + i][...] += dgi
        outs[0][...] = dx
        if below is not None:
            dy, dg = _rms_bwd_math(below_refs[0][...], below_refs[1][...], dx)
            outs[1 + n][...] = dy.astype(bf16)
            outs[2 + n][...] += dg

    row = pl.BlockSpec((tm, d), lambda i: (i, 0))
    vec = pl.BlockSpec((1, d), lambda i: (0, 0))
    in_specs, args = [], []
    for a, b, g in pairs:
        k = a.shape[1]
        in_specs += [pl.BlockSpec((tm, k), lambda i: (i, 0)), _resident((d, k), lambda i: (0, 0)), vec]
        args += [a, b, g]
    in_specs += [row, row] + [row, vec][:n_below] + [pl.BlockSpec((8, LANES), lambda i: (0, 0))] * len(extra)
    out_specs = [row] + [vec] * n + [row, vec][:n_below]
    out_shape = [SDS((s, d), f32)] + [SDS((1, d), f32)] * n + [SDS((s, d), bf16), SDS((1, d), f32)][:n_below]
    out = pl.pallas_call(body, grid=(s // tm,), in_specs=in_specs, out_specs=out_specs, out_shape=out_shape, name=name,
                         compiler_params=_cparams("arbitrary"))(*args, x, add, *(below or ()), *extra)
    return (out[0], out[1:1 + n]) + tuple(out[1 + n:])


def ffn_up(f, wg, wu, name, tm=512, tc=256):
    s, d = f.shape
    ff = wg.shape[-1]

    def body(f_ref, wg_ref, wu_ref, fac_ref, act_ref):
        fv = f_ref[...]
        for j in range(ff // tc):
            lo = j * tc
            gg = _dot(fv, wg_ref[:, lo:lo + tc])
            uu = _dot(fv, wu_ref[:, lo:lo + tc])
            sg = _sigmoid(gg)
            silu = gg * sg
            fac_ref[:, lo:lo + tc] = (uu * (sg + silu * (1.0 - sg))).astype(bf16)
            fac_ref[:, ff + lo:ff + lo + tc] = silu.astype(bf16)
            act_ref[:, lo:lo + tc] = (silu * uu).astype(bf16)

    w_spec = _resident((d, ff), lambda i: (0, 0))
    return pl.pallas_call(body, grid=(s // tm,), in_specs=[pl.BlockSpec((tm, d), lambda i: (i, 0)), w_spec, w_spec],
                          out_specs=[pl.BlockSpec((tm, 2 * ff), lambda i: (i, 0)), pl.BlockSpec((tm, ff), lambda i: (i, 0))],
                          out_shape=[SDS((s, 2 * ff), bf16), SDS((s, ff), bf16)], name=name,
                          compiler_params=_cparams("parallel"))(f, wg, wu)


def _gmlp_forward_chunk(u, v, w_refs, bias, ln_g, ln_b):
    gu, tu = _gelu(u)
    gv, tv = _gelu(v)
    mu = jnp.sum(gv, axis=-1, keepdims=True) * (1.0 / MAIN_WIDTH)
    xc = gv - mu
    rstd = lax.rsqrt(jnp.sum(xc * xc, axis=-1, keepdims=True) * (1.0 / MAIN_WIDTH) + LN_EPS)
    xhat = xc * rstd
    vln = xhat * ln_g + ln_b
    row = lax.broadcasted_iota(jnp.int32, (CHUNK, CHUNK), 0)
    col = lax.broadcasted_iota(jnp.int32, (CHUNK, CHUNK), 1)
    s_parts = []
    for g in range(A_GROUPS):
        w = jnp.where(col <= row, w_refs[g], jnp.zeros((), bf16))
        s_parts.append(_dot(w, vln[:, g * CHUNK:(g + 1) * CHUNK].astype(bf16)) + bias[:, g:g + 1])
    return gu, tu, tv, rstd, xhat, vln, s_parts


def gmlp_fwd(proj, ws, bs_t, ln_g, ln_b, name, tm=512):
    s = proj.shape[0]

    def body(u_ref, v_ref, w_ref, b_ref, g_ref, bb_ref, o_ref):
        bias = b_ref[...]
        for c in range(tm // CHUNK):
            rows = slice(c * CHUNK, (c + 1) * CHUNK)
            gu, _, _, _, _, _, s_parts = _gmlp_forward_chunk(u_ref[rows, :], v_ref[rows, :], w_ref, bias, g_ref[...], bb_ref[...])
            for g in range(A_GROUPS):
                cols = slice(g * CHUNK, (g + 1) * CHUNK)
                o_ref[rows, cols] = (gu[:, cols] * s_parts[g]).astype(bf16)

    vec = pl.BlockSpec((1, MAIN_WIDTH), lambda i: (0, 0))
    return pl.pallas_call(
        body, grid=(s // tm,),
        in_specs=[pl.BlockSpec((tm, MAIN_WIDTH), lambda i: (i, 0)), pl.BlockSpec((tm, MAIN_WIDTH), lambda i: (i, 1)),
                  pl.BlockSpec((A_GROUPS, CHUNK, CHUNK), lambda i: (0, 0, 0)), pl.BlockSpec((CHUNK, A_GROUPS), lambda i: (0, 0)), vec, vec],
        out_specs=pl.BlockSpec((tm, MAIN_WIDTH), lambda i: (i, 0)), out_shape=SDS((s, MAIN_WIDTH), bf16), name=name,
        compiler_params=_cparams("parallel"))(proj, proj, ws, bs_t, ln_g, ln_b)


def gmlp_bwd(proj, d_mixed, ws, ws_t, bs_t, ln_g, ln_b, name, tm=512):
    s = proj.shape[0]

    def body(u_ref, v_ref, dm_ref, w_ref, wt_ref, b_ref, g_ref, bb_ref, duv_ref, dw_ref, db_ref, dg_ref, dbb_ref):
        @pl.when(pl.program_id(0) == 0)
        def _():
            dw_ref[...] = jnp.zeros_like(dw_ref)
            db_ref[...] = jnp.zeros_like(db_ref)
            dg_ref[...] = jnp.zeros_like(dg_ref)
            dbb_ref[...] = jnp.zeros_like(dbb_ref)

        bias = b_ref[...]
        ln_gv = g_ref[...]
        row = lax.broadcasted_iota(jnp.int32, (CHUNK, CHUNK), 0)
        col = lax.broadcasted_iota(jnp.int32, (CHUNK, CHUNK), 1)
        lane = lax.broadcasted_iota(jnp.int32, (CHUNK, LANES), 1)
        for c in range(tm // CHUNK):
            rows = slice(c * CHUNK, (c + 1) * CHUNK)
            u = u_ref[rows, :]
            v = v_ref[rows, :]
            gu, tu, tv, rstd, xhat, vln, s_parts = _gmlp_forward_chunk(u, v, w_ref, bias, ln_gv, bb_ref[...])
            dm = dm_ref[rows, :]
            d_vln_parts = []
            d_gu_parts = []
            db_acc = jnp.zeros((CHUNK, LANES), f32)
            for g in range(A_GROUPS):
                cols = slice(g * CHUNK, (g + 1) * CHUNK)
                dmg = dm[:, cols]
                d_gu_parts.append(dmg * s_parts[g])
                d_s = dmg * gu[:, cols]
                db_acc = db_acc + jnp.where(lane == g, jnp.sum(d_s, axis=-1, keepdims=True), 0.0)
                d_sb = d_s.astype(bf16)
                dw_ref[g] += jnp.where(col <= row, _dot_nt(d_sb, vln[:, cols].astype(bf16)), 0.0)
                wt = jnp.where(row <= col, wt_ref[g], jnp.zeros((), bf16))
                d_vln_parts.append(_dot(wt, d_sb))
            db_ref[...] += db_acc
            d_vln = jnp.concatenate(d_vln_parts, axis=-1)
            d_gu = jnp.concatenate(d_gu_parts, axis=-1)
            dg_ref[...] += jnp.sum(d_vln * xhat, axis=0, keepdims=True)
            dbb_ref[...] += jnp.sum(d_vln, axis=0, keepdims=True)
            dxh = d_vln * ln_gv
            m1 = jnp.sum(dxh, axis=-1, keepdims=True) * (1.0 / MAIN_WIDTH)
            m2 = jnp.sum(dxh * xhat, axis=-1, keepdims=True) * (1.0 / MAIN_WIDTH)
            d_gv = rstd * (dxh - m1 - xhat * m2)
            duv_ref[rows, :MAIN_WIDTH] = (d_gu * _gelu_grad(u, tu)).astype(bf16)
            duv_ref[rows, MAIN_WIDTH:] = (d_gv * _gelu_grad(v, tv)).astype(bf16)

    vec = pl.BlockSpec((1, MAIN_WIDTH), lambda i: (0, 0))
    wspec = pl.BlockSpec((A_GROUPS, CHUNK, CHUNK), lambda i: (0, 0, 0))
    return pl.pallas_call(
        body, grid=(s // tm,),
        in_specs=[pl.BlockSpec((tm, MAIN_WIDTH), lambda i: (i, 0)), pl.BlockSpec((tm, MAIN_WIDTH), lambda i: (i, 1)),
                  pl.BlockSpec((tm, MAIN_WIDTH), lambda i: (i, 0)), wspec, wspec, pl.BlockSpec((CHUNK, A_GROUPS), lambda i: (0, 0)), vec, vec],
        out_specs=[pl.BlockSpec((tm, 2 * MAIN_WIDTH), lambda i: (i, 0)), wspec, pl.BlockSpec((CHUNK, LANES), lambda i: (0, 0)), vec, vec],
        out_shape=[SDS((s, 2 * MAIN_WIDTH), bf16), SDS((A_GROUPS, CHUNK, CHUNK), f32), SDS((CHUNK, LANES), f32),
                   SDS((1, MAIN_WIDTH), f32), SDS((1, MAIN_WIDTH), f32)],
        name=name, compiler_params=_cparams("arbitrary"))(proj, proj, d_mixed, ws, ws_t, bs_t, ln_g, ln_b)


def _head_mask(width, h):
    lane = lax.broadcasted_iota(jnp.int32, (1, width), 1)
    return (lane >= h * HEAD_DIM) & (lane < (h + 1) * HEAD_DIM)


def mem_attn_fwd(proj, q_block, kv, name, tm=512):
    s = proj.shape[0]
    n_mem = kv.shape[0]

    def body(q_ref, kv_ref, o_ref):
        q = q_ref[...].astype(f32)
        k = kv_ref[:, :MEM_WIDTH].astype(bf16)
        v = kv_ref[:, MEM_WIDTH:].astype(bf16)
        out = jnp.zeros((tm, MEM_WIDTH), f32)
        for h in range(MEM_HEADS):
            msk = _head_mask(MEM_WIDTH, h)
            qh = jnp.where(msk, q, 0.0).astype(bf16)
            sc = _dot_nt(qh, k) * ATT_SCALE
            e = jnp.exp(sc - jnp.max(sc, axis=-1, keepdims=True))
            p = e / jnp.sum(e, axis=-1, keepdims=True)
            out = jnp.where(msk, _dot(p.astype(bf16), v), out)
        o_ref[...] = out.astype(bf16)

    return pl.pallas_call(body, grid=(s // tm,),
                          in_specs=[pl.BlockSpec((tm, MEM_WIDTH), lambda i: (i, q_block)), pl.BlockSpec((n_mem, 2 * MEM_WIDTH), lambda i: (0, 0))],
                          out_specs=pl.BlockSpec((tm, MEM_WIDTH), lambda i: (i, 0)), out_shape=SDS((s, MEM_WIDTH), bf16), name=name,
                          compiler_params=_cparams("parallel"))(proj, kv)


def mem_attn_bwd(proj, q_block, kv, d_mixed, name, tm=512):
    s = proj.shape[0]
    n_mem = kv.shape[0]

    def body(q_ref, kv_ref, do_ref, dq_ref, dkv_ref):
        @pl.when(pl.program_id(0) == 0)
        def _():
            dkv_ref[...] = jnp.zeros_like(dkv_ref)

        q = q_ref[...].astype(f32)
        do = do_ref[...]
        k = kv_ref[:, :MEM_WIDTH].astype(bf16)
        v = kv_ref[:, MEM_WIDTH:].astype(bf16)
        dq = jnp.zeros((tm, MEM_WIDTH), f32)
        dk = jnp.zeros((n_mem, MEM_WIDTH), f32)
        dv = jnp.zeros((n_mem, MEM_WIDTH), f32)
        for h in range(MEM_HEADS):
            msk = _head_mask(MEM_WIDTH, h)
            qh = jnp.where(msk, q, 0.0).astype(bf16)
            doh = jnp.where(msk, do, 0.0).astype(bf16)
            sc = _dot_nt(qh, k) * ATT_SCALE
            e = jnp.exp(sc - jnp.max(sc, axis=-1, keepdims=True))
            p = e / jnp.sum(e, axis=-1, keepdims=True)
            dp = _dot_nt(doh, v)
            ds = p * (dp - jnp.sum(dp * p, axis=-1, keepdims=True))
            dsb = (ds * ATT_SCALE).astype(bf16)
            dq = jnp.where(msk, _dot(dsb, k), dq)
            dk = dk + _dot_tn(dsb, qh)
            dv = dv + _dot_tn(p.astype(bf16), doh)
        dq_ref[...] = dq.astype(bf16)
        dkv_ref[:, :MEM_WIDTH] += dk
        dkv_ref[:, MEM_WIDTH:] += dv

    return pl.pallas_call(
        body, grid=(s // tm,),
        in_specs=[pl.BlockSpec((tm, MEM_WIDTH), lambda i: (i, q_block)), pl.BlockSpec((n_mem, 2 * MEM_WIDTH), lambda i: (0, 0)),
                  pl.BlockSpec((tm, MEM_WIDTH), lambda i: (i, MAIN_WIDTH // MEM_WIDTH))],
        out_specs=[pl.BlockSpec((tm, MEM_WIDTH), lambda i: (i, 0)), pl.BlockSpec((n_mem, 2 * MEM_WIDTH), lambda i: (0, 0))],
        out_shape=[SDS((s, MEM_WIDTH), bf16), SDS((n_mem, 2 * MEM_WIDTH), f32)], name=name,
        compiler_params=_cparams("arbitrary"))(proj, kv, d_mixed)


def _tri(t, upper):
    r = lax.broadcasted_iota(jnp.int32, (t, t), 0)
    c = lax.broadcasted_iota(jnp.int32, (t, t), 1)
    return ((r <= c) if upper else (r >= c)).astype(f32)


def fgate_fwd(z_t, b, name, t=512):
    hh, s = z_t.shape

    def body(z_ref, b_ref, c_ref):
        u = _tri(t, True)
        carry = jnp.zeros((hh, 1), f32)
        for blk in range(s // t):
            x = z_ref[:, blk * t:(blk + 1) * t] + b_ref[...]
            logf = jnp.minimum(x, 0.0) - jnp.log(1.0 + jnp.exp(-jnp.abs(x)))
            y = jnp.dot(logf, u, precision=lax.Precision.HIGHEST, preferred_element_type=f32) + carry
            c_ref[:, blk * t:(blk + 1) * t] = y
            carry = y[:, t - 1:t]

    return pl.pallas_call(body, out_shape=SDS((hh, s), f32), name=name, compiler_params=_cparams())(z_t, b)


def fgate_bwd(dc_t, z_t, b, name, t=512):
    hh, s = z_t.shape

    def body(dc_ref, z_ref, b_ref, dz_ref, db_ref):
        low = _tri(t, False)
        carry = jnp.zeros((hh, 1), f32)
        total = jnp.zeros((hh, 1), f32)
        for blk in reversed(range(s // t)):
            cols = slice(blk * t, (blk + 1) * t)
            y = jnp.dot(dc_ref[:, cols], low, precision=lax.Precision.HIGHEST, preferred_element_type=f32) + carry
            carry = y[:, 0:1]
            dz = y * _sigmoid(-(z_ref[:, cols] + b_ref[...]))
            dz_ref[:, cols] = dz
            total = total + jnp.sum(dz, axis=-1, keepdims=True)
        db_ref[...] = jnp.broadcast_to(total, db_ref.shape)

    return pl.pallas_call(body, out_shape=[SDS((hh, s), f32), SDS((hh, LANES), f32)], name=name,
                          compiler_params=_cparams())(dc_t, z_t, b)


def _pair_masks():
    lane = lax.broadcasted_iota(jnp.int32, (1, LANES), 1)
    return [lane < HEAD_DIM, lane >= HEAD_DIM]


def _tile_base(cr_ref, hh, lo):
    return cr_ref[hh:hh + 1, pl.ds(lo, LANES)][:, 0:1]


def fox_fwd(q, kv, c_row, name, tq=512):
    s = kv.shape[0]
    nq = s // tq

    def body(q_ref, k_ref, v_ref, cr_ref, o_ref, lse_ref, ob_ref):
        i = pl.program_id(1)
        qv = q_ref[...]
        masks = _pair_masks()
        row = lax.broadcasted_iota(jnp.int32, (tq, tq), 0)
        col = lax.broadcasted_iota(jnp.int32, (tq, tq), 1)
        qh = [jnp.where(masks[hh], qv, jnp.zeros((), bf16)) * ATT_SCALE for hh in range(2)]
        ct = [_tile_base(cr_ref, hh, pl.multiple_of(i * tq, tq)) for hh in range(2)]

        def block(j, carry, diag):
            lo = pl.multiple_of(j * tq, tq)
            ks = k_ref[pl.ds(lo, tq), :]
            vs = v_ref[pl.ds(lo, tq), :]
            out = []
            for hh in range(2):
                m, l, acc = carry[hh]
                sc = _dot_nt(qh[hh], ks) + (ct[hh] - cr_ref[hh:hh + 1, pl.ds(lo, tq)])
                if diag:
                    sc = jnp.where(col <= row, sc, -jnp.inf)
                m_new = jnp.maximum(m, jnp.max(sc, axis=-1, keepdims=True))
                alpha = jnp.exp(m - m_new)
                p = jnp.exp(sc - m_new)
                l = alpha * l + jnp.sum(p, axis=-1, keepdims=True)
                p_hi = p.astype(bf16)
                p_lo = (p - p_hi.astype(f32)).astype(bf16)
                acc = alpha * acc + (_dot(p_hi, vs) + _dot(p_lo, vs))
                out.append((m_new, l, acc))
            return tuple(out)

        init = (jnp.full((tq, 1), -jnp.inf, f32), jnp.zeros((tq, 1), f32), jnp.zeros((tq, LANES), f32))
        carry = lax.fori_loop(0, i, functools.partial(block, diag=False), (init, init))
        res = [(acc / l, m + jnp.log(l)) for m, l, acc in block(i, carry, True)]
        out = jnp.where(masks[0], res[0][0], res[1][0])
        o_ref[...] = out
        ob_ref[...] = out.astype(bf16)
        lse_ref[...] = jnp.where(masks[0], res[0][1], res[1][1])

    return pl.pallas_call(
        body, grid=(FOX_PAIRS, nq),
        in_specs=[pl.BlockSpec((tq, LANES), lambda p, i: (i, p)), pl.BlockSpec((s, LANES), lambda p, i: (0, p)),
                  pl.BlockSpec((s, LANES), lambda p, i: (0, FOX_PAIRS + p)), pl.BlockSpec((None, 2, s), lambda p, i: (p, 0, 0))],
        out_specs=[pl.BlockSpec((tq, LANES), lambda p, i: (i, p)), pl.BlockSpec((None, tq, LANES), lambda p, i: (p, i, 0)),
                   pl.BlockSpec((tq, LANES), lambda p, i: (i, p))],
        out_shape=[SDS((s, MAIN_WIDTH), f32), SDS((FOX_PAIRS, s, LANES), f32), SDS((s, MAIN_WIDTH), bf16)], name=name,
        compiler_params=_cparams("parallel", "parallel"))(q, kv, kv, c_row)


def fox_bwd(q, kv, d_mixed, o, lse, c_row, name, tq=512):
    s = kv.shape[0]
    nq = s // tq

    def body(q_ref, k_ref, v_ref, do_ref, o_ref, lse_ref, cr_ref, dq_ref, dk_ref, dv_ref, dc_ref):
        j = pl.program_id(1)

        @pl.when(j == 0)
        def _():
            dq_ref[...] = jnp.zeros_like(dq_ref)

        masks = _pair_masks()
        sub = lax.broadcasted_iota(jnp.int32, (LANES, 1), 0)
        sub_masks = [sub < HEAD_DIM, sub >= HEAD_DIM]
        row = lax.broadcasted_iota(jnp.int32, (tq, tq), 0)
        col = lax.broadcasted_iota(jnp.int32, (tq, tq), 1)
        kj = k_ref[...]
        vj = v_ref[...]
        lo_j = pl.multiple_of(j * tq, tq)

        def block(i, carry, diag):
            dk_t, dv_t, dc0, dc1 = carry
            dcs = [dc0, dc1]
            lo = pl.multiple_of(i * tq, tq)
            qi = q_ref[pl.ds(lo, tq), :]
            qi = qi * ATT_SCALE
            qt_i = qi.T
            doi = do_ref[pl.ds(lo, tq), :]
            dot_i = doi.astype(bf16).T
            prod = doi.astype(bf16).astype(f32) * o_ref[pl.ds(lo, tq), :]
            lse_i = lse_ref[pl.ds(lo, tq), :]
            dq_i = jnp.zeros((tq, LANES), f32)
            for hh in range(2):
                qh = jnp.where(masks[hh], qi, jnp.zeros((), bf16))
                doh = jnp.where(masks[hh], doi, 0.0).astype(bf16)
                delta = jnp.sum(jnp.where(masks[hh], prod, 0.0), axis=-1, keepdims=True)
                sc = _dot_nt(qh, kj) + (_tile_base(cr_ref, hh, lo) - cr_ref[hh:hh + 1, pl.ds(lo_j, tq)])
                p = jnp.exp(sc - lse_i[:, hh * HEAD_DIM:hh * HEAD_DIM + 1])
                if diag:
                    p = jnp.where(col <= row, p, 0.0)
                dv_t = dv_t + _dot(jnp.where(sub_masks[hh], dot_i, jnp.zeros((), bf16)), p.astype(bf16))
                ds = p * (_dot_nt(doh, vj) - delta)
                dcs[hh] = dcs[hh] + jnp.sum(ds, axis=0, keepdims=True)
                dsb = ds.astype(bf16)
                dq_i = jnp.where(masks[hh], _dot(dsb, kj), dq_i)
                dk_t = dk_t + _dot(jnp.where(sub_masks[hh], qt_i, jnp.zeros((), bf16)), dsb)
            dq_ref[pl.ds(lo, tq), :] += dq_i * ATT_SCALE
            return dk_t, dv_t, dcs[0], dcs[1]

        zero = jnp.zeros((LANES, tq), f32)
        zrow = jnp.zeros((1, tq), f32)
        carry = block(j, (zero, zero, zrow, zrow), True)
        dk_t, dv_t, dc0, dc1 = lax.fori_loop(j + 1, nq, functools.partial(block, diag=False), carry)
        dk_ref[...] = dk_t.T.astype(bf16)
        dv_ref[...] = dv_t.T.astype(bf16)
        dc_ref[0:1, :] = -dc0
        dc_ref[1:2, :] = -dc1

    full = lambda p, j: (0, p)
    tile = lambda p, j: (j, p)
    return pl.pallas_call(
        body, grid=(FOX_PAIRS, nq),
        in_specs=[pl.BlockSpec((s, LANES), full), pl.BlockSpec((tq, LANES), tile), pl.BlockSpec((tq, LANES), lambda p, j: (j, FOX_PAIRS + p)),
                  pl.BlockSpec((s, LANES), full), pl.BlockSpec((s, LANES), full), pl.BlockSpec((None, s, LANES), lambda p, j: (p, 0, 0)),
                  pl.BlockSpec((None, 2, s), lambda p, j: (p, 0, 0))],
        out_specs=[pl.BlockSpec((s, LANES), full), pl.BlockSpec((tq, LANES), tile), pl.BlockSpec((tq, LANES), tile),
                   pl.BlockSpec((None, 2, tq), lambda p, j: (p, 0, j))],
        out_shape=[SDS((s, MAIN_WIDTH), f32), SDS((s, MAIN_WIDTH), bf16), SDS((s, MAIN_WIDTH), bf16), SDS((FOX_PAIRS, 2, s), f32)],
        name=name, compiler_params=_cparams("parallel", "arbitrary"))(q, kv, kv, d_mixed, o, lse, c_row)


def adamw(w, g, m, v, name, tr=256):
    r, c = w.shape
    tr = min(tr, r)
    assert r % tr == 0, (name, r, tr)
    c1 = 1.0 / (1.0 - ADAM_B1 ** ADAM_STEP)
    c2 = 1.0 / (1.0 - ADAM_B2 ** ADAM_STEP)

    def body(w_ref, g_ref, m_ref, v_ref, d_ref, mo_ref, vo_ref):
        gv = g_ref[...]
        mn = ADAM_B1 * m_ref[...] + (1.0 - ADAM_B1) * gv
        vn = ADAM_B2 * v_ref[...] + (1.0 - ADAM_B2) * gv * gv
        mo_ref[...] = mn
        vo_ref[...] = vn
        d_ref[...] = -ADAM_LR * ((mn * c1) / (jnp.sqrt(vn * c2) + ADAM_EPS) + ADAM_WD * w_ref[...])

    spec = pl.BlockSpec((tr, c), lambda i: (i, 0))
    return pl.pallas_call(body, grid=(r // tr,), in_specs=[spec] * 4, out_specs=[spec] * 3, out_shape=[SDS((r, c), f32)] * 3,
                          name=name, compiler_params=_cparams("parallel"))(w, g, m, v)


def sum_leading(x, name, out_dtype=f32, tr=None):
    n, r, c = x.shape
    tr = tr or r
    assert r % tr == 0

    def body(x_ref, o_ref):
        acc = x_ref[0].astype(f32)
        for k in range(1, n):
            acc = acc + x_ref[k].astype(f32)
        o_ref[...] = acc.astype(out_dtype)

    return pl.pallas_call(body, grid=(r // tr,), in_specs=[pl.BlockSpec((n, tr, c), lambda i: (0, i, 0))],
                          out_specs=pl.BlockSpec((tr, c), lambda i: (i, 0)), out_shape=SDS((r, c), out_dtype), name=name,
                          compiler_params=_cparams("parallel"))(x)


_ANY = pl.BlockSpec(memory_space=pl.ANY)
_DMA = pltpu.SemaphoreType.DMA


_HBM = pl.BlockSpec(memory_space=pltpu.HBM)
_SEM = pl.BlockSpec(memory_space=pltpu.SEMAPHORE)
_EFFECT = pltpu.SideEffectType.DATAFLOW_SIDE_EFFECTING
_FLIPS = [(0, 0, 1), (1, 0, 0), (0, 1, 0), (1, 1, 0), (1, 0, 1), (0, 1, 1), (1, 1, 1)]


def _me():
    return lax.axis_index("x"), lax.axis_index("y"), lax.axis_index("c")


def _peers():
    mx, my, mc = _me()
    return [(jnp.bitwise_xor(mx, fx), jnp.bitwise_xor(my, fy), jnp.bitwise_xor(mc, fc)) for fx, fy, fc in _FLIPS]


def _index(dev):
    return 4 * dev[0] + 2 * dev[1] + dev[2]


def _win(ref, axis, k, size, count=1):
    idx = [slice(None)] * len(ref.shape)
    idx[axis] = pl.ds(k * size, count * size)
    return ref.at[tuple(idx)]


def _hbm(a):
    return pltpu.with_memory_space_constraint(a, pltpu.HBM)


def _exchange_start(srcs, lands, copies_of, name):
    n = len(srcs)

    def body(*refs):
        src = refs[:n]
        send_sems, recv_sems, self_sems = refs[2 * n:2 * n + 3]
        land = refs[3 * n + 3:4 * n + 3]
        token = refs[4 * n + 3]
        me = _index(_me())
        for a in range(n):
            for s_ref, d_ref, peer in copies_of(a, src[a], land[a], me):
                if peer is None:
                    pltpu.make_async_copy(s_ref, d_ref, self_sems.at[a]).start()
                else:
                    pltpu.make_async_remote_copy(src_ref=s_ref, dst_ref=d_ref, send_sem=send_sems.at[a], recv_sem=recv_sems.at[a],
                                                 device_id=peer, device_id_type=MESH).start()
        token[...] = jnp.zeros_like(token)

    outs = pl.pallas_call(
        body, name=name,
        out_shape=(_DMA((n,)), _DMA((n,)), _DMA((n,)), *[pltpu.HBM(s.shape, s.dtype) for s in srcs],
                   *[pltpu.HBM(l.shape, l.dtype) for l in lands], SDS((8, LANES), f32)),
        in_specs=[_HBM] * (2 * n), out_specs=(_SEM, _SEM, _SEM, *[_HBM] * (2 * n), pl.BlockSpec(memory_space=pltpu.VMEM)),
        input_output_aliases={i: 3 + i for i in range(2 * n)},
        compiler_params=pltpu.CompilerParams(has_side_effects=_EFFECT),
    )(*[_hbm(s) for s in srcs], *[_hbm(lax.empty(l.shape, l.dtype)) for l in lands])
    return dict(sems=outs[:3], srcs=list(outs[3:3 + n]), lands=list(outs[3 + n:3 + 2 * n]), token=outs[3 + 2 * n])


def _exchange_wait(started, waits_of, after, name, which=None):
    which = list(range(len(started["srcs"]))) if which is None else which
    srcs, lands = [started["srcs"][a] for a in which], [started["lands"][a] for a in which]
    n = len(which)

    def body(*refs):
        src = refs[:n]
        land = refs[n:2 * n]
        send_sems, recv_sems, self_sems = refs[2 * n:2 * n + 3]
        me = _index(_me())
        for pos, a in enumerate(which):
            seven, (s_ref, d_ref) = waits_of(a, src[pos], land[pos], me)
            both = pltpu.make_async_remote_copy(src_ref=seven, dst_ref=seven, send_sem=send_sems.at[a], recv_sem=recv_sems.at[a],
                                                device_id=_me(), device_id_type=MESH)
            both.wait_send()
            both.wait_recv()
            pltpu.make_async_copy(s_ref, d_ref, self_sems.at[a]).wait()

    outs = pl.pallas_call(
        body, name=name, out_shape=tuple(pltpu.HBM(t.shape, t.dtype) for t in srcs + lands),
        in_specs=[_HBM] * (2 * n) + [_SEM] * 3 + [_ANY], out_specs=tuple([_HBM] * (2 * n)),
        input_output_aliases={i: i for i in range(2 * n)},
        compiler_params=pltpu.CompilerParams(has_side_effects=_EFFECT),
    )(*srcs, *lands, *started["sems"], after)
    return list(outs[n:])


def gather_start(locs, axes, name):
    lands = [SDS(tuple(N_DEV * d if i == ax else d for i, d in enumerate(l.shape)), l.dtype) for l, ax in zip(locs, axes)]

    def copies_of(a, src, land, me):
        mine = _win(land, axes[a], me, src.shape[axes[a]])
        return [(src, mine, peer) for peer in _peers()] + [(src, mine, None)]

    return _exchange_start(locs, lands, copies_of, name)


def gather_wait(started, axes, after, name, which=None):
    def waits_of(a, src, land, me):
        size = src.shape[axes[a]]
        return _win(land, axes[a], 0, size, N_DEV - 1), (src, _win(land, axes[a], me, size))

    return _exchange_wait(started, waits_of, after, name, which)


def scatter_start(grads, axes, name):
    lands = [SDS((N_DEV,) + tuple(d // N_DEV if i == ax else d for i, d in enumerate(g.shape)), g.dtype) for g, ax in zip(grads, axes)]

    def copies_of(a, src, land, me):
        size = src.shape[axes[a]] // N_DEV
        out = [(_win(src, axes[a], _index(peer), size), land.at[me], peer) for peer in _peers()]
        return out + [(_win(src, axes[a], me, size), land.at[me], None)]

    return _exchange_start(grads, lands, copies_of, name)


def scatter_wait(started, axes, after, name):
    def waits_of(a, src, land, me):
        size = src.shape[axes[a]] // N_DEV
        return land.at[pl.ds(0, N_DEV - 1)], (_win(src, axes[a], me, size), land.at[me])

    return _exchange_wait(started, waits_of, after, name)


def _row_tile(rows, cap=512):
    return max(t for t in range(8, min(rows, cap) + 1, 8) if rows % t == 0)


_SMALL = [
    ("ln_mix_pre", (2, 1024)), ("ln_mix_post", (2, 1024)), ("ln_ffn_pre", (2, 1024)), ("ln_ffn_post", (2, 1024)),
    ("ln_mem", (2, 1024)), ("w_spatial", (1, 6, 128, 128)), ("b_spatial", (1, 6, 128)), ("ln_shared", (1024,)),
    ("b_forget", (12,)), ("ln_v_g", (1, 768)), ("ln_v_b", (1, 768)),
]
_SMALL_TILE = 8 * LANES


def _small_rows(shape):
    return -(-math.prod(shape) // _SMALL_TILE) * 8


def _pack_small(vals, shapes):
    parts = []
    for name, shape in shapes:
        flat = vals[name].reshape(-1).astype(f32)
        rows = _small_rows(shape)
        parts.append(jnp.pad(flat, (0, rows * LANES - flat.shape[0])).reshape(rows, LANES))
    return jnp.concatenate(parts, axis=0)


def _unpack_small(buf, shapes):
    out = {}
    lo = 0
    for name, shape in shapes:
        rows = _small_rows(shape)
        out[name] = buf[lo:lo + rows].reshape(-1)[:math.prod(shape)].reshape(shape)
        lo += rows
    return out


def kernel(x, mem, ln_mix_pre, ln_mix_post, ln_ffn_pre, ln_ffn_post, ln_mem, w_mem_kv, w_out, w_ffn_gate, w_ffn_up, w_ffn_down, w_in_a, w_spatial, b_spatial, ln_v_g, ln_v_b, ln_shared, w_shared_kv, b_forget, w_in_b, loss_target, m_ln_mix_pre, m_ln_mix_post, m_ln_ffn_pre, m_ln_ffn_post, m_ln_mem, m_w_mem_kv, m_w_out, m_w_ffn_gate, m_w_ffn_up, m_w_ffn_down, m_w_in_a, m_w_spatial, m_b_spatial, m_ln_v_g, m_ln_v_b, m_ln_shared, m_w_shared_kv, m_b_forget, m_w_in_b, v_ln_mix_pre, v_ln_mix_post, v_ln_ffn_pre, v_ln_ffn_post, v_ln_mem, v_w_mem_kv, v_w_out, v_w_ffn_gate, v_w_ffn_up, v_w_ffn_down, v_w_in_a, v_w_spatial, v_b_spatial, v_ln_v_g, v_ln_v_b, v_ln_shared, v_w_shared_kv, v_b_forget, v_w_in_b):
    weights = dict(ln_mix_pre=ln_mix_pre, ln_mix_post=ln_mix_post, ln_ffn_pre=ln_ffn_pre, ln_ffn_post=ln_ffn_post, ln_mem=ln_mem,
                   w_mem_kv=w_mem_kv, w_out=w_out, w_ffn_gate=w_ffn_gate, w_ffn_up=w_ffn_up, w_ffn_down=w_ffn_down, w_in_a=w_in_a,
                   w_spatial=w_spatial, b_spatial=b_spatial, ln_v_g=ln_v_g, ln_v_b=ln_v_b, ln_shared=ln_shared,
                   w_shared_kv=w_shared_kv, b_forget=b_forget, w_in_b=w_in_b)
    mom_m = dict(ln_mix_pre=m_ln_mix_pre, ln_mix_post=m_ln_mix_post, ln_ffn_pre=m_ln_ffn_pre, ln_ffn_post=m_ln_ffn_post, ln_mem=m_ln_mem,
                 w_mem_kv=m_w_mem_kv, w_out=m_w_out, w_ffn_gate=m_w_ffn_gate, w_ffn_up=m_w_ffn_up, w_ffn_down=m_w_ffn_down, w_in_a=m_w_in_a,
                 w_spatial=m_w_spatial, b_spatial=m_b_spatial, ln_v_g=m_ln_v_g, ln_v_b=m_ln_v_b, ln_shared=m_ln_shared,
                 w_shared_kv=m_w_shared_kv, b_forget=m_b_forget, w_in_b=m_w_in_b)
    mom_v = dict(ln_mix_pre=v_ln_mix_pre, ln_mix_post=v_ln_mix_post, ln_ffn_pre=v_ln_ffn_pre, ln_ffn_post=v_ln_ffn_post, ln_mem=v_ln_mem,
                 w_mem_kv=v_w_mem_kv, w_out=v_w_out, w_ffn_gate=v_w_ffn_gate, w_ffn_up=v_w_ffn_up, w_ffn_down=v_w_ffn_down, w_in_a=v_w_in_a,
                 w_spatial=v_w_spatial, b_spatial=v_b_spatial, ln_v_g=v_ln_v_g, ln_v_b=v_ln_v_b, ln_shared=v_ln_shared,
                 w_shared_kv=v_w_shared_kv, b_forget=v_b_forget, w_in_b=v_w_in_b)
    names = list(weights)
    mx, my, mc = lax.axis_index("x"), lax.axis_index("y"), lax.axis_index("c")
    me = 4 * mx + 2 * my + mc

    h0 = x[0]
    mem0 = mem[0]
    tgt = loss_target[0]
    seq = h0.shape[0]

    vec = lambda a: a.reshape(1, -1)
    pad_to = lambda a, axis, size: jnp.pad(a, [(0, size - a.shape[i] if i == axis else 0) for i in range(a.ndim)])

    def after(tok, a):
        return a + tok[0, 0].astype(a.dtype)

    lnv_loc = pad_to(jnp.concatenate([ln_v_g, ln_v_b], axis=0), 0, 8)
    st_a = gather_start([w_in_a.astype(bf16), pad_to(lnv_loc, 1, LANES)[None]], [0, 0], "gather_a_start")
    mix_locs = lambda l, tok: [after(tok, w_mem_kv[l]).astype(bf16), w_out[l].astype(bf16)]

    def ffn_gather_start(l, tok):
        gate_up = gather_start([pad_to(after(tok, w_ffn_gate[l]).astype(bf16), 1, FF_SHARD_PAD),
                                pad_to(w_ffn_up[l].astype(bf16), 1, FF_SHARD_PAD)], [1, 1], f"gather_gate_up{l}_start")
        down = gather_start([pad_to(after(gate_up["token"], w_ffn_down[l]).astype(bf16), 0, FF_SHARD_PAD)], [0], f"gather_down{l}_start")
        return gate_up, down

    st_b = [gather_start(mix_locs(0, st_a["token"]), [0, 0], "gather_b0_start"), None]
    st_c = ffn_gather_start(0, st_b[0]["token"])
    st_d = gather_start([after(st_c[1]["token"], w_in_b[0]).astype(bf16), pad_to(w_shared_kv.astype(bf16), 1, KV_PAD)], [0, 0],
                        "gather_d_start")
    st_b[1] = gather_start(mix_locs(1, st_d["token"]), [0, 0], "gather_b1_start")
    st_e = ffn_gather_start(1, st_b[1]["token"])
    ws = w_spatial[0].astype(bf16)
    ws_t = ws.transpose(0, 2, 1)
    bs_t = b_spatial[0].T

    (a0,) = rms_fwd(h0, [after(st_e[1]["token"], vec(ln_mix_pre[0]))], "a0_norm")
    w_in_a8, lnv8 = gather_wait(st_a, [0, 0], a0, "gather_a_wait")
    w_in_a_full = w_in_a8.transpose(1, 0, 2).reshape(D_MODEL, -1)
    lnv_g = lnv8[:, 0, :MAIN_WIDTH // N_DEV].reshape(1, MAIN_WIDTH)
    lnv_b = lnv8[:, 1, :MAIN_WIDTH // N_DEV].reshape(1, MAIN_WIDTH)
    proj0 = mm(a0, w_in_a_full, "proj0", tn=896)
    main0 = gmlp_fwd(proj0, ws, bs_t, lnv_g, lnv_b, "gmlp_fwd")
    w_mkv, w_o = [None, None], [None, None]
    w_mkv[0], w_o[0] = gather_wait(st_b[0], [0, 0], main0, "gather_b0_wait")
    (memn0,) = rms_fwd(mem0, [vec(ln_mem[0])], "mem0_norm")
    kvm0 = mm(memn0, w_mkv[0], "kvm0")
    om0 = mem_attn_fwd(proj0, 2 * MAIN_WIDTH // MEM_WIDTH, kvm0, "mem_attn0")
    mixed0 = jnp.concatenate([main0, om0], axis=-1)
    y1_0, hmid0, f0 = mm_resnorm(mixed0, w_o[0], h0, vec(ln_mix_post[0]), [vec(ln_ffn_pre[0])], "mix_out0")
    w_g0, w_u0 = gather_wait(st_c[0], [1, 1], f0, "gather_gate_up0_wait")
    gu0, act0 = ffn_up(f0, w_g0, w_u0, "ffn_up0")
    (w_d0,) = gather_wait(st_c[1], [0], act0, "gather_down0_wait")
    y2_0, h1, a1, sin1 = mm_resnorm(act0, w_d0, hmid0, vec(ln_ffn_post[0]), [vec(ln_mix_pre[1]), vec(ln_shared)], "ffn_down0")

    w_inb, w_kv = gather_wait(st_d, [0, 0], sin1, "gather_d_wait")
    kvb = mm(sin1, w_kv, "kv_shared", out_dtype=bf16, tn=MAIN_WIDTH, ncols=2 * MAIN_WIDTH)
    zf = mm(sin1, w_kv, "forget_logits", tn=256, col0=2 * MAIN_WIDTH, ncols=256)
    qb = mm(a1, w_inb, "proj1", out_dtype=bf16)
    z_t = jnp.pad(zf[:, :FOX_HEADS].T, ((0, 16 - FOX_HEADS), (0, 0)))
    bf_col = jnp.pad(b_forget, (0, 16 - FOX_HEADS)).reshape(16, 1)
    c_t = fgate_fwd(z_t, bf_col, "fgate_fwd")
    c_row = c_t[:FOX_HEADS].reshape(FOX_PAIRS, 2, seq)
    main1, lse, main1_b = fox_fwd(qb, kvb, c_row, "fox_fwd")
    w_mkv[1], w_o[1] = gather_wait(st_b[1], [0, 0], main1, "gather_b1_wait")
    (memn1,) = rms_fwd(mem0, [vec(ln_mem[1])], "mem1_norm")
    kvm1 = mm(memn1, w_mkv[1], "kvm1")
    om1 = mem_attn_fwd(qb, MAIN_WIDTH // MEM_WIDTH, kvm1, "mem_attn1")
    mixed1 = jnp.concatenate([main1_b, om1], axis=-1)
    y1_1, hmid1, f1 = mm_resnorm(mixed1, w_o[1], h1, vec(ln_mix_post[1]), [vec(ln_ffn_pre[1])], "mix_out1")
    w_g1, w_u1 = gather_wait(st_e[0], [1, 1], f1, "gather_gate_up1_wait")
    gu1, act1 = ffn_up(f1, w_g1, w_u1, "ffn_up1")
    (w_d1,) = gather_wait(st_e[1], [0], act1, "gather_down1_wait")
    dh, d_y2_1, dg_fpost1, loss_tile = mm_resnorm_loss(act1, w_d1, hmid1, vec(ln_ffn_post[1]), tgt, "ffn_down1_loss")
    ffn_w = [(w_g0, w_u0, w_d0), (w_g1, w_u1, w_d1)]

    small = {}

    def ffn_backward(layer, dh_out, d_y2, hmid, f, gu, act, y1):
        w_g, w_u, w_d = ffn_w[layer]
        dw_down = mm_tn(act, d_y2, f"dw_down{layer}")
        rs_down = scatter_start([dw_down], [0], f"scatter_down{layer}_start")
        d_g, d_u = ffn_act_grad(d_y2, w_d, gu, f"ffn_act_grad{layer}")
        dw_g = mm_tn(f, d_g, f"dw_gate{layer}", dep=rs_down["token"])
        dw_u = mm_tn(f, d_u, f"dw_up{layer}")
        rs_gate_up = scatter_start([dw_g, dw_u], [1, 1], f"scatter_gate_up{layer}_start")
        dh_mid, d_y1, dg_fpre, dg_mpost = ffn_in_grad(d_g, d_u, w_g, w_u, hmid, dh_out, after(rs_gate_up["token"], vec(ln_ffn_pre[layer])),
                                                      y1, vec(ln_mix_post[layer]), f"ffn_in_grad{layer}")
        return dh_mid, d_y1, dg_fpre, dg_mpost, (rs_down, rs_gate_up)

    def mix_out_backward(layer, d_y1, mixed):
        dw_out = mm_tn(mixed, d_y1, f"dw_out{layer}")
        d_mixed = mm(d_y1, w_o[layer], f"d_mixed{layer}", trans_b=True)
        return d_mixed, dw_out

    def mem_backward(layer, q_src, q_block, kvm, memn, d_mixed):
        d_qm, d_kvm = mem_attn_bwd(q_src, q_block, kvm, d_mixed, f"mem_attn_bwd{layer}")
        d_kvm_b = d_kvm.astype(bf16)
        dw_mkv = mm_tn(memn, d_kvm_b, f"dw_mem_kv{layer}")
        d_memn = mm(d_kvm_b, w_mkv[layer], f"d_memn{layer}", trans_b=True)
        _, dg_mem = rms_bwd(mem0, vec(ln_mem[layer]), d_memn, None, bf16, f"mem_norm_bwd{layer}")
        return d_qm, dw_mkv, dg_mem


    dh_mid1, d_y1_1, dg_fpre1, dg_mpost1, rs_ffn1 = ffn_backward(1, dh, d_y2_1, hmid1, f1, gu1, act1, y1_1)
    d_mixed1, dw_out1 = mix_out_backward(1, d_y1_1, mixed1)
    d_qm1, dw_mkv1, dg_mem1 = mem_backward(1, qb, MAIN_WIDTH // MEM_WIDTH, kvm1, memn1, d_mixed1)
    rs_mix1 = scatter_start([dw_out1, dw_mkv1], [0, 0], "scatter_mix1_start")
    dq, dk, dv, dc = fox_bwd(qb, kvb, d_mixed1, main1, lse, after(rs_mix1["token"], c_row), "fox_bwd")
    dc_t = jnp.pad(dc.reshape(FOX_HEADS, seq), ((0, 16 - FOX_HEADS), (0, 0)))
    dz_t, db_f = fgate_bwd(dc_t, z_t, bf_col, "fgate_bwd")
    d_kvf = jnp.concatenate([dk, dv, jnp.pad(dz_t[:FOX_HEADS].T.astype(bf16), ((0, 0), (0, KV_PAD - KV_WIDTH)))], axis=-1)
    d_proj1 = jnp.concatenate([dq.astype(bf16), d_qm1], axis=-1)
    dw_in_b = mm_tn(a1, d_proj1, "dw_in_b")
    dw_kv = mm_tn(sin1, d_kvf, "dw_kv", tn=896)
    rs_2 = scatter_start([dw_in_b, dw_kv], [0, 0], "scatter_shared_start")
    dh1, (dg_pre1, dg_shared), d_y2_0, dg_fpost0 = proj_in_grad(
        [(d_proj1, w_inb, vec(ln_mix_pre[1])), (d_kvf, w_kv, vec(ln_shared))], h1, dh_mid1, "in_grad1", dep=rs_2["token"],
        below=(y2_0, vec(ln_ffn_post[0])))

    dh_mid0, d_y1_0, dg_fpre0, dg_mpost0, rs_ffn0 = ffn_backward(0, dh1, d_y2_0, hmid0, f0, gu0, act0, y1_0)
    d_mixed0, dw_out0 = mix_out_backward(0, d_y1_0, mixed0)
    d_qm0, dw_mkv0, dg_mem0 = mem_backward(0, proj0, 2 * MAIN_WIDTH // MEM_WIDTH, kvm0, memn0, d_mixed0)
    rs_mix0 = scatter_start([dw_out0, dw_mkv0], [0, 0], "scatter_mix0_start")
    d_uv, dw_s, db_s, dg_lnv, db_lnv = gmlp_bwd(proj0, d_mixed0, ws, ws_t, bs_t, after(rs_mix0["token"], lnv_g), lnv_b, "gmlp_bwd")

    small["ln_mix_pre"] = jnp.concatenate([jnp.zeros_like(dg_pre1), dg_pre1], axis=0)
    small["ln_mix_post"] = jnp.concatenate([dg_mpost0, dg_mpost1], axis=0)
    small["ln_ffn_pre"] = jnp.concatenate([dg_fpre0, dg_fpre1], axis=0)
    small["ln_ffn_post"] = jnp.concatenate([dg_fpost0, dg_fpost1], axis=0)
    small["ln_mem"] = jnp.concatenate([dg_mem0, dg_mem1], axis=0)
    small["w_spatial"] = dw_s[None]
    small["b_spatial"] = db_s[:, :A_GROUPS].T[None]
    small["ln_shared"] = dg_shared[0]
    small["b_forget"] = db_f[:FOX_HEADS, 0]
    small["ln_v_g"] = dg_lnv
    small["ln_v_b"] = db_lnv
    small_rows = jnp.concatenate([_pack_small(small, _SMALL), loss_tile], axis=0)
    st_small = gather_start([small_rows[None]], [0], "gather_small_grads_start")
    d_proj0 = jnp.concatenate([d_uv, after(st_small["token"], d_qm0)], axis=-1)
    dw_in_a = mm_tn(a0, d_proj0, "dw_in_a", tn=896)
    rs_in_a = scatter_start([dw_in_a.reshape(D_MODEL, N_DEV, -1).transpose(1, 0, 2)], [0], "scatter_in_a_start")
    grad_x, (dg_pre0,) = proj_in_grad([(d_proj0, w_in_a_full, vec(ln_mix_pre[0]))], h0, dh_mid0, "in_grad0", dep=rs_in_a["token"])
    st_last = gather_start([dg_pre0.reshape(1, 8, LANES)], [0], "gather_last_grad_start")

    def owned(started, axes, wait_after, name):
        recv = scatter_wait(started, axes, wait_after, name)
        return [sum_leading(r.reshape((N_DEV, -1, r.shape[-1])), f"{name}_sum{i}", tr=_row_tile(math.prod(r.shape[1:-1])))
                for i, r in enumerate(recv)]

    (g_down1,) = owned(rs_ffn1[0], [0], after(st_last["token"], grad_x[:8, :LANES]), "scatter_down1_wait")
    g_gu1 = owned(rs_ffn1[1], [1, 1], g_down1, "scatter_gate_up1_wait")
    g_mix1 = owned(rs_mix1, [0, 0], g_gu1[0], "scatter_mix1_wait")
    g2 = owned(rs_2, [0, 0], g_mix1[0], "scatter_shared_wait")
    (g_down0,) = owned(rs_ffn0[0], [0], g2[0], "scatter_down0_wait")
    g_gu0 = owned(rs_ffn0[1], [1, 1], g_down0, "scatter_gate_up0_wait")
    g_mix0 = owned(rs_mix0, [0, 0], g_gu0[0], "scatter_mix0_wait")
    (g_in_a,) = owned(rs_in_a, [0], g_mix0[0], "scatter_in_a_wait")
    g_local = dict(
        w_ffn_gate=jnp.stack([g_gu0[0], g_gu1[0]])[:, :, :FF_SHARD], w_ffn_up=jnp.stack([g_gu0[1], g_gu1[1]])[:, :, :FF_SHARD],
        w_ffn_down=jnp.stack([g_down0, g_down1])[:, :FF_SHARD], w_out=jnp.stack([g_mix0[0], g_mix1[0]]),
        w_mem_kv=jnp.stack([g_mix0[1], g_mix1[1]]), w_in_b=g2[0][None], w_shared_kv=g2[1][:, :KV_WIDTH], w_in_a=g_in_a[None])
    (small_all,) = gather_wait(st_small, [0], g_in_a, "gather_small_grads_wait")
    (last_all,) = gather_wait(st_last, [0], small_all, "gather_last_grad_wait")
    small_sum = sum_leading(small_all, "sum_small_grads")
    loss = small_sum[small_rows.shape[0] - 1, 0]
    g_small = _unpack_small(small_sum, _SMALL)
    g_small["ln_mix_pre"] = jnp.concatenate([sum_leading(last_all, "sum_last_grad").reshape(1, D_MODEL), g_small["ln_mix_pre"][1:]], axis=0)
    shard = MAIN_WIDTH // N_DEV
    for n in ("ln_v_g", "ln_v_b"):
        g_small[n] = lax.dynamic_slice_in_dim(g_small[n], me * shard, shard, axis=1)
    grad_w = {**g_small, **g_local}

    delta, new_m, new_v = {}, {}, {}
    for n in g_local:
        two_d = (-1, weights[n].shape[-1])
        d_, m_, v_ = adamw(weights[n].reshape(two_d), grad_w[n].reshape(two_d), mom_m[n].reshape(two_d), mom_v[n].reshape(two_d),
                           f"adamw_{n}", tr=_row_tile(math.prod(weights[n].shape[:-1])))
        delta[n], new_m[n], new_v[n] = (t.reshape(weights[n].shape) for t in (d_, m_, v_))
    small_local_shapes = [(n, tuple(weights[n].shape)) for n, _ in _SMALL]
    packed = [_pack_small(src, small_local_shapes) for src in (weights, grad_w, mom_m, mom_v)]
    outs = adamw(*packed, "adamw_small", tr=packed[0].shape[0])
    for dst, buf in zip((delta, new_m, new_v), outs):
        dst.update(_unpack_small(buf, small_local_shapes))

    return (loss, grad_x[None], *[grad_w[n] for n in names], *[delta[n] for n in names],
            *[new_m[n] for n in names], *[new_v[n] for n in names])
```

```python
import functools
import math

import jax
import jax.numpy as jnp
from jax import lax
from jax.experimental import pallas as pl
from jax.experimental.pallas import tpu as pltpu

f32 = jnp.float32
bf16 = jnp.bfloat16
SDS = jax.ShapeDtypeStruct

D_MODEL = 1024
MAIN_WIDTH = 768
MEM_WIDTH = 256
HEAD_DIM = 64
MEM_HEADS = 4
FOX_HEADS = 12
FOX_PAIRS = FOX_HEADS // 2
CHUNK = 128
A_GROUPS = 6
FF_SHARD = 352
FF_SHARD_PAD = 384
FF_PAD = 8 * FF_SHARD_PAD
KV_WIDTH = 2 * MAIN_WIDTH + FOX_HEADS
KV_PAD = 1792
RMS_EPS = 1e-6
LN_EPS = 1e-5
ATT_SCALE = HEAD_DIM ** -0.5
ADAM_LR, ADAM_B1, ADAM_B2, ADAM_EPS, ADAM_WD, ADAM_STEP = 0.001, 0.9, 0.999, 1e-08, 0.01, 10
N_DEV = 8
AXES = ("x", "y", "c")
MESH = pl.DeviceIdType.MESH
V7X_VMEM_LIMIT = 56 * 1024 * 1024
LANES = 128
FLAT_W = 512
ROW_PAD = 16


def _cparams(*sem):
    return pltpu.CompilerParams(dimension_semantics=sem or None, vmem_limit_bytes=V7X_VMEM_LIMIT)


def _dot(a, b):
    return jnp.dot(a, b, preferred_element_type=f32)


def _dot_nt(a, b):
    return lax.dot_general(a, b, (((1,), (1,)), ((), ())), preferred_element_type=f32)


def _dot_tn(a, b):
    return lax.dot_general(a, b, (((0,), (0,)), ((), ())), preferred_element_type=f32)


def _gelu(x):
    k = math.sqrt(2.0 / math.pi)
    t = jnp.tanh(k * (x + 0.044715 * x * x * x))
    return 0.5 * x * (1.0 + t), t


def _gelu_grad(x, t):
    k = math.sqrt(2.0 / math.pi)
    return 0.5 * (1.0 + t) + 0.5 * x * (1.0 - t * t) * k * (1.0 + 3.0 * 0.044715 * x * x)


def _sigmoid(x):
    return 1.0 / (1.0 + jnp.exp(-x))


def rms_fwd(x, gains, name, tm=512):
    m, d = x.shape
    tm = min(tm, m)
    n = len(gains)

    def body(x_ref, *refs):
        xv = x_ref[...]
        y = xv * lax.rsqrt(jnp.sum(xv * xv, axis=-1, keepdims=True) * (1.0 / d) + RMS_EPS)
        for g_ref, o_ref in zip(refs[:n], refs[n:]):
            o_ref[...] = (y * g_ref[...]).astype(bf16)

    row = pl.BlockSpec((tm, d), lambda i: (i, 0))
    vec = pl.BlockSpec((1, d), lambda i: (0, 0))
    return pl.pallas_call(body, grid=(m // tm,), in_specs=[row] + [vec] * n, out_specs=[row] * n,
                          out_shape=[SDS((m, d), bf16)] * n, name=name, compiler_params=_cparams("parallel"))(x, *gains)


def rms_bwd(x, g, dy, add, out_dtype, name, tm=512):
    m, d = x.shape
    tm = min(tm, m)
    has_add = add is not None

    def body(x_ref, g_ref, dy_ref, *refs):
        dx_ref, dg_ref = refs[-2], refs[-1]
        xv = x_ref[...]
        dyv = dy_ref[...].astype(f32)
        r = lax.rsqrt(jnp.sum(xv * xv, axis=-1, keepdims=True) * (1.0 / d) + RMS_EPS)
        xn = xv * r
        dyg = dyv * g_ref[...]
        dx = r * (dyg - xn * (jnp.sum(dyg * xn, axis=-1, keepdims=True) * (1.0 / d)))
        if has_add:
            dx = dx + refs[0][...]
        dx_ref[...] = dx.astype(out_dtype)

        @pl.when(pl.program_id(0) == 0)
        def _():
            dg_ref[...] = jnp.zeros_like(dg_ref)

        dg_ref[...] += jnp.sum(dyv * xn, axis=0, keepdims=True)

    row = pl.BlockSpec((tm, d), lambda i: (i, 0))
    vec = pl.BlockSpec((1, d), lambda i: (0, 0))
    ins = [x, g, dy] + ([add] if has_add else [])
    return pl.pallas_call(body, grid=(m // tm,), in_specs=[row, vec, row] + ([row] if has_add else []),
                          out_specs=[row, vec], out_shape=[SDS((m, d), out_dtype), SDS((1, d), f32)], name=name,
                          compiler_params=_cparams("arbitrary"))(*ins)


def mm(a, b, name, trans_b=False, out_dtype=f32, tm=1024, tn=1024, layer=None, col0=0, ncols=None, dep=None):
    m, k = a.shape
    n_all = b.shape[-2] if trans_b else b.shape[-1]
    n = n_all if ncols is None else ncols
    tm, tn = min(tm, m), min(tn, n)
    assert m % tm == 0 and n % tn == 0 and col0 % tn == 0 and not (trans_b and col0), (name, m, n, tm, tn)
    jb = col0 // tn
    lead = () if layer is None else (None,)
    sel = () if layer is None else (layer,)

    def body(a_ref, b_ref, *rest):
        r = _dot_nt(a_ref[...], b_ref[...]) if trans_b else _dot(a_ref[...], b_ref[...])
        rest[-1][...] = r.astype(out_dtype)

    if trans_b:
        b_spec = pl.BlockSpec(lead + (tn, k), lambda j, i: sel + (j, 0))
    else:
        b_spec = pl.BlockSpec(lead + (k, tn), lambda j, i: sel + (0, jb + j))
    deps = [] if dep is None else [dep]
    dep_specs = [pl.BlockSpec((8, LANES), lambda j, i: (0, 0))] * len(deps)
    return pl.pallas_call(body, grid=(n // tn, m // tm), in_specs=[pl.BlockSpec((tm, k), lambda j, i: (i, 0)), b_spec] + dep_specs,
                          out_specs=pl.BlockSpec((tm, tn), lambda j, i: (i, j)), out_shape=SDS((m, n), out_dtype),
                          name=name, compiler_params=_cparams("parallel", "parallel"))(a, b, *deps)


def mm_tn(a, g, name, tk=1024, tn=1024, out_dtype=bf16, dep=None):
    s, k = a.shape
    n = g.shape[1]
    tk, tn = min(tk, k), min(tn, n)
    assert k % tk == 0 and n % tn == 0, (name, k, n, tk, tn)

    def body(a_ref, g_ref, *rest):
        rest[-1][...] = _dot_tn(a_ref[...], g_ref[...]).astype(out_dtype)

    deps = [] if dep is None else [dep]
    dep_specs = [pl.BlockSpec((8, LANES), lambda i, j: (0, 0))] * len(deps)
    return pl.pallas_call(body, grid=(k // tk, n // tn),
                          in_specs=[pl.BlockSpec((s, tk), lambda i, j: (0, i)), pl.BlockSpec((s, tn), lambda i, j: (0, j))] + dep_specs,
                          out_specs=pl.BlockSpec((tk, tn), lambda i, j: (i, j)), out_shape=SDS((k, n), out_dtype), name=name,
                          compiler_params=_cparams("parallel", "parallel"))(a, g, *deps)


def _resident(shape, index_map):
    return pl.BlockSpec(shape, index_map, pipeline_mode=pl.Buffered(1))


def _rms(xv):
    return xv * lax.rsqrt(jnp.sum(xv * xv, axis=-1, keepdims=True) * (1.0 / xv.shape[-1]) + RMS_EPS)


def _rms_bwd_math(xv, g, dy):
    d = xv.shape[-1]
    r = lax.rsqrt(jnp.sum(xv * xv, axis=-1, keepdims=True) * (1.0 / d) + RMS_EPS)
    xn = xv * r
    dyg = dy * g
    dx = r * (dyg - xn * (jnp.sum(dyg * xn, axis=-1, keepdims=True) * (1.0 / d)))
    return dx, jnp.sum(dy * xn, axis=0, keepdims=True)


SUB_ROWS = 512


def mm_resnorm(a, b, h, g_post, gains, name, tm=512):
    m, k = a.shape
    d = b.shape[1]
    n = len(gains)

    def body(a_ref, b_ref, h_ref, gp_ref, *refs):
        for r in range(tm // SUB_ROWS):
            rows = slice(r * SUB_ROWS, (r + 1) * SUB_ROWS)
            y = _dot(a_ref[rows, :], b_ref[...])
            refs[n][rows, :] = y
            hn = h_ref[rows, :] + _rms(y) * gp_ref[...]
            refs[n + 1][rows, :] = hn
            if n:
                z = _rms(hn)
                for g_ref, o_ref in zip(refs[:n], refs[n + 2:]):
                    o_ref[rows, :] = (z * g_ref[...]).astype(bf16)

    row = pl.BlockSpec((tm, d), lambda i: (i, 0))
    vec = pl.BlockSpec((1, d), lambda i: (0, 0))
    return pl.pallas_call(body, grid=(m // tm,),
                          in_specs=[pl.BlockSpec((tm, k), lambda i: (i, 0)), _resident((k, d), lambda i: (0, 0)), row, vec] + [vec] * n,
                          out_specs=[row] * (n + 2), out_shape=[SDS((m, d), f32)] * 2 + [SDS((m, d), bf16)] * n, name=name,
                          compiler_params=_cparams("parallel"))(a, b, h, g_post, *gains)


def mm_resnorm_loss(a, b, h, g_post, tgt, name, tm=512):
    m, k = a.shape
    d = b.shape[1]

    def body(a_ref, b_ref, h_ref, gp_ref, t_ref, dh_ref, dy_ref, dg_ref, l_ref):
        @pl.when(pl.program_id(0) == 0)
        def _():
            dg_ref[...] = jnp.zeros_like(dg_ref)
            l_ref[...] = jnp.zeros_like(l_ref)

        y = _dot(a_ref[...], b_ref[...])
        e = h_ref[...] + _rms(y) * gp_ref[...] - t_ref[...]
        dh = e * (1.0 / d)
        dh_ref[...] = dh
        part = jnp.sum(jnp.sum(e * e, axis=-1, keepdims=True), axis=0, keepdims=True) * (0.5 / d)
        l_ref[...] += jnp.broadcast_to(part, l_ref.shape)
        dy, dg = _rms_bwd_math(y, gp_ref[...], dh)
        dy_ref[...] = dy.astype(bf16)
        dg_ref[...] += dg

    row = pl.BlockSpec((tm, d), lambda i: (i, 0))
    vec = pl.BlockSpec((1, d), lambda i: (0, 0))
    return pl.pallas_call(body, grid=(m // tm,),
                          in_specs=[pl.BlockSpec((tm, k), lambda i: (i, 0)), _resident((k, d), lambda i: (0, 0)), row, vec, row],
                          out_specs=[row, row, vec, pl.BlockSpec((8, LANES), lambda i: (0, 0))],
                          out_shape=[SDS((m, d), f32), SDS((m, d), bf16), SDS((1, d), f32), SDS((8, LANES), f32)], name=name,
                          compiler_params=_cparams("arbitrary"))(a, b, h, g_post, tgt)


def ffn_act_grad(d_y2, w_d, factors, name, tm=1024, tn=1536):
    s, d = d_y2.shape
    ff = w_d.shape[0]
    nb = ff // tn

    def body(a_ref, b_ref, g_ref, u_ref, dg_ref, du_ref):
        av = a_ref[...]
        tc = 256
        for c in range(tn // tc):
            cols = slice(c * tc, (c + 1) * tc)
            da = _dot_nt(av, b_ref[cols, :])
            dg_ref[:, cols] = (da * g_ref[:, cols].astype(f32)).astype(bf16)
            du_ref[:, cols] = (da * u_ref[:, cols].astype(f32)).astype(bf16)

    tile = pl.BlockSpec((tm, tn), lambda j, i: (i, j))
    return pl.pallas_call(body, grid=(nb, s // tm),
                          in_specs=[pl.BlockSpec((tm, d), lambda j, i: (i, 0)), pl.BlockSpec((tn, d), lambda j, i: (j, 0)), tile,
                                    pl.BlockSpec((tm, tn), lambda j, i: (i, nb + j))],
                          out_specs=[tile, tile], out_shape=[SDS((s, ff), bf16)] * 2, name=name,
                          compiler_params=_cparams("parallel", "parallel"))(d_y2, w_d, factors, factors)


def ffn_in_grad(d_g, d_u, w_g, w_u, hmid, dh_out, g_pre, y1, g_post, name, tm=512):
    s, ff = d_g.shape
    d = w_g.shape[0]

    def body(dg_ref, du_ref, wg_ref, wu_ref, hm_ref, dho_ref, gpre_ref, y1_ref, gpost_ref, dhm_ref, dy1_ref, dgpre_ref, dgpost_ref):
        @pl.when(pl.program_id(0) == 0)
        def _():
            dgpre_ref[...] = jnp.zeros_like(dgpre_ref)
            dgpost_ref[...] = jnp.zeros_like(dgpost_ref)

        for r in range(tm // SUB_ROWS):
            rows = slice(r * SUB_ROWS, (r + 1) * SUB_ROWS)
            d_f = _dot_nt(dg_ref[rows, :], wg_ref[...]) + _dot_nt(du_ref[rows, :], wu_ref[...])
            dx, dg1 = _rms_bwd_math(hm_ref[rows, :], gpre_ref[...], d_f)
            dh_mid = dho_ref[rows, :] + dx
            dhm_ref[rows, :] = dh_mid
            dgpre_ref[...] += dg1
            dy1, dg2 = _rms_bwd_math(y1_ref[rows, :], gpost_ref[...], dh_mid)
            dy1_ref[rows, :] = dy1.astype(bf16)
            dgpost_ref[...] += dg2

    row = pl.BlockSpec((tm, d), lambda i: (i, 0))
    vec = pl.BlockSpec((1, d), lambda i: (0, 0))
    wide = pl.BlockSpec((tm, ff), lambda i: (i, 0))
    w_spec = _resident((d, ff), lambda i: (0, 0))
    return pl.pallas_call(body, grid=(s // tm,), in_specs=[wide, wide, w_spec, w_spec, row, row, vec, row, vec],
                          out_specs=[row, row, vec, vec], out_shape=[SDS((s, d), f32), SDS((s, d), bf16), SDS((1, d), f32), SDS((1, d), f32)],
                          name=name, compiler_params=_cparams("arbitrary"))(d_g, d_u, w_g, w_u, hmid, dh_out, g_pre, y1, g_post)


def proj_in_grad(pairs, x, add, name, tm=512, dep=None, below=None):
    s, d = x.shape
    n = len(pairs)
    extra = [] if dep is None else [dep]
    n_below = 0 if below is None else 2

    def body(*refs):
        x_ref, add_ref = refs[3 * n], refs[3 * n + 1]
        below_refs = refs[3 * n + 2:3 * n + 2 + n_below]
        outs = refs[3 * n + 2 + n_below + len(extra):]

        @pl.when(pl.program_id(0) == 0)
        def _():
            for o in outs[1:1 + n] + outs[2 + n:]:
                o[...] = jnp.zeros_like(o)

        xv = x_ref[...]
        dx = add_ref[...]
        for i in range(n):
            a_ref, b_ref, g_ref = refs[3 * i:3 * i + 3]
            dxi, dgi = _rms_bwd_math(xv, g_ref[...], _dot_nt(a_ref[...], b_ref[...]))
            dx = dx + dxi
            outs[1 + i][...] += dgi
        outs[0][...] = dx
        if below is not None:
            dy, dg = _rms_bwd_math(below_refs[0][...], below_refs[1][...], dx)
            outs[1 + n][...] = dy.astype(bf16)
            outs[2 + n][...] += dg

    row = pl.BlockSpec((tm, d), lambda i: (i, 0))
    vec = pl.BlockSpec((1, d), lambda i: (0, 0))
    in_specs, args = [], []
    for a, b, g in pairs:
        k = a.shape[1]
        in_specs += [pl.BlockSpec((tm, k), lambda i: (i, 0)), _resident((d, k), lambda i: (0, 0)), vec]
        args += [a, b, g]
    in_specs += [row, row] + [row, vec][:n_below] + [pl.BlockSpec((8, LANES), lambda i: (0, 0))] * len(extra)
    out_specs = [row] + [vec] * n + [row, vec][:n_below]
    out_shape = [SDS((s, d), f32)] + [SDS((1, d), f32)] * n + [SDS((s, d), bf16), SDS((1, d), f32)][:n_below]
    out = pl.pallas_call(body, grid=(s // tm,), in_specs=in_specs, out_specs=out_specs, out_shape=out_shape, name=name,
                         compiler_params=_cparams("arbitrary"))(*args, x, add, *(below or ()), *extra)
    return (out[0], out[1:1 + n]) + tuple(out[1 + n:])


def ffn_up(f, wg, wu, name, tm=512, tc=256):
    s, d = f.shape
    ff = wg.shape[-1]

    def body(f_ref, wg_ref, wu_ref, fac_ref, act_ref):
        fv = f_ref[...]
        for j in range(ff // tc):
            lo = j * tc
            gg = _dot(fv, wg_ref[:, lo:lo + tc])
            uu = _dot(fv, wu_ref[:, lo:lo + tc])
            sg = _sigmoid(gg)
            silu = gg * sg
            fac_ref[:, lo:lo + tc] = (uu * (sg + silu * (1.0 - sg))).astype(bf16)
            fac_ref[:, ff + lo:ff + lo + tc] = silu.astype(bf16)
            act_ref[:, lo:lo + tc] = (silu * uu).astype(bf16)

    w_spec = _resident((d, ff), lambda i: (0, 0))
    return pl.pallas_call(body, grid=(s // tm,), in_specs=[pl.BlockSpec((tm, d), lambda i: (i, 0)), w_spec, w_spec],
                          out_specs=[pl.BlockSpec((tm, 2 * ff), lambda i: (i, 0)), pl.BlockSpec((tm, ff), lambda i: (i, 0))],
                          out_shape=[SDS((s, 2 * ff), bf16), SDS((s, ff), bf16)], name=name,
                          compiler_params=_cparams("parallel"))(f, wg, wu)


def _gmlp_forward_chunk(u, v, w_refs, bias, ln_g, ln_b):
    gu, tu = _gelu(u)
    gv, tv = _gelu(v)
    mu = jnp.sum(gv, axis=-1, keepdims=True) * (1.0 / MAIN_WIDTH)
    xc = gv - mu
    rstd = lax.rsqrt(jnp.sum(xc * xc, axis=-1, keepdims=True) * (1.0 / MAIN_WIDTH) + LN_EPS)
    xhat = xc * rstd
    vln = xhat * ln_g + ln_b
    row = lax.broadcasted_iota(jnp.int32, (CHUNK, CHUNK), 0)
    col = lax.broadcasted_iota(jnp.int32, (CHUNK, CHUNK), 1)
    s_parts = []
    for g in range(A_GROUPS):
        w = jnp.where(col <= row, w_refs[g], jnp.zeros((), bf16))
        s_parts.append(_dot(w, vln[:, g * CHUNK:(g + 1) * CHUNK].astype(bf16)) + bias[:, g:g + 1])
    return gu, tu, tv, rstd, xhat, vln, s_parts


def gmlp_fwd(proj, ws, bs_t, ln_g, ln_b, name, tm=512, out_width=MAIN_WIDTH):
    s = proj.shape[0]

    def body(u_ref, v_ref, w_ref, b_ref, g_ref, bb_ref, o_ref):
        bias = b_ref[...]
        for c in range(tm // CHUNK):
            rows = slice(c * CHUNK, (c + 1) * CHUNK)
            gu, _, _, _, _, _, s_parts = _gmlp_forward_chunk(u_ref[rows, :], v_ref[rows, :], w_ref, bias, g_ref[...], bb_ref[...])
            for g in range(A_GROUPS):
                cols = slice(g * CHUNK, (g + 1) * CHUNK)
                o_ref[rows, cols] = (gu[:, cols] * s_parts[g]).astype(bf16)

    vec = pl.BlockSpec((1, MAIN_WIDTH), lambda i: (0, 0))
    return pl.pallas_call(
        body, grid=(s // tm,),
        in_specs=[pl.BlockSpec((tm, MAIN_WIDTH), lambda i: (i, 0)), pl.BlockSpec((tm, MAIN_WIDTH), lambda i: (i, 1)),
                  pl.BlockSpec((A_GROUPS, CHUNK, CHUNK), lambda i: (0, 0, 0)), pl.BlockSpec((CHUNK, A_GROUPS), lambda i: (0, 0)), vec, vec],
        out_specs=pl.BlockSpec((tm, MAIN_WIDTH), lambda i: (i, 0)), out_shape=SDS((s, out_width), bf16), name=name,
        compiler_params=_cparams("parallel"))(proj, proj, ws, bs_t, ln_g, ln_b)


def gmlp_bwd(proj, d_mixed, ws, ws_t, bs_t, ln_g, ln_b, name, tm=512, out_width=2 * MAIN_WIDTH):
    s = proj.shape[0]

    def body(u_ref, v_ref, dm_ref, w_ref, wt_ref, b_ref, g_ref, bb_ref, duv_ref, dw_ref, db_ref, dg_ref, dbb_ref):
        @pl.when(pl.program_id(0) == 0)
        def _():
            dw_ref[...] = jnp.zeros_like(dw_ref)
            db_ref[...] = jnp.zeros_like(db_ref)
            dg_ref[...] = jnp.zeros_like(dg_ref)
            dbb_ref[...] = jnp.zeros_like(dbb_ref)

        bias = b_ref[...]
        ln_gv = g_ref[...]
        row = lax.broadcasted_iota(jnp.int32, (CHUNK, CHUNK), 0)
        col = lax.broadcasted_iota(jnp.int32, (CHUNK, CHUNK), 1)
        lane = lax.broadcasted_iota(jnp.int32, (CHUNK, LANES), 1)
        for c in range(tm // CHUNK):
            rows = slice(c * CHUNK, (c + 1) * CHUNK)
            u = u_ref[rows, :]
            v = v_ref[rows, :]
            gu, tu, tv, rstd, xhat, vln, s_parts = _gmlp_forward_chunk(u, v, w_ref, bias, ln_gv, bb_ref[...])
            dm = dm_ref[rows, :]
            d_vln_parts = []
            d_gu_parts = []
            db_acc = jnp.zeros((CHUNK, LANES), f32)
            for g in range(A_GROUPS):
                cols = slice(g * CHUNK, (g + 1) * CHUNK)
                dmg = dm[:, cols]
                d_gu_parts.append(dmg * s_parts[g])
                d_s = dmg * gu[:, cols]
                db_acc = db_acc + jnp.where(lane == g, jnp.sum(d_s, axis=-1, keepdims=True), 0.0)
                d_sb = d_s.astype(bf16)
                dw_ref[g] += jnp.where(col <= row, _dot_nt(d_sb, vln[:, cols].astype(bf16)), 0.0)
                wt = jnp.where(row <= col, wt_ref[g], jnp.zeros((), bf16))
                d_vln_parts.append(_dot(wt, d_sb))
            db_ref[...] += db_acc
            d_vln = jnp.concatenate(d_vln_parts, axis=-1)
            d_gu = jnp.concatenate(d_gu_parts, axis=-1)
            dg_ref[...] += jnp.sum(d_vln * xhat, axis=0, keepdims=True)
            dbb_ref[...] += jnp.sum(d_vln, axis=0, keepdims=True)
            dxh = d_vln * ln_gv
            m1 = jnp.sum(dxh, axis=-1, keepdims=True) * (1.0 / MAIN_WIDTH)
            m2 = jnp.sum(dxh * xhat, axis=-1, keepdims=True) * (1.0 / MAIN_WIDTH)
            d_gv = rstd * (dxh - m1 - xhat * m2)
            duv_ref[rows, :MAIN_WIDTH] = (d_gu * _gelu_grad(u, tu)).astype(bf16)
            duv_ref[rows, MAIN_WIDTH:] = (d_gv * _gelu_grad(v, tv)).astype(bf16)

    vec = pl.BlockSpec((1, MAIN_WIDTH), lambda i: (0, 0))
    wspec = pl.BlockSpec((A_GROUPS, CHUNK, CHUNK), lambda i: (0, 0, 0))
    return pl.pallas_call(
        body, grid=(s // tm,),
        in_specs=[pl.BlockSpec((tm, MAIN_WIDTH), lambda i: (i, 0)), pl.BlockSpec((tm, MAIN_WIDTH), lambda i: (i, 1)),
                  pl.BlockSpec((tm, MAIN_WIDTH), lambda i: (i, 0)), wspec, wspec, pl.BlockSpec((CHUNK, A_GROUPS), lambda i: (0, 0)), vec, vec],
        out_specs=[pl.BlockSpec((tm, 2 * MAIN_WIDTH), lambda i: (i, 0)), wspec, pl.BlockSpec((CHUNK, LANES), lambda i: (0, 0)), vec, vec],
        out_shape=[SDS((s, out_width), bf16), SDS((A_GROUPS, CHUNK, CHUNK), f32), SDS((CHUNK, LANES), f32),
                   SDS((1, MAIN_WIDTH), f32), SDS((1, MAIN_WIDTH), f32)],
        name=name, compiler_params=_cparams("arbitrary"))(proj, proj, d_mixed, ws, ws_t, bs_t, ln_g, ln_b)


def _head_mask(width, h):
    lane = lax.broadcasted_iota(jnp.int32, (1, width), 1)
    return (lane >= h * HEAD_DIM) & (lane < (h + 1) * HEAD_DIM)


def mem_attn_fwd(proj, q_block, kv, into, name, tm=512):
    s = proj.shape[0]
    n_mem = kv.shape[0]
    out_block = into.shape[1] // MEM_WIDTH - 1

    def body(q_ref, kv_ref, into_ref, o_ref):
        q = q_ref[...].astype(f32)
        k = kv_ref[:, :MEM_WIDTH].astype(bf16)
        v = kv_ref[:, MEM_WIDTH:].astype(bf16)
        out = jnp.zeros((tm, MEM_WIDTH), f32)
        for h in range(MEM_HEADS):
            msk = _head_mask(MEM_WIDTH, h)
            qh = jnp.where(msk, q, 0.0).astype(bf16)
            sc = _dot_nt(qh, k) * ATT_SCALE
            e = jnp.exp(sc - jnp.max(sc, axis=-1, keepdims=True))
            p = e / jnp.sum(e, axis=-1, keepdims=True)
            out = jnp.where(msk, _dot(p.astype(bf16), v), out)
        o_ref[...] = out.astype(bf16)

    return pl.pallas_call(body, grid=(s // tm,),
                          in_specs=[pl.BlockSpec((tm, MEM_WIDTH), lambda i: (i, q_block)), pl.BlockSpec((n_mem, 2 * MEM_WIDTH), lambda i: (0, 0)), _ANY],
                          out_specs=pl.BlockSpec((tm, MEM_WIDTH), lambda i: (i, out_block)), out_shape=SDS(into.shape, bf16), name=name,
                          input_output_aliases={2: 0}, compiler_params=_cparams("parallel"))(proj, kv, into)


def mem_attn_bwd(proj, q_block, kv, d_mixed, into, name, tm=512):
    s = proj.shape[0]
    n_mem = kv.shape[0]
    out_block = into.shape[1] // MEM_WIDTH - 1

    def body(q_ref, kv_ref, do_ref, into_ref, dq_ref, dkv_ref):
        @pl.when(pl.program_id(0) == 0)
        def _():
            dkv_ref[...] = jnp.zeros_like(dkv_ref)

        q = q_ref[...].astype(f32)
        do = do_ref[...]
        k = kv_ref[:, :MEM_WIDTH].astype(bf16)
        v = kv_ref[:, MEM_WIDTH:].astype(bf16)
        dq = jnp.zeros((tm, MEM_WIDTH), f32)
        dk = jnp.zeros((n_mem, MEM_WIDTH), f32)
        dv = jnp.zeros((n_mem, MEM_WIDTH), f32)
        for h in range(MEM_HEADS):
            msk = _head_mask(MEM_WIDTH, h)
            qh = jnp.where(msk, q, 0.0).astype(bf16)
            doh = jnp.where(msk, do, 0.0).astype(bf16)
            sc = _dot_nt(qh, k) * ATT_SCALE
            e = jnp.exp(sc - jnp.max(sc, axis=-1, keepdims=True))
            p = e / jnp.sum(e, axis=-1, keepdims=True)
            dp = _dot_nt(doh, v)
            ds = p * (dp - jnp.sum(dp * p, axis=-1, keepdims=True))
            dsb = (ds * ATT_SCALE).astype(bf16)
            dq = jnp.where(msk, _dot(dsb, k), dq)
            dk = dk + _dot_tn(dsb, qh)
            dv = dv + _dot_tn(p.astype(bf16), doh)
        dq_ref[...] = dq.astype(bf16)
        dkv_ref[:, :MEM_WIDTH] += dk
        dkv_ref[:, MEM_WIDTH:] += dv

    return pl.pallas_call(
        body, grid=(s // tm,),
        in_specs=[pl.BlockSpec((tm, MEM_WIDTH), lambda i: (i, q_block)), pl.BlockSpec((n_mem, 2 * MEM_WIDTH), lambda i: (0, 0)),
                  pl.BlockSpec((tm, MEM_WIDTH), lambda i: (i, MAIN_WIDTH // MEM_WIDTH)), _ANY],
        out_specs=[pl.BlockSpec((tm, MEM_WIDTH), lambda i: (i, out_block)), pl.BlockSpec((n_mem, 2 * MEM_WIDTH), lambda i: (0, 0))],
        out_shape=[SDS(into.shape, bf16), SDS((n_mem, 2 * MEM_WIDTH), f32)], name=name,
        input_output_aliases={3: 0}, compiler_params=_cparams("arbitrary"))(proj, kv, d_mixed, into)


def _tri(t, upper):
    r = lax.broadcasted_iota(jnp.int32, (t, t), 0)
    c = lax.broadcasted_iota(jnp.int32, (t, t), 1)
    return ((r <= c) if upper else (r >= c)).astype(f32)


def fgate_fwd(z_t, b, name, t=512):
    hh, s = z_t.shape

    def body(z_ref, b_ref, c_ref):
        u = _tri(t, True)
        carry = jnp.zeros((hh, 1), f32)
        for blk in range(s // t):
            x = z_ref[:, blk * t:(blk + 1) * t] + b_ref[...]
            logf = jnp.minimum(x, 0.0) - jnp.log(1.0 + jnp.exp(-jnp.abs(x)))
            y = jnp.dot(logf, u, precision=lax.Precision.HIGHEST, preferred_element_type=f32) + carry
            c_ref[:, blk * t:(blk + 1) * t] = y
            carry = y[:, t - 1:t]

    return pl.pallas_call(body, out_shape=SDS((hh, s), f32), name=name, compiler_params=_cparams())(z_t, b)


def fgate_bwd(dc_t, z_t, b, name, t=512):
    hh, s = z_t.shape

    def body(dc_ref, z_ref, b_ref, dz_ref, db_ref):
        low = _tri(t, False)
        carry = jnp.zeros((hh, 1), f32)
        total = jnp.zeros((hh, 1), f32)
        for blk in reversed(range(s // t)):
            cols = slice(blk * t, (blk + 1) * t)
            y = jnp.dot(dc_ref[:, cols], low, precision=lax.Precision.HIGHEST, preferred_element_type=f32) + carry
            carry = y[:, 0:1]
            dz = y * _sigmoid(-(z_ref[:, cols] + b_ref[...]))
            dz_ref[:, cols] = dz
            total = total + jnp.sum(dz, axis=-1, keepdims=True)
        db_ref[...] = jnp.broadcast_to(total, db_ref.shape)

    return pl.pallas_call(body, out_shape=[SDS((hh, s), f32), SDS((hh, LANES), f32)], name=name,
                          compiler_params=_cparams())(dc_t, z_t, b)


def _pair_masks():
    lane = lax.broadcasted_iota(jnp.int32, (1, LANES), 1)
    return [lane < HEAD_DIM, lane >= HEAD_DIM]


def _tile_base(cr_ref, hh, lo):
    return cr_ref[hh:hh + 1, pl.ds(lo, LANES)][:, 0:1]


def fox_fwd(q, kv, c_row, name, tq=512, out_width=MAIN_WIDTH):
    s = kv.shape[0]
    nq = s // tq

    def body(q_ref, k_ref, v_ref, cr_ref, o_ref, lse_ref, ob_ref):
        i = pl.program_id(1)
        qv = q_ref[...]
        masks = _pair_masks()
        row = lax.broadcasted_iota(jnp.int32, (tq, tq), 0)
        col = lax.broadcasted_iota(jnp.int32, (tq, tq), 1)
        qh = [jnp.where(masks[hh], qv, jnp.zeros((), bf16)) * ATT_SCALE for hh in range(2)]
        ct = [_tile_base(cr_ref, hh, pl.multiple_of(i * tq, tq)) for hh in range(2)]

        def block(j, carry, diag):
            lo = pl.multiple_of(j * tq, tq)
            ks = k_ref[pl.ds(lo, tq), :]
            vs = v_ref[pl.ds(lo, tq), :]
            out = []
            for hh in range(2):
                m, l, acc = carry[hh]
                sc = _dot_nt(qh[hh], ks) + (ct[hh] - cr_ref[hh:hh + 1, pl.ds(lo, tq)])
                if diag:
                    sc = jnp.where(col <= row, sc, -jnp.inf)
                m_new = jnp.maximum(m, jnp.max(sc, axis=-1, keepdims=True))
                alpha = jnp.exp(m - m_new)
                p = jnp.exp(sc - m_new)
                l = alpha * l + jnp.sum(p, axis=-1, keepdims=True)
                p_hi = p.astype(bf16)
                p_lo = (p - p_hi.astype(f32)).astype(bf16)
                acc = alpha * acc + (_dot(p_hi, vs) + _dot(p_lo, vs))
                out.append((m_new, l, acc))
            return tuple(out)

        init = (jnp.full((tq, 1), -jnp.inf, f32), jnp.zeros((tq, 1), f32), jnp.zeros((tq, LANES), f32))
        carry = lax.fori_loop(0, i, functools.partial(block, diag=False), (init, init))
        res = [(acc / l, m + jnp.log(l)) for m, l, acc in block(i, carry, True)]
        out = jnp.where(masks[0], res[0][0], res[1][0])
        o_ref[...] = out
        ob_ref[...] = out.astype(bf16)
        lse_ref[...] = jnp.where(masks[0], res[0][1], res[1][1])

    return pl.pallas_call(
        body, grid=(FOX_PAIRS, nq),
        in_specs=[pl.BlockSpec((tq, LANES), lambda p, i: (i, p)), pl.BlockSpec((s, LANES), lambda p, i: (0, p)),
                  pl.BlockSpec((s, LANES), lambda p, i: (0, FOX_PAIRS + p)), pl.BlockSpec((None, 2, s), lambda p, i: (p, 0, 0))],
        out_specs=[pl.BlockSpec((tq, LANES), lambda p, i: (i, p)), pl.BlockSpec((None, tq, LANES), lambda p, i: (p, i, 0)),
                   pl.BlockSpec((tq, LANES), lambda p, i: (i, p))],
        out_shape=[SDS((s, MAIN_WIDTH), f32), SDS((FOX_PAIRS, s, LANES), f32), SDS((s, out_width), bf16)], name=name,
        compiler_params=_cparams("parallel", "parallel"))(q, kv, kv, c_row)


def fox_bwd(q, kv, d_mixed, o, lse, c_row, name, tq=512):
    s = kv.shape[0]
    nq = s // tq

    def body(q_ref, k_ref, v_ref, do_ref, o_ref, lse_ref, cr_ref, dq_ref, dk_ref, dv_ref, dc_ref):
        j = pl.program_id(1)

        @pl.when(j == 0)
        def _():
            dq_ref[...] = jnp.zeros_like(dq_ref)

        masks = _pair_masks()
        sub = lax.broadcasted_iota(jnp.int32, (LANES, 1), 0)
        sub_masks = [sub < HEAD_DIM, sub >= HEAD_DIM]
        row = lax.broadcasted_iota(jnp.int32, (tq, tq), 0)
        col = lax.broadcasted_iota(jnp.int32, (tq, tq), 1)
        kj = k_ref[...]
        vj = v_ref[...]
        lo_j = pl.multiple_of(j * tq, tq)

        def block(i, carry, diag):
            dk_t, dv_t, dc0, dc1 = carry
            dcs = [dc0, dc1]
            lo = pl.multiple_of(i * tq, tq)
            qi = q_ref[pl.ds(lo, tq), :]
            qi = qi * ATT_SCALE
            qt_i = qi.T
            doi = do_ref[pl.ds(lo, tq), :]
            dot_i = doi.astype(bf16).T
            prod = doi.astype(bf16).astype(f32) * o_ref[pl.ds(lo, tq), :]
            lse_i = lse_ref[pl.ds(lo, tq), :]
            dq_i = jnp.zeros((tq, LANES), f32)
            for hh in range(2):
                qh = jnp.where(masks[hh], qi, jnp.zeros((), bf16))
                doh = jnp.where(masks[hh], doi, 0.0).astype(bf16)
                delta = jnp.sum(jnp.where(masks[hh], prod, 0.0), axis=-1, keepdims=True)
                sc = _dot_nt(qh, kj) + (_tile_base(cr_ref, hh, lo) - cr_ref[hh:hh + 1, pl.ds(lo_j, tq)])
                p = jnp.exp(sc - lse_i[:, hh * HEAD_DIM:hh * HEAD_DIM + 1])
                if diag:
                    p = jnp.where(col <= row, p, 0.0)
                dv_t = dv_t + _dot(jnp.where(sub_masks[hh], dot_i, jnp.zeros((), bf16)), p.astype(bf16))
                ds = p * (_dot_nt(doh, vj) - delta)
                dcs[hh] = dcs[hh] + jnp.sum(ds, axis=0, keepdims=True)
                dsb = ds.astype(bf16)
                dq_i = jnp.where(masks[hh], _dot(dsb, kj), dq_i)
                dk_t = dk_t + _dot(jnp.where(sub_masks[hh], qt_i, jnp.zeros((), bf16)), dsb)
            dq_ref[pl.ds(lo, tq), :] += dq_i * ATT_SCALE
            return dk_t, dv_t, dcs[0], dcs[1]

        zero = jnp.zeros((LANES, tq), f32)
        zrow = jnp.zeros((1, tq), f32)
        carry = block(j, (zero, zero, zrow, zrow), True)
        dk_t, dv_t, dc0, dc1 = lax.fori_loop(j + 1, nq, functools.partial(block, diag=False), carry)
        dk_ref[...] = dk_t.T.astype(bf16)
        dv_ref[...] = dv_t.T.astype(bf16)
        dc_ref[0:1, :] = -dc0
        dc_ref[1:2, :] = -dc1

    full = lambda p, j: (0, p)
    tile = lambda p, j: (j, p)
    return pl.pallas_call(
        body, grid=(FOX_PAIRS, nq),
        in_specs=[pl.BlockSpec((s, LANES), full), pl.BlockSpec((tq, LANES), tile), pl.BlockSpec((tq, LANES), lambda p, j: (j, FOX_PAIRS + p)),
                  pl.BlockSpec((s, LANES), full), pl.BlockSpec((s, LANES), full), pl.BlockSpec((None, s, LANES), lambda p, j: (p, 0, 0)),
                  pl.BlockSpec((None, 2, s), lambda p, j: (p, 0, 0))],
        out_specs=[pl.BlockSpec((s, LANES), full), pl.BlockSpec((tq, LANES), tile), pl.BlockSpec((tq, LANES), tile),
                   pl.BlockSpec((None, 2, tq), lambda p, j: (p, 0, j))],
        out_shape=[SDS((s, MAIN_WIDTH), f32), SDS((s, MAIN_WIDTH), bf16), SDS((s, MAIN_WIDTH), bf16), SDS((FOX_PAIRS, 2, s), f32)],
        name=name, compiler_params=_cparams("parallel", "arbitrary"))(q, kv, kv, d_mixed, o, lse, c_row)


def adamw(w, g, m, v, name, tr=256):
    r, c = w.shape
    tr = min(tr, r)
    assert r % tr == 0, (name, r, tr)
    c1 = 1.0 / (1.0 - ADAM_B1 ** ADAM_STEP)
    c2 = 1.0 / (1.0 - ADAM_B2 ** ADAM_STEP)

    def body(w_ref, g_ref, m_ref, v_ref, d_ref, mo_ref, vo_ref):
        gv = g_ref[...]
        mn = ADAM_B1 * m_ref[...] + (1.0 - ADAM_B1) * gv
        vn = ADAM_B2 * v_ref[...] + (1.0 - ADAM_B2) * gv * gv
        mo_ref[...] = mn
        vo_ref[...] = vn
        d_ref[...] = -ADAM_LR * ((mn * c1) / (jnp.sqrt(vn * c2) + ADAM_EPS) + ADAM_WD * w_ref[...])

    spec = pl.BlockSpec((tr, c), lambda i: (i, 0))
    return pl.pallas_call(body, grid=(r // tr,), in_specs=[spec] * 4, out_specs=[spec] * 3, out_shape=[SDS((r, c), f32)] * 3,
                          name=name, compiler_params=_cparams("parallel"))(w, g, m, v)


def sum_leading(x, name, out_dtype=f32, tr=None):
    n, r, c = x.shape
    tr = tr or r
    assert r % tr == 0

    def body(x_ref, o_ref):
        acc = x_ref[0].astype(f32)
        for k in range(1, n):
            acc = acc + x_ref[k].astype(f32)
        o_ref[...] = acc.astype(out_dtype)

    return pl.pallas_call(body, grid=(r // tr,), in_specs=[pl.BlockSpec((n, tr, c), lambda i: (0, i, 0))],
                          out_specs=pl.BlockSpec((tr, c), lambda i: (i, 0)), out_shape=SDS((r, c), out_dtype), name=name,
                          compiler_params=_cparams("parallel"))(x)


_ANY = pl.BlockSpec(memory_space=pl.ANY)
_DMA = pltpu.SemaphoreType.DMA


_HBM = pl.BlockSpec(memory_space=pltpu.HBM)
_SEM = pl.BlockSpec(memory_space=pltpu.SEMAPHORE)
_EFFECT = pltpu.SideEffectType.DATAFLOW_SIDE_EFFECTING
_FLIPS = [(0, 0, 1), (1, 0, 0), (0, 1, 0), (1, 1, 0), (1, 0, 1), (0, 1, 1), (1, 1, 1)]


def _me():
    return lax.axis_index("x"), lax.axis_index("y"), lax.axis_index("c")


def _peers():
    mx, my, mc = _me()
    return [(jnp.bitwise_xor(mx, fx), jnp.bitwise_xor(my, fy), jnp.bitwise_xor(mc, fc)) for fx, fy, fc in _FLIPS]


def _index(dev):
    return 4 * dev[0] + 2 * dev[1] + dev[2]


def _win(ref, axis, k, size, count=1):
    idx = [slice(None)] * len(ref.shape)
    idx[axis] = pl.ds(k * size, count * size)
    return ref.at[tuple(idx)]


def _hbm(a):
    return pltpu.with_memory_space_constraint(a, pltpu.HBM)


def _exchange_start(srcs, lands, copies_of, name):
    n = len(srcs)

    def body(*refs):
        src = refs[:n]
        send_sems, recv_sems, self_sems = refs[2 * n:2 * n + 3]
        land = refs[3 * n + 3:4 * n + 3]
        token = refs[4 * n + 3]
        me = _index(_me())
        for a in range(n):
            for s_ref, d_ref, peer in copies_of(a, src[a], land[a], me):
                if peer is None:
                    pltpu.make_async_copy(s_ref, d_ref, self_sems.at[a]).start()
                else:
                    pltpu.make_async_remote_copy(src_ref=s_ref, dst_ref=d_ref, send_sem=send_sems.at[a], recv_sem=recv_sems.at[a],
                                                 device_id=peer, device_id_type=MESH).start()
        token[...] = jnp.zeros_like(token)

    outs = pl.pallas_call(
        body, name=name,
        out_shape=(_DMA((n,)), _DMA((n,)), _DMA((n,)), *[pltpu.HBM(s.shape, s.dtype) for s in srcs],
                   *[pltpu.HBM(l.shape, l.dtype) for l in lands], SDS((8, LANES), f32)),
        in_specs=[_HBM] * (2 * n), out_specs=(_SEM, _SEM, _SEM, *[_HBM] * (2 * n), pl.BlockSpec(memory_space=pltpu.VMEM)),
        input_output_aliases={i: 3 + i for i in range(2 * n)},
        compiler_params=pltpu.CompilerParams(has_side_effects=_EFFECT),
    )(*[_hbm(s) for s in srcs], *[_hbm(lax.empty(l.shape, l.dtype)) for l in lands])
    return dict(sems=outs[:3], srcs=list(outs[3:3 + n]), lands=list(outs[3 + n:3 + 2 * n]), token=outs[3 + 2 * n])


def _exchange_wait(started, waits_of, after, name, which=None):
    which = list(range(len(started["srcs"]))) if which is None else which
    srcs, lands = [started["srcs"][a] for a in which], [started["lands"][a] for a in which]
    n = len(which)

    def body(*refs):
        src = refs[:n]
        land = refs[n:2 * n]
        send_sems, recv_sems, self_sems = refs[2 * n:2 * n + 3]
        me = _index(_me())
        for pos, a in enumerate(which):
            seven, (s_ref, d_ref) = waits_of(a, src[pos], land[pos], me)
            both = pltpu.make_async_remote_copy(src_ref=seven, dst_ref=seven, send_sem=send_sems.at[a], recv_sem=recv_sems.at[a],
                                                device_id=_me(), device_id_type=MESH)
            both.wait_send()
            both.wait_recv()
            pltpu.make_async_copy(s_ref, d_ref, self_sems.at[a]).wait()

    outs = pl.pallas_call(
        body, name=name, out_shape=tuple(pltpu.HBM(t.shape, t.dtype) for t in srcs + lands),
        in_specs=[_HBM] * (2 * n) + [_SEM] * 3 + [_ANY], out_specs=tuple([_HBM] * (2 * n)),
        input_output_aliases={i: i for i in range(2 * n)},
        compiler_params=pltpu.CompilerParams(has_side_effects=_EFFECT),
    )(*srcs, *lands, *started["sems"], after)
    return list(outs[n:])


def gather_start(locs, axes, name):
    lands = [SDS(tuple(N_DEV * d if i == ax else d for i, d in enumerate(l.shape)), l.dtype) for l, ax in zip(locs, axes)]

    def copies_of(a, src, land, me):
        mine = _win(land, axes[a], me, src.shape[axes[a]])
        return [(src, mine, peer) for peer in _peers()] + [(src, mine, None)]

    return _exchange_start(locs, lands, copies_of, name)


def gather_wait(started, axes, after, name, which=None):
    def waits_of(a, src, land, me):
        size = src.shape[axes[a]]
        return _win(land, axes[a], 0, size, N_DEV - 1), (src, _win(land, axes[a], me, size))

    return _exchange_wait(started, waits_of, after, name, which)


def scatter_start(grads, axes, name):
    lands = [SDS((N_DEV,) + tuple(d // N_DEV if i == ax else d for i, d in enumerate(g.shape)), g.dtype) for g, ax in zip(grads, axes)]

    def copies_of(a, src, land, me):
        size = src.shape[axes[a]] // N_DEV
        out = [(_win(src, axes[a], _index(peer), size), land.at[me], peer) for peer in _peers()]
        return out + [(_win(src, axes[a], me, size), land.at[me], None)]

    return _exchange_start(grads, lands, copies_of, name)


def scatter_wait(started, axes, after, name):
    def waits_of(a, src, land, me):
        size = src.shape[axes[a]] // N_DEV
        return land.at[pl.ds(0, N_DEV - 1)], (_win(src, axes[a], me, size), land.at[me])

    return _exchange_wait(started, waits_of, after, name)


def _row_tile(rows, cap=512):
    return max(t for t in range(8, min(rows, cap) + 1, 8) if rows % t == 0)


_SMALL = [
    ("ln_mix_pre", (2, 1024)), ("ln_mix_post", (2, 1024)), ("ln_ffn_pre", (2, 1024)), ("ln_ffn_post", (2, 1024)),
    ("ln_mem", (2, 1024)), ("w_spatial", (1, 6, 128, 128)), ("b_spatial", (1, 6, 128)), ("ln_shared", (1024,)),
    ("b_forget", (12,)), ("ln_v_g", (1, 768)), ("ln_v_b", (1, 768)),
]
_SMALL_TILE = 8 * LANES


def _small_rows(shape):
    return -(-math.prod(shape) // _SMALL_TILE) * 8


def _pack_small(vals, shapes):
    parts = []
    for name, shape in shapes:
        flat = vals[name].reshape(-1).astype(f32)
        rows = _small_rows(shape)
        parts.append(jnp.pad(flat, (0, rows * LANES - flat.shape[0])).reshape(rows, LANES))
    return jnp.concatenate(parts, axis=0)


def _unpack_small(buf, shapes):
    out = {}
    lo = 0
    for name, shape in shapes:
        rows = _small_rows(shape)
        out[name] = buf[lo:lo + rows].reshape(-1)[:math.prod(shape)].reshape(shape)
        lo += rows
    return out


def kernel(x, mem, ln_mix_pre, ln_mix_post, ln_ffn_pre, ln_ffn_post, ln_mem, w_mem_kv, w_out, w_ffn_gate, w_ffn_up, w_ffn_down, w_in_a, w_spatial, b_spatial, ln_v_g, ln_v_b, ln_shared, w_shared_kv, b_forget, w_in_b, loss_target, m_ln_mix_pre, m_ln_mix_post, m_ln_ffn_pre, m_ln_ffn_post, m_ln_mem, m_w_mem_kv, m_w_out, m_w_ffn_gate, m_w_ffn_up, m_w_ffn_down, m_w_in_a, m_w_spatial, m_b_spatial, m_ln_v_g, m_ln_v_b, m_ln_shared, m_w_shared_kv, m_b_forget, m_w_in_b, v_ln_mix_pre, v_ln_mix_post, v_ln_ffn_pre, v_ln_ffn_post, v_ln_mem, v_w_mem_kv, v_w_out, v_w_ffn_gate, v_w_ffn_up, v_w_ffn_down, v_w_in_a, v_w_spatial, v_b_spatial, v_ln_v_g, v_ln_v_b, v_ln_shared, v_w_shared_kv, v_b_forget, v_w_in_b):
    weights = dict(ln_mix_pre=ln_mix_pre, ln_mix_post=ln_mix_post, ln_ffn_pre=ln_ffn_pre, ln_ffn_post=ln_ffn_post, ln_mem=ln_mem,
                   w_mem_kv=w_mem_kv, w_out=w_out, w_ffn_gate=w_ffn_gate, w_ffn_up=w_ffn_up, w_ffn_down=w_ffn_down, w_in_a=w_in_a,
                   w_spatial=w_spatial, b_spatial=b_spatial, ln_v_g=ln_v_g, ln_v_b=ln_v_b, ln_shared=ln_shared,
                   w_shared_kv=w_shared_kv, b_forget=b_forget, w_in_b=w_in_b)
    mom_m = dict(ln_mix_pre=m_ln_mix_pre, ln_mix_post=m_ln_mix_post, ln_ffn_pre=m_ln_ffn_pre, ln_ffn_post=m_ln_ffn_post, ln_mem=m_ln_mem,
                 w_mem_kv=m_w_mem_kv, w_out=m_w_out, w_ffn_gate=m_w_ffn_gate, w_ffn_up=m_w_ffn_up, w_ffn_down=m_w_ffn_down, w_in_a=m_w_in_a,
                 w_spatial=m_w_spatial, b_spatial=m_b_spatial, ln_v_g=m_ln_v_g, ln_v_b=m_ln_v_b, ln_shared=m_ln_shared,
                 w_shared_kv=m_w_shared_kv, b_forget=m_b_forget, w_in_b=m_w_in_b)
    mom_v = dict(ln_mix_pre=v_ln_mix_pre, ln_mix_post=v_ln_mix_post, ln_ffn_pre=v_ln_ffn_pre, ln_ffn_post=v_ln_ffn_post, ln_mem=v_ln_mem,
                 w_mem_kv=v_w_mem_kv, w_out=v_w_out, w_ffn_gate=v_w_ffn_gate, w_ffn_up=v_w_ffn_up, w_ffn_down=v_w_ffn_down, w_in_a=v_w_in_a,
                 w_spatial=v_w_spatial, b_spatial=v_b_spatial, ln_v_g=v_ln_v_g, ln_v_b=v_ln_v_b, ln_shared=v_ln_shared,
                 w_shared_kv=v_w_shared_kv, b_forget=v_b_forget, w_in_b=v_w_in_b)
    names = list(weights)
    mx, my, mc = lax.axis_index("x"), lax.axis_index("y"), lax.axis_index("c")
    me = 4 * mx + 2 * my + mc

    h0 = x[0]
    mem0 = mem[0]
    tgt = loss_target[0]
    seq = h0.shape[0]

    vec = lambda a: a.reshape(1, -1)
    pad_to = lambda a, axis, size: jnp.pad(a, [(0, size - a.shape[i] if i == axis else 0) for i in range(a.ndim)])

    def after(tok, a):
        return a + tok[0, 0].astype(a.dtype)

    lnv_loc = pad_to(jnp.concatenate([ln_v_g, ln_v_b], axis=0), 0, 8)
    st_a = gather_start([w_in_a.astype(bf16), pad_to(lnv_loc, 1, LANES)[None]], [0, 0], "gather_a_start")
    mix_locs = lambda l, tok: [after(tok, w_mem_kv[l]).astype(bf16), w_out[l].astype(bf16)]

    def ffn_gather_start(l, tok):
        gate_up = gather_start([pad_to(after(tok, w_ffn_gate[l]).astype(bf16), 1, FF_SHARD_PAD),
                                pad_to(w_ffn_up[l].astype(bf16), 1, FF_SHARD_PAD)], [1, 1], f"gather_gate_up{l}_start")
        down = gather_start([pad_to(after(gate_up["token"], w_ffn_down[l]).astype(bf16), 0, FF_SHARD_PAD)], [0], f"gather_down{l}_start")
        return gate_up, down

    st_b = [gather_start(mix_locs(0, st_a["token"]), [0, 0], "gather_b0_start"), None]
    st_c = ffn_gather_start(0, st_b[0]["token"])
    st_d = gather_start([after(st_c[1]["token"], w_in_b[0]).astype(bf16), pad_to(w_shared_kv.astype(bf16), 1, KV_PAD)], [0, 0],
                        "gather_d_start")
    st_b[1] = gather_start(mix_locs(1, st_d["token"]), [0, 0], "gather_b1_start")
    st_e = ffn_gather_start(1, st_b[1]["token"])
    ws = w_spatial[0].astype(bf16)
    ws_t = ws.transpose(0, 2, 1)
    bs_t = b_spatial[0].T

    (a0,) = rms_fwd(h0, [after(st_e[1]["token"], vec(ln_mix_pre[0]))], "a0_norm")
    w_in_a8, lnv8 = gather_wait(st_a, [0, 0], a0, "gather_a_wait")
    w_in_a_full = w_in_a8.transpose(1, 0, 2).reshape(D_MODEL, -1)
    lnv_g = lnv8[:, 0, :MAIN_WIDTH // N_DEV].reshape(1, MAIN_WIDTH)
    lnv_b = lnv8[:, 1, :MAIN_WIDTH // N_DEV].reshape(1, MAIN_WIDTH)
    proj0 = mm(a0, w_in_a_full, "proj0", tn=896)
    main0 = gmlp_fwd(proj0, ws, bs_t, lnv_g, lnv_b, "gmlp_fwd", out_width=D_MODEL)
    w_mkv, w_o = [None, None], [None, None]
    w_mkv[0], w_o[0] = gather_wait(st_b[0], [0, 0], main0, "gather_b0_wait")
    (memn0,) = rms_fwd(mem0, [vec(ln_mem[0])], "mem0_norm")
    kvm0 = mm(memn0, w_mkv[0], "kvm0")
    mixed0 = mem_attn_fwd(proj0, 2 * MAIN_WIDTH // MEM_WIDTH, kvm0, main0, "mem_attn0")
    y1_0, hmid0, f0 = mm_resnorm(mixed0, w_o[0], h0, vec(ln_mix_post[0]), [vec(ln_ffn_pre[0])], "mix_out0")
    w_g0, w_u0 = gather_wait(st_c[0], [1, 1], f0, "gather_gate_up0_wait")
    gu0, act0 = ffn_up(f0, w_g0, w_u0, "ffn_up0")
    (w_d0,) = gather_wait(st_c[1], [0], act0, "gather_down0_wait")
    y2_0, h1, a1, sin1 = mm_resnorm(act0, w_d0, hmid0, vec(ln_ffn_post[0]), [vec(ln_mix_pre[1]), vec(ln_shared)], "ffn_down0")

    w_inb, w_kv = gather_wait(st_d, [0, 0], sin1, "gather_d_wait")
    kvb = mm(sin1, w_kv, "kv_shared", out_dtype=bf16, tn=MAIN_WIDTH, ncols=2 * MAIN_WIDTH)
    zf = mm(sin1, w_kv, "forget_logits", tn=256, col0=2 * MAIN_WIDTH, ncols=256)
    qb = mm(a1, w_inb, "proj1", out_dtype=bf16)
    z_t = jnp.pad(zf[:, :FOX_HEADS].T, ((0, 16 - FOX_HEADS), (0, 0)))
    bf_col = jnp.pad(b_forget, (0, 16 - FOX_HEADS)).reshape(16, 1)
    c_t = fgate_fwd(z_t, bf_col, "fgate_fwd")
    c_row = c_t[:FOX_HEADS].reshape(FOX_PAIRS, 2, seq)
    main1, lse, main1_b = fox_fwd(qb, kvb, c_row, "fox_fwd", out_width=D_MODEL)
    w_mkv[1], w_o[1] = gather_wait(st_b[1], [0, 0], main1, "gather_b1_wait")
    (memn1,) = rms_fwd(mem0, [vec(ln_mem[1])], "mem1_norm")
    kvm1 = mm(memn1, w_mkv[1], "kvm1")
    mixed1 = mem_attn_fwd(qb, MAIN_WIDTH // MEM_WIDTH, kvm1, main1_b, "mem_attn1")
    y1_1, hmid1, f1 = mm_resnorm(mixed1, w_o[1], h1, vec(ln_mix_post[1]), [vec(ln_ffn_pre[1])], "mix_out1")
    w_g1, w_u1 = gather_wait(st_e[0], [1, 1], f1, "gather_gate_up1_wait")
    gu1, act1 = ffn_up(f1, w_g1, w_u1, "ffn_up1")
    (w_d1,) = gather_wait(st_e[1], [0], act1, "gather_down1_wait")
    dh, d_y2_1, dg_fpost1, loss_tile = mm_resnorm_loss(act1, w_d1, hmid1, vec(ln_ffn_post[1]), tgt, "ffn_down1_loss")
    ffn_w = [(w_g0, w_u0, w_d0), (w_g1, w_u1, w_d1)]

    small = {}

    def ffn_backward(layer, dh_out, d_y2, hmid, f, gu, act, y1):
        w_g, w_u, w_d = ffn_w[layer]
        dw_down = mm_tn(act, d_y2, f"dw_down{layer}")
        rs_down = scatter_start([dw_down], [0], f"scatter_down{layer}_start")
        d_g, d_u = ffn_act_grad(d_y2, w_d, gu, f"ffn_act_grad{layer}")
        dw_g = mm_tn(f, d_g, f"dw_gate{layer}", dep=rs_down["token"])
        dw_u = mm_tn(f, d_u, f"dw_up{layer}")
        rs_gate_up = scatter_start([dw_g, dw_u], [1, 1], f"scatter_gate_up{layer}_start")
        dh_mid, d_y1, dg_fpre, dg_mpost = ffn_in_grad(d_g, d_u, w_g, w_u, hmid, dh_out, after(rs_gate_up["token"], vec(ln_ffn_pre[layer])),
                                                      y1, vec(ln_mix_post[layer]), f"ffn_in_grad{layer}")
        return dh_mid, d_y1, dg_fpre, dg_mpost, (rs_down, rs_gate_up)

    def mix_out_backward(layer, d_y1, mixed):
        dw_out = mm_tn(mixed, d_y1, f"dw_out{layer}")
        d_mixed = mm(d_y1, w_o[layer], f"d_mixed{layer}", trans_b=True)
        return d_mixed, dw_out

    def mem_backward(layer, q_src, q_block, kvm, memn, d_mixed, into):
        d_qm, d_kvm = mem_attn_bwd(q_src, q_block, kvm, d_mixed, into, f"mem_attn_bwd{layer}")
        d_kvm_b = d_kvm.astype(bf16)
        dw_mkv = mm_tn(memn, d_kvm_b, f"dw_mem_kv{layer}")
        d_memn = mm(d_kvm_b, w_mkv[layer], f"d_memn{layer}", trans_b=True)
        _, dg_mem = rms_bwd(mem0, vec(ln_mem[layer]), d_memn, None, bf16, f"mem_norm_bwd{layer}")
        return d_qm, dw_mkv, dg_mem


    dh_mid1, d_y1_1, dg_fpre1, dg_mpost1, rs_ffn1 = ffn_backward(1, dh, d_y2_1, hmid1, f1, gu1, act1, y1_1)
    d_mixed1, dw_out1 = mix_out_backward(1, d_y1_1, mixed1)
    d_qm1, dw_mkv1, dg_mem1 = mem_backward(1, qb, MAIN_WIDTH // MEM_WIDTH, kvm1, memn1, d_mixed1, lax.empty((seq, MEM_WIDTH), bf16))
    rs_mix1 = scatter_start([dw_out1, dw_mkv1], [0, 0], "scatter_mix1_start")
    dq, dk, dv, dc = fox_bwd(qb, kvb, d_mixed1, main1, lse, after(rs_mix1["token"], c_row), "fox_bwd")
    dc_t = jnp.pad(dc.reshape(FOX_HEADS, seq), ((0, 16 - FOX_HEADS), (0, 0)))
    dz_t, db_f = fgate_bwd(dc_t, z_t, bf_col, "fgate_bwd")
    d_kvf = jnp.concatenate([dk, dv, jnp.pad(dz_t[:FOX_HEADS].T.astype(bf16), ((0, 0), (0, KV_PAD - KV_WIDTH)))], axis=-1)
    d_proj1 = jnp.concatenate([dq.astype(bf16), d_qm1], axis=-1)
    dw_in_b = mm_tn(a1, d_proj1, "dw_in_b")
    dw_kv = mm_tn(sin1, d_kvf, "dw_kv", tn=896)
    rs_2 = scatter_start([dw_in_b, dw_kv], [0, 0], "scatter_shared_start")
    dh1, (dg_pre1, dg_shared), d_y2_0, dg_fpost0 = proj_in_grad(
        [(d_proj1, w_inb, vec(ln_mix_pre[1])), (d_kvf, w_kv, vec(ln_shared))], h1, dh_mid1, "in_grad1", dep=rs_2["token"],
        below=(y2_0, vec(ln_ffn_post[0])))

    dh_mid0, d_y1_0, dg_fpre0, dg_mpost0, rs_ffn0 = ffn_backward(0, dh1, d_y2_0, hmid0, f0, gu0, act0, y1_0)
    d_mixed0, dw_out0 = mix_out_backward(0, d_y1_0, mixed0)
    d_uv, dw_s, db_s, dg_lnv, db_lnv = gmlp_bwd(proj0, d_mixed0, ws, ws_t, bs_t, lnv_g, lnv_b, "gmlp_bwd", out_width=w_in_a_full.shape[1])
    d_proj0, dw_mkv0, dg_mem0 = mem_backward(0, proj0, 2 * MAIN_WIDTH // MEM_WIDTH, kvm0, memn0, d_mixed0, d_uv)
    rs_mix0 = scatter_start([dw_out0, dw_mkv0], [0, 0], "scatter_mix0_start")

    small["ln_mix_pre"] = jnp.concatenate([jnp.zeros_like(dg_pre1), dg_pre1], axis=0)
    small["ln_mix_post"] = jnp.concatenate([dg_mpost0, dg_mpost1], axis=0)
    small["ln_ffn_pre"] = jnp.concatenate([dg_fpre0, dg_fpre1], axis=0)
    small["ln_ffn_post"] = jnp.concatenate([dg_fpost0, dg_fpost1], axis=0)
    small["ln_mem"] = jnp.concatenate([dg_mem0, dg_mem1], axis=0)
    small["w_spatial"] = dw_s[None]
    small["b_spatial"] = db_s[:, :A_GROUPS].T[None]
    small["ln_shared"] = dg_shared[0]
    small["b_forget"] = db_f[:FOX_HEADS, 0]
    small["ln_v_g"] = dg_lnv
    small["ln_v_b"] = db_lnv
    small_rows = jnp.concatenate([_pack_small(small, _SMALL), after(rs_mix0["token"], loss_tile)], axis=0)
    st_small = gather_start([small_rows[None]], [0], "gather_small_grads_start")
    dw_in_a = mm_tn(a0, d_proj0, "dw_in_a", tn=896, dep=st_small["token"])
    rs_in_a = scatter_start([dw_in_a.reshape(D_MODEL, N_DEV, -1).transpose(1, 0, 2)], [0], "scatter_in_a_start")
    grad_x, (dg_pre0,) = proj_in_grad([(d_proj0, w_in_a_full, vec(ln_mix_pre[0]))], h0, dh_mid0, "in_grad0", dep=rs_in_a["token"])
    st_last = gather_start([dg_pre0.reshape(1, 8, LANES)], [0], "gather_last_grad_start")

    def owned(started, axes, wait_after, name):
        recv = scatter_wait(started, axes, wait_after, name)
        return [sum_leading(r.reshape((N_DEV, -1, r.shape[-1])), f"{name}_sum{i}", tr=_row_tile(math.prod(r.shape[1:-1])))
                for i, r in enumerate(recv)]

    (g_down1,) = owned(rs_ffn1[0], [0], after(st_last["token"], grad_x[:8, :LANES]), "scatter_down1_wait")
    g_gu1 = owned(rs_ffn1[1], [1, 1], g_down1, "scatter_gate_up1_wait")
    g_mix1 = owned(rs_mix1, [0, 0], g_gu1[0], "scatter_mix1_wait")
    g2 = owned(rs_2, [0, 0], g_mix1[0], "scatter_shared_wait")
    (g_down0,) = owned(rs_ffn0[0], [0], g2[0], "scatter_down0_wait")
    g_gu0 = owned(rs_ffn0[1], [1, 1], g_down0, "scatter_gate_up0_wait")
    g_mix0 = owned(rs_mix0, [0, 0], g_gu0[0], "scatter_mix0_wait")
    (g_in_a,) = owned(rs_in_a, [0], g_mix0[0], "scatter_in_a_wait")
    g_local = dict(
        w_ffn_gate=jnp.stack([g_gu0[0], g_gu1[0]])[:, :, :FF_SHARD], w_ffn_up=jnp.stack([g_gu0[1], g_gu1[1]])[:, :, :FF_SHARD],
        w_ffn_down=jnp.stack([g_down0, g_down1])[:, :FF_SHARD], w_out=jnp.stack([g_mix0[0], g_mix1[0]]),
        w_mem_kv=jnp.stack([g_mix0[1], g_mix1[1]]), w_in_b=g2[0][None], w_shared_kv=g2[1][:, :KV_WIDTH], w_in_a=g_in_a[None])
    (small_all,) = gather_wait(st_small, [0], g_in_a, "gather_small_grads_wait")
    (last_all,) = gather_wait(st_last, [0], small_all, "gather_last_grad_wait")
    small_sum = sum_leading(small_all, "sum_small_grads")
    loss = small_sum[small_rows.shape[0] - 1, 0]
    g_small = _unpack_small(small_sum, _SMALL)
    g_small["ln_mix_pre"] = jnp.concatenate([sum_leading(last_all, "sum_last_grad").reshape(1, D_MODEL), g_small["ln_mix_pre"][1:]], axis=0)
    shard = MAIN_WIDTH // N_DEV
    for n in ("ln_v_g", "ln_v_b"):
        g_small[n] = lax.dynamic_slice_in_dim(g_small[n], me * shard, shard, axis=1)
    grad_w = {**g_small, **g_local}

    delta, new_m, new_v = {}, {}, {}
    for n in g_local:
        two_d = (-1, weights[n].shape[-1])
        d_, m_, v_ = adamw(weights[n].reshape(two_d), grad_w[n].reshape(two_d), mom_m[n].reshape(two_d), mom_v[n].reshape(two_d),
                           f"adamw_{n}", tr=_row_tile(math.prod(weights[n].shape[:-1])))
        delta[n], new_m[n], new_v[n] = (t.reshape(weights[n].shape) for t in (d_, m_, v_))
    small_local_shapes = [(n, tuple(weights[n].shape)) for n, _ in _SMALL]
    packed = [_pack_small(src, small_local_shapes) for src in (weights, grad_w, mom_m, mom_v)]
    outs = adamw(*packed, "adamw_small", tr=packed[0].shape[0])
    for dst, buf in zip((delta, new_m, new_v), outs):
        dst.update(_unpack_small(buf, small_local_shapes))

    return (loss, grad_x[None], *[grad_w[n] for n in names], *[delta[n] for n in names],
            *[new_m[n] for n in names], *[new_v[n] for n in names])
```

```python
import functools
import math

import jax
import jax.numpy as jnp
from jax import lax
from jax.experimental import pallas as pl
from jax.experimental.pallas import tpu as pltpu

f32 = jnp.float32
bf16 = jnp.bfloat16
SDS = jax.ShapeDtypeStruct

D_MODEL = 1024
MAIN_WIDTH = 768
MEM_WIDTH = 256
HEAD_DIM = 64
MEM_HEADS = 4
FOX_HEADS = 12
FOX_PAIRS = FOX_HEADS // 2
CHUNK = 128
A_GROUPS = 6
FF_SHARD = 352
FF_SHARD_PAD = 384
FF_PAD = 8 * FF_SHARD_PAD
KV_WIDTH = 2 * MAIN_WIDTH + FOX_HEADS
KV_PAD = 1792
RMS_EPS = 1e-6
LN_EPS = 1e-5
ATT_SCALE = HEAD_DIM ** -0.5
ADAM_LR, ADAM_B1, ADAM_B2, ADAM_EPS, ADAM_WD, ADAM_STEP = 0.001, 0.9, 0.999, 1e-08, 0.01, 10
N_DEV = 8
AXES = ("x", "y", "c")
MESH = pl.DeviceIdType.MESH
V7X_VMEM_LIMIT = 56 * 1024 * 1024
LANES = 128
FLAT_W = 512
ROW_PAD = 16


def _cparams(*sem):
    return pltpu.CompilerParams(dimension_semantics=sem or None, vmem_limit_bytes=V7X_VMEM_LIMIT)


def _dot(a, b):
    return jnp.dot(a, b, preferred_element_type=f32)


def _dot_nt(a, b):
    return lax.dot_general(a, b, (((1,), (1,)), ((), ())), preferred_element_type=f32)


def _dot_tn(a, b):
    return lax.dot_general(a, b, (((0,), (0,)), ((), ())), preferred_element_type=f32)


def _gelu(x):
    k = math.sqrt(2.0 / math.pi)
    t = jnp.tanh(k * (x + 0.044715 * x * x * x))
    return 0.5 * x * (1.0 + t), t


def _gelu_grad(x, t):
    k = math.sqrt(2.0 / math.pi)
    return 0.5 * (1.0 + t) + 0.5 * x * (1.0 - t * t) * k * (1.0 + 3.0 * 0.044715 * x * x)


def _sigmoid(x):
    return 1.0 / (1.0 + jnp.exp(-x))


def rms_fwd(x, gains, name, tm=512):
    m, d = x.shape
    tm = min(tm, m)
    n = len(gains)

    def body(x_ref, *refs):
        xv = x_ref[...]
        y = xv * lax.rsqrt(jnp.sum(xv * xv, axis=-1, keepdims=True) * (1.0 / d) + RMS_EPS)
        for g_ref, o_ref in zip(refs[:n], refs[n:]):
            o_ref[...] = (y * g_ref[...]).astype(bf16)

    row = pl.BlockSpec((tm, d), lambda i: (i, 0))
    vec = pl.BlockSpec((1, d), lambda i: (0, 0))
    return pl.pallas_call(body, grid=(m // tm,), in_specs=[row] + [vec] * n, out_specs=[row] * n,
                          out_shape=[SDS((m, d), bf16)] * n, name=name, compiler_params=_cparams("parallel"))(x, *gains)


def rms_bwd(x, g, dy, add, out_dtype, name, tm=512):
    m, d = x.shape
    tm = min(tm, m)
    has_add = add is not None

    def body(x_ref, g_ref, dy_ref, *refs):
        dx_ref, dg_ref = refs[-2], refs[-1]
        xv = x_ref[...]
        dyv = dy_ref[...].astype(f32)
        r = lax.rsqrt(jnp.sum(xv * xv, axis=-1, keepdims=True) * (1.0 / d) + RMS_EPS)
        xn = xv * r
        dyg = dyv * g_ref[...]
        dx = r * (dyg - xn * (jnp.sum(dyg * xn, axis=-1, keepdims=True) * (1.0 / d)))
        if has_add:
            dx = dx + refs[0][...]
        dx_ref[...] = dx.astype(out_dtype)

        @pl.when(pl.program_id(0) == 0)
        def _():
            dg_ref[...] = jnp.zeros_like(dg_ref)

        dg_ref[...] += jnp.sum(dyv * xn, axis=0, keepdims=True)

    row = pl.BlockSpec((tm, d), lambda i: (i, 0))
    vec = pl.BlockSpec((1, d), lambda i: (0, 0))
    ins = [x, g, dy] + ([add] if has_add else [])
    return pl.pallas_call(body, grid=(m // tm,), in_specs=[row, vec, row] + ([row] if has_add else []),
                          out_specs=[row, vec], out_shape=[SDS((m, d), out_dtype), SDS((1, d), f32)], name=name,
                          compiler_params=_cparams("arbitrary"))(*ins)


def mm(a, b, name, trans_b=False, out_dtype=f32, tm=1024, tn=1024, layer=None, col0=0, ncols=None, dep=None):
    m, k = a.shape
    n_all = b.shape[-2] if trans_b else b.shape[-1]
    n = n_all if ncols is None else ncols
    tm, tn = min(tm, m), min(tn, n)
    assert m % tm == 0 and n % tn == 0 and col0 % tn == 0 and not (trans_b and col0), (name, m, n, tm, tn)
    jb = col0 // tn
    lead = () if layer is None else (None,)
    sel = () if layer is None else (layer,)

    def body(a_ref, b_ref, *rest):
        r = _dot_nt(a_ref[...], b_ref[...]) if trans_b else _dot(a_ref[...], b_ref[...])
        rest[-1][...] = r.astype(out_dtype)

    if trans_b:
        b_spec = pl.BlockSpec(lead + (tn, k), lambda j, i: sel + (j, 0))
    else:
        b_spec = pl.BlockSpec(lead + (k, tn), lambda j, i: sel + (0, jb + j))
    deps = [] if dep is None else [dep]
    dep_specs = [pl.BlockSpec((8, LANES), lambda j, i: (0, 0))] * len(deps)
    return pl.pallas_call(body, grid=(n // tn, m // tm), in_specs=[pl.BlockSpec((tm, k), lambda j, i: (i, 0)), b_spec] + dep_specs,
                          out_specs=pl.BlockSpec((tm, tn), lambda j, i: (i, j)), out_shape=SDS((m, n), out_dtype),
                          name=name, compiler_params=_cparams("parallel", "parallel"))(a, b, *deps)


def mm_tn(a, g, name, tk=1024, tn=1024, out_dtype=bf16, dep=None):
    s, k = a.shape
    n = g.shape[1]
    tk, tn = min(tk, k), min(tn, n)
    assert k % tk == 0 and n % tn == 0, (name, k, n, tk, tn)

    def body(a_ref, g_ref, *rest):
        rest[-1][...] = _dot_tn(a_ref[...], g_ref[...]).astype(out_dtype)

    deps = [] if dep is None else [dep]
    dep_specs = [pl.BlockSpec((8, LANES), lambda i, j: (0, 0))] * len(deps)
    return pl.pallas_call(body, grid=(k // tk, n // tn),
                          in_specs=[pl.BlockSpec((s, tk), lambda i, j: (0, i)), pl.BlockSpec((s, tn), lambda i, j: (0, j))] + dep_specs,
                          out_specs=pl.BlockSpec((tk, tn), lambda i, j: (i, j)), out_shape=SDS((k, n), out_dtype), name=name,
                          compiler_params=_cparams("parallel", "parallel"))(a, g, *deps)


def _resident(shape, index_map):
    return pl.BlockSpec(shape, index_map, pipeline_mode=pl.Buffered(1))


def _rms(xv):
    return xv * lax.rsqrt(jnp.sum(xv * xv, axis=-1, keepdims=True) * (1.0 / xv.shape[-1]) + RMS_EPS)


def _rms_bwd_math(xv, g, dy):
    d = xv.shape[-1]
    r = lax.rsqrt(jnp.sum(xv * xv, axis=-1, keepdims=True) * (1.0 / d) + RMS_EPS)
    xn = xv * r
    dyg = dy * g
    dx = r * (dyg - xn * (jnp.sum(dyg * xn, axis=-1, keepdims=True) * (1.0 / d)))
    return dx, jnp.sum(dy * xn, axis=0, keepdims=True)


SUB_ROWS = 512


def mm_resnorm(a, b, h, g_post, gains, name, tm=512):
    m, k = a.shape
    d = b.shape[1]
    n = len(gains)

    def body(a_ref, b_ref, h_ref, gp_ref, *refs):
        for r in range(tm // SUB_ROWS):
            rows = slice(r * SUB_ROWS, (r + 1) * SUB_ROWS)
            y = _dot(a_ref[rows, :], b_ref[...])
            refs[n][rows, :] = y
            hn = h_ref[rows, :] + _rms(y) * gp_ref[...]
            refs[n + 1][rows, :] = hn
            if n:
                z = _rms(hn)
                for g_ref, o_ref in zip(refs[:n], refs[n + 2:]):
                    o_ref[rows, :] = (z * g_ref[...]).astype(bf16)

    row = pl.BlockSpec((tm, d), lambda i: (i, 0))
    vec = pl.BlockSpec((1, d), lambda i: (0, 0))
    return pl.pallas_call(body, grid=(m // tm,),
                          in_specs=[pl.BlockSpec((tm, k), lambda i: (i, 0)), _resident((k, d), lambda i: (0, 0)), row, vec] + [vec] * n,
                          out_specs=[row] * (n + 2), out_shape=[SDS((m, d), f32)] * 2 + [SDS((m, d), bf16)] * n, name=name,
                          compiler_params=_cparams("parallel"))(a, b, h, g_post, *gains)


def mm_resnorm_loss(a, b, h, g_post, tgt, name, tm=512):
    m, k = a.shape
    d = b.shape[1]

    def body(a_ref, b_ref, h_ref, gp_ref, t_ref, dh_ref, dy_ref, dg_ref, l_ref):
        @pl.when(pl.program_id(0) == 0)
        def _():
            dg_ref[...] = jnp.zeros_like(dg_ref)
            l_ref[...] = jnp.zeros_like(l_ref)

        y = _dot(a_ref[...], b_ref[...])
        e = h_ref[...] + _rms(y) * gp_ref[...] - t_ref[...]
        dh = e * (1.0 / d)
        dh_ref[...] = dh
        part = jnp.sum(jnp.sum(e * e, axis=-1, keepdims=True), axis=0, keepdims=True) * (0.5 / d)
        l_ref[...] += jnp.broadcast_to(part, l_ref.shape)
        dy, dg = _rms_bwd_math(y, gp_ref[...], dh)
        dy_ref[...] = dy.astype(bf16)
        dg_ref[...] += dg

    row = pl.BlockSpec((tm, d), lambda i: (i, 0))
    vec = pl.BlockSpec((1, d), lambda i: (0, 0))
    return pl.pallas_call(body, grid=(m // tm,),
                          in_specs=[pl.BlockSpec((tm, k), lambda i: (i, 0)), _resident((k, d), lambda i: (0, 0)), row, vec, row],
                          out_specs=[row, row, vec, pl.BlockSpec((8, LANES), lambda i: (0, 0))],
                          out_shape=[SDS((m, d), f32), SDS((m, d), bf16), SDS((1, d), f32), SDS((8, LANES), f32)], name=name,
                          compiler_params=_cparams("arbitrary"))(a, b, h, g_post, tgt)


def ffn_act_grad(d_y2, w_d, factors, name, tm=1024, tn=1536):
    s, d = d_y2.shape
    ff = w_d.shape[0]
    nb = ff // tn

    def body(a_ref, b_ref, g_ref, u_ref, dg_ref, du_ref):
        av = a_ref[...]
        tc = 256
        for c in range(tn // tc):
            cols = slice(c * tc, (c + 1) * tc)
            da = _dot_nt(av, b_ref[cols, :])
            dg_ref[:, cols] = (da * g_ref[:, cols].astype(f32)).astype(bf16)
            du_ref[:, cols] = (da * u_ref[:, cols].astype(f32)).astype(bf16)

    tile = pl.BlockSpec((tm, tn), lambda j, i: (i, j))
    return pl.pallas_call(body, grid=(nb, s // tm),
                          in_specs=[pl.BlockSpec((tm, d), lambda j, i: (i, 0)), pl.BlockSpec((tn, d), lambda j, i: (j, 0)), tile,
                                    pl.BlockSpec((tm, tn), lambda j, i: (i, nb + j))],
                          out_specs=[tile, tile], out_shape=[SDS((s, ff), bf16)] * 2, name=name,
                          compiler_params=_cparams("parallel", "parallel"))(d_y2, w_d, factors, factors)


def ffn_in_grad(d_g, d_u, w_g, w_u, hmid, dh_out, g_pre, y1, g_post, name, tm=512):
    s, ff = d_g.shape
    d = w_g.shape[0]

    def body(dg_ref, du_ref, wg_ref, wu_ref, hm_ref, dho_ref, gpre_ref, y1_ref, gpost_ref, dhm_ref, dy1_ref, dgpre_ref, dgpost_ref):
        @pl.when(pl.program_id(0) == 0)
        def _():
            dgpre_ref[...] = jnp.zeros_like(dgpre_ref)
            dgpost_ref[...] = jnp.zeros_like(dgpost_ref)

        for r in range(tm // SUB_ROWS):
            rows = slice(r * SUB_ROWS, (r + 1) * SUB_ROWS)
            d_f = _dot_nt(dg_ref[rows, :], wg_ref[...]) + _dot_nt(du_ref[rows, :], wu_ref[...])
            dx, dg1 = _rms_bwd_math(hm_ref[rows, :], gpre_ref[...], d_f)
            dh_mid = dho_ref[rows, :] + dx
            dhm_ref[rows, :] = dh_mid
            dgpre_ref[...] += dg1
            dy1, dg2 = _rms_bwd_math(y1_ref[rows, :], gpost_ref[...], dh_mid)
            dy1_ref[rows, :] = dy1.astype(bf16)
            dgpost_ref[...] += dg2

    row = pl.BlockSpec((tm, d), lambda i: (i, 0))
    vec = pl.BlockSpec((1, d), lambda i: (0, 0))
    wide = pl.BlockSpec((tm, ff), lambda i: (i, 0))
    w_spec = _resident((d, ff), lambda i: (0, 0))
    return pl.pallas_call(body, grid=(s // tm,), in_specs=[wide, wide, w_spec, w_spec, row, row, vec, row, vec],
                          out_specs=[row, row, vec, vec], out_shape=[SDS((s, d), f32), SDS((s, d), bf16), SDS((1, d), f32), SDS((1, d), f32)],
                          name=name, compiler_params=_cparams("arbitrary"))(d_g, d_u, w_g, w_u, hmid, dh_out, g_pre, y1, g_post)


def proj_in_grad(pairs, x, add, name, tm=512, dep=None, below=None):
    s, d = x.shape
    n = len(pairs)
    extra = [] if dep is None else [dep]
    n_below = 0 if below is None else 2

    def body(*refs):
        x_ref, add_ref = refs[3 * n], refs[3 * n + 1]
        below_refs = refs[3 * n + 2:3 * n + 2 + n_below]
        outs = refs[3 * n + 2 + n_below + len(extra):]

        @pl.when(pl.program_id(0) == 0)
        def _():
            for o in outs[1:1 + n] + outs[2 + n:]:
                o[...] = jnp.zeros_like(o)

        xv = x_ref[...]
        dx = add_ref[...]
        for i in range(n):
            a_ref, b_ref, g_ref = refs[3 * i:3 * i + 3]
            dxi, dgi = _rms_bwd_math(xv, g_ref[...], _dot_nt(a_ref[...], b_ref[...]))
            dx = dx + dxi
            outs[1 + i][...] += dgi
        outs[0][...] = dx
        if below is not None:
            dy, dg = _rms_bwd_math(below_refs[0][...], below_refs[1][...], dx)
            outs[1 + n][...] = dy.astype(bf16)
            outs[2 + n][...] += dg

    row = pl.BlockSpec((tm, d), lambda i: (i, 0))
    vec = pl.BlockSpec((1, d), lambda i: (0, 0))
    in_specs, args = [], []
    for a, b, g in pairs:
        k = a.shape[1]
        in_specs += [pl.BlockSpec((tm, k), lambda i: (i, 0)), _resident((d, k), lambda i: (0, 0)), vec]
        args += [a, b, g]
    in_specs += [row, row] + [row, vec][:n_below] + [pl.BlockSpec((8, LANES), lambda i: (0, 0))] * len(extra)
    out_specs = [row] + [vec] * n + [row, vec][:n_below]
    out_shape = [SDS((s, d), f32)] + [SDS((1, d), f32)] * n + [SDS((s, d), bf16), SDS((1, d), f32)][:n_below]
    out = pl.pallas_call(body, grid=(s // tm,), in_specs=in_specs, out_specs=out_specs, out_shape=out_shape, name=name,
                         compiler_params=_cparams("arbitrary"))(*args, x, add, *(below or ()), *extra)
    return (out[0], out[1:1 + n]) + tuple(out[1 + n:])


def ffn_up(f, wg, wu, name, tm=512, tc=256):
    s, d = f.shape
    ff = wg.shape[-1]

    def body(f_ref, wg_ref, wu_ref, fac_ref, act_ref):
        fv = f_ref[...]
        for j in range(ff // tc):
            lo = j * tc
            gg = _dot(fv, wg_ref[:, lo:lo + tc])
            uu = _dot(fv, wu_ref[:, lo:lo + tc])
            sg = _sigmoid(gg)
            silu = gg * sg
            fac_ref[:, lo:lo + tc] = (uu * (sg + silu * (1.0 - sg))).astype(bf16)
            fac_ref[:, ff + lo:ff + lo + tc] = silu.astype(bf16)
            act_ref[:, lo:lo + tc] = (silu * uu).astype(bf16)

    w_spec = _resident((d, ff), lambda i: (0, 0))
    return pl.pallas_call(body, grid=(s // tm,), in_specs=[pl.BlockSpec((tm, d), lambda i: (i, 0)), w_spec, w_spec],
                          out_specs=[pl.BlockSpec((tm, 2 * ff), lambda i: (i, 0)), pl.BlockSpec((tm, ff), lambda i: (i, 0))],
                          out_shape=[SDS((s, 2 * ff), bf16), SDS((s, ff), bf16)], name=name,
                          compiler_params=_cparams("parallel"))(f, wg, wu)


def _gmlp_forward_chunk(u, v, w_refs, bias, ln_g, ln_b):
    gu, tu = _gelu(u)
    gv, tv = _gelu(v)
    mu = jnp.sum(gv, axis=-1, keepdims=True) * (1.0 / MAIN_WIDTH)
    xc = gv - mu
    rstd = lax.rsqrt(jnp.sum(xc * xc, axis=-1, keepdims=True) * (1.0 / MAIN_WIDTH) + LN_EPS)
    xhat = xc * rstd
    vln = xhat * ln_g + ln_b
    row = lax.broadcasted_iota(jnp.int32, (CHUNK, CHUNK), 0)
    col = lax.broadcasted_iota(jnp.int32, (CHUNK, CHUNK), 1)
    s_parts = []
    for g in range(A_GROUPS):
        w = jnp.where(col <= row, w_refs[g], jnp.zeros((), bf16))
        s_parts.append(_dot(w, vln[:, g * CHUNK:(g + 1) * CHUNK].astype(bf16)) + bias[:, g:g + 1])
    return gu, tu, tv, rstd, xhat, vln, s_parts


def gmlp_fwd(proj, ws, bs_t, ln_g, ln_b, name, tm=512, out_width=MAIN_WIDTH):
    s = proj.shape[0]

    def body(u_ref, v_ref, w_ref, b_ref, g_ref, bb_ref, o_ref):
        bias = b_ref[...]
        for c in range(tm // CHUNK):
            rows = slice(c * CHUNK, (c + 1) * CHUNK)
            gu, _, _, _, _, _, s_parts = _gmlp_forward_chunk(u_ref[rows, :], v_ref[rows, :], w_ref, bias, g_ref[...], bb_ref[...])
            for g in range(A_GROUPS):
                cols = slice(g * CHUNK, (g + 1) * CHUNK)
                o_ref[rows, cols] = (gu[:, cols] * s_parts[g]).astype(bf16)

    vec = pl.BlockSpec((1, MAIN_WIDTH), lambda i: (0, 0))
    return pl.pallas_call(
        body, grid=(s // tm,),
        in_specs=[pl.BlockSpec((tm, MAIN_WIDTH), lambda i: (i, 0)), pl.BlockSpec((tm, MAIN_WIDTH), lambda i: (i, 1)),
                  pl.BlockSpec((A_GROUPS, CHUNK, CHUNK), lambda i: (0, 0, 0)), pl.BlockSpec((CHUNK, A_GROUPS), lambda i: (0, 0)), vec, vec],
        out_specs=pl.BlockSpec((tm, MAIN_WIDTH), lambda i: (i, 0)), out_shape=SDS((s, out_width), bf16), name=name,
        compiler_params=_cparams("parallel"))(proj, proj, ws, bs_t, ln_g, ln_b)


def gmlp_bwd(proj, d_mixed, ws, ws_t, bs_t, ln_g, ln_b, name, tm=512, out_width=2 * MAIN_WIDTH):
    s = proj.shape[0]

    def body(u_ref, v_ref, dm_ref, w_ref, wt_ref, b_ref, g_ref, bb_ref, duv_ref, dw_ref, db_ref, dg_ref, dbb_ref):
        @pl.when(pl.program_id(0) == 0)
        def _():
            dw_ref[...] = jnp.zeros_like(dw_ref)
            db_ref[...] = jnp.zeros_like(db_ref)
            dg_ref[...] = jnp.zeros_like(dg_ref)
            dbb_ref[...] = jnp.zeros_like(dbb_ref)

        bias = b_ref[...]
        ln_gv = g_ref[...]
        row = lax.broadcasted_iota(jnp.int32, (CHUNK, CHUNK), 0)
        col = lax.broadcasted_iota(jnp.int32, (CHUNK, CHUNK), 1)
        lane = lax.broadcasted_iota(jnp.int32, (CHUNK, LANES), 1)
        for c in range(tm // CHUNK):
            rows = slice(c * CHUNK, (c + 1) * CHUNK)
            u = u_ref[rows, :]
            v = v_ref[rows, :]
            gu, tu, tv, rstd, xhat, vln, s_parts = _gmlp_forward_chunk(u, v, w_ref, bias, ln_gv, bb_ref[...])
            dm = dm_ref[rows, :]
            d_vln_parts = []
            d_gu_parts = []
            db_acc = jnp.zeros((CHUNK, LANES), f32)
            for g in range(A_GROUPS):
                cols = slice(g * CHUNK, (g + 1) * CHUNK)
                dmg = dm[:, cols]
                d_gu_parts.append(dmg * s_parts[g])
                d_s = dmg * gu[:, cols]
                db_acc = db_acc + jnp.where(lane == g, jnp.sum(d_s, axis=-1, keepdims=True), 0.0)
                d_sb = d_s.astype(bf16)
                dw_ref[g] += jnp.where(col <= row, _dot_nt(d_sb, vln[:, cols].astype(bf16)), 0.0)
                wt = jnp.where(row <= col, wt_ref[g], jnp.zeros((), bf16))
                d_vln_parts.append(_dot(wt, d_sb))
            db_ref[...] += db_acc
            d_vln = jnp.concatenate(d_vln_parts, axis=-1)
            d_gu = jnp.concatenate(d_gu_parts, axis=-1)
            dg_ref[...] += jnp.sum(d_vln * xhat, axis=0, keepdims=True)
            dbb_ref[...] += jnp.sum(d_vln, axis=0, keepdims=True)
            dxh = d_vln * ln_gv
            m1 = jnp.sum(dxh, axis=-1, keepdims=True) * (1.0 / MAIN_WIDTH)
            m2 = jnp.sum(dxh * xhat, axis=-1, keepdims=True) * (1.0 / MAIN_WIDTH)
            d_gv = rstd * (dxh - m1 - xhat * m2)
            duv_ref[rows, :MAIN_WIDTH] = (d_gu * _gelu_grad(u, tu)).astype(bf16)
            duv_ref[rows, MAIN_WIDTH:] = (d_gv * _gelu_grad(v, tv)).astype(bf16)

    vec = pl.BlockSpec((1, MAIN_WIDTH), lambda i: (0, 0))
    wspec = pl.BlockSpec((A_GROUPS, CHUNK, CHUNK), lambda i: (0, 0, 0))
    return pl.pallas_call(
        body, grid=(s // tm,),
        in_specs=[pl.BlockSpec((tm, MAIN_WIDTH), lambda i: (i, 0)), pl.BlockSpec((tm, MAIN_WIDTH), lambda i: (i, 1)),
                  pl.BlockSpec((tm, MAIN_WIDTH), lambda i: (i, 0)), wspec, wspec, pl.BlockSpec((CHUNK, A_GROUPS), lambda i: (0, 0)), vec, vec],
        out_specs=[pl.BlockSpec((tm, 2 * MAIN_WIDTH), lambda i: (i, 0)), wspec, pl.BlockSpec((CHUNK, LANES), lambda i: (0, 0)), vec, vec],
        out_shape=[SDS((s, out_width), bf16), SDS((A_GROUPS, CHUNK, CHUNK), f32), SDS((CHUNK, LANES), f32),
                   SDS((1, MAIN_WIDTH), f32), SDS((1, MAIN_WIDTH), f32)],
        name=name, compiler_params=_cparams("arbitrary"))(proj, proj, d_mixed, ws, ws_t, bs_t, ln_g, ln_b)


def _head_mask(width, h):
    lane = lax.broadcasted_iota(jnp.int32, (1, width), 1)
    return (lane >= h * HEAD_DIM) & (lane < (h + 1) * HEAD_DIM)


def mem_attn_fwd(proj, q_block, kv, into, name, tm=512):
    s = proj.shape[0]
    n_mem = kv.shape[0]
    out_block = into.shape[1] // MEM_WIDTH - 1

    def body(q_ref, kv_ref, into_ref, o_ref):
        q = q_ref[...].astype(f32)
        k = kv_ref[:, :MEM_WIDTH].astype(bf16)
        v = kv_ref[:, MEM_WIDTH:].astype(bf16)
        out = jnp.zeros((tm, MEM_WIDTH), f32)
        for h in range(MEM_HEADS):
            msk = _head_mask(MEM_WIDTH, h)
            qh = jnp.where(msk, q, 0.0).astype(bf16)
            sc = _dot_nt(qh, k) * ATT_SCALE
            e = jnp.exp(sc - jnp.max(sc, axis=-1, keepdims=True))
            p = e / jnp.sum(e, axis=-1, keepdims=True)
            out = jnp.where(msk, _dot(p.astype(bf16), v), out)
        o_ref[...] = out.astype(bf16)

    return pl.pallas_call(body, grid=(s // tm,),
                          in_specs=[pl.BlockSpec((tm, MEM_WIDTH), lambda i: (i, q_block)), pl.BlockSpec((n_mem, 2 * MEM_WIDTH), lambda i: (0, 0)), _ANY],
                          out_specs=pl.BlockSpec((tm, MEM_WIDTH), lambda i: (i, out_block)), out_shape=SDS(into.shape, bf16), name=name,
                          input_output_aliases={2: 0}, compiler_params=_cparams("parallel"))(proj, kv, into)


def mem_attn_bwd(proj, q_block, kv, d_mixed, into, name, tm=512):
    s = proj.shape[0]
    n_mem = kv.shape[0]
    out_block = into.shape[1] // MEM_WIDTH - 1

    def body(q_ref, kv_ref, do_ref, into_ref, dq_ref, dkv_ref):
        @pl.when(pl.program_id(0) == 0)
        def _():
            dkv_ref[...] = jnp.zeros_like(dkv_ref)

        q = q_ref[...].astype(f32)
        do = do_ref[...]
        k = kv_ref[:, :MEM_WIDTH].astype(bf16)
        v = kv_ref[:, MEM_WIDTH:].astype(bf16)
        dq = jnp.zeros((tm, MEM_WIDTH), f32)
        dk = jnp.zeros((n_mem, MEM_WIDTH), f32)
        dv = jnp.zeros((n_mem, MEM_WIDTH), f32)
        for h in range(MEM_HEADS):
            msk = _head_mask(MEM_WIDTH, h)
            qh = jnp.where(msk, q, 0.0).astype(bf16)
            doh = jnp.where(msk, do, 0.0).astype(bf16)
            sc = _dot_nt(qh, k) * ATT_SCALE
            e = jnp.exp(sc - jnp.max(sc, axis=-1, keepdims=True))
            p = e / jnp.sum(e, axis=-1, keepdims=True)
            dp = _dot_nt(doh, v)
            ds = p * (dp - jnp.sum(dp * p, axis=-1, keepdims=True))
            dsb = (ds * ATT_SCALE).astype(bf16)
            dq = jnp.where(msk, _dot(dsb, k), dq)
            dk = dk + _dot_tn(dsb, qh)
            dv = dv + _dot_tn(p.astype(bf16), doh)
        dq_ref[...] = dq.astype(bf16)
        dkv_ref[:, :MEM_WIDTH] += dk
        dkv_ref[:, MEM_WIDTH:] += dv

    return pl.pallas_call(
        body, grid=(s // tm,),
        in_specs=[pl.BlockSpec((tm, MEM_WIDTH), lambda i: (i, q_block)), pl.BlockSpec((n_mem, 2 * MEM_WIDTH), lambda i: (0, 0)),
                  pl.BlockSpec((tm, MEM_WIDTH), lambda i: (i, MAIN_WIDTH // MEM_WIDTH)), _ANY],
        out_specs=[pl.BlockSpec((tm, MEM_WIDTH), lambda i: (i, out_block)), pl.BlockSpec((n_mem, 2 * MEM_WIDTH), lambda i: (0, 0))],
        out_shape=[SDS(into.shape, bf16), SDS((n_mem, 2 * MEM_WIDTH), f32)], name=name,
        input_output_aliases={3: 0}, compiler_params=_cparams("arbitrary"))(proj, kv, d_mixed, into)


def _tri(t, upper):
    r = lax.broadcasted_iota(jnp.int32, (t, t), 0)
    c = lax.broadcasted_iota(jnp.int32, (t, t), 1)
    return ((r <= c) if upper else (r >= c)).astype(f32)


def fgate_fwd(z_t, b, name, t=512):
    hh, s = z_t.shape

    def body(z_ref, b_ref, c_ref):
        u = _tri(t, True)
        carry = jnp.zeros((hh, 1), f32)
        for blk in range(s // t):
            x = z_ref[:, blk * t:(blk + 1) * t] + b_ref[...]
            logf = jnp.minimum(x, 0.0) - jnp.log(1.0 + jnp.exp(-jnp.abs(x)))
            y = jnp.dot(logf, u, precision=lax.Precision.HIGHEST, preferred_element_type=f32) + carry
            c_ref[:, blk * t:(blk + 1) * t] = y
            carry = y[:, t - 1:t]

    return pl.pallas_call(body, out_shape=SDS((hh, s), f32), name=name, compiler_params=_cparams())(z_t, b)


def fgate_bwd(dc_t, z_t, b, name, t=512):
    hh, s = z_t.shape

    def body(dc_ref, z_ref, b_ref, dz_ref, db_ref):
        low = _tri(t, False)
        carry = jnp.zeros((hh, 1), f32)
        total = jnp.zeros((hh, 1), f32)
        for blk in reversed(range(s // t)):
            cols = slice(blk * t, (blk + 1) * t)
            y = jnp.dot(dc_ref[:, cols], low, precision=lax.Precision.HIGHEST, preferred_element_type=f32) + carry
            carry = y[:, 0:1]
            dz = y * _sigmoid(-(z_ref[:, cols] + b_ref[...]))
            dz_ref[:, cols] = dz
            total = total + jnp.sum(dz, axis=-1, keepdims=True)
        db_ref[...] = jnp.broadcast_to(total, db_ref.shape)

    return pl.pallas_call(body, out_shape=[SDS((hh, s), f32), SDS((hh, LANES), f32)], name=name,
                          compiler_params=_cparams())(dc_t, z_t, b)


def _pair_masks():
    lane = lax.broadcasted_iota(jnp.int32, (1, LANES), 1)
    return [lane < HEAD_DIM, lane >= HEAD_DIM]


def _tile_base(cr_ref, hh, lo):
    return cr_ref[hh:hh + 1, pl.ds(lo, LANES)][:, 0:1]


def fox_fwd(q, kv, c_row, name, tq=512, out_width=MAIN_WIDTH):
    s = kv.shape[0]
    nq = s // tq

    def body(q_ref, k_ref, v_ref, cr_ref, o_ref, lse_ref, ob_ref):
        i = pl.program_id(1)
        qv = q_ref[...]
        masks = _pair_masks()
        row = lax.broadcasted_iota(jnp.int32, (tq, tq), 0)
        col = lax.broadcasted_iota(jnp.int32, (tq, tq), 1)
        qh = [jnp.where(masks[hh], qv, jnp.zeros((), bf16)) * ATT_SCALE for hh in range(2)]
        ct = [_tile_base(cr_ref, hh, pl.multiple_of(i * tq, tq)) for hh in range(2)]

        def block(j, carry, diag):
            lo = pl.multiple_of(j * tq, tq)
            ks = k_ref[pl.ds(lo, tq), :]
            vs = v_ref[pl.ds(lo, tq), :]
            out = []
            for hh in range(2):
                m, l, acc = carry[hh]
                sc = _dot_nt(qh[hh], ks) + (ct[hh] - cr_ref[hh:hh + 1, pl.ds(lo, tq)])
                if diag:
                    sc = jnp.where(col <= row, sc, -jnp.inf)
                m_new = jnp.maximum(m, jnp.max(sc, axis=-1, keepdims=True))
                alpha = jnp.exp(m - m_new)
                p = jnp.exp(sc - m_new)
                l = alpha * l + jnp.sum(p, axis=-1, keepdims=True)
                p_hi = p.astype(bf16)
                p_lo = (p - p_hi.astype(f32)).astype(bf16)
                acc = alpha * acc + (_dot(p_hi, vs) + _dot(p_lo, vs))
                out.append((m_new, l, acc))
            return tuple(out)

        init = (jnp.full((tq, 1), -jnp.inf, f32), jnp.zeros((tq, 1), f32), jnp.zeros((tq, LANES), f32))
        carry = lax.fori_loop(0, i, functools.partial(block, diag=False), (init, init))
        res = [(acc / l, m + jnp.log(l)) for m, l, acc in block(i, carry, True)]
        out = jnp.where(masks[0], res[0][0], res[1][0])
        o_ref[...] = out
        ob_ref[...] = out.astype(bf16)
        lse_ref[...] = jnp.where(masks[0], res[0][1], res[1][1])

    return pl.pallas_call(
        body, grid=(FOX_PAIRS, nq),
        in_specs=[pl.BlockSpec((tq, LANES), lambda p, i: (i, p)), pl.BlockSpec((s, LANES), lambda p, i: (0, p)),
                  pl.BlockSpec((s, LANES), lambda p, i: (0, FOX_PAIRS + p)), pl.BlockSpec((None, 2, s), lambda p, i: (p, 0, 0))],
        out_specs=[pl.BlockSpec((tq, LANES), lambda p, i: (i, p)), pl.BlockSpec((None, tq, LANES), lambda p, i: (p, i, 0)),
                   pl.BlockSpec((tq, LANES), lambda p, i: (i, p))],
        out_shape=[SDS((s, MAIN_WIDTH), f32), SDS((FOX_PAIRS, s, LANES), f32), SDS((s, out_width), bf16)], name=name,
        compiler_params=_cparams("parallel", "parallel"))(q, kv, kv, c_row)


def fox_bwd(q, kv, d_mixed, o, lse, c_row, name, tq=512, dq_width=MAIN_WIDTH):
    s = kv.shape[0]
    nq = s // tq

    def body(q_ref, k_ref, v_ref, do_ref, o_ref, lse_ref, cr_ref, dqb_ref, dk_ref, dv_ref, dc_ref, dq_ref):
        j = pl.program_id(1)

        @pl.when(j == 0)
        def _():
            dq_ref[...] = jnp.zeros_like(dq_ref)

        masks = _pair_masks()
        sub = lax.broadcasted_iota(jnp.int32, (LANES, 1), 0)
        sub_masks = [sub < HEAD_DIM, sub >= HEAD_DIM]
        row = lax.broadcasted_iota(jnp.int32, (tq, tq), 0)
        col = lax.broadcasted_iota(jnp.int32, (tq, tq), 1)
        kj = k_ref[...]
        vj = v_ref[...]
        lo_j = pl.multiple_of(j * tq, tq)

        def block(i, carry, diag):
            dk_t, dv_t, dc0, dc1 = carry
            dcs = [dc0, dc1]
            lo = pl.multiple_of(i * tq, tq)
            qi = q_ref[pl.ds(lo, tq), :]
            qi = qi * ATT_SCALE
            qt_i = qi.T
            doi = do_ref[pl.ds(lo, tq), :]
            dot_i = doi.astype(bf16).T
            prod = doi.astype(bf16).astype(f32) * o_ref[pl.ds(lo, tq), :]
            lse_i = lse_ref[pl.ds(lo, tq), :]
            dq_i = jnp.zeros((tq, LANES), f32)
            for hh in range(2):
                qh = jnp.where(masks[hh], qi, jnp.zeros((), bf16))
                doh = jnp.where(masks[hh], doi, 0.0).astype(bf16)
                delta = jnp.sum(jnp.where(masks[hh], prod, 0.0), axis=-1, keepdims=True)
                sc = _dot_nt(qh, kj) + (_tile_base(cr_ref, hh, lo) - cr_ref[hh:hh + 1, pl.ds(lo_j, tq)])
                p = jnp.exp(sc - lse_i[:, hh * HEAD_DIM:hh * HEAD_DIM + 1])
                if diag:
                    p = jnp.where(col <= row, p, 0.0)
                dv_t = dv_t + _dot(jnp.where(sub_masks[hh], dot_i, jnp.zeros((), bf16)), p.astype(bf16))
                ds = p * (_dot_nt(doh, vj) - delta)
                dcs[hh] = dcs[hh] + jnp.sum(ds, axis=0, keepdims=True)
                dsb = ds.astype(bf16)
                dq_i = jnp.where(masks[hh], _dot(dsb, kj), dq_i)
                dk_t = dk_t + _dot(jnp.where(sub_masks[hh], qt_i, jnp.zeros((), bf16)), dsb)
            dq_ref[pl.ds(lo, tq), :] += dq_i * ATT_SCALE
            return dk_t, dv_t, dcs[0], dcs[1]

        zero = jnp.zeros((LANES, tq), f32)
        zrow = jnp.zeros((1, tq), f32)
        carry = block(j, (zero, zero, zrow, zrow), True)
        dk_t, dv_t, dc0, dc1 = lax.fori_loop(j + 1, nq, functools.partial(block, diag=False), carry)
        dk_ref[...] = dk_t.T.astype(bf16)
        dv_ref[...] = dv_t.T.astype(bf16)
        dc_ref[0:1, :] = -dc0
        dc_ref[1:2, :] = -dc1

        @pl.when(j == nq - 1)
        def _():
            dqb_ref[...] = dq_ref[...].astype(bf16)

    full = lambda p, j: (0, p)
    tile = lambda p, j: (j, p)
    return pl.pallas_call(
        body, grid=(FOX_PAIRS, nq),
        in_specs=[pl.BlockSpec((s, LANES), full), pl.BlockSpec((tq, LANES), tile), pl.BlockSpec((tq, LANES), lambda p, j: (j, FOX_PAIRS + p)),
                  pl.BlockSpec((s, LANES), full), pl.BlockSpec((s, LANES), full), pl.BlockSpec((None, s, LANES), lambda p, j: (p, 0, 0)),
                  pl.BlockSpec((None, 2, s), lambda p, j: (p, 0, 0))],
        out_specs=[pl.BlockSpec((s, LANES), full), pl.BlockSpec((tq, LANES), tile), pl.BlockSpec((tq, LANES), tile),
                   pl.BlockSpec((None, 2, tq), lambda p, j: (p, 0, j))],
        out_shape=[SDS((s, dq_width), bf16), SDS((s, MAIN_WIDTH), bf16), SDS((s, MAIN_WIDTH), bf16), SDS((FOX_PAIRS, 2, s), f32)],
        scratch_shapes=[pltpu.VMEM((s, LANES), f32)],
        name=name, compiler_params=_cparams("parallel", "arbitrary"))(q, kv, kv, d_mixed, o, lse, c_row)


def adamw(w, g, m, v, name, tr=256):
    r, c = w.shape
    tr = min(tr, r)
    assert r % tr == 0, (name, r, tr)
    c1 = 1.0 / (1.0 - ADAM_B1 ** ADAM_STEP)
    c2 = 1.0 / (1.0 - ADAM_B2 ** ADAM_STEP)

    def body(w_ref, g_ref, m_ref, v_ref, d_ref, mo_ref, vo_ref):
        gv = g_ref[...]
        mn = ADAM_B1 * m_ref[...] + (1.0 - ADAM_B1) * gv
        vn = ADAM_B2 * v_ref[...] + (1.0 - ADAM_B2) * gv * gv
        mo_ref[...] = mn
        vo_ref[...] = vn
        d_ref[...] = -ADAM_LR * ((mn * c1) / (jnp.sqrt(vn * c2) + ADAM_EPS) + ADAM_WD * w_ref[...])

    spec = pl.BlockSpec((tr, c), lambda i: (i, 0))
    return pl.pallas_call(body, grid=(r // tr,), in_specs=[spec] * 4, out_specs=[spec] * 3, out_shape=[SDS((r, c), f32)] * 3,
                          name=name, compiler_params=_cparams("parallel"))(w, g, m, v)


def sum_leading(x, name, out_dtype=f32, tr=None):
    n, r, c = x.shape
    tr = tr or r
    assert r % tr == 0

    def body(x_ref, o_ref):
        acc = x_ref[0].astype(f32)
        for k in range(1, n):
            acc = acc + x_ref[k].astype(f32)
        o_ref[...] = acc.astype(out_dtype)

    return pl.pallas_call(body, grid=(r // tr,), in_specs=[pl.BlockSpec((n, tr, c), lambda i: (0, i, 0))],
                          out_specs=pl.BlockSpec((tr, c), lambda i: (i, 0)), out_shape=SDS((r, c), out_dtype), name=name,
                          compiler_params=_cparams("parallel"))(x)


_ANY = pl.BlockSpec(memory_space=pl.ANY)
_DMA = pltpu.SemaphoreType.DMA


_HBM = pl.BlockSpec(memory_space=pltpu.HBM)
_SEM = pl.BlockSpec(memory_space=pltpu.SEMAPHORE)
_EFFECT = pltpu.SideEffectType.DATAFLOW_SIDE_EFFECTING
_FLIPS = [(0, 0, 1), (1, 0, 0), (0, 1, 0), (1, 1, 0), (1, 0, 1), (0, 1, 1), (1, 1, 1)]


def _me():
    return lax.axis_index("x"), lax.axis_index("y"), lax.axis_index("c")


def _peers():
    mx, my, mc = _me()
    return [(jnp.bitwise_xor(mx, fx), jnp.bitwise_xor(my, fy), jnp.bitwise_xor(mc, fc)) for fx, fy, fc in _FLIPS]


def _index(dev):
    return 4 * dev[0] + 2 * dev[1] + dev[2]


def _win(ref, axis, k, size, count=1):
    idx = [slice(None)] * len(ref.shape)
    idx[axis] = pl.ds(k * size, count * size)
    return ref.at[tuple(idx)]


def _hbm(a):
    return pltpu.with_memory_space_constraint(a, pltpu.HBM)


def _exchange_start(srcs, lands, copies_of, name):
    n = len(srcs)

    def body(*refs):
        src = refs[:n]
        send_sems, recv_sems, self_sems = refs[2 * n:2 * n + 3]
        land = refs[3 * n + 3:4 * n + 3]
        token = refs[4 * n + 3]
        me = _index(_me())
        for a in range(n):
            for s_ref, d_ref, peer in copies_of(a, src[a], land[a], me):
                if peer is None:
                    pltpu.make_async_copy(s_ref, d_ref, self_sems.at[a]).start()
                else:
                    pltpu.make_async_remote_copy(src_ref=s_ref, dst_ref=d_ref, send_sem=send_sems.at[a], recv_sem=recv_sems.at[a],
                                                 device_id=peer, device_id_type=MESH).start()
        token[...] = jnp.zeros_like(token)

    outs = pl.pallas_call(
        body, name=name,
        out_shape=(_DMA((n,)), _DMA((n,)), _DMA((n,)), *[pltpu.HBM(s.shape, s.dtype) for s in srcs],
                   *[pltpu.HBM(l.shape, l.dtype) for l in lands], SDS((8, LANES), f32)),
        in_specs=[_HBM] * (2 * n), out_specs=(_SEM, _SEM, _SEM, *[_HBM] * (2 * n), pl.BlockSpec(memory_space=pltpu.VMEM)),
        input_output_aliases={i: 3 + i for i in range(2 * n)},
        compiler_params=pltpu.CompilerParams(has_side_effects=_EFFECT),
    )(*[_hbm(s) for s in srcs], *[_hbm(lax.empty(l.shape, l.dtype)) for l in lands])
    return dict(sems=outs[:3], srcs=list(outs[3:3 + n]), lands=list(outs[3 + n:3 + 2 * n]), token=outs[3 + 2 * n])


def _exchange_wait(started, waits_of, after, name, which=None):
    which = list(range(len(started["srcs"]))) if which is None else which
    srcs, lands = [started["srcs"][a] for a in which], [started["lands"][a] for a in which]
    n = len(which)

    def body(*refs):
        src = refs[:n]
        land = refs[n:2 * n]
        send_sems, recv_sems, self_sems = refs[2 * n:2 * n + 3]
        me = _index(_me())
        for pos, a in enumerate(which):
            seven, (s_ref, d_ref) = waits_of(a, src[pos], land[pos], me)
            both = pltpu.make_async_remote_copy(src_ref=seven, dst_ref=seven, send_sem=send_sems.at[a], recv_sem=recv_sems.at[a],
                                                device_id=_me(), device_id_type=MESH)
            both.wait_send()
            both.wait_recv()
            pltpu.make_async_copy(s_ref, d_ref, self_sems.at[a]).wait()

    outs = pl.pallas_call(
        body, name=name, out_shape=tuple(pltpu.HBM(t.shape, t.dtype) for t in srcs + lands),
        in_specs=[_HBM] * (2 * n) + [_SEM] * 3 + [_ANY], out_specs=tuple([_HBM] * (2 * n)),
        input_output_aliases={i: i for i in range(2 * n)},
        compiler_params=pltpu.CompilerParams(has_side_effects=_EFFECT),
    )(*srcs, *lands, *started["sems"], after)
    return list(outs[n:])


def gather_start(locs, axes, name):
    lands = [SDS(tuple(N_DEV * d if i == ax else d for i, d in enumerate(l.shape)), l.dtype) for l, ax in zip(locs, axes)]

    def copies_of(a, src, land, me):
        mine = _win(land, axes[a], me, src.shape[axes[a]])
        return [(src, mine, peer) for peer in _peers()] + [(src, mine, None)]

    return _exchange_start(locs, lands, copies_of, name)


def gather_wait(started, axes, after, name, which=None):
    def waits_of(a, src, land, me):
        size = src.shape[axes[a]]
        return _win(land, axes[a], 0, size, N_DEV - 1), (src, _win(land, axes[a], me, size))

    return _exchange_wait(started, waits_of, after, name, which)


def scatter_start(grads, axes, name):
    lands = [SDS((N_DEV,) + tuple(d // N_DEV if i == ax else d for i, d in enumerate(g.shape)), g.dtype) for g, ax in zip(grads, axes)]

    def copies_of(a, src, land, me):
        size = src.shape[axes[a]] // N_DEV
        out = [(_win(src, axes[a], _index(peer), size), land.at[me], peer) for peer in _peers()]
        return out + [(_win(src, axes[a], me, size), land.at[me], None)]

    return _exchange_start(grads, lands, copies_of, name)


def scatter_wait(started, axes, after, name):
    def waits_of(a, src, land, me):
        size = src.shape[axes[a]] // N_DEV
        return land.at[pl.ds(0, N_DEV - 1)], (_win(src, axes[a], me, size), land.at[me])

    return _exchange_wait(started, waits_of, after, name)


def _row_tile(rows, cap=512):
    return max(t for t in range(8, min(rows, cap) + 1, 8) if rows % t == 0)


_SMALL = [
    ("ln_mix_pre", (2, 1024)), ("ln_mix_post", (2, 1024)), ("ln_ffn_pre", (2, 1024)), ("ln_ffn_post", (2, 1024)),
    ("ln_mem", (2, 1024)), ("w_spatial", (1, 6, 128, 128)), ("b_spatial", (1, 6, 128)), ("ln_shared", (1024,)),
    ("b_forget", (12,)), ("ln_v_g", (1, 768)), ("ln_v_b", (1, 768)),
]
_SMALL_TILE = 8 * LANES


def _small_rows(shape):
    return -(-math.prod(shape) // _SMALL_TILE) * 8


def _pack_small(vals, shapes):
    parts = []
    for name, shape in shapes:
        flat = vals[name].reshape(-1).astype(f32)
        rows = _small_rows(shape)
        parts.append(jnp.pad(flat, (0, rows * LANES - flat.shape[0])).reshape(rows, LANES))
    return jnp.concatenate(parts, axis=0)


def _unpack_small(buf, shapes):
    out = {}
    lo = 0
    for name, shape in shapes:
        rows = _small_rows(shape)
        out[name] = buf[lo:lo + rows].reshape(-1)[:math.prod(shape)].reshape(shape)
        lo += rows
    return out


def kernel(x, mem, ln_mix_pre, ln_mix_post, ln_ffn_pre, ln_ffn_post, ln_mem, w_mem_kv, w_out, w_ffn_gate, w_ffn_up, w_ffn_down, w_in_a, w_spatial, b_spatial, ln_v_g, ln_v_b, ln_shared, w_shared_kv, b_forget, w_in_b, loss_target, m_ln_mix_pre, m_ln_mix_post, m_ln_ffn_pre, m_ln_ffn_post, m_ln_mem, m_w_mem_kv, m_w_out, m_w_ffn_gate, m_w_ffn_up, m_w_ffn_down, m_w_in_a, m_w_spatial, m_b_spatial, m_ln_v_g, m_ln_v_b, m_ln_shared, m_w_shared_kv, m_b_forget, m_w_in_b, v_ln_mix_pre, v_ln_mix_post, v_ln_ffn_pre, v_ln_ffn_post, v_ln_mem, v_w_mem_kv, v_w_out, v_w_ffn_gate, v_w_ffn_up, v_w_ffn_down, v_w_in_a, v_w_spatial, v_b_spatial, v_ln_v_g, v_ln_v_b, v_ln_shared, v_w_shared_kv, v_b_forget, v_w_in_b):
    weights = dict(ln_mix_pre=ln_mix_pre, ln_mix_post=ln_mix_post, ln_ffn_pre=ln_ffn_pre, ln_ffn_post=ln_ffn_post, ln_mem=ln_mem,
                   w_mem_kv=w_mem_kv, w_out=w_out, w_ffn_gate=w_ffn_gate, w_ffn_up=w_ffn_up, w_ffn_down=w_ffn_down, w_in_a=w_in_a,
                   w_spatial=w_spatial, b_spatial=b_spatial, ln_v_g=ln_v_g, ln_v_b=ln_v_b, ln_shared=ln_shared,
                   w_shared_kv=w_shared_kv, b_forget=b_forget, w_in_b=w_in_b)
    mom_m = dict(ln_mix_pre=m_ln_mix_pre, ln_mix_post=m_ln_mix_post, ln_ffn_pre=m_ln_ffn_pre, ln_ffn_post=m_ln_ffn_post, ln_mem=m_ln_mem,
                 w_mem_kv=m_w_mem_kv, w_out=m_w_out, w_ffn_gate=m_w_ffn_gate, w_ffn_up=m_w_ffn_up, w_ffn_down=m_w_ffn_down, w_in_a=m_w_in_a,
                 w_spatial=m_w_spatial, b_spatial=m_b_spatial, ln_v_g=m_ln_v_g, ln_v_b=m_ln_v_b, ln_shared=m_ln_shared,
                 w_shared_kv=m_w_shared_kv, b_forget=m_b_forget, w_in_b=m_w_in_b)
    mom_v = dict(ln_mix_pre=v_ln_mix_pre, ln_mix_post=v_ln_mix_post, ln_ffn_pre=v_ln_ffn_pre, ln_ffn_post=v_ln_ffn_post, ln_mem=v_ln_mem,
                 w_mem_kv=v_w_mem_kv, w_out=v_w_out, w_ffn_gate=v_w_ffn_gate, w_ffn_up=v_w_ffn_up, w_ffn_down=v_w_ffn_down, w_in_a=v_w_in_a,
                 w_spatial=v_w_spatial, b_spatial=v_b_spatial, ln_v_g=v_ln_v_g, ln_v_b=v_ln_v_b, ln_shared=v_ln_shared,
                 w_shared_kv=v_w_shared_kv, b_forget=v_b_forget, w_in_b=v_w_in_b)
    names = list(weights)
    mx, my, mc = lax.axis_index("x"), lax.axis_index("y"), lax.axis_index("c")
    me = 4 * mx + 2 * my + mc

    h0 = x[0]
    mem0 = mem[0]
    tgt = loss_target[0]
    seq = h0.shape[0]

    vec = lambda a: a.reshape(1, -1)
    pad_to = lambda a, axis, size: jnp.pad(a, [(0, size - a.shape[i] if i == axis else 0) for i in range(a.ndim)])

    def after(tok, a):
        return a + tok[0, 0].astype(a.dtype)

    lnv_loc = pad_to(jnp.concatenate([ln_v_g, ln_v_b], axis=0), 0, 8)
    st_a = gather_start([w_in_a.astype(bf16), pad_to(lnv_loc, 1, LANES)[None]], [0, 0], "gather_a_start")
    mix_locs = lambda l, tok: [after(tok, w_mem_kv[l]).astype(bf16), w_out[l].astype(bf16)]

    def ffn_gather_start(l, tok):
        gate_up = gather_start([pad_to(after(tok, w_ffn_gate[l]).astype(bf16), 1, FF_SHARD_PAD),
                                pad_to(w_ffn_up[l].astype(bf16), 1, FF_SHARD_PAD)], [1, 1], f"gather_gate_up{l}_start")
        down = gather_start([pad_to(after(gate_up["token"], w_ffn_down[l]).astype(bf16), 0, FF_SHARD_PAD)], [0], f"gather_down{l}_start")
        return gate_up, down

    st_b = [gather_start(mix_locs(0, st_a["token"]), [0, 0], "gather_b0_start"), None]
    st_c = ffn_gather_start(0, st_b[0]["token"])
    st_d = gather_start([after(st_c[1]["token"], w_in_b[0]).astype(bf16), pad_to(w_shared_kv.astype(bf16), 1, KV_PAD)], [0, 0],
                        "gather_d_start")
    st_b[1] = gather_start(mix_locs(1, st_d["token"]), [0, 0], "gather_b1_start")
    st_e = ffn_gather_start(1, st_b[1]["token"])
    ws = w_spatial[0].astype(bf16)
    ws_t = ws.transpose(0, 2, 1)
    bs_t = b_spatial[0].T

    (a0,) = rms_fwd(h0, [after(st_e[1]["token"], vec(ln_mix_pre[0]))], "a0_norm")
    w_in_a8, lnv8 = gather_wait(st_a, [0, 0], a0, "gather_a_wait")
    w_in_a_full = w_in_a8.transpose(1, 0, 2).reshape(D_MODEL, -1)
    lnv_g = lnv8[:, 0, :MAIN_WIDTH // N_DEV].reshape(1, MAIN_WIDTH)
    lnv_b = lnv8[:, 1, :MAIN_WIDTH // N_DEV].reshape(1, MAIN_WIDTH)
    proj0 = mm(a0, w_in_a_full, "proj0", tn=896)
    main0 = gmlp_fwd(proj0, ws, bs_t, lnv_g, lnv_b, "gmlp_fwd", out_width=D_MODEL)
    w_mkv, w_o = [None, None], [None, None]
    w_mkv[0], w_o[0] = gather_wait(st_b[0], [0, 0], main0, "gather_b0_wait")
    (memn0,) = rms_fwd(mem0, [vec(ln_mem[0])], "mem0_norm")
    kvm0 = mm(memn0, w_mkv[0], "kvm0")
    mixed0 = mem_attn_fwd(proj0, 2 * MAIN_WIDTH // MEM_WIDTH, kvm0, main0, "mem_attn0")
    y1_0, hmid0, f0 = mm_resnorm(mixed0, w_o[0], h0, vec(ln_mix_post[0]), [vec(ln_ffn_pre[0])], "mix_out0")
    w_g0, w_u0 = gather_wait(st_c[0], [1, 1], f0, "gather_gate_up0_wait")
    gu0, act0 = ffn_up(f0, w_g0, w_u0, "ffn_up0")
    (w_d0,) = gather_wait(st_c[1], [0], act0, "gather_down0_wait")
    y2_0, h1, a1, sin1 = mm_resnorm(act0, w_d0, hmid0, vec(ln_ffn_post[0]), [vec(ln_mix_pre[1]), vec(ln_shared)], "ffn_down0")

    w_inb, w_kv = gather_wait(st_d, [0, 0], sin1, "gather_d_wait")
    kvb = mm(sin1, w_kv, "kv_shared", out_dtype=bf16, tn=MAIN_WIDTH, ncols=2 * MAIN_WIDTH)
    zf = mm(sin1, w_kv, "forget_logits", tn=256, col0=2 * MAIN_WIDTH, ncols=256)
    qb = mm(a1, w_inb, "proj1", out_dtype=bf16)
    z_t = jnp.pad(zf[:, :FOX_HEADS].T, ((0, 16 - FOX_HEADS), (0, 0)))
    bf_col = jnp.pad(b_forget, (0, 16 - FOX_HEADS)).reshape(16, 1)
    c_t = fgate_fwd(z_t, bf_col, "fgate_fwd")
    c_row = c_t[:FOX_HEADS].reshape(FOX_PAIRS, 2, seq)
    main1, lse, main1_b = fox_fwd(qb, kvb, c_row, "fox_fwd", out_width=D_MODEL)
    w_mkv[1], w_o[1] = gather_wait(st_b[1], [0, 0], main1, "gather_b1_wait")
    (memn1,) = rms_fwd(mem0, [vec(ln_mem[1])], "mem1_norm")
    kvm1 = mm(memn1, w_mkv[1], "kvm1")
    mixed1 = mem_attn_fwd(qb, MAIN_WIDTH // MEM_WIDTH, kvm1, main1_b, "mem_attn1")
    y1_1, hmid1, f1 = mm_resnorm(mixed1, w_o[1], h1, vec(ln_mix_post[1]), [vec(ln_ffn_pre[1])], "mix_out1")
    w_g1, w_u1 = gather_wait(st_e[0], [1, 1], f1, "gather_gate_up1_wait")
    gu1, act1 = ffn_up(f1, w_g1, w_u1, "ffn_up1")
    (w_d1,) = gather_wait(st_e[1], [0], act1, "gather_down1_wait")
    dh, d_y2_1, dg_fpost1, loss_tile = mm_resnorm_loss(act1, w_d1, hmid1, vec(ln_ffn_post[1]), tgt, "ffn_down1_loss")
    ffn_w = [(w_g0, w_u0, w_d0), (w_g1, w_u1, w_d1)]

    small = {}

    def ffn_backward(layer, dh_out, d_y2, hmid, f, gu, act, y1):
        w_g, w_u, w_d = ffn_w[layer]
        dw_down = mm_tn(act, d_y2, f"dw_down{layer}")
        rs_down = scatter_start([dw_down], [0], f"scatter_down{layer}_start")
        d_g, d_u = ffn_act_grad(d_y2, w_d, gu, f"ffn_act_grad{layer}")
        dw_g = mm_tn(f, d_g, f"dw_gate{layer}", dep=rs_down["token"])
        dw_u = mm_tn(f, d_u, f"dw_up{layer}")
        rs_gate_up = scatter_start([dw_g, dw_u], [1, 1], f"scatter_gate_up{layer}_start")
        dh_mid, d_y1, dg_fpre, dg_mpost = ffn_in_grad(d_g, d_u, w_g, w_u, hmid, dh_out, after(rs_gate_up["token"], vec(ln_ffn_pre[layer])),
                                                      y1, vec(ln_mix_post[layer]), f"ffn_in_grad{layer}")
        return dh_mid, d_y1, dg_fpre, dg_mpost, (rs_down, rs_gate_up)

    def mix_out_backward(layer, d_y1, mixed):
        dw_out = mm_tn(mixed, d_y1, f"dw_out{layer}")
        d_mixed = mm(d_y1, w_o[layer], f"d_mixed{layer}", trans_b=True)
        return d_mixed, dw_out

    def mem_backward(layer, q_src, q_block, kvm, memn, d_mixed, into):
        d_qm, d_kvm = mem_attn_bwd(q_src, q_block, kvm, d_mixed, into, f"mem_attn_bwd{layer}")
        d_kvm_b = d_kvm.astype(bf16)
        dw_mkv = mm_tn(memn, d_kvm_b, f"dw_mem_kv{layer}")
        d_memn = mm(d_kvm_b, w_mkv[layer], f"d_memn{layer}", trans_b=True)
        _, dg_mem = rms_bwd(mem0, vec(ln_mem[layer]), d_memn, None, bf16, f"mem_norm_bwd{layer}")
        return d_qm, dw_mkv, dg_mem


    dh_mid1, d_y1_1, dg_fpre1, dg_mpost1, rs_ffn1 = ffn_backward(1, dh, d_y2_1, hmid1, f1, gu1, act1, y1_1)
    d_mixed1, dw_out1 = mix_out_backward(1, d_y1_1, mixed1)
    dq_b, dk, dv, dc = fox_bwd(qb, kvb, d_mixed1, main1, lse, c_row, "fox_bwd", dq_width=D_MODEL)
    d_proj1, dw_mkv1, dg_mem1 = mem_backward(1, qb, MAIN_WIDTH // MEM_WIDTH, kvm1, memn1, d_mixed1, dq_b)
    rs_mix1 = scatter_start([dw_out1, dw_mkv1], [0, 0], "scatter_mix1_start")
    dc_t = jnp.pad(dc.reshape(FOX_HEADS, seq), ((0, 16 - FOX_HEADS), (0, 0)))
    dz_t, db_f = fgate_bwd(dc_t, z_t, bf_col, "fgate_bwd")
    d_kvf = jnp.concatenate([dk, dv, jnp.pad(dz_t[:FOX_HEADS].T.astype(bf16), ((0, 0), (0, KV_PAD - KV_WIDTH)))], axis=-1)
    dw_in_b = mm_tn(a1, d_proj1, "dw_in_b", dep=rs_mix1["token"])
    dw_kv = mm_tn(sin1, d_kvf, "dw_kv", tn=896)
    rs_2 = scatter_start([dw_in_b, dw_kv], [0, 0], "scatter_shared_start")
    dh1, (dg_pre1, dg_shared), d_y2_0, dg_fpost0 = proj_in_grad(
        [(d_proj1, w_inb, vec(ln_mix_pre[1])), (d_kvf, w_kv, vec(ln_shared))], h1, dh_mid1, "in_grad1", dep=rs_2["token"],
        below=(y2_0, vec(ln_ffn_post[0])))

    dh_mid0, d_y1_0, dg_fpre0, dg_mpost0, rs_ffn0 = ffn_backward(0, dh1, d_y2_0, hmid0, f0, gu0, act0, y1_0)
    d_mixed0, dw_out0 = mix_out_backward(0, d_y1_0, mixed0)
    d_uv, dw_s, db_s, dg_lnv, db_lnv = gmlp_bwd(proj0, d_mixed0, ws, ws_t, bs_t, lnv_g, lnv_b, "gmlp_bwd", out_width=w_in_a_full.shape[1])
    d_proj0, dw_mkv0, dg_mem0 = mem_backward(0, proj0, 2 * MAIN_WIDTH // MEM_WIDTH, kvm0, memn0, d_mixed0, d_uv)
    rs_mix0 = scatter_start([dw_out0, dw_mkv0], [0, 0], "scatter_mix0_start")

    small["ln_mix_pre"] = jnp.concatenate([jnp.zeros_like(dg_pre1), dg_pre1], axis=0)
    small["ln_mix_post"] = jnp.concatenate([dg_mpost0, dg_mpost1], axis=0)
    small["ln_ffn_pre"] = jnp.concatenate([dg_fpre0, dg_fpre1], axis=0)
    small["ln_ffn_post"] = jnp.concatenate([dg_fpost0, dg_fpost1], axis=0)
    small["ln_mem"] = jnp.concatenate([dg_mem0, dg_mem1], axis=0)
    small["w_spatial"] = dw_s[None]
    small["b_spatial"] = db_s[:, :A_GROUPS].T[None]
    small["ln_shared"] = dg_shared[0]
    small["b_forget"] = db_f[:FOX_HEADS, 0]
    small["ln_v_g"] = dg_lnv
    small["ln_v_b"] = db_lnv
    small_rows = jnp.concatenate([_pack_small(small, _SMALL), after(rs_mix0["token"], loss_tile)], axis=0)
    st_small = gather_start([small_rows[None]], [0], "gather_small_grads_start")
    dw_in_a = mm_tn(a0, d_proj0, "dw_in_a", tn=896, dep=st_small["token"])
    rs_in_a = scatter_start([dw_in_a.reshape(D_MODEL, N_DEV, -1).transpose(1, 0, 2)], [0], "scatter_in_a_start")
    grad_x, (dg_pre0,) = proj_in_grad([(d_proj0, w_in_a_full, vec(ln_mix_pre[0]))], h0, dh_mid0, "in_grad0", dep=rs_in_a["token"])
    st_last = gather_start([dg_pre0.reshape(1, 8, LANES)], [0], "gather_last_grad_start")

    def owned(started, axes, wait_after, name):
        recv = scatter_wait(started, axes, wait_after, name)
        return [sum_leading(r.reshape((N_DEV, -1, r.shape[-1])), f"{name}_sum{i}", tr=_row_tile(math.prod(r.shape[1:-1])))
                for i, r in enumerate(recv)]

    (g_down1,) = owned(rs_ffn1[0], [0], after(st_last["token"], grad_x[:8, :LANES]), "scatter_down1_wait")
    g_gu1 = owned(rs_ffn1[1], [1, 1], g_down1, "scatter_gate_up1_wait")
    g_mix1 = owned(rs_mix1, [0, 0], g_gu1[0], "scatter_mix1_wait")
    g2 = owned(rs_2, [0, 0], g_mix1[0], "scatter_shared_wait")
    (g_down0,) = owned(rs_ffn0[0], [0], g2[0], "scatter_down0_wait")
    g_gu0 = owned(rs_ffn0[1], [1, 1], g_down0, "scatter_gate_up0_wait")
    g_mix0 = owned(rs_mix0, [0, 0], g_gu0[0], "scatter_mix0_wait")
    (g_in_a,) = owned(rs_in_a, [0], g_mix0[0], "scatter_in_a_wait")
    g_local = dict(
        w_ffn_gate=jnp.stack([g_gu0[0], g_gu1[0]])[:, :, :FF_SHARD], w_ffn_up=jnp.stack([g_gu0[1], g_gu1[1]])[:, :, :FF_SHARD],
        w_ffn_down=jnp.stack([g_down0, g_down1])[:, :FF_SHARD], w_out=jnp.stack([g_mix0[0], g_mix1[0]]),
        w_mem_kv=jnp.stack([g_mix0[1], g_mix1[1]]), w_in_b=g2[0][None], w_shared_kv=g2[1][:, :KV_WIDTH], w_in_a=g_in_a[None])
    (small_all,) = gather_wait(st_small, [0], g_in_a, "gather_small_grads_wait")
    (last_all,) = gather_wait(st_last, [0], small_all, "gather_last_grad_wait")
    small_sum = sum_leading(small_all, "sum_small_grads")
    loss = small_sum[small_rows.shape[0] - 1, 0]
    g_small = _unpack_small(small_sum, _SMALL)
    g_small["ln_mix_pre"] = jnp.concatenate([sum_leading(last_all, "sum_last_grad").reshape(1, D_MODEL), g_small["ln_mix_pre"][1:]], axis=0)
    shard = MAIN_WIDTH // N_DEV
    for n in ("ln_v_g", "ln_v_b"):
        g_small[n] = lax.dynamic_slice_in_dim(g_small[n], me * shard, shard, axis=1)
    grad_w = {**g_small, **g_local}

    delta, new_m, new_v = {}, {}, {}
    for n in g_local:
        two_d = (-1, weights[n].shape[-1])
        d_, m_, v_ = adamw(weights[n].reshape(two_d), grad_w[n].reshape(two_d), mom_m[n].reshape(two_d), mom_v[n].reshape(two_d),
                           f"adamw_{n}", tr=_row_tile(math.prod(weights[n].shape[:-1])))
        delta[n], new_m[n], new_v[n] = (t.reshape(weights[n].shape) for t in (d_, m_, v_))
    small_local_shapes = [(n, tuple(weights[n].shape)) for n, _ in _SMALL]
    packed = [_pack_small(src, small_local_shapes) for src in (weights, grad_w, mom_m, mom_v)]
    outs = adamw(*packed, "adamw_small", tr=packed[0].shape[0])
    for dst, buf in zip((delta, new_m, new_v), outs):
        dst.update(_unpack_small(buf, small_local_shapes))

    return (loss, grad_x[None], *[grad_w[n] for n in names], *[delta[n] for n in names],
            *[new_m[n] for n in names], *[new_v[n] for n in names])
```

```python
import functools
import math

import jax
import jax.numpy as jnp
from jax import lax
from jax.experimental import pallas as pl
from jax.experimental.pallas import tpu as pltpu

f32 = jnp.float32
bf16 = jnp.bfloat16
SDS = jax.ShapeDtypeStruct

D_MODEL = 1024
MAIN_WIDTH = 768
MEM_WIDTH = 256
HEAD_DIM = 64
MEM_HEADS = 4
FOX_HEADS = 12
FOX_PAIRS = FOX_HEADS // 2
CHUNK = 128
A_GROUPS = 6
FF_SHARD = 352
FF_SHARD_PAD = 384
FF_PAD = 8 * FF_SHARD_PAD
KV_WIDTH = 2 * MAIN_WIDTH + FOX_HEADS
KV_PAD = 1792
RMS_EPS = 1e-6
LN_EPS = 1e-5
ATT_SCALE = HEAD_DIM ** -0.5
ADAM_LR, ADAM_B1, ADAM_B2, ADAM_EPS, ADAM_WD, ADAM_STEP = 0.001, 0.9, 0.999, 1e-08, 0.01, 10
N_DEV = 8
AXES = ("x", "y", "c")
MESH = pl.DeviceIdType.MESH
V7X_VMEM_LIMIT = 56 * 1024 * 1024
LANES = 128
FLAT_W = 512
ROW_PAD = 16


def _cparams(*sem):
    return pltpu.CompilerParams(dimension_semantics=sem or None, vmem_limit_bytes=V7X_VMEM_LIMIT)


def _dot(a, b):
    return jnp.dot(a, b, preferred_element_type=f32)


def _dot_nt(a, b):
    return lax.dot_general(a, b, (((1,), (1,)), ((), ())), preferred_element_type=f32)


def _dot_tn(a, b):
    return lax.dot_general(a, b, (((0,), (0,)), ((), ())), preferred_element_type=f32)


def _gelu(x):
    k = math.sqrt(2.0 / math.pi)
    t = jnp.tanh(k * (x + 0.044715 * x * x * x))
    return 0.5 * x * (1.0 + t), t


def _gelu_grad(x, t):
    k = math.sqrt(2.0 / math.pi)
    return 0.5 * (1.0 + t) + 0.5 * x * (1.0 - t * t) * k * (1.0 + 3.0 * 0.044715 * x * x)


def _sigmoid(x):
    return 1.0 / (1.0 + jnp.exp(-x))


def rms_fwd(x, gains, name, tm=512):
    m, d = x.shape
    tm = min(tm, m)
    n = len(gains)

    def body(x_ref, *refs):
        xv = x_ref[...]
        y = xv * lax.rsqrt(jnp.sum(xv * xv, axis=-1, keepdims=True) * (1.0 / d) + RMS_EPS)
        for g_ref, o_ref in zip(refs[:n], refs[n:]):
            o_ref[...] = (y * g_ref[...]).astype(bf16)

    row = pl.BlockSpec((tm, d), lambda i: (i, 0))
    vec = pl.BlockSpec((1, d), lambda i: (0, 0))
    return pl.pallas_call(body, grid=(m // tm,), in_specs=[row] + [vec] * n, out_specs=[row] * n,
                          out_shape=[SDS((m, d), bf16)] * n, name=name, compiler_params=_cparams("parallel"))(x, *gains)


def rms_bwd(x, g, dy, add, out_dtype, name, tm=512):
    m, d = x.shape
    tm = min(tm, m)
    has_add = add is not None

    def body(x_ref, g_ref, dy_ref, *refs):
        dx_ref, dg_ref = refs[-2], refs[-1]
        xv = x_ref[...]
        dyv = dy_ref[...].astype(f32)
        r = lax.rsqrt(jnp.sum(xv * xv, axis=-1, keepdims=True) * (1.0 / d) + RMS_EPS)
        xn = xv * r
        dyg = dyv * g_ref[...]
        dx = r * (dyg - xn * (jnp.sum(dyg * xn, axis=-1, keepdims=True) * (1.0 / d)))
        if has_add:
            dx = dx + refs[0][...]
        dx_ref[...] = dx.astype(out_dtype)

        @pl.when(pl.program_id(0) == 0)
        def _():
            dg_ref[...] = jnp.zeros_like(dg_ref)

        dg_ref[...] += jnp.sum(dyv * xn, axis=0, keepdims=True)

    row = pl.BlockSpec((tm, d), lambda i: (i, 0))
    vec = pl.BlockSpec((1, d), lambda i: (0, 0))
    ins = [x, g, dy] + ([add] if has_add else [])
    return pl.pallas_call(body, grid=(m // tm,), in_specs=[row, vec, row] + ([row] if has_add else []),
                          out_specs=[row, vec], out_shape=[SDS((m, d), out_dtype), SDS((1, d), f32)], name=name,
                          compiler_params=_cparams("arbitrary"))(*ins)


def mm(a, b, name, trans_b=False, out_dtype=f32, tm=1024, tn=1024, layer=None, col0=0, ncols=None, dep=None):
    m, k = a.shape
    n_all = b.shape[-2] if trans_b else b.shape[-1]
    n = n_all if ncols is None else ncols
    tm, tn = min(tm, m), min(tn, n)
    assert m % tm == 0 and n % tn == 0 and col0 % tn == 0 and not (trans_b and col0), (name, m, n, tm, tn)
    jb = col0 // tn
    lead = () if layer is None else (None,)
    sel = () if layer is None else (layer,)

    def body(a_ref, b_ref, *rest):
        r = _dot_nt(a_ref[...], b_ref[...]) if trans_b else _dot(a_ref[...], b_ref[...])
        rest[-1][...] = r.astype(out_dtype)

    if trans_b:
        b_spec = pl.BlockSpec(lead + (tn, k), lambda j, i: sel + (j, 0))
    else:
        b_spec = pl.BlockSpec(lead + (k, tn), lambda j, i: sel + (0, jb + j))
    deps = [] if dep is None else [dep]
    dep_specs = [pl.BlockSpec((8, LANES), lambda j, i: (0, 0))] * len(deps)
    return pl.pallas_call(body, grid=(n // tn, m // tm), in_specs=[pl.BlockSpec((tm, k), lambda j, i: (i, 0)), b_spec] + dep_specs,
                          out_specs=pl.BlockSpec((tm, tn), lambda j, i: (i, j)), out_shape=SDS((m, n), out_dtype),
                          name=name, compiler_params=_cparams("parallel", "parallel"))(a, b, *deps)


def mm_tn(a, g, name, tk=1024, tn=1024, out_dtype=bf16, dep=None):
    s, k = a.shape
    n = g.shape[1]
    tk, tn = min(tk, k), min(tn, n)
    assert k % tk == 0 and n % tn == 0, (name, k, n, tk, tn)

    def body(a_ref, g_ref, *rest):
        rest[-1][...] = _dot_tn(a_ref[...], g_ref[...]).astype(out_dtype)

    deps = [] if dep is None else [dep]
    dep_specs = [pl.BlockSpec((8, LANES), lambda i, j: (0, 0))] * len(deps)
    return pl.pallas_call(body, grid=(k // tk, n // tn),
                          in_specs=[pl.BlockSpec((s, tk), lambda i, j: (0, i)), pl.BlockSpec((s, tn), lambda i, j: (0, j))] + dep_specs,
                          out_specs=pl.BlockSpec((tk, tn), lambda i, j: (i, j)), out_shape=SDS((k, n), out_dtype), name=name,
                          compiler_params=_cparams("parallel", "parallel"))(a, g, *deps)


def _resident(shape, index_map):
    return pl.BlockSpec(shape, index_map, pipeline_mode=pl.Buffered(1))


def _rms(xv):
    return xv * lax.rsqrt(jnp.sum(xv * xv, axis=-1, keepdims=True) * (1.0 / xv.shape[-1]) + RMS_EPS)


def _rms_bwd_math(xv, g, dy):
    d = xv.shape[-1]
    r = lax.rsqrt(jnp.sum(xv * xv, axis=-1, keepdims=True) * (1.0 / d) + RMS_EPS)
    xn = xv * r
    dyg = dy * g
    dx = r * (dyg - xn * (jnp.sum(dyg * xn, axis=-1, keepdims=True) * (1.0 / d)))
    return dx, jnp.sum(dy * xn, axis=0, keepdims=True)


SUB_ROWS = 512


def mm_resnorm(a, b, h, g_post, gains, name, tm=512):
    m, k = a.shape
    d = b.shape[1]
    n = len(gains)

    def body(a_ref, b_ref, h_ref, gp_ref, *refs):
        for r in range(tm // SUB_ROWS):
            rows = slice(r * SUB_ROWS, (r + 1) * SUB_ROWS)
            y = _dot(a_ref[rows, :], b_ref[...])
            refs[n][rows, :] = y
            hn = h_ref[rows, :] + _rms(y) * gp_ref[...]
            refs[n + 1][rows, :] = hn
            if n:
                z = _rms(hn)
                for g_ref, o_ref in zip(refs[:n], refs[n + 2:]):
                    o_ref[rows, :] = (z * g_ref[...]).astype(bf16)

    row = pl.BlockSpec((tm, d), lambda i: (i, 0))
    vec = pl.BlockSpec((1, d), lambda i: (0, 0))
    return pl.pallas_call(body, grid=(m // tm,),
                          in_specs=[pl.BlockSpec((tm, k), lambda i: (i, 0)), _resident((k, d), lambda i: (0, 0)), row, vec] + [vec] * n,
                          out_specs=[row] * (n + 2), out_shape=[SDS((m, d), f32)] * 2 + [SDS((m, d), bf16)] * n, name=name,
                          compiler_params=_cparams("parallel"))(a, b, h, g_post, *gains)


def mm_resnorm_loss(a, b, h, g_post, tgt, name, tm=512):
    m, k = a.shape
    d = b.shape[1]

    def body(a_ref, b_ref, h_ref, gp_ref, t_ref, dh_ref, dy_ref, dg_ref, l_ref):
        @pl.when(pl.program_id(0) == 0)
        def _():
            dg_ref[...] = jnp.zeros_like(dg_ref)
            l_ref[...] = jnp.zeros_like(l_ref)

        y = _dot(a_ref[...], b_ref[...])
        e = h_ref[...] + _rms(y) * gp_ref[...] - t_ref[...]
        dh = e * (1.0 / d)
        dh_ref[...] = dh
        part = jnp.sum(jnp.sum(e * e, axis=-1, keepdims=True), axis=0, keepdims=True) * (0.5 / d)
        l_ref[...] += jnp.broadcast_to(part, l_ref.shape)
        dy, dg = _rms_bwd_math(y, gp_ref[...], dh)
        dy_ref[...] = dy.astype(bf16)
        dg_ref[...] += dg

    row = pl.BlockSpec((tm, d), lambda i: (i, 0))
    vec = pl.BlockSpec((1, d), lambda i: (0, 0))
    return pl.pallas_call(body, grid=(m // tm,),
                          in_specs=[pl.BlockSpec((tm, k), lambda i: (i, 0)), _resident((k, d), lambda i: (0, 0)), row, vec, row],
                          out_specs=[row, row, vec, pl.BlockSpec((8, LANES), lambda i: (0, 0))],
                          out_shape=[SDS((m, d), f32), SDS((m, d), bf16), SDS((1, d), f32), SDS((8, LANES), f32)], name=name,
                          compiler_params=_cparams("arbitrary"))(a, b, h, g_post, tgt)


def ffn_act_grad(d_y2, w_d, factors, name, tm=1024, tn=1536):
    s, d = d_y2.shape
    ff = w_d.shape[0]
    nb = ff // tn

    def body(a_ref, b_ref, g_ref, u_ref, dg_ref, du_ref):
        av = a_ref[...]
        tc = 256
        for c in range(tn // tc):
            cols = slice(c * tc, (c + 1) * tc)
            da = _dot_nt(av, b_ref[cols, :])
            dg_ref[:, cols] = (da * g_ref[:, cols].astype(f32)).astype(bf16)
            du_ref[:, cols] = (da * u_ref[:, cols].astype(f32)).astype(bf16)

    tile = pl.BlockSpec((tm, tn), lambda j, i: (i, j))
    return pl.pallas_call(body, grid=(nb, s // tm),
                          in_specs=[pl.BlockSpec((tm, d), lambda j, i: (i, 0)), pl.BlockSpec((tn, d), lambda j, i: (j, 0)), tile,
                                    pl.BlockSpec((tm, tn), lambda j, i: (i, nb + j))],
                          out_specs=[tile, tile], out_shape=[SDS((s, ff), bf16)] * 2, name=name,
                          compiler_params=_cparams("parallel", "parallel"))(d_y2, w_d, factors, factors)


def ffn_in_grad(d_g, d_u, w_g, w_u, hmid, dh_out, g_pre, y1, g_post, name, tm=512):
    s, ff = d_g.shape
    d = w_g.shape[0]

    def body(dg_ref, du_ref, wg_ref, wu_ref, hm_ref, dho_ref, gpre_ref, y1_ref, gpost_ref, dhm_ref, dy1_ref, dgpre_ref, dgpost_ref):
        @pl.when(pl.program_id(0) == 0)
        def _():
            dgpre_ref[...] = jnp.zeros_like(dgpre_ref)
            dgpost_ref[...] = jnp.zeros_like(dgpost_ref)

        for r in range(tm // SUB_ROWS):
            rows = slice(r * SUB_ROWS, (r + 1) * SUB_ROWS)
            d_f = _dot_nt(dg_ref[rows, :], wg_ref[...]) + _dot_nt(du_ref[rows, :], wu_ref[...])
            dx, dg1 = _rms_bwd_math(hm_ref[rows, :], gpre_ref[...], d_f)
            dh_mid = dho_ref[rows, :] + dx
            dhm_ref[rows, :] = dh_mid
            dgpre_ref[...] += dg1
            dy1, dg2 = _rms_bwd_math(y1_ref[rows, :], gpost_ref[...], dh_mid)
            dy1_ref[rows, :] = dy1.astype(bf16)
            dgpost_ref[...] += dg2

    row = pl.BlockSpec((tm, d), lambda i: (i, 0))
    vec = pl.BlockSpec((1, d), lambda i: (0, 0))
    wide = pl.BlockSpec((tm, ff), lambda i: (i, 0))
    w_spec = _resident((d, ff), lambda i: (0, 0))
    return pl.pallas_call(body, grid=(s // tm,), in_specs=[wide, wide, w_spec, w_spec, row, row, vec, row, vec],
                          out_specs=[row, row, vec, vec], out_shape=[SDS((s, d), f32), SDS((s, d), bf16), SDS((1, d), f32), SDS((1, d), f32)],
                          name=name, compiler_params=_cparams("arbitrary"))(d_g, d_u, w_g, w_u, hmid, dh_out, g_pre, y1, g_post)


def proj_in_grad(pairs, x, add, name, tm=512, dep=None, below=None):
    s, d = x.shape
    n = len(pairs)
    extra = [] if dep is None else [dep]
    n_below = 0 if below is None else 2

    def body(*refs):
        x_ref, add_ref = refs[3 * n], refs[3 * n + 1]
        below_refs = refs[3 * n + 2:3 * n + 2 + n_below]
        outs = refs[3 * n + 2 + n_below + len(extra):]

        @pl.when(pl.program_id(0) == 0)
        def _():
            for o in outs[1:1 + n] + outs[2 + n:]:
                o[...] = jnp.zeros_like(o)

        xv = x_ref[...]
        dx = add_ref[...]
        for i in range(n):
            a_ref, b_ref, g_ref = refs[3 * i:3 * i + 3]
            dxi, dgi = _rms_bwd_math(xv, g_ref[...], _dot_nt(a_ref[...], b_ref[...]))
            dx = dx + dxi
            outs[1 + i][...] += dgi
        outs[0][...] = dx
        if below is not None:
            dy, dg = _rms_bwd_math(below_refs[0][...], below_refs[1][...], dx)
            outs[1 + n][...] = dy.astype(bf16)
            outs[2 + n][...] += dg

    row = pl.BlockSpec((tm, d), lambda i: (i, 0))
    vec = pl.BlockSpec((1, d), lambda i: (0, 0))
    in_specs, args = [], []
    for a, b, g in pairs:
        k = a.shape[1]
        in_specs += [pl.BlockSpec((tm, k), lambda i: (i, 0)), _resident((d, k), lambda i: (0, 0)), vec]
        args += [a, b, g]
    in_specs += [row, row] + [row, vec][:n_below] + [pl.BlockSpec((8, LANES), lambda i: (0, 0))] * len(extra)
    out_specs = [row] + [vec] * n + [row, vec][:n_below]
    out_shape = [SDS((s, d), f32)] + [SDS((1, d), f32)] * n + [SDS((s, d), bf16), SDS((1, d), f32)][:n_below]
    out = pl.pallas_call(body, grid=(s // tm,), in_specs=in_specs, out_specs=out_specs, out_shape=out_shape, name=name,
                         compiler_params=_cparams("arbitrary"))(*args, x, add, *(below or ()), *extra)
    return (out[0], out[1:1 + n]) + tuple(out[1 + n:])


def ffn_up(f, wg, wu, name, tm=512, tc=256):
    s, d = f.shape
    ff = wg.shape[-1]

    def body(f_ref, wg_ref, wu_ref, fac_ref, act_ref):
        fv = f_ref[...]
        for j in range(ff // tc):
            lo = j * tc
            gg = _dot(fv, wg_ref[:, lo:lo + tc])
            uu = _dot(fv, wu_ref[:, lo:lo + tc])
            sg = _sigmoid(gg)
            silu = gg * sg
            fac_ref[:, lo:lo + tc] = (uu * (sg + silu * (1.0 - sg))).astype(bf16)
            fac_ref[:, ff + lo:ff + lo + tc] = silu.astype(bf16)
            act_ref[:, lo:lo + tc] = (silu * uu).astype(bf16)

    w_spec = _resident((d, ff), lambda i: (0, 0))
    return pl.pallas_call(body, grid=(s // tm,), in_specs=[pl.BlockSpec((tm, d), lambda i: (i, 0)), w_spec, w_spec],
                          out_specs=[pl.BlockSpec((tm, 2 * ff), lambda i: (i, 0)), pl.BlockSpec((tm, ff), lambda i: (i, 0))],
                          out_shape=[SDS((s, 2 * ff), bf16), SDS((s, ff), bf16)], name=name,
                          compiler_params=_cparams("parallel"))(f, wg, wu)


def _gmlp_forward_chunk(u, v, w_refs, bias, ln_g, ln_b):
    gu, tu = _gelu(u)
    gv, tv = _gelu(v)
    mu = jnp.sum(gv, axis=-1, keepdims=True) * (1.0 / MAIN_WIDTH)
    xc = gv - mu
    rstd = lax.rsqrt(jnp.sum(xc * xc, axis=-1, keepdims=True) * (1.0 / MAIN_WIDTH) + LN_EPS)
    xhat = xc * rstd
    vln = xhat * ln_g + ln_b
    row = lax.broadcasted_iota(jnp.int32, (CHUNK, CHUNK), 0)
    col = lax.broadcasted_iota(jnp.int32, (CHUNK, CHUNK), 1)
    s_parts = []
    for g in range(A_GROUPS):
        w = jnp.where(col <= row, w_refs[g], jnp.zeros((), bf16))
        s_parts.append(_dot(w, vln[:, g * CHUNK:(g + 1) * CHUNK].astype(bf16)) + bias[:, g:g + 1])
    return gu, tu, tv, rstd, xhat, vln, s_parts


def gmlp_fwd(proj, ws, bs_t, ln_g, ln_b, name, tm=512, out_width=MAIN_WIDTH):
    s = proj.shape[0]

    def body(u_ref, v_ref, w_ref, b_ref, g_ref, bb_ref, o_ref):
        bias = b_ref[...]
        for c in range(tm // CHUNK):
            rows = slice(c * CHUNK, (c + 1) * CHUNK)
            gu, _, _, _, _, _, s_parts = _gmlp_forward_chunk(u_ref[rows, :], v_ref[rows, :], w_ref, bias, g_ref[...], bb_ref[...])
            for g in range(A_GROUPS):
                cols = slice(g * CHUNK, (g + 1) * CHUNK)
                o_ref[rows, cols] = (gu[:, cols] * s_parts[g]).astype(bf16)

    vec = pl.BlockSpec((1, MAIN_WIDTH), lambda i: (0, 0))
    return pl.pallas_call(
        body, grid=(s // tm,),
        in_specs=[pl.BlockSpec((tm, MAIN_WIDTH), lambda i: (i, 0)), pl.BlockSpec((tm, MAIN_WIDTH), lambda i: (i, 1)),
                  pl.BlockSpec((A_GROUPS, CHUNK, CHUNK), lambda i: (0, 0, 0)), pl.BlockSpec((CHUNK, A_GROUPS), lambda i: (0, 0)), vec, vec],
        out_specs=pl.BlockSpec((tm, MAIN_WIDTH), lambda i: (i, 0)), out_shape=SDS((s, out_width), bf16), name=name,
        compiler_params=_cparams("parallel"))(proj, proj, ws, bs_t, ln_g, ln_b)


def gmlp_bwd(proj, d_mixed, ws, ws_t, bs_t, ln_g, ln_b, name, tm=512, out_width=2 * MAIN_WIDTH):
    s = proj.shape[0]

    def body(u_ref, v_ref, dm_ref, w_ref, wt_ref, b_ref, g_ref, bb_ref, duv_ref, dw_ref, db_ref, dg_ref, dbb_ref):
        @pl.when(pl.program_id(0) == 0)
        def _():
            dw_ref[...] = jnp.zeros_like(dw_ref)
            db_ref[...] = jnp.zeros_like(db_ref)
            dg_ref[...] = jnp.zeros_like(dg_ref)
            dbb_ref[...] = jnp.zeros_like(dbb_ref)

        bias = b_ref[...]
        ln_gv = g_ref[...]
        row = lax.broadcasted_iota(jnp.int32, (CHUNK, CHUNK), 0)
        col = lax.broadcasted_iota(jnp.int32, (CHUNK, CHUNK), 1)
        lane = lax.broadcasted_iota(jnp.int32, (CHUNK, LANES), 1)
        for c in range(tm // CHUNK):
            rows = slice(c * CHUNK, (c + 1) * CHUNK)
            u = u_ref[rows, :]
            v = v_ref[rows, :]
            gu, tu, tv, rstd, xhat, vln, s_parts = _gmlp_forward_chunk(u, v, w_ref, bias, ln_gv, bb_ref[...])
            dm = dm_ref[rows, :]
            d_vln_parts = []
            d_gu_parts = []
            db_acc = jnp.zeros((CHUNK, LANES), f32)
            for g in range(A_GROUPS):
                cols = slice(g * CHUNK, (g + 1) * CHUNK)
                dmg = dm[:, cols]
                d_gu_parts.append(dmg * s_parts[g])
                d_s = dmg * gu[:, cols]
                db_acc = db_acc + jnp.where(lane == g, jnp.sum(d_s, axis=-1, keepdims=True), 0.0)
                d_sb = d_s.astype(bf16)
                dw_ref[g] += jnp.where(col <= row, _dot_nt(d_sb, vln[:, cols].astype(bf16)), 0.0)
                wt = jnp.where(row <= col, wt_ref[g], jnp.zeros((), bf16))
                d_vln_parts.append(_dot(wt, d_sb))
            db_ref[...] += db_acc
            d_vln = jnp.concatenate(d_vln_parts, axis=-1)
            d_gu = jnp.concatenate(d_gu_parts, axis=-1)
            dg_ref[...] += jnp.sum(d_vln * xhat, axis=0, keepdims=True)
            dbb_ref[...] += jnp.sum(d_vln, axis=0, keepdims=True)
            dxh = d_vln * ln_gv
            m1 = jnp.sum(dxh, axis=-1, keepdims=True) * (1.0 / MAIN_WIDTH)
            m2 = jnp.sum(dxh * xhat, axis=-1, keepdims=True) * (1.0 / MAIN_WIDTH)
            d_gv = rstd * (dxh - m1 - xhat * m2)
            duv_ref[rows, :MAIN_WIDTH] = (d_gu * _gelu_grad(u, tu)).astype(bf16)
            duv_ref[rows, MAIN_WIDTH:] = (d_gv * _gelu_grad(v, tv)).astype(bf16)

    vec = pl.BlockSpec((1, MAIN_WIDTH), lambda i: (0, 0))
    wspec = pl.BlockSpec((A_GROUPS, CHUNK, CHUNK), lambda i: (0, 0, 0))
    return pl.pallas_call(
        body, grid=(s // tm,),
        in_specs=[pl.BlockSpec((tm, MAIN_WIDTH), lambda i: (i, 0)), pl.BlockSpec((tm, MAIN_WIDTH), lambda i: (i, 1)),
                  pl.BlockSpec((tm, MAIN_WIDTH), lambda i: (i, 0)), wspec, wspec, pl.BlockSpec((CHUNK, A_GROUPS), lambda i: (0, 0)), vec, vec],
        out_specs=[pl.BlockSpec((tm, 2 * MAIN_WIDTH), lambda i: (i, 0)), wspec, pl.BlockSpec((CHUNK, LANES), lambda i: (0, 0)), vec, vec],
        out_shape=[SDS((s, out_width), bf16), SDS((A_GROUPS, CHUNK, CHUNK), f32), SDS((CHUNK, LANES), f32),
                   SDS((1, MAIN_WIDTH), f32), SDS((1, MAIN_WIDTH), f32)],
        name=name, compiler_params=_cparams("arbitrary"))(proj, proj, d_mixed, ws, ws_t, bs_t, ln_g, ln_b)


def _head_mask(width, h):
    lane = lax.broadcasted_iota(jnp.int32, (1, width), 1)
    return (lane >= h * HEAD_DIM) & (lane < (h + 1) * HEAD_DIM)


def mem_attn_fwd(proj, q_block, kv, into, name, tm=512):
    s = proj.shape[0]
    n_mem = kv.shape[0]
    out_block = into.shape[1] // MEM_WIDTH - 1

    def body(q_ref, kv_ref, into_ref, o_ref):
        q = q_ref[...].astype(f32)
        k = kv_ref[:, :MEM_WIDTH].astype(bf16)
        v = kv_ref[:, MEM_WIDTH:].astype(bf16)
        out = jnp.zeros((tm, MEM_WIDTH), f32)
        for h in range(MEM_HEADS):
            msk = _head_mask(MEM_WIDTH, h)
            qh = jnp.where(msk, q, 0.0).astype(bf16)
            sc = _dot_nt(qh, k) * ATT_SCALE
            e = jnp.exp(sc - jnp.max(sc, axis=-1, keepdims=True))
            p = e / jnp.sum(e, axis=-1, keepdims=True)
            out = jnp.where(msk, _dot(p.astype(bf16), v), out)
        o_ref[...] = out.astype(bf16)

    return pl.pallas_call(body, grid=(s // tm,),
                          in_specs=[pl.BlockSpec((tm, MEM_WIDTH), lambda i: (i, q_block)), pl.BlockSpec((n_mem, 2 * MEM_WIDTH), lambda i: (0, 0)), _ANY],
                          out_specs=pl.BlockSpec((tm, MEM_WIDTH), lambda i: (i, out_block)), out_shape=SDS(into.shape, bf16), name=name,
                          input_output_aliases={2: 0}, compiler_params=_cparams("parallel"))(proj, kv, into)


def mem_attn_bwd(proj, q_block, kv, d_mixed, into, name, tm=512):
    s = proj.shape[0]
    n_mem = kv.shape[0]
    out_block = into.shape[1] // MEM_WIDTH - 1

    def body(q_ref, kv_ref, do_ref, into_ref, dq_ref, dkv_ref):
        @pl.when(pl.program_id(0) == 0)
        def _():
            dkv_ref[...] = jnp.zeros_like(dkv_ref)

        q = q_ref[...].astype(f32)
        do = do_ref[...]
        k = kv_ref[:, :MEM_WIDTH].astype(bf16)
        v = kv_ref[:, MEM_WIDTH:].astype(bf16)
        dq = jnp.zeros((tm, MEM_WIDTH), f32)
        dk = jnp.zeros((n_mem, MEM_WIDTH), f32)
        dv = jnp.zeros((n_mem, MEM_WIDTH), f32)
        for h in range(MEM_HEADS):
            msk = _head_mask(MEM_WIDTH, h)
            qh = jnp.where(msk, q, 0.0).astype(bf16)
            doh = jnp.where(msk, do, 0.0).astype(bf16)
            sc = _dot_nt(qh, k) * ATT_SCALE
            e = jnp.exp(sc - jnp.max(sc, axis=-1, keepdims=True))
            p = e / jnp.sum(e, axis=-1, keepdims=True)
            dp = _dot_nt(doh, v)
            ds = p * (dp - jnp.sum(dp * p, axis=-1, keepdims=True))
            dsb = (ds * ATT_SCALE).astype(bf16)
            dq = jnp.where(msk, _dot(dsb, k), dq)
            dk = dk + _dot_tn(dsb, qh)
            dv = dv + _dot_tn(p.astype(bf16), doh)
        dq_ref[...] = dq.astype(bf16)
        dkv_ref[:, :MEM_WIDTH] += dk
        dkv_ref[:, MEM_WIDTH:] += dv

    return pl.pallas_call(
        body, grid=(s // tm,),
        in_specs=[pl.BlockSpec((tm, MEM_WIDTH), lambda i: (i, q_block)), pl.BlockSpec((n_mem, 2 * MEM_WIDTH), lambda i: (0, 0)),
                  pl.BlockSpec((tm, MEM_WIDTH), lambda i: (i, MAIN_WIDTH // MEM_WIDTH)), _ANY],
        out_specs=[pl.BlockSpec((tm, MEM_WIDTH), lambda i: (i, out_block)), pl.BlockSpec((n_mem, 2 * MEM_WIDTH), lambda i: (0, 0))],
        out_shape=[SDS(into.shape, bf16), SDS((n_mem, 2 * MEM_WIDTH), f32)], name=name,
        input_output_aliases={3: 0}, compiler_params=_cparams("arbitrary"))(proj, kv, d_mixed, into)


def _tri(t, upper):
    r = lax.broadcasted_iota(jnp.int32, (t, t), 0)
    c = lax.broadcasted_iota(jnp.int32, (t, t), 1)
    return ((r <= c) if upper else (r >= c)).astype(f32)


def fgate_fwd(z_t, b, name, t=512):
    hh, s = z_t.shape

    def body(z_ref, b_ref, c_ref):
        u = _tri(t, True)
        carry = jnp.zeros((hh, 1), f32)
        for blk in range(s // t):
            x = z_ref[:, blk * t:(blk + 1) * t] + b_ref[...]
            logf = jnp.minimum(x, 0.0) - jnp.log(1.0 + jnp.exp(-jnp.abs(x)))
            y = jnp.dot(logf, u, precision=lax.Precision.HIGHEST, preferred_element_type=f32) + carry
            c_ref[:, blk * t:(blk + 1) * t] = y
            carry = y[:, t - 1:t]

    return pl.pallas_call(body, out_shape=SDS((hh, s), f32), name=name, compiler_params=_cparams())(z_t, b)


def fgate_bwd(dc_t, z_t, b, name, t=512):
    hh, s = z_t.shape

    def body(dc_ref, z_ref, b_ref, dz_ref, db_ref):
        low = _tri(t, False)
        carry = jnp.zeros((hh, 1), f32)
        total = jnp.zeros((hh, 1), f32)
        for blk in reversed(range(s // t)):
            cols = slice(blk * t, (blk + 1) * t)
            y = jnp.dot(dc_ref[:, cols], low, precision=lax.Precision.HIGHEST, preferred_element_type=f32) + carry
            carry = y[:, 0:1]
            dz = y * _sigmoid(-(z_ref[:, cols] + b_ref[...]))
            dz_ref[:, cols] = dz
            total = total + jnp.sum(dz, axis=-1, keepdims=True)
        db_ref[...] = jnp.broadcast_to(total, db_ref.shape)

    return pl.pallas_call(body, out_shape=[SDS((hh, s), f32), SDS((hh, LANES), f32)], name=name,
                          compiler_params=_cparams())(dc_t, z_t, b)


def _pair_masks():
    lane = lax.broadcasted_iota(jnp.int32, (1, LANES), 1)
    return [lane < HEAD_DIM, lane >= HEAD_DIM]


def _tile_base(cr_ref, hh, lo):
    return cr_ref[hh:hh + 1, pl.ds(lo, LANES)][:, 0:1]


def fox_fwd(q, kv, c_row, name, tq=512, out_width=MAIN_WIDTH):
    s = kv.shape[0]
    nq = s // tq

    def body(q_ref, k_ref, v_ref, cr_ref, o_ref, lse_ref, ob_ref):
        i = pl.program_id(1)
        qv = q_ref[...]
        masks = _pair_masks()
        row = lax.broadcasted_iota(jnp.int32, (tq, tq), 0)
        col = lax.broadcasted_iota(jnp.int32, (tq, tq), 1)
        qh = [jnp.where(masks[hh], qv, jnp.zeros((), bf16)) * ATT_SCALE for hh in range(2)]
        ct = [_tile_base(cr_ref, hh, pl.multiple_of(i * tq, tq)) for hh in range(2)]

        def block(j, carry, diag):
            lo = pl.multiple_of(j * tq, tq)
            ks = k_ref[pl.ds(lo, tq), :]
            vs = v_ref[pl.ds(lo, tq), :]
            out = []
            for hh in range(2):
                m, l, acc = carry[hh]
                sc = _dot_nt(qh[hh], ks) + (ct[hh] - cr_ref[hh:hh + 1, pl.ds(lo, tq)])
                if diag:
                    sc = jnp.where(col <= row, sc, -jnp.inf)
                m_new = jnp.maximum(m, jnp.max(sc, axis=-1, keepdims=True))
                alpha = jnp.exp(m - m_new)
                p = jnp.exp(sc - m_new)
                l = alpha * l + jnp.sum(p, axis=-1, keepdims=True)
                p_hi = p.astype(bf16)
                p_lo = (p - p_hi.astype(f32)).astype(bf16)
                acc = alpha * acc + (_dot(p_hi, vs) + _dot(p_lo, vs))
                out.append((m_new, l, acc))
            return tuple(out)

        init = (jnp.full((tq, 1), -jnp.inf, f32), jnp.zeros((tq, 1), f32), jnp.zeros((tq, LANES), f32))
        carry = lax.fori_loop(0, i, functools.partial(block, diag=False), (init, init))
        res = [(acc / l, m + jnp.log(l)) for m, l, acc in block(i, carry, True)]
        out = jnp.where(masks[0], res[0][0], res[1][0])
        o_ref[...] = out
        ob_ref[...] = out.astype(bf16)
        lse_ref[...] = jnp.where(masks[0], res[0][1], res[1][1])

    return pl.pallas_call(
        body, grid=(FOX_PAIRS, nq),
        in_specs=[pl.BlockSpec((tq, LANES), lambda p, i: (i, p)), pl.BlockSpec((s, LANES), lambda p, i: (0, p)),
                  pl.BlockSpec((s, LANES), lambda p, i: (0, FOX_PAIRS + p)), pl.BlockSpec((None, 2, s), lambda p, i: (p, 0, 0))],
        out_specs=[pl.BlockSpec((tq, LANES), lambda p, i: (i, p)), pl.BlockSpec((None, tq, LANES), lambda p, i: (p, i, 0)),
                   pl.BlockSpec((tq, LANES), lambda p, i: (i, p))],
        out_shape=[SDS((s, MAIN_WIDTH), f32), SDS((FOX_PAIRS, s, LANES), f32), SDS((s, out_width), bf16)], name=name,
        compiler_params=_cparams("parallel", "parallel"))(q, kv, kv, c_row)


def fox_bwd(q, kv, d_mixed, o, lse, c_row, name, tq=512, dq_width=MAIN_WIDTH):
    s = kv.shape[0]
    nq = s // tq

    def body(q_ref, k_ref, v_ref, do_ref, o_ref, lse_ref, cr_ref, dqb_ref, dk_ref, dv_ref, dc_ref, dq_ref):
        j = pl.program_id(1)

        @pl.when(j == 0)
        def _():
            dq_ref[...] = jnp.zeros_like(dq_ref)

        masks = _pair_masks()
        sub = lax.broadcasted_iota(jnp.int32, (LANES, 1), 0)
        sub_masks = [sub < HEAD_DIM, sub >= HEAD_DIM]
        row = lax.broadcasted_iota(jnp.int32, (tq, tq), 0)
        col = lax.broadcasted_iota(jnp.int32, (tq, tq), 1)
        kj = k_ref[...]
        vj = v_ref[...]
        lo_j = pl.multiple_of(j * tq, tq)

        def block(i, carry, diag):
            dk_t, dv_t, dc0, dc1 = carry
            dcs = [dc0, dc1]
            lo = pl.multiple_of(i * tq, tq)
            qi = q_ref[pl.ds(lo, tq), :]
            qi = qi * ATT_SCALE
            qt_i = qi.T
            doi = do_ref[pl.ds(lo, tq), :]
            dot_i = doi.astype(bf16).T
            prod = doi.astype(bf16).astype(f32) * o_ref[pl.ds(lo, tq), :]
            lse_i = lse_ref[pl.ds(lo, tq), :]
            dq_i = jnp.zeros((tq, LANES), f32)
            for hh in range(2):
                qh = jnp.where(masks[hh], qi, jnp.zeros((), bf16))
                doh = jnp.where(masks[hh], doi, 0.0).astype(bf16)
                delta = jnp.sum(jnp.where(masks[hh], prod, 0.0), axis=-1, keepdims=True)
                sc = _dot_nt(qh, kj) + (_tile_base(cr_ref, hh, lo) - cr_ref[hh:hh + 1, pl.ds(lo_j, tq)])
                p = jnp.exp(sc - lse_i[:, hh * HEAD_DIM:hh * HEAD_DIM + 1])
                if diag:
                    p = jnp.where(col <= row, p, 0.0)
                dv_t = dv_t + _dot(jnp.where(sub_masks[hh], dot_i, jnp.zeros((), bf16)), p.astype(bf16))
                ds = p * (_dot_nt(doh, vj) - delta)
                dcs[hh] = dcs[hh] + jnp.sum(ds, axis=0, keepdims=True)
                dsb = ds.astype(bf16)
                dq_i = jnp.where(masks[hh], _dot(dsb, kj), dq_i)
                dk_t = dk_t + _dot(jnp.where(sub_masks[hh], qt_i, jnp.zeros((), bf16)), dsb)
            dq_ref[pl.ds(lo, tq), :] += dq_i * ATT_SCALE
            return dk_t, dv_t, dcs[0], dcs[1]

        zero = jnp.zeros((LANES, tq), f32)
        zrow = jnp.zeros((1, tq), f32)
        carry = block(j, (zero, zero, zrow, zrow), True)
        dk_t, dv_t, dc0, dc1 = lax.fori_loop(j + 1, nq, functools.partial(block, diag=False), carry)
        dk_ref[...] = dk_t.T.astype(bf16)
        dv_ref[...] = dv_t.T.astype(bf16)
        dc_ref[0:1, :] = -dc0
        dc_ref[1:2, :] = -dc1

        @pl.when(j == nq - 1)
        def _():
            dqb_ref[...] = dq_ref[...].astype(bf16)

    full = lambda p, j: (0, p)
    tile = lambda p, j: (j, p)
    return pl.pallas_call(
        body, grid=(FOX_PAIRS, nq),
        in_specs=[pl.BlockSpec((s, LANES), full), pl.BlockSpec((tq, LANES), tile), pl.BlockSpec((tq, LANES), lambda p, j: (j, FOX_PAIRS + p)),
                  pl.BlockSpec((s, LANES), full), pl.BlockSpec((s, LANES), full), pl.BlockSpec((None, s, LANES), lambda p, j: (p, 0, 0)),
                  pl.BlockSpec((None, 2, s), lambda p, j: (p, 0, 0))],
        out_specs=[pl.BlockSpec((s, LANES), full), pl.BlockSpec((tq, LANES), tile), pl.BlockSpec((tq, LANES), tile),
                   pl.BlockSpec((None, 2, tq), lambda p, j: (p, 0, j))],
        out_shape=[SDS((s, dq_width), bf16), SDS((s, MAIN_WIDTH), bf16), SDS((s, MAIN_WIDTH), bf16), SDS((FOX_PAIRS, 2, s), f32)],
        scratch_shapes=[pltpu.VMEM((s, LANES), f32)],
        name=name, compiler_params=_cparams("parallel", "arbitrary"))(q, kv, kv, d_mixed, o, lse, c_row)


def adamw(w, g, m, v, name, tr=256):
    r, c = w.shape
    tr = min(tr, r)
    assert r % tr == 0, (name, r, tr)
    c1 = 1.0 / (1.0 - ADAM_B1 ** ADAM_STEP)
    c2 = 1.0 / (1.0 - ADAM_B2 ** ADAM_STEP)

    def body(w_ref, g_ref, m_ref, v_ref, d_ref, mo_ref, vo_ref):
        gv = g_ref[...]
        mn = ADAM_B1 * m_ref[...] + (1.0 - ADAM_B1) * gv
        vn = ADAM_B2 * v_ref[...] + (1.0 - ADAM_B2) * gv * gv
        mo_ref[...] = mn
        vo_ref[...] = vn
        d_ref[...] = -ADAM_LR * ((mn * c1) / (jnp.sqrt(vn * c2) + ADAM_EPS) + ADAM_WD * w_ref[...])

    spec = pl.BlockSpec((tr, c), lambda i: (i, 0))
    return pl.pallas_call(body, grid=(r // tr,), in_specs=[spec] * 4, out_specs=[spec] * 3, out_shape=[SDS((r, c), f32)] * 3,
                          name=name, compiler_params=_cparams("parallel"))(w, g, m, v)


def sum_leading(x, name, out_dtype=f32, tr=None):
    n, r, c = x.shape
    tr = tr or r
    assert r % tr == 0

    def body(x_ref, o_ref):
        acc = x_ref[0].astype(f32)
        for k in range(1, n):
            acc = acc + x_ref[k].astype(f32)
        o_ref[...] = acc.astype(out_dtype)

    return pl.pallas_call(body, grid=(r // tr,), in_specs=[pl.BlockSpec((n, tr, c), lambda i: (0, i, 0))],
                          out_specs=pl.BlockSpec((tr, c), lambda i: (i, 0)), out_shape=SDS((r, c), out_dtype), name=name,
                          compiler_params=_cparams("parallel"))(x)


_ANY = pl.BlockSpec(memory_space=pl.ANY)
_DMA = pltpu.SemaphoreType.DMA


_HBM = pl.BlockSpec(memory_space=pltpu.HBM)
_SEM = pl.BlockSpec(memory_space=pltpu.SEMAPHORE)
_EFFECT = pltpu.SideEffectType.DATAFLOW_SIDE_EFFECTING
_FLIPS = [(0, 0, 1), (1, 0, 0), (0, 1, 0), (1, 1, 0), (1, 0, 1), (0, 1, 1), (1, 1, 1)]


def _me():
    return lax.axis_index("x"), lax.axis_index("y"), lax.axis_index("c")


def _peers():
    mx, my, mc = _me()
    return [(jnp.bitwise_xor(mx, fx), jnp.bitwise_xor(my, fy), jnp.bitwise_xor(mc, fc)) for fx, fy, fc in _FLIPS]


def _index(dev):
    return 4 * dev[0] + 2 * dev[1] + dev[2]


def _win(ref, axis, k, size, count=1):
    idx = [slice(None)] * len(ref.shape)
    idx[axis] = pl.ds(k * size, count * size)
    return ref.at[tuple(idx)]


def _hbm(a):
    return pltpu.with_memory_space_constraint(a, pltpu.HBM)


def _exchange_start(srcs, lands, copies_of, name):
    n = len(srcs)

    def body(*refs):
        src = refs[:n]
        send_sems, recv_sems, self_sems = refs[2 * n:2 * n + 3]
        land = refs[3 * n + 3:4 * n + 3]
        token = refs[4 * n + 3]
        me = _index(_me())
        for a in range(n):
            for s_ref, d_ref, peer in copies_of(a, src[a], land[a], me):
                if peer is None:
                    pltpu.make_async_copy(s_ref, d_ref, self_sems.at[a]).start()
                else:
                    pltpu.make_async_remote_copy(src_ref=s_ref, dst_ref=d_ref, send_sem=send_sems.at[a], recv_sem=recv_sems.at[a],
                                                 device_id=peer, device_id_type=MESH).start()
        token[...] = jnp.zeros_like(token)

    outs = pl.pallas_call(
        body, name=name,
        out_shape=(_DMA((n,)), _DMA((n,)), _DMA((n,)), *[pltpu.HBM(s.shape, s.dtype) for s in srcs],
                   *[pltpu.HBM(l.shape, l.dtype) for l in lands], SDS((8, LANES), f32)),
        in_specs=[_HBM] * (2 * n), out_specs=(_SEM, _SEM, _SEM, *[_HBM] * (2 * n), pl.BlockSpec(memory_space=pltpu.VMEM)),
        input_output_aliases={i: 3 + i for i in range(2 * n)},
        compiler_params=pltpu.CompilerParams(has_side_effects=_EFFECT),
    )(*[_hbm(s) for s in srcs], *[_hbm(lax.empty(l.shape, l.dtype)) for l in lands])
    return dict(sems=outs[:3], srcs=list(outs[3:3 + n]), lands=list(outs[3 + n:3 + 2 * n]), token=outs[3 + 2 * n])


def _exchange_wait(started, waits_of, after, name, which=None):
    which = list(range(len(started["srcs"]))) if which is None else which
    srcs, lands = [started["srcs"][a] for a in which], [started["lands"][a] for a in which]
    n = len(which)

    def body(*refs):
        src = refs[:n]
        land = refs[n:2 * n]
        send_sems, recv_sems, self_sems = refs[2 * n:2 * n + 3]
        me = _index(_me())
        for pos, a in enumerate(which):
            seven, (s_ref, d_ref) = waits_of(a, src[pos], land[pos], me)
            both = pltpu.make_async_remote_copy(src_ref=seven, dst_ref=seven, send_sem=send_sems.at[a], recv_sem=recv_sems.at[a],
                                                device_id=_me(), device_id_type=MESH)
            both.wait_send()
            both.wait_recv()
            pltpu.make_async_copy(s_ref, d_ref, self_sems.at[a]).wait()

    outs = pl.pallas_call(
        body, name=name, out_shape=tuple(pltpu.HBM(t.shape, t.dtype) for t in srcs + lands),
        in_specs=[_HBM] * (2 * n) + [_SEM] * 3 + [_ANY], out_specs=tuple([_HBM] * (2 * n)),
        input_output_aliases={i: i for i in range(2 * n)},
        compiler_params=pltpu.CompilerParams(has_side_effects=_EFFECT),
    )(*srcs, *lands, *started["sems"], after)
    return list(outs[n:])


def gather_start(locs, axes, name):
    lands = [SDS(tuple(N_DEV * d if i == ax else d for i, d in enumerate(l.shape)), l.dtype) for l, ax in zip(locs, axes)]

    def copies_of(a, src, land, me):
        mine = _win(land, axes[a], me, src.shape[axes[a]])
        return [(src, mine, peer) for peer in _peers()] + [(src, mine, None)]

    return _exchange_start(locs, lands, copies_of, name)


def gather_wait(started, axes, after, name, which=None):
    def waits_of(a, src, land, me):
        size = src.shape[axes[a]]
        return _win(land, axes[a], 0, size, N_DEV - 1), (src, _win(land, axes[a], me, size))

    return _exchange_wait(started, waits_of, after, name, which)


def scatter_start(grads, axes, name):
    lands = [SDS((N_DEV,) + tuple(d // N_DEV if i == ax else d for i, d in enumerate(g.shape)), g.dtype) for g, ax in zip(grads, axes)]

    def copies_of(a, src, land, me):
        size = src.shape[axes[a]] // N_DEV
        out = [(_win(src, axes[a], _index(peer), size), land.at[me], peer) for peer in _peers()]
        return out + [(_win(src, axes[a], me, size), land.at[me], None)]

    return _exchange_start(grads, lands, copies_of, name)


def scatter_wait(started, axes, after, name):
    def waits_of(a, src, land, me):
        size = src.shape[axes[a]] // N_DEV
        return land.at[pl.ds(0, N_DEV - 1)], (_win(src, axes[a], me, size), land.at[me])

    return _exchange_wait(started, waits_of, after, name)


def _row_tile(rows, cap=512):
    return max(t for t in range(8, min(rows, cap) + 1, 8) if rows % t == 0)


_SMALL = [
    ("ln_mix_pre", (2, 1024)), ("ln_mix_post", (2, 1024)), ("ln_ffn_pre", (2, 1024)), ("ln_ffn_post", (2, 1024)),
    ("ln_mem", (2, 1024)), ("w_spatial", (1, 6, 128, 128)), ("b_spatial", (1, 6, 128)), ("ln_shared", (1024,)),
    ("b_forget", (12,)), ("ln_v_g", (1, 768)), ("ln_v_b", (1, 768)),
]
_SMALL_TILE = 8 * LANES


def _small_rows(shape):
    return -(-math.prod(shape) // _SMALL_TILE) * 8


def _pack_small(vals, shapes):
    parts = []
    for name, shape in shapes:
        flat = vals[name].reshape(-1).astype(f32)
        rows = _small_rows(shape)
        parts.append(jnp.pad(flat, (0, rows * LANES - flat.shape[0])).reshape(rows, LANES))
    return jnp.concatenate(parts, axis=0)


def _unpack_small(buf, shapes):
    out = {}
    lo = 0
    for name, shape in shapes:
        rows = _small_rows(shape)
        out[name] = buf[lo:lo + rows].reshape(-1)[:math.prod(shape)].reshape(shape)
        lo += rows
    return out


def kernel(x, mem, ln_mix_pre, ln_mix_post, ln_ffn_pre, ln_ffn_post, ln_mem, w_mem_kv, w_out, w_ffn_gate, w_ffn_up, w_ffn_down, w_in_a, w_spatial, b_spatial, ln_v_g, ln_v_b, ln_shared, w_shared_kv, b_forget, w_in_b, loss_target, m_ln_mix_pre, m_ln_mix_post, m_ln_ffn_pre, m_ln_ffn_post, m_ln_mem, m_w_mem_kv, m_w_out, m_w_ffn_gate, m_w_ffn_up, m_w_ffn_down, m_w_in_a, m_w_spatial, m_b_spatial, m_ln_v_g, m_ln_v_b, m_ln_shared, m_w_shared_kv, m_b_forget, m_w_in_b, v_ln_mix_pre, v_ln_mix_post, v_ln_ffn_pre, v_ln_ffn_post, v_ln_mem, v_w_mem_kv, v_w_out, v_w_ffn_gate, v_w_ffn_up, v_w_ffn_down, v_w_in_a, v_w_spatial, v_b_spatial, v_ln_v_g, v_ln_v_b, v_ln_shared, v_w_shared_kv, v_b_forget, v_w_in_b):
    weights = dict(ln_mix_pre=ln_mix_pre, ln_mix_post=ln_mix_post, ln_ffn_pre=ln_ffn_pre, ln_ffn_post=ln_ffn_post, ln_mem=ln_mem,
                   w_mem_kv=w_mem_kv, w_out=w_out, w_ffn_gate=w_ffn_gate, w_ffn_up=w_ffn_up, w_ffn_down=w_ffn_down, w_in_a=w_in_a,
                   w_spatial=w_spatial, b_spatial=b_spatial, ln_v_g=ln_v_g, ln_v_b=ln_v_b, ln_shared=ln_shared,
                   w_shared_kv=w_shared_kv, b_forget=b_forget, w_in_b=w_in_b)
    mom_m = dict(ln_mix_pre=m_ln_mix_pre, ln_mix_post=m_ln_mix_post, ln_ffn_pre=m_ln_ffn_pre, ln_ffn_post=m_ln_ffn_post, ln_mem=m_ln_mem,
                 w_mem_kv=m_w_mem_kv, w_out=m_w_out, w_ffn_gate=m_w_ffn_gate, w_ffn_up=m_w_ffn_up, w_ffn_down=m_w_ffn_down, w_in_a=m_w_in_a,
                 w_spatial=m_w_spatial, b_spatial=m_b_spatial, ln_v_g=m_ln_v_g, ln_v_b=m_ln_v_b, ln_shared=m_ln_shared,
                 w_shared_kv=m_w_shared_kv, b_forget=m_b_forget, w_in_b=m_w_in_b)
    mom_v = dict(ln_mix_pre=v_ln_mix_pre, ln_mix_post=v_ln_mix_post, ln_ffn_pre=v_ln_ffn_pre, ln_ffn_post=v_ln_ffn_post, ln_mem=v_ln_mem,
                 w_mem_kv=v_w_mem_kv, w_out=v_w_out, w_ffn_gate=v_w_ffn_gate, w_ffn_up=v_w_ffn_up, w_ffn_down=v_w_ffn_down, w_in_a=v_w_in_a,
                 w_spatial=v_w_spatial, b_spatial=v_b_spatial, ln_v_g=v_ln_v_g, ln_v_b=v_ln_v_b, ln_shared=v_ln_shared,
                 w_shared_kv=v_w_shared_kv, b_forget=v_b_forget, w_in_b=v_w_in_b)
    names = list(weights)
    mx, my, mc = lax.axis_index("x"), lax.axis_index("y"), lax.axis_index("c")
    me = 4 * mx + 2 * my + mc

    h0 = x[0]
    mem0 = mem[0]
    tgt = loss_target[0]
    seq = h0.shape[0]

    vec = lambda a: a.reshape(1, -1)
    pad_to = lambda a, axis, size: jnp.pad(a, [(0, size - a.shape[i] if i == axis else 0) for i in range(a.ndim)])

    def after(tok, a):
        return a + tok[0, 0].astype(a.dtype)

    lnv_loc = pad_to(jnp.concatenate([ln_v_g, ln_v_b], axis=0), 0, 8)
    st_a = gather_start([w_in_a.astype(bf16), pad_to(lnv_loc, 1, LANES)[None]], [0, 0], "gather_a_start")
    mix_locs = lambda l, tok: [after(tok, w_mem_kv[l]).astype(bf16), w_out[l].astype(bf16)]

    def ffn_gather_start(l, tok):
        gate_up = gather_start([pad_to(after(tok, w_ffn_gate[l]).astype(bf16), 1, FF_SHARD_PAD),
                                pad_to(w_ffn_up[l].astype(bf16), 1, FF_SHARD_PAD)], [1, 1], f"gather_gate_up{l}_start")
        down = gather_start([pad_to(after(gate_up["token"], w_ffn_down[l]).astype(bf16), 0, FF_SHARD_PAD)], [0], f"gather_down{l}_start")
        return gate_up, down

    st_b = [gather_start(mix_locs(0, st_a["token"]), [0, 0], "gather_b0_start"), None]
    st_c = ffn_gather_start(0, st_b[0]["token"])
    st_d = gather_start([after(st_c[1]["token"], w_in_b[0]).astype(bf16), pad_to(w_shared_kv.astype(bf16), 1, KV_PAD)], [0, 0],
                        "gather_d_start")
    st_b[1] = gather_start(mix_locs(1, st_d["token"]), [0, 0], "gather_b1_start")
    st_e = ffn_gather_start(1, st_b[1]["token"])
    ws = w_spatial[0].astype(bf16)
    ws_t = ws.transpose(0, 2, 1)
    bs_t = b_spatial[0].T

    (a0,) = rms_fwd(h0, [after(st_e[1]["token"], vec(ln_mix_pre[0]))], "a0_norm")
    w_in_a8, lnv8 = gather_wait(st_a, [0, 0], a0, "gather_a_wait")
    w_in_a_full = w_in_a8.transpose(1, 0, 2).reshape(D_MODEL, -1)
    lnv_g = lnv8[:, 0, :MAIN_WIDTH // N_DEV].reshape(1, MAIN_WIDTH)
    lnv_b = lnv8[:, 1, :MAIN_WIDTH // N_DEV].reshape(1, MAIN_WIDTH)
    proj0 = mm(a0, w_in_a_full, "proj0", tn=896)
    main0 = gmlp_fwd(proj0, ws, bs_t, lnv_g, lnv_b, "gmlp_fwd", out_width=D_MODEL)
    w_mkv, w_o = [None, None], [None, None]
    w_mkv[0], w_o[0] = gather_wait(st_b[0], [0, 0], main0, "gather_b0_wait")
    (memn0,) = rms_fwd(mem0, [vec(ln_mem[0])], "mem0_norm")
    kvm0 = mm(memn0, w_mkv[0], "kvm0")
    mixed0 = mem_attn_fwd(proj0, 2 * MAIN_WIDTH // MEM_WIDTH, kvm0, main0, "mem_attn0")
    y1_0, hmid0, f0 = mm_resnorm(mixed0, w_o[0], h0, vec(ln_mix_post[0]), [vec(ln_ffn_pre[0])], "mix_out0")
    w_g0, w_u0 = gather_wait(st_c[0], [1, 1], f0, "gather_gate_up0_wait")
    gu0, act0 = ffn_up(f0, w_g0, w_u0, "ffn_up0")
    (w_d0,) = gather_wait(st_c[1], [0], act0, "gather_down0_wait")
    y2_0, h1, a1, sin1 = mm_resnorm(act0, w_d0, hmid0, vec(ln_ffn_post[0]), [vec(ln_mix_pre[1]), vec(ln_shared)], "ffn_down0")

    w_inb, w_kv = gather_wait(st_d, [0, 0], sin1, "gather_d_wait")
    kvb = mm(sin1, w_kv, "kv_shared", out_dtype=bf16, tn=MAIN_WIDTH, ncols=2 * MAIN_WIDTH)
    zf = mm(sin1, w_kv, "forget_logits", tn=256, col0=2 * MAIN_WIDTH, ncols=256)
    qb = mm(a1, w_inb, "proj1", out_dtype=bf16)
    z_t = jnp.pad(zf[:, :FOX_HEADS].T, ((0, 16 - FOX_HEADS), (0, 0)))
    bf_col = jnp.pad(b_forget, (0, 16 - FOX_HEADS)).reshape(16, 1)
    c_t = fgate_fwd(z_t, bf_col, "fgate_fwd")
    c_row = c_t[:FOX_HEADS].reshape(FOX_PAIRS, 2, seq)
    main1, lse, main1_b = fox_fwd(qb, kvb, c_row, "fox_fwd", out_width=D_MODEL)
    w_mkv[1], w_o[1] = gather_wait(st_b[1], [0, 0], main1, "gather_b1_wait")
    (memn1,) = rms_fwd(mem0, [vec(ln_mem[1])], "mem1_norm")
    kvm1 = mm(memn1, w_mkv[1], "kvm1")
    mixed1 = mem_attn_fwd(qb, MAIN_WIDTH // MEM_WIDTH, kvm1, main1_b, "mem_attn1")
    y1_1, hmid1, f1 = mm_resnorm(mixed1, w_o[1], h1, vec(ln_mix_post[1]), [vec(ln_ffn_pre[1])], "mix_out1")
    w_g1, w_u1 = gather_wait(st_e[0], [1, 1], f1, "gather_gate_up1_wait")
    gu1, act1 = ffn_up(f1, w_g1, w_u1, "ffn_up1")
    (w_d1,) = gather_wait(st_e[1], [0], act1, "gather_down1_wait")
    dh, d_y2_1, dg_fpost1, loss_tile = mm_resnorm_loss(act1, w_d1, hmid1, vec(ln_ffn_post[1]), tgt, "ffn_down1_loss")
    ffn_w = [(w_g0, w_u0, w_d0), (w_g1, w_u1, w_d1)]

    small = {}

    def ffn_backward(layer, dh_out, d_y2, hmid, f, gu, act, y1):
        w_g, w_u, w_d = ffn_w[layer]
        dw_down = mm_tn(act, d_y2, f"dw_down{layer}")
        rs_down = scatter_start([dw_down], [0], f"scatter_down{layer}_start")
        d_g, d_u = ffn_act_grad(d_y2, w_d, gu, f"ffn_act_grad{layer}")
        dw_g = mm_tn(f, d_g, f"dw_gate{layer}", dep=rs_down["token"])
        dw_u = mm_tn(f, d_u, f"dw_up{layer}")
        rs_gate_up = scatter_start([dw_g, dw_u], [1, 1], f"scatter_gate_up{layer}_start")
        dh_mid, d_y1, dg_fpre, dg_mpost = ffn_in_grad(d_g, d_u, w_g, w_u, hmid, dh_out, after(rs_gate_up["token"], vec(ln_ffn_pre[layer])),
                                                      y1, vec(ln_mix_post[layer]), f"ffn_in_grad{layer}")
        return dh_mid, d_y1, dg_fpre, dg_mpost, (rs_down, rs_gate_up)

    def mix_out_backward(layer, d_y1, mixed):
        dw_out = mm_tn(mixed, d_y1, f"dw_out{layer}")
        d_mixed = mm(d_y1, w_o[layer], f"d_mixed{layer}", trans_b=True)
        return d_mixed, dw_out

    def mem_backward(layer, q_src, q_block, kvm, memn, d_mixed, into):
        d_qm, d_kvm = mem_attn_bwd(q_src, q_block, kvm, d_mixed, into, f"mem_attn_bwd{layer}")
        d_kvm_b = d_kvm.astype(bf16)
        dw_mkv = mm_tn(memn, d_kvm_b, f"dw_mem_kv{layer}")
        d_memn = mm(d_kvm_b, w_mkv[layer], f"d_memn{layer}", trans_b=True)
        _, dg_mem = rms_bwd(mem0, vec(ln_mem[layer]), d_memn, None, bf16, f"mem_norm_bwd{layer}")
        return d_qm, dw_mkv, dg_mem


    dh_mid1, d_y1_1, dg_fpre1, dg_mpost1, rs_ffn1 = ffn_backward(1, dh, d_y2_1, hmid1, f1, gu1, act1, y1_1)
    d_mixed1, dw_out1 = mix_out_backward(1, d_y1_1, mixed1)
    dq_b, dk, dv, dc = fox_bwd(qb, kvb, d_mixed1, main1, lse, c_row, "fox_bwd", dq_width=D_MODEL)
    d_proj1, dw_mkv1, dg_mem1 = mem_backward(1, qb, MAIN_WIDTH // MEM_WIDTH, kvm1, memn1, d_mixed1, dq_b)
    rs_mix1 = scatter_start([dw_out1, dw_mkv1], [0, 0], "scatter_mix1_start")
    dc_t = jnp.pad(dc.reshape(FOX_HEADS, seq), ((0, 16 - FOX_HEADS), (0, 0)))
    dz_t, db_f = fgate_bwd(dc_t, z_t, bf_col, "fgate_bwd")
    d_kvf = jnp.concatenate([dk, dv, jnp.pad(dz_t[:FOX_HEADS].T.astype(bf16), ((0, 0), (0, KV_PAD - KV_WIDTH)))], axis=-1)
    dw_in_b = mm_tn(a1, d_proj1, "dw_in_b", dep=rs_mix1["token"])
    dw_kv = mm_tn(sin1, d_kvf, "dw_kv", tn=896)
    rs_2 = scatter_start([dw_in_b, dw_kv], [0, 0], "scatter_shared_start")
    dh1, (dg_pre1, dg_shared), d_y2_0, dg_fpost0 = proj_in_grad(
        [(d_proj1, w_inb, vec(ln_mix_pre[1])), (d_kvf, w_kv, vec(ln_shared))], h1, dh_mid1, "in_grad1", dep=rs_2["token"],
        below=(y2_0, vec(ln_ffn_post[0])))

    dh_mid0, d_y1_0, dg_fpre0, dg_mpost0, rs_ffn0 = ffn_backward(0, dh1, d_y2_0, hmid0, f0, gu0, act0, y1_0)
    d_mixed0, dw_out0 = mix_out_backward(0, d_y1_0, mixed0)
    d_uv, dw_s, db_s, dg_lnv, db_lnv = gmlp_bwd(proj0, d_mixed0, ws, ws_t, bs_t, lnv_g, lnv_b, "gmlp_bwd", out_width=w_in_a_full.shape[1])
    d_proj0, dw_mkv0, dg_mem0 = mem_backward(0, proj0, 2 * MAIN_WIDTH // MEM_WIDTH, kvm0, memn0, d_mixed0, d_uv)
    rs_mix0 = scatter_start([dw_out0, dw_mkv0], [0, 0], "scatter_mix0_start")

    small["ln_mix_pre"] = jnp.concatenate([jnp.zeros_like(dg_pre1), dg_pre1], axis=0)
    small["ln_mix_post"] = jnp.concatenate([dg_mpost0, dg_mpost1], axis=0)
    small["ln_ffn_pre"] = jnp.concatenate([dg_fpre0, dg_fpre1], axis=0)
    small["ln_ffn_post"] = jnp.concatenate([dg_fpost0, dg_fpost1], axis=0)
    small["ln_mem"] = jnp.concatenate([dg_mem0, dg_mem1], axis=0)
    small["w_spatial"] = dw_s[None]
    small["b_spatial"] = db_s[:, :A_GROUPS].T[None]
    small["ln_shared"] = dg_shared[0]
    small["b_forget"] = db_f[:FOX_HEADS, 0]
    small["ln_v_g"] = dg_lnv
    small["ln_v_b"] = db_lnv
    small_rows = jnp.concatenate([_pack_small(small, _SMALL), after(rs_mix0["token"], loss_tile)], axis=0)
    st_small = gather_start([small_rows[None]], [0], "gather_small_grads_start")
    dw_in_a_t = mm_tn(d_proj0, a0, "dw_in_a", tk=896, dep=st_small["token"])
    rs_in_a = scatter_start([dw_in_a_t], [0], "scatter_in_a_start")
    grad_x, (dg_pre0,) = proj_in_grad([(d_proj0, w_in_a_full, vec(ln_mix_pre[0]))], h0, dh_mid0, "in_grad0", dep=rs_in_a["token"])
    st_last = gather_start([dg_pre0.reshape(1, 8, LANES)], [0], "gather_last_grad_start")

    def owned(started, axes, wait_after, name):
        recv = scatter_wait(started, axes, wait_after, name)
        return [sum_leading(r.reshape((N_DEV, -1, r.shape[-1])), f"{name}_sum{i}", tr=_row_tile(math.prod(r.shape[1:-1])))
                for i, r in enumerate(recv)]

    (g_down1,) = owned(rs_ffn1[0], [0], after(st_last["token"], grad_x[:8, :LANES]), "scatter_down1_wait")
    g_gu1 = owned(rs_ffn1[1], [1, 1], g_down1, "scatter_gate_up1_wait")
    g_mix1 = owned(rs_mix1, [0, 0], g_gu1[0], "scatter_mix1_wait")
    g2 = owned(rs_2, [0, 0], g_mix1[0], "scatter_shared_wait")
    (g_down0,) = owned(rs_ffn0[0], [0], g2[0], "scatter_down0_wait")
    g_gu0 = owned(rs_ffn0[1], [1, 1], g_down0, "scatter_gate_up0_wait")
    g_mix0 = owned(rs_mix0, [0, 0], g_gu0[0], "scatter_mix0_wait")
    (g_in_a,) = owned(rs_in_a, [0], g_mix0[0], "scatter_in_a_wait")
    g_local = dict(
        w_ffn_gate=jnp.stack([g_gu0[0], g_gu1[0]])[:, :, :FF_SHARD], w_ffn_up=jnp.stack([g_gu0[1], g_gu1[1]])[:, :, :FF_SHARD],
        w_ffn_down=jnp.stack([g_down0, g_down1])[:, :FF_SHARD], w_out=jnp.stack([g_mix0[0], g_mix1[0]]),
        w_mem_kv=jnp.stack([g_mix0[1], g_mix1[1]]), w_in_b=g2[0][None], w_shared_kv=g2[1][:, :KV_WIDTH], w_in_a=g_in_a.T[None])
    (small_all,) = gather_wait(st_small, [0], g_in_a, "gather_small_grads_wait")
    (last_all,) = gather_wait(st_last, [0], small_all, "gather_last_grad_wait")
    small_sum = sum_leading(small_all, "sum_small_grads")
    loss = small_sum[small_rows.shape[0] - 1, 0]
    g_small = _unpack_small(small_sum, _SMALL)
    g_small["ln_mix_pre"] = jnp.concatenate([sum_leading(last_all, "sum_last_grad").reshape(1, D_MODEL), g_small["ln_mix_pre"][1:]], axis=0)
    shard = MAIN_WIDTH // N_DEV
    for n in ("ln_v_g", "ln_v_b"):
        g_small[n] = lax.dynamic_slice_in_dim(g_small[n], me * shard, shard, axis=1)
    grad_w = {**g_small, **g_local}

    delta, new_m, new_v = {}, {}, {}
    for n in g_local:
        two_d = (-1, weights[n].shape[-1])
        d_, m_, v_ = adamw(weights[n].reshape(two_d), grad_w[n].reshape(two_d), mom_m[n].reshape(two_d), mom_v[n].reshape(two_d),
                           f"adamw_{n}", tr=_row_tile(math.prod(weights[n].shape[:-1])))
        delta[n], new_m[n], new_v[n] = (t.reshape(weights[n].shape) for t in (d_, m_, v_))
    small_local_shapes = [(n, tuple(weights[n].shape)) for n, _ in _SMALL]
    packed = [_pack_small(src, small_local_shapes) for src in (weights, grad_w, mom_m, mom_v)]
    outs = adamw(*packed, "adamw_small", tr=packed[0].shape[0])
    for dst, buf in zip((delta, new_m, new_v), outs):
        dst.update(_unpack_small(buf, small_local_shapes))

    return (loss, grad_x[None], *[grad_w[n] for n in names], *[delta[n] for n in names],
            *[new_m[n] for n in names], *[new_v[n] for n in names])
```

```python
import functools
import math

import jax
import jax.numpy as jnp
from jax import lax
from jax.experimental import pallas as pl
from jax.experimental.pallas import tpu as pltpu

f32 = jnp.float32
bf16 = jnp.bfloat16
SDS = jax.ShapeDtypeStruct

D_MODEL = 1024
MAIN_WIDTH = 768
MEM_WIDTH = 256
HEAD_DIM = 64
MEM_HEADS = 4
FOX_HEADS = 12
FOX_PAIRS = FOX_HEADS // 2
CHUNK = 128
A_GROUPS = 6
FF_SHARD = 352
FF_SHARD_PAD = 384
FF_PAD = 8 * FF_SHARD_PAD
KV_WIDTH = 2 * MAIN_WIDTH + FOX_HEADS
KV_PAD = 1792
RMS_EPS = 1e-6
LN_EPS = 1e-5
ATT_SCALE = HEAD_DIM ** -0.5
ADAM_LR, ADAM_B1, ADAM_B2, ADAM_EPS, ADAM_WD, ADAM_STEP = 0.001, 0.9, 0.999, 1e-08, 0.01, 10
N_DEV = 8
AXES = ("x", "y", "c")
MESH = pl.DeviceIdType.MESH
V7X_VMEM_LIMIT = 56 * 1024 * 1024
LANES = 128
FLAT_W = 512
ROW_PAD = 16


def _cparams(*sem):
    return pltpu.CompilerParams(dimension_semantics=sem or None, vmem_limit_bytes=V7X_VMEM_LIMIT)


def _dot(a, b):
    return jnp.dot(a, b, preferred_element_type=f32)


def _dot_nt(a, b):
    return lax.dot_general(a, b, (((1,), (1,)), ((), ())), preferred_element_type=f32)


def _dot_tn(a, b):
    return lax.dot_general(a, b, (((0,), (0,)), ((), ())), preferred_element_type=f32)


def _gelu(x):
    k = math.sqrt(2.0 / math.pi)
    t = jnp.tanh(k * (x + 0.044715 * x * x * x))
    return 0.5 * x * (1.0 + t), t


def _gelu_grad(x, t):
    k = math.sqrt(2.0 / math.pi)
    return 0.5 * (1.0 + t) + 0.5 * x * (1.0 - t * t) * k * (1.0 + 3.0 * 0.044715 * x * x)


def _sigmoid(x):
    return 1.0 / (1.0 + jnp.exp(-x))


def rms_fwd(x, gains, name, tm=512):
    m, d = x.shape
    tm = min(tm, m)
    n = len(gains)

    def body(x_ref, *refs):
        xv = x_ref[...]
        y = xv * lax.rsqrt(jnp.sum(xv * xv, axis=-1, keepdims=True) * (1.0 / d) + RMS_EPS)
        for g_ref, o_ref in zip(refs[:n], refs[n:]):
            o_ref[...] = (y * g_ref[...]).astype(bf16)

    row = pl.BlockSpec((tm, d), lambda i: (i, 0))
    vec = pl.BlockSpec((1, d), lambda i: (0, 0))
    return pl.pallas_call(body, grid=(m // tm,), in_specs=[row] + [vec] * n, out_specs=[row] * n,
                          out_shape=[SDS((m, d), bf16)] * n, name=name, compiler_params=_cparams("parallel"))(x, *gains)


def rms_bwd(x, g, dy, add, out_dtype, name, tm=512):
    m, d = x.shape
    tm = min(tm, m)
    has_add = add is not None

    def body(x_ref, g_ref, dy_ref, *refs):
        dx_ref, dg_ref = refs[-2], refs[-1]
        xv = x_ref[...]
        dyv = dy_ref[...].astype(f32)
        r = lax.rsqrt(jnp.sum(xv * xv, axis=-1, keepdims=True) * (1.0 / d) + RMS_EPS)
        xn = xv * r
        dyg = dyv * g_ref[...]
        dx = r * (dyg - xn * (jnp.sum(dyg * xn, axis=-1, keepdims=True) * (1.0 / d)))
        if has_add:
            dx = dx + refs[0][...]
        dx_ref[...] = dx.astype(out_dtype)

        @pl.when(pl.program_id(0) == 0)
        def _():
            dg_ref[...] = jnp.zeros_like(dg_ref)

        dg_ref[...] += jnp.sum(dyv * xn, axis=0, keepdims=True)

    row = pl.BlockSpec((tm, d), lambda i: (i, 0))
    vec = pl.BlockSpec((1, d), lambda i: (0, 0))
    ins = [x, g, dy] + ([add] if has_add else [])
    return pl.pallas_call(body, grid=(m // tm,), in_specs=[row, vec, row] + ([row] if has_add else []),
                          out_specs=[row, vec], out_shape=[SDS((m, d), out_dtype), SDS((1, d), f32)], name=name,
                          compiler_params=_cparams("arbitrary"))(*ins)


def mm(a, b, name, trans_b=False, out_dtype=f32, tm=1024, tn=1024, layer=None, col0=0, ncols=None, dep=None):
    m, k = a.shape
    n_all = b.shape[-2] if trans_b else b.shape[-1]
    n = n_all if ncols is None else ncols
    tm, tn = min(tm, m), min(tn, n)
    assert m % tm == 0 and n % tn == 0 and col0 % tn == 0 and not (trans_b and col0), (name, m, n, tm, tn)
    jb = col0 // tn
    lead = () if layer is None else (None,)
    sel = () if layer is None else (layer,)

    def body(a_ref, b_ref, *rest):
        r = _dot_nt(a_ref[...], b_ref[...]) if trans_b else _dot(a_ref[...], b_ref[...])
        rest[-1][...] = r.astype(out_dtype)

    if trans_b:
        b_spec = pl.BlockSpec(lead + (tn, k), lambda j, i: sel + (j, 0))
    else:
        b_spec = pl.BlockSpec(lead + (k, tn), lambda j, i: sel + (0, jb + j))
    deps = [] if dep is None else [dep]
    dep_specs = [pl.BlockSpec((8, LANES), lambda j, i: (0, 0))] * len(deps)
    return pl.pallas_call(body, grid=(n // tn, m // tm), in_specs=[pl.BlockSpec((tm, k), lambda j, i: (i, 0)), b_spec] + dep_specs,
                          out_specs=pl.BlockSpec((tm, tn), lambda j, i: (i, j)), out_shape=SDS((m, n), out_dtype),
                          name=name, compiler_params=_cparams("parallel", "parallel"))(a, b, *deps)


def mm_tn(a, g, name, tk=1024, tn=1024, out_dtype=bf16, dep=None):
    s, k = a.shape
    n = g.shape[1]
    tk, tn = min(tk, k), min(tn, n)
    assert k % tk == 0 and n % tn == 0, (name, k, n, tk, tn)

    def body(a_ref, g_ref, *rest):
        rest[-1][...] = _dot_tn(a_ref[...], g_ref[...]).astype(out_dtype)

    deps = [] if dep is None else [dep]
    dep_specs = [pl.BlockSpec((8, LANES), lambda i, j: (0, 0))] * len(deps)
    return pl.pallas_call(body, grid=(k // tk, n // tn),
                          in_specs=[pl.BlockSpec((s, tk), lambda i, j: (0, i)), pl.BlockSpec((s, tn), lambda i, j: (0, j))] + dep_specs,
                          out_specs=pl.BlockSpec((tk, tn), lambda i, j: (i, j)), out_shape=SDS((k, n), out_dtype), name=name,
                          compiler_params=_cparams("parallel", "parallel"))(a, g, *deps)


def _resident(shape, index_map):
    return pl.BlockSpec(shape, index_map, pipeline_mode=pl.Buffered(1))


def _rms(xv):
    return xv * lax.rsqrt(jnp.sum(xv * xv, axis=-1, keepdims=True) * (1.0 / xv.shape[-1]) + RMS_EPS)


def _rms_bwd_math(xv, g, dy):
    d = xv.shape[-1]
    r = lax.rsqrt(jnp.sum(xv * xv, axis=-1, keepdims=True) * (1.0 / d) + RMS_EPS)
    xn = xv * r
    dyg = dy * g
    dx = r * (dyg - xn * (jnp.sum(dyg * xn, axis=-1, keepdims=True) * (1.0 / d)))
    return dx, jnp.sum(dy * xn, axis=0, keepdims=True)


SUB_ROWS = 512


def mm_resnorm(a, b, h, g_post, gains, name, tm=512):
    m, k = a.shape
    d = b.shape[1]
    n = len(gains)

    def body(a_ref, b_ref, h_ref, gp_ref, *refs):
        for r in range(tm // SUB_ROWS):
            rows = slice(r * SUB_ROWS, (r + 1) * SUB_ROWS)
            y = _dot(a_ref[rows, :], b_ref[...])
            refs[n][rows, :] = y
            hn = h_ref[rows, :] + _rms(y) * gp_ref[...]
            refs[n + 1][rows, :] = hn
            if n:
                z = _rms(hn)
                for g_ref, o_ref in zip(refs[:n], refs[n + 2:]):
                    o_ref[rows, :] = (z * g_ref[...]).astype(bf16)

    row = pl.BlockSpec((tm, d), lambda i: (i, 0))
    vec = pl.BlockSpec((1, d), lambda i: (0, 0))
    return pl.pallas_call(body, grid=(m // tm,),
                          in_specs=[pl.BlockSpec((tm, k), lambda i: (i, 0)), _resident((k, d), lambda i: (0, 0)), row, vec] + [vec] * n,
                          out_specs=[row] * (n + 2), out_shape=[SDS((m, d), f32)] * 2 + [SDS((m, d), bf16)] * n, name=name,
                          compiler_params=_cparams("parallel"))(a, b, h, g_post, *gains)


def mm_resnorm_loss(a, b, h, g_post, tgt, name, tm=512):
    m, k = a.shape
    d = b.shape[1]

    def body(a_ref, b_ref, h_ref, gp_ref, t_ref, dh_ref, dy_ref, dg_ref, l_ref):
        @pl.when(pl.program_id(0) == 0)
        def _():
            dg_ref[...] = jnp.zeros_like(dg_ref)
            l_ref[...] = jnp.zeros_like(l_ref)

        y = _dot(a_ref[...], b_ref[...])
        e = h_ref[...] + _rms(y) * gp_ref[...] - t_ref[...]
        dh = e * (1.0 / d)
        dh_ref[...] = dh
        part = jnp.sum(jnp.sum(e * e, axis=-1, keepdims=True), axis=0, keepdims=True) * (0.5 / d)
        l_ref[...] += jnp.broadcast_to(part, l_ref.shape)
        dy, dg = _rms_bwd_math(y, gp_ref[...], dh)
        dy_ref[...] = dy.astype(bf16)
        dg_ref[...] += dg

    row = pl.BlockSpec((tm, d), lambda i: (i, 0))
    vec = pl.BlockSpec((1, d), lambda i: (0, 0))
    return pl.pallas_call(body, grid=(m // tm,),
                          in_specs=[pl.BlockSpec((tm, k), lambda i: (i, 0)), _resident((k, d), lambda i: (0, 0)), row, vec, row],
                          out_specs=[row, row, vec, pl.BlockSpec((8, LANES), lambda i: (0, 0))],
                          out_shape=[SDS((m, d), f32), SDS((m, d), bf16), SDS((1, d), f32), SDS((8, LANES), f32)], name=name,
                          compiler_params=_cparams("arbitrary"))(a, b, h, g_post, tgt)


def ffn_act_grad(d_y2, w_d, factors, name, tm=1024, tn=1536):
    s, d = d_y2.shape
    ff = w_d.shape[0]
    nb = ff // tn

    def body(a_ref, b_ref, g_ref, u_ref, dg_ref, du_ref):
        av = a_ref[...]
        tc = 256
        for c in range(tn // tc):
            cols = slice(c * tc, (c + 1) * tc)
            da = _dot_nt(av, b_ref[cols, :])
            dg_ref[:, cols] = (da * g_ref[:, cols].astype(f32)).astype(bf16)
            du_ref[:, cols] = (da * u_ref[:, cols].astype(f32)).astype(bf16)

    tile = pl.BlockSpec((tm, tn), lambda j, i: (i, j))
    return pl.pallas_call(body, grid=(nb, s // tm),
                          in_specs=[pl.BlockSpec((tm, d), lambda j, i: (i, 0)), pl.BlockSpec((tn, d), lambda j, i: (j, 0)), tile,
                                    pl.BlockSpec((tm, tn), lambda j, i: (i, nb + j))],
                          out_specs=[tile, tile], out_shape=[SDS((s, ff), bf16)] * 2, name=name,
                          compiler_params=_cparams("parallel", "parallel"))(d_y2, w_d, factors, factors)


def ffn_in_grad(d_g, d_u, w_g, w_u, hmid, dh_out, g_pre, y1, g_post, name, tm=512):
    s, ff = d_g.shape
    d = w_g.shape[0]

    def body(dg_ref, du_ref, wg_ref, wu_ref, hm_ref, dho_ref, gpre_ref, y1_ref, gpost_ref, dhm_ref, dy1_ref, dgpre_ref, dgpost_ref):
        @pl.when(pl.program_id(0) == 0)
        def _():
            dgpre_ref[...] = jnp.zeros_like(dgpre_ref)
            dgpost_ref[...] = jnp.zeros_like(dgpost_ref)

        for r in range(tm // SUB_ROWS):
            rows = slice(r * SUB_ROWS, (r + 1) * SUB_ROWS)
            d_f = _dot_nt(dg_ref[rows, :], wg_ref[...]) + _dot_nt(du_ref[rows, :], wu_ref[...])
            dx, dg1 = _rms_bwd_math(hm_ref[rows, :], gpre_ref[...], d_f)
            dh_mid = dho_ref[rows, :] + dx
            dhm_ref[rows, :] = dh_mid
            dgpre_ref[...] += dg1
            dy1, dg2 = _rms_bwd_math(y1_ref[rows, :], gpost_ref[...], dh_mid)
            dy1_ref[rows, :] = dy1.astype(bf16)
            dgpost_ref[...] += dg2

    row = pl.BlockSpec((tm, d), lambda i: (i, 0))
    vec = pl.BlockSpec((1, d), lambda i: (0, 0))
    wide = pl.BlockSpec((tm, ff), lambda i: (i, 0))
    w_spec = _resident((d, ff), lambda i: (0, 0))
    return pl.pallas_call(body, grid=(s // tm,), in_specs=[wide, wide, w_spec, w_spec, row, row, vec, row, vec],
                          out_specs=[row, row, vec, vec], out_shape=[SDS((s, d), f32), SDS((s, d), bf16), SDS((1, d), f32), SDS((1, d), f32)],
                          name=name, compiler_params=_cparams("arbitrary"))(d_g, d_u, w_g, w_u, hmid, dh_out, g_pre, y1, g_post)


def proj_in_grad(pairs, x, add, name, tm=512, dep=None, below=None):
    s, d = x.shape
    n = len(pairs)
    extra = [] if dep is None else [dep]
    n_below = 0 if below is None else 2

    def body(*refs):
        x_ref, add_ref = refs[3 * n], refs[3 * n + 1]
        below_refs = refs[3 * n + 2:3 * n + 2 + n_below]
        outs = refs[3 * n + 2 + n_below + len(extra):]

        @pl.when(pl.program_id(0) == 0)
        def _():
            for o in outs[1:1 + n] + outs[2 + n:]:
                o[...] = jnp.zeros_like(o)

        xv = x_ref[...]
        dx = add_ref[...]
        for i in range(n):
            a_ref, b_ref, g_ref = refs[3 * i:3 * i + 3]
            dxi, dgi = _rms_bwd_math(xv, g_ref[...], _dot_nt(a_ref[...], b_ref[...]))
            dx = dx + dxi
            outs[1 + i][...] += dgi
        outs[0][...] = dx
        if below is not None:
            dy, dg = _rms_bwd_math(below_refs[0][...], below_refs[1][...], dx)
            outs[1 + n][...] = dy.astype(bf16)
            outs[2 + n][...] += dg

    row = pl.BlockSpec((tm, d), lambda i: (i, 0))
    vec = pl.BlockSpec((1, d), lambda i: (0, 0))
    in_specs, args = [], []
    for a, b, g in pairs:
        k = a.shape[1]
        in_specs += [pl.BlockSpec((tm, k), lambda i: (i, 0)), _resident((d, k), lambda i: (0, 0)), vec]
        args += [a, b, g]
    in_specs += [row, row] + [row, vec][:n_below] + [pl.BlockSpec((8, LANES), lambda i: (0, 0))] * len(extra)
    out_specs = [row] + [vec] * n + [row, vec][:n_below]
    out_shape = [SDS((s, d), f32)] + [SDS((1, d), f32)] * n + [SDS((s, d), bf16), SDS((1, d), f32)][:n_below]
    out = pl.pallas_call(body, grid=(s // tm,), in_specs=in_specs, out_specs=out_specs, out_shape=out_shape, name=name,
                         compiler_params=_cparams("arbitrary"))(*args, x, add, *(below or ()), *extra)
    return (out[0], out[1:1 + n]) + tuple(out[1 + n:])


def ffn_up(f, wg, wu, name, tm=512, tc=256):
    s, d = f.shape
    ff = wg.shape[-1]

    def body(f_ref, wg_ref, wu_ref, fac_ref, act_ref):
        fv = f_ref[...]
        for j in range(ff // tc):
            lo = j * tc
            gg = _dot(fv, wg_ref[:, lo:lo + tc])
            uu = _dot(fv, wu_ref[:, lo:lo + tc])
            sg = _sigmoid(gg)
            silu = gg * sg
            fac_ref[:, lo:lo + tc] = (uu * (sg + silu * (1.0 - sg))).astype(bf16)
            fac_ref[:, ff + lo:ff + lo + tc] = silu.astype(bf16)
            act_ref[:, lo:lo + tc] = (silu * uu).astype(bf16)

    w_spec = _resident((d, ff), lambda i: (0, 0))
    return pl.pallas_call(body, grid=(s // tm,), in_specs=[pl.BlockSpec((tm, d), lambda i: (i, 0)), w_spec, w_spec],
                          out_specs=[pl.BlockSpec((tm, 2 * ff), lambda i: (i, 0)), pl.BlockSpec((tm, ff), lambda i: (i, 0))],
                          out_shape=[SDS((s, 2 * ff), bf16), SDS((s, ff), bf16)], name=name,
                          compiler_params=_cparams("parallel"))(f, wg, wu)


def _gmlp_forward_chunk(u, v, w_refs, bias, ln_g, ln_b):
    gu, tu = _gelu(u)
    gv, tv = _gelu(v)
    mu = jnp.sum(gv, axis=-1, keepdims=True) * (1.0 / MAIN_WIDTH)
    xc = gv - mu
    rstd = lax.rsqrt(jnp.sum(xc * xc, axis=-1, keepdims=True) * (1.0 / MAIN_WIDTH) + LN_EPS)
    xhat = xc * rstd
    vln = xhat * ln_g + ln_b
    row = lax.broadcasted_iota(jnp.int32, (CHUNK, CHUNK), 0)
    col = lax.broadcasted_iota(jnp.int32, (CHUNK, CHUNK), 1)
    s_parts = []
    for g in range(A_GROUPS):
        w = jnp.where(col <= row, w_refs[g], jnp.zeros((), bf16))
        s_parts.append(_dot(w, vln[:, g * CHUNK:(g + 1) * CHUNK].astype(bf16)) + bias[:, g:g + 1])
    return gu, tu, tv, rstd, xhat, vln, s_parts


def gmlp_fwd(proj, ws, bs_t, ln_g, ln_b, name, tm=512, out_width=MAIN_WIDTH):
    s = proj.shape[0]

    def body(u_ref, v_ref, w_ref, b_ref, g_ref, bb_ref, o_ref):
        bias = b_ref[...]
        for c in range(tm // CHUNK):
            rows = slice(c * CHUNK, (c + 1) * CHUNK)
            gu, _, _, _, _, _, s_parts = _gmlp_forward_chunk(u_ref[rows, :], v_ref[rows, :], w_ref, bias, g_ref[...], bb_ref[...])
            for g in range(A_GROUPS):
                cols = slice(g * CHUNK, (g + 1) * CHUNK)
                o_ref[rows, cols] = (gu[:, cols] * s_parts[g]).astype(bf16)

    vec = pl.BlockSpec((1, MAIN_WIDTH), lambda i: (0, 0))
    return pl.pallas_call(
        body, grid=(s // tm,),
        in_specs=[pl.BlockSpec((tm, MAIN_WIDTH), lambda i: (i, 0)), pl.BlockSpec((tm, MAIN_WIDTH), lambda i: (i, 1)),
                  pl.BlockSpec((A_GROUPS, CHUNK, CHUNK), lambda i: (0, 0, 0)), pl.BlockSpec((CHUNK, A_GROUPS), lambda i: (0, 0)), vec, vec],
        out_specs=pl.BlockSpec((tm, MAIN_WIDTH), lambda i: (i, 0)), out_shape=SDS((s, out_width), bf16), name=name,
        compiler_params=_cparams("parallel"))(proj, proj, ws, bs_t, ln_g, ln_b)


def gmlp_bwd(proj, d_mixed, ws, ws_t, bs_t, ln_g, ln_b, name, tm=512, out_width=2 * MAIN_WIDTH):
    s = proj.shape[0]

    def body(u_ref, v_ref, dm_ref, w_ref, wt_ref, b_ref, g_ref, bb_ref, duv_ref, dw_ref, db_ref, dg_ref, dbb_ref):
        @pl.when(pl.program_id(0) == 0)
        def _():
            dw_ref[...] = jnp.zeros_like(dw_ref)
            db_ref[...] = jnp.zeros_like(db_ref)
            dg_ref[...] = jnp.zeros_like(dg_ref)
            dbb_ref[...] = jnp.zeros_like(dbb_ref)

        bias = b_ref[...]
        ln_gv = g_ref[...]
        row = lax.broadcasted_iota(jnp.int32, (CHUNK, CHUNK), 0)
        col = lax.broadcasted_iota(jnp.int32, (CHUNK, CHUNK), 1)
        lane = lax.broadcasted_iota(jnp.int32, (CHUNK, LANES), 1)
        for c in range(tm // CHUNK):
            rows = slice(c * CHUNK, (c + 1) * CHUNK)
            u = u_ref[rows, :]
            v = v_ref[rows, :]
            gu, tu, tv, rstd, xhat, vln, s_parts = _gmlp_forward_chunk(u, v, w_ref, bias, ln_gv, bb_ref[...])
            dm = dm_ref[rows, :]
            d_vln_parts = []
            d_gu_parts = []
            db_acc = jnp.zeros((CHUNK, LANES), f32)
            for g in range(A_GROUPS):
                cols = slice(g * CHUNK, (g + 1) * CHUNK)
                dmg = dm[:, cols]
                d_gu_parts.append(dmg * s_parts[g])
                d_s = dmg * gu[:, cols]
                db_acc = db_acc + jnp.where(lane == g, jnp.sum(d_s, axis=-1, keepdims=True), 0.0)
                d_sb = d_s.astype(bf16)
                dw_ref[g] += jnp.where(col <= row, _dot_nt(d_sb, vln[:, cols].astype(bf16)), 0.0)
                wt = jnp.where(row <= col, wt_ref[g], jnp.zeros((), bf16))
                d_vln_parts.append(_dot(wt, d_sb))
            db_ref[...] += db_acc
            d_vln = jnp.concatenate(d_vln_parts, axis=-1)
            d_gu = jnp.concatenate(d_gu_parts, axis=-1)
            dg_ref[...] += jnp.sum(d_vln * xhat, axis=0, keepdims=True)
            dbb_ref[...] += jnp.sum(d_vln, axis=0, keepdims=True)
            dxh = d_vln * ln_gv
            m1 = jnp.sum(dxh, axis=-1, keepdims=True) * (1.0 / MAIN_WIDTH)
            m2 = jnp.sum(dxh * xhat, axis=-1, keepdims=True) * (1.0 / MAIN_WIDTH)
            d_gv = rstd * (dxh - m1 - xhat * m2)
            duv_ref[rows, :MAIN_WIDTH] = (d_gu * _gelu_grad(u, tu)).astype(bf16)
            duv_ref[rows, MAIN_WIDTH:] = (d_gv * _gelu_grad(v, tv)).astype(bf16)

    vec = pl.BlockSpec((1, MAIN_WIDTH), lambda i: (0, 0))
    wspec = pl.BlockSpec((A_GROUPS, CHUNK, CHUNK), lambda i: (0, 0, 0))
    return pl.pallas_call(
        body, grid=(s // tm,),
        in_specs=[pl.BlockSpec((tm, MAIN_WIDTH), lambda i: (i, 0)), pl.BlockSpec((tm, MAIN_WIDTH), lambda i: (i, 1)),
                  pl.BlockSpec((tm, MAIN_WIDTH), lambda i: (i, 0)), wspec, wspec, pl.BlockSpec((CHUNK, A_GROUPS), lambda i: (0, 0)), vec, vec],
        out_specs=[pl.BlockSpec((tm, 2 * MAIN_WIDTH), lambda i: (i, 0)), wspec, pl.BlockSpec((CHUNK, LANES), lambda i: (0, 0)), vec, vec],
        out_shape=[SDS((s, out_width), bf16), SDS((A_GROUPS, CHUNK, CHUNK), f32), SDS((CHUNK, LANES), f32),
                   SDS((1, MAIN_WIDTH), f32), SDS((1, MAIN_WIDTH), f32)],
        name=name, compiler_params=_cparams("arbitrary"))(proj, proj, d_mixed, ws, ws_t, bs_t, ln_g, ln_b)


def _head_mask(width, h):
    lane = lax.broadcasted_iota(jnp.int32, (1, width), 1)
    return (lane >= h * HEAD_DIM) & (lane < (h + 1) * HEAD_DIM)


def mem_attn_fwd(proj, q_block, kv, into, name, tm=512):
    s = proj.shape[0]
    n_mem = kv.shape[0]
    out_block = into.shape[1] // MEM_WIDTH - 1

    def body(q_ref, kv_ref, into_ref, o_ref):
        q = q_ref[...].astype(f32)
        k = kv_ref[:, :MEM_WIDTH].astype(bf16)
        v = kv_ref[:, MEM_WIDTH:].astype(bf16)
        out = jnp.zeros((tm, MEM_WIDTH), f32)
        for h in range(MEM_HEADS):
            msk = _head_mask(MEM_WIDTH, h)
            qh = jnp.where(msk, q, 0.0).astype(bf16)
            sc = _dot_nt(qh, k) * ATT_SCALE
            e = jnp.exp(sc - jnp.max(sc, axis=-1, keepdims=True))
            p = e / jnp.sum(e, axis=-1, keepdims=True)
            out = jnp.where(msk, _dot(p.astype(bf16), v), out)
        o_ref[...] = out.astype(bf16)

    return pl.pallas_call(body, grid=(s // tm,),
                          in_specs=[pl.BlockSpec((tm, MEM_WIDTH), lambda i: (i, q_block)), pl.BlockSpec((n_mem, 2 * MEM_WIDTH), lambda i: (0, 0)), _ANY],
                          out_specs=pl.BlockSpec((tm, MEM_WIDTH), lambda i: (i, out_block)), out_shape=SDS(into.shape, bf16), name=name,
                          input_output_aliases={2: 0}, compiler_params=_cparams("parallel"))(proj, kv, into)


def mem_attn_bwd(proj, q_block, kv, d_mixed, into, name, tm=512):
    s = proj.shape[0]
    n_mem = kv.shape[0]
    out_block = into.shape[1] // MEM_WIDTH - 1

    def body(q_ref, kv_ref, do_ref, into_ref, dq_ref, dkv_ref):
        @pl.when(pl.program_id(0) == 0)
        def _():
            dkv_ref[...] = jnp.zeros_like(dkv_ref)

        q = q_ref[...].astype(f32)
        do = do_ref[...]
        k = kv_ref[:, :MEM_WIDTH].astype(bf16)
        v = kv_ref[:, MEM_WIDTH:].astype(bf16)
        dq = jnp.zeros((tm, MEM_WIDTH), f32)
        dk = jnp.zeros((n_mem, MEM_WIDTH), f32)
        dv = jnp.zeros((n_mem, MEM_WIDTH), f32)
        for h in range(MEM_HEADS):
            msk = _head_mask(MEM_WIDTH, h)
            qh = jnp.where(msk, q, 0.0).astype(bf16)
            doh = jnp.where(msk, do, 0.0).astype(bf16)
            sc = _dot_nt(qh, k) * ATT_SCALE
            e = jnp.exp(sc - jnp.max(sc, axis=-1, keepdims=True))
            p = e / jnp.sum(e, axis=-1, keepdims=True)
            dp = _dot_nt(doh, v)
            ds = p * (dp - jnp.sum(dp * p, axis=-1, keepdims=True))
            dsb = (ds * ATT_SCALE).astype(bf16)
            dq = jnp.where(msk, _dot(dsb, k), dq)
            dk = dk + _dot_tn(dsb, qh)
            dv = dv + _dot_tn(p.astype(bf16), doh)
        dq_ref[...] = dq.astype(bf16)
        dkv_ref[:, :MEM_WIDTH] += dk
        dkv_ref[:, MEM_WIDTH:] += dv

    return pl.pallas_call(
        body, grid=(s // tm,),
        in_specs=[pl.BlockSpec((tm, MEM_WIDTH), lambda i: (i, q_block)), pl.BlockSpec((n_mem, 2 * MEM_WIDTH), lambda i: (0, 0)),
                  pl.BlockSpec((tm, MEM_WIDTH), lambda i: (i, MAIN_WIDTH // MEM_WIDTH)), _ANY],
        out_specs=[pl.BlockSpec((tm, MEM_WIDTH), lambda i: (i, out_block)), pl.BlockSpec((n_mem, 2 * MEM_WIDTH), lambda i: (0, 0))],
        out_shape=[SDS(into.shape, bf16), SDS((n_mem, 2 * MEM_WIDTH), f32)], name=name,
        input_output_aliases={3: 0}, compiler_params=_cparams("arbitrary"))(proj, kv, d_mixed, into)


def _tri(t, upper):
    r = lax.broadcasted_iota(jnp.int32, (t, t), 0)
    c = lax.broadcasted_iota(jnp.int32, (t, t), 1)
    return ((r <= c) if upper else (r >= c)).astype(f32)


def fgate_fwd(z_t, b, name, t=512):
    hh, s = z_t.shape

    def body(z_ref, b_ref, c_ref):
        u = _tri(t, True)
        carry = jnp.zeros((hh, 1), f32)
        for blk in range(s // t):
            x = z_ref[:, blk * t:(blk + 1) * t] + b_ref[...]
            logf = jnp.minimum(x, 0.0) - jnp.log(1.0 + jnp.exp(-jnp.abs(x)))
            y = jnp.dot(logf, u, precision=lax.Precision.HIGHEST, preferred_element_type=f32) + carry
            c_ref[:, blk * t:(blk + 1) * t] = y
            carry = y[:, t - 1:t]

    return pl.pallas_call(body, out_shape=SDS((hh, s), f32), name=name, compiler_params=_cparams())(z_t, b)


def fgate_bwd(dc_t, z_t, b, name, t=512):
    hh, s = z_t.shape

    def body(dc_ref, z_ref, b_ref, dz_ref, db_ref):
        low = _tri(t, False)
        carry = jnp.zeros((hh, 1), f32)
        total = jnp.zeros((hh, 1), f32)
        for blk in reversed(range(s // t)):
            cols = slice(blk * t, (blk + 1) * t)
            y = jnp.dot(dc_ref[:, cols], low, precision=lax.Precision.HIGHEST, preferred_element_type=f32) + carry
            carry = y[:, 0:1]
            dz = y * _sigmoid(-(z_ref[:, cols] + b_ref[...]))
            dz_ref[:, cols] = dz
            total = total + jnp.sum(dz, axis=-1, keepdims=True)
        db_ref[...] = jnp.broadcast_to(total, db_ref.shape)

    return pl.pallas_call(body, out_shape=[SDS((hh, s), f32), SDS((hh, LANES), f32)], name=name,
                          compiler_params=_cparams())(dc_t, z_t, b)


def _pair_masks():
    lane = lax.broadcasted_iota(jnp.int32, (1, LANES), 1)
    return [lane < HEAD_DIM, lane >= HEAD_DIM]


def _tile_base(cr_ref, hh, lo):
    return cr_ref[hh:hh + 1, pl.ds(lo, LANES)][:, 0:1]


def fox_fwd(q, kv, c_row, name, tq=512, out_width=MAIN_WIDTH):
    s = kv.shape[0]
    nq = s // tq

    def body(q_ref, k_ref, v_ref, cr_ref, o_ref, lse_ref, ob_ref):
        i = pl.program_id(1)
        qv = q_ref[...]
        masks = _pair_masks()
        row = lax.broadcasted_iota(jnp.int32, (tq, tq), 0)
        col = lax.broadcasted_iota(jnp.int32, (tq, tq), 1)
        qh = [jnp.where(masks[hh], qv, jnp.zeros((), bf16)) * ATT_SCALE for hh in range(2)]
        ct = [_tile_base(cr_ref, hh, pl.multiple_of(i * tq, tq)) for hh in range(2)]

        def block(j, carry, diag):
            lo = pl.multiple_of(j * tq, tq)
            ks = k_ref[pl.ds(lo, tq), :]
            vs = v_ref[pl.ds(lo, tq), :]
            out = []
            for hh in range(2):
                m, l, acc = carry[hh]
                sc = _dot_nt(qh[hh], ks) + (ct[hh] - cr_ref[hh:hh + 1, pl.ds(lo, tq)])
                if diag:
                    sc = jnp.where(col <= row, sc, -jnp.inf)
                m_new = jnp.maximum(m, jnp.max(sc, axis=-1, keepdims=True))
                alpha = jnp.exp(m - m_new)
                p = jnp.exp(sc - m_new)
                l = alpha * l + jnp.sum(p, axis=-1, keepdims=True)
                p_hi = p.astype(bf16)
                p_lo = (p - p_hi.astype(f32)).astype(bf16)
                acc = alpha * acc + (_dot(p_hi, vs) + _dot(p_lo, vs))
                out.append((m_new, l, acc))
            return tuple(out)

        init = (jnp.full((tq, 1), -jnp.inf, f32), jnp.zeros((tq, 1), f32), jnp.zeros((tq, LANES), f32))
        carry = lax.fori_loop(0, i, functools.partial(block, diag=False), (init, init))
        res = [(acc / l, m + jnp.log(l)) for m, l, acc in block(i, carry, True)]
        out = jnp.where(masks[0], res[0][0], res[1][0])
        o_ref[...] = out
        ob_ref[...] = out.astype(bf16)
        lse_ref[...] = jnp.where(masks[0], res[0][1], res[1][1])

    return pl.pallas_call(
        body, grid=(FOX_PAIRS, nq),
        in_specs=[pl.BlockSpec((tq, LANES), lambda p, i: (i, p)), pl.BlockSpec((s, LANES), lambda p, i: (0, p)),
                  pl.BlockSpec((s, LANES), lambda p, i: (0, FOX_PAIRS + p)), pl.BlockSpec((None, 2, s), lambda p, i: (p, 0, 0))],
        out_specs=[pl.BlockSpec((tq, LANES), lambda p, i: (i, p)), pl.BlockSpec((None, tq, LANES), lambda p, i: (p, i, 0)),
                   pl.BlockSpec((tq, LANES), lambda p, i: (i, p))],
        out_shape=[SDS((s, MAIN_WIDTH), f32), SDS((FOX_PAIRS, s, LANES), f32), SDS((s, out_width), bf16)], name=name,
        compiler_params=_cparams("parallel", "parallel"))(q, kv, kv, c_row)


def fox_bwd(q, kv, d_mixed, o, lse, c_row, name, tq=512, dq_width=MAIN_WIDTH):
    s = kv.shape[0]
    nq = s // tq

    def body(q_ref, k_ref, v_ref, do_ref, o_ref, lse_ref, cr_ref, dqb_ref, dk_ref, dv_ref, dc_ref, dq_ref):
        j = pl.program_id(1)

        @pl.when(j == 0)
        def _():
            dq_ref[...] = jnp.zeros_like(dq_ref)

        masks = _pair_masks()
        sub = lax.broadcasted_iota(jnp.int32, (LANES, 1), 0)
        sub_masks = [sub < HEAD_DIM, sub >= HEAD_DIM]
        row = lax.broadcasted_iota(jnp.int32, (tq, tq), 0)
        col = lax.broadcasted_iota(jnp.int32, (tq, tq), 1)
        kj = k_ref[...]
        vj = v_ref[...]
        lo_j = pl.multiple_of(j * tq, tq)

        def block(i, carry, diag):
            dk_t, dv_t, dc0, dc1 = carry
            dcs = [dc0, dc1]
            lo = pl.multiple_of(i * tq, tq)
            qi = q_ref[pl.ds(lo, tq), :]
            qi = qi * ATT_SCALE
            qt_i = qi.T
            doi = do_ref[pl.ds(lo, tq), :]
            dot_i = doi.astype(bf16).T
            prod = doi.astype(bf16).astype(f32) * o_ref[pl.ds(lo, tq), :]
            lse_i = lse_ref[pl.ds(lo, tq), :]
            dq_i = jnp.zeros((tq, LANES), f32)
            for hh in range(2):
                qh = jnp.where(masks[hh], qi, jnp.zeros((), bf16))
                doh = jnp.where(masks[hh], doi, 0.0).astype(bf16)
                delta = jnp.sum(jnp.where(masks[hh], prod, 0.0), axis=-1, keepdims=True)
                sc = _dot_nt(qh, kj) + (_tile_base(cr_ref, hh, lo) - cr_ref[hh:hh + 1, pl.ds(lo_j, tq)])
                p = jnp.exp(sc - lse_i[:, hh * HEAD_DIM:hh * HEAD_DIM + 1])
                if diag:
                    p = jnp.where(col <= row, p, 0.0)
                dv_t = dv_t + _dot(jnp.where(sub_masks[hh], dot_i, jnp.zeros((), bf16)), p.astype(bf16))
                ds = p * (_dot_nt(doh, vj) - delta)
                dcs[hh] = dcs[hh] + jnp.sum(ds, axis=0, keepdims=True)
                dsb = ds.astype(bf16)
                dq_i = jnp.where(masks[hh], _dot(dsb, kj), dq_i)
                dk_t = dk_t + _dot(jnp.where(sub_masks[hh], qt_i, jnp.zeros((), bf16)), dsb)
            dq_ref[pl.ds(lo, tq), :] += dq_i * ATT_SCALE
            return dk_t, dv_t, dcs[0], dcs[1]

        zero = jnp.zeros((LANES, tq), f32)
        zrow = jnp.zeros((1, tq), f32)
        carry = block(j, (zero, zero, zrow, zrow), True)
        dk_t, dv_t, dc0, dc1 = lax.fori_loop(j + 1, nq, functools.partial(block, diag=False), carry)
        dk_ref[...] = dk_t.T.astype(bf16)
        dv_ref[...] = dv_t.T.astype(bf16)
        dc_ref[0:1, :] = -dc0
        dc_ref[1:2, :] = -dc1

        @pl.when(j == nq - 1)
        def _():
            dqb_ref[...] = dq_ref[...].astype(bf16)

    full = lambda p, j: (0, p)
    tile = lambda p, j: (j, p)
    return pl.pallas_call(
        body, grid=(FOX_PAIRS, nq),
        in_specs=[pl.BlockSpec((s, LANES), full), pl.BlockSpec((tq, LANES), tile), pl.BlockSpec((tq, LANES), lambda p, j: (j, FOX_PAIRS + p)),
                  pl.BlockSpec((s, LANES), full), pl.BlockSpec((s, LANES), full), pl.BlockSpec((None, s, LANES), lambda p, j: (p, 0, 0)),
                  pl.BlockSpec((None, 2, s), lambda p, j: (p, 0, 0))],
        out_specs=[pl.BlockSpec((s, LANES), full), pl.BlockSpec((tq, LANES), tile), pl.BlockSpec((tq, LANES), tile),
                   pl.BlockSpec((None, 2, tq), lambda p, j: (p, 0, j))],
        out_shape=[SDS((s, dq_width), bf16), SDS((s, MAIN_WIDTH), bf16), SDS((s, MAIN_WIDTH), bf16), SDS((FOX_PAIRS, 2, s), f32)],
        scratch_shapes=[pltpu.VMEM((s, LANES), f32)],
        name=name, compiler_params=_cparams("parallel", "arbitrary"))(q, kv, kv, d_mixed, o, lse, c_row)


def adamw(w, g, m, v, name, tr=256):
    r, c = w.shape
    tr = min(tr, r)
    assert r % tr == 0, (name, r, tr)
    c1 = 1.0 / (1.0 - ADAM_B1 ** ADAM_STEP)
    c2 = 1.0 / (1.0 - ADAM_B2 ** ADAM_STEP)

    def body(w_ref, g_ref, m_ref, v_ref, d_ref, mo_ref, vo_ref):
        gv = g_ref[...]
        mn = ADAM_B1 * m_ref[...] + (1.0 - ADAM_B1) * gv
        vn = ADAM_B2 * v_ref[...] + (1.0 - ADAM_B2) * gv * gv
        mo_ref[...] = mn
        vo_ref[...] = vn
        d_ref[...] = -ADAM_LR * ((mn * c1) / (jnp.sqrt(vn * c2) + ADAM_EPS) + ADAM_WD * w_ref[...])

    spec = pl.BlockSpec((tr, c), lambda i: (i, 0))
    return pl.pallas_call(body, grid=(r // tr,), in_specs=[spec] * 4, out_specs=[spec] * 3, out_shape=[SDS((r, c), f32)] * 3,
                          name=name, compiler_params=_cparams("parallel"))(w, g, m, v)


def adamw_owned(w, parts, m, v, name, tr):
    nl, r, c = w.shape
    cp = parts[0].shape[2]
    assert r % tr == 0 and len(parts) == nl, (name, r, tr)
    c1 = 1.0 / (1.0 - ADAM_B1 ** ADAM_STEP)
    c2 = 1.0 / (1.0 - ADAM_B2 ** ADAM_STEP)

    def body(*refs):
        w_ref, p_refs, (m_ref, v_ref) = refs[0], refs[1:1 + nl], refs[1 + nl:3 + nl]
        g_ref, d_ref, mo_ref, vo_ref = refs[3 + nl:]
        layer = pl.program_id(0)

        def total(p_ref):
            acc = p_ref[0].astype(f32)
            for k in range(1, N_DEV):
                acc = acc + p_ref[k].astype(f32)
            return acc

        gv = total(p_refs[0])
        for l in range(1, nl):
            gv = jnp.where(layer == l, total(p_refs[l]), gv)
        gv = gv[:, :c]
        g_ref[...] = gv
        mn = ADAM_B1 * m_ref[...] + (1.0 - ADAM_B1) * gv
        vn = ADAM_B2 * v_ref[...] + (1.0 - ADAM_B2) * gv * gv
        mo_ref[...] = mn
        vo_ref[...] = vn
        d_ref[...] = -ADAM_LR * ((mn * c1) / (jnp.sqrt(vn * c2) + ADAM_EPS) + ADAM_WD * w_ref[...])

    spec = pl.BlockSpec((None, tr, c), lambda l, i: (l, i, 0))
    part = pl.BlockSpec((N_DEV, tr, cp), lambda l, i: (0, i, 0))
    return pl.pallas_call(body, grid=(nl, r // tr), in_specs=[spec] + [part] * nl + [spec, spec], out_specs=[spec] * 4,
                          out_shape=[SDS((nl, r, c), f32)] * 4, name=name,
                          compiler_params=_cparams("parallel", "parallel"))(w, *parts, m, v)


def sum_leading(x, name, out_dtype=f32, tr=None):
    n, r, c = x.shape
    tr = tr or r
    assert r % tr == 0

    def body(x_ref, o_ref):
        acc = x_ref[0].astype(f32)
        for k in range(1, n):
            acc = acc + x_ref[k].astype(f32)
        o_ref[...] = acc.astype(out_dtype)

    return pl.pallas_call(body, grid=(r // tr,), in_specs=[pl.BlockSpec((n, tr, c), lambda i: (0, i, 0))],
                          out_specs=pl.BlockSpec((tr, c), lambda i: (i, 0)), out_shape=SDS((r, c), out_dtype), name=name,
                          compiler_params=_cparams("parallel"))(x)


_ANY = pl.BlockSpec(memory_space=pl.ANY)
_DMA = pltpu.SemaphoreType.DMA


_HBM = pl.BlockSpec(memory_space=pltpu.HBM)
_SEM = pl.BlockSpec(memory_space=pltpu.SEMAPHORE)
_EFFECT = pltpu.SideEffectType.DATAFLOW_SIDE_EFFECTING
_FLIPS = [(0, 0, 1), (1, 0, 0), (0, 1, 0), (1, 1, 0), (1, 0, 1), (0, 1, 1), (1, 1, 1)]


def _me():
    return lax.axis_index("x"), lax.axis_index("y"), lax.axis_index("c")


def _peers():
    mx, my, mc = _me()
    return [(jnp.bitwise_xor(mx, fx), jnp.bitwise_xor(my, fy), jnp.bitwise_xor(mc, fc)) for fx, fy, fc in _FLIPS]


def _index(dev):
    return 4 * dev[0] + 2 * dev[1] + dev[2]


def _win(ref, axis, k, size, count=1):
    idx = [slice(None)] * len(ref.shape)
    idx[axis] = pl.ds(k * size, count * size)
    return ref.at[tuple(idx)]


def _hbm(a):
    return pltpu.with_memory_space_constraint(a, pltpu.HBM)


def _exchange_start(srcs, lands, copies_of, name):
    n = len(srcs)

    def body(*refs):
        src = refs[:n]
        send_sems, recv_sems, self_sems = refs[2 * n:2 * n + 3]
        land = refs[3 * n + 3:4 * n + 3]
        token = refs[4 * n + 3]
        me = _index(_me())
        for a in range(n):
            for s_ref, d_ref, peer in copies_of(a, src[a], land[a], me):
                if peer is None:
                    pltpu.make_async_copy(s_ref, d_ref, self_sems.at[a]).start()
                else:
                    pltpu.make_async_remote_copy(src_ref=s_ref, dst_ref=d_ref, send_sem=send_sems.at[a], recv_sem=recv_sems.at[a],
                                                 device_id=peer, device_id_type=MESH).start()
        token[...] = jnp.zeros_like(token)

    outs = pl.pallas_call(
        body, name=name,
        out_shape=(_DMA((n,)), _DMA((n,)), _DMA((n,)), *[pltpu.HBM(s.shape, s.dtype) for s in srcs],
                   *[pltpu.HBM(l.shape, l.dtype) for l in lands], SDS((8, LANES), f32)),
        in_specs=[_HBM] * (2 * n), out_specs=(_SEM, _SEM, _SEM, *[_HBM] * (2 * n), pl.BlockSpec(memory_space=pltpu.VMEM)),
        input_output_aliases={i: 3 + i for i in range(2 * n)},
        compiler_params=pltpu.CompilerParams(has_side_effects=_EFFECT),
    )(*[_hbm(s) for s in srcs], *[_hbm(lax.empty(l.shape, l.dtype)) for l in lands])
    return dict(sems=outs[:3], srcs=list(outs[3:3 + n]), lands=list(outs[3 + n:3 + 2 * n]), token=outs[3 + 2 * n])


def _exchange_wait(started, waits_of, after, name, which=None):
    which = list(range(len(started["srcs"]))) if which is None else which
    srcs, lands = [started["srcs"][a] for a in which], [started["lands"][a] for a in which]
    n = len(which)

    def body(*refs):
        src = refs[:n]
        land = refs[n:2 * n]
        send_sems, recv_sems, self_sems = refs[2 * n:2 * n + 3]
        me = _index(_me())
        for pos, a in enumerate(which):
            seven, (s_ref, d_ref) = waits_of(a, src[pos], land[pos], me)
            both = pltpu.make_async_remote_copy(src_ref=seven, dst_ref=seven, send_sem=send_sems.at[a], recv_sem=recv_sems.at[a],
                                                device_id=_me(), device_id_type=MESH)
            both.wait_send()
            both.wait_recv()
            pltpu.make_async_copy(s_ref, d_ref, self_sems.at[a]).wait()

    outs = pl.pallas_call(
        body, name=name, out_shape=tuple(pltpu.HBM(t.shape, t.dtype) for t in srcs + lands),
        in_specs=[_HBM] * (2 * n) + [_SEM] * 3 + [_ANY], out_specs=tuple([_HBM] * (2 * n)),
        input_output_aliases={i: i for i in range(2 * n)},
        compiler_params=pltpu.CompilerParams(has_side_effects=_EFFECT),
    )(*srcs, *lands, *started["sems"], after)
    return list(outs[n:])


def gather_start(locs, axes, name):
    lands = [SDS(tuple(N_DEV * d if i == ax else d for i, d in enumerate(l.shape)), l.dtype) for l, ax in zip(locs, axes)]

    def copies_of(a, src, land, me):
        mine = _win(land, axes[a], me, src.shape[axes[a]])
        return [(src, mine, peer) for peer in _peers()] + [(src, mine, None)]

    return _exchange_start(locs, lands, copies_of, name)


def gather_wait(started, axes, after, name, which=None):
    def waits_of(a, src, land, me):
        size = src.shape[axes[a]]
        return _win(land, axes[a], 0, size, N_DEV - 1), (src, _win(land, axes[a], me, size))

    return _exchange_wait(started, waits_of, after, name, which)


def scatter_start(grads, axes, name):
    lands = [SDS((N_DEV,) + tuple(d // N_DEV if i == ax else d for i, d in enumerate(g.shape)), g.dtype) for g, ax in zip(grads, axes)]

    def copies_of(a, src, land, me):
        size = src.shape[axes[a]] // N_DEV
        out = [(_win(src, axes[a], _index(peer), size), land.at[me], peer) for peer in _peers()]
        return out + [(_win(src, axes[a], me, size), land.at[me], None)]

    return _exchange_start(grads, lands, copies_of, name)


def scatter_wait(started, axes, after, name):
    def waits_of(a, src, land, me):
        size = src.shape[axes[a]] // N_DEV
        return land.at[pl.ds(0, N_DEV - 1)], (_win(src, axes[a], me, size), land.at[me])

    return _exchange_wait(started, waits_of, after, name)


def _row_tile(rows, cap=512):
    return max(t for t in range(8, min(rows, cap) + 1, 8) if rows % t == 0)


_SMALL = [
    ("ln_mix_pre", (2, 1024)), ("ln_mix_post", (2, 1024)), ("ln_ffn_pre", (2, 1024)), ("ln_ffn_post", (2, 1024)),
    ("ln_mem", (2, 1024)), ("w_spatial", (1, 6, 128, 128)), ("b_spatial", (1, 6, 128)), ("ln_shared", (1024,)),
    ("b_forget", (12,)), ("ln_v_g", (1, 768)), ("ln_v_b", (1, 768)),
]
_SMALL_TILE = 8 * LANES


def _small_rows(shape):
    return -(-math.prod(shape) // _SMALL_TILE) * 8


def _pack_small(vals, shapes):
    parts = []
    for name, shape in shapes:
        flat = vals[name].reshape(-1).astype(f32)
        rows = _small_rows(shape)
        parts.append(jnp.pad(flat, (0, rows * LANES - flat.shape[0])).reshape(rows, LANES))
    return jnp.concatenate(parts, axis=0)


def _unpack_small(buf, shapes):
    out = {}
    lo = 0
    for name, shape in shapes:
        rows = _small_rows(shape)
        out[name] = buf[lo:lo + rows].reshape(-1)[:math.prod(shape)].reshape(shape)
        lo += rows
    return out


def kernel(x, mem, ln_mix_pre, ln_mix_post, ln_ffn_pre, ln_ffn_post, ln_mem, w_mem_kv, w_out, w_ffn_gate, w_ffn_up, w_ffn_down, w_in_a, w_spatial, b_spatial, ln_v_g, ln_v_b, ln_shared, w_shared_kv, b_forget, w_in_b, loss_target, m_ln_mix_pre, m_ln_mix_post, m_ln_ffn_pre, m_ln_ffn_post, m_ln_mem, m_w_mem_kv, m_w_out, m_w_ffn_gate, m_w_ffn_up, m_w_ffn_down, m_w_in_a, m_w_spatial, m_b_spatial, m_ln_v_g, m_ln_v_b, m_ln_shared, m_w_shared_kv, m_b_forget, m_w_in_b, v_ln_mix_pre, v_ln_mix_post, v_ln_ffn_pre, v_ln_ffn_post, v_ln_mem, v_w_mem_kv, v_w_out, v_w_ffn_gate, v_w_ffn_up, v_w_ffn_down, v_w_in_a, v_w_spatial, v_b_spatial, v_ln_v_g, v_ln_v_b, v_ln_shared, v_w_shared_kv, v_b_forget, v_w_in_b):
    weights = dict(ln_mix_pre=ln_mix_pre, ln_mix_post=ln_mix_post, ln_ffn_pre=ln_ffn_pre, ln_ffn_post=ln_ffn_post, ln_mem=ln_mem,
                   w_mem_kv=w_mem_kv, w_out=w_out, w_ffn_gate=w_ffn_gate, w_ffn_up=w_ffn_up, w_ffn_down=w_ffn_down, w_in_a=w_in_a,
                   w_spatial=w_spatial, b_spatial=b_spatial, ln_v_g=ln_v_g, ln_v_b=ln_v_b, ln_shared=ln_shared,
                   w_shared_kv=w_shared_kv, b_forget=b_forget, w_in_b=w_in_b)
    mom_m = dict(ln_mix_pre=m_ln_mix_pre, ln_mix_post=m_ln_mix_post, ln_ffn_pre=m_ln_ffn_pre, ln_ffn_post=m_ln_ffn_post, ln_mem=m_ln_mem,
                 w_mem_kv=m_w_mem_kv, w_out=m_w_out, w_ffn_gate=m_w_ffn_gate, w_ffn_up=m_w_ffn_up, w_ffn_down=m_w_ffn_down, w_in_a=m_w_in_a,
                 w_spatial=m_w_spatial, b_spatial=m_b_spatial, ln_v_g=m_ln_v_g, ln_v_b=m_ln_v_b, ln_shared=m_ln_shared,
                 w_shared_kv=m_w_shared_kv, b_forget=m_b_forget, w_in_b=m_w_in_b)
    mom_v = dict(ln_mix_pre=v_ln_mix_pre, ln_mix_post=v_ln_mix_post, ln_ffn_pre=v_ln_ffn_pre, ln_ffn_post=v_ln_ffn_post, ln_mem=v_ln_mem,
                 w_mem_kv=v_w_mem_kv, w_out=v_w_out, w_ffn_gate=v_w_ffn_gate, w_ffn_up=v_w_ffn_up, w_ffn_down=v_w_ffn_down, w_in_a=v_w_in_a,
                 w_spatial=v_w_spatial, b_spatial=v_b_spatial, ln_v_g=v_ln_v_g, ln_v_b=v_ln_v_b, ln_shared=v_ln_shared,
                 w_shared_kv=v_w_shared_kv, b_forget=v_b_forget, w_in_b=v_w_in_b)
    names = list(weights)
    mx, my, mc = lax.axis_index("x"), lax.axis_index("y"), lax.axis_index("c")
    me = 4 * mx + 2 * my + mc

    h0 = x[0]
    mem0 = mem[0]
    tgt = loss_target[0]
    seq = h0.shape[0]

    vec = lambda a: a.reshape(1, -1)
    pad_to = lambda a, axis, size: jnp.pad(a, [(0, size - a.shape[i] if i == axis else 0) for i in range(a.ndim)])

    def after(tok, a):
        return a + tok[0, 0].astype(a.dtype)

    lnv_loc = pad_to(jnp.concatenate([ln_v_g, ln_v_b], axis=0), 0, 8)
    st_a = gather_start([w_in_a.astype(bf16), pad_to(lnv_loc, 1, LANES)[None]], [0, 0], "gather_a_start")
    mix_locs = lambda l, tok: [after(tok, w_mem_kv[l]).astype(bf16), w_out[l].astype(bf16)]

    def ffn_gather_start(l, tok):
        gate_up = gather_start([pad_to(after(tok, w_ffn_gate[l]).astype(bf16), 1, FF_SHARD_PAD),
                                pad_to(w_ffn_up[l].astype(bf16), 1, FF_SHARD_PAD)], [1, 1], f"gather_gate_up{l}_start")
        down = gather_start([pad_to(after(gate_up["token"], w_ffn_down[l]).astype(bf16), 0, FF_SHARD_PAD)], [0], f"gather_down{l}_start")
        return gate_up, down

    st_b = [gather_start(mix_locs(0, st_a["token"]), [0, 0], "gather_b0_start"), None]
    st_c = ffn_gather_start(0, st_b[0]["token"])
    st_d = gather_start([after(st_c[1]["token"], w_in_b[0]).astype(bf16), pad_to(w_shared_kv.astype(bf16), 1, KV_PAD)], [0, 0],
                        "gather_d_start")
    st_b[1] = gather_start(mix_locs(1, st_d["token"]), [0, 0], "gather_b1_start")
    st_e = ffn_gather_start(1, st_b[1]["token"])
    ws = w_spatial[0].astype(bf16)
    ws_t = ws.transpose(0, 2, 1)
    bs_t = b_spatial[0].T

    (a0,) = rms_fwd(h0, [after(st_e[1]["token"], vec(ln_mix_pre[0]))], "a0_norm")
    w_in_a8, lnv8 = gather_wait(st_a, [0, 0], a0, "gather_a_wait")
    w_in_a_full = w_in_a8.transpose(1, 0, 2).reshape(D_MODEL, -1)
    lnv_g = lnv8[:, 0, :MAIN_WIDTH // N_DEV].reshape(1, MAIN_WIDTH)
    lnv_b = lnv8[:, 1, :MAIN_WIDTH // N_DEV].reshape(1, MAIN_WIDTH)
    proj0 = mm(a0, w_in_a_full, "proj0", tn=896)
    main0 = gmlp_fwd(proj0, ws, bs_t, lnv_g, lnv_b, "gmlp_fwd", out_width=D_MODEL)
    w_mkv, w_o = [None, None], [None, None]
    w_mkv[0], w_o[0] = gather_wait(st_b[0], [0, 0], main0, "gather_b0_wait")
    (memn0,) = rms_fwd(mem0, [vec(ln_mem[0])], "mem0_norm")
    kvm0 = mm(memn0, w_mkv[0], "kvm0")
    mixed0 = mem_attn_fwd(proj0, 2 * MAIN_WIDTH // MEM_WIDTH, kvm0, main0, "mem_attn0")
    y1_0, hmid0, f0 = mm_resnorm(mixed0, w_o[0], h0, vec(ln_mix_post[0]), [vec(ln_ffn_pre[0])], "mix_out0")
    w_g0, w_u0 = gather_wait(st_c[0], [1, 1], f0, "gather_gate_up0_wait")
    gu0, act0 = ffn_up(f0, w_g0, w_u0, "ffn_up0")
    (w_d0,) = gather_wait(st_c[1], [0], act0, "gather_down0_wait")
    y2_0, h1, a1, sin1 = mm_resnorm(act0, w_d0, hmid0, vec(ln_ffn_post[0]), [vec(ln_mix_pre[1]), vec(ln_shared)], "ffn_down0")

    w_inb, w_kv = gather_wait(st_d, [0, 0], sin1, "gather_d_wait")
    kvb = mm(sin1, w_kv, "kv_shared", out_dtype=bf16, tn=MAIN_WIDTH, ncols=2 * MAIN_WIDTH)
    zf = mm(sin1, w_kv, "forget_logits", tn=256, col0=2 * MAIN_WIDTH, ncols=256)
    qb = mm(a1, w_inb, "proj1", out_dtype=bf16)
    z_t = jnp.pad(zf[:, :FOX_HEADS].T, ((0, 16 - FOX_HEADS), (0, 0)))
    bf_col = jnp.pad(b_forget, (0, 16 - FOX_HEADS)).reshape(16, 1)
    c_t = fgate_fwd(z_t, bf_col, "fgate_fwd")
    c_row = c_t[:FOX_HEADS].reshape(FOX_PAIRS, 2, seq)
    main1, lse, main1_b = fox_fwd(qb, kvb, c_row, "fox_fwd", out_width=D_MODEL)
    w_mkv[1], w_o[1] = gather_wait(st_b[1], [0, 0], main1, "gather_b1_wait")
    (memn1,) = rms_fwd(mem0, [vec(ln_mem[1])], "mem1_norm")
    kvm1 = mm(memn1, w_mkv[1], "kvm1")
    mixed1 = mem_attn_fwd(qb, MAIN_WIDTH // MEM_WIDTH, kvm1, main1_b, "mem_attn1")
    y1_1, hmid1, f1 = mm_resnorm(mixed1, w_o[1], h1, vec(ln_mix_post[1]), [vec(ln_ffn_pre[1])], "mix_out1")
    w_g1, w_u1 = gather_wait(st_e[0], [1, 1], f1, "gather_gate_up1_wait")
    gu1, act1 = ffn_up(f1, w_g1, w_u1, "ffn_up1")
    (w_d1,) = gather_wait(st_e[1], [0], act1, "gather_down1_wait")
    dh, d_y2_1, dg_fpost1, loss_tile = mm_resnorm_loss(act1, w_d1, hmid1, vec(ln_ffn_post[1]), tgt, "ffn_down1_loss")
    ffn_w = [(w_g0, w_u0, w_d0), (w_g1, w_u1, w_d1)]

    small = {}

    def ffn_backward(layer, dh_out, d_y2, hmid, f, gu, act, y1):
        w_g, w_u, w_d = ffn_w[layer]
        dw_down = mm_tn(act, d_y2, f"dw_down{layer}")
        rs_down = scatter_start([dw_down], [0], f"scatter_down{layer}_start")
        d_g, d_u = ffn_act_grad(d_y2, w_d, gu, f"ffn_act_grad{layer}")
        dw_g = mm_tn(f, d_g, f"dw_gate{layer}", dep=rs_down["token"])
        dw_u = mm_tn(f, d_u, f"dw_up{layer}")
        rs_gate_up = scatter_start([dw_g, dw_u], [1, 1], f"scatter_gate_up{layer}_start")
        dh_mid, d_y1, dg_fpre, dg_mpost = ffn_in_grad(d_g, d_u, w_g, w_u, hmid, dh_out, after(rs_gate_up["token"], vec(ln_ffn_pre[layer])),
                                                      y1, vec(ln_mix_post[layer]), f"ffn_in_grad{layer}")
        return dh_mid, d_y1, dg_fpre, dg_mpost, (rs_down, rs_gate_up)

    def mix_out_backward(layer, d_y1, mixed):
        dw_out = mm_tn(mixed, d_y1, f"dw_out{layer}")
        d_mixed = mm(d_y1, w_o[layer], f"d_mixed{layer}", trans_b=True)
        return d_mixed, dw_out

    def mem_backward(layer, q_src, q_block, kvm, memn, d_mixed, into):
        d_qm, d_kvm = mem_attn_bwd(q_src, q_block, kvm, d_mixed, into, f"mem_attn_bwd{layer}")
        d_kvm_b = d_kvm.astype(bf16)
        dw_mkv = mm_tn(memn, d_kvm_b, f"dw_mem_kv{layer}")
        d_memn = mm(d_kvm_b, w_mkv[layer], f"d_memn{layer}", trans_b=True)
        _, dg_mem = rms_bwd(mem0, vec(ln_mem[layer]), d_memn, None, bf16, f"mem_norm_bwd{layer}")
        return d_qm, dw_mkv, dg_mem


    dh_mid1, d_y1_1, dg_fpre1, dg_mpost1, rs_ffn1 = ffn_backward(1, dh, d_y2_1, hmid1, f1, gu1, act1, y1_1)
    d_mixed1, dw_out1 = mix_out_backward(1, d_y1_1, mixed1)
    dq_b, dk, dv, dc = fox_bwd(qb, kvb, d_mixed1, main1, lse, c_row, "fox_bwd", dq_width=D_MODEL)
    d_proj1, dw_mkv1, dg_mem1 = mem_backward(1, qb, MAIN_WIDTH // MEM_WIDTH, kvm1, memn1, d_mixed1, dq_b)
    rs_mix1 = scatter_start([dw_out1, dw_mkv1], [0, 0], "scatter_mix1_start")
    dc_t = jnp.pad(dc.reshape(FOX_HEADS, seq), ((0, 16 - FOX_HEADS), (0, 0)))
    dz_t, db_f = fgate_bwd(dc_t, z_t, bf_col, "fgate_bwd")
    d_kvf = jnp.concatenate([dk, dv, jnp.pad(dz_t[:FOX_HEADS].T.astype(bf16), ((0, 0), (0, KV_PAD - KV_WIDTH)))], axis=-1)
    dw_in_b = mm_tn(a1, d_proj1, "dw_in_b", dep=rs_mix1["token"])
    dw_kv = mm_tn(sin1, d_kvf, "dw_kv", tn=896)
    rs_2 = scatter_start([dw_in_b, dw_kv], [0, 0], "scatter_shared_start")
    dh1, (dg_pre1, dg_shared), d_y2_0, dg_fpost0 = proj_in_grad(
        [(d_proj1, w_inb, vec(ln_mix_pre[1])), (d_kvf, w_kv, vec(ln_shared))], h1, dh_mid1, "in_grad1", dep=rs_2["token"],
        below=(y2_0, vec(ln_ffn_post[0])))

    dh_mid0, d_y1_0, dg_fpre0, dg_mpost0, rs_ffn0 = ffn_backward(0, dh1, d_y2_0, hmid0, f0, gu0, act0, y1_0)
    d_mixed0, dw_out0 = mix_out_backward(0, d_y1_0, mixed0)
    d_uv, dw_s, db_s, dg_lnv, db_lnv = gmlp_bwd(proj0, d_mixed0, ws, ws_t, bs_t, lnv_g, lnv_b, "gmlp_bwd", out_width=w_in_a_full.shape[1])
    d_proj0, dw_mkv0, dg_mem0 = mem_backward(0, proj0, 2 * MAIN_WIDTH // MEM_WIDTH, kvm0, memn0, d_mixed0, d_uv)
    rs_mix0 = scatter_start([dw_out0, dw_mkv0], [0, 0], "scatter_mix0_start")

    small["ln_mix_pre"] = jnp.concatenate([jnp.zeros_like(dg_pre1), dg_pre1], axis=0)
    small["ln_mix_post"] = jnp.concatenate([dg_mpost0, dg_mpost1], axis=0)
    small["ln_ffn_pre"] = jnp.concatenate([dg_fpre0, dg_fpre1], axis=0)
    small["ln_ffn_post"] = jnp.concatenate([dg_fpost0, dg_fpost1], axis=0)
    small["ln_mem"] = jnp.concatenate([dg_mem0, dg_mem1], axis=0)
    small["w_spatial"] = dw_s[None]
    small["b_spatial"] = db_s[:, :A_GROUPS].T[None]
    small["ln_shared"] = dg_shared[0]
    small["b_forget"] = db_f[:FOX_HEADS, 0]
    small["ln_v_g"] = dg_lnv
    small["ln_v_b"] = db_lnv
    small_rows = jnp.concatenate([_pack_small(small, _SMALL), after(rs_mix0["token"], loss_tile)], axis=0)
    st_small = gather_start([small_rows[None]], [0], "gather_small_grads_start")
    dw_in_a_t = mm_tn(d_proj0, a0, "dw_in_a", tk=896, dep=st_small["token"])
    rs_in_a = scatter_start([dw_in_a_t], [0], "scatter_in_a_start")
    grad_x, (dg_pre0,) = proj_in_grad([(d_proj0, w_in_a_full, vec(ln_mix_pre[0]))], h0, dh_mid0, "in_grad0", dep=rs_in_a["token"])
    st_last = gather_start([dg_pre0.reshape(1, 8, LANES)], [0], "gather_last_grad_start")

    (p_down1,) = scatter_wait(rs_ffn1[0], [0], after(st_last["token"], grad_x[:8, :LANES]), "scatter_down1_wait")
    p_gate1, p_up1 = scatter_wait(rs_ffn1[1], [1, 1], p_down1, "scatter_gate_up1_wait")
    p_out1, p_mkv1 = scatter_wait(rs_mix1, [0, 0], p_gate1, "scatter_mix1_wait")
    p_in_b, p_kv = scatter_wait(rs_2, [0, 0], p_out1, "scatter_shared_wait")
    (p_down0,) = scatter_wait(rs_ffn0[0], [0], p_in_b, "scatter_down0_wait")
    p_gate0, p_up0 = scatter_wait(rs_ffn0[1], [1, 1], p_down0, "scatter_gate_up0_wait")
    p_out0, p_mkv0 = scatter_wait(rs_mix0, [0, 0], p_gate0, "scatter_mix0_wait")
    (p_in_a,) = scatter_wait(rs_in_a, [0], p_out0, "scatter_in_a_wait")
    owned_parts = dict(w_ffn_gate=[p_gate0, p_gate1], w_ffn_up=[p_up0, p_up1], w_ffn_down=[p_down0, p_down1], w_out=[p_out0, p_out1],
                       w_mem_kv=[p_mkv0, p_mkv1], w_in_b=[p_in_b], w_shared_kv=[p_kv])
    g_in_a = sum_leading(p_in_a, "sum_in_a", tr=_row_tile(p_in_a.shape[1]))
    (small_all,) = gather_wait(st_small, [0], g_in_a, "gather_small_grads_wait")
    (last_all,) = gather_wait(st_last, [0], small_all, "gather_last_grad_wait")
    small_sum = sum_leading(small_all, "sum_small_grads")
    loss = small_sum[small_rows.shape[0] - 1, 0]
    g_small = _unpack_small(small_sum, _SMALL)
    g_small["ln_mix_pre"] = jnp.concatenate([sum_leading(last_all, "sum_last_grad").reshape(1, D_MODEL), g_small["ln_mix_pre"][1:]], axis=0)
    shard = MAIN_WIDTH // N_DEV
    for n in ("ln_v_g", "ln_v_b"):
        g_small[n] = lax.dynamic_slice_in_dim(g_small[n], me * shard, shard, axis=1)
    grad_w = dict(g_small)

    delta, new_m, new_v = {}, {}, {}
    for n, parts in owned_parts.items():
        shape = weights[n].shape
        three_d = shape if len(shape) == 3 else (1,) + shape
        outs = adamw_owned(weights[n].reshape(three_d), parts, mom_m[n].reshape(three_d), mom_v[n].reshape(three_d), f"adamw_{n}",
                           tr=_row_tile(three_d[1]))
        grad_w[n], delta[n], new_m[n], new_v[n] = (t.reshape(shape) for t in outs)
    grad_w["w_in_a"] = g_in_a.T[None]
    d_, m_, v_ = adamw(w_in_a[0], g_in_a.T, m_w_in_a[0], v_w_in_a[0], "adamw_w_in_a", tr=512)
    delta["w_in_a"], new_m["w_in_a"], new_v["w_in_a"] = d_[None], m_[None], v_[None]
    small_local_shapes = [(n, tuple(weights[n].shape)) for n, _ in _SMALL]
    packed = [_pack_small(src, small_local_shapes) for src in (weights, grad_w, mom_m, mom_v)]
    outs = adamw(*packed, "adamw_small", tr=packed[0].shape[0])
    for dst, buf in zip((delta, new_m, new_v), outs):
        dst.update(_unpack_small(buf, small_local_shapes))

    return (loss, grad_x[None], *[grad_w[n] for n in names], *[delta[n] for n in names],
            *[new_m[n] for n in names], *[new_v[n] for n in names])
```

```python
import functools
import math

import jax
import jax.numpy as jnp
from jax import lax
from jax.experimental import pallas as pl
from jax.experimental.pallas import tpu as pltpu

f32 = jnp.float32
bf16 = jnp.bfloat16
SDS = jax.ShapeDtypeStruct

D_MODEL = 1024
MAIN_WIDTH = 768
MEM_WIDTH = 256
HEAD_DIM = 64
MEM_HEADS = 4
FOX_HEADS = 12
FOX_PAIRS = FOX_HEADS // 2
CHUNK = 128
A_GROUPS = 6
FF_SHARD = 352
FF_SHARD_PAD = 384
FF_PAD = 8 * FF_SHARD_PAD
KV_WIDTH = 2 * MAIN_WIDTH + FOX_HEADS
KV_PAD = 1792
RMS_EPS = 1e-6
LN_EPS = 1e-5
ATT_SCALE = HEAD_DIM ** -0.5
ADAM_LR, ADAM_B1, ADAM_B2, ADAM_EPS, ADAM_WD, ADAM_STEP = 0.001, 0.9, 0.999, 1e-08, 0.01, 10
N_DEV = 8
AXES = ("x", "y", "c")
MESH = pl.DeviceIdType.MESH
V7X_VMEM_LIMIT = 56 * 1024 * 1024
LANES = 128
FLAT_W = 512
ROW_PAD = 16


def _cparams(*sem):
    return pltpu.CompilerParams(dimension_semantics=sem or None, vmem_limit_bytes=V7X_VMEM_LIMIT)


def _dot(a, b):
    return jnp.dot(a, b, preferred_element_type=f32)


def _dot_nt(a, b):
    return lax.dot_general(a, b, (((1,), (1,)), ((), ())), preferred_element_type=f32)


def _dot_tn(a, b):
    return lax.dot_general(a, b, (((0,), (0,)), ((), ())), preferred_element_type=f32)


def _gelu(x):
    k = math.sqrt(2.0 / math.pi)
    t = jnp.tanh(k * (x + 0.044715 * x * x * x))
    return 0.5 * x * (1.0 + t), t


def _gelu_grad(x, t):
    k = math.sqrt(2.0 / math.pi)
    return 0.5 * (1.0 + t) + 0.5 * x * (1.0 - t * t) * k * (1.0 + 3.0 * 0.044715 * x * x)


def _sigmoid(x):
    return 1.0 / (1.0 + jnp.exp(-x))


def rms_fwd(x, gains, name, tm=512):
    m, d = x.shape
    tm = min(tm, m)
    n = len(gains)

    def body(x_ref, *refs):
        xv = x_ref[...]
        y = xv * lax.rsqrt(jnp.sum(xv * xv, axis=-1, keepdims=True) * (1.0 / d) + RMS_EPS)
        for g_ref, o_ref in zip(refs[:n], refs[n:]):
            o_ref[...] = (y * g_ref[...]).astype(bf16)

    row = pl.BlockSpec((tm, d), lambda i: (i, 0))
    vec = pl.BlockSpec((1, d), lambda i: (0, 0))
    return pl.pallas_call(body, grid=(m // tm,), in_specs=[row] + [vec] * n, out_specs=[row] * n,
                          out_shape=[SDS((m, d), bf16)] * n, name=name, compiler_params=_cparams("parallel"))(x, *gains)


def rms_bwd(x, g, dy, add, out_dtype, name, tm=512):
    m, d = x.shape
    tm = min(tm, m)
    has_add = add is not None

    def body(x_ref, g_ref, dy_ref, *refs):
        dx_ref, dg_ref = refs[-2], refs[-1]
        xv = x_ref[...]
        dyv = dy_ref[...].astype(f32)
        r = lax.rsqrt(jnp.sum(xv * xv, axis=-1, keepdims=True) * (1.0 / d) + RMS_EPS)
        xn = xv * r
        dyg = dyv * g_ref[...]
        dx = r * (dyg - xn * (jnp.sum(dyg * xn, axis=-1, keepdims=True) * (1.0 / d)))
        if has_add:
            dx = dx + refs[0][...]
        dx_ref[...] = dx.astype(out_dtype)

        @pl.when(pl.program_id(0) == 0)
        def _():
            dg_ref[...] = jnp.zeros_like(dg_ref)

        dg_ref[...] += jnp.sum(dyv * xn, axis=0, keepdims=True)

    row = pl.BlockSpec((tm, d), lambda i: (i, 0))
    vec = pl.BlockSpec((1, d), lambda i: (0, 0))
    ins = [x, g, dy] + ([add] if has_add else [])
    return pl.pallas_call(body, grid=(m // tm,), in_specs=[row, vec, row] + ([row] if has_add else []),
                          out_specs=[row, vec], out_shape=[SDS((m, d), out_dtype), SDS((1, d), f32)], name=name,
                          compiler_params=_cparams("arbitrary"))(*ins)


def mm(a, b, name, trans_b=False, out_dtype=f32, tm=1024, tn=1024, layer=None, col0=0, ncols=None, dep=None):
    m, k = a.shape
    n_all = b.shape[-2] if trans_b else b.shape[-1]
    n = n_all if ncols is None else ncols
    tm, tn = min(tm, m), min(tn, n)
    assert m % tm == 0 and n % tn == 0 and col0 % tn == 0 and not (trans_b and col0), (name, m, n, tm, tn)
    jb = col0 // tn
    lead = () if layer is None else (None,)
    sel = () if layer is None else (layer,)

    def body(a_ref, b_ref, *rest):
        r = _dot_nt(a_ref[...], b_ref[...]) if trans_b else _dot(a_ref[...], b_ref[...])
        rest[-1][...] = r.astype(out_dtype)

    if trans_b:
        b_spec = pl.BlockSpec(lead + (tn, k), lambda j, i: sel + (j, 0))
    else:
        b_spec = pl.BlockSpec(lead + (k, tn), lambda j, i: sel + (0, jb + j))
    deps = [] if dep is None else [dep]
    dep_specs = [pl.BlockSpec((8, LANES), lambda j, i: (0, 0))] * len(deps)
    return pl.pallas_call(body, grid=(n // tn, m // tm), in_specs=[pl.BlockSpec((tm, k), lambda j, i: (i, 0)), b_spec] + dep_specs,
                          out_specs=pl.BlockSpec((tm, tn), lambda j, i: (i, j)), out_shape=SDS((m, n), out_dtype),
                          name=name, compiler_params=_cparams("parallel", "parallel"))(a, b, *deps)


def mm_tn(a, g, name, tk=1024, tn=1024, out_dtype=bf16, dep=None):
    s, k = a.shape
    n = g.shape[1]
    tk, tn = min(tk, k), min(tn, n)
    assert k % tk == 0 and n % tn == 0, (name, k, n, tk, tn)

    def body(a_ref, g_ref, *rest):
        rest[-1][...] = _dot_tn(a_ref[...], g_ref[...]).astype(out_dtype)

    deps = [] if dep is None else [dep]
    dep_specs = [pl.BlockSpec((8, LANES), lambda i, j: (0, 0))] * len(deps)
    return pl.pallas_call(body, grid=(k // tk, n // tn),
                          in_specs=[pl.BlockSpec((s, tk), lambda i, j: (0, i)), pl.BlockSpec((s, tn), lambda i, j: (0, j))] + dep_specs,
                          out_specs=pl.BlockSpec((tk, tn), lambda i, j: (i, j)), out_shape=SDS((k, n), out_dtype), name=name,
                          compiler_params=_cparams("parallel", "parallel"))(a, g, *deps)


def _resident(shape, index_map):
    return pl.BlockSpec(shape, index_map, pipeline_mode=pl.Buffered(1))


def _rms(xv):
    return xv * lax.rsqrt(jnp.sum(xv * xv, axis=-1, keepdims=True) * (1.0 / xv.shape[-1]) + RMS_EPS)


def _rms_bwd_math(xv, g, dy):
    d = xv.shape[-1]
    r = lax.rsqrt(jnp.sum(xv * xv, axis=-1, keepdims=True) * (1.0 / d) + RMS_EPS)
    xn = xv * r
    dyg = dy * g
    dx = r * (dyg - xn * (jnp.sum(dyg * xn, axis=-1, keepdims=True) * (1.0 / d)))
    return dx, jnp.sum(dy * xn, axis=0, keepdims=True)


SUB_ROWS = 512


def mm_resnorm(a, b, h, g_post, gains, name, tm=512):
    m, k = a.shape
    d = b.shape[1]
    n = len(gains)

    def body(a_ref, b_ref, h_ref, gp_ref, *refs):
        for r in range(tm // SUB_ROWS):
            rows = slice(r * SUB_ROWS, (r + 1) * SUB_ROWS)
            y = _dot(a_ref[rows, :], b_ref[...])
            refs[n][rows, :] = y
            hn = h_ref[rows, :] + _rms(y) * gp_ref[...]
            refs[n + 1][rows, :] = hn
            if n:
                z = _rms(hn)
                for g_ref, o_ref in zip(refs[:n], refs[n + 2:]):
                    o_ref[rows, :] = (z * g_ref[...]).astype(bf16)

    row = pl.BlockSpec((tm, d), lambda i: (i, 0))
    vec = pl.BlockSpec((1, d), lambda i: (0, 0))
    return pl.pallas_call(body, grid=(m // tm,),
                          in_specs=[pl.BlockSpec((tm, k), lambda i: (i, 0)), _resident((k, d), lambda i: (0, 0)), row, vec] + [vec] * n,
                          out_specs=[row] * (n + 2), out_shape=[SDS((m, d), f32)] * 2 + [SDS((m, d), bf16)] * n, name=name,
                          compiler_params=_cparams("parallel"))(a, b, h, g_post, *gains)


def mm_resnorm_loss(a, b, h, g_post, tgt, name, tm=512):
    m, k = a.shape
    d = b.shape[1]

    def body(a_ref, b_ref, h_ref, gp_ref, t_ref, dh_ref, dy_ref, dg_ref, l_ref):
        @pl.when(pl.program_id(0) == 0)
        def _():
            dg_ref[...] = jnp.zeros_like(dg_ref)
            l_ref[...] = jnp.zeros_like(l_ref)

        y = _dot(a_ref[...], b_ref[...])
        e = h_ref[...] + _rms(y) * gp_ref[...] - t_ref[...]
        dh = e * (1.0 / d)
        dh_ref[...] = dh
        part = jnp.sum(jnp.sum(e * e, axis=-1, keepdims=True), axis=0, keepdims=True) * (0.5 / d)
        l_ref[...] += jnp.broadcast_to(part, l_ref.shape)
        dy, dg = _rms_bwd_math(y, gp_ref[...], dh)
        dy_ref[...] = dy.astype(bf16)
        dg_ref[...] += dg

    row = pl.BlockSpec((tm, d), lambda i: (i, 0))
    vec = pl.BlockSpec((1, d), lambda i: (0, 0))
    return pl.pallas_call(body, grid=(m // tm,),
                          in_specs=[pl.BlockSpec((tm, k), lambda i: (i, 0)), _resident((k, d), lambda i: (0, 0)), row, vec, row],
                          out_specs=[row, row, vec, pl.BlockSpec((8, LANES), lambda i: (0, 0))],
                          out_shape=[SDS((m, d), f32), SDS((m, d), bf16), SDS((1, d), f32), SDS((8, LANES), f32)], name=name,
                          compiler_params=_cparams("arbitrary"))(a, b, h, g_post, tgt)


def ffn_act_grad(d_y2, w_d, factors, name, tm=1024, tn=1536):
    s, d = d_y2.shape
    ff = w_d.shape[0]
    nb = ff // tn

    def body(a_ref, b_ref, g_ref, u_ref, dg_ref, du_ref):
        av = a_ref[...]
        tc = 256
        for c in range(tn // tc):
            cols = slice(c * tc, (c + 1) * tc)
            da = _dot_nt(av, b_ref[cols, :])
            dg_ref[:, cols] = (da * g_ref[:, cols].astype(f32)).astype(bf16)
            du_ref[:, cols] = (da * u_ref[:, cols].astype(f32)).astype(bf16)

    tile = pl.BlockSpec((tm, tn), lambda j, i: (i, j))
    return pl.pallas_call(body, grid=(nb, s // tm),
                          in_specs=[pl.BlockSpec((tm, d), lambda j, i: (i, 0)), pl.BlockSpec((tn, d), lambda j, i: (j, 0)), tile,
                                    pl.BlockSpec((tm, tn), lambda j, i: (i, nb + j))],
                          out_specs=[tile, tile], out_shape=[SDS((s, ff), bf16)] * 2, name=name,
                          compiler_params=_cparams("parallel", "parallel"))(d_y2, w_d, factors, factors)


def ffn_in_grad(d_g, d_u, w_g, w_u, hmid, dh_out, g_pre, y1, g_post, name, tm=512):
    s, ff = d_g.shape
    d = w_g.shape[0]

    def body(dg_ref, du_ref, wg_ref, wu_ref, hm_ref, dho_ref, gpre_ref, y1_ref, gpost_ref, dhm_ref, dy1_ref, dgpre_ref, dgpost_ref):
        @pl.when(pl.program_id(0) == 0)
        def _():
            dgpre_ref[...] = jnp.zeros_like(dgpre_ref)
            dgpost_ref[...] = jnp.zeros_like(dgpost_ref)

        for r in range(tm // SUB_ROWS):
            rows = slice(r * SUB_ROWS, (r + 1) * SUB_ROWS)
            d_f = _dot_nt(dg_ref[rows, :], wg_ref[...]) + _dot_nt(du_ref[rows, :], wu_ref[...])
            dx, dg1 = _rms_bwd_math(hm_ref[rows, :], gpre_ref[...], d_f)
            dh_mid = dho_ref[rows, :] + dx
            dhm_ref[rows, :] = dh_mid
            dgpre_ref[...] += dg1
            dy1, dg2 = _rms_bwd_math(y1_ref[rows, :], gpost_ref[...], dh_mid)
            dy1_ref[rows, :] = dy1.astype(bf16)
            dgpost_ref[...] += dg2

    row = pl.BlockSpec((tm, d), lambda i: (i, 0))
    vec = pl.BlockSpec((1, d), lambda i: (0, 0))
    wide = pl.BlockSpec((tm, ff), lambda i: (i, 0))
    w_spec = _resident((d, ff), lambda i: (0, 0))
    return pl.pallas_call(body, grid=(s // tm,), in_specs=[wide, wide, w_spec, w_spec, row, row, vec, row, vec],
                          out_specs=[row, row, vec, vec], out_shape=[SDS((s, d), f32), SDS((s, d), bf16), SDS((1, d), f32), SDS((1, d), f32)],
                          name=name, compiler_params=_cparams("arbitrary"))(d_g, d_u, w_g, w_u, hmid, dh_out, g_pre, y1, g_post)


def proj_in_grad(pairs, x, add, name, tm=512, dep=None, below=None):
    s, d = x.shape
    n = len(pairs)
    extra = [] if dep is None else [dep]
    n_below = 0 if below is None else 2

    def body(*refs):
        x_ref, add_ref = refs[3 * n], refs[3 * n + 1]
        below_refs = refs[3 * n + 2:3 * n + 2 + n_below]
        outs = refs[3 * n + 2 + n_below + len(extra):]

        @pl.when(pl.program_id(0) == 0)
        def _():
            for o in outs[1:1 + n] + outs[2 + n:]:
                o[...] = jnp.zeros_like(o)

        xv = x_ref[...]
        dx = add_ref[...]
        for i in range(n):
            a_ref, b_ref, g_ref = refs[3 * i:3 * i + 3]
            dxi, dgi = _rms_bwd_math(xv, g_ref[...], _dot_nt(a_ref[...], b_ref[...]))
            dx = dx + dxi
            outs[1 + i][...] += dgi
        outs[0][...] = dx
        if below is not None:
            dy, dg = _rms_bwd_math(below_refs[0][...], below_refs[1][...], dx)
            outs[1 + n][...] = dy.astype(bf16)
            outs[2 + n][...] += dg

    row = pl.BlockSpec((tm, d), lambda i: (i, 0))
    vec = pl.BlockSpec((1, d), lambda i: (0, 0))
    in_specs, args = [], []
    for a, b, g in pairs:
        k = a.shape[1]
        in_specs += [pl.BlockSpec((tm, k), lambda i: (i, 0)), _resident((d, k), lambda i: (0, 0)), vec]
        args += [a, b, g]
    in_specs += [row, row] + [row, vec][:n_below] + [pl.BlockSpec((8, LANES), lambda i: (0, 0))] * len(extra)
    out_specs = [row] + [vec] * n + [row, vec][:n_below]
    out_shape = [SDS((s, d), f32)] + [SDS((1, d), f32)] * n + [SDS((s, d), bf16), SDS((1, d), f32)][:n_below]
    out = pl.pallas_call(body, grid=(s // tm,), in_specs=in_specs, out_specs=out_specs, out_shape=out_shape, name=name,
                         compiler_params=_cparams("arbitrary"))(*args, x, add, *(below or ()), *extra)
    return (out[0], out[1:1 + n]) + tuple(out[1 + n:])


def ffn_up(f, wg, wu, name, tm=512, tc=256):
    s, d = f.shape
    ff = wg.shape[-1]

    def body(f_ref, wg_ref, wu_ref, fac_ref, act_ref):
        fv = f_ref[...]
        for j in range(ff // tc):
            lo = j * tc
            gg = _dot(fv, wg_ref[:, lo:lo + tc])
            uu = _dot(fv, wu_ref[:, lo:lo + tc])
            sg = _sigmoid(gg)
            silu = gg * sg
            fac_ref[:, lo:lo + tc] = (uu * (sg + silu * (1.0 - sg))).astype(bf16)
            fac_ref[:, ff + lo:ff + lo + tc] = silu.astype(bf16)
            act_ref[:, lo:lo + tc] = (silu * uu).astype(bf16)

    w_spec = _resident((d, ff), lambda i: (0, 0))
    return pl.pallas_call(body, grid=(s // tm,), in_specs=[pl.BlockSpec((tm, d), lambda i: (i, 0)), w_spec, w_spec],
                          out_specs=[pl.BlockSpec((tm, 2 * ff), lambda i: (i, 0)), pl.BlockSpec((tm, ff), lambda i: (i, 0))],
                          out_shape=[SDS((s, 2 * ff), bf16), SDS((s, ff), bf16)], name=name,
                          compiler_params=_cparams("parallel"))(f, wg, wu)


def _gmlp_forward_chunk(u, v, w_refs, bias, ln_g, ln_b):
    gu, tu = _gelu(u)
    gv, tv = _gelu(v)
    mu = jnp.sum(gv, axis=-1, keepdims=True) * (1.0 / MAIN_WIDTH)
    xc = gv - mu
    rstd = lax.rsqrt(jnp.sum(xc * xc, axis=-1, keepdims=True) * (1.0 / MAIN_WIDTH) + LN_EPS)
    xhat = xc * rstd
    vln = xhat * ln_g + ln_b
    row = lax.broadcasted_iota(jnp.int32, (CHUNK, CHUNK), 0)
    col = lax.broadcasted_iota(jnp.int32, (CHUNK, CHUNK), 1)
    s_parts = []
    for g in range(A_GROUPS):
        w = jnp.where(col <= row, w_refs[g], jnp.zeros((), bf16))
        s_parts.append(_dot(w, vln[:, g * CHUNK:(g + 1) * CHUNK].astype(bf16)) + bias[:, g:g + 1])
    return gu, tu, tv, rstd, xhat, vln, s_parts


def gmlp_fwd(proj, ws, bs_t, ln_g, ln_b, name, tm=512, out_width=MAIN_WIDTH):
    s = proj.shape[0]

    def body(u_ref, v_ref, w_ref, b_ref, g_ref, bb_ref, o_ref):
        bias = b_ref[...]
        for c in range(tm // CHUNK):
            rows = slice(c * CHUNK, (c + 1) * CHUNK)
            gu, _, _, _, _, _, s_parts = _gmlp_forward_chunk(u_ref[rows, :], v_ref[rows, :], w_ref, bias, g_ref[...], bb_ref[...])
            for g in range(A_GROUPS):
                cols = slice(g * CHUNK, (g + 1) * CHUNK)
                o_ref[rows, cols] = (gu[:, cols] * s_parts[g]).astype(bf16)

    vec = pl.BlockSpec((1, MAIN_WIDTH), lambda i: (0, 0))
    return pl.pallas_call(
        body, grid=(s // tm,),
        in_specs=[pl.BlockSpec((tm, MAIN_WIDTH), lambda i: (i, 0)), pl.BlockSpec((tm, MAIN_WIDTH), lambda i: (i, 1)),
                  pl.BlockSpec((A_GROUPS, CHUNK, CHUNK), lambda i: (0, 0, 0)), pl.BlockSpec((CHUNK, A_GROUPS), lambda i: (0, 0)), vec, vec],
        out_specs=pl.BlockSpec((tm, MAIN_WIDTH), lambda i: (i, 0)), out_shape=SDS((s, out_width), bf16), name=name,
        compiler_params=_cparams("parallel"))(proj, proj, ws, bs_t, ln_g, ln_b)


def gmlp_bwd(proj, d_mixed, ws, ws_t, bs_t, ln_g, ln_b, name, tm=512, out_width=2 * MAIN_WIDTH):
    s = proj.shape[0]

    def body(u_ref, v_ref, dm_ref, w_ref, wt_ref, b_ref, g_ref, bb_ref, duv_ref, dw_ref, db_ref, dg_ref, dbb_ref):
        @pl.when(pl.program_id(0) == 0)
        def _():
            dw_ref[...] = jnp.zeros_like(dw_ref)
            db_ref[...] = jnp.zeros_like(db_ref)
            dg_ref[...] = jnp.zeros_like(dg_ref)
            dbb_ref[...] = jnp.zeros_like(dbb_ref)

        bias = b_ref[...]
        ln_gv = g_ref[...]
        row = lax.broadcasted_iota(jnp.int32, (CHUNK, CHUNK), 0)
        col = lax.broadcasted_iota(jnp.int32, (CHUNK, CHUNK), 1)
        lane = lax.broadcasted_iota(jnp.int32, (CHUNK, LANES), 1)
        for c in range(tm // CHUNK):
            rows = slice(c * CHUNK, (c + 1) * CHUNK)
            u = u_ref[rows, :]
            v = v_ref[rows, :]
            gu, tu, tv, rstd, xhat, vln, s_parts = _gmlp_forward_chunk(u, v, w_ref, bias, ln_gv, bb_ref[...])
            dm = dm_ref[rows, :]
            d_vln_parts = []
            d_gu_parts = []
            db_acc = jnp.zeros((CHUNK, LANES), f32)
            for g in range(A_GROUPS):
                cols = slice(g * CHUNK, (g + 1) * CHUNK)
                dmg = dm[:, cols]
                d_gu_parts.append(dmg * s_parts[g])
                d_s = dmg * gu[:, cols]
                db_acc = db_acc + jnp.where(lane == g, jnp.sum(d_s, axis=-1, keepdims=True), 0.0)
                d_sb = d_s.astype(bf16)
                dw_ref[g] += jnp.where(col <= row, _dot_nt(d_sb, vln[:, cols].astype(bf16)), 0.0)
                wt = jnp.where(row <= col, wt_ref[g], jnp.zeros((), bf16))
                d_vln_parts.append(_dot(wt, d_sb))
            db_ref[...] += db_acc
            d_vln = jnp.concatenate(d_vln_parts, axis=-1)
            d_gu = jnp.concatenate(d_gu_parts, axis=-1)
            dg_ref[...] += jnp.sum(d_vln * xhat, axis=0, keepdims=True)
            dbb_ref[...] += jnp.sum(d_vln, axis=0, keepdims=True)
            dxh = d_vln * ln_gv
            m1 = jnp.sum(dxh, axis=-1, keepdims=True) * (1.0 / MAIN_WIDTH)
            m2 = jnp.sum(dxh * xhat, axis=-1, keepdims=True) * (1.0 / MAIN_WIDTH)
            d_gv = rstd * (dxh - m1 - xhat * m2)
            duv_ref[rows, :MAIN_WIDTH] = (d_gu * _gelu_grad(u, tu)).astype(bf16)
            duv_ref[rows, MAIN_WIDTH:] = (d_gv * _gelu_grad(v, tv)).astype(bf16)

    vec = pl.BlockSpec((1, MAIN_WIDTH), lambda i: (0, 0))
    wspec = pl.BlockSpec((A_GROUPS, CHUNK, CHUNK), lambda i: (0, 0, 0))
    return pl.pallas_call(
        body, grid=(s // tm,),
        in_specs=[pl.BlockSpec((tm, MAIN_WIDTH), lambda i: (i, 0)), pl.BlockSpec((tm, MAIN_WIDTH), lambda i: (i, 1)),
                  pl.BlockSpec((tm, MAIN_WIDTH), lambda i: (i, 0)), wspec, wspec, pl.BlockSpec((CHUNK, A_GROUPS), lambda i: (0, 0)), vec, vec],
        out_specs=[pl.BlockSpec((tm, 2 * MAIN_WIDTH), lambda i: (i, 0)), wspec, pl.BlockSpec((CHUNK, LANES), lambda i: (0, 0)), vec, vec],
        out_shape=[SDS((s, out_width), bf16), SDS((A_GROUPS, CHUNK, CHUNK), f32), SDS((CHUNK, LANES), f32),
                   SDS((1, MAIN_WIDTH), f32), SDS((1, MAIN_WIDTH), f32)],
        name=name, compiler_params=_cparams("arbitrary"))(proj, proj, d_mixed, ws, ws_t, bs_t, ln_g, ln_b)


def _head_mask(width, h):
    lane = lax.broadcasted_iota(jnp.int32, (1, width), 1)
    return (lane >= h * HEAD_DIM) & (lane < (h + 1) * HEAD_DIM)


def mem_attn_fwd(proj, q_block, kv, into, name, tm=512):
    s = proj.shape[0]
    n_mem = kv.shape[0]
    out_block = into.shape[1] // MEM_WIDTH - 1

    def body(q_ref, kv_ref, into_ref, o_ref):
        q = q_ref[...].astype(f32)
        k = kv_ref[:, :MEM_WIDTH].astype(bf16)
        v = kv_ref[:, MEM_WIDTH:].astype(bf16)
        out = jnp.zeros((tm, MEM_WIDTH), f32)
        for h in range(MEM_HEADS):
            msk = _head_mask(MEM_WIDTH, h)
            qh = jnp.where(msk, q, 0.0).astype(bf16)
            sc = _dot_nt(qh, k) * ATT_SCALE
            e = jnp.exp(sc - jnp.max(sc, axis=-1, keepdims=True))
            p = e / jnp.sum(e, axis=-1, keepdims=True)
            out = jnp.where(msk, _dot(p.astype(bf16), v), out)
        o_ref[...] = out.astype(bf16)

    return pl.pallas_call(body, grid=(s // tm,),
                          in_specs=[pl.BlockSpec((tm, MEM_WIDTH), lambda i: (i, q_block)), pl.BlockSpec((n_mem, 2 * MEM_WIDTH), lambda i: (0, 0)), _ANY],
                          out_specs=pl.BlockSpec((tm, MEM_WIDTH), lambda i: (i, out_block)), out_shape=SDS(into.shape, bf16), name=name,
                          input_output_aliases={2: 0}, compiler_params=_cparams("parallel"))(proj, kv, into)


def mem_attn_bwd(proj, q_block, kv, d_mixed, into, name, tm=512):
    s = proj.shape[0]
    n_mem = kv.shape[0]
    out_block = into.shape[1] // MEM_WIDTH - 1

    def body(q_ref, kv_ref, do_ref, into_ref, dq_ref, dkv_ref):
        @pl.when(pl.program_id(0) == 0)
        def _():
            dkv_ref[...] = jnp.zeros_like(dkv_ref)

        q = q_ref[...].astype(f32)
        do = do_ref[...]
        k = kv_ref[:, :MEM_WIDTH].astype(bf16)
        v = kv_ref[:, MEM_WIDTH:].astype(bf16)
        dq = jnp.zeros((tm, MEM_WIDTH), f32)
        dk = jnp.zeros((n_mem, MEM_WIDTH), f32)
        dv = jnp.zeros((n_mem, MEM_WIDTH), f32)
        for h in range(MEM_HEADS):
            msk = _head_mask(MEM_WIDTH, h)
            qh = jnp.where(msk, q, 0.0).astype(bf16)
            doh = jnp.where(msk, do, 0.0).astype(bf16)
            sc = _dot_nt(qh, k) * ATT_SCALE
            e = jnp.exp(sc - jnp.max(sc, axis=-1, keepdims=True))
            p = e / jnp.sum(e, axis=-1, keepdims=True)
            dp = _dot_nt(doh, v)
            ds = p * (dp - jnp.sum(dp * p, axis=-1, keepdims=True))
            dsb = (ds * ATT_SCALE).astype(bf16)
            dq = jnp.where(msk, _dot(dsb, k), dq)
            dk = dk + _dot_tn(dsb, qh)
            dv = dv + _dot_tn(p.astype(bf16), doh)
        dq_ref[...] = dq.astype(bf16)
        dkv_ref[:, :MEM_WIDTH] += dk
        dkv_ref[:, MEM_WIDTH:] += dv

    return pl.pallas_call(
        body, grid=(s // tm,),
        in_specs=[pl.BlockSpec((tm, MEM_WIDTH), lambda i: (i, q_block)), pl.BlockSpec((n_mem, 2 * MEM_WIDTH), lambda i: (0, 0)),
                  pl.BlockSpec((tm, MEM_WIDTH), lambda i: (i, MAIN_WIDTH // MEM_WIDTH)), _ANY],
        out_specs=[pl.BlockSpec((tm, MEM_WIDTH), lambda i: (i, out_block)), pl.BlockSpec((n_mem, 2 * MEM_WIDTH), lambda i: (0, 0))],
        out_shape=[SDS(into.shape, bf16), SDS((n_mem, 2 * MEM_WIDTH), f32)], name=name,
        input_output_aliases={3: 0}, compiler_params=_cparams("arbitrary"))(proj, kv, d_mixed, into)


def _tri(t, upper):
    r = lax.broadcasted_iota(jnp.int32, (t, t), 0)
    c = lax.broadcasted_iota(jnp.int32, (t, t), 1)
    return ((r <= c) if upper else (r >= c)).astype(f32)


def fgate_fwd(z_t, b, name, t=512):
    hh, s = z_t.shape

    def body(z_ref, b_ref, c_ref):
        u = _tri(t, True)
        carry = jnp.zeros((hh, 1), f32)
        for blk in range(s // t):
            x = z_ref[:, blk * t:(blk + 1) * t] + b_ref[...]
            logf = jnp.minimum(x, 0.0) - jnp.log(1.0 + jnp.exp(-jnp.abs(x)))
            y = jnp.dot(logf, u, precision=lax.Precision.HIGHEST, preferred_element_type=f32) + carry
            c_ref[:, blk * t:(blk + 1) * t] = y
            carry = y[:, t - 1:t]

    return pl.pallas_call(body, out_shape=SDS((hh, s), f32), name=name, compiler_params=_cparams())(z_t, b)


def fgate_bwd(dc_t, z_t, b, name, t=512):
    hh, s = z_t.shape

    def body(dc_ref, z_ref, b_ref, dz_ref, db_ref):
        low = _tri(t, False)
        carry = jnp.zeros((hh, 1), f32)
        total = jnp.zeros((hh, 1), f32)
        for blk in reversed(range(s // t)):
            cols = slice(blk * t, (blk + 1) * t)
            y = jnp.dot(dc_ref[:, cols], low, precision=lax.Precision.HIGHEST, preferred_element_type=f32) + carry
            carry = y[:, 0:1]
            dz = y * _sigmoid(-(z_ref[:, cols] + b_ref[...]))
            dz_ref[:, cols] = dz
            total = total + jnp.sum(dz, axis=-1, keepdims=True)
        db_ref[...] = jnp.broadcast_to(total, db_ref.shape)

    return pl.pallas_call(body, out_shape=[SDS((hh, s), f32), SDS((hh, LANES), f32)], name=name,
                          compiler_params=_cparams())(dc_t, z_t, b)


def _pair_masks():
    lane = lax.broadcasted_iota(jnp.int32, (1, LANES), 1)
    return [lane < HEAD_DIM, lane >= HEAD_DIM]


def _tile_base(cr_ref, hh, lo):
    return cr_ref[hh:hh + 1, pl.ds(lo, LANES)][:, 0:1]


def fox_fwd(q, kv, c_row, name, tq=512, out_width=MAIN_WIDTH):
    s = kv.shape[0]
    nq = s // tq

    def body(q_ref, k_ref, v_ref, cr_ref, o_ref, lse_ref, ob_ref):
        i = pl.program_id(1)
        qv = q_ref[...]
        masks = _pair_masks()
        row = lax.broadcasted_iota(jnp.int32, (tq, tq), 0)
        col = lax.broadcasted_iota(jnp.int32, (tq, tq), 1)
        qh = [jnp.where(masks[hh], qv, jnp.zeros((), bf16)) * ATT_SCALE for hh in range(2)]
        ct = [_tile_base(cr_ref, hh, pl.multiple_of(i * tq, tq)) for hh in range(2)]

        def block(j, carry, diag):
            lo = pl.multiple_of(j * tq, tq)
            ks = k_ref[pl.ds(lo, tq), :]
            vs = v_ref[pl.ds(lo, tq), :]
            out = []
            for hh in range(2):
                m, l, acc = carry[hh]
                sc = _dot_nt(qh[hh], ks) + (ct[hh] - cr_ref[hh:hh + 1, pl.ds(lo, tq)])
                if diag:
                    sc = jnp.where(col <= row, sc, -jnp.inf)
                m_new = jnp.maximum(m, jnp.max(sc, axis=-1, keepdims=True))
                alpha = jnp.exp(m - m_new)
                p = jnp.exp(sc - m_new)
                l = alpha * l + jnp.sum(p, axis=-1, keepdims=True)
                p_hi = p.astype(bf16)
                p_lo = (p - p_hi.astype(f32)).astype(bf16)
                acc = alpha * acc + (_dot(p_hi, vs) + _dot(p_lo, vs))
                out.append((m_new, l, acc))
            return tuple(out)

        init = (jnp.full((tq, 1), -jnp.inf, f32), jnp.zeros((tq, 1), f32), jnp.zeros((tq, LANES), f32))
        carry = lax.fori_loop(0, i, functools.partial(block, diag=False), (init, init))
        res = [(acc / l, m + jnp.log(l)) for m, l, acc in block(i, carry, True)]
        out = jnp.where(masks[0], res[0][0], res[1][0])
        o_ref[...] = out
        ob_ref[...] = out.astype(bf16)
        lse_ref[...] = jnp.where(masks[0], res[0][1], res[1][1])

    return pl.pallas_call(
        body, grid=(FOX_PAIRS, nq),
        in_specs=[pl.BlockSpec((tq, LANES), lambda p, i: (i, p)), pl.BlockSpec((s, LANES), lambda p, i: (0, p)),
                  pl.BlockSpec((s, LANES), lambda p, i: (0, FOX_PAIRS + p)), pl.BlockSpec((None, 2, s), lambda p, i: (p, 0, 0))],
        out_specs=[pl.BlockSpec((tq, LANES), lambda p, i: (i, p)), pl.BlockSpec((None, tq, LANES), lambda p, i: (p, i, 0)),
                   pl.BlockSpec((tq, LANES), lambda p, i: (i, p))],
        out_shape=[SDS((s, MAIN_WIDTH), f32), SDS((FOX_PAIRS, s, LANES), f32), SDS((s, out_width), bf16)], name=name,
        compiler_params=_cparams("parallel", "parallel"))(q, kv, kv, c_row)


def fox_bwd(q, kv, d_mixed, o, lse, c_row, name, tq=512, dq_width=MAIN_WIDTH):
    s = kv.shape[0]
    nq = s // tq

    def body(q_ref, k_ref, v_ref, do_ref, o_ref, lse_ref, cr_ref, dqb_ref, dk_ref, dv_ref, dc_ref, dq_ref):
        j = pl.program_id(1)

        @pl.when(j == 0)
        def _():
            dq_ref[...] = jnp.zeros_like(dq_ref)

        masks = _pair_masks()
        sub = lax.broadcasted_iota(jnp.int32, (LANES, 1), 0)
        sub_masks = [sub < HEAD_DIM, sub >= HEAD_DIM]
        row = lax.broadcasted_iota(jnp.int32, (tq, tq), 0)
        col = lax.broadcasted_iota(jnp.int32, (tq, tq), 1)
        kj = k_ref[...]
        vj = v_ref[...]
        lo_j = pl.multiple_of(j * tq, tq)

        def block(i, carry, diag):
            dk_t, dv_t, dc0, dc1 = carry
            dcs = [dc0, dc1]
            lo = pl.multiple_of(i * tq, tq)
            qi = q_ref[pl.ds(lo, tq), :]
            qi = qi * ATT_SCALE
            qt_i = qi.T
            doi = do_ref[pl.ds(lo, tq), :]
            dot_i = doi.astype(bf16).T
            prod = doi.astype(bf16).astype(f32) * o_ref[pl.ds(lo, tq), :]
            lse_i = lse_ref[pl.ds(lo, tq), :]
            dq_i = jnp.zeros((tq, LANES), f32)
            for hh in range(2):
                qh = jnp.where(masks[hh], qi, jnp.zeros((), bf16))
                doh = jnp.where(masks[hh], doi, 0.0).astype(bf16)
                delta = jnp.sum(jnp.where(masks[hh], prod, 0.0), axis=-1, keepdims=True)
                sc = _dot_nt(qh, kj) + (_tile_base(cr_ref, hh, lo) - cr_ref[hh:hh + 1, pl.ds(lo_j, tq)])
                p = jnp.exp(sc - lse_i[:, hh * HEAD_DIM:hh * HEAD_DIM + 1])
                if diag:
                    p = jnp.where(col <= row, p, 0.0)
                dv_t = dv_t + _dot(jnp.where(sub_masks[hh], dot_i, jnp.zeros((), bf16)), p.astype(bf16))
                ds = p * (_dot_nt(doh, vj) - delta)
                dcs[hh] = dcs[hh] + jnp.sum(ds, axis=0, keepdims=True)
                dsb = ds.astype(bf16)
                dq_i = jnp.where(masks[hh], _dot(dsb, kj), dq_i)
                dk_t = dk_t + _dot(jnp.where(sub_masks[hh], qt_i, jnp.zeros((), bf16)), dsb)
            dq_ref[pl.ds(lo, tq), :] += dq_i * ATT_SCALE
            return dk_t, dv_t, dcs[0], dcs[1]

        zero = jnp.zeros((LANES, tq), f32)
        zrow = jnp.zeros((1, tq), f32)
        carry = block(j, (zero, zero, zrow, zrow), True)
        dk_t, dv_t, dc0, dc1 = lax.fori_loop(j + 1, nq, functools.partial(block, diag=False), carry)
        dk_ref[...] = dk_t.T.astype(bf16)
        dv_ref[...] = dv_t.T.astype(bf16)
        dc_ref[0:1, :] = -dc0
        dc_ref[1:2, :] = -dc1

        @pl.when(j == nq - 1)
        def _():
            dqb_ref[...] = dq_ref[...].astype(bf16)

    full = lambda p, j: (0, p)
    tile = lambda p, j: (j, p)
    return pl.pallas_call(
        body, grid=(FOX_PAIRS, nq),
        in_specs=[pl.BlockSpec((s, LANES), full), pl.BlockSpec((tq, LANES), tile), pl.BlockSpec((tq, LANES), lambda p, j: (j, FOX_PAIRS + p)),
                  pl.BlockSpec((s, LANES), full), pl.BlockSpec((s, LANES), full), pl.BlockSpec((None, s, LANES), lambda p, j: (p, 0, 0)),
                  pl.BlockSpec((None, 2, s), lambda p, j: (p, 0, 0))],
        out_specs=[pl.BlockSpec((s, LANES), full), pl.BlockSpec((tq, LANES), tile), pl.BlockSpec((tq, LANES), tile),
                   pl.BlockSpec((None, 2, tq), lambda p, j: (p, 0, j))],
        out_shape=[SDS((s, dq_width), bf16), SDS((s, MAIN_WIDTH), bf16), SDS((s, MAIN_WIDTH), bf16), SDS((FOX_PAIRS, 2, s), f32)],
        scratch_shapes=[pltpu.VMEM((s, LANES), f32)],
        name=name, compiler_params=_cparams("parallel", "arbitrary"))(q, kv, kv, d_mixed, o, lse, c_row)


def adamw(w, g, m, v, name, tr=256):
    r, c = w.shape
    tr = min(tr, r)
    assert r % tr == 0, (name, r, tr)
    c1 = 1.0 / (1.0 - ADAM_B1 ** ADAM_STEP)
    c2 = 1.0 / (1.0 - ADAM_B2 ** ADAM_STEP)

    def body(w_ref, g_ref, m_ref, v_ref, d_ref, mo_ref, vo_ref):
        gv = g_ref[...]
        mn = ADAM_B1 * m_ref[...] + (1.0 - ADAM_B1) * gv
        vn = ADAM_B2 * v_ref[...] + (1.0 - ADAM_B2) * gv * gv
        mo_ref[...] = mn
        vo_ref[...] = vn
        d_ref[...] = -ADAM_LR * ((mn * c1) / (jnp.sqrt(vn * c2) + ADAM_EPS) + ADAM_WD * w_ref[...])

    spec = pl.BlockSpec((tr, c), lambda i: (i, 0))
    return pl.pallas_call(body, grid=(r // tr,), in_specs=[spec] * 4, out_specs=[spec] * 3, out_shape=[SDS((r, c), f32)] * 3,
                          name=name, compiler_params=_cparams("parallel"))(w, g, m, v)


def adamw_owned(w, parts, m, v, name, tr):
    nl, r, c = w.shape
    cp = parts[0].shape[2]
    assert r % tr == 0 and len(parts) == nl, (name, r, tr)
    c1 = 1.0 / (1.0 - ADAM_B1 ** ADAM_STEP)
    c2 = 1.0 / (1.0 - ADAM_B2 ** ADAM_STEP)

    def body(*refs):
        w_ref, p_refs, (m_ref, v_ref) = refs[0], refs[1:1 + nl], refs[1 + nl:3 + nl]
        g_ref, d_ref, mo_ref, vo_ref = refs[3 + nl:]
        layer = pl.program_id(0)

        def total(p_ref):
            acc = p_ref[0].astype(f32)
            for k in range(1, N_DEV):
                acc = acc + p_ref[k].astype(f32)
            return acc

        gv = total(p_refs[0])
        for l in range(1, nl):
            gv = jnp.where(layer == l, total(p_refs[l]), gv)
        gv = gv[:, :c]
        g_ref[...] = gv
        mn = ADAM_B1 * m_ref[...] + (1.0 - ADAM_B1) * gv
        vn = ADAM_B2 * v_ref[...] + (1.0 - ADAM_B2) * gv * gv
        mo_ref[...] = mn
        vo_ref[...] = vn
        d_ref[...] = -ADAM_LR * ((mn * c1) / (jnp.sqrt(vn * c2) + ADAM_EPS) + ADAM_WD * w_ref[...])

    spec = pl.BlockSpec((None, tr, c), lambda l, i: (l, i, 0))
    last = r // tr - 1

    def part_spec(mine):
        return pl.BlockSpec((N_DEV, tr, cp), lambda l, i: (0, jnp.where(l == mine, i, jnp.where(l < mine, 0, last)), 0))

    return pl.pallas_call(body, grid=(nl, r // tr), in_specs=[spec] + [part_spec(l) for l in range(nl)] + [spec, spec], out_specs=[spec] * 4,
                          out_shape=[SDS((nl, r, c), f32)] * 4, name=name,
                          compiler_params=_cparams("parallel", "parallel"))(w, *parts, m, v)


def sum_leading(x, name, out_dtype=f32, tr=None):
    n, r, c = x.shape
    tr = tr or r
    assert r % tr == 0

    def body(x_ref, o_ref):
        acc = x_ref[0].astype(f32)
        for k in range(1, n):
            acc = acc + x_ref[k].astype(f32)
        o_ref[...] = acc.astype(out_dtype)

    return pl.pallas_call(body, grid=(r // tr,), in_specs=[pl.BlockSpec((n, tr, c), lambda i: (0, i, 0))],
                          out_specs=pl.BlockSpec((tr, c), lambda i: (i, 0)), out_shape=SDS((r, c), out_dtype), name=name,
                          compiler_params=_cparams("parallel"))(x)


_ANY = pl.BlockSpec(memory_space=pl.ANY)
_DMA = pltpu.SemaphoreType.DMA


_HBM = pl.BlockSpec(memory_space=pltpu.HBM)
_SEM = pl.BlockSpec(memory_space=pltpu.SEMAPHORE)
_EFFECT = pltpu.SideEffectType.DATAFLOW_SIDE_EFFECTING
_FLIPS = [(0, 0, 1), (1, 0, 0), (0, 1, 0), (1, 1, 0), (1, 0, 1), (0, 1, 1), (1, 1, 1)]


def _me():
    return lax.axis_index("x"), lax.axis_index("y"), lax.axis_index("c")


def _peers():
    mx, my, mc = _me()
    return [(jnp.bitwise_xor(mx, fx), jnp.bitwise_xor(my, fy), jnp.bitwise_xor(mc, fc)) for fx, fy, fc in _FLIPS]


def _index(dev):
    return 4 * dev[0] + 2 * dev[1] + dev[2]


def _win(ref, axis, k, size, count=1):
    idx = [slice(None)] * len(ref.shape)
    idx[axis] = pl.ds(k * size, count * size)
    return ref.at[tuple(idx)]


def _hbm(a):
    return pltpu.with_memory_space_constraint(a, pltpu.HBM)


def _exchange_start(srcs, lands, copies_of, name):
    n = len(srcs)

    def body(*refs):
        src = refs[:n]
        send_sems, recv_sems, self_sems = refs[2 * n:2 * n + 3]
        land = refs[3 * n + 3:4 * n + 3]
        token = refs[4 * n + 3]
        me = _index(_me())
        for a in range(n):
            for s_ref, d_ref, peer in copies_of(a, src[a], land[a], me):
                if peer is None:
                    pltpu.make_async_copy(s_ref, d_ref, self_sems.at[a]).start()
                else:
                    pltpu.make_async_remote_copy(src_ref=s_ref, dst_ref=d_ref, send_sem=send_sems.at[a], recv_sem=recv_sems.at[a],
                                                 device_id=peer, device_id_type=MESH).start()
        token[...] = jnp.zeros_like(token)

    outs = pl.pallas_call(
        body, name=name,
        out_shape=(_DMA((n,)), _DMA((n,)), _DMA((n,)), *[pltpu.HBM(s.shape, s.dtype) for s in srcs],
                   *[pltpu.HBM(l.shape, l.dtype) for l in lands], SDS((8, LANES), f32)),
        in_specs=[_HBM] * (2 * n), out_specs=(_SEM, _SEM, _SEM, *[_HBM] * (2 * n), pl.BlockSpec(memory_space=pltpu.VMEM)),
        input_output_aliases={i: 3 + i for i in range(2 * n)},
        compiler_params=pltpu.CompilerParams(has_side_effects=_EFFECT),
    )(*[_hbm(s) for s in srcs], *[_hbm(lax.empty(l.shape, l.dtype)) for l in lands])
    return dict(sems=outs[:3], srcs=list(outs[3:3 + n]), lands=list(outs[3 + n:3 + 2 * n]), token=outs[3 + 2 * n])


def _exchange_wait(started, waits_of, after, name, which=None):
    which = list(range(len(started["srcs"]))) if which is None else which
    srcs, lands = [started["srcs"][a] for a in which], [started["lands"][a] for a in which]
    n = len(which)

    def body(*refs):
        src = refs[:n]
        land = refs[n:2 * n]
        send_sems, recv_sems, self_sems = refs[2 * n:2 * n + 3]
        me = _index(_me())
        for pos, a in enumerate(which):
            seven, (s_ref, d_ref) = waits_of(a, src[pos], land[pos], me)
            both = pltpu.make_async_remote_copy(src_ref=seven, dst_ref=seven, send_sem=send_sems.at[a], recv_sem=recv_sems.at[a],
                                                device_id=_me(), device_id_type=MESH)
            both.wait_send()
            both.wait_recv()
            pltpu.make_async_copy(s_ref, d_ref, self_sems.at[a]).wait()

    outs = pl.pallas_call(
        body, name=name, out_shape=tuple(pltpu.HBM(t.shape, t.dtype) for t in srcs + lands),
        in_specs=[_HBM] * (2 * n) + [_SEM] * 3 + [_ANY], out_specs=tuple([_HBM] * (2 * n)),
        input_output_aliases={i: i for i in range(2 * n)},
        compiler_params=pltpu.CompilerParams(has_side_effects=_EFFECT),
    )(*srcs, *lands, *started["sems"], after)
    return list(outs[n:])


def gather_start(locs, axes, name):
    lands = [SDS(tuple(N_DEV * d if i == ax else d for i, d in enumerate(l.shape)), l.dtype) for l, ax in zip(locs, axes)]

    def copies_of(a, src, land, me):
        mine = _win(land, axes[a], me, src.shape[axes[a]])
        return [(src, mine, peer) for peer in _peers()] + [(src, mine, None)]

    return _exchange_start(locs, lands, copies_of, name)


def gather_wait(started, axes, after, name, which=None):
    def waits_of(a, src, land, me):
        size = src.shape[axes[a]]
        return _win(land, axes[a], 0, size, N_DEV - 1), (src, _win(land, axes[a], me, size))

    return _exchange_wait(started, waits_of, after, name, which)


def scatter_start(grads, axes, name):
    lands = [SDS((N_DEV,) + tuple(d // N_DEV if i == ax else d for i, d in enumerate(g.shape)), g.dtype) for g, ax in zip(grads, axes)]

    def copies_of(a, src, land, me):
        size = src.shape[axes[a]] // N_DEV
        out = [(_win(src, axes[a], _index(peer), size), land.at[me], peer) for peer in _peers()]
        return out + [(_win(src, axes[a], me, size), land.at[me], None)]

    return _exchange_start(grads, lands, copies_of, name)


def scatter_wait(started, axes, after, name):
    def waits_of(a, src, land, me):
        size = src.shape[axes[a]] // N_DEV
        return land.at[pl.ds(0, N_DEV - 1)], (_win(src, axes[a], me, size), land.at[me])

    return _exchange_wait(started, waits_of, after, name)


def _row_tile(rows, cap=512):
    return max(t for t in range(8, min(rows, cap) + 1, 8) if rows % t == 0)


_SMALL = [
    ("ln_mix_pre", (2, 1024)), ("ln_mix_post", (2, 1024)), ("ln_ffn_pre", (2, 1024)), ("ln_ffn_post", (2, 1024)),
    ("ln_mem", (2, 1024)), ("w_spatial", (1, 6, 128, 128)), ("b_spatial", (1, 6, 128)), ("ln_shared", (1024,)),
    ("b_forget", (12,)), ("ln_v_g", (1, 768)), ("ln_v_b", (1, 768)),
]
_SMALL_TILE = 8 * LANES


def _small_rows(shape):
    return -(-math.prod(shape) // _SMALL_TILE) * 8


def _pack_small(vals, shapes):
    parts = []
    for name, shape in shapes:
        flat = vals[name].reshape(-1).astype(f32)
        rows = _small_rows(shape)
        parts.append(jnp.pad(flat, (0, rows * LANES - flat.shape[0])).reshape(rows, LANES))
    return jnp.concatenate(parts, axis=0)


def _unpack_small(buf, shapes):
    out = {}
    lo = 0
    for name, shape in shapes:
        rows = _small_rows(shape)
        out[name] = buf[lo:lo + rows].reshape(-1)[:math.prod(shape)].reshape(shape)
        lo += rows
    return out


def kernel(x, mem, ln_mix_pre, ln_mix_post, ln_ffn_pre, ln_ffn_post, ln_mem, w_mem_kv, w_out, w_ffn_gate, w_ffn_up, w_ffn_down, w_in_a, w_spatial, b_spatial, ln_v_g, ln_v_b, ln_shared, w_shared_kv, b_forget, w_in_b, loss_target, m_ln_mix_pre, m_ln_mix_post, m_ln_ffn_pre, m_ln_ffn_post, m_ln_mem, m_w_mem_kv, m_w_out, m_w_ffn_gate, m_w_ffn_up, m_w_ffn_down, m_w_in_a, m_w_spatial, m_b_spatial, m_ln_v_g, m_ln_v_b, m_ln_shared, m_w_shared_kv, m_b_forget, m_w_in_b, v_ln_mix_pre, v_ln_mix_post, v_ln_ffn_pre, v_ln_ffn_post, v_ln_mem, v_w_mem_kv, v_w_out, v_w_ffn_gate, v_w_ffn_up, v_w_ffn_down, v_w_in_a, v_w_spatial, v_b_spatial, v_ln_v_g, v_ln_v_b, v_ln_shared, v_w_shared_kv, v_b_forget, v_w_in_b):
    weights = dict(ln_mix_pre=ln_mix_pre, ln_mix_post=ln_mix_post, ln_ffn_pre=ln_ffn_pre, ln_ffn_post=ln_ffn_post, ln_mem=ln_mem,
                   w_mem_kv=w_mem_kv, w_out=w_out, w_ffn_gate=w_ffn_gate, w_ffn_up=w_ffn_up, w_ffn_down=w_ffn_down, w_in_a=w_in_a,
                   w_spatial=w_spatial, b_spatial=b_spatial, ln_v_g=ln_v_g, ln_v_b=ln_v_b, ln_shared=ln_shared,
                   w_shared_kv=w_shared_kv, b_forget=b_forget, w_in_b=w_in_b)
    mom_m = dict(ln_mix_pre=m_ln_mix_pre, ln_mix_post=m_ln_mix_post, ln_ffn_pre=m_ln_ffn_pre, ln_ffn_post=m_ln_ffn_post, ln_mem=m_ln_mem,
                 w_mem_kv=m_w_mem_kv, w_out=m_w_out, w_ffn_gate=m_w_ffn_gate, w_ffn_up=m_w_ffn_up, w_ffn_down=m_w_ffn_down, w_in_a=m_w_in_a,
                 w_spatial=m_w_spatial, b_spatial=m_b_spatial, ln_v_g=m_ln_v_g, ln_v_b=m_ln_v_b, ln_shared=m_ln_shared,
                 w_shared_kv=m_w_shared_kv, b_forget=m_b_forget, w_in_b=m_w_in_b)
    mom_v = dict(ln_mix_pre=v_ln_mix_pre, ln_mix_post=v_ln_mix_post, ln_ffn_pre=v_ln_ffn_pre, ln_ffn_post=v_ln_ffn_post, ln_mem=v_ln_mem,
                 w_mem_kv=v_w_mem_kv, w_out=v_w_out, w_ffn_gate=v_w_ffn_gate, w_ffn_up=v_w_ffn_up, w_ffn_down=v_w_ffn_down, w_in_a=v_w_in_a,
                 w_spatial=v_w_spatial, b_spatial=v_b_spatial, ln_v_g=v_ln_v_g, ln_v_b=v_ln_v_b, ln_shared=v_ln_shared,
                 w_shared_kv=v_w_shared_kv, b_forget=v_b_forget, w_in_b=v_w_in_b)
    names = list(weights)
    mx, my, mc = lax.axis_index("x"), lax.axis_index("y"), lax.axis_index("c")
    me = 4 * mx + 2 * my + mc

    h0 = x[0]
    mem0 = mem[0]
    tgt = loss_target[0]
    seq = h0.shape[0]

    vec = lambda a: a.reshape(1, -1)
    pad_to = lambda a, axis, size: jnp.pad(a, [(0, size - a.shape[i] if i == axis else 0) for i in range(a.ndim)])

    def after(tok, a):
        return a + tok[0, 0].astype(a.dtype)

    lnv_loc = pad_to(jnp.concatenate([ln_v_g, ln_v_b], axis=0), 0, 8)
    st_a = gather_start([w_in_a.astype(bf16), pad_to(lnv_loc, 1, LANES)[None]], [0, 0], "gather_a_start")
    mix_locs = lambda l, tok: [after(tok, w_mem_kv[l]).astype(bf16), w_out[l].astype(bf16)]

    def ffn_gather_start(l, tok):
        gate_up = gather_start([pad_to(after(tok, w_ffn_gate[l]).astype(bf16), 1, FF_SHARD_PAD),
                                pad_to(w_ffn_up[l].astype(bf16), 1, FF_SHARD_PAD)], [1, 1], f"gather_gate_up{l}_start")
        down = gather_start([pad_to(after(gate_up["token"], w_ffn_down[l]).astype(bf16), 0, FF_SHARD_PAD)], [0], f"gather_down{l}_start")
        return gate_up, down

    st_b = [gather_start(mix_locs(0, st_a["token"]), [0, 0], "gather_b0_start"), None]
    st_c = ffn_gather_start(0, st_b[0]["token"])
    st_d = gather_start([after(st_c[1]["token"], w_in_b[0]).astype(bf16), pad_to(w_shared_kv.astype(bf16), 1, KV_PAD)], [0, 0],
                        "gather_d_start")
    st_b[1] = gather_start(mix_locs(1, st_d["token"]), [0, 0], "gather_b1_start")
    st_e = ffn_gather_start(1, st_b[1]["token"])
    ws = w_spatial[0].astype(bf16)
    ws_t = ws.transpose(0, 2, 1)
    bs_t = b_spatial[0].T

    (a0,) = rms_fwd(h0, [after(st_e[1]["token"], vec(ln_mix_pre[0]))], "a0_norm")
    w_in_a8, lnv8 = gather_wait(st_a, [0, 0], a0, "gather_a_wait")
    w_in_a_full = w_in_a8.transpose(1, 0, 2).reshape(D_MODEL, -1)
    lnv_g = lnv8[:, 0, :MAIN_WIDTH // N_DEV].reshape(1, MAIN_WIDTH)
    lnv_b = lnv8[:, 1, :MAIN_WIDTH // N_DEV].reshape(1, MAIN_WIDTH)
    proj0 = mm(a0, w_in_a_full, "proj0", tn=896)
    main0 = gmlp_fwd(proj0, ws, bs_t, lnv_g, lnv_b, "gmlp_fwd", out_width=D_MODEL)
    w_mkv, w_o = [None, None], [None, None]
    w_mkv[0], w_o[0] = gather_wait(st_b[0], [0, 0], main0, "gather_b0_wait")
    (memn0,) = rms_fwd(mem0, [vec(ln_mem[0])], "mem0_norm")
    kvm0 = mm(memn0, w_mkv[0], "kvm0")
    mixed0 = mem_attn_fwd(proj0, 2 * MAIN_WIDTH // MEM_WIDTH, kvm0, main0, "mem_attn0")
    y1_0, hmid0, f0 = mm_resnorm(mixed0, w_o[0], h0, vec(ln_mix_post[0]), [vec(ln_ffn_pre[0])], "mix_out0")
    w_g0, w_u0 = gather_wait(st_c[0], [1, 1], f0, "gather_gate_up0_wait")
    gu0, act0 = ffn_up(f0, w_g0, w_u0, "ffn_up0")
    (w_d0,) = gather_wait(st_c[1], [0], act0, "gather_down0_wait")
    y2_0, h1, a1, sin1 = mm_resnorm(act0, w_d0, hmid0, vec(ln_ffn_post[0]), [vec(ln_mix_pre[1]), vec(ln_shared)], "ffn_down0")

    w_inb, w_kv = gather_wait(st_d, [0, 0], sin1, "gather_d_wait")
    kvb = mm(sin1, w_kv, "kv_shared", out_dtype=bf16, tn=MAIN_WIDTH, ncols=2 * MAIN_WIDTH)
    zf = mm(sin1, w_kv, "forget_logits", tn=256, col0=2 * MAIN_WIDTH, ncols=256)
    qb = mm(a1, w_inb, "proj1", out_dtype=bf16)
    z_t = jnp.pad(zf[:, :FOX_HEADS].T, ((0, 16 - FOX_HEADS), (0, 0)))
    bf_col = jnp.pad(b_forget, (0, 16 - FOX_HEADS)).reshape(16, 1)
    c_t = fgate_fwd(z_t, bf_col, "fgate_fwd")
    c_row = c_t[:FOX_HEADS].reshape(FOX_PAIRS, 2, seq)
    main1, lse, main1_b = fox_fwd(qb, kvb, c_row, "fox_fwd", out_width=D_MODEL)
    w_mkv[1], w_o[1] = gather_wait(st_b[1], [0, 0], main1, "gather_b1_wait")
    (memn1,) = rms_fwd(mem0, [vec(ln_mem[1])], "mem1_norm")
    kvm1 = mm(memn1, w_mkv[1], "kvm1")
    mixed1 = mem_attn_fwd(qb, MAIN_WIDTH // MEM_WIDTH, kvm1, main1_b, "mem_attn1")
    y1_1, hmid1, f1 = mm_resnorm(mixed1, w_o[1], h1, vec(ln_mix_post[1]), [vec(ln_ffn_pre[1])], "mix_out1")
    w_g1, w_u1 = gather_wait(st_e[0], [1, 1], f1, "gather_gate_up1_wait")
    gu1, act1 = ffn_up(f1, w_g1, w_u1, "ffn_up1")
    (w_d1,) = gather_wait(st_e[1], [0], act1, "gather_down1_wait")
    dh, d_y2_1, dg_fpost1, loss_tile = mm_resnorm_loss(act1, w_d1, hmid1, vec(ln_ffn_post[1]), tgt, "ffn_down1_loss")
    ffn_w = [(w_g0, w_u0, w_d0), (w_g1, w_u1, w_d1)]

    small = {}

    def ffn_backward(layer, dh_out, d_y2, hmid, f, gu, act, y1):
        w_g, w_u, w_d = ffn_w[layer]
        dw_down = mm_tn(act, d_y2, f"dw_down{layer}")
        rs_down = scatter_start([dw_down], [0], f"scatter_down{layer}_start")
        d_g, d_u = ffn_act_grad(d_y2, w_d, gu, f"ffn_act_grad{layer}")
        dw_g = mm_tn(f, d_g, f"dw_gate{layer}", dep=rs_down["token"])
        dw_u = mm_tn(f, d_u, f"dw_up{layer}")
        rs_gate_up = scatter_start([dw_g, dw_u], [1, 1], f"scatter_gate_up{layer}_start")
        dh_mid, d_y1, dg_fpre, dg_mpost = ffn_in_grad(d_g, d_u, w_g, w_u, hmid, dh_out, after(rs_gate_up["token"], vec(ln_ffn_pre[layer])),
                                                      y1, vec(ln_mix_post[layer]), f"ffn_in_grad{layer}")
        return dh_mid, d_y1, dg_fpre, dg_mpost, (rs_down, rs_gate_up)

    def mix_out_backward(layer, d_y1, mixed):
        dw_out = mm_tn(mixed, d_y1, f"dw_out{layer}")
        d_mixed = mm(d_y1, w_o[layer], f"d_mixed{layer}", trans_b=True)
        return d_mixed, dw_out

    def mem_backward(layer, q_src, q_block, kvm, memn, d_mixed, into):
        d_qm, d_kvm = mem_attn_bwd(q_src, q_block, kvm, d_mixed, into, f"mem_attn_bwd{layer}")
        d_kvm_b = d_kvm.astype(bf16)
        dw_mkv = mm_tn(memn, d_kvm_b, f"dw_mem_kv{layer}")
        d_memn = mm(d_kvm_b, w_mkv[layer], f"d_memn{layer}", trans_b=True)
        _, dg_mem = rms_bwd(mem0, vec(ln_mem[layer]), d_memn, None, bf16, f"mem_norm_bwd{layer}")
        return d_qm, dw_mkv, dg_mem


    dh_mid1, d_y1_1, dg_fpre1, dg_mpost1, rs_ffn1 = ffn_backward(1, dh, d_y2_1, hmid1, f1, gu1, act1, y1_1)
    d_mixed1, dw_out1 = mix_out_backward(1, d_y1_1, mixed1)
    dq_b, dk, dv, dc = fox_bwd(qb, kvb, d_mixed1, main1, lse, c_row, "fox_bwd", dq_width=D_MODEL)
    d_proj1, dw_mkv1, dg_mem1 = mem_backward(1, qb, MAIN_WIDTH // MEM_WIDTH, kvm1, memn1, d_mixed1, dq_b)
    rs_mix1 = scatter_start([dw_out1, dw_mkv1], [0, 0], "scatter_mix1_start")
    dc_t = jnp.pad(dc.reshape(FOX_HEADS, seq), ((0, 16 - FOX_HEADS), (0, 0)))
    dz_t, db_f = fgate_bwd(dc_t, z_t, bf_col, "fgate_bwd")
    d_kvf = jnp.concatenate([dk, dv, jnp.pad(dz_t[:FOX_HEADS].T.astype(bf16), ((0, 0), (0, KV_PAD - KV_WIDTH)))], axis=-1)
    dw_in_b = mm_tn(a1, d_proj1, "dw_in_b", dep=rs_mix1["token"])
    dw_kv = mm_tn(sin1, d_kvf, "dw_kv", tn=896)
    rs_2 = scatter_start([dw_in_b, dw_kv], [0, 0], "scatter_shared_start")
    dh1, (dg_pre1, dg_shared), d_y2_0, dg_fpost0 = proj_in_grad(
        [(d_proj1, w_inb, vec(ln_mix_pre[1])), (d_kvf, w_kv, vec(ln_shared))], h1, dh_mid1, "in_grad1", dep=rs_2["token"],
        below=(y2_0, vec(ln_ffn_post[0])))

    dh_mid0, d_y1_0, dg_fpre0, dg_mpost0, rs_ffn0 = ffn_backward(0, dh1, d_y2_0, hmid0, f0, gu0, act0, y1_0)
    d_mixed0, dw_out0 = mix_out_backward(0, d_y1_0, mixed0)
    d_uv, dw_s, db_s, dg_lnv, db_lnv = gmlp_bwd(proj0, d_mixed0, ws, ws_t, bs_t, lnv_g, lnv_b, "gmlp_bwd", out_width=w_in_a_full.shape[1])
    d_proj0, dw_mkv0, dg_mem0 = mem_backward(0, proj0, 2 * MAIN_WIDTH // MEM_WIDTH, kvm0, memn0, d_mixed0, d_uv)
    rs_mix0 = scatter_start([dw_out0, dw_mkv0], [0, 0], "scatter_mix0_start")

    small["ln_mix_pre"] = jnp.concatenate([jnp.zeros_like(dg_pre1), dg_pre1], axis=0)
    small["ln_mix_post"] = jnp.concatenate([dg_mpost0, dg_mpost1], axis=0)
    small["ln_ffn_pre"] = jnp.concatenate([dg_fpre0, dg_fpre1], axis=0)
    small["ln_ffn_post"] = jnp.concatenate([dg_fpost0, dg_fpost1], axis=0)
    small["ln_mem"] = jnp.concatenate([dg_mem0, dg_mem1], axis=0)
    small["w_spatial"] = dw_s[None]
    small["b_spatial"] = db_s[:, :A_GROUPS].T[None]
    small["ln_shared"] = dg_shared[0]
    small["b_forget"] = db_f[:FOX_HEADS, 0]
    small["ln_v_g"] = dg_lnv
    small["ln_v_b"] = db_lnv
    small_rows = jnp.concatenate([_pack_small(small, _SMALL), after(rs_mix0["token"], loss_tile)], axis=0)
    st_small = gather_start([small_rows[None]], [0], "gather_small_grads_start")
    dw_in_a_t = mm_tn(d_proj0, a0, "dw_in_a", tk=896, dep=st_small["token"])
    rs_in_a = scatter_start([dw_in_a_t], [0], "scatter_in_a_start")
    grad_x, (dg_pre0,) = proj_in_grad([(d_proj0, w_in_a_full, vec(ln_mix_pre[0]))], h0, dh_mid0, "in_grad0", dep=rs_in_a["token"])
    st_last = gather_start([dg_pre0.reshape(1, 8, LANES)], [0], "gather_last_grad_start")

    (p_down1,) = scatter_wait(rs_ffn1[0], [0], after(st_last["token"], grad_x[:8, :LANES]), "scatter_down1_wait")
    p_gate1, p_up1 = scatter_wait(rs_ffn1[1], [1, 1], p_down1, "scatter_gate_up1_wait")
    p_out1, p_mkv1 = scatter_wait(rs_mix1, [0, 0], p_gate1, "scatter_mix1_wait")
    p_in_b, p_kv = scatter_wait(rs_2, [0, 0], p_out1, "scatter_shared_wait")
    (p_down0,) = scatter_wait(rs_ffn0[0], [0], p_in_b, "scatter_down0_wait")
    p_gate0, p_up0 = scatter_wait(rs_ffn0[1], [1, 1], p_down0, "scatter_gate_up0_wait")
    p_out0, p_mkv0 = scatter_wait(rs_mix0, [0, 0], p_gate0, "scatter_mix0_wait")
    (p_in_a,) = scatter_wait(rs_in_a, [0], p_out0, "scatter_in_a_wait")
    owned_parts = dict(w_ffn_gate=[p_gate0, p_gate1], w_ffn_up=[p_up0, p_up1], w_ffn_down=[p_down0, p_down1], w_out=[p_out0, p_out1],
                       w_mem_kv=[p_mkv0, p_mkv1], w_in_b=[p_in_b], w_shared_kv=[p_kv])
    g_in_a = sum_leading(p_in_a, "sum_in_a", tr=_row_tile(p_in_a.shape[1]))
    (small_all,) = gather_wait(st_small, [0], g_in_a, "gather_small_grads_wait")
    (last_all,) = gather_wait(st_last, [0], small_all, "gather_last_grad_wait")
    small_sum = sum_leading(small_all, "sum_small_grads")
    loss = small_sum[small_rows.shape[0] - 1, 0]
    g_small = _unpack_small(small_sum, _SMALL)
    g_small["ln_mix_pre"] = jnp.concatenate([sum_leading(last_all, "sum_last_grad").reshape(1, D_MODEL), g_small["ln_mix_pre"][1:]], axis=0)
    shard = MAIN_WIDTH // N_DEV
    for n in ("ln_v_g", "ln_v_b"):
        g_small[n] = lax.dynamic_slice_in_dim(g_small[n], me * shard, shard, axis=1)
    grad_w = dict(g_small)

    delta, new_m, new_v = {}, {}, {}
    for n, parts in owned_parts.items():
        shape = weights[n].shape
        three_d = shape if len(shape) == 3 else (1,) + shape
        outs = adamw_owned(weights[n].reshape(three_d), parts, mom_m[n].reshape(three_d), mom_v[n].reshape(three_d), f"adamw_{n}",
                           tr=_row_tile(three_d[1]))
        grad_w[n], delta[n], new_m[n], new_v[n] = (t.reshape(shape) for t in outs)
    grad_w["w_in_a"] = g_in_a.T[None]
    d_, m_, v_ = adamw(w_in_a[0], g_in_a.T, m_w_in_a[0], v_w_in_a[0], "adamw_w_in_a", tr=512)
    delta["w_in_a"], new_m["w_in_a"], new_v["w_in_a"] = d_[None], m_[None], v_[None]
    small_local_shapes = [(n, tuple(weights[n].shape)) for n, _ in _SMALL]
    packed = [_pack_small(src, small_local_shapes) for src in (weights, grad_w, mom_m, mom_v)]
    outs = adamw(*packed, "adamw_small", tr=packed[0].shape[0])
    for dst, buf in zip((delta, new_m, new_v), outs):
        dst.update(_unpack_small(buf, small_local_shapes))

    return (loss, grad_x[None], *[grad_w[n] for n in names], *[delta[n] for n in names],
            *[new_m[n] for n in names], *[new_v[n] for n in names])
```

```python
import functools
import math

import jax
import jax.numpy as jnp
from jax import lax
from jax.experimental import pallas as pl
from jax.experimental.pallas import tpu as pltpu

f32 = jnp.float32
bf16 = jnp.bfloat16
SDS = jax.ShapeDtypeStruct

D_MODEL = 1024
MAIN_WIDTH = 768
MEM_WIDTH = 256
HEAD_DIM = 64
MEM_HEADS = 4
FOX_HEADS = 12
FOX_PAIRS = FOX_HEADS // 2
CHUNK = 128
A_GROUPS = 6
FF_SHARD = 352
FF_SHARD_PAD = 384
FF_PAD = 8 * FF_SHARD_PAD
KV_WIDTH = 2 * MAIN_WIDTH + FOX_HEADS
KV_PAD = 1792
RMS_EPS = 1e-6
LN_EPS = 1e-5
ATT_SCALE = HEAD_DIM ** -0.5
ADAM_LR, ADAM_B1, ADAM_B2, ADAM_EPS, ADAM_WD, ADAM_STEP = 0.001, 0.9, 0.999, 1e-08, 0.01, 10
N_DEV = 8
AXES = ("x", "y", "c")
MESH = pl.DeviceIdType.MESH
V7X_VMEM_LIMIT = 56 * 1024 * 1024
LANES = 128
FLAT_W = 512
ROW_PAD = 16


def _cparams(*sem):
    return pltpu.CompilerParams(dimension_semantics=sem or None, vmem_limit_bytes=V7X_VMEM_LIMIT)


def _dot(a, b):
    return jnp.dot(a, b, preferred_element_type=f32)


def _dot_nt(a, b):
    return lax.dot_general(a, b, (((1,), (1,)), ((), ())), preferred_element_type=f32)


def _dot_tn(a, b):
    return lax.dot_general(a, b, (((0,), (0,)), ((), ())), preferred_element_type=f32)


def _gelu(x):
    k = math.sqrt(2.0 / math.pi)
    t = jnp.tanh(k * (x + 0.044715 * x * x * x))
    return 0.5 * x * (1.0 + t), t


def _gelu_grad(x, t):
    k = math.sqrt(2.0 / math.pi)
    return 0.5 * (1.0 + t) + 0.5 * x * (1.0 - t * t) * k * (1.0 + 3.0 * 0.044715 * x * x)


def _sigmoid(x):
    return 1.0 / (1.0 + jnp.exp(-x))


def rms_fwd(x, gains, name, tm=512):
    m, d = x.shape
    tm = min(tm, m)
    n = len(gains)

    def body(x_ref, *refs):
        xv = x_ref[...]
        y = xv * lax.rsqrt(jnp.sum(xv * xv, axis=-1, keepdims=True) * (1.0 / d) + RMS_EPS)
        for g_ref, o_ref in zip(refs[:n], refs[n:]):
            o_ref[...] = (y * g_ref[...]).astype(bf16)

    row = pl.BlockSpec((tm, d), lambda i: (i, 0))
    vec = pl.BlockSpec((1, d), lambda i: (0, 0))
    return pl.pallas_call(body, grid=(m // tm,), in_specs=[row] + [vec] * n, out_specs=[row] * n,
                          out_shape=[SDS((m, d), bf16)] * n, name=name, compiler_params=_cparams("parallel"))(x, *gains)


def rms_bwd(x, g, dy, add, out_dtype, name, tm=512):
    m, d = x.shape
    tm = min(tm, m)
    has_add = add is not None

    def body(x_ref, g_ref, dy_ref, *refs):
        dx_ref, dg_ref = refs[-2], refs[-1]
        xv = x_ref[...]
        dyv = dy_ref[...].astype(f32)
        r = lax.rsqrt(jnp.sum(xv * xv, axis=-1, keepdims=True) * (1.0 / d) + RMS_EPS)
        xn = xv * r
        dyg = dyv * g_ref[...]
        dx = r * (dyg - xn * (jnp.sum(dyg * xn, axis=-1, keepdims=True) * (1.0 / d)))
        if has_add:
            dx = dx + refs[0][...]
        dx_ref[...] = dx.astype(out_dtype)

        @pl.when(pl.program_id(0) == 0)
        def _():
            dg_ref[...] = jnp.zeros_like(dg_ref)

        dg_ref[...] += jnp.sum(dyv * xn, axis=0, keepdims=True)

    row = pl.BlockSpec((tm, d), lambda i: (i, 0))
    vec = pl.BlockSpec((1, d), lambda i: (0, 0))
    ins = [x, g, dy] + ([add] if has_add else [])
    return pl.pallas_call(body, grid=(m // tm,), in_specs=[row, vec, row] + ([row] if has_add else []),
                          out_specs=[row, vec], out_shape=[SDS((m, d), out_dtype), SDS((1, d), f32)], name=name,
                          compiler_params=_cparams("arbitrary"))(*ins)


def mm(a, b, name, trans_b=False, out_dtype=f32, tm=1024, tn=1024, layer=None, col0=0, ncols=None, dep=None):
    m, k = a.shape
    n_all = b.shape[-2] if trans_b else b.shape[-1]
    n = n_all if ncols is None else ncols
    tm, tn = min(tm, m), min(tn, n)
    assert m % tm == 0 and n % tn == 0 and col0 % tn == 0 and not (trans_b and col0), (name, m, n, tm, tn)
    jb = col0 // tn
    lead = () if layer is None else (None,)
    sel = () if layer is None else (layer,)

    def body(a_ref, b_ref, *rest):
        r = _dot_nt(a_ref[...], b_ref[...]) if trans_b else _dot(a_ref[...], b_ref[...])
        rest[-1][...] = r.astype(out_dtype)

    if trans_b:
        b_spec = pl.BlockSpec(lead + (tn, k), lambda j, i: sel + (j, 0))
    else:
        b_spec = pl.BlockSpec(lead + (k, tn), lambda j, i: sel + (0, jb + j))
    deps = [] if dep is None else [dep]
    dep_specs = [pl.BlockSpec((8, LANES), lambda j, i: (0, 0))] * len(deps)
    return pl.pallas_call(body, grid=(n // tn, m // tm), in_specs=[pl.BlockSpec((tm, k), lambda j, i: (i, 0)), b_spec] + dep_specs,
                          out_specs=pl.BlockSpec((tm, tn), lambda j, i: (i, j)), out_shape=SDS((m, n), out_dtype),
                          name=name, compiler_params=_cparams("parallel", "parallel"))(a, b, *deps)


def mm_tn(a, g, name, tk=1024, tn=1024, out_dtype=bf16, dep=None):
    s, k = a.shape
    n = g.shape[1]
    tk, tn = min(tk, k), min(tn, n)
    assert k % tk == 0 and n % tn == 0, (name, k, n, tk, tn)

    def body(a_ref, g_ref, *rest):
        rest[-1][...] = _dot_tn(a_ref[...], g_ref[...]).astype(out_dtype)

    deps = [] if dep is None else [dep]
    dep_specs = [pl.BlockSpec((8, LANES), lambda i, j: (0, 0))] * len(deps)
    return pl.pallas_call(body, grid=(k // tk, n // tn),
                          in_specs=[pl.BlockSpec((s, tk), lambda i, j: (0, i)), pl.BlockSpec((s, tn), lambda i, j: (0, j))] + dep_specs,
                          out_specs=pl.BlockSpec((tk, tn), lambda i, j: (i, j)), out_shape=SDS((k, n), out_dtype), name=name,
                          compiler_params=_cparams("parallel", "parallel"))(a, g, *deps)


def _resident(shape, index_map):
    return pl.BlockSpec(shape, index_map, pipeline_mode=pl.Buffered(1))


def _rms(xv):
    return xv * lax.rsqrt(jnp.sum(xv * xv, axis=-1, keepdims=True) * (1.0 / xv.shape[-1]) + RMS_EPS)


def _rms_bwd_math(xv, g, dy):
    d = xv.shape[-1]
    r = lax.rsqrt(jnp.sum(xv * xv, axis=-1, keepdims=True) * (1.0 / d) + RMS_EPS)
    xn = xv * r
    dyg = dy * g
    dx = r * (dyg - xn * (jnp.sum(dyg * xn, axis=-1, keepdims=True) * (1.0 / d)))
    return dx, jnp.sum(dy * xn, axis=0, keepdims=True)


SUB_ROWS = 512


def mm_resnorm(a, b, h, g_post, gains, name, tm=512):
    m, k = a.shape
    d = b.shape[1]
    n = len(gains)

    def body(a_ref, b_ref, h_ref, gp_ref, *refs):
        for r in range(tm // SUB_ROWS):
            rows = slice(r * SUB_ROWS, (r + 1) * SUB_ROWS)
            y = _dot(a_ref[rows, :], b_ref[...])
            refs[n][rows, :] = y
            hn = h_ref[rows, :] + _rms(y) * gp_ref[...]
            refs[n + 1][rows, :] = hn
            if n:
                z = _rms(hn)
                for g_ref, o_ref in zip(refs[:n], refs[n + 2:]):
                    o_ref[rows, :] = (z * g_ref[...]).astype(bf16)

    row = pl.BlockSpec((tm, d), lambda i: (i, 0))
    vec = pl.BlockSpec((1, d), lambda i: (0, 0))
    return pl.pallas_call(body, grid=(m // tm,),
                          in_specs=[pl.BlockSpec((tm, k), lambda i: (i, 0)), _resident((k, d), lambda i: (0, 0)), row, vec] + [vec] * n,
                          out_specs=[row] * (n + 2), out_shape=[SDS((m, d), f32)] * 2 + [SDS((m, d), bf16)] * n, name=name,
                          compiler_params=_cparams("parallel"))(a, b, h, g_post, *gains)


def mm_resnorm_loss(a, b, h, g_post, tgt, name, tm=512):
    m, k = a.shape
    d = b.shape[1]

    def body(a_ref, b_ref, h_ref, gp_ref, t_ref, dh_ref, dy_ref, dg_ref, l_ref):
        @pl.when(pl.program_id(0) == 0)
        def _():
            dg_ref[...] = jnp.zeros_like(dg_ref)
            l_ref[...] = jnp.zeros_like(l_ref)

        y = _dot(a_ref[...], b_ref[...])
        e = h_ref[...] + _rms(y) * gp_ref[...] - t_ref[...]
        dh = e * (1.0 / d)
        dh_ref[...] = dh
        part = jnp.sum(jnp.sum(e * e, axis=-1, keepdims=True), axis=0, keepdims=True) * (0.5 / d)
        l_ref[...] += jnp.broadcast_to(part, l_ref.shape)
        dy, dg = _rms_bwd_math(y, gp_ref[...], dh)
        dy_ref[...] = dy.astype(bf16)
        dg_ref[...] += dg

    row = pl.BlockSpec((tm, d), lambda i: (i, 0))
    vec = pl.BlockSpec((1, d), lambda i: (0, 0))
    return pl.pallas_call(body, grid=(m // tm,),
                          in_specs=[pl.BlockSpec((tm, k), lambda i: (i, 0)), _resident((k, d), lambda i: (0, 0)), row, vec, row],
                          out_specs=[row, row, vec, pl.BlockSpec((8, LANES), lambda i: (0, 0))],
                          out_shape=[SDS((m, d), f32), SDS((m, d), bf16), SDS((1, d), f32), SDS((8, LANES), f32)], name=name,
                          compiler_params=_cparams("arbitrary"))(a, b, h, g_post, tgt)


def ffn_act_grad(d_y2, w_d, factors, name, tm=1024, tn=1536):
    s, d = d_y2.shape
    ff = w_d.shape[0]
    nb = ff // tn

    def body(a_ref, b_ref, g_ref, u_ref, dg_ref, du_ref):
        av = a_ref[...]
        tc = 256
        for c in range(tn // tc):
            cols = slice(c * tc, (c + 1) * tc)
            da = _dot_nt(av, b_ref[cols, :])
            dg_ref[:, cols] = (da * g_ref[:, cols].astype(f32)).astype(bf16)
            du_ref[:, cols] = (da * u_ref[:, cols].astype(f32)).astype(bf16)

    tile = pl.BlockSpec((tm, tn), lambda j, i: (i, j))
    return pl.pallas_call(body, grid=(nb, s // tm),
                          in_specs=[pl.BlockSpec((tm, d), lambda j, i: (i, 0)), pl.BlockSpec((tn, d), lambda j, i: (j, 0)), tile,
                                    pl.BlockSpec((tm, tn), lambda j, i: (i, nb + j))],
                          out_specs=[tile, tile], out_shape=[SDS((s, ff), bf16)] * 2, name=name,
                          compiler_params=_cparams("parallel", "parallel"))(d_y2, w_d, factors, factors)


def ffn_in_grad(d_g, d_u, w_g, w_u, hmid, dh_out, g_pre, y1, g_post, name, tm=512):
    s, ff = d_g.shape
    d = w_g.shape[0]

    def body(dg_ref, du_ref, wg_ref, wu_ref, hm_ref, dho_ref, gpre_ref, y1_ref, gpost_ref, dhm_ref, dy1_ref, dgpre_ref, dgpost_ref):
        @pl.when(pl.program_id(0) == 0)
        def _():
            dgpre_ref[...] = jnp.zeros_like(dgpre_ref)
            dgpost_ref[...] = jnp.zeros_like(dgpost_ref)

        for r in range(tm // SUB_ROWS):
            rows = slice(r * SUB_ROWS, (r + 1) * SUB_ROWS)
            d_f = _dot_nt(dg_ref[rows, :], wg_ref[...]) + _dot_nt(du_ref[rows, :], wu_ref[...])
            dx, dg1 = _rms_bwd_math(hm_ref[rows, :], gpre_ref[...], d_f)
            dh_mid = dho_ref[rows, :] + dx
            dhm_ref[rows, :] = dh_mid
            dgpre_ref[...] += dg1
            dy1, dg2 = _rms_bwd_math(y1_ref[rows, :], gpost_ref[...], dh_mid)
            dy1_ref[rows, :] = dy1.astype(bf16)
            dgpost_ref[...] += dg2

    row = pl.BlockSpec((tm, d), lambda i: (i, 0))
    vec = pl.BlockSpec((1, d), lambda i: (0, 0))
    wide = pl.BlockSpec((tm, ff), lambda i: (i, 0))
    w_spec = _resident((d, ff), lambda i: (0, 0))
    return pl.pallas_call(body, grid=(s // tm,), in_specs=[wide, wide, w_spec, w_spec, row, row, vec, row, vec],
                          out_specs=[row, row, vec, vec], out_shape=[SDS((s, d), f32), SDS((s, d), bf16), SDS((1, d), f32), SDS((1, d), f32)],
                          name=name, compiler_params=_cparams("arbitrary"))(d_g, d_u, w_g, w_u, hmid, dh_out, g_pre, y1, g_post)


def proj_in_grad(pairs, x, add, name, tm=512, dep=None, below=None):
    s, d = x.shape
    n = len(pairs)
    extra = [] if dep is None else [dep]
    n_below = 0 if below is None else 2

    def body(*refs):
        x_ref, add_ref = refs[3 * n], refs[3 * n + 1]
        below_refs = refs[3 * n + 2:3 * n + 2 + n_below]
        outs = refs[3 * n + 2 + n_below + len(extra):]

        @pl.when(pl.program_id(0) == 0)
        def _():
            for o in outs[1:1 + n] + outs[2 + n:]:
                o[...] = jnp.zeros_like(o)

        xv = x_ref[...]
        dx = add_ref[...]
        for i in range(n):
            a_ref, b_ref, g_ref = refs[3 * i:3 * i + 3]
            dxi, dgi = _rms_bwd_math(xv, g_ref[...], _dot_nt(a_ref[...], b_ref[...]))
            dx = dx + dxi
            outs[1 + i][...] += dgi
        outs[0][...] = dx
        if below is not None:
            dy, dg = _rms_bwd_math(below_refs[0][...], below_refs[1][...], dx)
            outs[1 + n][...] = dy.astype(bf16)
            outs[2 + n][...] += dg

    row = pl.BlockSpec((tm, d), lambda i: (i, 0))
    vec = pl.BlockSpec((1, d), lambda i: (0, 0))
    in_specs, args = [], []
    for a, b, g in pairs:
        k = a.shape[1]
        in_specs += [pl.BlockSpec((tm, k), lambda i: (i, 0)), _resident((d, k), lambda i: (0, 0)), vec]
        args += [a, b, g]
    in_specs += [row, row] + [row, vec][:n_below] + [pl.BlockSpec((8, LANES), lambda i: (0, 0))] * len(extra)
    out_specs = [row] + [vec] * n + [row, vec][:n_below]
    out_shape = [SDS((s, d), f32)] + [SDS((1, d), f32)] * n + [SDS((s, d), bf16), SDS((1, d), f32)][:n_below]
    out = pl.pallas_call(body, grid=(s // tm,), in_specs=in_specs, out_specs=out_specs, out_shape=out_shape, name=name,
                         compiler_params=_cparams("arbitrary"))(*args, x, add, *(below or ()), *extra)
    return (out[0], out[1:1 + n]) + tuple(out[1 + n:])


def ffn_up(f, wg, wu, name, tm=512, tc=256):
    s, d = f.shape
    ff = wg.shape[-1]

    def body(f_ref, wg_ref, wu_ref, fac_ref, act_ref):
        fv = f_ref[...]
        for j in range(ff // tc):
            lo = j * tc
            gg = _dot(fv, wg_ref[:, lo:lo + tc])
            uu = _dot(fv, wu_ref[:, lo:lo + tc])
            sg = _sigmoid(gg)
            silu = gg * sg
            fac_ref[:, lo:lo + tc] = (uu * (sg + silu * (1.0 - sg))).astype(bf16)
            fac_ref[:, ff + lo:ff + lo + tc] = silu.astype(bf16)
            act_ref[:, lo:lo + tc] = (silu * uu).astype(bf16)

    w_spec = _resident((d, ff), lambda i: (0, 0))
    return pl.pallas_call(body, grid=(s // tm,), in_specs=[pl.BlockSpec((tm, d), lambda i: (i, 0)), w_spec, w_spec],
                          out_specs=[pl.BlockSpec((tm, 2 * ff), lambda i: (i, 0)), pl.BlockSpec((tm, ff), lambda i: (i, 0))],
                          out_shape=[SDS((s, 2 * ff), bf16), SDS((s, ff), bf16)], name=name,
                          compiler_params=_cparams("parallel"))(f, wg, wu)


def _gmlp_forward_chunk(u, v, w_refs, bias, ln_g, ln_b):
    gu, tu = _gelu(u)
    gv, tv = _gelu(v)
    mu = jnp.sum(gv, axis=-1, keepdims=True) * (1.0 / MAIN_WIDTH)
    xc = gv - mu
    rstd = lax.rsqrt(jnp.sum(xc * xc, axis=-1, keepdims=True) * (1.0 / MAIN_WIDTH) + LN_EPS)
    xhat = xc * rstd
    vln = xhat * ln_g + ln_b
    row = lax.broadcasted_iota(jnp.int32, (CHUNK, CHUNK), 0)
    col = lax.broadcasted_iota(jnp.int32, (CHUNK, CHUNK), 1)
    s_parts = []
    for g in range(A_GROUPS):
        w = jnp.where(col <= row, w_refs[g], jnp.zeros((), bf16))
        s_parts.append(_dot(w, vln[:, g * CHUNK:(g + 1) * CHUNK].astype(bf16)) + bias[:, g:g + 1])
    return gu, tu, tv, rstd, xhat, vln, s_parts


def gmlp_fwd(proj, ws, bs_t, ln_g, ln_b, name, tm=512, out_width=MAIN_WIDTH):
    s = proj.shape[0]

    def body(u_ref, v_ref, w_ref, b_ref, g_ref, bb_ref, o_ref):
        bias = b_ref[...]
        for c in range(tm // CHUNK):
            rows = slice(c * CHUNK, (c + 1) * CHUNK)
            gu, _, _, _, _, _, s_parts = _gmlp_forward_chunk(u_ref[rows, :], v_ref[rows, :], w_ref, bias, g_ref[...], bb_ref[...])
            for g in range(A_GROUPS):
                cols = slice(g * CHUNK, (g + 1) * CHUNK)
                o_ref[rows, cols] = (gu[:, cols] * s_parts[g]).astype(bf16)

    vec = pl.BlockSpec((1, MAIN_WIDTH), lambda i: (0, 0))
    return pl.pallas_call(
        body, grid=(s // tm,),
        in_specs=[pl.BlockSpec((tm, MAIN_WIDTH), lambda i: (i, 0)), pl.BlockSpec((tm, MAIN_WIDTH), lambda i: (i, 1)),
                  pl.BlockSpec((A_GROUPS, CHUNK, CHUNK), lambda i: (0, 0, 0)), pl.BlockSpec((CHUNK, A_GROUPS), lambda i: (0, 0)), vec, vec],
        out_specs=pl.BlockSpec((tm, MAIN_WIDTH), lambda i: (i, 0)), out_shape=SDS((s, out_width), bf16), name=name,
        compiler_params=_cparams("parallel"))(proj, proj, ws, bs_t, ln_g, ln_b)


def gmlp_bwd(proj, d_mixed, ws, ws_t, bs_t, ln_g, ln_b, name, tm=512, out_width=2 * MAIN_WIDTH):
    s = proj.shape[0]

    def body(u_ref, v_ref, dm_ref, w_ref, wt_ref, b_ref, g_ref, bb_ref, duv_ref, dw_ref, db_ref, dg_ref, dbb_ref):
        @pl.when(pl.program_id(0) == 0)
        def _():
            dw_ref[...] = jnp.zeros_like(dw_ref)
            db_ref[...] = jnp.zeros_like(db_ref)
            dg_ref[...] = jnp.zeros_like(dg_ref)
            dbb_ref[...] = jnp.zeros_like(dbb_ref)

        bias = b_ref[...]
        ln_gv = g_ref[...]
        row = lax.broadcasted_iota(jnp.int32, (CHUNK, CHUNK), 0)
        col = lax.broadcasted_iota(jnp.int32, (CHUNK, CHUNK), 1)
        lane = lax.broadcasted_iota(jnp.int32, (CHUNK, LANES), 1)
        for c in range(tm // CHUNK):
            rows = slice(c * CHUNK, (c + 1) * CHUNK)
            u = u_ref[rows, :]
            v = v_ref[rows, :]
            gu, tu, tv, rstd, xhat, vln, s_parts = _gmlp_forward_chunk(u, v, w_ref, bias, ln_gv, bb_ref[...])
            dm = dm_ref[rows, :]
            d_vln_parts = []
            d_gu_parts = []
            db_acc = jnp.zeros((CHUNK, LANES), f32)
            for g in range(A_GROUPS):
                cols = slice(g * CHUNK, (g + 1) * CHUNK)
                dmg = dm[:, cols]
                d_gu_parts.append(dmg * s_parts[g])
                d_s = dmg * gu[:, cols]
                db_acc = db_acc + jnp.where(lane == g, jnp.sum(d_s, axis=-1, keepdims=True), 0.0)
                d_sb = d_s.astype(bf16)
                dw_ref[g] += jnp.where(col <= row, _dot_nt(d_sb, vln[:, cols].astype(bf16)), 0.0)
                wt = jnp.where(row <= col, wt_ref[g], jnp.zeros((), bf16))
                d_vln_parts.append(_dot(wt, d_sb))
            db_ref[...] += db_acc
            d_vln = jnp.concatenate(d_vln_parts, axis=-1)
            d_gu = jnp.concatenate(d_gu_parts, axis=-1)
            dg_ref[...] += jnp.sum(d_vln * xhat, axis=0, keepdims=True)
            dbb_ref[...] += jnp.sum(d_vln, axis=0, keepdims=True)
            dxh = d_vln * ln_gv
            m1 = jnp.sum(dxh, axis=-1, keepdims=True) * (1.0 / MAIN_WIDTH)
            m2 = jnp.sum(dxh * xhat, axis=-1, keepdims=True) * (1.0 / MAIN_WIDTH)
            d_gv = rstd * (dxh - m1 - xhat * m2)
            duv_ref[rows, :MAIN_WIDTH] = (d_gu * _gelu_grad(u, tu)).astype(bf16)
            duv_ref[rows, MAIN_WIDTH:] = (d_gv * _gelu_grad(v, tv)).astype(bf16)

    vec = pl.BlockSpec((1, MAIN_WIDTH), lambda i: (0, 0))
    wspec = pl.BlockSpec((A_GROUPS, CHUNK, CHUNK), lambda i: (0, 0, 0))
    return pl.pallas_call(
        body, grid=(s // tm,),
        in_specs=[pl.BlockSpec((tm, MAIN_WIDTH), lambda i: (i, 0)), pl.BlockSpec((tm, MAIN_WIDTH), lambda i: (i, 1)),
                  pl.BlockSpec((tm, MAIN_WIDTH), lambda i: (i, 0)), wspec, wspec, pl.BlockSpec((CHUNK, A_GROUPS), lambda i: (0, 0)), vec, vec],
        out_specs=[pl.BlockSpec((tm, 2 * MAIN_WIDTH), lambda i: (i, 0)), wspec, pl.BlockSpec((CHUNK, LANES), lambda i: (0, 0)), vec, vec],
        out_shape=[SDS((s, out_width), bf16), SDS((A_GROUPS, CHUNK, CHUNK), f32), SDS((CHUNK, LANES), f32),
                   SDS((1, MAIN_WIDTH), f32), SDS((1, MAIN_WIDTH), f32)],
        name=name, compiler_params=_cparams("arbitrary"))(proj, proj, d_mixed, ws, ws_t, bs_t, ln_g, ln_b)


def _head_mask(width, h):
    lane = lax.broadcasted_iota(jnp.int32, (1, width), 1)
    return (lane >= h * HEAD_DIM) & (lane < (h + 1) * HEAD_DIM)


def mem_attn_fwd(proj, q_block, kv, into, name, tm=512):
    s = proj.shape[0]
    n_mem = kv.shape[0]
    out_block = into.shape[1] // MEM_WIDTH - 1

    def body(q_ref, kv_ref, into_ref, o_ref):
        q = q_ref[...].astype(f32)
        k = kv_ref[:, :MEM_WIDTH].astype(bf16)
        v = kv_ref[:, MEM_WIDTH:].astype(bf16)
        out = jnp.zeros((tm, MEM_WIDTH), f32)
        for h in range(MEM_HEADS):
            msk = _head_mask(MEM_WIDTH, h)
            qh = jnp.where(msk, q, 0.0).astype(bf16)
            sc = _dot_nt(qh, k) * ATT_SCALE
            e = jnp.exp(sc - jnp.max(sc, axis=-1, keepdims=True))
            p = e / jnp.sum(e, axis=-1, keepdims=True)
            out = jnp.where(msk, _dot(p.astype(bf16), v), out)
        o_ref[...] = out.astype(bf16)

    return pl.pallas_call(body, grid=(s // tm,),
                          in_specs=[pl.BlockSpec((tm, MEM_WIDTH), lambda i: (i, q_block)), pl.BlockSpec((n_mem, 2 * MEM_WIDTH), lambda i: (0, 0)), _ANY],
                          out_specs=pl.BlockSpec((tm, MEM_WIDTH), lambda i: (i, out_block)), out_shape=SDS(into.shape, bf16), name=name,
                          input_output_aliases={2: 0}, compiler_params=_cparams("parallel"))(proj, kv, into)


def mem_attn_bwd(proj, q_block, kv, d_mixed, into, name, tm=512):
    s = proj.shape[0]
    n_mem = kv.shape[0]
    out_block = into.shape[1] // MEM_WIDTH - 1

    def body(q_ref, kv_ref, do_ref, into_ref, dq_ref, dkv_ref):
        @pl.when(pl.program_id(0) == 0)
        def _():
            dkv_ref[...] = jnp.zeros_like(dkv_ref)

        q = q_ref[...].astype(f32)
        do = do_ref[...]
        k = kv_ref[:, :MEM_WIDTH].astype(bf16)
        v = kv_ref[:, MEM_WIDTH:].astype(bf16)
        dq = jnp.zeros((tm, MEM_WIDTH), f32)
        dk = jnp.zeros((n_mem, MEM_WIDTH), f32)
        dv = jnp.zeros((n_mem, MEM_WIDTH), f32)
        for h in range(MEM_HEADS):
            msk = _head_mask(MEM_WIDTH, h)
            qh = jnp.where(msk, q, 0.0).astype(bf16)
            doh = jnp.where(msk, do, 0.0).astype(bf16)
            sc = _dot_nt(qh, k) * ATT_SCALE
            e = jnp.exp(sc - jnp.max(sc, axis=-1, keepdims=True))
            p = e / jnp.sum(e, axis=-1, keepdims=True)
            dp = _dot_nt(doh, v)
            ds = p * (dp - jnp.sum(dp * p, axis=-1, keepdims=True))
            dsb = (ds * ATT_SCALE).astype(bf16)
            dq = jnp.where(msk, _dot(dsb, k), dq)
            dk = dk + _dot_tn(dsb, qh)
            dv = dv + _dot_tn(p.astype(bf16), doh)
        dq_ref[...] = dq.astype(bf16)
        dkv_ref[:, :MEM_WIDTH] += dk
        dkv_ref[:, MEM_WIDTH:] += dv

    return pl.pallas_call(
        body, grid=(s // tm,),
        in_specs=[pl.BlockSpec((tm, MEM_WIDTH), lambda i: (i, q_block)), pl.BlockSpec((n_mem, 2 * MEM_WIDTH), lambda i: (0, 0)),
                  pl.BlockSpec((tm, MEM_WIDTH), lambda i: (i, MAIN_WIDTH // MEM_WIDTH)), _ANY],
        out_specs=[pl.BlockSpec((tm, MEM_WIDTH), lambda i: (i, out_block)), pl.BlockSpec((n_mem, 2 * MEM_WIDTH), lambda i: (0, 0))],
        out_shape=[SDS(into.shape, bf16), SDS((n_mem, 2 * MEM_WIDTH), f32)], name=name,
        input_output_aliases={3: 0}, compiler_params=_cparams("arbitrary"))(proj, kv, d_mixed, into)


def _tri(t, upper):
    r = lax.broadcasted_iota(jnp.int32, (t, t), 0)
    c = lax.broadcasted_iota(jnp.int32, (t, t), 1)
    return ((r <= c) if upper else (r >= c)).astype(f32)


def fgate_fwd(z_t, b, name, t=512):
    hh, s = z_t.shape

    def body(z_ref, b_ref, c_ref):
        u = _tri(t, True)
        carry = jnp.zeros((hh, 1), f32)
        for blk in range(s // t):
            x = z_ref[:, blk * t:(blk + 1) * t] + b_ref[...]
            logf = jnp.minimum(x, 0.0) - jnp.log(1.0 + jnp.exp(-jnp.abs(x)))
            y = jnp.dot(logf, u, precision=lax.Precision.HIGHEST, preferred_element_type=f32) + carry
            c_ref[:, blk * t:(blk + 1) * t] = y
            carry = y[:, t - 1:t]

    return pl.pallas_call(body, out_shape=SDS((hh, s), f32), name=name, compiler_params=_cparams())(z_t, b)


def fgate_bwd(dc_t, z_t, b, name, t=512):
    hh, s = z_t.shape

    def body(dc_ref, z_ref, b_ref, dz_ref, db_ref):
        low = _tri(t, False)
        carry = jnp.zeros((hh, 1), f32)
        total = jnp.zeros((hh, 1), f32)
        for blk in reversed(range(s // t)):
            cols = slice(blk * t, (blk + 1) * t)
            y = jnp.dot(dc_ref[:, cols], low, precision=lax.Precision.HIGHEST, preferred_element_type=f32) + carry
            carry = y[:, 0:1]
            dz = y * _sigmoid(-(z_ref[:, cols] + b_ref[...]))
            dz_ref[:, cols] = dz
            total = total + jnp.sum(dz, axis=-1, keepdims=True)
        db_ref[...] = jnp.broadcast_to(total, db_ref.shape)

    return pl.pallas_call(body, out_shape=[SDS((hh, s), f32), SDS((hh, LANES), f32)], name=name,
                          compiler_params=_cparams())(dc_t, z_t, b)


def _pair_masks():
    lane = lax.broadcasted_iota(jnp.int32, (1, LANES), 1)
    return [lane < HEAD_DIM, lane >= HEAD_DIM]


def _tile_base(cr_ref, hh, lo):
    return cr_ref[hh:hh + 1, pl.ds(lo, LANES)][:, 0:1]


def fox_fwd(q, kv, c_row, name, tq=512, out_width=MAIN_WIDTH):
    s = kv.shape[0]
    nq = s // tq

    def body(q_ref, k_ref, v_ref, cr_ref, o_ref, lse_ref, ob_ref):
        i = pl.program_id(1)
        qv = q_ref[...]
        masks = _pair_masks()
        row = lax.broadcasted_iota(jnp.int32, (tq, tq), 0)
        col = lax.broadcasted_iota(jnp.int32, (tq, tq), 1)
        qh = [jnp.where(masks[hh], qv, jnp.zeros((), bf16)) * ATT_SCALE for hh in range(2)]
        ct = [_tile_base(cr_ref, hh, pl.multiple_of(i * tq, tq)) for hh in range(2)]

        def block(j, carry, diag):
            lo = pl.multiple_of(j * tq, tq)
            ks = k_ref[pl.ds(lo, tq), :]
            vs = v_ref[pl.ds(lo, tq), :]
            out = []
            for hh in range(2):
                m, l, acc = carry[hh]
                sc = _dot_nt(qh[hh], ks) + (ct[hh] - cr_ref[hh:hh + 1, pl.ds(lo, tq)])
                if diag:
                    sc = jnp.where(col <= row, sc, -jnp.inf)
                m_new = jnp.maximum(m, jnp.max(sc, axis=-1, keepdims=True))
                alpha = jnp.exp(m - m_new)
                p = jnp.exp(sc - m_new)
                l = alpha * l + jnp.sum(p, axis=-1, keepdims=True)
                p_hi = p.astype(bf16)
                p_lo = (p - p_hi.astype(f32)).astype(bf16)
                acc = alpha * acc + (_dot(p_hi, vs) + _dot(p_lo, vs))
                out.append((m_new, l, acc))
            return tuple(out)

        init = (jnp.full((tq, 1), -jnp.inf, f32), jnp.zeros((tq, 1), f32), jnp.zeros((tq, LANES), f32))
        carry = lax.fori_loop(0, i, functools.partial(block, diag=False), (init, init))
        res = [(acc / l, m + jnp.log(l)) for m, l, acc in block(i, carry, True)]
        out = jnp.where(masks[0], res[0][0], res[1][0])
        o_ref[...] = out
        ob_ref[...] = out.astype(bf16)
        lse_ref[...] = jnp.where(masks[0], res[0][1], res[1][1])

    return pl.pallas_call(
        body, grid=(FOX_PAIRS, nq),
        in_specs=[pl.BlockSpec((tq, LANES), lambda p, i: (i, p)), pl.BlockSpec((s, LANES), lambda p, i: (0, p)),
                  pl.BlockSpec((s, LANES), lambda p, i: (0, FOX_PAIRS + p)), pl.BlockSpec((None, 2, s), lambda p, i: (p, 0, 0))],
        out_specs=[pl.BlockSpec((tq, LANES), lambda p, i: (i, p)), pl.BlockSpec((None, tq, LANES), lambda p, i: (p, i, 0)),
                   pl.BlockSpec((tq, LANES), lambda p, i: (i, p))],
        out_shape=[SDS((s, MAIN_WIDTH), f32), SDS((FOX_PAIRS, s, LANES), f32), SDS((s, out_width), bf16)], name=name,
        compiler_params=_cparams("parallel", "parallel"))(q, kv, kv, c_row)


def fox_bwd(q, kv, d_mixed, o, lse, c_row, name, tq=512, dq_width=MAIN_WIDTH):
    s = kv.shape[0]
    nq = s // tq

    def body(q_ref, k_ref, v_ref, do_ref, o_ref, lse_ref, cr_ref, dqb_ref, dk_ref, dv_ref, dc_ref, dq_ref):
        j = pl.program_id(1)

        @pl.when(j == 0)
        def _():
            dq_ref[...] = jnp.zeros_like(dq_ref)

        masks = _pair_masks()
        sub = lax.broadcasted_iota(jnp.int32, (LANES, 1), 0)
        sub_masks = [sub < HEAD_DIM, sub >= HEAD_DIM]
        row = lax.broadcasted_iota(jnp.int32, (tq, tq), 0)
        col = lax.broadcasted_iota(jnp.int32, (tq, tq), 1)
        kj = k_ref[...]
        vj = v_ref[...]
        lo_j = pl.multiple_of(j * tq, tq)

        def block(i, carry, diag):
            dk_t, dv_t, dc0, dc1 = carry
            dcs = [dc0, dc1]
            lo = pl.multiple_of(i * tq, tq)
            qi = q_ref[pl.ds(lo, tq), :]
            qi = qi * ATT_SCALE
            qt_i = qi.T
            doi = do_ref[pl.ds(lo, tq), :]
            dot_i = doi.astype(bf16).T
            prod = doi.astype(bf16).astype(f32) * o_ref[pl.ds(lo, tq), :]
            lse_i = lse_ref[pl.ds(lo, tq), :]
            dq_i = jnp.zeros((tq, LANES), f32)
            for hh in range(2):
                qh = jnp.where(masks[hh], qi, jnp.zeros((), bf16))
                doh = jnp.where(masks[hh], doi, 0.0).astype(bf16)
                delta = jnp.sum(jnp.where(masks[hh], prod, 0.0), axis=-1, keepdims=True)
                sc = _dot_nt(qh, kj) + (_tile_base(cr_ref, hh, lo) - cr_ref[hh:hh + 1, pl.ds(lo_j, tq)])
                p = jnp.exp(sc - lse_i[:, hh * HEAD_DIM:hh * HEAD_DIM + 1])
                if diag:
                    p = jnp.where(col <= row, p, 0.0)
                dv_t = dv_t + _dot(jnp.where(sub_masks[hh], dot_i, jnp.zeros((), bf16)), p.astype(bf16))
                ds = p * (_dot_nt(doh, vj) - delta)
                dcs[hh] = dcs[hh] + jnp.sum(ds, axis=0, keepdims=True)
                dsb = ds.astype(bf16)
                dq_i = jnp.where(masks[hh], _dot(dsb, kj), dq_i)
                dk_t = dk_t + _dot(jnp.where(sub_masks[hh], qt_i, jnp.zeros((), bf16)), dsb)
            dq_ref[pl.ds(lo, tq), :] += dq_i * ATT_SCALE
            return dk_t, dv_t, dcs[0], dcs[1]

        zero = jnp.zeros((LANES, tq), f32)
        zrow = jnp.zeros((1, tq), f32)
        carry = block(j, (zero, zero, zrow, zrow), True)
        dk_t, dv_t, dc0, dc1 = lax.fori_loop(j + 1, nq, functools.partial(block, diag=False), carry)
        dk_ref[...] = dk_t.T.astype(bf16)
        dv_ref[...] = dv_t.T.astype(bf16)
        dc_ref[0:1, :] = -dc0
        dc_ref[1:2, :] = -dc1

        @pl.when(j == nq - 1)
        def _():
            dqb_ref[...] = dq_ref[...].astype(bf16)

    full = lambda p, j: (0, p)
    tile = lambda p, j: (j, p)
    return pl.pallas_call(
        body, grid=(FOX_PAIRS, nq),
        in_specs=[pl.BlockSpec((s, LANES), full), pl.BlockSpec((tq, LANES), tile), pl.BlockSpec((tq, LANES), lambda p, j: (j, FOX_PAIRS + p)),
                  pl.BlockSpec((s, LANES), full), pl.BlockSpec((s, LANES), full), pl.BlockSpec((None, s, LANES), lambda p, j: (p, 0, 0)),
                  pl.BlockSpec((None, 2, s), lambda p, j: (p, 0, 0))],
        out_specs=[pl.BlockSpec((s, LANES), full), pl.BlockSpec((tq, LANES), tile), pl.BlockSpec((tq, LANES), tile),
                   pl.BlockSpec((None, 2, tq), lambda p, j: (p, 0, j))],
        out_shape=[SDS((s, dq_width), bf16), SDS((s, MAIN_WIDTH), bf16), SDS((s, MAIN_WIDTH), bf16), SDS((FOX_PAIRS, 2, s), f32)],
        scratch_shapes=[pltpu.VMEM((s, LANES), f32)],
        name=name, compiler_params=_cparams("parallel", "arbitrary"))(q, kv, kv, d_mixed, o, lse, c_row)


def adamw(w, g, m, v, name, tr=256):
    r, c = w.shape
    tr = min(tr, r)
    assert r % tr == 0, (name, r, tr)
    c1 = 1.0 / (1.0 - ADAM_B1 ** ADAM_STEP)
    c2 = 1.0 / (1.0 - ADAM_B2 ** ADAM_STEP)

    def body(w_ref, g_ref, m_ref, v_ref, d_ref, mo_ref, vo_ref):
        gv = g_ref[...]
        mn = ADAM_B1 * m_ref[...] + (1.0 - ADAM_B1) * gv
        vn = ADAM_B2 * v_ref[...] + (1.0 - ADAM_B2) * gv * gv
        mo_ref[...] = mn
        vo_ref[...] = vn
        d_ref[...] = -ADAM_LR * ((mn * c1) / (jnp.sqrt(vn * c2) + ADAM_EPS) + ADAM_WD * w_ref[...])

    spec = pl.BlockSpec((tr, c), lambda i: (i, 0))
    return pl.pallas_call(body, grid=(r // tr,), in_specs=[spec] * 4, out_specs=[spec] * 3, out_shape=[SDS((r, c), f32)] * 3,
                          name=name, compiler_params=_cparams("parallel"))(w, g, m, v)


def adamw_owned(w, parts, m, v, name, tr):
    nl, r, c = w.shape
    cp = parts[0].shape[2]
    assert r % tr == 0 and len(parts) == nl, (name, r, tr)
    c1 = 1.0 / (1.0 - ADAM_B1 ** ADAM_STEP)
    c2 = 1.0 / (1.0 - ADAM_B2 ** ADAM_STEP)

    def body(*refs):
        w_ref, p_refs, (m_ref, v_ref) = refs[0], refs[1:1 + nl], refs[1 + nl:3 + nl]
        g_ref, d_ref, mo_ref, vo_ref = refs[3 + nl:]
        layer = pl.program_id(0)

        def total(p_ref):
            acc = p_ref[0].astype(f32)
            for k in range(1, N_DEV):
                acc = acc + p_ref[k].astype(f32)
            return acc

        gv = total(p_refs[0])
        for l in range(1, nl):
            gv = jnp.where(layer == l, total(p_refs[l]), gv)
        gv = gv[:, :c]
        g_ref[...] = gv
        mn = ADAM_B1 * m_ref[...] + (1.0 - ADAM_B1) * gv
        vn = ADAM_B2 * v_ref[...] + (1.0 - ADAM_B2) * gv * gv
        mo_ref[...] = mn
        vo_ref[...] = vn
        d_ref[...] = -ADAM_LR * ((mn * c1) / (jnp.sqrt(vn * c2) + ADAM_EPS) + ADAM_WD * w_ref[...])

    spec = pl.BlockSpec((None, tr, c), lambda l, i: (l, i, 0))
    last = r // tr - 1

    def part_spec(mine):
        return pl.BlockSpec((N_DEV, tr, cp), lambda l, i: (0, jnp.where(l == mine, i, jnp.where(l < mine, 0, last)), 0))

    return pl.pallas_call(body, grid=(nl, r // tr), in_specs=[spec] + [part_spec(l) for l in range(nl)] + [spec, spec], out_specs=[spec] * 4,
                          out_shape=[SDS((nl, r, c), f32)] * 4, name=name,
                          compiler_params=_cparams("parallel", "parallel"))(w, *parts, m, v)


def sum_leading(x, name, out_dtype=f32, tr=None):
    n, r, c = x.shape
    tr = tr or r
    assert r % tr == 0

    def body(x_ref, o_ref):
        acc = x_ref[0].astype(f32)
        for k in range(1, n):
            acc = acc + x_ref[k].astype(f32)
        o_ref[...] = acc.astype(out_dtype)

    return pl.pallas_call(body, grid=(r // tr,), in_specs=[pl.BlockSpec((n, tr, c), lambda i: (0, i, 0))],
                          out_specs=pl.BlockSpec((tr, c), lambda i: (i, 0)), out_shape=SDS((r, c), out_dtype), name=name,
                          compiler_params=_cparams("parallel"))(x)


_ANY = pl.BlockSpec(memory_space=pl.ANY)
_DMA = pltpu.SemaphoreType.DMA


_HBM = pl.BlockSpec(memory_space=pltpu.HBM)
_SEM = pl.BlockSpec(memory_space=pltpu.SEMAPHORE)
_EFFECT = pltpu.SideEffectType.DATAFLOW_SIDE_EFFECTING
_FLIPS = [(0, 0, 1), (1, 0, 0), (0, 1, 0), (1, 1, 0), (1, 0, 1), (0, 1, 1), (1, 1, 1)]


def _me():
    return lax.axis_index("x"), lax.axis_index("y"), lax.axis_index("c")


def _peers():
    mx, my, mc = _me()
    return [(jnp.bitwise_xor(mx, fx), jnp.bitwise_xor(my, fy), jnp.bitwise_xor(mc, fc)) for fx, fy, fc in _FLIPS]


def _index(dev):
    return 4 * dev[0] + 2 * dev[1] + dev[2]


def _win(ref, axis, k, size, count=1):
    idx = [slice(None)] * len(ref.shape)
    idx[axis] = pl.ds(k * size, count * size)
    return ref.at[tuple(idx)]


def _hbm(a):
    return pltpu.with_memory_space_constraint(a, pltpu.HBM)


def _exchange_start(srcs, lands, copies_of, name):
    n = len(srcs)

    def body(*refs):
        src = refs[:n]
        send_sems, recv_sems, self_sems = refs[2 * n:2 * n + 3]
        land = refs[3 * n + 3:4 * n + 3]
        token = refs[4 * n + 3]
        me = _index(_me())
        for a in range(n):
            for s_ref, d_ref, peer in copies_of(a, src[a], land[a], me):
                if peer is None:
                    pltpu.make_async_copy(s_ref, d_ref, self_sems.at[a]).start()
                else:
                    pltpu.make_async_remote_copy(src_ref=s_ref, dst_ref=d_ref, send_sem=send_sems.at[a], recv_sem=recv_sems.at[a],
                                                 device_id=peer, device_id_type=MESH).start()
        token[...] = jnp.zeros_like(token)

    outs = pl.pallas_call(
        body, name=name,
        out_shape=(_DMA((n,)), _DMA((n,)), _DMA((n,)), *[pltpu.HBM(s.shape, s.dtype) for s in srcs],
                   *[pltpu.HBM(l.shape, l.dtype) for l in lands], SDS((8, LANES), f32)),
        in_specs=[_HBM] * (2 * n), out_specs=(_SEM, _SEM, _SEM, *[_HBM] * (2 * n), pl.BlockSpec(memory_space=pltpu.VMEM)),
        input_output_aliases={i: 3 + i for i in range(2 * n)},
        compiler_params=pltpu.CompilerParams(has_side_effects=_EFFECT),
    )(*[_hbm(s) for s in srcs], *[_hbm(lax.empty(l.shape, l.dtype)) for l in lands])
    return dict(sems=outs[:3], srcs=list(outs[3:3 + n]), lands=list(outs[3 + n:3 + 2 * n]), token=outs[3 + 2 * n])


def _exchange_wait(started, waits_of, after, name, which=None):
    which = list(range(len(started["srcs"]))) if which is None else which
    srcs, lands = [started["srcs"][a] for a in which], [started["lands"][a] for a in which]
    n = len(which)

    def body(*refs):
        src = refs[:n]
        land = refs[n:2 * n]
        send_sems, recv_sems, self_sems = refs[2 * n:2 * n + 3]
        me = _index(_me())
        for pos, a in enumerate(which):
            seven, (s_ref, d_ref) = waits_of(a, src[pos], land[pos], me)
            both = pltpu.make_async_remote_copy(src_ref=seven, dst_ref=seven, send_sem=send_sems.at[a], recv_sem=recv_sems.at[a],
                                                device_id=_me(), device_id_type=MESH)
            both.wait_send()
            both.wait_recv()
            pltpu.make_async_copy(s_ref, d_ref, self_sems.at[a]).wait()

    outs = pl.pallas_call(
        body, name=name, out_shape=tuple(pltpu.HBM(t.shape, t.dtype) for t in srcs + lands),
        in_specs=[_HBM] * (2 * n) + [_SEM] * 3 + [_ANY], out_specs=tuple([_HBM] * (2 * n)),
        input_output_aliases={i: i for i in range(2 * n)},
        compiler_params=pltpu.CompilerParams(has_side_effects=_EFFECT),
    )(*srcs, *lands, *started["sems"], after)
    return list(outs[n:])


def gather_start(locs, axes, name):
    lands = [SDS(tuple(N_DEV * d if i == ax else d for i, d in enumerate(l.shape)), l.dtype) for l, ax in zip(locs, axes)]

    def copies_of(a, src, land, me):
        mine = _win(land, axes[a], me, src.shape[axes[a]])
        return [(src, mine, peer) for peer in _peers()] + [(src, mine, None)]

    return _exchange_start(locs, lands, copies_of, name)


def gather_wait(started, axes, after, name, which=None):
    def waits_of(a, src, land, me):
        size = src.shape[axes[a]]
        return _win(land, axes[a], 0, size, N_DEV - 1), (src, _win(land, axes[a], me, size))

    return _exchange_wait(started, waits_of, after, name, which)


def scatter_start(grads, axes, name):
    lands = [SDS((N_DEV,) + tuple(d // N_DEV if i == ax else d for i, d in enumerate(g.shape)), g.dtype) for g, ax in zip(grads, axes)]

    def copies_of(a, src, land, me):
        size = src.shape[axes[a]] // N_DEV
        out = [(_win(src, axes[a], _index(peer), size), land.at[me], peer) for peer in _peers()]
        return out + [(_win(src, axes[a], me, size), land.at[me], None)]

    return _exchange_start(grads, lands, copies_of, name)


def scatter_wait(started, axes, after, name):
    def waits_of(a, src, land, me):
        size = src.shape[axes[a]] // N_DEV
        return land.at[pl.ds(0, N_DEV - 1)], (_win(src, axes[a], me, size), land.at[me])

    return _exchange_wait(started, waits_of, after, name)


def _row_tile(rows, cap=512):
    return max(t for t in range(8, min(rows, cap) + 1, 8) if rows % t == 0)


_SMALL = [
    ("ln_mix_pre", (2, 1024)), ("ln_mix_post", (2, 1024)), ("ln_ffn_pre", (2, 1024)), ("ln_ffn_post", (2, 1024)),
    ("ln_mem", (2, 1024)), ("w_spatial", (1, 6, 128, 128)), ("b_spatial", (1, 6, 128)), ("ln_shared", (1024,)),
    ("b_forget", (12,)), ("ln_v_g", (1, 768)), ("ln_v_b", (1, 768)),
]
_SMALL_TILE = 8 * LANES


def _small_rows(shape):
    return -(-math.prod(shape) // _SMALL_TILE) * 8


def _pack_small(vals, shapes):
    parts = []
    for name, shape in shapes:
        flat = vals[name].reshape(-1).astype(f32)
        rows = _small_rows(shape)
        parts.append(jnp.pad(flat, (0, rows * LANES - flat.shape[0])).reshape(rows, LANES))
    return jnp.concatenate(parts, axis=0)


def _unpack_small(buf, shapes):
    out = {}
    lo = 0
    for name, shape in shapes:
        rows = _small_rows(shape)
        out[name] = buf[lo:lo + rows].reshape(-1)[:math.prod(shape)].reshape(shape)
        lo += rows
    return out


def kernel(x, mem, ln_mix_pre, ln_mix_post, ln_ffn_pre, ln_ffn_post, ln_mem, w_mem_kv, w_out, w_ffn_gate, w_ffn_up, w_ffn_down, w_in_a, w_spatial, b_spatial, ln_v_g, ln_v_b, ln_shared, w_shared_kv, b_forget, w_in_b, loss_target, m_ln_mix_pre, m_ln_mix_post, m_ln_ffn_pre, m_ln_ffn_post, m_ln_mem, m_w_mem_kv, m_w_out, m_w_ffn_gate, m_w_ffn_up, m_w_ffn_down, m_w_in_a, m_w_spatial, m_b_spatial, m_ln_v_g, m_ln_v_b, m_ln_shared, m_w_shared_kv, m_b_forget, m_w_in_b, v_ln_mix_pre, v_ln_mix_post, v_ln_ffn_pre, v_ln_ffn_post, v_ln_mem, v_w_mem_kv, v_w_out, v_w_ffn_gate, v_w_ffn_up, v_w_ffn_down, v_w_in_a, v_w_spatial, v_b_spatial, v_ln_v_g, v_ln_v_b, v_ln_shared, v_w_shared_kv, v_b_forget, v_w_in_b):
    weights = dict(ln_mix_pre=ln_mix_pre, ln_mix_post=ln_mix_post, ln_ffn_pre=ln_ffn_pre, ln_ffn_post=ln_ffn_post, ln_mem=ln_mem,
                   w_mem_kv=w_mem_kv, w_out=w_out, w_ffn_gate=w_ffn_gate, w_ffn_up=w_ffn_up, w_ffn_down=w_ffn_down, w_in_a=w_in_a,
                   w_spatial=w_spatial, b_spatial=b_spatial, ln_v_g=ln_v_g, ln_v_b=ln_v_b, ln_shared=ln_shared,
                   w_shared_kv=w_shared_kv, b_forget=b_forget, w_in_b=w_in_b)
    mom_m = dict(ln_mix_pre=m_ln_mix_pre, ln_mix_post=m_ln_mix_post, ln_ffn_pre=m_ln_ffn_pre, ln_ffn_post=m_ln_ffn_post, ln_mem=m_ln_mem,
                 w_mem_kv=m_w_mem_kv, w_out=m_w_out, w_ffn_gate=m_w_ffn_gate, w_ffn_up=m_w_ffn_up, w_ffn_down=m_w_ffn_down, w_in_a=m_w_in_a,
                 w_spatial=m_w_spatial, b_spatial=m_b_spatial, ln_v_g=m_ln_v_g, ln_v_b=m_ln_v_b, ln_shared=m_ln_shared,
                 w_shared_kv=m_w_shared_kv, b_forget=m_b_forget, w_in_b=m_w_in_b)
    mom_v = dict(ln_mix_pre=v_ln_mix_pre, ln_mix_post=v_ln_mix_post, ln_ffn_pre=v_ln_ffn_pre, ln_ffn_post=v_ln_ffn_post, ln_mem=v_ln_mem,
                 w_mem_kv=v_w_mem_kv, w_out=v_w_out, w_ffn_gate=v_w_ffn_gate, w_ffn_up=v_w_ffn_up, w_ffn_down=v_w_ffn_down, w_in_a=v_w_in_a,
                 w_spatial=v_w_spatial, b_spatial=v_b_spatial, ln_v_g=v_ln_v_g, ln_v_b=v_ln_v_b, ln_shared=v_ln_shared,
                 w_shared_kv=v_w_shared_kv, b_forget=v_b_forget, w_in_b=v_w_in_b)
    names = list(weights)
    mx, my, mc = lax.axis_index("x"), lax.axis_index("y"), lax.axis_index("c")
    me = 4 * mx + 2 * my + mc

    h0 = x[0]
    mem0 = mem[0]
    tgt = loss_target[0]
    seq = h0.shape[0]

    vec = lambda a: a.reshape(1, -1)
    pad_to = lambda a, axis, size: jnp.pad(a, [(0, size - a.shape[i] if i == axis else 0) for i in range(a.ndim)])

    def after(tok, a):
        return a + tok[0, 0].astype(a.dtype)

    lnv_loc = pad_to(jnp.concatenate([ln_v_g, ln_v_b], axis=0), 0, 8)
    st_a = gather_start([w_in_a.astype(bf16), pad_to(lnv_loc, 1, LANES)[None]], [0, 0], "gather_a_start")
    mix_locs = lambda l, tok: [after(tok, w_mem_kv[l]).astype(bf16), w_out[l].astype(bf16)]

    def ffn_gather_start(l, tok):
        gate_up = gather_start([pad_to(after(tok, w_ffn_gate[l]).astype(bf16), 1, FF_SHARD_PAD),
                                pad_to(w_ffn_up[l].astype(bf16), 1, FF_SHARD_PAD)], [1, 1], f"gather_gate_up{l}_start")
        down = gather_start([pad_to(after(gate_up["token"], w_ffn_down[l]).astype(bf16), 0, FF_SHARD_PAD)], [0], f"gather_down{l}_start")
        return gate_up, down

    st_b = [gather_start(mix_locs(0, st_a["token"]), [0, 0], "gather_b0_start"), None]
    st_c = ffn_gather_start(0, st_b[0]["token"])
    st_d = gather_start([after(st_c[1]["token"], w_in_b[0]).astype(bf16), pad_to(w_shared_kv.astype(bf16), 1, KV_PAD)], [0, 0],
                        "gather_d_start")
    st_b[1] = gather_start(mix_locs(1, st_d["token"]), [0, 0], "gather_b1_start")
    st_e = ffn_gather_start(1, st_b[1]["token"])
    ws = w_spatial[0].astype(bf16)
    ws_t = ws.transpose(0, 2, 1)
    bs_t = b_spatial[0].T

    (a0,) = rms_fwd(h0, [after(st_e[1]["token"], vec(ln_mix_pre[0]))], "a0_norm")
    w_in_a8, lnv8 = gather_wait(st_a, [0, 0], a0, "gather_a_wait")
    w_in_a_full = w_in_a8.transpose(1, 0, 2).reshape(D_MODEL, -1)
    lnv_g = lnv8[:, 0, :MAIN_WIDTH // N_DEV].reshape(1, MAIN_WIDTH)
    lnv_b = lnv8[:, 1, :MAIN_WIDTH // N_DEV].reshape(1, MAIN_WIDTH)
    proj0 = mm(a0, w_in_a_full, "proj0", tn=896)
    main0 = gmlp_fwd(proj0, ws, bs_t, lnv_g, lnv_b, "gmlp_fwd", out_width=D_MODEL)
    w_mkv, w_o = [None, None], [None, None]
    w_mkv[0], w_o[0] = gather_wait(st_b[0], [0, 0], main0, "gather_b0_wait")
    (memn0,) = rms_fwd(mem0, [vec(ln_mem[0])], "mem0_norm")
    kvm0 = mm(memn0, w_mkv[0], "kvm0")
    mixed0 = mem_attn_fwd(proj0, 2 * MAIN_WIDTH // MEM_WIDTH, kvm0, main0, "mem_attn0")
    y1_0, hmid0, f0 = mm_resnorm(mixed0, w_o[0], h0, vec(ln_mix_post[0]), [vec(ln_ffn_pre[0])], "mix_out0")
    w_g0, w_u0 = gather_wait(st_c[0], [1, 1], f0, "gather_gate_up0_wait")
    gu0, act0 = ffn_up(f0, w_g0, w_u0, "ffn_up0")
    (w_d0,) = gather_wait(st_c[1], [0], act0, "gather_down0_wait")
    y2_0, h1, a1, sin1 = mm_resnorm(act0, w_d0, hmid0, vec(ln_ffn_post[0]), [vec(ln_mix_pre[1]), vec(ln_shared)], "ffn_down0")

    w_inb, w_kv = gather_wait(st_d, [0, 0], sin1, "gather_d_wait")
    kvb = mm(sin1, w_kv, "kv_shared", out_dtype=bf16, tn=MAIN_WIDTH, ncols=2 * MAIN_WIDTH)
    zf = mm(sin1, w_kv, "forget_logits", tn=256, col0=2 * MAIN_WIDTH, ncols=256)
    qb = mm(a1, w_inb, "proj1", out_dtype=bf16)
    z_t = jnp.pad(zf[:, :FOX_HEADS].T, ((0, 16 - FOX_HEADS), (0, 0)))
    bf_col = jnp.pad(b_forget, (0, 16 - FOX_HEADS)).reshape(16, 1)
    c_t = fgate_fwd(z_t, bf_col, "fgate_fwd")
    c_row = c_t[:FOX_HEADS].reshape(FOX_PAIRS, 2, seq)
    main1, lse, main1_b = fox_fwd(qb, kvb, c_row, "fox_fwd", out_width=D_MODEL)
    w_mkv[1], w_o[1] = gather_wait(st_b[1], [0, 0], main1, "gather_b1_wait")
    (memn1,) = rms_fwd(mem0, [vec(ln_mem[1])], "mem1_norm")
    kvm1 = mm(memn1, w_mkv[1], "kvm1")
    mixed1 = mem_attn_fwd(qb, MAIN_WIDTH // MEM_WIDTH, kvm1, main1_b, "mem_attn1")
    y1_1, hmid1, f1 = mm_resnorm(mixed1, w_o[1], h1, vec(ln_mix_post[1]), [vec(ln_ffn_pre[1])], "mix_out1")
    w_g1, w_u1 = gather_wait(st_e[0], [1, 1], f1, "gather_gate_up1_wait")
    gu1, act1 = ffn_up(f1, w_g1, w_u1, "ffn_up1")
    (w_d1,) = gather_wait(st_e[1], [0], act1, "gather_down1_wait")
    dh, d_y2_1, dg_fpost1, loss_tile = mm_resnorm_loss(act1, w_d1, hmid1, vec(ln_ffn_post[1]), tgt, "ffn_down1_loss")
    ffn_w = [(w_g0, w_u0, w_d0), (w_g1, w_u1, w_d1)]

    small = {}

    def ffn_backward(layer, dh_out, d_y2, hmid, f, gu, act, y1):
        w_g, w_u, w_d = ffn_w[layer]
        dw_down = mm_tn(act, d_y2, f"dw_down{layer}")
        rs_down = scatter_start([dw_down], [0], f"scatter_down{layer}_start")
        d_g, d_u = ffn_act_grad(d_y2, w_d, gu, f"ffn_act_grad{layer}")
        dw_g = mm_tn(f, d_g, f"dw_gate{layer}", dep=rs_down["token"])
        dw_u = mm_tn(f, d_u, f"dw_up{layer}")
        rs_gate_up = scatter_start([dw_g, dw_u], [1, 1], f"scatter_gate_up{layer}_start")
        dh_mid, d_y1, dg_fpre, dg_mpost = ffn_in_grad(d_g, d_u, w_g, w_u, hmid, dh_out, after(rs_gate_up["token"], vec(ln_ffn_pre[layer])),
                                                      y1, vec(ln_mix_post[layer]), f"ffn_in_grad{layer}")
        return dh_mid, d_y1, dg_fpre, dg_mpost, (rs_down, rs_gate_up)

    def mix_out_backward(layer, d_y1, mixed):
        dw_out = mm_tn(mixed, d_y1, f"dw_out{layer}")
        d_mixed = mm(d_y1, w_o[layer], f"d_mixed{layer}", trans_b=True)
        return d_mixed, dw_out

    def mem_backward(layer, q_src, q_block, kvm, memn, d_mixed, into):
        d_qm, d_kvm = mem_attn_bwd(q_src, q_block, kvm, d_mixed, into, f"mem_attn_bwd{layer}")
        d_kvm_b = d_kvm.astype(bf16)
        dw_mkv = mm_tn(memn, d_kvm_b, f"dw_mem_kv{layer}")
        d_memn = mm(d_kvm_b, w_mkv[layer], f"d_memn{layer}", trans_b=True)
        _, dg_mem = rms_bwd(mem0, vec(ln_mem[layer]), d_memn, None, bf16, f"mem_norm_bwd{layer}")
        return d_qm, dw_mkv, dg_mem


    dh_mid1, d_y1_1, dg_fpre1, dg_mpost1, rs_ffn1 = ffn_backward(1, dh, d_y2_1, hmid1, f1, gu1, act1, y1_1)
    d_mixed1, dw_out1 = mix_out_backward(1, d_y1_1, mixed1)
    dq_b, dk, dv, dc = fox_bwd(qb, kvb, d_mixed1, main1, lse, c_row, "fox_bwd", dq_width=D_MODEL)
    d_proj1, dw_mkv1, dg_mem1 = mem_backward(1, qb, MAIN_WIDTH // MEM_WIDTH, kvm1, memn1, d_mixed1, dq_b)
    rs_mix1 = scatter_start([dw_out1, dw_mkv1], [0, 0], "scatter_mix1_start")
    dc_t = jnp.pad(dc.reshape(FOX_HEADS, seq), ((0, 16 - FOX_HEADS), (0, 0)))
    dz_t, db_f = fgate_bwd(dc_t, z_t, bf_col, "fgate_bwd")
    d_kvf = jnp.concatenate([dk, dv, jnp.pad(dz_t[:FOX_HEADS].T.astype(bf16), ((0, 0), (0, KV_PAD - KV_WIDTH)))], axis=-1)
    dw_in_b = mm_tn(a1, d_proj1, "dw_in_b", dep=rs_mix1["token"])
    dw_kv = mm_tn(sin1, d_kvf, "dw_kv", tn=896)
    rs_2 = scatter_start([dw_in_b, dw_kv], [0, 0], "scatter_shared_start")
    dh1, (dg_pre1, dg_shared), d_y2_0, dg_fpost0 = proj_in_grad(
        [(d_proj1, w_inb, vec(ln_mix_pre[1])), (d_kvf, w_kv, vec(ln_shared))], h1, dh_mid1, "in_grad1", dep=rs_2["token"],
        below=(y2_0, vec(ln_ffn_post[0])))

    dh_mid0, d_y1_0, dg_fpre0, dg_mpost0, rs_ffn0 = ffn_backward(0, dh1, d_y2_0, hmid0, f0, gu0, act0, y1_0)
    d_mixed0, dw_out0 = mix_out_backward(0, d_y1_0, mixed0)
    d_uv, dw_s, db_s, dg_lnv, db_lnv = gmlp_bwd(proj0, d_mixed0, ws, ws_t, bs_t, lnv_g, lnv_b, "gmlp_bwd", out_width=w_in_a_full.shape[1])
    d_proj0, dw_mkv0, dg_mem0 = mem_backward(0, proj0, 2 * MAIN_WIDTH // MEM_WIDTH, kvm0, memn0, d_mixed0, d_uv)
    rs_mix0 = scatter_start([dw_out0, dw_mkv0], [0, 0], "scatter_mix0_start")

    small["ln_mix_pre"] = jnp.concatenate([jnp.zeros_like(dg_pre1), dg_pre1], axis=0)
    small["ln_mix_post"] = jnp.concatenate([dg_mpost0, dg_mpost1], axis=0)
    small["ln_ffn_pre"] = jnp.concatenate([dg_fpre0, dg_fpre1], axis=0)
    small["ln_ffn_post"] = jnp.concatenate([dg_fpost0, dg_fpost1], axis=0)
    small["ln_mem"] = jnp.concatenate([dg_mem0, dg_mem1], axis=0)
    small["w_spatial"] = dw_s[None]
    small["b_spatial"] = db_s[:, :A_GROUPS].T[None]
    small["ln_shared"] = dg_shared[0]
    small["b_forget"] = db_f[:FOX_HEADS, 0]
    small["ln_v_g"] = dg_lnv
    small["ln_v_b"] = db_lnv
    small_rows = jnp.concatenate([_pack_small(small, _SMALL), after(rs_mix0["token"], loss_tile)], axis=0)
    st_small = gather_start([small_rows[None]], [0], "gather_small_grads_start")
    dw_in_a_t = mm_tn(d_proj0, a0, "dw_in_a", tk=896, dep=st_small["token"])
    rs_in_a = scatter_start([dw_in_a_t], [0], "scatter_in_a_start")
    grad_x, (dg_pre0,) = proj_in_grad([(d_proj0, w_in_a_full, vec(ln_mix_pre[0]))], h0, dh_mid0, "in_grad0", dep=rs_in_a["token"])
    st_last = gather_start([dg_pre0.reshape(1, 8, LANES)], [0], "gather_last_grad_start")

    (p_down1,) = scatter_wait(rs_ffn1[0], [0], after(st_last["token"], grad_x[:8, :LANES]), "scatter_down1_wait")
    p_gate1, p_up1 = scatter_wait(rs_ffn1[1], [1, 1], p_down1, "scatter_gate_up1_wait")
    p_out1, p_mkv1 = scatter_wait(rs_mix1, [0, 0], p_gate1, "scatter_mix1_wait")
    p_in_b, p_kv = scatter_wait(rs_2, [0, 0], p_out1, "scatter_shared_wait")
    (p_down0,) = scatter_wait(rs_ffn0[0], [0], p_in_b, "scatter_down0_wait")
    p_gate0, p_up0 = scatter_wait(rs_ffn0[1], [1, 1], p_down0, "scatter_gate_up0_wait")
    p_out0, p_mkv0 = scatter_wait(rs_mix0, [0, 0], p_gate0, "scatter_mix0_wait")
    owned_parts = dict(w_ffn_gate=[p_gate0, p_gate1], w_ffn_up=[p_up0, p_up1], w_ffn_down=[p_down0, p_down1], w_out=[p_out0, p_out1],
                       w_mem_kv=[p_mkv0, p_mkv1], w_in_b=[p_in_b], w_shared_kv=[p_kv])

    grad_w, delta, new_m, new_v = {}, {}, {}, {}
    for n, parts in owned_parts.items():
        shape = weights[n].shape
        three_d = shape if len(shape) == 3 else (1,) + shape
        outs = adamw_owned(weights[n].reshape(three_d), parts, mom_m[n].reshape(three_d), mom_v[n].reshape(three_d), f"adamw_{n}",
                           tr=_row_tile(three_d[1]))
        grad_w[n], delta[n], new_m[n], new_v[n] = (t.reshape(shape) for t in outs)
    (p_in_a,) = scatter_wait(rs_in_a, [0], delta["w_shared_kv"], "scatter_in_a_wait")
    g_in_a = sum_leading(p_in_a, "sum_in_a", tr=_row_tile(p_in_a.shape[1]))
    (small_all,) = gather_wait(st_small, [0], g_in_a, "gather_small_grads_wait")
    (last_all,) = gather_wait(st_last, [0], small_all, "gather_last_grad_wait")
    small_sum = sum_leading(small_all, "sum_small_grads")
    loss = small_sum[small_rows.shape[0] - 1, 0]
    g_small = _unpack_small(small_sum, _SMALL)
    g_small["ln_mix_pre"] = jnp.concatenate([sum_leading(last_all, "sum_last_grad").reshape(1, D_MODEL), g_small["ln_mix_pre"][1:]], axis=0)
    shard = MAIN_WIDTH // N_DEV
    for n in ("ln_v_g", "ln_v_b"):
        g_small[n] = lax.dynamic_slice_in_dim(g_small[n], me * shard, shard, axis=1)
    grad_w.update(g_small)
    grad_w["w_in_a"] = g_in_a.T[None]
    d_, m_, v_ = adamw(w_in_a[0], g_in_a.T, m_w_in_a[0], v_w_in_a[0], "adamw_w_in_a", tr=512)
    delta["w_in_a"], new_m["w_in_a"], new_v["w_in_a"] = d_[None], m_[None], v_[None]
    small_local_shapes = [(n, tuple(weights[n].shape)) for n, _ in _SMALL]
    packed = [_pack_small(src, small_local_shapes) for src in (weights, grad_w, mom_m, mom_v)]
    outs = adamw(*packed, "adamw_small", tr=packed[0].shape[0])
    for dst, buf in zip((delta, new_m, new_v), outs):
        dst.update(_unpack_small(buf, small_local_shapes))

    return (loss, grad_x[None], *[grad_w[n] for n in names], *[delta[n] for n in names],
            *[new_m[n] for n in names], *[new_v[n] for n in names])
```

```python
import functools
import math

import jax
import jax.numpy as jnp
from jax import lax
from jax.experimental import pallas as pl
from jax.experimental.pallas import tpu as pltpu

f32 = jnp.float32
bf16 = jnp.bfloat16
SDS = jax.ShapeDtypeStruct

D_MODEL = 1024
MAIN_WIDTH = 768
MEM_WIDTH = 256
HEAD_DIM = 64
MEM_HEADS = 4
FOX_HEADS = 12
FOX_PAIRS = FOX_HEADS // 2
CHUNK = 128
A_GROUPS = 6
FF_SHARD_PAD = 384
KV_WIDTH = 2 * MAIN_WIDTH + FOX_HEADS
KV_PAD = 1792
RMS_EPS = 1e-6
LN_EPS = 1e-5
ATT_SCALE = HEAD_DIM ** -0.5
ADAM_LR, ADAM_B1, ADAM_B2, ADAM_EPS, ADAM_WD, ADAM_STEP = 0.001, 0.9, 0.999, 1e-08, 0.01, 10
N_DEV = 8
MESH = pl.DeviceIdType.MESH
V7X_VMEM_LIMIT = 56 * 1024 * 1024
LANES = 128


def _cparams(*sem):
    return pltpu.CompilerParams(dimension_semantics=sem or None, vmem_limit_bytes=V7X_VMEM_LIMIT)


def _dot(a, b):
    return jnp.dot(a, b, preferred_element_type=f32)


def _dot_nt(a, b):
    return lax.dot_general(a, b, (((1,), (1,)), ((), ())), preferred_element_type=f32)


def _dot_tn(a, b):
    return lax.dot_general(a, b, (((0,), (0,)), ((), ())), preferred_element_type=f32)


def _gelu(x):
    k = math.sqrt(2.0 / math.pi)
    t = jnp.tanh(k * (x + 0.044715 * x * x * x))
    return 0.5 * x * (1.0 + t), t


def _gelu_grad(x, t):
    k = math.sqrt(2.0 / math.pi)
    return 0.5 * (1.0 + t) + 0.5 * x * (1.0 - t * t) * k * (1.0 + 3.0 * 0.044715 * x * x)


def _sigmoid(x):
    return 1.0 / (1.0 + jnp.exp(-x))


def rms_fwd(x, gains, name, tm=512):
    m, d = x.shape
    tm = min(tm, m)
    n = len(gains)

    def body(x_ref, *refs):
        xv = x_ref[...]
        y = xv * lax.rsqrt(jnp.sum(xv * xv, axis=-1, keepdims=True) * (1.0 / d) + RMS_EPS)
        for g_ref, o_ref in zip(refs[:n], refs[n:]):
            o_ref[...] = (y * g_ref[...]).astype(bf16)

    row = pl.BlockSpec((tm, d), lambda i: (i, 0))
    vec = pl.BlockSpec((1, d), lambda i: (0, 0))
    return pl.pallas_call(body, grid=(m // tm,), in_specs=[row] + [vec] * n, out_specs=[row] * n,
                          out_shape=[SDS((m, d), bf16)] * n, name=name, compiler_params=_cparams("parallel"))(x, *gains)


def rms_bwd(x, g, dy, add, out_dtype, name, tm=512):
    m, d = x.shape
    tm = min(tm, m)
    has_add = add is not None

    def body(x_ref, g_ref, dy_ref, *refs):
        dx_ref, dg_ref = refs[-2], refs[-1]
        xv = x_ref[...]
        dyv = dy_ref[...].astype(f32)
        r = lax.rsqrt(jnp.sum(xv * xv, axis=-1, keepdims=True) * (1.0 / d) + RMS_EPS)
        xn = xv * r
        dyg = dyv * g_ref[...]
        dx = r * (dyg - xn * (jnp.sum(dyg * xn, axis=-1, keepdims=True) * (1.0 / d)))
        if has_add:
            dx = dx + refs[0][...]
        dx_ref[...] = dx.astype(out_dtype)

        @pl.when(pl.program_id(0) == 0)
        def _():
            dg_ref[...] = jnp.zeros_like(dg_ref)

        dg_ref[...] += jnp.sum(dyv * xn, axis=0, keepdims=True)

    row = pl.BlockSpec((tm, d), lambda i: (i, 0))
    vec = pl.BlockSpec((1, d), lambda i: (0, 0))
    ins = [x, g, dy] + ([add] if has_add else [])
    return pl.pallas_call(body, grid=(m // tm,), in_specs=[row, vec, row] + ([row] if has_add else []),
                          out_specs=[row, vec], out_shape=[SDS((m, d), out_dtype), SDS((1, d), f32)], name=name,
                          compiler_params=_cparams("arbitrary"))(*ins)


def mm(a, b, name, trans_b=False, out_dtype=f32, tm=1024, tn=1024, col0=0, ncols=None, dep=None):
    m, k = a.shape
    n_all = b.shape[0] if trans_b else b.shape[1]
    n = n_all if ncols is None else ncols
    tm, tn = min(tm, m), min(tn, n)
    assert m % tm == 0 and n % tn == 0 and col0 % tn == 0 and not (trans_b and col0), (name, m, n, tm, tn)
    jb = col0 // tn

    def body(a_ref, b_ref, *rest):
        r = _dot_nt(a_ref[...], b_ref[...]) if trans_b else _dot(a_ref[...], b_ref[...])
        rest[-1][...] = r.astype(out_dtype)

    if trans_b:
        b_spec = pl.BlockSpec((tn, k), lambda j, i: (j, 0))
    else:
        b_spec = pl.BlockSpec((k, tn), lambda j, i: (0, jb + j))
    deps = [] if dep is None else [dep]
    dep_specs = [pl.BlockSpec((8, LANES), lambda j, i: (0, 0))] * len(deps)
    return pl.pallas_call(body, grid=(n // tn, m // tm), in_specs=[pl.BlockSpec((tm, k), lambda j, i: (i, 0)), b_spec] + dep_specs,
                          out_specs=pl.BlockSpec((tm, tn), lambda j, i: (i, j)), out_shape=SDS((m, n), out_dtype),
                          name=name, compiler_params=_cparams("parallel", "parallel"))(a, b, *deps)


def mm_tn(a, g, name, tk=1024, tn=1024, out_dtype=bf16, dep=None):
    s, k = a.shape
    n = g.shape[1]
    tk, tn = min(tk, k), min(tn, n)
    assert k % tk == 0 and n % tn == 0, (name, k, n, tk, tn)

    def body(a_ref, g_ref, *rest):
        rest[-1][...] = _dot_tn(a_ref[...], g_ref[...]).astype(out_dtype)

    deps = [] if dep is None else [dep]
    dep_specs = [pl.BlockSpec((8, LANES), lambda i, j: (0, 0))] * len(deps)
    return pl.pallas_call(body, grid=(k // tk, n // tn),
                          in_specs=[pl.BlockSpec((s, tk), lambda i, j: (0, i)), pl.BlockSpec((s, tn), lambda i, j: (0, j))] + dep_specs,
                          out_specs=pl.BlockSpec((tk, tn), lambda i, j: (i, j)), out_shape=SDS((k, n), out_dtype), name=name,
                          compiler_params=_cparams("parallel", "parallel"))(a, g, *deps)


def _resident(shape, index_map):
    return pl.BlockSpec(shape, index_map, pipeline_mode=pl.Buffered(1))


def _rms(xv):
    return xv * lax.rsqrt(jnp.sum(xv * xv, axis=-1, keepdims=True) * (1.0 / xv.shape[-1]) + RMS_EPS)


def _rms_bwd_math(xv, g, dy):
    d = xv.shape[-1]
    r = lax.rsqrt(jnp.sum(xv * xv, axis=-1, keepdims=True) * (1.0 / d) + RMS_EPS)
    xn = xv * r
    dyg = dy * g
    dx = r * (dyg - xn * (jnp.sum(dyg * xn, axis=-1, keepdims=True) * (1.0 / d)))
    return dx, jnp.sum(dy * xn, axis=0, keepdims=True)


SUB_ROWS = 512


def mm_resnorm(a, b, h, g_post, gains, name, tm=512):
    m, k = a.shape
    d = b.shape[1]
    n = len(gains)

    def body(a_ref, b_ref, h_ref, gp_ref, *refs):
        for r in range(tm // SUB_ROWS):
            rows = slice(r * SUB_ROWS, (r + 1) * SUB_ROWS)
            y = _dot(a_ref[rows, :], b_ref[...])
            refs[n][rows, :] = y
            hn = h_ref[rows, :] + _rms(y) * gp_ref[...]
            refs[n + 1][rows, :] = hn
            if n:
                z = _rms(hn)
                for g_ref, o_ref in zip(refs[:n], refs[n + 2:]):
                    o_ref[rows, :] = (z * g_ref[...]).astype(bf16)

    row = pl.BlockSpec((tm, d), lambda i: (i, 0))
    vec = pl.BlockSpec((1, d), lambda i: (0, 0))
    return pl.pallas_call(body, grid=(m // tm,),
                          in_specs=[pl.BlockSpec((tm, k), lambda i: (i, 0)), _resident((k, d), lambda i: (0, 0)), row, vec] + [vec] * n,
                          out_specs=[row] * (n + 2), out_shape=[SDS((m, d), f32)] * 2 + [SDS((m, d), bf16)] * n, name=name,
                          compiler_params=_cparams("parallel"))(a, b, h, g_post, *gains)


def mm_resnorm_loss(a, b, h, g_post, tgt, name, tm=512):
    m, k = a.shape
    d = b.shape[1]

    def body(a_ref, b_ref, h_ref, gp_ref, t_ref, dh_ref, dy_ref, dg_ref, l_ref):
        @pl.when(pl.program_id(0) == 0)
        def _():
            dg_ref[...] = jnp.zeros_like(dg_ref)
            l_ref[...] = jnp.zeros_like(l_ref)

        y = _dot(a_ref[...], b_ref[...])
        e = h_ref[...] + _rms(y) * gp_ref[...] - t_ref[...]
        dh = e * (1.0 / d)
        dh_ref[...] = dh
        part = jnp.sum(jnp.sum(e * e, axis=-1, keepdims=True), axis=0, keepdims=True) * (0.5 / d)
        l_ref[...] += jnp.broadcast_to(part, l_ref.shape)
        dy, dg = _rms_bwd_math(y, gp_ref[...], dh)
        dy_ref[...] = dy.astype(bf16)
        dg_ref[...] += dg

    row = pl.BlockSpec((tm, d), lambda i: (i, 0))
    vec = pl.BlockSpec((1, d), lambda i: (0, 0))
    return pl.pallas_call(body, grid=(m // tm,),
                          in_specs=[pl.BlockSpec((tm, k), lambda i: (i, 0)), _resident((k, d), lambda i: (0, 0)), row, vec, row],
                          out_specs=[row, row, vec, pl.BlockSpec((8, LANES), lambda i: (0, 0))],
                          out_shape=[SDS((m, d), f32), SDS((m, d), bf16), SDS((1, d), f32), SDS((8, LANES), f32)], name=name,
                          compiler_params=_cparams("arbitrary"))(a, b, h, g_post, tgt)


def ffn_act_grad(d_y2, w_d, factors, name, tm=1024, tn=1536):
    s, d = d_y2.shape
    ff = w_d.shape[0]
    nb = ff // tn

    def body(a_ref, b_ref, g_ref, u_ref, dg_ref, du_ref):
        av = a_ref[...]
        tc = 256
        for c in range(tn // tc):
            cols = slice(c * tc, (c + 1) * tc)
            da = _dot_nt(av, b_ref[cols, :])
            dg_ref[:, cols] = (da * g_ref[:, cols].astype(f32)).astype(bf16)
            du_ref[:, cols] = (da * u_ref[:, cols].astype(f32)).astype(bf16)

    tile = pl.BlockSpec((tm, tn), lambda j, i: (i, j))
    return pl.pallas_call(body, grid=(nb, s // tm),
                          in_specs=[pl.BlockSpec((tm, d), lambda j, i: (i, 0)), pl.BlockSpec((tn, d), lambda j, i: (j, 0)), tile,
                                    pl.BlockSpec((tm, tn), lambda j, i: (i, nb + j))],
                          out_specs=[tile, tile], out_shape=[SDS((s, ff), bf16)] * 2, name=name,
                          compiler_params=_cparams("parallel", "parallel"))(d_y2, w_d, factors, factors)


def ffn_in_grad(d_g, d_u, w_g, w_u, hmid, dh_out, g_pre, y1, g_post, name, tm=512):
    s, ff = d_g.shape
    d = w_g.shape[0]

    def body(dg_ref, du_ref, wg_ref, wu_ref, hm_ref, dho_ref, gpre_ref, y1_ref, gpost_ref, dhm_ref, dy1_ref, dgpre_ref, dgpost_ref):
        @pl.when(pl.program_id(0) == 0)
        def _():
            dgpre_ref[...] = jnp.zeros_like(dgpre_ref)
            dgpost_ref[...] = jnp.zeros_like(dgpost_ref)

        for r in range(tm // SUB_ROWS):
            rows = slice(r * SUB_ROWS, (r + 1) * SUB_ROWS)
            d_f = _dot_nt(dg_ref[rows, :], wg_ref[...]) + _dot_nt(du_ref[rows, :], wu_ref[...])
            dx, dg1 = _rms_bwd_math(hm_ref[rows, :], gpre_ref[...], d_f)
            dh_mid = dho_ref[rows, :] + dx
            dhm_ref[rows, :] = dh_mid
            dgpre_ref[...] += dg1
            dy1, dg2 = _rms_bwd_math(y1_ref[rows, :], gpost_ref[...], dh_mid)
            dy1_ref[rows, :] = dy1.astype(bf16)
            dgpost_ref[...] += dg2

    row = pl.BlockSpec((tm, d), lambda i: (i, 0))
    vec = pl.BlockSpec((1, d), lambda i: (0, 0))
    wide = pl.BlockSpec((tm, ff), lambda i: (i, 0))
    w_spec = _resident((d, ff), lambda i: (0, 0))
    return pl.pallas_call(body, grid=(s // tm,), in_specs=[wide, wide, w_spec, w_spec, row, row, vec, row, vec],
                          out_specs=[row, row, vec, vec], out_shape=[SDS((s, d), f32), SDS((s, d), bf16), SDS((1, d), f32), SDS((1, d), f32)],
                          name=name, compiler_params=_cparams("arbitrary"))(d_g, d_u, w_g, w_u, hmid, dh_out, g_pre, y1, g_post)


def proj_in_grad(pairs, x, add, name, tm=512, dep=None, below=None):
    s, d = x.shape
    n = len(pairs)
    extra = [] if dep is None else [dep]
    n_below = 0 if below is None else 2

    def body(*refs):
        x_ref, add_ref = refs[3 * n], refs[3 * n + 1]
        below_refs = refs[3 * n + 2:3 * n + 2 + n_below]
        outs = refs[3 * n + 2 + n_below + len(extra):]

        @pl.when(pl.program_id(0) == 0)
        def _():
            for o in outs[1:1 + n] + outs[2 + n:]:
                o[...] = jnp.zeros_like(o)

        xv = x_ref[...]
        dx = add_ref[...]
        for i in range(n):
            a_ref, b_ref, g_ref = refs[3 * i:3 * i + 3]
            dxi, dgi = _rms_bwd_math(xv, g_ref[...], _dot_nt(a_ref[...], b_ref[...]))
            dx = dx + dxi
            outs[1 + i][...] += dgi
        outs[0][...] = dx
        if below is not None:
            dy, dg = _rms_bwd_math(below_refs[0][...], below_refs[1][...], dx)
            outs[1 + n][...] = dy.astype(bf16)
            outs[2 + n][...] += dg

    row = pl.BlockSpec((tm, d), lambda i: (i, 0))
    vec = pl.BlockSpec((1, d), lambda i: (0, 0))
    in_specs, args = [], []
    for a, b, g in pairs:
        k = a.shape[1]
        in_specs += [pl.BlockSpec((tm, k), lambda i: (i, 0)), _resident((d, k), lambda i: (0, 0)), vec]
        args += [a, b, g]
    in_specs += [row, row] + [row, vec][:n_below] + [pl.BlockSpec((8, LANES), lambda i: (0, 0))] * len(extra)
    out_specs = [row] + [vec] * n + [row, vec][:n_below]
    out_shape = [SDS((s, d), f32)] + [SDS((1, d), f32)] * n + [SDS((s, d), bf16), SDS((1, d), f32)][:n_below]
    out = pl.pallas_call(body, grid=(s // tm,), in_specs=in_specs, out_specs=out_specs, out_shape=out_shape, name=name,
                         compiler_params=_cparams("arbitrary"))(*args, x, add, *(below or ()), *extra)
    return (out[0], out[1:1 + n]) + tuple(out[1 + n:])


def ffn_up(f, wg, wu, name, tm=512, tc=256):
    s, d = f.shape
    ff = wg.shape[-1]

    def body(f_ref, wg_ref, wu_ref, fac_ref, act_ref):
        fv = f_ref[...]
        for j in range(ff // tc):
            lo = j * tc
            gg = _dot(fv, wg_ref[:, lo:lo + tc])
            uu = _dot(fv, wu_ref[:, lo:lo + tc])
            sg = _sigmoid(gg)
            silu = gg * sg
            fac_ref[:, lo:lo + tc] = (uu * (sg + silu * (1.0 - sg))).astype(bf16)
            fac_ref[:, ff + lo:ff + lo + tc] = silu.astype(bf16)
            act_ref[:, lo:lo + tc] = (silu * uu).astype(bf16)

    w_spec = _resident((d, ff), lambda i: (0, 0))
    return pl.pallas_call(body, grid=(s // tm,), in_specs=[pl.BlockSpec((tm, d), lambda i: (i, 0)), w_spec, w_spec],
                          out_specs=[pl.BlockSpec((tm, 2 * ff), lambda i: (i, 0)), pl.BlockSpec((tm, ff), lambda i: (i, 0))],
                          out_shape=[SDS((s, 2 * ff), bf16), SDS((s, ff), bf16)], name=name,
                          compiler_params=_cparams("parallel"))(f, wg, wu)


def _gmlp_forward_chunk(u, v, w_refs, bias, ln_g, ln_b):
    gu, tu = _gelu(u)
    gv, tv = _gelu(v)
    mu = jnp.sum(gv, axis=-1, keepdims=True) * (1.0 / MAIN_WIDTH)
    xc = gv - mu
    rstd = lax.rsqrt(jnp.sum(xc * xc, axis=-1, keepdims=True) * (1.0 / MAIN_WIDTH) + LN_EPS)
    xhat = xc * rstd
    vln = xhat * ln_g + ln_b
    row = lax.broadcasted_iota(jnp.int32, (CHUNK, CHUNK), 0)
    col = lax.broadcasted_iota(jnp.int32, (CHUNK, CHUNK), 1)
    s_parts = []
    for g in range(A_GROUPS):
        w = jnp.where(col <= row, w_refs[g], jnp.zeros((), bf16))
        s_parts.append(_dot(w, vln[:, g * CHUNK:(g + 1) * CHUNK].astype(bf16)) + bias[:, g:g + 1])
    return gu, tu, tv, rstd, xhat, vln, s_parts


def gmlp_fwd(proj, ws, bs_t, ln_g, ln_b, name, tm=512, out_width=MAIN_WIDTH):
    s = proj.shape[0]

    def body(u_ref, v_ref, w_ref, b_ref, g_ref, bb_ref, o_ref):
        bias = b_ref[...]
        for c in range(tm // CHUNK):
            rows = slice(c * CHUNK, (c + 1) * CHUNK)
            gu, _, _, _, _, _, s_parts = _gmlp_forward_chunk(u_ref[rows, :], v_ref[rows, :], w_ref, bias, g_ref[...], bb_ref[...])
            for g in range(A_GROUPS):
                cols = slice(g * CHUNK, (g + 1) * CHUNK)
                o_ref[rows, cols] = (gu[:, cols] * s_parts[g]).astype(bf16)

    vec = pl.BlockSpec((1, MAIN_WIDTH), lambda i: (0, 0))
    return pl.pallas_call(
        body, grid=(s // tm,),
        in_specs=[pl.BlockSpec((tm, MAIN_WIDTH), lambda i: (i, 0)), pl.BlockSpec((tm, MAIN_WIDTH), lambda i: (i, 1)),
                  pl.BlockSpec((A_GROUPS, CHUNK, CHUNK), lambda i: (0, 0, 0)), pl.BlockSpec((CHUNK, A_GROUPS), lambda i: (0, 0)), vec, vec],
        out_specs=pl.BlockSpec((tm, MAIN_WIDTH), lambda i: (i, 0)), out_shape=SDS((s, out_width), bf16), name=name,
        compiler_params=_cparams("parallel"))(proj, proj, ws, bs_t, ln_g, ln_b)


def gmlp_bwd(proj, d_mixed, ws, ws_t, bs_t, ln_g, ln_b, name, tm=512, out_width=2 * MAIN_WIDTH):
    s = proj.shape[0]

    def body(u_ref, v_ref, dm_ref, w_ref, wt_ref, b_ref, g_ref, bb_ref, duv_ref, dw_ref, db_ref, dg_ref, dbb_ref):
        @pl.when(pl.program_id(0) == 0)
        def _():
            dw_ref[...] = jnp.zeros_like(dw_ref)
            db_ref[...] = jnp.zeros_like(db_ref)
            dg_ref[...] = jnp.zeros_like(dg_ref)
            dbb_ref[...] = jnp.zeros_like(dbb_ref)

        bias = b_ref[...]
        ln_gv = g_ref[...]
        row = lax.broadcasted_iota(jnp.int32, (CHUNK, CHUNK), 0)
        col = lax.broadcasted_iota(jnp.int32, (CHUNK, CHUNK), 1)
        lane = lax.broadcasted_iota(jnp.int32, (CHUNK, LANES), 1)
        for c in range(tm // CHUNK):
            rows = slice(c * CHUNK, (c + 1) * CHUNK)
            u = u_ref[rows, :]
            v = v_ref[rows, :]
            gu, tu, tv, rstd, xhat, vln, s_parts = _gmlp_forward_chunk(u, v, w_ref, bias, ln_gv, bb_ref[...])
            dm = dm_ref[rows, :]
            d_vln_parts = []
            d_gu_parts = []
            db_acc = jnp.zeros((CHUNK, LANES), f32)
            for g in range(A_GROUPS):
                cols = slice(g * CHUNK, (g + 1) * CHUNK)
                dmg = dm[:, cols]
                d_gu_parts.append(dmg * s_parts[g])
                d_s = dmg * gu[:, cols]
                db_acc = db_acc + jnp.where(lane == g, jnp.sum(d_s, axis=-1, keepdims=True), 0.0)
                d_sb = d_s.astype(bf16)
                dw_ref[g] += jnp.where(col <= row, _dot_nt(d_sb, vln[:, cols].astype(bf16)), 0.0)
                wt = jnp.where(row <= col, wt_ref[g], jnp.zeros((), bf16))
                d_vln_parts.append(_dot(wt, d_sb))
            db_ref[...] += db_acc
            d_vln = jnp.concatenate(d_vln_parts, axis=-1)
            d_gu = jnp.concatenate(d_gu_parts, axis=-1)
            dg_ref[...] += jnp.sum(d_vln * xhat, axis=0, keepdims=True)
            dbb_ref[...] += jnp.sum(d_vln, axis=0, keepdims=True)
            dxh = d_vln * ln_gv
            m1 = jnp.sum(dxh, axis=-1, keepdims=True) * (1.0 / MAIN_WIDTH)
            m2 = jnp.sum(dxh * xhat, axis=-1, keepdims=True) * (1.0 / MAIN_WIDTH)
            d_gv = rstd * (dxh - m1 - xhat * m2)
            duv_ref[rows, :MAIN_WIDTH] = (d_gu * _gelu_grad(u, tu)).astype(bf16)
            duv_ref[rows, MAIN_WIDTH:] = (d_gv * _gelu_grad(v, tv)).astype(bf16)

    vec = pl.BlockSpec((1, MAIN_WIDTH), lambda i: (0, 0))
    wspec = pl.BlockSpec((A_GROUPS, CHUNK, CHUNK), lambda i: (0, 0, 0))
    return pl.pallas_call(
        body, grid=(s // tm,),
        in_specs=[pl.BlockSpec((tm, MAIN_WIDTH), lambda i: (i, 0)), pl.BlockSpec((tm, MAIN_WIDTH), lambda i: (i, 1)),
                  pl.BlockSpec((tm, MAIN_WIDTH), lambda i: (i, 0)), wspec, wspec, pl.BlockSpec((CHUNK, A_GROUPS), lambda i: (0, 0)), vec, vec],
        out_specs=[pl.BlockSpec((tm, 2 * MAIN_WIDTH), lambda i: (i, 0)), wspec, pl.BlockSpec((CHUNK, LANES), lambda i: (0, 0)), vec, vec],
        out_shape=[SDS((s, out_width), bf16), SDS((A_GROUPS, CHUNK, CHUNK), f32), SDS((CHUNK, LANES), f32),
                   SDS((1, MAIN_WIDTH), f32), SDS((1, MAIN_WIDTH), f32)],
        name=name, compiler_params=_cparams("arbitrary"))(proj, proj, d_mixed, ws, ws_t, bs_t, ln_g, ln_b)


def _head_mask(width, h):
    lane = lax.broadcasted_iota(jnp.int32, (1, width), 1)
    return (lane >= h * HEAD_DIM) & (lane < (h + 1) * HEAD_DIM)


def mem_attn_fwd(proj, q_block, kv, into, name, tm=512):
    s = proj.shape[0]
    n_mem = kv.shape[0]
    out_block = into.shape[1] // MEM_WIDTH - 1

    def body(q_ref, kv_ref, into_ref, o_ref):
        q = q_ref[...].astype(f32)
        k = kv_ref[:, :MEM_WIDTH].astype(bf16)
        v = kv_ref[:, MEM_WIDTH:].astype(bf16)
        out = jnp.zeros((tm, MEM_WIDTH), f32)
        for h in range(MEM_HEADS):
            msk = _head_mask(MEM_WIDTH, h)
            qh = jnp.where(msk, q, 0.0).astype(bf16)
            sc = _dot_nt(qh, k) * ATT_SCALE
            e = jnp.exp(sc - jnp.max(sc, axis=-1, keepdims=True))
            p = e / jnp.sum(e, axis=-1, keepdims=True)
            out = jnp.where(msk, _dot(p.astype(bf16), v), out)
        o_ref[...] = out.astype(bf16)

    return pl.pallas_call(body, grid=(s // tm,),
                          in_specs=[pl.BlockSpec((tm, MEM_WIDTH), lambda i: (i, q_block)), pl.BlockSpec((n_mem, 2 * MEM_WIDTH), lambda i: (0, 0)), _ANY],
                          out_specs=pl.BlockSpec((tm, MEM_WIDTH), lambda i: (i, out_block)), out_shape=SDS(into.shape, bf16), name=name,
                          input_output_aliases={2: 0}, compiler_params=_cparams("parallel"))(proj, kv, into)


def mem_attn_bwd(proj, q_block, kv, d_mixed, into, name, tm=512):
    s = proj.shape[0]
    n_mem = kv.shape[0]
    out_block = into.shape[1] // MEM_WIDTH - 1

    def body(q_ref, kv_ref, do_ref, into_ref, dq_ref, dkv_ref):
        @pl.when(pl.program_id(0) == 0)
        def _():
            dkv_ref[...] = jnp.zeros_like(dkv_ref)

        q = q_ref[...].astype(f32)
        do = do_ref[...]
        k = kv_ref[:, :MEM_WIDTH].astype(bf16)
        v = kv_ref[:, MEM_WIDTH:].astype(bf16)
        dq = jnp.zeros((tm, MEM_WIDTH), f32)
        dk = jnp.zeros((n_mem, MEM_WIDTH), f32)
        dv = jnp.zeros((n_mem, MEM_WIDTH), f32)
        for h in range(MEM_HEADS):
            msk = _head_mask(MEM_WIDTH, h)
            qh = jnp.where(msk, q, 0.0).astype(bf16)
            doh = jnp.where(msk, do, 0.0).astype(bf16)
            sc = _dot_nt(qh, k) * ATT_SCALE
            e = jnp.exp(sc - jnp.max(sc, axis=-1, keepdims=True))
            p = e / jnp.sum(e, axis=-1, keepdims=True)
            dp = _dot_nt(doh, v)
            ds = p * (dp - jnp.sum(dp * p, axis=-1, keepdims=True))
            dsb = (ds * ATT_SCALE).astype(bf16)
            dq = jnp.where(msk, _dot(dsb, k), dq)
            dk = dk + _dot_tn(dsb, qh)
            dv = dv + _dot_tn(p.astype(bf16), doh)
        dq_ref[...] = dq.astype(bf16)
        dkv_ref[:, :MEM_WIDTH] += dk
        dkv_ref[:, MEM_WIDTH:] += dv

    return pl.pallas_call(
        body, grid=(s // tm,),
        in_specs=[pl.BlockSpec((tm, MEM_WIDTH), lambda i: (i, q_block)), pl.BlockSpec((n_mem, 2 * MEM_WIDTH), lambda i: (0, 0)),
                  pl.BlockSpec((tm, MEM_WIDTH), lambda i: (i, MAIN_WIDTH // MEM_WIDTH)), _ANY],
        out_specs=[pl.BlockSpec((tm, MEM_WIDTH), lambda i: (i, out_block)), pl.BlockSpec((n_mem, 2 * MEM_WIDTH), lambda i: (0, 0))],
        out_shape=[SDS(into.shape, bf16), SDS((n_mem, 2 * MEM_WIDTH), f32)], name=name,
        input_output_aliases={3: 0}, compiler_params=_cparams("arbitrary"))(proj, kv, d_mixed, into)


def _tri(t, upper):
    r = lax.broadcasted_iota(jnp.int32, (t, t), 0)
    c = lax.broadcasted_iota(jnp.int32, (t, t), 1)
    return ((r <= c) if upper else (r >= c)).astype(f32)


def fgate_fwd(z_t, b, name, t=512):
    hh, s = z_t.shape

    def body(z_ref, b_ref, c_ref):
        u = _tri(t, True)
        carry = jnp.zeros((hh, 1), f32)
        for blk in range(s // t):
            x = z_ref[:, blk * t:(blk + 1) * t] + b_ref[...]
            logf = jnp.minimum(x, 0.0) - jnp.log(1.0 + jnp.exp(-jnp.abs(x)))
            y = jnp.dot(logf, u, precision=lax.Precision.HIGHEST, preferred_element_type=f32) + carry
            c_ref[:, blk * t:(blk + 1) * t] = y
            carry = y[:, t - 1:t]

    return pl.pallas_call(body, out_shape=SDS((hh, s), f32), name=name, compiler_params=_cparams())(z_t, b)


def fgate_bwd(dc_t, z_t, b, name, t=512):
    hh, s = z_t.shape

    def body(dc_ref, z_ref, b_ref, dz_ref, db_ref):
        low = _tri(t, False)
        carry = jnp.zeros((hh, 1), f32)
        total = jnp.zeros((hh, 1), f32)
        for blk in reversed(range(s // t)):
            cols = slice(blk * t, (blk + 1) * t)
            y = jnp.dot(dc_ref[:, cols], low, precision=lax.Precision.HIGHEST, preferred_element_type=f32) + carry
            carry = y[:, 0:1]
            dz = y * _sigmoid(-(z_ref[:, cols] + b_ref[...]))
            dz_ref[:, cols] = dz
            total = total + jnp.sum(dz, axis=-1, keepdims=True)
        db_ref[...] = jnp.broadcast_to(total, db_ref.shape)

    return pl.pallas_call(body, out_shape=[SDS((hh, s), f32), SDS((hh, LANES), f32)], name=name,
                          compiler_params=_cparams())(dc_t, z_t, b)


def _pair_masks():
    lane = lax.broadcasted_iota(jnp.int32, (1, LANES), 1)
    return [lane < HEAD_DIM, lane >= HEAD_DIM]


def _tile_base(cr_ref, hh, lo):
    return cr_ref[hh:hh + 1, pl.ds(lo, LANES)][:, 0:1]


def fox_fwd(q, kv, c_row, name, tq=512, out_width=MAIN_WIDTH):
    s = kv.shape[0]
    nq = s // tq

    def body(q_ref, k_ref, v_ref, cr_ref, o_ref, lse_ref, ob_ref):
        i = pl.program_id(1)
        qv = q_ref[...]
        masks = _pair_masks()
        row = lax.broadcasted_iota(jnp.int32, (tq, tq), 0)
        col = lax.broadcasted_iota(jnp.int32, (tq, tq), 1)
        qh = [jnp.where(masks[hh], qv, jnp.zeros((), bf16)) * ATT_SCALE for hh in range(2)]
        ct = [_tile_base(cr_ref, hh, pl.multiple_of(i * tq, tq)) for hh in range(2)]

        def block(j, carry, diag):
            lo = pl.multiple_of(j * tq, tq)
            ks = k_ref[pl.ds(lo, tq), :]
            vs = v_ref[pl.ds(lo, tq), :]
            out = []
            for hh in range(2):
                m, l, acc = carry[hh]
                sc = _dot_nt(qh[hh], ks) + (ct[hh] - cr_ref[hh:hh + 1, pl.ds(lo, tq)])
                if diag:
                    sc = jnp.where(col <= row, sc, -jnp.inf)
                m_new = jnp.maximum(m, jnp.max(sc, axis=-1, keepdims=True))
                alpha = jnp.exp(m - m_new)
                p = jnp.exp(sc - m_new)
                l = alpha * l + jnp.sum(p, axis=-1, keepdims=True)
                p_hi = p.astype(bf16)
                p_lo = (p - p_hi.astype(f32)).astype(bf16)
                acc = alpha * acc + (_dot(p_hi, vs) + _dot(p_lo, vs))
                out.append((m_new, l, acc))
            return tuple(out)

        init = (jnp.full((tq, 1), -jnp.inf, f32), jnp.zeros((tq, 1), f32), jnp.zeros((tq, LANES), f32))
        carry = lax.fori_loop(0, i, functools.partial(block, diag=False), (init, init))
        res = [(acc / l, m + jnp.log(l)) for m, l, acc in block(i, carry, True)]
        out = jnp.where(masks[0], res[0][0], res[1][0])
        o_ref[...] = out
        ob_ref[...] = out.astype(bf16)
        lse_ref[...] = jnp.where(masks[0], res[0][1], res[1][1])

    return pl.pallas_call(
        body, grid=(FOX_PAIRS, nq),
        in_specs=[pl.BlockSpec((tq, LANES), lambda p, i: (i, p)), pl.BlockSpec((s, LANES), lambda p, i: (0, p)),
                  pl.BlockSpec((s, LANES), lambda p, i: (0, FOX_PAIRS + p)), pl.BlockSpec((None, 2, s), lambda p, i: (p, 0, 0))],
        out_specs=[pl.BlockSpec((tq, LANES), lambda p, i: (i, p)), pl.BlockSpec((None, tq, LANES), lambda p, i: (p, i, 0)),
                   pl.BlockSpec((tq, LANES), lambda p, i: (i, p))],
        out_shape=[SDS((s, MAIN_WIDTH), f32), SDS((FOX_PAIRS, s, LANES), f32), SDS((s, out_width), bf16)], name=name,
        compiler_params=_cparams("parallel", "parallel"))(q, kv, kv, c_row)


def fox_bwd(q, kv, d_mixed, o, lse, c_row, name, tq=512, dq_width=MAIN_WIDTH):
    s = kv.shape[0]
    nq = s // tq

    def body(q_ref, k_ref, v_ref, do_ref, o_ref, lse_ref, cr_ref, dqb_ref, dk_ref, dv_ref, dc_ref, dq_ref):
        j = pl.program_id(1)

        @pl.when(j == 0)
        def _():
            dq_ref[...] = jnp.zeros_like(dq_ref)

        masks = _pair_masks()
        sub = lax.broadcasted_iota(jnp.int32, (LANES, 1), 0)
        sub_masks = [sub < HEAD_DIM, sub >= HEAD_DIM]
        row = lax.broadcasted_iota(jnp.int32, (tq, tq), 0)
        col = lax.broadcasted_iota(jnp.int32, (tq, tq), 1)
        kj = k_ref[...]
        vj = v_ref[...]
        lo_j = pl.multiple_of(j * tq, tq)

        def block(i, carry, diag):
            dk_t, dv_t, dc0, dc1 = carry
            dcs = [dc0, dc1]
            lo = pl.multiple_of(i * tq, tq)
            qi = q_ref[pl.ds(lo, tq), :]
            qi = qi * ATT_SCALE
            qt_i = qi.T
            doi = do_ref[pl.ds(lo, tq), :]
            dot_i = doi.astype(bf16).T
            prod = doi.astype(bf16).astype(f32) * o_ref[pl.ds(lo, tq), :]
            lse_i = lse_ref[pl.ds(lo, tq), :]
            dq_i = jnp.zeros((tq, LANES), f32)
            for hh in range(2):
                qh = jnp.where(masks[hh], qi, jnp.zeros((), bf16))
                doh = jnp.where(masks[hh], doi, 0.0).astype(bf16)
                delta = jnp.sum(jnp.where(masks[hh], prod, 0.0), axis=-1, keepdims=True)
                sc = _dot_nt(qh, kj) + (_tile_base(cr_ref, hh, lo) - cr_ref[hh:hh + 1, pl.ds(lo_j, tq)])
                p = jnp.exp(sc - lse_i[:, hh * HEAD_DIM:hh * HEAD_DIM + 1])
                if diag:
                    p = jnp.where(col <= row, p, 0.0)
                dv_t = dv_t + _dot(jnp.where(sub_masks[hh], dot_i, jnp.zeros((), bf16)), p.astype(bf16))
                ds = p * (_dot_nt(doh, vj) - delta)
                dcs[hh] = dcs[hh] + jnp.sum(ds, axis=0, keepdims=True)
                dsb = ds.astype(bf16)
                dq_i = jnp.where(masks[hh], _dot(dsb, kj), dq_i)
                dk_t = dk_t + _dot(jnp.where(sub_masks[hh], qt_i, jnp.zeros((), bf16)), dsb)
            dq_ref[pl.ds(lo, tq), :] += dq_i * ATT_SCALE
            return dk_t, dv_t, dcs[0], dcs[1]

        zero = jnp.zeros((LANES, tq), f32)
        zrow = jnp.zeros((1, tq), f32)
        carry = block(j, (zero, zero, zrow, zrow), True)
        dk_t, dv_t, dc0, dc1 = lax.fori_loop(j + 1, nq, functools.partial(block, diag=False), carry)
        dk_ref[...] = dk_t.T.astype(bf16)
        dv_ref[...] = dv_t.T.astype(bf16)
        dc_ref[0:1, :] = -dc0
        dc_ref[1:2, :] = -dc1

        @pl.when(j == nq - 1)
        def _():
            dqb_ref[...] = dq_ref[...].astype(bf16)

    full = lambda p, j: (0, p)
    tile = lambda p, j: (j, p)
    return pl.pallas_call(
        body, grid=(FOX_PAIRS, nq),
        in_specs=[pl.BlockSpec((s, LANES), full), pl.BlockSpec((tq, LANES), tile), pl.BlockSpec((tq, LANES), lambda p, j: (j, FOX_PAIRS + p)),
                  pl.BlockSpec((s, LANES), full), pl.BlockSpec((s, LANES), full), pl.BlockSpec((None, s, LANES), lambda p, j: (p, 0, 0)),
                  pl.BlockSpec((None, 2, s), lambda p, j: (p, 0, 0))],
        out_specs=[pl.BlockSpec((s, LANES), full), pl.BlockSpec((tq, LANES), tile), pl.BlockSpec((tq, LANES), tile),
                   pl.BlockSpec((None, 2, tq), lambda p, j: (p, 0, j))],
        out_shape=[SDS((s, dq_width), bf16), SDS((s, MAIN_WIDTH), bf16), SDS((s, MAIN_WIDTH), bf16), SDS((FOX_PAIRS, 2, s), f32)],
        scratch_shapes=[pltpu.VMEM((s, LANES), f32)],
        name=name, compiler_params=_cparams("parallel", "arbitrary"))(q, kv, kv, d_mixed, o, lse, c_row)


def adamw(w, g, m, v, name, tr=256):
    r, c = w.shape
    tr = min(tr, r)
    assert r % tr == 0, (name, r, tr)
    c1 = 1.0 / (1.0 - ADAM_B1 ** ADAM_STEP)
    c2 = 1.0 / (1.0 - ADAM_B2 ** ADAM_STEP)

    def body(w_ref, g_ref, m_ref, v_ref, d_ref, mo_ref, vo_ref):
        gv = g_ref[...]
        mn = ADAM_B1 * m_ref[...] + (1.0 - ADAM_B1) * gv
        vn = ADAM_B2 * v_ref[...] + (1.0 - ADAM_B2) * gv * gv
        mo_ref[...] = mn
        vo_ref[...] = vn
        d_ref[...] = -ADAM_LR * ((mn * c1) / (jnp.sqrt(vn * c2) + ADAM_EPS) + ADAM_WD * w_ref[...])

    spec = pl.BlockSpec((tr, c), lambda i: (i, 0))
    return pl.pallas_call(body, grid=(r // tr,), in_specs=[spec] * 4, out_specs=[spec] * 3, out_shape=[SDS((r, c), f32)] * 3,
                          name=name, compiler_params=_cparams("parallel"))(w, g, m, v)


def adamw_owned(w, parts, m, v, name, tr):
    nl, r, c = w.shape
    cp = parts[0].shape[2]
    assert r % tr == 0 and len(parts) == nl, (name, r, tr)
    c1 = 1.0 / (1.0 - ADAM_B1 ** ADAM_STEP)
    c2 = 1.0 / (1.0 - ADAM_B2 ** ADAM_STEP)

    def body(*refs):
        w_ref, p_refs, (m_ref, v_ref) = refs[0], refs[1:1 + nl], refs[1 + nl:3 + nl]
        g_ref, d_ref, mo_ref, vo_ref = refs[3 + nl:]
        layer = pl.program_id(0)

        def total(p_ref):
            acc = p_ref[0].astype(f32)
            for k in range(1, N_DEV):
                acc = acc + p_ref[k].astype(f32)
            return acc

        gv = total(p_refs[0])
        for l in range(1, nl):
            gv = jnp.where(layer == l, total(p_refs[l]), gv)
        gv = gv[:, :c]
        g_ref[...] = gv
        mn = ADAM_B1 * m_ref[...] + (1.0 - ADAM_B1) * gv
        vn = ADAM_B2 * v_ref[...] + (1.0 - ADAM_B2) * gv * gv
        mo_ref[...] = mn
        vo_ref[...] = vn
        d_ref[...] = -ADAM_LR * ((mn * c1) / (jnp.sqrt(vn * c2) + ADAM_EPS) + ADAM_WD * w_ref[...])

    spec = pl.BlockSpec((None, tr, c), lambda l, i: (l, i, 0))
    last = r // tr - 1

    def part_spec(mine):
        return pl.BlockSpec((N_DEV, tr, cp), lambda l, i: (0, jnp.where(l == mine, i, jnp.where(l < mine, 0, last)), 0))

    return pl.pallas_call(body, grid=(nl, r // tr), in_specs=[spec] + [part_spec(l) for l in range(nl)] + [spec, spec], out_specs=[spec] * 4,
                          out_shape=[SDS((nl, r, c), f32)] * 4, name=name,
                          compiler_params=_cparams("parallel", "parallel"))(w, *parts, m, v)


def sum_leading(x, name, out_dtype=f32, tr=None):
    n, r, c = x.shape
    tr = tr or r
    assert r % tr == 0

    def body(x_ref, o_ref):
        acc = x_ref[0].astype(f32)
        for k in range(1, n):
            acc = acc + x_ref[k].astype(f32)
        o_ref[...] = acc.astype(out_dtype)

    return pl.pallas_call(body, grid=(r // tr,), in_specs=[pl.BlockSpec((n, tr, c), lambda i: (0, i, 0))],
                          out_specs=pl.BlockSpec((tr, c), lambda i: (i, 0)), out_shape=SDS((r, c), out_dtype), name=name,
                          compiler_params=_cparams("parallel"))(x)


_ANY = pl.BlockSpec(memory_space=pl.ANY)
_DMA = pltpu.SemaphoreType.DMA


_HBM = pl.BlockSpec(memory_space=pltpu.HBM)
_SEM = pl.BlockSpec(memory_space=pltpu.SEMAPHORE)
_EFFECT = pltpu.SideEffectType.DATAFLOW_SIDE_EFFECTING
_FLIPS = [(0, 0, 1), (1, 0, 0), (0, 1, 0), (1, 1, 0), (1, 0, 1), (0, 1, 1), (1, 1, 1)]


def _me():
    return lax.axis_index("x"), lax.axis_index("y"), lax.axis_index("c")


def _peers():
    mx, my, mc = _me()
    return [(jnp.bitwise_xor(mx, fx), jnp.bitwise_xor(my, fy), jnp.bitwise_xor(mc, fc)) for fx, fy, fc in _FLIPS]


def _index(dev):
    return 4 * dev[0] + 2 * dev[1] + dev[2]


def _win(ref, axis, k, size, count=1):
    idx = [slice(None)] * len(ref.shape)
    idx[axis] = pl.ds(k * size, count * size)
    return ref.at[tuple(idx)]


def _hbm(a):
    return pltpu.with_memory_space_constraint(a, pltpu.HBM)


def _exchange_start(srcs, lands, copies_of, name):
    n = len(srcs)

    def body(*refs):
        src = refs[:n]
        send_sems, recv_sems, self_sems = refs[2 * n:2 * n + 3]
        land = refs[3 * n + 3:4 * n + 3]
        token = refs[4 * n + 3]
        me = _index(_me())
        for a in range(n):
            for s_ref, d_ref, peer in copies_of(a, src[a], land[a], me):
                if peer is None:
                    pltpu.make_async_copy(s_ref, d_ref, self_sems.at[a]).start()
                else:
                    pltpu.make_async_remote_copy(src_ref=s_ref, dst_ref=d_ref, send_sem=send_sems.at[a], recv_sem=recv_sems.at[a],
                                                 device_id=peer, device_id_type=MESH).start()
        token[...] = jnp.zeros_like(token)

    outs = pl.pallas_call(
        body, name=name,
        out_shape=(_DMA((n,)), _DMA((n,)), _DMA((n,)), *[pltpu.HBM(s.shape, s.dtype) for s in srcs],
                   *[pltpu.HBM(l.shape, l.dtype) for l in lands], SDS((8, LANES), f32)),
        in_specs=[_HBM] * (2 * n), out_specs=(_SEM, _SEM, _SEM, *[_HBM] * (2 * n), pl.BlockSpec(memory_space=pltpu.VMEM)),
        input_output_aliases={i: 3 + i for i in range(2 * n)},
        compiler_params=pltpu.CompilerParams(has_side_effects=_EFFECT),
    )(*[_hbm(s) for s in srcs], *[_hbm(lax.empty(l.shape, l.dtype)) for l in lands])
    return dict(sems=outs[:3], srcs=list(outs[3:3 + n]), lands=list(outs[3 + n:3 + 2 * n]), token=outs[3 + 2 * n])


def _exchange_wait(started, waits_of, after, name, which=None):
    which = list(range(len(started["srcs"]))) if which is None else which
    srcs, lands = [started["srcs"][a] for a in which], [started["lands"][a] for a in which]
    n = len(which)

    def body(*refs):
        src = refs[:n]
        land = refs[n:2 * n]
        send_sems, recv_sems, self_sems = refs[2 * n:2 * n + 3]
        me = _index(_me())
        for pos, a in enumerate(which):
            seven, (s_ref, d_ref) = waits_of(a, src[pos], land[pos], me)
            both = pltpu.make_async_remote_copy(src_ref=seven, dst_ref=seven, send_sem=send_sems.at[a], recv_sem=recv_sems.at[a],
                                                device_id=_me(), device_id_type=MESH)
            both.wait_send()
            both.wait_recv()
            pltpu.make_async_copy(s_ref, d_ref, self_sems.at[a]).wait()

    outs = pl.pallas_call(
        body, name=name, out_shape=tuple(pltpu.HBM(t.shape, t.dtype) for t in srcs + lands),
        in_specs=[_HBM] * (2 * n) + [_SEM] * 3 + [_ANY], out_specs=tuple([_HBM] * (2 * n)),
        input_output_aliases={i: i for i in range(2 * n)},
        compiler_params=pltpu.CompilerParams(has_side_effects=_EFFECT),
    )(*srcs, *lands, *started["sems"], after)
    return list(outs[n:])


def gather_start(locs, axes, name):
    lands = [SDS(tuple(N_DEV * d if i == ax else d for i, d in enumerate(l.shape)), l.dtype) for l, ax in zip(locs, axes)]

    def copies_of(a, src, land, me):
        mine = _win(land, axes[a], me, src.shape[axes[a]])
        return [(src, mine, peer) for peer in _peers()] + [(src, mine, None)]

    return _exchange_start(locs, lands, copies_of, name)


def gather_wait(started, axes, after, name, which=None):
    def waits_of(a, src, land, me):
        size = src.shape[axes[a]]
        return _win(land, axes[a], 0, size, N_DEV - 1), (src, _win(land, axes[a], me, size))

    return _exchange_wait(started, waits_of, after, name, which)


def scatter_start(grads, axes, name):
    lands = [SDS((N_DEV,) + tuple(d // N_DEV if i == ax else d for i, d in enumerate(g.shape)), g.dtype) for g, ax in zip(grads, axes)]

    def copies_of(a, src, land, me):
        size = src.shape[axes[a]] // N_DEV
        out = [(_win(src, axes[a], _index(peer), size), land.at[me], peer) for peer in _peers()]
        return out + [(_win(src, axes[a], me, size), land.at[me], None)]

    return _exchange_start(grads, lands, copies_of, name)


def scatter_wait(started, axes, after, name):
    def waits_of(a, src, land, me):
        size = src.shape[axes[a]] // N_DEV
        return land.at[pl.ds(0, N_DEV - 1)], (_win(src, axes[a], me, size), land.at[me])

    return _exchange_wait(started, waits_of, after, name)


def _row_tile(rows, cap=512):
    return max(t for t in range(8, min(rows, cap) + 1, 8) if rows % t == 0)


_SMALL = [
    ("ln_mix_pre", (2, 1024)), ("ln_mix_post", (2, 1024)), ("ln_ffn_pre", (2, 1024)), ("ln_ffn_post", (2, 1024)),
    ("ln_mem", (2, 1024)), ("w_spatial", (1, 6, 128, 128)), ("b_spatial", (1, 6, 128)), ("ln_shared", (1024,)),
    ("b_forget", (12,)), ("ln_v_g", (1, 768)), ("ln_v_b", (1, 768)),
]
_SMALL_TILE = 8 * LANES


def _small_rows(shape):
    return -(-math.prod(shape) // _SMALL_TILE) * 8


def _pack_small(vals, shapes):
    parts = []
    for name, shape in shapes:
        flat = vals[name].reshape(-1).astype(f32)
        rows = _small_rows(shape)
        parts.append(jnp.pad(flat, (0, rows * LANES - flat.shape[0])).reshape(rows, LANES))
    return jnp.concatenate(parts, axis=0)


def _unpack_small(buf, shapes):
    out = {}
    lo = 0
    for name, shape in shapes:
        rows = _small_rows(shape)
        out[name] = buf[lo:lo + rows].reshape(-1)[:math.prod(shape)].reshape(shape)
        lo += rows
    return out


def kernel(x, mem, ln_mix_pre, ln_mix_post, ln_ffn_pre, ln_ffn_post, ln_mem, w_mem_kv, w_out, w_ffn_gate, w_ffn_up, w_ffn_down, w_in_a, w_spatial, b_spatial, ln_v_g, ln_v_b, ln_shared, w_shared_kv, b_forget, w_in_b, loss_target, m_ln_mix_pre, m_ln_mix_post, m_ln_ffn_pre, m_ln_ffn_post, m_ln_mem, m_w_mem_kv, m_w_out, m_w_ffn_gate, m_w_ffn_up, m_w_ffn_down, m_w_in_a, m_w_spatial, m_b_spatial, m_ln_v_g, m_ln_v_b, m_ln_shared, m_w_shared_kv, m_b_forget, m_w_in_b, v_ln_mix_pre, v_ln_mix_post, v_ln_ffn_pre, v_ln_ffn_post, v_ln_mem, v_w_mem_kv, v_w_out, v_w_ffn_gate, v_w_ffn_up, v_w_ffn_down, v_w_in_a, v_w_spatial, v_b_spatial, v_ln_v_g, v_ln_v_b, v_ln_shared, v_w_shared_kv, v_b_forget, v_w_in_b):
    weights = dict(ln_mix_pre=ln_mix_pre, ln_mix_post=ln_mix_post, ln_ffn_pre=ln_ffn_pre, ln_ffn_post=ln_ffn_post, ln_mem=ln_mem,
                   w_mem_kv=w_mem_kv, w_out=w_out, w_ffn_gate=w_ffn_gate, w_ffn_up=w_ffn_up, w_ffn_down=w_ffn_down, w_in_a=w_in_a,
                   w_spatial=w_spatial, b_spatial=b_spatial, ln_v_g=ln_v_g, ln_v_b=ln_v_b, ln_shared=ln_shared,
                   w_shared_kv=w_shared_kv, b_forget=b_forget, w_in_b=w_in_b)
    mom_m = dict(ln_mix_pre=m_ln_mix_pre, ln_mix_post=m_ln_mix_post, ln_ffn_pre=m_ln_ffn_pre, ln_ffn_post=m_ln_ffn_post, ln_mem=m_ln_mem,
                 w_mem_kv=m_w_mem_kv, w_out=m_w_out, w_ffn_gate=m_w_ffn_gate, w_ffn_up=m_w_ffn_up, w_ffn_down=m_w_ffn_down, w_in_a=m_w_in_a,
                 w_spatial=m_w_spatial, b_spatial=m_b_spatial, ln_v_g=m_ln_v_g, ln_v_b=m_ln_v_b, ln_shared=m_ln_shared,
                 w_shared_kv=m_w_shared_kv, b_forget=m_b_forget, w_in_b=m_w_in_b)
    mom_v = dict(ln_mix_pre=v_ln_mix_pre, ln_mix_post=v_ln_mix_post, ln_ffn_pre=v_ln_ffn_pre, ln_ffn_post=v_ln_ffn_post, ln_mem=v_ln_mem,
                 w_mem_kv=v_w_mem_kv, w_out=v_w_out, w_ffn_gate=v_w_ffn_gate, w_ffn_up=v_w_ffn_up, w_ffn_down=v_w_ffn_down, w_in_a=v_w_in_a,
                 w_spatial=v_w_spatial, b_spatial=v_b_spatial, ln_v_g=v_ln_v_g, ln_v_b=v_ln_v_b, ln_shared=v_ln_shared,
                 w_shared_kv=v_w_shared_kv, b_forget=v_b_forget, w_in_b=v_w_in_b)
    names = list(weights)
    mx, my, mc = lax.axis_index("x"), lax.axis_index("y"), lax.axis_index("c")
    me = 4 * mx + 2 * my + mc

    h0 = x[0]
    mem0 = mem[0]
    tgt = loss_target[0]
    seq = h0.shape[0]

    vec = lambda a: a.reshape(1, -1)
    pad_to = lambda a, axis, size: jnp.pad(a, [(0, size - a.shape[i] if i == axis else 0) for i in range(a.ndim)])

    def after(tok, a):
        return a + tok[0, 0].astype(a.dtype)

    lnv_loc = pad_to(jnp.concatenate([ln_v_g, ln_v_b], axis=0), 0, 8)
    st_a = gather_start([w_in_a.astype(bf16), pad_to(lnv_loc, 1, LANES)[None]], [0, 0], "gather_a_start")
    mix_locs = lambda l, tok: [after(tok, w_mem_kv[l]).astype(bf16), w_out[l].astype(bf16)]

    def ffn_gather_start(l, tok):
        gate_up = gather_start([pad_to(after(tok, w_ffn_gate[l]).astype(bf16), 1, FF_SHARD_PAD),
                                pad_to(w_ffn_up[l].astype(bf16), 1, FF_SHARD_PAD)], [1, 1], f"gather_gate_up{l}_start")
        down = gather_start([pad_to(after(gate_up["token"], w_ffn_down[l]).astype(bf16), 0, FF_SHARD_PAD)], [0], f"gather_down{l}_start")
        return gate_up, down

    st_b = [gather_start(mix_locs(0, st_a["token"]), [0, 0], "gather_b0_start"), None]
    st_c = ffn_gather_start(0, st_b[0]["token"])
    st_d = gather_start([after(st_c[1]["token"], w_in_b[0]).astype(bf16), pad_to(w_shared_kv.astype(bf16), 1, KV_PAD)], [0, 0],
                        "gather_d_start")
    st_b[1] = gather_start(mix_locs(1, st_d["token"]), [0, 0], "gather_b1_start")
    st_e = ffn_gather_start(1, st_b[1]["token"])
    ws = w_spatial[0].astype(bf16)
    ws_t = ws.transpose(0, 2, 1)
    bs_t = b_spatial[0].T

    (a0,) = rms_fwd(h0, [after(st_e[1]["token"], vec(ln_mix_pre[0]))], "a0_norm")
    w_in_a8, lnv8 = gather_wait(st_a, [0, 0], a0, "gather_a_wait")
    w_in_a_full = w_in_a8.transpose(1, 0, 2).reshape(D_MODEL, -1)
    lnv_g = lnv8[:, 0, :MAIN_WIDTH // N_DEV].reshape(1, MAIN_WIDTH)
    lnv_b = lnv8[:, 1, :MAIN_WIDTH // N_DEV].reshape(1, MAIN_WIDTH)
    proj0 = mm(a0, w_in_a_full, "proj0", tn=896)
    main0 = gmlp_fwd(proj0, ws, bs_t, lnv_g, lnv_b, "gmlp_fwd", out_width=D_MODEL)
    w_mkv, w_o = [None, None], [None, None]
    w_mkv[0], w_o[0] = gather_wait(st_b[0], [0, 0], main0, "gather_b0_wait")
    (memn0,) = rms_fwd(mem0, [vec(ln_mem[0])], "mem0_norm")
    kvm0 = mm(memn0, w_mkv[0], "kvm0")
    mixed0 = mem_attn_fwd(proj0, 2 * MAIN_WIDTH // MEM_WIDTH, kvm0, main0, "mem_attn0")
    y1_0, hmid0, f0 = mm_resnorm(mixed0, w_o[0], h0, vec(ln_mix_post[0]), [vec(ln_ffn_pre[0])], "mix_out0")
    w_g0, w_u0 = gather_wait(st_c[0], [1, 1], f0, "gather_gate_up0_wait")
    gu0, act0 = ffn_up(f0, w_g0, w_u0, "ffn_up0")
    (w_d0,) = gather_wait(st_c[1], [0], act0, "gather_down0_wait")
    y2_0, h1, a1, sin1 = mm_resnorm(act0, w_d0, hmid0, vec(ln_ffn_post[0]), [vec(ln_mix_pre[1]), vec(ln_shared)], "ffn_down0")

    w_inb, w_kv = gather_wait(st_d, [0, 0], sin1, "gather_d_wait")
    kvb = mm(sin1, w_kv, "kv_shared", out_dtype=bf16, tn=MAIN_WIDTH, ncols=2 * MAIN_WIDTH)
    zf = mm(sin1, w_kv, "forget_logits", tn=256, col0=2 * MAIN_WIDTH, ncols=256)
    qb = mm(a1, w_inb, "proj1", out_dtype=bf16)
    z_t = jnp.pad(zf[:, :FOX_HEADS].T, ((0, 16 - FOX_HEADS), (0, 0)))
    bf_col = jnp.pad(b_forget, (0, 16 - FOX_HEADS)).reshape(16, 1)
    c_t = fgate_fwd(z_t, bf_col, "fgate_fwd")
    c_row = c_t[:FOX_HEADS].reshape(FOX_PAIRS, 2, seq)
    main1, lse, main1_b = fox_fwd(qb, kvb, c_row, "fox_fwd", out_width=D_MODEL)
    w_mkv[1], w_o[1] = gather_wait(st_b[1], [0, 0], main1, "gather_b1_wait")
    (memn1,) = rms_fwd(mem0, [vec(ln_mem[1])], "mem1_norm")
    kvm1 = mm(memn1, w_mkv[1], "kvm1")
    mixed1 = mem_attn_fwd(qb, MAIN_WIDTH // MEM_WIDTH, kvm1, main1_b, "mem_attn1")
    y1_1, hmid1, f1 = mm_resnorm(mixed1, w_o[1], h1, vec(ln_mix_post[1]), [vec(ln_ffn_pre[1])], "mix_out1")
    w_g1, w_u1 = gather_wait(st_e[0], [1, 1], f1, "gather_gate_up1_wait")
    gu1, act1 = ffn_up(f1, w_g1, w_u1, "ffn_up1")
    (w_d1,) = gather_wait(st_e[1], [0], act1, "gather_down1_wait")
    dh, d_y2_1, dg_fpost1, loss_tile = mm_resnorm_loss(act1, w_d1, hmid1, vec(ln_ffn_post[1]), tgt, "ffn_down1_loss")
    ffn_w = [(w_g0, w_u0, w_d0), (w_g1, w_u1, w_d1)]

    small = {}

    def ffn_backward(layer, dh_out, d_y2, hmid, f, gu, act, y1):
        w_g, w_u, w_d = ffn_w[layer]
        dw_down = mm_tn(act, d_y2, f"dw_down{layer}")
        rs_down = scatter_start([dw_down], [0], f"scatter_down{layer}_start")
        d_g, d_u = ffn_act_grad(d_y2, w_d, gu, f"ffn_act_grad{layer}")
        dw_g = mm_tn(d_g, f, f"dw_gate{layer}", dep=rs_down["token"])
        dw_u = mm_tn(d_u, f, f"dw_up{layer}")
        rs_gate_up = scatter_start([dw_g, dw_u], [0, 0], f"scatter_gate_up{layer}_start")
        dh_mid, d_y1, dg_fpre, dg_mpost = ffn_in_grad(d_g, d_u, w_g, w_u, hmid, dh_out, after(rs_gate_up["token"], vec(ln_ffn_pre[layer])),
                                                      y1, vec(ln_mix_post[layer]), f"ffn_in_grad{layer}")
        return dh_mid, d_y1, dg_fpre, dg_mpost, (rs_down, rs_gate_up)

    def mix_out_backward(layer, d_y1, mixed):
        dw_out = mm_tn(mixed, d_y1, f"dw_out{layer}")
        d_mixed = mm(d_y1, w_o[layer], f"d_mixed{layer}", trans_b=True)
        return d_mixed, dw_out

    def mem_backward(layer, q_src, q_block, kvm, memn, d_mixed, into):
        d_qm, d_kvm = mem_attn_bwd(q_src, q_block, kvm, d_mixed, into, f"mem_attn_bwd{layer}")
        d_kvm_b = d_kvm.astype(bf16)
        dw_mkv = mm_tn(memn, d_kvm_b, f"dw_mem_kv{layer}")
        d_memn = mm(d_kvm_b, w_mkv[layer], f"d_memn{layer}", trans_b=True)
        _, dg_mem = rms_bwd(mem0, vec(ln_mem[layer]), d_memn, None, bf16, f"mem_norm_bwd{layer}")
        return d_qm, dw_mkv, dg_mem


    dh_mid1, d_y1_1, dg_fpre1, dg_mpost1, rs_ffn1 = ffn_backward(1, dh, d_y2_1, hmid1, f1, gu1, act1, y1_1)
    d_mixed1, dw_out1 = mix_out_backward(1, d_y1_1, mixed1)
    dq_b, dk, dv, dc = fox_bwd(qb, kvb, d_mixed1, main1, lse, c_row, "fox_bwd", dq_width=D_MODEL)
    d_proj1, dw_mkv1, dg_mem1 = mem_backward(1, qb, MAIN_WIDTH // MEM_WIDTH, kvm1, memn1, d_mixed1, dq_b)
    rs_mix1 = scatter_start([dw_out1, dw_mkv1], [0, 0], "scatter_mix1_start")
    dc_t = jnp.pad(dc.reshape(FOX_HEADS, seq), ((0, 16 - FOX_HEADS), (0, 0)))
    dz_t, db_f = fgate_bwd(dc_t, z_t, bf_col, "fgate_bwd")
    d_kvf = jnp.concatenate([dk, dv, jnp.pad(dz_t[:FOX_HEADS].T.astype(bf16), ((0, 0), (0, KV_PAD - KV_WIDTH)))], axis=-1)
    dw_in_b = mm_tn(a1, d_proj1, "dw_in_b", dep=rs_mix1["token"])
    dw_kv = mm_tn(sin1, d_kvf, "dw_kv", tn=896)
    rs_2 = scatter_start([dw_in_b, dw_kv], [0, 0], "scatter_shared_start")
    dh1, (dg_pre1, dg_shared), d_y2_0, dg_fpost0 = proj_in_grad(
        [(d_proj1, w_inb, vec(ln_mix_pre[1])), (d_kvf, w_kv, vec(ln_shared))], h1, dh_mid1, "in_grad1", dep=rs_2["token"],
        below=(y2_0, vec(ln_ffn_post[0])))

    dh_mid0, d_y1_0, dg_fpre0, dg_mpost0, rs_ffn0 = ffn_backward(0, dh1, d_y2_0, hmid0, f0, gu0, act0, y1_0)
    d_mixed0, dw_out0 = mix_out_backward(0, d_y1_0, mixed0)
    d_uv, dw_s, db_s, dg_lnv, db_lnv = gmlp_bwd(proj0, d_mixed0, ws, ws_t, bs_t, lnv_g, lnv_b, "gmlp_bwd", out_width=w_in_a_full.shape[1])
    d_proj0, dw_mkv0, dg_mem0 = mem_backward(0, proj0, 2 * MAIN_WIDTH // MEM_WIDTH, kvm0, memn0, d_mixed0, d_uv)
    rs_mix0 = scatter_start([dw_out0, dw_mkv0], [0, 0], "scatter_mix0_start")

    small["ln_mix_pre"] = jnp.concatenate([jnp.zeros_like(dg_pre1), dg_pre1], axis=0)
    small["ln_mix_post"] = jnp.concatenate([dg_mpost0, dg_mpost1], axis=0)
    small["ln_ffn_pre"] = jnp.concatenate([dg_fpre0, dg_fpre1], axis=0)
    small["ln_ffn_post"] = jnp.concatenate([dg_fpost0, dg_fpost1], axis=0)
    small["ln_mem"] = jnp.concatenate([dg_mem0, dg_mem1], axis=0)
    small["w_spatial"] = dw_s[None]
    small["b_spatial"] = db_s[:, :A_GROUPS].T[None]
    small["ln_shared"] = dg_shared[0]
    small["b_forget"] = db_f[:FOX_HEADS, 0]
    small["ln_v_g"] = dg_lnv
    small["ln_v_b"] = db_lnv
    small_rows = jnp.concatenate([_pack_small(small, _SMALL), after(rs_mix0["token"], loss_tile)], axis=0)
    st_small = gather_start([small_rows[None]], [0], "gather_small_grads_start")
    dw_in_a_t = mm_tn(d_proj0, a0, "dw_in_a", tk=896, dep=st_small["token"])
    rs_in_a = scatter_start([dw_in_a_t], [0], "scatter_in_a_start")
    grad_x, (dg_pre0,) = proj_in_grad([(d_proj0, w_in_a_full, vec(ln_mix_pre[0]))], h0, dh_mid0, "in_grad0", dep=rs_in_a["token"])
    st_last = gather_start([dg_pre0.reshape(1, 8, LANES)], [0], "gather_last_grad_start")

    (p_down1,) = scatter_wait(rs_ffn1[0], [0], after(st_last["token"], grad_x[:8, :LANES]), "scatter_down1_wait")
    p_gate1, p_up1 = scatter_wait(rs_ffn1[1], [0, 0], p_down1, "scatter_gate_up1_wait")
    p_out1, p_mkv1 = scatter_wait(rs_mix1, [0, 0], p_gate1, "scatter_mix1_wait")
    p_in_b, p_kv = scatter_wait(rs_2, [0, 0], p_out1, "scatter_shared_wait")
    (p_down0,) = scatter_wait(rs_ffn0[0], [0], p_in_b, "scatter_down0_wait")
    p_gate0, p_up0 = scatter_wait(rs_ffn0[1], [0, 0], p_down0, "scatter_gate_up0_wait")
    p_out0, p_mkv0 = scatter_wait(rs_mix0, [0, 0], p_gate0, "scatter_mix0_wait")
    owned_parts = dict(w_ffn_gate=[p_gate0, p_gate1], w_ffn_up=[p_up0, p_up1], w_ffn_down=[p_down0, p_down1], w_out=[p_out0, p_out1],
                       w_mem_kv=[p_mkv0, p_mkv1], w_in_b=[p_in_b], w_shared_kv=[p_kv])

    grad_w, delta, new_m, new_v = {}, {}, {}, {}
    transposed = ("w_ffn_gate", "w_ffn_up", "w_in_a")

    def adamw_sharded(n, parts):
        shape = weights[n].shape
        three_d = shape if len(shape) == 3 else (1,) + shape
        view = (lambda t: t.reshape(three_d).transpose(0, 2, 1)) if n in transposed else (lambda t: t.reshape(three_d))
        back = (lambda t: t.transpose(0, 2, 1).reshape(shape)) if n in transposed else (lambda t: t.reshape(shape))
        w_view = view(weights[n])
        outs = adamw_owned(w_view, parts, view(mom_m[n]), view(mom_v[n]), f"adamw_{n}", tr=_row_tile(w_view.shape[1]))
        grad_w[n], delta[n], new_m[n], new_v[n] = (back(t) for t in outs)

    for n, parts in owned_parts.items():
        adamw_sharded(n, parts)
    (p_in_a,) = scatter_wait(rs_in_a, [0], delta["w_shared_kv"], "scatter_in_a_wait")
    adamw_sharded("w_in_a", [p_in_a])
    (small_all,) = gather_wait(st_small, [0], p_in_a, "gather_small_grads_wait")
    (last_all,) = gather_wait(st_last, [0], small_all, "gather_last_grad_wait")
    small_sum = sum_leading(small_all, "sum_small_grads")
    loss = small_sum[small_rows.shape[0] - 1, 0]
    g_small = _unpack_small(small_sum, _SMALL)
    g_small["ln_mix_pre"] = jnp.concatenate([sum_leading(last_all, "sum_last_grad").reshape(1, D_MODEL), g_small["ln_mix_pre"][1:]], axis=0)
    shard = MAIN_WIDTH // N_DEV
    for n in ("ln_v_g", "ln_v_b"):
        g_small[n] = lax.dynamic_slice_in_dim(g_small[n], me * shard, shard, axis=1)
    grad_w.update(g_small)
    small_local_shapes = [(n, tuple(weights[n].shape)) for n, _ in _SMALL]
    packed = [_pack_small(src, small_local_shapes) for src in (weights, grad_w, mom_m, mom_v)]
    outs = adamw(*packed, "adamw_small", tr=packed[0].shape[0])
    for dst, buf in zip((delta, new_m, new_v), outs):
        dst.update(_unpack_small(buf, small_local_shapes))

    return (loss, grad_x[None], *[grad_w[n] for n in names], *[delta[n] for n in names],
            *[new_m[n] for n in names], *[new_v[n] for n in names])
```

```python
import functools
import math

import jax
import jax.numpy as jnp
from jax import lax
from jax.experimental import pallas as pl
from jax.experimental.pallas import tpu as pltpu

f32 = jnp.float32
bf16 = jnp.bfloat16
SDS = jax.ShapeDtypeStruct

D_MODEL = 1024
MAIN_WIDTH = 768
MEM_WIDTH = 256
HEAD_DIM = 64
MEM_HEADS = 4
FOX_HEADS = 12
FOX_PAIRS = FOX_HEADS // 2
CHUNK = 128
A_GROUPS = 6
FF_SHARD_PAD = 384
KV_WIDTH = 2 * MAIN_WIDTH + FOX_HEADS
KV_PAD = 1792
RMS_EPS = 1e-6
LN_EPS = 1e-5
ATT_SCALE = HEAD_DIM ** -0.5
ADAM_LR, ADAM_B1, ADAM_B2, ADAM_EPS, ADAM_WD, ADAM_STEP = 0.001, 0.9, 0.999, 1e-08, 0.01, 10
N_DEV = 8
MESH = pl.DeviceIdType.MESH
V7X_VMEM_LIMIT = 56 * 1024 * 1024
LANES = 128


def _cparams(*sem):
    return pltpu.CompilerParams(dimension_semantics=sem or None, vmem_limit_bytes=V7X_VMEM_LIMIT)


def _dot(a, b):
    return jnp.dot(a, b, preferred_element_type=f32)


def _dot_nt(a, b):
    return lax.dot_general(a, b, (((1,), (1,)), ((), ())), preferred_element_type=f32)


def _dot_tn(a, b):
    return lax.dot_general(a, b, (((0,), (0,)), ((), ())), preferred_element_type=f32)


def _gelu(x):
    k = math.sqrt(2.0 / math.pi)
    t = jnp.tanh(k * (x + 0.044715 * x * x * x))
    return 0.5 * x * (1.0 + t), t


def _gelu_grad(x, t):
    k = math.sqrt(2.0 / math.pi)
    return 0.5 * (1.0 + t) + 0.5 * x * (1.0 - t * t) * k * (1.0 + 3.0 * 0.044715 * x * x)


def _sigmoid(x):
    return 1.0 / (1.0 + jnp.exp(-x))


def rms_fwd(x, gains, name, tm=512):
    m, d = x.shape
    tm = min(tm, m)
    n = len(gains)

    def body(x_ref, *refs):
        xv = x_ref[...]
        y = xv * lax.rsqrt(jnp.sum(xv * xv, axis=-1, keepdims=True) * (1.0 / d) + RMS_EPS)
        for g_ref, o_ref in zip(refs[:n], refs[n:]):
            o_ref[...] = (y * g_ref[...]).astype(bf16)

    row = pl.BlockSpec((tm, d), lambda i: (i, 0))
    vec = pl.BlockSpec((1, d), lambda i: (0, 0))
    return pl.pallas_call(body, grid=(m // tm,), in_specs=[row] + [vec] * n, out_specs=[row] * n,
                          out_shape=[SDS((m, d), bf16)] * n, name=name, compiler_params=_cparams("parallel"))(x, *gains)


def rms_bwd(x, g, dy, add, out_dtype, name, tm=512):
    m, d = x.shape
    tm = min(tm, m)
    has_add = add is not None

    def body(x_ref, g_ref, dy_ref, *refs):
        dx_ref, dg_ref = refs[-2], refs[-1]
        xv = x_ref[...]
        dyv = dy_ref[...].astype(f32)
        r = lax.rsqrt(jnp.sum(xv * xv, axis=-1, keepdims=True) * (1.0 / d) + RMS_EPS)
        xn = xv * r
        dyg = dyv * g_ref[...]
        dx = r * (dyg - xn * (jnp.sum(dyg * xn, axis=-1, keepdims=True) * (1.0 / d)))
        if has_add:
            dx = dx + refs[0][...]
        dx_ref[...] = dx.astype(out_dtype)

        @pl.when(pl.program_id(0) == 0)
        def _():
            dg_ref[...] = jnp.zeros_like(dg_ref)

        dg_ref[...] += jnp.sum(dyv * xn, axis=0, keepdims=True)

    row = pl.BlockSpec((tm, d), lambda i: (i, 0))
    vec = pl.BlockSpec((1, d), lambda i: (0, 0))
    ins = [x, g, dy] + ([add] if has_add else [])
    return pl.pallas_call(body, grid=(m // tm,), in_specs=[row, vec, row] + ([row] if has_add else []),
                          out_specs=[row, vec], out_shape=[SDS((m, d), out_dtype), SDS((1, d), f32)], name=name,
                          compiler_params=_cparams("arbitrary"))(*ins)


def mm(a, b, name, trans_b=False, out_dtype=f32, tm=1024, tn=1024, col0=0, ncols=None, dep=None):
    m, k = a.shape
    n_all = b.shape[0] if trans_b else b.shape[1]
    n = n_all if ncols is None else ncols
    tm, tn = min(tm, m), min(tn, n)
    assert m % tm == 0 and n % tn == 0 and col0 % tn == 0 and not (trans_b and col0), (name, m, n, tm, tn)
    jb = col0 // tn

    def body(a_ref, b_ref, *rest):
        r = _dot_nt(a_ref[...], b_ref[...]) if trans_b else _dot(a_ref[...], b_ref[...])
        rest[-1][...] = r.astype(out_dtype)

    if trans_b:
        b_spec = pl.BlockSpec((tn, k), lambda j, i: (j, 0))
    else:
        b_spec = pl.BlockSpec((k, tn), lambda j, i: (0, jb + j))
    deps = [] if dep is None else [dep]
    dep_specs = [pl.BlockSpec((8, LANES), lambda j, i: (0, 0))] * len(deps)
    return pl.pallas_call(body, grid=(n // tn, m // tm), in_specs=[pl.BlockSpec((tm, k), lambda j, i: (i, 0)), b_spec] + dep_specs,
                          out_specs=pl.BlockSpec((tm, tn), lambda j, i: (i, j)), out_shape=SDS((m, n), out_dtype),
                          name=name, compiler_params=_cparams("parallel", "parallel"))(a, b, *deps)


def mm_tn(a, g, name, tk=1024, tn=1024, out_dtype=bf16, dep=None):
    s, k = a.shape
    n = g.shape[1]
    tk, tn = min(tk, k), min(tn, n)
    assert k % tk == 0 and n % tn == 0, (name, k, n, tk, tn)

    def body(a_ref, g_ref, *rest):
        rest[-1][...] = _dot_tn(a_ref[...], g_ref[...]).astype(out_dtype)

    deps = [] if dep is None else [dep]
    dep_specs = [pl.BlockSpec((8, LANES), lambda i, j: (0, 0))] * len(deps)
    return pl.pallas_call(body, grid=(k // tk, n // tn),
                          in_specs=[pl.BlockSpec((s, tk), lambda i, j: (0, i)), pl.BlockSpec((s, tn), lambda i, j: (0, j))] + dep_specs,
                          out_specs=pl.BlockSpec((tk, tn), lambda i, j: (i, j)), out_shape=SDS((k, n), out_dtype), name=name,
                          compiler_params=_cparams("parallel", "parallel"))(a, g, *deps)


def _resident(shape, index_map):
    return pl.BlockSpec(shape, index_map, pipeline_mode=pl.Buffered(1))


def _rms(xv):
    return xv * lax.rsqrt(jnp.sum(xv * xv, axis=-1, keepdims=True) * (1.0 / xv.shape[-1]) + RMS_EPS)


def _rms_bwd_math(xv, g, dy):
    d = xv.shape[-1]
    r = lax.rsqrt(jnp.sum(xv * xv, axis=-1, keepdims=True) * (1.0 / d) + RMS_EPS)
    xn = xv * r
    dyg = dy * g
    dx = r * (dyg - xn * (jnp.sum(dyg * xn, axis=-1, keepdims=True) * (1.0 / d)))
    return dx, jnp.sum(dy * xn, axis=0, keepdims=True)


SUB_ROWS = 512


def mm_resnorm(a, b, h, g_post, gains, name, tm=512):
    m, k = a.shape
    d = b.shape[1]
    n = len(gains)

    def body(a_ref, b_ref, h_ref, gp_ref, *refs):
        for r in range(tm // SUB_ROWS):
            rows = slice(r * SUB_ROWS, (r + 1) * SUB_ROWS)
            y = _dot(a_ref[rows, :], b_ref[...])
            refs[n][rows, :] = y
            hn = h_ref[rows, :] + _rms(y) * gp_ref[...]
            refs[n + 1][rows, :] = hn
            if n:
                z = _rms(hn)
                for g_ref, o_ref in zip(refs[:n], refs[n + 2:]):
                    o_ref[rows, :] = (z * g_ref[...]).astype(bf16)

    row = pl.BlockSpec((tm, d), lambda i: (i, 0))
    vec = pl.BlockSpec((1, d), lambda i: (0, 0))
    return pl.pallas_call(body, grid=(m // tm,),
                          in_specs=[pl.BlockSpec((tm, k), lambda i: (i, 0)), _resident((k, d), lambda i: (0, 0)), row, vec] + [vec] * n,
                          out_specs=[row] * (n + 2), out_shape=[SDS((m, d), f32)] * 2 + [SDS((m, d), bf16)] * n, name=name,
                          compiler_params=_cparams("parallel"))(a, b, h, g_post, *gains)


def mm_resnorm_loss(a, b, h, g_post, tgt, name, tm=512):
    m, k = a.shape
    d = b.shape[1]

    def body(a_ref, b_ref, h_ref, gp_ref, t_ref, dh_ref, dy_ref, dg_ref, l_ref):
        @pl.when(pl.program_id(0) == 0)
        def _():
            dg_ref[...] = jnp.zeros_like(dg_ref)
            l_ref[...] = jnp.zeros_like(l_ref)

        y = _dot(a_ref[...], b_ref[...])
        e = h_ref[...] + _rms(y) * gp_ref[...] - t_ref[...]
        dh = e * (1.0 / d)
        dh_ref[...] = dh
        part = jnp.sum(jnp.sum(e * e, axis=-1, keepdims=True), axis=0, keepdims=True) * (0.5 / d)
        l_ref[...] += jnp.broadcast_to(part, l_ref.shape)
        dy, dg = _rms_bwd_math(y, gp_ref[...], dh)
        dy_ref[...] = dy.astype(bf16)
        dg_ref[...] += dg

    row = pl.BlockSpec((tm, d), lambda i: (i, 0))
    vec = pl.BlockSpec((1, d), lambda i: (0, 0))
    return pl.pallas_call(body, grid=(m // tm,),
                          in_specs=[pl.BlockSpec((tm, k), lambda i: (i, 0)), _resident((k, d), lambda i: (0, 0)), row, vec, row],
                          out_specs=[row, row, vec, pl.BlockSpec((8, LANES), lambda i: (0, 0))],
                          out_shape=[SDS((m, d), f32), SDS((m, d), bf16), SDS((1, d), f32), SDS((8, LANES), f32)], name=name,
                          compiler_params=_cparams("arbitrary"))(a, b, h, g_post, tgt)


def ffn_act_grad(d_y2, w_d, factors, name, tm=1024, tn=1536):
    s, d = d_y2.shape
    ff = w_d.shape[0]
    nb = ff // tn

    def body(a_ref, b_ref, g_ref, u_ref, dg_ref, du_ref):
        av = a_ref[...]
        tc = 256
        for c in range(tn // tc):
            cols = slice(c * tc, (c + 1) * tc)
            da = _dot_nt(av, b_ref[cols, :])
            dg_ref[:, cols] = (da * g_ref[:, cols].astype(f32)).astype(bf16)
            du_ref[:, cols] = (da * u_ref[:, cols].astype(f32)).astype(bf16)

    tile = pl.BlockSpec((tm, tn), lambda j, i: (i, j))
    return pl.pallas_call(body, grid=(nb, s // tm),
                          in_specs=[pl.BlockSpec((tm, d), lambda j, i: (i, 0)), pl.BlockSpec((tn, d), lambda j, i: (j, 0)), tile,
                                    pl.BlockSpec((tm, tn), lambda j, i: (i, nb + j))],
                          out_specs=[tile, tile], out_shape=[SDS((s, ff), bf16)] * 2, name=name,
                          compiler_params=_cparams("parallel", "parallel"))(d_y2, w_d, factors, factors)


def ffn_in_grad(d_g, d_u, w_g, w_u, hmid, dh_out, g_pre, y1, g_post, name, tm=512):
    s, ff = d_g.shape
    d = w_g.shape[0]

    def body(dg_ref, du_ref, wg_ref, wu_ref, hm_ref, dho_ref, gpre_ref, y1_ref, gpost_ref, dhm_ref, dy1_ref, dgpre_ref, dgpost_ref):
        @pl.when(pl.program_id(0) == 0)
        def _():
            dgpre_ref[...] = jnp.zeros_like(dgpre_ref)
            dgpost_ref[...] = jnp.zeros_like(dgpost_ref)

        for r in range(tm // SUB_ROWS):
            rows = slice(r * SUB_ROWS, (r + 1) * SUB_ROWS)
            d_f = _dot_nt(dg_ref[rows, :], wg_ref[...]) + _dot_nt(du_ref[rows, :], wu_ref[...])
            dx, dg1 = _rms_bwd_math(hm_ref[rows, :], gpre_ref[...], d_f)
            dh_mid = dho_ref[rows, :] + dx
            dhm_ref[rows, :] = dh_mid
            dgpre_ref[...] += dg1
            dy1, dg2 = _rms_bwd_math(y1_ref[rows, :], gpost_ref[...], dh_mid)
            dy1_ref[rows, :] = dy1.astype(bf16)
            dgpost_ref[...] += dg2

    row = pl.BlockSpec((tm, d), lambda i: (i, 0))
    vec = pl.BlockSpec((1, d), lambda i: (0, 0))
    wide = pl.BlockSpec((tm, ff), lambda i: (i, 0))
    w_spec = _resident((d, ff), lambda i: (0, 0))
    return pl.pallas_call(body, grid=(s // tm,), in_specs=[wide, wide, w_spec, w_spec, row, row, vec, row, vec],
                          out_specs=[row, row, vec, vec], out_shape=[SDS((s, d), f32), SDS((s, d), bf16), SDS((1, d), f32), SDS((1, d), f32)],
                          name=name, compiler_params=_cparams("arbitrary"))(d_g, d_u, w_g, w_u, hmid, dh_out, g_pre, y1, g_post)


def proj_in_grad(pairs, x, add, name, tm=512, dep=None, below=None):
    s, d = x.shape
    n = len(pairs)
    extra = [] if dep is None else [dep]
    n_below = 0 if below is None else 2

    def body(*refs):
        x_ref, add_ref = refs[3 * n], refs[3 * n + 1]
        below_refs = refs[3 * n + 2:3 * n + 2 + n_below]
        outs = refs[3 * n + 2 + n_below + len(extra):]

        @pl.when(pl.program_id(0) == 0)
        def _():
            for o in outs[1:1 + n] + outs[2 + n:]:
                o[...] = jnp.zeros_like(o)

        xv = x_ref[...]
        dx = add_ref[...]
        for i in range(n):
            a_ref, b_ref, g_ref = refs[3 * i:3 * i + 3]
            dxi, dgi = _rms_bwd_math(xv, g_ref[...], _dot_nt(a_ref[...], b_ref[...]))
            dx = dx + dxi
            outs[1 + i][...] += dgi
        outs[0][...] = dx
        if below is not None:
            dy, dg = _rms_bwd_math(below_refs[0][...], below_refs[1][...], dx)
            outs[1 + n][...] = dy.astype(bf16)
            outs[2 + n][...] += dg

    row = pl.BlockSpec((tm, d), lambda i: (i, 0))
    vec = pl.BlockSpec((1, d), lambda i: (0, 0))
    in_specs, args = [], []
    for a, b, g in pairs:
        k = a.shape[1]
        in_specs += [pl.BlockSpec((tm, k), lambda i: (i, 0)), _resident((d, k), lambda i: (0, 0)), vec]
        args += [a, b, g]
    in_specs += [row, row] + [row, vec][:n_below] + [pl.BlockSpec((8, LANES), lambda i: (0, 0))] * len(extra)
    out_specs = [row] + [vec] * n + [row, vec][:n_below]
    out_shape = [SDS((s, d), f32)] + [SDS((1, d), f32)] * n + [SDS((s, d), bf16), SDS((1, d), f32)][:n_below]
    out = pl.pallas_call(body, grid=(s // tm,), in_specs=in_specs, out_specs=out_specs, out_shape=out_shape, name=name,
                         compiler_params=_cparams("arbitrary"))(*args, x, add, *(below or ()), *extra)
    return (out[0], out[1:1 + n]) + tuple(out[1 + n:])


def ffn_up(f, wg, wu, name, tm=512, tc=256):
    s, d = f.shape
    ff = wg.shape[-1]

    def body(f_ref, wg_ref, wu_ref, fac_ref, act_ref):
        fv = f_ref[...]
        for j in range(ff // tc):
            lo = j * tc
            gg = _dot(fv, wg_ref[:, lo:lo + tc])
            uu = _dot(fv, wu_ref[:, lo:lo + tc])
            sg = _sigmoid(gg)
            silu = gg * sg
            fac_ref[:, lo:lo + tc] = (uu * (sg + silu * (1.0 - sg))).astype(bf16)
            fac_ref[:, ff + lo:ff + lo + tc] = silu.astype(bf16)
            act_ref[:, lo:lo + tc] = (silu * uu).astype(bf16)

    w_spec = _resident((d, ff), lambda i: (0, 0))
    return pl.pallas_call(body, grid=(s // tm,), in_specs=[pl.BlockSpec((tm, d), lambda i: (i, 0)), w_spec, w_spec],
                          out_specs=[pl.BlockSpec((tm, 2 * ff), lambda i: (i, 0)), pl.BlockSpec((tm, ff), lambda i: (i, 0))],
                          out_shape=[SDS((s, 2 * ff), bf16), SDS((s, ff), bf16)], name=name,
                          compiler_params=_cparams("parallel"))(f, wg, wu)


def _gmlp_forward_chunk(u, v, w_refs, bias, ln_g, ln_b):
    gu, tu = _gelu(u)
    gv, tv = _gelu(v)
    mu = jnp.sum(gv, axis=-1, keepdims=True) * (1.0 / MAIN_WIDTH)
    xc = gv - mu
    rstd = lax.rsqrt(jnp.sum(xc * xc, axis=-1, keepdims=True) * (1.0 / MAIN_WIDTH) + LN_EPS)
    xhat = xc * rstd
    vln = xhat * ln_g + ln_b
    row = lax.broadcasted_iota(jnp.int32, (CHUNK, CHUNK), 0)
    col = lax.broadcasted_iota(jnp.int32, (CHUNK, CHUNK), 1)
    s_parts = []
    for g in range(A_GROUPS):
        w = jnp.where(col <= row, w_refs[g], jnp.zeros((), bf16))
        s_parts.append(_dot(w, vln[:, g * CHUNK:(g + 1) * CHUNK].astype(bf16)) + bias[:, g:g + 1])
    return gu, tu, tv, rstd, xhat, vln, s_parts


def gmlp_fwd(proj, ws, bs_t, ln_g, ln_b, name, tm=512, out_width=MAIN_WIDTH):
    s = proj.shape[0]

    def body(u_ref, v_ref, w_ref, b_ref, g_ref, bb_ref, o_ref):
        bias = b_ref[...]
        for c in range(tm // CHUNK):
            rows = slice(c * CHUNK, (c + 1) * CHUNK)
            gu, _, _, _, _, _, s_parts = _gmlp_forward_chunk(u_ref[rows, :], v_ref[rows, :], w_ref, bias, g_ref[...], bb_ref[...])
            for g in range(A_GROUPS):
                cols = slice(g * CHUNK, (g + 1) * CHUNK)
                o_ref[rows, cols] = (gu[:, cols] * s_parts[g]).astype(bf16)

    vec = pl.BlockSpec((1, MAIN_WIDTH), lambda i: (0, 0))
    return pl.pallas_call(
        body, grid=(s // tm,),
        in_specs=[pl.BlockSpec((tm, MAIN_WIDTH), lambda i: (i, 0)), pl.BlockSpec((tm, MAIN_WIDTH), lambda i: (i, 1)),
                  pl.BlockSpec((A_GROUPS, CHUNK, CHUNK), lambda i: (0, 0, 0)), pl.BlockSpec((CHUNK, A_GROUPS), lambda i: (0, 0)), vec, vec],
        out_specs=pl.BlockSpec((tm, MAIN_WIDTH), lambda i: (i, 0)), out_shape=SDS((s, out_width), bf16), name=name,
        compiler_params=_cparams("parallel"))(proj, proj, ws, bs_t, ln_g, ln_b)


def gmlp_bwd(proj, d_mixed, ws, ws_t, bs_t, ln_g, ln_b, name, tm=512, out_width=2 * MAIN_WIDTH):
    s = proj.shape[0]

    def body(u_ref, v_ref, dm_ref, w_ref, wt_ref, b_ref, g_ref, bb_ref, duv_ref, dw_ref, db_ref, dg_ref, dbb_ref):
        @pl.when(pl.program_id(0) == 0)
        def _():
            dw_ref[...] = jnp.zeros_like(dw_ref)
            db_ref[...] = jnp.zeros_like(db_ref)
            dg_ref[...] = jnp.zeros_like(dg_ref)
            dbb_ref[...] = jnp.zeros_like(dbb_ref)

        bias = b_ref[...]
        ln_gv = g_ref[...]
        row = lax.broadcasted_iota(jnp.int32, (CHUNK, CHUNK), 0)
        col = lax.broadcasted_iota(jnp.int32, (CHUNK, CHUNK), 1)
        lane = lax.broadcasted_iota(jnp.int32, (CHUNK, LANES), 1)
        for c in range(tm // CHUNK):
            rows = slice(c * CHUNK, (c + 1) * CHUNK)
            u = u_ref[rows, :]
            v = v_ref[rows, :]
            gu, tu, tv, rstd, xhat, vln, s_parts = _gmlp_forward_chunk(u, v, w_ref, bias, ln_gv, bb_ref[...])
            dm = dm_ref[rows, :]
            d_vln_parts = []
            d_gu_parts = []
            db_acc = jnp.zeros((CHUNK, LANES), f32)
            for g in range(A_GROUPS):
                cols = slice(g * CHUNK, (g + 1) * CHUNK)
                dmg = dm[:, cols]
                d_gu_parts.append(dmg * s_parts[g])
                d_s = dmg * gu[:, cols]
                db_acc = db_acc + jnp.where(lane == g, jnp.sum(d_s, axis=-1, keepdims=True), 0.0)
                d_sb = d_s.astype(bf16)
                dw_ref[g] += jnp.where(col <= row, _dot_nt(d_sb, vln[:, cols].astype(bf16)), 0.0)
                wt = jnp.where(row <= col, wt_ref[g], jnp.zeros((), bf16))
                d_vln_parts.append(_dot(wt, d_sb))
            db_ref[...] += db_acc
            d_vln = jnp.concatenate(d_vln_parts, axis=-1)
            d_gu = jnp.concatenate(d_gu_parts, axis=-1)
            dg_ref[...] += jnp.sum(d_vln * xhat, axis=0, keepdims=True)
            dbb_ref[...] += jnp.sum(d_vln, axis=0, keepdims=True)
            dxh = d_vln * ln_gv
            m1 = jnp.sum(dxh, axis=-1, keepdims=True) * (1.0 / MAIN_WIDTH)
            m2 = jnp.sum(dxh * xhat, axis=-1, keepdims=True) * (1.0 / MAIN_WIDTH)
            d_gv = rstd * (dxh - m1 - xhat * m2)
            duv_ref[rows, :MAIN_WIDTH] = (d_gu * _gelu_grad(u, tu)).astype(bf16)
            duv_ref[rows, MAIN_WIDTH:] = (d_gv * _gelu_grad(v, tv)).astype(bf16)

    vec = pl.BlockSpec((1, MAIN_WIDTH), lambda i: (0, 0))
    wspec = pl.BlockSpec((A_GROUPS, CHUNK, CHUNK), lambda i: (0, 0, 0))
    return pl.pallas_call(
        body, grid=(s // tm,),
        in_specs=[pl.BlockSpec((tm, MAIN_WIDTH), lambda i: (i, 0)), pl.BlockSpec((tm, MAIN_WIDTH), lambda i: (i, 1)),
                  pl.BlockSpec((tm, MAIN_WIDTH), lambda i: (i, 0)), wspec, wspec, pl.BlockSpec((CHUNK, A_GROUPS), lambda i: (0, 0)), vec, vec],
        out_specs=[pl.BlockSpec((tm, 2 * MAIN_WIDTH), lambda i: (i, 0)), wspec, pl.BlockSpec((CHUNK, LANES), lambda i: (0, 0)), vec, vec],
        out_shape=[SDS((s, out_width), bf16), SDS((A_GROUPS, CHUNK, CHUNK), f32), SDS((CHUNK, LANES), f32),
                   SDS((1, MAIN_WIDTH), f32), SDS((1, MAIN_WIDTH), f32)],
        name=name, compiler_params=_cparams("arbitrary"))(proj, proj, d_mixed, ws, ws_t, bs_t, ln_g, ln_b)


def _head_mask(width, h):
    lane = lax.broadcasted_iota(jnp.int32, (1, width), 1)
    return (lane >= h * HEAD_DIM) & (lane < (h + 1) * HEAD_DIM)


def mem_attn_fwd(proj, q_block, kv, into, name, tm=512):
    s = proj.shape[0]
    n_mem = kv.shape[0]
    out_block = into.shape[1] // MEM_WIDTH - 1

    def body(q_ref, kv_ref, into_ref, o_ref):
        q = q_ref[...].astype(f32)
        k = kv_ref[:, :MEM_WIDTH].astype(bf16)
        v = kv_ref[:, MEM_WIDTH:].astype(bf16)
        out = jnp.zeros((tm, MEM_WIDTH), f32)
        for h in range(MEM_HEADS):
            msk = _head_mask(MEM_WIDTH, h)
            qh = jnp.where(msk, q, 0.0).astype(bf16)
            sc = _dot_nt(qh, k) * ATT_SCALE
            e = jnp.exp(sc - jnp.max(sc, axis=-1, keepdims=True))
            p = e / jnp.sum(e, axis=-1, keepdims=True)
            out = jnp.where(msk, _dot(p.astype(bf16), v), out)
        o_ref[...] = out.astype(bf16)

    return pl.pallas_call(body, grid=(s // tm,),
                          in_specs=[pl.BlockSpec((tm, MEM_WIDTH), lambda i: (i, q_block)), pl.BlockSpec((n_mem, 2 * MEM_WIDTH), lambda i: (0, 0)), _ANY],
                          out_specs=pl.BlockSpec((tm, MEM_WIDTH), lambda i: (i, out_block)), out_shape=SDS(into.shape, bf16), name=name,
                          input_output_aliases={2: 0}, compiler_params=_cparams("parallel"))(proj, kv, into)


def mem_attn_bwd(proj, q_block, kv, d_mixed, into, name, tm=512):
    s = proj.shape[0]
    n_mem = kv.shape[0]
    out_block = into.shape[1] // MEM_WIDTH - 1

    def body(q_ref, kv_ref, do_ref, into_ref, dq_ref, dkv_ref):
        @pl.when(pl.program_id(0) == 0)
        def _():
            dkv_ref[...] = jnp.zeros_like(dkv_ref)

        q = q_ref[...].astype(f32)
        do = do_ref[...]
        k = kv_ref[:, :MEM_WIDTH].astype(bf16)
        v = kv_ref[:, MEM_WIDTH:].astype(bf16)
        dq = jnp.zeros((tm, MEM_WIDTH), f32)
        dk = jnp.zeros((n_mem, MEM_WIDTH), f32)
        dv = jnp.zeros((n_mem, MEM_WIDTH), f32)
        for h in range(MEM_HEADS):
            msk = _head_mask(MEM_WIDTH, h)
            qh = jnp.where(msk, q, 0.0).astype(bf16)
            doh = jnp.where(msk, do, 0.0).astype(bf16)
            sc = _dot_nt(qh, k) * ATT_SCALE
            e = jnp.exp(sc - jnp.max(sc, axis=-1, keepdims=True))
            p = e / jnp.sum(e, axis=-1, keepdims=True)
            dp = _dot_nt(doh, v)
            ds = p * (dp - jnp.sum(dp * p, axis=-1, keepdims=True))
            dsb = (ds * ATT_SCALE).astype(bf16)
            dq = jnp.where(msk, _dot(dsb, k), dq)
            dk = dk + _dot_tn(dsb, qh)
            dv = dv + _dot_tn(p.astype(bf16), doh)
        dq_ref[...] = dq.astype(bf16)
        dkv_ref[:, :MEM_WIDTH] += dk
        dkv_ref[:, MEM_WIDTH:] += dv

    return pl.pallas_call(
        body, grid=(s // tm,),
        in_specs=[pl.BlockSpec((tm, MEM_WIDTH), lambda i: (i, q_block)), pl.BlockSpec((n_mem, 2 * MEM_WIDTH), lambda i: (0, 0)),
                  pl.BlockSpec((tm, MEM_WIDTH), lambda i: (i, MAIN_WIDTH // MEM_WIDTH)), _ANY],
        out_specs=[pl.BlockSpec((tm, MEM_WIDTH), lambda i: (i, out_block)), pl.BlockSpec((n_mem, 2 * MEM_WIDTH), lambda i: (0, 0))],
        out_shape=[SDS(into.shape, bf16), SDS((n_mem, 2 * MEM_WIDTH), f32)], name=name,
        input_output_aliases={3: 0}, compiler_params=_cparams("arbitrary"))(proj, kv, d_mixed, into)


def _tri(t, upper):
    r = lax.broadcasted_iota(jnp.int32, (t, t), 0)
    c = lax.broadcasted_iota(jnp.int32, (t, t), 1)
    return ((r <= c) if upper else (r >= c)).astype(f32)


def fgate_fwd(z_t, b, name, t=512):
    hh, s = z_t.shape

    def body(z_ref, b_ref, c_ref):
        u = _tri(t, True)
        carry = jnp.zeros((hh, 1), f32)
        for blk in range(s // t):
            x = z_ref[:, blk * t:(blk + 1) * t] + b_ref[...]
            logf = jnp.minimum(x, 0.0) - jnp.log(1.0 + jnp.exp(-jnp.abs(x)))
            y = jnp.dot(logf, u, precision=lax.Precision.HIGHEST, preferred_element_type=f32) + carry
            c_ref[:, blk * t:(blk + 1) * t] = y
            carry = y[:, t - 1:t]

    return pl.pallas_call(body, out_shape=SDS((hh, s), f32), name=name, compiler_params=_cparams())(z_t, b)


def fgate_bwd(dc_t, z_t, b, name, t=512):
    hh, s = z_t.shape

    def body(dc_ref, z_ref, b_ref, dz_ref, db_ref):
        low = _tri(t, False)
        carry = jnp.zeros((hh, 1), f32)
        total = jnp.zeros((hh, 1), f32)
        for blk in reversed(range(s // t)):
            cols = slice(blk * t, (blk + 1) * t)
            y = jnp.dot(dc_ref[:, cols], low, precision=lax.Precision.HIGHEST, preferred_element_type=f32) + carry
            carry = y[:, 0:1]
            dz = y * _sigmoid(-(z_ref[:, cols] + b_ref[...]))
            dz_ref[:, cols] = dz
            total = total + jnp.sum(dz, axis=-1, keepdims=True)
        db_ref[...] = jnp.broadcast_to(total, db_ref.shape)

    return pl.pallas_call(body, out_shape=[SDS((hh, s), f32), SDS((hh, LANES), f32)], name=name,
                          compiler_params=_cparams())(dc_t, z_t, b)


def _pair_masks():
    lane = lax.broadcasted_iota(jnp.int32, (1, LANES), 1)
    return [lane < HEAD_DIM, lane >= HEAD_DIM]


def _tile_base(cr_ref, hh, lo):
    return cr_ref[hh:hh + 1, pl.ds(lo, LANES)][:, 0:1]


def fox_fwd(q, kv, c_row, name, tq=512, out_width=MAIN_WIDTH):
    s = kv.shape[0]
    nq = s // tq

    def body(q_ref, k_ref, v_ref, cr_ref, o_ref, lse_ref, ob_ref):
        i = pl.program_id(1)
        qv = q_ref[...]
        masks = _pair_masks()
        row = lax.broadcasted_iota(jnp.int32, (tq, tq), 0)
        col = lax.broadcasted_iota(jnp.int32, (tq, tq), 1)
        qh = [jnp.where(masks[hh], qv, jnp.zeros((), bf16)) * ATT_SCALE for hh in range(2)]
        ct = [_tile_base(cr_ref, hh, pl.multiple_of(i * tq, tq)) for hh in range(2)]

        def block(j, carry, diag):
            lo = pl.multiple_of(j * tq, tq)
            ks = k_ref[pl.ds(lo, tq), :]
            vs = v_ref[pl.ds(lo, tq), :]
            out = []
            for hh in range(2):
                m, l, acc = carry[hh]
                sc = _dot_nt(qh[hh], ks) + (ct[hh] - cr_ref[hh:hh + 1, pl.ds(lo, tq)])
                if diag:
                    sc = jnp.where(col <= row, sc, -jnp.inf)
                m_new = jnp.maximum(m, jnp.max(sc, axis=-1, keepdims=True))
                alpha = jnp.exp(m - m_new)
                p = jnp.exp(sc - m_new)
                l = alpha * l + jnp.sum(p, axis=-1, keepdims=True)
                p_hi = p.astype(bf16)
                p_lo = (p - p_hi.astype(f32)).astype(bf16)
                acc = alpha * acc + (_dot(p_hi, vs) + _dot(p_lo, vs))
                out.append((m_new, l, acc))
            return tuple(out)

        init = (jnp.full((tq, 1), -jnp.inf, f32), jnp.zeros((tq, 1), f32), jnp.zeros((tq, LANES), f32))
        carry = lax.fori_loop(0, i, functools.partial(block, diag=False), (init, init))
        res = [(acc / l, m + jnp.log(l)) for m, l, acc in block(i, carry, True)]
        out = jnp.where(masks[0], res[0][0], res[1][0])
        o_ref[...] = out
        ob_ref[...] = out.astype(bf16)
        lse_ref[...] = jnp.where(masks[0], res[0][1], res[1][1])

    return pl.pallas_call(
        body, grid=(FOX_PAIRS, nq),
        in_specs=[pl.BlockSpec((tq, LANES), lambda p, i: (i, p)), pl.BlockSpec((s, LANES), lambda p, i: (0, p)),
                  pl.BlockSpec((s, LANES), lambda p, i: (0, FOX_PAIRS + p)), pl.BlockSpec((None, 2, s), lambda p, i: (p, 0, 0))],
        out_specs=[pl.BlockSpec((tq, LANES), lambda p, i: (i, p)), pl.BlockSpec((None, tq, LANES), lambda p, i: (p, i, 0)),
                   pl.BlockSpec((tq, LANES), lambda p, i: (i, p))],
        out_shape=[SDS((s, MAIN_WIDTH), f32), SDS((FOX_PAIRS, s, LANES), f32), SDS((s, out_width), bf16)], name=name,
        compiler_params=_cparams("parallel", "parallel"))(q, kv, kv, c_row)


def fox_bwd(q, kv, d_mixed, o, lse, c_row, name, tq=512, dq_width=MAIN_WIDTH):
    s = kv.shape[0]
    nq = s // tq

    def body(q_ref, k_ref, v_ref, do_ref, o_ref, lse_ref, cr_ref, dqb_ref, dk_ref, dv_ref, dc_ref, dq_ref):
        j = pl.program_id(1)

        @pl.when(j == 0)
        def _():
            dq_ref[...] = jnp.zeros_like(dq_ref)

        masks = _pair_masks()
        sub = lax.broadcasted_iota(jnp.int32, (LANES, 1), 0)
        sub_masks = [sub < HEAD_DIM, sub >= HEAD_DIM]
        row = lax.broadcasted_iota(jnp.int32, (tq, tq), 0)
        col = lax.broadcasted_iota(jnp.int32, (tq, tq), 1)
        kj = k_ref[...]
        vj = v_ref[...]
        lo_j = pl.multiple_of(j * tq, tq)

        def block(i, carry, diag):
            dk_t, dv_t, dc0, dc1 = carry
            dcs = [dc0, dc1]
            lo = pl.multiple_of(i * tq, tq)
            qi = q_ref[pl.ds(lo, tq), :]
            qi = qi * ATT_SCALE
            qt_i = qi.T
            doi = do_ref[pl.ds(lo, tq), :]
            dot_i = doi.astype(bf16).T
            prod = doi.astype(bf16).astype(f32) * o_ref[pl.ds(lo, tq), :]
            lse_i = lse_ref[pl.ds(lo, tq), :]
            dq_i = jnp.zeros((tq, LANES), f32)
            for hh in range(2):
                qh = jnp.where(masks[hh], qi, jnp.zeros((), bf16))
                doh = jnp.where(masks[hh], doi, 0.0).astype(bf16)
                delta = jnp.sum(jnp.where(masks[hh], prod, 0.0), axis=-1, keepdims=True)
                sc = _dot_nt(qh, kj) + (_tile_base(cr_ref, hh, lo) - cr_ref[hh:hh + 1, pl.ds(lo_j, tq)])
                p = jnp.exp(sc - lse_i[:, hh * HEAD_DIM:hh * HEAD_DIM + 1])
                if diag:
                    p = jnp.where(col <= row, p, 0.0)
                dv_t = dv_t + _dot(jnp.where(sub_masks[hh], dot_i, jnp.zeros((), bf16)), p.astype(bf16))
                ds = p * (_dot_nt(doh, vj) - delta)
                dcs[hh] = dcs[hh] + jnp.sum(ds, axis=0, keepdims=True)
                dsb = ds.astype(bf16)
                dq_i = jnp.where(masks[hh], _dot(dsb, kj), dq_i)
                dk_t = dk_t + _dot(jnp.where(sub_masks[hh], qt_i, jnp.zeros((), bf16)), dsb)
            dq_ref[pl.ds(lo, tq), :] += dq_i * ATT_SCALE
            return dk_t, dv_t, dcs[0], dcs[1]

        zero = jnp.zeros((LANES, tq), f32)
        zrow = jnp.zeros((1, tq), f32)
        carry = block(j, (zero, zero, zrow, zrow), True)
        dk_t, dv_t, dc0, dc1 = lax.fori_loop(j + 1, nq, functools.partial(block, diag=False), carry)
        dk_ref[...] = dk_t.T.astype(bf16)
        dv_ref[...] = dv_t.T.astype(bf16)
        dc_ref[0:1, :] = -dc0
        dc_ref[1:2, :] = -dc1

        @pl.when(j == nq - 1)
        def _():
            dqb_ref[...] = dq_ref[...].astype(bf16)

    full = lambda p, j: (0, p)
    tile = lambda p, j: (j, p)
    return pl.pallas_call(
        body, grid=(FOX_PAIRS, nq),
        in_specs=[pl.BlockSpec((s, LANES), full), pl.BlockSpec((tq, LANES), tile), pl.BlockSpec((tq, LANES), lambda p, j: (j, FOX_PAIRS + p)),
                  pl.BlockSpec((s, LANES), full), pl.BlockSpec((s, LANES), full), pl.BlockSpec((None, s, LANES), lambda p, j: (p, 0, 0)),
                  pl.BlockSpec((None, 2, s), lambda p, j: (p, 0, 0))],
        out_specs=[pl.BlockSpec((s, LANES), full), pl.BlockSpec((tq, LANES), tile), pl.BlockSpec((tq, LANES), tile),
                   pl.BlockSpec((None, 2, tq), lambda p, j: (p, 0, j))],
        out_shape=[SDS((s, dq_width), bf16), SDS((s, MAIN_WIDTH), bf16), SDS((s, MAIN_WIDTH), bf16), SDS((FOX_PAIRS, 2, s), f32)],
        scratch_shapes=[pltpu.VMEM((s, LANES), f32)],
        name=name, compiler_params=_cparams("parallel", "arbitrary"))(q, kv, kv, d_mixed, o, lse, c_row)


def adamw(w, g, m, v, name, tr=256):
    r, c = w.shape
    tr = min(tr, r)
    assert r % tr == 0, (name, r, tr)
    c1 = 1.0 / (1.0 - ADAM_B1 ** ADAM_STEP)
    c2 = 1.0 / (1.0 - ADAM_B2 ** ADAM_STEP)

    def body(w_ref, g_ref, m_ref, v_ref, d_ref, mo_ref, vo_ref):
        gv = g_ref[...]
        mn = ADAM_B1 * m_ref[...] + (1.0 - ADAM_B1) * gv
        vn = ADAM_B2 * v_ref[...] + (1.0 - ADAM_B2) * gv * gv
        mo_ref[...] = mn
        vo_ref[...] = vn
        d_ref[...] = -ADAM_LR * ((mn * c1) / (jnp.sqrt(vn * c2) + ADAM_EPS) + ADAM_WD * w_ref[...])

    spec = pl.BlockSpec((tr, c), lambda i: (i, 0))
    return pl.pallas_call(body, grid=(r // tr,), in_specs=[spec] * 4, out_specs=[spec] * 3, out_shape=[SDS((r, c), f32)] * 3,
                          name=name, compiler_params=_cparams("parallel"))(w, g, m, v)


def adamw_owned(w, parts, m, v, name, tr):
    nl, r, c = w.shape
    cp = parts[0].shape[2]
    assert r % tr == 0 and len(parts) == nl, (name, r, tr)
    c1 = 1.0 / (1.0 - ADAM_B1 ** ADAM_STEP)
    c2 = 1.0 / (1.0 - ADAM_B2 ** ADAM_STEP)

    def body(*refs):
        w_ref, p_refs, (m_ref, v_ref) = refs[0], refs[1:1 + nl], refs[1 + nl:3 + nl]
        g_ref, d_ref, mo_ref, vo_ref = refs[3 + nl:]
        layer = pl.program_id(0)

        def total(p_ref):
            acc = p_ref[0].astype(f32)
            for k in range(1, N_DEV):
                acc = acc + p_ref[k].astype(f32)
            return acc

        gv = total(p_refs[0])
        for l in range(1, nl):
            gv = jnp.where(layer == l, total(p_refs[l]), gv)
        gv = gv[:, :c]
        g_ref[...] = gv
        mn = ADAM_B1 * m_ref[...] + (1.0 - ADAM_B1) * gv
        vn = ADAM_B2 * v_ref[...] + (1.0 - ADAM_B2) * gv * gv
        mo_ref[...] = mn
        vo_ref[...] = vn
        d_ref[...] = -ADAM_LR * ((mn * c1) / (jnp.sqrt(vn * c2) + ADAM_EPS) + ADAM_WD * w_ref[...])

    spec = pl.BlockSpec((None, tr, c), lambda l, i: (l, i, 0))
    last = r // tr - 1

    def part_spec(mine):
        return pl.BlockSpec((N_DEV, tr, cp), lambda l, i: (0, jnp.where(l == mine, i, jnp.where(l < mine, 0, last)), 0))

    return pl.pallas_call(body, grid=(nl, r // tr), in_specs=[spec] + [part_spec(l) for l in range(nl)] + [spec, spec], out_specs=[spec] * 4,
                          out_shape=[SDS((nl, r, c), f32)] * 4, name=name,
                          compiler_params=_cparams("parallel", "parallel"))(w, *parts, m, v)


def sum_leading(x, name, out_dtype=f32, tr=None):
    n, r, c = x.shape
    tr = tr or r
    assert r % tr == 0

    def body(x_ref, o_ref):
        acc = x_ref[0].astype(f32)
        for k in range(1, n):
            acc = acc + x_ref[k].astype(f32)
        o_ref[...] = acc.astype(out_dtype)

    return pl.pallas_call(body, grid=(r // tr,), in_specs=[pl.BlockSpec((n, tr, c), lambda i: (0, i, 0))],
                          out_specs=pl.BlockSpec((tr, c), lambda i: (i, 0)), out_shape=SDS((r, c), out_dtype), name=name,
                          compiler_params=_cparams("parallel"))(x)


_ANY = pl.BlockSpec(memory_space=pl.ANY)
_DMA = pltpu.SemaphoreType.DMA


_HBM = pl.BlockSpec(memory_space=pltpu.HBM)
_SEM = pl.BlockSpec(memory_space=pltpu.SEMAPHORE)
_EFFECT = pltpu.SideEffectType.DATAFLOW_SIDE_EFFECTING
_FLIPS = [(0, 0, 1), (1, 0, 0), (0, 1, 0), (1, 1, 0), (1, 0, 1), (0, 1, 1), (1, 1, 1)]


def _me():
    return lax.axis_index("x"), lax.axis_index("y"), lax.axis_index("c")


def _peers():
    mx, my, mc = _me()
    return [(jnp.bitwise_xor(mx, fx), jnp.bitwise_xor(my, fy), jnp.bitwise_xor(mc, fc)) for fx, fy, fc in _FLIPS]


def _index(dev):
    return 4 * dev[0] + 2 * dev[1] + dev[2]


def _win(ref, axis, k, size, count=1):
    idx = [slice(None)] * len(ref.shape)
    idx[axis] = pl.ds(k * size, count * size)
    return ref.at[tuple(idx)]


def _hbm(a):
    return pltpu.with_memory_space_constraint(a, pltpu.HBM)


def _exchange_start(srcs, lands, copies_of, name):
    n = len(srcs)

    def body(*refs):
        src = refs[:n]
        send_sems, recv_sems, self_sems = refs[2 * n:2 * n + 3]
        land = refs[3 * n + 3:4 * n + 3]
        token = refs[4 * n + 3]
        me = _index(_me())
        for a in range(n):
            for s_ref, d_ref, peer in copies_of(a, src[a], land[a], me):
                if peer is None:
                    pltpu.make_async_copy(s_ref, d_ref, self_sems.at[a]).start()
                else:
                    pltpu.make_async_remote_copy(src_ref=s_ref, dst_ref=d_ref, send_sem=send_sems.at[a], recv_sem=recv_sems.at[a],
                                                 device_id=peer, device_id_type=MESH).start()
        token[...] = jnp.zeros_like(token)

    outs = pl.pallas_call(
        body, name=name,
        out_shape=(_DMA((n,)), _DMA((n,)), _DMA((n,)), *[pltpu.HBM(s.shape, s.dtype) for s in srcs],
                   *[pltpu.HBM(l.shape, l.dtype) for l in lands], SDS((8, LANES), f32)),
        in_specs=[_HBM] * (2 * n), out_specs=(_SEM, _SEM, _SEM, *[_HBM] * (2 * n), pl.BlockSpec(memory_space=pltpu.VMEM)),
        input_output_aliases={i: 3 + i for i in range(2 * n)},
        compiler_params=pltpu.CompilerParams(has_side_effects=_EFFECT),
    )(*[_hbm(s) for s in srcs], *[_hbm(lax.empty(l.shape, l.dtype)) for l in lands])
    return dict(sems=outs[:3], srcs=list(outs[3:3 + n]), lands=list(outs[3 + n:3 + 2 * n]), token=outs[3 + 2 * n])


def _exchange_wait(started, waits_of, after, name, which=None):
    which = list(range(len(started["srcs"]))) if which is None else which
    srcs, lands = [started["srcs"][a] for a in which], [started["lands"][a] for a in which]
    n = len(which)

    def body(*refs):
        src = refs[:n]
        land = refs[n:2 * n]
        send_sems, recv_sems, self_sems = refs[2 * n:2 * n + 3]
        me = _index(_me())
        for pos, a in enumerate(which):
            seven, (s_ref, d_ref) = waits_of(a, src[pos], land[pos], me)
            both = pltpu.make_async_remote_copy(src_ref=seven, dst_ref=seven, send_sem=send_sems.at[a], recv_sem=recv_sems.at[a],
                                                device_id=_me(), device_id_type=MESH)
            both.wait_send()
            both.wait_recv()
            pltpu.make_async_copy(s_ref, d_ref, self_sems.at[a]).wait()

    outs = pl.pallas_call(
        body, name=name, out_shape=tuple(pltpu.HBM(t.shape, t.dtype) for t in srcs + lands),
        in_specs=[_HBM] * (2 * n) + [_SEM] * 3 + [_ANY], out_specs=tuple([_HBM] * (2 * n)),
        input_output_aliases={i: i for i in range(2 * n)},
        compiler_params=pltpu.CompilerParams(has_side_effects=_EFFECT),
    )(*srcs, *lands, *started["sems"], after)
    return list(outs[n:])


def gather_start(locs, axes, name):
    lands = [SDS(tuple(N_DEV * d if i == ax else d for i, d in enumerate(l.shape)), l.dtype) for l, ax in zip(locs, axes)]

    def copies_of(a, src, land, me):
        mine = _win(land, axes[a], me, src.shape[axes[a]])
        return [(src, mine, peer) for peer in _peers()] + [(src, mine, None)]

    return _exchange_start(locs, lands, copies_of, name)


def gather_wait(started, axes, after, name, which=None):
    def waits_of(a, src, land, me):
        size = src.shape[axes[a]]
        return _win(land, axes[a], 0, size, N_DEV - 1), (src, _win(land, axes[a], me, size))

    return _exchange_wait(started, waits_of, after, name, which)


def _part(ref, axis, k, stride, used):
    idx = [slice(None)] * len(ref.shape)
    idx[axis] = pl.ds(k * stride, used)
    return ref.at[tuple(idx)]


def scatter_start(grads, axes, name, used=None):
    strides = [g.shape[ax] // N_DEV for g, ax in zip(grads, axes)]
    used = used or strides
    lands = [SDS((N_DEV,) + tuple(u if i == ax else d for i, d in enumerate(g.shape)), g.dtype) for g, ax, u in zip(grads, axes, used)]

    def copies_of(a, src, land, me):
        out = [(_part(src, axes[a], _index(peer), strides[a], used[a]), land.at[me], peer) for peer in _peers()]
        return out + [(_part(src, axes[a], me, strides[a], used[a]), land.at[me], None)]

    return _exchange_start(grads, lands, copies_of, name)


def scatter_wait(started, axes, after, name, used=None):
    def waits_of(a, src, land, me):
        stride = src.shape[axes[a]] // N_DEV
        return land.at[pl.ds(0, N_DEV - 1)], (_part(src, axes[a], me, stride, used[a] if used else stride), land.at[me])

    return _exchange_wait(started, waits_of, after, name)


def _row_tile(rows, cap=512):
    return max(t for t in range(8, min(rows, cap) + 1, 8) if rows % t == 0)


_SMALL = [
    ("ln_mix_pre", (2, 1024)), ("ln_mix_post", (2, 1024)), ("ln_ffn_pre", (2, 1024)), ("ln_ffn_post", (2, 1024)),
    ("ln_mem", (2, 1024)), ("w_spatial", (1, 6, 128, 128)), ("b_spatial", (1, 6, 128)), ("ln_shared", (1024,)),
    ("b_forget", (12,)), ("ln_v_g", (1, 768)), ("ln_v_b", (1, 768)),
]
_SMALL_TILE = 8 * LANES


def _small_rows(shape):
    return -(-math.prod(shape) // _SMALL_TILE) * 8


def _pack_small(vals, shapes):
    parts = []
    for name, shape in shapes:
        flat = vals[name].reshape(-1).astype(f32)
        rows = _small_rows(shape)
        parts.append(jnp.pad(flat, (0, rows * LANES - flat.shape[0])).reshape(rows, LANES))
    return jnp.concatenate(parts, axis=0)


def _unpack_small(buf, shapes):
    out = {}
    lo = 0
    for name, shape in shapes:
        rows = _small_rows(shape)
        out[name] = buf[lo:lo + rows].reshape(-1)[:math.prod(shape)].reshape(shape)
        lo += rows
    return out


def kernel(x, mem, ln_mix_pre, ln_mix_post, ln_ffn_pre, ln_ffn_post, ln_mem, w_mem_kv, w_out, w_ffn_gate, w_ffn_up, w_ffn_down, w_in_a, w_spatial, b_spatial, ln_v_g, ln_v_b, ln_shared, w_shared_kv, b_forget, w_in_b, loss_target, m_ln_mix_pre, m_ln_mix_post, m_ln_ffn_pre, m_ln_ffn_post, m_ln_mem, m_w_mem_kv, m_w_out, m_w_ffn_gate, m_w_ffn_up, m_w_ffn_down, m_w_in_a, m_w_spatial, m_b_spatial, m_ln_v_g, m_ln_v_b, m_ln_shared, m_w_shared_kv, m_b_forget, m_w_in_b, v_ln_mix_pre, v_ln_mix_post, v_ln_ffn_pre, v_ln_ffn_post, v_ln_mem, v_w_mem_kv, v_w_out, v_w_ffn_gate, v_w_ffn_up, v_w_ffn_down, v_w_in_a, v_w_spatial, v_b_spatial, v_ln_v_g, v_ln_v_b, v_ln_shared, v_w_shared_kv, v_b_forget, v_w_in_b):
    weights = dict(ln_mix_pre=ln_mix_pre, ln_mix_post=ln_mix_post, ln_ffn_pre=ln_ffn_pre, ln_ffn_post=ln_ffn_post, ln_mem=ln_mem,
                   w_mem_kv=w_mem_kv, w_out=w_out, w_ffn_gate=w_ffn_gate, w_ffn_up=w_ffn_up, w_ffn_down=w_ffn_down, w_in_a=w_in_a,
                   w_spatial=w_spatial, b_spatial=b_spatial, ln_v_g=ln_v_g, ln_v_b=ln_v_b, ln_shared=ln_shared,
                   w_shared_kv=w_shared_kv, b_forget=b_forget, w_in_b=w_in_b)
    mom_m = dict(ln_mix_pre=m_ln_mix_pre, ln_mix_post=m_ln_mix_post, ln_ffn_pre=m_ln_ffn_pre, ln_ffn_post=m_ln_ffn_post, ln_mem=m_ln_mem,
                 w_mem_kv=m_w_mem_kv, w_out=m_w_out, w_ffn_gate=m_w_ffn_gate, w_ffn_up=m_w_ffn_up, w_ffn_down=m_w_ffn_down, w_in_a=m_w_in_a,
                 w_spatial=m_w_spatial, b_spatial=m_b_spatial, ln_v_g=m_ln_v_g, ln_v_b=m_ln_v_b, ln_shared=m_ln_shared,
                 w_shared_kv=m_w_shared_kv, b_forget=m_b_forget, w_in_b=m_w_in_b)
    mom_v = dict(ln_mix_pre=v_ln_mix_pre, ln_mix_post=v_ln_mix_post, ln_ffn_pre=v_ln_ffn_pre, ln_ffn_post=v_ln_ffn_post, ln_mem=v_ln_mem,
                 w_mem_kv=v_w_mem_kv, w_out=v_w_out, w_ffn_gate=v_w_ffn_gate, w_ffn_up=v_w_ffn_up, w_ffn_down=v_w_ffn_down, w_in_a=v_w_in_a,
                 w_spatial=v_w_spatial, b_spatial=v_b_spatial, ln_v_g=v_ln_v_g, ln_v_b=v_ln_v_b, ln_shared=v_ln_shared,
                 w_shared_kv=v_w_shared_kv, b_forget=v_b_forget, w_in_b=v_w_in_b)
    names = list(weights)
    mx, my, mc = lax.axis_index("x"), lax.axis_index("y"), lax.axis_index("c")
    me = 4 * mx + 2 * my + mc

    h0 = x[0]
    mem0 = mem[0]
    tgt = loss_target[0]
    seq = h0.shape[0]

    vec = lambda a: a.reshape(1, -1)
    pad_to = lambda a, axis, size: jnp.pad(a, [(0, size - a.shape[i] if i == axis else 0) for i in range(a.ndim)])

    def after(tok, a):
        return a + tok[0, 0].astype(a.dtype)

    lnv_loc = pad_to(jnp.concatenate([ln_v_g, ln_v_b], axis=0), 0, 8)
    st_a = gather_start([w_in_a.astype(bf16), pad_to(lnv_loc, 1, LANES)[None]], [0, 0], "gather_a_start")
    mix_locs = lambda l, tok: [after(tok, w_mem_kv[l]).astype(bf16), w_out[l].astype(bf16)]

    def ffn_gather_start(l, tok):
        gate_up = gather_start([pad_to(after(tok, w_ffn_gate[l]).astype(bf16), 1, FF_SHARD_PAD),
                                pad_to(w_ffn_up[l].astype(bf16), 1, FF_SHARD_PAD)], [1, 1], f"gather_gate_up{l}_start")
        down = gather_start([pad_to(after(gate_up["token"], w_ffn_down[l]).astype(bf16), 0, FF_SHARD_PAD)], [0], f"gather_down{l}_start")
        return gate_up, down

    st_b = [gather_start(mix_locs(0, st_a["token"]), [0, 0], "gather_b0_start"), None]
    st_c = ffn_gather_start(0, st_b[0]["token"])
    st_d = gather_start([after(st_c[1]["token"], w_in_b[0]).astype(bf16), pad_to(w_shared_kv.astype(bf16), 1, KV_PAD)], [0, 0],
                        "gather_d_start")
    st_b[1] = gather_start(mix_locs(1, st_d["token"]), [0, 0], "gather_b1_start")
    st_e = ffn_gather_start(1, st_b[1]["token"])
    ws = w_spatial[0].astype(bf16)
    ws_t = ws.transpose(0, 2, 1)
    bs_t = b_spatial[0].T

    (a0,) = rms_fwd(h0, [after(st_e[1]["token"], vec(ln_mix_pre[0]))], "a0_norm")
    w_in_a8, lnv8 = gather_wait(st_a, [0, 0], a0, "gather_a_wait")
    w_in_a_full = w_in_a8.transpose(1, 0, 2).reshape(D_MODEL, -1)
    lnv_g = lnv8[:, 0, :MAIN_WIDTH // N_DEV].reshape(1, MAIN_WIDTH)
    lnv_b = lnv8[:, 1, :MAIN_WIDTH // N_DEV].reshape(1, MAIN_WIDTH)
    proj0 = mm(a0, w_in_a_full, "proj0", tn=896)
    main0 = gmlp_fwd(proj0, ws, bs_t, lnv_g, lnv_b, "gmlp_fwd", out_width=D_MODEL)
    w_mkv, w_o = [None, None], [None, None]
    w_mkv[0], w_o[0] = gather_wait(st_b[0], [0, 0], main0, "gather_b0_wait")
    (memn0,) = rms_fwd(mem0, [vec(ln_mem[0])], "mem0_norm")
    kvm0 = mm(memn0, w_mkv[0], "kvm0")
    mixed0 = mem_attn_fwd(proj0, 2 * MAIN_WIDTH // MEM_WIDTH, kvm0, main0, "mem_attn0")
    y1_0, hmid0, f0 = mm_resnorm(mixed0, w_o[0], h0, vec(ln_mix_post[0]), [vec(ln_ffn_pre[0])], "mix_out0")
    w_g0, w_u0 = gather_wait(st_c[0], [1, 1], f0, "gather_gate_up0_wait")
    gu0, act0 = ffn_up(f0, w_g0, w_u0, "ffn_up0")
    (w_d0,) = gather_wait(st_c[1], [0], act0, "gather_down0_wait")
    y2_0, h1, a1, sin1 = mm_resnorm(act0, w_d0, hmid0, vec(ln_ffn_post[0]), [vec(ln_mix_pre[1]), vec(ln_shared)], "ffn_down0")

    w_inb, w_kv = gather_wait(st_d, [0, 0], sin1, "gather_d_wait")
    kvb = mm(sin1, w_kv, "kv_shared", out_dtype=bf16, tn=MAIN_WIDTH, ncols=2 * MAIN_WIDTH)
    zf = mm(sin1, w_kv, "forget_logits", tn=256, col0=2 * MAIN_WIDTH, ncols=256)
    qb = mm(a1, w_inb, "proj1", out_dtype=bf16)
    z_t = jnp.pad(zf[:, :FOX_HEADS].T, ((0, 16 - FOX_HEADS), (0, 0)))
    bf_col = jnp.pad(b_forget, (0, 16 - FOX_HEADS)).reshape(16, 1)
    c_t = fgate_fwd(z_t, bf_col, "fgate_fwd")
    c_row = c_t[:FOX_HEADS].reshape(FOX_PAIRS, 2, seq)
    main1, lse, main1_b = fox_fwd(qb, kvb, c_row, "fox_fwd", out_width=D_MODEL)
    w_mkv[1], w_o[1] = gather_wait(st_b[1], [0, 0], main1, "gather_b1_wait")
    (memn1,) = rms_fwd(mem0, [vec(ln_mem[1])], "mem1_norm")
    kvm1 = mm(memn1, w_mkv[1], "kvm1")
    mixed1 = mem_attn_fwd(qb, MAIN_WIDTH // MEM_WIDTH, kvm1, main1_b, "mem_attn1")
    y1_1, hmid1, f1 = mm_resnorm(mixed1, w_o[1], h1, vec(ln_mix_post[1]), [vec(ln_ffn_pre[1])], "mix_out1")
    w_g1, w_u1 = gather_wait(st_e[0], [1, 1], f1, "gather_gate_up1_wait")
    gu1, act1 = ffn_up(f1, w_g1, w_u1, "ffn_up1")
    (w_d1,) = gather_wait(st_e[1], [0], act1, "gather_down1_wait")
    dh, d_y2_1, dg_fpost1, loss_tile = mm_resnorm_loss(act1, w_d1, hmid1, vec(ln_ffn_post[1]), tgt, "ffn_down1_loss")
    ffn_w = [(w_g0, w_u0, w_d0), (w_g1, w_u1, w_d1)]
    ff_shard = w_ffn_down.shape[1]

    small = {}

    def ffn_backward(layer, dh_out, d_y2, hmid, f, gu, act, y1):
        w_g, w_u, w_d = ffn_w[layer]
        dw_down = mm_tn(act, d_y2, f"dw_down{layer}")
        rs_down = scatter_start([dw_down], [0], f"scatter_down{layer}_start", used=[ff_shard])
        d_g, d_u = ffn_act_grad(d_y2, w_d, gu, f"ffn_act_grad{layer}")
        dw_g = mm_tn(d_g, f, f"dw_gate{layer}", dep=rs_down["token"])
        dw_u = mm_tn(d_u, f, f"dw_up{layer}")
        rs_gate_up = scatter_start([dw_g, dw_u], [0, 0], f"scatter_gate_up{layer}_start", used=[ff_shard] * 2)
        dh_mid, d_y1, dg_fpre, dg_mpost = ffn_in_grad(d_g, d_u, w_g, w_u, hmid, dh_out, after(rs_gate_up["token"], vec(ln_ffn_pre[layer])),
                                                      y1, vec(ln_mix_post[layer]), f"ffn_in_grad{layer}")
        return dh_mid, d_y1, dg_fpre, dg_mpost, (rs_down, rs_gate_up)

    def mix_out_backward(layer, d_y1, mixed):
        dw_out = mm_tn(mixed, d_y1, f"dw_out{layer}")
        d_mixed = mm(d_y1, w_o[layer], f"d_mixed{layer}", trans_b=True)
        return d_mixed, dw_out

    def mem_backward(layer, q_src, q_block, kvm, memn, d_mixed, into):
        d_qm, d_kvm = mem_attn_bwd(q_src, q_block, kvm, d_mixed, into, f"mem_attn_bwd{layer}")
        d_kvm_b = d_kvm.astype(bf16)
        dw_mkv = mm_tn(memn, d_kvm_b, f"dw_mem_kv{layer}")
        d_memn = mm(d_kvm_b, w_mkv[layer], f"d_memn{layer}", trans_b=True)
        _, dg_mem = rms_bwd(mem0, vec(ln_mem[layer]), d_memn, None, bf16, f"mem_norm_bwd{layer}")
        return d_qm, dw_mkv, dg_mem


    dh_mid1, d_y1_1, dg_fpre1, dg_mpost1, rs_ffn1 = ffn_backward(1, dh, d_y2_1, hmid1, f1, gu1, act1, y1_1)
    d_mixed1, dw_out1 = mix_out_backward(1, d_y1_1, mixed1)
    dq_b, dk, dv, dc = fox_bwd(qb, kvb, d_mixed1, main1, lse, c_row, "fox_bwd", dq_width=D_MODEL)
    d_proj1, dw_mkv1, dg_mem1 = mem_backward(1, qb, MAIN_WIDTH // MEM_WIDTH, kvm1, memn1, d_mixed1, dq_b)
    rs_mix1 = scatter_start([dw_out1, dw_mkv1], [0, 0], "scatter_mix1_start")
    dc_t = jnp.pad(dc.reshape(FOX_HEADS, seq), ((0, 16 - FOX_HEADS), (0, 0)))
    dz_t, db_f = fgate_bwd(dc_t, z_t, bf_col, "fgate_bwd")
    d_kvf = jnp.concatenate([dk, dv, jnp.pad(dz_t[:FOX_HEADS].T.astype(bf16), ((0, 0), (0, KV_PAD - KV_WIDTH)))], axis=-1)
    dw_in_b = mm_tn(a1, d_proj1, "dw_in_b", dep=rs_mix1["token"])
    dw_kv = mm_tn(sin1, d_kvf, "dw_kv", tn=896)
    rs_2 = scatter_start([dw_in_b, dw_kv], [0, 0], "scatter_shared_start")
    dh1, (dg_pre1, dg_shared), d_y2_0, dg_fpost0 = proj_in_grad(
        [(d_proj1, w_inb, vec(ln_mix_pre[1])), (d_kvf, w_kv, vec(ln_shared))], h1, dh_mid1, "in_grad1", dep=rs_2["token"],
        below=(y2_0, vec(ln_ffn_post[0])))

    dh_mid0, d_y1_0, dg_fpre0, dg_mpost0, rs_ffn0 = ffn_backward(0, dh1, d_y2_0, hmid0, f0, gu0, act0, y1_0)
    d_mixed0, dw_out0 = mix_out_backward(0, d_y1_0, mixed0)
    d_uv, dw_s, db_s, dg_lnv, db_lnv = gmlp_bwd(proj0, d_mixed0, ws, ws_t, bs_t, lnv_g, lnv_b, "gmlp_bwd", out_width=w_in_a_full.shape[1])
    d_proj0, dw_mkv0, dg_mem0 = mem_backward(0, proj0, 2 * MAIN_WIDTH // MEM_WIDTH, kvm0, memn0, d_mixed0, d_uv)
    rs_mix0 = scatter_start([dw_out0, dw_mkv0], [0, 0], "scatter_mix0_start")

    small["ln_mix_pre"] = jnp.concatenate([jnp.zeros_like(dg_pre1), dg_pre1], axis=0)
    small["ln_mix_post"] = jnp.concatenate([dg_mpost0, dg_mpost1], axis=0)
    small["ln_ffn_pre"] = jnp.concatenate([dg_fpre0, dg_fpre1], axis=0)
    small["ln_ffn_post"] = jnp.concatenate([dg_fpost0, dg_fpost1], axis=0)
    small["ln_mem"] = jnp.concatenate([dg_mem0, dg_mem1], axis=0)
    small["w_spatial"] = dw_s[None]
    small["b_spatial"] = db_s[:, :A_GROUPS].T[None]
    small["ln_shared"] = dg_shared[0]
    small["b_forget"] = db_f[:FOX_HEADS, 0]
    small["ln_v_g"] = dg_lnv
    small["ln_v_b"] = db_lnv
    small_rows = jnp.concatenate([_pack_small(small, _SMALL), after(rs_mix0["token"], loss_tile)], axis=0)
    st_small = gather_start([small_rows[None]], [0], "gather_small_grads_start")
    dw_in_a_t = mm_tn(d_proj0, a0, "dw_in_a", tk=896, dep=st_small["token"])
    rs_in_a = scatter_start([dw_in_a_t], [0], "scatter_in_a_start")
    grad_x, (dg_pre0,) = proj_in_grad([(d_proj0, w_in_a_full, vec(ln_mix_pre[0]))], h0, dh_mid0, "in_grad0", dep=rs_in_a["token"])
    st_last = gather_start([dg_pre0.reshape(1, 8, LANES)], [0], "gather_last_grad_start")

    (p_down1,) = scatter_wait(rs_ffn1[0], [0], after(st_last["token"], grad_x[:8, :LANES]), "scatter_down1_wait", used=[ff_shard])
    p_gate1, p_up1 = scatter_wait(rs_ffn1[1], [0, 0], p_down1, "scatter_gate_up1_wait", used=[ff_shard] * 2)
    p_out1, p_mkv1 = scatter_wait(rs_mix1, [0, 0], p_gate1, "scatter_mix1_wait")
    p_in_b, p_kv = scatter_wait(rs_2, [0, 0], p_out1, "scatter_shared_wait")
    (p_down0,) = scatter_wait(rs_ffn0[0], [0], p_in_b, "scatter_down0_wait", used=[ff_shard])
    p_gate0, p_up0 = scatter_wait(rs_ffn0[1], [0, 0], p_down0, "scatter_gate_up0_wait", used=[ff_shard] * 2)
    p_out0, p_mkv0 = scatter_wait(rs_mix0, [0, 0], p_gate0, "scatter_mix0_wait")
    owned_parts = dict(w_ffn_gate=[p_gate0, p_gate1], w_ffn_up=[p_up0, p_up1], w_ffn_down=[p_down0, p_down1], w_out=[p_out0, p_out1],
                       w_mem_kv=[p_mkv0, p_mkv1], w_in_b=[p_in_b], w_shared_kv=[p_kv])

    grad_w, delta, new_m, new_v = {}, {}, {}, {}
    transposed = ("w_ffn_gate", "w_ffn_up", "w_in_a")

    def adamw_sharded(n, parts):
        shape = weights[n].shape
        three_d = shape if len(shape) == 3 else (1,) + shape
        view = (lambda t: t.reshape(three_d).transpose(0, 2, 1)) if n in transposed else (lambda t: t.reshape(three_d))
        back = (lambda t: t.transpose(0, 2, 1).reshape(shape)) if n in transposed else (lambda t: t.reshape(shape))
        w_view = view(weights[n])
        outs = adamw_owned(w_view, parts, view(mom_m[n]), view(mom_v[n]), f"adamw_{n}", tr=_row_tile(w_view.shape[1]))
        grad_w[n], delta[n], new_m[n], new_v[n] = (back(t) for t in outs)

    for n, parts in owned_parts.items():
        adamw_sharded(n, parts)
    (p_in_a,) = scatter_wait(rs_in_a, [0], delta["w_shared_kv"], "scatter_in_a_wait")
    adamw_sharded("w_in_a", [p_in_a])
    (small_all,) = gather_wait(st_small, [0], p_in_a, "gather_small_grads_wait")
    (last_all,) = gather_wait(st_last, [0], small_all, "gather_last_grad_wait")
    small_sum = sum_leading(small_all, "sum_small_grads")
    loss = small_sum[small_rows.shape[0] - 1, 0]
    g_small = _unpack_small(small_sum, _SMALL)
    g_small["ln_mix_pre"] = jnp.concatenate([sum_leading(last_all, "sum_last_grad").reshape(1, D_MODEL), g_small["ln_mix_pre"][1:]], axis=0)
    shard = MAIN_WIDTH // N_DEV
    for n in ("ln_v_g", "ln_v_b"):
        g_small[n] = lax.dynamic_slice_in_dim(g_small[n], me * shard, shard, axis=1)
    grad_w.update(g_small)
    small_local_shapes = [(n, tuple(weights[n].shape)) for n, _ in _SMALL]
    packed = [_pack_small(src, small_local_shapes) for src in (weights, grad_w, mom_m, mom_v)]
    outs = adamw(*packed, "adamw_small", tr=packed[0].shape[0])
    for dst, buf in zip((delta, new_m, new_v), outs):
        dst.update(_unpack_small(buf, small_local_shapes))

    return (loss, grad_x[None], *[grad_w[n] for n in names], *[delta[n] for n in names],
            *[new_m[n] for n in names], *[new_v[n] for n in names])
```

```python
import functools
import math

import jax
import jax.numpy as jnp
from jax import lax
from jax.experimental import pallas as pl
from jax.experimental.pallas import tpu as pltpu

f32 = jnp.float32
bf16 = jnp.bfloat16
SDS = jax.ShapeDtypeStruct

D_MODEL = 1024
MAIN_WIDTH = 768
MEM_WIDTH = 256
HEAD_DIM = 64
MEM_HEADS = 4
FOX_HEADS = 12
FOX_PAIRS = FOX_HEADS // 2
CHUNK = 128
A_GROUPS = 6
FF_SHARD_PAD = 384
KV_WIDTH = 2 * MAIN_WIDTH + FOX_HEADS
KV_PAD = 1792
RMS_EPS = 1e-6
LN_EPS = 1e-5
ATT_SCALE = HEAD_DIM ** -0.5
ADAM_LR, ADAM_B1, ADAM_B2, ADAM_EPS, ADAM_WD, ADAM_STEP = 0.001, 0.9, 0.999, 1e-08, 0.01, 10
N_DEV = 8
MESH = pl.DeviceIdType.MESH
V7X_VMEM_LIMIT = 56 * 1024 * 1024
LANES = 128


def _cparams(*sem):
    return pltpu.CompilerParams(dimension_semantics=sem or None, vmem_limit_bytes=V7X_VMEM_LIMIT)


def _dot(a, b):
    return jnp.dot(a, b, preferred_element_type=f32)


def _dot_nt(a, b):
    return lax.dot_general(a, b, (((1,), (1,)), ((), ())), preferred_element_type=f32)


def _dot_tn(a, b):
    return lax.dot_general(a, b, (((0,), (0,)), ((), ())), preferred_element_type=f32)


def _gelu(x):
    k = math.sqrt(2.0 / math.pi)
    t = jnp.tanh(k * (x + 0.044715 * x * x * x))
    return 0.5 * x * (1.0 + t), t


def _gelu_grad(x, t):
    k = math.sqrt(2.0 / math.pi)
    return 0.5 * (1.0 + t) + 0.5 * x * (1.0 - t * t) * k * (1.0 + 3.0 * 0.044715 * x * x)


def _sigmoid(x):
    return 1.0 / (1.0 + jnp.exp(-x))


def rms_fwd(x, gains, name, tm=512):
    m, d = x.shape
    tm = min(tm, m)
    n = len(gains)

    def body(x_ref, *refs):
        xv = x_ref[...]
        y = xv * lax.rsqrt(jnp.sum(xv * xv, axis=-1, keepdims=True) * (1.0 / d) + RMS_EPS)
        for g_ref, o_ref in zip(refs[:n], refs[n:]):
            o_ref[...] = (y * g_ref[...]).astype(bf16)

    row = pl.BlockSpec((tm, d), lambda i: (i, 0))
    vec = pl.BlockSpec((1, d), lambda i: (0, 0))
    return pl.pallas_call(body, grid=(m // tm,), in_specs=[row] + [vec] * n, out_specs=[row] * n,
                          out_shape=[SDS((m, d), bf16)] * n, name=name, compiler_params=_cparams("parallel"))(x, *gains)


def rms_bwd(x, g, dy, add, out_dtype, name, tm=512):
    m, d = x.shape
    tm = min(tm, m)
    has_add = add is not None

    def body(x_ref, g_ref, dy_ref, *refs):
        dx_ref, dg_ref = refs[-2], refs[-1]
        xv = x_ref[...]
        dyv = dy_ref[...].astype(f32)
        r = lax.rsqrt(jnp.sum(xv * xv, axis=-1, keepdims=True) * (1.0 / d) + RMS_EPS)
        xn = xv * r
        dyg = dyv * g_ref[...]
        dx = r * (dyg - xn * (jnp.sum(dyg * xn, axis=-1, keepdims=True) * (1.0 / d)))
        if has_add:
            dx = dx + refs[0][...]
        dx_ref[...] = dx.astype(out_dtype)

        @pl.when(pl.program_id(0) == 0)
        def _():
            dg_ref[...] = jnp.zeros_like(dg_ref)

        dg_ref[...] += jnp.sum(dyv * xn, axis=0, keepdims=True)

    row = pl.BlockSpec((tm, d), lambda i: (i, 0))
    vec = pl.BlockSpec((1, d), lambda i: (0, 0))
    ins = [x, g, dy] + ([add] if has_add else [])
    return pl.pallas_call(body, grid=(m // tm,), in_specs=[row, vec, row] + ([row] if has_add else []),
                          out_specs=[row, vec], out_shape=[SDS((m, d), out_dtype), SDS((1, d), f32)], name=name,
                          compiler_params=_cparams("arbitrary"))(*ins)


def mm(a, b, name, trans_b=False, out_dtype=f32, tm=1024, tn=1024, col0=0, ncols=None, dep=None):
    m, k = a.shape
    n_all = b.shape[0] if trans_b else b.shape[1]
    n = n_all if ncols is None else ncols
    tm, tn = min(tm, m), min(tn, n)
    assert m % tm == 0 and n % tn == 0 and col0 % tn == 0 and not (trans_b and col0), (name, m, n, tm, tn)
    jb = col0 // tn

    def body(a_ref, b_ref, *rest):
        r = _dot_nt(a_ref[...], b_ref[...]) if trans_b else _dot(a_ref[...], b_ref[...])
        rest[-1][...] = r.astype(out_dtype)

    if trans_b:
        b_spec = pl.BlockSpec((tn, k), lambda j, i: (j, 0))
    else:
        b_spec = pl.BlockSpec((k, tn), lambda j, i: (0, jb + j))
    deps = [] if dep is None else [dep]
    dep_specs = [pl.BlockSpec((8, LANES), lambda j, i: (0, 0))] * len(deps)
    return pl.pallas_call(body, grid=(n // tn, m // tm), in_specs=[pl.BlockSpec((tm, k), lambda j, i: (i, 0)), b_spec] + dep_specs,
                          out_specs=pl.BlockSpec((tm, tn), lambda j, i: (i, j)), out_shape=SDS((m, n), out_dtype),
                          name=name, compiler_params=_cparams("parallel", "parallel"))(a, b, *deps)


def mm_tn(a, g, name, tk=1024, tn=1024, out_dtype=bf16, dep=None):
    s, k = a.shape
    n = g.shape[1]
    tk, tn = min(tk, k), min(tn, n)
    assert k % tk == 0 and n % tn == 0, (name, k, n, tk, tn)

    def body(a_ref, g_ref, *rest):
        rest[-1][...] = _dot_tn(a_ref[...], g_ref[...]).astype(out_dtype)

    deps = [] if dep is None else [dep]
    dep_specs = [pl.BlockSpec((8, LANES), lambda i, j: (0, 0))] * len(deps)
    return pl.pallas_call(body, grid=(k // tk, n // tn),
                          in_specs=[pl.BlockSpec((s, tk), lambda i, j: (0, i)), pl.BlockSpec((s, tn), lambda i, j: (0, j))] + dep_specs,
                          out_specs=pl.BlockSpec((tk, tn), lambda i, j: (i, j)), out_shape=SDS((k, n), out_dtype), name=name,
                          compiler_params=_cparams("parallel", "parallel"))(a, g, *deps)


def _resident(shape, index_map):
    return pl.BlockSpec(shape, index_map, pipeline_mode=pl.Buffered(1))


def _rms(xv):
    return xv * lax.rsqrt(jnp.sum(xv * xv, axis=-1, keepdims=True) * (1.0 / xv.shape[-1]) + RMS_EPS)


def _rms_bwd_math(xv, g, dy):
    d = xv.shape[-1]
    r = lax.rsqrt(jnp.sum(xv * xv, axis=-1, keepdims=True) * (1.0 / d) + RMS_EPS)
    xn = xv * r
    dyg = dy * g
    dx = r * (dyg - xn * (jnp.sum(dyg * xn, axis=-1, keepdims=True) * (1.0 / d)))
    return dx, jnp.sum(dy * xn, axis=0, keepdims=True)


SUB_ROWS = 512


def mm_resnorm(a, b, h, g_post, gains, name, tm=512):
    m, k = a.shape
    d = b.shape[1]
    n = len(gains)

    def body(a_ref, b_ref, h_ref, gp_ref, *refs):
        for r in range(tm // SUB_ROWS):
            rows = slice(r * SUB_ROWS, (r + 1) * SUB_ROWS)
            y = _dot(a_ref[rows, :], b_ref[...])
            refs[n][rows, :] = y
            hn = h_ref[rows, :] + _rms(y) * gp_ref[...]
            refs[n + 1][rows, :] = hn
            if n:
                z = _rms(hn)
                for g_ref, o_ref in zip(refs[:n], refs[n + 2:]):
                    o_ref[rows, :] = (z * g_ref[...]).astype(bf16)

    row = pl.BlockSpec((tm, d), lambda i: (i, 0))
    vec = pl.BlockSpec((1, d), lambda i: (0, 0))
    return pl.pallas_call(body, grid=(m // tm,),
                          in_specs=[pl.BlockSpec((tm, k), lambda i: (i, 0)), _resident((k, d), lambda i: (0, 0)), row, vec] + [vec] * n,
                          out_specs=[row] * (n + 2), out_shape=[SDS((m, d), f32)] * 2 + [SDS((m, d), bf16)] * n, name=name,
                          compiler_params=_cparams("parallel"))(a, b, h, g_post, *gains)


def mm_resnorm_loss(a, b, h, g_post, tgt, name, tm=512):
    m, k = a.shape
    d = b.shape[1]

    def body(a_ref, b_ref, h_ref, gp_ref, t_ref, dh_ref, dy_ref, dg_ref, l_ref):
        @pl.when(pl.program_id(0) == 0)
        def _():
            dg_ref[...] = jnp.zeros_like(dg_ref)
            l_ref[...] = jnp.zeros_like(l_ref)

        y = _dot(a_ref[...], b_ref[...])
        e = h_ref[...] + _rms(y) * gp_ref[...] - t_ref[...]
        dh = e * (1.0 / d)
        dh_ref[...] = dh
        part = jnp.sum(jnp.sum(e * e, axis=-1, keepdims=True), axis=0, keepdims=True) * (0.5 / d)
        l_ref[...] += jnp.broadcast_to(part, l_ref.shape)
        dy, dg = _rms_bwd_math(y, gp_ref[...], dh)
        dy_ref[...] = dy.astype(bf16)
        dg_ref[...] += dg

    row = pl.BlockSpec((tm, d), lambda i: (i, 0))
    vec = pl.BlockSpec((1, d), lambda i: (0, 0))
    return pl.pallas_call(body, grid=(m // tm,),
                          in_specs=[pl.BlockSpec((tm, k), lambda i: (i, 0)), _resident((k, d), lambda i: (0, 0)), row, vec, row],
                          out_specs=[row, row, vec, pl.BlockSpec((8, LANES), lambda i: (0, 0))],
                          out_shape=[SDS((m, d), f32), SDS((m, d), bf16), SDS((1, d), f32), SDS((8, LANES), f32)], name=name,
                          compiler_params=_cparams("arbitrary"))(a, b, h, g_post, tgt)


FFN_TILE = 1536


def ffn_act_grad(d_y2, w_d, factors, name, tm=1024):
    s, d = d_y2.shape
    ff = w_d.shape[0]
    tn = FFN_TILE
    nb = ff // tn

    def body(a_ref, b_ref, g_ref, u_ref, dg_ref, du_ref):
        av = a_ref[...]
        tc = 256
        for c in range(tn // tc):
            cols = slice(c * tc, (c + 1) * tc)
            da = _dot_nt(av, b_ref[cols, :])
            dg_ref[:, cols] = (da * g_ref[:, cols].astype(f32)).astype(bf16)
            du_ref[:, cols] = (da * u_ref[:, cols].astype(f32)).astype(bf16)

    tile = pl.BlockSpec((tm, tn), lambda j, i: (i, j))
    return pl.pallas_call(body, grid=(nb, s // tm),
                          in_specs=[pl.BlockSpec((tm, d), lambda j, i: (i, 0)), pl.BlockSpec((tn, d), lambda j, i: (j, 0)),
                                    pl.BlockSpec((tm, tn), lambda j, i: (i, 2 * j)), pl.BlockSpec((tm, tn), lambda j, i: (i, 2 * j + 1))],
                          out_specs=[tile, tile], out_shape=[SDS((s, ff), bf16)] * 2, name=name,
                          compiler_params=_cparams("parallel", "parallel"))(d_y2, w_d, factors, factors)


def ffn_in_grad(d_g, d_u, w_g, w_u, hmid, dh_out, g_pre, y1, g_post, name, tm=512):
    s, ff = d_g.shape
    d = w_g.shape[0]

    def body(dg_ref, du_ref, wg_ref, wu_ref, hm_ref, dho_ref, gpre_ref, y1_ref, gpost_ref, dhm_ref, dy1_ref, dgpre_ref, dgpost_ref):
        @pl.when(pl.program_id(0) == 0)
        def _():
            dgpre_ref[...] = jnp.zeros_like(dgpre_ref)
            dgpost_ref[...] = jnp.zeros_like(dgpost_ref)

        for r in range(tm // SUB_ROWS):
            rows = slice(r * SUB_ROWS, (r + 1) * SUB_ROWS)
            d_f = _dot_nt(dg_ref[rows, :], wg_ref[...]) + _dot_nt(du_ref[rows, :], wu_ref[...])
            dx, dg1 = _rms_bwd_math(hm_ref[rows, :], gpre_ref[...], d_f)
            dh_mid = dho_ref[rows, :] + dx
            dhm_ref[rows, :] = dh_mid
            dgpre_ref[...] += dg1
            dy1, dg2 = _rms_bwd_math(y1_ref[rows, :], gpost_ref[...], dh_mid)
            dy1_ref[rows, :] = dy1.astype(bf16)
            dgpost_ref[...] += dg2

    row = pl.BlockSpec((tm, d), lambda i: (i, 0))
    vec = pl.BlockSpec((1, d), lambda i: (0, 0))
    wide = pl.BlockSpec((tm, ff), lambda i: (i, 0))
    w_spec = _resident((d, ff), lambda i: (0, 0))
    return pl.pallas_call(body, grid=(s // tm,), in_specs=[wide, wide, w_spec, w_spec, row, row, vec, row, vec],
                          out_specs=[row, row, vec, vec], out_shape=[SDS((s, d), f32), SDS((s, d), bf16), SDS((1, d), f32), SDS((1, d), f32)],
                          name=name, compiler_params=_cparams("arbitrary"))(d_g, d_u, w_g, w_u, hmid, dh_out, g_pre, y1, g_post)


def proj_in_grad(pairs, x, add, name, tm=512, dep=None, below=None):
    s, d = x.shape
    n = len(pairs)
    extra = [] if dep is None else [dep]
    n_below = 0 if below is None else 2

    def body(*refs):
        x_ref, add_ref = refs[3 * n], refs[3 * n + 1]
        below_refs = refs[3 * n + 2:3 * n + 2 + n_below]
        outs = refs[3 * n + 2 + n_below + len(extra):]

        @pl.when(pl.program_id(0) == 0)
        def _():
            for o in outs[1:1 + n] + outs[2 + n:]:
                o[...] = jnp.zeros_like(o)

        xv = x_ref[...]
        dx = add_ref[...]
        for i in range(n):
            a_ref, b_ref, g_ref = refs[3 * i:3 * i + 3]
            dxi, dgi = _rms_bwd_math(xv, g_ref[...], _dot_nt(a_ref[...], b_ref[...]))
            dx = dx + dxi
            outs[1 + i][...] += dgi
        outs[0][...] = dx
        if below is not None:
            dy, dg = _rms_bwd_math(below_refs[0][...], below_refs[1][...], dx)
            outs[1 + n][...] = dy.astype(bf16)
            outs[2 + n][...] += dg

    row = pl.BlockSpec((tm, d), lambda i: (i, 0))
    vec = pl.BlockSpec((1, d), lambda i: (0, 0))
    in_specs, args = [], []
    for a, b, g in pairs:
        k = a.shape[1]
        in_specs += [pl.BlockSpec((tm, k), lambda i: (i, 0)), _resident((d, k), lambda i: (0, 0)), vec]
        args += [a, b, g]
    in_specs += [row, row] + [row, vec][:n_below] + [pl.BlockSpec((8, LANES), lambda i: (0, 0))] * len(extra)
    out_specs = [row] + [vec] * n + [row, vec][:n_below]
    out_shape = [SDS((s, d), f32)] + [SDS((1, d), f32)] * n + [SDS((s, d), bf16), SDS((1, d), f32)][:n_below]
    out = pl.pallas_call(body, grid=(s // tm,), in_specs=in_specs, out_specs=out_specs, out_shape=out_shape, name=name,
                         compiler_params=_cparams("arbitrary"))(*args, x, add, *(below or ()), *extra)
    return (out[0], out[1:1 + n]) + tuple(out[1 + n:])


def ffn_up(f, wg, wu, name, tm=1024, tc=256):
    s, d = f.shape
    ff = wg.shape[-1]
    tn = FFN_TILE

    def body(f_ref, wg_ref, wu_ref, fac_ref, act_ref):
        fv = f_ref[...]
        for c in range(tn // tc):
            lo = c * tc
            gg = _dot(fv, wg_ref[:, lo:lo + tc])
            uu = _dot(fv, wu_ref[:, lo:lo + tc])
            sg = _sigmoid(gg)
            silu = gg * sg
            fac_ref[:, lo:lo + tc] = (uu * (sg + silu * (1.0 - sg))).astype(bf16)
            fac_ref[:, tn + lo:tn + lo + tc] = silu.astype(bf16)
            act_ref[:, lo:lo + tc] = (silu * uu).astype(bf16)

    w_spec = pl.BlockSpec((d, tn), lambda j, i: (0, j))
    return pl.pallas_call(body, grid=(ff // tn, s // tm), in_specs=[pl.BlockSpec((tm, d), lambda j, i: (i, 0)), w_spec, w_spec],
                          out_specs=[pl.BlockSpec((tm, 2 * tn), lambda j, i: (i, j)), pl.BlockSpec((tm, tn), lambda j, i: (i, j))],
                          out_shape=[SDS((s, 2 * ff), bf16), SDS((s, ff), bf16)], name=name,
                          compiler_params=_cparams("parallel", "parallel"))(f, wg, wu)


def _gmlp_forward_chunk(u, v, w_refs, bias, ln_g, ln_b):
    gu, tu = _gelu(u)
    gv, tv = _gelu(v)
    mu = jnp.sum(gv, axis=-1, keepdims=True) * (1.0 / MAIN_WIDTH)
    xc = gv - mu
    rstd = lax.rsqrt(jnp.sum(xc * xc, axis=-1, keepdims=True) * (1.0 / MAIN_WIDTH) + LN_EPS)
    xhat = xc * rstd
    vln = xhat * ln_g + ln_b
    row = lax.broadcasted_iota(jnp.int32, (CHUNK, CHUNK), 0)
    col = lax.broadcasted_iota(jnp.int32, (CHUNK, CHUNK), 1)
    s_parts = []
    for g in range(A_GROUPS):
        w = jnp.where(col <= row, w_refs[g], jnp.zeros((), bf16))
        s_parts.append(_dot(w, vln[:, g * CHUNK:(g + 1) * CHUNK].astype(bf16)) + bias[:, g:g + 1])
    return gu, tu, tv, rstd, xhat, vln, s_parts


def gmlp_fwd(proj, ws, bs_t, ln_g, ln_b, name, tm=512, out_width=MAIN_WIDTH):
    s = proj.shape[0]

    def body(u_ref, v_ref, w_ref, b_ref, g_ref, bb_ref, o_ref):
        bias = b_ref[...]
        for c in range(tm // CHUNK):
            rows = slice(c * CHUNK, (c + 1) * CHUNK)
            gu, _, _, _, _, _, s_parts = _gmlp_forward_chunk(u_ref[rows, :], v_ref[rows, :], w_ref, bias, g_ref[...], bb_ref[...])
            for g in range(A_GROUPS):
                cols = slice(g * CHUNK, (g + 1) * CHUNK)
                o_ref[rows, cols] = (gu[:, cols] * s_parts[g]).astype(bf16)

    vec = pl.BlockSpec((1, MAIN_WIDTH), lambda i: (0, 0))
    return pl.pallas_call(
        body, grid=(s // tm,),
        in_specs=[pl.BlockSpec((tm, MAIN_WIDTH), lambda i: (i, 0)), pl.BlockSpec((tm, MAIN_WIDTH), lambda i: (i, 1)),
                  pl.BlockSpec((A_GROUPS, CHUNK, CHUNK), lambda i: (0, 0, 0)), pl.BlockSpec((CHUNK, A_GROUPS), lambda i: (0, 0)), vec, vec],
        out_specs=pl.BlockSpec((tm, MAIN_WIDTH), lambda i: (i, 0)), out_shape=SDS((s, out_width), bf16), name=name,
        compiler_params=_cparams("parallel"))(proj, proj, ws, bs_t, ln_g, ln_b)


def gmlp_bwd(proj, d_mixed, ws, ws_t, bs_t, ln_g, ln_b, name, tm=512, out_width=2 * MAIN_WIDTH):
    s = proj.shape[0]

    def body(u_ref, v_ref, dm_ref, w_ref, wt_ref, b_ref, g_ref, bb_ref, duv_ref, dw_ref, db_ref, dg_ref, dbb_ref):
        @pl.when(pl.program_id(0) == 0)
        def _():
            dw_ref[...] = jnp.zeros_like(dw_ref)
            db_ref[...] = jnp.zeros_like(db_ref)
            dg_ref[...] = jnp.zeros_like(dg_ref)
            dbb_ref[...] = jnp.zeros_like(dbb_ref)

        bias = b_ref[...]
        ln_gv = g_ref[...]
        row = lax.broadcasted_iota(jnp.int32, (CHUNK, CHUNK), 0)
        col = lax.broadcasted_iota(jnp.int32, (CHUNK, CHUNK), 1)
        lane = lax.broadcasted_iota(jnp.int32, (CHUNK, LANES), 1)
        for c in range(tm // CHUNK):
            rows = slice(c * CHUNK, (c + 1) * CHUNK)
            u = u_ref[rows, :]
            v = v_ref[rows, :]
            gu, tu, tv, rstd, xhat, vln, s_parts = _gmlp_forward_chunk(u, v, w_ref, bias, ln_gv, bb_ref[...])
            dm = dm_ref[rows, :]
            d_vln_parts = []
            d_gu_parts = []
            db_acc = jnp.zeros((CHUNK, LANES), f32)
            for g in range(A_GROUPS):
                cols = slice(g * CHUNK, (g + 1) * CHUNK)
                dmg = dm[:, cols]
                d_gu_parts.append(dmg * s_parts[g])
                d_s = dmg * gu[:, cols]
                db_acc = db_acc + jnp.where(lane == g, jnp.sum(d_s, axis=-1, keepdims=True), 0.0)
                d_sb = d_s.astype(bf16)
                dw_ref[g] += jnp.where(col <= row, _dot_nt(d_sb, vln[:, cols].astype(bf16)), 0.0)
                wt = jnp.where(row <= col, wt_ref[g], jnp.zeros((), bf16))
                d_vln_parts.append(_dot(wt, d_sb))
            db_ref[...] += db_acc
            d_vln = jnp.concatenate(d_vln_parts, axis=-1)
            d_gu = jnp.concatenate(d_gu_parts, axis=-1)
            dg_ref[...] += jnp.sum(d_vln * xhat, axis=0, keepdims=True)
            dbb_ref[...] += jnp.sum(d_vln, axis=0, keepdims=True)
            dxh = d_vln * ln_gv
            m1 = jnp.sum(dxh, axis=-1, keepdims=True) * (1.0 / MAIN_WIDTH)
            m2 = jnp.sum(dxh * xhat, axis=-1, keepdims=True) * (1.0 / MAIN_WIDTH)
            d_gv = rstd * (dxh - m1 - xhat * m2)
            duv_ref[rows, :MAIN_WIDTH] = (d_gu * _gelu_grad(u, tu)).astype(bf16)
            duv_ref[rows, MAIN_WIDTH:] = (d_gv * _gelu_grad(v, tv)).astype(bf16)

    vec = pl.BlockSpec((1, MAIN_WIDTH), lambda i: (0, 0))
    wspec = pl.BlockSpec((A_GROUPS, CHUNK, CHUNK), lambda i: (0, 0, 0))
    return pl.pallas_call(
        body, grid=(s // tm,),
        in_specs=[pl.BlockSpec((tm, MAIN_WIDTH), lambda i: (i, 0)), pl.BlockSpec((tm, MAIN_WIDTH), lambda i: (i, 1)),
                  pl.BlockSpec((tm, MAIN_WIDTH), lambda i: (i, 0)), wspec, wspec, pl.BlockSpec((CHUNK, A_GROUPS), lambda i: (0, 0)), vec, vec],
        out_specs=[pl.BlockSpec((tm, 2 * MAIN_WIDTH), lambda i: (i, 0)), wspec, pl.BlockSpec((CHUNK, LANES), lambda i: (0, 0)), vec, vec],
        out_shape=[SDS((s, out_width), bf16), SDS((A_GROUPS, CHUNK, CHUNK), f32), SDS((CHUNK, LANES), f32),
                   SDS((1, MAIN_WIDTH), f32), SDS((1, MAIN_WIDTH), f32)],
        name=name, compiler_params=_cparams("arbitrary"))(proj, proj, d_mixed, ws, ws_t, bs_t, ln_g, ln_b)


def _head_mask(width, h):
    lane = lax.broadcasted_iota(jnp.int32, (1, width), 1)
    return (lane >= h * HEAD_DIM) & (lane < (h + 1) * HEAD_DIM)


def mem_attn_fwd(proj, q_block, kv, into, name, tm=512):
    s = proj.shape[0]
    n_mem = kv.shape[0]
    out_block = into.shape[1] // MEM_WIDTH - 1

    def body(q_ref, kv_ref, into_ref, o_ref):
        q = q_ref[...].astype(f32)
        k = kv_ref[:, :MEM_WIDTH].astype(bf16)
        v = kv_ref[:, MEM_WIDTH:].astype(bf16)
        out = jnp.zeros((tm, MEM_WIDTH), f32)
        for h in range(MEM_HEADS):
            msk = _head_mask(MEM_WIDTH, h)
            qh = jnp.where(msk, q, 0.0).astype(bf16)
            sc = _dot_nt(qh, k) * ATT_SCALE
            e = jnp.exp(sc - jnp.max(sc, axis=-1, keepdims=True))
            p = e / jnp.sum(e, axis=-1, keepdims=True)
            out = jnp.where(msk, _dot(p.astype(bf16), v), out)
        o_ref[...] = out.astype(bf16)

    return pl.pallas_call(body, grid=(s // tm,),
                          in_specs=[pl.BlockSpec((tm, MEM_WIDTH), lambda i: (i, q_block)), pl.BlockSpec((n_mem, 2 * MEM_WIDTH), lambda i: (0, 0)), _ANY],
                          out_specs=pl.BlockSpec((tm, MEM_WIDTH), lambda i: (i, out_block)), out_shape=SDS(into.shape, bf16), name=name,
                          input_output_aliases={2: 0}, compiler_params=_cparams("parallel"))(proj, kv, into)


def mem_attn_bwd(proj, q_block, kv, d_mixed, into, name, tm=512):
    s = proj.shape[0]
    n_mem = kv.shape[0]
    out_block = into.shape[1] // MEM_WIDTH - 1

    def body(q_ref, kv_ref, do_ref, into_ref, dq_ref, dkv_ref):
        @pl.when(pl.program_id(0) == 0)
        def _():
            dkv_ref[...] = jnp.zeros_like(dkv_ref)

        q = q_ref[...].astype(f32)
        do = do_ref[...]
        k = kv_ref[:, :MEM_WIDTH].astype(bf16)
        v = kv_ref[:, MEM_WIDTH:].astype(bf16)
        dq = jnp.zeros((tm, MEM_WIDTH), f32)
        dk = jnp.zeros((n_mem, MEM_WIDTH), f32)
        dv = jnp.zeros((n_mem, MEM_WIDTH), f32)
        for h in range(MEM_HEADS):
            msk = _head_mask(MEM_WIDTH, h)
            qh = jnp.where(msk, q, 0.0).astype(bf16)
            doh = jnp.where(msk, do, 0.0).astype(bf16)
            sc = _dot_nt(qh, k) * ATT_SCALE
            e = jnp.exp(sc - jnp.max(sc, axis=-1, keepdims=True))
            p = e / jnp.sum(e, axis=-1, keepdims=True)
            dp = _dot_nt(doh, v)
            ds = p * (dp - jnp.sum(dp * p, axis=-1, keepdims=True))
            dsb = (ds * ATT_SCALE).astype(bf16)
            dq = jnp.where(msk, _dot(dsb, k), dq)
            dk = dk + _dot_tn(dsb, qh)
            dv = dv + _dot_tn(p.astype(bf16), doh)
        dq_ref[...] = dq.astype(bf16)
        dkv_ref[:, :MEM_WIDTH] += dk
        dkv_ref[:, MEM_WIDTH:] += dv

    return pl.pallas_call(
        body, grid=(s // tm,),
        in_specs=[pl.BlockSpec((tm, MEM_WIDTH), lambda i: (i, q_block)), pl.BlockSpec((n_mem, 2 * MEM_WIDTH), lambda i: (0, 0)),
                  pl.BlockSpec((tm, MEM_WIDTH), lambda i: (i, MAIN_WIDTH // MEM_WIDTH)), _ANY],
        out_specs=[pl.BlockSpec((tm, MEM_WIDTH), lambda i: (i, out_block)), pl.BlockSpec((n_mem, 2 * MEM_WIDTH), lambda i: (0, 0))],
        out_shape=[SDS(into.shape, bf16), SDS((n_mem, 2 * MEM_WIDTH), f32)], name=name,
        input_output_aliases={3: 0}, compiler_params=_cparams("arbitrary"))(proj, kv, d_mixed, into)


def _tri(t, upper):
    r = lax.broadcasted_iota(jnp.int32, (t, t), 0)
    c = lax.broadcasted_iota(jnp.int32, (t, t), 1)
    return ((r <= c) if upper else (r >= c)).astype(f32)


def fgate_fwd(z_t, b, name, t=512):
    hh, s = z_t.shape

    def body(z_ref, b_ref, c_ref):
        u = _tri(t, True)
        carry = jnp.zeros((hh, 1), f32)
        for blk in range(s // t):
            x = z_ref[:, blk * t:(blk + 1) * t] + b_ref[...]
            logf = jnp.minimum(x, 0.0) - jnp.log(1.0 + jnp.exp(-jnp.abs(x)))
            y = jnp.dot(logf, u, precision=lax.Precision.HIGHEST, preferred_element_type=f32) + carry
            c_ref[:, blk * t:(blk + 1) * t] = y
            carry = y[:, t - 1:t]

    return pl.pallas_call(body, out_shape=SDS((hh, s), f32), name=name, compiler_params=_cparams())(z_t, b)


def fgate_bwd(dc_t, z_t, b, name, t=512):
    hh, s = z_t.shape

    def body(dc_ref, z_ref, b_ref, dz_ref, db_ref):
        low = _tri(t, False)
        carry = jnp.zeros((hh, 1), f32)
        total = jnp.zeros((hh, 1), f32)
        for blk in reversed(range(s // t)):
            cols = slice(blk * t, (blk + 1) * t)
            y = jnp.dot(dc_ref[:, cols], low, precision=lax.Precision.HIGHEST, preferred_element_type=f32) + carry
            carry = y[:, 0:1]
            dz = y * _sigmoid(-(z_ref[:, cols] + b_ref[...]))
            dz_ref[:, cols] = dz
            total = total + jnp.sum(dz, axis=-1, keepdims=True)
        db_ref[...] = jnp.broadcast_to(total, db_ref.shape)

    return pl.pallas_call(body, out_shape=[SDS((hh, s), f32), SDS((hh, LANES), f32)], name=name,
                          compiler_params=_cparams())(dc_t, z_t, b)


def _pair_masks():
    lane = lax.broadcasted_iota(jnp.int32, (1, LANES), 1)
    return [lane < HEAD_DIM, lane >= HEAD_DIM]


def _tile_base(cr_ref, hh, lo):
    return cr_ref[hh:hh + 1, pl.ds(lo, LANES)][:, 0:1]


def fox_fwd(q, kv, c_row, name, tq=512, out_width=MAIN_WIDTH):
    s = kv.shape[0]
    nq = s // tq

    def body(q_ref, k_ref, v_ref, cr_ref, o_ref, lse_ref, ob_ref):
        i = pl.program_id(1)
        qv = q_ref[...]
        masks = _pair_masks()
        row = lax.broadcasted_iota(jnp.int32, (tq, tq), 0)
        col = lax.broadcasted_iota(jnp.int32, (tq, tq), 1)
        qh = [jnp.where(masks[hh], qv, jnp.zeros((), bf16)) * ATT_SCALE for hh in range(2)]
        ct = [_tile_base(cr_ref, hh, pl.multiple_of(i * tq, tq)) for hh in range(2)]

        def block(j, carry, diag):
            lo = pl.multiple_of(j * tq, tq)
            ks = k_ref[pl.ds(lo, tq), :]
            vs = v_ref[pl.ds(lo, tq), :]
            out = []
            for hh in range(2):
                m, l, acc = carry[hh]
                sc = _dot_nt(qh[hh], ks) + (ct[hh] - cr_ref[hh:hh + 1, pl.ds(lo, tq)])
                if diag:
                    sc = jnp.where(col <= row, sc, -jnp.inf)
                m_new = jnp.maximum(m, jnp.max(sc, axis=-1, keepdims=True))
                alpha = jnp.exp(m - m_new)
                p = jnp.exp(sc - m_new)
                l = alpha * l + jnp.sum(p, axis=-1, keepdims=True)
                p_hi = p.astype(bf16)
                p_lo = (p - p_hi.astype(f32)).astype(bf16)
                acc = alpha * acc + (_dot(p_hi, vs) + _dot(p_lo, vs))
                out.append((m_new, l, acc))
            return tuple(out)

        init = (jnp.full((tq, 1), -jnp.inf, f32), jnp.zeros((tq, 1), f32), jnp.zeros((tq, LANES), f32))
        carry = lax.fori_loop(0, i, functools.partial(block, diag=False), (init, init))
        res = [(acc / l, m + jnp.log(l)) for m, l, acc in block(i, carry, True)]
        out = jnp.where(masks[0], res[0][0], res[1][0])
        o_ref[...] = out
        ob_ref[...] = out.astype(bf16)
        lse_ref[...] = jnp.where(masks[0], res[0][1], res[1][1])

    return pl.pallas_call(
        body, grid=(FOX_PAIRS, nq),
        in_specs=[pl.BlockSpec((tq, LANES), lambda p, i: (i, p)), pl.BlockSpec((s, LANES), lambda p, i: (0, p)),
                  pl.BlockSpec((s, LANES), lambda p, i: (0, FOX_PAIRS + p)), pl.BlockSpec((None, 2, s), lambda p, i: (p, 0, 0))],
        out_specs=[pl.BlockSpec((tq, LANES), lambda p, i: (i, p)), pl.BlockSpec((None, tq, LANES), lambda p, i: (p, i, 0)),
                   pl.BlockSpec((tq, LANES), lambda p, i: (i, p))],
        out_shape=[SDS((s, MAIN_WIDTH), f32), SDS((FOX_PAIRS, s, LANES), f32), SDS((s, out_width), bf16)], name=name,
        compiler_params=_cparams("parallel", "parallel"))(q, kv, kv, c_row)


def fox_bwd(q, kv, d_mixed, o, lse, c_row, name, tq=512, dq_width=MAIN_WIDTH):
    s = kv.shape[0]
    nq = s // tq

    def body(q_ref, k_ref, v_ref, do_ref, o_ref, lse_ref, cr_ref, dqb_ref, dk_ref, dv_ref, dc_ref, dq_ref):
        j = pl.program_id(1)

        @pl.when(j == 0)
        def _():
            dq_ref[...] = jnp.zeros_like(dq_ref)

        masks = _pair_masks()
        sub = lax.broadcasted_iota(jnp.int32, (LANES, 1), 0)
        sub_masks = [sub < HEAD_DIM, sub >= HEAD_DIM]
        row = lax.broadcasted_iota(jnp.int32, (tq, tq), 0)
        col = lax.broadcasted_iota(jnp.int32, (tq, tq), 1)
        kj = k_ref[...]
        vj = v_ref[...]
        lo_j = pl.multiple_of(j * tq, tq)

        def block(i, carry, diag):
            dk_t, dv_t, dc0, dc1 = carry
            dcs = [dc0, dc1]
            lo = pl.multiple_of(i * tq, tq)
            qi = q_ref[pl.ds(lo, tq), :]
            qi = qi * ATT_SCALE
            qt_i = qi.T
            doi = do_ref[pl.ds(lo, tq), :]
            dot_i = doi.astype(bf16).T
            prod = doi.astype(bf16).astype(f32) * o_ref[pl.ds(lo, tq), :]
            lse_i = lse_ref[pl.ds(lo, tq), :]
            dq_i = jnp.zeros((tq, LANES), f32)
            for hh in range(2):
                qh = jnp.where(masks[hh], qi, jnp.zeros((), bf16))
                doh = jnp.where(masks[hh], doi, 0.0).astype(bf16)
                delta = jnp.sum(jnp.where(masks[hh], prod, 0.0), axis=-1, keepdims=True)
                sc = _dot_nt(qh, kj) + (_tile_base(cr_ref, hh, lo) - cr_ref[hh:hh + 1, pl.ds(lo_j, tq)])
                p = jnp.exp(sc - lse_i[:, hh * HEAD_DIM:hh * HEAD_DIM + 1])
                if diag:
                    p = jnp.where(col <= row, p, 0.0)
                dv_t = dv_t + _dot(jnp.where(sub_masks[hh], dot_i, jnp.zeros((), bf16)), p.astype(bf16))
                ds = p * (_dot_nt(doh, vj) - delta)
                dcs[hh] = dcs[hh] + jnp.sum(ds, axis=0, keepdims=True)
                dsb = ds.astype(bf16)
                dq_i = jnp.where(masks[hh], _dot(dsb, kj), dq_i)
                dk_t = dk_t + _dot(jnp.where(sub_masks[hh], qt_i, jnp.zeros((), bf16)), dsb)
            dq_ref[pl.ds(lo, tq), :] += dq_i * ATT_SCALE
            return dk_t, dv_t, dcs[0], dcs[1]

        zero = jnp.zeros((LANES, tq), f32)
        zrow = jnp.zeros((1, tq), f32)
        carry = block(j, (zero, zero, zrow, zrow), True)
        dk_t, dv_t, dc0, dc1 = lax.fori_loop(j + 1, nq, functools.partial(block, diag=False), carry)
        dk_ref[...] = dk_t.T.astype(bf16)
        dv_ref[...] = dv_t.T.astype(bf16)
        dc_ref[0:1, :] = -dc0
        dc_ref[1:2, :] = -dc1

        @pl.when(j == nq - 1)
        def _():
            dqb_ref[...] = dq_ref[...].astype(bf16)

    full = lambda p, j: (0, p)
    tile = lambda p, j: (j, p)
    return pl.pallas_call(
        body, grid=(FOX_PAIRS, nq),
        in_specs=[pl.BlockSpec((s, LANES), full), pl.BlockSpec((tq, LANES), tile), pl.BlockSpec((tq, LANES), lambda p, j: (j, FOX_PAIRS + p)),
                  pl.BlockSpec((s, LANES), full), pl.BlockSpec((s, LANES), full), pl.BlockSpec((None, s, LANES), lambda p, j: (p, 0, 0)),
                  pl.BlockSpec((None, 2, s), lambda p, j: (p, 0, 0))],
        out_specs=[pl.BlockSpec((s, LANES), full), pl.BlockSpec((tq, LANES), tile), pl.BlockSpec((tq, LANES), tile),
                   pl.BlockSpec((None, 2, tq), lambda p, j: (p, 0, j))],
        out_shape=[SDS((s, dq_width), bf16), SDS((s, MAIN_WIDTH), bf16), SDS((s, MAIN_WIDTH), bf16), SDS((FOX_PAIRS, 2, s), f32)],
        scratch_shapes=[pltpu.VMEM((s, LANES), f32)],
        name=name, compiler_params=_cparams("parallel", "arbitrary"))(q, kv, kv, d_mixed, o, lse, c_row)


def adamw(w, g, m, v, name, tr=256):
    r, c = w.shape
    tr = min(tr, r)
    assert r % tr == 0, (name, r, tr)
    c1 = 1.0 / (1.0 - ADAM_B1 ** ADAM_STEP)
    c2 = 1.0 / (1.0 - ADAM_B2 ** ADAM_STEP)

    def body(w_ref, g_ref, m_ref, v_ref, d_ref, mo_ref, vo_ref):
        gv = g_ref[...]
        mn = ADAM_B1 * m_ref[...] + (1.0 - ADAM_B1) * gv
        vn = ADAM_B2 * v_ref[...] + (1.0 - ADAM_B2) * gv * gv
        mo_ref[...] = mn
        vo_ref[...] = vn
        d_ref[...] = -ADAM_LR * ((mn * c1) / (jnp.sqrt(vn * c2) + ADAM_EPS) + ADAM_WD * w_ref[...])

    spec = pl.BlockSpec((tr, c), lambda i: (i, 0))
    return pl.pallas_call(body, grid=(r // tr,), in_specs=[spec] * 4, out_specs=[spec] * 3, out_shape=[SDS((r, c), f32)] * 3,
                          name=name, compiler_params=_cparams("parallel"))(w, g, m, v)


def adamw_owned(w, parts, m, v, name, tr):
    nl, r, c = w.shape
    cp = parts[0].shape[2]
    assert r % tr == 0 and len(parts) == nl, (name, r, tr)
    c1 = 1.0 / (1.0 - ADAM_B1 ** ADAM_STEP)
    c2 = 1.0 / (1.0 - ADAM_B2 ** ADAM_STEP)

    def body(*refs):
        w_ref, p_refs, (m_ref, v_ref) = refs[0], refs[1:1 + nl], refs[1 + nl:3 + nl]
        g_ref, d_ref, mo_ref, vo_ref = refs[3 + nl:]
        layer = pl.program_id(0)

        def total(p_ref):
            acc = p_ref[0].astype(f32)
            for k in range(1, N_DEV):
                acc = acc + p_ref[k].astype(f32)
            return acc

        gv = total(p_refs[0])
        for l in range(1, nl):
            gv = jnp.where(layer == l, total(p_refs[l]), gv)
        gv = gv[:, :c]
        g_ref[...] = gv
        mn = ADAM_B1 * m_ref[...] + (1.0 - ADAM_B1) * gv
        vn = ADAM_B2 * v_ref[...] + (1.0 - ADAM_B2) * gv * gv
        mo_ref[...] = mn
        vo_ref[...] = vn
        d_ref[...] = -ADAM_LR * ((mn * c1) / (jnp.sqrt(vn * c2) + ADAM_EPS) + ADAM_WD * w_ref[...])

    spec = pl.BlockSpec((None, tr, c), lambda l, i: (l, i, 0))
    last = r // tr - 1

    def part_spec(mine):
        return pl.BlockSpec((N_DEV, tr, cp), lambda l, i: (0, jnp.where(l == mine, i, jnp.where(l < mine, 0, last)), 0))

    return pl.pallas_call(body, grid=(nl, r // tr), in_specs=[spec] + [part_spec(l) for l in range(nl)] + [spec, spec], out_specs=[spec] * 4,
                          out_shape=[SDS((nl, r, c), f32)] * 4, name=name,
                          compiler_params=_cparams("parallel", "parallel"))(w, *parts, m, v)


def sum_leading(x, name, out_dtype=f32, tr=None):
    n, r, c = x.shape
    tr = tr or r
    assert r % tr == 0

    def body(x_ref, o_ref):
        acc = x_ref[0].astype(f32)
        for k in range(1, n):
            acc = acc + x_ref[k].astype(f32)
        o_ref[...] = acc.astype(out_dtype)

    return pl.pallas_call(body, grid=(r // tr,), in_specs=[pl.BlockSpec((n, tr, c), lambda i: (0, i, 0))],
                          out_specs=pl.BlockSpec((tr, c), lambda i: (i, 0)), out_shape=SDS((r, c), out_dtype), name=name,
                          compiler_params=_cparams("parallel"))(x)


_ANY = pl.BlockSpec(memory_space=pl.ANY)
_DMA = pltpu.SemaphoreType.DMA


_HBM = pl.BlockSpec(memory_space=pltpu.HBM)
_SEM = pl.BlockSpec(memory_space=pltpu.SEMAPHORE)
_EFFECT = pltpu.SideEffectType.DATAFLOW_SIDE_EFFECTING
_FLIPS = [(0, 0, 1), (1, 0, 0), (0, 1, 0), (1, 1, 0), (1, 0, 1), (0, 1, 1), (1, 1, 1)]


def _me():
    return lax.axis_index("x"), lax.axis_index("y"), lax.axis_index("c")


def _peers():
    mx, my, mc = _me()
    return [(jnp.bitwise_xor(mx, fx), jnp.bitwise_xor(my, fy), jnp.bitwise_xor(mc, fc)) for fx, fy, fc in _FLIPS]


def _index(dev):
    return 4 * dev[0] + 2 * dev[1] + dev[2]


def _win(ref, axis, k, size, count=1):
    idx = [slice(None)] * len(ref.shape)
    idx[axis] = pl.ds(k * size, count * size)
    return ref.at[tuple(idx)]


def _hbm(a):
    return pltpu.with_memory_space_constraint(a, pltpu.HBM)


def _exchange_start(srcs, lands, copies_of, name):
    n = len(srcs)

    def body(*refs):
        src = refs[:n]
        send_sems, recv_sems, self_sems = refs[2 * n:2 * n + 3]
        land = refs[3 * n + 3:4 * n + 3]
        token = refs[4 * n + 3]
        me = _index(_me())
        for a in range(n):
            for s_ref, d_ref, peer in copies_of(a, src[a], land[a], me):
                if peer is None:
                    pltpu.make_async_copy(s_ref, d_ref, self_sems.at[a]).start()
                else:
                    pltpu.make_async_remote_copy(src_ref=s_ref, dst_ref=d_ref, send_sem=send_sems.at[a], recv_sem=recv_sems.at[a],
                                                 device_id=peer, device_id_type=MESH).start()
        token[...] = jnp.zeros_like(token)

    outs = pl.pallas_call(
        body, name=name,
        out_shape=(_DMA((n,)), _DMA((n,)), _DMA((n,)), *[pltpu.HBM(s.shape, s.dtype) for s in srcs],
                   *[pltpu.HBM(l.shape, l.dtype) for l in lands], SDS((8, LANES), f32)),
        in_specs=[_HBM] * (2 * n), out_specs=(_SEM, _SEM, _SEM, *[_HBM] * (2 * n), pl.BlockSpec(memory_space=pltpu.VMEM)),
        input_output_aliases={i: 3 + i for i in range(2 * n)},
        compiler_params=pltpu.CompilerParams(has_side_effects=_EFFECT),
    )(*[_hbm(s) for s in srcs], *[_hbm(lax.empty(l.shape, l.dtype)) for l in lands])
    return dict(sems=outs[:3], srcs=list(outs[3:3 + n]), lands=list(outs[3 + n:3 + 2 * n]), token=outs[3 + 2 * n])


def _exchange_wait(started, waits_of, after, name, which=None):
    which = list(range(len(started["srcs"]))) if which is None else which
    srcs, lands = [started["srcs"][a] for a in which], [started["lands"][a] for a in which]
    n = len(which)

    def body(*refs):
        src = refs[:n]
        land = refs[n:2 * n]
        send_sems, recv_sems, self_sems = refs[2 * n:2 * n + 3]
        me = _index(_me())
        for pos, a in enumerate(which):
            seven, (s_ref, d_ref) = waits_of(a, src[pos], land[pos], me)
            both = pltpu.make_async_remote_copy(src_ref=seven, dst_ref=seven, send_sem=send_sems.at[a], recv_sem=recv_sems.at[a],
                                                device_id=_me(), device_id_type=MESH)
            both.wait_send()
            both.wait_recv()
            pltpu.make_async_copy(s_ref, d_ref, self_sems.at[a]).wait()

    outs = pl.pallas_call(
        body, name=name, out_shape=tuple(pltpu.HBM(t.shape, t.dtype) for t in srcs + lands),
        in_specs=[_HBM] * (2 * n) + [_SEM] * 3 + [_ANY], out_specs=tuple([_HBM] * (2 * n)),
        input_output_aliases={i: i for i in range(2 * n)},
        compiler_params=pltpu.CompilerParams(has_side_effects=_EFFECT),
    )(*srcs, *lands, *started["sems"], after)
    return list(outs[n:])


def gather_start(locs, axes, name):
    lands = [SDS(tuple(N_DEV * d if i == ax else d for i, d in enumerate(l.shape)), l.dtype) for l, ax in zip(locs, axes)]

    def copies_of(a, src, land, me):
        mine = _win(land, axes[a], me, src.shape[axes[a]])
        return [(src, mine, peer) for peer in _peers()] + [(src, mine, None)]

    return _exchange_start(locs, lands, copies_of, name)


def gather_wait(started, axes, after, name, which=None):
    def waits_of(a, src, land, me):
        size = src.shape[axes[a]]
        return _win(land, axes[a], 0, size, N_DEV - 1), (src, _win(land, axes[a], me, size))

    return _exchange_wait(started, waits_of, after, name, which)


def _part(ref, axis, k, stride, used):
    idx = [slice(None)] * len(ref.shape)
    idx[axis] = pl.ds(k * stride, used)
    return ref.at[tuple(idx)]


def scatter_start(grads, axes, name, used=None):
    strides = [g.shape[ax] // N_DEV for g, ax in zip(grads, axes)]
    used = used or strides
    lands = [SDS((N_DEV,) + tuple(u if i == ax else d for i, d in enumerate(g.shape)), g.dtype) for g, ax, u in zip(grads, axes, used)]

    def copies_of(a, src, land, me):
        out = [(_part(src, axes[a], _index(peer), strides[a], used[a]), land.at[me], peer) for peer in _peers()]
        return out + [(_part(src, axes[a], me, strides[a], used[a]), land.at[me], None)]

    return _exchange_start(grads, lands, copies_of, name)


def scatter_wait(started, axes, after, name, used=None):
    def waits_of(a, src, land, me):
        stride = src.shape[axes[a]] // N_DEV
        return land.at[pl.ds(0, N_DEV - 1)], (_part(src, axes[a], me, stride, used[a] if used else stride), land.at[me])

    return _exchange_wait(started, waits_of, after, name)


def _row_tile(rows, cap=512):
    return max(t for t in range(8, min(rows, cap) + 1, 8) if rows % t == 0)


_SMALL = [
    ("ln_mix_pre", (2, 1024)), ("ln_mix_post", (2, 1024)), ("ln_ffn_pre", (2, 1024)), ("ln_ffn_post", (2, 1024)),
    ("ln_mem", (2, 1024)), ("w_spatial", (1, 6, 128, 128)), ("b_spatial", (1, 6, 128)), ("ln_shared", (1024,)),
    ("b_forget", (12,)), ("ln_v_g", (1, 768)), ("ln_v_b", (1, 768)),
]
_SMALL_TILE = 8 * LANES


def _small_rows(shape):
    return -(-math.prod(shape) // _SMALL_TILE) * 8


def _pack_small(vals, shapes):
    parts = []
    for name, shape in shapes:
        flat = vals[name].reshape(-1).astype(f32)
        rows = _small_rows(shape)
        parts.append(jnp.pad(flat, (0, rows * LANES - flat.shape[0])).reshape(rows, LANES))
    return jnp.concatenate(parts, axis=0)


def _unpack_small(buf, shapes):
    out = {}
    lo = 0
    for name, shape in shapes:
        rows = _small_rows(shape)
        out[name] = buf[lo:lo + rows].reshape(-1)[:math.prod(shape)].reshape(shape)
        lo += rows
    return out


def kernel(x, mem, ln_mix_pre, ln_mix_post, ln_ffn_pre, ln_ffn_post, ln_mem, w_mem_kv, w_out, w_ffn_gate, w_ffn_up, w_ffn_down, w_in_a, w_spatial, b_spatial, ln_v_g, ln_v_b, ln_shared, w_shared_kv, b_forget, w_in_b, loss_target, m_ln_mix_pre, m_ln_mix_post, m_ln_ffn_pre, m_ln_ffn_post, m_ln_mem, m_w_mem_kv, m_w_out, m_w_ffn_gate, m_w_ffn_up, m_w_ffn_down, m_w_in_a, m_w_spatial, m_b_spatial, m_ln_v_g, m_ln_v_b, m_ln_shared, m_w_shared_kv, m_b_forget, m_w_in_b, v_ln_mix_pre, v_ln_mix_post, v_ln_ffn_pre, v_ln_ffn_post, v_ln_mem, v_w_mem_kv, v_w_out, v_w_ffn_gate, v_w_ffn_up, v_w_ffn_down, v_w_in_a, v_w_spatial, v_b_spatial, v_ln_v_g, v_ln_v_b, v_ln_shared, v_w_shared_kv, v_b_forget, v_w_in_b):
    weights = dict(ln_mix_pre=ln_mix_pre, ln_mix_post=ln_mix_post, ln_ffn_pre=ln_ffn_pre, ln_ffn_post=ln_ffn_post, ln_mem=ln_mem,
                   w_mem_kv=w_mem_kv, w_out=w_out, w_ffn_gate=w_ffn_gate, w_ffn_up=w_ffn_up, w_ffn_down=w_ffn_down, w_in_a=w_in_a,
                   w_spatial=w_spatial, b_spatial=b_spatial, ln_v_g=ln_v_g, ln_v_b=ln_v_b, ln_shared=ln_shared,
                   w_shared_kv=w_shared_kv, b_forget=b_forget, w_in_b=w_in_b)
    mom_m = dict(ln_mix_pre=m_ln_mix_pre, ln_mix_post=m_ln_mix_post, ln_ffn_pre=m_ln_ffn_pre, ln_ffn_post=m_ln_ffn_post, ln_mem=m_ln_mem,
                 w_mem_kv=m_w_mem_kv, w_out=m_w_out, w_ffn_gate=m_w_ffn_gate, w_ffn_up=m_w_ffn_up, w_ffn_down=m_w_ffn_down, w_in_a=m_w_in_a,
                 w_spatial=m_w_spatial, b_spatial=m_b_spatial, ln_v_g=m_ln_v_g, ln_v_b=m_ln_v_b, ln_shared=m_ln_shared,
                 w_shared_kv=m_w_shared_kv, b_forget=m_b_forget, w_in_b=m_w_in_b)
    mom_v = dict(ln_mix_pre=v_ln_mix_pre, ln_mix_post=v_ln_mix_post, ln_ffn_pre=v_ln_ffn_pre, ln_ffn_post=v_ln_ffn_post, ln_mem=v_ln_mem,
                 w_mem_kv=v_w_mem_kv, w_out=v_w_out, w_ffn_gate=v_w_ffn_gate, w_ffn_up=v_w_ffn_up, w_ffn_down=v_w_ffn_down, w_in_a=v_w_in_a,
                 w_spatial=v_w_spatial, b_spatial=v_b_spatial, ln_v_g=v_ln_v_g, ln_v_b=v_ln_v_b, ln_shared=v_ln_shared,
                 w_shared_kv=v_w_shared_kv, b_forget=v_b_forget, w_in_b=v_w_in_b)
    names = list(weights)
    mx, my, mc = lax.axis_index("x"), lax.axis_index("y"), lax.axis_index("c")
    me = 4 * mx + 2 * my + mc

    h0 = x[0]
    mem0 = mem[0]
    tgt = loss_target[0]
    seq = h0.shape[0]

    vec = lambda a: a.reshape(1, -1)
    pad_to = lambda a, axis, size: jnp.pad(a, [(0, size - a.shape[i] if i == axis else 0) for i in range(a.ndim)])

    def after(tok, a):
        return a + tok[0, 0].astype(a.dtype)

    lnv_loc = pad_to(jnp.concatenate([ln_v_g, ln_v_b], axis=0), 0, 8)
    st_a = gather_start([w_in_a.astype(bf16), pad_to(lnv_loc, 1, LANES)[None]], [0, 0], "gather_a_start")
    mix_locs = lambda l, tok: [after(tok, w_mem_kv[l]).astype(bf16), w_out[l].astype(bf16)]

    def ffn_gather_start(l, tok):
        gate_up = gather_start([pad_to(after(tok, w_ffn_gate[l]).astype(bf16), 1, FF_SHARD_PAD),
                                pad_to(w_ffn_up[l].astype(bf16), 1, FF_SHARD_PAD)], [1, 1], f"gather_gate_up{l}_start")
        down = gather_start([pad_to(after(gate_up["token"], w_ffn_down[l]).astype(bf16), 0, FF_SHARD_PAD)], [0], f"gather_down{l}_start")
        return gate_up, down

    st_b = [gather_start(mix_locs(0, st_a["token"]), [0, 0], "gather_b0_start"), None]
    st_c = ffn_gather_start(0, st_b[0]["token"])
    st_d = gather_start([after(st_c[1]["token"], w_in_b[0]).astype(bf16), pad_to(w_shared_kv.astype(bf16), 1, KV_PAD)], [0, 0],
                        "gather_d_start")
    st_b[1] = gather_start(mix_locs(1, st_d["token"]), [0, 0], "gather_b1_start")
    st_e = ffn_gather_start(1, st_b[1]["token"])
    ws = w_spatial[0].astype(bf16)
    ws_t = ws.transpose(0, 2, 1)
    bs_t = b_spatial[0].T

    (a0,) = rms_fwd(h0, [after(st_e[1]["token"], vec(ln_mix_pre[0]))], "a0_norm")
    w_in_a8, lnv8 = gather_wait(st_a, [0, 0], a0, "gather_a_wait")
    w_in_a_full = w_in_a8.transpose(1, 0, 2).reshape(D_MODEL, -1)
    lnv_g = lnv8[:, 0, :MAIN_WIDTH // N_DEV].reshape(1, MAIN_WIDTH)
    lnv_b = lnv8[:, 1, :MAIN_WIDTH // N_DEV].reshape(1, MAIN_WIDTH)
    proj0 = mm(a0, w_in_a_full, "proj0", tn=896)
    main0 = gmlp_fwd(proj0, ws, bs_t, lnv_g, lnv_b, "gmlp_fwd", out_width=D_MODEL)
    w_mkv, w_o = [None, None], [None, None]
    w_mkv[0], w_o[0] = gather_wait(st_b[0], [0, 0], main0, "gather_b0_wait")
    (memn0,) = rms_fwd(mem0, [vec(ln_mem[0])], "mem0_norm")
    kvm0 = mm(memn0, w_mkv[0], "kvm0")
    mixed0 = mem_attn_fwd(proj0, 2 * MAIN_WIDTH // MEM_WIDTH, kvm0, main0, "mem_attn0")
    y1_0, hmid0, f0 = mm_resnorm(mixed0, w_o[0], h0, vec(ln_mix_post[0]), [vec(ln_ffn_pre[0])], "mix_out0")
    w_g0, w_u0 = gather_wait(st_c[0], [1, 1], f0, "gather_gate_up0_wait")
    gu0, act0 = ffn_up(f0, w_g0, w_u0, "ffn_up0")
    (w_d0,) = gather_wait(st_c[1], [0], act0, "gather_down0_wait")
    y2_0, h1, a1, sin1 = mm_resnorm(act0, w_d0, hmid0, vec(ln_ffn_post[0]), [vec(ln_mix_pre[1]), vec(ln_shared)], "ffn_down0")

    w_inb, w_kv = gather_wait(st_d, [0, 0], sin1, "gather_d_wait")
    kvb = mm(sin1, w_kv, "kv_shared", out_dtype=bf16, tn=MAIN_WIDTH, ncols=2 * MAIN_WIDTH)
    zf = mm(sin1, w_kv, "forget_logits", tn=256, col0=2 * MAIN_WIDTH, ncols=256)
    qb = mm(a1, w_inb, "proj1", out_dtype=bf16)
    z_t = jnp.pad(zf[:, :FOX_HEADS].T, ((0, 16 - FOX_HEADS), (0, 0)))
    bf_col = jnp.pad(b_forget, (0, 16 - FOX_HEADS)).reshape(16, 1)
    c_t = fgate_fwd(z_t, bf_col, "fgate_fwd")
    c_row = c_t[:FOX_HEADS].reshape(FOX_PAIRS, 2, seq)
    main1, lse, main1_b = fox_fwd(qb, kvb, c_row, "fox_fwd", out_width=D_MODEL)
    w_mkv[1], w_o[1] = gather_wait(st_b[1], [0, 0], main1, "gather_b1_wait")
    (memn1,) = rms_fwd(mem0, [vec(ln_mem[1])], "mem1_norm")
    kvm1 = mm(memn1, w_mkv[1], "kvm1")
    mixed1 = mem_attn_fwd(qb, MAIN_WIDTH // MEM_WIDTH, kvm1, main1_b, "mem_attn1")
    y1_1, hmid1, f1 = mm_resnorm(mixed1, w_o[1], h1, vec(ln_mix_post[1]), [vec(ln_ffn_pre[1])], "mix_out1")
    w_g1, w_u1 = gather_wait(st_e[0], [1, 1], f1, "gather_gate_up1_wait")
    gu1, act1 = ffn_up(f1, w_g1, w_u1, "ffn_up1")
    (w_d1,) = gather_wait(st_e[1], [0], act1, "gather_down1_wait")
    dh, d_y2_1, dg_fpost1, loss_tile = mm_resnorm_loss(act1, w_d1, hmid1, vec(ln_ffn_post[1]), tgt, "ffn_down1_loss")
    ffn_w = [(w_g0, w_u0, w_d0), (w_g1, w_u1, w_d1)]
    ff_shard = w_ffn_down.shape[1]

    small = {}

    def ffn_backward(layer, dh_out, d_y2, hmid, f, gu, act, y1):
        w_g, w_u, w_d = ffn_w[layer]
        dw_down = mm_tn(act, d_y2, f"dw_down{layer}")
        rs_down = scatter_start([dw_down], [0], f"scatter_down{layer}_start", used=[ff_shard])
        d_g, d_u = ffn_act_grad(d_y2, w_d, gu, f"ffn_act_grad{layer}")
        dw_g = mm_tn(d_g, f, f"dw_gate{layer}", dep=rs_down["token"])
        dw_u = mm_tn(d_u, f, f"dw_up{layer}")
        rs_gate_up = scatter_start([dw_g, dw_u], [0, 0], f"scatter_gate_up{layer}_start", used=[ff_shard] * 2)
        dh_mid, d_y1, dg_fpre, dg_mpost = ffn_in_grad(d_g, d_u, w_g, w_u, hmid, dh_out, after(rs_gate_up["token"], vec(ln_ffn_pre[layer])),
                                                      y1, vec(ln_mix_post[layer]), f"ffn_in_grad{layer}")
        return dh_mid, d_y1, dg_fpre, dg_mpost, (rs_down, rs_gate_up)

    def mix_out_backward(layer, d_y1, mixed):
        dw_out = mm_tn(mixed, d_y1, f"dw_out{layer}")
        d_mixed = mm(d_y1, w_o[layer], f"d_mixed{layer}", trans_b=True)
        return d_mixed, dw_out

    def mem_backward(layer, q_src, q_block, kvm, memn, d_mixed, into):
        d_qm, d_kvm = mem_attn_bwd(q_src, q_block, kvm, d_mixed, into, f"mem_attn_bwd{layer}")
        d_kvm_b = d_kvm.astype(bf16)
        dw_mkv = mm_tn(memn, d_kvm_b, f"dw_mem_kv{layer}")
        d_memn = mm(d_kvm_b, w_mkv[layer], f"d_memn{layer}", trans_b=True)
        _, dg_mem = rms_bwd(mem0, vec(ln_mem[layer]), d_memn, None, bf16, f"mem_norm_bwd{layer}")
        return d_qm, dw_mkv, dg_mem


    dh_mid1, d_y1_1, dg_fpre1, dg_mpost1, rs_ffn1 = ffn_backward(1, dh, d_y2_1, hmid1, f1, gu1, act1, y1_1)
    d_mixed1, dw_out1 = mix_out_backward(1, d_y1_1, mixed1)
    dq_b, dk, dv, dc = fox_bwd(qb, kvb, d_mixed1, main1, lse, c_row, "fox_bwd", dq_width=D_MODEL)
    d_proj1, dw_mkv1, dg_mem1 = mem_backward(1, qb, MAIN_WIDTH // MEM_WIDTH, kvm1, memn1, d_mixed1, dq_b)
    rs_mix1 = scatter_start([dw_out1, dw_mkv1], [0, 0], "scatter_mix1_start")
    dc_t = jnp.pad(dc.reshape(FOX_HEADS, seq), ((0, 16 - FOX_HEADS), (0, 0)))
    dz_t, db_f = fgate_bwd(dc_t, z_t, bf_col, "fgate_bwd")
    d_kvf = jnp.concatenate([dk, dv, jnp.pad(dz_t[:FOX_HEADS].T.astype(bf16), ((0, 0), (0, KV_PAD - KV_WIDTH)))], axis=-1)
    dw_in_b = mm_tn(a1, d_proj1, "dw_in_b", dep=rs_mix1["token"])
    dw_kv = mm_tn(sin1, d_kvf, "dw_kv", tn=896)
    rs_2 = scatter_start([dw_in_b, dw_kv], [0, 0], "scatter_shared_start")
    dh1, (dg_pre1, dg_shared), d_y2_0, dg_fpost0 = proj_in_grad(
        [(d_proj1, w_inb, vec(ln_mix_pre[1])), (d_kvf, w_kv, vec(ln_shared))], h1, dh_mid1, "in_grad1", dep=rs_2["token"],
        below=(y2_0, vec(ln_ffn_post[0])))

    dh_mid0, d_y1_0, dg_fpre0, dg_mpost0, rs_ffn0 = ffn_backward(0, dh1, d_y2_0, hmid0, f0, gu0, act0, y1_0)
    d_mixed0, dw_out0 = mix_out_backward(0, d_y1_0, mixed0)
    d_uv, dw_s, db_s, dg_lnv, db_lnv = gmlp_bwd(proj0, d_mixed0, ws, ws_t, bs_t, lnv_g, lnv_b, "gmlp_bwd", out_width=w_in_a_full.shape[1])
    d_proj0, dw_mkv0, dg_mem0 = mem_backward(0, proj0, 2 * MAIN_WIDTH // MEM_WIDTH, kvm0, memn0, d_mixed0, d_uv)
    rs_mix0 = scatter_start([dw_out0, dw_mkv0], [0, 0], "scatter_mix0_start")

    small["ln_mix_pre"] = jnp.concatenate([jnp.zeros_like(dg_pre1), dg_pre1], axis=0)
    small["ln_mix_post"] = jnp.concatenate([dg_mpost0, dg_mpost1], axis=0)
    small["ln_ffn_pre"] = jnp.concatenate([dg_fpre0, dg_fpre1], axis=0)
    small["ln_ffn_post"] = jnp.concatenate([dg_fpost0, dg_fpost1], axis=0)
    small["ln_mem"] = jnp.concatenate([dg_mem0, dg_mem1], axis=0)
    small["w_spatial"] = dw_s[None]
    small["b_spatial"] = db_s[:, :A_GROUPS].T[None]
    small["ln_shared"] = dg_shared[0]
    small["b_forget"] = db_f[:FOX_HEADS, 0]
    small["ln_v_g"] = dg_lnv
    small["ln_v_b"] = db_lnv
    small_rows = jnp.concatenate([_pack_small(small, _SMALL), after(rs_mix0["token"], loss_tile)], axis=0)
    st_small = gather_start([small_rows[None]], [0], "gather_small_grads_start")
    dw_in_a_t = mm_tn(d_proj0, a0, "dw_in_a", tk=896, dep=st_small["token"])
    rs_in_a = scatter_start([dw_in_a_t], [0], "scatter_in_a_start")
    grad_x, (dg_pre0,) = proj_in_grad([(d_proj0, w_in_a_full, vec(ln_mix_pre[0]))], h0, dh_mid0, "in_grad0", dep=rs_in_a["token"])
    st_last = gather_start([dg_pre0.reshape(1, 8, LANES)], [0], "gather_last_grad_start")

    (p_down1,) = scatter_wait(rs_ffn1[0], [0], after(st_last["token"], grad_x[:8, :LANES]), "scatter_down1_wait", used=[ff_shard])
    p_gate1, p_up1 = scatter_wait(rs_ffn1[1], [0, 0], p_down1, "scatter_gate_up1_wait", used=[ff_shard] * 2)
    p_out1, p_mkv1 = scatter_wait(rs_mix1, [0, 0], p_gate1, "scatter_mix1_wait")
    p_in_b, p_kv = scatter_wait(rs_2, [0, 0], p_out1, "scatter_shared_wait")
    (p_down0,) = scatter_wait(rs_ffn0[0], [0], p_in_b, "scatter_down0_wait", used=[ff_shard])
    p_gate0, p_up0 = scatter_wait(rs_ffn0[1], [0, 0], p_down0, "scatter_gate_up0_wait", used=[ff_shard] * 2)
    p_out0, p_mkv0 = scatter_wait(rs_mix0, [0, 0], p_gate0, "scatter_mix0_wait")
    owned_parts = dict(w_ffn_gate=[p_gate0, p_gate1], w_ffn_up=[p_up0, p_up1], w_ffn_down=[p_down0, p_down1], w_out=[p_out0, p_out1],
                       w_mem_kv=[p_mkv0, p_mkv1], w_in_b=[p_in_b], w_shared_kv=[p_kv])

    grad_w, delta, new_m, new_v = {}, {}, {}, {}
    transposed = ("w_ffn_gate", "w_ffn_up", "w_in_a")

    def adamw_sharded(n, parts):
        shape = weights[n].shape
        three_d = shape if len(shape) == 3 else (1,) + shape
        view = (lambda t: t.reshape(three_d).transpose(0, 2, 1)) if n in transposed else (lambda t: t.reshape(three_d))
        back = (lambda t: t.transpose(0, 2, 1).reshape(shape)) if n in transposed else (lambda t: t.reshape(shape))
        w_view = view(weights[n])
        outs = adamw_owned(w_view, parts, view(mom_m[n]), view(mom_v[n]), f"adamw_{n}", tr=_row_tile(w_view.shape[1]))
        grad_w[n], delta[n], new_m[n], new_v[n] = (back(t) for t in outs)

    for n, parts in owned_parts.items():
        adamw_sharded(n, parts)
    (p_in_a,) = scatter_wait(rs_in_a, [0], delta["w_shared_kv"], "scatter_in_a_wait")
    adamw_sharded("w_in_a", [p_in_a])
    (small_all,) = gather_wait(st_small, [0], p_in_a, "gather_small_grads_wait")
    (last_all,) = gather_wait(st_last, [0], small_all, "gather_last_grad_wait")
    small_sum = sum_leading(small_all, "sum_small_grads")
    loss = small_sum[small_rows.shape[0] - 1, 0]
    g_small = _unpack_small(small_sum, _SMALL)
    g_small["ln_mix_pre"] = jnp.concatenate([sum_leading(last_all, "sum_last_grad").reshape(1, D_MODEL), g_small["ln_mix_pre"][1:]], axis=0)
    shard = MAIN_WIDTH // N_DEV
    for n in ("ln_v_g", "ln_v_b"):
        g_small[n] = lax.dynamic_slice_in_dim(g_small[n], me * shard, shard, axis=1)
    grad_w.update(g_small)
    small_local_shapes = [(n, tuple(weights[n].shape)) for n, _ in _SMALL]
    packed = [_pack_small(src, small_local_shapes) for src in (weights, grad_w, mom_m, mom_v)]
    outs = adamw(*packed, "adamw_small", tr=packed[0].shape[0])
    for dst, buf in zip((delta, new_m, new_v), outs):
        dst.update(_unpack_small(buf, small_local_shapes))

    return (loss, grad_x[None], *[grad_w[n] for n in names], *[delta[n] for n in names],
            *[new_m[n] for n in names], *[new_v[n] for n in names])
```

```python
import functools
import math

import jax
import jax.numpy as jnp
from jax import lax
from jax.experimental import pallas as pl
from jax.experimental.pallas import tpu as pltpu

f32 = jnp.float32
bf16 = jnp.bfloat16
SDS = jax.ShapeDtypeStruct

D_MODEL = 1024
MAIN_WIDTH = 768
MEM_WIDTH = 256
HEAD_DIM = 64
MEM_HEADS = 4
FOX_HEADS = 12
FOX_PAIRS = FOX_HEADS // 2
CHUNK = 128
A_GROUPS = 6
FF_SHARD_PAD = 384
KV_WIDTH = 2 * MAIN_WIDTH + FOX_HEADS
KV_PAD = 1792
RMS_EPS = 1e-6
LN_EPS = 1e-5
ATT_SCALE = HEAD_DIM ** -0.5
ADAM_LR, ADAM_B1, ADAM_B2, ADAM_EPS, ADAM_WD, ADAM_STEP = 0.001, 0.9, 0.999, 1e-08, 0.01, 10
N_DEV = 8
MESH = pl.DeviceIdType.MESH
V7X_VMEM_LIMIT = 56 * 1024 * 1024
LANES = 128


def _cparams(*sem):
    return pltpu.CompilerParams(dimension_semantics=sem or None, vmem_limit_bytes=V7X_VMEM_LIMIT)


def _dot(a, b):
    return jnp.dot(a, b, preferred_element_type=f32)


def _dot_nt(a, b):
    return lax.dot_general(a, b, (((1,), (1,)), ((), ())), preferred_element_type=f32)


def _dot_tn(a, b):
    return lax.dot_general(a, b, (((0,), (0,)), ((), ())), preferred_element_type=f32)


def _gelu(x):
    k = math.sqrt(2.0 / math.pi)
    t = jnp.tanh(k * (x + 0.044715 * x * x * x))
    return 0.5 * x * (1.0 + t), t


def _gelu_grad(x, t):
    k = math.sqrt(2.0 / math.pi)
    return 0.5 * (1.0 + t) + 0.5 * x * (1.0 - t * t) * k * (1.0 + 3.0 * 0.044715 * x * x)


def _sigmoid(x):
    return 1.0 / (1.0 + jnp.exp(-x))


def rms_fwd(x, gains, name, tm=512):
    m, d = x.shape
    tm = min(tm, m)
    n = len(gains)

    def body(x_ref, *refs):
        xv = x_ref[...]
        y = xv * lax.rsqrt(jnp.sum(xv * xv, axis=-1, keepdims=True) * (1.0 / d) + RMS_EPS)
        for g_ref, o_ref in zip(refs[:n], refs[n:]):
            o_ref[...] = (y * g_ref[...]).astype(bf16)

    row = pl.BlockSpec((tm, d), lambda i: (i, 0))
    vec = pl.BlockSpec((1, d), lambda i: (0, 0))
    return pl.pallas_call(body, grid=(m // tm,), in_specs=[row] + [vec] * n, out_specs=[row] * n,
                          out_shape=[SDS((m, d), bf16)] * n, name=name, compiler_params=_cparams("parallel"))(x, *gains)


def rms_bwd(x, g, dy, add, out_dtype, name, tm=512):
    m, d = x.shape
    tm = min(tm, m)
    has_add = add is not None

    def body(x_ref, g_ref, dy_ref, *refs):
        dx_ref, dg_ref = refs[-2], refs[-1]
        xv = x_ref[...]
        dyv = dy_ref[...].astype(f32)
        r = lax.rsqrt(jnp.sum(xv * xv, axis=-1, keepdims=True) * (1.0 / d) + RMS_EPS)
        xn = xv * r
        dyg = dyv * g_ref[...]
        dx = r * (dyg - xn * (jnp.sum(dyg * xn, axis=-1, keepdims=True) * (1.0 / d)))
        if has_add:
            dx = dx + refs[0][...]
        dx_ref[...] = dx.astype(out_dtype)

        @pl.when(pl.program_id(0) == 0)
        def _():
            dg_ref[...] = jnp.zeros_like(dg_ref)

        dg_ref[...] += jnp.sum(dyv * xn, axis=0, keepdims=True)

    row = pl.BlockSpec((tm, d), lambda i: (i, 0))
    vec = pl.BlockSpec((1, d), lambda i: (0, 0))
    ins = [x, g, dy] + ([add] if has_add else [])
    return pl.pallas_call(body, grid=(m // tm,), in_specs=[row, vec, row] + ([row] if has_add else []),
                          out_specs=[row, vec], out_shape=[SDS((m, d), out_dtype), SDS((1, d), f32)], name=name,
                          compiler_params=_cparams("arbitrary"))(*ins)


def mm(a, b, name, trans_b=False, out_dtype=f32, tm=1024, tn=1024, col0=0, ncols=None, dep=None):
    m, k = a.shape
    n_all = b.shape[0] if trans_b else b.shape[1]
    n = n_all if ncols is None else ncols
    tm, tn = min(tm, m), min(tn, n)
    assert m % tm == 0 and n % tn == 0 and col0 % tn == 0 and not (trans_b and col0), (name, m, n, tm, tn)
    jb = col0 // tn

    def body(a_ref, b_ref, *rest):
        r = _dot_nt(a_ref[...], b_ref[...]) if trans_b else _dot(a_ref[...], b_ref[...])
        rest[-1][...] = r.astype(out_dtype)

    if trans_b:
        b_spec = pl.BlockSpec((tn, k), lambda j, i: (j, 0))
    else:
        b_spec = pl.BlockSpec((k, tn), lambda j, i: (0, jb + j))
    deps = [] if dep is None else [dep]
    dep_specs = [pl.BlockSpec((8, LANES), lambda j, i: (0, 0))] * len(deps)
    return pl.pallas_call(body, grid=(n // tn, m // tm), in_specs=[pl.BlockSpec((tm, k), lambda j, i: (i, 0)), b_spec] + dep_specs,
                          out_specs=pl.BlockSpec((tm, tn), lambda j, i: (i, j)), out_shape=SDS((m, n), out_dtype),
                          name=name, compiler_params=_cparams("parallel", "parallel"))(a, b, *deps)


def mm_tn(a, g, name, tk=1024, tn=1024, out_dtype=bf16, dep=None):
    s, k = a.shape
    n = g.shape[1]
    tk, tn = min(tk, k), min(tn, n)
    assert k % tk == 0 and n % tn == 0, (name, k, n, tk, tn)

    def body(a_ref, g_ref, *rest):
        rest[-1][...] = _dot_tn(a_ref[...], g_ref[...]).astype(out_dtype)

    deps = [] if dep is None else [dep]
    dep_specs = [pl.BlockSpec((8, LANES), lambda i, j: (0, 0))] * len(deps)
    return pl.pallas_call(body, grid=(k // tk, n // tn),
                          in_specs=[pl.BlockSpec((s, tk), lambda i, j: (0, i)), pl.BlockSpec((s, tn), lambda i, j: (0, j))] + dep_specs,
                          out_specs=pl.BlockSpec((tk, tn), lambda i, j: (i, j)), out_shape=SDS((k, n), out_dtype), name=name,
                          compiler_params=_cparams("parallel", "parallel"))(a, g, *deps)


def _resident(shape, index_map):
    return pl.BlockSpec(shape, index_map, pipeline_mode=pl.Buffered(1))


def _rms(xv):
    return xv * lax.rsqrt(jnp.sum(xv * xv, axis=-1, keepdims=True) * (1.0 / xv.shape[-1]) + RMS_EPS)


def _rms_bwd_math(xv, g, dy):
    d = xv.shape[-1]
    r = lax.rsqrt(jnp.sum(xv * xv, axis=-1, keepdims=True) * (1.0 / d) + RMS_EPS)
    xn = xv * r
    dyg = dy * g
    dx = r * (dyg - xn * (jnp.sum(dyg * xn, axis=-1, keepdims=True) * (1.0 / d)))
    return dx, jnp.sum(dy * xn, axis=0, keepdims=True)


SUB_ROWS = 512


def mm_resnorm(a, b, h, g_post, gains, name, tm=512):
    m, k = a.shape
    d = b.shape[1]
    n = len(gains)

    def body(a_ref, b_ref, h_ref, gp_ref, *refs):
        for r in range(tm // SUB_ROWS):
            rows = slice(r * SUB_ROWS, (r + 1) * SUB_ROWS)
            y = _dot(a_ref[rows, :], b_ref[...])
            refs[n][rows, :] = y
            hn = h_ref[rows, :] + _rms(y) * gp_ref[...]
            refs[n + 1][rows, :] = hn
            if n:
                z = _rms(hn)
                for g_ref, o_ref in zip(refs[:n], refs[n + 2:]):
                    o_ref[rows, :] = (z * g_ref[...]).astype(bf16)

    row = pl.BlockSpec((tm, d), lambda i: (i, 0))
    vec = pl.BlockSpec((1, d), lambda i: (0, 0))
    return pl.pallas_call(body, grid=(m // tm,),
                          in_specs=[pl.BlockSpec((tm, k), lambda i: (i, 0)), _resident((k, d), lambda i: (0, 0)), row, vec] + [vec] * n,
                          out_specs=[row] * (n + 2), out_shape=[SDS((m, d), f32)] * 2 + [SDS((m, d), bf16)] * n, name=name,
                          compiler_params=_cparams("parallel"))(a, b, h, g_post, *gains)


def mm_resnorm_loss(a, b, h, g_post, tgt, name, tm=512):
    m, k = a.shape
    d = b.shape[1]

    def body(a_ref, b_ref, h_ref, gp_ref, t_ref, dh_ref, dy_ref, dg_ref, l_ref):
        @pl.when(pl.program_id(0) == 0)
        def _():
            dg_ref[...] = jnp.zeros_like(dg_ref)
            l_ref[...] = jnp.zeros_like(l_ref)

        y = _dot(a_ref[...], b_ref[...])
        e = h_ref[...] + _rms(y) * gp_ref[...] - t_ref[...]
        dh = e * (1.0 / d)
        dh_ref[...] = dh
        part = jnp.sum(jnp.sum(e * e, axis=-1, keepdims=True), axis=0, keepdims=True) * (0.5 / d)
        l_ref[...] += jnp.broadcast_to(part, l_ref.shape)
        dy, dg = _rms_bwd_math(y, gp_ref[...], dh)
        dy_ref[...] = dy.astype(bf16)
        dg_ref[...] += dg

    row = pl.BlockSpec((tm, d), lambda i: (i, 0))
    vec = pl.BlockSpec((1, d), lambda i: (0, 0))
    return pl.pallas_call(body, grid=(m // tm,),
                          in_specs=[pl.BlockSpec((tm, k), lambda i: (i, 0)), _resident((k, d), lambda i: (0, 0)), row, vec, row],
                          out_specs=[row, row, vec, pl.BlockSpec((8, LANES), lambda i: (0, 0))],
                          out_shape=[SDS((m, d), f32), SDS((m, d), bf16), SDS((1, d), f32), SDS((8, LANES), f32)], name=name,
                          compiler_params=_cparams("arbitrary"))(a, b, h, g_post, tgt)


FFN_TILE = 1536


def ffn_act_grad(d_y2, w_d, factors, name, tm=1024):
    s, d = d_y2.shape
    ff = w_d.shape[0]
    tn = FFN_TILE
    nb = ff // tn

    def body(a_ref, b_ref, g_ref, u_ref, dg_ref, du_ref):
        av = a_ref[...]
        tc = 256
        for c in range(tn // tc):
            cols = slice(c * tc, (c + 1) * tc)
            da = _dot_nt(av, b_ref[cols, :])
            dg_ref[:, cols] = (da * g_ref[:, cols].astype(f32)).astype(bf16)
            du_ref[:, cols] = (da * u_ref[:, cols].astype(f32)).astype(bf16)

    tile = pl.BlockSpec((tm, tn), lambda j, i: (i, j))
    return pl.pallas_call(body, grid=(nb, s // tm),
                          in_specs=[pl.BlockSpec((tm, d), lambda j, i: (i, 0)), pl.BlockSpec((tn, d), lambda j, i: (j, 0)),
                                    pl.BlockSpec((tm, tn), lambda j, i: (i, 2 * j)), pl.BlockSpec((tm, tn), lambda j, i: (i, 2 * j + 1))],
                          out_specs=[tile, tile], out_shape=[SDS((s, ff), bf16)] * 2, name=name,
                          compiler_params=_cparams("parallel", "parallel"))(d_y2, w_d, factors, factors)


def ffn_in_grad(d_g, d_u, w_g, w_u, hmid, dh_out, g_pre, y1, g_post, name, tm=512):
    s, ff = d_g.shape
    d = w_g.shape[0]

    def body(dg_ref, du_ref, wg_ref, wu_ref, hm_ref, dho_ref, gpre_ref, y1_ref, gpost_ref, dhm_ref, dy1_ref, dgpre_ref, dgpost_ref):
        @pl.when(pl.program_id(0) == 0)
        def _():
            dgpre_ref[...] = jnp.zeros_like(dgpre_ref)
            dgpost_ref[...] = jnp.zeros_like(dgpost_ref)

        for r in range(tm // SUB_ROWS):
            rows = slice(r * SUB_ROWS, (r + 1) * SUB_ROWS)
            d_f = _dot_nt(dg_ref[rows, :], wg_ref[...]) + _dot_nt(du_ref[rows, :], wu_ref[...])
            dx, dg1 = _rms_bwd_math(hm_ref[rows, :], gpre_ref[...], d_f)
            dh_mid = dho_ref[rows, :] + dx
            dhm_ref[rows, :] = dh_mid
            dgpre_ref[...] += dg1
            dy1, dg2 = _rms_bwd_math(y1_ref[rows, :], gpost_ref[...], dh_mid)
            dy1_ref[rows, :] = dy1.astype(bf16)
            dgpost_ref[...] += dg2

    row = pl.BlockSpec((tm, d), lambda i: (i, 0))
    vec = pl.BlockSpec((1, d), lambda i: (0, 0))
    wide = pl.BlockSpec((tm, ff), lambda i: (i, 0))
    w_spec = _resident((d, ff), lambda i: (0, 0))
    return pl.pallas_call(body, grid=(s // tm,), in_specs=[wide, wide, w_spec, w_spec, row, row, vec, row, vec],
                          out_specs=[row, row, vec, vec], out_shape=[SDS((s, d), f32), SDS((s, d), bf16), SDS((1, d), f32), SDS((1, d), f32)],
                          name=name, compiler_params=_cparams("arbitrary"))(d_g, d_u, w_g, w_u, hmid, dh_out, g_pre, y1, g_post)


def proj_in_grad(pairs, x, add, name, tm=512, dep=None, below=None):
    s, d = x.shape
    n = len(pairs)
    extra = [] if dep is None else [dep]
    n_below = 0 if below is None else 2

    def body(*refs):
        x_ref, add_ref = refs[3 * n], refs[3 * n + 1]
        below_refs = refs[3 * n + 2:3 * n + 2 + n_below]
        outs = refs[3 * n + 2 + n_below + len(extra):]

        @pl.when(pl.program_id(0) == 0)
        def _():
            for o in outs[1:1 + n] + outs[2 + n:]:
                o[...] = jnp.zeros_like(o)

        xv = x_ref[...]
        dx = add_ref[...]
        for i in range(n):
            a_ref, b_ref, g_ref = refs[3 * i:3 * i + 3]
            dxi, dgi = _rms_bwd_math(xv, g_ref[...], _dot_nt(a_ref[...], b_ref[...]))
            dx = dx + dxi
            outs[1 + i][...] += dgi
        outs[0][...] = dx
        if below is not None:
            dy, dg = _rms_bwd_math(below_refs[0][...], below_refs[1][...], dx)
            outs[1 + n][...] = dy.astype(bf16)
            outs[2 + n][...] += dg

    row = pl.BlockSpec((tm, d), lambda i: (i, 0))
    vec = pl.BlockSpec((1, d), lambda i: (0, 0))
    in_specs, args = [], []
    for a, b, g in pairs:
        k = a.shape[1]
        in_specs += [pl.BlockSpec((tm, k), lambda i: (i, 0)), _resident((d, k), lambda i: (0, 0)), vec]
        args += [a, b, g]
    in_specs += [row, row] + [row, vec][:n_below] + [pl.BlockSpec((8, LANES), lambda i: (0, 0))] * len(extra)
    out_specs = [row] + [vec] * n + [row, vec][:n_below]
    out_shape = [SDS((s, d), f32)] + [SDS((1, d), f32)] * n + [SDS((s, d), bf16), SDS((1, d), f32)][:n_below]
    out = pl.pallas_call(body, grid=(s // tm,), in_specs=in_specs, out_specs=out_specs, out_shape=out_shape, name=name,
                         compiler_params=_cparams("arbitrary"))(*args, x, add, *(below or ()), *extra)
    return (out[0], out[1:1 + n]) + tuple(out[1 + n:])


def ffn_up(f, wg, wu, name, tm=1024, tc=256):
    s, d = f.shape
    ff = wg.shape[-1]
    tn = FFN_TILE

    def body(f_ref, wg_ref, wu_ref, fac_ref, act_ref):
        fv = f_ref[...]
        for c in range(tn // tc):
            lo = c * tc
            gg = _dot(fv, wg_ref[:, lo:lo + tc])
            uu = _dot(fv, wu_ref[:, lo:lo + tc])
            sg = _sigmoid(gg)
            silu = gg * sg
            fac_ref[:, lo:lo + tc] = (uu * (sg + silu * (1.0 - sg))).astype(bf16)
            fac_ref[:, tn + lo:tn + lo + tc] = silu.astype(bf16)
            act_ref[:, lo:lo + tc] = (silu * uu).astype(bf16)

    w_spec = pl.BlockSpec((d, tn), lambda j, i: (0, j))
    return pl.pallas_call(body, grid=(ff // tn, s // tm), in_specs=[pl.BlockSpec((tm, d), lambda j, i: (i, 0)), w_spec, w_spec],
                          out_specs=[pl.BlockSpec((tm, 2 * tn), lambda j, i: (i, j)), pl.BlockSpec((tm, tn), lambda j, i: (i, j))],
                          out_shape=[SDS((s, 2 * ff), bf16), SDS((s, ff), bf16)], name=name,
                          compiler_params=_cparams("parallel", "parallel"))(f, wg, wu)


def _gmlp_forward_chunk(u, v, w_refs, bias, ln_g, ln_b):
    gu, tu = _gelu(u)
    gv, tv = _gelu(v)
    mu = jnp.sum(gv, axis=-1, keepdims=True) * (1.0 / MAIN_WIDTH)
    xc = gv - mu
    rstd = lax.rsqrt(jnp.sum(xc * xc, axis=-1, keepdims=True) * (1.0 / MAIN_WIDTH) + LN_EPS)
    xhat = xc * rstd
    vln = xhat * ln_g + ln_b
    row = lax.broadcasted_iota(jnp.int32, (CHUNK, CHUNK), 0)
    col = lax.broadcasted_iota(jnp.int32, (CHUNK, CHUNK), 1)
    s_parts = []
    for g in range(A_GROUPS):
        w = jnp.where(col <= row, w_refs[g], jnp.zeros((), bf16))
        s_parts.append(_dot(w, vln[:, g * CHUNK:(g + 1) * CHUNK].astype(bf16)) + bias[:, g:g + 1])
    return gu, tu, tv, rstd, xhat, vln, s_parts


def gmlp_fwd(proj, ws, bs_t, ln_g, ln_b, name, tm=512, out_width=MAIN_WIDTH):
    s = proj.shape[0]

    def body(u_ref, v_ref, w_ref, b_ref, g_ref, bb_ref, o_ref):
        bias = b_ref[...]
        for c in range(tm // CHUNK):
            rows = slice(c * CHUNK, (c + 1) * CHUNK)
            gu, _, _, _, _, _, s_parts = _gmlp_forward_chunk(u_ref[rows, :], v_ref[rows, :], w_ref, bias, g_ref[...], bb_ref[...])
            for g in range(A_GROUPS):
                cols = slice(g * CHUNK, (g + 1) * CHUNK)
                o_ref[rows, cols] = (gu[:, cols] * s_parts[g]).astype(bf16)

    vec = pl.BlockSpec((1, MAIN_WIDTH), lambda i: (0, 0))
    return pl.pallas_call(
        body, grid=(s // tm,),
        in_specs=[pl.BlockSpec((tm, MAIN_WIDTH), lambda i: (i, 0)), pl.BlockSpec((tm, MAIN_WIDTH), lambda i: (i, 1)),
                  pl.BlockSpec((A_GROUPS, CHUNK, CHUNK), lambda i: (0, 0, 0)), pl.BlockSpec((CHUNK, A_GROUPS), lambda i: (0, 0)), vec, vec],
        out_specs=pl.BlockSpec((tm, MAIN_WIDTH), lambda i: (i, 0)), out_shape=SDS((s, out_width), bf16), name=name,
        compiler_params=_cparams("parallel"))(proj, proj, ws, bs_t, ln_g, ln_b)


def gmlp_bwd(proj, d_mixed, ws, ws_t, bs_t, ln_g, ln_b, name, tm=512, out_width=2 * MAIN_WIDTH):
    s = proj.shape[0]

    def body(u_ref, v_ref, dm_ref, w_ref, wt_ref, b_ref, g_ref, bb_ref, duv_ref, dw_ref, db_ref, dg_ref, dbb_ref):
        @pl.when(pl.program_id(0) == 0)
        def _():
            dw_ref[...] = jnp.zeros_like(dw_ref)
            db_ref[...] = jnp.zeros_like(db_ref)
            dg_ref[...] = jnp.zeros_like(dg_ref)
            dbb_ref[...] = jnp.zeros_like(dbb_ref)

        bias = b_ref[...]
        ln_gv = g_ref[...]
        row = lax.broadcasted_iota(jnp.int32, (CHUNK, CHUNK), 0)
        col = lax.broadcasted_iota(jnp.int32, (CHUNK, CHUNK), 1)
        lane = lax.broadcasted_iota(jnp.int32, (CHUNK, LANES), 1)
        for c in range(tm // CHUNK):
            rows = slice(c * CHUNK, (c + 1) * CHUNK)
            u = u_ref[rows, :]
            v = v_ref[rows, :]
            gu, tu, tv, rstd, xhat, vln, s_parts = _gmlp_forward_chunk(u, v, w_ref, bias, ln_gv, bb_ref[...])
            dm = dm_ref[rows, :]
            d_vln_parts = []
            d_gu_parts = []
            db_acc = jnp.zeros((CHUNK, LANES), f32)
            for g in range(A_GROUPS):
                cols = slice(g * CHUNK, (g + 1) * CHUNK)
                dmg = dm[:, cols]
                d_gu_parts.append(dmg * s_parts[g])
                d_s = dmg * gu[:, cols]
                db_acc = db_acc + jnp.where(lane == g, jnp.sum(d_s, axis=-1, keepdims=True), 0.0)
                d_sb = d_s.astype(bf16)
                dw_ref[g] += jnp.where(col <= row, _dot_nt(d_sb, vln[:, cols].astype(bf16)), 0.0)
                wt = jnp.where(row <= col, wt_ref[g], jnp.zeros((), bf16))
                d_vln_parts.append(_dot(wt, d_sb))
            db_ref[...] += db_acc
            d_vln = jnp.concatenate(d_vln_parts, axis=-1)
            d_gu = jnp.concatenate(d_gu_parts, axis=-1)
            dg_ref[...] += jnp.sum(d_vln * xhat, axis=0, keepdims=True)
            dbb_ref[...] += jnp.sum(d_vln, axis=0, keepdims=True)
            dxh = d_vln * ln_gv
            m1 = jnp.sum(dxh, axis=-1, keepdims=True) * (1.0 / MAIN_WIDTH)
            m2 = jnp.sum(dxh * xhat, axis=-1, keepdims=True) * (1.0 / MAIN_WIDTH)
            d_gv = rstd * (dxh - m1 - xhat * m2)
            duv_ref[rows, :MAIN_WIDTH] = (d_gu * _gelu_grad(u, tu)).astype(bf16)
            duv_ref[rows, MAIN_WIDTH:] = (d_gv * _gelu_grad(v, tv)).astype(bf16)

    vec = pl.BlockSpec((1, MAIN_WIDTH), lambda i: (0, 0))
    wspec = pl.BlockSpec((A_GROUPS, CHUNK, CHUNK), lambda i: (0, 0, 0))
    return pl.pallas_call(
        body, grid=(s // tm,),
        in_specs=[pl.BlockSpec((tm, MAIN_WIDTH), lambda i: (i, 0)), pl.BlockSpec((tm, MAIN_WIDTH), lambda i: (i, 1)),
                  pl.BlockSpec((tm, MAIN_WIDTH), lambda i: (i, 0)), wspec, wspec, pl.BlockSpec((CHUNK, A_GROUPS), lambda i: (0, 0)), vec, vec],
        out_specs=[pl.BlockSpec((tm, 2 * MAIN_WIDTH), lambda i: (i, 0)), wspec, pl.BlockSpec((CHUNK, LANES), lambda i: (0, 0)), vec, vec],
        out_shape=[SDS((s, out_width), bf16), SDS((A_GROUPS, CHUNK, CHUNK), f32), SDS((CHUNK, LANES), f32),
                   SDS((1, MAIN_WIDTH), f32), SDS((1, MAIN_WIDTH), f32)],
        name=name, compiler_params=_cparams("arbitrary"))(proj, proj, d_mixed, ws, ws_t, bs_t, ln_g, ln_b)


def _head_mask(width, h):
    lane = lax.broadcasted_iota(jnp.int32, (1, width), 1)
    return (lane >= h * HEAD_DIM) & (lane < (h + 1) * HEAD_DIM)


def mem_attn_fwd(proj, q_block, kv, into, name, tm=512):
    s = proj.shape[0]
    n_mem = kv.shape[0]
    out_block = into.shape[1] // MEM_WIDTH - 1

    def body(q_ref, kv_ref, into_ref, o_ref):
        q = q_ref[...].astype(f32)
        k = kv_ref[:, :MEM_WIDTH].astype(bf16)
        v = kv_ref[:, MEM_WIDTH:].astype(bf16)
        out = jnp.zeros((tm, MEM_WIDTH), f32)
        for h in range(MEM_HEADS):
            msk = _head_mask(MEM_WIDTH, h)
            qh = jnp.where(msk, q, 0.0).astype(bf16)
            sc = _dot_nt(qh, k) * ATT_SCALE
            e = jnp.exp(sc - jnp.max(sc, axis=-1, keepdims=True))
            p = e / jnp.sum(e, axis=-1, keepdims=True)
            out = jnp.where(msk, _dot(p.astype(bf16), v), out)
        o_ref[...] = out.astype(bf16)

    return pl.pallas_call(body, grid=(s // tm,),
                          in_specs=[pl.BlockSpec((tm, MEM_WIDTH), lambda i: (i, q_block)), pl.BlockSpec((n_mem, 2 * MEM_WIDTH), lambda i: (0, 0)), _ANY],
                          out_specs=pl.BlockSpec((tm, MEM_WIDTH), lambda i: (i, out_block)), out_shape=SDS(into.shape, bf16), name=name,
                          input_output_aliases={2: 0}, compiler_params=_cparams("parallel"))(proj, kv, into)


def mem_attn_bwd(proj, q_block, kv, d_mixed, into, name, tm=512):
    s = proj.shape[0]
    n_mem = kv.shape[0]
    out_block = into.shape[1] // MEM_WIDTH - 1

    def body(q_ref, kv_ref, do_ref, into_ref, dq_ref, dkv_ref):
        @pl.when(pl.program_id(0) == 0)
        def _():
            dkv_ref[...] = jnp.zeros_like(dkv_ref)

        q = q_ref[...].astype(f32)
        do = do_ref[...]
        k = kv_ref[:, :MEM_WIDTH].astype(bf16)
        v = kv_ref[:, MEM_WIDTH:].astype(bf16)
        dq = jnp.zeros((tm, MEM_WIDTH), f32)
        dk = jnp.zeros((n_mem, MEM_WIDTH), f32)
        dv = jnp.zeros((n_mem, MEM_WIDTH), f32)
        for h in range(MEM_HEADS):
            msk = _head_mask(MEM_WIDTH, h)
            qh = jnp.where(msk, q, 0.0).astype(bf16)
            doh = jnp.where(msk, do, 0.0).astype(bf16)
            sc = _dot_nt(qh, k) * ATT_SCALE
            e = jnp.exp(sc - jnp.max(sc, axis=-1, keepdims=True))
            p = e / jnp.sum(e, axis=-1, keepdims=True)
            dp = _dot_nt(doh, v)
            ds = p * (dp - jnp.sum(dp * p, axis=-1, keepdims=True))
            dsb = (ds * ATT_SCALE).astype(bf16)
            dq = jnp.where(msk, _dot(dsb, k), dq)
            dk = dk + _dot_tn(dsb, qh)
            dv = dv + _dot_tn(p.astype(bf16), doh)
        dq_ref[...] = dq.astype(bf16)
        dkv_ref[:, :MEM_WIDTH] += dk
        dkv_ref[:, MEM_WIDTH:] += dv

    return pl.pallas_call(
        body, grid=(s // tm,),
        in_specs=[pl.BlockSpec((tm, MEM_WIDTH), lambda i: (i, q_block)), pl.BlockSpec((n_mem, 2 * MEM_WIDTH), lambda i: (0, 0)),
                  pl.BlockSpec((tm, MEM_WIDTH), lambda i: (i, MAIN_WIDTH // MEM_WIDTH)), _ANY],
        out_specs=[pl.BlockSpec((tm, MEM_WIDTH), lambda i: (i, out_block)), pl.BlockSpec((n_mem, 2 * MEM_WIDTH), lambda i: (0, 0))],
        out_shape=[SDS(into.shape, bf16), SDS((n_mem, 2 * MEM_WIDTH), f32)], name=name,
        input_output_aliases={3: 0}, compiler_params=_cparams("arbitrary"))(proj, kv, d_mixed, into)


def _tri(t, upper):
    r = lax.broadcasted_iota(jnp.int32, (t, t), 0)
    c = lax.broadcasted_iota(jnp.int32, (t, t), 1)
    return ((r <= c) if upper else (r >= c)).astype(f32)


def fgate_fwd(z_t, b, name, t=512):
    hh, s = z_t.shape

    def body(z_ref, b_ref, c_ref):
        u = _tri(t, True)
        carry = jnp.zeros((hh, 1), f32)
        for blk in range(s // t):
            x = z_ref[:, blk * t:(blk + 1) * t] + b_ref[...]
            logf = jnp.minimum(x, 0.0) - jnp.log(1.0 + jnp.exp(-jnp.abs(x)))
            y = jnp.dot(logf, u, precision=lax.Precision.HIGHEST, preferred_element_type=f32) + carry
            c_ref[:, blk * t:(blk + 1) * t] = y
            carry = y[:, t - 1:t]

    return pl.pallas_call(body, out_shape=SDS((hh, s), f32), name=name, compiler_params=_cparams())(z_t, b)


def fgate_bwd(dc_t, z_t, b, name, t=512):
    hh, s = z_t.shape

    def body(dc_ref, z_ref, b_ref, dz_ref, db_ref):
        low = _tri(t, False)
        carry = jnp.zeros((hh, 1), f32)
        total = jnp.zeros((hh, 1), f32)
        for blk in reversed(range(s // t)):
            cols = slice(blk * t, (blk + 1) * t)
            y = jnp.dot(dc_ref[:, cols], low, precision=lax.Precision.HIGHEST, preferred_element_type=f32) + carry
            carry = y[:, 0:1]
            dz = y * _sigmoid(-(z_ref[:, cols] + b_ref[...]))
            dz_ref[:, cols] = dz
            total = total + jnp.sum(dz, axis=-1, keepdims=True)
        db_ref[...] = jnp.broadcast_to(total, db_ref.shape)

    return pl.pallas_call(body, out_shape=[SDS((hh, s), f32), SDS((hh, LANES), f32)], name=name,
                          compiler_params=_cparams())(dc_t, z_t, b)


def _pair_masks():
    lane = lax.broadcasted_iota(jnp.int32, (1, LANES), 1)
    return [lane < HEAD_DIM, lane >= HEAD_DIM]


def _tile_base(cr_ref, hh, lo):
    return cr_ref[hh:hh + 1, pl.ds(lo, LANES)][:, 0:1]


def fox_fwd(q, kv, c_row, name, tq=512, out_width=MAIN_WIDTH):
    s = kv.shape[0]
    nq = s // tq

    def body(q_ref, k_ref, v_ref, cr_ref, o_ref, lse_ref, ob_ref):
        i = pl.program_id(1)
        qv = q_ref[...]
        masks = _pair_masks()
        row = lax.broadcasted_iota(jnp.int32, (tq, tq), 0)
        col = lax.broadcasted_iota(jnp.int32, (tq, tq), 1)
        qh = [jnp.where(masks[hh], qv, jnp.zeros((), bf16)) * ATT_SCALE for hh in range(2)]
        ct = [_tile_base(cr_ref, hh, pl.multiple_of(i * tq, tq)) for hh in range(2)]

        def block(j, carry, diag):
            lo = pl.multiple_of(j * tq, tq)
            ks = k_ref[pl.ds(lo, tq), :]
            vs = v_ref[pl.ds(lo, tq), :]
            out = []
            for hh in range(2):
                m, l, acc = carry[hh]
                sc = _dot_nt(qh[hh], ks) + (ct[hh] - cr_ref[hh:hh + 1, pl.ds(lo, tq)])
                if diag:
                    sc = jnp.where(col <= row, sc, -jnp.inf)
                m_new = jnp.maximum(m, jnp.max(sc, axis=-1, keepdims=True))
                alpha = jnp.exp(m - m_new)
                p = jnp.exp(sc - m_new)
                l = alpha * l + jnp.sum(p, axis=-1, keepdims=True)
                p_hi = p.astype(bf16)
                p_lo = (p - p_hi.astype(f32)).astype(bf16)
                acc = alpha * acc + (_dot(p_hi, vs) + _dot(p_lo, vs))
                out.append((m_new, l, acc))
            return tuple(out)

        init = (jnp.full((tq, 1), -jnp.inf, f32), jnp.zeros((tq, 1), f32), jnp.zeros((tq, LANES), f32))
        carry = lax.fori_loop(0, i, functools.partial(block, diag=False), (init, init))
        res = [(acc / l, m + jnp.log(l)) for m, l, acc in block(i, carry, True)]
        out = jnp.where(masks[0], res[0][0], res[1][0])
        o_ref[...] = out
        ob_ref[...] = out.astype(bf16)
        lse_ref[...] = jnp.where(masks[0], res[0][1], res[1][1])

    return pl.pallas_call(
        body, grid=(FOX_PAIRS, nq),
        in_specs=[pl.BlockSpec((tq, LANES), lambda p, i: (i, p)), pl.BlockSpec((s, LANES), lambda p, i: (0, p)),
                  pl.BlockSpec((s, LANES), lambda p, i: (0, FOX_PAIRS + p)), pl.BlockSpec((None, 2, s), lambda p, i: (p, 0, 0))],
        out_specs=[pl.BlockSpec((tq, LANES), lambda p, i: (i, p)), pl.BlockSpec((None, tq, LANES), lambda p, i: (p, i, 0)),
                   pl.BlockSpec((tq, LANES), lambda p, i: (i, p))],
        out_shape=[SDS((s, MAIN_WIDTH), f32), SDS((FOX_PAIRS, s, LANES), f32), SDS((s, out_width), bf16)], name=name,
        compiler_params=_cparams("parallel", "parallel"))(q, kv, kv, c_row)


def fox_bwd(q, kv, d_mixed, o, lse, c_row, name, tq=512, dq_width=MAIN_WIDTH):
    s = kv.shape[0]
    nq = s // tq

    def body(q_ref, k_ref, v_ref, do_ref, o_ref, lse_ref, cr_ref, dqb_ref, dk_ref, dv_ref, dc_ref, dq_ref):
        j = pl.program_id(1)

        @pl.when(j == 0)
        def _():
            dq_ref[...] = jnp.zeros_like(dq_ref)

        masks = _pair_masks()
        sub = lax.broadcasted_iota(jnp.int32, (LANES, 1), 0)
        sub_masks = [sub < HEAD_DIM, sub >= HEAD_DIM]
        row = lax.broadcasted_iota(jnp.int32, (tq, tq), 0)
        col = lax.broadcasted_iota(jnp.int32, (tq, tq), 1)
        kj = k_ref[...]
        vj = v_ref[...]
        lo_j = pl.multiple_of(j * tq, tq)

        def block(i, carry, diag):
            dk_t, dv_t, dc0, dc1 = carry
            dcs = [dc0, dc1]
            lo = pl.multiple_of(i * tq, tq)
            qi = q_ref[pl.ds(lo, tq), :]
            qi = qi * ATT_SCALE
            qt_i = qi.T
            doi = do_ref[pl.ds(lo, tq), :]
            dot_i = doi.astype(bf16).T
            prod = doi.astype(bf16).astype(f32) * o_ref[pl.ds(lo, tq), :]
            lse_i = lse_ref[pl.ds(lo, tq), :]
            dq_i = jnp.zeros((tq, LANES), f32)
            for hh in range(2):
                qh = jnp.where(masks[hh], qi, jnp.zeros((), bf16))
                doh = jnp.where(masks[hh], doi, 0.0).astype(bf16)
                delta = jnp.sum(jnp.where(masks[hh], prod, 0.0), axis=-1, keepdims=True)
                sc = _dot_nt(qh, kj) + (_tile_base(cr_ref, hh, lo) - cr_ref[hh:hh + 1, pl.ds(lo_j, tq)])
                p = jnp.exp(sc - lse_i[:, hh * HEAD_DIM:hh * HEAD_DIM + 1])
                if diag:
                    p = jnp.where(col <= row, p, 0.0)
                dv_t = dv_t + _dot(jnp.where(sub_masks[hh], dot_i, jnp.zeros((), bf16)), p.astype(bf16))
                ds = p * (_dot_nt(doh, vj) - delta)
                dcs[hh] = dcs[hh] + jnp.sum(ds, axis=0, keepdims=True)
                dsb = ds.astype(bf16)
                dq_i = jnp.where(masks[hh], _dot(dsb, kj), dq_i)
                dk_t = dk_t + _dot(jnp.where(sub_masks[hh], qt_i, jnp.zeros((), bf16)), dsb)
            dq_ref[pl.ds(lo, tq), :] += dq_i * ATT_SCALE
            return dk_t, dv_t, dcs[0], dcs[1]

        zero = jnp.zeros((LANES, tq), f32)
        zrow = jnp.zeros((1, tq), f32)
        carry = block(j, (zero, zero, zrow, zrow), True)
        dk_t, dv_t, dc0, dc1 = lax.fori_loop(j + 1, nq, functools.partial(block, diag=False), carry)
        dk_ref[...] = dk_t.T.astype(bf16)
        dv_ref[...] = dv_t.T.astype(bf16)
        dc_ref[0:1, :] = -dc0
        dc_ref[1:2, :] = -dc1

        @pl.when(j == nq - 1)
        def _():
            dqb_ref[...] = dq_ref[...].astype(bf16)

    full = lambda p, j: (0, p)
    tile = lambda p, j: (j, p)
    return pl.pallas_call(
        body, grid=(FOX_PAIRS, nq),
        in_specs=[pl.BlockSpec((s, LANES), full), pl.BlockSpec((tq, LANES), tile), pl.BlockSpec((tq, LANES), lambda p, j: (j, FOX_PAIRS + p)),
                  pl.BlockSpec((s, LANES), full), pl.BlockSpec((s, LANES), full), pl.BlockSpec((None, s, LANES), lambda p, j: (p, 0, 0)),
                  pl.BlockSpec((None, 2, s), lambda p, j: (p, 0, 0))],
        out_specs=[pl.BlockSpec((s, LANES), full), pl.BlockSpec((tq, LANES), tile), pl.BlockSpec((tq, LANES), tile),
                   pl.BlockSpec((None, 2, tq), lambda p, j: (p, 0, j))],
        out_shape=[SDS((s, dq_width), bf16), SDS((s, MAIN_WIDTH), bf16), SDS((s, MAIN_WIDTH), bf16), SDS((FOX_PAIRS, 2, s), f32)],
        scratch_shapes=[pltpu.VMEM((s, LANES), f32)],
        name=name, compiler_params=_cparams("parallel", "arbitrary"))(q, kv, kv, d_mixed, o, lse, c_row)


def adamw(w, g, m, v, name, tr=256):
    r, c = w.shape
    tr = min(tr, r)
    assert r % tr == 0, (name, r, tr)
    c1 = 1.0 / (1.0 - ADAM_B1 ** ADAM_STEP)
    c2 = 1.0 / (1.0 - ADAM_B2 ** ADAM_STEP)

    def body(w_ref, g_ref, m_ref, v_ref, d_ref, mo_ref, vo_ref):
        gv = g_ref[...]
        mn = ADAM_B1 * m_ref[...] + (1.0 - ADAM_B1) * gv
        vn = ADAM_B2 * v_ref[...] + (1.0 - ADAM_B2) * gv * gv
        mo_ref[...] = mn
        vo_ref[...] = vn
        d_ref[...] = -ADAM_LR * ((mn * c1) / (jnp.sqrt(vn * c2) + ADAM_EPS) + ADAM_WD * w_ref[...])

    spec = pl.BlockSpec((tr, c), lambda i: (i, 0))
    return pl.pallas_call(body, grid=(r // tr,), in_specs=[spec] * 4, out_specs=[spec] * 3, out_shape=[SDS((r, c), f32)] * 3,
                          name=name, compiler_params=_cparams("parallel"))(w, g, m, v)


def adamw_owned(w, parts, m, v, name, tr):
    nl, r, c = w.shape
    cp = parts[0].shape[2]
    assert r % tr == 0 and len(parts) == nl, (name, r, tr)
    c1 = 1.0 / (1.0 - ADAM_B1 ** ADAM_STEP)
    c2 = 1.0 / (1.0 - ADAM_B2 ** ADAM_STEP)

    def body(*refs):
        w_ref, p_refs, (m_ref, v_ref) = refs[0], refs[1:1 + nl], refs[1 + nl:3 + nl]
        g_ref, d_ref, mo_ref, vo_ref = refs[3 + nl:]
        layer = pl.program_id(0)

        def total(p_ref):
            acc = p_ref[0].astype(f32)
            for k in range(1, N_DEV):
                acc = acc + p_ref[k].astype(f32)
            return acc

        gv = total(p_refs[0])
        for l in range(1, nl):
            gv = jnp.where(layer == l, total(p_refs[l]), gv)
        gv = gv[:, :c]
        g_ref[...] = gv
        mn = ADAM_B1 * m_ref[...] + (1.0 - ADAM_B1) * gv
        vn = ADAM_B2 * v_ref[...] + (1.0 - ADAM_B2) * gv * gv
        mo_ref[...] = mn
        vo_ref[...] = vn
        d_ref[...] = -ADAM_LR * ((mn * c1) / (jnp.sqrt(vn * c2) + ADAM_EPS) + ADAM_WD * w_ref[...])

    spec = pl.BlockSpec((None, tr, c), lambda l, i: (l, i, 0))
    last = r // tr - 1

    def part_spec(mine):
        return pl.BlockSpec((N_DEV, tr, cp), lambda l, i: (0, jnp.where(l == mine, i, jnp.where(l < mine, 0, last)), 0))

    return pl.pallas_call(body, grid=(nl, r // tr), in_specs=[spec] + [part_spec(l) for l in range(nl)] + [spec, spec], out_specs=[spec] * 4,
                          out_shape=[SDS((nl, r, c), f32)] * 4, name=name,
                          compiler_params=_cparams("parallel", "parallel"))(w, *parts, m, v)


def sum_leading(x, name, out_dtype=f32, tr=None):
    n, r, c = x.shape
    tr = tr or r
    assert r % tr == 0

    def body(x_ref, o_ref):
        acc = x_ref[0].astype(f32)
        for k in range(1, n):
            acc = acc + x_ref[k].astype(f32)
        o_ref[...] = acc.astype(out_dtype)

    return pl.pallas_call(body, grid=(r // tr,), in_specs=[pl.BlockSpec((n, tr, c), lambda i: (0, i, 0))],
                          out_specs=pl.BlockSpec((tr, c), lambda i: (i, 0)), out_shape=SDS((r, c), out_dtype), name=name,
                          compiler_params=_cparams("parallel"))(x)


_ANY = pl.BlockSpec(memory_space=pl.ANY)
_DMA = pltpu.SemaphoreType.DMA


_HBM = pl.BlockSpec(memory_space=pltpu.HBM)
_SEM = pl.BlockSpec(memory_space=pltpu.SEMAPHORE)
_EFFECT = pltpu.SideEffectType.DATAFLOW_SIDE_EFFECTING
_FLIPS = [(0, 0, 1), (1, 0, 0), (0, 1, 0), (1, 1, 0), (1, 0, 1), (0, 1, 1), (1, 1, 1)]


def _me():
    return lax.axis_index("x"), lax.axis_index("y"), lax.axis_index("c")


def _peers():
    mx, my, mc = _me()
    return [(jnp.bitwise_xor(mx, fx), jnp.bitwise_xor(my, fy), jnp.bitwise_xor(mc, fc)) for fx, fy, fc in _FLIPS]


def _index(dev):
    return 4 * dev[0] + 2 * dev[1] + dev[2]


def _win(ref, axis, k, size, count=1):
    idx = [slice(None)] * len(ref.shape)
    idx[axis] = pl.ds(k * size, count * size)
    return ref.at[tuple(idx)]


def _hbm(a):
    return pltpu.with_memory_space_constraint(a, pltpu.HBM)


def _exchange_start(srcs, lands, copies_of, name):
    n = len(srcs)

    def body(*refs):
        src = refs[:n]
        send_sems, recv_sems, self_sems = refs[2 * n:2 * n + 3]
        land = refs[3 * n + 3:4 * n + 3]
        token = refs[4 * n + 3]
        me = _index(_me())
        for a in range(n):
            for s_ref, d_ref, peer in copies_of(a, src[a], land[a], me):
                if peer is None:
                    pltpu.make_async_copy(s_ref, d_ref, self_sems.at[a]).start()
                else:
                    pltpu.make_async_remote_copy(src_ref=s_ref, dst_ref=d_ref, send_sem=send_sems.at[a], recv_sem=recv_sems.at[a],
                                                 device_id=peer, device_id_type=MESH).start()
        token[...] = jnp.zeros_like(token)

    outs = pl.pallas_call(
        body, name=name,
        out_shape=(_DMA((n,)), _DMA((n,)), _DMA((n,)), *[pltpu.HBM(s.shape, s.dtype) for s in srcs],
                   *[pltpu.HBM(l.shape, l.dtype) for l in lands], SDS((8, LANES), f32)),
        in_specs=[_HBM] * (2 * n), out_specs=(_SEM, _SEM, _SEM, *[_HBM] * (2 * n), pl.BlockSpec(memory_space=pltpu.VMEM)),
        input_output_aliases={i: 3 + i for i in range(2 * n)},
        compiler_params=pltpu.CompilerParams(has_side_effects=_EFFECT),
    )(*[_hbm(s) for s in srcs], *[_hbm(lax.empty(l.shape, l.dtype)) for l in lands])
    return dict(sems=outs[:3], srcs=list(outs[3:3 + n]), lands=list(outs[3 + n:3 + 2 * n]), token=outs[3 + 2 * n])


def _exchange_wait(started, waits_of, after, name, which=None):
    which = list(range(len(started["srcs"]))) if which is None else which
    srcs, lands = [started["srcs"][a] for a in which], [started["lands"][a] for a in which]
    n = len(which)

    def body(*refs):
        src = refs[:n]
        land = refs[n:2 * n]
        send_sems, recv_sems, self_sems = refs[2 * n:2 * n + 3]
        me = _index(_me())
        for pos, a in enumerate(which):
            seven, (s_ref, d_ref) = waits_of(a, src[pos], land[pos], me)
            both = pltpu.make_async_remote_copy(src_ref=seven, dst_ref=seven, send_sem=send_sems.at[a], recv_sem=recv_sems.at[a],
                                                device_id=_me(), device_id_type=MESH)
            both.wait_send()
            both.wait_recv()
            pltpu.make_async_copy(s_ref, d_ref, self_sems.at[a]).wait()

    outs = pl.pallas_call(
        body, name=name, out_shape=tuple(pltpu.HBM(t.shape, t.dtype) for t in srcs + lands),
        in_specs=[_HBM] * (2 * n) + [_SEM] * 3 + [_ANY], out_specs=tuple([_HBM] * (2 * n)),
        input_output_aliases={i: i for i in range(2 * n)},
        compiler_params=pltpu.CompilerParams(has_side_effects=_EFFECT),
    )(*srcs, *lands, *started["sems"], after)
    return list(outs[n:])


def gather_start(locs, axes, name):
    lands = [SDS(tuple(N_DEV * d if i == ax else d for i, d in enumerate(l.shape)), l.dtype) for l, ax in zip(locs, axes)]

    def copies_of(a, src, land, me):
        mine = _win(land, axes[a], me, src.shape[axes[a]])
        return [(src, mine, peer) for peer in _peers()] + [(src, mine, None)]

    return _exchange_start(locs, lands, copies_of, name)


def gather_wait(started, axes, after, name, which=None):
    def waits_of(a, src, land, me):
        size = src.shape[axes[a]]
        return _win(land, axes[a], 0, size, N_DEV - 1), (src, _win(land, axes[a], me, size))

    return _exchange_wait(started, waits_of, after, name, which)


def _part(ref, axis, k, stride, used):
    idx = [slice(None)] * len(ref.shape)
    idx[axis] = pl.ds(k * stride, used)
    return ref.at[tuple(idx)]


def scatter_start(grads, axes, name, used=None):
    strides = [g.shape[ax] // N_DEV for g, ax in zip(grads, axes)]
    used = used or strides
    lands = [SDS((N_DEV,) + tuple(u if i == ax else d for i, d in enumerate(g.shape)), g.dtype) for g, ax, u in zip(grads, axes, used)]

    def copies_of(a, src, land, me):
        out = [(_part(src, axes[a], _index(peer), strides[a], used[a]), land.at[me], peer) for peer in _peers()]
        return out + [(_part(src, axes[a], me, strides[a], used[a]), land.at[me], None)]

    return _exchange_start(grads, lands, copies_of, name)


def scatter_wait(started, axes, after, name, used=None):
    def waits_of(a, src, land, me):
        stride = src.shape[axes[a]] // N_DEV
        return land.at[pl.ds(0, N_DEV - 1)], (_part(src, axes[a], me, stride, used[a] if used else stride), land.at[me])

    return _exchange_wait(started, waits_of, after, name)


def _row_tile(rows, cap=512):
    return max(t for t in range(8, min(rows, cap) + 1, 8) if rows % t == 0)


_SMALL = [
    ("ln_mix_pre", (2, 1024)), ("ln_mix_post", (2, 1024)), ("ln_ffn_pre", (2, 1024)), ("ln_ffn_post", (2, 1024)),
    ("ln_mem", (2, 1024)), ("w_spatial", (1, 6, 128, 128)), ("b_spatial", (1, 6, 128)), ("ln_shared", (1024,)),
    ("b_forget", (12,)), ("ln_v_g", (1, 768)), ("ln_v_b", (1, 768)),
]
_SMALL_TILE = 8 * LANES


def _small_rows(shape):
    return -(-math.prod(shape) // _SMALL_TILE) * 8


def _pack_small(vals, shapes):
    parts = []
    for name, shape in shapes:
        flat = vals[name].reshape(-1).astype(f32)
        rows = _small_rows(shape)
        parts.append(jnp.pad(flat, (0, rows * LANES - flat.shape[0])).reshape(rows, LANES))
    return jnp.concatenate(parts, axis=0)


def _unpack_small(buf, shapes):
    out = {}
    lo = 0
    for name, shape in shapes:
        rows = _small_rows(shape)
        out[name] = buf[lo:lo + rows].reshape(-1)[:math.prod(shape)].reshape(shape)
        lo += rows
    return out


def kernel(x, mem, ln_mix_pre, ln_mix_post, ln_ffn_pre, ln_ffn_post, ln_mem, w_mem_kv, w_out, w_ffn_gate, w_ffn_up, w_ffn_down, w_in_a, w_spatial, b_spatial, ln_v_g, ln_v_b, ln_shared, w_shared_kv, b_forget, w_in_b, loss_target, m_ln_mix_pre, m_ln_mix_post, m_ln_ffn_pre, m_ln_ffn_post, m_ln_mem, m_w_mem_kv, m_w_out, m_w_ffn_gate, m_w_ffn_up, m_w_ffn_down, m_w_in_a, m_w_spatial, m_b_spatial, m_ln_v_g, m_ln_v_b, m_ln_shared, m_w_shared_kv, m_b_forget, m_w_in_b, v_ln_mix_pre, v_ln_mix_post, v_ln_ffn_pre, v_ln_ffn_post, v_ln_mem, v_w_mem_kv, v_w_out, v_w_ffn_gate, v_w_ffn_up, v_w_ffn_down, v_w_in_a, v_w_spatial, v_b_spatial, v_ln_v_g, v_ln_v_b, v_ln_shared, v_w_shared_kv, v_b_forget, v_w_in_b):
    weights = dict(ln_mix_pre=ln_mix_pre, ln_mix_post=ln_mix_post, ln_ffn_pre=ln_ffn_pre, ln_ffn_post=ln_ffn_post, ln_mem=ln_mem,
                   w_mem_kv=w_mem_kv, w_out=w_out, w_ffn_gate=w_ffn_gate, w_ffn_up=w_ffn_up, w_ffn_down=w_ffn_down, w_in_a=w_in_a,
                   w_spatial=w_spatial, b_spatial=b_spatial, ln_v_g=ln_v_g, ln_v_b=ln_v_b, ln_shared=ln_shared,
                   w_shared_kv=w_shared_kv, b_forget=b_forget, w_in_b=w_in_b)
    mom_m = dict(ln_mix_pre=m_ln_mix_pre, ln_mix_post=m_ln_mix_post, ln_ffn_pre=m_ln_ffn_pre, ln_ffn_post=m_ln_ffn_post, ln_mem=m_ln_mem,
                 w_mem_kv=m_w_mem_kv, w_out=m_w_out, w_ffn_gate=m_w_ffn_gate, w_ffn_up=m_w_ffn_up, w_ffn_down=m_w_ffn_down, w_in_a=m_w_in_a,
                 w_spatial=m_w_spatial, b_spatial=m_b_spatial, ln_v_g=m_ln_v_g, ln_v_b=m_ln_v_b, ln_shared=m_ln_shared,
                 w_shared_kv=m_w_shared_kv, b_forget=m_b_forget, w_in_b=m_w_in_b)
    mom_v = dict(ln_mix_pre=v_ln_mix_pre, ln_mix_post=v_ln_mix_post, ln_ffn_pre=v_ln_ffn_pre, ln_ffn_post=v_ln_ffn_post, ln_mem=v_ln_mem,
                 w_mem_kv=v_w_mem_kv, w_out=v_w_out, w_ffn_gate=v_w_ffn_gate, w_ffn_up=v_w_ffn_up, w_ffn_down=v_w_ffn_down, w_in_a=v_w_in_a,
                 w_spatial=v_w_spatial, b_spatial=v_b_spatial, ln_v_g=v_ln_v_g, ln_v_b=v_ln_v_b, ln_shared=v_ln_shared,
                 w_shared_kv=v_w_shared_kv, b_forget=v_b_forget, w_in_b=v_w_in_b)
    names = list(weights)
    mx, my, mc = lax.axis_index("x"), lax.axis_index("y"), lax.axis_index("c")
    me = 4 * mx + 2 * my + mc

    h0 = x[0]
    mem0 = mem[0]
    tgt = loss_target[0]
    seq = h0.shape[0]

    vec = lambda a: a.reshape(1, -1)
    pad_to = lambda a, axis, size: jnp.pad(a, [(0, size - a.shape[i] if i == axis else 0) for i in range(a.ndim)])

    def after(tok, a):
        return a + tok[0, 0].astype(a.dtype)

    lnv_loc = pad_to(jnp.concatenate([ln_v_g, ln_v_b], axis=0), 0, 8)
    st_a = gather_start([w_in_a.astype(bf16), pad_to(lnv_loc, 1, LANES)[None]], [0, 0], "gather_a_start")
    mix_locs = lambda l, tok: [after(tok, w_mem_kv[l]).astype(bf16), w_out[l].astype(bf16)]

    def ffn_gather_start(l, tok):
        gate_up = gather_start([pad_to(after(tok, w_ffn_gate[l]).astype(bf16), 1, FF_SHARD_PAD),
                                pad_to(w_ffn_up[l].astype(bf16), 1, FF_SHARD_PAD)], [1, 1], f"gather_gate_up{l}_start")
        down = gather_start([pad_to(after(gate_up["token"], w_ffn_down[l]).astype(bf16), 0, FF_SHARD_PAD)], [0], f"gather_down{l}_start")
        return gate_up, down

    st_b = [gather_start(mix_locs(0, st_a["token"]), [0, 0], "gather_b0_start"), None]
    st_c = ffn_gather_start(0, st_b[0]["token"])
    st_d = gather_start([after(st_c[1]["token"], w_in_b[0]).astype(bf16), pad_to(w_shared_kv.astype(bf16), 1, KV_PAD)], [0, 0],
                        "gather_d_start")
    l1_axes = [0, 0, 1, 1, 0]
    st_l1 = gather_start(mix_locs(1, st_d["token"]) + [pad_to(w_ffn_gate[1].astype(bf16), 1, FF_SHARD_PAD),
                                                       pad_to(w_ffn_up[1].astype(bf16), 1, FF_SHARD_PAD),
                                                       pad_to(w_ffn_down[1].astype(bf16), 0, FF_SHARD_PAD)], l1_axes, "gather_layer1_start")
    ws = w_spatial[0].astype(bf16)
    ws_t = ws.transpose(0, 2, 1)
    bs_t = b_spatial[0].T

    (a0,) = rms_fwd(h0, [after(st_l1["token"], vec(ln_mix_pre[0]))], "a0_norm")
    w_in_a8, lnv8 = gather_wait(st_a, [0, 0], a0, "gather_a_wait")
    w_in_a_full = w_in_a8.transpose(1, 0, 2).reshape(D_MODEL, -1)
    lnv_g = lnv8[:, 0, :MAIN_WIDTH // N_DEV].reshape(1, MAIN_WIDTH)
    lnv_b = lnv8[:, 1, :MAIN_WIDTH // N_DEV].reshape(1, MAIN_WIDTH)
    proj0 = mm(a0, w_in_a_full, "proj0", tn=896)
    main0 = gmlp_fwd(proj0, ws, bs_t, lnv_g, lnv_b, "gmlp_fwd", out_width=D_MODEL)
    w_mkv, w_o = [None, None], [None, None]
    w_mkv[0], w_o[0] = gather_wait(st_b[0], [0, 0], main0, "gather_b0_wait")
    (memn0,) = rms_fwd(mem0, [vec(ln_mem[0])], "mem0_norm")
    kvm0 = mm(memn0, w_mkv[0], "kvm0")
    mixed0 = mem_attn_fwd(proj0, 2 * MAIN_WIDTH // MEM_WIDTH, kvm0, main0, "mem_attn0")
    y1_0, hmid0, f0 = mm_resnorm(mixed0, w_o[0], h0, vec(ln_mix_post[0]), [vec(ln_ffn_pre[0])], "mix_out0")
    w_g0, w_u0 = gather_wait(st_c[0], [1, 1], f0, "gather_gate_up0_wait")
    gu0, act0 = ffn_up(f0, w_g0, w_u0, "ffn_up0")
    (w_d0,) = gather_wait(st_c[1], [0], act0, "gather_down0_wait")
    y2_0, h1, a1, sin1 = mm_resnorm(act0, w_d0, hmid0, vec(ln_ffn_post[0]), [vec(ln_mix_pre[1]), vec(ln_shared)], "ffn_down0")

    w_inb, w_kv = gather_wait(st_d, [0, 0], sin1, "gather_d_wait")
    kvb = mm(sin1, w_kv, "kv_shared", out_dtype=bf16, tn=MAIN_WIDTH, ncols=2 * MAIN_WIDTH)
    zf = mm(sin1, w_kv, "forget_logits", tn=256, col0=2 * MAIN_WIDTH, ncols=256)
    qb = mm(a1, w_inb, "proj1", out_dtype=bf16)
    z_t = jnp.pad(zf[:, :FOX_HEADS].T, ((0, 16 - FOX_HEADS), (0, 0)))
    bf_col = jnp.pad(b_forget, (0, 16 - FOX_HEADS)).reshape(16, 1)
    c_t = fgate_fwd(z_t, bf_col, "fgate_fwd")
    c_row = c_t[:FOX_HEADS].reshape(FOX_PAIRS, 2, seq)
    main1, lse, main1_b = fox_fwd(qb, kvb, c_row, "fox_fwd", out_width=D_MODEL)
    w_mkv[1], w_o[1] = gather_wait(st_l1, l1_axes, main1, "gather_b1_wait", which=[0, 1])
    (memn1,) = rms_fwd(mem0, [vec(ln_mem[1])], "mem1_norm")
    kvm1 = mm(memn1, w_mkv[1], "kvm1")
    mixed1 = mem_attn_fwd(qb, MAIN_WIDTH // MEM_WIDTH, kvm1, main1_b, "mem_attn1")
    y1_1, hmid1, f1 = mm_resnorm(mixed1, w_o[1], h1, vec(ln_mix_post[1]), [vec(ln_ffn_pre[1])], "mix_out1")
    w_g1, w_u1 = gather_wait(st_l1, l1_axes, f1, "gather_gate_up1_wait", which=[2, 3])
    gu1, act1 = ffn_up(f1, w_g1, w_u1, "ffn_up1")
    (w_d1,) = gather_wait(st_l1, l1_axes, act1, "gather_down1_wait", which=[4])
    dh, d_y2_1, dg_fpost1, loss_tile = mm_resnorm_loss(act1, w_d1, hmid1, vec(ln_ffn_post[1]), tgt, "ffn_down1_loss")
    ffn_w = [(w_g0, w_u0, w_d0), (w_g1, w_u1, w_d1)]
    ff_shard = w_ffn_down.shape[1]

    small = {}

    def ffn_backward(layer, dh_out, d_y2, hmid, f, gu, act, y1):
        w_g, w_u, w_d = ffn_w[layer]
        dw_down = mm_tn(act, d_y2, f"dw_down{layer}")
        rs_down = scatter_start([dw_down], [0], f"scatter_down{layer}_start", used=[ff_shard])
        d_g, d_u = ffn_act_grad(d_y2, w_d, gu, f"ffn_act_grad{layer}")
        dw_g = mm_tn(d_g, f, f"dw_gate{layer}", dep=rs_down["token"])
        dw_u = mm_tn(d_u, f, f"dw_up{layer}")
        rs_gate_up = scatter_start([dw_g, dw_u], [0, 0], f"scatter_gate_up{layer}_start", used=[ff_shard] * 2)
        dh_mid, d_y1, dg_fpre, dg_mpost = ffn_in_grad(d_g, d_u, w_g, w_u, hmid, dh_out, after(rs_gate_up["token"], vec(ln_ffn_pre[layer])),
                                                      y1, vec(ln_mix_post[layer]), f"ffn_in_grad{layer}")
        return dh_mid, d_y1, dg_fpre, dg_mpost, (rs_down, rs_gate_up)

    def mix_out_backward(layer, d_y1, mixed):
        dw_out = mm_tn(mixed, d_y1, f"dw_out{layer}")
        d_mixed = mm(d_y1, w_o[layer], f"d_mixed{layer}", trans_b=True)
        return d_mixed, dw_out

    def mem_backward(layer, q_src, q_block, kvm, memn, d_mixed, into):
        d_qm, d_kvm = mem_attn_bwd(q_src, q_block, kvm, d_mixed, into, f"mem_attn_bwd{layer}")
        d_kvm_b = d_kvm.astype(bf16)
        dw_mkv = mm_tn(memn, d_kvm_b, f"dw_mem_kv{layer}")
        d_memn = mm(d_kvm_b, w_mkv[layer], f"d_memn{layer}", trans_b=True)
        _, dg_mem = rms_bwd(mem0, vec(ln_mem[layer]), d_memn, None, bf16, f"mem_norm_bwd{layer}")
        return d_qm, dw_mkv, dg_mem


    dh_mid1, d_y1_1, dg_fpre1, dg_mpost1, rs_ffn1 = ffn_backward(1, dh, d_y2_1, hmid1, f1, gu1, act1, y1_1)
    d_mixed1, dw_out1 = mix_out_backward(1, d_y1_1, mixed1)
    dq_b, dk, dv, dc = fox_bwd(qb, kvb, d_mixed1, main1, lse, c_row, "fox_bwd", dq_width=D_MODEL)
    d_proj1, dw_mkv1, dg_mem1 = mem_backward(1, qb, MAIN_WIDTH // MEM_WIDTH, kvm1, memn1, d_mixed1, dq_b)
    rs_mix1 = scatter_start([dw_out1, dw_mkv1], [0, 0], "scatter_mix1_start")
    dc_t = jnp.pad(dc.reshape(FOX_HEADS, seq), ((0, 16 - FOX_HEADS), (0, 0)))
    dz_t, db_f = fgate_bwd(dc_t, z_t, bf_col, "fgate_bwd")
    d_kvf = jnp.concatenate([dk, dv, jnp.pad(dz_t[:FOX_HEADS].T.astype(bf16), ((0, 0), (0, KV_PAD - KV_WIDTH)))], axis=-1)
    dw_in_b = mm_tn(a1, d_proj1, "dw_in_b", dep=rs_mix1["token"])
    dw_kv = mm_tn(sin1, d_kvf, "dw_kv", tn=896)
    rs_2 = scatter_start([dw_in_b, dw_kv], [0, 0], "scatter_shared_start")
    dh1, (dg_pre1, dg_shared), d_y2_0, dg_fpost0 = proj_in_grad(
        [(d_proj1, w_inb, vec(ln_mix_pre[1])), (d_kvf, w_kv, vec(ln_shared))], h1, dh_mid1, "in_grad1", dep=rs_2["token"],
        below=(y2_0, vec(ln_ffn_post[0])))

    dh_mid0, d_y1_0, dg_fpre0, dg_mpost0, rs_ffn0 = ffn_backward(0, dh1, d_y2_0, hmid0, f0, gu0, act0, y1_0)
    d_mixed0, dw_out0 = mix_out_backward(0, d_y1_0, mixed0)
    d_uv, dw_s, db_s, dg_lnv, db_lnv = gmlp_bwd(proj0, d_mixed0, ws, ws_t, bs_t, lnv_g, lnv_b, "gmlp_bwd", out_width=w_in_a_full.shape[1])
    d_proj0, dw_mkv0, dg_mem0 = mem_backward(0, proj0, 2 * MAIN_WIDTH // MEM_WIDTH, kvm0, memn0, d_mixed0, d_uv)
    rs_mix0 = scatter_start([dw_out0, dw_mkv0], [0, 0], "scatter_mix0_start")

    small["ln_mix_pre"] = jnp.concatenate([jnp.zeros_like(dg_pre1), dg_pre1], axis=0)
    small["ln_mix_post"] = jnp.concatenate([dg_mpost0, dg_mpost1], axis=0)
    small["ln_ffn_pre"] = jnp.concatenate([dg_fpre0, dg_fpre1], axis=0)
    small["ln_ffn_post"] = jnp.concatenate([dg_fpost0, dg_fpost1], axis=0)
    small["ln_mem"] = jnp.concatenate([dg_mem0, dg_mem1], axis=0)
    small["w_spatial"] = dw_s[None]
    small["b_spatial"] = db_s[:, :A_GROUPS].T[None]
    small["ln_shared"] = dg_shared[0]
    small["b_forget"] = db_f[:FOX_HEADS, 0]
    small["ln_v_g"] = dg_lnv
    small["ln_v_b"] = db_lnv
    small_rows = jnp.concatenate([_pack_small(small, _SMALL), after(rs_mix0["token"], loss_tile)], axis=0)
    st_small = gather_start([small_rows[None]], [0], "gather_small_grads_start")
    dw_in_a_t = mm_tn(d_proj0, a0, "dw_in_a", tk=896, dep=st_small["token"])
    rs_in_a = scatter_start([dw_in_a_t], [0], "scatter_in_a_start")
    grad_x, (dg_pre0,) = proj_in_grad([(d_proj0, w_in_a_full, vec(ln_mix_pre[0]))], h0, dh_mid0, "in_grad0", dep=rs_in_a["token"])
    st_last = gather_start([dg_pre0.reshape(1, 8, LANES)], [0], "gather_last_grad_start")

    (p_down1,) = scatter_wait(rs_ffn1[0], [0], after(st_last["token"], grad_x[:8, :LANES]), "scatter_down1_wait", used=[ff_shard])
    p_gate1, p_up1 = scatter_wait(rs_ffn1[1], [0, 0], p_down1, "scatter_gate_up1_wait", used=[ff_shard] * 2)
    p_out1, p_mkv1 = scatter_wait(rs_mix1, [0, 0], p_gate1, "scatter_mix1_wait")
    p_in_b, p_kv = scatter_wait(rs_2, [0, 0], p_out1, "scatter_shared_wait")
    (p_down0,) = scatter_wait(rs_ffn0[0], [0], p_in_b, "scatter_down0_wait", used=[ff_shard])
    p_gate0, p_up0 = scatter_wait(rs_ffn0[1], [0, 0], p_down0, "scatter_gate_up0_wait", used=[ff_shard] * 2)
    p_out0, p_mkv0 = scatter_wait(rs_mix0, [0, 0], p_gate0, "scatter_mix0_wait")
    owned_parts = dict(w_ffn_gate=[p_gate0, p_gate1], w_ffn_up=[p_up0, p_up1], w_ffn_down=[p_down0, p_down1], w_out=[p_out0, p_out1],
                       w_mem_kv=[p_mkv0, p_mkv1], w_in_b=[p_in_b], w_shared_kv=[p_kv])

    grad_w, delta, new_m, new_v = {}, {}, {}, {}
    transposed = ("w_ffn_gate", "w_ffn_up", "w_in_a")

    def adamw_sharded(n, parts):
        shape = weights[n].shape
        three_d = shape if len(shape) == 3 else (1,) + shape
        view = (lambda t: t.reshape(three_d).transpose(0, 2, 1)) if n in transposed else (lambda t: t.reshape(three_d))
        back = (lambda t: t.transpose(0, 2, 1).reshape(shape)) if n in transposed else (lambda t: t.reshape(shape))
        w_view = view(weights[n])
        outs = adamw_owned(w_view, parts, view(mom_m[n]), view(mom_v[n]), f"adamw_{n}", tr=_row_tile(w_view.shape[1]))
        grad_w[n], delta[n], new_m[n], new_v[n] = (back(t) for t in outs)

    for n, parts in owned_parts.items():
        adamw_sharded(n, parts)
    (p_in_a,) = scatter_wait(rs_in_a, [0], delta["w_shared_kv"], "scatter_in_a_wait")
    adamw_sharded("w_in_a", [p_in_a])
    (small_all,) = gather_wait(st_small, [0], p_in_a, "gather_small_grads_wait")
    (last_all,) = gather_wait(st_last, [0], small_all, "gather_last_grad_wait")
    small_sum = sum_leading(small_all, "sum_small_grads")
    loss = small_sum[small_rows.shape[0] - 1, 0]
    g_small = _unpack_small(small_sum, _SMALL)
    g_small["ln_mix_pre"] = jnp.concatenate([sum_leading(last_all, "sum_last_grad").reshape(1, D_MODEL), g_small["ln_mix_pre"][1:]], axis=0)
    shard = MAIN_WIDTH // N_DEV
    for n in ("ln_v_g", "ln_v_b"):
        g_small[n] = lax.dynamic_slice_in_dim(g_small[n], me * shard, shard, axis=1)
    grad_w.update(g_small)
    small_local_shapes = [(n, tuple(weights[n].shape)) for n, _ in _SMALL]
    packed = [_pack_small(src, small_local_shapes) for src in (weights, grad_w, mom_m, mom_v)]
    outs = adamw(*packed, "adamw_small", tr=packed[0].shape[0])
    for dst, buf in zip((delta, new_m, new_v), outs):
        dst.update(_unpack_small(buf, small_local_shapes))

    return (loss, grad_x[None], *[grad_w[n] for n in names], *[delta[n] for n in names],
            *[new_m[n] for n in names], *[new_v[n] for n in names])
```

```python
import functools
import math

import jax
import jax.numpy as jnp
from jax import lax
from jax.experimental import pallas as pl
from jax.experimental.pallas import tpu as pltpu

f32 = jnp.float32
bf16 = jnp.bfloat16
SDS = jax.ShapeDtypeStruct

D_MODEL = 1024
MAIN_WIDTH = 768
MEM_WIDTH = 256
HEAD_DIM = 64
MEM_HEADS = 4
FOX_HEADS = 12
FOX_PAIRS = FOX_HEADS // 2
CHUNK = 128
A_GROUPS = 6
FF_SHARD_PAD = 384
KV_WIDTH = 2 * MAIN_WIDTH + FOX_HEADS
KV_PAD = 1792
RMS_EPS = 1e-6
LN_EPS = 1e-5
ATT_SCALE = HEAD_DIM ** -0.5
ADAM_LR, ADAM_B1, ADAM_B2, ADAM_EPS, ADAM_WD, ADAM_STEP = 0.001, 0.9, 0.999, 1e-08, 0.01, 10
N_DEV = 8
MESH = pl.DeviceIdType.MESH
V7X_VMEM_LIMIT = 56 * 1024 * 1024
LANES = 128


def _cparams(*sem):
    return pltpu.CompilerParams(dimension_semantics=sem or None, vmem_limit_bytes=V7X_VMEM_LIMIT)


def _dot(a, b):
    return jnp.dot(a, b, preferred_element_type=f32)


def _dot_nt(a, b):
    return lax.dot_general(a, b, (((1,), (1,)), ((), ())), preferred_element_type=f32)


def _dot_tn(a, b):
    return lax.dot_general(a, b, (((0,), (0,)), ((), ())), preferred_element_type=f32)


def _gelu(x):
    k = math.sqrt(2.0 / math.pi)
    t = jnp.tanh(k * (x + 0.044715 * x * x * x))
    return 0.5 * x * (1.0 + t), t


def _gelu_grad(x, t):
    k = math.sqrt(2.0 / math.pi)
    return 0.5 * (1.0 + t) + 0.5 * x * (1.0 - t * t) * k * (1.0 + 3.0 * 0.044715 * x * x)


def _sigmoid(x):
    return 1.0 / (1.0 + jnp.exp(-x))


def rms_fwd(x, gains, name, tm=512):
    m, d = x.shape
    tm = min(tm, m)
    n = len(gains)

    def body(x_ref, *refs):
        xv = x_ref[...]
        y = xv * lax.rsqrt(jnp.sum(xv * xv, axis=-1, keepdims=True) * (1.0 / d) + RMS_EPS)
        for g_ref, o_ref in zip(refs[:n], refs[n:]):
            o_ref[...] = (y * g_ref[...]).astype(bf16)

    row = pl.BlockSpec((tm, d), lambda i: (i, 0))
    vec = pl.BlockSpec((1, d), lambda i: (0, 0))
    return pl.pallas_call(body, grid=(m // tm,), in_specs=[row] + [vec] * n, out_specs=[row] * n,
                          out_shape=[SDS((m, d), bf16)] * n, name=name, compiler_params=_cparams("parallel"))(x, *gains)


def rms_bwd(x, g, dy, add, out_dtype, name, tm=512):
    m, d = x.shape
    tm = min(tm, m)
    has_add = add is not None

    def body(x_ref, g_ref, dy_ref, *refs):
        dx_ref, dg_ref = refs[-2], refs[-1]
        xv = x_ref[...]
        dyv = dy_ref[...].astype(f32)
        r = lax.rsqrt(jnp.sum(xv * xv, axis=-1, keepdims=True) * (1.0 / d) + RMS_EPS)
        xn = xv * r
        dyg = dyv * g_ref[...]
        dx = r * (dyg - xn * (jnp.sum(dyg * xn, axis=-1, keepdims=True) * (1.0 / d)))
        if has_add:
            dx = dx + refs[0][...]
        dx_ref[...] = dx.astype(out_dtype)

        @pl.when(pl.program_id(0) == 0)
        def _():
            dg_ref[...] = jnp.zeros_like(dg_ref)

        dg_ref[...] += jnp.sum(dyv * xn, axis=0, keepdims=True)

    row = pl.BlockSpec((tm, d), lambda i: (i, 0))
    vec = pl.BlockSpec((1, d), lambda i: (0, 0))
    ins = [x, g, dy] + ([add] if has_add else [])
    return pl.pallas_call(body, grid=(m // tm,), in_specs=[row, vec, row] + ([row] if has_add else []),
                          out_specs=[row, vec], out_shape=[SDS((m, d), out_dtype), SDS((1, d), f32)], name=name,
                          compiler_params=_cparams("arbitrary"))(*ins)


def mm(a, b, name, trans_b=False, out_dtype=f32, tm=1024, tn=1024, col0=0, ncols=None, dep=None):
    m, k = a.shape
    n_all = b.shape[0] if trans_b else b.shape[1]
    n = n_all if ncols is None else ncols
    tm, tn = min(tm, m), min(tn, n)
    assert m % tm == 0 and n % tn == 0 and col0 % tn == 0 and not (trans_b and col0), (name, m, n, tm, tn)
    jb = col0 // tn

    def body(a_ref, b_ref, *rest):
        r = _dot_nt(a_ref[...], b_ref[...]) if trans_b else _dot(a_ref[...], b_ref[...])
        rest[-1][...] = r.astype(out_dtype)

    if trans_b:
        b_spec = pl.BlockSpec((tn, k), lambda j, i: (j, 0))
    else:
        b_spec = pl.BlockSpec((k, tn), lambda j, i: (0, jb + j))
    deps = [] if dep is None else [dep]
    dep_specs = [pl.BlockSpec((8, LANES), lambda j, i: (0, 0))] * len(deps)
    return pl.pallas_call(body, grid=(n // tn, m // tm), in_specs=[pl.BlockSpec((tm, k), lambda j, i: (i, 0)), b_spec] + dep_specs,
                          out_specs=pl.BlockSpec((tm, tn), lambda j, i: (i, j)), out_shape=SDS((m, n), out_dtype),
                          name=name, compiler_params=_cparams("parallel", "parallel"))(a, b, *deps)


def mm_tn(a, g, name, tk=1024, tn=1024, out_dtype=bf16, dep=None):
    s, k = a.shape
    n = g.shape[1]
    tk, tn = min(tk, k), min(tn, n)
    assert k % tk == 0 and n % tn == 0, (name, k, n, tk, tn)

    def body(a_ref, g_ref, *rest):
        rest[-1][...] = _dot_tn(a_ref[...], g_ref[...]).astype(out_dtype)

    deps = [] if dep is None else [dep]
    dep_specs = [pl.BlockSpec((8, LANES), lambda i, j: (0, 0))] * len(deps)
    return pl.pallas_call(body, grid=(k // tk, n // tn),
                          in_specs=[pl.BlockSpec((s, tk), lambda i, j: (0, i)), pl.BlockSpec((s, tn), lambda i, j: (0, j))] + dep_specs,
                          out_specs=pl.BlockSpec((tk, tn), lambda i, j: (i, j)), out_shape=SDS((k, n), out_dtype), name=name,
                          compiler_params=_cparams("parallel", "parallel"))(a, g, *deps)


def _resident(shape, index_map):
    return pl.BlockSpec(shape, index_map, pipeline_mode=pl.Buffered(1))


def _rms(xv):
    return xv * lax.rsqrt(jnp.sum(xv * xv, axis=-1, keepdims=True) * (1.0 / xv.shape[-1]) + RMS_EPS)


def _rms_bwd_math(xv, g, dy):
    d = xv.shape[-1]
    r = lax.rsqrt(jnp.sum(xv * xv, axis=-1, keepdims=True) * (1.0 / d) + RMS_EPS)
    xn = xv * r
    dyg = dy * g
    dx = r * (dyg - xn * (jnp.sum(dyg * xn, axis=-1, keepdims=True) * (1.0 / d)))
    return dx, jnp.sum(dy * xn, axis=0, keepdims=True)


SUB_ROWS = 512


def mm_resnorm(a, b, h, g_post, gains, name, tm=512):
    m, k = a.shape
    d = b.shape[1]
    n = len(gains)

    def body(a_ref, b_ref, h_ref, gp_ref, *refs):
        for r in range(tm // SUB_ROWS):
            rows = slice(r * SUB_ROWS, (r + 1) * SUB_ROWS)
            y = _dot(a_ref[rows, :], b_ref[...])
            refs[n][rows, :] = y
            hn = h_ref[rows, :] + _rms(y) * gp_ref[...]
            refs[n + 1][rows, :] = hn
            if n:
                z = _rms(hn)
                for g_ref, o_ref in zip(refs[:n], refs[n + 2:]):
                    o_ref[rows, :] = (z * g_ref[...]).astype(bf16)

    row = pl.BlockSpec((tm, d), lambda i: (i, 0))
    vec = pl.BlockSpec((1, d), lambda i: (0, 0))
    return pl.pallas_call(body, grid=(m // tm,),
                          in_specs=[pl.BlockSpec((tm, k), lambda i: (i, 0)), _resident((k, d), lambda i: (0, 0)), row, vec] + [vec] * n,
                          out_specs=[row] * (n + 2), out_shape=[SDS((m, d), f32)] * 2 + [SDS((m, d), bf16)] * n, name=name,
                          compiler_params=_cparams("parallel"))(a, b, h, g_post, *gains)


def mm_resnorm_loss(a, b, h, g_post, tgt, name, tm=512):
    m, k = a.shape
    d = b.shape[1]

    def body(a_ref, b_ref, h_ref, gp_ref, t_ref, dh_ref, dy_ref, dg_ref, l_ref):
        @pl.when(pl.program_id(0) == 0)
        def _():
            dg_ref[...] = jnp.zeros_like(dg_ref)
            l_ref[...] = jnp.zeros_like(l_ref)

        y = _dot(a_ref[...], b_ref[...])
        e = h_ref[...] + _rms(y) * gp_ref[...] - t_ref[...]
        dh = e * (1.0 / d)
        dh_ref[...] = dh
        part = jnp.sum(jnp.sum(e * e, axis=-1, keepdims=True), axis=0, keepdims=True) * (0.5 / d)
        l_ref[...] += jnp.broadcast_to(part, l_ref.shape)
        dy, dg = _rms_bwd_math(y, gp_ref[...], dh)
        dy_ref[...] = dy.astype(bf16)
        dg_ref[...] += dg

    row = pl.BlockSpec((tm, d), lambda i: (i, 0))
    vec = pl.BlockSpec((1, d), lambda i: (0, 0))
    return pl.pallas_call(body, grid=(m // tm,),
                          in_specs=[pl.BlockSpec((tm, k), lambda i: (i, 0)), _resident((k, d), lambda i: (0, 0)), row, vec, row],
                          out_specs=[row, row, vec, pl.BlockSpec((8, LANES), lambda i: (0, 0))],
                          out_shape=[SDS((m, d), f32), SDS((m, d), bf16), SDS((1, d), f32), SDS((8, LANES), f32)], name=name,
                          compiler_params=_cparams("arbitrary"))(a, b, h, g_post, tgt)


FFN_TILE = 1536


def ffn_act_grad(d_y2, w_d, factors, name, tm=1024):
    s, d = d_y2.shape
    ff = w_d.shape[0]
    tn = FFN_TILE
    nb = ff // tn

    def body(a_ref, b_ref, g_ref, u_ref, dg_ref, du_ref):
        av = a_ref[...]
        tc = 256
        for c in range(tn // tc):
            cols = slice(c * tc, (c + 1) * tc)
            da = _dot_nt(av, b_ref[cols, :])
            dg_ref[:, cols] = (da * g_ref[:, cols].astype(f32)).astype(bf16)
            du_ref[:, cols] = (da * u_ref[:, cols].astype(f32)).astype(bf16)

    tile = pl.BlockSpec((tm, tn), lambda j, i: (i, j))
    return pl.pallas_call(body, grid=(nb, s // tm),
                          in_specs=[pl.BlockSpec((tm, d), lambda j, i: (i, 0)), pl.BlockSpec((tn, d), lambda j, i: (j, 0)),
                                    pl.BlockSpec((tm, tn), lambda j, i: (i, 2 * j)), pl.BlockSpec((tm, tn), lambda j, i: (i, 2 * j + 1))],
                          out_specs=[tile, tile], out_shape=[SDS((s, ff), bf16)] * 2, name=name,
                          compiler_params=_cparams("parallel", "parallel"))(d_y2, w_d, factors, factors)


def ffn_in_grad(d_g, d_u, w_g, w_u, hmid, dh_out, g_pre, y1, g_post, name, tm=512):
    s, ff = d_g.shape
    d = w_g.shape[0]

    def body(dg_ref, du_ref, wg_ref, wu_ref, hm_ref, dho_ref, gpre_ref, y1_ref, gpost_ref, dhm_ref, dy1_ref, dgpre_ref, dgpost_ref):
        @pl.when(pl.program_id(0) == 0)
        def _():
            dgpre_ref[...] = jnp.zeros_like(dgpre_ref)
            dgpost_ref[...] = jnp.zeros_like(dgpost_ref)

        for r in range(tm // SUB_ROWS):
            rows = slice(r * SUB_ROWS, (r + 1) * SUB_ROWS)
            d_f = _dot_nt(dg_ref[rows, :], wg_ref[...]) + _dot_nt(du_ref[rows, :], wu_ref[...])
            dx, dg1 = _rms_bwd_math(hm_ref[rows, :], gpre_ref[...], d_f)
            dh_mid = dho_ref[rows, :] + dx
            dhm_ref[rows, :] = dh_mid
            dgpre_ref[...] += dg1
            dy1, dg2 = _rms_bwd_math(y1_ref[rows, :], gpost_ref[...], dh_mid)
            dy1_ref[rows, :] = dy1.astype(bf16)
            dgpost_ref[...] += dg2

    row = pl.BlockSpec((tm, d), lambda i: (i, 0))
    vec = pl.BlockSpec((1, d), lambda i: (0, 0))
    wide = pl.BlockSpec((tm, ff), lambda i: (i, 0))
    w_spec = _resident((d, ff), lambda i: (0, 0))
    return pl.pallas_call(body, grid=(s // tm,), in_specs=[wide, wide, w_spec, w_spec, row, row, vec, row, vec],
                          out_specs=[row, row, vec, vec], out_shape=[SDS((s, d), f32), SDS((s, d), bf16), SDS((1, d), f32), SDS((1, d), f32)],
                          name=name, compiler_params=_cparams("arbitrary"))(d_g, d_u, w_g, w_u, hmid, dh_out, g_pre, y1, g_post)


def proj_in_grad(pairs, x, add, name, tm=512, dep=None, below=None):
    s, d = x.shape
    n = len(pairs)
    extra = [] if dep is None else [dep]
    n_below = 0 if below is None else 2

    def body(*refs):
        x_ref, add_ref = refs[3 * n], refs[3 * n + 1]
        below_refs = refs[3 * n + 2:3 * n + 2 + n_below]
        outs = refs[3 * n + 2 + n_below + len(extra):]

        @pl.when(pl.program_id(0) == 0)
        def _():
            for o in outs[1:1 + n] + outs[2 + n:]:
                o[...] = jnp.zeros_like(o)

        xv = x_ref[...]
        dx = add_ref[...]
        for i in range(n):
            a_ref, b_ref, g_ref = refs[3 * i:3 * i + 3]
            dxi, dgi = _rms_bwd_math(xv, g_ref[...], _dot(a_ref[...], b_ref[...]))
            dx = dx + dxi
            outs[1 + i][...] += dgi
        outs[0][...] = dx
        if below is not None:
            dy, dg = _rms_bwd_math(below_refs[0][...], below_refs[1][...], dx)
            outs[1 + n][...] = dy.astype(bf16)
            outs[2 + n][...] += dg

    row = pl.BlockSpec((tm, d), lambda i: (i, 0))
    vec = pl.BlockSpec((1, d), lambda i: (0, 0))
    in_specs, args = [], []
    for a, b, g in pairs:
        k = a.shape[1]
        in_specs += [pl.BlockSpec((tm, k), lambda i: (i, 0)), _resident((k, d), lambda i: (0, 0)), vec]
        args += [a, b, g]
    in_specs += [row, row] + [row, vec][:n_below] + [pl.BlockSpec((8, LANES), lambda i: (0, 0))] * len(extra)
    out_specs = [row] + [vec] * n + [row, vec][:n_below]
    out_shape = [SDS((s, d), f32)] + [SDS((1, d), f32)] * n + [SDS((s, d), bf16), SDS((1, d), f32)][:n_below]
    out = pl.pallas_call(body, grid=(s // tm,), in_specs=in_specs, out_specs=out_specs, out_shape=out_shape, name=name,
                         compiler_params=_cparams("arbitrary"))(*args, x, add, *(below or ()), *extra)
    return (out[0], out[1:1 + n]) + tuple(out[1 + n:])


def ffn_up(f, wg, wu, name, tm=1024, tc=256):
    s, d = f.shape
    ff = wg.shape[-1]
    tn = FFN_TILE

    def body(f_ref, wg_ref, wu_ref, fac_ref, act_ref):
        fv = f_ref[...]
        for c in range(tn // tc):
            lo = c * tc
            gg = _dot(fv, wg_ref[:, lo:lo + tc])
            uu = _dot(fv, wu_ref[:, lo:lo + tc])
            sg = _sigmoid(gg)
            silu = gg * sg
            fac_ref[:, lo:lo + tc] = (uu * (sg + silu * (1.0 - sg))).astype(bf16)
            fac_ref[:, tn + lo:tn + lo + tc] = silu.astype(bf16)
            act_ref[:, lo:lo + tc] = (silu * uu).astype(bf16)

    w_spec = pl.BlockSpec((d, tn), lambda j, i: (0, j))
    return pl.pallas_call(body, grid=(ff // tn, s // tm), in_specs=[pl.BlockSpec((tm, d), lambda j, i: (i, 0)), w_spec, w_spec],
                          out_specs=[pl.BlockSpec((tm, 2 * tn), lambda j, i: (i, j)), pl.BlockSpec((tm, tn), lambda j, i: (i, j))],
                          out_shape=[SDS((s, 2 * ff), bf16), SDS((s, ff), bf16)], name=name,
                          compiler_params=_cparams("parallel", "parallel"))(f, wg, wu)


def _gmlp_forward_chunk(u, v, w_refs, bias, ln_g, ln_b):
    gu, tu = _gelu(u)
    gv, tv = _gelu(v)
    mu = jnp.sum(gv, axis=-1, keepdims=True) * (1.0 / MAIN_WIDTH)
    xc = gv - mu
    rstd = lax.rsqrt(jnp.sum(xc * xc, axis=-1, keepdims=True) * (1.0 / MAIN_WIDTH) + LN_EPS)
    xhat = xc * rstd
    vln = xhat * ln_g + ln_b
    row = lax.broadcasted_iota(jnp.int32, (CHUNK, CHUNK), 0)
    col = lax.broadcasted_iota(jnp.int32, (CHUNK, CHUNK), 1)
    s_parts = []
    for g in range(A_GROUPS):
        w = jnp.where(col <= row, w_refs[g], jnp.zeros((), bf16))
        s_parts.append(_dot(w, vln[:, g * CHUNK:(g + 1) * CHUNK].astype(bf16)) + bias[:, g:g + 1])
    return gu, tu, tv, rstd, xhat, vln, s_parts


def gmlp_fwd(proj, ws, bs_t, ln_g, ln_b, name, tm=512, out_width=MAIN_WIDTH):
    s = proj.shape[0]

    def body(u_ref, v_ref, w_ref, b_ref, g_ref, bb_ref, o_ref):
        bias = b_ref[...]
        for c in range(tm // CHUNK):
            rows = slice(c * CHUNK, (c + 1) * CHUNK)
            gu, _, _, _, _, _, s_parts = _gmlp_forward_chunk(u_ref[rows, :], v_ref[rows, :], w_ref, bias, g_ref[...], bb_ref[...])
            for g in range(A_GROUPS):
                cols = slice(g * CHUNK, (g + 1) * CHUNK)
                o_ref[rows, cols] = (gu[:, cols] * s_parts[g]).astype(bf16)

    vec = pl.BlockSpec((1, MAIN_WIDTH), lambda i: (0, 0))
    return pl.pallas_call(
        body, grid=(s // tm,),
        in_specs=[pl.BlockSpec((tm, MAIN_WIDTH), lambda i: (i, 0)), pl.BlockSpec((tm, MAIN_WIDTH), lambda i: (i, 1)),
                  pl.BlockSpec((A_GROUPS, CHUNK, CHUNK), lambda i: (0, 0, 0)), pl.BlockSpec((CHUNK, A_GROUPS), lambda i: (0, 0)), vec, vec],
        out_specs=pl.BlockSpec((tm, MAIN_WIDTH), lambda i: (i, 0)), out_shape=SDS((s, out_width), bf16), name=name,
        compiler_params=_cparams("parallel"))(proj, proj, ws, bs_t, ln_g, ln_b)


def gmlp_bwd(proj, d_mixed, ws, ws_t, bs_t, ln_g, ln_b, name, tm=512, out_width=2 * MAIN_WIDTH):
    s = proj.shape[0]

    def body(u_ref, v_ref, dm_ref, w_ref, wt_ref, b_ref, g_ref, bb_ref, duv_ref, dw_ref, db_ref, dg_ref, dbb_ref):
        @pl.when(pl.program_id(0) == 0)
        def _():
            dw_ref[...] = jnp.zeros_like(dw_ref)
            db_ref[...] = jnp.zeros_like(db_ref)
            dg_ref[...] = jnp.zeros_like(dg_ref)
            dbb_ref[...] = jnp.zeros_like(dbb_ref)

        bias = b_ref[...]
        ln_gv = g_ref[...]
        row = lax.broadcasted_iota(jnp.int32, (CHUNK, CHUNK), 0)
        col = lax.broadcasted_iota(jnp.int32, (CHUNK, CHUNK), 1)
        lane = lax.broadcasted_iota(jnp.int32, (CHUNK, LANES), 1)
        for c in range(tm // CHUNK):
            rows = slice(c * CHUNK, (c + 1) * CHUNK)
            u = u_ref[rows, :]
            v = v_ref[rows, :]
            gu, tu, tv, rstd, xhat, vln, s_parts = _gmlp_forward_chunk(u, v, w_ref, bias, ln_gv, bb_ref[...])
            dm = dm_ref[rows, :]
            d_vln_parts = []
            d_gu_parts = []
            db_acc = jnp.zeros((CHUNK, LANES), f32)
            for g in range(A_GROUPS):
                cols = slice(g * CHUNK, (g + 1) * CHUNK)
                dmg = dm[:, cols]
                d_gu_parts.append(dmg * s_parts[g])
                d_s = dmg * gu[:, cols]
                db_acc = db_acc + jnp.where(lane == g, jnp.sum(d_s, axis=-1, keepdims=True), 0.0)
                d_sb = d_s.astype(bf16)
                dw_ref[g] += jnp.where(col <= row, _dot_nt(d_sb, vln[:, cols].astype(bf16)), 0.0)
                wt = jnp.where(row <= col, wt_ref[g], jnp.zeros((), bf16))
                d_vln_parts.append(_dot(wt, d_sb))
            db_ref[...] += db_acc
            d_vln = jnp.concatenate(d_vln_parts, axis=-1)
            d_gu = jnp.concatenate(d_gu_parts, axis=-1)
            dg_ref[...] += jnp.sum(d_vln * xhat, axis=0, keepdims=True)
            dbb_ref[...] += jnp.sum(d_vln, axis=0, keepdims=True)
            dxh = d_vln * ln_gv
            m1 = jnp.sum(dxh, axis=-1, keepdims=True) * (1.0 / MAIN_WIDTH)
            m2 = jnp.sum(dxh * xhat, axis=-1, keepdims=True) * (1.0 / MAIN_WIDTH)
            d_gv = rstd * (dxh - m1 - xhat * m2)
            duv_ref[rows, :MAIN_WIDTH] = (d_gu * _gelu_grad(u, tu)).astype(bf16)
            duv_ref[rows, MAIN_WIDTH:] = (d_gv * _gelu_grad(v, tv)).astype(bf16)

    vec = pl.BlockSpec((1, MAIN_WIDTH), lambda i: (0, 0))
    wspec = pl.BlockSpec((A_GROUPS, CHUNK, CHUNK), lambda i: (0, 0, 0))
    return pl.pallas_call(
        body, grid=(s // tm,),
        in_specs=[pl.BlockSpec((tm, MAIN_WIDTH), lambda i: (i, 0)), pl.BlockSpec((tm, MAIN_WIDTH), lambda i: (i, 1)),
                  pl.BlockSpec((tm, MAIN_WIDTH), lambda i: (i, 0)), wspec, wspec, pl.BlockSpec((CHUNK, A_GROUPS), lambda i: (0, 0)), vec, vec],
        out_specs=[pl.BlockSpec((tm, 2 * MAIN_WIDTH), lambda i: (i, 0)), wspec, pl.BlockSpec((CHUNK, LANES), lambda i: (0, 0)), vec, vec],
        out_shape=[SDS((s, out_width), bf16), SDS((A_GROUPS, CHUNK, CHUNK), f32), SDS((CHUNK, LANES), f32),
                   SDS((1, MAIN_WIDTH), f32), SDS((1, MAIN_WIDTH), f32)],
        name=name, compiler_params=_cparams("arbitrary"))(proj, proj, d_mixed, ws, ws_t, bs_t, ln_g, ln_b)


def _head_mask(width, h):
    lane = lax.broadcasted_iota(jnp.int32, (1, width), 1)
    return (lane >= h * HEAD_DIM) & (lane < (h + 1) * HEAD_DIM)


def mem_attn_fwd(proj, q_block, kv, into, name, tm=512):
    s = proj.shape[0]
    n_mem = kv.shape[0]
    out_block = into.shape[1] // MEM_WIDTH - 1

    def body(q_ref, kv_ref, into_ref, o_ref):
        q = q_ref[...].astype(f32)
        k = kv_ref[:, :MEM_WIDTH].astype(bf16)
        v = kv_ref[:, MEM_WIDTH:].astype(bf16)
        out = jnp.zeros((tm, MEM_WIDTH), f32)
        for h in range(MEM_HEADS):
            msk = _head_mask(MEM_WIDTH, h)
            qh = jnp.where(msk, q, 0.0).astype(bf16)
            sc = _dot_nt(qh, k) * ATT_SCALE
            e = jnp.exp(sc - jnp.max(sc, axis=-1, keepdims=True))
            p = e / jnp.sum(e, axis=-1, keepdims=True)
            out = jnp.where(msk, _dot(p.astype(bf16), v), out)
        o_ref[...] = out.astype(bf16)

    return pl.pallas_call(body, grid=(s // tm,),
                          in_specs=[pl.BlockSpec((tm, MEM_WIDTH), lambda i: (i, q_block)), pl.BlockSpec((n_mem, 2 * MEM_WIDTH), lambda i: (0, 0)), _ANY],
                          out_specs=pl.BlockSpec((tm, MEM_WIDTH), lambda i: (i, out_block)), out_shape=SDS(into.shape, bf16), name=name,
                          input_output_aliases={2: 0}, compiler_params=_cparams("parallel"))(proj, kv, into)


def mem_attn_bwd(proj, q_block, kv, d_mixed, into, name, tm=512):
    s = proj.shape[0]
    n_mem = kv.shape[0]
    out_block = into.shape[1] // MEM_WIDTH - 1

    def body(q_ref, kv_ref, do_ref, into_ref, dq_ref, dkv_ref):
        @pl.when(pl.program_id(0) == 0)
        def _():
            dkv_ref[...] = jnp.zeros_like(dkv_ref)

        q = q_ref[...].astype(f32)
        do = do_ref[...]
        k = kv_ref[:, :MEM_WIDTH].astype(bf16)
        v = kv_ref[:, MEM_WIDTH:].astype(bf16)
        dq = jnp.zeros((tm, MEM_WIDTH), f32)
        dk = jnp.zeros((n_mem, MEM_WIDTH), f32)
        dv = jnp.zeros((n_mem, MEM_WIDTH), f32)
        for h in range(MEM_HEADS):
            msk = _head_mask(MEM_WIDTH, h)
            qh = jnp.where(msk, q, 0.0).astype(bf16)
            doh = jnp.where(msk, do, 0.0).astype(bf16)
            sc = _dot_nt(qh, k) * ATT_SCALE
            e = jnp.exp(sc - jnp.max(sc, axis=-1, keepdims=True))
            p = e / jnp.sum(e, axis=-1, keepdims=True)
            dp = _dot_nt(doh, v)
            ds = p * (dp - jnp.sum(dp * p, axis=-1, keepdims=True))
            dsb = (ds * ATT_SCALE).astype(bf16)
            dq = jnp.where(msk, _dot(dsb, k), dq)
            dk = dk + _dot_tn(dsb, qh)
            dv = dv + _dot_tn(p.astype(bf16), doh)
        dq_ref[...] = dq.astype(bf16)
        dkv_ref[:, :MEM_WIDTH] += dk
        dkv_ref[:, MEM_WIDTH:] += dv

    return pl.pallas_call(
        body, grid=(s // tm,),
        in_specs=[pl.BlockSpec((tm, MEM_WIDTH), lambda i: (i, q_block)), pl.BlockSpec((n_mem, 2 * MEM_WIDTH), lambda i: (0, 0)),
                  pl.BlockSpec((tm, MEM_WIDTH), lambda i: (i, MAIN_WIDTH // MEM_WIDTH)), _ANY],
        out_specs=[pl.BlockSpec((tm, MEM_WIDTH), lambda i: (i, out_block)), pl.BlockSpec((n_mem, 2 * MEM_WIDTH), lambda i: (0, 0))],
        out_shape=[SDS(into.shape, bf16), SDS((n_mem, 2 * MEM_WIDTH), f32)], name=name,
        input_output_aliases={3: 0}, compiler_params=_cparams("arbitrary"))(proj, kv, d_mixed, into)


def _tri(t, upper):
    r = lax.broadcasted_iota(jnp.int32, (t, t), 0)
    c = lax.broadcasted_iota(jnp.int32, (t, t), 1)
    return ((r <= c) if upper else (r >= c)).astype(f32)


def fgate_fwd(z_t, b, name, t=512):
    hh, s = z_t.shape

    def body(z_ref, b_ref, c_ref):
        u = _tri(t, True)
        carry = jnp.zeros((hh, 1), f32)
        for blk in range(s // t):
            x = z_ref[:, blk * t:(blk + 1) * t] + b_ref[...]
            logf = jnp.minimum(x, 0.0) - jnp.log(1.0 + jnp.exp(-jnp.abs(x)))
            y = jnp.dot(logf, u, precision=lax.Precision.HIGHEST, preferred_element_type=f32) + carry
            c_ref[:, blk * t:(blk + 1) * t] = y
            carry = y[:, t - 1:t]

    return pl.pallas_call(body, out_shape=SDS((hh, s), f32), name=name, compiler_params=_cparams())(z_t, b)


def fgate_bwd(dc_t, z_t, b, name, t=512):
    hh, s = z_t.shape

    def body(dc_ref, z_ref, b_ref, dz_ref, db_ref):
        low = _tri(t, False)
        carry = jnp.zeros((hh, 1), f32)
        total = jnp.zeros((hh, 1), f32)
        for blk in reversed(range(s // t)):
            cols = slice(blk * t, (blk + 1) * t)
            y = jnp.dot(dc_ref[:, cols], low, precision=lax.Precision.HIGHEST, preferred_element_type=f32) + carry
            carry = y[:, 0:1]
            dz = y * _sigmoid(-(z_ref[:, cols] + b_ref[...]))
            dz_ref[:, cols] = dz
            total = total + jnp.sum(dz, axis=-1, keepdims=True)
        db_ref[...] = jnp.broadcast_to(total, db_ref.shape)

    return pl.pallas_call(body, out_shape=[SDS((hh, s), f32), SDS((hh, LANES), f32)], name=name,
                          compiler_params=_cparams())(dc_t, z_t, b)


def _pair_masks():
    lane = lax.broadcasted_iota(jnp.int32, (1, LANES), 1)
    return [lane < HEAD_DIM, lane >= HEAD_DIM]


def _tile_base(cr_ref, hh, lo):
    return cr_ref[hh:hh + 1, pl.ds(lo, LANES)][:, 0:1]


def fox_fwd(q, kv, c_row, name, tq=512, out_width=MAIN_WIDTH):
    s = kv.shape[0]
    nq = s // tq

    def body(q_ref, k_ref, v_ref, cr_ref, o_ref, lse_ref, ob_ref):
        i = pl.program_id(1)
        qv = q_ref[...]
        masks = _pair_masks()
        row = lax.broadcasted_iota(jnp.int32, (tq, tq), 0)
        col = lax.broadcasted_iota(jnp.int32, (tq, tq), 1)
        qh = [jnp.where(masks[hh], qv, jnp.zeros((), bf16)) * ATT_SCALE for hh in range(2)]
        ct = [_tile_base(cr_ref, hh, pl.multiple_of(i * tq, tq)) for hh in range(2)]

        def block(j, carry, diag):
            lo = pl.multiple_of(j * tq, tq)
            ks = k_ref[pl.ds(lo, tq), :]
            vs = v_ref[pl.ds(lo, tq), :]
            out = []
            for hh in range(2):
                m, l, acc = carry[hh]
                sc = _dot_nt(qh[hh], ks) + (ct[hh] - cr_ref[hh:hh + 1, pl.ds(lo, tq)])
                if diag:
                    sc = jnp.where(col <= row, sc, -jnp.inf)
                m_new = jnp.maximum(m, jnp.max(sc, axis=-1, keepdims=True))
                alpha = jnp.exp(m - m_new)
                p = jnp.exp(sc - m_new)
                l = alpha * l + jnp.sum(p, axis=-1, keepdims=True)
                p_hi = p.astype(bf16)
                p_lo = (p - p_hi.astype(f32)).astype(bf16)
                acc = alpha * acc + (_dot(p_hi, vs) + _dot(p_lo, vs))
                out.append((m_new, l, acc))
            return tuple(out)

        init = (jnp.full((tq, 1), -jnp.inf, f32), jnp.zeros((tq, 1), f32), jnp.zeros((tq, LANES), f32))
        carry = lax.fori_loop(0, i, functools.partial(block, diag=False), (init, init))
        res = [(acc / l, m + jnp.log(l)) for m, l, acc in block(i, carry, True)]
        out = jnp.where(masks[0], res[0][0], res[1][0])
        o_ref[...] = out
        ob_ref[...] = out.astype(bf16)
        lse_ref[...] = jnp.where(masks[0], res[0][1], res[1][1])

    return pl.pallas_call(
        body, grid=(FOX_PAIRS, nq),
        in_specs=[pl.BlockSpec((tq, LANES), lambda p, i: (i, p)), pl.BlockSpec((s, LANES), lambda p, i: (0, p)),
                  pl.BlockSpec((s, LANES), lambda p, i: (0, FOX_PAIRS + p)), pl.BlockSpec((None, 2, s), lambda p, i: (p, 0, 0))],
        out_specs=[pl.BlockSpec((tq, LANES), lambda p, i: (i, p)), pl.BlockSpec((None, tq, LANES), lambda p, i: (p, i, 0)),
                   pl.BlockSpec((tq, LANES), lambda p, i: (i, p))],
        out_shape=[SDS((s, MAIN_WIDTH), f32), SDS((FOX_PAIRS, s, LANES), f32), SDS((s, out_width), bf16)], name=name,
        compiler_params=_cparams("parallel", "parallel"))(q, kv, kv, c_row)


def fox_bwd(q, kv, d_mixed, o, lse, c_row, name, tq=512, dq_width=MAIN_WIDTH):
    s = kv.shape[0]
    nq = s // tq

    def body(q_ref, k_ref, v_ref, do_ref, o_ref, lse_ref, cr_ref, dqb_ref, dk_ref, dv_ref, dc_ref, dq_ref):
        j = pl.program_id(1)

        @pl.when(j == 0)
        def _():
            dq_ref[...] = jnp.zeros_like(dq_ref)

        masks = _pair_masks()
        sub = lax.broadcasted_iota(jnp.int32, (LANES, 1), 0)
        sub_masks = [sub < HEAD_DIM, sub >= HEAD_DIM]
        row = lax.broadcasted_iota(jnp.int32, (tq, tq), 0)
        col = lax.broadcasted_iota(jnp.int32, (tq, tq), 1)
        kj = k_ref[...]
        vj = v_ref[...]
        lo_j = pl.multiple_of(j * tq, tq)

        def block(i, carry, diag):
            dk_t, dv_t, dc0, dc1 = carry
            dcs = [dc0, dc1]
            lo = pl.multiple_of(i * tq, tq)
            qi = q_ref[pl.ds(lo, tq), :]
            qi = qi * ATT_SCALE
            qt_i = qi.T
            doi = do_ref[pl.ds(lo, tq), :]
            dot_i = doi.astype(bf16).T
            prod = doi.astype(bf16).astype(f32) * o_ref[pl.ds(lo, tq), :]
            lse_i = lse_ref[pl.ds(lo, tq), :]
            dq_i = jnp.zeros((tq, LANES), f32)
            for hh in range(2):
                qh = jnp.where(masks[hh], qi, jnp.zeros((), bf16))
                doh = jnp.where(masks[hh], doi, 0.0).astype(bf16)
                delta = jnp.sum(jnp.where(masks[hh], prod, 0.0), axis=-1, keepdims=True)
                sc = _dot_nt(qh, kj) + (_tile_base(cr_ref, hh, lo) - cr_ref[hh:hh + 1, pl.ds(lo_j, tq)])
                p = jnp.exp(sc - lse_i[:, hh * HEAD_DIM:hh * HEAD_DIM + 1])
                if diag:
                    p = jnp.where(col <= row, p, 0.0)
                dv_t = dv_t + _dot(jnp.where(sub_masks[hh], dot_i, jnp.zeros((), bf16)), p.astype(bf16))
                ds = p * (_dot_nt(doh, vj) - delta)
                dcs[hh] = dcs[hh] + jnp.sum(ds, axis=0, keepdims=True)
                dsb = ds.astype(bf16)
                dq_i = jnp.where(masks[hh], _dot(dsb, kj), dq_i)
                dk_t = dk_t + _dot(jnp.where(sub_masks[hh], qt_i, jnp.zeros((), bf16)), dsb)
            dq_ref[pl.ds(lo, tq), :] += dq_i * ATT_SCALE
            return dk_t, dv_t, dcs[0], dcs[1]

        zero = jnp.zeros((LANES, tq), f32)
        zrow = jnp.zeros((1, tq), f32)
        carry = block(j, (zero, zero, zrow, zrow), True)
        dk_t, dv_t, dc0, dc1 = lax.fori_loop(j + 1, nq, functools.partial(block, diag=False), carry)
        dk_ref[...] = dk_t.T.astype(bf16)
        dv_ref[...] = dv_t.T.astype(bf16)
        dc_ref[0:1, :] = -dc0
        dc_ref[1:2, :] = -dc1

        @pl.when(j == nq - 1)
        def _():
            dqb_ref[...] = dq_ref[...].astype(bf16)

    full = lambda p, j: (0, p)
    tile = lambda p, j: (j, p)
    return pl.pallas_call(
        body, grid=(FOX_PAIRS, nq),
        in_specs=[pl.BlockSpec((s, LANES), full), pl.BlockSpec((tq, LANES), tile), pl.BlockSpec((tq, LANES), lambda p, j: (j, FOX_PAIRS + p)),
                  pl.BlockSpec((s, LANES), full), pl.BlockSpec((s, LANES), full), pl.BlockSpec((None, s, LANES), lambda p, j: (p, 0, 0)),
                  pl.BlockSpec((None, 2, s), lambda p, j: (p, 0, 0))],
        out_specs=[pl.BlockSpec((s, LANES), full), pl.BlockSpec((tq, LANES), tile), pl.BlockSpec((tq, LANES), tile),
                   pl.BlockSpec((None, 2, tq), lambda p, j: (p, 0, j))],
        out_shape=[SDS((s, dq_width), bf16), SDS((s, MAIN_WIDTH), bf16), SDS((s, MAIN_WIDTH), bf16), SDS((FOX_PAIRS, 2, s), f32)],
        scratch_shapes=[pltpu.VMEM((s, LANES), f32)],
        name=name, compiler_params=_cparams("parallel", "arbitrary"))(q, kv, kv, d_mixed, o, lse, c_row)


def adamw(w, g, m, v, name, tr=256):
    r, c = w.shape
    tr = min(tr, r)
    assert r % tr == 0, (name, r, tr)
    c1 = 1.0 / (1.0 - ADAM_B1 ** ADAM_STEP)
    c2 = 1.0 / (1.0 - ADAM_B2 ** ADAM_STEP)

    def body(w_ref, g_ref, m_ref, v_ref, d_ref, mo_ref, vo_ref):
        gv = g_ref[...]
        mn = ADAM_B1 * m_ref[...] + (1.0 - ADAM_B1) * gv
        vn = ADAM_B2 * v_ref[...] + (1.0 - ADAM_B2) * gv * gv
        mo_ref[...] = mn
        vo_ref[...] = vn
        d_ref[...] = -ADAM_LR * ((mn * c1) / (jnp.sqrt(vn * c2) + ADAM_EPS) + ADAM_WD * w_ref[...])

    spec = pl.BlockSpec((tr, c), lambda i: (i, 0))
    return pl.pallas_call(body, grid=(r // tr,), in_specs=[spec] * 4, out_specs=[spec] * 3, out_shape=[SDS((r, c), f32)] * 3,
                          name=name, compiler_params=_cparams("parallel"))(w, g, m, v)


def adamw_owned(w, parts, m, v, name, tr):
    nl, r, c = w.shape
    cp = parts[0].shape[2]
    assert r % tr == 0 and len(parts) == nl, (name, r, tr)
    c1 = 1.0 / (1.0 - ADAM_B1 ** ADAM_STEP)
    c2 = 1.0 / (1.0 - ADAM_B2 ** ADAM_STEP)

    def body(*refs):
        w_ref, p_refs, (m_ref, v_ref) = refs[0], refs[1:1 + nl], refs[1 + nl:3 + nl]
        g_ref, d_ref, mo_ref, vo_ref = refs[3 + nl:]
        layer = pl.program_id(0)

        def total(p_ref):
            acc = p_ref[0].astype(f32)
            for k in range(1, N_DEV):
                acc = acc + p_ref[k].astype(f32)
            return acc

        gv = total(p_refs[0])
        for l in range(1, nl):
            gv = jnp.where(layer == l, total(p_refs[l]), gv)
        gv = gv[:, :c]
        g_ref[...] = gv
        mn = ADAM_B1 * m_ref[...] + (1.0 - ADAM_B1) * gv
        vn = ADAM_B2 * v_ref[...] + (1.0 - ADAM_B2) * gv * gv
        mo_ref[...] = mn
        vo_ref[...] = vn
        d_ref[...] = -ADAM_LR * ((mn * c1) / (jnp.sqrt(vn * c2) + ADAM_EPS) + ADAM_WD * w_ref[...])

    spec = pl.BlockSpec((None, tr, c), lambda l, i: (l, i, 0))
    last = r // tr - 1

    def part_spec(mine):
        return pl.BlockSpec((N_DEV, tr, cp), lambda l, i: (0, jnp.where(l == mine, i, jnp.where(l < mine, 0, last)), 0))

    return pl.pallas_call(body, grid=(nl, r // tr), in_specs=[spec] + [part_spec(l) for l in range(nl)] + [spec, spec], out_specs=[spec] * 4,
                          out_shape=[SDS((nl, r, c), f32)] * 4, name=name,
                          compiler_params=_cparams("parallel", "parallel"))(w, *parts, m, v)


def sum_leading(x, name, out_dtype=f32, tr=None):
    n, r, c = x.shape
    tr = tr or r
    assert r % tr == 0

    def body(x_ref, o_ref):
        acc = x_ref[0].astype(f32)
        for k in range(1, n):
            acc = acc + x_ref[k].astype(f32)
        o_ref[...] = acc.astype(out_dtype)

    return pl.pallas_call(body, grid=(r // tr,), in_specs=[pl.BlockSpec((n, tr, c), lambda i: (0, i, 0))],
                          out_specs=pl.BlockSpec((tr, c), lambda i: (i, 0)), out_shape=SDS((r, c), out_dtype), name=name,
                          compiler_params=_cparams("parallel"))(x)


_ANY = pl.BlockSpec(memory_space=pl.ANY)
_DMA = pltpu.SemaphoreType.DMA


_HBM = pl.BlockSpec(memory_space=pltpu.HBM)
_SEM = pl.BlockSpec(memory_space=pltpu.SEMAPHORE)
_EFFECT = pltpu.SideEffectType.DATAFLOW_SIDE_EFFECTING
_FLIPS = [(0, 0, 1), (1, 0, 0), (0, 1, 0), (1, 1, 0), (1, 0, 1), (0, 1, 1), (1, 1, 1)]


def _me():
    return lax.axis_index("x"), lax.axis_index("y"), lax.axis_index("c")


def _peers():
    mx, my, mc = _me()
    return [(jnp.bitwise_xor(mx, fx), jnp.bitwise_xor(my, fy), jnp.bitwise_xor(mc, fc)) for fx, fy, fc in _FLIPS]


def _index(dev):
    return 4 * dev[0] + 2 * dev[1] + dev[2]


def _win(ref, axis, k, size, count=1):
    idx = [slice(None)] * len(ref.shape)
    idx[axis] = pl.ds(k * size, count * size)
    return ref.at[tuple(idx)]


def _hbm(a):
    return pltpu.with_memory_space_constraint(a, pltpu.HBM)


def _exchange_start(srcs, lands, copies_of, name):
    n = len(srcs)

    def body(*refs):
        src = refs[:n]
        send_sems, recv_sems, self_sems = refs[2 * n:2 * n + 3]
        land = refs[3 * n + 3:4 * n + 3]
        token = refs[4 * n + 3]
        me = _index(_me())
        for a in range(n):
            for s_ref, d_ref, peer in copies_of(a, src[a], land[a], me):
                if peer is None:
                    pltpu.make_async_copy(s_ref, d_ref, self_sems.at[a]).start()
                else:
                    pltpu.make_async_remote_copy(src_ref=s_ref, dst_ref=d_ref, send_sem=send_sems.at[a], recv_sem=recv_sems.at[a],
                                                 device_id=peer, device_id_type=MESH).start()
        token[...] = jnp.zeros_like(token)

    outs = pl.pallas_call(
        body, name=name,
        out_shape=(_DMA((n,)), _DMA((n,)), _DMA((n,)), *[pltpu.HBM(s.shape, s.dtype) for s in srcs],
                   *[pltpu.HBM(l.shape, l.dtype) for l in lands], SDS((8, LANES), f32)),
        in_specs=[_HBM] * (2 * n), out_specs=(_SEM, _SEM, _SEM, *[_HBM] * (2 * n), pl.BlockSpec(memory_space=pltpu.VMEM)),
        input_output_aliases={i: 3 + i for i in range(2 * n)},
        compiler_params=pltpu.CompilerParams(has_side_effects=_EFFECT),
    )(*[_hbm(s) for s in srcs], *[_hbm(lax.empty(l.shape, l.dtype)) for l in lands])
    return dict(sems=outs[:3], srcs=list(outs[3:3 + n]), lands=list(outs[3 + n:3 + 2 * n]), token=outs[3 + 2 * n])


def _exchange_wait(started, waits_of, after, name, which=None):
    which = list(range(len(started["srcs"]))) if which is None else which
    srcs, lands = [started["srcs"][a] for a in which], [started["lands"][a] for a in which]
    n = len(which)

    def body(*refs):
        src = refs[:n]
        land = refs[n:2 * n]
        send_sems, recv_sems, self_sems = refs[2 * n:2 * n + 3]
        me = _index(_me())
        for pos, a in enumerate(which):
            seven, (s_ref, d_ref) = waits_of(a, src[pos], land[pos], me)
            both = pltpu.make_async_remote_copy(src_ref=seven, dst_ref=seven, send_sem=send_sems.at[a], recv_sem=recv_sems.at[a],
                                                device_id=_me(), device_id_type=MESH)
            both.wait_send()
            both.wait_recv()
            pltpu.make_async_copy(s_ref, d_ref, self_sems.at[a]).wait()

    outs = pl.pallas_call(
        body, name=name, out_shape=tuple(pltpu.HBM(t.shape, t.dtype) for t in srcs + lands),
        in_specs=[_HBM] * (2 * n) + [_SEM] * 3 + [_ANY], out_specs=tuple([_HBM] * (2 * n)),
        input_output_aliases={i: i for i in range(2 * n)},
        compiler_params=pltpu.CompilerParams(has_side_effects=_EFFECT),
    )(*srcs, *lands, *started["sems"], after)
    return list(outs[n:])


def gather_start(locs, axes, name):
    lands = [SDS(tuple(N_DEV * d if i == ax else d for i, d in enumerate(l.shape)), l.dtype) for l, ax in zip(locs, axes)]

    def copies_of(a, src, land, me):
        mine = _win(land, axes[a], me, src.shape[axes[a]])
        return [(src, mine, peer) for peer in _peers()] + [(src, mine, None)]

    return _exchange_start(locs, lands, copies_of, name)


def gather_wait(started, axes, after, name, which=None):
    def waits_of(a, src, land, me):
        size = src.shape[axes[a]]
        return _win(land, axes[a], 0, size, N_DEV - 1), (src, _win(land, axes[a], me, size))

    return _exchange_wait(started, waits_of, after, name, which)


def _part(ref, axis, k, stride, used):
    idx = [slice(None)] * len(ref.shape)
    idx[axis] = pl.ds(k * stride, used)
    return ref.at[tuple(idx)]


def scatter_start(grads, axes, name, used=None):
    strides = [g.shape[ax] // N_DEV for g, ax in zip(grads, axes)]
    used = used or strides
    lands = [SDS((N_DEV,) + tuple(u if i == ax else d for i, d in enumerate(g.shape)), g.dtype) for g, ax, u in zip(grads, axes, used)]

    def copies_of(a, src, land, me):
        out = [(_part(src, axes[a], _index(peer), strides[a], used[a]), land.at[me], peer) for peer in _peers()]
        return out + [(_part(src, axes[a], me, strides[a], used[a]), land.at[me], None)]

    return _exchange_start(grads, lands, copies_of, name)


def scatter_wait(started, axes, after, name, used=None):
    def waits_of(a, src, land, me):
        stride = src.shape[axes[a]] // N_DEV
        return land.at[pl.ds(0, N_DEV - 1)], (_part(src, axes[a], me, stride, used[a] if used else stride), land.at[me])

    return _exchange_wait(started, waits_of, after, name)


def _row_tile(rows, cap=512):
    return max(t for t in range(8, min(rows, cap) + 1, 8) if rows % t == 0)


_SMALL = [
    ("ln_mix_pre", (2, 1024)), ("ln_mix_post", (2, 1024)), ("ln_ffn_pre", (2, 1024)), ("ln_ffn_post", (2, 1024)),
    ("ln_mem", (2, 1024)), ("w_spatial", (1, 6, 128, 128)), ("b_spatial", (1, 6, 128)), ("ln_shared", (1024,)),
    ("b_forget", (12,)), ("ln_v_g", (1, 768)), ("ln_v_b", (1, 768)),
]
_SMALL_TILE = 8 * LANES


def _small_rows(shape):
    return -(-math.prod(shape) // _SMALL_TILE) * 8


def _pack_small(vals, shapes):
    parts = []
    for name, shape in shapes:
        flat = vals[name].reshape(-1).astype(f32)
        rows = _small_rows(shape)
        parts.append(jnp.pad(flat, (0, rows * LANES - flat.shape[0])).reshape(rows, LANES))
    return jnp.concatenate(parts, axis=0)


def _unpack_small(buf, shapes):
    out = {}
    lo = 0
    for name, shape in shapes:
        rows = _small_rows(shape)
        out[name] = buf[lo:lo + rows].reshape(-1)[:math.prod(shape)].reshape(shape)
        lo += rows
    return out


def kernel(x, mem, ln_mix_pre, ln_mix_post, ln_ffn_pre, ln_ffn_post, ln_mem, w_mem_kv, w_out, w_ffn_gate, w_ffn_up, w_ffn_down, w_in_a, w_spatial, b_spatial, ln_v_g, ln_v_b, ln_shared, w_shared_kv, b_forget, w_in_b, loss_target, m_ln_mix_pre, m_ln_mix_post, m_ln_ffn_pre, m_ln_ffn_post, m_ln_mem, m_w_mem_kv, m_w_out, m_w_ffn_gate, m_w_ffn_up, m_w_ffn_down, m_w_in_a, m_w_spatial, m_b_spatial, m_ln_v_g, m_ln_v_b, m_ln_shared, m_w_shared_kv, m_b_forget, m_w_in_b, v_ln_mix_pre, v_ln_mix_post, v_ln_ffn_pre, v_ln_ffn_post, v_ln_mem, v_w_mem_kv, v_w_out, v_w_ffn_gate, v_w_ffn_up, v_w_ffn_down, v_w_in_a, v_w_spatial, v_b_spatial, v_ln_v_g, v_ln_v_b, v_ln_shared, v_w_shared_kv, v_b_forget, v_w_in_b):
    weights = dict(ln_mix_pre=ln_mix_pre, ln_mix_post=ln_mix_post, ln_ffn_pre=ln_ffn_pre, ln_ffn_post=ln_ffn_post, ln_mem=ln_mem,
                   w_mem_kv=w_mem_kv, w_out=w_out, w_ffn_gate=w_ffn_gate, w_ffn_up=w_ffn_up, w_ffn_down=w_ffn_down, w_in_a=w_in_a,
                   w_spatial=w_spatial, b_spatial=b_spatial, ln_v_g=ln_v_g, ln_v_b=ln_v_b, ln_shared=ln_shared,
                   w_shared_kv=w_shared_kv, b_forget=b_forget, w_in_b=w_in_b)
    mom_m = dict(ln_mix_pre=m_ln_mix_pre, ln_mix_post=m_ln_mix_post, ln_ffn_pre=m_ln_ffn_pre, ln_ffn_post=m_ln_ffn_post, ln_mem=m_ln_mem,
                 w_mem_kv=m_w_mem_kv, w_out=m_w_out, w_ffn_gate=m_w_ffn_gate, w_ffn_up=m_w_ffn_up, w_ffn_down=m_w_ffn_down, w_in_a=m_w_in_a,
                 w_spatial=m_w_spatial, b_spatial=m_b_spatial, ln_v_g=m_ln_v_g, ln_v_b=m_ln_v_b, ln_shared=m_ln_shared,
                 w_shared_kv=m_w_shared_kv, b_forget=m_b_forget, w_in_b=m_w_in_b)
    mom_v = dict(ln_mix_pre=v_ln_mix_pre, ln_mix_post=v_ln_mix_post, ln_ffn_pre=v_ln_ffn_pre, ln_ffn_post=v_ln_ffn_post, ln_mem=v_ln_mem,
                 w_mem_kv=v_w_mem_kv, w_out=v_w_out, w_ffn_gate=v_w_ffn_gate, w_ffn_up=v_w_ffn_up, w_ffn_down=v_w_ffn_down, w_in_a=v_w_in_a,
                 w_spatial=v_w_spatial, b_spatial=v_b_spatial, ln_v_g=v_ln_v_g, ln_v_b=v_ln_v_b, ln_shared=v_ln_shared,
                 w_shared_kv=v_w_shared_kv, b_forget=v_b_forget, w_in_b=v_w_in_b)
    names = list(weights)
    mx, my, mc = lax.axis_index("x"), lax.axis_index("y"), lax.axis_index("c")
    me = 4 * mx + 2 * my + mc

    h0 = x[0]
    mem0 = mem[0]
    tgt = loss_target[0]
    seq = h0.shape[0]

    vec = lambda a: a.reshape(1, -1)
    pad_to = lambda a, axis, size: jnp.pad(a, [(0, size - a.shape[i] if i == axis else 0) for i in range(a.ndim)])

    def after(tok, a):
        return a + tok[0, 0].astype(a.dtype)

    lnv_loc = pad_to(jnp.concatenate([ln_v_g, ln_v_b], axis=0), 0, 8)
    st_a = gather_start([w_in_a.astype(bf16), pad_to(lnv_loc, 1, LANES)[None]], [0, 0], "gather_a_start")
    mix_locs = lambda l, tok: [after(tok, w_mem_kv[l]).astype(bf16), w_out[l].astype(bf16)]

    def ffn_gather_start(l, tok):
        gate_up = gather_start([pad_to(after(tok, w_ffn_gate[l]).astype(bf16), 1, FF_SHARD_PAD),
                                pad_to(w_ffn_up[l].astype(bf16), 1, FF_SHARD_PAD)], [1, 1], f"gather_gate_up{l}_start")
        down = gather_start([pad_to(after(gate_up["token"], w_ffn_down[l]).astype(bf16), 0, FF_SHARD_PAD)], [0], f"gather_down{l}_start")
        return gate_up, down

    st_b = [gather_start(mix_locs(0, st_a["token"]), [0, 0], "gather_b0_start"), None]
    st_c = ffn_gather_start(0, st_b[0]["token"])
    st_d = gather_start([after(st_c[1]["token"], w_in_b[0]).astype(bf16), pad_to(w_shared_kv.astype(bf16), 1, KV_PAD)], [0, 0],
                        "gather_d_start")
    st_b[1] = gather_start(mix_locs(1, st_d["token"]), [0, 0], "gather_b1_start")
    st_e = ffn_gather_start(1, st_b[1]["token"])
    ws = w_spatial[0].astype(bf16)
    ws_t = ws.transpose(0, 2, 1)
    bs_t = b_spatial[0].T

    (a0,) = rms_fwd(h0, [after(st_e[1]["token"], vec(ln_mix_pre[0]))], "a0_norm")
    w_in_a8, lnv8 = gather_wait(st_a, [0, 0], a0, "gather_a_wait")
    w_in_a_full = w_in_a8.transpose(1, 0, 2).reshape(D_MODEL, -1)
    lnv_g = lnv8[:, 0, :MAIN_WIDTH // N_DEV].reshape(1, MAIN_WIDTH)
    lnv_b = lnv8[:, 1, :MAIN_WIDTH // N_DEV].reshape(1, MAIN_WIDTH)
    proj0 = mm(a0, w_in_a_full, "proj0", tn=896)
    main0 = gmlp_fwd(proj0, ws, bs_t, lnv_g, lnv_b, "gmlp_fwd", out_width=D_MODEL)
    w_mkv, w_o = [None, None], [None, None]
    w_mkv[0], w_o[0] = gather_wait(st_b[0], [0, 0], main0, "gather_b0_wait")
    (memn0,) = rms_fwd(mem0, [vec(ln_mem[0])], "mem0_norm")
    kvm0 = mm(memn0, w_mkv[0], "kvm0")
    mixed0 = mem_attn_fwd(proj0, 2 * MAIN_WIDTH // MEM_WIDTH, kvm0, main0, "mem_attn0")
    y1_0, hmid0, f0 = mm_resnorm(mixed0, w_o[0], h0, vec(ln_mix_post[0]), [vec(ln_ffn_pre[0])], "mix_out0")
    w_g0, w_u0 = gather_wait(st_c[0], [1, 1], f0, "gather_gate_up0_wait")
    gu0, act0 = ffn_up(f0, w_g0, w_u0, "ffn_up0")
    (w_d0,) = gather_wait(st_c[1], [0], act0, "gather_down0_wait")
    y2_0, h1, a1, sin1 = mm_resnorm(act0, w_d0, hmid0, vec(ln_ffn_post[0]), [vec(ln_mix_pre[1]), vec(ln_shared)], "ffn_down0")

    w_inb, w_kv = gather_wait(st_d, [0, 0], sin1, "gather_d_wait")
    kvb = mm(sin1, w_kv, "kv_shared", out_dtype=bf16, tn=MAIN_WIDTH, ncols=2 * MAIN_WIDTH)
    zf = mm(sin1, w_kv, "forget_logits", tn=256, col0=2 * MAIN_WIDTH, ncols=256)
    qb = mm(a1, w_inb, "proj1", out_dtype=bf16)
    z_t = jnp.pad(zf[:, :FOX_HEADS].T, ((0, 16 - FOX_HEADS), (0, 0)))
    bf_col = jnp.pad(b_forget, (0, 16 - FOX_HEADS)).reshape(16, 1)
    c_t = fgate_fwd(z_t, bf_col, "fgate_fwd")
    c_row = c_t[:FOX_HEADS].reshape(FOX_PAIRS, 2, seq)
    main1, lse, main1_b = fox_fwd(qb, kvb, c_row, "fox_fwd", out_width=D_MODEL)
    w_mkv[1], w_o[1] = gather_wait(st_b[1], [0, 0], main1, "gather_b1_wait")
    (memn1,) = rms_fwd(mem0, [vec(ln_mem[1])], "mem1_norm")
    kvm1 = mm(memn1, w_mkv[1], "kvm1")
    mixed1 = mem_attn_fwd(qb, MAIN_WIDTH // MEM_WIDTH, kvm1, main1_b, "mem_attn1")
    y1_1, hmid1, f1 = mm_resnorm(mixed1, w_o[1], h1, vec(ln_mix_post[1]), [vec(ln_ffn_pre[1])], "mix_out1")
    w_g1, w_u1 = gather_wait(st_e[0], [1, 1], f1, "gather_gate_up1_wait")
    gu1, act1 = ffn_up(f1, w_g1, w_u1, "ffn_up1")
    (w_d1,) = gather_wait(st_e[1], [0], act1, "gather_down1_wait")
    dh, d_y2_1, dg_fpost1, loss_tile = mm_resnorm_loss(act1, w_d1, hmid1, vec(ln_ffn_post[1]), tgt, "ffn_down1_loss")
    ffn_w = [(w_g0, w_u0, w_d0), (w_g1, w_u1, w_d1)]
    ff_shard = w_ffn_down.shape[1]

    small = {}

    def ffn_backward(layer, dh_out, d_y2, hmid, f, gu, act, y1):
        w_g, w_u, w_d = ffn_w[layer]
        dw_down = mm_tn(act, d_y2, f"dw_down{layer}")
        rs_down = scatter_start([dw_down], [0], f"scatter_down{layer}_start", used=[ff_shard])
        d_g, d_u = ffn_act_grad(d_y2, w_d, gu, f"ffn_act_grad{layer}")
        dw_g = mm_tn(d_g, f, f"dw_gate{layer}", dep=rs_down["token"])
        dw_u = mm_tn(d_u, f, f"dw_up{layer}")
        rs_gate_up = scatter_start([dw_g, dw_u], [0, 0], f"scatter_gate_up{layer}_start", used=[ff_shard] * 2)
        dh_mid, d_y1, dg_fpre, dg_mpost = ffn_in_grad(d_g, d_u, w_g, w_u, hmid, dh_out, after(rs_gate_up["token"], vec(ln_ffn_pre[layer])),
                                                      y1, vec(ln_mix_post[layer]), f"ffn_in_grad{layer}")
        return dh_mid, d_y1, dg_fpre, dg_mpost, (rs_down, rs_gate_up)

    def mix_out_backward(layer, d_y1, mixed):
        dw_out = mm_tn(mixed, d_y1, f"dw_out{layer}")
        d_mixed = mm(d_y1, w_o[layer], f"d_mixed{layer}", trans_b=True)
        return d_mixed, dw_out

    def mem_backward(layer, q_src, q_block, kvm, memn, d_mixed, into):
        d_qm, d_kvm = mem_attn_bwd(q_src, q_block, kvm, d_mixed, into, f"mem_attn_bwd{layer}")
        d_kvm_b = d_kvm.astype(bf16)
        dw_mkv = mm_tn(memn, d_kvm_b, f"dw_mem_kv{layer}")
        d_memn = mm(d_kvm_b, w_mkv[layer], f"d_memn{layer}", trans_b=True)
        _, dg_mem = rms_bwd(mem0, vec(ln_mem[layer]), d_memn, None, bf16, f"mem_norm_bwd{layer}")
        return d_qm, dw_mkv, dg_mem


    dh_mid1, d_y1_1, dg_fpre1, dg_mpost1, rs_ffn1 = ffn_backward(1, dh, d_y2_1, hmid1, f1, gu1, act1, y1_1)
    d_mixed1, dw_out1 = mix_out_backward(1, d_y1_1, mixed1)
    dq_b, dk, dv, dc = fox_bwd(qb, kvb, d_mixed1, main1, lse, c_row, "fox_bwd", dq_width=D_MODEL)
    d_proj1, dw_mkv1, dg_mem1 = mem_backward(1, qb, MAIN_WIDTH // MEM_WIDTH, kvm1, memn1, d_mixed1, dq_b)
    rs_mix1 = scatter_start([dw_out1, dw_mkv1], [0, 0], "scatter_mix1_start")
    dc_t = jnp.pad(dc.reshape(FOX_HEADS, seq), ((0, 16 - FOX_HEADS), (0, 0)))
    dz_t, db_f = fgate_bwd(dc_t, z_t, bf_col, "fgate_bwd")
    d_kvf = jnp.concatenate([dk, dv, jnp.pad(dz_t[:FOX_HEADS].T.astype(bf16), ((0, 0), (0, KV_PAD - KV_WIDTH)))], axis=-1)
    dw_in_b = mm_tn(a1, d_proj1, "dw_in_b", dep=rs_mix1["token"])
    dw_kv = mm_tn(sin1, d_kvf, "dw_kv", tn=896)
    rs_2 = scatter_start([dw_in_b, dw_kv], [0, 0], "scatter_shared_start")
    dh1, (dg_pre1, dg_shared), d_y2_0, dg_fpost0 = proj_in_grad(
        [(d_proj1, w_inb.T, vec(ln_mix_pre[1])), (d_kvf, w_kv.T, vec(ln_shared))], h1, dh_mid1, "in_grad1", dep=rs_2["token"],
        below=(y2_0, vec(ln_ffn_post[0])))

    dh_mid0, d_y1_0, dg_fpre0, dg_mpost0, rs_ffn0 = ffn_backward(0, dh1, d_y2_0, hmid0, f0, gu0, act0, y1_0)
    d_mixed0, dw_out0 = mix_out_backward(0, d_y1_0, mixed0)
    d_uv, dw_s, db_s, dg_lnv, db_lnv = gmlp_bwd(proj0, d_mixed0, ws, ws_t, bs_t, lnv_g, lnv_b, "gmlp_bwd", out_width=w_in_a_full.shape[1])
    d_proj0, dw_mkv0, dg_mem0 = mem_backward(0, proj0, 2 * MAIN_WIDTH // MEM_WIDTH, kvm0, memn0, d_mixed0, d_uv)
    rs_mix0 = scatter_start([dw_out0, dw_mkv0], [0, 0], "scatter_mix0_start")

    small["ln_mix_pre"] = jnp.concatenate([jnp.zeros_like(dg_pre1), dg_pre1], axis=0)
    small["ln_mix_post"] = jnp.concatenate([dg_mpost0, dg_mpost1], axis=0)
    small["ln_ffn_pre"] = jnp.concatenate([dg_fpre0, dg_fpre1], axis=0)
    small["ln_ffn_post"] = jnp.concatenate([dg_fpost0, dg_fpost1], axis=0)
    small["ln_mem"] = jnp.concatenate([dg_mem0, dg_mem1], axis=0)
    small["w_spatial"] = dw_s[None]
    small["b_spatial"] = db_s[:, :A_GROUPS].T[None]
    small["ln_shared"] = dg_shared[0]
    small["b_forget"] = db_f[:FOX_HEADS, 0]
    small["ln_v_g"] = dg_lnv
    small["ln_v_b"] = db_lnv
    small_rows = jnp.concatenate([_pack_small(small, _SMALL), after(rs_mix0["token"], loss_tile)], axis=0)
    st_small = gather_start([small_rows[None]], [0], "gather_small_grads_start")
    dw_in_a_t = mm_tn(d_proj0, a0, "dw_in_a", tk=896, dep=st_small["token"])
    rs_in_a = scatter_start([dw_in_a_t], [0], "scatter_in_a_start")
    grad_x, (dg_pre0,) = proj_in_grad([(d_proj0, w_in_a_full.T, vec(ln_mix_pre[0]))], h0, dh_mid0, "in_grad0", dep=rs_in_a["token"])
    st_last = gather_start([dg_pre0.reshape(1, 8, LANES)], [0], "gather_last_grad_start")

    (p_down1,) = scatter_wait(rs_ffn1[0], [0], after(st_last["token"], grad_x[:8, :LANES]), "scatter_down1_wait", used=[ff_shard])
    p_gate1, p_up1 = scatter_wait(rs_ffn1[1], [0, 0], p_down1, "scatter_gate_up1_wait", used=[ff_shard] * 2)
    p_out1, p_mkv1 = scatter_wait(rs_mix1, [0, 0], p_gate1, "scatter_mix1_wait")
    p_in_b, p_kv = scatter_wait(rs_2, [0, 0], p_out1, "scatter_shared_wait")
    (p_down0,) = scatter_wait(rs_ffn0[0], [0], p_in_b, "scatter_down0_wait", used=[ff_shard])
    p_gate0, p_up0 = scatter_wait(rs_ffn0[1], [0, 0], p_down0, "scatter_gate_up0_wait", used=[ff_shard] * 2)
    p_out0, p_mkv0 = scatter_wait(rs_mix0, [0, 0], p_gate0, "scatter_mix0_wait")
    owned_parts = dict(w_ffn_gate=[p_gate0, p_gate1], w_ffn_up=[p_up0, p_up1], w_ffn_down=[p_down0, p_down1], w_out=[p_out0, p_out1],
                       w_mem_kv=[p_mkv0, p_mkv1], w_in_b=[p_in_b], w_shared_kv=[p_kv])

    grad_w, delta, new_m, new_v = {}, {}, {}, {}
    transposed = ("w_ffn_gate", "w_ffn_up", "w_in_a")

    def adamw_sharded(n, parts):
        shape = weights[n].shape
        three_d = shape if len(shape) == 3 else (1,) + shape
        view = (lambda t: t.reshape(three_d).transpose(0, 2, 1)) if n in transposed else (lambda t: t.reshape(three_d))
        back = (lambda t: t.transpose(0, 2, 1).reshape(shape)) if n in transposed else (lambda t: t.reshape(shape))
        w_view = view(weights[n])
        outs = adamw_owned(w_view, parts, view(mom_m[n]), view(mom_v[n]), f"adamw_{n}", tr=_row_tile(w_view.shape[1]))
        grad_w[n], delta[n], new_m[n], new_v[n] = (back(t) for t in outs)

    for n, parts in owned_parts.items():
        adamw_sharded(n, parts)
    (p_in_a,) = scatter_wait(rs_in_a, [0], delta["w_shared_kv"], "scatter_in_a_wait")
    adamw_sharded("w_in_a", [p_in_a])
    (small_all,) = gather_wait(st_small, [0], p_in_a, "gather_small_grads_wait")
    (last_all,) = gather_wait(st_last, [0], small_all, "gather_last_grad_wait")
    small_sum = sum_leading(small_all, "sum_small_grads")
    loss = small_sum[small_rows.shape[0] - 1, 0]
    g_small = _unpack_small(small_sum, _SMALL)
    g_small["ln_mix_pre"] = jnp.concatenate([sum_leading(last_all, "sum_last_grad").reshape(1, D_MODEL), g_small["ln_mix_pre"][1:]], axis=0)
    shard = MAIN_WIDTH // N_DEV
    for n in ("ln_v_g", "ln_v_b"):
        g_small[n] = lax.dynamic_slice_in_dim(g_small[n], me * shard, shard, axis=1)
    grad_w.update(g_small)
    small_local_shapes = [(n, tuple(weights[n].shape)) for n, _ in _SMALL]
    packed = [_pack_small(src, small_local_shapes) for src in (weights, grad_w, mom_m, mom_v)]
    outs = adamw(*packed, "adamw_small", tr=packed[0].shape[0])
    for dst, buf in zip((delta, new_m, new_v), outs):
        dst.update(_unpack_small(buf, small_local_shapes))

    return (loss, grad_x[None], *[grad_w[n] for n in names], *[delta[n] for n in names],
            *[new_m[n] for n in names], *[new_v[n] for n in names])
```

```python
import functools
import math

import jax
import jax.numpy as jnp
from jax import lax
from jax.experimental import pallas as pl
from jax.experimental.pallas import tpu as pltpu

f32 = jnp.float32
bf16 = jnp.bfloat16
SDS = jax.ShapeDtypeStruct

D_MODEL = 1024
MAIN_WIDTH = 768
MEM_WIDTH = 256
HEAD_DIM = 64
MEM_HEADS = 4
FOX_HEADS = 12
FOX_PAIRS = FOX_HEADS // 2
CHUNK = 128
A_GROUPS = 6
FF_SHARD_PAD = 384
KV_WIDTH = 2 * MAIN_WIDTH + FOX_HEADS
KV_PAD = 1792
RMS_EPS = 1e-6
LN_EPS = 1e-5
ATT_SCALE = HEAD_DIM ** -0.5
ADAM_LR, ADAM_B1, ADAM_B2, ADAM_EPS, ADAM_WD, ADAM_STEP = 0.001, 0.9, 0.999, 1e-08, 0.01, 10
N_DEV = 8
MESH = pl.DeviceIdType.MESH
V7X_VMEM_LIMIT = 56 * 1024 * 1024
LANES = 128


def _cparams(*sem):
    return pltpu.CompilerParams(dimension_semantics=sem or None, vmem_limit_bytes=V7X_VMEM_LIMIT)


def _dot(a, b):
    return jnp.dot(a, b, preferred_element_type=f32)


def _dot_nt(a, b):
    return lax.dot_general(a, b, (((1,), (1,)), ((), ())), preferred_element_type=f32)


def _dot_tn(a, b):
    return lax.dot_general(a, b, (((0,), (0,)), ((), ())), preferred_element_type=f32)


def _gelu(x):
    k = math.sqrt(2.0 / math.pi)
    t = jnp.tanh(k * (x + 0.044715 * x * x * x))
    return 0.5 * x * (1.0 + t), t


def _gelu_grad(x, t):
    k = math.sqrt(2.0 / math.pi)
    return 0.5 * (1.0 + t) + 0.5 * x * (1.0 - t * t) * k * (1.0 + 3.0 * 0.044715 * x * x)


def _sigmoid(x):
    return 1.0 / (1.0 + jnp.exp(-x))


def rms_fwd(x, gains, name, tm=512):
    m, d = x.shape
    tm = min(tm, m)
    n = len(gains)

    def body(x_ref, *refs):
        xv = x_ref[...]
        y = xv * lax.rsqrt(jnp.sum(xv * xv, axis=-1, keepdims=True) * (1.0 / d) + RMS_EPS)
        for g_ref, o_ref in zip(refs[:n], refs[n:]):
            o_ref[...] = (y * g_ref[...]).astype(bf16)

    row = pl.BlockSpec((tm, d), lambda i: (i, 0))
    vec = pl.BlockSpec((1, d), lambda i: (0, 0))
    return pl.pallas_call(body, grid=(m // tm,), in_specs=[row] + [vec] * n, out_specs=[row] * n,
                          out_shape=[SDS((m, d), bf16)] * n, name=name, compiler_params=_cparams("parallel"))(x, *gains)


def rms_bwd(x, g, dy, add, out_dtype, name, tm=512):
    m, d = x.shape
    tm = min(tm, m)
    has_add = add is not None

    def body(x_ref, g_ref, dy_ref, *refs):
        dx_ref, dg_ref = refs[-2], refs[-1]
        xv = x_ref[...]
        dyv = dy_ref[...].astype(f32)
        r = lax.rsqrt(jnp.sum(xv * xv, axis=-1, keepdims=True) * (1.0 / d) + RMS_EPS)
        xn = xv * r
        dyg = dyv * g_ref[...]
        dx = r * (dyg - xn * (jnp.sum(dyg * xn, axis=-1, keepdims=True) * (1.0 / d)))
        if has_add:
            dx = dx + refs[0][...]
        dx_ref[...] = dx.astype(out_dtype)

        @pl.when(pl.program_id(0) == 0)
        def _():
            dg_ref[...] = jnp.zeros_like(dg_ref)

        dg_ref[...] += jnp.sum(dyv * xn, axis=0, keepdims=True)

    row = pl.BlockSpec((tm, d), lambda i: (i, 0))
    vec = pl.BlockSpec((1, d), lambda i: (0, 0))
    ins = [x, g, dy] + ([add] if has_add else [])
    return pl.pallas_call(body, grid=(m // tm,), in_specs=[row, vec, row] + ([row] if has_add else []),
                          out_specs=[row, vec], out_shape=[SDS((m, d), out_dtype), SDS((1, d), f32)], name=name,
                          compiler_params=_cparams("arbitrary"))(*ins)


def mm(a, b, name, trans_b=False, out_dtype=f32, tm=1024, tn=1024, col0=0, ncols=None, dep=None):
    m, k = a.shape
    n_all = b.shape[0] if trans_b else b.shape[1]
    n = n_all if ncols is None else ncols
    tm, tn = min(tm, m), min(tn, n)
    assert m % tm == 0 and n % tn == 0 and col0 % tn == 0 and not (trans_b and col0), (name, m, n, tm, tn)
    jb = col0 // tn

    def body(a_ref, b_ref, *rest):
        r = _dot_nt(a_ref[...], b_ref[...]) if trans_b else _dot(a_ref[...], b_ref[...])
        rest[-1][...] = r.astype(out_dtype)

    if trans_b:
        b_spec = pl.BlockSpec((tn, k), lambda j, i: (j, 0))
    else:
        b_spec = pl.BlockSpec((k, tn), lambda j, i: (0, jb + j))
    deps = [] if dep is None else [dep]
    dep_specs = [pl.BlockSpec((8, LANES), lambda j, i: (0, 0))] * len(deps)
    return pl.pallas_call(body, grid=(n // tn, m // tm), in_specs=[pl.BlockSpec((tm, k), lambda j, i: (i, 0)), b_spec] + dep_specs,
                          out_specs=pl.BlockSpec((tm, tn), lambda j, i: (i, j)), out_shape=SDS((m, n), out_dtype),
                          name=name, compiler_params=_cparams("parallel", "parallel"))(a, b, *deps)


def mm_tn(a, g, name, tk=1024, tn=1024, out_dtype=bf16, dep=None):
    s, k = a.shape
    n = g.shape[1]
    tk, tn = min(tk, k), min(tn, n)
    assert k % tk == 0 and n % tn == 0, (name, k, n, tk, tn)

    def body(a_ref, g_ref, *rest):
        rest[-1][...] = _dot_tn(a_ref[...], g_ref[...]).astype(out_dtype)

    deps = [] if dep is None else [dep]
    dep_specs = [pl.BlockSpec((8, LANES), lambda i, j: (0, 0))] * len(deps)
    return pl.pallas_call(body, grid=(k // tk, n // tn),
                          in_specs=[pl.BlockSpec((s, tk), lambda i, j: (0, i)), pl.BlockSpec((s, tn), lambda i, j: (0, j))] + dep_specs,
                          out_specs=pl.BlockSpec((tk, tn), lambda i, j: (i, j)), out_shape=SDS((k, n), out_dtype), name=name,
                          compiler_params=_cparams("parallel", "parallel"))(a, g, *deps)


def _resident(shape, index_map):
    return pl.BlockSpec(shape, index_map, pipeline_mode=pl.Buffered(1))


def _rms(xv):
    return xv * lax.rsqrt(jnp.sum(xv * xv, axis=-1, keepdims=True) * (1.0 / xv.shape[-1]) + RMS_EPS)


def _rms_bwd_math(xv, g, dy):
    d = xv.shape[-1]
    r = lax.rsqrt(jnp.sum(xv * xv, axis=-1, keepdims=True) * (1.0 / d) + RMS_EPS)
    xn = xv * r
    dyg = dy * g
    dx = r * (dyg - xn * (jnp.sum(dyg * xn, axis=-1, keepdims=True) * (1.0 / d)))
    return dx, jnp.sum(dy * xn, axis=0, keepdims=True)


SUB_ROWS = 512


def mm_resnorm(a, b, h, g_post, gains, name, tm=512):
    m, k = a.shape
    d = b.shape[1]
    n = len(gains)

    def body(a_ref, b_ref, h_ref, gp_ref, *refs):
        for r in range(tm // SUB_ROWS):
            rows = slice(r * SUB_ROWS, (r + 1) * SUB_ROWS)
            y = _dot(a_ref[rows, :], b_ref[...])
            refs[n][rows, :] = y
            hn = h_ref[rows, :] + _rms(y) * gp_ref[...]
            refs[n + 1][rows, :] = hn
            if n:
                z = _rms(hn)
                for g_ref, o_ref in zip(refs[:n], refs[n + 2:]):
                    o_ref[rows, :] = (z * g_ref[...]).astype(bf16)

    row = pl.BlockSpec((tm, d), lambda i: (i, 0))
    vec = pl.BlockSpec((1, d), lambda i: (0, 0))
    return pl.pallas_call(body, grid=(m // tm,),
                          in_specs=[pl.BlockSpec((tm, k), lambda i: (i, 0)), _resident((k, d), lambda i: (0, 0)), row, vec] + [vec] * n,
                          out_specs=[row] * (n + 2), out_shape=[SDS((m, d), f32)] * 2 + [SDS((m, d), bf16)] * n, name=name,
                          compiler_params=_cparams("parallel"))(a, b, h, g_post, *gains)


def mm_resnorm_loss(a, b, h, g_post, tgt, name, tm=512):
    m, k = a.shape
    d = b.shape[1]

    def body(a_ref, b_ref, h_ref, gp_ref, t_ref, dh_ref, dy_ref, dg_ref, l_ref):
        @pl.when(pl.program_id(0) == 0)
        def _():
            dg_ref[...] = jnp.zeros_like(dg_ref)
            l_ref[...] = jnp.zeros_like(l_ref)

        y = _dot(a_ref[...], b_ref[...])
        e = h_ref[...] + _rms(y) * gp_ref[...] - t_ref[...]
        dh = e * (1.0 / d)
        dh_ref[...] = dh
        part = jnp.sum(jnp.sum(e * e, axis=-1, keepdims=True), axis=0, keepdims=True) * (0.5 / d)
        l_ref[...] += jnp.broadcast_to(part, l_ref.shape)
        dy, dg = _rms_bwd_math(y, gp_ref[...], dh)
        dy_ref[...] = dy.astype(bf16)
        dg_ref[...] += dg

    row = pl.BlockSpec((tm, d), lambda i: (i, 0))
    vec = pl.BlockSpec((1, d), lambda i: (0, 0))
    return pl.pallas_call(body, grid=(m // tm,),
                          in_specs=[pl.BlockSpec((tm, k), lambda i: (i, 0)), _resident((k, d), lambda i: (0, 0)), row, vec, row],
                          out_specs=[row, row, vec, pl.BlockSpec((8, LANES), lambda i: (0, 0))],
                          out_shape=[SDS((m, d), f32), SDS((m, d), bf16), SDS((1, d), f32), SDS((8, LANES), f32)], name=name,
                          compiler_params=_cparams("arbitrary"))(a, b, h, g_post, tgt)


FFN_TILE = 1536


def ffn_act_grad(d_y2, w_d, factors, name, tm=1024):
    s, d = d_y2.shape
    ff = w_d.shape[0]
    tn = FFN_TILE
    nb = ff // tn

    def body(a_ref, b_ref, g_ref, u_ref, dg_ref, du_ref):
        av = a_ref[...]
        tc = 256
        for c in range(tn // tc):
            cols = slice(c * tc, (c + 1) * tc)
            da = _dot_nt(av, b_ref[cols, :])
            dg_ref[:, cols] = (da * g_ref[:, cols].astype(f32)).astype(bf16)
            du_ref[:, cols] = (da * u_ref[:, cols].astype(f32)).astype(bf16)

    tile = pl.BlockSpec((tm, tn), lambda j, i: (i, j))
    return pl.pallas_call(body, grid=(nb, s // tm),
                          in_specs=[pl.BlockSpec((tm, d), lambda j, i: (i, 0)), pl.BlockSpec((tn, d), lambda j, i: (j, 0)),
                                    pl.BlockSpec((tm, tn), lambda j, i: (i, 2 * j)), pl.BlockSpec((tm, tn), lambda j, i: (i, 2 * j + 1))],
                          out_specs=[tile, tile], out_shape=[SDS((s, ff), bf16)] * 2, name=name,
                          compiler_params=_cparams("parallel", "parallel"))(d_y2, w_d, factors, factors)


def ffn_in_grad(d_g, d_u, w_g, w_u, hmid, dh_out, g_pre, y1, g_post, name, tm=512):
    s, ff = d_g.shape
    d = w_g.shape[0]

    def body(dg_ref, du_ref, wg_ref, wu_ref, hm_ref, dho_ref, gpre_ref, y1_ref, gpost_ref, dhm_ref, dy1_ref, dgpre_ref, dgpost_ref):
        @pl.when(pl.program_id(0) == 0)
        def _():
            dgpre_ref[...] = jnp.zeros_like(dgpre_ref)
            dgpost_ref[...] = jnp.zeros_like(dgpost_ref)

        for r in range(tm // SUB_ROWS):
            rows = slice(r * SUB_ROWS, (r + 1) * SUB_ROWS)
            d_f = _dot_nt(dg_ref[rows, :], wg_ref[...]) + _dot_nt(du_ref[rows, :], wu_ref[...])
            dx, dg1 = _rms_bwd_math(hm_ref[rows, :], gpre_ref[...], d_f)
            dh_mid = dho_ref[rows, :] + dx
            dhm_ref[rows, :] = dh_mid
            dgpre_ref[...] += dg1
            dy1, dg2 = _rms_bwd_math(y1_ref[rows, :], gpost_ref[...], dh_mid)
            dy1_ref[rows, :] = dy1.astype(bf16)
            dgpost_ref[...] += dg2

    row = pl.BlockSpec((tm, d), lambda i: (i, 0))
    vec = pl.BlockSpec((1, d), lambda i: (0, 0))
    wide = pl.BlockSpec((tm, ff), lambda i: (i, 0))
    w_spec = _resident((d, ff), lambda i: (0, 0))
    return pl.pallas_call(body, grid=(s // tm,), in_specs=[wide, wide, w_spec, w_spec, row, row, vec, row, vec],
                          out_specs=[row, row, vec, vec], out_shape=[SDS((s, d), f32), SDS((s, d), bf16), SDS((1, d), f32), SDS((1, d), f32)],
                          name=name, compiler_params=_cparams("arbitrary"))(d_g, d_u, w_g, w_u, hmid, dh_out, g_pre, y1, g_post)


def proj_in_grad(pairs, x, add, name, tm=512, dep=None, below=None):
    s, d = x.shape
    n = len(pairs)
    extra = [] if dep is None else [dep]
    n_below = 0 if below is None else 2

    def body(*refs):
        x_ref, add_ref = refs[3 * n], refs[3 * n + 1]
        below_refs = refs[3 * n + 2:3 * n + 2 + n_below]
        outs = refs[3 * n + 2 + n_below + len(extra):]

        @pl.when(pl.program_id(0) == 0)
        def _():
            for o in outs[1:1 + n] + outs[2 + n:]:
                o[...] = jnp.zeros_like(o)

        xv = x_ref[...]
        dx = add_ref[...]
        for i in range(n):
            a_ref, b_ref, g_ref = refs[3 * i:3 * i + 3]
            dxi, dgi = _rms_bwd_math(xv, g_ref[...], _dot_nt(a_ref[...], b_ref[...]))
            dx = dx + dxi
            outs[1 + i][...] += dgi
        outs[0][...] = dx
        if below is not None:
            dy, dg = _rms_bwd_math(below_refs[0][...], below_refs[1][...], dx)
            outs[1 + n][...] = dy.astype(bf16)
            outs[2 + n][...] += dg

    row = pl.BlockSpec((tm, d), lambda i: (i, 0))
    vec = pl.BlockSpec((1, d), lambda i: (0, 0))
    in_specs, args = [], []
    for a, b, g in pairs:
        k = a.shape[1]
        in_specs += [pl.BlockSpec((tm, k), lambda i: (i, 0)), _resident((d, k), lambda i: (0, 0)), vec]
        args += [a, b, g]
    in_specs += [row, row] + [row, vec][:n_below] + [pl.BlockSpec((8, LANES), lambda i: (0, 0))] * len(extra)
    out_specs = [row] + [vec] * n + [row, vec][:n_below]
    out_shape = [SDS((s, d), f32)] + [SDS((1, d), f32)] * n + [SDS((s, d), bf16), SDS((1, d), f32)][:n_below]
    out = pl.pallas_call(body, grid=(s // tm,), in_specs=in_specs, out_specs=out_specs, out_shape=out_shape, name=name,
                         compiler_params=_cparams("arbitrary"))(*args, x, add, *(below or ()), *extra)
    return (out[0], out[1:1 + n]) + tuple(out[1 + n:])


def ffn_up(f, wg, wu, name, tm=1024, tc=256):
    s, d = f.shape
    ff = wg.shape[-1]
    tn = FFN_TILE

    def body(f_ref, wg_ref, wu_ref, fac_ref, act_ref):
        fv = f_ref[...]
        for c in range(tn // tc):
            lo = c * tc
            gg = _dot(fv, wg_ref[:, lo:lo + tc])
            uu = _dot(fv, wu_ref[:, lo:lo + tc])
            sg = _sigmoid(gg)
            silu = gg * sg
            fac_ref[:, lo:lo + tc] = (uu * (sg + silu * (1.0 - sg))).astype(bf16)
            fac_ref[:, tn + lo:tn + lo + tc] = silu.astype(bf16)
            act_ref[:, lo:lo + tc] = (silu * uu).astype(bf16)

    w_spec = pl.BlockSpec((d, tn), lambda j, i: (0, j))
    return pl.pallas_call(body, grid=(ff // tn, s // tm), in_specs=[pl.BlockSpec((tm, d), lambda j, i: (i, 0)), w_spec, w_spec],
                          out_specs=[pl.BlockSpec((tm, 2 * tn), lambda j, i: (i, j)), pl.BlockSpec((tm, tn), lambda j, i: (i, j))],
                          out_shape=[SDS((s, 2 * ff), bf16), SDS((s, ff), bf16)], name=name,
                          compiler_params=_cparams("parallel", "parallel"))(f, wg, wu)


def _gmlp_forward_chunk(u, v, w_refs, bias, ln_g, ln_b):
    gu, tu = _gelu(u)
    gv, tv = _gelu(v)
    mu = jnp.sum(gv, axis=-1, keepdims=True) * (1.0 / MAIN_WIDTH)
    xc = gv - mu
    rstd = lax.rsqrt(jnp.sum(xc * xc, axis=-1, keepdims=True) * (1.0 / MAIN_WIDTH) + LN_EPS)
    xhat = xc * rstd
    vln = xhat * ln_g + ln_b
    row = lax.broadcasted_iota(jnp.int32, (CHUNK, CHUNK), 0)
    col = lax.broadcasted_iota(jnp.int32, (CHUNK, CHUNK), 1)
    s_parts = []
    for g in range(A_GROUPS):
        w = jnp.where(col <= row, w_refs[g], jnp.zeros((), bf16))
        s_parts.append(_dot(w, vln[:, g * CHUNK:(g + 1) * CHUNK].astype(bf16)) + bias[:, g:g + 1])
    return gu, tu, tv, rstd, xhat, vln, s_parts


def gmlp_fwd(proj, ws, bs_t, ln_g, ln_b, name, tm=512, out_width=MAIN_WIDTH):
    s = proj.shape[0]

    def body(u_ref, v_ref, w_ref, b_ref, g_ref, bb_ref, o_ref):
        bias = b_ref[...]
        for c in range(tm // CHUNK):
            rows = slice(c * CHUNK, (c + 1) * CHUNK)
            gu, _, _, _, _, _, s_parts = _gmlp_forward_chunk(u_ref[rows, :], v_ref[rows, :], w_ref, bias, g_ref[...], bb_ref[...])
            for g in range(A_GROUPS):
                cols = slice(g * CHUNK, (g + 1) * CHUNK)
                o_ref[rows, cols] = (gu[:, cols] * s_parts[g]).astype(bf16)

    vec = pl.BlockSpec((1, MAIN_WIDTH), lambda i: (0, 0))
    return pl.pallas_call(
        body, grid=(s // tm,),
        in_specs=[pl.BlockSpec((tm, MAIN_WIDTH), lambda i: (i, 0)), pl.BlockSpec((tm, MAIN_WIDTH), lambda i: (i, 1)),
                  pl.BlockSpec((A_GROUPS, CHUNK, CHUNK), lambda i: (0, 0, 0)), pl.BlockSpec((CHUNK, A_GROUPS), lambda i: (0, 0)), vec, vec],
        out_specs=pl.BlockSpec((tm, MAIN_WIDTH), lambda i: (i, 0)), out_shape=SDS((s, out_width), bf16), name=name,
        compiler_params=_cparams("parallel"))(proj, proj, ws, bs_t, ln_g, ln_b)


def gmlp_bwd(proj, d_mixed, ws, ws_t, bs_t, ln_g, ln_b, name, tm=512, out_width=2 * MAIN_WIDTH):
    s = proj.shape[0]

    def body(u_ref, v_ref, dm_ref, w_ref, wt_ref, b_ref, g_ref, bb_ref, duv_ref, dw_ref, db_ref, dg_ref, dbb_ref):
        @pl.when(pl.program_id(0) == 0)
        def _():
            dw_ref[...] = jnp.zeros_like(dw_ref)
            db_ref[...] = jnp.zeros_like(db_ref)
            dg_ref[...] = jnp.zeros_like(dg_ref)
            dbb_ref[...] = jnp.zeros_like(dbb_ref)

        bias = b_ref[...]
        ln_gv = g_ref[...]
        row = lax.broadcasted_iota(jnp.int32, (CHUNK, CHUNK), 0)
        col = lax.broadcasted_iota(jnp.int32, (CHUNK, CHUNK), 1)
        lane = lax.broadcasted_iota(jnp.int32, (CHUNK, LANES), 1)
        for c in range(tm // CHUNK):
            rows = slice(c * CHUNK, (c + 1) * CHUNK)
            u = u_ref[rows, :]
            v = v_ref[rows, :]
            gu, tu, tv, rstd, xhat, vln, s_parts = _gmlp_forward_chunk(u, v, w_ref, bias, ln_gv, bb_ref[...])
            dm = dm_ref[rows, :]
            d_vln_parts = []
            d_gu_parts = []
            db_acc = jnp.zeros((CHUNK, LANES), f32)
            for g in range(A_GROUPS):
                cols = slice(g * CHUNK, (g + 1) * CHUNK)
                dmg = dm[:, cols]
                d_gu_parts.append(dmg * s_parts[g])
                d_s = dmg * gu[:, cols]
                db_acc = db_acc + jnp.where(lane == g, jnp.sum(d_s, axis=-1, keepdims=True), 0.0)
                d_sb = d_s.astype(bf16)
                dw_ref[g] += jnp.where(col <= row, _dot_nt(d_sb, vln[:, cols].astype(bf16)), 0.0)
                wt = jnp.where(row <= col, wt_ref[g], jnp.zeros((), bf16))
                d_vln_parts.append(_dot(wt, d_sb))
            db_ref[...] += db_acc
            d_vln = jnp.concatenate(d_vln_parts, axis=-1)
            d_gu = jnp.concatenate(d_gu_parts, axis=-1)
            dg_ref[...] += jnp.sum(d_vln * xhat, axis=0, keepdims=True)
            dbb_ref[...] += jnp.sum(d_vln, axis=0, keepdims=True)
            dxh = d_vln * ln_gv
            m1 = jnp.sum(dxh, axis=-1, keepdims=True) * (1.0 / MAIN_WIDTH)
            m2 = jnp.sum(dxh * xhat, axis=-1, keepdims=True) * (1.0 / MAIN_WIDTH)
            d_gv = rstd * (dxh - m1 - xhat * m2)
            duv_ref[rows, :MAIN_WIDTH] = (d_gu * _gelu_grad(u, tu)).astype(bf16)
            duv_ref[rows, MAIN_WIDTH:] = (d_gv * _gelu_grad(v, tv)).astype(bf16)

    vec = pl.BlockSpec((1, MAIN_WIDTH), lambda i: (0, 0))
    wspec = pl.BlockSpec((A_GROUPS, CHUNK, CHUNK), lambda i: (0, 0, 0))
    return pl.pallas_call(
        body, grid=(s // tm,),
        in_specs=[pl.BlockSpec((tm, MAIN_WIDTH), lambda i: (i, 0)), pl.BlockSpec((tm, MAIN_WIDTH), lambda i: (i, 1)),
                  pl.BlockSpec((tm, MAIN_WIDTH), lambda i: (i, 0)), wspec, wspec, pl.BlockSpec((CHUNK, A_GROUPS), lambda i: (0, 0)), vec, vec],
        out_specs=[pl.BlockSpec((tm, 2 * MAIN_WIDTH), lambda i: (i, 0)), wspec, pl.BlockSpec((CHUNK, LANES), lambda i: (0, 0)), vec, vec],
        out_shape=[SDS((s, out_width), bf16), SDS((A_GROUPS, CHUNK, CHUNK), f32), SDS((CHUNK, LANES), f32),
                   SDS((1, MAIN_WIDTH), f32), SDS((1, MAIN_WIDTH), f32)],
        name=name, compiler_params=_cparams("arbitrary"))(proj, proj, d_mixed, ws, ws_t, bs_t, ln_g, ln_b)


def _head_mask(width, h):
    lane = lax.broadcasted_iota(jnp.int32, (1, width), 1)
    return (lane >= h * HEAD_DIM) & (lane < (h + 1) * HEAD_DIM)


def mem_attn_fwd(proj, q_block, kv, into, name, tm=1024):
    s = proj.shape[0]
    n_mem = kv.shape[0]
    out_block = into.shape[1] // MEM_WIDTH - 1

    def body(q_ref, kv_ref, into_ref, o_ref):
        q = q_ref[...].astype(f32)
        k = kv_ref[:, :MEM_WIDTH].astype(bf16)
        v = kv_ref[:, MEM_WIDTH:].astype(bf16)
        out = jnp.zeros((tm, MEM_WIDTH), f32)
        for h in range(MEM_HEADS):
            msk = _head_mask(MEM_WIDTH, h)
            qh = jnp.where(msk, q, 0.0).astype(bf16)
            sc = _dot_nt(qh, k) * ATT_SCALE
            e = jnp.exp(sc - jnp.max(sc, axis=-1, keepdims=True))
            p = e / jnp.sum(e, axis=-1, keepdims=True)
            out = jnp.where(msk, _dot(p.astype(bf16), v), out)
        o_ref[...] = out.astype(bf16)

    return pl.pallas_call(body, grid=(s // tm,),
                          in_specs=[pl.BlockSpec((tm, MEM_WIDTH), lambda i: (i, q_block)), pl.BlockSpec((n_mem, 2 * MEM_WIDTH), lambda i: (0, 0)), _ANY],
                          out_specs=pl.BlockSpec((tm, MEM_WIDTH), lambda i: (i, out_block)), out_shape=SDS(into.shape, bf16), name=name,
                          input_output_aliases={2: 0}, compiler_params=_cparams("parallel"))(proj, kv, into)


def mem_attn_bwd(proj, q_block, kv, d_mixed, into, name, tm=1024):
    s = proj.shape[0]
    n_mem = kv.shape[0]
    out_block = into.shape[1] // MEM_WIDTH - 1

    def body(q_ref, kv_ref, do_ref, into_ref, dq_ref, dkv_ref):
        @pl.when(pl.program_id(0) == 0)
        def _():
            dkv_ref[...] = jnp.zeros_like(dkv_ref)

        q = q_ref[...].astype(f32)
        do = do_ref[...]
        k = kv_ref[:, :MEM_WIDTH].astype(bf16)
        v = kv_ref[:, MEM_WIDTH:].astype(bf16)
        dq = jnp.zeros((tm, MEM_WIDTH), f32)
        dk = jnp.zeros((n_mem, MEM_WIDTH), f32)
        dv = jnp.zeros((n_mem, MEM_WIDTH), f32)
        for h in range(MEM_HEADS):
            msk = _head_mask(MEM_WIDTH, h)
            qh = jnp.where(msk, q, 0.0).astype(bf16)
            doh = jnp.where(msk, do, 0.0).astype(bf16)
            sc = _dot_nt(qh, k) * ATT_SCALE
            e = jnp.exp(sc - jnp.max(sc, axis=-1, keepdims=True))
            p = e / jnp.sum(e, axis=-1, keepdims=True)
            dp = _dot_nt(doh, v)
            ds = p * (dp - jnp.sum(dp * p, axis=-1, keepdims=True))
            dsb = (ds * ATT_SCALE).astype(bf16)
            dq = jnp.where(msk, _dot(dsb, k), dq)
            dk = dk + _dot_tn(dsb, qh)
            dv = dv + _dot_tn(p.astype(bf16), doh)
        dq_ref[...] = dq.astype(bf16)
        dkv_ref[:, :MEM_WIDTH] += dk
        dkv_ref[:, MEM_WIDTH:] += dv

    return pl.pallas_call(
        body, grid=(s // tm,),
        in_specs=[pl.BlockSpec((tm, MEM_WIDTH), lambda i: (i, q_block)), pl.BlockSpec((n_mem, 2 * MEM_WIDTH), lambda i: (0, 0)),
                  pl.BlockSpec((tm, MEM_WIDTH), lambda i: (i, MAIN_WIDTH // MEM_WIDTH)), _ANY],
        out_specs=[pl.BlockSpec((tm, MEM_WIDTH), lambda i: (i, out_block)), pl.BlockSpec((n_mem, 2 * MEM_WIDTH), lambda i: (0, 0))],
        out_shape=[SDS(into.shape, bf16), SDS((n_mem, 2 * MEM_WIDTH), f32)], name=name,
        input_output_aliases={3: 0}, compiler_params=_cparams("arbitrary"))(proj, kv, d_mixed, into)


def _tri(t, upper):
    r = lax.broadcasted_iota(jnp.int32, (t, t), 0)
    c = lax.broadcasted_iota(jnp.int32, (t, t), 1)
    return ((r <= c) if upper else (r >= c)).astype(f32)


def fgate_fwd(z_t, b, name, t=512):
    hh, s = z_t.shape

    def body(z_ref, b_ref, c_ref):
        u = _tri(t, True)
        carry = jnp.zeros((hh, 1), f32)
        for blk in range(s // t):
            x = z_ref[:, blk * t:(blk + 1) * t] + b_ref[...]
            logf = jnp.minimum(x, 0.0) - jnp.log(1.0 + jnp.exp(-jnp.abs(x)))
            y = jnp.dot(logf, u, precision=lax.Precision.HIGHEST, preferred_element_type=f32) + carry
            c_ref[:, blk * t:(blk + 1) * t] = y
            carry = y[:, t - 1:t]

    return pl.pallas_call(body, out_shape=SDS((hh, s), f32), name=name, compiler_params=_cparams())(z_t, b)


def fgate_bwd(dc_t, z_t, b, name, t=512):
    hh, s = z_t.shape

    def body(dc_ref, z_ref, b_ref, dz_ref, db_ref):
        low = _tri(t, False)
        carry = jnp.zeros((hh, 1), f32)
        total = jnp.zeros((hh, 1), f32)
        for blk in reversed(range(s // t)):
            cols = slice(blk * t, (blk + 1) * t)
            y = jnp.dot(dc_ref[:, cols], low, precision=lax.Precision.HIGHEST, preferred_element_type=f32) + carry
            carry = y[:, 0:1]
            dz = y * _sigmoid(-(z_ref[:, cols] + b_ref[...]))
            dz_ref[:, cols] = dz
            total = total + jnp.sum(dz, axis=-1, keepdims=True)
        db_ref[...] = jnp.broadcast_to(total, db_ref.shape)

    return pl.pallas_call(body, out_shape=[SDS((hh, s), f32), SDS((hh, LANES), f32)], name=name,
                          compiler_params=_cparams())(dc_t, z_t, b)


def _pair_masks():
    lane = lax.broadcasted_iota(jnp.int32, (1, LANES), 1)
    return [lane < HEAD_DIM, lane >= HEAD_DIM]


def _tile_base(cr_ref, hh, lo):
    return cr_ref[hh:hh + 1, pl.ds(lo, LANES)][:, 0:1]


def fox_fwd(q, kv, c_row, name, tq=512, out_width=MAIN_WIDTH):
    s = kv.shape[0]
    nq = s // tq

    def body(q_ref, k_ref, v_ref, cr_ref, o_ref, lse_ref, ob_ref):
        i = pl.program_id(1)
        qv = q_ref[...]
        masks = _pair_masks()
        row = lax.broadcasted_iota(jnp.int32, (tq, tq), 0)
        col = lax.broadcasted_iota(jnp.int32, (tq, tq), 1)
        qh = [jnp.where(masks[hh], qv, jnp.zeros((), bf16)) * ATT_SCALE for hh in range(2)]
        ct = [_tile_base(cr_ref, hh, pl.multiple_of(i * tq, tq)) for hh in range(2)]

        def block(j, carry, diag):
            lo = pl.multiple_of(j * tq, tq)
            ks = k_ref[pl.ds(lo, tq), :]
            vs = v_ref[pl.ds(lo, tq), :]
            out = []
            for hh in range(2):
                m, l, acc = carry[hh]
                sc = _dot_nt(qh[hh], ks) + (ct[hh] - cr_ref[hh:hh + 1, pl.ds(lo, tq)])
                if diag:
                    sc = jnp.where(col <= row, sc, -jnp.inf)
                m_new = jnp.maximum(m, jnp.max(sc, axis=-1, keepdims=True))
                alpha = jnp.exp(m - m_new)
                p = jnp.exp(sc - m_new)
                l = alpha * l + jnp.sum(p, axis=-1, keepdims=True)
                p_hi = p.astype(bf16)
                p_lo = (p - p_hi.astype(f32)).astype(bf16)
                acc = alpha * acc + (_dot(p_hi, vs) + _dot(p_lo, vs))
                out.append((m_new, l, acc))
            return tuple(out)

        init = (jnp.full((tq, 1), -jnp.inf, f32), jnp.zeros((tq, 1), f32), jnp.zeros((tq, LANES), f32))
        carry = lax.fori_loop(0, i, functools.partial(block, diag=False), (init, init))
        res = [(acc / l, m + jnp.log(l)) for m, l, acc in block(i, carry, True)]
        out = jnp.where(masks[0], res[0][0], res[1][0])
        o_ref[...] = out
        ob_ref[...] = out.astype(bf16)
        lse_ref[...] = jnp.where(masks[0], res[0][1], res[1][1])

    return pl.pallas_call(
        body, grid=(FOX_PAIRS, nq),
        in_specs=[pl.BlockSpec((tq, LANES), lambda p, i: (i, p)), pl.BlockSpec((s, LANES), lambda p, i: (0, p)),
                  pl.BlockSpec((s, LANES), lambda p, i: (0, FOX_PAIRS + p)), pl.BlockSpec((None, 2, s), lambda p, i: (p, 0, 0))],
        out_specs=[pl.BlockSpec((tq, LANES), lambda p, i: (i, p)), pl.BlockSpec((None, tq, LANES), lambda p, i: (p, i, 0)),
                   pl.BlockSpec((tq, LANES), lambda p, i: (i, p))],
        out_shape=[SDS((s, MAIN_WIDTH), f32), SDS((FOX_PAIRS, s, LANES), f32), SDS((s, out_width), bf16)], name=name,
        compiler_params=_cparams("parallel", "parallel"))(q, kv, kv, c_row)


def fox_bwd(q, kv, d_mixed, o, lse, c_row, name, tq=512, dq_width=MAIN_WIDTH):
    s = kv.shape[0]
    nq = s // tq

    def body(q_ref, k_ref, v_ref, do_ref, o_ref, lse_ref, cr_ref, dqb_ref, dk_ref, dv_ref, dc_ref, dq_ref):
        j = pl.program_id(1)

        @pl.when(j == 0)
        def _():
            dq_ref[...] = jnp.zeros_like(dq_ref)

        masks = _pair_masks()
        sub = lax.broadcasted_iota(jnp.int32, (LANES, 1), 0)
        sub_masks = [sub < HEAD_DIM, sub >= HEAD_DIM]
        row = lax.broadcasted_iota(jnp.int32, (tq, tq), 0)
        col = lax.broadcasted_iota(jnp.int32, (tq, tq), 1)
        kj = k_ref[...]
        vj = v_ref[...]
        lo_j = pl.multiple_of(j * tq, tq)

        def block(i, carry, diag):
            dk_t, dv_t, dc0, dc1 = carry
            dcs = [dc0, dc1]
            lo = pl.multiple_of(i * tq, tq)
            qi = q_ref[pl.ds(lo, tq), :]
            qi = qi * ATT_SCALE
            qt_i = qi.T
            doi = do_ref[pl.ds(lo, tq), :]
            dot_i = doi.astype(bf16).T
            prod = doi.astype(bf16).astype(f32) * o_ref[pl.ds(lo, tq), :]
            lse_i = lse_ref[pl.ds(lo, tq), :]
            dq_i = jnp.zeros((tq, LANES), f32)
            for hh in range(2):
                qh = jnp.where(masks[hh], qi, jnp.zeros((), bf16))
                doh = jnp.where(masks[hh], doi, 0.0).astype(bf16)
                delta = jnp.sum(jnp.where(masks[hh], prod, 0.0), axis=-1, keepdims=True)
                sc = _dot_nt(qh, kj) + (_tile_base(cr_ref, hh, lo) - cr_ref[hh:hh + 1, pl.ds(lo_j, tq)])
                p = jnp.exp(sc - lse_i[:, hh * HEAD_DIM:hh * HEAD_DIM + 1])
                if diag:
                    p = jnp.where(col <= row, p, 0.0)
                dv_t = dv_t + _dot(jnp.where(sub_masks[hh], dot_i, jnp.zeros((), bf16)), p.astype(bf16))
                ds = p * (_dot_nt(doh, vj) - delta)
                dcs[hh] = dcs[hh] + jnp.sum(ds, axis=0, keepdims=True)
                dsb = ds.astype(bf16)
                dq_i = jnp.where(masks[hh], _dot(dsb, kj), dq_i)
                dk_t = dk_t + _dot(jnp.where(sub_masks[hh], qt_i, jnp.zeros((), bf16)), dsb)
            dq_ref[pl.ds(lo, tq), :] += dq_i * ATT_SCALE
            return dk_t, dv_t, dcs[0], dcs[1]

        zero = jnp.zeros((LANES, tq), f32)
        zrow = jnp.zeros((1, tq), f32)
        carry = block(j, (zero, zero, zrow, zrow), True)
        dk_t, dv_t, dc0, dc1 = lax.fori_loop(j + 1, nq, functools.partial(block, diag=False), carry)
        dk_ref[...] = dk_t.T.astype(bf16)
        dv_ref[...] = dv_t.T.astype(bf16)
        dc_ref[0:1, :] = -dc0
        dc_ref[1:2, :] = -dc1

        @pl.when(j == nq - 1)
        def _():
            dqb_ref[...] = dq_ref[...].astype(bf16)

    full = lambda p, j: (0, p)
    tile = lambda p, j: (j, p)
    return pl.pallas_call(
        body, grid=(FOX_PAIRS, nq),
        in_specs=[pl.BlockSpec((s, LANES), full), pl.BlockSpec((tq, LANES), tile), pl.BlockSpec((tq, LANES), lambda p, j: (j, FOX_PAIRS + p)),
                  pl.BlockSpec((s, LANES), full), pl.BlockSpec((s, LANES), full), pl.BlockSpec((None, s, LANES), lambda p, j: (p, 0, 0)),
                  pl.BlockSpec((None, 2, s), lambda p, j: (p, 0, 0))],
        out_specs=[pl.BlockSpec((s, LANES), full), pl.BlockSpec((tq, LANES), tile), pl.BlockSpec((tq, LANES), tile),
                   pl.BlockSpec((None, 2, tq), lambda p, j: (p, 0, j))],
        out_shape=[SDS((s, dq_width), bf16), SDS((s, MAIN_WIDTH), bf16), SDS((s, MAIN_WIDTH), bf16), SDS((FOX_PAIRS, 2, s), f32)],
        scratch_shapes=[pltpu.VMEM((s, LANES), f32)],
        name=name, compiler_params=_cparams("parallel", "arbitrary"))(q, kv, kv, d_mixed, o, lse, c_row)


def adamw(w, g, m, v, name, tr=256):
    r, c = w.shape
    tr = min(tr, r)
    assert r % tr == 0, (name, r, tr)
    c1 = 1.0 / (1.0 - ADAM_B1 ** ADAM_STEP)
    c2 = 1.0 / (1.0 - ADAM_B2 ** ADAM_STEP)

    def body(w_ref, g_ref, m_ref, v_ref, d_ref, mo_ref, vo_ref):
        gv = g_ref[...]
        mn = ADAM_B1 * m_ref[...] + (1.0 - ADAM_B1) * gv
        vn = ADAM_B2 * v_ref[...] + (1.0 - ADAM_B2) * gv * gv
        mo_ref[...] = mn
        vo_ref[...] = vn
        d_ref[...] = -ADAM_LR * ((mn * c1) / (jnp.sqrt(vn * c2) + ADAM_EPS) + ADAM_WD * w_ref[...])

    spec = pl.BlockSpec((tr, c), lambda i: (i, 0))
    return pl.pallas_call(body, grid=(r // tr,), in_specs=[spec] * 4, out_specs=[spec] * 3, out_shape=[SDS((r, c), f32)] * 3,
                          name=name, compiler_params=_cparams("parallel"))(w, g, m, v)


def adamw_owned(w, parts, m, v, name, tr):
    nl, r, c = w.shape
    cp = parts[0].shape[2]
    assert r % tr == 0 and len(parts) == nl, (name, r, tr)
    c1 = 1.0 / (1.0 - ADAM_B1 ** ADAM_STEP)
    c2 = 1.0 / (1.0 - ADAM_B2 ** ADAM_STEP)

    def body(*refs):
        w_ref, p_refs, (m_ref, v_ref) = refs[0], refs[1:1 + nl], refs[1 + nl:3 + nl]
        g_ref, d_ref, mo_ref, vo_ref = refs[3 + nl:]
        layer = pl.program_id(0)

        def total(p_ref):
            acc = p_ref[0].astype(f32)
            for k in range(1, N_DEV):
                acc = acc + p_ref[k].astype(f32)
            return acc

        gv = total(p_refs[0])
        for l in range(1, nl):
            gv = jnp.where(layer == l, total(p_refs[l]), gv)
        gv = gv[:, :c]
        g_ref[...] = gv
        mn = ADAM_B1 * m_ref[...] + (1.0 - ADAM_B1) * gv
        vn = ADAM_B2 * v_ref[...] + (1.0 - ADAM_B2) * gv * gv
        mo_ref[...] = mn
        vo_ref[...] = vn
        d_ref[...] = -ADAM_LR * ((mn * c1) / (jnp.sqrt(vn * c2) + ADAM_EPS) + ADAM_WD * w_ref[...])

    spec = pl.BlockSpec((None, tr, c), lambda l, i: (l, i, 0))
    last = r // tr - 1

    def part_spec(mine):
        return pl.BlockSpec((N_DEV, tr, cp), lambda l, i: (0, jnp.where(l == mine, i, jnp.where(l < mine, 0, last)), 0))

    return pl.pallas_call(body, grid=(nl, r // tr), in_specs=[spec] + [part_spec(l) for l in range(nl)] + [spec, spec], out_specs=[spec] * 4,
                          out_shape=[SDS((nl, r, c), f32)] * 4, name=name,
                          compiler_params=_cparams("parallel", "parallel"))(w, *parts, m, v)


def sum_leading(x, name, out_dtype=f32, tr=None):
    n, r, c = x.shape
    tr = tr or r
    assert r % tr == 0

    def body(x_ref, o_ref):
        acc = x_ref[0].astype(f32)
        for k in range(1, n):
            acc = acc + x_ref[k].astype(f32)
        o_ref[...] = acc.astype(out_dtype)

    return pl.pallas_call(body, grid=(r // tr,), in_specs=[pl.BlockSpec((n, tr, c), lambda i: (0, i, 0))],
                          out_specs=pl.BlockSpec((tr, c), lambda i: (i, 0)), out_shape=SDS((r, c), out_dtype), name=name,
                          compiler_params=_cparams("parallel"))(x)


_ANY = pl.BlockSpec(memory_space=pl.ANY)
_DMA = pltpu.SemaphoreType.DMA


_HBM = pl.BlockSpec(memory_space=pltpu.HBM)
_SEM = pl.BlockSpec(memory_space=pltpu.SEMAPHORE)
_EFFECT = pltpu.SideEffectType.DATAFLOW_SIDE_EFFECTING
_FLIPS = [(0, 0, 1), (1, 0, 0), (0, 1, 0), (1, 1, 0), (1, 0, 1), (0, 1, 1), (1, 1, 1)]


def _me():
    return lax.axis_index("x"), lax.axis_index("y"), lax.axis_index("c")


def _peers():
    mx, my, mc = _me()
    return [(jnp.bitwise_xor(mx, fx), jnp.bitwise_xor(my, fy), jnp.bitwise_xor(mc, fc)) for fx, fy, fc in _FLIPS]


def _index(dev):
    return 4 * dev[0] + 2 * dev[1] + dev[2]


def _win(ref, axis, k, size, count=1):
    idx = [slice(None)] * len(ref.shape)
    idx[axis] = pl.ds(k * size, count * size)
    return ref.at[tuple(idx)]


def _hbm(a):
    return pltpu.with_memory_space_constraint(a, pltpu.HBM)


def _exchange_start(srcs, lands, copies_of, name):
    n = len(srcs)

    def body(*refs):
        src = refs[:n]
        send_sems, recv_sems, self_sems = refs[2 * n:2 * n + 3]
        land = refs[3 * n + 3:4 * n + 3]
        token = refs[4 * n + 3]
        me = _index(_me())
        for a in range(n):
            for s_ref, d_ref, peer in copies_of(a, src[a], land[a], me):
                if peer is None:
                    pltpu.make_async_copy(s_ref, d_ref, self_sems.at[a]).start()
                else:
                    pltpu.make_async_remote_copy(src_ref=s_ref, dst_ref=d_ref, send_sem=send_sems.at[a], recv_sem=recv_sems.at[a],
                                                 device_id=peer, device_id_type=MESH).start()
        token[...] = jnp.zeros_like(token)

    outs = pl.pallas_call(
        body, name=name,
        out_shape=(_DMA((n,)), _DMA((n,)), _DMA((n,)), *[pltpu.HBM(s.shape, s.dtype) for s in srcs],
                   *[pltpu.HBM(l.shape, l.dtype) for l in lands], SDS((8, LANES), f32)),
        in_specs=[_HBM] * (2 * n), out_specs=(_SEM, _SEM, _SEM, *[_HBM] * (2 * n), pl.BlockSpec(memory_space=pltpu.VMEM)),
        input_output_aliases={i: 3 + i for i in range(2 * n)},
        compiler_params=pltpu.CompilerParams(has_side_effects=_EFFECT),
    )(*[_hbm(s) for s in srcs], *[_hbm(lax.empty(l.shape, l.dtype)) for l in lands])
    return dict(sems=outs[:3], srcs=list(outs[3:3 + n]), lands=list(outs[3 + n:3 + 2 * n]), token=outs[3 + 2 * n])


def _exchange_wait(started, waits_of, after, name, which=None):
    which = list(range(len(started["srcs"]))) if which is None else which
    srcs, lands = [started["srcs"][a] for a in which], [started["lands"][a] for a in which]
    n = len(which)

    def body(*refs):
        src = refs[:n]
        land = refs[n:2 * n]
        send_sems, recv_sems, self_sems = refs[2 * n:2 * n + 3]
        me = _index(_me())
        for pos, a in enumerate(which):
            seven, (s_ref, d_ref) = waits_of(a, src[pos], land[pos], me)
            both = pltpu.make_async_remote_copy(src_ref=seven, dst_ref=seven, send_sem=send_sems.at[a], recv_sem=recv_sems.at[a],
                                                device_id=_me(), device_id_type=MESH)
            both.wait_send()
            both.wait_recv()
            pltpu.make_async_copy(s_ref, d_ref, self_sems.at[a]).wait()

    outs = pl.pallas_call(
        body, name=name, out_shape=tuple(pltpu.HBM(t.shape, t.dtype) for t in srcs + lands),
        in_specs=[_HBM] * (2 * n) + [_SEM] * 3 + [_ANY], out_specs=tuple([_HBM] * (2 * n)),
        input_output_aliases={i: i for i in range(2 * n)},
        compiler_params=pltpu.CompilerParams(has_side_effects=_EFFECT),
    )(*srcs, *lands, *started["sems"], after)
    return list(outs[n:])


def gather_start(locs, axes, name):
    lands = [SDS(tuple(N_DEV * d if i == ax else d for i, d in enumerate(l.shape)), l.dtype) for l, ax in zip(locs, axes)]

    def copies_of(a, src, land, me):
        mine = _win(land, axes[a], me, src.shape[axes[a]])
        return [(src, mine, peer) for peer in _peers()] + [(src, mine, None)]

    return _exchange_start(locs, lands, copies_of, name)


def gather_wait(started, axes, after, name, which=None):
    def waits_of(a, src, land, me):
        size = src.shape[axes[a]]
        return _win(land, axes[a], 0, size, N_DEV - 1), (src, _win(land, axes[a], me, size))

    return _exchange_wait(started, waits_of, after, name, which)


def _part(ref, axis, k, stride, used):
    idx = [slice(None)] * len(ref.shape)
    idx[axis] = pl.ds(k * stride, used)
    return ref.at[tuple(idx)]


def scatter_start(grads, axes, name, used=None):
    strides = [g.shape[ax] // N_DEV for g, ax in zip(grads, axes)]
    used = used or strides
    lands = [SDS((N_DEV,) + tuple(u if i == ax else d for i, d in enumerate(g.shape)), g.dtype) for g, ax, u in zip(grads, axes, used)]

    def copies_of(a, src, land, me):
        out = [(_part(src, axes[a], _index(peer), strides[a], used[a]), land.at[me], peer) for peer in _peers()]
        return out + [(_part(src, axes[a], me, strides[a], used[a]), land.at[me], None)]

    return _exchange_start(grads, lands, copies_of, name)


def scatter_wait(started, axes, after, name, used=None):
    def waits_of(a, src, land, me):
        stride = src.shape[axes[a]] // N_DEV
        return land.at[pl.ds(0, N_DEV - 1)], (_part(src, axes[a], me, stride, used[a] if used else stride), land.at[me])

    return _exchange_wait(started, waits_of, after, name)


def _row_tile(rows, cap=512):
    return max(t for t in range(8, min(rows, cap) + 1, 8) if rows % t == 0)


_SMALL = [
    ("ln_mix_pre", (2, 1024)), ("ln_mix_post", (2, 1024)), ("ln_ffn_pre", (2, 1024)), ("ln_ffn_post", (2, 1024)),
    ("ln_mem", (2, 1024)), ("w_spatial", (1, 6, 128, 128)), ("b_spatial", (1, 6, 128)), ("ln_shared", (1024,)),
    ("b_forget", (12,)), ("ln_v_g", (1, 768)), ("ln_v_b", (1, 768)),
]
_SMALL_TILE = 8 * LANES


def _small_rows(shape):
    return -(-math.prod(shape) // _SMALL_TILE) * 8


def _pack_small(vals, shapes):
    parts = []
    for name, shape in shapes:
        flat = vals[name].reshape(-1).astype(f32)
        rows = _small_rows(shape)
        parts.append(jnp.pad(flat, (0, rows * LANES - flat.shape[0])).reshape(rows, LANES))
    return jnp.concatenate(parts, axis=0)


def _unpack_small(buf, shapes):
    out = {}
    lo = 0
    for name, shape in shapes:
        rows = _small_rows(shape)
        out[name] = buf[lo:lo + rows].reshape(-1)[:math.prod(shape)].reshape(shape)
        lo += rows
    return out


def kernel(x, mem, ln_mix_pre, ln_mix_post, ln_ffn_pre, ln_ffn_post, ln_mem, w_mem_kv, w_out, w_ffn_gate, w_ffn_up, w_ffn_down, w_in_a, w_spatial, b_spatial, ln_v_g, ln_v_b, ln_shared, w_shared_kv, b_forget, w_in_b, loss_target, m_ln_mix_pre, m_ln_mix_post, m_ln_ffn_pre, m_ln_ffn_post, m_ln_mem, m_w_mem_kv, m_w_out, m_w_ffn_gate, m_w_ffn_up, m_w_ffn_down, m_w_in_a, m_w_spatial, m_b_spatial, m_ln_v_g, m_ln_v_b, m_ln_shared, m_w_shared_kv, m_b_forget, m_w_in_b, v_ln_mix_pre, v_ln_mix_post, v_ln_ffn_pre, v_ln_ffn_post, v_ln_mem, v_w_mem_kv, v_w_out, v_w_ffn_gate, v_w_ffn_up, v_w_ffn_down, v_w_in_a, v_w_spatial, v_b_spatial, v_ln_v_g, v_ln_v_b, v_ln_shared, v_w_shared_kv, v_b_forget, v_w_in_b):
    weights = dict(ln_mix_pre=ln_mix_pre, ln_mix_post=ln_mix_post, ln_ffn_pre=ln_ffn_pre, ln_ffn_post=ln_ffn_post, ln_mem=ln_mem,
                   w_mem_kv=w_mem_kv, w_out=w_out, w_ffn_gate=w_ffn_gate, w_ffn_up=w_ffn_up, w_ffn_down=w_ffn_down, w_in_a=w_in_a,
                   w_spatial=w_spatial, b_spatial=b_spatial, ln_v_g=ln_v_g, ln_v_b=ln_v_b, ln_shared=ln_shared,
                   w_shared_kv=w_shared_kv, b_forget=b_forget, w_in_b=w_in_b)
    mom_m = dict(ln_mix_pre=m_ln_mix_pre, ln_mix_post=m_ln_mix_post, ln_ffn_pre=m_ln_ffn_pre, ln_ffn_post=m_ln_ffn_post, ln_mem=m_ln_mem,
                 w_mem_kv=m_w_mem_kv, w_out=m_w_out, w_ffn_gate=m_w_ffn_gate, w_ffn_up=m_w_ffn_up, w_ffn_down=m_w_ffn_down, w_in_a=m_w_in_a,
                 w_spatial=m_w_spatial, b_spatial=m_b_spatial, ln_v_g=m_ln_v_g, ln_v_b=m_ln_v_b, ln_shared=m_ln_shared,
                 w_shared_kv=m_w_shared_kv, b_forget=m_b_forget, w_in_b=m_w_in_b)
    mom_v = dict(ln_mix_pre=v_ln_mix_pre, ln_mix_post=v_ln_mix_post, ln_ffn_pre=v_ln_ffn_pre, ln_ffn_post=v_ln_ffn_post, ln_mem=v_ln_mem,
                 w_mem_kv=v_w_mem_kv, w_out=v_w_out, w_ffn_gate=v_w_ffn_gate, w_ffn_up=v_w_ffn_up, w_ffn_down=v_w_ffn_down, w_in_a=v_w_in_a,
                 w_spatial=v_w_spatial, b_spatial=v_b_spatial, ln_v_g=v_ln_v_g, ln_v_b=v_ln_v_b, ln_shared=v_ln_shared,
                 w_shared_kv=v_w_shared_kv, b_forget=v_b_forget, w_in_b=v_w_in_b)
    names = list(weights)
    mx, my, mc = lax.axis_index("x"), lax.axis_index("y"), lax.axis_index("c")
    me = 4 * mx + 2 * my + mc

    h0 = x[0]
    mem0 = mem[0]
    tgt = loss_target[0]
    seq = h0.shape[0]

    vec = lambda a: a.reshape(1, -1)
    pad_to = lambda a, axis, size: jnp.pad(a, [(0, size - a.shape[i] if i == axis else 0) for i in range(a.ndim)])

    def after(tok, a):
        return a + tok[0, 0].astype(a.dtype)

    lnv_loc = pad_to(jnp.concatenate([ln_v_g, ln_v_b], axis=0), 0, 8)
    st_a = gather_start([w_in_a.astype(bf16), pad_to(lnv_loc, 1, LANES)[None]], [0, 0], "gather_a_start")
    mix_locs = lambda l, tok: [after(tok, w_mem_kv[l]).astype(bf16), w_out[l].astype(bf16)]

    def ffn_gather_start(l, tok):
        gate_up = gather_start([pad_to(after(tok, w_ffn_gate[l]).astype(bf16), 1, FF_SHARD_PAD),
                                pad_to(w_ffn_up[l].astype(bf16), 1, FF_SHARD_PAD)], [1, 1], f"gather_gate_up{l}_start")
        down = gather_start([pad_to(after(gate_up["token"], w_ffn_down[l]).astype(bf16), 0, FF_SHARD_PAD)], [0], f"gather_down{l}_start")
        return gate_up, down

    st_b = [gather_start(mix_locs(0, st_a["token"]), [0, 0], "gather_b0_start"), None]
    st_c = ffn_gather_start(0, st_b[0]["token"])
    st_d = gather_start([after(st_c[1]["token"], w_in_b[0]).astype(bf16), pad_to(w_shared_kv.astype(bf16), 1, KV_PAD)], [0, 0],
                        "gather_d_start")
    st_b[1] = gather_start(mix_locs(1, st_d["token"]), [0, 0], "gather_b1_start")
    st_e = ffn_gather_start(1, st_b[1]["token"])
    ws = w_spatial[0].astype(bf16)
    ws_t = ws.transpose(0, 2, 1)
    bs_t = b_spatial[0].T

    (a0,) = rms_fwd(h0, [after(st_e[1]["token"], vec(ln_mix_pre[0]))], "a0_norm")
    w_in_a8, lnv8 = gather_wait(st_a, [0, 0], a0, "gather_a_wait")
    w_in_a_full = w_in_a8.transpose(1, 0, 2).reshape(D_MODEL, -1)
    lnv_g = lnv8[:, 0, :MAIN_WIDTH // N_DEV].reshape(1, MAIN_WIDTH)
    lnv_b = lnv8[:, 1, :MAIN_WIDTH // N_DEV].reshape(1, MAIN_WIDTH)
    proj0 = mm(a0, w_in_a_full, "proj0", tn=896)
    main0 = gmlp_fwd(proj0, ws, bs_t, lnv_g, lnv_b, "gmlp_fwd", out_width=D_MODEL)
    w_mkv, w_o = [None, None], [None, None]
    w_mkv[0], w_o[0] = gather_wait(st_b[0], [0, 0], main0, "gather_b0_wait")
    (memn0,) = rms_fwd(mem0, [vec(ln_mem[0])], "mem0_norm")
    kvm0 = mm(memn0, w_mkv[0], "kvm0")
    mixed0 = mem_attn_fwd(proj0, 2 * MAIN_WIDTH // MEM_WIDTH, kvm0, main0, "mem_attn0")
    y1_0, hmid0, f0 = mm_resnorm(mixed0, w_o[0], h0, vec(ln_mix_post[0]), [vec(ln_ffn_pre[0])], "mix_out0")
    w_g0, w_u0 = gather_wait(st_c[0], [1, 1], f0, "gather_gate_up0_wait")
    gu0, act0 = ffn_up(f0, w_g0, w_u0, "ffn_up0")
    (w_d0,) = gather_wait(st_c[1], [0], act0, "gather_down0_wait")
    y2_0, h1, a1, sin1 = mm_resnorm(act0, w_d0, hmid0, vec(ln_ffn_post[0]), [vec(ln_mix_pre[1]), vec(ln_shared)], "ffn_down0")

    w_inb, w_kv = gather_wait(st_d, [0, 0], sin1, "gather_d_wait")
    kvb = mm(sin1, w_kv, "kv_shared", out_dtype=bf16, tn=MAIN_WIDTH, ncols=2 * MAIN_WIDTH)
    zf = mm(sin1, w_kv, "forget_logits", tn=256, col0=2 * MAIN_WIDTH, ncols=256)
    qb = mm(a1, w_inb, "proj1", out_dtype=bf16)
    z_t = jnp.pad(zf[:, :FOX_HEADS].T, ((0, 16 - FOX_HEADS), (0, 0)))
    bf_col = jnp.pad(b_forget, (0, 16 - FOX_HEADS)).reshape(16, 1)
    c_t = fgate_fwd(z_t, bf_col, "fgate_fwd")
    c_row = c_t[:FOX_HEADS].reshape(FOX_PAIRS, 2, seq)
    main1, lse, main1_b = fox_fwd(qb, kvb, c_row, "fox_fwd", out_width=D_MODEL)
    w_mkv[1], w_o[1] = gather_wait(st_b[1], [0, 0], main1, "gather_b1_wait")
    (memn1,) = rms_fwd(mem0, [vec(ln_mem[1])], "mem1_norm")
    kvm1 = mm(memn1, w_mkv[1], "kvm1")
    mixed1 = mem_attn_fwd(qb, MAIN_WIDTH // MEM_WIDTH, kvm1, main1_b, "mem_attn1")
    y1_1, hmid1, f1 = mm_resnorm(mixed1, w_o[1], h1, vec(ln_mix_post[1]), [vec(ln_ffn_pre[1])], "mix_out1")
    w_g1, w_u1 = gather_wait(st_e[0], [1, 1], f1, "gather_gate_up1_wait")
    gu1, act1 = ffn_up(f1, w_g1, w_u1, "ffn_up1")
    (w_d1,) = gather_wait(st_e[1], [0], act1, "gather_down1_wait")
    dh, d_y2_1, dg_fpost1, loss_tile = mm_resnorm_loss(act1, w_d1, hmid1, vec(ln_ffn_post[1]), tgt, "ffn_down1_loss")
    ffn_w = [(w_g0, w_u0, w_d0), (w_g1, w_u1, w_d1)]
    ff_shard = w_ffn_down.shape[1]

    small = {}

    def ffn_backward(layer, dh_out, d_y2, hmid, f, gu, act, y1):
        w_g, w_u, w_d = ffn_w[layer]
        dw_down = mm_tn(act, d_y2, f"dw_down{layer}")
        rs_down = scatter_start([dw_down], [0], f"scatter_down{layer}_start", used=[ff_shard])
        d_g, d_u = ffn_act_grad(d_y2, w_d, gu, f"ffn_act_grad{layer}")
        dw_g = mm_tn(d_g, f, f"dw_gate{layer}", dep=rs_down["token"])
        dw_u = mm_tn(d_u, f, f"dw_up{layer}")
        rs_gate_up = scatter_start([dw_g, dw_u], [0, 0], f"scatter_gate_up{layer}_start", used=[ff_shard] * 2)
        dh_mid, d_y1, dg_fpre, dg_mpost = ffn_in_grad(d_g, d_u, w_g, w_u, hmid, dh_out, after(rs_gate_up["token"], vec(ln_ffn_pre[layer])),
                                                      y1, vec(ln_mix_post[layer]), f"ffn_in_grad{layer}")
        return dh_mid, d_y1, dg_fpre, dg_mpost, (rs_down, rs_gate_up)

    def mix_out_backward(layer, d_y1, mixed):
        dw_out = mm_tn(mixed, d_y1, f"dw_out{layer}")
        d_mixed = mm(d_y1, w_o[layer], f"d_mixed{layer}", trans_b=True)
        return d_mixed, dw_out

    def mem_backward(layer, q_src, q_block, kvm, memn, d_mixed, into):
        d_qm, d_kvm = mem_attn_bwd(q_src, q_block, kvm, d_mixed, into, f"mem_attn_bwd{layer}")
        d_kvm_b = d_kvm.astype(bf16)
        dw_mkv = mm_tn(memn, d_kvm_b, f"dw_mem_kv{layer}")
        d_memn = mm(d_kvm_b, w_mkv[layer], f"d_memn{layer}", trans_b=True)
        _, dg_mem = rms_bwd(mem0, vec(ln_mem[layer]), d_memn, None, bf16, f"mem_norm_bwd{layer}")
        return d_qm, dw_mkv, dg_mem


    dh_mid1, d_y1_1, dg_fpre1, dg_mpost1, rs_ffn1 = ffn_backward(1, dh, d_y2_1, hmid1, f1, gu1, act1, y1_1)
    d_mixed1, dw_out1 = mix_out_backward(1, d_y1_1, mixed1)
    dq_b, dk, dv, dc = fox_bwd(qb, kvb, d_mixed1, main1, lse, c_row, "fox_bwd", dq_width=D_MODEL)
    d_proj1, dw_mkv1, dg_mem1 = mem_backward(1, qb, MAIN_WIDTH // MEM_WIDTH, kvm1, memn1, d_mixed1, dq_b)
    rs_mix1 = scatter_start([dw_out1, dw_mkv1], [0, 0], "scatter_mix1_start")
    dc_t = jnp.pad(dc.reshape(FOX_HEADS, seq), ((0, 16 - FOX_HEADS), (0, 0)))
    dz_t, db_f = fgate_bwd(dc_t, z_t, bf_col, "fgate_bwd")
    d_kvf = jnp.concatenate([dk, dv, jnp.pad(dz_t[:FOX_HEADS].T.astype(bf16), ((0, 0), (0, KV_PAD - KV_WIDTH)))], axis=-1)
    dw_in_b = mm_tn(a1, d_proj1, "dw_in_b", dep=rs_mix1["token"])
    dw_kv = mm_tn(sin1, d_kvf, "dw_kv", tn=896)
    rs_2 = scatter_start([dw_in_b, dw_kv], [0, 0], "scatter_shared_start")
    dh1, (dg_pre1, dg_shared), d_y2_0, dg_fpost0 = proj_in_grad(
        [(d_proj1, w_inb, vec(ln_mix_pre[1])), (d_kvf, w_kv, vec(ln_shared))], h1, dh_mid1, "in_grad1", dep=rs_2["token"],
        below=(y2_0, vec(ln_ffn_post[0])))

    dh_mid0, d_y1_0, dg_fpre0, dg_mpost0, rs_ffn0 = ffn_backward(0, dh1, d_y2_0, hmid0, f0, gu0, act0, y1_0)
    d_mixed0, dw_out0 = mix_out_backward(0, d_y1_0, mixed0)
    d_uv, dw_s, db_s, dg_lnv, db_lnv = gmlp_bwd(proj0, d_mixed0, ws, ws_t, bs_t, lnv_g, lnv_b, "gmlp_bwd", out_width=w_in_a_full.shape[1])
    d_proj0, dw_mkv0, dg_mem0 = mem_backward(0, proj0, 2 * MAIN_WIDTH // MEM_WIDTH, kvm0, memn0, d_mixed0, d_uv)
    rs_mix0 = scatter_start([dw_out0, dw_mkv0], [0, 0], "scatter_mix0_start")

    small["ln_mix_pre"] = jnp.concatenate([jnp.zeros_like(dg_pre1), dg_pre1], axis=0)
    small["ln_mix_post"] = jnp.concatenate([dg_mpost0, dg_mpost1], axis=0)
    small["ln_ffn_pre"] = jnp.concatenate([dg_fpre0, dg_fpre1], axis=0)
    small["ln_ffn_post"] = jnp.concatenate([dg_fpost0, dg_fpost1], axis=0)
    small["ln_mem"] = jnp.concatenate([dg_mem0, dg_mem1], axis=0)
    small["w_spatial"] = dw_s[None]
    small["b_spatial"] = db_s[:, :A_GROUPS].T[None]
    small["ln_shared"] = dg_shared[0]
    small["b_forget"] = db_f[:FOX_HEADS, 0]
    small["ln_v_g"] = dg_lnv
    small["ln_v_b"] = db_lnv
    small_rows = jnp.concatenate([_pack_small(small, _SMALL), after(rs_mix0["token"], loss_tile)], axis=0)
    st_small = gather_start([small_rows[None]], [0], "gather_small_grads_start")
    dw_in_a_t = mm_tn(d_proj0, a0, "dw_in_a", tk=896, dep=st_small["token"])
    rs_in_a = scatter_start([dw_in_a_t], [0], "scatter_in_a_start")
    grad_x, (dg_pre0,) = proj_in_grad([(d_proj0, w_in_a_full, vec(ln_mix_pre[0]))], h0, dh_mid0, "in_grad0", dep=rs_in_a["token"])
    st_last = gather_start([dg_pre0.reshape(1, 8, LANES)], [0], "gather_last_grad_start")

    (p_down1,) = scatter_wait(rs_ffn1[0], [0], after(st_last["token"], grad_x[:8, :LANES]), "scatter_down1_wait", used=[ff_shard])
    p_gate1, p_up1 = scatter_wait(rs_ffn1[1], [0, 0], p_down1, "scatter_gate_up1_wait", used=[ff_shard] * 2)
    p_out1, p_mkv1 = scatter_wait(rs_mix1, [0, 0], p_gate1, "scatter_mix1_wait")
    p_in_b, p_kv = scatter_wait(rs_2, [0, 0], p_out1, "scatter_shared_wait")
    (p_down0,) = scatter_wait(rs_ffn0[0], [0], p_in_b, "scatter_down0_wait", used=[ff_shard])
    p_gate0, p_up0 = scatter_wait(rs_ffn0[1], [0, 0], p_down0, "scatter_gate_up0_wait", used=[ff_shard] * 2)
    p_out0, p_mkv0 = scatter_wait(rs_mix0, [0, 0], p_gate0, "scatter_mix0_wait")
    owned_parts = dict(w_ffn_gate=[p_gate0, p_gate1], w_ffn_up=[p_up0, p_up1], w_ffn_down=[p_down0, p_down1], w_out=[p_out0, p_out1],
                       w_mem_kv=[p_mkv0, p_mkv1], w_in_b=[p_in_b], w_shared_kv=[p_kv])

    grad_w, delta, new_m, new_v = {}, {}, {}, {}
    transposed = ("w_ffn_gate", "w_ffn_up", "w_in_a")

    def adamw_sharded(n, parts):
        shape = weights[n].shape
        three_d = shape if len(shape) == 3 else (1,) + shape
        view = (lambda t: t.reshape(three_d).transpose(0, 2, 1)) if n in transposed else (lambda t: t.reshape(three_d))
        back = (lambda t: t.transpose(0, 2, 1).reshape(shape)) if n in transposed else (lambda t: t.reshape(shape))
        w_view = view(weights[n])
        outs = adamw_owned(w_view, parts, view(mom_m[n]), view(mom_v[n]), f"adamw_{n}", tr=_row_tile(w_view.shape[1]))
        grad_w[n], delta[n], new_m[n], new_v[n] = (back(t) for t in outs)

    for n, parts in owned_parts.items():
        adamw_sharded(n, parts)
    (p_in_a,) = scatter_wait(rs_in_a, [0], delta["w_shared_kv"], "scatter_in_a_wait")
    adamw_sharded("w_in_a", [p_in_a])
    (small_all,) = gather_wait(st_small, [0], p_in_a, "gather_small_grads_wait")
    (last_all,) = gather_wait(st_last, [0], small_all, "gather_last_grad_wait")
    small_sum = sum_leading(small_all, "sum_small_grads")
    loss = small_sum[small_rows.shape[0] - 1, 0]
    g_small = _unpack_small(small_sum, _SMALL)
    g_small["ln_mix_pre"] = jnp.concatenate([sum_leading(last_all, "sum_last_grad").reshape(1, D_MODEL), g_small["ln_mix_pre"][1:]], axis=0)
    shard = MAIN_WIDTH // N_DEV
    for n in ("ln_v_g", "ln_v_b"):
        g_small[n] = lax.dynamic_slice_in_dim(g_small[n], me * shard, shard, axis=1)
    grad_w.update(g_small)
    small_local_shapes = [(n, tuple(weights[n].shape)) for n, _ in _SMALL]
    packed = [_pack_small(src, small_local_shapes) for src in (weights, grad_w, mom_m, mom_v)]
    outs = adamw(*packed, "adamw_small", tr=packed[0].shape[0])
    for dst, buf in zip((delta, new_m, new_v), outs):
        dst.update(_unpack_small(buf, small_local_shapes))

    return (loss, grad_x[None], *[grad_w[n] for n in names], *[delta[n] for n in names],
            *[new_m[n] for n in names], *[new_v[n] for n in names])
```

```python
import functools
import math

import jax
import jax.numpy as jnp
from jax import lax
from jax.experimental import pallas as pl
from jax.experimental.pallas import tpu as pltpu

f32 = jnp.float32
bf16 = jnp.bfloat16
SDS = jax.ShapeDtypeStruct

D_MODEL = 1024
MAIN_WIDTH = 768
MEM_WIDTH = 256
HEAD_DIM = 64
MEM_HEADS = 4
FOX_HEADS = 12
FOX_PAIRS = FOX_HEADS // 2
CHUNK = 128
A_GROUPS = 6
FF_SHARD_PAD = 384
KV_WIDTH = 2 * MAIN_WIDTH + FOX_HEADS
KV_PAD = 1792
RMS_EPS = 1e-6
LN_EPS = 1e-5
ATT_SCALE = HEAD_DIM ** -0.5
ADAM_LR, ADAM_B1, ADAM_B2, ADAM_EPS, ADAM_WD, ADAM_STEP = 0.001, 0.9, 0.999, 1e-08, 0.01, 10
N_DEV = 8
MESH = pl.DeviceIdType.MESH
V7X_VMEM_LIMIT = 56 * 1024 * 1024
LANES = 128


def _cparams(*sem):
    return pltpu.CompilerParams(dimension_semantics=sem or None, vmem_limit_bytes=V7X_VMEM_LIMIT)


def _dot(a, b):
    return jnp.dot(a, b, preferred_element_type=f32)


def _dot_nt(a, b):
    return lax.dot_general(a, b, (((1,), (1,)), ((), ())), preferred_element_type=f32)


def _dot_tn(a, b):
    return lax.dot_general(a, b, (((0,), (0,)), ((), ())), preferred_element_type=f32)


def _gelu(x):
    k = math.sqrt(2.0 / math.pi)
    t = jnp.tanh(k * (x + 0.044715 * x * x * x))
    return 0.5 * x * (1.0 + t), t


def _gelu_grad(x, t):
    k = math.sqrt(2.0 / math.pi)
    return 0.5 * (1.0 + t) + 0.5 * x * (1.0 - t * t) * k * (1.0 + 3.0 * 0.044715 * x * x)


def _sigmoid(x):
    return 1.0 / (1.0 + jnp.exp(-x))


def rms_fwd(x, gains, name, tm=512):
    m, d = x.shape
    tm = min(tm, m)
    n = len(gains)

    def body(x_ref, *refs):
        xv = x_ref[...]
        y = xv * lax.rsqrt(jnp.sum(xv * xv, axis=-1, keepdims=True) * (1.0 / d) + RMS_EPS)
        for g_ref, o_ref in zip(refs[:n], refs[n:]):
            o_ref[...] = (y * g_ref[...]).astype(bf16)

    row = pl.BlockSpec((tm, d), lambda i: (i, 0))
    vec = pl.BlockSpec((1, d), lambda i: (0, 0))
    return pl.pallas_call(body, grid=(m // tm,), in_specs=[row] + [vec] * n, out_specs=[row] * n,
                          out_shape=[SDS((m, d), bf16)] * n, name=name, compiler_params=_cparams("parallel"))(x, *gains)


def rms_bwd(x, g, dy, add, out_dtype, name, tm=512):
    m, d = x.shape
    tm = min(tm, m)
    has_add = add is not None

    def body(x_ref, g_ref, dy_ref, *refs):
        dx_ref, dg_ref = refs[-2], refs[-1]
        xv = x_ref[...]
        dyv = dy_ref[...].astype(f32)
        r = lax.rsqrt(jnp.sum(xv * xv, axis=-1, keepdims=True) * (1.0 / d) + RMS_EPS)
        xn = xv * r
        dyg = dyv * g_ref[...]
        dx = r * (dyg - xn * (jnp.sum(dyg * xn, axis=-1, keepdims=True) * (1.0 / d)))
        if has_add:
            dx = dx + refs[0][...]
        dx_ref[...] = dx.astype(out_dtype)

        @pl.when(pl.program_id(0) == 0)
        def _():
            dg_ref[...] = jnp.zeros_like(dg_ref)

        dg_ref[...] += jnp.sum(dyv * xn, axis=0, keepdims=True)

    row = pl.BlockSpec((tm, d), lambda i: (i, 0))
    vec = pl.BlockSpec((1, d), lambda i: (0, 0))
    ins = [x, g, dy] + ([add] if has_add else [])
    return pl.pallas_call(body, grid=(m // tm,), in_specs=[row, vec, row] + ([row] if has_add else []),
                          out_specs=[row, vec], out_shape=[SDS((m, d), out_dtype), SDS((1, d), f32)], name=name,
                          compiler_params=_cparams("arbitrary"))(*ins)


def mm(a, b, name, trans_b=False, out_dtype=f32, tm=1024, tn=1024, col0=0, ncols=None, dep=None):
    m, k = a.shape
    n_all = b.shape[0] if trans_b else b.shape[1]
    n = n_all if ncols is None else ncols
    tm, tn = min(tm, m), min(tn, n)
    assert m % tm == 0 and n % tn == 0 and col0 % tn == 0 and not (trans_b and col0), (name, m, n, tm, tn)
    jb = col0 // tn

    def body(a_ref, b_ref, *rest):
        r = _dot_nt(a_ref[...], b_ref[...]) if trans_b else _dot(a_ref[...], b_ref[...])
        rest[-1][...] = r.astype(out_dtype)

    if trans_b:
        b_spec = pl.BlockSpec((tn, k), lambda j, i: (j, 0))
    else:
        b_spec = pl.BlockSpec((k, tn), lambda j, i: (0, jb + j))
    deps = [] if dep is None else [dep]
    dep_specs = [pl.BlockSpec((8, LANES), lambda j, i: (0, 0))] * len(deps)
    return pl.pallas_call(body, grid=(n // tn, m // tm), in_specs=[pl.BlockSpec((tm, k), lambda j, i: (i, 0)), b_spec] + dep_specs,
                          out_specs=pl.BlockSpec((tm, tn), lambda j, i: (i, j)), out_shape=SDS((m, n), out_dtype),
                          name=name, compiler_params=_cparams("parallel", "parallel"))(a, b, *deps)


def mm_tn(a, g, name, tk=1024, tn=1024, out_dtype=bf16, dep=None):
    s, k = a.shape
    n = g.shape[1]
    tk, tn = min(tk, k), min(tn, n)
    assert k % tk == 0 and n % tn == 0, (name, k, n, tk, tn)

    def body(a_ref, g_ref, *rest):
        rest[-1][...] = _dot_tn(a_ref[...], g_ref[...]).astype(out_dtype)

    deps = [] if dep is None else [dep]
    dep_specs = [pl.BlockSpec((8, LANES), lambda i, j: (0, 0))] * len(deps)
    return pl.pallas_call(body, grid=(k // tk, n // tn),
                          in_specs=[pl.BlockSpec((s, tk), lambda i, j: (0, i)), pl.BlockSpec((s, tn), lambda i, j: (0, j))] + dep_specs,
                          out_specs=pl.BlockSpec((tk, tn), lambda i, j: (i, j)), out_shape=SDS((k, n), out_dtype), name=name,
                          compiler_params=_cparams("parallel", "parallel"))(a, g, *deps)


def _resident(shape, index_map):
    return pl.BlockSpec(shape, index_map, pipeline_mode=pl.Buffered(1))


def _rms(xv):
    return xv * lax.rsqrt(jnp.sum(xv * xv, axis=-1, keepdims=True) * (1.0 / xv.shape[-1]) + RMS_EPS)


def _rms_bwd_math(xv, g, dy):
    d = xv.shape[-1]
    r = lax.rsqrt(jnp.sum(xv * xv, axis=-1, keepdims=True) * (1.0 / d) + RMS_EPS)
    xn = xv * r
    dyg = dy * g
    dx = r * (dyg - xn * (jnp.sum(dyg * xn, axis=-1, keepdims=True) * (1.0 / d)))
    return dx, jnp.sum(dy * xn, axis=0, keepdims=True)


SUB_ROWS = 512


def mm_resnorm(a, b, h, g_post, gains, name, tm=512):
    m, k = a.shape
    d = b.shape[1]
    n = len(gains)

    def body(a_ref, b_ref, h_ref, gp_ref, *refs):
        for r in range(tm // SUB_ROWS):
            rows = slice(r * SUB_ROWS, (r + 1) * SUB_ROWS)
            y = _dot(a_ref[rows, :], b_ref[...])
            refs[n][rows, :] = y
            hn = h_ref[rows, :] + _rms(y) * gp_ref[...]
            refs[n + 1][rows, :] = hn
            if n:
                z = _rms(hn)
                for g_ref, o_ref in zip(refs[:n], refs[n + 2:]):
                    o_ref[rows, :] = (z * g_ref[...]).astype(bf16)

    row = pl.BlockSpec((tm, d), lambda i: (i, 0))
    vec = pl.BlockSpec((1, d), lambda i: (0, 0))
    return pl.pallas_call(body, grid=(m // tm,),
                          in_specs=[pl.BlockSpec((tm, k), lambda i: (i, 0)), _resident((k, d), lambda i: (0, 0)), row, vec] + [vec] * n,
                          out_specs=[row] * (n + 2), out_shape=[SDS((m, d), f32)] * 2 + [SDS((m, d), bf16)] * n, name=name,
                          compiler_params=_cparams("parallel"))(a, b, h, g_post, *gains)


def mm_resnorm_loss(a, b, h, g_post, tgt, name, tm=512):
    m, k = a.shape
    d = b.shape[1]

    def body(a_ref, b_ref, h_ref, gp_ref, t_ref, dh_ref, dy_ref, dg_ref, l_ref):
        @pl.when(pl.program_id(0) == 0)
        def _():
            dg_ref[...] = jnp.zeros_like(dg_ref)
            l_ref[...] = jnp.zeros_like(l_ref)

        y = _dot(a_ref[...], b_ref[...])
        e = h_ref[...] + _rms(y) * gp_ref[...] - t_ref[...]
        dh = e * (1.0 / d)
        dh_ref[...] = dh
        part = jnp.sum(jnp.sum(e * e, axis=-1, keepdims=True), axis=0, keepdims=True) * (0.5 / d)
        l_ref[...] += jnp.broadcast_to(part, l_ref.shape)
        dy, dg = _rms_bwd_math(y, gp_ref[...], dh)
        dy_ref[...] = dy.astype(bf16)
        dg_ref[...] += dg

    row = pl.BlockSpec((tm, d), lambda i: (i, 0))
    vec = pl.BlockSpec((1, d), lambda i: (0, 0))
    return pl.pallas_call(body, grid=(m // tm,),
                          in_specs=[pl.BlockSpec((tm, k), lambda i: (i, 0)), _resident((k, d), lambda i: (0, 0)), row, vec, row],
                          out_specs=[row, row, vec, pl.BlockSpec((8, LANES), lambda i: (0, 0))],
                          out_shape=[SDS((m, d), f32), SDS((m, d), bf16), SDS((1, d), f32), SDS((8, LANES), f32)], name=name,
                          compiler_params=_cparams("arbitrary"))(a, b, h, g_post, tgt)


FFN_TILE = 1536


def ffn_act_grad(d_y2, w_d, factors, name, tm=1024):
    s, d = d_y2.shape
    ff = w_d.shape[0]
    tn = FFN_TILE
    nb = ff // tn

    def body(a_ref, b_ref, g_ref, u_ref, dg_ref, du_ref):
        av = a_ref[...]
        tc = 256
        for c in range(tn // tc):
            cols = slice(c * tc, (c + 1) * tc)
            da = _dot_nt(av, b_ref[cols, :])
            dg_ref[:, cols] = (da * g_ref[:, cols].astype(f32)).astype(bf16)
            du_ref[:, cols] = (da * u_ref[:, cols].astype(f32)).astype(bf16)

    tile = pl.BlockSpec((tm, tn), lambda j, i: (i, j))
    return pl.pallas_call(body, grid=(nb, s // tm),
                          in_specs=[pl.BlockSpec((tm, d), lambda j, i: (i, 0)), pl.BlockSpec((tn, d), lambda j, i: (j, 0)),
                                    pl.BlockSpec((tm, tn), lambda j, i: (i, 2 * j)), pl.BlockSpec((tm, tn), lambda j, i: (i, 2 * j + 1))],
                          out_specs=[tile, tile], out_shape=[SDS((s, ff), bf16)] * 2, name=name,
                          compiler_params=_cparams("parallel", "parallel"))(d_y2, w_d, factors, factors)


def ffn_in_grad(d_g, d_u, w_g, w_u, hmid, dh_out, g_pre, y1, g_post, name, tm=512):
    s, ff = d_g.shape
    d = w_g.shape[0]

    def body(dg_ref, du_ref, wg_ref, wu_ref, hm_ref, dho_ref, gpre_ref, y1_ref, gpost_ref, dhm_ref, dy1_ref, dgpre_ref, dgpost_ref):
        @pl.when(pl.program_id(0) == 0)
        def _():
            dgpre_ref[...] = jnp.zeros_like(dgpre_ref)
            dgpost_ref[...] = jnp.zeros_like(dgpost_ref)

        for r in range(tm // SUB_ROWS):
            rows = slice(r * SUB_ROWS, (r + 1) * SUB_ROWS)
            d_f = _dot_nt(dg_ref[rows, :], wg_ref[...]) + _dot_nt(du_ref[rows, :], wu_ref[...])
            dx, dg1 = _rms_bwd_math(hm_ref[rows, :], gpre_ref[...], d_f)
            dh_mid = dho_ref[rows, :] + dx
            dhm_ref[rows, :] = dh_mid
            dgpre_ref[...] += dg1
            dy1, dg2 = _rms_bwd_math(y1_ref[rows, :], gpost_ref[...], dh_mid)
            dy1_ref[rows, :] = dy1.astype(bf16)
            dgpost_ref[...] += dg2

    row = pl.BlockSpec((tm, d), lambda i: (i, 0))
    vec = pl.BlockSpec((1, d), lambda i: (0, 0))
    wide = pl.BlockSpec((tm, ff), lambda i: (i, 0))
    w_spec = _resident((d, ff), lambda i: (0, 0))
    return pl.pallas_call(body, grid=(s // tm,), in_specs=[wide, wide, w_spec, w_spec, row, row, vec, row, vec],
                          out_specs=[row, row, vec, vec], out_shape=[SDS((s, d), f32), SDS((s, d), bf16), SDS((1, d), f32), SDS((1, d), f32)],
                          name=name, compiler_params=_cparams("arbitrary"))(d_g, d_u, w_g, w_u, hmid, dh_out, g_pre, y1, g_post)


def proj_in_grad(pairs, x, add, name, tm=512, dep=None, below=None):
    s, d = x.shape
    n = len(pairs)
    extra = [] if dep is None else [dep]
    n_below = 0 if below is None else 2

    def body(*refs):
        x_ref, add_ref = refs[3 * n], refs[3 * n + 1]
        below_refs = refs[3 * n + 2:3 * n + 2 + n_below]
        outs = refs[3 * n + 2 + n_below + len(extra):]

        @pl.when(pl.program_id(0) == 0)
        def _():
            for o in outs[1:1 + n] + outs[2 + n:]:
                o[...] = jnp.zeros_like(o)

        xv = x_ref[...]
        dx = add_ref[...]
        for i in range(n):
            a_ref, b_ref, g_ref = refs[3 * i:3 * i + 3]
            dxi, dgi = _rms_bwd_math(xv, g_ref[...], _dot_nt(a_ref[...], b_ref[...]))
            dx = dx + dxi
            outs[1 + i][...] += dgi
        outs[0][...] = dx
        if below is not None:
            dy, dg = _rms_bwd_math(below_refs[0][...], below_refs[1][...], dx)
            outs[1 + n][...] = dy.astype(bf16)
            outs[2 + n][...] += dg

    row = pl.BlockSpec((tm, d), lambda i: (i, 0))
    vec = pl.BlockSpec((1, d), lambda i: (0, 0))
    in_specs, args = [], []
    for a, b, g in pairs:
        k = a.shape[1]
        in_specs += [pl.BlockSpec((tm, k), lambda i: (i, 0)), _resident((d, k), lambda i: (0, 0)), vec]
        args += [a, b, g]
    in_specs += [row, row] + [row, vec][:n_below] + [pl.BlockSpec((8, LANES), lambda i: (0, 0))] * len(extra)
    out_specs = [row] + [vec] * n + [row, vec][:n_below]
    out_shape = [SDS((s, d), f32)] + [SDS((1, d), f32)] * n + [SDS((s, d), bf16), SDS((1, d), f32)][:n_below]
    out = pl.pallas_call(body, grid=(s // tm,), in_specs=in_specs, out_specs=out_specs, out_shape=out_shape, name=name,
                         compiler_params=_cparams("arbitrary"))(*args, x, add, *(below or ()), *extra)
    return (out[0], out[1:1 + n]) + tuple(out[1 + n:])


def ffn_up(f, wg, wu, name, tm=1024, tc=256):
    s, d = f.shape
    ff = wg.shape[-1]
    tn = FFN_TILE

    def body(f_ref, wg_ref, wu_ref, fac_ref, act_ref):
        fv = f_ref[...]
        for c in range(tn // tc):
            lo = c * tc
            gg = _dot(fv, wg_ref[:, lo:lo + tc])
            uu = _dot(fv, wu_ref[:, lo:lo + tc])
            sg = _sigmoid(gg)
            silu = gg * sg
            fac_ref[:, lo:lo + tc] = (uu * (sg + silu * (1.0 - sg))).astype(bf16)
            fac_ref[:, tn + lo:tn + lo + tc] = silu.astype(bf16)
            act_ref[:, lo:lo + tc] = (silu * uu).astype(bf16)

    w_spec = pl.BlockSpec((d, tn), lambda j, i: (0, j))
    return pl.pallas_call(body, grid=(ff // tn, s // tm), in_specs=[pl.BlockSpec((tm, d), lambda j, i: (i, 0)), w_spec, w_spec],
                          out_specs=[pl.BlockSpec((tm, 2 * tn), lambda j, i: (i, j)), pl.BlockSpec((tm, tn), lambda j, i: (i, j))],
                          out_shape=[SDS((s, 2 * ff), bf16), SDS((s, ff), bf16)], name=name,
                          compiler_params=_cparams("parallel", "parallel"))(f, wg, wu)


def _gmlp_forward_chunk(u, v, w_refs, bias, ln_g, ln_b):
    gu, tu = _gelu(u)
    gv, tv = _gelu(v)
    mu = jnp.sum(gv, axis=-1, keepdims=True) * (1.0 / MAIN_WIDTH)
    xc = gv - mu
    rstd = lax.rsqrt(jnp.sum(xc * xc, axis=-1, keepdims=True) * (1.0 / MAIN_WIDTH) + LN_EPS)
    xhat = xc * rstd
    vln = xhat * ln_g + ln_b
    row = lax.broadcasted_iota(jnp.int32, (CHUNK, CHUNK), 0)
    col = lax.broadcasted_iota(jnp.int32, (CHUNK, CHUNK), 1)
    s_parts = []
    for g in range(A_GROUPS):
        w = jnp.where(col <= row, w_refs[g], jnp.zeros((), bf16))
        s_parts.append(_dot(w, vln[:, g * CHUNK:(g + 1) * CHUNK].astype(bf16)) + bias[:, g:g + 1])
    return gu, tu, tv, rstd, xhat, vln, s_parts


def gmlp_fwd(proj, ws, bs_t, ln_g, ln_b, name, tm=1024, out_width=MAIN_WIDTH):
    s = proj.shape[0]

    def body(u_ref, v_ref, w_ref, b_ref, g_ref, bb_ref, o_ref):
        bias = b_ref[...]
        for c in range(tm // CHUNK):
            rows = slice(c * CHUNK, (c + 1) * CHUNK)
            gu, _, _, _, _, _, s_parts = _gmlp_forward_chunk(u_ref[rows, :], v_ref[rows, :], w_ref, bias, g_ref[...], bb_ref[...])
            for g in range(A_GROUPS):
                cols = slice(g * CHUNK, (g + 1) * CHUNK)
                o_ref[rows, cols] = (gu[:, cols] * s_parts[g]).astype(bf16)

    vec = pl.BlockSpec((1, MAIN_WIDTH), lambda i: (0, 0))
    return pl.pallas_call(
        body, grid=(s // tm,),
        in_specs=[pl.BlockSpec((tm, MAIN_WIDTH), lambda i: (i, 0)), pl.BlockSpec((tm, MAIN_WIDTH), lambda i: (i, 1)),
                  pl.BlockSpec((A_GROUPS, CHUNK, CHUNK), lambda i: (0, 0, 0)), pl.BlockSpec((CHUNK, A_GROUPS), lambda i: (0, 0)), vec, vec],
        out_specs=pl.BlockSpec((tm, MAIN_WIDTH), lambda i: (i, 0)), out_shape=SDS((s, out_width), bf16), name=name,
        compiler_params=_cparams("parallel"))(proj, proj, ws, bs_t, ln_g, ln_b)


def gmlp_bwd(proj, d_mixed, ws, ws_t, bs_t, ln_g, ln_b, name, tm=1024, out_width=2 * MAIN_WIDTH):
    s = proj.shape[0]

    def body(u_ref, v_ref, dm_ref, w_ref, wt_ref, b_ref, g_ref, bb_ref, duv_ref, dw_ref, db_ref, dg_ref, dbb_ref):
        @pl.when(pl.program_id(0) == 0)
        def _():
            dw_ref[...] = jnp.zeros_like(dw_ref)
            db_ref[...] = jnp.zeros_like(db_ref)
            dg_ref[...] = jnp.zeros_like(dg_ref)
            dbb_ref[...] = jnp.zeros_like(dbb_ref)

        bias = b_ref[...]
        ln_gv = g_ref[...]
        row = lax.broadcasted_iota(jnp.int32, (CHUNK, CHUNK), 0)
        col = lax.broadcasted_iota(jnp.int32, (CHUNK, CHUNK), 1)
        lane = lax.broadcasted_iota(jnp.int32, (CHUNK, LANES), 1)
        for c in range(tm // CHUNK):
            rows = slice(c * CHUNK, (c + 1) * CHUNK)
            u = u_ref[rows, :]
            v = v_ref[rows, :]
            gu, tu, tv, rstd, xhat, vln, s_parts = _gmlp_forward_chunk(u, v, w_ref, bias, ln_gv, bb_ref[...])
            dm = dm_ref[rows, :]
            d_vln_parts = []
            d_gu_parts = []
            db_acc = jnp.zeros((CHUNK, LANES), f32)
            for g in range(A_GROUPS):
                cols = slice(g * CHUNK, (g + 1) * CHUNK)
                dmg = dm[:, cols]
                d_gu_parts.append(dmg * s_parts[g])
                d_s = dmg * gu[:, cols]
                db_acc = db_acc + jnp.where(lane == g, jnp.sum(d_s, axis=-1, keepdims=True), 0.0)
                d_sb = d_s.astype(bf16)
                dw_ref[g] += jnp.where(col <= row, _dot_nt(d_sb, vln[:, cols].astype(bf16)), 0.0)
                wt = jnp.where(row <= col, wt_ref[g], jnp.zeros((), bf16))
                d_vln_parts.append(_dot(wt, d_sb))
            db_ref[...] += db_acc
            d_vln = jnp.concatenate(d_vln_parts, axis=-1)
            d_gu = jnp.concatenate(d_gu_parts, axis=-1)
            dg_ref[...] += jnp.sum(d_vln * xhat, axis=0, keepdims=True)
            dbb_ref[...] += jnp.sum(d_vln, axis=0, keepdims=True)
            dxh = d_vln * ln_gv
            m1 = jnp.sum(dxh, axis=-1, keepdims=True) * (1.0 / MAIN_WIDTH)
            m2 = jnp.sum(dxh * xhat, axis=-1, keepdims=True) * (1.0 / MAIN_WIDTH)
            d_gv = rstd * (dxh - m1 - xhat * m2)
            duv_ref[rows, :MAIN_WIDTH] = (d_gu * _gelu_grad(u, tu)).astype(bf16)
            duv_ref[rows, MAIN_WIDTH:] = (d_gv * _gelu_grad(v, tv)).astype(bf16)

    vec = pl.BlockSpec((1, MAIN_WIDTH), lambda i: (0, 0))
    wspec = pl.BlockSpec((A_GROUPS, CHUNK, CHUNK), lambda i: (0, 0, 0))
    return pl.pallas_call(
        body, grid=(s // tm,),
        in_specs=[pl.BlockSpec((tm, MAIN_WIDTH), lambda i: (i, 0)), pl.BlockSpec((tm, MAIN_WIDTH), lambda i: (i, 1)),
                  pl.BlockSpec((tm, MAIN_WIDTH), lambda i: (i, 0)), wspec, wspec, pl.BlockSpec((CHUNK, A_GROUPS), lambda i: (0, 0)), vec, vec],
        out_specs=[pl.BlockSpec((tm, 2 * MAIN_WIDTH), lambda i: (i, 0)), wspec, pl.BlockSpec((CHUNK, LANES), lambda i: (0, 0)), vec, vec],
        out_shape=[SDS((s, out_width), bf16), SDS((A_GROUPS, CHUNK, CHUNK), f32), SDS((CHUNK, LANES), f32),
                   SDS((1, MAIN_WIDTH), f32), SDS((1, MAIN_WIDTH), f32)],
        name=name, compiler_params=_cparams("arbitrary"))(proj, proj, d_mixed, ws, ws_t, bs_t, ln_g, ln_b)


def _head_mask(width, h):
    lane = lax.broadcasted_iota(jnp.int32, (1, width), 1)
    return (lane >= h * HEAD_DIM) & (lane < (h + 1) * HEAD_DIM)


def mem_attn_fwd(proj, q_block, kv, into, name, tm=1024):
    s = proj.shape[0]
    n_mem = kv.shape[0]
    out_block = into.shape[1] // MEM_WIDTH - 1

    def body(q_ref, kv_ref, into_ref, o_ref):
        q = q_ref[...].astype(f32)
        k = kv_ref[:, :MEM_WIDTH].astype(bf16)
        v = kv_ref[:, MEM_WIDTH:].astype(bf16)
        out = jnp.zeros((tm, MEM_WIDTH), f32)
        for h in range(MEM_HEADS):
            msk = _head_mask(MEM_WIDTH, h)
            qh = jnp.where(msk, q, 0.0).astype(bf16)
            sc = _dot_nt(qh, k) * ATT_SCALE
            e = jnp.exp(sc - jnp.max(sc, axis=-1, keepdims=True))
            p = e / jnp.sum(e, axis=-1, keepdims=True)
            out = jnp.where(msk, _dot(p.astype(bf16), v), out)
        o_ref[...] = out.astype(bf16)

    return pl.pallas_call(body, grid=(s // tm,),
                          in_specs=[pl.BlockSpec((tm, MEM_WIDTH), lambda i: (i, q_block)), pl.BlockSpec((n_mem, 2 * MEM_WIDTH), lambda i: (0, 0)), _ANY],
                          out_specs=pl.BlockSpec((tm, MEM_WIDTH), lambda i: (i, out_block)), out_shape=SDS(into.shape, bf16), name=name,
                          input_output_aliases={2: 0}, compiler_params=_cparams("parallel"))(proj, kv, into)


def mem_attn_bwd(proj, q_block, kv, d_mixed, into, name, tm=1024):
    s = proj.shape[0]
    n_mem = kv.shape[0]
    out_block = into.shape[1] // MEM_WIDTH - 1

    def body(q_ref, kv_ref, do_ref, into_ref, dq_ref, dkv_ref):
        @pl.when(pl.program_id(0) == 0)
        def _():
            dkv_ref[...] = jnp.zeros_like(dkv_ref)

        q = q_ref[...].astype(f32)
        do = do_ref[...]
        k = kv_ref[:, :MEM_WIDTH].astype(bf16)
        v = kv_ref[:, MEM_WIDTH:].astype(bf16)
        dq = jnp.zeros((tm, MEM_WIDTH), f32)
        dk = jnp.zeros((n_mem, MEM_WIDTH), f32)
        dv = jnp.zeros((n_mem, MEM_WIDTH), f32)
        for h in range(MEM_HEADS):
            msk = _head_mask(MEM_WIDTH, h)
            qh = jnp.where(msk, q, 0.0).astype(bf16)
            doh = jnp.where(msk, do, 0.0).astype(bf16)
            sc = _dot_nt(qh, k) * ATT_SCALE
            e = jnp.exp(sc - jnp.max(sc, axis=-1, keepdims=True))
            p = e / jnp.sum(e, axis=-1, keepdims=True)
            dp = _dot_nt(doh, v)
            ds = p * (dp - jnp.sum(dp * p, axis=-1, keepdims=True))
            dsb = (ds * ATT_SCALE).astype(bf16)
            dq = jnp.where(msk, _dot(dsb, k), dq)
            dk = dk + _dot_tn(dsb, qh)
            dv = dv + _dot_tn(p.astype(bf16), doh)
        dq_ref[...] = dq.astype(bf16)
        dkv_ref[:, :MEM_WIDTH] += dk
        dkv_ref[:, MEM_WIDTH:] += dv

    return pl.pallas_call(
        body, grid=(s // tm,),
        in_specs=[pl.BlockSpec((tm, MEM_WIDTH), lambda i: (i, q_block)), pl.BlockSpec((n_mem, 2 * MEM_WIDTH), lambda i: (0, 0)),
                  pl.BlockSpec((tm, MEM_WIDTH), lambda i: (i, MAIN_WIDTH // MEM_WIDTH)), _ANY],
        out_specs=[pl.BlockSpec((tm, MEM_WIDTH), lambda i: (i, out_block)), pl.BlockSpec((n_mem, 2 * MEM_WIDTH), lambda i: (0, 0))],
        out_shape=[SDS(into.shape, bf16), SDS((n_mem, 2 * MEM_WIDTH), f32)], name=name,
        input_output_aliases={3: 0}, compiler_params=_cparams("arbitrary"))(proj, kv, d_mixed, into)


def _tri(t, upper):
    r = lax.broadcasted_iota(jnp.int32, (t, t), 0)
    c = lax.broadcasted_iota(jnp.int32, (t, t), 1)
    return ((r <= c) if upper else (r >= c)).astype(f32)


def fgate_fwd(z_t, b, name, t=512):
    hh, s = z_t.shape

    def body(z_ref, b_ref, c_ref):
        u = _tri(t, True)
        carry = jnp.zeros((hh, 1), f32)
        for blk in range(s // t):
            x = z_ref[:, blk * t:(blk + 1) * t] + b_ref[...]
            logf = jnp.minimum(x, 0.0) - jnp.log(1.0 + jnp.exp(-jnp.abs(x)))
            y = jnp.dot(logf, u, precision=lax.Precision.HIGHEST, preferred_element_type=f32) + carry
            c_ref[:, blk * t:(blk + 1) * t] = y
            carry = y[:, t - 1:t]

    return pl.pallas_call(body, out_shape=SDS((hh, s), f32), name=name, compiler_params=_cparams())(z_t, b)


def fgate_bwd(dc_t, z_t, b, name, t=512):
    hh, s = z_t.shape

    def body(dc_ref, z_ref, b_ref, dz_ref, db_ref):
        low = _tri(t, False)
        carry = jnp.zeros((hh, 1), f32)
        total = jnp.zeros((hh, 1), f32)
        for blk in reversed(range(s // t)):
            cols = slice(blk * t, (blk + 1) * t)
            y = jnp.dot(dc_ref[:, cols], low, precision=lax.Precision.HIGHEST, preferred_element_type=f32) + carry
            carry = y[:, 0:1]
            dz = y * _sigmoid(-(z_ref[:, cols] + b_ref[...]))
            dz_ref[:, cols] = dz
            total = total + jnp.sum(dz, axis=-1, keepdims=True)
        db_ref[...] = jnp.broadcast_to(total, db_ref.shape)

    return pl.pallas_call(body, out_shape=[SDS((hh, s), f32), SDS((hh, LANES), f32)], name=name,
                          compiler_params=_cparams())(dc_t, z_t, b)


def _pair_masks():
    lane = lax.broadcasted_iota(jnp.int32, (1, LANES), 1)
    return [lane < HEAD_DIM, lane >= HEAD_DIM]


def _tile_base(cr_ref, hh, lo):
    return cr_ref[hh:hh + 1, pl.ds(lo, LANES)][:, 0:1]


def fox_fwd(q, kv, c_row, name, tq=512, out_width=MAIN_WIDTH):
    s = kv.shape[0]
    nq = s // tq

    def body(q_ref, k_ref, v_ref, cr_ref, o_ref, lse_ref, ob_ref):
        i = pl.program_id(1)
        qv = q_ref[...]
        masks = _pair_masks()
        row = lax.broadcasted_iota(jnp.int32, (tq, tq), 0)
        col = lax.broadcasted_iota(jnp.int32, (tq, tq), 1)
        qh = [jnp.where(masks[hh], qv, jnp.zeros((), bf16)) * ATT_SCALE for hh in range(2)]
        ct = [_tile_base(cr_ref, hh, pl.multiple_of(i * tq, tq)) for hh in range(2)]

        def block(j, carry, diag):
            lo = pl.multiple_of(j * tq, tq)
            ks = k_ref[pl.ds(lo, tq), :]
            vs = v_ref[pl.ds(lo, tq), :]
            out = []
            for hh in range(2):
                m, l, acc = carry[hh]
                sc = _dot_nt(qh[hh], ks) + (ct[hh] - cr_ref[hh:hh + 1, pl.ds(lo, tq)])
                if diag:
                    sc = jnp.where(col <= row, sc, -jnp.inf)
                m_new = jnp.maximum(m, jnp.max(sc, axis=-1, keepdims=True))
                alpha = jnp.exp(m - m_new)
                p = jnp.exp(sc - m_new)
                l = alpha * l + jnp.sum(p, axis=-1, keepdims=True)
                p_hi = p.astype(bf16)
                p_lo = (p - p_hi.astype(f32)).astype(bf16)
                acc = alpha * acc + (_dot(p_hi, vs) + _dot(p_lo, vs))
                out.append((m_new, l, acc))
            return tuple(out)

        init = (jnp.full((tq, 1), -jnp.inf, f32), jnp.zeros((tq, 1), f32), jnp.zeros((tq, LANES), f32))
        carry = lax.fori_loop(0, i, functools.partial(block, diag=False), (init, init))
        res = [(acc / l, m + jnp.log(l)) for m, l, acc in block(i, carry, True)]
        out = jnp.where(masks[0], res[0][0], res[1][0])
        o_ref[...] = out
        ob_ref[...] = out.astype(bf16)
        lse_ref[...] = jnp.where(masks[0], res[0][1], res[1][1])

    return pl.pallas_call(
        body, grid=(FOX_PAIRS, nq),
        in_specs=[pl.BlockSpec((tq, LANES), lambda p, i: (i, p)), pl.BlockSpec((s, LANES), lambda p, i: (0, p)),
                  pl.BlockSpec((s, LANES), lambda p, i: (0, FOX_PAIRS + p)), pl.BlockSpec((None, 2, s), lambda p, i: (p, 0, 0))],
        out_specs=[pl.BlockSpec((tq, LANES), lambda p, i: (i, p)), pl.BlockSpec((None, tq, LANES), lambda p, i: (p, i, 0)),
                   pl.BlockSpec((tq, LANES), lambda p, i: (i, p))],
        out_shape=[SDS((s, MAIN_WIDTH), f32), SDS((FOX_PAIRS, s, LANES), f32), SDS((s, out_width), bf16)], name=name,
        compiler_params=_cparams("parallel", "parallel"))(q, kv, kv, c_row)


def fox_bwd(q, kv, d_mixed, o, lse, c_row, name, tq=512, dq_width=MAIN_WIDTH):
    s = kv.shape[0]
    nq = s // tq

    def body(q_ref, k_ref, v_ref, do_ref, o_ref, lse_ref, cr_ref, dqb_ref, dk_ref, dv_ref, dc_ref, dq_ref):
        j = pl.program_id(1)

        @pl.when(j == 0)
        def _():
            dq_ref[...] = jnp.zeros_like(dq_ref)

        masks = _pair_masks()
        sub = lax.broadcasted_iota(jnp.int32, (LANES, 1), 0)
        sub_masks = [sub < HEAD_DIM, sub >= HEAD_DIM]
        row = lax.broadcasted_iota(jnp.int32, (tq, tq), 0)
        col = lax.broadcasted_iota(jnp.int32, (tq, tq), 1)
        kj = k_ref[...]
        vj = v_ref[...]
        lo_j = pl.multiple_of(j * tq, tq)

        def block(i, carry, diag):
            dk_t, dv_t, dc0, dc1 = carry
            dcs = [dc0, dc1]
            lo = pl.multiple_of(i * tq, tq)
            qi = q_ref[pl.ds(lo, tq), :]
            qi = qi * ATT_SCALE
            qt_i = qi.T
            doi = do_ref[pl.ds(lo, tq), :]
            dot_i = doi.astype(bf16).T
            prod = doi.astype(bf16).astype(f32) * o_ref[pl.ds(lo, tq), :]
            lse_i = lse_ref[pl.ds(lo, tq), :]
            dq_i = jnp.zeros((tq, LANES), f32)
            for hh in range(2):
                qh = jnp.where(masks[hh], qi, jnp.zeros((), bf16))
                doh = jnp.where(masks[hh], doi, 0.0).astype(bf16)
                delta = jnp.sum(jnp.where(masks[hh], prod, 0.0), axis=-1, keepdims=True)
                sc = _dot_nt(qh, kj) + (_tile_base(cr_ref, hh, lo) - cr_ref[hh:hh + 1, pl.ds(lo_j, tq)])
                p = jnp.exp(sc - lse_i[:, hh * HEAD_DIM:hh * HEAD_DIM + 1])
                if diag:
                    p = jnp.where(col <= row, p, 0.0)
                dv_t = dv_t + _dot(jnp.where(sub_masks[hh], dot_i, jnp.zeros((), bf16)), p.astype(bf16))
                ds = p * (_dot_nt(doh, vj) - delta)
                dcs[hh] = dcs[hh] + jnp.sum(ds, axis=0, keepdims=True)
                dsb = ds.astype(bf16)
                dq_i = jnp.where(masks[hh], _dot(dsb, kj), dq_i)
                dk_t = dk_t + _dot(jnp.where(sub_masks[hh], qt_i, jnp.zeros((), bf16)), dsb)
            dq_ref[pl.ds(lo, tq), :] += dq_i * ATT_SCALE
            return dk_t, dv_t, dcs[0], dcs[1]

        zero = jnp.zeros((LANES, tq), f32)
        zrow = jnp.zeros((1, tq), f32)
        carry = block(j, (zero, zero, zrow, zrow), True)
        dk_t, dv_t, dc0, dc1 = lax.fori_loop(j + 1, nq, functools.partial(block, diag=False), carry)
        dk_ref[...] = dk_t.T.astype(bf16)
        dv_ref[...] = dv_t.T.astype(bf16)
        dc_ref[0:1, :] = -dc0
        dc_ref[1:2, :] = -dc1

        @pl.when(j == nq - 1)
        def _():
            dqb_ref[...] = dq_ref[...].astype(bf16)

    full = lambda p, j: (0, p)
    tile = lambda p, j: (j, p)
    return pl.pallas_call(
        body, grid=(FOX_PAIRS, nq),
        in_specs=[pl.BlockSpec((s, LANES), full), pl.BlockSpec((tq, LANES), tile), pl.BlockSpec((tq, LANES), lambda p, j: (j, FOX_PAIRS + p)),
                  pl.BlockSpec((s, LANES), full), pl.BlockSpec((s, LANES), full), pl.BlockSpec((None, s, LANES), lambda p, j: (p, 0, 0)),
                  pl.BlockSpec((None, 2, s), lambda p, j: (p, 0, 0))],
        out_specs=[pl.BlockSpec((s, LANES), full), pl.BlockSpec((tq, LANES), tile), pl.BlockSpec((tq, LANES), tile),
                   pl.BlockSpec((None, 2, tq), lambda p, j: (p, 0, j))],
        out_shape=[SDS((s, dq_width), bf16), SDS((s, MAIN_WIDTH), bf16), SDS((s, MAIN_WIDTH), bf16), SDS((FOX_PAIRS, 2, s), f32)],
        scratch_shapes=[pltpu.VMEM((s, LANES), f32)],
        name=name, compiler_params=_cparams("parallel", "arbitrary"))(q, kv, kv, d_mixed, o, lse, c_row)


def adamw(w, g, m, v, name, tr=256):
    r, c = w.shape
    tr = min(tr, r)
    assert r % tr == 0, (name, r, tr)
    c1 = 1.0 / (1.0 - ADAM_B1 ** ADAM_STEP)
    c2 = 1.0 / (1.0 - ADAM_B2 ** ADAM_STEP)

    def body(w_ref, g_ref, m_ref, v_ref, d_ref, mo_ref, vo_ref):
        gv = g_ref[...]
        mn = ADAM_B1 * m_ref[...] + (1.0 - ADAM_B1) * gv
        vn = ADAM_B2 * v_ref[...] + (1.0 - ADAM_B2) * gv * gv
        mo_ref[...] = mn
        vo_ref[...] = vn
        d_ref[...] = -ADAM_LR * ((mn * c1) / (jnp.sqrt(vn * c2) + ADAM_EPS) + ADAM_WD * w_ref[...])

    spec = pl.BlockSpec((tr, c), lambda i: (i, 0))
    return pl.pallas_call(body, grid=(r // tr,), in_specs=[spec] * 4, out_specs=[spec] * 3, out_shape=[SDS((r, c), f32)] * 3,
                          name=name, compiler_params=_cparams("parallel"))(w, g, m, v)


def adamw_owned(w, parts, m, v, name, tr):
    nl, r, c = w.shape
    cp = parts[0].shape[2]
    assert r % tr == 0 and len(parts) == nl, (name, r, tr)
    c1 = 1.0 / (1.0 - ADAM_B1 ** ADAM_STEP)
    c2 = 1.0 / (1.0 - ADAM_B2 ** ADAM_STEP)

    def body(*refs):
        w_ref, p_refs, (m_ref, v_ref) = refs[0], refs[1:1 + nl], refs[1 + nl:3 + nl]
        g_ref, d_ref, mo_ref, vo_ref = refs[3 + nl:]
        layer = pl.program_id(0)

        def total(p_ref):
            acc = p_ref[0].astype(f32)
            for k in range(1, N_DEV):
                acc = acc + p_ref[k].astype(f32)
            return acc

        gv = total(p_refs[0])
        for l in range(1, nl):
            gv = jnp.where(layer == l, total(p_refs[l]), gv)
        gv = gv[:, :c]
        g_ref[...] = gv
        mn = ADAM_B1 * m_ref[...] + (1.0 - ADAM_B1) * gv
        vn = ADAM_B2 * v_ref[...] + (1.0 - ADAM_B2) * gv * gv
        mo_ref[...] = mn
        vo_ref[...] = vn
        d_ref[...] = -ADAM_LR * ((mn * c1) / (jnp.sqrt(vn * c2) + ADAM_EPS) + ADAM_WD * w_ref[...])

    spec = pl.BlockSpec((None, tr, c), lambda l, i: (l, i, 0))
    last = r // tr - 1

    def part_spec(mine):
        return pl.BlockSpec((N_DEV, tr, cp), lambda l, i: (0, jnp.where(l == mine, i, jnp.where(l < mine, 0, last)), 0))

    return pl.pallas_call(body, grid=(nl, r // tr), in_specs=[spec] + [part_spec(l) for l in range(nl)] + [spec, spec], out_specs=[spec] * 4,
                          out_shape=[SDS((nl, r, c), f32)] * 4, name=name,
                          compiler_params=_cparams("parallel", "parallel"))(w, *parts, m, v)


def sum_leading(x, name, out_dtype=f32, tr=None):
    n, r, c = x.shape
    tr = tr or r
    assert r % tr == 0

    def body(x_ref, o_ref):
        acc = x_ref[0].astype(f32)
        for k in range(1, n):
            acc = acc + x_ref[k].astype(f32)
        o_ref[...] = acc.astype(out_dtype)

    return pl.pallas_call(body, grid=(r // tr,), in_specs=[pl.BlockSpec((n, tr, c), lambda i: (0, i, 0))],
                          out_specs=pl.BlockSpec((tr, c), lambda i: (i, 0)), out_shape=SDS((r, c), out_dtype), name=name,
                          compiler_params=_cparams("parallel"))(x)


_ANY = pl.BlockSpec(memory_space=pl.ANY)
_DMA = pltpu.SemaphoreType.DMA


_HBM = pl.BlockSpec(memory_space=pltpu.HBM)
_SEM = pl.BlockSpec(memory_space=pltpu.SEMAPHORE)
_EFFECT = pltpu.SideEffectType.DATAFLOW_SIDE_EFFECTING
_FLIPS = [(0, 0, 1), (1, 0, 0), (0, 1, 0), (1, 1, 0), (1, 0, 1), (0, 1, 1), (1, 1, 1)]


def _me():
    return lax.axis_index("x"), lax.axis_index("y"), lax.axis_index("c")


def _peers():
    mx, my, mc = _me()
    return [(jnp.bitwise_xor(mx, fx), jnp.bitwise_xor(my, fy), jnp.bitwise_xor(mc, fc)) for fx, fy, fc in _FLIPS]


def _index(dev):
    return 4 * dev[0] + 2 * dev[1] + dev[2]


def _win(ref, axis, k, size, count=1):
    idx = [slice(None)] * len(ref.shape)
    idx[axis] = pl.ds(k * size, count * size)
    return ref.at[tuple(idx)]


def _hbm(a):
    return pltpu.with_memory_space_constraint(a, pltpu.HBM)


def _exchange_start(srcs, lands, copies_of, name):
    n = len(srcs)

    def body(*refs):
        src = refs[:n]
        send_sems, recv_sems, self_sems = refs[2 * n:2 * n + 3]
        land = refs[3 * n + 3:4 * n + 3]
        token = refs[4 * n + 3]
        me = _index(_me())
        for a in range(n):
            for s_ref, d_ref, peer in copies_of(a, src[a], land[a], me):
                if peer is None:
                    pltpu.make_async_copy(s_ref, d_ref, self_sems.at[a]).start()
                else:
                    pltpu.make_async_remote_copy(src_ref=s_ref, dst_ref=d_ref, send_sem=send_sems.at[a], recv_sem=recv_sems.at[a],
                                                 device_id=peer, device_id_type=MESH).start()
        token[...] = jnp.zeros_like(token)

    outs = pl.pallas_call(
        body, name=name,
        out_shape=(_DMA((n,)), _DMA((n,)), _DMA((n,)), *[pltpu.HBM(s.shape, s.dtype) for s in srcs],
                   *[pltpu.HBM(l.shape, l.dtype) for l in lands], SDS((8, LANES), f32)),
        in_specs=[_HBM] * (2 * n), out_specs=(_SEM, _SEM, _SEM, *[_HBM] * (2 * n), pl.BlockSpec(memory_space=pltpu.VMEM)),
        input_output_aliases={i: 3 + i for i in range(2 * n)},
        compiler_params=pltpu.CompilerParams(has_side_effects=_EFFECT),
    )(*[_hbm(s) for s in srcs], *[_hbm(lax.empty(l.shape, l.dtype)) for l in lands])
    return dict(sems=outs[:3], srcs=list(outs[3:3 + n]), lands=list(outs[3 + n:3 + 2 * n]), token=outs[3 + 2 * n])


def _exchange_wait(started, waits_of, after, name, which=None):
    which = list(range(len(started["srcs"]))) if which is None else which
    srcs, lands = [started["srcs"][a] for a in which], [started["lands"][a] for a in which]
    n = len(which)

    def body(*refs):
        src = refs[:n]
        land = refs[n:2 * n]
        send_sems, recv_sems, self_sems = refs[2 * n:2 * n + 3]
        me = _index(_me())
        for pos, a in enumerate(which):
            seven, (s_ref, d_ref) = waits_of(a, src[pos], land[pos], me)
            both = pltpu.make_async_remote_copy(src_ref=seven, dst_ref=seven, send_sem=send_sems.at[a], recv_sem=recv_sems.at[a],
                                                device_id=_me(), device_id_type=MESH)
            both.wait_send()
            both.wait_recv()
            pltpu.make_async_copy(s_ref, d_ref, self_sems.at[a]).wait()

    outs = pl.pallas_call(
        body, name=name, out_shape=tuple(pltpu.HBM(t.shape, t.dtype) for t in srcs + lands),
        in_specs=[_HBM] * (2 * n) + [_SEM] * 3 + [_ANY], out_specs=tuple([_HBM] * (2 * n)),
        input_output_aliases={i: i for i in range(2 * n)},
        compiler_params=pltpu.CompilerParams(has_side_effects=_EFFECT),
    )(*srcs, *lands, *started["sems"], after)
    return list(outs[n:])


def gather_start(locs, axes, name):
    lands = [SDS(tuple(N_DEV * d if i == ax else d for i, d in enumerate(l.shape)), l.dtype) for l, ax in zip(locs, axes)]

    def copies_of(a, src, land, me):
        mine = _win(land, axes[a], me, src.shape[axes[a]])
        return [(src, mine, peer) for peer in _peers()] + [(src, mine, None)]

    return _exchange_start(locs, lands, copies_of, name)


def gather_wait(started, axes, after, name, which=None):
    def waits_of(a, src, land, me):
        size = src.shape[axes[a]]
        return _win(land, axes[a], 0, size, N_DEV - 1), (src, _win(land, axes[a], me, size))

    return _exchange_wait(started, waits_of, after, name, which)


def _part(ref, axis, k, stride, used):
    idx = [slice(None)] * len(ref.shape)
    idx[axis] = pl.ds(k * stride, used)
    return ref.at[tuple(idx)]


def scatter_start(grads, axes, name, used=None):
    strides = [g.shape[ax] // N_DEV for g, ax in zip(grads, axes)]
    used = used or strides
    lands = [SDS((N_DEV,) + tuple(u if i == ax else d for i, d in enumerate(g.shape)), g.dtype) for g, ax, u in zip(grads, axes, used)]

    def copies_of(a, src, land, me):
        out = [(_part(src, axes[a], _index(peer), strides[a], used[a]), land.at[me], peer) for peer in _peers()]
        return out + [(_part(src, axes[a], me, strides[a], used[a]), land.at[me], None)]

    return _exchange_start(grads, lands, copies_of, name)


def scatter_wait(started, axes, after, name, used=None):
    def waits_of(a, src, land, me):
        stride = src.shape[axes[a]] // N_DEV
        return land.at[pl.ds(0, N_DEV - 1)], (_part(src, axes[a], me, stride, used[a] if used else stride), land.at[me])

    return _exchange_wait(started, waits_of, after, name)


def _row_tile(rows, cap=512):
    return max(t for t in range(8, min(rows, cap) + 1, 8) if rows % t == 0)


_SMALL = [
    ("ln_mix_pre", (2, 1024)), ("ln_mix_post", (2, 1024)), ("ln_ffn_pre", (2, 1024)), ("ln_ffn_post", (2, 1024)),
    ("ln_mem", (2, 1024)), ("w_spatial", (1, 6, 128, 128)), ("b_spatial", (1, 6, 128)), ("ln_shared", (1024,)),
    ("b_forget", (12,)), ("ln_v_g", (1, 768)), ("ln_v_b", (1, 768)),
]
_SMALL_TILE = 8 * LANES


def _small_rows(shape):
    return -(-math.prod(shape) // _SMALL_TILE) * 8


def _pack_small(vals, shapes):
    parts = []
    for name, shape in shapes:
        flat = vals[name].reshape(-1).astype(f32)
        rows = _small_rows(shape)
        parts.append(jnp.pad(flat, (0, rows * LANES - flat.shape[0])).reshape(rows, LANES))
    return jnp.concatenate(parts, axis=0)


def _unpack_small(buf, shapes):
    out = {}
    lo = 0
    for name, shape in shapes:
        rows = _small_rows(shape)
        out[name] = buf[lo:lo + rows].reshape(-1)[:math.prod(shape)].reshape(shape)
        lo += rows
    return out


def kernel(x, mem, ln_mix_pre, ln_mix_post, ln_ffn_pre, ln_ffn_post, ln_mem, w_mem_kv, w_out, w_ffn_gate, w_ffn_up, w_ffn_down, w_in_a, w_spatial, b_spatial, ln_v_g, ln_v_b, ln_shared, w_shared_kv, b_forget, w_in_b, loss_target, m_ln_mix_pre, m_ln_mix_post, m_ln_ffn_pre, m_ln_ffn_post, m_ln_mem, m_w_mem_kv, m_w_out, m_w_ffn_gate, m_w_ffn_up, m_w_ffn_down, m_w_in_a, m_w_spatial, m_b_spatial, m_ln_v_g, m_ln_v_b, m_ln_shared, m_w_shared_kv, m_b_forget, m_w_in_b, v_ln_mix_pre, v_ln_mix_post, v_ln_ffn_pre, v_ln_ffn_post, v_ln_mem, v_w_mem_kv, v_w_out, v_w_ffn_gate, v_w_ffn_up, v_w_ffn_down, v_w_in_a, v_w_spatial, v_b_spatial, v_ln_v_g, v_ln_v_b, v_ln_shared, v_w_shared_kv, v_b_forget, v_w_in_b):
    weights = dict(ln_mix_pre=ln_mix_pre, ln_mix_post=ln_mix_post, ln_ffn_pre=ln_ffn_pre, ln_ffn_post=ln_ffn_post, ln_mem=ln_mem,
                   w_mem_kv=w_mem_kv, w_out=w_out, w_ffn_gate=w_ffn_gate, w_ffn_up=w_ffn_up, w_ffn_down=w_ffn_down, w_in_a=w_in_a,
                   w_spatial=w_spatial, b_spatial=b_spatial, ln_v_g=ln_v_g, ln_v_b=ln_v_b, ln_shared=ln_shared,
                   w_shared_kv=w_shared_kv, b_forget=b_forget, w_in_b=w_in_b)
    mom_m = dict(ln_mix_pre=m_ln_mix_pre, ln_mix_post=m_ln_mix_post, ln_ffn_pre=m_ln_ffn_pre, ln_ffn_post=m_ln_ffn_post, ln_mem=m_ln_mem,
                 w_mem_kv=m_w_mem_kv, w_out=m_w_out, w_ffn_gate=m_w_ffn_gate, w_ffn_up=m_w_ffn_up, w_ffn_down=m_w_ffn_down, w_in_a=m_w_in_a,
                 w_spatial=m_w_spatial, b_spatial=m_b_spatial, ln_v_g=m_ln_v_g, ln_v_b=m_ln_v_b, ln_shared=m_ln_shared,
                 w_shared_kv=m_w_shared_kv, b_forget=m_b_forget, w_in_b=m_w_in_b)
    mom_v = dict(ln_mix_pre=v_ln_mix_pre, ln_mix_post=v_ln_mix_post, ln_ffn_pre=v_ln_ffn_pre, ln_ffn_post=v_ln_ffn_post, ln_mem=v_ln_mem,
                 w_mem_kv=v_w_mem_kv, w_out=v_w_out, w_ffn_gate=v_w_ffn_gate, w_ffn_up=v_w_ffn_up, w_ffn_down=v_w_ffn_down, w_in_a=v_w_in_a,
                 w_spatial=v_w_spatial, b_spatial=v_b_spatial, ln_v_g=v_ln_v_g, ln_v_b=v_ln_v_b, ln_shared=v_ln_shared,
                 w_shared_kv=v_w_shared_kv, b_forget=v_b_forget, w_in_b=v_w_in_b)
    names = list(weights)
    mx, my, mc = lax.axis_index("x"), lax.axis_index("y"), lax.axis_index("c")
    me = 4 * mx + 2 * my + mc

    h0 = x[0]
    mem0 = mem[0]
    tgt = loss_target[0]
    seq = h0.shape[0]

    vec = lambda a: a.reshape(1, -1)
    pad_to = lambda a, axis, size: jnp.pad(a, [(0, size - a.shape[i] if i == axis else 0) for i in range(a.ndim)])

    def after(tok, a):
        return a + tok[0, 0].astype(a.dtype)

    lnv_loc = pad_to(jnp.concatenate([ln_v_g, ln_v_b], axis=0), 0, 8)
    st_a = gather_start([w_in_a.astype(bf16), pad_to(lnv_loc, 1, LANES)[None]], [0, 0], "gather_a_start")
    mix_locs = lambda l, tok: [after(tok, w_mem_kv[l]).astype(bf16), w_out[l].astype(bf16)]

    def ffn_gather_start(l, tok):
        gate_up = gather_start([pad_to(after(tok, w_ffn_gate[l]).astype(bf16), 1, FF_SHARD_PAD),
                                pad_to(w_ffn_up[l].astype(bf16), 1, FF_SHARD_PAD)], [1, 1], f"gather_gate_up{l}_start")
        down = gather_start([pad_to(after(gate_up["token"], w_ffn_down[l]).astype(bf16), 0, FF_SHARD_PAD)], [0], f"gather_down{l}_start")
        return gate_up, down

    st_b = [gather_start(mix_locs(0, st_a["token"]), [0, 0], "gather_b0_start"), None]
    st_c = ffn_gather_start(0, st_b[0]["token"])
    st_d = gather_start([after(st_c[1]["token"], w_in_b[0]).astype(bf16), pad_to(w_shared_kv.astype(bf16), 1, KV_PAD)], [0, 0],
                        "gather_d_start")
    st_b[1] = gather_start(mix_locs(1, st_d["token"]), [0, 0], "gather_b1_start")
    st_e = ffn_gather_start(1, st_b[1]["token"])
    ws = w_spatial[0].astype(bf16)
    ws_t = ws.transpose(0, 2, 1)
    bs_t = b_spatial[0].T

    (a0,) = rms_fwd(h0, [after(st_e[1]["token"], vec(ln_mix_pre[0]))], "a0_norm")
    w_in_a8, lnv8 = gather_wait(st_a, [0, 0], a0, "gather_a_wait")
    w_in_a_full = w_in_a8.transpose(1, 0, 2).reshape(D_MODEL, -1)
    lnv_g = lnv8[:, 0, :MAIN_WIDTH // N_DEV].reshape(1, MAIN_WIDTH)
    lnv_b = lnv8[:, 1, :MAIN_WIDTH // N_DEV].reshape(1, MAIN_WIDTH)
    proj0 = mm(a0, w_in_a_full, "proj0", tn=896)
    main0 = gmlp_fwd(proj0, ws, bs_t, lnv_g, lnv_b, "gmlp_fwd", out_width=D_MODEL)
    w_mkv, w_o = [None, None], [None, None]
    w_mkv[0], w_o[0] = gather_wait(st_b[0], [0, 0], main0, "gather_b0_wait")
    (memn0,) = rms_fwd(mem0, [vec(ln_mem[0])], "mem0_norm")
    kvm0 = mm(memn0, w_mkv[0], "kvm0")
    mixed0 = mem_attn_fwd(proj0, 2 * MAIN_WIDTH // MEM_WIDTH, kvm0, main0, "mem_attn0")
    y1_0, hmid0, f0 = mm_resnorm(mixed0, w_o[0], h0, vec(ln_mix_post[0]), [vec(ln_ffn_pre[0])], "mix_out0")
    w_g0, w_u0 = gather_wait(st_c[0], [1, 1], f0, "gather_gate_up0_wait")
    gu0, act0 = ffn_up(f0, w_g0, w_u0, "ffn_up0")
    (w_d0,) = gather_wait(st_c[1], [0], act0, "gather_down0_wait")
    y2_0, h1, a1, sin1 = mm_resnorm(act0, w_d0, hmid0, vec(ln_ffn_post[0]), [vec(ln_mix_pre[1]), vec(ln_shared)], "ffn_down0")

    w_inb, w_kv = gather_wait(st_d, [0, 0], sin1, "gather_d_wait")
    kvb = mm(sin1, w_kv, "kv_shared", out_dtype=bf16, tn=MAIN_WIDTH, ncols=2 * MAIN_WIDTH)
    zf = mm(sin1, w_kv, "forget_logits", tn=256, col0=2 * MAIN_WIDTH, ncols=256)
    qb = mm(a1, w_inb, "proj1", out_dtype=bf16)
    z_t = jnp.pad(zf[:, :FOX_HEADS].T, ((0, 16 - FOX_HEADS), (0, 0)))
    bf_col = jnp.pad(b_forget, (0, 16 - FOX_HEADS)).reshape(16, 1)
    c_t = fgate_fwd(z_t, bf_col, "fgate_fwd")
    c_row = c_t[:FOX_HEADS].reshape(FOX_PAIRS, 2, seq)
    main1, lse, main1_b = fox_fwd(qb, kvb, c_row, "fox_fwd", out_width=D_MODEL)
    w_mkv[1], w_o[1] = gather_wait(st_b[1], [0, 0], main1, "gather_b1_wait")
    (memn1,) = rms_fwd(mem0, [vec(ln_mem[1])], "mem1_norm")
    kvm1 = mm(memn1, w_mkv[1], "kvm1")
    mixed1 = mem_attn_fwd(qb, MAIN_WIDTH // MEM_WIDTH, kvm1, main1_b, "mem_attn1")
    y1_1, hmid1, f1 = mm_resnorm(mixed1, w_o[1], h1, vec(ln_mix_post[1]), [vec(ln_ffn_pre[1])], "mix_out1")
    w_g1, w_u1 = gather_wait(st_e[0], [1, 1], f1, "gather_gate_up1_wait")
    gu1, act1 = ffn_up(f1, w_g1, w_u1, "ffn_up1")
    (w_d1,) = gather_wait(st_e[1], [0], act1, "gather_down1_wait")
    dh, d_y2_1, dg_fpost1, loss_tile = mm_resnorm_loss(act1, w_d1, hmid1, vec(ln_ffn_post[1]), tgt, "ffn_down1_loss")
    ffn_w = [(w_g0, w_u0, w_d0), (w_g1, w_u1, w_d1)]
    ff_shard = w_ffn_down.shape[1]

    small = {}

    def ffn_backward(layer, dh_out, d_y2, hmid, f, gu, act, y1):
        w_g, w_u, w_d = ffn_w[layer]
        dw_down = mm_tn(act, d_y2, f"dw_down{layer}")
        rs_down = scatter_start([dw_down], [0], f"scatter_down{layer}_start", used=[ff_shard])
        d_g, d_u = ffn_act_grad(d_y2, w_d, gu, f"ffn_act_grad{layer}")
        dw_g = mm_tn(d_g, f, f"dw_gate{layer}", dep=rs_down["token"])
        dw_u = mm_tn(d_u, f, f"dw_up{layer}")
        rs_gate_up = scatter_start([dw_g, dw_u], [0, 0], f"scatter_gate_up{layer}_start", used=[ff_shard] * 2)
        dh_mid, d_y1, dg_fpre, dg_mpost = ffn_in_grad(d_g, d_u, w_g, w_u, hmid, dh_out, after(rs_gate_up["token"], vec(ln_ffn_pre[layer])),
                                                      y1, vec(ln_mix_post[layer]), f"ffn_in_grad{layer}")
        return dh_mid, d_y1, dg_fpre, dg_mpost, (rs_down, rs_gate_up)

    def mix_out_backward(layer, d_y1, mixed):
        dw_out = mm_tn(mixed, d_y1, f"dw_out{layer}")
        d_mixed = mm(d_y1, w_o[layer], f"d_mixed{layer}", trans_b=True)
        return d_mixed, dw_out

    def mem_backward(layer, q_src, q_block, kvm, memn, d_mixed, into):
        d_qm, d_kvm = mem_attn_bwd(q_src, q_block, kvm, d_mixed, into, f"mem_attn_bwd{layer}")
        d_kvm_b = d_kvm.astype(bf16)
        dw_mkv = mm_tn(memn, d_kvm_b, f"dw_mem_kv{layer}")
        d_memn = mm(d_kvm_b, w_mkv[layer], f"d_memn{layer}", trans_b=True)
        _, dg_mem = rms_bwd(mem0, vec(ln_mem[layer]), d_memn, None, bf16, f"mem_norm_bwd{layer}")
        return d_qm, dw_mkv, dg_mem


    dh_mid1, d_y1_1, dg_fpre1, dg_mpost1, rs_ffn1 = ffn_backward(1, dh, d_y2_1, hmid1, f1, gu1, act1, y1_1)
    d_mixed1, dw_out1 = mix_out_backward(1, d_y1_1, mixed1)
    dq_b, dk, dv, dc = fox_bwd(qb, kvb, d_mixed1, main1, lse, c_row, "fox_bwd", dq_width=D_MODEL)
    d_proj1, dw_mkv1, dg_mem1 = mem_backward(1, qb, MAIN_WIDTH // MEM_WIDTH, kvm1, memn1, d_mixed1, dq_b)
    rs_mix1 = scatter_start([dw_out1, dw_mkv1], [0, 0], "scatter_mix1_start")
    dc_t = jnp.pad(dc.reshape(FOX_HEADS, seq), ((0, 16 - FOX_HEADS), (0, 0)))
    dz_t, db_f = fgate_bwd(dc_t, z_t, bf_col, "fgate_bwd")
    d_kvf = jnp.concatenate([dk, dv, jnp.pad(dz_t[:FOX_HEADS].T.astype(bf16), ((0, 0), (0, KV_PAD - KV_WIDTH)))], axis=-1)
    dw_in_b = mm_tn(a1, d_proj1, "dw_in_b", dep=rs_mix1["token"])
    dw_kv = mm_tn(sin1, d_kvf, "dw_kv", tn=896)
    rs_2 = scatter_start([dw_in_b, dw_kv], [0, 0], "scatter_shared_start")
    dh1, (dg_pre1, dg_shared), d_y2_0, dg_fpost0 = proj_in_grad(
        [(d_proj1, w_inb, vec(ln_mix_pre[1])), (d_kvf, w_kv, vec(ln_shared))], h1, dh_mid1, "in_grad1", dep=rs_2["token"],
        below=(y2_0, vec(ln_ffn_post[0])))

    dh_mid0, d_y1_0, dg_fpre0, dg_mpost0, rs_ffn0 = ffn_backward(0, dh1, d_y2_0, hmid0, f0, gu0, act0, y1_0)
    d_mixed0, dw_out0 = mix_out_backward(0, d_y1_0, mixed0)
    d_uv, dw_s, db_s, dg_lnv, db_lnv = gmlp_bwd(proj0, d_mixed0, ws, ws_t, bs_t, lnv_g, lnv_b, "gmlp_bwd", out_width=w_in_a_full.shape[1])
    d_proj0, dw_mkv0, dg_mem0 = mem_backward(0, proj0, 2 * MAIN_WIDTH // MEM_WIDTH, kvm0, memn0, d_mixed0, d_uv)
    rs_mix0 = scatter_start([dw_out0, dw_mkv0], [0, 0], "scatter_mix0_start")

    small["ln_mix_pre"] = jnp.concatenate([jnp.zeros_like(dg_pre1), dg_pre1], axis=0)
    small["ln_mix_post"] = jnp.concatenate([dg_mpost0, dg_mpost1], axis=0)
    small["ln_ffn_pre"] = jnp.concatenate([dg_fpre0, dg_fpre1], axis=0)
    small["ln_ffn_post"] = jnp.concatenate([dg_fpost0, dg_fpost1], axis=0)
    small["ln_mem"] = jnp.concatenate([dg_mem0, dg_mem1], axis=0)
    small["w_spatial"] = dw_s[None]
    small["b_spatial"] = db_s[:, :A_GROUPS].T[None]
    small["ln_shared"] = dg_shared[0]
    small["b_forget"] = db_f[:FOX_HEADS, 0]
    small["ln_v_g"] = dg_lnv
    small["ln_v_b"] = db_lnv
    small_rows = jnp.concatenate([_pack_small(small, _SMALL), after(rs_mix0["token"], loss_tile)], axis=0)
    st_small = gather_start([small_rows[None]], [0], "gather_small_grads_start")
    dw_in_a_t = mm_tn(d_proj0, a0, "dw_in_a", tk=896, dep=st_small["token"])
    rs_in_a = scatter_start([dw_in_a_t], [0], "scatter_in_a_start")
    grad_x, (dg_pre0,) = proj_in_grad([(d_proj0, w_in_a_full, vec(ln_mix_pre[0]))], h0, dh_mid0, "in_grad0", dep=rs_in_a["token"])
    st_last = gather_start([dg_pre0.reshape(1, 8, LANES)], [0], "gather_last_grad_start")

    (p_down1,) = scatter_wait(rs_ffn1[0], [0], after(st_last["token"], grad_x[:8, :LANES]), "scatter_down1_wait", used=[ff_shard])
    p_gate1, p_up1 = scatter_wait(rs_ffn1[1], [0, 0], p_down1, "scatter_gate_up1_wait", used=[ff_shard] * 2)
    p_out1, p_mkv1 = scatter_wait(rs_mix1, [0, 0], p_gate1, "scatter_mix1_wait")
    p_in_b, p_kv = scatter_wait(rs_2, [0, 0], p_out1, "scatter_shared_wait")
    (p_down0,) = scatter_wait(rs_ffn0[0], [0], p_in_b, "scatter_down0_wait", used=[ff_shard])
    p_gate0, p_up0 = scatter_wait(rs_ffn0[1], [0, 0], p_down0, "scatter_gate_up0_wait", used=[ff_shard] * 2)
    p_out0, p_mkv0 = scatter_wait(rs_mix0, [0, 0], p_gate0, "scatter_mix0_wait")
    owned_parts = dict(w_ffn_gate=[p_gate0, p_gate1], w_ffn_up=[p_up0, p_up1], w_ffn_down=[p_down0, p_down1], w_out=[p_out0, p_out1],
                       w_mem_kv=[p_mkv0, p_mkv1], w_in_b=[p_in_b], w_shared_kv=[p_kv])

    grad_w, delta, new_m, new_v = {}, {}, {}, {}
    transposed = ("w_ffn_gate", "w_ffn_up", "w_in_a")

    def adamw_sharded(n, parts):
        shape = weights[n].shape
        three_d = shape if len(shape) == 3 else (1,) + shape
        view = (lambda t: t.reshape(three_d).transpose(0, 2, 1)) if n in transposed else (lambda t: t.reshape(three_d))
        back = (lambda t: t.transpose(0, 2, 1).reshape(shape)) if n in transposed else (lambda t: t.reshape(shape))
        w_view = view(weights[n])
        outs = adamw_owned(w_view, parts, view(mom_m[n]), view(mom_v[n]), f"adamw_{n}", tr=_row_tile(w_view.shape[1]))
        grad_w[n], delta[n], new_m[n], new_v[n] = (back(t) for t in outs)

    for n, parts in owned_parts.items():
        adamw_sharded(n, parts)
    (p_in_a,) = scatter_wait(rs_in_a, [0], delta["w_shared_kv"], "scatter_in_a_wait")
    adamw_sharded("w_in_a", [p_in_a])
    (small_all,) = gather_wait(st_small, [0], p_in_a, "gather_small_grads_wait")
    (last_all,) = gather_wait(st_last, [0], small_all, "gather_last_grad_wait")
    small_sum = sum_leading(small_all, "sum_small_grads")
    loss = small_sum[small_rows.shape[0] - 1, 0]
    g_small = _unpack_small(small_sum, _SMALL)
    g_small["ln_mix_pre"] = jnp.concatenate([sum_leading(last_all, "sum_last_grad").reshape(1, D_MODEL), g_small["ln_mix_pre"][1:]], axis=0)
    shard = MAIN_WIDTH // N_DEV
    for n in ("ln_v_g", "ln_v_b"):
        g_small[n] = lax.dynamic_slice_in_dim(g_small[n], me * shard, shard, axis=1)
    grad_w.update(g_small)
    small_local_shapes = [(n, tuple(weights[n].shape)) for n, _ in _SMALL]
    packed = [_pack_small(src, small_local_shapes) for src in (weights, grad_w, mom_m, mom_v)]
    outs = adamw(*packed, "adamw_small", tr=packed[0].shape[0])
    for dst, buf in zip((delta, new_m, new_v), outs):
        dst.update(_unpack_small(buf, small_local_shapes))

    return (loss, grad_x[None], *[grad_w[n] for n in names], *[delta[n] for n in names],
            *[new_m[n] for n in names], *[new_v[n] for n in names])
```

```python
import functools
import math

import jax
import jax.numpy as jnp
from jax import lax
from jax.experimental import pallas as pl
from jax.experimental.pallas import tpu as pltpu

f32 = jnp.float32
bf16 = jnp.bfloat16
SDS = jax.ShapeDtypeStruct

D_MODEL = 1024
MAIN_WIDTH = 768
MEM_WIDTH = 256
HEAD_DIM = 64
MEM_HEADS = 4
FOX_HEADS = 12
FOX_PAIRS = FOX_HEADS // 2
CHUNK = 128
A_GROUPS = 6
FF_SHARD_PAD = 384
KV_WIDTH = 2 * MAIN_WIDTH + FOX_HEADS
KV_PAD = 1792
RMS_EPS = 1e-6
LN_EPS = 1e-5
ATT_SCALE = HEAD_DIM ** -0.5
ADAM_LR, ADAM_B1, ADAM_B2, ADAM_EPS, ADAM_WD, ADAM_STEP = 0.001, 0.9, 0.999, 1e-08, 0.01, 10
N_DEV = 8
MESH = pl.DeviceIdType.MESH
V7X_VMEM_LIMIT = 56 * 1024 * 1024
LANES = 128


def _cparams(*sem):
    return pltpu.CompilerParams(dimension_semantics=sem or None, vmem_limit_bytes=V7X_VMEM_LIMIT)


def _dot(a, b):
    return jnp.dot(a, b, preferred_element_type=f32)


def _dot_nt(a, b):
    return lax.dot_general(a, b, (((1,), (1,)), ((), ())), preferred_element_type=f32)


def _dot_tn(a, b):
    return lax.dot_general(a, b, (((0,), (0,)), ((), ())), preferred_element_type=f32)


def _gelu(x):
    k = math.sqrt(2.0 / math.pi)
    t = jnp.tanh(k * (x + 0.044715 * x * x * x))
    return 0.5 * x * (1.0 + t), t


def _gelu_grad(x, t):
    k = math.sqrt(2.0 / math.pi)
    return 0.5 * (1.0 + t) + 0.5 * x * (1.0 - t * t) * k * (1.0 + 3.0 * 0.044715 * x * x)


def _sigmoid(x):
    return 1.0 / (1.0 + jnp.exp(-x))


def rms_fwd(x, gains, name, tm=512):
    m, d = x.shape
    tm = min(tm, m)
    n = len(gains)

    def body(x_ref, *refs):
        xv = x_ref[...]
        y = xv * lax.rsqrt(jnp.sum(xv * xv, axis=-1, keepdims=True) * (1.0 / d) + RMS_EPS)
        for g_ref, o_ref in zip(refs[:n], refs[n:]):
            o_ref[...] = (y * g_ref[...]).astype(bf16)

    row = pl.BlockSpec((tm, d), lambda i: (i, 0))
    vec = pl.BlockSpec((1, d), lambda i: (0, 0))
    return pl.pallas_call(body, grid=(m // tm,), in_specs=[row] + [vec] * n, out_specs=[row] * n,
                          out_shape=[SDS((m, d), bf16)] * n, name=name, compiler_params=_cparams("parallel"))(x, *gains)


def rms_bwd(x, g, dy, add, out_dtype, name, tm=512):
    m, d = x.shape
    tm = min(tm, m)
    has_add = add is not None

    def body(x_ref, g_ref, dy_ref, *refs):
        dx_ref, dg_ref = refs[-2], refs[-1]
        xv = x_ref[...]
        dyv = dy_ref[...].astype(f32)
        r = lax.rsqrt(jnp.sum(xv * xv, axis=-1, keepdims=True) * (1.0 / d) + RMS_EPS)
        xn = xv * r
        dyg = dyv * g_ref[...]
        dx = r * (dyg - xn * (jnp.sum(dyg * xn, axis=-1, keepdims=True) * (1.0 / d)))
        if has_add:
            dx = dx + refs[0][...]
        dx_ref[...] = dx.astype(out_dtype)

        @pl.when(pl.program_id(0) == 0)
        def _():
            dg_ref[...] = jnp.zeros_like(dg_ref)

        dg_ref[...] += jnp.sum(dyv * xn, axis=0, keepdims=True)

    row = pl.BlockSpec((tm, d), lambda i: (i, 0))
    vec = pl.BlockSpec((1, d), lambda i: (0, 0))
    ins = [x, g, dy] + ([add] if has_add else [])
    return pl.pallas_call(body, grid=(m // tm,), in_specs=[row, vec, row] + ([row] if has_add else []),
                          out_specs=[row, vec], out_shape=[SDS((m, d), out_dtype), SDS((1, d), f32)], name=name,
                          compiler_params=_cparams("arbitrary"))(*ins)


def mm(a, b, name, trans_b=False, out_dtype=f32, tm=1024, tn=1024, col0=0, ncols=None, dep=None):
    m, k = a.shape
    n_all = b.shape[0] if trans_b else b.shape[1]
    n = n_all if ncols is None else ncols
    tm, tn = min(tm, m), min(tn, n)
    assert m % tm == 0 and n % tn == 0 and col0 % tn == 0 and not (trans_b and col0), (name, m, n, tm, tn)
    jb = col0 // tn

    def body(a_ref, b_ref, *rest):
        r = _dot_nt(a_ref[...], b_ref[...]) if trans_b else _dot(a_ref[...], b_ref[...])
        rest[-1][...] = r.astype(out_dtype)

    if trans_b:
        b_spec = pl.BlockSpec((tn, k), lambda j, i: (j, 0))
    else:
        b_spec = pl.BlockSpec((k, tn), lambda j, i: (0, jb + j))
    deps = [] if dep is None else [dep]
    dep_specs = [pl.BlockSpec((8, LANES), lambda j, i: (0, 0))] * len(deps)
    return pl.pallas_call(body, grid=(n // tn, m // tm), in_specs=[pl.BlockSpec((tm, k), lambda j, i: (i, 0)), b_spec] + dep_specs,
                          out_specs=pl.BlockSpec((tm, tn), lambda j, i: (i, j)), out_shape=SDS((m, n), out_dtype),
                          name=name, compiler_params=_cparams("parallel", "parallel"))(a, b, *deps)


def mm_tn(a, g, name, tk=1024, tn=1024, out_dtype=bf16, dep=None):
    s, k = a.shape
    n = g.shape[1]
    tk, tn = min(tk, k), min(tn, n)
    assert k % tk == 0 and n % tn == 0, (name, k, n, tk, tn)

    def body(a_ref, g_ref, *rest):
        rest[-1][...] = _dot_tn(a_ref[...], g_ref[...]).astype(out_dtype)

    deps = [] if dep is None else [dep]
    dep_specs = [pl.BlockSpec((8, LANES), lambda i, j: (0, 0))] * len(deps)
    return pl.pallas_call(body, grid=(k // tk, n // tn),
                          in_specs=[pl.BlockSpec((s, tk), lambda i, j: (0, i)), pl.BlockSpec((s, tn), lambda i, j: (0, j))] + dep_specs,
                          out_specs=pl.BlockSpec((tk, tn), lambda i, j: (i, j)), out_shape=SDS((k, n), out_dtype), name=name,
                          compiler_params=_cparams("parallel", "parallel"))(a, g, *deps)


def _resident(shape, index_map):
    return pl.BlockSpec(shape, index_map, pipeline_mode=pl.Buffered(1))


def _rms(xv):
    return xv * lax.rsqrt(jnp.sum(xv * xv, axis=-1, keepdims=True) * (1.0 / xv.shape[-1]) + RMS_EPS)


def _rms_bwd_math(xv, g, dy):
    d = xv.shape[-1]
    r = lax.rsqrt(jnp.sum(xv * xv, axis=-1, keepdims=True) * (1.0 / d) + RMS_EPS)
    xn = xv * r
    dyg = dy * g
    dx = r * (dyg - xn * (jnp.sum(dyg * xn, axis=-1, keepdims=True) * (1.0 / d)))
    return dx, jnp.sum(dy * xn, axis=0, keepdims=True)


SUB_ROWS = 512


def mm_resnorm(a, b, h, g_post, gains, name, tm=512):
    m, k = a.shape
    d = b.shape[1]
    n = len(gains)

    def body(a_ref, b_ref, h_ref, gp_ref, *refs):
        sub = max(tm, SUB_ROWS)
        for r in range(tm // sub):
            rows = slice(r * sub, (r + 1) * sub)
            y = _dot(a_ref[rows, :], b_ref[...])
            refs[n][rows, :] = y
            hn = h_ref[rows, :] + _rms(y) * gp_ref[...]
            refs[n + 1][rows, :] = hn
            if n:
                z = _rms(hn)
                for g_ref, o_ref in zip(refs[:n], refs[n + 2:]):
                    o_ref[rows, :] = (z * g_ref[...]).astype(bf16)

    row = pl.BlockSpec((tm, d), lambda i: (i, 0))
    vec = pl.BlockSpec((1, d), lambda i: (0, 0))
    return pl.pallas_call(body, grid=(m // tm,),
                          in_specs=[pl.BlockSpec((tm, k), lambda i: (i, 0)), _resident((k, d), lambda i: (0, 0)), row, vec] + [vec] * n,
                          out_specs=[row] * (n + 2), out_shape=[SDS((m, d), f32)] * 2 + [SDS((m, d), bf16)] * n, name=name,
                          compiler_params=_cparams("parallel"))(a, b, h, g_post, *gains)


def mm_resnorm_loss(a, b, h, g_post, tgt, name, tm=512):
    m, k = a.shape
    d = b.shape[1]

    def body(a_ref, b_ref, h_ref, gp_ref, t_ref, dh_ref, dy_ref, dg_ref, l_ref):
        @pl.when(pl.program_id(0) == 0)
        def _():
            dg_ref[...] = jnp.zeros_like(dg_ref)
            l_ref[...] = jnp.zeros_like(l_ref)

        y = _dot(a_ref[...], b_ref[...])
        e = h_ref[...] + _rms(y) * gp_ref[...] - t_ref[...]
        dh = e * (1.0 / d)
        dh_ref[...] = dh
        part = jnp.sum(jnp.sum(e * e, axis=-1, keepdims=True), axis=0, keepdims=True) * (0.5 / d)
        l_ref[...] += jnp.broadcast_to(part, l_ref.shape)
        dy, dg = _rms_bwd_math(y, gp_ref[...], dh)
        dy_ref[...] = dy.astype(bf16)
        dg_ref[...] += dg

    row = pl.BlockSpec((tm, d), lambda i: (i, 0))
    vec = pl.BlockSpec((1, d), lambda i: (0, 0))
    return pl.pallas_call(body, grid=(m // tm,),
                          in_specs=[pl.BlockSpec((tm, k), lambda i: (i, 0)), _resident((k, d), lambda i: (0, 0)), row, vec, row],
                          out_specs=[row, row, vec, pl.BlockSpec((8, LANES), lambda i: (0, 0))],
                          out_shape=[SDS((m, d), f32), SDS((m, d), bf16), SDS((1, d), f32), SDS((8, LANES), f32)], name=name,
                          compiler_params=_cparams("arbitrary"))(a, b, h, g_post, tgt)


FFN_TILE = 1536


def ffn_act_grad(d_y2, w_d, factors, name, tm=1024):
    s, d = d_y2.shape
    ff = w_d.shape[0]
    tn = FFN_TILE
    nb = ff // tn

    def body(a_ref, b_ref, g_ref, u_ref, dg_ref, du_ref):
        av = a_ref[...]
        tc = 256
        for c in range(tn // tc):
            cols = slice(c * tc, (c + 1) * tc)
            da = _dot_nt(av, b_ref[cols, :])
            dg_ref[:, cols] = (da * g_ref[:, cols].astype(f32)).astype(bf16)
            du_ref[:, cols] = (da * u_ref[:, cols].astype(f32)).astype(bf16)

    tile = pl.BlockSpec((tm, tn), lambda j, i: (i, j))
    return pl.pallas_call(body, grid=(nb, s // tm),
                          in_specs=[pl.BlockSpec((tm, d), lambda j, i: (i, 0)), pl.BlockSpec((tn, d), lambda j, i: (j, 0)),
                                    pl.BlockSpec((tm, tn), lambda j, i: (i, 2 * j)), pl.BlockSpec((tm, tn), lambda j, i: (i, 2 * j + 1))],
                          out_specs=[tile, tile], out_shape=[SDS((s, ff), bf16)] * 2, name=name,
                          compiler_params=_cparams("parallel", "parallel"))(d_y2, w_d, factors, factors)


def ffn_in_grad(d_g, d_u, w_g, w_u, hmid, dh_out, g_pre, y1, g_post, name, tm=512):
    s, ff = d_g.shape
    d = w_g.shape[0]

    def body(dg_ref, du_ref, wg_ref, wu_ref, hm_ref, dho_ref, gpre_ref, y1_ref, gpost_ref, dhm_ref, dy1_ref, dgpre_ref, dgpost_ref):
        @pl.when(pl.program_id(0) == 0)
        def _():
            dgpre_ref[...] = jnp.zeros_like(dgpre_ref)
            dgpost_ref[...] = jnp.zeros_like(dgpost_ref)

        for r in range(tm // SUB_ROWS):
            rows = slice(r * SUB_ROWS, (r + 1) * SUB_ROWS)
            d_f = _dot_nt(dg_ref[rows, :], wg_ref[...]) + _dot_nt(du_ref[rows, :], wu_ref[...])
            dx, dg1 = _rms_bwd_math(hm_ref[rows, :], gpre_ref[...], d_f)
            dh_mid = dho_ref[rows, :] + dx
            dhm_ref[rows, :] = dh_mid
            dgpre_ref[...] += dg1
            dy1, dg2 = _rms_bwd_math(y1_ref[rows, :], gpost_ref[...], dh_mid)
            dy1_ref[rows, :] = dy1.astype(bf16)
            dgpost_ref[...] += dg2

    row = pl.BlockSpec((tm, d), lambda i: (i, 0))
    vec = pl.BlockSpec((1, d), lambda i: (0, 0))
    wide = pl.BlockSpec((tm, ff), lambda i: (i, 0))
    w_spec = _resident((d, ff), lambda i: (0, 0))
    return pl.pallas_call(body, grid=(s // tm,), in_specs=[wide, wide, w_spec, w_spec, row, row, vec, row, vec],
                          out_specs=[row, row, vec, vec], out_shape=[SDS((s, d), f32), SDS((s, d), bf16), SDS((1, d), f32), SDS((1, d), f32)],
                          name=name, compiler_params=_cparams("arbitrary"))(d_g, d_u, w_g, w_u, hmid, dh_out, g_pre, y1, g_post)


def proj_in_grad(pairs, x, add, name, tm=512, dep=None, below=None):
    s, d = x.shape
    n = len(pairs)
    extra = [] if dep is None else [dep]
    n_below = 0 if below is None else 2

    def body(*refs):
        x_ref, add_ref = refs[3 * n], refs[3 * n + 1]
        below_refs = refs[3 * n + 2:3 * n + 2 + n_below]
        outs = refs[3 * n + 2 + n_below + len(extra):]

        @pl.when(pl.program_id(0) == 0)
        def _():
            for o in outs[1:1 + n] + outs[2 + n:]:
                o[...] = jnp.zeros_like(o)

        xv = x_ref[...]
        dx = add_ref[...]
        for i in range(n):
            a_ref, b_ref, g_ref = refs[3 * i:3 * i + 3]
            dxi, dgi = _rms_bwd_math(xv, g_ref[...], _dot_nt(a_ref[...], b_ref[...]))
            dx = dx + dxi
            outs[1 + i][...] += dgi
        outs[0][...] = dx
        if below is not None:
            dy, dg = _rms_bwd_math(below_refs[0][...], below_refs[1][...], dx)
            outs[1 + n][...] = dy.astype(bf16)
            outs[2 + n][...] += dg

    row = pl.BlockSpec((tm, d), lambda i: (i, 0))
    vec = pl.BlockSpec((1, d), lambda i: (0, 0))
    in_specs, args = [], []
    for a, b, g in pairs:
        k = a.shape[1]
        in_specs += [pl.BlockSpec((tm, k), lambda i: (i, 0)), _resident((d, k), lambda i: (0, 0)), vec]
        args += [a, b, g]
    in_specs += [row, row] + [row, vec][:n_below] + [pl.BlockSpec((8, LANES), lambda i: (0, 0))] * len(extra)
    out_specs = [row] + [vec] * n + [row, vec][:n_below]
    out_shape = [SDS((s, d), f32)] + [SDS((1, d), f32)] * n + [SDS((s, d), bf16), SDS((1, d), f32)][:n_below]
    out = pl.pallas_call(body, grid=(s // tm,), in_specs=in_specs, out_specs=out_specs, out_shape=out_shape, name=name,
                         compiler_params=_cparams("arbitrary"))(*args, x, add, *(below or ()), *extra)
    return (out[0], out[1:1 + n]) + tuple(out[1 + n:])


def ffn_up(f, wg, wu, name, tm=1024, tc=256):
    s, d = f.shape
    ff = wg.shape[-1]
    tn = FFN_TILE

    def body(f_ref, wg_ref, wu_ref, fac_ref, act_ref):
        fv = f_ref[...]
        for c in range(tn // tc):
            lo = c * tc
            gg = _dot(fv, wg_ref[:, lo:lo + tc])
            uu = _dot(fv, wu_ref[:, lo:lo + tc])
            sg = _sigmoid(gg)
            silu = gg * sg
            fac_ref[:, lo:lo + tc] = (uu * (sg + silu * (1.0 - sg))).astype(bf16)
            fac_ref[:, tn + lo:tn + lo + tc] = silu.astype(bf16)
            act_ref[:, lo:lo + tc] = (silu * uu).astype(bf16)

    w_spec = pl.BlockSpec((d, tn), lambda j, i: (0, j))
    return pl.pallas_call(body, grid=(ff // tn, s // tm), in_specs=[pl.BlockSpec((tm, d), lambda j, i: (i, 0)), w_spec, w_spec],
                          out_specs=[pl.BlockSpec((tm, 2 * tn), lambda j, i: (i, j)), pl.BlockSpec((tm, tn), lambda j, i: (i, j))],
                          out_shape=[SDS((s, 2 * ff), bf16), SDS((s, ff), bf16)], name=name,
                          compiler_params=_cparams("parallel", "parallel"))(f, wg, wu)


def _gmlp_forward_chunk(u, v, w_refs, bias, ln_g, ln_b):
    gu, tu = _gelu(u)
    gv, tv = _gelu(v)
    mu = jnp.sum(gv, axis=-1, keepdims=True) * (1.0 / MAIN_WIDTH)
    xc = gv - mu
    rstd = lax.rsqrt(jnp.sum(xc * xc, axis=-1, keepdims=True) * (1.0 / MAIN_WIDTH) + LN_EPS)
    xhat = xc * rstd
    vln = xhat * ln_g + ln_b
    row = lax.broadcasted_iota(jnp.int32, (CHUNK, CHUNK), 0)
    col = lax.broadcasted_iota(jnp.int32, (CHUNK, CHUNK), 1)
    s_parts = []
    for g in range(A_GROUPS):
        w = jnp.where(col <= row, w_refs[g], jnp.zeros((), bf16))
        s_parts.append(_dot(w, vln[:, g * CHUNK:(g + 1) * CHUNK].astype(bf16)) + bias[:, g:g + 1])
    return gu, tu, tv, rstd, xhat, vln, s_parts


def gmlp_fwd(proj, ws, bs_t, ln_g, ln_b, name, tm=512, out_width=MAIN_WIDTH):
    s = proj.shape[0]

    def body(u_ref, v_ref, w_ref, b_ref, g_ref, bb_ref, o_ref):
        bias = b_ref[...]
        for c in range(tm // CHUNK):
            rows = slice(c * CHUNK, (c + 1) * CHUNK)
            gu, _, _, _, _, _, s_parts = _gmlp_forward_chunk(u_ref[rows, :], v_ref[rows, :], w_ref, bias, g_ref[...], bb_ref[...])
            for g in range(A_GROUPS):
                cols = slice(g * CHUNK, (g + 1) * CHUNK)
                o_ref[rows, cols] = (gu[:, cols] * s_parts[g]).astype(bf16)

    vec = pl.BlockSpec((1, MAIN_WIDTH), lambda i: (0, 0))
    return pl.pallas_call(
        body, grid=(s // tm,),
        in_specs=[pl.BlockSpec((tm, MAIN_WIDTH), lambda i: (i, 0)), pl.BlockSpec((tm, MAIN_WIDTH), lambda i: (i, 1)),
                  pl.BlockSpec((A_GROUPS, CHUNK, CHUNK), lambda i: (0, 0, 0)), pl.BlockSpec((CHUNK, A_GROUPS), lambda i: (0, 0)), vec, vec],
        out_specs=pl.BlockSpec((tm, MAIN_WIDTH), lambda i: (i, 0)), out_shape=SDS((s, out_width), bf16), name=name,
        compiler_params=_cparams("parallel"))(proj, proj, ws, bs_t, ln_g, ln_b)


def gmlp_bwd(proj, d_mixed, ws, ws_t, bs_t, ln_g, ln_b, name, tm=512, out_width=2 * MAIN_WIDTH):
    s = proj.shape[0]

    def body(u_ref, v_ref, dm_ref, w_ref, wt_ref, b_ref, g_ref, bb_ref, duv_ref, dw_ref, db_ref, dg_ref, dbb_ref):
        @pl.when(pl.program_id(0) == 0)
        def _():
            dw_ref[...] = jnp.zeros_like(dw_ref)
            db_ref[...] = jnp.zeros_like(db_ref)
            dg_ref[...] = jnp.zeros_like(dg_ref)
            dbb_ref[...] = jnp.zeros_like(dbb_ref)

        bias = b_ref[...]
        ln_gv = g_ref[...]
        row = lax.broadcasted_iota(jnp.int32, (CHUNK, CHUNK), 0)
        col = lax.broadcasted_iota(jnp.int32, (CHUNK, CHUNK), 1)
        lane = lax.broadcasted_iota(jnp.int32, (CHUNK, LANES), 1)
        for c in range(tm // CHUNK):
            rows = slice(c * CHUNK, (c + 1) * CHUNK)
            u = u_ref[rows, :]
            v = v_ref[rows, :]
            gu, tu, tv, rstd, xhat, vln, s_parts = _gmlp_forward_chunk(u, v, w_ref, bias, ln_gv, bb_ref[...])
            dm = dm_ref[rows, :]
            d_vln_parts = []
            d_gu_parts = []
            db_acc = jnp.zeros((CHUNK, LANES), f32)
            for g in range(A_GROUPS):
                cols = slice(g * CHUNK, (g + 1) * CHUNK)
                dmg = dm[:, cols]
                d_gu_parts.append(dmg * s_parts[g])
                d_s = dmg * gu[:, cols]
                db_acc = db_acc + jnp.where(lane == g, jnp.sum(d_s, axis=-1, keepdims=True), 0.0)
                d_sb = d_s.astype(bf16)
                dw_ref[g] += jnp.where(col <= row, _dot_nt(d_sb, vln[:, cols].astype(bf16)), 0.0)
                wt = jnp.where(row <= col, wt_ref[g], jnp.zeros((), bf16))
                d_vln_parts.append(_dot(wt, d_sb))
            db_ref[...] += db_acc
            d_vln = jnp.concatenate(d_vln_parts, axis=-1)
            d_gu = jnp.concatenate(d_gu_parts, axis=-1)
            dg_ref[...] += jnp.sum(d_vln * xhat, axis=0, keepdims=True)
            dbb_ref[...] += jnp.sum(d_vln, axis=0, keepdims=True)
            dxh = d_vln * ln_gv
            m1 = jnp.sum(dxh, axis=-1, keepdims=True) * (1.0 / MAIN_WIDTH)
            m2 = jnp.sum(dxh * xhat, axis=-1, keepdims=True) * (1.0 / MAIN_WIDTH)
            d_gv = rstd * (dxh - m1 - xhat * m2)
            duv_ref[rows, :MAIN_WIDTH] = (d_gu * _gelu_grad(u, tu)).astype(bf16)
            duv_ref[rows, MAIN_WIDTH:] = (d_gv * _gelu_grad(v, tv)).astype(bf16)

    vec = pl.BlockSpec((1, MAIN_WIDTH), lambda i: (0, 0))
    wspec = pl.BlockSpec((A_GROUPS, CHUNK, CHUNK), lambda i: (0, 0, 0))
    return pl.pallas_call(
        body, grid=(s // tm,),
        in_specs=[pl.BlockSpec((tm, MAIN_WIDTH), lambda i: (i, 0)), pl.BlockSpec((tm, MAIN_WIDTH), lambda i: (i, 1)),
                  pl.BlockSpec((tm, MAIN_WIDTH), lambda i: (i, 0)), wspec, wspec, pl.BlockSpec((CHUNK, A_GROUPS), lambda i: (0, 0)), vec, vec],
        out_specs=[pl.BlockSpec((tm, 2 * MAIN_WIDTH), lambda i: (i, 0)), wspec, pl.BlockSpec((CHUNK, LANES), lambda i: (0, 0)), vec, vec],
        out_shape=[SDS((s, out_width), bf16), SDS((A_GROUPS, CHUNK, CHUNK), f32), SDS((CHUNK, LANES), f32),
                   SDS((1, MAIN_WIDTH), f32), SDS((1, MAIN_WIDTH), f32)],
        name=name, compiler_params=_cparams("arbitrary"))(proj, proj, d_mixed, ws, ws_t, bs_t, ln_g, ln_b)


def _head_mask(width, h):
    lane = lax.broadcasted_iota(jnp.int32, (1, width), 1)
    return (lane >= h * HEAD_DIM) & (lane < (h + 1) * HEAD_DIM)


def mem_attn_fwd(proj, q_block, kv, into, name, tm=1024):
    s = proj.shape[0]
    n_mem = kv.shape[0]
    out_block = into.shape[1] // MEM_WIDTH - 1

    def body(q_ref, kv_ref, into_ref, o_ref):
        q = q_ref[...].astype(f32)
        k = kv_ref[:, :MEM_WIDTH].astype(bf16)
        v = kv_ref[:, MEM_WIDTH:].astype(bf16)
        out = jnp.zeros((tm, MEM_WIDTH), f32)
        for h in range(MEM_HEADS):
            msk = _head_mask(MEM_WIDTH, h)
            qh = jnp.where(msk, q, 0.0).astype(bf16)
            sc = _dot_nt(qh, k) * ATT_SCALE
            e = jnp.exp(sc - jnp.max(sc, axis=-1, keepdims=True))
            p = e / jnp.sum(e, axis=-1, keepdims=True)
            out = jnp.where(msk, _dot(p.astype(bf16), v), out)
        o_ref[...] = out.astype(bf16)

    return pl.pallas_call(body, grid=(s // tm,),
                          in_specs=[pl.BlockSpec((tm, MEM_WIDTH), lambda i: (i, q_block)), pl.BlockSpec((n_mem, 2 * MEM_WIDTH), lambda i: (0, 0)), _ANY],
                          out_specs=pl.BlockSpec((tm, MEM_WIDTH), lambda i: (i, out_block)), out_shape=SDS(into.shape, bf16), name=name,
                          input_output_aliases={2: 0}, compiler_params=_cparams("parallel"))(proj, kv, into)


def mem_attn_bwd(proj, q_block, kv, d_mixed, into, name, tm=1024):
    s = proj.shape[0]
    n_mem = kv.shape[0]
    out_block = into.shape[1] // MEM_WIDTH - 1

    def body(q_ref, kv_ref, do_ref, into_ref, dq_ref, dkv_ref):
        @pl.when(pl.program_id(0) == 0)
        def _():
            dkv_ref[...] = jnp.zeros_like(dkv_ref)

        q = q_ref[...].astype(f32)
        do = do_ref[...]
        k = kv_ref[:, :MEM_WIDTH].astype(bf16)
        v = kv_ref[:, MEM_WIDTH:].astype(bf16)
        dq = jnp.zeros((tm, MEM_WIDTH), f32)
        dk = jnp.zeros((n_mem, MEM_WIDTH), f32)
        dv = jnp.zeros((n_mem, MEM_WIDTH), f32)
        for h in range(MEM_HEADS):
            msk = _head_mask(MEM_WIDTH, h)
            qh = jnp.where(msk, q, 0.0).astype(bf16)
            doh = jnp.where(msk, do, 0.0).astype(bf16)
            sc = _dot_nt(qh, k) * ATT_SCALE
            e = jnp.exp(sc - jnp.max(sc, axis=-1, keepdims=True))
            p = e / jnp.sum(e, axis=-1, keepdims=True)
            dp = _dot_nt(doh, v)
            ds = p * (dp - jnp.sum(dp * p, axis=-1, keepdims=True))
            dsb = (ds * ATT_SCALE).astype(bf16)
            dq = jnp.where(msk, _dot(dsb, k), dq)
            dk = dk + _dot_tn(dsb, qh)
            dv = dv + _dot_tn(p.astype(bf16), doh)
        dq_ref[...] = dq.astype(bf16)
        dkv_ref[:, :MEM_WIDTH] += dk
        dkv_ref[:, MEM_WIDTH:] += dv

    return pl.pallas_call(
        body, grid=(s // tm,),
        in_specs=[pl.BlockSpec((tm, MEM_WIDTH), lambda i: (i, q_block)), pl.BlockSpec((n_mem, 2 * MEM_WIDTH), lambda i: (0, 0)),
                  pl.BlockSpec((tm, MEM_WIDTH), lambda i: (i, MAIN_WIDTH // MEM_WIDTH)), _ANY],
        out_specs=[pl.BlockSpec((tm, MEM_WIDTH), lambda i: (i, out_block)), pl.BlockSpec((n_mem, 2 * MEM_WIDTH), lambda i: (0, 0))],
        out_shape=[SDS(into.shape, bf16), SDS((n_mem, 2 * MEM_WIDTH), f32)], name=name,
        input_output_aliases={3: 0}, compiler_params=_cparams("arbitrary"))(proj, kv, d_mixed, into)


def _tri(t, upper):
    r = lax.broadcasted_iota(jnp.int32, (t, t), 0)
    c = lax.broadcasted_iota(jnp.int32, (t, t), 1)
    return ((r <= c) if upper else (r >= c)).astype(f32)


def fgate_fwd(z_t, b, name, t=512):
    hh, s = z_t.shape

    def body(z_ref, b_ref, c_ref):
        u = _tri(t, True)
        carry = jnp.zeros((hh, 1), f32)
        for blk in range(s // t):
            x = z_ref[:, blk * t:(blk + 1) * t] + b_ref[...]
            logf = jnp.minimum(x, 0.0) - jnp.log(1.0 + jnp.exp(-jnp.abs(x)))
            y = jnp.dot(logf, u, precision=lax.Precision.HIGHEST, preferred_element_type=f32) + carry
            c_ref[:, blk * t:(blk + 1) * t] = y
            carry = y[:, t - 1:t]

    return pl.pallas_call(body, out_shape=SDS((hh, s), f32), name=name, compiler_params=_cparams())(z_t, b)


def fgate_bwd(dc_t, z_t, b, name, t=512):
    hh, s = z_t.shape

    def body(dc_ref, z_ref, b_ref, dz_ref, db_ref):
        low = _tri(t, False)
        carry = jnp.zeros((hh, 1), f32)
        total = jnp.zeros((hh, 1), f32)
        for blk in reversed(range(s // t)):
            cols = slice(blk * t, (blk + 1) * t)
            y = jnp.dot(dc_ref[:, cols], low, precision=lax.Precision.HIGHEST, preferred_element_type=f32) + carry
            carry = y[:, 0:1]
            dz = y * _sigmoid(-(z_ref[:, cols] + b_ref[...]))
            dz_ref[:, cols] = dz
            total = total + jnp.sum(dz, axis=-1, keepdims=True)
        db_ref[...] = jnp.broadcast_to(total, db_ref.shape)

    return pl.pallas_call(body, out_shape=[SDS((hh, s), f32), SDS((hh, LANES), f32)], name=name,
                          compiler_params=_cparams())(dc_t, z_t, b)


def _pair_masks():
    lane = lax.broadcasted_iota(jnp.int32, (1, LANES), 1)
    return [lane < HEAD_DIM, lane >= HEAD_DIM]


def _tile_base(cr_ref, hh, lo):
    return cr_ref[hh:hh + 1, pl.ds(lo, LANES)][:, 0:1]


def fox_fwd(q, kv, c_row, name, tq=512, out_width=MAIN_WIDTH):
    s = kv.shape[0]
    nq = s // tq

    def body(q_ref, k_ref, v_ref, cr_ref, o_ref, lse_ref, ob_ref):
        i = pl.program_id(1)
        qv = q_ref[...]
        masks = _pair_masks()
        row = lax.broadcasted_iota(jnp.int32, (tq, tq), 0)
        col = lax.broadcasted_iota(jnp.int32, (tq, tq), 1)
        qh = [jnp.where(masks[hh], qv, jnp.zeros((), bf16)) * ATT_SCALE for hh in range(2)]
        ct = [_tile_base(cr_ref, hh, pl.multiple_of(i * tq, tq)) for hh in range(2)]

        def block(j, carry, diag):
            lo = pl.multiple_of(j * tq, tq)
            ks = k_ref[pl.ds(lo, tq), :]
            vs = v_ref[pl.ds(lo, tq), :]
            out = []
            for hh in range(2):
                m, l, acc = carry[hh]
                sc = _dot_nt(qh[hh], ks) + (ct[hh] - cr_ref[hh:hh + 1, pl.ds(lo, tq)])
                if diag:
                    sc = jnp.where(col <= row, sc, -jnp.inf)
                m_new = jnp.maximum(m, jnp.max(sc, axis=-1, keepdims=True))
                alpha = jnp.exp(m - m_new)
                p = jnp.exp(sc - m_new)
                l = alpha * l + jnp.sum(p, axis=-1, keepdims=True)
                p_hi = p.astype(bf16)
                p_lo = (p - p_hi.astype(f32)).astype(bf16)
                acc = alpha * acc + (_dot(p_hi, vs) + _dot(p_lo, vs))
                out.append((m_new, l, acc))
            return tuple(out)

        init = (jnp.full((tq, 1), -jnp.inf, f32), jnp.zeros((tq, 1), f32), jnp.zeros((tq, LANES), f32))
        carry = lax.fori_loop(0, i, functools.partial(block, diag=False), (init, init))
        res = [(acc / l, m + jnp.log(l)) for m, l, acc in block(i, carry, True)]
        out = jnp.where(masks[0], res[0][0], res[1][0])
        o_ref[...] = out
        ob_ref[...] = out.astype(bf16)
        lse_ref[...] = jnp.where(masks[0], res[0][1], res[1][1])

    return pl.pallas_call(
        body, grid=(FOX_PAIRS, nq),
        in_specs=[pl.BlockSpec((tq, LANES), lambda p, i: (i, p)), pl.BlockSpec((s, LANES), lambda p, i: (0, p)),
                  pl.BlockSpec((s, LANES), lambda p, i: (0, FOX_PAIRS + p)), pl.BlockSpec((None, 2, s), lambda p, i: (p, 0, 0))],
        out_specs=[pl.BlockSpec((tq, LANES), lambda p, i: (i, p)), pl.BlockSpec((None, tq, LANES), lambda p, i: (p, i, 0)),
                   pl.BlockSpec((tq, LANES), lambda p, i: (i, p))],
        out_shape=[SDS((s, MAIN_WIDTH), f32), SDS((FOX_PAIRS, s, LANES), f32), SDS((s, out_width), bf16)], name=name,
        compiler_params=_cparams("parallel", "parallel"))(q, kv, kv, c_row)


def fox_bwd(q, kv, d_mixed, o, lse, c_row, name, tq=512, dq_width=MAIN_WIDTH):
    s = kv.shape[0]
    nq = s // tq

    def body(q_ref, k_ref, v_ref, do_ref, o_ref, lse_ref, cr_ref, dqb_ref, dk_ref, dv_ref, dc_ref, dq_ref):
        j = pl.program_id(1)

        @pl.when(j == 0)
        def _():
            dq_ref[...] = jnp.zeros_like(dq_ref)

        masks = _pair_masks()
        sub = lax.broadcasted_iota(jnp.int32, (LANES, 1), 0)
        sub_masks = [sub < HEAD_DIM, sub >= HEAD_DIM]
        row = lax.broadcasted_iota(jnp.int32, (tq, tq), 0)
        col = lax.broadcasted_iota(jnp.int32, (tq, tq), 1)
        kj = k_ref[...]
        vj = v_ref[...]
        lo_j = pl.multiple_of(j * tq, tq)

        def block(i, carry, diag):
            dk_t, dv_t, dc0, dc1 = carry
            dcs = [dc0, dc1]
            lo = pl.multiple_of(i * tq, tq)
            qi = q_ref[pl.ds(lo, tq), :]
            qi = qi * ATT_SCALE
            qt_i = qi.T
            doi = do_ref[pl.ds(lo, tq), :]
            dot_i = doi.astype(bf16).T
            prod = doi.astype(bf16).astype(f32) * o_ref[pl.ds(lo, tq), :]
            lse_i = lse_ref[pl.ds(lo, tq), :]
            dq_i = jnp.zeros((tq, LANES), f32)
            for hh in range(2):
                qh = jnp.where(masks[hh], qi, jnp.zeros((), bf16))
                doh = jnp.where(masks[hh], doi, 0.0).astype(bf16)
                delta = jnp.sum(jnp.where(masks[hh], prod, 0.0), axis=-1, keepdims=True)
                sc = _dot_nt(qh, kj) + (_tile_base(cr_ref, hh, lo) - cr_ref[hh:hh + 1, pl.ds(lo_j, tq)])
                p = jnp.exp(sc - lse_i[:, hh * HEAD_DIM:hh * HEAD_DIM + 1])
                if diag:
                    p = jnp.where(col <= row, p, 0.0)
                dv_t = dv_t + _dot(jnp.where(sub_masks[hh], dot_i, jnp.zeros((), bf16)), p.astype(bf16))
                ds = p * (_dot_nt(doh, vj) - delta)
                dcs[hh] = dcs[hh] + jnp.sum(ds, axis=0, keepdims=True)
                dsb = ds.astype(bf16)
                dq_i = jnp.where(masks[hh], _dot(dsb, kj), dq_i)
                dk_t = dk_t + _dot(jnp.where(sub_masks[hh], qt_i, jnp.zeros((), bf16)), dsb)
            dq_ref[pl.ds(lo, tq), :] += dq_i * ATT_SCALE
            return dk_t, dv_t, dcs[0], dcs[1]

        zero = jnp.zeros((LANES, tq), f32)
        zrow = jnp.zeros((1, tq), f32)
        carry = block(j, (zero, zero, zrow, zrow), True)
        dk_t, dv_t, dc0, dc1 = lax.fori_loop(j + 1, nq, functools.partial(block, diag=False), carry)
        dk_ref[...] = dk_t.T.astype(bf16)
        dv_ref[...] = dv_t.T.astype(bf16)
        dc_ref[0:1, :] = -dc0
        dc_ref[1:2, :] = -dc1

        @pl.when(j == nq - 1)
        def _():
            dqb_ref[...] = dq_ref[...].astype(bf16)

    full = lambda p, j: (0, p)
    tile = lambda p, j: (j, p)
    return pl.pallas_call(
        body, grid=(FOX_PAIRS, nq),
        in_specs=[pl.BlockSpec((s, LANES), full), pl.BlockSpec((tq, LANES), tile), pl.BlockSpec((tq, LANES), lambda p, j: (j, FOX_PAIRS + p)),
                  pl.BlockSpec((s, LANES), full), pl.BlockSpec((s, LANES), full), pl.BlockSpec((None, s, LANES), lambda p, j: (p, 0, 0)),
                  pl.BlockSpec((None, 2, s), lambda p, j: (p, 0, 0))],
        out_specs=[pl.BlockSpec((s, LANES), full), pl.BlockSpec((tq, LANES), tile), pl.BlockSpec((tq, LANES), tile),
                   pl.BlockSpec((None, 2, tq), lambda p, j: (p, 0, j))],
        out_shape=[SDS((s, dq_width), bf16), SDS((s, MAIN_WIDTH), bf16), SDS((s, MAIN_WIDTH), bf16), SDS((FOX_PAIRS, 2, s), f32)],
        scratch_shapes=[pltpu.VMEM((s, LANES), f32)],
        name=name, compiler_params=_cparams("parallel", "arbitrary"))(q, kv, kv, d_mixed, o, lse, c_row)


def adamw(w, g, m, v, name, tr=256):
    r, c = w.shape
    tr = min(tr, r)
    assert r % tr == 0, (name, r, tr)
    c1 = 1.0 / (1.0 - ADAM_B1 ** ADAM_STEP)
    c2 = 1.0 / (1.0 - ADAM_B2 ** ADAM_STEP)

    def body(w_ref, g_ref, m_ref, v_ref, d_ref, mo_ref, vo_ref):
        gv = g_ref[...]
        mn = ADAM_B1 * m_ref[...] + (1.0 - ADAM_B1) * gv
        vn = ADAM_B2 * v_ref[...] + (1.0 - ADAM_B2) * gv * gv
        mo_ref[...] = mn
        vo_ref[...] = vn
        d_ref[...] = -ADAM_LR * ((mn * c1) / (jnp.sqrt(vn * c2) + ADAM_EPS) + ADAM_WD * w_ref[...])

    spec = pl.BlockSpec((tr, c), lambda i: (i, 0))
    return pl.pallas_call(body, grid=(r // tr,), in_specs=[spec] * 4, out_specs=[spec] * 3, out_shape=[SDS((r, c), f32)] * 3,
                          name=name, compiler_params=_cparams("parallel"))(w, g, m, v)


def adamw_owned(w, parts, m, v, name, tr):
    nl, r, c = w.shape
    cp = parts[0].shape[2]
    assert r % tr == 0 and len(parts) == nl, (name, r, tr)
    c1 = 1.0 / (1.0 - ADAM_B1 ** ADAM_STEP)
    c2 = 1.0 / (1.0 - ADAM_B2 ** ADAM_STEP)

    def body(*refs):
        w_ref, p_refs, (m_ref, v_ref) = refs[0], refs[1:1 + nl], refs[1 + nl:3 + nl]
        g_ref, d_ref, mo_ref, vo_ref = refs[3 + nl:]
        layer = pl.program_id(0)

        def total(p_ref):
            acc = p_ref[0].astype(f32)
            for k in range(1, N_DEV):
                acc = acc + p_ref[k].astype(f32)
            return acc

        gv = total(p_refs[0])
        for l in range(1, nl):
            gv = jnp.where(layer == l, total(p_refs[l]), gv)
        gv = gv[:, :c]
        g_ref[...] = gv
        mn = ADAM_B1 * m_ref[...] + (1.0 - ADAM_B1) * gv
        vn = ADAM_B2 * v_ref[...] + (1.0 - ADAM_B2) * gv * gv
        mo_ref[...] = mn
        vo_ref[...] = vn
        d_ref[...] = -ADAM_LR * ((mn * c1) / (jnp.sqrt(vn * c2) + ADAM_EPS) + ADAM_WD * w_ref[...])

    spec = pl.BlockSpec((None, tr, c), lambda l, i: (l, i, 0))
    last = r // tr - 1

    def part_spec(mine):
        return pl.BlockSpec((N_DEV, tr, cp), lambda l, i: (0, jnp.where(l == mine, i, jnp.where(l < mine, 0, last)), 0))

    return pl.pallas_call(body, grid=(nl, r // tr), in_specs=[spec] + [part_spec(l) for l in range(nl)] + [spec, spec], out_specs=[spec] * 4,
                          out_shape=[SDS((nl, r, c), f32)] * 4, name=name,
                          compiler_params=_cparams("parallel", "parallel"))(w, *parts, m, v)


def sum_leading(x, name, out_dtype=f32, tr=None):
    n, r, c = x.shape
    tr = tr or r
    assert r % tr == 0

    def body(x_ref, o_ref):
        acc = x_ref[0].astype(f32)
        for k in range(1, n):
            acc = acc + x_ref[k].astype(f32)
        o_ref[...] = acc.astype(out_dtype)

    return pl.pallas_call(body, grid=(r // tr,), in_specs=[pl.BlockSpec((n, tr, c), lambda i: (0, i, 0))],
                          out_specs=pl.BlockSpec((tr, c), lambda i: (i, 0)), out_shape=SDS((r, c), out_dtype), name=name,
                          compiler_params=_cparams("parallel"))(x)


_ANY = pl.BlockSpec(memory_space=pl.ANY)
_DMA = pltpu.SemaphoreType.DMA


_HBM = pl.BlockSpec(memory_space=pltpu.HBM)
_SEM = pl.BlockSpec(memory_space=pltpu.SEMAPHORE)
_EFFECT = pltpu.SideEffectType.DATAFLOW_SIDE_EFFECTING
_FLIPS = [(0, 0, 1), (1, 0, 0), (0, 1, 0), (1, 1, 0), (1, 0, 1), (0, 1, 1), (1, 1, 1)]


def _me():
    return lax.axis_index("x"), lax.axis_index("y"), lax.axis_index("c")


def _peers():
    mx, my, mc = _me()
    return [(jnp.bitwise_xor(mx, fx), jnp.bitwise_xor(my, fy), jnp.bitwise_xor(mc, fc)) for fx, fy, fc in _FLIPS]


def _index(dev):
    return 4 * dev[0] + 2 * dev[1] + dev[2]


def _win(ref, axis, k, size, count=1):
    idx = [slice(None)] * len(ref.shape)
    idx[axis] = pl.ds(k * size, count * size)
    return ref.at[tuple(idx)]


def _hbm(a):
    return pltpu.with_memory_space_constraint(a, pltpu.HBM)


def _exchange_start(srcs, lands, copies_of, name):
    n = len(srcs)

    def body(*refs):
        src = refs[:n]
        send_sems, recv_sems, self_sems = refs[2 * n:2 * n + 3]
        land = refs[3 * n + 3:4 * n + 3]
        token = refs[4 * n + 3]
        me = _index(_me())
        for a in range(n):
            for s_ref, d_ref, peer in copies_of(a, src[a], land[a], me):
                if peer is None:
                    pltpu.make_async_copy(s_ref, d_ref, self_sems.at[a]).start()
                else:
                    pltpu.make_async_remote_copy(src_ref=s_ref, dst_ref=d_ref, send_sem=send_sems.at[a], recv_sem=recv_sems.at[a],
                                                 device_id=peer, device_id_type=MESH).start()
        token[...] = jnp.zeros_like(token)

    outs = pl.pallas_call(
        body, name=name,
        out_shape=(_DMA((n,)), _DMA((n,)), _DMA((n,)), *[pltpu.HBM(s.shape, s.dtype) for s in srcs],
                   *[pltpu.HBM(l.shape, l.dtype) for l in lands], SDS((8, LANES), f32)),
        in_specs=[_HBM] * (2 * n), out_specs=(_SEM, _SEM, _SEM, *[_HBM] * (2 * n), pl.BlockSpec(memory_space=pltpu.VMEM)),
        input_output_aliases={i: 3 + i for i in range(2 * n)},
        compiler_params=pltpu.CompilerParams(has_side_effects=_EFFECT),
    )(*[_hbm(s) for s in srcs], *[_hbm(lax.empty(l.shape, l.dtype)) for l in lands])
    return dict(sems=outs[:3], srcs=list(outs[3:3 + n]), lands=list(outs[3 + n:3 + 2 * n]), token=outs[3 + 2 * n])


def _exchange_wait(started, waits_of, after, name, which=None):
    which = list(range(len(started["srcs"]))) if which is None else which
    srcs, lands = [started["srcs"][a] for a in which], [started["lands"][a] for a in which]
    n = len(which)

    def body(*refs):
        src = refs[:n]
        land = refs[n:2 * n]
        send_sems, recv_sems, self_sems = refs[2 * n:2 * n + 3]
        me = _index(_me())
        for pos, a in enumerate(which):
            seven, (s_ref, d_ref) = waits_of(a, src[pos], land[pos], me)
            both = pltpu.make_async_remote_copy(src_ref=seven, dst_ref=seven, send_sem=send_sems.at[a], recv_sem=recv_sems.at[a],
                                                device_id=_me(), device_id_type=MESH)
            both.wait_send()
            both.wait_recv()
            pltpu.make_async_copy(s_ref, d_ref, self_sems.at[a]).wait()

    outs = pl.pallas_call(
        body, name=name, out_shape=tuple(pltpu.HBM(t.shape, t.dtype) for t in srcs + lands),
        in_specs=[_HBM] * (2 * n) + [_SEM] * 3 + [_ANY], out_specs=tuple([_HBM] * (2 * n)),
        input_output_aliases={i: i for i in range(2 * n)},
        compiler_params=pltpu.CompilerParams(has_side_effects=_EFFECT),
    )(*srcs, *lands, *started["sems"], after)
    return list(outs[n:])


def gather_start(locs, axes, name):
    lands = [SDS(tuple(N_DEV * d if i == ax else d for i, d in enumerate(l.shape)), l.dtype) for l, ax in zip(locs, axes)]

    def copies_of(a, src, land, me):
        mine = _win(land, axes[a], me, src.shape[axes[a]])
        return [(src, mine, peer) for peer in _peers()] + [(src, mine, None)]

    return _exchange_start(locs, lands, copies_of, name)


def gather_wait(started, axes, after, name, which=None):
    def waits_of(a, src, land, me):
        size = src.shape[axes[a]]
        return _win(land, axes[a], 0, size, N_DEV - 1), (src, _win(land, axes[a], me, size))

    return _exchange_wait(started, waits_of, after, name, which)


def _part(ref, axis, k, stride, used):
    idx = [slice(None)] * len(ref.shape)
    idx[axis] = pl.ds(k * stride, used)
    return ref.at[tuple(idx)]


def scatter_start(grads, axes, name, used=None):
    strides = [g.shape[ax] // N_DEV for g, ax in zip(grads, axes)]
    used = used or strides
    lands = [SDS((N_DEV,) + tuple(u if i == ax else d for i, d in enumerate(g.shape)), g.dtype) for g, ax, u in zip(grads, axes, used)]

    def copies_of(a, src, land, me):
        out = [(_part(src, axes[a], _index(peer), strides[a], used[a]), land.at[me], peer) for peer in _peers()]
        return out + [(_part(src, axes[a], me, strides[a], used[a]), land.at[me], None)]

    return _exchange_start(grads, lands, copies_of, name)


def scatter_wait(started, axes, after, name, used=None):
    def waits_of(a, src, land, me):
        stride = src.shape[axes[a]] // N_DEV
        return land.at[pl.ds(0, N_DEV - 1)], (_part(src, axes[a], me, stride, used[a] if used else stride), land.at[me])

    return _exchange_wait(started, waits_of, after, name)


def _row_tile(rows, cap=512):
    return max(t for t in range(8, min(rows, cap) + 1, 8) if rows % t == 0)


_SMALL = [
    ("ln_mix_pre", (2, 1024)), ("ln_mix_post", (2, 1024)), ("ln_ffn_pre", (2, 1024)), ("ln_ffn_post", (2, 1024)),
    ("ln_mem", (2, 1024)), ("w_spatial", (1, 6, 128, 128)), ("b_spatial", (1, 6, 128)), ("ln_shared", (1024,)),
    ("b_forget", (12,)), ("ln_v_g", (1, 768)), ("ln_v_b", (1, 768)),
]
_SMALL_TILE = 8 * LANES


def _small_rows(shape):
    return -(-math.prod(shape) // _SMALL_TILE) * 8


def _pack_small(vals, shapes):
    parts = []
    for name, shape in shapes:
        flat = vals[name].reshape(-1).astype(f32)
        rows = _small_rows(shape)
        parts.append(jnp.pad(flat, (0, rows * LANES - flat.shape[0])).reshape(rows, LANES))
    return jnp.concatenate(parts, axis=0)


def _unpack_small(buf, shapes):
    out = {}
    lo = 0
    for name, shape in shapes:
        rows = _small_rows(shape)
        out[name] = buf[lo:lo + rows].reshape(-1)[:math.prod(shape)].reshape(shape)
        lo += rows
    return out


def kernel(x, mem, ln_mix_pre, ln_mix_post, ln_ffn_pre, ln_ffn_post, ln_mem, w_mem_kv, w_out, w_ffn_gate, w_ffn_up, w_ffn_down, w_in_a, w_spatial, b_spatial, ln_v_g, ln_v_b, ln_shared, w_shared_kv, b_forget, w_in_b, loss_target, m_ln_mix_pre, m_ln_mix_post, m_ln_ffn_pre, m_ln_ffn_post, m_ln_mem, m_w_mem_kv, m_w_out, m_w_ffn_gate, m_w_ffn_up, m_w_ffn_down, m_w_in_a, m_w_spatial, m_b_spatial, m_ln_v_g, m_ln_v_b, m_ln_shared, m_w_shared_kv, m_b_forget, m_w_in_b, v_ln_mix_pre, v_ln_mix_post, v_ln_ffn_pre, v_ln_ffn_post, v_ln_mem, v_w_mem_kv, v_w_out, v_w_ffn_gate, v_w_ffn_up, v_w_ffn_down, v_w_in_a, v_w_spatial, v_b_spatial, v_ln_v_g, v_ln_v_b, v_ln_shared, v_w_shared_kv, v_b_forget, v_w_in_b):
    weights = dict(ln_mix_pre=ln_mix_pre, ln_mix_post=ln_mix_post, ln_ffn_pre=ln_ffn_pre, ln_ffn_post=ln_ffn_post, ln_mem=ln_mem,
                   w_mem_kv=w_mem_kv, w_out=w_out, w_ffn_gate=w_ffn_gate, w_ffn_up=w_ffn_up, w_ffn_down=w_ffn_down, w_in_a=w_in_a,
                   w_spatial=w_spatial, b_spatial=b_spatial, ln_v_g=ln_v_g, ln_v_b=ln_v_b, ln_shared=ln_shared,
                   w_shared_kv=w_shared_kv, b_forget=b_forget, w_in_b=w_in_b)
    mom_m = dict(ln_mix_pre=m_ln_mix_pre, ln_mix_post=m_ln_mix_post, ln_ffn_pre=m_ln_ffn_pre, ln_ffn_post=m_ln_ffn_post, ln_mem=m_ln_mem,
                 w_mem_kv=m_w_mem_kv, w_out=m_w_out, w_ffn_gate=m_w_ffn_gate, w_ffn_up=m_w_ffn_up, w_ffn_down=m_w_ffn_down, w_in_a=m_w_in_a,
                 w_spatial=m_w_spatial, b_spatial=m_b_spatial, ln_v_g=m_ln_v_g, ln_v_b=m_ln_v_b, ln_shared=m_ln_shared,
                 w_shared_kv=m_w_shared_kv, b_forget=m_b_forget, w_in_b=m_w_in_b)
    mom_v = dict(ln_mix_pre=v_ln_mix_pre, ln_mix_post=v_ln_mix_post, ln_ffn_pre=v_ln_ffn_pre, ln_ffn_post=v_ln_ffn_post, ln_mem=v_ln_mem,
                 w_mem_kv=v_w_mem_kv, w_out=v_w_out, w_ffn_gate=v_w_ffn_gate, w_ffn_up=v_w_ffn_up, w_ffn_down=v_w_ffn_down, w_in_a=v_w_in_a,
                 w_spatial=v_w_spatial, b_spatial=v_b_spatial, ln_v_g=v_ln_v_g, ln_v_b=v_ln_v_b, ln_shared=v_ln_shared,
                 w_shared_kv=v_w_shared_kv, b_forget=v_b_forget, w_in_b=v_w_in_b)
    names = list(weights)
    mx, my, mc = lax.axis_index("x"), lax.axis_index("y"), lax.axis_index("c")
    me = 4 * mx + 2 * my + mc

    h0 = x[0]
    mem0 = mem[0]
    tgt = loss_target[0]
    seq = h0.shape[0]

    vec = lambda a: a.reshape(1, -1)
    pad_to = lambda a, axis, size: jnp.pad(a, [(0, size - a.shape[i] if i == axis else 0) for i in range(a.ndim)])

    def after(tok, a):
        return a + tok[0, 0].astype(a.dtype)

    lnv_loc = pad_to(jnp.concatenate([ln_v_g, ln_v_b], axis=0), 0, 8)
    st_a = gather_start([w_in_a.astype(bf16), pad_to(lnv_loc, 1, LANES)[None]], [0, 0], "gather_a_start")
    mix_locs = lambda l, tok: [after(tok, w_mem_kv[l]).astype(bf16), w_out[l].astype(bf16)]

    def ffn_gather_start(l, tok):
        gate_up = gather_start([pad_to(after(tok, w_ffn_gate[l]).astype(bf16), 1, FF_SHARD_PAD),
                                pad_to(w_ffn_up[l].astype(bf16), 1, FF_SHARD_PAD)], [1, 1], f"gather_gate_up{l}_start")
        down = gather_start([pad_to(after(gate_up["token"], w_ffn_down[l]).astype(bf16), 0, FF_SHARD_PAD)], [0], f"gather_down{l}_start")
        return gate_up, down

    st_b = [gather_start(mix_locs(0, st_a["token"]), [0, 0], "gather_b0_start"), None]
    st_c = ffn_gather_start(0, st_b[0]["token"])
    st_d = gather_start([after(st_c[1]["token"], w_in_b[0]).astype(bf16), pad_to(w_shared_kv.astype(bf16), 1, KV_PAD)], [0, 0],
                        "gather_d_start")
    st_b[1] = gather_start(mix_locs(1, st_d["token"]), [0, 0], "gather_b1_start")
    st_e = ffn_gather_start(1, st_b[1]["token"])
    ws = w_spatial[0].astype(bf16)
    ws_t = ws.transpose(0, 2, 1)
    bs_t = b_spatial[0].T

    (a0,) = rms_fwd(h0, [after(st_e[1]["token"], vec(ln_mix_pre[0]))], "a0_norm")
    w_in_a8, lnv8 = gather_wait(st_a, [0, 0], a0, "gather_a_wait")
    w_in_a_full = w_in_a8.transpose(1, 0, 2).reshape(D_MODEL, -1)
    lnv_g = lnv8[:, 0, :MAIN_WIDTH // N_DEV].reshape(1, MAIN_WIDTH)
    lnv_b = lnv8[:, 1, :MAIN_WIDTH // N_DEV].reshape(1, MAIN_WIDTH)
    proj0 = mm(a0, w_in_a_full, "proj0", tn=896)
    main0 = gmlp_fwd(proj0, ws, bs_t, lnv_g, lnv_b, "gmlp_fwd", out_width=D_MODEL)
    w_mkv, w_o = [None, None], [None, None]
    w_mkv[0], w_o[0] = gather_wait(st_b[0], [0, 0], main0, "gather_b0_wait")
    (memn0,) = rms_fwd(mem0, [vec(ln_mem[0])], "mem0_norm")
    kvm0 = mm(memn0, w_mkv[0], "kvm0")
    mixed0 = mem_attn_fwd(proj0, 2 * MAIN_WIDTH // MEM_WIDTH, kvm0, main0, "mem_attn0")
    y1_0, hmid0, f0 = mm_resnorm(mixed0, w_o[0], h0, vec(ln_mix_post[0]), [vec(ln_ffn_pre[0])], "mix_out0", tm=1024)
    w_g0, w_u0 = gather_wait(st_c[0], [1, 1], f0, "gather_gate_up0_wait")
    gu0, act0 = ffn_up(f0, w_g0, w_u0, "ffn_up0")
    (w_d0,) = gather_wait(st_c[1], [0], act0, "gather_down0_wait")
    y2_0, h1, a1, sin1 = mm_resnorm(act0, w_d0, hmid0, vec(ln_ffn_post[0]), [vec(ln_mix_pre[1]), vec(ln_shared)], "ffn_down0")

    w_inb, w_kv = gather_wait(st_d, [0, 0], sin1, "gather_d_wait")
    kvb = mm(sin1, w_kv, "kv_shared", out_dtype=bf16, tn=MAIN_WIDTH, ncols=2 * MAIN_WIDTH)
    zf = mm(sin1, w_kv, "forget_logits", tn=256, col0=2 * MAIN_WIDTH, ncols=256)
    qb = mm(a1, w_inb, "proj1", out_dtype=bf16)
    z_t = jnp.pad(zf[:, :FOX_HEADS].T, ((0, 16 - FOX_HEADS), (0, 0)))
    bf_col = jnp.pad(b_forget, (0, 16 - FOX_HEADS)).reshape(16, 1)
    c_t = fgate_fwd(z_t, bf_col, "fgate_fwd")
    c_row = c_t[:FOX_HEADS].reshape(FOX_PAIRS, 2, seq)
    main1, lse, main1_b = fox_fwd(qb, kvb, c_row, "fox_fwd", out_width=D_MODEL)
    w_mkv[1], w_o[1] = gather_wait(st_b[1], [0, 0], main1, "gather_b1_wait")
    (memn1,) = rms_fwd(mem0, [vec(ln_mem[1])], "mem1_norm")
    kvm1 = mm(memn1, w_mkv[1], "kvm1")
    mixed1 = mem_attn_fwd(qb, MAIN_WIDTH // MEM_WIDTH, kvm1, main1_b, "mem_attn1")
    y1_1, hmid1, f1 = mm_resnorm(mixed1, w_o[1], h1, vec(ln_mix_post[1]), [vec(ln_ffn_pre[1])], "mix_out1", tm=1024)
    w_g1, w_u1 = gather_wait(st_e[0], [1, 1], f1, "gather_gate_up1_wait")
    gu1, act1 = ffn_up(f1, w_g1, w_u1, "ffn_up1")
    (w_d1,) = gather_wait(st_e[1], [0], act1, "gather_down1_wait")
    dh, d_y2_1, dg_fpost1, loss_tile = mm_resnorm_loss(act1, w_d1, hmid1, vec(ln_ffn_post[1]), tgt, "ffn_down1_loss")
    ffn_w = [(w_g0, w_u0, w_d0), (w_g1, w_u1, w_d1)]
    ff_shard = w_ffn_down.shape[1]

    small = {}

    def ffn_backward(layer, dh_out, d_y2, hmid, f, gu, act, y1):
        w_g, w_u, w_d = ffn_w[layer]
        dw_down = mm_tn(act, d_y2, f"dw_down{layer}")
        rs_down = scatter_start([dw_down], [0], f"scatter_down{layer}_start", used=[ff_shard])
        d_g, d_u = ffn_act_grad(d_y2, w_d, gu, f"ffn_act_grad{layer}")
        dw_g = mm_tn(d_g, f, f"dw_gate{layer}", dep=rs_down["token"])
        dw_u = mm_tn(d_u, f, f"dw_up{layer}")
        rs_gate_up = scatter_start([dw_g, dw_u], [0, 0], f"scatter_gate_up{layer}_start", used=[ff_shard] * 2)
        dh_mid, d_y1, dg_fpre, dg_mpost = ffn_in_grad(d_g, d_u, w_g, w_u, hmid, dh_out, after(rs_gate_up["token"], vec(ln_ffn_pre[layer])),
                                                      y1, vec(ln_mix_post[layer]), f"ffn_in_grad{layer}")
        return dh_mid, d_y1, dg_fpre, dg_mpost, (rs_down, rs_gate_up)

    def mix_out_backward(layer, d_y1, mixed):
        dw_out = mm_tn(mixed, d_y1, f"dw_out{layer}")
        d_mixed = mm(d_y1, w_o[layer], f"d_mixed{layer}", trans_b=True)
        return d_mixed, dw_out

    def mem_backward(layer, q_src, q_block, kvm, memn, d_mixed, into):
        d_qm, d_kvm = mem_attn_bwd(q_src, q_block, kvm, d_mixed, into, f"mem_attn_bwd{layer}")
        d_kvm_b = d_kvm.astype(bf16)
        dw_mkv = mm_tn(memn, d_kvm_b, f"dw_mem_kv{layer}")
        d_memn = mm(d_kvm_b, w_mkv[layer], f"d_memn{layer}", trans_b=True)
        _, dg_mem = rms_bwd(mem0, vec(ln_mem[layer]), d_memn, None, bf16, f"mem_norm_bwd{layer}")
        return d_qm, dw_mkv, dg_mem


    dh_mid1, d_y1_1, dg_fpre1, dg_mpost1, rs_ffn1 = ffn_backward(1, dh, d_y2_1, hmid1, f1, gu1, act1, y1_1)
    d_mixed1, dw_out1 = mix_out_backward(1, d_y1_1, mixed1)
    dq_b, dk, dv, dc = fox_bwd(qb, kvb, d_mixed1, main1, lse, c_row, "fox_bwd", dq_width=D_MODEL)
    d_proj1, dw_mkv1, dg_mem1 = mem_backward(1, qb, MAIN_WIDTH // MEM_WIDTH, kvm1, memn1, d_mixed1, dq_b)
    rs_mix1 = scatter_start([dw_out1, dw_mkv1], [0, 0], "scatter_mix1_start")
    dc_t = jnp.pad(dc.reshape(FOX_HEADS, seq), ((0, 16 - FOX_HEADS), (0, 0)))
    dz_t, db_f = fgate_bwd(dc_t, z_t, bf_col, "fgate_bwd")
    d_kvf = jnp.concatenate([dk, dv, jnp.pad(dz_t[:FOX_HEADS].T.astype(bf16), ((0, 0), (0, KV_PAD - KV_WIDTH)))], axis=-1)
    dw_in_b = mm_tn(a1, d_proj1, "dw_in_b", dep=rs_mix1["token"])
    dw_kv = mm_tn(sin1, d_kvf, "dw_kv", tn=896)
    rs_2 = scatter_start([dw_in_b, dw_kv], [0, 0], "scatter_shared_start")
    dh1, (dg_pre1, dg_shared), d_y2_0, dg_fpost0 = proj_in_grad(
        [(d_proj1, w_inb, vec(ln_mix_pre[1])), (d_kvf, w_kv, vec(ln_shared))], h1, dh_mid1, "in_grad1", dep=rs_2["token"],
        below=(y2_0, vec(ln_ffn_post[0])))

    dh_mid0, d_y1_0, dg_fpre0, dg_mpost0, rs_ffn0 = ffn_backward(0, dh1, d_y2_0, hmid0, f0, gu0, act0, y1_0)
    d_mixed0, dw_out0 = mix_out_backward(0, d_y1_0, mixed0)
    d_uv, dw_s, db_s, dg_lnv, db_lnv = gmlp_bwd(proj0, d_mixed0, ws, ws_t, bs_t, lnv_g, lnv_b, "gmlp_bwd", out_width=w_in_a_full.shape[1])
    d_proj0, dw_mkv0, dg_mem0 = mem_backward(0, proj0, 2 * MAIN_WIDTH // MEM_WIDTH, kvm0, memn0, d_mixed0, d_uv)
    rs_mix0 = scatter_start([dw_out0, dw_mkv0], [0, 0], "scatter_mix0_start")

    small["ln_mix_pre"] = jnp.concatenate([jnp.zeros_like(dg_pre1), dg_pre1], axis=0)
    small["ln_mix_post"] = jnp.concatenate([dg_mpost0, dg_mpost1], axis=0)
    small["ln_ffn_pre"] = jnp.concatenate([dg_fpre0, dg_fpre1], axis=0)
    small["ln_ffn_post"] = jnp.concatenate([dg_fpost0, dg_fpost1], axis=0)
    small["ln_mem"] = jnp.concatenate([dg_mem0, dg_mem1], axis=0)
    small["w_spatial"] = dw_s[None]
    small["b_spatial"] = db_s[:, :A_GROUPS].T[None]
    small["ln_shared"] = dg_shared[0]
    small["b_forget"] = db_f[:FOX_HEADS, 0]
    small["ln_v_g"] = dg_lnv
    small["ln_v_b"] = db_lnv
    small_rows = jnp.concatenate([_pack_small(small, _SMALL), after(rs_mix0["token"], loss_tile)], axis=0)
    st_small = gather_start([small_rows[None]], [0], "gather_small_grads_start")
    dw_in_a_t = mm_tn(d_proj0, a0, "dw_in_a", tk=896, dep=st_small["token"])
    rs_in_a = scatter_start([dw_in_a_t], [0], "scatter_in_a_start")
    grad_x, (dg_pre0,) = proj_in_grad([(d_proj0, w_in_a_full, vec(ln_mix_pre[0]))], h0, dh_mid0, "in_grad0", dep=rs_in_a["token"])
    st_last = gather_start([dg_pre0.reshape(1, 8, LANES)], [0], "gather_last_grad_start")

    (p_down1,) = scatter_wait(rs_ffn1[0], [0], after(st_last["token"], grad_x[:8, :LANES]), "scatter_down1_wait", used=[ff_shard])
    p_gate1, p_up1 = scatter_wait(rs_ffn1[1], [0, 0], p_down1, "scatter_gate_up1_wait", used=[ff_shard] * 2)
    p_out1, p_mkv1 = scatter_wait(rs_mix1, [0, 0], p_gate1, "scatter_mix1_wait")
    p_in_b, p_kv = scatter_wait(rs_2, [0, 0], p_out1, "scatter_shared_wait")
    (p_down0,) = scatter_wait(rs_ffn0[0], [0], p_in_b, "scatter_down0_wait", used=[ff_shard])
    p_gate0, p_up0 = scatter_wait(rs_ffn0[1], [0, 0], p_down0, "scatter_gate_up0_wait", used=[ff_shard] * 2)
    p_out0, p_mkv0 = scatter_wait(rs_mix0, [0, 0], p_gate0, "scatter_mix0_wait")
    owned_parts = dict(w_ffn_gate=[p_gate0, p_gate1], w_ffn_up=[p_up0, p_up1], w_ffn_down=[p_down0, p_down1], w_out=[p_out0, p_out1],
                       w_mem_kv=[p_mkv0, p_mkv1], w_in_b=[p_in_b], w_shared_kv=[p_kv])

    grad_w, delta, new_m, new_v = {}, {}, {}, {}
    transposed = ("w_ffn_gate", "w_ffn_up", "w_in_a")

    def adamw_sharded(n, parts):
        shape = weights[n].shape
        three_d = shape if len(shape) == 3 else (1,) + shape
        view = (lambda t: t.reshape(three_d).transpose(0, 2, 1)) if n in transposed else (lambda t: t.reshape(three_d))
        back = (lambda t: t.transpose(0, 2, 1).reshape(shape)) if n in transposed else (lambda t: t.reshape(shape))
        w_view = view(weights[n])
        outs = adamw_owned(w_view, parts, view(mom_m[n]), view(mom_v[n]), f"adamw_{n}", tr=_row_tile(w_view.shape[1]))
        grad_w[n], delta[n], new_m[n], new_v[n] = (back(t) for t in outs)

    for n, parts in owned_parts.items():
        adamw_sharded(n, parts)
    (p_in_a,) = scatter_wait(rs_in_a, [0], delta["w_shared_kv"], "scatter_in_a_wait")
    adamw_sharded("w_in_a", [p_in_a])
    (small_all,) = gather_wait(st_small, [0], p_in_a, "gather_small_grads_wait")
    (last_all,) = gather_wait(st_last, [0], small_all, "gather_last_grad_wait")
    small_sum = sum_leading(small_all, "sum_small_grads")
    loss = small_sum[small_rows.shape[0] - 1, 0]
    g_small = _unpack_small(small_sum, _SMALL)
    g_small["ln_mix_pre"] = jnp.concatenate([sum_leading(last_all, "sum_last_grad").reshape(1, D_MODEL), g_small["ln_mix_pre"][1:]], axis=0)
    shard = MAIN_WIDTH // N_DEV
    for n in ("ln_v_g", "ln_v_b"):
        g_small[n] = lax.dynamic_slice_in_dim(g_small[n], me * shard, shard, axis=1)
    grad_w.update(g_small)
    small_local_shapes = [(n, tuple(weights[n].shape)) for n, _ in _SMALL]
    packed = [_pack_small(src, small_local_shapes) for src in (weights, grad_w, mom_m, mom_v)]
    outs = adamw(*packed, "adamw_small", tr=packed[0].shape[0])
    for dst, buf in zip((delta, new_m, new_v), outs):
        dst.update(_unpack_small(buf, small_local_shapes))

    return (loss, grad_x[None], *[grad_w[n] for n in names], *[delta[n] for n in names],
            *[new_m[n] for n in names], *[new_v[n] for n in names])
```

```python
import functools
import math

import jax
import jax.numpy as jnp
from jax import lax
from jax.experimental import pallas as pl
from jax.experimental.pallas import tpu as pltpu

f32 = jnp.float32
bf16 = jnp.bfloat16
SDS = jax.ShapeDtypeStruct

D_MODEL = 1024
MAIN_WIDTH = 768
MEM_WIDTH = 256
HEAD_DIM = 64
MEM_HEADS = 4
FOX_HEADS = 12
FOX_PAIRS = FOX_HEADS // 2
CHUNK = 128
A_GROUPS = 6
FF_SHARD_PAD = 384
KV_WIDTH = 2 * MAIN_WIDTH + FOX_HEADS
KV_PAD = 1792
RMS_EPS = 1e-6
LN_EPS = 1e-5
ATT_SCALE = HEAD_DIM ** -0.5
ADAM_LR, ADAM_B1, ADAM_B2, ADAM_EPS, ADAM_WD, ADAM_STEP = 0.001, 0.9, 0.999, 1e-08, 0.01, 10
N_DEV = 8
MESH = pl.DeviceIdType.MESH
V7X_VMEM_LIMIT = 56 * 1024 * 1024
LANES = 128


def _cparams(*sem):
    return pltpu.CompilerParams(dimension_semantics=sem or None, vmem_limit_bytes=V7X_VMEM_LIMIT)


def _dot(a, b):
    return jnp.dot(a, b, preferred_element_type=f32)


def _dot_nt(a, b):
    return lax.dot_general(a, b, (((1,), (1,)), ((), ())), preferred_element_type=f32)


def _dot_tn(a, b):
    return lax.dot_general(a, b, (((0,), (0,)), ((), ())), preferred_element_type=f32)


def _gelu(x):
    k = math.sqrt(2.0 / math.pi)
    t = jnp.tanh(k * (x + 0.044715 * x * x * x))
    return 0.5 * x * (1.0 + t), t


def _gelu_grad(x, t):
    k = math.sqrt(2.0 / math.pi)
    return 0.5 * (1.0 + t) + 0.5 * x * (1.0 - t * t) * k * (1.0 + 3.0 * 0.044715 * x * x)


def _sigmoid(x):
    return 1.0 / (1.0 + jnp.exp(-x))


def rms_fwd(x, gains, name, tm=512):
    m, d = x.shape
    tm = min(tm, m)
    n = len(gains)

    def body(x_ref, *refs):
        xv = x_ref[...]
        y = xv * lax.rsqrt(jnp.sum(xv * xv, axis=-1, keepdims=True) * (1.0 / d) + RMS_EPS)
        for g_ref, o_ref in zip(refs[:n], refs[n:]):
            o_ref[...] = (y * g_ref[...]).astype(bf16)

    row = pl.BlockSpec((tm, d), lambda i: (i, 0))
    vec = pl.BlockSpec((1, d), lambda i: (0, 0))
    return pl.pallas_call(body, grid=(m // tm,), in_specs=[row] + [vec] * n, out_specs=[row] * n,
                          out_shape=[SDS((m, d), bf16)] * n, name=name, compiler_params=_cparams("parallel"))(x, *gains)


def rms_bwd(x, g, dy, add, out_dtype, name, tm=512):
    m, d = x.shape
    tm = min(tm, m)
    has_add = add is not None

    def body(x_ref, g_ref, dy_ref, *refs):
        dx_ref, dg_ref = refs[-2], refs[-1]
        xv = x_ref[...]
        dyv = dy_ref[...].astype(f32)
        r = lax.rsqrt(jnp.sum(xv * xv, axis=-1, keepdims=True) * (1.0 / d) + RMS_EPS)
        xn = xv * r
        dyg = dyv * g_ref[...]
        dx = r * (dyg - xn * (jnp.sum(dyg * xn, axis=-1, keepdims=True) * (1.0 / d)))
        if has_add:
            dx = dx + refs[0][...]
        dx_ref[...] = dx.astype(out_dtype)

        @pl.when(pl.program_id(0) == 0)
        def _():
            dg_ref[...] = jnp.zeros_like(dg_ref)

        dg_ref[...] += jnp.sum(dyv * xn, axis=0, keepdims=True)

    row = pl.BlockSpec((tm, d), lambda i: (i, 0))
    vec = pl.BlockSpec((1, d), lambda i: (0, 0))
    ins = [x, g, dy] + ([add] if has_add else [])
    return pl.pallas_call(body, grid=(m // tm,), in_specs=[row, vec, row] + ([row] if has_add else []),
                          out_specs=[row, vec], out_shape=[SDS((m, d), out_dtype), SDS((1, d), f32)], name=name,
                          compiler_params=_cparams("arbitrary"))(*ins)


def mm(a, b, name, trans_b=False, out_dtype=f32, tm=2048, tn=1024, col0=0, ncols=None, dep=None):
    m, k = a.shape
    n_all = b.shape[0] if trans_b else b.shape[1]
    n = n_all if ncols is None else ncols
    tm, tn = min(tm, m), min(tn, n)
    assert m % tm == 0 and n % tn == 0 and col0 % tn == 0 and not (trans_b and col0), (name, m, n, tm, tn)
    jb = col0 // tn

    def body(a_ref, b_ref, *rest):
        r = _dot_nt(a_ref[...], b_ref[...]) if trans_b else _dot(a_ref[...], b_ref[...])
        rest[-1][...] = r.astype(out_dtype)

    if trans_b:
        b_spec = pl.BlockSpec((tn, k), lambda j, i: (j, 0))
    else:
        b_spec = pl.BlockSpec((k, tn), lambda j, i: (0, jb + j))
    deps = [] if dep is None else [dep]
    dep_specs = [pl.BlockSpec((8, LANES), lambda j, i: (0, 0))] * len(deps)
    return pl.pallas_call(body, grid=(n // tn, m // tm), in_specs=[pl.BlockSpec((tm, k), lambda j, i: (i, 0)), b_spec] + dep_specs,
                          out_specs=pl.BlockSpec((tm, tn), lambda j, i: (i, j)), out_shape=SDS((m, n), out_dtype),
                          name=name, compiler_params=_cparams("parallel", "parallel"))(a, b, *deps)


def mm_tn(a, g, name, tk=1024, tn=1024, out_dtype=bf16, dep=None):
    s, k = a.shape
    n = g.shape[1]
    tk, tn = min(tk, k), min(tn, n)
    assert k % tk == 0 and n % tn == 0, (name, k, n, tk, tn)

    def body(a_ref, g_ref, *rest):
        rest[-1][...] = _dot_tn(a_ref[...], g_ref[...]).astype(out_dtype)

    deps = [] if dep is None else [dep]
    dep_specs = [pl.BlockSpec((8, LANES), lambda i, j: (0, 0))] * len(deps)
    return pl.pallas_call(body, grid=(k // tk, n // tn),
                          in_specs=[pl.BlockSpec((s, tk), lambda i, j: (0, i)), pl.BlockSpec((s, tn), lambda i, j: (0, j))] + dep_specs,
                          out_specs=pl.BlockSpec((tk, tn), lambda i, j: (i, j)), out_shape=SDS((k, n), out_dtype), name=name,
                          compiler_params=_cparams("parallel", "parallel"))(a, g, *deps)


def _resident(shape, index_map):
    return pl.BlockSpec(shape, index_map, pipeline_mode=pl.Buffered(1))


def _rms(xv):
    return xv * lax.rsqrt(jnp.sum(xv * xv, axis=-1, keepdims=True) * (1.0 / xv.shape[-1]) + RMS_EPS)


def _rms_bwd_math(xv, g, dy):
    d = xv.shape[-1]
    r = lax.rsqrt(jnp.sum(xv * xv, axis=-1, keepdims=True) * (1.0 / d) + RMS_EPS)
    xn = xv * r
    dyg = dy * g
    dx = r * (dyg - xn * (jnp.sum(dyg * xn, axis=-1, keepdims=True) * (1.0 / d)))
    return dx, jnp.sum(dy * xn, axis=0, keepdims=True)


SUB_ROWS = 512


def mm_resnorm(a, b, h, g_post, gains, name, tm=512):
    m, k = a.shape
    d = b.shape[1]
    n = len(gains)

    def body(a_ref, b_ref, h_ref, gp_ref, *refs):
        sub = max(tm, SUB_ROWS)
        for r in range(tm // sub):
            rows = slice(r * sub, (r + 1) * sub)
            y = _dot(a_ref[rows, :], b_ref[...])
            refs[n][rows, :] = y
            hn = h_ref[rows, :] + _rms(y) * gp_ref[...]
            refs[n + 1][rows, :] = hn
            if n:
                z = _rms(hn)
                for g_ref, o_ref in zip(refs[:n], refs[n + 2:]):
                    o_ref[rows, :] = (z * g_ref[...]).astype(bf16)

    row = pl.BlockSpec((tm, d), lambda i: (i, 0))
    vec = pl.BlockSpec((1, d), lambda i: (0, 0))
    return pl.pallas_call(body, grid=(m // tm,),
                          in_specs=[pl.BlockSpec((tm, k), lambda i: (i, 0)), _resident((k, d), lambda i: (0, 0)), row, vec] + [vec] * n,
                          out_specs=[row] * (n + 2), out_shape=[SDS((m, d), f32)] * 2 + [SDS((m, d), bf16)] * n, name=name,
                          compiler_params=_cparams("parallel"))(a, b, h, g_post, *gains)


def mm_resnorm_loss(a, b, h, g_post, tgt, name, tm=512):
    m, k = a.shape
    d = b.shape[1]

    def body(a_ref, b_ref, h_ref, gp_ref, t_ref, dh_ref, dy_ref, dg_ref, l_ref):
        @pl.when(pl.program_id(0) == 0)
        def _():
            dg_ref[...] = jnp.zeros_like(dg_ref)
            l_ref[...] = jnp.zeros_like(l_ref)

        y = _dot(a_ref[...], b_ref[...])
        e = h_ref[...] + _rms(y) * gp_ref[...] - t_ref[...]
        dh = e * (1.0 / d)
        dh_ref[...] = dh
        part = jnp.sum(jnp.sum(e * e, axis=-1, keepdims=True), axis=0, keepdims=True) * (0.5 / d)
        l_ref[...] += jnp.broadcast_to(part, l_ref.shape)
        dy, dg = _rms_bwd_math(y, gp_ref[...], dh)
        dy_ref[...] = dy.astype(bf16)
        dg_ref[...] += dg

    row = pl.BlockSpec((tm, d), lambda i: (i, 0))
    vec = pl.BlockSpec((1, d), lambda i: (0, 0))
    return pl.pallas_call(body, grid=(m // tm,),
                          in_specs=[pl.BlockSpec((tm, k), lambda i: (i, 0)), _resident((k, d), lambda i: (0, 0)), row, vec, row],
                          out_specs=[row, row, vec, pl.BlockSpec((8, LANES), lambda i: (0, 0))],
                          out_shape=[SDS((m, d), f32), SDS((m, d), bf16), SDS((1, d), f32), SDS((8, LANES), f32)], name=name,
                          compiler_params=_cparams("arbitrary"))(a, b, h, g_post, tgt)


FFN_TILE = 1536


def ffn_act_grad(d_y2, w_d, factors, name, tm=1024):
    s, d = d_y2.shape
    ff = w_d.shape[0]
    tn = FFN_TILE
    nb = ff // tn

    def body(a_ref, b_ref, g_ref, u_ref, dg_ref, du_ref):
        av = a_ref[...]
        tc = 256
        for c in range(tn // tc):
            cols = slice(c * tc, (c + 1) * tc)
            da = _dot_nt(av, b_ref[cols, :])
            dg_ref[:, cols] = (da * g_ref[:, cols].astype(f32)).astype(bf16)
            du_ref[:, cols] = (da * u_ref[:, cols].astype(f32)).astype(bf16)

    tile = pl.BlockSpec((tm, tn), lambda j, i: (i, j))
    return pl.pallas_call(body, grid=(nb, s // tm),
                          in_specs=[pl.BlockSpec((tm, d), lambda j, i: (i, 0)), pl.BlockSpec((tn, d), lambda j, i: (j, 0)),
                                    pl.BlockSpec((tm, tn), lambda j, i: (i, 2 * j)), pl.BlockSpec((tm, tn), lambda j, i: (i, 2 * j + 1))],
                          out_specs=[tile, tile], out_shape=[SDS((s, ff), bf16)] * 2, name=name,
                          compiler_params=_cparams("parallel", "parallel"))(d_y2, w_d, factors, factors)


def ffn_in_grad(d_g, d_u, w_g, w_u, hmid, dh_out, g_pre, y1, g_post, name, tm=512):
    s, ff = d_g.shape
    d = w_g.shape[0]

    def body(dg_ref, du_ref, wg_ref, wu_ref, hm_ref, dho_ref, gpre_ref, y1_ref, gpost_ref, dhm_ref, dy1_ref, dgpre_ref, dgpost_ref):
        @pl.when(pl.program_id(0) == 0)
        def _():
            dgpre_ref[...] = jnp.zeros_like(dgpre_ref)
            dgpost_ref[...] = jnp.zeros_like(dgpost_ref)

        for r in range(tm // SUB_ROWS):
            rows = slice(r * SUB_ROWS, (r + 1) * SUB_ROWS)
            d_f = _dot_nt(dg_ref[rows, :], wg_ref[...]) + _dot_nt(du_ref[rows, :], wu_ref[...])
            dx, dg1 = _rms_bwd_math(hm_ref[rows, :], gpre_ref[...], d_f)
            dh_mid = dho_ref[rows, :] + dx
            dhm_ref[rows, :] = dh_mid
            dgpre_ref[...] += dg1
            dy1, dg2 = _rms_bwd_math(y1_ref[rows, :], gpost_ref[...], dh_mid)
            dy1_ref[rows, :] = dy1.astype(bf16)
            dgpost_ref[...] += dg2

    row = pl.BlockSpec((tm, d), lambda i: (i, 0))
    vec = pl.BlockSpec((1, d), lambda i: (0, 0))
    wide = pl.BlockSpec((tm, ff), lambda i: (i, 0))
    w_spec = _resident((d, ff), lambda i: (0, 0))
    return pl.pallas_call(body, grid=(s // tm,), in_specs=[wide, wide, w_spec, w_spec, row, row, vec, row, vec],
                          out_specs=[row, row, vec, vec], out_shape=[SDS((s, d), f32), SDS((s, d), bf16), SDS((1, d), f32), SDS((1, d), f32)],
                          name=name, compiler_params=_cparams("arbitrary"))(d_g, d_u, w_g, w_u, hmid, dh_out, g_pre, y1, g_post)


def proj_in_grad(pairs, x, add, name, tm=512, dep=None, below=None):
    s, d = x.shape
    n = len(pairs)
    extra = [] if dep is None else [dep]
    n_below = 0 if below is None else 2

    def body(*refs):
        x_ref, add_ref = refs[3 * n], refs[3 * n + 1]
        below_refs = refs[3 * n + 2:3 * n + 2 + n_below]
        outs = refs[3 * n + 2 + n_below + len(extra):]

        @pl.when(pl.program_id(0) == 0)
        def _():
            for o in outs[1:1 + n] + outs[2 + n:]:
                o[...] = jnp.zeros_like(o)

        xv = x_ref[...]
        dx = add_ref[...]
        for i in range(n):
            a_ref, b_ref, g_ref = refs[3 * i:3 * i + 3]
            dxi, dgi = _rms_bwd_math(xv, g_ref[...], _dot_nt(a_ref[...], b_ref[...]))
            dx = dx + dxi
            outs[1 + i][...] += dgi
        outs[0][...] = dx
        if below is not None:
            dy, dg = _rms_bwd_math(below_refs[0][...], below_refs[1][...], dx)
            outs[1 + n][...] = dy.astype(bf16)
            outs[2 + n][...] += dg

    row = pl.BlockSpec((tm, d), lambda i: (i, 0))
    vec = pl.BlockSpec((1, d), lambda i: (0, 0))
    in_specs, args = [], []
    for a, b, g in pairs:
        k = a.shape[1]
        in_specs += [pl.BlockSpec((tm, k), lambda i: (i, 0)), _resident((d, k), lambda i: (0, 0)), vec]
        args += [a, b, g]
    in_specs += [row, row] + [row, vec][:n_below] + [pl.BlockSpec((8, LANES), lambda i: (0, 0))] * len(extra)
    out_specs = [row] + [vec] * n + [row, vec][:n_below]
    out_shape = [SDS((s, d), f32)] + [SDS((1, d), f32)] * n + [SDS((s, d), bf16), SDS((1, d), f32)][:n_below]
    out = pl.pallas_call(body, grid=(s // tm,), in_specs=in_specs, out_specs=out_specs, out_shape=out_shape, name=name,
                         compiler_params=_cparams("arbitrary"))(*args, x, add, *(below or ()), *extra)
    return (out[0], out[1:1 + n]) + tuple(out[1 + n:])


def ffn_up(f, wg, wu, name, tm=1024, tc=256):
    s, d = f.shape
    ff = wg.shape[-1]
    tn = FFN_TILE

    def body(f_ref, wg_ref, wu_ref, fac_ref, act_ref):
        fv = f_ref[...]
        for c in range(tn // tc):
            lo = c * tc
            gg = _dot(fv, wg_ref[:, lo:lo + tc])
            uu = _dot(fv, wu_ref[:, lo:lo + tc])
            sg = _sigmoid(gg)
            silu = gg * sg
            fac_ref[:, lo:lo + tc] = (uu * (sg + silu * (1.0 - sg))).astype(bf16)
            fac_ref[:, tn + lo:tn + lo + tc] = silu.astype(bf16)
            act_ref[:, lo:lo + tc] = (silu * uu).astype(bf16)

    w_spec = pl.BlockSpec((d, tn), lambda j, i: (0, j))
    return pl.pallas_call(body, grid=(ff // tn, s // tm), in_specs=[pl.BlockSpec((tm, d), lambda j, i: (i, 0)), w_spec, w_spec],
                          out_specs=[pl.BlockSpec((tm, 2 * tn), lambda j, i: (i, j)), pl.BlockSpec((tm, tn), lambda j, i: (i, j))],
                          out_shape=[SDS((s, 2 * ff), bf16), SDS((s, ff), bf16)], name=name,
                          compiler_params=_cparams("parallel", "parallel"))(f, wg, wu)


def _gmlp_forward_chunk(u, v, w_refs, bias, ln_g, ln_b):
    gu, tu = _gelu(u)
    gv, tv = _gelu(v)
    mu = jnp.sum(gv, axis=-1, keepdims=True) * (1.0 / MAIN_WIDTH)
    xc = gv - mu
    rstd = lax.rsqrt(jnp.sum(xc * xc, axis=-1, keepdims=True) * (1.0 / MAIN_WIDTH) + LN_EPS)
    xhat = xc * rstd
    vln = xhat * ln_g + ln_b
    row = lax.broadcasted_iota(jnp.int32, (CHUNK, CHUNK), 0)
    col = lax.broadcasted_iota(jnp.int32, (CHUNK, CHUNK), 1)
    s_parts = []
    for g in range(A_GROUPS):
        w = jnp.where(col <= row, w_refs[g], jnp.zeros((), bf16))
        s_parts.append(_dot(w, vln[:, g * CHUNK:(g + 1) * CHUNK].astype(bf16)) + bias[:, g:g + 1])
    return gu, tu, tv, rstd, xhat, vln, s_parts


def gmlp_fwd(proj, ws, bs_t, ln_g, ln_b, name, tm=512, out_width=MAIN_WIDTH):
    s = proj.shape[0]

    def body(u_ref, v_ref, w_ref, b_ref, g_ref, bb_ref, o_ref):
        bias = b_ref[...]
        for c in range(tm // CHUNK):
            rows = slice(c * CHUNK, (c + 1) * CHUNK)
            gu, _, _, _, _, _, s_parts = _gmlp_forward_chunk(u_ref[rows, :], v_ref[rows, :], w_ref, bias, g_ref[...], bb_ref[...])
            for g in range(A_GROUPS):
                cols = slice(g * CHUNK, (g + 1) * CHUNK)
                o_ref[rows, cols] = (gu[:, cols] * s_parts[g]).astype(bf16)

    vec = pl.BlockSpec((1, MAIN_WIDTH), lambda i: (0, 0))
    return pl.pallas_call(
        body, grid=(s // tm,),
        in_specs=[pl.BlockSpec((tm, MAIN_WIDTH), lambda i: (i, 0)), pl.BlockSpec((tm, MAIN_WIDTH), lambda i: (i, 1)),
                  pl.BlockSpec((A_GROUPS, CHUNK, CHUNK), lambda i: (0, 0, 0)), pl.BlockSpec((CHUNK, A_GROUPS), lambda i: (0, 0)), vec, vec],
        out_specs=pl.BlockSpec((tm, MAIN_WIDTH), lambda i: (i, 0)), out_shape=SDS((s, out_width), bf16), name=name,
        compiler_params=_cparams("parallel"))(proj, proj, ws, bs_t, ln_g, ln_b)


def gmlp_bwd(proj, d_mixed, ws, ws_t, bs_t, ln_g, ln_b, name, tm=512, out_width=2 * MAIN_WIDTH):
    s = proj.shape[0]

    def body(u_ref, v_ref, dm_ref, w_ref, wt_ref, b_ref, g_ref, bb_ref, duv_ref, dw_ref, db_ref, dg_ref, dbb_ref):
        @pl.when(pl.program_id(0) == 0)
        def _():
            dw_ref[...] = jnp.zeros_like(dw_ref)
            db_ref[...] = jnp.zeros_like(db_ref)
            dg_ref[...] = jnp.zeros_like(dg_ref)
            dbb_ref[...] = jnp.zeros_like(dbb_ref)

        bias = b_ref[...]
        ln_gv = g_ref[...]
        row = lax.broadcasted_iota(jnp.int32, (CHUNK, CHUNK), 0)
        col = lax.broadcasted_iota(jnp.int32, (CHUNK, CHUNK), 1)
        lane = lax.broadcasted_iota(jnp.int32, (CHUNK, LANES), 1)
        for c in range(tm // CHUNK):
            rows = slice(c * CHUNK, (c + 1) * CHUNK)
            u = u_ref[rows, :]
            v = v_ref[rows, :]
            gu, tu, tv, rstd, xhat, vln, s_parts = _gmlp_forward_chunk(u, v, w_ref, bias, ln_gv, bb_ref[...])
            dm = dm_ref[rows, :]
            d_vln_parts = []
            d_gu_parts = []
            db_acc = jnp.zeros((CHUNK, LANES), f32)
            for g in range(A_GROUPS):
                cols = slice(g * CHUNK, (g + 1) * CHUNK)
                dmg = dm[:, cols]
                d_gu_parts.append(dmg * s_parts[g])
                d_s = dmg * gu[:, cols]
                db_acc = db_acc + jnp.where(lane == g, jnp.sum(d_s, axis=-1, keepdims=True), 0.0)
                d_sb = d_s.astype(bf16)
                dw_ref[g] += jnp.where(col <= row, _dot_nt(d_sb, vln[:, cols].astype(bf16)), 0.0)
                wt = jnp.where(row <= col, wt_ref[g], jnp.zeros((), bf16))
                d_vln_parts.append(_dot(wt, d_sb))
            db_ref[...] += db_acc
            d_vln = jnp.concatenate(d_vln_parts, axis=-1)
            d_gu = jnp.concatenate(d_gu_parts, axis=-1)
            dg_ref[...] += jnp.sum(d_vln * xhat, axis=0, keepdims=True)
            dbb_ref[...] += jnp.sum(d_vln, axis=0, keepdims=True)
            dxh = d_vln * ln_gv
            m1 = jnp.sum(dxh, axis=-1, keepdims=True) * (1.0 / MAIN_WIDTH)
            m2 = jnp.sum(dxh * xhat, axis=-1, keepdims=True) * (1.0 / MAIN_WIDTH)
            d_gv = rstd * (dxh - m1 - xhat * m2)
            duv_ref[rows, :MAIN_WIDTH] = (d_gu * _gelu_grad(u, tu)).astype(bf16)
            duv_ref[rows, MAIN_WIDTH:] = (d_gv * _gelu_grad(v, tv)).astype(bf16)

    vec = pl.BlockSpec((1, MAIN_WIDTH), lambda i: (0, 0))
    wspec = pl.BlockSpec((A_GROUPS, CHUNK, CHUNK), lambda i: (0, 0, 0))
    return pl.pallas_call(
        body, grid=(s // tm,),
        in_specs=[pl.BlockSpec((tm, MAIN_WIDTH), lambda i: (i, 0)), pl.BlockSpec((tm, MAIN_WIDTH), lambda i: (i, 1)),
                  pl.BlockSpec((tm, MAIN_WIDTH), lambda i: (i, 0)), wspec, wspec, pl.BlockSpec((CHUNK, A_GROUPS), lambda i: (0, 0)), vec, vec],
        out_specs=[pl.BlockSpec((tm, 2 * MAIN_WIDTH), lambda i: (i, 0)), wspec, pl.BlockSpec((CHUNK, LANES), lambda i: (0, 0)), vec, vec],
        out_shape=[SDS((s, out_width), bf16), SDS((A_GROUPS, CHUNK, CHUNK), f32), SDS((CHUNK, LANES), f32),
                   SDS((1, MAIN_WIDTH), f32), SDS((1, MAIN_WIDTH), f32)],
        name=name, compiler_params=_cparams("arbitrary"))(proj, proj, d_mixed, ws, ws_t, bs_t, ln_g, ln_b)


def _head_mask(width, h):
    lane = lax.broadcasted_iota(jnp.int32, (1, width), 1)
    return (lane >= h * HEAD_DIM) & (lane < (h + 1) * HEAD_DIM)


def mem_attn_fwd(proj, q_block, kv, into, name, tm=1024):
    s = proj.shape[0]
    n_mem = kv.shape[0]
    out_block = into.shape[1] // MEM_WIDTH - 1

    def body(q_ref, kv_ref, into_ref, o_ref):
        q = q_ref[...].astype(f32)
        k = kv_ref[:, :MEM_WIDTH].astype(bf16)
        v = kv_ref[:, MEM_WIDTH:].astype(bf16)
        out = jnp.zeros((tm, MEM_WIDTH), f32)
        for h in range(MEM_HEADS):
            msk = _head_mask(MEM_WIDTH, h)
            qh = jnp.where(msk, q, 0.0).astype(bf16)
            sc = _dot_nt(qh, k) * ATT_SCALE
            e = jnp.exp(sc - jnp.max(sc, axis=-1, keepdims=True))
            p = e / jnp.sum(e, axis=-1, keepdims=True)
            out = jnp.where(msk, _dot(p.astype(bf16), v), out)
        o_ref[...] = out.astype(bf16)

    return pl.pallas_call(body, grid=(s // tm,),
                          in_specs=[pl.BlockSpec((tm, MEM_WIDTH), lambda i: (i, q_block)), pl.BlockSpec((n_mem, 2 * MEM_WIDTH), lambda i: (0, 0)), _ANY],
                          out_specs=pl.BlockSpec((tm, MEM_WIDTH), lambda i: (i, out_block)), out_shape=SDS(into.shape, bf16), name=name,
                          input_output_aliases={2: 0}, compiler_params=_cparams("parallel"))(proj, kv, into)


def mem_attn_bwd(proj, q_block, kv, d_mixed, into, name, tm=1024):
    s = proj.shape[0]
    n_mem = kv.shape[0]
    out_block = into.shape[1] // MEM_WIDTH - 1

    def body(q_ref, kv_ref, do_ref, into_ref, dq_ref, dkv_ref):
        @pl.when(pl.program_id(0) == 0)
        def _():
            dkv_ref[...] = jnp.zeros_like(dkv_ref)

        q = q_ref[...].astype(f32)
        do = do_ref[...]
        k = kv_ref[:, :MEM_WIDTH].astype(bf16)
        v = kv_ref[:, MEM_WIDTH:].astype(bf16)
        dq = jnp.zeros((tm, MEM_WIDTH), f32)
        dk = jnp.zeros((n_mem, MEM_WIDTH), f32)
        dv = jnp.zeros((n_mem, MEM_WIDTH), f32)
        for h in range(MEM_HEADS):
            msk = _head_mask(MEM_WIDTH, h)
            qh = jnp.where(msk, q, 0.0).astype(bf16)
            doh = jnp.where(msk, do, 0.0).astype(bf16)
            sc = _dot_nt(qh, k) * ATT_SCALE
            e = jnp.exp(sc - jnp.max(sc, axis=-1, keepdims=True))
            p = e / jnp.sum(e, axis=-1, keepdims=True)
            dp = _dot_nt(doh, v)
            ds = p * (dp - jnp.sum(dp * p, axis=-1, keepdims=True))
            dsb = (ds * ATT_SCALE).astype(bf16)
            dq = jnp.where(msk, _dot(dsb, k), dq)
            dk = dk + _dot_tn(dsb, qh)
            dv = dv + _dot_tn(p.astype(bf16), doh)
        dq_ref[...] = dq.astype(bf16)
        dkv_ref[:, :MEM_WIDTH] += dk
        dkv_ref[:, MEM_WIDTH:] += dv

    return pl.pallas_call(
        body, grid=(s // tm,),
        in_specs=[pl.BlockSpec((tm, MEM_WIDTH), lambda i: (i, q_block)), pl.BlockSpec((n_mem, 2 * MEM_WIDTH), lambda i: (0, 0)),
                  pl.BlockSpec((tm, MEM_WIDTH), lambda i: (i, MAIN_WIDTH // MEM_WIDTH)), _ANY],
        out_specs=[pl.BlockSpec((tm, MEM_WIDTH), lambda i: (i, out_block)), pl.BlockSpec((n_mem, 2 * MEM_WIDTH), lambda i: (0, 0))],
        out_shape=[SDS(into.shape, bf16), SDS((n_mem, 2 * MEM_WIDTH), f32)], name=name,
        input_output_aliases={3: 0}, compiler_params=_cparams("arbitrary"))(proj, kv, d_mixed, into)


def _tri(t, upper):
    r = lax.broadcasted_iota(jnp.int32, (t, t), 0)
    c = lax.broadcasted_iota(jnp.int32, (t, t), 1)
    return ((r <= c) if upper else (r >= c)).astype(f32)


def fgate_fwd(z_t, b, name, t=512):
    hh, s = z_t.shape

    def body(z_ref, b_ref, c_ref):
        u = _tri(t, True)
        carry = jnp.zeros((hh, 1), f32)
        for blk in range(s // t):
            x = z_ref[:, blk * t:(blk + 1) * t] + b_ref[...]
            logf = jnp.minimum(x, 0.0) - jnp.log(1.0 + jnp.exp(-jnp.abs(x)))
            y = jnp.dot(logf, u, precision=lax.Precision.HIGHEST, preferred_element_type=f32) + carry
            c_ref[:, blk * t:(blk + 1) * t] = y
            carry = y[:, t - 1:t]

    return pl.pallas_call(body, out_shape=SDS((hh, s), f32), name=name, compiler_params=_cparams())(z_t, b)


def fgate_bwd(dc_t, z_t, b, name, t=512):
    hh, s = z_t.shape

    def body(dc_ref, z_ref, b_ref, dz_ref, db_ref):
        low = _tri(t, False)
        carry = jnp.zeros((hh, 1), f32)
        total = jnp.zeros((hh, 1), f32)
        for blk in reversed(range(s // t)):
            cols = slice(blk * t, (blk + 1) * t)
            y = jnp.dot(dc_ref[:, cols], low, precision=lax.Precision.HIGHEST, preferred_element_type=f32) + carry
            carry = y[:, 0:1]
            dz = y * _sigmoid(-(z_ref[:, cols] + b_ref[...]))
            dz_ref[:, cols] = dz
            total = total + jnp.sum(dz, axis=-1, keepdims=True)
        db_ref[...] = jnp.broadcast_to(total, db_ref.shape)

    return pl.pallas_call(body, out_shape=[SDS((hh, s), f32), SDS((hh, LANES), f32)], name=name,
                          compiler_params=_cparams())(dc_t, z_t, b)


def _pair_masks():
    lane = lax.broadcasted_iota(jnp.int32, (1, LANES), 1)
    return [lane < HEAD_DIM, lane >= HEAD_DIM]


def _tile_base(cr_ref, hh, lo):
    return cr_ref[hh:hh + 1, pl.ds(lo, LANES)][:, 0:1]


def fox_fwd(q, kv, c_row, name, tq=512, out_width=MAIN_WIDTH):
    s = kv.shape[0]
    nq = s // tq

    def body(q_ref, k_ref, v_ref, cr_ref, o_ref, lse_ref, ob_ref):
        i = pl.program_id(1)
        qv = q_ref[...]
        masks = _pair_masks()
        row = lax.broadcasted_iota(jnp.int32, (tq, tq), 0)
        col = lax.broadcasted_iota(jnp.int32, (tq, tq), 1)
        qh = [jnp.where(masks[hh], qv, jnp.zeros((), bf16)) * ATT_SCALE for hh in range(2)]
        ct = [_tile_base(cr_ref, hh, pl.multiple_of(i * tq, tq)) for hh in range(2)]

        def block(j, carry, diag):
            lo = pl.multiple_of(j * tq, tq)
            ks = k_ref[pl.ds(lo, tq), :]
            vs = v_ref[pl.ds(lo, tq), :]
            out = []
            for hh in range(2):
                m, l, acc = carry[hh]
                sc = _dot_nt(qh[hh], ks) + (ct[hh] - cr_ref[hh:hh + 1, pl.ds(lo, tq)])
                if diag:
                    sc = jnp.where(col <= row, sc, -jnp.inf)
                m_new = jnp.maximum(m, jnp.max(sc, axis=-1, keepdims=True))
                alpha = jnp.exp(m - m_new)
                p = jnp.exp(sc - m_new)
                l = alpha * l + jnp.sum(p, axis=-1, keepdims=True)
                p_hi = p.astype(bf16)
                p_lo = (p - p_hi.astype(f32)).astype(bf16)
                acc = alpha * acc + (_dot(p_hi, vs) + _dot(p_lo, vs))
                out.append((m_new, l, acc))
            return tuple(out)

        init = (jnp.full((tq, 1), -jnp.inf, f32), jnp.zeros((tq, 1), f32), jnp.zeros((tq, LANES), f32))
        carry = lax.fori_loop(0, i, functools.partial(block, diag=False), (init, init))
        res = [(acc / l, m + jnp.log(l)) for m, l, acc in block(i, carry, True)]
        out = jnp.where(masks[0], res[0][0], res[1][0])
        o_ref[...] = out
        ob_ref[...] = out.astype(bf16)
        lse_ref[...] = jnp.where(masks[0], res[0][1], res[1][1])

    return pl.pallas_call(
        body, grid=(FOX_PAIRS, nq),
        in_specs=[pl.BlockSpec((tq, LANES), lambda p, i: (i, p)), pl.BlockSpec((s, LANES), lambda p, i: (0, p)),
                  pl.BlockSpec((s, LANES), lambda p, i: (0, FOX_PAIRS + p)), pl.BlockSpec((None, 2, s), lambda p, i: (p, 0, 0))],
        out_specs=[pl.BlockSpec((tq, LANES), lambda p, i: (i, p)), pl.BlockSpec((None, tq, LANES), lambda p, i: (p, i, 0)),
                   pl.BlockSpec((tq, LANES), lambda p, i: (i, p))],
        out_shape=[SDS((s, MAIN_WIDTH), f32), SDS((FOX_PAIRS, s, LANES), f32), SDS((s, out_width), bf16)], name=name,
        compiler_params=_cparams("parallel", "parallel"))(q, kv, kv, c_row)


def fox_bwd(q, kv, d_mixed, o, lse, c_row, name, tq=512, dq_width=MAIN_WIDTH):
    s = kv.shape[0]
    nq = s // tq

    def body(q_ref, k_ref, v_ref, do_ref, o_ref, lse_ref, cr_ref, dqb_ref, dk_ref, dv_ref, dc_ref, dq_ref):
        j = pl.program_id(1)

        @pl.when(j == 0)
        def _():
            dq_ref[...] = jnp.zeros_like(dq_ref)

        masks = _pair_masks()
        sub = lax.broadcasted_iota(jnp.int32, (LANES, 1), 0)
        sub_masks = [sub < HEAD_DIM, sub >= HEAD_DIM]
        row = lax.broadcasted_iota(jnp.int32, (tq, tq), 0)
        col = lax.broadcasted_iota(jnp.int32, (tq, tq), 1)
        kj = k_ref[...]
        vj = v_ref[...]
        lo_j = pl.multiple_of(j * tq, tq)

        def block(i, carry, diag):
            dk_t, dv_t, dc0, dc1 = carry
            dcs = [dc0, dc1]
            lo = pl.multiple_of(i * tq, tq)
            qi = q_ref[pl.ds(lo, tq), :]
            qi = qi * ATT_SCALE
            qt_i = qi.T
            doi = do_ref[pl.ds(lo, tq), :]
            dot_i = doi.astype(bf16).T
            prod = doi.astype(bf16).astype(f32) * o_ref[pl.ds(lo, tq), :]
            lse_i = lse_ref[pl.ds(lo, tq), :]
            dq_i = jnp.zeros((tq, LANES), f32)
            for hh in range(2):
                qh = jnp.where(masks[hh], qi, jnp.zeros((), bf16))
                doh = jnp.where(masks[hh], doi, 0.0).astype(bf16)
                delta = jnp.sum(jnp.where(masks[hh], prod, 0.0), axis=-1, keepdims=True)
                sc = _dot_nt(qh, kj) + (_tile_base(cr_ref, hh, lo) - cr_ref[hh:hh + 1, pl.ds(lo_j, tq)])
                p = jnp.exp(sc - lse_i[:, hh * HEAD_DIM:hh * HEAD_DIM + 1])
                if diag:
                    p = jnp.where(col <= row, p, 0.0)
                dv_t = dv_t + _dot(jnp.where(sub_masks[hh], dot_i, jnp.zeros((), bf16)), p.astype(bf16))
                ds = p * (_dot_nt(doh, vj) - delta)
                dcs[hh] = dcs[hh] + jnp.sum(ds, axis=0, keepdims=True)
                dsb = ds.astype(bf16)
                dq_i = jnp.where(masks[hh], _dot(dsb, kj), dq_i)
                dk_t = dk_t + _dot(jnp.where(sub_masks[hh], qt_i, jnp.zeros((), bf16)), dsb)
            dq_ref[pl.ds(lo, tq), :] += dq_i * ATT_SCALE
            return dk_t, dv_t, dcs[0], dcs[1]

        zero = jnp.zeros((LANES, tq), f32)
        zrow = jnp.zeros((1, tq), f32)
        carry = block(j, (zero, zero, zrow, zrow), True)
        dk_t, dv_t, dc0, dc1 = lax.fori_loop(j + 1, nq, functools.partial(block, diag=False), carry)
        dk_ref[...] = dk_t.T.astype(bf16)
        dv_ref[...] = dv_t.T.astype(bf16)
        dc_ref[0:1, :] = -dc0
        dc_ref[1:2, :] = -dc1

        @pl.when(j == nq - 1)
        def _():
            dqb_ref[...] = dq_ref[...].astype(bf16)

    full = lambda p, j: (0, p)
    tile = lambda p, j: (j, p)
    return pl.pallas_call(
        body, grid=(FOX_PAIRS, nq),
        in_specs=[pl.BlockSpec((s, LANES), full), pl.BlockSpec((tq, LANES), tile), pl.BlockSpec((tq, LANES), lambda p, j: (j, FOX_PAIRS + p)),
                  pl.BlockSpec((s, LANES), full), pl.BlockSpec((s, LANES), full), pl.BlockSpec((None, s, LANES), lambda p, j: (p, 0, 0)),
                  pl.BlockSpec((None, 2, s), lambda p, j: (p, 0, 0))],
        out_specs=[pl.BlockSpec((s, LANES), full), pl.BlockSpec((tq, LANES), tile), pl.BlockSpec((tq, LANES), tile),
                   pl.BlockSpec((None, 2, tq), lambda p, j: (p, 0, j))],
        out_shape=[SDS((s, dq_width), bf16), SDS((s, MAIN_WIDTH), bf16), SDS((s, MAIN_WIDTH), bf16), SDS((FOX_PAIRS, 2, s), f32)],
        scratch_shapes=[pltpu.VMEM((s, LANES), f32)],
        name=name, compiler_params=_cparams("parallel", "arbitrary"))(q, kv, kv, d_mixed, o, lse, c_row)


def adamw(w, g, m, v, name, tr=256):
    r, c = w.shape
    tr = min(tr, r)
    assert r % tr == 0, (name, r, tr)
    c1 = 1.0 / (1.0 - ADAM_B1 ** ADAM_STEP)
    c2 = 1.0 / (1.0 - ADAM_B2 ** ADAM_STEP)

    def body(w_ref, g_ref, m_ref, v_ref, d_ref, mo_ref, vo_ref):
        gv = g_ref[...]
        mn = ADAM_B1 * m_ref[...] + (1.0 - ADAM_B1) * gv
        vn = ADAM_B2 * v_ref[...] + (1.0 - ADAM_B2) * gv * gv
        mo_ref[...] = mn
        vo_ref[...] = vn
        d_ref[...] = -ADAM_LR * ((mn * c1) / (jnp.sqrt(vn * c2) + ADAM_EPS) + ADAM_WD * w_ref[...])

    spec = pl.BlockSpec((tr, c), lambda i: (i, 0))
    return pl.pallas_call(body, grid=(r // tr,), in_specs=[spec] * 4, out_specs=[spec] * 3, out_shape=[SDS((r, c), f32)] * 3,
                          name=name, compiler_params=_cparams("parallel"))(w, g, m, v)


def adamw_owned(w, parts, m, v, name, tr):
    nl, r, c = w.shape
    cp = parts[0].shape[2]
    assert r % tr == 0 and len(parts) == nl, (name, r, tr)
    c1 = 1.0 / (1.0 - ADAM_B1 ** ADAM_STEP)
    c2 = 1.0 / (1.0 - ADAM_B2 ** ADAM_STEP)

    def body(*refs):
        w_ref, p_refs, (m_ref, v_ref) = refs[0], refs[1:1 + nl], refs[1 + nl:3 + nl]
        g_ref, d_ref, mo_ref, vo_ref = refs[3 + nl:]
        layer = pl.program_id(0)

        def total(p_ref):
            acc = p_ref[0].astype(f32)
            for k in range(1, N_DEV):
                acc = acc + p_ref[k].astype(f32)
            return acc

        gv = total(p_refs[0])
        for l in range(1, nl):
            gv = jnp.where(layer == l, total(p_refs[l]), gv)
        gv = gv[:, :c]
        g_ref[...] = gv
        mn = ADAM_B1 * m_ref[...] + (1.0 - ADAM_B1) * gv
        vn = ADAM_B2 * v_ref[...] + (1.0 - ADAM_B2) * gv * gv
        mo_ref[...] = mn
        vo_ref[...] = vn
        d_ref[...] = -ADAM_LR * ((mn * c1) / (jnp.sqrt(vn * c2) + ADAM_EPS) + ADAM_WD * w_ref[...])

    spec = pl.BlockSpec((None, tr, c), lambda l, i: (l, i, 0))
    last = r // tr - 1

    def part_spec(mine):
        return pl.BlockSpec((N_DEV, tr, cp), lambda l, i: (0, jnp.where(l == mine, i, jnp.where(l < mine, 0, last)), 0))

    return pl.pallas_call(body, grid=(nl, r // tr), in_specs=[spec] + [part_spec(l) for l in range(nl)] + [spec, spec], out_specs=[spec] * 4,
                          out_shape=[SDS((nl, r, c), f32)] * 4, name=name,
                          compiler_params=_cparams("parallel", "parallel"))(w, *parts, m, v)


def sum_leading(x, name, out_dtype=f32, tr=None):
    n, r, c = x.shape
    tr = tr or r
    assert r % tr == 0

    def body(x_ref, o_ref):
        acc = x_ref[0].astype(f32)
        for k in range(1, n):
            acc = acc + x_ref[k].astype(f32)
        o_ref[...] = acc.astype(out_dtype)

    return pl.pallas_call(body, grid=(r // tr,), in_specs=[pl.BlockSpec((n, tr, c), lambda i: (0, i, 0))],
                          out_specs=pl.BlockSpec((tr, c), lambda i: (i, 0)), out_shape=SDS((r, c), out_dtype), name=name,
                          compiler_params=_cparams("parallel"))(x)


_ANY = pl.BlockSpec(memory_space=pl.ANY)
_DMA = pltpu.SemaphoreType.DMA


_HBM = pl.BlockSpec(memory_space=pltpu.HBM)
_SEM = pl.BlockSpec(memory_space=pltpu.SEMAPHORE)
_EFFECT = pltpu.SideEffectType.DATAFLOW_SIDE_EFFECTING
_FLIPS = [(0, 0, 1), (1, 0, 0), (0, 1, 0), (1, 1, 0), (1, 0, 1), (0, 1, 1), (1, 1, 1)]


def _me():
    return lax.axis_index("x"), lax.axis_index("y"), lax.axis_index("c")


def _peers():
    mx, my, mc = _me()
    return [(jnp.bitwise_xor(mx, fx), jnp.bitwise_xor(my, fy), jnp.bitwise_xor(mc, fc)) for fx, fy, fc in _FLIPS]


def _index(dev):
    return 4 * dev[0] + 2 * dev[1] + dev[2]


def _win(ref, axis, k, size, count=1):
    idx = [slice(None)] * len(ref.shape)
    idx[axis] = pl.ds(k * size, count * size)
    return ref.at[tuple(idx)]


def _hbm(a):
    return pltpu.with_memory_space_constraint(a, pltpu.HBM)


def _exchange_start(srcs, lands, copies_of, name):
    n = len(srcs)

    def body(*refs):
        src = refs[:n]
        send_sems, recv_sems, self_sems = refs[2 * n:2 * n + 3]
        land = refs[3 * n + 3:4 * n + 3]
        token = refs[4 * n + 3]
        me = _index(_me())
        for a in range(n):
            for s_ref, d_ref, peer in copies_of(a, src[a], land[a], me):
                if peer is None:
                    pltpu.make_async_copy(s_ref, d_ref, self_sems.at[a]).start()
                else:
                    pltpu.make_async_remote_copy(src_ref=s_ref, dst_ref=d_ref, send_sem=send_sems.at[a], recv_sem=recv_sems.at[a],
                                                 device_id=peer, device_id_type=MESH).start()
        token[...] = jnp.zeros_like(token)

    outs = pl.pallas_call(
        body, name=name,
        out_shape=(_DMA((n,)), _DMA((n,)), _DMA((n,)), *[pltpu.HBM(s.shape, s.dtype) for s in srcs],
                   *[pltpu.HBM(l.shape, l.dtype) for l in lands], SDS((8, LANES), f32)),
        in_specs=[_HBM] * (2 * n), out_specs=(_SEM, _SEM, _SEM, *[_HBM] * (2 * n), pl.BlockSpec(memory_space=pltpu.VMEM)),
        input_output_aliases={i: 3 + i for i in range(2 * n)},
        compiler_params=pltpu.CompilerParams(has_side_effects=_EFFECT),
    )(*[_hbm(s) for s in srcs], *[_hbm(lax.empty(l.shape, l.dtype)) for l in lands])
    return dict(sems=outs[:3], srcs=list(outs[3:3 + n]), lands=list(outs[3 + n:3 + 2 * n]), token=outs[3 + 2 * n])


def _exchange_wait(started, waits_of, after, name, which=None):
    which = list(range(len(started["srcs"]))) if which is None else which
    srcs, lands = [started["srcs"][a] for a in which], [started["lands"][a] for a in which]
    n = len(which)

    def body(*refs):
        src = refs[:n]
        land = refs[n:2 * n]
        send_sems, recv_sems, self_sems = refs[2 * n:2 * n + 3]
        me = _index(_me())
        for pos, a in enumerate(which):
            seven, (s_ref, d_ref) = waits_of(a, src[pos], land[pos], me)
            both = pltpu.make_async_remote_copy(src_ref=seven, dst_ref=seven, send_sem=send_sems.at[a], recv_sem=recv_sems.at[a],
                                                device_id=_me(), device_id_type=MESH)
            both.wait_send()
            both.wait_recv()
            pltpu.make_async_copy(s_ref, d_ref, self_sems.at[a]).wait()

    outs = pl.pallas_call(
        body, name=name, out_shape=tuple(pltpu.HBM(t.shape, t.dtype) for t in srcs + lands),
        in_specs=[_HBM] * (2 * n) + [_SEM] * 3 + [_ANY], out_specs=tuple([_HBM] * (2 * n)),
        input_output_aliases={i: i for i in range(2 * n)},
        compiler_params=pltpu.CompilerParams(has_side_effects=_EFFECT),
    )(*srcs, *lands, *started["sems"], after)
    return list(outs[n:])


def gather_start(locs, axes, name):
    lands = [SDS(tuple(N_DEV * d if i == ax else d for i, d in enumerate(l.shape)), l.dtype) for l, ax in zip(locs, axes)]

    def copies_of(a, src, land, me):
        mine = _win(land, axes[a], me, src.shape[axes[a]])
        return [(src, mine, peer) for peer in _peers()] + [(src, mine, None)]

    return _exchange_start(locs, lands, copies_of, name)


def gather_wait(started, axes, after, name, which=None):
    def waits_of(a, src, land, me):
        size = src.shape[axes[a]]
        return _win(land, axes[a], 0, size, N_DEV - 1), (src, _win(land, axes[a], me, size))

    return _exchange_wait(started, waits_of, after, name, which)


def _part(ref, axis, k, stride, used):
    idx = [slice(None)] * len(ref.shape)
    idx[axis] = pl.ds(k * stride, used)
    return ref.at[tuple(idx)]


def scatter_start(grads, axes, name, used=None):
    strides = [g.shape[ax] // N_DEV for g, ax in zip(grads, axes)]
    used = used or strides
    lands = [SDS((N_DEV,) + tuple(u if i == ax else d for i, d in enumerate(g.shape)), g.dtype) for g, ax, u in zip(grads, axes, used)]

    def copies_of(a, src, land, me):
        out = [(_part(src, axes[a], _index(peer), strides[a], used[a]), land.at[me], peer) for peer in _peers()]
        return out + [(_part(src, axes[a], me, strides[a], used[a]), land.at[me], None)]

    return _exchange_start(grads, lands, copies_of, name)


def scatter_wait(started, axes, after, name, used=None):
    def waits_of(a, src, land, me):
        stride = src.shape[axes[a]] // N_DEV
        return land.at[pl.ds(0, N_DEV - 1)], (_part(src, axes[a], me, stride, used[a] if used else stride), land.at[me])

    return _exchange_wait(started, waits_of, after, name)


def _row_tile(rows, cap=512):
    return max(t for t in range(8, min(rows, cap) + 1, 8) if rows % t == 0)


_SMALL = [
    ("ln_mix_pre", (2, 1024)), ("ln_mix_post", (2, 1024)), ("ln_ffn_pre", (2, 1024)), ("ln_ffn_post", (2, 1024)),
    ("ln_mem", (2, 1024)), ("w_spatial", (1, 6, 128, 128)), ("b_spatial", (1, 6, 128)), ("ln_shared", (1024,)),
    ("b_forget", (12,)), ("ln_v_g", (1, 768)), ("ln_v_b", (1, 768)),
]
_SMALL_TILE = 8 * LANES


def _small_rows(shape):
    return -(-math.prod(shape) // _SMALL_TILE) * 8


def _pack_small(vals, shapes):
    parts = []
    for name, shape in shapes:
        flat = vals[name].reshape(-1).astype(f32)
        rows = _small_rows(shape)
        parts.append(jnp.pad(flat, (0, rows * LANES - flat.shape[0])).reshape(rows, LANES))
    return jnp.concatenate(parts, axis=0)


def _unpack_small(buf, shapes):
    out = {}
    lo = 0
    for name, shape in shapes:
        rows = _small_rows(shape)
        out[name] = buf[lo:lo + rows].reshape(-1)[:math.prod(shape)].reshape(shape)
        lo += rows
    return out


def kernel(x, mem, ln_mix_pre, ln_mix_post, ln_ffn_pre, ln_ffn_post, ln_mem, w_mem_kv, w_out, w_ffn_gate, w_ffn_up, w_ffn_down, w_in_a, w_spatial, b_spatial, ln_v_g, ln_v_b, ln_shared, w_shared_kv, b_forget, w_in_b, loss_target, m_ln_mix_pre, m_ln_mix_post, m_ln_ffn_pre, m_ln_ffn_post, m_ln_mem, m_w_mem_kv, m_w_out, m_w_ffn_gate, m_w_ffn_up, m_w_ffn_down, m_w_in_a, m_w_spatial, m_b_spatial, m_ln_v_g, m_ln_v_b, m_ln_shared, m_w_shared_kv, m_b_forget, m_w_in_b, v_ln_mix_pre, v_ln_mix_post, v_ln_ffn_pre, v_ln_ffn_post, v_ln_mem, v_w_mem_kv, v_w_out, v_w_ffn_gate, v_w_ffn_up, v_w_ffn_down, v_w_in_a, v_w_spatial, v_b_spatial, v_ln_v_g, v_ln_v_b, v_ln_shared, v_w_shared_kv, v_b_forget, v_w_in_b):
    weights = dict(ln_mix_pre=ln_mix_pre, ln_mix_post=ln_mix_post, ln_ffn_pre=ln_ffn_pre, ln_ffn_post=ln_ffn_post, ln_mem=ln_mem,
                   w_mem_kv=w_mem_kv, w_out=w_out, w_ffn_gate=w_ffn_gate, w_ffn_up=w_ffn_up, w_ffn_down=w_ffn_down, w_in_a=w_in_a,
                   w_spatial=w_spatial, b_spatial=b_spatial, ln_v_g=ln_v_g, ln_v_b=ln_v_b, ln_shared=ln_shared,
                   w_shared_kv=w_shared_kv, b_forget=b_forget, w_in_b=w_in_b)
    mom_m = dict(ln_mix_pre=m_ln_mix_pre, ln_mix_post=m_ln_mix_post, ln_ffn_pre=m_ln_ffn_pre, ln_ffn_post=m_ln_ffn_post, ln_mem=m_ln_mem,
                 w_mem_kv=m_w_mem_kv, w_out=m_w_out, w_ffn_gate=m_w_ffn_gate, w_ffn_up=m_w_ffn_up, w_ffn_down=m_w_ffn_down, w_in_a=m_w_in_a,
                 w_spatial=m_w_spatial, b_spatial=m_b_spatial, ln_v_g=m_ln_v_g, ln_v_b=m_ln_v_b, ln_shared=m_ln_shared,
                 w_shared_kv=m_w_shared_kv, b_forget=m_b_forget, w_in_b=m_w_in_b)
    mom_v = dict(ln_mix_pre=v_ln_mix_pre, ln_mix_post=v_ln_mix_post, ln_ffn_pre=v_ln_ffn_pre, ln_ffn_post=v_ln_ffn_post, ln_mem=v_ln_mem,
                 w_mem_kv=v_w_mem_kv, w_out=v_w_out, w_ffn_gate=v_w_ffn_gate, w_ffn_up=v_w_ffn_up, w_ffn_down=v_w_ffn_down, w_in_a=v_w_in_a,
                 w_spatial=v_w_spatial, b_spatial=v_b_spatial, ln_v_g=v_ln_v_g, ln_v_b=v_ln_v_b, ln_shared=v_ln_shared,
                 w_shared_kv=v_w_shared_kv, b_forget=v_b_forget, w_in_b=v_w_in_b)
    names = list(weights)
    mx, my, mc = lax.axis_index("x"), lax.axis_index("y"), lax.axis_index("c")
    me = 4 * mx + 2 * my + mc

    h0 = x[0]
    mem0 = mem[0]
    tgt = loss_target[0]
    seq = h0.shape[0]

    vec = lambda a: a.reshape(1, -1)
    pad_to = lambda a, axis, size: jnp.pad(a, [(0, size - a.shape[i] if i == axis else 0) for i in range(a.ndim)])

    def after(tok, a):
        return a + tok[0, 0].astype(a.dtype)

    lnv_loc = pad_to(jnp.concatenate([ln_v_g, ln_v_b], axis=0), 0, 8)
    st_a = gather_start([w_in_a.astype(bf16), pad_to(lnv_loc, 1, LANES)[None]], [0, 0], "gather_a_start")
    mix_locs = lambda l, tok: [after(tok, w_mem_kv[l]).astype(bf16), w_out[l].astype(bf16)]

    def ffn_gather_start(l, tok):
        gate_up = gather_start([pad_to(after(tok, w_ffn_gate[l]).astype(bf16), 1, FF_SHARD_PAD),
                                pad_to(w_ffn_up[l].astype(bf16), 1, FF_SHARD_PAD)], [1, 1], f"gather_gate_up{l}_start")
        down = gather_start([pad_to(after(gate_up["token"], w_ffn_down[l]).astype(bf16), 0, FF_SHARD_PAD)], [0], f"gather_down{l}_start")
        return gate_up, down

    st_b = [gather_start(mix_locs(0, st_a["token"]), [0, 0], "gather_b0_start"), None]
    st_c = ffn_gather_start(0, st_b[0]["token"])
    st_d = gather_start([after(st_c[1]["token"], w_in_b[0]).astype(bf16), pad_to(w_shared_kv.astype(bf16), 1, KV_PAD)], [0, 0],
                        "gather_d_start")
    st_b[1] = gather_start(mix_locs(1, st_d["token"]), [0, 0], "gather_b1_start")
    st_e = ffn_gather_start(1, st_b[1]["token"])
    ws = w_spatial[0].astype(bf16)
    ws_t = ws.transpose(0, 2, 1)
    bs_t = b_spatial[0].T

    (a0,) = rms_fwd(h0, [after(st_e[1]["token"], vec(ln_mix_pre[0]))], "a0_norm")
    w_in_a8, lnv8 = gather_wait(st_a, [0, 0], a0, "gather_a_wait")
    w_in_a_full = w_in_a8.transpose(1, 0, 2).reshape(D_MODEL, -1)
    lnv_g = lnv8[:, 0, :MAIN_WIDTH // N_DEV].reshape(1, MAIN_WIDTH)
    lnv_b = lnv8[:, 1, :MAIN_WIDTH // N_DEV].reshape(1, MAIN_WIDTH)
    proj0 = mm(a0, w_in_a_full, "proj0", tn=896)
    main0 = gmlp_fwd(proj0, ws, bs_t, lnv_g, lnv_b, "gmlp_fwd", out_width=D_MODEL)
    w_mkv, w_o = [None, None], [None, None]
    w_mkv[0], w_o[0] = gather_wait(st_b[0], [0, 0], main0, "gather_b0_wait")
    (memn0,) = rms_fwd(mem0, [vec(ln_mem[0])], "mem0_norm")
    kvm0 = mm(memn0, w_mkv[0], "kvm0")
    mixed0 = mem_attn_fwd(proj0, 2 * MAIN_WIDTH // MEM_WIDTH, kvm0, main0, "mem_attn0")
    y1_0, hmid0, f0 = mm_resnorm(mixed0, w_o[0], h0, vec(ln_mix_post[0]), [vec(ln_ffn_pre[0])], "mix_out0", tm=1024)
    w_g0, w_u0 = gather_wait(st_c[0], [1, 1], f0, "gather_gate_up0_wait")
    gu0, act0 = ffn_up(f0, w_g0, w_u0, "ffn_up0")
    (w_d0,) = gather_wait(st_c[1], [0], act0, "gather_down0_wait")
    y2_0, h1, a1, sin1 = mm_resnorm(act0, w_d0, hmid0, vec(ln_ffn_post[0]), [vec(ln_mix_pre[1]), vec(ln_shared)], "ffn_down0")

    w_inb, w_kv = gather_wait(st_d, [0, 0], sin1, "gather_d_wait")
    kvb = mm(sin1, w_kv, "kv_shared", out_dtype=bf16, tn=MAIN_WIDTH, ncols=2 * MAIN_WIDTH)
    zf = mm(sin1, w_kv, "forget_logits", tn=256, col0=2 * MAIN_WIDTH, ncols=256)
    qb = mm(a1, w_inb, "proj1", out_dtype=bf16)
    z_t = jnp.pad(zf[:, :FOX_HEADS].T, ((0, 16 - FOX_HEADS), (0, 0)))
    bf_col = jnp.pad(b_forget, (0, 16 - FOX_HEADS)).reshape(16, 1)
    c_t = fgate_fwd(z_t, bf_col, "fgate_fwd")
    c_row = c_t[:FOX_HEADS].reshape(FOX_PAIRS, 2, seq)
    main1, lse, main1_b = fox_fwd(qb, kvb, c_row, "fox_fwd", out_width=D_MODEL)
    w_mkv[1], w_o[1] = gather_wait(st_b[1], [0, 0], main1, "gather_b1_wait")
    (memn1,) = rms_fwd(mem0, [vec(ln_mem[1])], "mem1_norm")
    kvm1 = mm(memn1, w_mkv[1], "kvm1")
    mixed1 = mem_attn_fwd(qb, MAIN_WIDTH // MEM_WIDTH, kvm1, main1_b, "mem_attn1")
    y1_1, hmid1, f1 = mm_resnorm(mixed1, w_o[1], h1, vec(ln_mix_post[1]), [vec(ln_ffn_pre[1])], "mix_out1", tm=1024)
    w_g1, w_u1 = gather_wait(st_e[0], [1, 1], f1, "gather_gate_up1_wait")
    gu1, act1 = ffn_up(f1, w_g1, w_u1, "ffn_up1")
    (w_d1,) = gather_wait(st_e[1], [0], act1, "gather_down1_wait")
    dh, d_y2_1, dg_fpost1, loss_tile = mm_resnorm_loss(act1, w_d1, hmid1, vec(ln_ffn_post[1]), tgt, "ffn_down1_loss")
    ffn_w = [(w_g0, w_u0, w_d0), (w_g1, w_u1, w_d1)]
    ff_shard = w_ffn_down.shape[1]

    small = {}

    def ffn_backward(layer, dh_out, d_y2, hmid, f, gu, act, y1):
        w_g, w_u, w_d = ffn_w[layer]
        dw_down = mm_tn(act, d_y2, f"dw_down{layer}")
        rs_down = scatter_start([dw_down], [0], f"scatter_down{layer}_start", used=[ff_shard])
        d_g, d_u = ffn_act_grad(d_y2, w_d, gu, f"ffn_act_grad{layer}")
        dw_g = mm_tn(d_g, f, f"dw_gate{layer}", dep=rs_down["token"])
        dw_u = mm_tn(d_u, f, f"dw_up{layer}")
        rs_gate_up = scatter_start([dw_g, dw_u], [0, 0], f"scatter_gate_up{layer}_start", used=[ff_shard] * 2)
        dh_mid, d_y1, dg_fpre, dg_mpost = ffn_in_grad(d_g, d_u, w_g, w_u, hmid, dh_out, after(rs_gate_up["token"], vec(ln_ffn_pre[layer])),
                                                      y1, vec(ln_mix_post[layer]), f"ffn_in_grad{layer}")
        return dh_mid, d_y1, dg_fpre, dg_mpost, (rs_down, rs_gate_up)

    def mix_out_backward(layer, d_y1, mixed):
        dw_out = mm_tn(mixed, d_y1, f"dw_out{layer}")
        d_mixed = mm(d_y1, w_o[layer], f"d_mixed{layer}", trans_b=True)
        return d_mixed, dw_out

    def mem_backward(layer, q_src, q_block, kvm, memn, d_mixed, into):
        d_qm, d_kvm = mem_attn_bwd(q_src, q_block, kvm, d_mixed, into, f"mem_attn_bwd{layer}")
        d_kvm_b = d_kvm.astype(bf16)
        dw_mkv = mm_tn(memn, d_kvm_b, f"dw_mem_kv{layer}")
        d_memn = mm(d_kvm_b, w_mkv[layer], f"d_memn{layer}", trans_b=True)
        _, dg_mem = rms_bwd(mem0, vec(ln_mem[layer]), d_memn, None, bf16, f"mem_norm_bwd{layer}")
        return d_qm, dw_mkv, dg_mem


    dh_mid1, d_y1_1, dg_fpre1, dg_mpost1, rs_ffn1 = ffn_backward(1, dh, d_y2_1, hmid1, f1, gu1, act1, y1_1)
    d_mixed1, dw_out1 = mix_out_backward(1, d_y1_1, mixed1)
    dq_b, dk, dv, dc = fox_bwd(qb, kvb, d_mixed1, main1, lse, c_row, "fox_bwd", dq_width=D_MODEL)
    d_proj1, dw_mkv1, dg_mem1 = mem_backward(1, qb, MAIN_WIDTH // MEM_WIDTH, kvm1, memn1, d_mixed1, dq_b)
    rs_mix1 = scatter_start([dw_out1, dw_mkv1], [0, 0], "scatter_mix1_start")
    dc_t = jnp.pad(dc.reshape(FOX_HEADS, seq), ((0, 16 - FOX_HEADS), (0, 0)))
    dz_t, db_f = fgate_bwd(dc_t, z_t, bf_col, "fgate_bwd")
    d_kvf = jnp.concatenate([dk, dv, jnp.pad(dz_t[:FOX_HEADS].T.astype(bf16), ((0, 0), (0, KV_PAD - KV_WIDTH)))], axis=-1)
    dw_in_b = mm_tn(a1, d_proj1, "dw_in_b", dep=rs_mix1["token"])
    dw_kv = mm_tn(sin1, d_kvf, "dw_kv", tn=896)
    rs_2 = scatter_start([dw_in_b, dw_kv], [0, 0], "scatter_shared_start")
    dh1, (dg_pre1, dg_shared), d_y2_0, dg_fpost0 = proj_in_grad(
        [(d_proj1, w_inb, vec(ln_mix_pre[1])), (d_kvf, w_kv, vec(ln_shared))], h1, dh_mid1, "in_grad1", dep=rs_2["token"],
        below=(y2_0, vec(ln_ffn_post[0])))

    dh_mid0, d_y1_0, dg_fpre0, dg_mpost0, rs_ffn0 = ffn_backward(0, dh1, d_y2_0, hmid0, f0, gu0, act0, y1_0)
    d_mixed0, dw_out0 = mix_out_backward(0, d_y1_0, mixed0)
    d_uv, dw_s, db_s, dg_lnv, db_lnv = gmlp_bwd(proj0, d_mixed0, ws, ws_t, bs_t, lnv_g, lnv_b, "gmlp_bwd", out_width=w_in_a_full.shape[1])
    d_proj0, dw_mkv0, dg_mem0 = mem_backward(0, proj0, 2 * MAIN_WIDTH // MEM_WIDTH, kvm0, memn0, d_mixed0, d_uv)
    rs_mix0 = scatter_start([dw_out0, dw_mkv0], [0, 0], "scatter_mix0_start")

    small["ln_mix_pre"] = jnp.concatenate([jnp.zeros_like(dg_pre1), dg_pre1], axis=0)
    small["ln_mix_post"] = jnp.concatenate([dg_mpost0, dg_mpost1], axis=0)
    small["ln_ffn_pre"] = jnp.concatenate([dg_fpre0, dg_fpre1], axis=0)
    small["ln_ffn_post"] = jnp.concatenate([dg_fpost0, dg_fpost1], axis=0)
    small["ln_mem"] = jnp.concatenate([dg_mem0, dg_mem1], axis=0)
    small["w_spatial"] = dw_s[None]
    small["b_spatial"] = db_s[:, :A_GROUPS].T[None]
    small["ln_shared"] = dg_shared[0]
    small["b_forget"] = db_f[:FOX_HEADS, 0]
    small["ln_v_g"] = dg_lnv
    small["ln_v_b"] = db_lnv
    small_rows = jnp.concatenate([_pack_small(small, _SMALL), after(rs_mix0["token"], loss_tile)], axis=0)
    st_small = gather_start([small_rows[None]], [0], "gather_small_grads_start")
    dw_in_a_t = mm_tn(d_proj0, a0, "dw_in_a", tk=896, dep=st_small["token"])
    rs_in_a = scatter_start([dw_in_a_t], [0], "scatter_in_a_start")
    grad_x, (dg_pre0,) = proj_in_grad([(d_proj0, w_in_a_full, vec(ln_mix_pre[0]))], h0, dh_mid0, "in_grad0", dep=rs_in_a["token"])
    st_last = gather_start([dg_pre0.reshape(1, 8, LANES)], [0], "gather_last_grad_start")

    (p_down1,) = scatter_wait(rs_ffn1[0], [0], after(st_last["token"], grad_x[:8, :LANES]), "scatter_down1_wait", used=[ff_shard])
    p_gate1, p_up1 = scatter_wait(rs_ffn1[1], [0, 0], p_down1, "scatter_gate_up1_wait", used=[ff_shard] * 2)
    p_out1, p_mkv1 = scatter_wait(rs_mix1, [0, 0], p_gate1, "scatter_mix1_wait")
    p_in_b, p_kv = scatter_wait(rs_2, [0, 0], p_out1, "scatter_shared_wait")
    (p_down0,) = scatter_wait(rs_ffn0[0], [0], p_in_b, "scatter_down0_wait", used=[ff_shard])
    p_gate0, p_up0 = scatter_wait(rs_ffn0[1], [0, 0], p_down0, "scatter_gate_up0_wait", used=[ff_shard] * 2)
    p_out0, p_mkv0 = scatter_wait(rs_mix0, [0, 0], p_gate0, "scatter_mix0_wait")
    owned_parts = dict(w_ffn_gate=[p_gate0, p_gate1], w_ffn_up=[p_up0, p_up1], w_ffn_down=[p_down0, p_down1], w_out=[p_out0, p_out1],
                       w_mem_kv=[p_mkv0, p_mkv1], w_in_b=[p_in_b], w_shared_kv=[p_kv])

    grad_w, delta, new_m, new_v = {}, {}, {}, {}
    transposed = ("w_ffn_gate", "w_ffn_up", "w_in_a")

    def adamw_sharded(n, parts):
        shape = weights[n].shape
        three_d = shape if len(shape) == 3 else (1,) + shape
        view = (lambda t: t.reshape(three_d).transpose(0, 2, 1)) if n in transposed else (lambda t: t.reshape(three_d))
        back = (lambda t: t.transpose(0, 2, 1).reshape(shape)) if n in transposed else (lambda t: t.reshape(shape))
        w_view = view(weights[n])
        outs = adamw_owned(w_view, parts, view(mom_m[n]), view(mom_v[n]), f"adamw_{n}", tr=_row_tile(w_view.shape[1]))
        grad_w[n], delta[n], new_m[n], new_v[n] = (back(t) for t in outs)

    for n, parts in owned_parts.items():
        adamw_sharded(n, parts)
    (p_in_a,) = scatter_wait(rs_in_a, [0], delta["w_shared_kv"], "scatter_in_a_wait")
    adamw_sharded("w_in_a", [p_in_a])
    (small_all,) = gather_wait(st_small, [0], p_in_a, "gather_small_grads_wait")
    (last_all,) = gather_wait(st_last, [0], small_all, "gather_last_grad_wait")
    small_sum = sum_leading(small_all, "sum_small_grads")
    loss = small_sum[small_rows.shape[0] - 1, 0]
    g_small = _unpack_small(small_sum, _SMALL)
    g_small["ln_mix_pre"] = jnp.concatenate([sum_leading(last_all, "sum_last_grad").reshape(1, D_MODEL), g_small["ln_mix_pre"][1:]], axis=0)
    shard = MAIN_WIDTH // N_DEV
    for n in ("ln_v_g", "ln_v_b"):
        g_small[n] = lax.dynamic_slice_in_dim(g_small[n], me * shard, shard, axis=1)
    grad_w.update(g_small)
    small_local_shapes = [(n, tuple(weights[n].shape)) for n, _ in _SMALL]
    packed = [_pack_small(src, small_local_shapes) for src in (weights, grad_w, mom_m, mom_v)]
    outs = adamw(*packed, "adamw_small", tr=packed[0].shape[0])
    for dst, buf in zip((delta, new_m, new_v), outs):
        dst.update(_unpack_small(buf, small_local_shapes))

    return (loss, grad_x[None], *[grad_w[n] for n in names], *[delta[n] for n in names],
            *[new_m[n] for n in names], *[new_v[n] for n in names])
```
